```python
import jax, jax.numpy as jnp
from jax import lax
import numpy as np

D_MODEL = 1024
BATCH = 8
SEQ = 8192
DEPTH = 1

N_HEADS = 8
HEAD_DIM = 64
ATTN_WIDTH = N_HEADS * HEAD_DIM
CONV_WIDTH = D_MODEL // 2
CONV_K = 3
FFN_CONV_K = 3
D_FF = 2816
Q_BLOCK = 128
N_MOD = 6
RMS_EPS = 1e-6
NEG_INF = -1e30
IN_SPLITS = [CONV_WIDTH, CONV_WIDTH, CONV_WIDTH,
             ATTN_WIDTH, ATTN_WIDTH, ATTN_WIDTH,
             N_HEADS,
             D_MODEL, D_MODEL]
IN_WIDTH = sum(IN_SPLITS)
IN_OFFSETS = list(np.cumsum(IN_SPLITS)[:-1])

kernel_name = "hybrid_shortconv_fox_convffn_adaln"


def rmsnorm(x, g):
    xf = x.astype(jnp.float32)
    inv = lax.rsqrt(jnp.mean(xf * xf, axis=-1, keepdims=True) + RMS_EPS)
    return (xf * inv).astype(x.dtype) * g


def causal_dwconv(u, w):
    K = w.shape[0]
    S = u.shape[1]
    up = jnp.pad(u, ((0, 0), (K - 1, 0), (0, 0)))
    out = up[:, 0:S, :] * w[0]
    for k in range(1, K):
        out = out + up[:, k:k + S, :] * w[k]
    return out


def forgetting_attention(q, k, v, log_f):
    B, S, H, hd = q.shape
    nb = S // Q_BLOCK
    scale = 1.0 / np.sqrt(hd)
    F = jnp.cumsum(log_f, axis=1)
    qb = q.reshape(B, nb, Q_BLOCK, H, hd).transpose(1, 0, 3, 2, 4)
    Fq = F.reshape(B, nb, Q_BLOCK, H).transpose(1, 0, 3, 2)
    kh = k.transpose(0, 2, 1, 3)
    vh = v.transpose(0, 2, 1, 3)
    Fk = F.transpose(0, 2, 1)
    kpos = jnp.arange(S)

    def one_block(args):
        i, qi, Fqi = args
        qpos = i * Q_BLOCK + jnp.arange(Q_BLOCK)
        s = jnp.einsum('bhqd,bhkd->bhqk', qi, kh).astype(jnp.float32) * scale
        s = s + (Fqi[..., None] - Fk[:, :, None, :])
        s = jnp.where(kpos[None, :] <= qpos[:, None], s, NEG_INF)
        p = jax.nn.softmax(s, axis=-1)
        return jnp.einsum('bhqk,bhkd->bhqd', p.astype(vh.dtype), vh)

    o = lax.map(one_block, (jnp.arange(nb), qb, Fq))
    return o.transpose(1, 0, 3, 2, 4).reshape(B, S, H * hd)


def _fwd_setup_inputs(seed: int = 0) -> dict:
    key = jax.random.key(seed)
    ks = jax.random.split(key, 17)
    L = DEPTH

    def nrm(k, shape, s):
        return jax.random.normal(k, shape, jnp.float32) * s

    return {
        "x": nrm(ks[0], (BATCH, SEQ, D_MODEL), 1.0),
        "c": nrm(ks[1], (BATCH, D_MODEL), 1.0),
        "w_ada": nrm(ks[2], (L, D_MODEL, N_MOD * D_MODEL), 0.5 * D_MODEL ** -0.5),
        "b_ada": nrm(ks[3], (L, N_MOD * D_MODEL), 0.02),
        "norm1_g": 1.0 + nrm(ks[4], (L, D_MODEL), 0.02),
        "w_in": nrm(ks[5], (L, D_MODEL, IN_WIDTH), D_MODEL ** -0.5),
        "b_f": 3.0 + nrm(ks[6], (L, N_HEADS), 0.5),
        "conv_a_w": nrm(ks[7], (L, CONV_K, CONV_WIDTH), CONV_K ** -0.5),
        "q_norm_g": 1.0 + nrm(ks[8], (L, HEAD_DIM), 0.02),
        "k_norm_g": 1.0 + nrm(ks[9], (L, HEAD_DIM), 0.02),
        "w_branch_a": nrm(ks[10], (L, CONV_WIDTH, D_MODEL), CONV_WIDTH ** -0.5),
        "w_branch_b": nrm(ks[11], (L, ATTN_WIDTH, D_MODEL), ATTN_WIDTH ** -0.5),
        "w_out": nrm(ks[12], (L, D_MODEL, D_MODEL), D_MODEL ** -0.5),
        "norm2_g": 1.0 + nrm(ks[13], (L, D_MODEL), 0.02),
        "w_up": nrm(ks[14], (L, D_MODEL, 2 * D_FF), D_MODEL ** -0.5),
        "conv_ffn_w": nrm(ks[15], (L, FFN_CONV_K, 2 * D_FF), FFN_CONV_K ** -0.5),
        "w_down": nrm(ks[16], (L, D_FF, D_MODEL), D_FF ** -0.5),
    }


def _fwd_reference(x, c, w_ada, b_ada, norm1_g, w_in, b_f, conv_a_w, q_norm_g, k_norm_g,
              w_branch_a, w_branch_b, w_out, norm2_g, w_up, conv_ffn_w, w_down):
    B, S, _ = x.shape
    for l in range(DEPTH):
        mod = jnp.einsum('bd,de->be', jax.nn.silu(c), w_ada[l]) + b_ada[l]
        sh1, sc1, g1, sh2, sc2, g2 = jnp.split(mod, N_MOD, axis=-1)

        h = rmsnorm(x, norm1_g[l]) * (1.0 + sc1[:, None, :]) + sh1[:, None, :]
        proj = jnp.einsum('bsd,de->bse', h, w_in[l])
        cb, cc, cv, q, k, v, f_logit, ga, gb = jnp.split(proj, IN_OFFSETS, axis=-1)

        ya = cb * causal_dwconv(cc * cv, conv_a_w[l])

        q = rmsnorm(q.reshape(B, S, N_HEADS, HEAD_DIM), q_norm_g[l])
        k = rmsnorm(k.reshape(B, S, N_HEADS, HEAD_DIM), k_norm_g[l])
        v = v.reshape(B, S, N_HEADS, HEAD_DIM)
        log_f = jax.nn.log_sigmoid((f_logit + b_f[l]).astype(jnp.float32))
        yb = forgetting_attention(q, k, v, log_f)

        ya = jnp.einsum('bsc,cd->bsd', ya, w_branch_a[l])
        yb = jnp.einsum('bsc,cd->bsd', yb, w_branch_b[l])
        merged = jax.nn.sigmoid(ga) * ya + jax.nn.sigmoid(gb) * yb
        mix = jnp.einsum('bsd,de->bse', merged, w_out[l])
        x = x + g1[:, None, :] * mix

        h2 = rmsnorm(x, norm2_g[l]) * (1.0 + sc2[:, None, :]) + sh2[:, None, :]
        u = jnp.einsum('bsd,df->bsf', h2, w_up[l])
        u = causal_dwconv(u, conv_ffn_w[l])
        gate, val = jnp.split(u, 2, axis=-1)
        ff = jnp.einsum('bsf,fd->bsd', jax.nn.silu(gate) * val, w_down[l])
        x = x + g2[:, None, :] * ff
    return x


import jax as _jax
import jax.numpy as _jnp

TWIN_FORMAT = 'train_step'
FWD_PARAMS = ['x', 'c', 'w_ada', 'b_ada', 'norm1_g', 'w_in', 'b_f', 'conv_a_w', 'q_norm_g', 'k_norm_g', 'w_branch_a', 'w_branch_b', 'w_out', 'norm2_g', 'w_up', 'conv_ffn_w', 'w_down']
TWIN_WEIGHTS = ['w_ada', 'b_ada', 'norm1_g', 'w_in', 'b_f', 'conv_a_w', 'q_norm_g', 'k_norm_g', 'w_branch_a', 'w_branch_b', 'w_out', 'norm2_g', 'w_up', 'conv_ffn_w', 'w_down']
TWIN_DIFF_INPUT = 'x'
TWIN_INPUTS = ['x', 'c', 'w_ada', 'b_ada', 'norm1_g', 'w_in', 'b_f', 'conv_a_w', 'q_norm_g', 'k_norm_g', 'w_branch_a', 'w_branch_b', 'w_out', 'norm2_g', 'w_up', 'conv_ffn_w', 'w_down', 'loss_target', 'm_w_ada', 'm_b_ada', 'm_norm1_g', 'm_w_in', 'm_b_f', 'm_conv_a_w', 'm_q_norm_g', 'm_k_norm_g', 'm_w_branch_a', 'm_w_branch_b', 'm_w_out', 'm_norm2_g', 'm_w_up', 'm_conv_ffn_w', 'm_w_down', 'v_w_ada', 'v_b_ada', 'v_norm1_g', 'v_w_in', 'v_b_f', 'v_conv_a_w', 'v_q_norm_g', 'v_k_norm_g', 'v_w_branch_a', 'v_w_branch_b', 'v_w_out', 'v_norm2_g', 'v_w_up', 'v_conv_ffn_w', 'v_w_down']
TWIN_OUTPUTS = ['loss', 'grad_x', 'grad_w_ada', 'grad_b_ada', 'grad_norm1_g', 'grad_w_in', 'grad_b_f', 'grad_conv_a_w', 'grad_q_norm_g', 'grad_k_norm_g', 'grad_w_branch_a', 'grad_w_branch_b', 'grad_w_out', 'grad_norm2_g', 'grad_w_up', 'grad_conv_ffn_w', 'grad_w_down', 'delta_w_ada', 'delta_b_ada', 'delta_norm1_g', 'delta_w_in', 'delta_b_f', 'delta_conv_a_w', 'delta_q_norm_g', 'delta_k_norm_g', 'delta_w_branch_a', 'delta_w_branch_b', 'delta_w_out', 'delta_norm2_g', 'delta_w_up', 'delta_conv_ffn_w', 'delta_w_down', 'new_m_w_ada', 'new_m_b_ada', 'new_m_norm1_g', 'new_m_w_in', 'new_m_b_f', 'new_m_conv_a_w', 'new_m_q_norm_g', 'new_m_k_norm_g', 'new_m_w_branch_a', 'new_m_w_branch_b', 'new_m_w_out', 'new_m_norm2_g', 'new_m_w_up', 'new_m_conv_ffn_w', 'new_m_w_down', 'new_v_w_ada', 'new_v_b_ada', 'new_v_norm1_g', 'new_v_w_in', 'new_v_b_f', 'new_v_conv_a_w', 'new_v_q_norm_g', 'new_v_k_norm_g', 'new_v_w_branch_a', 'new_v_w_branch_b', 'new_v_w_out', 'new_v_norm2_g', 'new_v_w_up', 'new_v_conv_ffn_w', 'new_v_w_down']
TWIN_LEAF_KINDS = {'loss': 'loss', 'grad_x': 'grad_x', 'grad_w_ada': 'grad_w', 'grad_b_ada': 'grad_w', 'grad_norm1_g': 'grad_w', 'grad_w_in': 'grad_w', 'grad_b_f': 'grad_w', 'grad_conv_a_w': 'grad_w', 'grad_q_norm_g': 'grad_w', 'grad_k_norm_g': 'grad_w', 'grad_w_branch_a': 'grad_w', 'grad_w_branch_b': 'grad_w', 'grad_w_out': 'grad_w', 'grad_norm2_g': 'grad_w', 'grad_w_up': 'grad_w', 'grad_conv_ffn_w': 'grad_w', 'grad_w_down': 'grad_w', 'delta_w_ada': 'delta_w', 'delta_b_ada': 'delta_w', 'delta_norm1_g': 'delta_w', 'delta_w_in': 'delta_w', 'delta_b_f': 'delta_w', 'delta_conv_a_w': 'delta_w', 'delta_q_norm_g': 'delta_w', 'delta_k_norm_g': 'delta_w', 'delta_w_branch_a': 'delta_w', 'delta_w_branch_b': 'delta_w', 'delta_w_out': 'delta_w', 'delta_norm2_g': 'delta_w', 'delta_w_up': 'delta_w', 'delta_conv_ffn_w': 'delta_w', 'delta_w_down': 'delta_w', 'new_m_w_ada': 'new_m', 'new_m_b_ada': 'new_m', 'new_m_norm1_g': 'new_m', 'new_m_w_in': 'new_m', 'new_m_b_f': 'new_m', 'new_m_conv_a_w': 'new_m', 'new_m_q_norm_g': 'new_m', 'new_m_k_norm_g': 'new_m', 'new_m_w_branch_a': 'new_m', 'new_m_w_branch_b': 'new_m', 'new_m_w_out': 'new_m', 'new_m_norm2_g': 'new_m', 'new_m_w_up': 'new_m', 'new_m_conv_ffn_w': 'new_m', 'new_m_w_down': 'new_m', 'new_v_w_ada': 'new_v', 'new_v_b_ada': 'new_v', 'new_v_norm1_g': 'new_v', 'new_v_w_in': 'new_v', 'new_v_b_f': 'new_v', 'new_v_conv_a_w': 'new_v', 'new_v_q_norm_g': 'new_v', 'new_v_k_norm_g': 'new_v', 'new_v_w_branch_a': 'new_v', 'new_v_w_branch_b': 'new_v', 'new_v_w_out': 'new_v', 'new_v_norm2_g': 'new_v', 'new_v_w_up': 'new_v', 'new_v_conv_ffn_w': 'new_v', 'new_v_w_down': 'new_v'}


def _forward(args):
    return _fwd_reference(*[args[k] for k in FWD_PARAMS])


def _output_shape():
    def fwd():
        inp = _fwd_setup_inputs(0)
        return _fwd_reference(*[inp[k] for k in FWD_PARAMS])
    out = _jax.eval_shape(fwd)
    return out.shape, out.dtype

N_MICROBATCH = 1
ADAM_LR = 0.001
ADAM_B1 = 0.9
ADAM_B2 = 0.999
ADAM_EPS = 1e-08
ADAM_WD = 0.01
ADAM_STEP = 10
PER_EXAMPLE_BATCH_AXIS = {'x': 0, 'c': 0, 'loss_target': 0}
SHARED_INPUTS = []
_WEIGHT_DTYPES = {'w_ada': _jnp.float32, 'b_ada': _jnp.float32, 'norm1_g': _jnp.float32, 'w_in': _jnp.float32, 'b_f': _jnp.float32, 'conv_a_w': _jnp.float32, 'q_norm_g': _jnp.float32, 'k_norm_g': _jnp.float32, 'w_branch_a': _jnp.float32, 'w_branch_b': _jnp.float32, 'w_out': _jnp.float32, 'norm2_g': _jnp.float32, 'w_up': _jnp.float32, 'conv_ffn_w': _jnp.float32, 'w_down': _jnp.float32}
MOMENT_SCALE = {'w_ada': 1.911711e+00, 'b_ada': 5.407321e+00, 'norm1_g': 9.135728e+00, 'w_in': 1.553594e-01, 'b_f': 6.831398e+00, 'conv_a_w': 3.498561e+00, 'q_norm_g': 1.081620e+00, 'k_norm_g': 1.079304e+00, 'w_branch_a': 1.763009e-01, 'w_branch_b': 9.932727e-02, 'w_out': 1.715829e-01, 'norm2_g': 6.865343e+00, 'w_up': 1.829261e-01, 'conv_ffn_w': 1.041671e+00, 'w_down': 1.271664e-01}


def _to_microbatches(a, axis):
    t = _jnp.moveaxis(a, axis, 0)
    t = t.reshape((N_MICROBATCH, t.shape[0] // N_MICROBATCH) + t.shape[1:])
    return _jnp.moveaxis(t, 1, axis + 1)


def setup_inputs(seed: int = 0) -> dict:
    inp = _fwd_setup_inputs(seed)
    key = _jax.random.fold_in(_jax.random.key(seed), 7919)
    shape, _ = _output_shape()
    out = dict(inp)
    out["loss_target"] = _jax.random.normal(_jax.random.fold_in(key, 0), shape, _jnp.float32)
    for i, name in enumerate(TWIN_WEIGHTS):
        w = inp[name].astype(_jnp.float32)
        if MOMENT_SCALE is None:
            s = _jnp.sqrt(_jnp.mean(_jnp.square(w)) + 1e-30)
        else:
            s = MOMENT_SCALE[name]
        km, kv = _jax.random.split(_jax.random.fold_in(key, i + 1))
        out[name] = w
        out["m_" + name] = s * _jax.random.normal(km, w.shape, _jnp.float32)
        out["v_" + name] = (s * s) * _jax.random.uniform(kv, w.shape, _jnp.float32, 0.5, 1.5)
    if N_MICROBATCH > 1:
        for name, axis in PER_EXAMPLE_BATCH_AXIS.items():
            out[name] = _to_microbatches(out[name], axis)
    return {'x': out['x'], 'c': out['c'], 'w_ada': out['w_ada'], 'b_ada': out['b_ada'], 'norm1_g': out['norm1_g'], 'w_in': out['w_in'], 'b_f': out['b_f'], 'conv_a_w': out['conv_a_w'], 'q_norm_g': out['q_norm_g'], 'k_norm_g': out['k_norm_g'], 'w_branch_a': out['w_branch_a'], 'w_branch_b': out['w_branch_b'], 'w_out': out['w_out'], 'norm2_g': out['norm2_g'], 'w_up': out['w_up'], 'conv_ffn_w': out['conv_ffn_w'], 'w_down': out['w_down'], 'loss_target': out['loss_target'], 'm_w_ada': out['m_w_ada'], 'm_b_ada': out['m_b_ada'], 'm_norm1_g': out['m_norm1_g'], 'm_w_in': out['m_w_in'], 'm_b_f': out['m_b_f'], 'm_conv_a_w': out['m_conv_a_w'], 'm_q_norm_g': out['m_q_norm_g'], 'm_k_norm_g': out['m_k_norm_g'], 'm_w_branch_a': out['m_w_branch_a'], 'm_w_branch_b': out['m_w_branch_b'], 'm_w_out': out['m_w_out'], 'm_norm2_g': out['m_norm2_g'], 'm_w_up': out['m_w_up'], 'm_conv_ffn_w': out['m_conv_ffn_w'], 'm_w_down': out['m_w_down'], 'v_w_ada': out['v_w_ada'], 'v_b_ada': out['v_b_ada'], 'v_norm1_g': out['v_norm1_g'], 'v_w_in': out['v_w_in'], 'v_b_f': out['v_b_f'], 'v_conv_a_w': out['v_conv_a_w'], 'v_q_norm_g': out['v_q_norm_g'], 'v_k_norm_g': out['v_k_norm_g'], 'v_w_branch_a': out['v_w_branch_a'], 'v_w_branch_b': out['v_w_branch_b'], 'v_w_out': out['v_w_out'], 'v_norm2_g': out['v_norm2_g'], 'v_w_up': out['v_w_up'], 'v_conv_ffn_w': out['v_conv_ffn_w'], 'v_w_down': out['v_w_down']}


def _loss(weights, diff, rest, loss_target):
    with _jax.named_scope("forward"):
        args = {**rest, TWIN_DIFF_INPUT: diff, **{k: w.astype(_WEIGHT_DTYPES[k]) for k, w in weights.items()}}
        y = _forward(args)
    with _jax.named_scope("loss_head"):
        err = _jnp.square(y.astype(_jnp.float32) - loss_target)
        return 0.5 * _jnp.sum(_jnp.mean(err, axis=-1)) if err.ndim else 0.5 * err


def _adamw(w, g, m, v):
    m = ADAM_B1 * m + (1.0 - ADAM_B1) * g
    v = ADAM_B2 * v + (1.0 - ADAM_B2) * _jnp.square(g)
    m_hat = m / (1.0 - ADAM_B1 ** ADAM_STEP)
    v_hat = v / (1.0 - ADAM_B2 ** ADAM_STEP)
    delta = -ADAM_LR * (m_hat / (_jnp.sqrt(v_hat) + ADAM_EPS) + ADAM_WD * w)
    return delta, m, v


def reference(x, c, w_ada, b_ada, norm1_g, w_in, b_f, conv_a_w, q_norm_g, k_norm_g, w_branch_a, w_branch_b, w_out, norm2_g, w_up, conv_ffn_w, w_down, loss_target, m_w_ada, m_b_ada, m_norm1_g, m_w_in, m_b_f, m_conv_a_w, m_q_norm_g, m_k_norm_g, m_w_branch_a, m_w_branch_b, m_w_out, m_norm2_g, m_w_up, m_conv_ffn_w, m_w_down, v_w_ada, v_b_ada, v_norm1_g, v_w_in, v_b_f, v_conv_a_w, v_q_norm_g, v_k_norm_g, v_w_branch_a, v_w_branch_b, v_w_out, v_norm2_g, v_w_up, v_conv_ffn_w, v_w_down):
    given = dict(x=x, c=c, w_ada=w_ada, b_ada=b_ada, norm1_g=norm1_g, w_in=w_in, b_f=b_f, conv_a_w=conv_a_w, q_norm_g=q_norm_g, k_norm_g=k_norm_g, w_branch_a=w_branch_a, w_branch_b=w_branch_b, w_out=w_out, norm2_g=norm2_g, w_up=w_up, conv_ffn_w=conv_ffn_w, w_down=w_down, loss_target=loss_target, m_w_ada=m_w_ada, m_b_ada=m_b_ada, m_norm1_g=m_norm1_g, m_w_in=m_w_in, m_b_f=m_b_f, m_conv_a_w=m_conv_a_w, m_q_norm_g=m_q_norm_g, m_k_norm_g=m_k_norm_g, m_w_branch_a=m_w_branch_a, m_w_branch_b=m_w_branch_b, m_w_out=m_w_out, m_norm2_g=m_norm2_g, m_w_up=m_w_up, m_conv_ffn_w=m_conv_ffn_w, m_w_down=m_w_down, v_w_ada=v_w_ada, v_b_ada=v_b_ada, v_norm1_g=v_norm1_g, v_w_in=v_w_in, v_b_f=v_b_f, v_conv_a_w=v_conv_a_w, v_q_norm_g=v_q_norm_g, v_k_norm_g=v_k_norm_g, v_w_branch_a=v_w_branch_a, v_w_branch_b=v_w_branch_b, v_w_out=v_w_out, v_norm2_g=v_norm2_g, v_w_up=v_w_up, v_conv_ffn_w=v_conv_ffn_w, v_w_down=v_w_down)
    weights = {n: given[n] for n in TWIN_WEIGHTS}
    shared = {n: given[n] for n in SHARED_INPUTS}
    per_example = {n: given[n] for n in ['x', 'c']}
    grad_fn = _jax.value_and_grad(_loss, argnums=(0, 1))

    def one_microbatch(ex, loss_target):
        ex = dict(ex)
        diff = ex.pop(TWIN_DIFF_INPUT)
        return grad_fn(weights, diff, {**shared, **ex}, loss_target)

    if N_MICROBATCH == 1:
        loss, (grad_w, grad_x) = one_microbatch(per_example, given["loss_target"])
    else:
        def body(carry, xs):
            loss_sum, grad_sum = carry
            l_k, (gw_k, gx_k) = one_microbatch(xs[0], xs[1])
            with _jax.named_scope("update"):
                return (loss_sum + l_k, _jax.tree.map(_jnp.add, grad_sum, gw_k)), gx_k

        init = (_jnp.zeros((), _jnp.float32), _jax.tree.map(_jnp.zeros_like, weights))
        (loss, grad_w), grad_x = _jax.lax.scan(body, init, (per_example, given["loss_target"]))
    with _jax.named_scope("update"):
        delta_w, new_m, new_v = {}, {}, {}
        for n in TWIN_WEIGHTS:
            delta_w[n], new_m[n], new_v[n] = _adamw(weights[n], grad_w[n], given["m_" + n], given["v_" + n])
    return (loss, grad_x, *[grad_w[n] for n in TWIN_WEIGHTS], *[delta_w[n] for n in TWIN_WEIGHTS],
            *[new_m[n] for n in TWIN_WEIGHTS], *[new_v[n] for n in TWIN_WEIGHTS])
```

```python
import functools

import jax
import jax.numpy as jnp
import numpy as np
from jax import lax
from jax.experimental import pallas as pl
from jax.experimental.pallas import tpu as pltpu

F32 = jnp.float32
BF16 = jnp.bfloat16

N_DEV = 8
D = 1024
N_HEADS = 8
HEAD_DIM = 64
ATTN_W = 512
CONV_W = 512
D_FF = 2816
N_MOD = 6
IN_W = 5128
RMS_EPS = 1e-6
NEG_INF = -1e30

IN_W_PAD = 5376
COL_GA = 3072
COL_GB = 4096
COL_F = 5120
F_PAD = 128

ADAM_LR = 0.001
ADAM_B1 = 0.9
ADAM_B2 = 0.999
ADAM_EPS = 1e-08
ADAM_WD = 0.01
ADAM_STEP = 10

LANES = 128
SUBLANES = 8
VMEM_LIMIT = 52 * 1024 * 1024
TOKEN_TILE = 512
ATTN_BLOCK = 512
PACK_W = 1024

MESH = pl.DeviceIdType.MESH
AXES = ("x", "y", "c")


def _params(sem=None, **kw):
    return pltpu.CompilerParams(dimension_semantics=sem, vmem_limit_bytes=VMEM_LIMIT, **kw)


def _full(shape):
    nd = len(shape)
    return pl.BlockSpec(shape, lambda *_: (0,) * nd)


def _matmul(a, b, *, name, tm, tn, tk, out_dtype=F32, trans_a=False):
    if trans_a:
        K, M = a.shape
    else:
        M, K = a.shape
    N = b.shape[1]
    assert b.shape[0] == K and M % tm == 0 and N % tn == 0 and K % tk == 0, (name, a.shape, b.shape)
    nm, nn, nk = M // tm, N // tn, K // tk

    def body(a_ref, b_ref, o_ref, acc_ref):
        k = pl.program_id(2)

        @pl.when(k == 0)
        def _():
            acc_ref[...] = jnp.zeros_like(acc_ref)

        av = a_ref[...].astype(BF16)
        bv = b_ref[...].astype(BF16)
        if trans_a:
            acc_ref[...] += lax.dot_general(av, bv, (((0,), (0,)), ((), ())), preferred_element_type=F32)
        else:
            acc_ref[...] += jnp.dot(av, bv, preferred_element_type=F32)

        @pl.when(k == nk - 1)
        def _():
            o_ref[...] = acc_ref[...].astype(out_dtype)

    if trans_a:
        a_spec = pl.BlockSpec((tk, tm), lambda j, i, k: (k, i))
    else:
        a_spec = pl.BlockSpec((tm, tk), lambda j, i, k: (i, k))
    return pl.pallas_call(
        body, name=name, grid=(nn, nm, nk),
        in_specs=[a_spec, pl.BlockSpec((tk, tn), lambda j, i, k: (k, j))],
        out_specs=pl.BlockSpec((tm, tn), lambda j, i, k: (i, j)),
        out_shape=jax.ShapeDtypeStruct((M, N), out_dtype),
        scratch_shapes=[pltpu.VMEM((tm, tn), F32)],
        compiler_params=_params(("parallel", "parallel", "arbitrary")),
    )(a, b)


def _split_dot(x, mat, parts):
    out = None
    rem = x
    for p in range(parts):
        piece = rem.astype(BF16)
        term = jnp.dot(piece, mat, preferred_element_type=F32)
        out = term if out is None else out + term
        if p + 1 < parts:
            rem = rem - piece.astype(F32)
    return out


def _sigmoid(x):
    return 1.0 / (1.0 + jnp.exp(-x))


def _rows8(x):
    r, c = x.shape
    return jnp.sum(x.reshape(r // SUBLANES, SUBLANES, c), axis=0)


def _shift_down(blk, prev8, n):
    rolled = pltpu.roll(blk, n, axis=0)
    prev_rolled = pltpu.roll(prev8, n, axis=0)
    rows = lax.broadcasted_iota(jnp.int32, prev8.shape, 0)
    first = jnp.where(rows < n, prev_rolled, rolled[0:SUBLANES])
    return jnp.concatenate([first, rolled[SUBLANES:]], axis=0)


def _shift_up(blk, next8, n):
    r = blk.shape[0]
    rolled = pltpu.roll(blk, r - n, axis=0)
    next_rolled = pltpu.roll(next8, SUBLANES - n, axis=0)
    rows = lax.broadcasted_iota(jnp.int32, next8.shape, 0)
    last = jnp.where(rows >= SUBLANES - n, next_rolled, rolled[r - SUBLANES:])
    return jnp.concatenate([rolled[:r - SUBLANES], last], axis=0)


def _prev_spec(tm, width, col):
    per = tm // SUBLANES
    return pl.BlockSpec((SUBLANES, width), lambda i, *_: (jnp.maximum(i * per - 1, 0), col))


def _next_spec(tm, width, col, n_tiles):
    per = tm // SUBLANES
    last = n_tiles * per - 1
    return pl.BlockSpec((SUBLANES, width), lambda i, *_: (jnp.minimum((i + 1) * per, last), col))


def _group_matrix():
    idx = np.arange(ATTN_W) // HEAD_DIM
    return jnp.asarray((idx[:, None] == idx[None, :]).astype(np.float32), BF16)


def _head_sum_matrix():
    idx = np.arange(ATTN_W) // HEAD_DIM
    return jnp.asarray((idx[:, None] == np.arange(F_PAD)[None, :]).astype(np.float32), BF16)


def _norm_mod(x, g, sc, sh, *, name):
    T = x.shape[0]
    tm = min(TOKEN_TILE, T)

    def body(x_ref, g_ref, sc_ref, sh_ref, o_ref):
        xv = x_ref[...]
        inv = lax.rsqrt(jnp.mean(xv * xv, axis=-1, keepdims=True) + RMS_EPS)
        o_ref[...] = ((xv * inv) * g_ref[...] * (1.0 + sc_ref[...]) + sh_ref[...]).astype(BF16)

    row = pl.BlockSpec((tm, D), lambda i: (i, 0))
    return pl.pallas_call(
        body, name=name, grid=(T // tm,),
        in_specs=[row, _full((1, D)), _full((1, D)), _full((1, D))],
        out_specs=row, out_shape=jax.ShapeDtypeStruct((T, D), BF16),
        compiler_params=_params(("parallel",)),
    )(x, g, sc, sh)


def _branch_prep(proj, conv_w8, qg, kg, bf_pad, gmat):
    T = proj.shape[0]
    tm = min(TOKEN_TILE, T)
    nt = T // tm

    def body(cb_ref, cc_ref, cv_ref, q_ref, k_ref, f_ref, ccp_ref, cvp_ref, w_ref, qg_ref, kg_ref, bf_ref, g_ref,
             ya_ref, qs_ref, kh_ref, lf_ref):
        i = pl.program_id(0)
        z = cc_ref[...] * cv_ref[...]
        zp = jnp.where(i > 0, ccp_ref[...] * cvp_ref[...], 0.0)
        w = w_ref[...]
        cz = _shift_down(z, zp, 2) * w[0:1] + _shift_down(z, zp, 1) * w[1:2] + z * w[2:3]
        ya_ref[...] = (cb_ref[...] * cz).astype(BF16)
        gm = g_ref[...]
        for src, gain, dst, scale in ((q_ref, qg_ref, qs_ref, 1.0 / np.sqrt(HEAD_DIM)), (k_ref, kg_ref, kh_ref, 1.0)):
            v = src[...]
            ms = _split_dot(v * v, gm, 2) * (1.0 / HEAD_DIM)
            dst[...] = ((v * lax.rsqrt(ms + RMS_EPS)) * gain[...] * scale).astype(BF16)
        fx = f_ref[...] + bf_ref[...]
        lf_ref[...] = jnp.minimum(fx, 0.0) - jnp.log(1.0 + jnp.exp(-jnp.abs(fx)))

    blk = lambda col: pl.BlockSpec((tm, CONV_W), lambda i: (i, col))
    out_blk = pl.BlockSpec((tm, CONV_W), lambda i: (i, 0))
    f_blk = pl.BlockSpec((tm, F_PAD), lambda i: (i, COL_F // F_PAD))
    return pl.pallas_call(
        body, name="branch_prep", grid=(nt,),
        in_specs=[blk(0), blk(1), blk(2), blk(3), blk(4), f_blk,
                  _prev_spec(tm, CONV_W, 1), _prev_spec(tm, CONV_W, 2),
                  _full((SUBLANES, CONV_W)), _full((1, ATTN_W)), _full((1, ATTN_W)), _full((1, F_PAD)),
                  _full((ATTN_W, ATTN_W))],
        out_specs=[out_blk, out_blk, out_blk, pl.BlockSpec((tm, F_PAD), lambda i: (i, 0))],
        out_shape=[jax.ShapeDtypeStruct((T, CONV_W), BF16)] * 3 + [jax.ShapeDtypeStruct((T, F_PAD), F32)],
        compiler_params=_params(("parallel",)),
    )(proj, proj, proj, proj, proj, proj, proj, proj, conv_w8, qg, kg, bf_pad, gmat)


def _cumsum(x, *, reverse, name):
    T = x.shape[0]
    tm = min(TOKEN_TILE, T)
    nt = T // tm

    def body(x_ref, o_ref, carry_ref):
        i = pl.program_id(0)

        @pl.when(i == 0)
        def _():
            carry_ref[...] = jnp.zeros_like(carry_ref)

        r = lax.broadcasted_iota(jnp.int32, (tm, tm), 0)
        c = lax.broadcasted_iota(jnp.int32, (tm, tm), 1)
        tri = jnp.where((c >= r) if reverse else (c <= r), 1.0, 0.0).astype(BF16)
        xv = x_ref[...]
        out = _split_dot_left(tri, xv, 3) + carry_ref[0:1]
        o_ref[...] = out
        carry_ref[...] = jnp.broadcast_to(out[0:1] if reverse else out[tm - 1:tm], carry_ref.shape)

    idx = (lambda i: (nt - 1 - i, 0)) if reverse else (lambda i: (i, 0))
    return pl.pallas_call(
        body, name=name, grid=(nt,),
        in_specs=[pl.BlockSpec((tm, F_PAD), idx)], out_specs=pl.BlockSpec((tm, F_PAD), idx),
        out_shape=jax.ShapeDtypeStruct((T, F_PAD), F32),
        scratch_shapes=[pltpu.VMEM((SUBLANES, F_PAD), F32)],
        compiler_params=_params(("arbitrary",)),
    )(x)


def _split_dot_left(mat, x, parts):
    out = None
    rem = x
    for p in range(parts):
        piece = rem.astype(BF16)
        term = jnp.dot(mat, piece, preferred_element_type=F32)
        out = term if out is None else out + term
        if p + 1 < parts:
            rem = rem - piece.astype(F32)
    return out


def _merge_fwd(ya, yb, proj):
    T = ya.shape[0]
    tm = min(TOKEN_TILE, T)

    def body(ya_ref, yb_ref, ga_ref, gb_ref, o_ref):
        o_ref[...] = (_sigmoid(ga_ref[...]) * ya_ref[...] + _sigmoid(gb_ref[...]) * yb_ref[...]).astype(BF16)

    row = pl.BlockSpec((tm, D), lambda i: (i, 0))
    return pl.pallas_call(
        body, name="merge_fwd", grid=(T // tm,),
        in_specs=[row, row, pl.BlockSpec((tm, D), lambda i: (i, COL_GA // D)),
                  pl.BlockSpec((tm, D), lambda i: (i, COL_GB // D))],
        out_specs=row, out_shape=jax.ShapeDtypeStruct((T, D), BF16),
        compiler_params=_params(("parallel",)),
    )(ya, yb, proj, proj)


def _resid_norm(x, mix, g1, g, sc, sh):
    T = x.shape[0]
    tm = min(TOKEN_TILE, T)

    def body(x_ref, mix_ref, g1_ref, g_ref, sc_ref, sh_ref, x1_ref, h_ref):
        x1 = x_ref[...] + g1_ref[...] * mix_ref[...]
        x1_ref[...] = x1
        inv = lax.rsqrt(jnp.mean(x1 * x1, axis=-1, keepdims=True) + RMS_EPS)
        h_ref[...] = ((x1 * inv) * g_ref[...] * (1.0 + sc_ref[...]) + sh_ref[...]).astype(BF16)

    row = pl.BlockSpec((tm, D), lambda i: (i, 0))
    vec = _full((1, D))
    return pl.pallas_call(
        body, name="resid_norm", grid=(T // tm,),
        in_specs=[row, row, vec, vec, vec, vec], out_specs=[row, row],
        out_shape=[jax.ShapeDtypeStruct((T, D), F32), jax.ShapeDtypeStruct((T, D), BF16)],
        compiler_params=_params(("parallel",)),
    )(x, mix, g1, g, sc, sh)


FFN_TM = 256
FFN_TC = 1408


def _ffn_act_fwd(u, w8):
    T = u.shape[0]
    tm = min(FFN_TM, T)
    nt = T // tm
    nc = D_FF // FFN_TC

    def body(ug_ref, uv_ref, ugp_ref, uvp_ref, wg_ref, wv_ref, o_ref):
        i = pl.program_id(1)

        def conv(u_ref, p_ref, w_ref):
            uv = u_ref[...]
            up = jnp.where(i > 0, p_ref[...], 0.0)
            w = w_ref[...]
            return _shift_down(uv, up, 2) * w[0:1] + _shift_down(uv, up, 1) * w[1:2] + uv * w[2:3]

        gate = conv(ug_ref, ugp_ref, wg_ref)
        val = conv(uv_ref, uvp_ref, wv_ref)
        o_ref[...] = (gate * _sigmoid(gate) * val).astype(BF16)

    per = tm // SUBLANES
    blk = lambda off: pl.BlockSpec((tm, FFN_TC), lambda j, i: (i, j + off))
    prev = lambda off: pl.BlockSpec((SUBLANES, FFN_TC), lambda j, i: (jnp.maximum(i * per - 1, 0), j + off))
    wblk = lambda off: pl.BlockSpec((SUBLANES, FFN_TC), lambda j, i: (0, j + off))
    return pl.pallas_call(
        body, name="ffn_act_fwd", grid=(nc, nt),
        in_specs=[blk(0), blk(nc), prev(0), prev(nc), wblk(0), wblk(nc)],
        out_specs=pl.BlockSpec((tm, FFN_TC), lambda j, i: (i, j)),
        out_shape=jax.ShapeDtypeStruct((T, D_FF), BF16),
        compiler_params=_params(("parallel", "parallel")),
    )(u, u, u, u, w8, w8)


def _loss_head(x1, ff, g2, target):
    T = x1.shape[0]
    tm = min(TOKEN_TILE, T)

    def body(x1_ref, ff_ref, g2_ref, t_ref, dy_ref, dff_ref, loss_ref, dg2_ref):
        i = pl.program_id(0)

        @pl.when(i == 0)
        def _():
            loss_ref[...] = jnp.zeros_like(loss_ref)
            dg2_ref[...] = jnp.zeros_like(dg2_ref)

        ff = ff_ref[...]
        err = x1_ref[...] + g2_ref[...] * ff - t_ref[...]
        dy = err * (1.0 / D)
        dy_ref[...] = dy
        dff_ref[...] = (dy * g2_ref[...]).astype(BF16)
        loss_ref[...] += _rows8(err * err)
        dg2_ref[...] += _rows8(dy * ff)

    row = pl.BlockSpec((tm, D), lambda i: (i, 0))
    acc = _full((SUBLANES, D))
    return pl.pallas_call(
        body, name="loss_head", grid=(T // tm,),
        in_specs=[row, row, _full((1, D)), row], out_specs=[row, row, acc, acc],
        out_shape=[jax.ShapeDtypeStruct((T, D), F32), jax.ShapeDtypeStruct((T, D), BF16),
                   jax.ShapeDtypeStruct((SUBLANES, D), F32), jax.ShapeDtypeStruct((SUBLANES, D), F32)],
        compiler_params=_params(("arbitrary",)),
    )(x1, ff, g2, target)


def _nt_dot(a, b):
    return lax.dot_general(a, b, (((1,), (1,)), ((), ())), preferred_element_type=F32)


def _attn_fwd(qs, kh, vh, f_col, f_row):
    nh, T, hd = qs.shape
    bq = min(ATTN_BLOCK, T)
    nq = T // bq

    def body(q_ref, k_ref, v_ref, fq_ref, fk_ref, o_ref, lse_ref):
        qi = pl.program_id(1)
        q = q_ref[0]
        fq = fq_ref[0]

        def step(kb, carry, masked):
            m, l, acc = carry
            start = pl.multiple_of(kb * bq, bq)
            k = k_ref[0, pl.ds(start, bq), :]
            v = v_ref[0, pl.ds(start, bq), :]
            s = _nt_dot(q, k) + (fq - fk_ref[0, :, pl.ds(start, bq)])
            if masked:
                r = lax.broadcasted_iota(jnp.int32, (bq, bq), 0)
                c = lax.broadcasted_iota(jnp.int32, (bq, bq), 1)
                s = jnp.where(c <= r, s, NEG_INF)
            m_new = jnp.maximum(m, jnp.max(s, axis=-1, keepdims=True))
            alpha = jnp.exp(m - m_new)
            p = jnp.exp(s - m_new)
            l = alpha * l + jnp.sum(p, axis=-1, keepdims=True)
            acc = alpha * acc + jnp.dot(p.astype(BF16), v, preferred_element_type=F32)
            return m_new, l, acc

        init = (jnp.full((bq, 1), NEG_INF, F32), jnp.zeros((bq, 1), F32), jnp.zeros((bq, hd), F32))
        carry = lax.fori_loop(0, qi, lambda kb, cr: step(kb, cr, False), init)
        m, l, acc = step(qi, carry, True)
        o_ref[0] = acc / l
        lse_ref[0] = m + jnp.log(l)

    tile = pl.BlockSpec((1, bq, hd), lambda h, i: (h, i, 0))
    whole = pl.BlockSpec((1, T, hd), lambda h, i: (h, 0, 0))
    col = pl.BlockSpec((1, bq, 1), lambda h, i: (h, i, 0))
    row = pl.BlockSpec((1, 1, T), lambda h, i: (h, 0, 0))
    return pl.pallas_call(
        body, name="attn_fwd", grid=(nh, nq),
        in_specs=[tile, whole, whole, col, row], out_specs=[tile, col],
        out_shape=[jax.ShapeDtypeStruct((nh, T, hd), F32), jax.ShapeDtypeStruct((nh, T, 1), F32)],
        compiler_params=_params(("parallel", "parallel")),
    )(qs, kh, vh, f_col, f_row)


def _attn_bwd_kv(qs, kh, vh, do, f_col, f_row, lse_row, delta_row):
    nh, T, hd = qs.shape
    bk = min(ATTN_BLOCK, T)
    nk = T // bk

    def body(q_ref, do_ref, k_ref, v_ref, fk_ref, fq_ref, lse_ref, dl_ref, dk_ref, dv_ref, df_ref):
        ki = pl.program_id(1)
        k = k_ref[0]
        v = v_ref[0]
        fk = fk_ref[0]

        def step(qb, carry, masked):
            dk, dv, df = carry
            start = pl.multiple_of(qb * bk, bk)
            q = q_ref[0, pl.ds(start, bk), :]
            g = do_ref[0, pl.ds(start, bk), :]
            cols = pl.ds(start, bk)
            st = _nt_dot(k, q) + (fq_ref[0, :, cols] - fk)
            pt = jnp.exp(st - lse_ref[0, :, cols])
            if masked:
                r = lax.broadcasted_iota(jnp.int32, (bk, bk), 0)
                c = lax.broadcasted_iota(jnp.int32, (bk, bk), 1)
                pt = jnp.where(c >= r, pt, 0.0)
            dv = dv + jnp.dot(pt.astype(BF16), g, preferred_element_type=F32)
            dst = pt * (_nt_dot(v, g) - dl_ref[0, :, cols])
            dk = dk + jnp.dot(dst.astype(BF16), q, preferred_element_type=F32)
            df = df + jnp.sum(dst, axis=-1, keepdims=True)
            return dk, dv, df

        init = (jnp.zeros((bk, hd), F32), jnp.zeros((bk, hd), F32), jnp.zeros((bk, 1), F32))
        carry = step(ki, init, True)
        dk, dv, df = lax.fori_loop(ki + 1, nk, lambda qb, cr: step(qb, cr, False), carry)
        dk_ref[0] = dk
        dv_ref[0] = dv
        df_ref[0] = df

    tile = pl.BlockSpec((1, bk, hd), lambda h, i: (h, i, 0))
    whole = pl.BlockSpec((1, T, hd), lambda h, i: (h, 0, 0))
    col = pl.BlockSpec((1, bk, 1), lambda h, i: (h, i, 0))
    row = pl.BlockSpec((1, 1, T), lambda h, i: (h, 0, 0))
    return pl.pallas_call(
        body, name="attn_bwd_kv", grid=(nh, nk),
        in_specs=[whole, whole, tile, tile, col, row, row, row], out_specs=[tile, tile, col],
        out_shape=[jax.ShapeDtypeStruct((nh, T, hd), F32), jax.ShapeDtypeStruct((nh, T, hd), F32),
                   jax.ShapeDtypeStruct((nh, T, 1), F32)],
        compiler_params=_params(("parallel", "parallel")),
    )(qs, do, kh, vh, f_col, f_row, lse_row, delta_row)


def _attn_bwd_q(qs, kh, vh, do, f_col, f_row, lse_col, delta_col):
    nh, T, hd = qs.shape
    bq = min(ATTN_BLOCK, T)
    nq = T // bq

    def body(q_ref, do_ref, k_ref, v_ref, fq_ref, fk_ref, lse_ref, dl_ref, dq_ref, rs_ref):
        qi = pl.program_id(1)
        q = q_ref[0]
        g = do_ref[0]
        fq = fq_ref[0]
        lse = lse_ref[0]
        dl = dl_ref[0]

        def step(kb, carry, masked):
            dq, rs = carry
            start = pl.multiple_of(kb * bq, bq)
            k = k_ref[0, pl.ds(start, bq), :]
            v = v_ref[0, pl.ds(start, bq), :]
            s = _nt_dot(q, k) + (fq - fk_ref[0, :, pl.ds(start, bq)])
            p = jnp.exp(s - lse)
            if masked:
                r = lax.broadcasted_iota(jnp.int32, (bq, bq), 0)
                c = lax.broadcasted_iota(jnp.int32, (bq, bq), 1)
                p = jnp.where(c <= r, p, 0.0)
            ds = p * (_nt_dot(g, v) - dl)
            dq = dq + jnp.dot(ds.astype(BF16), k, preferred_element_type=F32)
            return dq, rs + jnp.sum(ds, axis=-1, keepdims=True)

        init = (jnp.zeros((bq, hd), F32), jnp.zeros((bq, 1), F32))
        carry = lax.fori_loop(0, qi, lambda kb, cr: step(kb, cr, False), init)
        dq_ref[0], rs_ref[0] = step(qi, carry, True)

    tile = pl.BlockSpec((1, bq, hd), lambda h, i: (h, i, 0))
    whole = pl.BlockSpec((1, T, hd), lambda h, i: (h, 0, 0))
    col = pl.BlockSpec((1, bq, 1), lambda h, i: (h, i, 0))
    row = pl.BlockSpec((1, 1, T), lambda h, i: (h, 0, 0))
    return pl.pallas_call(
        body, name="attn_bwd_q", grid=(nh, nq),
        in_specs=[tile, tile, whole, whole, col, row, col, col], out_specs=[tile, col],
        out_shape=[jax.ShapeDtypeStruct((nh, T, hd), F32), jax.ShapeDtypeStruct((nh, T, 1), F32)],
        compiler_params=_params(("parallel", "parallel")),
    )(qs, do, kh, vh, f_col, f_row, lse_col, delta_col)


def _ffn_act_bwd(u, da, w8):
    T = u.shape[0]
    tm = min(FFN_TM, T)
    nt = T // tm
    nc = D_FF // FFN_TC

    def body(ug_ref, uv_ref, ugp_ref, uvp_ref, ugn_ref, uvn_ref, da_ref, dan_ref, wg_ref, wv_ref,
             dug_ref, duv_ref, dwg_ref, dwv_ref):
        i = pl.program_id(1)

        @pl.when(i == 0)
        def _():
            dwg_ref[...] = jnp.zeros_like(dwg_ref)
            dwv_ref[...] = jnp.zeros_like(dwv_ref)

        first, last = i == 0, i == nt - 1
        wg, wv = wg_ref[...], wv_ref[...]
        zeros8 = jnp.zeros((SUBLANES, FFN_TC), F32)

        def window(u_ref, p_ref, n_ref, w):
            e = jnp.concatenate([jnp.where(first, 0.0, p_ref[...]), u_ref[...], n_ref[...]], axis=0)
            e1 = pltpu.roll(e, 1, axis=0)
            e2 = pltpu.roll(e, 2, axis=0)
            return e, e1, e2, e2 * w[0:1] + e1 * w[1:2] + e * w[2:3]

        eg, eg1, eg2, cg = window(ug_ref, ugp_ref, ugn_ref, wg)
        ev, ev1, ev2, cv = window(uv_ref, uvp_ref, uvn_ref, wv)
        dae = jnp.concatenate([zeros8, da_ref[...].astype(F32), jnp.where(last, 0.0, dan_ref[...].astype(F32))],
                              axis=0)
        sg = _sigmoid(cg)
        dgate = dae * cv * sg * (1.0 + cg * (1.0 - sg))
        dval = dae * cg * sg
        n = tm + 2 * SUBLANES

        def back(d, w):
            return d * w[2:3] + pltpu.roll(d, n - 1, axis=0) * w[1:2] + pltpu.roll(d, n - 2, axis=0) * w[0:1]

        inner = slice(SUBLANES, SUBLANES + tm)
        dug_ref[...] = back(dgate, wg)[inner].astype(BF16)
        duv_ref[...] = back(dval, wv)[inner].astype(BF16)

        def wgrad(d, e, e1, e2):
            rows = [jnp.sum((d * t)[inner], axis=0, keepdims=True) for t in (e2, e1, e)]
            return jnp.concatenate(rows + [jnp.zeros((SUBLANES - 3, FFN_TC), F32)], axis=0)

        dwg_ref[...] += wgrad(dgate, eg, eg1, eg2)
        dwv_ref[...] += wgrad(dval, ev, ev1, ev2)

    per = tm // SUBLANES
    last_blk = nt * per - 1
    blk = lambda off: pl.BlockSpec((tm, FFN_TC), lambda j, i: (i, j + off))
    prev = lambda off: pl.BlockSpec((SUBLANES, FFN_TC), lambda j, i: (jnp.maximum(i * per - 1, 0), j + off))
    nxt = lambda off: pl.BlockSpec((SUBLANES, FFN_TC), lambda j, i: (jnp.minimum((i + 1) * per, last_blk), j + off))
    wblk = lambda off: pl.BlockSpec((SUBLANES, FFN_TC), lambda j, i: (0, j + off))
    dug, duv, dwg, dwv = pl.pallas_call(
        body, name="ffn_act_bwd", grid=(nc, nt),
        in_specs=[blk(0), blk(nc), prev(0), prev(nc), nxt(0), nxt(nc), blk(0), nxt(0), wblk(0), wblk(nc)],
        out_specs=[blk(0), blk(0), wblk(0), wblk(0)],
        out_shape=[jax.ShapeDtypeStruct((T, D_FF), BF16)] * 2 + [jax.ShapeDtypeStruct((SUBLANES, D_FF), F32)] * 2,
        compiler_params=_params(("parallel", "arbitrary")),
    )(u, u, u, u, u, u, da, da, w8, w8)
    return jnp.concatenate([dug, duv], axis=1), jnp.concatenate([dwg, dwv], axis=1)


def _norm_bwd(xin, dh, dres, g, sc, *, name):
    T = xin.shape[0]
    tm = min(TOKEN_TILE, T)

    def body(x_ref, dh_ref, dr_ref, g_ref, sc_ref, dx_ref, dsh_ref, dsc_ref, dg_ref):
        i = pl.program_id(0)

        @pl.when(i == 0)
        def _():
            dsh_ref[...] = jnp.zeros_like(dsh_ref)
            dsc_ref[...] = jnp.zeros_like(dsc_ref)
            dg_ref[...] = jnp.zeros_like(dg_ref)

        xv = x_ref[...]
        dh = dh_ref[...]
        gv = g_ref[...]
        one_sc = 1.0 + sc_ref[...]
        inv = lax.rsqrt(jnp.mean(xv * xv, axis=-1, keepdims=True) + RMS_EPS)
        xn = xv * inv
        dxn = dh * (gv * one_sc)
        dx_ref[...] = dr_ref[...] + inv * (dxn - xn * jnp.mean(dxn * xn, axis=-1, keepdims=True))
        dhxn = dh * xn
        dsh_ref[...] += _rows8(dh)
        dsc_ref[...] += _rows8(dhxn * gv)
        dg_ref[...] += _rows8(dhxn * one_sc)

    row = pl.BlockSpec((tm, D), lambda i: (i, 0))
    acc = _full((SUBLANES, D))
    return pl.pallas_call(
        body, name=name, grid=(T // tm,),
        in_specs=[row, row, row, _full((1, D)), _full((1, D))], out_specs=[row, acc, acc, acc],
        out_shape=[jax.ShapeDtypeStruct((T, D), F32)] + [jax.ShapeDtypeStruct((SUBLANES, D), F32)] * 3,
        compiler_params=_params(("arbitrary",)),
    )(xin, dh, dres, g, sc)


def _gate_bwd(dx1, mix, g1):
    T = dx1.shape[0]
    tm = min(TOKEN_TILE, T)

    def body(dx_ref, mix_ref, g1_ref, dmix_ref, dg1_ref):
        @pl.when(pl.program_id(0) == 0)
        def _():
            dg1_ref[...] = jnp.zeros_like(dg1_ref)

        dx = dx_ref[...]
        dmix_ref[...] = (dx * g1_ref[...]).astype(BF16)
        dg1_ref[...] += _rows8(dx * mix_ref[...])

    row = pl.BlockSpec((tm, D), lambda i: (i, 0))
    return pl.pallas_call(
        body, name="gate_bwd", grid=(T // tm,),
        in_specs=[row, row, _full((1, D))], out_specs=[row, _full((SUBLANES, D))],
        out_shape=[jax.ShapeDtypeStruct((T, D), BF16), jax.ShapeDtypeStruct((SUBLANES, D), F32)],
        compiler_params=_params(("arbitrary",)),
    )(dx1, mix, g1)


def _merge_bwd(dmerged, ya, yb, proj):
    T = ya.shape[0]
    tm = min(TOKEN_TILE, T)

    def body(dm_ref, ya_ref, yb_ref, ga_ref, gb_ref, dya_ref, dyb_ref, dga_ref, dgb_ref):
        dm = dm_ref[...]
        sa = _sigmoid(ga_ref[...])
        sb = _sigmoid(gb_ref[...])
        dya_ref[...] = (dm * sa).astype(BF16)
        dyb_ref[...] = (dm * sb).astype(BF16)
        dga_ref[...] = (dm * ya_ref[...] * sa * (1.0 - sa)).astype(BF16)
        dgb_ref[...] = (dm * yb_ref[...] * sb * (1.0 - sb)).astype(BF16)

    row = pl.BlockSpec((tm, D), lambda i: (i, 0))
    return pl.pallas_call(
        body, name="merge_bwd", grid=(T // tm,),
        in_specs=[row, row, row, pl.BlockSpec((tm, D), lambda i: (i, COL_GA // D)),
                  pl.BlockSpec((tm, D), lambda i: (i, COL_GB // D))],
        out_specs=[row] * 4, out_shape=[jax.ShapeDtypeStruct((T, D), BF16)] * 4,
        compiler_params=_params(("parallel",)),
    )(dmerged, ya, yb, proj, proj)


def _conv_branch_bwd(proj, dya0, conv_w8):
    T = proj.shape[0]
    tm = min(FFN_TM, T)
    nt = T // tm

    def body(cb_ref, cc_ref, cv_ref, cbn_ref, ccp_ref, cvp_ref, ccn_ref, cvn_ref, d_ref, dn_ref, w_ref,
             dcb_ref, dcc_ref, dcv_ref, dw_ref):
        i = pl.program_id(0)

        @pl.when(i == 0)
        def _():
            dw_ref[...] = jnp.zeros_like(dw_ref)

        first, last = i == 0, i == nt - 1
        w = w_ref[...]
        cc = jnp.concatenate([ccp_ref[...], cc_ref[...], ccn_ref[...]], axis=0)
        cv = jnp.concatenate([cvp_ref[...], cv_ref[...], cvn_ref[...]], axis=0)
        rows = lax.broadcasted_iota(jnp.int32, cc.shape, 0)
        z = jnp.where(jnp.logical_and(first, rows < SUBLANES), 0.0, cc * cv)
        z1 = pltpu.roll(z, 1, axis=0)
        z2 = pltpu.roll(z, 2, axis=0)
        cz = z2 * w[0:1] + z1 * w[1:2] + z * w[2:3]
        zeros8 = jnp.zeros((SUBLANES, CONV_W), F32)
        de = jnp.concatenate([zeros8, d_ref[...], jnp.where(last, 0.0, dn_ref[...])], axis=0)
        cbe = jnp.concatenate([zeros8, cb_ref[...], cbn_ref[...]], axis=0)
        dcz = de * cbe
        n = tm + 2 * SUBLANES
        dz = dcz * w[2:3] + pltpu.roll(dcz, n - 1, axis=0) * w[1:2] + pltpu.roll(dcz, n - 2, axis=0) * w[0:1]
        inner = slice(SUBLANES, SUBLANES + tm)
        dcb_ref[...] = (de * cz)[inner].astype(BF16)
        dcc_ref[...] = (dz * cv)[inner].astype(BF16)
        dcv_ref[...] = (dz * cc)[inner].astype(BF16)
        wrows = [jnp.sum((dcz * t)[inner], axis=0, keepdims=True) for t in (z2, z1, z)]
        dw_ref[...] += jnp.concatenate(wrows + [jnp.zeros((SUBLANES - 3, CONV_W), F32)], axis=0)

    blk = lambda col: pl.BlockSpec((tm, CONV_W), lambda i: (i, col))
    out_blk = pl.BlockSpec((tm, CONV_W), lambda i: (i, 0))
    return pl.pallas_call(
        body, name="conv_branch_bwd", grid=(nt,),
        in_specs=[blk(0), blk(1), blk(2), _next_spec(tm, CONV_W, 0, nt),
                  _prev_spec(tm, CONV_W, 1), _prev_spec(tm, CONV_W, 2),
                  _next_spec(tm, CONV_W, 1, nt), _next_spec(tm, CONV_W, 2, nt),
                  out_blk, _next_spec(tm, CONV_W, 0, nt), _full((SUBLANES, CONV_W))],
        out_specs=[out_blk, out_blk, out_blk, _full((SUBLANES, CONV_W))],
        out_shape=[jax.ShapeDtypeStruct((T, CONV_W), BF16)] * 3 + [jax.ShapeDtypeStruct((SUBLANES, CONV_W), F32)],
        compiler_params=_params(("arbitrary",)),
    )(proj, proj, proj, proj, proj, proj, proj, proj, dya0, dya0, conv_w8)


def _delta(do, o, hsum):
    T = do.shape[0]
    tm = min(TOKEN_TILE, T)

    def body(do_ref, o_ref, s_ref, out_ref):
        out_ref[...] = _split_dot(do_ref[...] * o_ref[...], s_ref[...], 3)

    row = pl.BlockSpec((tm, ATTN_W), lambda i: (i, 0))
    return pl.pallas_call(
        body, name="attn_delta", grid=(T // tm,),
        in_specs=[row, row, _full((ATTN_W, F_PAD))], out_specs=pl.BlockSpec((tm, F_PAD), lambda i: (i, 0)),
        out_shape=jax.ShapeDtypeStruct((T, F_PAD), F32),
        compiler_params=_params(("parallel",)),
    )(do, o, hsum)


def _qk_norm_bwd(proj, dqs, dkh, dlogf, qg, kg, bf_pad, gmat):
    T = proj.shape[0]
    tm = min(TOKEN_TILE, T)

    def body(q_ref, k_ref, f_ref, dqs_ref, dkh_ref, dlf_ref, qg_ref, kg_ref, bf_ref, g_ref,
             dq_ref, dk_ref, dfl_ref, dqg_ref, dkg_ref, dbf_ref):
        @pl.when(pl.program_id(0) == 0)
        def _():
            dqg_ref[...] = jnp.zeros_like(dqg_ref)
            dkg_ref[...] = jnp.zeros_like(dkg_ref)
            dbf_ref[...] = jnp.zeros_like(dbf_ref)

        gm = g_ref[...]
        for src, d_src, gain, scale, dst, dgain in (
                (q_ref, dqs_ref, qg_ref, 1.0 / np.sqrt(HEAD_DIM), dq_ref, dqg_ref),
                (k_ref, dkh_ref, kg_ref, 1.0, dk_ref, dkg_ref)):
            v = src[...]
            dhat = d_src[...] * scale
            inv = lax.rsqrt(_split_dot(v * v, gm, 2) * (1.0 / HEAD_DIM) + RMS_EPS)
            vn = v * inv
            dgain[...] += _rows8(dhat * vn)
            dvn = dhat * gain[...]
            mean = _split_dot(dvn * vn, gm, 2) * (1.0 / HEAD_DIM)
            dst[...] = (inv * (dvn - vn * mean)).astype(BF16)
        fx = f_ref[...] + bf_ref[...]
        dfl = dlf_ref[...] * _sigmoid(-fx)
        dfl_ref[...] = dfl.astype(BF16)
        dbf_ref[...] += _rows8(dfl)

    blk = lambda col: pl.BlockSpec((tm, ATTN_W), lambda i: (i, col))
    out_blk = pl.BlockSpec((tm, ATTN_W), lambda i: (i, 0))
    f_in = pl.BlockSpec((tm, F_PAD), lambda i: (i, COL_F // F_PAD))
    f_blk = pl.BlockSpec((tm, F_PAD), lambda i: (i, 0))
    return pl.pallas_call(
        body, name="qk_norm_bwd", grid=(T // tm,),
        in_specs=[blk(3), blk(4), f_in, out_blk, out_blk, f_blk, _full((1, ATTN_W)), _full((1, ATTN_W)),
                  _full((1, F_PAD)), _full((ATTN_W, ATTN_W))],
        out_specs=[out_blk, out_blk, f_blk, _full((SUBLANES, ATTN_W)), _full((SUBLANES, ATTN_W)),
                   _full((SUBLANES, F_PAD))],
        out_shape=[jax.ShapeDtypeStruct((T, ATTN_W), BF16)] * 2 + [jax.ShapeDtypeStruct((T, F_PAD), BF16)]
        + [jax.ShapeDtypeStruct((SUBLANES, ATTN_W), F32)] * 2 + [jax.ShapeDtypeStruct((SUBLANES, F_PAD), F32)],
        compiler_params=_params(("arbitrary",)),
    )(proj, proj, proj, dqs, dkh, dlogf, qg, kg, bf_pad, gmat)


def _to_heads(a):
    T = a.shape[0]
    return a.reshape(T, N_HEADS, HEAD_DIM).transpose(1, 0, 2)


def _from_heads(a):
    return a.transpose(1, 0, 2).reshape(a.shape[1], ATTN_W)


def _pad_rows8(w):
    return jnp.pad(w, ((0, SUBLANES - w.shape[0]), (0, 0)))


def _fold8(acc):
    return jnp.sum(acc, axis=0, keepdims=True)


def _local_step(x, target, mod, wts):
    T = x.shape[0]
    tm = min(TOKEN_TILE, T)
    sh1, sc1, g1, sh2, sc2, g2 = [mod[:, i * D:(i + 1) * D] for i in range(N_MOD)]
    w_in, w_in_t = wts["w_in"], wts["w_in_t"]
    conv_a8 = _pad_rows8(wts["conv_a_w"])
    conv_f8 = _pad_rows8(wts["conv_ffn_w"])
    qg = jnp.tile(wts["q_norm_g"], (1, N_HEADS))
    kg = jnp.tile(wts["k_norm_g"], (1, N_HEADS))
    bf_pad = jnp.pad(wts["b_f"], ((0, 0), (0, F_PAD - N_HEADS)))
    gmat = _group_matrix()
    hsum = _head_sum_matrix()

    h = _norm_mod(x, wts["norm1_g"], sc1, sh1, name="norm1_fwd")
    proj = _matmul(h, w_in, name="mm_in", tm=tm, tn=896, tk=D)
    ya0, qs, kh, logf = _branch_prep(proj, conv_a8, qg, kg, bf_pad, gmat)
    fcum = _cumsum(logf, reverse=False, name="gate_cumsum")
    f_heads = fcum[:, :N_HEADS].T
    f_col = f_heads.reshape(N_HEADS, T, 1)
    f_row = f_heads.reshape(N_HEADS, 1, T)
    qs_h, kh_h = _to_heads(qs), _to_heads(kh)
    v_h = _to_heads(proj[:, 2560:3072].astype(BF16))
    o_h, lse = _attn_fwd(qs_h, kh_h, v_h, f_col, f_row)
    o = _from_heads(o_h)
    ya = _matmul(ya0, wts["w_branch_a"], name="mm_branch_a", tm=tm, tn=D, tk=CONV_W)
    yb = _matmul(o, wts["w_branch_b"], name="mm_branch_b", tm=tm, tn=D, tk=ATTN_W)
    merged = _merge_fwd(ya, yb, proj)
    mix = _matmul(merged, wts["w_out"], name="mm_out", tm=tm, tn=D, tk=D)
    x1, h2 = _resid_norm(x, mix, g1, wts["norm2_g"], sc2, sh2)
    u = _matmul(h2, wts["w_up"], name="mm_up", tm=tm, tn=1408, tk=D)
    act = _ffn_act_fwd(u, conv_f8)
    ff = _matmul(act, wts["w_down"], name="mm_down", tm=tm, tn=D, tk=D_FF)
    dy, dff, sq8, dg2_8 = _loss_head(x1, ff, g2, target)
    sq = jnp.sum(sq8).reshape(1, 1)

    grads = {}
    da = _matmul(dff, wts["w_down_t"], name="mm_down_dx", tm=tm, tn=1408, tk=D)
    grads["w_down"] = _matmul(act, dff, name="mm_down_dw", tm=1408, tn=D, tk=tm, trans_a=True)
    du, dconv_f8 = _ffn_act_bwd(u, da, conv_f8)
    grads["conv_ffn_w"] = dconv_f8[:3]
    dh2 = _matmul(du, wts["w_up_t"], name="mm_up_dx", tm=tm, tn=D, tk=1408)
    grads["w_up"] = _matmul(h2, du, name="mm_up_dw", tm=D, tn=1408, tk=tm, trans_a=True)
    dx1, dsh2_8, dsc2_8, dn2_8 = _norm_bwd(x1, dh2, dy, wts["norm2_g"], sc2, name="norm2_bwd")
    grads["norm2_g"] = _fold8(dn2_8)

    dmix, dg1_8 = _gate_bwd(dx1, mix, g1)
    dmerged = _matmul(dmix, wts["w_out_t"], name="mm_out_dx", tm=tm, tn=D, tk=D)
    grads["w_out"] = _matmul(merged, dmix, name="mm_out_dw", tm=D, tn=D, tk=tm, trans_a=True)
    dya, dyb, dga, dgb = _merge_bwd(dmerged, ya, yb, proj)
    dya0 = _matmul(dya, wts["w_branch_a_t"], name="mm_branch_a_dx", tm=tm, tn=CONV_W, tk=D)
    grads["w_branch_a"] = _matmul(ya0, dya, name="mm_branch_a_dw", tm=CONV_W, tn=D, tk=tm, trans_a=True)
    do = _matmul(dyb, wts["w_branch_b_t"], name="mm_branch_b_dx", tm=tm, tn=ATTN_W, tk=D)
    grads["w_branch_b"] = _matmul(o, dyb, name="mm_branch_b_dw", tm=ATTN_W, tn=D, tk=tm, trans_a=True)
    dcb, dcc, dcv, dconv_a8 = _conv_branch_bwd(proj, dya0, conv_a8)
    grads["conv_a_w"] = dconv_a8[:3]

    delta_heads = _delta(do, o, hsum)[:, :N_HEADS].T
    do_h = _to_heads(do.astype(BF16))
    dk_h, dv_h, dfk = _attn_bwd_kv(qs_h, kh_h, v_h, do_h, f_col, f_row, lse.reshape(N_HEADS, 1, T),
                                   delta_heads.reshape(N_HEADS, 1, T))
    dqs_h, dfq = _attn_bwd_q(qs_h, kh_h, v_h, do_h, f_col, f_row, lse, delta_heads.reshape(N_HEADS, T, 1))
    dfcum = jnp.pad((dfq - dfk).reshape(N_HEADS, T).T, ((0, 0), (0, F_PAD - N_HEADS)))
    dlogf = _cumsum(dfcum, reverse=True, name="gate_cumsum_bwd")
    dq, dk, dfl, dqg8, dkg8, dbf8 = _qk_norm_bwd(proj, _from_heads(dqs_h), _from_heads(dk_h), dlogf, qg, kg,
                                                 bf_pad, gmat)
    grads["q_norm_g"] = jnp.sum(_fold8(dqg8).reshape(N_HEADS, HEAD_DIM), axis=0, keepdims=True)
    grads["k_norm_g"] = jnp.sum(_fold8(dkg8).reshape(N_HEADS, HEAD_DIM), axis=0, keepdims=True)
    grads["b_f"] = _fold8(dbf8)[:, :N_HEADS]
    dproj = jnp.concatenate(
        [dcb, dcc, dcv, dq, dk, _from_heads(dv_h).astype(BF16), dga, dgb, dfl,
         jnp.zeros((T, IN_W_PAD - COL_F - F_PAD), BF16)], axis=1)
    dh = _matmul(dproj, w_in_t, name="mm_in_dx", tm=tm, tn=D, tk=896)
    grads["w_in_pad"] = _matmul(h, dproj, name="mm_in_dw", tm=D, tn=896, tk=tm, trans_a=True)
    grad_x, dsh1_8, dsc1_8, dn1_8 = _norm_bwd(x, dh, dx1, wts["norm1_g"], sc1, name="norm1_bwd")
    grads["norm1_g"] = _fold8(dn1_8)
    grads["mod"] = jnp.concatenate([_fold8(a) for a in (dsh1_8, dsc1_8, dg1_8, dsh2_8, dsc2_8, dg2_8)], axis=1)
    return sq, grad_x, grads


def _me_and_peers():
    mx, my, mc = lax.axis_index("x"), lax.axis_index("y"), lax.axis_index("c")
    me = 4 * mx + 2 * my + mc
    peers = []
    for k in range(1, N_DEV):
        px = 1 - mx if k & 4 else mx
        py = 1 - my if k & 2 else my
        pc = 1 - mc if k & 1 else mc
        peers.append(((px, py, pc), 4 * px + 2 * py + pc))
    return me, peers


HBM_SPEC = pl.BlockSpec(memory_space=pltpu.HBM)


def _exchange(x, *, name, scatter):
    out_shape = x.shape if scatter else (N_DEV,) + x.shape

    def body(x_ref, out_ref, send_sems, recv_sems, local_sem):
        me, peers = _me_and_peers()

        def src(idx):
            return x_ref.at[idx] if scatter else x_ref

        local = pltpu.make_async_copy(src(me), out_ref.at[me], local_sem)
        local.start()

        def copy(k, from_idx, to_slot, device):
            return pltpu.make_async_remote_copy(
                src_ref=src(from_idx), dst_ref=out_ref.at[to_slot], send_sem=send_sems.at[k],
                recv_sem=recv_sems.at[k], device_id=device, device_id_type=MESH)

        sends = [copy(k, idx, me, dev) for k, (dev, idx) in enumerate(peers)]
        for cp in sends:
            cp.start()
        for k, (dev, idx) in enumerate(peers):
            copy(k, idx, idx, dev).wait_recv()
        for cp in sends:
            cp.wait_send()
        local.wait()

    return pl.pallas_call(
        body, name=name, in_specs=[HBM_SPEC], out_specs=HBM_SPEC,
        out_shape=jax.ShapeDtypeStruct(out_shape, x.dtype),
        scratch_shapes=[pltpu.SemaphoreType.DMA((N_DEV - 1,)), pltpu.SemaphoreType.DMA((N_DEV - 1,)),
                        pltpu.SemaphoreType.DMA(())],
        compiler_params=pltpu.CompilerParams(has_side_effects=True),
    )(x)


def _ada_fwd(c_all, w_shard, b_shard):
    n = w_shard.shape[1]

    def body(c_ref, w_ref, b_ref, o_ref):
        cv = c_ref[...]
        act = (cv * _sigmoid(cv)).astype(BF16)
        o_ref[...] = jnp.dot(act, w_ref[...].astype(BF16), preferred_element_type=F32) + b_ref[...]

    return pl.pallas_call(
        body, name="ada_fwd", in_specs=[_full((N_DEV, D)), _full((D, n)), _full((1, n))],
        out_specs=_full((N_DEV, n)), out_shape=jax.ShapeDtypeStruct((N_DEV, n), F32), grid=(1,),
        compiler_params=_params(("arbitrary",)),
    )(c_all, w_shard, b_shard)


def _ada_bwd(c_all_t, dmod_pad):
    n = dmod_pad.shape[1]

    def body(c_ref, d_ref, o_ref):
        cv = c_ref[...]
        act = (cv * _sigmoid(cv)).astype(BF16)
        o_ref[...] = jnp.dot(act, d_ref[...].astype(BF16), preferred_element_type=F32)

    return pl.pallas_call(
        body, name="ada_bwd", in_specs=[_full((D, LANES)), _full((LANES, n))],
        out_specs=_full((D, n)), out_shape=jax.ShapeDtypeStruct((D, n), F32), grid=(1,),
        compiler_params=_params(("arbitrary",)),
    )(c_all_t, dmod_pad)


ADAM_ROWS = 64


def _adamw(parts, w, m, v, *, name):
    n, R, C = parts.shape
    tr = ADAM_ROWS if R % ADAM_ROWS == 0 else R

    def body(p_ref, w_ref, m_ref, v_ref, g_ref, d_ref, nm_ref, nv_ref):
        g = p_ref[0]
        for j in range(1, n):
            g = g + p_ref[j]
        g_ref[...] = g
        nm = ADAM_B1 * m_ref[...] + (1.0 - ADAM_B1) * g
        nv = ADAM_B2 * v_ref[...] + (1.0 - ADAM_B2) * (g * g)
        nm_ref[...] = nm
        nv_ref[...] = nv
        m_hat = nm / (1.0 - ADAM_B1 ** ADAM_STEP)
        v_hat = nv / (1.0 - ADAM_B2 ** ADAM_STEP)
        d_ref[...] = -ADAM_LR * (m_hat / (jnp.sqrt(v_hat) + ADAM_EPS) + ADAM_WD * w_ref[...])

    row = pl.BlockSpec((tr, C), lambda i: (i, 0))
    return pl.pallas_call(
        body, name=name, grid=(R // tr,),
        in_specs=[pl.BlockSpec((n, tr, C), lambda i: (0, i, 0)), row, row, row], out_specs=[row] * 4,
        out_shape=[jax.ShapeDtypeStruct((R, C), F32)] * 4,
        compiler_params=_params(("parallel",)),
    )(parts, w, m, v)


SHARDED = (("w_in", D, IN_W, 1), ("w_branch_a", CONV_W, D, 1), ("w_branch_b", ATTN_W, D, 1), ("w_out", D, D, 0),
           ("w_up", D, 2 * D_FF, 1), ("w_down", D_FF, D, 0), ("conv_a_w", 3, CONV_W, 1),
           ("conv_ffn_w", 3, 2 * D_FF, 1))
MATRICES = SHARDED[:6]
REPLICATED = (("b_ada", N_MOD * D), ("norm1_g", D), ("norm2_g", D), ("b_f", N_HEADS), ("q_norm_g", HEAD_DIM),
              ("k_norm_g", HEAD_DIM))


def _shard_shape(rows, cols, axis):
    return (rows // N_DEV, cols) if axis == 0 else (rows, cols // N_DEV)


def _pack_rows(flat, multiple):
    length = flat.shape[-1]
    rows = -(-length // PACK_W)
    rows = -(-rows // multiple) * multiple
    pad = [(0, 0)] * (flat.ndim - 1) + [(0, rows * PACK_W - length)]
    return jnp.pad(flat, pad).reshape(flat.shape[:-1] + (rows, PACK_W))


def _pack_shards(shards, spec, multiple, dtype):
    flat = jnp.concatenate([shards[name].reshape(-1).astype(dtype) for name, *_ in spec])
    return _pack_rows(flat, multiple)


def _unpack_shards(packed, spec):
    flat = packed.reshape(-1)
    out, off = {}, 0
    for name, rows, cols, axis in spec:
        r, c = _shard_shape(rows, cols, axis)
        out[name] = flat[off:off + r * c].reshape(r, c)
        off += r * c
    return out


def _unpack_gathered(gathered, spec):
    flat = gathered.reshape(N_DEV, -1)
    out, off = {}, 0
    for name, rows, cols, axis in spec:
        r, c = _shard_shape(rows, cols, axis)
        seg = flat[:, off:off + r * c].reshape(N_DEV, r, c)
        out[name] = seg.reshape(rows, cols) if axis == 0 else seg.transpose(1, 0, 2).reshape(rows, cols)
        off += r * c
    return out


def _pack_full_by_dest(full, spec, multiple):
    segs = []
    for name, rows, cols, axis in spec:
        r, c = _shard_shape(rows, cols, axis)
        a = full[name]
        seg = a.reshape(N_DEV, r, c) if axis == 0 else a.reshape(rows, N_DEV, c).transpose(1, 0, 2)
        segs.append(seg.reshape(N_DEV, r * c))
    return _pack_rows(jnp.concatenate(segs, axis=1), multiple)


def _pad_in(w_in):
    return jnp.concatenate([w_in[:, :COL_GA], w_in[:, COL_GA + N_HEADS:], w_in[:, COL_GA:COL_GA + N_HEADS],
                            jnp.zeros((w_in.shape[0], IN_W_PAD - IN_W), w_in.dtype)], axis=1)


def _unpad_in(g):
    return jnp.concatenate([g[:, :COL_GA], g[:, COL_F:COL_F + N_HEADS], g[:, COL_GA:COL_F]], axis=1)


def kernel(x, c, w_ada, b_ada, norm1_g, w_in, b_f, conv_a_w, q_norm_g, k_norm_g, w_branch_a, w_branch_b, w_out, norm2_g, w_up, conv_ffn_w, w_down, loss_target, m_w_ada, m_b_ada, m_norm1_g, m_w_in, m_b_f, m_conv_a_w, m_q_norm_g, m_k_norm_g, m_w_branch_a, m_w_branch_b, m_w_out, m_norm2_g, m_w_up, m_conv_ffn_w, m_w_down, v_w_ada, v_b_ada, v_norm1_g, v_w_in, v_b_f, v_conv_a_w, v_q_norm_g, v_k_norm_g, v_w_branch_a, v_w_branch_b, v_w_out, v_norm2_g, v_w_up, v_conv_ffn_w, v_w_down):
    names = ("w_ada", "b_ada", "norm1_g", "w_in", "b_f", "conv_a_w", "q_norm_g", "k_norm_g", "w_branch_a",
             "w_branch_b", "w_out", "norm2_g", "w_up", "conv_ffn_w", "w_down")
    squeeze = lambda a: a[0] if a.ndim == 3 else a
    W = dict(zip(names, map(squeeze, (w_ada, b_ada, norm1_g, w_in, b_f, conv_a_w, q_norm_g, k_norm_g, w_branch_a,
                                      w_branch_b, w_out, norm2_g, w_up, conv_ffn_w, w_down))))
    M = dict(zip(names, map(squeeze, (m_w_ada, m_b_ada, m_norm1_g, m_w_in, m_b_f, m_conv_a_w, m_q_norm_g,
                                      m_k_norm_g, m_w_branch_a, m_w_branch_b, m_w_out, m_norm2_g, m_w_up,
                                      m_conv_ffn_w, m_w_down))))
    V = dict(zip(names, map(squeeze, (v_w_ada, v_b_ada, v_norm1_g, v_w_in, v_b_f, v_conv_a_w, v_q_norm_g,
                                      v_k_norm_g, v_w_branch_a, v_w_branch_b, v_w_out, v_norm2_g, v_w_up,
                                      v_conv_ffn_w, v_w_down))))
    me = 4 * lax.axis_index("x") + 2 * lax.axis_index("y") + lax.axis_index("c")
    ada_n = N_MOD * D // N_DEV

    small = jnp.concatenate([c.reshape(-1), W["conv_a_w"].reshape(-1), W["conv_ffn_w"].reshape(-1)])
    small_all = _exchange(_pack_rows(small, SUBLANES), name="gather_small", scatter=False).reshape(N_DEV, -1)
    c_all = small_all[:, :D]
    conv_all = _unpack_gathered(small_all[:, D:], SHARDED[6:])
    mats = _unpack_gathered(_exchange(_pack_shards(W, MATRICES, 16, BF16), name="gather_weights", scatter=False),
                            MATRICES)

    b_shard = lax.dynamic_slice(W["b_ada"], (0, me * ada_n), (1, ada_n))
    mod_part = _ada_fwd(c_all, W["w_ada"], b_shard)
    mod_all = _exchange(mod_part, name="gather_mod", scatter=False)
    mod = lax.dynamic_index_in_dim(mod_all, me, axis=1, keepdims=False).reshape(1, N_MOD * D)

    wts = {"w_in": _pad_in(mats["w_in"])}
    wts["w_in_t"] = wts["w_in"].T
    for name in ("w_branch_a", "w_branch_b", "w_out", "w_up", "w_down"):
        wts[name] = mats[name]
        wts[name + "_t"] = mats[name].T
    wts.update(conv_all)
    for name in ("norm1_g", "norm2_g", "q_norm_g", "k_norm_g", "b_f"):
        wts[name] = W[name]

    sq, grad_x, grads = _local_step(x[0], loss_target[0], mod, wts)
    loss = lax.psum(sq[0, 0] * (0.5 / D), AXES)

    grads["b_ada"] = grads["mod"]
    rep_flat = lambda src: jnp.concatenate([src[name].reshape(-1) for name, _ in REPLICATED])
    rep_parts = _exchange(_pack_rows(rep_flat(grads), 16), name="gather_small_grads", scatter=False)
    rep_out = _adamw(rep_parts, *[_pack_rows(rep_flat(s), 16) for s in (W, M, V)], name="adamw_replicated")

    dmod_all = rep_parts.reshape(N_DEV, -1)[:, :N_MOD * D]
    dmod_mine = lax.dynamic_slice(dmod_all, (0, me * ada_n), (N_DEV, ada_n))
    g_ada = _ada_bwd(jnp.pad(c_all.T, ((0, 0), (0, LANES - N_DEV))),
                     jnp.pad(dmod_mine, ((0, LANES - N_DEV), (0, 0))))
    ada_out = _adamw(g_ada[None], W["w_ada"], M["w_ada"], V["w_ada"], name="adamw_ada")

    grads["w_in"] = _unpad_in(grads["w_in_pad"])
    big_parts = _exchange(_pack_full_by_dest(grads, SHARDED, ADAM_ROWS), name="scatter_grads", scatter=True)
    big_out = _adamw(big_parts, *[_pack_shards(s, SHARDED, ADAM_ROWS, F32) for s in (W, M, V)],
                     name="adamw_sharded")

    results = []
    for kind in range(4):
        per = {"w_ada": ada_out[kind]}
        per.update(_unpack_shards(big_out[kind], SHARDED))
        flat, off = rep_out[kind].reshape(-1), 0
        for name, n in REPLICATED:
            per[name] = flat[off:off + n].reshape(1, n)
            off += n
        results.append(per)
    restore = lambda name, a: a[None] if W[name].ndim == 2 and name not in dict(REPLICATED) else a
    outs = [loss, grad_x[None]]
    for per in results:
        outs.extend(restore(name, per[name]) for name in names)
    return tuple(outs)
```

```python
import functools

import jax
import jax.numpy as jnp
import numpy as np
from jax import lax
from jax.experimental import pallas as pl
from jax.experimental.pallas import tpu as pltpu

F32 = jnp.float32
BF16 = jnp.bfloat16

N_DEV = 8
D = 1024
N_HEADS = 8
HEAD_DIM = 64
ATTN_W = 512
CONV_W = 512
D_FF = 2816
N_MOD = 6
IN_W = 5128
RMS_EPS = 1e-6
NEG_INF = -1e30

IN_W_PAD = 5376
COL_GA = 3072
COL_GB = 4096
COL_F = 5120
F_PAD = 128

ADAM_LR = 0.001
ADAM_B1 = 0.9
ADAM_B2 = 0.999
ADAM_EPS = 1e-08
ADAM_WD = 0.01
ADAM_STEP = 10

LANES = 128
SUBLANES = 8
VMEM_LIMIT = 52 * 1024 * 1024
TOKEN_TILE = 512
ATTN_BLOCK = 512
PACK_W = 1024

MESH = pl.DeviceIdType.MESH
AXES = ("x", "y", "c")


def _params(sem=None, **kw):
    return pltpu.CompilerParams(dimension_semantics=sem, vmem_limit_bytes=VMEM_LIMIT, **kw)


def _full(shape):
    nd = len(shape)
    return pl.BlockSpec(shape, lambda *_: (0,) * nd)


def _matmul(a, b, *, name, tm, tn, tk, out_dtype=F32, trans_a=False):
    if trans_a:
        K, M = a.shape
    else:
        M, K = a.shape
    N = b.shape[1]
    assert b.shape[0] == K and M % tm == 0 and N % tn == 0 and K % tk == 0, (name, a.shape, b.shape)
    nm, nn, nk = M // tm, N // tn, K // tk

    def body(a_ref, b_ref, o_ref, acc_ref):
        k = pl.program_id(2)

        @pl.when(k == 0)
        def _():
            acc_ref[...] = jnp.zeros_like(acc_ref)

        av = a_ref[...].astype(BF16)
        bv = b_ref[...].astype(BF16)
        if trans_a:
            acc_ref[...] += lax.dot_general(av, bv, (((0,), (0,)), ((), ())), preferred_element_type=F32)
        else:
            acc_ref[...] += jnp.dot(av, bv, preferred_element_type=F32)

        @pl.when(k == nk - 1)
        def _():
            o_ref[...] = acc_ref[...].astype(out_dtype)

    if trans_a:
        a_spec = pl.BlockSpec((tk, tm), lambda j, i, k: (k, i))
    else:
        a_spec = pl.BlockSpec((tm, tk), lambda j, i, k: (i, k))
    return pl.pallas_call(
        body, name=name, grid=(nn, nm, nk),
        in_specs=[a_spec, pl.BlockSpec((tk, tn), lambda j, i, k: (k, j))],
        out_specs=pl.BlockSpec((tm, tn), lambda j, i, k: (i, j)),
        out_shape=jax.ShapeDtypeStruct((M, N), out_dtype),
        scratch_shapes=[pltpu.VMEM((tm, tn), F32)],
        compiler_params=_params(("parallel", "parallel", "arbitrary")),
    )(a, b)


def _split_dot(x, mat, parts):
    out = None
    rem = x
    for p in range(parts):
        piece = rem.astype(BF16)
        term = jnp.dot(piece, mat, preferred_element_type=F32)
        out = term if out is None else out + term
        if p + 1 < parts:
            rem = rem - piece.astype(F32)
    return out


def _sigmoid(x):
    return 1.0 / (1.0 + jnp.exp(-x))


def _rows8(x):
    r, c = x.shape
    return jnp.sum(x.reshape(r // SUBLANES, SUBLANES, c), axis=0)


def _shift_down(blk, prev8, n):
    rolled = pltpu.roll(blk, n, axis=0)
    prev_rolled = pltpu.roll(prev8, n, axis=0)
    rows = lax.broadcasted_iota(jnp.int32, prev8.shape, 0)
    first = jnp.where(rows < n, prev_rolled, rolled[0:SUBLANES])
    return jnp.concatenate([first, rolled[SUBLANES:]], axis=0)


def _shift_up(blk, next8, n):
    r = blk.shape[0]
    rolled = pltpu.roll(blk, r - n, axis=0)
    next_rolled = pltpu.roll(next8, SUBLANES - n, axis=0)
    rows = lax.broadcasted_iota(jnp.int32, next8.shape, 0)
    last = jnp.where(rows >= SUBLANES - n, next_rolled, rolled[r - SUBLANES:])
    return jnp.concatenate([rolled[:r - SUBLANES], last], axis=0)


def _prev_spec(tm, width, col):
    per = tm // SUBLANES
    return pl.BlockSpec((SUBLANES, width), lambda i, *_: (jnp.maximum(i * per - 1, 0), col))


def _next_spec(tm, width, col, n_tiles):
    per = tm // SUBLANES
    last = n_tiles * per - 1
    return pl.BlockSpec((SUBLANES, width), lambda i, *_: (jnp.minimum((i + 1) * per, last), col))


def _group_matrix():
    idx = np.arange(ATTN_W) // HEAD_DIM
    return jnp.asarray((idx[:, None] == idx[None, :]).astype(np.float32), BF16)


def _head_sum_matrix():
    idx = np.arange(ATTN_W) // HEAD_DIM
    return jnp.asarray((idx[:, None] == np.arange(F_PAD)[None, :]).astype(np.float32), BF16)


def _norm_mod(x, g, sc, sh, *, name):
    T = x.shape[0]
    tm = min(TOKEN_TILE, T)

    def body(x_ref, g_ref, sc_ref, sh_ref, o_ref):
        xv = x_ref[...]
        inv = lax.rsqrt(jnp.mean(xv * xv, axis=-1, keepdims=True) + RMS_EPS)
        o_ref[...] = ((xv * inv) * g_ref[...] * (1.0 + sc_ref[...]) + sh_ref[...]).astype(BF16)

    row = pl.BlockSpec((tm, D), lambda i: (i, 0))
    return pl.pallas_call(
        body, name=name, grid=(T // tm,),
        in_specs=[row, _full((1, D)), _full((1, D)), _full((1, D))],
        out_specs=row, out_shape=jax.ShapeDtypeStruct((T, D), BF16),
        compiler_params=_params(("parallel",)),
    )(x, g, sc, sh)


def _branch_prep(proj, conv_w8, qg, kg, bf_pad, gmat):
    T = proj.shape[0]
    tm = min(TOKEN_TILE, T)
    nt = T // tm

    def body(cb_ref, cc_ref, cv_ref, q_ref, k_ref, f_ref, ccp_ref, cvp_ref, w_ref, qg_ref, kg_ref, bf_ref, g_ref,
             ya_ref, qs_ref, kh_ref, lf_ref):
        i = pl.program_id(0)
        z = cc_ref[...] * cv_ref[...]
        zp = jnp.where(i > 0, ccp_ref[...] * cvp_ref[...], 0.0)
        w = w_ref[...]
        cz = _shift_down(z, zp, 2) * w[0:1] + _shift_down(z, zp, 1) * w[1:2] + z * w[2:3]
        ya_ref[...] = (cb_ref[...] * cz).astype(BF16)
        gm = g_ref[...]
        for src, gain, dst, scale in ((q_ref, qg_ref, qs_ref, 1.0 / np.sqrt(HEAD_DIM)), (k_ref, kg_ref, kh_ref, 1.0)):
            v = src[...]
            ms = _split_dot(v * v, gm, 2) * (1.0 / HEAD_DIM)
            dst[...] = ((v * lax.rsqrt(ms + RMS_EPS)) * gain[...] * scale).astype(BF16)
        fx = f_ref[...] + bf_ref[...]
        lf_ref[...] = jnp.minimum(fx, 0.0) - jnp.log(1.0 + jnp.exp(-jnp.abs(fx)))

    blk = lambda col: pl.BlockSpec((tm, CONV_W), lambda i: (i, col))
    out_blk = pl.BlockSpec((tm, CONV_W), lambda i: (i, 0))
    f_blk = pl.BlockSpec((tm, F_PAD), lambda i: (i, COL_F // F_PAD))
    return pl.pallas_call(
        body, name="branch_prep", grid=(nt,),
        in_specs=[blk(0), blk(1), blk(2), blk(3), blk(4), f_blk,
                  _prev_spec(tm, CONV_W, 1), _prev_spec(tm, CONV_W, 2),
                  _full((SUBLANES, CONV_W)), _full((1, ATTN_W)), _full((1, ATTN_W)), _full((1, F_PAD)),
                  _full((ATTN_W, ATTN_W))],
        out_specs=[out_blk, out_blk, out_blk, pl.BlockSpec((tm, F_PAD), lambda i: (i, 0))],
        out_shape=[jax.ShapeDtypeStruct((T, CONV_W), BF16)] * 3 + [jax.ShapeDtypeStruct((T, F_PAD), F32)],
        compiler_params=_params(("parallel",)),
    )(proj, proj, proj, proj, proj, proj, proj, proj, conv_w8, qg, kg, bf_pad, gmat)


def _cumsum(x, *, reverse, name):
    T = x.shape[0]
    tm = min(TOKEN_TILE, T)
    nt = T // tm

    def body(x_ref, o_ref, carry_ref):
        i = pl.program_id(0)

        @pl.when(i == 0)
        def _():
            carry_ref[...] = jnp.zeros_like(carry_ref)

        r = lax.broadcasted_iota(jnp.int32, (tm, tm), 0)
        c = lax.broadcasted_iota(jnp.int32, (tm, tm), 1)
        tri = jnp.where((c >= r) if reverse else (c <= r), 1.0, 0.0).astype(BF16)
        xv = x_ref[...]
        out = _split_dot_left(tri, xv, 3) + carry_ref[0:1]
        o_ref[...] = out
        carry_ref[...] = jnp.broadcast_to(out[0:1] if reverse else out[tm - 1:tm], carry_ref.shape)

    idx = (lambda i: (nt - 1 - i, 0)) if reverse else (lambda i: (i, 0))
    return pl.pallas_call(
        body, name=name, grid=(nt,),
        in_specs=[pl.BlockSpec((tm, F_PAD), idx)], out_specs=pl.BlockSpec((tm, F_PAD), idx),
        out_shape=jax.ShapeDtypeStruct((T, F_PAD), F32),
        scratch_shapes=[pltpu.VMEM((SUBLANES, F_PAD), F32)],
        compiler_params=_params(("arbitrary",)),
    )(x)


def _split_dot_left(mat, x, parts):
    out = None
    rem = x
    for p in range(parts):
        piece = rem.astype(BF16)
        term = jnp.dot(mat, piece, preferred_element_type=F32)
        out = term if out is None else out + term
        if p + 1 < parts:
            rem = rem - piece.astype(F32)
    return out


def _merge_fwd(ya, yb, proj):
    T = ya.shape[0]
    tm = min(TOKEN_TILE, T)

    def body(ya_ref, yb_ref, ga_ref, gb_ref, o_ref):
        o_ref[...] = (_sigmoid(ga_ref[...]) * ya_ref[...] + _sigmoid(gb_ref[...]) * yb_ref[...]).astype(BF16)

    row = pl.BlockSpec((tm, D), lambda i: (i, 0))
    return pl.pallas_call(
        body, name="merge_fwd", grid=(T // tm,),
        in_specs=[row, row, pl.BlockSpec((tm, D), lambda i: (i, COL_GA // D)),
                  pl.BlockSpec((tm, D), lambda i: (i, COL_GB // D))],
        out_specs=row, out_shape=jax.ShapeDtypeStruct((T, D), BF16),
        compiler_params=_params(("parallel",)),
    )(ya, yb, proj, proj)


def _resid_norm(x, mix, g1, g, sc, sh):
    T = x.shape[0]
    tm = min(TOKEN_TILE, T)

    def body(x_ref, mix_ref, g1_ref, g_ref, sc_ref, sh_ref, x1_ref, h_ref):
        x1 = x_ref[...] + g1_ref[...] * mix_ref[...]
        x1_ref[...] = x1
        inv = lax.rsqrt(jnp.mean(x1 * x1, axis=-1, keepdims=True) + RMS_EPS)
        h_ref[...] = ((x1 * inv) * g_ref[...] * (1.0 + sc_ref[...]) + sh_ref[...]).astype(BF16)

    row = pl.BlockSpec((tm, D), lambda i: (i, 0))
    vec = _full((1, D))
    return pl.pallas_call(
        body, name="resid_norm", grid=(T // tm,),
        in_specs=[row, row, vec, vec, vec, vec], out_specs=[row, row],
        out_shape=[jax.ShapeDtypeStruct((T, D), F32), jax.ShapeDtypeStruct((T, D), BF16)],
        compiler_params=_params(("parallel",)),
    )(x, mix, g1, g, sc, sh)


FFN_TM = 256
FFN_TC = 1408


def _ffn_act_fwd(u, w8):
    T = u.shape[0]
    tm = min(FFN_TM, T)
    nt = T // tm
    nc = D_FF // FFN_TC

    def body(ug_ref, uv_ref, ugp_ref, uvp_ref, wg_ref, wv_ref, o_ref):
        i = pl.program_id(1)

        def conv(u_ref, p_ref, w_ref):
            uv = u_ref[...]
            up = jnp.where(i > 0, p_ref[...], 0.0)
            w = w_ref[...]
            return _shift_down(uv, up, 2) * w[0:1] + _shift_down(uv, up, 1) * w[1:2] + uv * w[2:3]

        gate = conv(ug_ref, ugp_ref, wg_ref)
        val = conv(uv_ref, uvp_ref, wv_ref)
        o_ref[...] = (gate * _sigmoid(gate) * val).astype(BF16)

    per = tm // SUBLANES
    blk = lambda off: pl.BlockSpec((tm, FFN_TC), lambda j, i: (i, j + off))
    prev = lambda off: pl.BlockSpec((SUBLANES, FFN_TC), lambda j, i: (jnp.maximum(i * per - 1, 0), j + off))
    wblk = lambda off: pl.BlockSpec((SUBLANES, FFN_TC), lambda j, i: (0, j + off))
    return pl.pallas_call(
        body, name="ffn_act_fwd", grid=(nc, nt),
        in_specs=[blk(0), blk(nc), prev(0), prev(nc), wblk(0), wblk(nc)],
        out_specs=pl.BlockSpec((tm, FFN_TC), lambda j, i: (i, j)),
        out_shape=jax.ShapeDtypeStruct((T, D_FF), BF16),
        compiler_params=_params(("parallel", "parallel")),
    )(u, u, u, u, w8, w8)


def _loss_head(x1, ff, g2, target):
    T = x1.shape[0]
    tm = min(TOKEN_TILE, T)

    def body(x1_ref, ff_ref, g2_ref, t_ref, dy_ref, dff_ref, loss_ref, dg2_ref):
        i = pl.program_id(0)

        @pl.when(i == 0)
        def _():
            loss_ref[...] = jnp.zeros_like(loss_ref)
            dg2_ref[...] = jnp.zeros_like(dg2_ref)

        ff = ff_ref[...]
        err = x1_ref[...] + g2_ref[...] * ff - t_ref[...]
        dy = err * (1.0 / D)
        dy_ref[...] = dy
        dff_ref[...] = (dy * g2_ref[...]).astype(BF16)
        loss_ref[...] += _rows8(err * err)
        dg2_ref[...] += _rows8(dy * ff)

    row = pl.BlockSpec((tm, D), lambda i: (i, 0))
    acc = _full((SUBLANES, D))
    return pl.pallas_call(
        body, name="loss_head", grid=(T // tm,),
        in_specs=[row, row, _full((1, D)), row], out_specs=[row, row, acc, acc],
        out_shape=[jax.ShapeDtypeStruct((T, D), F32), jax.ShapeDtypeStruct((T, D), BF16),
                   jax.ShapeDtypeStruct((SUBLANES, D), F32), jax.ShapeDtypeStruct((SUBLANES, D), F32)],
        compiler_params=_params(("arbitrary",)),
    )(x1, ff, g2, target)


def _nt_dot(a, b):
    return lax.dot_general(a, b, (((1,), (1,)), ((), ())), preferred_element_type=F32)


def _attn_fwd(qs, kh, vh, f_col, f_row):
    nh, T, hd = qs.shape
    bq = min(ATTN_BLOCK, T)
    nq = T // bq

    def body(q_ref, k_ref, v_ref, fq_ref, fk_ref, o_ref, lse_ref):
        qi = pl.program_id(1)
        q = q_ref[0]
        fq = fq_ref[0]

        def step(kb, carry, masked):
            m, l, acc = carry
            start = pl.multiple_of(kb * bq, bq)
            k = k_ref[0, pl.ds(start, bq), :]
            v = v_ref[0, pl.ds(start, bq), :]
            s = _nt_dot(q, k) + (fq - fk_ref[0, :, pl.ds(start, bq)])
            if masked:
                r = lax.broadcasted_iota(jnp.int32, (bq, bq), 0)
                c = lax.broadcasted_iota(jnp.int32, (bq, bq), 1)
                s = jnp.where(c <= r, s, NEG_INF)
            m_new = jnp.maximum(m, jnp.max(s, axis=-1, keepdims=True))
            alpha = jnp.exp(m - m_new)
            p = jnp.exp(s - m_new)
            l = alpha * l + jnp.sum(p, axis=-1, keepdims=True)
            acc = alpha * acc + jnp.dot(p.astype(BF16), v, preferred_element_type=F32)
            return m_new, l, acc

        init = (jnp.full((bq, 1), NEG_INF, F32), jnp.zeros((bq, 1), F32), jnp.zeros((bq, hd), F32))
        carry = lax.fori_loop(0, qi, lambda kb, cr: step(kb, cr, False), init)
        m, l, acc = step(qi, carry, True)
        o_ref[0] = acc / l
        lse_ref[0] = m + jnp.log(l)

    tile = pl.BlockSpec((1, bq, hd), lambda h, i: (h, i, 0))
    whole = pl.BlockSpec((1, T, hd), lambda h, i: (h, 0, 0))
    col = pl.BlockSpec((1, bq, 1), lambda h, i: (h, i, 0))
    row = pl.BlockSpec((1, 1, T), lambda h, i: (h, 0, 0))
    return pl.pallas_call(
        body, name="attn_fwd", grid=(nh, nq),
        in_specs=[tile, whole, whole, col, row], out_specs=[tile, col],
        out_shape=[jax.ShapeDtypeStruct((nh, T, hd), F32), jax.ShapeDtypeStruct((nh, T, 1), F32)],
        compiler_params=_params(("parallel", "parallel")),
    )(qs, kh, vh, f_col, f_row)


def _attn_bwd_kv(qs, kh, vh, do, f_col, f_row, lse_row, delta_row):
    nh, T, hd = qs.shape
    bk = min(ATTN_BLOCK, T)
    nk = T // bk

    def body(q_ref, do_ref, k_ref, v_ref, fk_ref, fq_ref, lse_ref, dl_ref, dk_ref, dv_ref, df_ref):
        ki = pl.program_id(1)
        k = k_ref[0]
        v = v_ref[0]
        fk = fk_ref[0]

        def step(qb, carry, masked):
            dk, dv, df = carry
            start = pl.multiple_of(qb * bk, bk)
            q = q_ref[0, pl.ds(start, bk), :]
            g = do_ref[0, pl.ds(start, bk), :]
            cols = pl.ds(start, bk)
            st = _nt_dot(k, q) + (fq_ref[0, :, cols] - fk)
            pt = jnp.exp(st - lse_ref[0, :, cols])
            if masked:
                r = lax.broadcasted_iota(jnp.int32, (bk, bk), 0)
                c = lax.broadcasted_iota(jnp.int32, (bk, bk), 1)
                pt = jnp.where(c >= r, pt, 0.0)
            dv = dv + jnp.dot(pt.astype(BF16), g, preferred_element_type=F32)
            dst = pt * (_nt_dot(v, g) - dl_ref[0, :, cols])
            dk = dk + jnp.dot(dst.astype(BF16), q, preferred_element_type=F32)
            df = df + jnp.sum(dst, axis=-1, keepdims=True)
            return dk, dv, df

        init = (jnp.zeros((bk, hd), F32), jnp.zeros((bk, hd), F32), jnp.zeros((bk, 1), F32))
        carry = step(ki, init, True)
        dk, dv, df = lax.fori_loop(ki + 1, nk, lambda qb, cr: step(qb, cr, False), carry)
        dk_ref[0] = dk
        dv_ref[0] = dv
        df_ref[0] = df

    tile = pl.BlockSpec((1, bk, hd), lambda h, i: (h, i, 0))
    whole = pl.BlockSpec((1, T, hd), lambda h, i: (h, 0, 0))
    col = pl.BlockSpec((1, bk, 1), lambda h, i: (h, i, 0))
    row = pl.BlockSpec((1, 1, T), lambda h, i: (h, 0, 0))
    return pl.pallas_call(
        body, name="attn_bwd_kv", grid=(nh, nk),
        in_specs=[whole, whole, tile, tile, col, row, row, row], out_specs=[tile, tile, col],
        out_shape=[jax.ShapeDtypeStruct((nh, T, hd), F32), jax.ShapeDtypeStruct((nh, T, hd), F32),
                   jax.ShapeDtypeStruct((nh, T, 1), F32)],
        compiler_params=_params(("parallel", "parallel")),
    )(qs, do, kh, vh, f_col, f_row, lse_row, delta_row)


def _attn_bwd_q(qs, kh, vh, do, f_col, f_row, lse_col, delta_col):
    nh, T, hd = qs.shape
    bq = min(ATTN_BLOCK, T)
    nq = T // bq

    def body(q_ref, do_ref, k_ref, v_ref, fq_ref, fk_ref, lse_ref, dl_ref, dq_ref, rs_ref):
        qi = pl.program_id(1)
        q = q_ref[0]
        g = do_ref[0]
        fq = fq_ref[0]
        lse = lse_ref[0]
        dl = dl_ref[0]

        def step(kb, carry, masked):
            dq, rs = carry
            start = pl.multiple_of(kb * bq, bq)
            k = k_ref[0, pl.ds(start, bq), :]
            v = v_ref[0, pl.ds(start, bq), :]
            s = _nt_dot(q, k) + (fq - fk_ref[0, :, pl.ds(start, bq)])
            p = jnp.exp(s - lse)
            if masked:
                r = lax.broadcasted_iota(jnp.int32, (bq, bq), 0)
                c = lax.broadcasted_iota(jnp.int32, (bq, bq), 1)
                p = jnp.where(c <= r, p, 0.0)
            ds = p * (_nt_dot(g, v) - dl)
            dq = dq + jnp.dot(ds.astype(BF16), k, preferred_element_type=F32)
            return dq, rs + jnp.sum(ds, axis=-1, keepdims=True)

        init = (jnp.zeros((bq, hd), F32), jnp.zeros((bq, 1), F32))
        carry = lax.fori_loop(0, qi, lambda kb, cr: step(kb, cr, False), init)
        dq_ref[0], rs_ref[0] = step(qi, carry, True)

    tile = pl.BlockSpec((1, bq, hd), lambda h, i: (h, i, 0))
    whole = pl.BlockSpec((1, T, hd), lambda h, i: (h, 0, 0))
    col = pl.BlockSpec((1, bq, 1), lambda h, i: (h, i, 0))
    row = pl.BlockSpec((1, 1, T), lambda h, i: (h, 0, 0))
    return pl.pallas_call(
        body, name="attn_bwd_q", grid=(nh, nq),
        in_specs=[tile, tile, whole, whole, col, row, col, col], out_specs=[tile, col],
        out_shape=[jax.ShapeDtypeStruct((nh, T, hd), F32), jax.ShapeDtypeStruct((nh, T, 1), F32)],
        compiler_params=_params(("parallel", "parallel")),
    )(qs, do, kh, vh, f_col, f_row, lse_col, delta_col)


def _ffn_act_bwd(u, da, w8):
    T = u.shape[0]
    tm = min(FFN_TM, T)
    nt = T // tm
    nc = D_FF // FFN_TC

    def body(ug_ref, uv_ref, ugp_ref, uvp_ref, ugn_ref, uvn_ref, da_ref, dan_ref, wg_ref, wv_ref,
             dug_ref, duv_ref, dwg_ref, dwv_ref):
        i = pl.program_id(1)

        @pl.when(i == 0)
        def _():
            dwg_ref[...] = jnp.zeros_like(dwg_ref)
            dwv_ref[...] = jnp.zeros_like(dwv_ref)

        first, last = i == 0, i == nt - 1
        wg, wv = wg_ref[...], wv_ref[...]
        zeros8 = jnp.zeros((SUBLANES, FFN_TC), F32)

        def window(u_ref, p_ref, n_ref, w):
            e = jnp.concatenate([jnp.where(first, 0.0, p_ref[...]), u_ref[...], n_ref[...]], axis=0)
            e1 = pltpu.roll(e, 1, axis=0)
            e2 = pltpu.roll(e, 2, axis=0)
            return e, e1, e2, e2 * w[0:1] + e1 * w[1:2] + e * w[2:3]

        eg, eg1, eg2, cg = window(ug_ref, ugp_ref, ugn_ref, wg)
        ev, ev1, ev2, cv = window(uv_ref, uvp_ref, uvn_ref, wv)
        dae = jnp.concatenate([zeros8, da_ref[...].astype(F32), jnp.where(last, 0.0, dan_ref[...].astype(F32))],
                              axis=0)
        sg = _sigmoid(cg)
        dgate = dae * cv * sg * (1.0 + cg * (1.0 - sg))
        dval = dae * cg * sg
        n = tm + 2 * SUBLANES

        def back(d, w):
            return d * w[2:3] + pltpu.roll(d, n - 1, axis=0) * w[1:2] + pltpu.roll(d, n - 2, axis=0) * w[0:1]

        inner = slice(SUBLANES, SUBLANES + tm)
        dug_ref[...] = back(dgate, wg)[inner].astype(BF16)
        duv_ref[...] = back(dval, wv)[inner].astype(BF16)

        def wgrad(d, e, e1, e2):
            rows = [jnp.sum((d * t)[inner], axis=0, keepdims=True) for t in (e2, e1, e)]
            return jnp.concatenate(rows + [jnp.zeros((SUBLANES - 3, FFN_TC), F32)], axis=0)

        dwg_ref[...] += wgrad(dgate, eg, eg1, eg2)
        dwv_ref[...] += wgrad(dval, ev, ev1, ev2)

    per = tm // SUBLANES
    last_blk = nt * per - 1
    blk = lambda off: pl.BlockSpec((tm, FFN_TC), lambda j, i: (i, j + off))
    prev = lambda off: pl.BlockSpec((SUBLANES, FFN_TC), lambda j, i: (jnp.maximum(i * per - 1, 0), j + off))
    nxt = lambda off: pl.BlockSpec((SUBLANES, FFN_TC), lambda j, i: (jnp.minimum((i + 1) * per, last_blk), j + off))
    wblk = lambda off: pl.BlockSpec((SUBLANES, FFN_TC), lambda j, i: (0, j + off))
    dug, duv, dwg, dwv = pl.pallas_call(
        body, name="ffn_act_bwd", grid=(nc, nt),
        in_specs=[blk(0), blk(nc), prev(0), prev(nc), nxt(0), nxt(nc), blk(0), nxt(0), wblk(0), wblk(nc)],
        out_specs=[blk(0), blk(0), wblk(0), wblk(0)],
        out_shape=[jax.ShapeDtypeStruct((T, D_FF), BF16)] * 2 + [jax.ShapeDtypeStruct((SUBLANES, D_FF), F32)] * 2,
        compiler_params=_params(("parallel", "arbitrary")),
    )(u, u, u, u, u, u, da, da, w8, w8)
    return jnp.concatenate([dug, duv], axis=1), jnp.concatenate([dwg, dwv], axis=1)


def _norm_bwd(xin, dh, dres, g, sc, *, name):
    T = xin.shape[0]
    tm = min(TOKEN_TILE, T)

    def body(x_ref, dh_ref, dr_ref, g_ref, sc_ref, dx_ref, dsh_ref, dsc_ref, dg_ref):
        i = pl.program_id(0)

        @pl.when(i == 0)
        def _():
            dsh_ref[...] = jnp.zeros_like(dsh_ref)
            dsc_ref[...] = jnp.zeros_like(dsc_ref)
            dg_ref[...] = jnp.zeros_like(dg_ref)

        xv = x_ref[...]
        dh = dh_ref[...]
        gv = g_ref[...]
        one_sc = 1.0 + sc_ref[...]
        inv = lax.rsqrt(jnp.mean(xv * xv, axis=-1, keepdims=True) + RMS_EPS)
        xn = xv * inv
        dxn = dh * (gv * one_sc)
        dx_ref[...] = dr_ref[...] + inv * (dxn - xn * jnp.mean(dxn * xn, axis=-1, keepdims=True))
        dhxn = dh * xn
        dsh_ref[...] += _rows8(dh)
        dsc_ref[...] += _rows8(dhxn * gv)
        dg_ref[...] += _rows8(dhxn * one_sc)

    row = pl.BlockSpec((tm, D), lambda i: (i, 0))
    acc = _full((SUBLANES, D))
    return pl.pallas_call(
        body, name=name, grid=(T // tm,),
        in_specs=[row, row, row, _full((1, D)), _full((1, D))], out_specs=[row, acc, acc, acc],
        out_shape=[jax.ShapeDtypeStruct((T, D), F32)] + [jax.ShapeDtypeStruct((SUBLANES, D), F32)] * 3,
        compiler_params=_params(("arbitrary",)),
    )(xin, dh, dres, g, sc)


def _gate_bwd(dx1, mix, g1):
    T = dx1.shape[0]
    tm = min(TOKEN_TILE, T)

    def body(dx_ref, mix_ref, g1_ref, dmix_ref, dg1_ref):
        @pl.when(pl.program_id(0) == 0)
        def _():
            dg1_ref[...] = jnp.zeros_like(dg1_ref)

        dx = dx_ref[...]
        dmix_ref[...] = (dx * g1_ref[...]).astype(BF16)
        dg1_ref[...] += _rows8(dx * mix_ref[...])

    row = pl.BlockSpec((tm, D), lambda i: (i, 0))
    return pl.pallas_call(
        body, name="gate_bwd", grid=(T // tm,),
        in_specs=[row, row, _full((1, D))], out_specs=[row, _full((SUBLANES, D))],
        out_shape=[jax.ShapeDtypeStruct((T, D), BF16), jax.ShapeDtypeStruct((SUBLANES, D), F32)],
        compiler_params=_params(("arbitrary",)),
    )(dx1, mix, g1)


def _merge_bwd(dmerged, ya, yb, proj):
    T = ya.shape[0]
    tm = min(TOKEN_TILE, T)

    def body(dm_ref, ya_ref, yb_ref, ga_ref, gb_ref, dya_ref, dyb_ref, dga_ref, dgb_ref):
        dm = dm_ref[...]
        sa = _sigmoid(ga_ref[...])
        sb = _sigmoid(gb_ref[...])
        dya_ref[...] = (dm * sa).astype(BF16)
        dyb_ref[...] = (dm * sb).astype(BF16)
        dga_ref[...] = (dm * ya_ref[...] * sa * (1.0 - sa)).astype(BF16)
        dgb_ref[...] = (dm * yb_ref[...] * sb * (1.0 - sb)).astype(BF16)

    row = pl.BlockSpec((tm, D), lambda i: (i, 0))
    return pl.pallas_call(
        body, name="merge_bwd", grid=(T // tm,),
        in_specs=[row, row, row, pl.BlockSpec((tm, D), lambda i: (i, COL_GA // D)),
                  pl.BlockSpec((tm, D), lambda i: (i, COL_GB // D))],
        out_specs=[row] * 4, out_shape=[jax.ShapeDtypeStruct((T, D), BF16)] * 4,
        compiler_params=_params(("parallel",)),
    )(dmerged, ya, yb, proj, proj)


def _conv_branch_bwd(proj, dya0, conv_w8):
    T = proj.shape[0]
    tm = min(FFN_TM, T)
    nt = T // tm

    def body(cb_ref, cc_ref, cv_ref, cbn_ref, ccp_ref, cvp_ref, ccn_ref, cvn_ref, d_ref, dn_ref, w_ref,
             dcb_ref, dcc_ref, dcv_ref, dw_ref):
        i = pl.program_id(0)

        @pl.when(i == 0)
        def _():
            dw_ref[...] = jnp.zeros_like(dw_ref)

        first, last = i == 0, i == nt - 1
        w = w_ref[...]
        cc = jnp.concatenate([ccp_ref[...], cc_ref[...], ccn_ref[...]], axis=0)
        cv = jnp.concatenate([cvp_ref[...], cv_ref[...], cvn_ref[...]], axis=0)
        rows = lax.broadcasted_iota(jnp.int32, cc.shape, 0)
        z = jnp.where(jnp.logical_and(first, rows < SUBLANES), 0.0, cc * cv)
        z1 = pltpu.roll(z, 1, axis=0)
        z2 = pltpu.roll(z, 2, axis=0)
        cz = z2 * w[0:1] + z1 * w[1:2] + z * w[2:3]
        zeros8 = jnp.zeros((SUBLANES, CONV_W), F32)
        de = jnp.concatenate([zeros8, d_ref[...], jnp.where(last, 0.0, dn_ref[...])], axis=0)
        cbe = jnp.concatenate([zeros8, cb_ref[...], cbn_ref[...]], axis=0)
        dcz = de * cbe
        n = tm + 2 * SUBLANES
        dz = dcz * w[2:3] + pltpu.roll(dcz, n - 1, axis=0) * w[1:2] + pltpu.roll(dcz, n - 2, axis=0) * w[0:1]
        inner = slice(SUBLANES, SUBLANES + tm)
        dcb_ref[...] = (de * cz)[inner].astype(BF16)
        dcc_ref[...] = (dz * cv)[inner].astype(BF16)
        dcv_ref[...] = (dz * cc)[inner].astype(BF16)
        wrows = [jnp.sum((dcz * t)[inner], axis=0, keepdims=True) for t in (z2, z1, z)]
        dw_ref[...] += jnp.concatenate(wrows + [jnp.zeros((SUBLANES - 3, CONV_W), F32)], axis=0)

    blk = lambda col: pl.BlockSpec((tm, CONV_W), lambda i: (i, col))
    out_blk = pl.BlockSpec((tm, CONV_W), lambda i: (i, 0))
    return pl.pallas_call(
        body, name="conv_branch_bwd", grid=(nt,),
        in_specs=[blk(0), blk(1), blk(2), _next_spec(tm, CONV_W, 0, nt),
                  _prev_spec(tm, CONV_W, 1), _prev_spec(tm, CONV_W, 2),
                  _next_spec(tm, CONV_W, 1, nt), _next_spec(tm, CONV_W, 2, nt),
                  out_blk, _next_spec(tm, CONV_W, 0, nt), _full((SUBLANES, CONV_W))],
        out_specs=[out_blk, out_blk, out_blk, _full((SUBLANES, CONV_W))],
        out_shape=[jax.ShapeDtypeStruct((T, CONV_W), BF16)] * 3 + [jax.ShapeDtypeStruct((SUBLANES, CONV_W), F32)],
        compiler_params=_params(("arbitrary",)),
    )(proj, proj, proj, proj, proj, proj, proj, proj, dya0, dya0, conv_w8)


def _delta(do, o, hsum):
    T = do.shape[0]
    tm = min(TOKEN_TILE, T)

    def body(do_ref, o_ref, s_ref, out_ref):
        out_ref[...] = _split_dot(do_ref[...] * o_ref[...], s_ref[...], 3)

    row = pl.BlockSpec((tm, ATTN_W), lambda i: (i, 0))
    return pl.pallas_call(
        body, name="attn_delta", grid=(T // tm,),
        in_specs=[row, row, _full((ATTN_W, F_PAD))], out_specs=pl.BlockSpec((tm, F_PAD), lambda i: (i, 0)),
        out_shape=jax.ShapeDtypeStruct((T, F_PAD), F32),
        compiler_params=_params(("parallel",)),
    )(do, o, hsum)


def _qk_norm_bwd(proj, dqs, dkh, dlogf, qg, kg, bf_pad, gmat):
    T = proj.shape[0]
    tm = min(TOKEN_TILE, T)

    def body(q_ref, k_ref, f_ref, dqs_ref, dkh_ref, dlf_ref, qg_ref, kg_ref, bf_ref, g_ref,
             dq_ref, dk_ref, dfl_ref, dqg_ref, dkg_ref, dbf_ref):
        @pl.when(pl.program_id(0) == 0)
        def _():
            dqg_ref[...] = jnp.zeros_like(dqg_ref)
            dkg_ref[...] = jnp.zeros_like(dkg_ref)
            dbf_ref[...] = jnp.zeros_like(dbf_ref)

        gm = g_ref[...]
        for src, d_src, gain, scale, dst, dgain in (
                (q_ref, dqs_ref, qg_ref, 1.0 / np.sqrt(HEAD_DIM), dq_ref, dqg_ref),
                (k_ref, dkh_ref, kg_ref, 1.0, dk_ref, dkg_ref)):
            v = src[...]
            dhat = d_src[...] * scale
            inv = lax.rsqrt(_split_dot(v * v, gm, 2) * (1.0 / HEAD_DIM) + RMS_EPS)
            vn = v * inv
            dgain[...] += _rows8(dhat * vn)
            dvn = dhat * gain[...]
            mean = _split_dot(dvn * vn, gm, 2) * (1.0 / HEAD_DIM)
            dst[...] = (inv * (dvn - vn * mean)).astype(BF16)
        fx = f_ref[...] + bf_ref[...]
        dfl = dlf_ref[...] * _sigmoid(-fx)
        dfl_ref[...] = dfl.astype(BF16)
        dbf_ref[...] += _rows8(dfl)

    blk = lambda col: pl.BlockSpec((tm, ATTN_W), lambda i: (i, col))
    out_blk = pl.BlockSpec((tm, ATTN_W), lambda i: (i, 0))
    f_in = pl.BlockSpec((tm, F_PAD), lambda i: (i, COL_F // F_PAD))
    f_blk = pl.BlockSpec((tm, F_PAD), lambda i: (i, 0))
    return pl.pallas_call(
        body, name="qk_norm_bwd", grid=(T // tm,),
        in_specs=[blk(3), blk(4), f_in, out_blk, out_blk, f_blk, _full((1, ATTN_W)), _full((1, ATTN_W)),
                  _full((1, F_PAD)), _full((ATTN_W, ATTN_W))],
        out_specs=[out_blk, out_blk, f_blk, _full((SUBLANES, ATTN_W)), _full((SUBLANES, ATTN_W)),
                   _full((SUBLANES, F_PAD))],
        out_shape=[jax.ShapeDtypeStruct((T, ATTN_W), BF16)] * 2 + [jax.ShapeDtypeStruct((T, F_PAD), BF16)]
        + [jax.ShapeDtypeStruct((SUBLANES, ATTN_W), F32)] * 2 + [jax.ShapeDtypeStruct((SUBLANES, F_PAD), F32)],
        compiler_params=_params(("arbitrary",)),
    )(proj, proj, proj, dqs, dkh, dlogf, qg, kg, bf_pad, gmat)


def _to_heads(a):
    T = a.shape[0]
    return a.reshape(T, N_HEADS, HEAD_DIM).transpose(1, 0, 2)


def _from_heads(a):
    return a.transpose(1, 0, 2).reshape(a.shape[1], ATTN_W)


def _pad_rows8(w):
    return jnp.pad(w, ((0, SUBLANES - w.shape[0]), (0, 0)))


def _fold8(acc):
    return jnp.sum(acc, axis=0, keepdims=True)


def _local_step(x, target, mod, wts):
    T = x.shape[0]
    tm = min(TOKEN_TILE, T)
    sh1, sc1, g1, sh2, sc2, g2 = [mod[:, i * D:(i + 1) * D] for i in range(N_MOD)]
    w_in, w_in_t = wts["w_in"], wts["w_in_t"]
    conv_a8 = _pad_rows8(wts["conv_a_w"])
    conv_f8 = _pad_rows8(wts["conv_ffn_w"])
    qg = jnp.tile(wts["q_norm_g"], (1, N_HEADS))
    kg = jnp.tile(wts["k_norm_g"], (1, N_HEADS))
    bf_pad = jnp.pad(wts["b_f"], ((0, 0), (0, F_PAD - N_HEADS)))
    gmat = _group_matrix()
    hsum = _head_sum_matrix()

    h = _norm_mod(x, wts["norm1_g"], sc1, sh1, name="norm1_fwd")
    proj = _matmul(h, w_in, name="mm_in", tm=tm, tn=896, tk=D)
    ya0, qs, kh, logf = _branch_prep(proj, conv_a8, qg, kg, bf_pad, gmat)
    fcum = _cumsum(logf, reverse=False, name="gate_cumsum")
    f_heads = fcum[:, :N_HEADS].T
    f_col = f_heads.reshape(N_HEADS, T, 1)
    f_row = f_heads.reshape(N_HEADS, 1, T)
    qs_h, kh_h = _to_heads(qs), _to_heads(kh)
    v_h = _to_heads(proj[:, 2560:3072].astype(BF16))
    o_h, lse = _attn_fwd(qs_h, kh_h, v_h, f_col, f_row)
    o = _from_heads(o_h)
    ya = _matmul(ya0, wts["w_branch_a"], name="mm_branch_a", tm=tm, tn=D, tk=CONV_W)
    yb = _matmul(o, wts["w_branch_b"], name="mm_branch_b", tm=tm, tn=D, tk=ATTN_W)
    merged = _merge_fwd(ya, yb, proj)
    mix = _matmul(merged, wts["w_out"], name="mm_out", tm=tm, tn=D, tk=D)
    x1, h2 = _resid_norm(x, mix, g1, wts["norm2_g"], sc2, sh2)
    u = _matmul(h2, wts["w_up"], name="mm_up", tm=tm, tn=1408, tk=D)
    act = _ffn_act_fwd(u, conv_f8)
    ff = _matmul(act, wts["w_down"], name="mm_down", tm=tm, tn=D, tk=D_FF)
    dy, dff, sq8, dg2_8 = _loss_head(x1, ff, g2, target)
    sq = jnp.sum(sq8).reshape(1, 1)

    grads = {}
    da = _matmul(dff, wts["w_down_t"], name="mm_down_dx", tm=tm, tn=1408, tk=D)
    grads["w_down"] = _matmul(act, dff, name="mm_down_dw", tm=1408, tn=D, tk=tm, trans_a=True)
    du, dconv_f8 = _ffn_act_bwd(u, da, conv_f8)
    grads["conv_ffn_w"] = dconv_f8[:3]
    dh2 = _matmul(du, wts["w_up_t"], name="mm_up_dx", tm=tm, tn=D, tk=1408)
    grads["w_up"] = _matmul(h2, du, name="mm_up_dw", tm=D, tn=1408, tk=tm, trans_a=True)
    dx1, dsh2_8, dsc2_8, dn2_8 = _norm_bwd(x1, dh2, dy, wts["norm2_g"], sc2, name="norm2_bwd")
    grads["norm2_g"] = _fold8(dn2_8)

    dmix, dg1_8 = _gate_bwd(dx1, mix, g1)
    dmerged = _matmul(dmix, wts["w_out_t"], name="mm_out_dx", tm=tm, tn=D, tk=D)
    grads["w_out"] = _matmul(merged, dmix, name="mm_out_dw", tm=D, tn=D, tk=tm, trans_a=True)
    dya, dyb, dga, dgb = _merge_bwd(dmerged, ya, yb, proj)
    dya0 = _matmul(dya, wts["w_branch_a_t"], name="mm_branch_a_dx", tm=tm, tn=CONV_W, tk=D)
    grads["w_branch_a"] = _matmul(ya0, dya, name="mm_branch_a_dw", tm=CONV_W, tn=D, tk=tm, trans_a=True)
    do = _matmul(dyb, wts["w_branch_b_t"], name="mm_branch_b_dx", tm=tm, tn=ATTN_W, tk=D)
    grads["w_branch_b"] = _matmul(o, dyb, name="mm_branch_b_dw", tm=ATTN_W, tn=D, tk=tm, trans_a=True)
    dcb, dcc, dcv, dconv_a8 = _conv_branch_bwd(proj, dya0, conv_a8)
    grads["conv_a_w"] = dconv_a8[:3]

    delta_heads = _delta(do, o, hsum)[:, :N_HEADS].T
    do_h = _to_heads(do.astype(BF16))
    dk_h, dv_h, dfk = _attn_bwd_kv(qs_h, kh_h, v_h, do_h, f_col, f_row, lse.reshape(N_HEADS, 1, T),
                                   delta_heads.reshape(N_HEADS, 1, T))
    dqs_h, dfq = _attn_bwd_q(qs_h, kh_h, v_h, do_h, f_col, f_row, lse, delta_heads.reshape(N_HEADS, T, 1))
    dfcum = jnp.pad((dfq - dfk).reshape(N_HEADS, T).T, ((0, 0), (0, F_PAD - N_HEADS)))
    dlogf = _cumsum(dfcum, reverse=True, name="gate_cumsum_bwd")
    dq, dk, dfl, dqg8, dkg8, dbf8 = _qk_norm_bwd(proj, _from_heads(dqs_h), _from_heads(dk_h), dlogf, qg, kg,
                                                 bf_pad, gmat)
    grads["q_norm_g"] = jnp.sum(_fold8(dqg8).reshape(N_HEADS, HEAD_DIM), axis=0, keepdims=True)
    grads["k_norm_g"] = jnp.sum(_fold8(dkg8).reshape(N_HEADS, HEAD_DIM), axis=0, keepdims=True)
    grads["b_f"] = _fold8(dbf8)[:, :N_HEADS]
    dproj = jnp.concatenate(
        [dcb, dcc, dcv, dq, dk, _from_heads(dv_h).astype(BF16), dga, dgb, dfl,
         jnp.zeros((T, IN_W_PAD - COL_F - F_PAD), BF16)], axis=1)
    dh = _matmul(dproj, w_in_t, name="mm_in_dx", tm=tm, tn=D, tk=896)
    grads["w_in_pad"] = _matmul(h, dproj, name="mm_in_dw", tm=D, tn=896, tk=tm, trans_a=True)
    grad_x, dsh1_8, dsc1_8, dn1_8 = _norm_bwd(x, dh, dx1, wts["norm1_g"], sc1, name="norm1_bwd")
    grads["norm1_g"] = _fold8(dn1_8)
    grads["mod"] = jnp.concatenate([_fold8(a) for a in (dsh1_8, dsc1_8, dg1_8, dsh2_8, dsc2_8, dg2_8)], axis=1)
    return sq, grad_x, grads


def _me_and_peers():
    mx, my, mc = lax.axis_index("x"), lax.axis_index("y"), lax.axis_index("c")
    me = 4 * mx + 2 * my + mc
    peers = []
    for k in range(1, N_DEV):
        px = 1 - mx if k & 4 else mx
        py = 1 - my if k & 2 else my
        pc = 1 - mc if k & 1 else mc
        peers.append(((px, py, pc), 4 * px + 2 * py + pc))
    return me, peers


HBM_SPEC = pl.BlockSpec(memory_space=pltpu.HBM)


def _exchange(xs, *, name, scatter):
    n = len(xs)
    out_shapes = [jax.ShapeDtypeStruct(x.shape if scatter else (N_DEV,) + x.shape, x.dtype) for x in xs]

    def body(*refs):
        x_refs, out_refs = refs[:n], refs[n:2 * n]
        send_sems, recv_sems, local_sems = refs[2 * n:]
        me, peers = _me_and_peers()

        def src(a, idx):
            return x_refs[a].at[idx] if scatter else x_refs[a]

        def copy(a, k, from_idx, to_slot, device):
            return pltpu.make_async_remote_copy(
                src_ref=src(a, from_idx), dst_ref=out_refs[a].at[to_slot], send_sem=send_sems.at[a, k],
                recv_sem=recv_sems.at[a, k], device_id=device, device_id_type=MESH)

        local = [pltpu.make_async_copy(src(a, me), out_refs[a].at[me], local_sems.at[a]) for a in range(n)]
        for cp in local:
            cp.start()
        sends = [copy(a, k, idx, me, dev) for a in range(n) for k, (dev, idx) in enumerate(peers)]
        for cp in sends:
            cp.start()
        for a in range(n):
            for k, (dev, idx) in enumerate(peers):
                copy(a, k, idx, idx, dev).wait_recv()
        for cp in sends:
            cp.wait_send()
        for cp in local:
            cp.wait()

    return pl.pallas_call(
        body, name=name, in_specs=[HBM_SPEC] * n, out_specs=[HBM_SPEC] * n, out_shape=out_shapes,
        scratch_shapes=[pltpu.SemaphoreType.DMA((n, N_DEV - 1)), pltpu.SemaphoreType.DMA((n, N_DEV - 1)),
                        pltpu.SemaphoreType.DMA((n,))],
        compiler_params=pltpu.CompilerParams(has_side_effects=True),
    )(*xs)


def _ada_fwd(c_all, w_shard, b_shard):
    n = w_shard.shape[1]

    def body(c_ref, w_ref, b_ref, o_ref):
        cv = c_ref[...]
        act = (cv * _sigmoid(cv)).astype(BF16)
        o_ref[...] = jnp.dot(act, w_ref[...].astype(BF16), preferred_element_type=F32) + b_ref[...]

    return pl.pallas_call(
        body, name="ada_fwd", in_specs=[_full((N_DEV, D)), _full((D, n)), _full((1, n))],
        out_specs=_full((N_DEV, n)), out_shape=jax.ShapeDtypeStruct((N_DEV, n), F32), grid=(1,),
        compiler_params=_params(("arbitrary",)),
    )(c_all, w_shard, b_shard)


def _ada_bwd(c_all_t, dmod_pad):
    n = dmod_pad.shape[1]

    def body(c_ref, d_ref, o_ref):
        cv = c_ref[...]
        act = (cv * _sigmoid(cv)).astype(BF16)
        o_ref[...] = jnp.dot(act, d_ref[...].astype(BF16), preferred_element_type=F32)

    return pl.pallas_call(
        body, name="ada_bwd", in_specs=[_full((D, LANES)), _full((LANES, n))],
        out_specs=_full((D, n)), out_shape=jax.ShapeDtypeStruct((D, n), F32), grid=(1,),
        compiler_params=_params(("arbitrary",)),
    )(c_all_t, dmod_pad)


ADAM_ROWS = 64


def _adamw(parts, w, m, v, *, name):
    n, R, C = parts.shape
    tr = next((t for t in (ADAM_ROWS, 32, 16, SUBLANES) if R % t == 0), R)

    def body(p_ref, w_ref, m_ref, v_ref, g_ref, d_ref, nm_ref, nv_ref):
        g = p_ref[0].astype(F32)
        for j in range(1, n):
            g = g + p_ref[j].astype(F32)
        g_ref[...] = g
        nm = ADAM_B1 * m_ref[...] + (1.0 - ADAM_B1) * g
        nv = ADAM_B2 * v_ref[...] + (1.0 - ADAM_B2) * (g * g)
        nm_ref[...] = nm
        nv_ref[...] = nv
        m_hat = nm / (1.0 - ADAM_B1 ** ADAM_STEP)
        v_hat = nv / (1.0 - ADAM_B2 ** ADAM_STEP)
        d_ref[...] = -ADAM_LR * (m_hat / (jnp.sqrt(v_hat) + ADAM_EPS) + ADAM_WD * w_ref[...])

    row = pl.BlockSpec((tr, C), lambda i: (i, 0))
    return pl.pallas_call(
        body, name=name, grid=(R // tr,),
        in_specs=[pl.BlockSpec((n, tr, C), lambda i: (0, i, 0)), row, row, row], out_specs=[row] * 4,
        out_shape=[jax.ShapeDtypeStruct((R, C), F32)] * 4,
        compiler_params=_params(("parallel",)),
    )(parts, w, m, v)


SHARDED = (("w_in", D, IN_W, 1), ("w_branch_a", CONV_W, D, 1), ("w_branch_b", ATTN_W, D, 1), ("w_out", D, D, 0),
           ("w_up", D, 2 * D_FF, 1), ("w_down", D_FF, D, 0), ("conv_a_w", 3, CONV_W, 1),
           ("conv_ffn_w", 3, 2 * D_FF, 1))
MATRICES = SHARDED[:6]
CONVS = SHARDED[6:]
REPLICATED = (("b_ada", N_MOD * D), ("norm1_g", D), ("norm2_g", D), ("b_f", N_HEADS), ("q_norm_g", HEAD_DIM),
              ("k_norm_g", HEAD_DIM))


def _shard_shape(rows, cols, axis):
    return (rows // N_DEV, cols) if axis == 0 else (rows, cols // N_DEV)


def _pack_rows(flat, multiple):
    length = flat.shape[-1]
    rows = -(-length // PACK_W)
    rows = -(-rows // multiple) * multiple
    pad = [(0, 0)] * (flat.ndim - 1) + [(0, rows * PACK_W - length)]
    return jnp.pad(flat, pad).reshape(flat.shape[:-1] + (rows, PACK_W))


def _pack_shards(shards, spec, multiple, dtype):
    flat = jnp.concatenate([shards[name].reshape(-1).astype(dtype) for name, *_ in spec])
    return _pack_rows(flat, multiple)


def _join_shards(gathered, axis):
    if axis == 0:
        return gathered.reshape(N_DEV * gathered.shape[1], gathered.shape[2])
    return jnp.concatenate([gathered[j] for j in range(N_DEV)], axis=1)


def _split_shards(full, axis):
    if axis == 0:
        return full.reshape(N_DEV, full.shape[0] // N_DEV, full.shape[1])
    c = full.shape[1] // N_DEV
    return jnp.stack([full[:, j * c:(j + 1) * c] for j in range(N_DEV)])


def _unpack_shards(packed, spec):
    flat = packed.reshape(-1)
    out, off = {}, 0
    for name, rows, cols, axis in spec:
        r, c = _shard_shape(rows, cols, axis)
        out[name] = flat[off:off + r * c].reshape(r, c)
        off += r * c
    return out


def _unpack_gathered(gathered, spec):
    flat = gathered.reshape(N_DEV, -1)
    out, off = {}, 0
    for name, rows, cols, axis in spec:
        r, c = _shard_shape(rows, cols, axis)
        seg = flat[:, off:off + r * c].reshape(N_DEV, r, c)
        out[name] = seg.reshape(rows, cols) if axis == 0 else seg.transpose(1, 0, 2).reshape(rows, cols)
        off += r * c
    return out


def _pack_full_by_dest(full, spec, multiple):
    segs = []
    for name, rows, cols, axis in spec:
        r, c = _shard_shape(rows, cols, axis)
        a = full[name]
        seg = a.reshape(N_DEV, r, c) if axis == 0 else a.reshape(rows, N_DEV, c).transpose(1, 0, 2)
        segs.append(seg.reshape(N_DEV, r * c))
    return _pack_rows(jnp.concatenate(segs, axis=1), multiple)


def _pad_in(w_in):
    return jnp.concatenate([w_in[:, :COL_GA], w_in[:, COL_GA + N_HEADS:], w_in[:, COL_GA:COL_GA + N_HEADS],
                            jnp.zeros((w_in.shape[0], IN_W_PAD - IN_W), w_in.dtype)], axis=1)


def _unpad_in(g):
    return jnp.concatenate([g[:, :COL_GA], g[:, COL_F:COL_F + N_HEADS], g[:, COL_GA:COL_F]], axis=1)


def kernel(x, c, w_ada, b_ada, norm1_g, w_in, b_f, conv_a_w, q_norm_g, k_norm_g, w_branch_a, w_branch_b, w_out, norm2_g, w_up, conv_ffn_w, w_down, loss_target, m_w_ada, m_b_ada, m_norm1_g, m_w_in, m_b_f, m_conv_a_w, m_q_norm_g, m_k_norm_g, m_w_branch_a, m_w_branch_b, m_w_out, m_norm2_g, m_w_up, m_conv_ffn_w, m_w_down, v_w_ada, v_b_ada, v_norm1_g, v_w_in, v_b_f, v_conv_a_w, v_q_norm_g, v_k_norm_g, v_w_branch_a, v_w_branch_b, v_w_out, v_norm2_g, v_w_up, v_conv_ffn_w, v_w_down):
    names = ("w_ada", "b_ada", "norm1_g", "w_in", "b_f", "conv_a_w", "q_norm_g", "k_norm_g", "w_branch_a",
             "w_branch_b", "w_out", "norm2_g", "w_up", "conv_ffn_w", "w_down")
    squeeze = lambda a: a[0] if a.ndim == 3 else a
    W = dict(zip(names, map(squeeze, (w_ada, b_ada, norm1_g, w_in, b_f, conv_a_w, q_norm_g, k_norm_g, w_branch_a,
                                      w_branch_b, w_out, norm2_g, w_up, conv_ffn_w, w_down))))
    M = dict(zip(names, map(squeeze, (m_w_ada, m_b_ada, m_norm1_g, m_w_in, m_b_f, m_conv_a_w, m_q_norm_g,
                                      m_k_norm_g, m_w_branch_a, m_w_branch_b, m_w_out, m_norm2_g, m_w_up,
                                      m_conv_ffn_w, m_w_down))))
    V = dict(zip(names, map(squeeze, (v_w_ada, v_b_ada, v_norm1_g, v_w_in, v_b_f, v_conv_a_w, v_q_norm_g,
                                      v_k_norm_g, v_w_branch_a, v_w_branch_b, v_w_out, v_norm2_g, v_w_up,
                                      v_conv_ffn_w, v_w_down))))
    me = 4 * lax.axis_index("x") + 2 * lax.axis_index("y") + lax.axis_index("c")
    ada_n = N_MOD * D // N_DEV

    small = jnp.concatenate([c.reshape(-1), W["conv_a_w"].reshape(-1), W["conv_ffn_w"].reshape(-1)])
    gathered = _exchange([_pack_rows(small, SUBLANES)] + [W[name].astype(BF16) for name, *_ in MATRICES],
                         name="gather_weights", scatter=False)
    small_all = gathered[0].reshape(N_DEV, -1)
    c_all = small_all[:, :D]
    conv_all = _unpack_gathered(small_all[:, D:], CONVS)
    mats = {name: _join_shards(g, axis) for (name, _, _, axis), g in zip(MATRICES, gathered[1:])}

    b_shard = lax.dynamic_slice(W["b_ada"], (0, me * ada_n), (1, ada_n))
    mod_part = _ada_fwd(c_all, W["w_ada"], b_shard)
    mod_all, = _exchange([mod_part], name="gather_mod", scatter=False)
    mod = lax.dynamic_index_in_dim(mod_all, me, axis=1, keepdims=False).reshape(1, N_MOD * D)

    wts = {"w_in": _pad_in(mats["w_in"])}
    wts["w_in_t"] = wts["w_in"].T
    for name in ("w_branch_a", "w_branch_b", "w_out", "w_up", "w_down"):
        wts[name] = mats[name]
        wts[name + "_t"] = mats[name].T
    wts.update(conv_all)
    for name in ("norm1_g", "norm2_g", "q_norm_g", "k_norm_g", "b_f"):
        wts[name] = W[name]

    sq, grad_x, grads = _local_step(x[0], loss_target[0], mod, wts)
    loss = lax.psum(sq[0, 0] * (0.5 / D), AXES)

    grads["b_ada"] = grads["mod"]
    rep_flat = lambda src: jnp.concatenate([src[name].reshape(-1) for name, _ in REPLICATED])
    rep_parts, = _exchange([_pack_rows(rep_flat(grads), 16)], name="gather_small_grads", scatter=False)
    rep_out = _adamw(rep_parts, *[_pack_rows(rep_flat(s), 16) for s in (W, M, V)], name="adamw_replicated")

    dmod_all = rep_parts.reshape(N_DEV, -1)[:, :N_MOD * D]
    dmod_mine = lax.dynamic_slice(dmod_all, (0, me * ada_n), (N_DEV, ada_n))
    g_ada = _ada_bwd(jnp.pad(c_all.T, ((0, 0), (0, LANES - N_DEV))),
                     jnp.pad(dmod_mine, ((0, LANES - N_DEV), (0, 0))))
    ada_out = _adamw(g_ada[None], W["w_ada"], M["w_ada"], V["w_ada"], name="adamw_ada")

    grads["w_in"] = _unpad_in(grads["w_in_pad"])
    parts = _exchange([_split_shards(grads[name], axis).astype(BF16) for name, _, _, axis in MATRICES]
                      + [_pack_full_by_dest(grads, CONVS, SUBLANES)], name="scatter_grads", scatter=True)
    mat_out = {name: _adamw(p, W[name], M[name], V[name], name="adamw_" + name)
               for (name, *_), p in zip(MATRICES, parts)}
    conv_out = _adamw(parts[-1], *[_pack_shards(s, CONVS, SUBLANES, F32) for s in (W, M, V)], name="adamw_conv")

    results = []
    for kind in range(4):
        per = {"w_ada": ada_out[kind]}
        per.update({name: out[kind] for name, out in mat_out.items()})
        per.update(_unpack_shards(conv_out[kind], CONVS))
        flat, off = rep_out[kind].reshape(-1), 0
        for name, n in REPLICATED:
            per[name] = flat[off:off + n].reshape(1, n)
            off += n
        results.append(per)
    restore = lambda name, a: a[None] if W[name].ndim == 2 and name not in dict(REPLICATED) else a
    outs = [loss, grad_x[None]]
    for per in results:
        outs.extend(restore(name, per[name]) for name in names)
    return tuple(outs)
```

```python
import functools

import jax
import jax.numpy as jnp
import numpy as np
from jax import lax
from jax.experimental import pallas as pl
from jax.experimental.pallas import tpu as pltpu

F32 = jnp.float32
BF16 = jnp.bfloat16

N_DEV = 8
D = 1024
N_HEADS = 8
HEAD_DIM = 64
ATTN_W = 512
CONV_W = 512
D_FF = 2816
N_MOD = 6
IN_W = 5128
RMS_EPS = 1e-6
NEG_INF = -1e30

IN_W_PAD = 5376
COL_GA = 3072
COL_GB = 4096
COL_F = 5120
F_PAD = 128

ADAM_LR = 0.001
ADAM_B1 = 0.9
ADAM_B2 = 0.999
ADAM_EPS = 1e-08
ADAM_WD = 0.01
ADAM_STEP = 10

LANES = 128
SUBLANES = 8
VMEM_LIMIT = 52 * 1024 * 1024
TOKEN_TILE = 512
ATTN_BLOCK = 512
PACK_W = 1024

MESH = pl.DeviceIdType.MESH
AXES = ("x", "y", "c")


def _params(sem=None, **kw):
    return pltpu.CompilerParams(dimension_semantics=sem, vmem_limit_bytes=VMEM_LIMIT, **kw)


def _full(shape):
    nd = len(shape)
    return pl.BlockSpec(shape, lambda *_: (0,) * nd)


def _matmul(a, b, *, name, tm, tn, tk, out_dtype=F32, trans_a=False, a_heads=False, out_heads=False):
    if a_heads:
        nh, rows, lanes = a.shape
        assert lanes == LANES and (tm if trans_a else tk) == LANES
        a_shape = (rows, nh * LANES)
    else:
        a_shape = a.shape
    if trans_a:
        K, M = a_shape
    else:
        M, K = a_shape
    N = b.shape[1]
    assert b.shape[0] == K and M % tm == 0 and N % tn == 0 and K % tk == 0, (name, a.shape, b.shape)
    assert not out_heads or tn == LANES
    nm, nn, nk = M // tm, N // tn, K // tk

    def body(a_ref, b_ref, o_ref, acc_ref):
        k = pl.program_id(2)

        @pl.when(k == 0)
        def _():
            acc_ref[...] = jnp.zeros_like(acc_ref)

        av = (a_ref[0] if a_heads else a_ref[...]).astype(BF16)
        bv = b_ref[...].astype(BF16)
        if trans_a:
            acc_ref[...] += lax.dot_general(av, bv, (((0,), (0,)), ((), ())), preferred_element_type=F32)
        else:
            acc_ref[...] += jnp.dot(av, bv, preferred_element_type=F32)

        @pl.when(k == nk - 1)
        def _():
            if out_heads:
                o_ref[0] = acc_ref[...].astype(out_dtype)
            else:
                o_ref[...] = acc_ref[...].astype(out_dtype)

    if a_heads and trans_a:
        a_spec = pl.BlockSpec((1, tk, LANES), lambda j, i, k: (i, k, 0))
    elif a_heads:
        a_spec = pl.BlockSpec((1, tm, LANES), lambda j, i, k: (k, i, 0))
    elif trans_a:
        a_spec = pl.BlockSpec((tk, tm), lambda j, i, k: (k, i))
    else:
        a_spec = pl.BlockSpec((tm, tk), lambda j, i, k: (i, k))
    if out_heads:
        out_spec = pl.BlockSpec((1, tm, LANES), lambda j, i, k: (j, i, 0))
        out_shape = jax.ShapeDtypeStruct((N // LANES, M, LANES), out_dtype)
    else:
        out_spec = pl.BlockSpec((tm, tn), lambda j, i, k: (i, j))
        out_shape = jax.ShapeDtypeStruct((M, N), out_dtype)
    return pl.pallas_call(
        body, name=name, grid=(nn, nm, nk),
        in_specs=[a_spec, pl.BlockSpec((tk, tn), lambda j, i, k: (k, j))],
        out_specs=out_spec, out_shape=out_shape,
        scratch_shapes=[pltpu.VMEM((tm, tn), F32)],
        compiler_params=_params(("parallel", "parallel", "arbitrary")),
    )(a, b)


def _split_dot(x, mat, parts):
    out = None
    rem = x
    for p in range(parts):
        piece = rem.astype(BF16)
        term = jnp.dot(piece, mat, preferred_element_type=F32)
        out = term if out is None else out + term
        if p + 1 < parts:
            rem = rem - piece.astype(F32)
    return out


def _sigmoid(x):
    return 1.0 / (1.0 + jnp.exp(-x))


def _rows8(x):
    r, c = x.shape
    return jnp.sum(x.reshape(r // SUBLANES, SUBLANES, c), axis=0)


def _shift_down(blk, prev8, n):
    rolled = pltpu.roll(blk, n, axis=0)
    prev_rolled = pltpu.roll(prev8, n, axis=0)
    rows = lax.broadcasted_iota(jnp.int32, prev8.shape, 0)
    first = jnp.where(rows < n, prev_rolled, rolled[0:SUBLANES])
    return jnp.concatenate([first, rolled[SUBLANES:]], axis=0)


def _shift_up(blk, next8, n):
    r = blk.shape[0]
    rolled = pltpu.roll(blk, r - n, axis=0)
    next_rolled = pltpu.roll(next8, SUBLANES - n, axis=0)
    rows = lax.broadcasted_iota(jnp.int32, next8.shape, 0)
    last = jnp.where(rows >= SUBLANES - n, next_rolled, rolled[r - SUBLANES:])
    return jnp.concatenate([rolled[:r - SUBLANES], last], axis=0)


def _prev_spec(tm, width, col):
    per = tm // SUBLANES
    return pl.BlockSpec((SUBLANES, width), lambda i, *_: (jnp.maximum(i * per - 1, 0), col))


def _next_spec(tm, width, col, n_tiles):
    per = tm // SUBLANES
    last = n_tiles * per - 1
    return pl.BlockSpec((SUBLANES, width), lambda i, *_: (jnp.minimum((i + 1) * per, last), col))


def _group_matrix():
    idx = np.arange(ATTN_W) // HEAD_DIM
    return jnp.asarray((idx[:, None] == idx[None, :]).astype(np.float32), BF16)


def _norm_mod(x, g, sc, sh, *, name):
    T = x.shape[0]
    tm = min(TOKEN_TILE, T)

    def body(x_ref, g_ref, sc_ref, sh_ref, o_ref):
        xv = x_ref[...]
        inv = lax.rsqrt(jnp.mean(xv * xv, axis=-1, keepdims=True) + RMS_EPS)
        o_ref[...] = ((xv * inv) * g_ref[...] * (1.0 + sc_ref[...]) + sh_ref[...]).astype(BF16)

    row = pl.BlockSpec((tm, D), lambda i: (i, 0))
    return pl.pallas_call(
        body, name=name, grid=(T // tm,),
        in_specs=[row, _full((1, D)), _full((1, D)), _full((1, D))],
        out_specs=row, out_shape=jax.ShapeDtypeStruct((T, D), BF16),
        compiler_params=_params(("parallel",)),
    )(x, g, sc, sh)


LANE_ONE = 64
LANE_F = 67
LANE_LSE = 70
LANE_SUM = 73


def _pieces(x):
    hi = x.astype(BF16).astype(F32)
    rest = x - hi
    mid = rest.astype(BF16).astype(F32)
    return hi, mid, rest - mid


def _aug(lane, data, entries):
    out = jnp.where(lane < HEAD_DIM, data, 0.0)
    for idx, val in entries:
        out = jnp.where(lane == idx, val, out)
    return out


def _run(start, vals):
    return [(start + i, v) for i, v in enumerate(vals)]


def _head_lanes(a, h):
    blk = a[:, LANES * (h // 2):LANES * (h // 2) + LANES]
    return blk if h % 2 == 0 else pltpu.roll(blk, HEAD_DIM, axis=1)


def _branch_prep(proj, fcum, conv_w8, qg, kg, gmat):
    T = proj.shape[0]
    tm = min(TOKEN_TILE, T)
    nt = T // tm

    def body(cb_ref, cc_ref, cv_ref, q_ref, k_ref, v_ref, f_ref, ccp_ref, cvp_ref, w_ref, qg_ref, kg_ref, g_ref,
             ya_ref, qa_ref, ka_ref, va_ref):
        i = pl.program_id(0)
        z = cc_ref[...] * cv_ref[...]
        zp = jnp.where(i > 0, ccp_ref[...] * cvp_ref[...], 0.0)
        w = w_ref[...]
        cz = _shift_down(z, zp, 2) * w[0:1] + _shift_down(z, zp, 1) * w[1:2] + z * w[2:3]
        ya_ref[...] = (cb_ref[...] * cz).astype(BF16)
        gm = g_ref[...]

        def normed(src, gain, scale):
            v = src[...]
            ms = _split_dot(v * v, gm, 2) * (1.0 / HEAD_DIM)
            return (v * lax.rsqrt(ms + RMS_EPS)) * gain[...] * scale

        qn = normed(q_ref, qg_ref, 1.0 / np.sqrt(HEAD_DIM))
        kn = normed(k_ref, kg_ref, 1.0)
        vv = v_ref[...]
        fall = f_ref[...]
        lane = lax.broadcasted_iota(jnp.int32, (tm, LANES), 1)
        ones3 = [1.0, 1.0, 1.0]
        for h in range(N_HEADS):
            hi, mid, lo = _pieces(fall[:, h:h + 1])
            qa_ref[h] = _aug(lane, _head_lanes(qn, h), _run(LANE_ONE, ones3) + _run(LANE_F, [hi, mid, lo])
                             + [(LANE_SUM, 1.0)]).astype(BF16)
            ka_ref[h] = _aug(lane, _head_lanes(kn, h), _run(LANE_ONE, [-hi, -mid, -lo]) + _run(LANE_F, ones3)
                             + _run(LANE_LSE, ones3)).astype(BF16)
            va_ref[h] = _aug(lane, _head_lanes(vv, h), _run(LANE_ONE, ones3)).astype(BF16)

    blk = lambda col: pl.BlockSpec((tm, CONV_W), lambda i: (i, col))
    heads = pl.BlockSpec((N_HEADS, tm, LANES), lambda i: (0, i, 0))
    return pl.pallas_call(
        body, name="branch_prep", grid=(nt,),
        in_specs=[blk(0), blk(1), blk(2), blk(3), blk(4), blk(5), pl.BlockSpec((tm, F_PAD), lambda i: (i, 0)),
                  _prev_spec(tm, CONV_W, 1), _prev_spec(tm, CONV_W, 2),
                  _full((SUBLANES, CONV_W)), _full((1, ATTN_W)), _full((1, ATTN_W)), _full((ATTN_W, ATTN_W))],
        out_specs=[pl.BlockSpec((tm, CONV_W), lambda i: (i, 0)), heads, heads, heads],
        out_shape=[jax.ShapeDtypeStruct((T, CONV_W), BF16)] + [jax.ShapeDtypeStruct((N_HEADS, T, LANES), BF16)] * 3,
        compiler_params=_params(("parallel",)),
    )(proj, proj, proj, proj, proj, proj, fcum, proj, proj, conv_w8, qg, kg, gmat)


def _cumsum(x, *, reverse, name, col=0, gate_bias=None):
    T = x.shape[0]
    tm = min(TOKEN_TILE, T)
    nt = T // tm

    def body(x_ref, b_ref, o_ref, carry_ref):
        i = pl.program_id(0)

        @pl.when(i == 0)
        def _():
            carry_ref[...] = jnp.zeros_like(carry_ref)

        r = lax.broadcasted_iota(jnp.int32, (tm, tm), 0)
        c = lax.broadcasted_iota(jnp.int32, (tm, tm), 1)
        tri = jnp.where((c >= r) if reverse else (c <= r), 1.0, 0.0).astype(BF16)
        xv = x_ref[...]
        if gate_bias is not None:
            fx = xv + b_ref[...]
            xv = jnp.minimum(fx, 0.0) - jnp.log(1.0 + jnp.exp(-jnp.abs(fx)))
        out = _split_dot_left(tri, xv, 3) + carry_ref[0:1]
        o_ref[...] = out
        carry_ref[...] = jnp.broadcast_to(out[0:1] if reverse else out[tm - 1:tm], carry_ref.shape)

    rows = (lambda i: nt - 1 - i) if reverse else (lambda i: i)
    bias = jnp.zeros((1, F_PAD), F32) if gate_bias is None else gate_bias
    return pl.pallas_call(
        body, name=name, grid=(nt,),
        in_specs=[pl.BlockSpec((tm, F_PAD), lambda i: (rows(i), col)), _full((1, F_PAD))],
        out_specs=pl.BlockSpec((tm, F_PAD), lambda i: (rows(i), 0)),
        out_shape=jax.ShapeDtypeStruct((T, F_PAD), F32),
        scratch_shapes=[pltpu.VMEM((SUBLANES, F_PAD), F32)],
        compiler_params=_params(("arbitrary",)),
    )(x, bias)


def _split_dot_left(mat, x, parts):
    out = None
    rem = x
    for p in range(parts):
        piece = rem.astype(BF16)
        term = jnp.dot(mat, piece, preferred_element_type=F32)
        out = term if out is None else out + term
        if p + 1 < parts:
            rem = rem - piece.astype(F32)
    return out


def _merge_fwd(ya, yb, proj):
    T = ya.shape[0]
    tm = min(TOKEN_TILE, T)

    def body(ya_ref, yb_ref, ga_ref, gb_ref, o_ref):
        o_ref[...] = (_sigmoid(ga_ref[...]) * ya_ref[...] + _sigmoid(gb_ref[...]) * yb_ref[...]).astype(BF16)

    row = pl.BlockSpec((tm, D), lambda i: (i, 0))
    return pl.pallas_call(
        body, name="merge_fwd", grid=(T // tm,),
        in_specs=[row, row, pl.BlockSpec((tm, D), lambda i: (i, COL_GA // D)),
                  pl.BlockSpec((tm, D), lambda i: (i, COL_GB // D))],
        out_specs=row, out_shape=jax.ShapeDtypeStruct((T, D), BF16),
        compiler_params=_params(("parallel",)),
    )(ya, yb, proj, proj)


def _resid_norm(x, mix, g1, g, sc, sh):
    T = x.shape[0]
    tm = min(TOKEN_TILE, T)

    def body(x_ref, mix_ref, g1_ref, g_ref, sc_ref, sh_ref, x1_ref, h_ref):
        x1 = x_ref[...] + g1_ref[...] * mix_ref[...]
        x1_ref[...] = x1
        inv = lax.rsqrt(jnp.mean(x1 * x1, axis=-1, keepdims=True) + RMS_EPS)
        h_ref[...] = ((x1 * inv) * g_ref[...] * (1.0 + sc_ref[...]) + sh_ref[...]).astype(BF16)

    row = pl.BlockSpec((tm, D), lambda i: (i, 0))
    vec = _full((1, D))
    return pl.pallas_call(
        body, name="resid_norm", grid=(T // tm,),
        in_specs=[row, row, vec, vec, vec, vec], out_specs=[row, row],
        out_shape=[jax.ShapeDtypeStruct((T, D), F32), jax.ShapeDtypeStruct((T, D), BF16)],
        compiler_params=_params(("parallel",)),
    )(x, mix, g1, g, sc, sh)


FFN_TM = 256
FFN_TC = 1408


def _ffn_act_fwd(u, w8):
    T = u.shape[0]
    tm = min(FFN_TM, T)
    nt = T // tm
    nc = D_FF // FFN_TC

    def body(ug_ref, uv_ref, ugp_ref, uvp_ref, wg_ref, wv_ref, o_ref):
        i = pl.program_id(1)

        def conv(u_ref, p_ref, w_ref):
            uv = u_ref[...]
            up = jnp.where(i > 0, p_ref[...], 0.0)
            w = w_ref[...]
            return _shift_down(uv, up, 2) * w[0:1] + _shift_down(uv, up, 1) * w[1:2] + uv * w[2:3]

        gate = conv(ug_ref, ugp_ref, wg_ref)
        val = conv(uv_ref, uvp_ref, wv_ref)
        o_ref[...] = (gate * _sigmoid(gate) * val).astype(BF16)

    per = tm // SUBLANES
    blk = lambda off: pl.BlockSpec((tm, FFN_TC), lambda j, i: (i, j + off))
    prev = lambda off: pl.BlockSpec((SUBLANES, FFN_TC), lambda j, i: (jnp.maximum(i * per - 1, 0), j + off))
    wblk = lambda off: pl.BlockSpec((SUBLANES, FFN_TC), lambda j, i: (0, j + off))
    return pl.pallas_call(
        body, name="ffn_act_fwd", grid=(nc, nt),
        in_specs=[blk(0), blk(nc), prev(0), prev(nc), wblk(0), wblk(nc)],
        out_specs=pl.BlockSpec((tm, FFN_TC), lambda j, i: (i, j)),
        out_shape=jax.ShapeDtypeStruct((T, D_FF), BF16),
        compiler_params=_params(("parallel", "parallel")),
    )(u, u, u, u, w8, w8)


def _loss_head(x1, ff, g2, target):
    T = x1.shape[0]
    tm = min(TOKEN_TILE, T)

    def body(x1_ref, ff_ref, g2_ref, t_ref, dy_ref, dff_ref, loss_ref, dg2_ref):
        i = pl.program_id(0)

        @pl.when(i == 0)
        def _():
            loss_ref[...] = jnp.zeros_like(loss_ref)
            dg2_ref[...] = jnp.zeros_like(dg2_ref)

        ff = ff_ref[...]
        err = x1_ref[...] + g2_ref[...] * ff - t_ref[...]
        dy = err * (1.0 / D)
        dy_ref[...] = dy
        dff_ref[...] = (dy * g2_ref[...]).astype(BF16)
        loss_ref[...] += _rows8(err * err)
        dg2_ref[...] += _rows8(dy * ff)

    row = pl.BlockSpec((tm, D), lambda i: (i, 0))
    acc = _full((SUBLANES, D))
    return pl.pallas_call(
        body, name="loss_head", grid=(T // tm,),
        in_specs=[row, row, _full((1, D)), row], out_specs=[row, row, acc, acc],
        out_shape=[jax.ShapeDtypeStruct((T, D), F32), jax.ShapeDtypeStruct((T, D), BF16),
                   jax.ShapeDtypeStruct((SUBLANES, D), F32), jax.ShapeDtypeStruct((SUBLANES, D), F32)],
        compiler_params=_params(("arbitrary",)),
    )(x1, ff, g2, target)


def _nt_dot(a, b):
    return lax.dot_general(a, b, (((1,), (1,)), ((), ())), preferred_element_type=F32)


def _causal(n, keys_on_rows=False):
    r = lax.broadcasted_iota(jnp.int32, (n, n), 0)
    c = lax.broadcasted_iota(jnp.int32, (n, n), 1)
    return (c >= r) if keys_on_rows else (c <= r)


def _attn_fwd(qa, ka, va):
    nh, T, _ = qa.shape
    bq = min(ATTN_BLOCK, T)
    nq = T // bq

    def body(q_ref, k_ref, v_ref, o_ref, qb_ref):
        qi = pl.program_id(1)
        q = q_ref[0]

        def step(kb, carry, masked):
            m, acc = carry
            start = pl.multiple_of(kb * bq, bq)
            s = _nt_dot(q, k_ref[0, pl.ds(start, bq), :])
            if masked:
                s = jnp.where(_causal(bq), s, NEG_INF)
            m_new = jnp.maximum(m, jnp.max(s, axis=-1, keepdims=True))
            p = jnp.exp(s - m_new).astype(BF16)
            acc = jnp.exp(m - m_new) * acc + jnp.dot(p, v_ref[0, pl.ds(start, bq), :], preferred_element_type=F32)
            return m_new, acc

        init = (jnp.full((bq, 1), NEG_INF, F32), jnp.zeros((bq, LANES), F32))
        carry = lax.fori_loop(0, qi, lambda kb, cr: step(kb, cr, False), init)
        m, acc = step(qi, carry, True)
        l = acc[:, LANE_ONE:LANE_ONE + 1]
        o_ref[0] = acc / l
        lse = m + jnp.log(l)
        lane = lax.broadcasted_iota(jnp.int32, (bq, LANES), 1)
        qf = q.astype(F32)
        for idx, piece in _run(LANE_LSE, _pieces(lse)):
            qf = jnp.where(lane == idx, -piece, qf)
        qb_ref[0] = qf.astype(BF16)

    tile = pl.BlockSpec((1, bq, LANES), lambda h, i: (h, i, 0))
    whole = pl.BlockSpec((1, T, LANES), lambda h, i: (h, 0, 0))
    return pl.pallas_call(
        body, name="attn_fwd", grid=(nh, nq),
        in_specs=[tile, whole, whole], out_specs=[tile, tile],
        out_shape=[jax.ShapeDtypeStruct((nh, T, LANES), F32), jax.ShapeDtypeStruct((nh, T, LANES), BF16)],
        compiler_params=_params(("parallel", "parallel")),
    )(qa, ka, va)


def _attn_delta(do_pad, o):
    nh, T, _ = o.shape
    tm = min(TOKEN_TILE, T)

    def body(do_ref, o_ref, out_ref):
        g = do_ref[0].astype(F32)
        delta = jnp.sum(g * o_ref[0], axis=-1, keepdims=True)
        lane = lax.broadcasted_iota(jnp.int32, (tm, LANES), 1)
        for idx, piece in _run(LANE_ONE, _pieces(delta)):
            g = jnp.where(lane == idx, -piece, g)
        out_ref[0] = g.astype(BF16)

    tile = pl.BlockSpec((1, tm, LANES), lambda h, i: (h, i, 0))
    return pl.pallas_call(
        body, name="attn_delta", grid=(nh, T // tm), in_specs=[tile, tile], out_specs=tile,
        out_shape=jax.ShapeDtypeStruct((nh, T, LANES), BF16),
        compiler_params=_params(("parallel", "parallel")),
    )(do_pad, o)


def _attn_bwd_kv(qb, ka, va, doa):
    nh, T, _ = qb.shape
    bk = min(ATTN_BLOCK, T)
    nk = T // bk

    def body(q_ref, do_ref, k_ref, v_ref, dk_ref, dv_ref):
        ki = pl.program_id(1)
        k = k_ref[0]
        v = v_ref[0]

        def step(qi, carry, masked):
            dk, dv = carry
            start = pl.multiple_of(qi * bk, bk)
            q = q_ref[0, pl.ds(start, bk), :]
            g = do_ref[0, pl.ds(start, bk), :]
            pt = jnp.exp(_nt_dot(k, q))
            if masked:
                pt = jnp.where(_causal(bk, keys_on_rows=True), pt, 0.0)
            dv = dv + jnp.dot(pt.astype(BF16), g, preferred_element_type=F32)
            dst = (pt * _nt_dot(v, g)).astype(BF16)
            dk = dk + jnp.dot(dst, q, preferred_element_type=F32)
            return dk, dv

        init = (jnp.zeros((bk, LANES), F32), jnp.zeros((bk, LANES), F32))
        carry = step(ki, init, True)
        dk_ref[0], dv_ref[0] = lax.fori_loop(ki + 1, nk, lambda qi, cr: step(qi, cr, False), carry)

    tile = pl.BlockSpec((1, bk, LANES), lambda h, i: (h, i, 0))
    whole = pl.BlockSpec((1, T, LANES), lambda h, i: (h, 0, 0))
    return pl.pallas_call(
        body, name="attn_bwd_kv", grid=(nh, nk),
        in_specs=[whole, whole, tile, tile], out_specs=[tile, tile],
        out_shape=[jax.ShapeDtypeStruct((nh, T, LANES), F32)] * 2,
        compiler_params=_params(("parallel", "parallel")),
    )(qb, doa, ka, va)


def _attn_bwd_q(qb, ka, va, doa):
    nh, T, _ = qb.shape
    bq = min(ATTN_BLOCK, T)
    nq = T // bq

    def body(q_ref, do_ref, k_ref, v_ref, dq_ref):
        qi = pl.program_id(1)
        q = q_ref[0]
        g = do_ref[0]

        def step(kb, dq, masked):
            start = pl.multiple_of(kb * bq, bq)
            k = k_ref[0, pl.ds(start, bq), :]
            p = jnp.exp(_nt_dot(q, k))
            if masked:
                p = jnp.where(_causal(bq), p, 0.0)
            ds = (p * _nt_dot(g, v_ref[0, pl.ds(start, bq), :])).astype(BF16)
            return dq + jnp.dot(ds, k, preferred_element_type=F32)

        dq = lax.fori_loop(0, qi, lambda kb, cr: step(kb, cr, False), jnp.zeros((bq, LANES), F32))
        dq_ref[0] = step(qi, dq, True)

    tile = pl.BlockSpec((1, bq, LANES), lambda h, i: (h, i, 0))
    whole = pl.BlockSpec((1, T, LANES), lambda h, i: (h, 0, 0))
    return pl.pallas_call(
        body, name="attn_bwd_q", grid=(nh, nq),
        in_specs=[tile, tile, whole, whole], out_specs=tile,
        out_shape=jax.ShapeDtypeStruct((nh, T, LANES), F32),
        compiler_params=_params(("parallel", "parallel")),
    )(qb, doa, ka, va)


def _attn_unpack(dq_h, dk_h, dv_h):
    nh, T, _ = dq_h.shape
    tm = min(TOKEN_TILE, T)

    def body(dq_ref, dk_ref, dv_ref, q_out, k_out, v_out, f_out):
        lane = lax.broadcasted_iota(jnp.int32, (tm, LANES), 1)
        low = lane < HEAD_DIM
        for src, dst in ((dq_ref, q_out), (dk_ref, k_out), (dv_ref, v_out)):
            for pair in range(nh // 2):
                both = jnp.where(low, src[2 * pair], pltpu.roll(src[2 * pair + 1], HEAD_DIM, axis=1))
                dst[:, LANES * pair:LANES * (pair + 1)] = both.astype(dst.dtype)
        df = jnp.zeros((tm, LANES), F32)
        for h in range(nh):
            col = dq_ref[h][:, LANE_F:LANE_F + 1] - dk_ref[h][:, LANE_SUM:LANE_SUM + 1]
            df = jnp.where(lane == h, col, df)
        f_out[...] = df

    heads = pl.BlockSpec((nh, tm, LANES), lambda i: (0, i, 0))
    tok = pl.BlockSpec((tm, ATTN_W), lambda i: (i, 0))
    return pl.pallas_call(
        body, name="attn_unpack", grid=(T // tm,), in_specs=[heads, heads, heads],
        out_specs=[tok, tok, tok, pl.BlockSpec((tm, F_PAD), lambda i: (i, 0))],
        out_shape=[jax.ShapeDtypeStruct((T, ATTN_W), F32), jax.ShapeDtypeStruct((T, ATTN_W), F32),
                   jax.ShapeDtypeStruct((T, ATTN_W), BF16), jax.ShapeDtypeStruct((T, F_PAD), F32)],
        compiler_params=_params(("parallel",)),
    )(dq_h, dk_h, dv_h)


def _ffn_act_bwd(u, da, w8):
    T = u.shape[0]
    tm = min(FFN_TM, T)
    nt = T // tm
    nc = D_FF // FFN_TC

    def body(ug_ref, uv_ref, ugp_ref, uvp_ref, ugn_ref, uvn_ref, da_ref, dan_ref, wg_ref, wv_ref,
             dug_ref, duv_ref, dwg_ref, dwv_ref):
        i = pl.program_id(1)

        @pl.when(i == 0)
        def _():
            dwg_ref[...] = jnp.zeros_like(dwg_ref)
            dwv_ref[...] = jnp.zeros_like(dwv_ref)

        first, last = i == 0, i == nt - 1
        wg, wv = wg_ref[...], wv_ref[...]
        zeros8 = jnp.zeros((SUBLANES, FFN_TC), F32)

        def window(u_ref, p_ref, n_ref, w):
            e = jnp.concatenate([jnp.where(first, 0.0, p_ref[...]), u_ref[...], n_ref[...]], axis=0)
            e1 = pltpu.roll(e, 1, axis=0)
            e2 = pltpu.roll(e, 2, axis=0)
            return e, e1, e2, e2 * w[0:1] + e1 * w[1:2] + e * w[2:3]

        eg, eg1, eg2, cg = window(ug_ref, ugp_ref, ugn_ref, wg)
        ev, ev1, ev2, cv = window(uv_ref, uvp_ref, uvn_ref, wv)
        dae = jnp.concatenate([zeros8, da_ref[...].astype(F32), jnp.where(last, 0.0, dan_ref[...].astype(F32))],
                              axis=0)
        sg = _sigmoid(cg)
        dgate = dae * cv * sg * (1.0 + cg * (1.0 - sg))
        dval = dae * cg * sg
        n = tm + 2 * SUBLANES

        def back(d, w):
            return d * w[2:3] + pltpu.roll(d, n - 1, axis=0) * w[1:2] + pltpu.roll(d, n - 2, axis=0) * w[0:1]

        inner = slice(SUBLANES, SUBLANES + tm)
        dug_ref[...] = back(dgate, wg)[inner].astype(BF16)
        duv_ref[...] = back(dval, wv)[inner].astype(BF16)

        def wgrad(d, e, e1, e2):
            rows = [jnp.sum((d * t)[inner], axis=0, keepdims=True) for t in (e2, e1, e)]
            return jnp.concatenate(rows + [jnp.zeros((SUBLANES - 3, FFN_TC), F32)], axis=0)

        dwg_ref[...] += wgrad(dgate, eg, eg1, eg2)
        dwv_ref[...] += wgrad(dval, ev, ev1, ev2)

    per = tm // SUBLANES
    last_blk = nt * per - 1
    blk = lambda off: pl.BlockSpec((tm, FFN_TC), lambda j, i: (i, j + off))
    prev = lambda off: pl.BlockSpec((SUBLANES, FFN_TC), lambda j, i: (jnp.maximum(i * per - 1, 0), j + off))
    nxt = lambda off: pl.BlockSpec((SUBLANES, FFN_TC), lambda j, i: (jnp.minimum((i + 1) * per, last_blk), j + off))
    wblk = lambda off: pl.BlockSpec((SUBLANES, FFN_TC), lambda j, i: (0, j + off))
    dug, duv, dwg, dwv = pl.pallas_call(
        body, name="ffn_act_bwd", grid=(nc, nt),
        in_specs=[blk(0), blk(nc), prev(0), prev(nc), nxt(0), nxt(nc), blk(0), nxt(0), wblk(0), wblk(nc)],
        out_specs=[blk(0), blk(0), wblk(0), wblk(0)],
        out_shape=[jax.ShapeDtypeStruct((T, D_FF), BF16)] * 2 + [jax.ShapeDtypeStruct((SUBLANES, D_FF), F32)] * 2,
        compiler_params=_params(("parallel", "arbitrary")),
    )(u, u, u, u, u, u, da, da, w8, w8)
    return jnp.concatenate([dug, duv], axis=1), jnp.concatenate([dwg, dwv], axis=1)


def _norm_bwd(xin, dh, dres, g, sc, *, name):
    T = xin.shape[0]
    tm = min(TOKEN_TILE, T)

    def body(x_ref, dh_ref, dr_ref, g_ref, sc_ref, dx_ref, dsh_ref, dsc_ref, dg_ref):
        i = pl.program_id(0)

        @pl.when(i == 0)
        def _():
            dsh_ref[...] = jnp.zeros_like(dsh_ref)
            dsc_ref[...] = jnp.zeros_like(dsc_ref)
            dg_ref[...] = jnp.zeros_like(dg_ref)

        xv = x_ref[...]
        dh = dh_ref[...]
        gv = g_ref[...]
        one_sc = 1.0 + sc_ref[...]
        inv = lax.rsqrt(jnp.mean(xv * xv, axis=-1, keepdims=True) + RMS_EPS)
        xn = xv * inv
        dxn = dh * (gv * one_sc)
        dx_ref[...] = dr_ref[...] + inv * (dxn - xn * jnp.mean(dxn * xn, axis=-1, keepdims=True))
        dhxn = dh * xn
        dsh_ref[...] += _rows8(dh)
        dsc_ref[...] += _rows8(dhxn * gv)
        dg_ref[...] += _rows8(dhxn * one_sc)

    row = pl.BlockSpec((tm, D), lambda i: (i, 0))
    acc = _full((SUBLANES, D))
    return pl.pallas_call(
        body, name=name, grid=(T // tm,),
        in_specs=[row, row, row, _full((1, D)), _full((1, D))], out_specs=[row, acc, acc, acc],
        out_shape=[jax.ShapeDtypeStruct((T, D), F32)] + [jax.ShapeDtypeStruct((SUBLANES, D), F32)] * 3,
        compiler_params=_params(("arbitrary",)),
    )(xin, dh, dres, g, sc)


def _gate_bwd(dx1, mix, g1):
    T = dx1.shape[0]
    tm = min(TOKEN_TILE, T)

    def body(dx_ref, mix_ref, g1_ref, dmix_ref, dg1_ref):
        @pl.when(pl.program_id(0) == 0)
        def _():
            dg1_ref[...] = jnp.zeros_like(dg1_ref)

        dx = dx_ref[...]
        dmix_ref[...] = (dx * g1_ref[...]).astype(BF16)
        dg1_ref[...] += _rows8(dx * mix_ref[...])

    row = pl.BlockSpec((tm, D), lambda i: (i, 0))
    return pl.pallas_call(
        body, name="gate_bwd", grid=(T // tm,),
        in_specs=[row, row, _full((1, D))], out_specs=[row, _full((SUBLANES, D))],
        out_shape=[jax.ShapeDtypeStruct((T, D), BF16), jax.ShapeDtypeStruct((SUBLANES, D), F32)],
        compiler_params=_params(("arbitrary",)),
    )(dx1, mix, g1)


def _merge_bwd(dmerged, ya, yb, proj):
    T = ya.shape[0]
    tm = min(TOKEN_TILE, T)

    def body(dm_ref, ya_ref, yb_ref, ga_ref, gb_ref, dya_ref, dyb_ref, dga_ref, dgb_ref):
        dm = dm_ref[...]
        sa = _sigmoid(ga_ref[...])
        sb = _sigmoid(gb_ref[...])
        dya_ref[...] = (dm * sa).astype(BF16)
        dyb_ref[...] = (dm * sb).astype(BF16)
        dga_ref[...] = (dm * ya_ref[...] * sa * (1.0 - sa)).astype(BF16)
        dgb_ref[...] = (dm * yb_ref[...] * sb * (1.0 - sb)).astype(BF16)

    row = pl.BlockSpec((tm, D), lambda i: (i, 0))
    return pl.pallas_call(
        body, name="merge_bwd", grid=(T // tm,),
        in_specs=[row, row, row, pl.BlockSpec((tm, D), lambda i: (i, COL_GA // D)),
                  pl.BlockSpec((tm, D), lambda i: (i, COL_GB // D))],
        out_specs=[row] * 4, out_shape=[jax.ShapeDtypeStruct((T, D), BF16)] * 4,
        compiler_params=_params(("parallel",)),
    )(dmerged, ya, yb, proj, proj)


def _conv_branch_bwd(proj, dya0, conv_w8):
    T = proj.shape[0]
    tm = min(FFN_TM, T)
    nt = T // tm

    def body(cb_ref, cc_ref, cv_ref, cbn_ref, ccp_ref, cvp_ref, ccn_ref, cvn_ref, d_ref, dn_ref, w_ref,
             dcb_ref, dcc_ref, dcv_ref, dw_ref):
        i = pl.program_id(0)

        @pl.when(i == 0)
        def _():
            dw_ref[...] = jnp.zeros_like(dw_ref)

        first, last = i == 0, i == nt - 1
        w = w_ref[...]
        cc = jnp.concatenate([ccp_ref[...], cc_ref[...], ccn_ref[...]], axis=0)
        cv = jnp.concatenate([cvp_ref[...], cv_ref[...], cvn_ref[...]], axis=0)
        rows = lax.broadcasted_iota(jnp.int32, cc.shape, 0)
        z = jnp.where(jnp.logical_and(first, rows < SUBLANES), 0.0, cc * cv)
        z1 = pltpu.roll(z, 1, axis=0)
        z2 = pltpu.roll(z, 2, axis=0)
        cz = z2 * w[0:1] + z1 * w[1:2] + z * w[2:3]
        zeros8 = jnp.zeros((SUBLANES, CONV_W), F32)
        de = jnp.concatenate([zeros8, d_ref[...], jnp.where(last, 0.0, dn_ref[...])], axis=0)
        cbe = jnp.concatenate([zeros8, cb_ref[...], cbn_ref[...]], axis=0)
        dcz = de * cbe
        n = tm + 2 * SUBLANES
        dz = dcz * w[2:3] + pltpu.roll(dcz, n - 1, axis=0) * w[1:2] + pltpu.roll(dcz, n - 2, axis=0) * w[0:1]
        inner = slice(SUBLANES, SUBLANES + tm)
        dcb_ref[...] = (de * cz)[inner].astype(BF16)
        dcc_ref[...] = (dz * cv)[inner].astype(BF16)
        dcv_ref[...] = (dz * cc)[inner].astype(BF16)
        wrows = [jnp.sum((dcz * t)[inner], axis=0, keepdims=True) for t in (z2, z1, z)]
        dw_ref[...] += jnp.concatenate(wrows + [jnp.zeros((SUBLANES - 3, CONV_W), F32)], axis=0)

    blk = lambda col: pl.BlockSpec((tm, CONV_W), lambda i: (i, col))
    out_blk = pl.BlockSpec((tm, CONV_W), lambda i: (i, 0))
    return pl.pallas_call(
        body, name="conv_branch_bwd", grid=(nt,),
        in_specs=[blk(0), blk(1), blk(2), _next_spec(tm, CONV_W, 0, nt),
                  _prev_spec(tm, CONV_W, 1), _prev_spec(tm, CONV_W, 2),
                  _next_spec(tm, CONV_W, 1, nt), _next_spec(tm, CONV_W, 2, nt),
                  out_blk, _next_spec(tm, CONV_W, 0, nt), _full((SUBLANES, CONV_W))],
        out_specs=[out_blk, out_blk, out_blk, _full((SUBLANES, CONV_W))],
        out_shape=[jax.ShapeDtypeStruct((T, CONV_W), BF16)] * 3 + [jax.ShapeDtypeStruct((SUBLANES, CONV_W), F32)],
        compiler_params=_params(("arbitrary",)),
    )(proj, proj, proj, proj, proj, proj, proj, proj, dya0, dya0, conv_w8)


def _qk_norm_bwd(proj, dqs, dkh, dlogf, qg, kg, bf_pad, gmat):
    T = proj.shape[0]
    tm = min(TOKEN_TILE, T)

    def body(q_ref, k_ref, f_ref, dqs_ref, dkh_ref, dlf_ref, qg_ref, kg_ref, bf_ref, g_ref,
             dq_ref, dk_ref, dfl_ref, dqg_ref, dkg_ref, dbf_ref):
        @pl.when(pl.program_id(0) == 0)
        def _():
            dqg_ref[...] = jnp.zeros_like(dqg_ref)
            dkg_ref[...] = jnp.zeros_like(dkg_ref)
            dbf_ref[...] = jnp.zeros_like(dbf_ref)

        gm = g_ref[...]
        for src, d_src, gain, scale, dst, dgain in (
                (q_ref, dqs_ref, qg_ref, 1.0 / np.sqrt(HEAD_DIM), dq_ref, dqg_ref),
                (k_ref, dkh_ref, kg_ref, 1.0, dk_ref, dkg_ref)):
            v = src[...]
            dhat = d_src[...] * scale
            inv = lax.rsqrt(_split_dot(v * v, gm, 2) * (1.0 / HEAD_DIM) + RMS_EPS)
            vn = v * inv
            dgain[...] += _rows8(dhat * vn)
            dvn = dhat * gain[...]
            mean = _split_dot(dvn * vn, gm, 2) * (1.0 / HEAD_DIM)
            dst[...] = (inv * (dvn - vn * mean)).astype(BF16)
        fx = f_ref[...] + bf_ref[...]
        dfl = dlf_ref[...] * _sigmoid(-fx)
        dfl_ref[...] = dfl.astype(BF16)
        dbf_ref[...] += _rows8(dfl)

    blk = lambda col: pl.BlockSpec((tm, ATTN_W), lambda i: (i, col))
    out_blk = pl.BlockSpec((tm, ATTN_W), lambda i: (i, 0))
    f_in = pl.BlockSpec((tm, F_PAD), lambda i: (i, COL_F // F_PAD))
    f_blk = pl.BlockSpec((tm, F_PAD), lambda i: (i, 0))
    return pl.pallas_call(
        body, name="qk_norm_bwd", grid=(T // tm,),
        in_specs=[blk(3), blk(4), f_in, out_blk, out_blk, f_blk, _full((1, ATTN_W)), _full((1, ATTN_W)),
                  _full((1, F_PAD)), _full((ATTN_W, ATTN_W))],
        out_specs=[out_blk, out_blk, f_blk, _full((SUBLANES, ATTN_W)), _full((SUBLANES, ATTN_W)),
                   _full((SUBLANES, F_PAD))],
        out_shape=[jax.ShapeDtypeStruct((T, ATTN_W), BF16)] * 2 + [jax.ShapeDtypeStruct((T, F_PAD), BF16)]
        + [jax.ShapeDtypeStruct((SUBLANES, ATTN_W), F32)] * 2 + [jax.ShapeDtypeStruct((SUBLANES, F_PAD), F32)],
        compiler_params=_params(("arbitrary",)),
    )(proj, proj, proj, dqs, dkh, dlogf, qg, kg, bf_pad, gmat)


def _pad_rows8(w):
    return jnp.pad(w, ((0, SUBLANES - w.shape[0]), (0, 0)))


def _fold8(acc):
    return jnp.sum(acc, axis=0, keepdims=True)


def _local_step(x, target, mod, wts):
    T = x.shape[0]
    tm = min(TOKEN_TILE, T)
    sh1, sc1, g1, sh2, sc2, g2 = [mod[:, i * D:(i + 1) * D] for i in range(N_MOD)]
    w_in, w_in_t = wts["w_in"], wts["w_in_t"]
    conv_a8 = _pad_rows8(wts["conv_a_w"])
    conv_f8 = _pad_rows8(wts["conv_ffn_w"])
    qg = jnp.tile(wts["q_norm_g"], (1, N_HEADS))
    kg = jnp.tile(wts["k_norm_g"], (1, N_HEADS))
    bf_pad = jnp.pad(wts["b_f"], ((0, 0), (0, F_PAD - N_HEADS)))
    gmat = _group_matrix()

    h = _norm_mod(x, wts["norm1_g"], sc1, sh1, name="norm1_fwd")
    proj = _matmul(h, w_in, name="mm_in", tm=tm, tn=896, tk=D)
    fcum = _cumsum(proj, reverse=False, name="gate_cumsum", col=COL_F // F_PAD, gate_bias=bf_pad)
    ya0, qa, ka, va = _branch_prep(proj, fcum, conv_a8, qg, kg, gmat)
    o_h, qb = _attn_fwd(qa, ka, va)
    ya = _matmul(ya0, wts["w_branch_a"], name="mm_branch_a", tm=tm, tn=D, tk=CONV_W)
    yb = _matmul(o_h, wts["w_branch_b_heads"], name="mm_branch_b", tm=tm, tn=D, tk=LANES, a_heads=True)
    merged = _merge_fwd(ya, yb, proj)
    mix = _matmul(merged, wts["w_out"], name="mm_out", tm=tm, tn=D, tk=D)
    x1, h2 = _resid_norm(x, mix, g1, wts["norm2_g"], sc2, sh2)
    u = _matmul(h2, wts["w_up"], name="mm_up", tm=tm, tn=1408, tk=D)
    act = _ffn_act_fwd(u, conv_f8)
    ff = _matmul(act, wts["w_down"], name="mm_down", tm=tm, tn=D, tk=D_FF)
    dy, dff, sq8, dg2_8 = _loss_head(x1, ff, g2, target)
    sq = jnp.sum(sq8).reshape(1, 1)

    grads = {}
    da = _matmul(dff, wts["w_down_t"], name="mm_down_dx", tm=tm, tn=1408, tk=D)
    grads["w_down"] = _matmul(act, dff, name="mm_down_dw", tm=1408, tn=D, tk=tm, trans_a=True)
    du, dconv_f8 = _ffn_act_bwd(u, da, conv_f8)
    grads["conv_ffn_w"] = dconv_f8[:3]
    dh2 = _matmul(du, wts["w_up_t"], name="mm_up_dx", tm=tm, tn=D, tk=1408)
    grads["w_up"] = _matmul(h2, du, name="mm_up_dw", tm=D, tn=1408, tk=tm, trans_a=True)
    dx1, dsh2_8, dsc2_8, dn2_8 = _norm_bwd(x1, dh2, dy, wts["norm2_g"], sc2, name="norm2_bwd")
    grads["norm2_g"] = _fold8(dn2_8)

    dmix, dg1_8 = _gate_bwd(dx1, mix, g1)
    dmerged = _matmul(dmix, wts["w_out_t"], name="mm_out_dx", tm=tm, tn=D, tk=D)
    grads["w_out"] = _matmul(merged, dmix, name="mm_out_dw", tm=D, tn=D, tk=tm, trans_a=True)
    dya, dyb, dga, dgb = _merge_bwd(dmerged, ya, yb, proj)
    dya0 = _matmul(dya, wts["w_branch_a_t"], name="mm_branch_a_dx", tm=tm, tn=CONV_W, tk=D)
    grads["w_branch_a"] = _matmul(ya0, dya, name="mm_branch_a_dw", tm=CONV_W, tn=D, tk=tm, trans_a=True)
    do_pad = _matmul(dyb, wts["w_branch_b_heads_t"], name="mm_branch_b_dx", tm=tm, tn=LANES, tk=D, out_dtype=BF16,
                     out_heads=True)
    dwbb = _matmul(o_h, dyb, name="mm_branch_b_dw", tm=LANES, tn=D, tk=tm, trans_a=True, a_heads=True)
    grads["w_branch_b"] = dwbb.reshape(N_HEADS, LANES, D)[:, :HEAD_DIM].reshape(ATTN_W, D)
    dcb, dcc, dcv, dconv_a8 = _conv_branch_bwd(proj, dya0, conv_a8)
    grads["conv_a_w"] = dconv_a8[:3]

    doa = _attn_delta(do_pad, o_h)
    dk_h, dv_h = _attn_bwd_kv(qb, ka, va, doa)
    dq_h = _attn_bwd_q(qb, ka, va, doa)
    dq_tok, dk_tok, dv_tok, dfcum = _attn_unpack(dq_h, dk_h, dv_h)
    dlogf = _cumsum(dfcum, reverse=True, name="gate_cumsum_bwd")
    dq, dk, dfl, dqg8, dkg8, dbf8 = _qk_norm_bwd(proj, dq_tok, dk_tok, dlogf, qg, kg, bf_pad, gmat)
    grads["q_norm_g"] = jnp.sum(_fold8(dqg8).reshape(N_HEADS, HEAD_DIM), axis=0, keepdims=True)
    grads["k_norm_g"] = jnp.sum(_fold8(dkg8).reshape(N_HEADS, HEAD_DIM), axis=0, keepdims=True)
    grads["b_f"] = _fold8(dbf8)[:, :N_HEADS]
    dproj = jnp.concatenate(
        [dcb, dcc, dcv, dq, dk, dv_tok, dga, dgb, dfl, jnp.zeros((T, IN_W_PAD - COL_F - F_PAD), BF16)], axis=1)
    dh = _matmul(dproj, w_in_t, name="mm_in_dx", tm=tm, tn=D, tk=896)
    grads["w_in_pad"] = _matmul(h, dproj, name="mm_in_dw", tm=D, tn=896, tk=tm, trans_a=True)
    grad_x, dsh1_8, dsc1_8, dn1_8 = _norm_bwd(x, dh, dx1, wts["norm1_g"], sc1, name="norm1_bwd")
    grads["norm1_g"] = _fold8(dn1_8)
    grads["mod"] = jnp.concatenate([_fold8(a) for a in (dsh1_8, dsc1_8, dg1_8, dsh2_8, dsc2_8, dg2_8)], axis=1)
    return sq, grad_x, grads


def _me_and_peers():
    mx, my, mc = lax.axis_index("x"), lax.axis_index("y"), lax.axis_index("c")
    me = 4 * mx + 2 * my + mc
    peers = []
    for k in range(1, N_DEV):
        px = 1 - mx if k & 4 else mx
        py = 1 - my if k & 2 else my
        pc = 1 - mc if k & 1 else mc
        peers.append(((px, py, pc), 4 * px + 2 * py + pc))
    return me, peers


HBM_SPEC = pl.BlockSpec(memory_space=pltpu.HBM)


def _exchange(xs, *, name, scatter):
    n = len(xs)
    out_shapes = [jax.ShapeDtypeStruct(x.shape if scatter else (N_DEV,) + x.shape, x.dtype) for x in xs]

    def body(*refs):
        x_refs, out_refs = refs[:n], refs[n:2 * n]
        send_sems, recv_sems, local_sems = refs[2 * n:]
        me, peers = _me_and_peers()

        def src(a, idx):
            return x_refs[a].at[idx] if scatter else x_refs[a]

        def copy(a, k, from_idx, to_slot, device):
            return pltpu.make_async_remote_copy(
                src_ref=src(a, from_idx), dst_ref=out_refs[a].at[to_slot], send_sem=send_sems.at[a, k],
                recv_sem=recv_sems.at[a, k], device_id=device, device_id_type=MESH)

        local = [pltpu.make_async_copy(src(a, me), out_refs[a].at[me], local_sems.at[a]) for a in range(n)]
        for cp in local:
            cp.start()
        sends = [copy(a, k, idx, me, dev) for a in range(n) for k, (dev, idx) in enumerate(peers)]
        for cp in sends:
            cp.start()
        for a in range(n):
            for k, (dev, idx) in enumerate(peers):
                copy(a, k, idx, idx, dev).wait_recv()
        for cp in sends:
            cp.wait_send()
        for cp in local:
            cp.wait()

    return pl.pallas_call(
        body, name=name, in_specs=[HBM_SPEC] * n, out_specs=[HBM_SPEC] * n, out_shape=out_shapes,
        scratch_shapes=[pltpu.SemaphoreType.DMA((n, N_DEV - 1)), pltpu.SemaphoreType.DMA((n, N_DEV - 1)),
                        pltpu.SemaphoreType.DMA((n,))],
        compiler_params=pltpu.CompilerParams(has_side_effects=True),
    )(*xs)


def _ada_fwd(c_all, w_shard, b_shard):
    n = w_shard.shape[1]

    def body(c_ref, w_ref, b_ref, o_ref):
        cv = c_ref[...]
        act = (cv * _sigmoid(cv)).astype(BF16)
        o_ref[...] = jnp.dot(act, w_ref[...].astype(BF16), preferred_element_type=F32) + b_ref[...]

    return pl.pallas_call(
        body, name="ada_fwd", in_specs=[_full((N_DEV, D)), _full((D, n)), _full((1, n))],
        out_specs=_full((N_DEV, n)), out_shape=jax.ShapeDtypeStruct((N_DEV, n), F32), grid=(1,),
        compiler_params=_params(("arbitrary",)),
    )(c_all, w_shard, b_shard)


def _ada_bwd(c_all_t, dmod_pad):
    n = dmod_pad.shape[1]

    def body(c_ref, d_ref, o_ref):
        cv = c_ref[...]
        act = (cv * _sigmoid(cv)).astype(BF16)
        o_ref[...] = jnp.dot(act, d_ref[...].astype(BF16), preferred_element_type=F32)

    return pl.pallas_call(
        body, name="ada_bwd", in_specs=[_full((D, LANES)), _full((LANES, n))],
        out_specs=_full((D, n)), out_shape=jax.ShapeDtypeStruct((D, n), F32), grid=(1,),
        compiler_params=_params(("arbitrary",)),
    )(c_all_t, dmod_pad)


ADAM_ROWS = 64


def _adamw(parts, w, m, v, *, name):
    n, R, C = parts.shape
    tr = next((t for t in (ADAM_ROWS, 32, 16, SUBLANES) if R % t == 0), R)

    def body(p_ref, w_ref, m_ref, v_ref, g_ref, d_ref, nm_ref, nv_ref):
        g = p_ref[0].astype(F32)
        for j in range(1, n):
            g = g + p_ref[j].astype(F32)
        g_ref[...] = g
        nm = ADAM_B1 * m_ref[...] + (1.0 - ADAM_B1) * g
        nv = ADAM_B2 * v_ref[...] + (1.0 - ADAM_B2) * (g * g)
        nm_ref[...] = nm
        nv_ref[...] = nv
        m_hat = nm / (1.0 - ADAM_B1 ** ADAM_STEP)
        v_hat = nv / (1.0 - ADAM_B2 ** ADAM_STEP)
        d_ref[...] = -ADAM_LR * (m_hat / (jnp.sqrt(v_hat) + ADAM_EPS) + ADAM_WD * w_ref[...])

    row = pl.BlockSpec((tr, C), lambda i: (i, 0))
    return pl.pallas_call(
        body, name=name, grid=(R // tr,),
        in_specs=[pl.BlockSpec((n, tr, C), lambda i: (0, i, 0)), row, row, row], out_specs=[row] * 4,
        out_shape=[jax.ShapeDtypeStruct((R, C), F32)] * 4,
        compiler_params=_params(("parallel",)),
    )(parts, w, m, v)


SHARDED = (("w_in", D, IN_W, 1), ("w_branch_a", CONV_W, D, 1), ("w_branch_b", ATTN_W, D, 1), ("w_out", D, D, 0),
           ("w_up", D, 2 * D_FF, 1), ("w_down", D_FF, D, 0), ("conv_a_w", 3, CONV_W, 1),
           ("conv_ffn_w", 3, 2 * D_FF, 1))
MATRICES = SHARDED[:6]
CONVS = SHARDED[6:]
REPLICATED = (("b_ada", N_MOD * D), ("norm1_g", D), ("norm2_g", D), ("b_f", N_HEADS), ("q_norm_g", HEAD_DIM),
              ("k_norm_g", HEAD_DIM))


def _shard_shape(rows, cols, axis):
    return (rows // N_DEV, cols) if axis == 0 else (rows, cols // N_DEV)


def _pack_rows(flat, multiple):
    length = flat.shape[-1]
    rows = -(-length // PACK_W)
    rows = -(-rows // multiple) * multiple
    pad = [(0, 0)] * (flat.ndim - 1) + [(0, rows * PACK_W - length)]
    return jnp.pad(flat, pad).reshape(flat.shape[:-1] + (rows, PACK_W))


def _pack_shards(shards, spec, multiple, dtype):
    flat = jnp.concatenate([shards[name].reshape(-1).astype(dtype) for name, *_ in spec])
    return _pack_rows(flat, multiple)


def _join_shards(gathered, axis):
    if axis == 0:
        return gathered.reshape(N_DEV * gathered.shape[1], gathered.shape[2])
    return jnp.concatenate([gathered[j] for j in range(N_DEV)], axis=1)


def _split_shards(full, axis):
    if axis == 0:
        return full.reshape(N_DEV, full.shape[0] // N_DEV, full.shape[1])
    c = full.shape[1] // N_DEV
    return jnp.stack([full[:, j * c:(j + 1) * c] for j in range(N_DEV)])


def _unpack_shards(packed, spec):
    flat = packed.reshape(-1)
    out, off = {}, 0
    for name, rows, cols, axis in spec:
        r, c = _shard_shape(rows, cols, axis)
        out[name] = flat[off:off + r * c].reshape(r, c)
        off += r * c
    return out


def _unpack_gathered(gathered, spec):
    flat = gathered.reshape(N_DEV, -1)
    out, off = {}, 0
    for name, rows, cols, axis in spec:
        r, c = _shard_shape(rows, cols, axis)
        seg = flat[:, off:off + r * c].reshape(N_DEV, r, c)
        out[name] = seg.reshape(rows, cols) if axis == 0 else seg.transpose(1, 0, 2).reshape(rows, cols)
        off += r * c
    return out


def _pack_full_by_dest(full, spec, multiple):
    segs = []
    for name, rows, cols, axis in spec:
        r, c = _shard_shape(rows, cols, axis)
        a = full[name]
        seg = a.reshape(N_DEV, r, c) if axis == 0 else a.reshape(rows, N_DEV, c).transpose(1, 0, 2)
        segs.append(seg.reshape(N_DEV, r * c))
    return _pack_rows(jnp.concatenate(segs, axis=1), multiple)


def _pad_in(w_in):
    return jnp.concatenate([w_in[:, :COL_GA], w_in[:, COL_GA + N_HEADS:], w_in[:, COL_GA:COL_GA + N_HEADS],
                            jnp.zeros((w_in.shape[0], IN_W_PAD - IN_W), w_in.dtype)], axis=1)


def _pad_head_rows(w):
    n = w.shape[1]
    padded = jnp.pad(w.reshape(N_HEADS, HEAD_DIM, n), ((0, 0), (0, LANES - HEAD_DIM), (0, 0)))
    return padded.reshape(N_HEADS * LANES, n)


def _unpad_in(g):
    return jnp.concatenate([g[:, :COL_GA], g[:, COL_F:COL_F + N_HEADS], g[:, COL_GA:COL_F]], axis=1)


def kernel(x, c, w_ada, b_ada, norm1_g, w_in, b_f, conv_a_w, q_norm_g, k_norm_g, w_branch_a, w_branch_b, w_out, norm2_g, w_up, conv_ffn_w, w_down, loss_target, m_w_ada, m_b_ada, m_norm1_g, m_w_in, m_b_f, m_conv_a_w, m_q_norm_g, m_k_norm_g, m_w_branch_a, m_w_branch_b, m_w_out, m_norm2_g, m_w_up, m_conv_ffn_w, m_w_down, v_w_ada, v_b_ada, v_norm1_g, v_w_in, v_b_f, v_conv_a_w, v_q_norm_g, v_k_norm_g, v_w_branch_a, v_w_branch_b, v_w_out, v_norm2_g, v_w_up, v_conv_ffn_w, v_w_down):
    names = ("w_ada", "b_ada", "norm1_g", "w_in", "b_f", "conv_a_w", "q_norm_g", "k_norm_g", "w_branch_a",
             "w_branch_b", "w_out", "norm2_g", "w_up", "conv_ffn_w", "w_down")
    squeeze = lambda a: a[0] if a.ndim == 3 else a
    W = dict(zip(names, map(squeeze, (w_ada, b_ada, norm1_g, w_in, b_f, conv_a_w, q_norm_g, k_norm_g, w_branch_a,
                                      w_branch_b, w_out, norm2_g, w_up, conv_ffn_w, w_down))))
    M = dict(zip(names, map(squeeze, (m_w_ada, m_b_ada, m_norm1_g, m_w_in, m_b_f, m_conv_a_w, m_q_norm_g,
                                      m_k_norm_g, m_w_branch_a, m_w_branch_b, m_w_out, m_norm2_g, m_w_up,
                                      m_conv_ffn_w, m_w_down))))
    V = dict(zip(names, map(squeeze, (v_w_ada, v_b_ada, v_norm1_g, v_w_in, v_b_f, v_conv_a_w, v_q_norm_g,
                                      v_k_norm_g, v_w_branch_a, v_w_branch_b, v_w_out, v_norm2_g, v_w_up,
                                      v_conv_ffn_w, v_w_down))))
    me = 4 * lax.axis_index("x") + 2 * lax.axis_index("y") + lax.axis_index("c")
    ada_n = N_MOD * D // N_DEV

    small = jnp.concatenate([c.reshape(-1), W["conv_a_w"].reshape(-1), W["conv_ffn_w"].reshape(-1)])
    gathered = _exchange([_pack_rows(small, SUBLANES)] + [W[name].astype(BF16) for name, *_ in MATRICES],
                         name="gather_weights", scatter=False)
    small_all = gathered[0].reshape(N_DEV, -1)
    c_all = small_all[:, :D]
    conv_all = _unpack_gathered(small_all[:, D:], CONVS)
    mats = {name: _join_shards(g, axis) for (name, _, _, axis), g in zip(MATRICES, gathered[1:])}

    b_shard = lax.dynamic_slice(W["b_ada"], (0, me * ada_n), (1, ada_n))
    mod_part = _ada_fwd(c_all, W["w_ada"], b_shard)
    mod_all, = _exchange([mod_part], name="gather_mod", scatter=False)
    mod = lax.dynamic_index_in_dim(mod_all, me, axis=1, keepdims=False).reshape(1, N_MOD * D)

    wts = {"w_in": _pad_in(mats["w_in"])}
    wts["w_in_t"] = wts["w_in"].T
    for name in ("w_branch_a", "w_out", "w_up", "w_down"):
        wts[name] = mats[name]
        wts[name + "_t"] = mats[name].T
    wts["w_branch_b_heads"] = _pad_head_rows(mats["w_branch_b"])
    wts["w_branch_b_heads_t"] = wts["w_branch_b_heads"].T
    wts.update(conv_all)
    for name in ("norm1_g", "norm2_g", "q_norm_g", "k_norm_g", "b_f"):
        wts[name] = W[name]

    sq, grad_x, grads = _local_step(x[0], loss_target[0], mod, wts)
    loss = lax.psum(sq[0, 0] * (0.5 / D), AXES)

    grads["b_ada"] = grads["mod"]
    rep_flat = lambda src: jnp.concatenate([src[name].reshape(-1) for name, _ in REPLICATED])
    rep_parts, = _exchange([_pack_rows(rep_flat(grads), 16)], name="gather_small_grads", scatter=False)
    rep_out = _adamw(rep_parts, *[_pack_rows(rep_flat(s), 16) for s in (W, M, V)], name="adamw_replicated")

    dmod_all = rep_parts.reshape(N_DEV, -1)[:, :N_MOD * D]
    dmod_mine = lax.dynamic_slice(dmod_all, (0, me * ada_n), (N_DEV, ada_n))
    g_ada = _ada_bwd(jnp.pad(c_all.T, ((0, 0), (0, LANES - N_DEV))),
                     jnp.pad(dmod_mine, ((0, LANES - N_DEV), (0, 0))))
    ada_out = _adamw(g_ada[None], W["w_ada"], M["w_ada"], V["w_ada"], name="adamw_ada")

    grads["w_in"] = _unpad_in(grads["w_in_pad"])
    parts = _exchange([_split_shards(grads[name], axis).astype(BF16) for name, _, _, axis in MATRICES]
                      + [_pack_full_by_dest(grads, CONVS, SUBLANES)], name="scatter_grads", scatter=True)
    mat_out = {name: _adamw(p, W[name], M[name], V[name], name="adamw_" + name)
               for (name, *_), p in zip(MATRICES, parts)}
    conv_out = _adamw(parts[-1], *[_pack_shards(s, CONVS, SUBLANES, F32) for s in (W, M, V)], name="adamw_conv")

    results = []
    for kind in range(4):
        per = {"w_ada": ada_out[kind]}
        per.update({name: out[kind] for name, out in mat_out.items()})
        per.update(_unpack_shards(conv_out[kind], CONVS))
        flat, off = rep_out[kind].reshape(-1), 0
        for name, n in REPLICATED:
            per[name] = flat[off:off + n].reshape(1, n)
            off += n
        results.append(per)
    restore = lambda name, a: a[None] if W[name].ndim == 2 and name not in dict(REPLICATED) else a
    outs = [loss, grad_x[None]]
    for per in results:
        outs.extend(restore(name, per[name]) for name in names)
    return tuple(outs)
```

```python
import functools

import jax
import jax.numpy as jnp
import numpy as np
from jax import lax
from jax.experimental import pallas as pl
from jax.experimental.pallas import tpu as pltpu

F32 = jnp.float32
BF16 = jnp.bfloat16

N_DEV = 8
D = 1024
N_HEADS = 8
HEAD_DIM = 64
ATTN_W = 512
CONV_W = 512
D_FF = 2816
N_MOD = 6
IN_W = 5128
RMS_EPS = 1e-6
NEG_INF = -1e30

IN_W_PAD = 5376
COL_GA = 3072
COL_GB = 4096
COL_F = 5120
F_PAD = 128

ADAM_LR = 0.001
ADAM_B1 = 0.9
ADAM_B2 = 0.999
ADAM_EPS = 1e-08
ADAM_WD = 0.01
ADAM_STEP = 10

LANES = 128
SUBLANES = 8
VMEM_LIMIT = 52 * 1024 * 1024
TOKEN_TILE = 512
ATTN_BLOCK = 512
PACK_W = 1024

MESH = pl.DeviceIdType.MESH
AXES = ("x", "y", "c")


def _params(sem=None, **kw):
    return pltpu.CompilerParams(dimension_semantics=sem, vmem_limit_bytes=VMEM_LIMIT, **kw)


def _full(shape):
    nd = len(shape)
    return pl.BlockSpec(shape, lambda *_: (0,) * nd)


def _tn_dot(a, b):
    return lax.dot_general(a, b, (((0,), (0,)), ((), ())), preferred_element_type=F32)


def _matmul(a, b, *, name, tm, tn, tk, out_dtype=F32, trans_a=False):
    if trans_a:
        K, M = a.shape
    else:
        M, K = a.shape
    N = b.shape[1]
    assert b.shape[0] == K and M % tm == 0 and N % tn == 0 and K % tk == 0, (name, a.shape, b.shape)
    nm, nn, nk = M // tm, N // tn, K // tk

    def body(a_ref, b_ref, o_ref, acc_ref):
        k = pl.program_id(2)

        @pl.when(k == 0)
        def _():
            acc_ref[...] = jnp.zeros_like(acc_ref)

        av = a_ref[...].astype(BF16)
        bv = b_ref[...].astype(BF16)
        if trans_a:
            acc_ref[...] += _tn_dot(av, bv)
        else:
            acc_ref[...] += jnp.dot(av, bv, preferred_element_type=F32)

        @pl.when(k == nk - 1)
        def _():
            o_ref[...] = acc_ref[...].astype(out_dtype)

    if trans_a:
        a_spec = pl.BlockSpec((tk, tm), lambda j, i, k: (k, i))
    else:
        a_spec = pl.BlockSpec((tm, tk), lambda j, i, k: (i, k))
    return pl.pallas_call(
        body, name=name, grid=(nn, nm, nk),
        in_specs=[a_spec, pl.BlockSpec((tk, tn), lambda j, i, k: (k, j))],
        out_specs=pl.BlockSpec((tm, tn), lambda j, i, k: (i, j)),
        out_shape=jax.ShapeDtypeStruct((M, N), out_dtype),
        scratch_shapes=[pltpu.VMEM((tm, tn), F32)],
        compiler_params=_params(("parallel", "parallel", "arbitrary")),
    )(a, b)


def _split_dot(x, mat, parts):
    out = None
    rem = x
    for p in range(parts):
        piece = rem.astype(BF16)
        term = jnp.dot(piece, mat, preferred_element_type=F32)
        out = term if out is None else out + term
        if p + 1 < parts:
            rem = rem - piece.astype(F32)
    return out


def _sigmoid(x):
    return 1.0 / (1.0 + jnp.exp(-x))


def _rows8(x):
    r, c = x.shape
    return jnp.sum(x.reshape(r // SUBLANES, SUBLANES, c), axis=0)


def _shift_down(blk, prev8, n):
    rolled = pltpu.roll(blk, n, axis=0)
    prev_rolled = pltpu.roll(prev8, n, axis=0)
    rows = lax.broadcasted_iota(jnp.int32, prev8.shape, 0)
    first = jnp.where(rows < n, prev_rolled, rolled[0:SUBLANES])
    return jnp.concatenate([first, rolled[SUBLANES:]], axis=0)


def _shift_up(blk, next8, n):
    r = blk.shape[0]
    rolled = pltpu.roll(blk, r - n, axis=0)
    next_rolled = pltpu.roll(next8, SUBLANES - n, axis=0)
    rows = lax.broadcasted_iota(jnp.int32, next8.shape, 0)
    last = jnp.where(rows >= SUBLANES - n, next_rolled, rolled[r - SUBLANES:])
    return jnp.concatenate([rolled[:r - SUBLANES], last], axis=0)


def _prev_spec(tm, width, col):
    per = tm // SUBLANES
    return pl.BlockSpec((SUBLANES, width), lambda i, *_: (jnp.maximum(i * per - 1, 0), col))


def _next_spec(tm, width, col, n_tiles):
    per = tm // SUBLANES
    last = n_tiles * per - 1
    return pl.BlockSpec((SUBLANES, width), lambda i, *_: (jnp.minimum((i + 1) * per, last), col))


def _group_matrix():
    idx = np.arange(ATTN_W) // HEAD_DIM
    return jnp.asarray((idx[:, None] == idx[None, :]).astype(np.float32), BF16)


def _norm_mod(x, g, sc, sh, *, name):
    T = x.shape[0]
    tm = min(TOKEN_TILE, T)

    def body(x_ref, g_ref, sc_ref, sh_ref, o_ref):
        xv = x_ref[...]
        inv = lax.rsqrt(jnp.mean(xv * xv, axis=-1, keepdims=True) + RMS_EPS)
        o_ref[...] = ((xv * inv) * g_ref[...] * (1.0 + sc_ref[...]) + sh_ref[...]).astype(BF16)

    row = pl.BlockSpec((tm, D), lambda i: (i, 0))
    return pl.pallas_call(
        body, name=name, grid=(T // tm,),
        in_specs=[row, _full((1, D)), _full((1, D)), _full((1, D))],
        out_specs=row, out_shape=jax.ShapeDtypeStruct((T, D), BF16),
        compiler_params=_params(("parallel",)),
    )(x, g, sc, sh)


LANE_ONE = 64
LANE_F = 67
LANE_LSE = 70
LANE_SUM = 73


def _pieces(x):
    hi = x.astype(BF16).astype(F32)
    rest = x - hi
    mid = rest.astype(BF16).astype(F32)
    return hi, mid, rest - mid


def _aug(lane, data, entries):
    out = jnp.where(lane < HEAD_DIM, data, 0.0)
    for idx, val in entries:
        out = jnp.where(lane == idx, val, out)
    return out


def _run(start, vals):
    return [(start + i, v) for i, v in enumerate(vals)]


def _head_lanes(a, h):
    blk = a[:, LANES * (h // 2):LANES * (h // 2) + LANES]
    return blk if h % 2 == 0 else pltpu.roll(blk, HEAD_DIM, axis=1)


def _branch_prep(proj, fcum, conv_w8, qg, kg, gmat):
    T = proj.shape[0]
    tm = min(TOKEN_TILE, T)
    nt = T // tm

    def body(cb_ref, cc_ref, cv_ref, q_ref, k_ref, v_ref, f_ref, ccp_ref, cvp_ref, w_ref, qg_ref, kg_ref, g_ref,
             ya_ref, qa_ref, ka_ref, va_ref):
        i = pl.program_id(0)
        z = cc_ref[...] * cv_ref[...]
        zp = jnp.where(i > 0, ccp_ref[...] * cvp_ref[...], 0.0)
        w = w_ref[...]
        cz = _shift_down(z, zp, 2) * w[0:1] + _shift_down(z, zp, 1) * w[1:2] + z * w[2:3]
        ya_ref[...] = (cb_ref[...] * cz).astype(BF16)
        gm = g_ref[...]

        def normed(src, gain, scale):
            v = src[...]
            ms = _split_dot(v * v, gm, 2) * (1.0 / HEAD_DIM)
            return (v * lax.rsqrt(ms + RMS_EPS)) * gain[...] * scale

        qn = normed(q_ref, qg_ref, 1.0 / np.sqrt(HEAD_DIM))
        kn = normed(k_ref, kg_ref, 1.0)
        vv = v_ref[...]
        fall = f_ref[...]
        lane = lax.broadcasted_iota(jnp.int32, (tm, LANES), 1)
        ones3 = [1.0, 1.0, 1.0]
        for h in range(N_HEADS):
            hi, mid, lo = _pieces(fall[:, h:h + 1])
            qa_ref[h] = _aug(lane, _head_lanes(qn, h), _run(LANE_ONE, ones3) + _run(LANE_F, [hi, mid, lo])
                             + [(LANE_SUM, 1.0)]).astype(BF16)
            ka_ref[h] = _aug(lane, _head_lanes(kn, h), _run(LANE_ONE, [-hi, -mid, -lo]) + _run(LANE_F, ones3)
                             + _run(LANE_LSE, ones3)).astype(BF16)
            va_ref[h] = _aug(lane, _head_lanes(vv, h), _run(LANE_ONE, ones3)).astype(BF16)

    blk = lambda col: pl.BlockSpec((tm, CONV_W), lambda i: (i, col))
    heads = pl.BlockSpec((N_HEADS, tm, LANES), lambda i: (0, i, 0))
    return pl.pallas_call(
        body, name="branch_prep", grid=(nt,),
        in_specs=[blk(0), blk(1), blk(2), blk(3), blk(4), blk(5), pl.BlockSpec((tm, F_PAD), lambda i: (i, 0)),
                  _prev_spec(tm, CONV_W, 1), _prev_spec(tm, CONV_W, 2),
                  _full((SUBLANES, CONV_W)), _full((1, ATTN_W)), _full((1, ATTN_W)), _full((ATTN_W, ATTN_W))],
        out_specs=[pl.BlockSpec((tm, CONV_W), lambda i: (i, 0)), heads, heads, heads],
        out_shape=[jax.ShapeDtypeStruct((T, CONV_W), BF16)] + [jax.ShapeDtypeStruct((N_HEADS, T, LANES), BF16)] * 3,
        compiler_params=_params(("parallel",)),
    )(proj, proj, proj, proj, proj, proj, fcum, proj, proj, conv_w8, qg, kg, gmat)


def _cumsum(x, *, reverse, name, col=0, gate_bias=None):
    T = x.shape[0]
    tm = min(TOKEN_TILE, T)
    nt = T // tm

    def body(x_ref, b_ref, o_ref, carry_ref):
        i = pl.program_id(0)

        @pl.when(i == 0)
        def _():
            carry_ref[...] = jnp.zeros_like(carry_ref)

        r = lax.broadcasted_iota(jnp.int32, (tm, tm), 0)
        c = lax.broadcasted_iota(jnp.int32, (tm, tm), 1)
        tri = jnp.where((c >= r) if reverse else (c <= r), 1.0, 0.0).astype(BF16)
        xv = x_ref[...]
        if gate_bias is not None:
            fx = xv + b_ref[...]
            xv = jnp.minimum(fx, 0.0) - jnp.log(1.0 + jnp.exp(-jnp.abs(fx)))
        out = _split_dot_left(tri, xv, 3) + carry_ref[0:1]
        o_ref[...] = out
        carry_ref[...] = jnp.broadcast_to(out[0:1] if reverse else out[tm - 1:tm], carry_ref.shape)

    rows = (lambda i: nt - 1 - i) if reverse else (lambda i: i)
    bias = jnp.zeros((1, F_PAD), F32) if gate_bias is None else gate_bias
    return pl.pallas_call(
        body, name=name, grid=(nt,),
        in_specs=[pl.BlockSpec((tm, F_PAD), lambda i: (rows(i), col)), _full((1, F_PAD))],
        out_specs=pl.BlockSpec((tm, F_PAD), lambda i: (rows(i), 0)),
        out_shape=jax.ShapeDtypeStruct((T, F_PAD), F32),
        scratch_shapes=[pltpu.VMEM((SUBLANES, F_PAD), F32)],
        compiler_params=_params(("arbitrary",)),
    )(x, bias)


def _split_dot_left(mat, x, parts):
    out = None
    rem = x
    for p in range(parts):
        piece = rem.astype(BF16)
        term = jnp.dot(mat, piece, preferred_element_type=F32)
        out = term if out is None else out + term
        if p + 1 < parts:
            rem = rem - piece.astype(F32)
    return out


def _merge_fwd(ya, yb, proj):
    T = ya.shape[0]
    tm = min(TOKEN_TILE, T)

    def body(ya_ref, yb_ref, ga_ref, gb_ref, o_ref):
        o_ref[...] = (_sigmoid(ga_ref[...]) * ya_ref[...] + _sigmoid(gb_ref[...]) * yb_ref[...]).astype(BF16)

    row = pl.BlockSpec((tm, D), lambda i: (i, 0))
    return pl.pallas_call(
        body, name="merge_fwd", grid=(T // tm,),
        in_specs=[row, row, pl.BlockSpec((tm, D), lambda i: (i, COL_GA // D)),
                  pl.BlockSpec((tm, D), lambda i: (i, COL_GB // D))],
        out_specs=row, out_shape=jax.ShapeDtypeStruct((T, D), BF16),
        compiler_params=_params(("parallel",)),
    )(ya, yb, proj, proj)


def _resid_norm(x, mix, g1, g, sc, sh):
    T = x.shape[0]
    tm = min(TOKEN_TILE, T)

    def body(x_ref, mix_ref, g1_ref, g_ref, sc_ref, sh_ref, x1_ref, h_ref):
        x1 = x_ref[...] + g1_ref[...] * mix_ref[...]
        x1_ref[...] = x1
        inv = lax.rsqrt(jnp.mean(x1 * x1, axis=-1, keepdims=True) + RMS_EPS)
        h_ref[...] = ((x1 * inv) * g_ref[...] * (1.0 + sc_ref[...]) + sh_ref[...]).astype(BF16)

    row = pl.BlockSpec((tm, D), lambda i: (i, 0))
    vec = _full((1, D))
    return pl.pallas_call(
        body, name="resid_norm", grid=(T // tm,),
        in_specs=[row, row, vec, vec, vec, vec], out_specs=[row, row],
        out_shape=[jax.ShapeDtypeStruct((T, D), F32), jax.ShapeDtypeStruct((T, D), BF16)],
        compiler_params=_params(("parallel",)),
    )(x, mix, g1, g, sc, sh)


FFN_TM = 256
FFN_TC = 1408


def _ffn_act_fwd(u, w8):
    T = u.shape[0]
    tm = min(FFN_TM, T)
    nt = T // tm
    nc = D_FF // FFN_TC

    def body(ug_ref, uv_ref, ugp_ref, uvp_ref, wg_ref, wv_ref, o_ref):
        i = pl.program_id(1)

        def conv(u_ref, p_ref, w_ref):
            uv = u_ref[...]
            up = jnp.where(i > 0, p_ref[...], 0.0)
            w = w_ref[...]
            return _shift_down(uv, up, 2) * w[0:1] + _shift_down(uv, up, 1) * w[1:2] + uv * w[2:3]

        gate = conv(ug_ref, ugp_ref, wg_ref)
        val = conv(uv_ref, uvp_ref, wv_ref)
        o_ref[...] = (gate * _sigmoid(gate) * val).astype(BF16)

    per = tm // SUBLANES
    blk = lambda off: pl.BlockSpec((tm, FFN_TC), lambda j, i: (i, j + off))
    prev = lambda off: pl.BlockSpec((SUBLANES, FFN_TC), lambda j, i: (jnp.maximum(i * per - 1, 0), j + off))
    wblk = lambda off: pl.BlockSpec((SUBLANES, FFN_TC), lambda j, i: (0, j + off))
    return pl.pallas_call(
        body, name="ffn_act_fwd", grid=(nc, nt),
        in_specs=[blk(0), blk(nc), prev(0), prev(nc), wblk(0), wblk(nc)],
        out_specs=pl.BlockSpec((tm, FFN_TC), lambda j, i: (i, j)),
        out_shape=jax.ShapeDtypeStruct((T, D_FF), BF16),
        compiler_params=_params(("parallel", "parallel")),
    )(u, u, u, u, w8, w8)


def _loss_head(x1, ff, g2, target):
    T = x1.shape[0]
    tm = min(TOKEN_TILE, T)

    def body(x1_ref, ff_ref, g2_ref, t_ref, dy_ref, dff_ref, loss_ref, dg2_ref):
        i = pl.program_id(0)

        @pl.when(i == 0)
        def _():
            loss_ref[...] = jnp.zeros_like(loss_ref)
            dg2_ref[...] = jnp.zeros_like(dg2_ref)

        ff = ff_ref[...]
        err = x1_ref[...] + g2_ref[...] * ff - t_ref[...]
        dy = err * (1.0 / D)
        dy_ref[...] = dy
        dff_ref[...] = (dy * g2_ref[...]).astype(BF16)
        loss_ref[...] += _rows8(err * err)
        dg2_ref[...] += _rows8(dy * ff)

    row = pl.BlockSpec((tm, D), lambda i: (i, 0))
    acc = _full((SUBLANES, D))
    return pl.pallas_call(
        body, name="loss_head", grid=(T // tm,),
        in_specs=[row, row, _full((1, D)), row], out_specs=[row, row, acc, acc],
        out_shape=[jax.ShapeDtypeStruct((T, D), F32), jax.ShapeDtypeStruct((T, D), BF16),
                   jax.ShapeDtypeStruct((SUBLANES, D), F32), jax.ShapeDtypeStruct((SUBLANES, D), F32)],
        compiler_params=_params(("arbitrary",)),
    )(x1, ff, g2, target)


def _nt_dot(a, b):
    return lax.dot_general(a, b, (((1,), (1,)), ((), ())), preferred_element_type=F32)


def _causal(n, keys_on_rows=False):
    r = lax.broadcasted_iota(jnp.int32, (n, n), 0)
    c = lax.broadcasted_iota(jnp.int32, (n, n), 1)
    return (c >= r) if keys_on_rows else (c <= r)


def _sweep(lo, hi, step, carry):
    pairs = (hi - lo) // 2
    carry = lax.fori_loop(0, pairs, lambda j, cr: step(lo + 2 * j + 1, step(lo + 2 * j, cr)), carry)
    return lax.fori_loop(lo + 2 * pairs, hi, step, carry)


def _attn_fwd(qa, ka, va):
    nh, T, _ = qa.shape
    bq = min(ATTN_BLOCK, T)
    nq = T // bq

    def body(q_ref, k_ref, v_ref, o_ref, qb_ref):
        qi = pl.program_id(1)
        q = q_ref[0]

        def step(kb, carry, masked):
            m, acc = carry
            start = pl.multiple_of(kb * bq, bq)
            s = _nt_dot(q, k_ref[0, pl.ds(start, bq), :])
            if masked:
                s = jnp.where(_causal(bq), s, NEG_INF)
            m_new = jnp.maximum(m, jnp.max(s, axis=-1, keepdims=True))
            p = jnp.exp(s - m_new).astype(BF16)
            acc = jnp.exp(m - m_new) * acc + jnp.dot(p, v_ref[0, pl.ds(start, bq), :], preferred_element_type=F32)
            return m_new, acc

        init = (jnp.full((bq, 1), NEG_INF, F32), jnp.zeros((bq, LANES), F32))
        carry = _sweep(0, qi, lambda kb, cr: step(kb, cr, False), init)
        m, acc = step(qi, carry, True)
        l = acc[:, LANE_ONE:LANE_ONE + 1]
        o_ref[0] = acc / l
        lse = m + jnp.log(l)
        lane = lax.broadcasted_iota(jnp.int32, (bq, LANES), 1)
        qf = q.astype(F32)
        for idx, piece in _run(LANE_LSE, _pieces(lse)):
            qf = jnp.where(lane == idx, -piece, qf)
        qb_ref[0] = qf.astype(BF16)

    tile = pl.BlockSpec((1, bq, LANES), lambda h, i: (h, i, 0))
    whole = pl.BlockSpec((1, T, LANES), lambda h, i: (h, 0, 0))
    return pl.pallas_call(
        body, name="attn_fwd", grid=(nh, nq),
        in_specs=[tile, whole, whole], out_specs=[tile, tile],
        out_shape=[jax.ShapeDtypeStruct((nh, T, LANES), F32), jax.ShapeDtypeStruct((nh, T, LANES), BF16)],
        compiler_params=_params(("parallel", "parallel")),
    )(qa, ka, va)


def _branch_merge_fwd(ya0, o_h, proj, wba, wbb_heads):
    nh, T, _ = o_h.shape
    tm = min(TOKEN_TILE, T)

    def body(ya0_ref, o_ref, ga_ref, gb_ref, wa_ref, wb_ref, ya_ref, yb_ref, m_ref):
        ya = jnp.dot(ya0_ref[...], wa_ref[...], preferred_element_type=F32)
        yb = jnp.dot(o_ref[0].astype(BF16), wb_ref[0:LANES, :], preferred_element_type=F32)
        for h in range(1, nh):
            yb += jnp.dot(o_ref[h].astype(BF16), wb_ref[h * LANES:(h + 1) * LANES, :], preferred_element_type=F32)
        ya_ref[...] = ya
        yb_ref[...] = yb
        m_ref[...] = (_sigmoid(ga_ref[...]) * ya + _sigmoid(gb_ref[...]) * yb).astype(BF16)

    row = pl.BlockSpec((tm, D), lambda i: (i, 0))
    return pl.pallas_call(
        body, name="branch_merge_fwd", grid=(T // tm,),
        in_specs=[pl.BlockSpec((tm, CONV_W), lambda i: (i, 0)), pl.BlockSpec((nh, tm, LANES), lambda i: (0, i, 0)),
                  pl.BlockSpec((tm, D), lambda i: (i, COL_GA // D)), pl.BlockSpec((tm, D), lambda i: (i, COL_GB // D)),
                  _full((CONV_W, D)), _full((nh * LANES, D))],
        out_specs=[row, row, row],
        out_shape=[jax.ShapeDtypeStruct((T, D), F32), jax.ShapeDtypeStruct((T, D), F32),
                   jax.ShapeDtypeStruct((T, D), BF16)],
        compiler_params=_params(("parallel",)),
    )(ya0, o_h, proj, proj, wba, wbb_heads)


def _branch_b_bwd(dyb, o_h, wbb_heads_t):
    nh, T, _ = o_h.shape
    tm = min(TOKEN_TILE, T)

    def body(dyb_ref, o_ref, w_ref, out_ref):
        do = jnp.dot(dyb_ref[...], w_ref[...], preferred_element_type=F32)
        lane = lax.broadcasted_iota(jnp.int32, (tm, LANES), 1)
        for h in range(nh):
            g = do[:, h * LANES:(h + 1) * LANES].astype(BF16).astype(F32)
            delta = jnp.sum(g * o_ref[h], axis=-1, keepdims=True)
            for idx, piece in _run(LANE_ONE, _pieces(delta)):
                g = jnp.where(lane == idx, -piece, g)
            out_ref[h] = g.astype(BF16)

    heads = pl.BlockSpec((nh, tm, LANES), lambda i: (0, i, 0))
    return pl.pallas_call(
        body, name="branch_b_bwd", grid=(T // tm,),
        in_specs=[pl.BlockSpec((tm, D), lambda i: (i, 0)), heads, _full((D, nh * LANES))],
        out_specs=heads, out_shape=jax.ShapeDtypeStruct((nh, T, LANES), BF16),
        compiler_params=_params(("parallel",)),
    )(dyb, o_h, wbb_heads_t)


def _branch_b_dw(o_h, dyb):
    nh, T, _ = o_h.shape
    tk = min(TOKEN_TILE, T)

    def body(o_ref, dyb_ref, out_ref):
        @pl.when(pl.program_id(0) == 0)
        def _():
            out_ref[...] = jnp.zeros_like(out_ref)

        g = dyb_ref[...]
        for h in range(nh):
            out_ref[h] += _tn_dot(o_ref[h].astype(BF16), g)

    return pl.pallas_call(
        body, name="branch_b_dw", grid=(T // tk,),
        in_specs=[pl.BlockSpec((nh, tk, LANES), lambda k: (0, k, 0)), pl.BlockSpec((tk, D), lambda k: (k, 0))],
        out_specs=_full((nh, LANES, D)), out_shape=jax.ShapeDtypeStruct((nh, LANES, D), F32),
        compiler_params=_params(("arbitrary",)),
    )(o_h, dyb)


def _attn_bwd(qb, ka, va, doa):
    nh, T, _ = qb.shape
    bk = min(ATTN_BLOCK, T)
    nk = T // bk

    def body(q_ref, do_ref, k_ref, v_ref, dq_ref, dk_ref, dv_ref):
        ki = pl.program_id(1)

        @pl.when(ki == 0)
        def _():
            dq_ref[...] = jnp.zeros_like(dq_ref)

        k = k_ref[0]
        v = v_ref[0]

        def step(qi, carry, masked):
            dk, dv = carry
            rows = pl.ds(pl.multiple_of(qi * bk, bk), bk)
            q = q_ref[0, rows, :]
            g = do_ref[0, rows, :]
            pt = jnp.exp(_nt_dot(k, q))
            if masked:
                pt = jnp.where(_causal(bk, keys_on_rows=True), pt, 0.0)
            dv = dv + jnp.dot(pt.astype(BF16), g, preferred_element_type=F32)
            dst = (pt * _nt_dot(v, g)).astype(BF16)
            dk = dk + jnp.dot(dst, q, preferred_element_type=F32)
            dq_ref[0, rows, :] += _tn_dot(dst, k)
            return dk, dv

        init = (jnp.zeros((bk, LANES), F32), jnp.zeros((bk, LANES), F32))
        carry = step(ki, init, True)
        dk_ref[0], dv_ref[0] = _sweep(ki + 1, nk, lambda qi, cr: step(qi, cr, False), carry)

    tile = pl.BlockSpec((1, bk, LANES), lambda h, i: (h, i, 0))
    whole = pl.BlockSpec((1, T, LANES), lambda h, i: (h, 0, 0))
    return pl.pallas_call(
        body, name="attn_bwd", grid=(nh, nk),
        in_specs=[whole, whole, tile, tile], out_specs=[whole, tile, tile],
        out_shape=[jax.ShapeDtypeStruct((nh, T, LANES), F32)] * 3,
        compiler_params=_params(("parallel", "arbitrary")),
    )(qb, doa, ka, va)


def _attn_unpack(dq_h, dk_h, dv_h):
    nh, T, _ = dq_h.shape
    tm = min(TOKEN_TILE, T)

    def body(dq_ref, dk_ref, dv_ref, q_out, k_out, v_out, f_out):
        lane = lax.broadcasted_iota(jnp.int32, (tm, LANES), 1)
        low = lane < HEAD_DIM
        for src, dst in ((dq_ref, q_out), (dk_ref, k_out), (dv_ref, v_out)):
            for pair in range(nh // 2):
                both = jnp.where(low, src[2 * pair], pltpu.roll(src[2 * pair + 1], HEAD_DIM, axis=1))
                dst[:, LANES * pair:LANES * (pair + 1)] = both.astype(dst.dtype)
        df = jnp.zeros((tm, LANES), F32)
        for h in range(nh):
            col = dq_ref[h][:, LANE_F:LANE_F + 1] - dk_ref[h][:, LANE_SUM:LANE_SUM + 1]
            df = jnp.where(lane == h, col, df)
        f_out[...] = df

    heads = pl.BlockSpec((nh, tm, LANES), lambda i: (0, i, 0))
    tok = pl.BlockSpec((tm, ATTN_W), lambda i: (i, 0))
    return pl.pallas_call(
        body, name="attn_unpack", grid=(T // tm,), in_specs=[heads, heads, heads],
        out_specs=[tok, tok, tok, pl.BlockSpec((tm, F_PAD), lambda i: (i, 0))],
        out_shape=[jax.ShapeDtypeStruct((T, ATTN_W), F32), jax.ShapeDtypeStruct((T, ATTN_W), F32),
                   jax.ShapeDtypeStruct((T, ATTN_W), BF16), jax.ShapeDtypeStruct((T, F_PAD), F32)],
        compiler_params=_params(("parallel",)),
    )(dq_h, dk_h, dv_h)


def _ffn_act_bwd(u, da, w8):
    T = u.shape[0]
    tm = min(FFN_TM, T)
    nt = T // tm
    nc = D_FF // FFN_TC

    def body(ug_ref, uv_ref, ugp_ref, uvp_ref, ugn_ref, uvn_ref, da_ref, dan_ref, wg_ref, wv_ref,
             dug_ref, duv_ref, dwg_ref, dwv_ref):
        i = pl.program_id(1)

        @pl.when(i == 0)
        def _():
            dwg_ref[...] = jnp.zeros_like(dwg_ref)
            dwv_ref[...] = jnp.zeros_like(dwv_ref)

        first, last = i == 0, i == nt - 1
        wg, wv = wg_ref[...], wv_ref[...]
        zeros8 = jnp.zeros((SUBLANES, FFN_TC), F32)

        def window(u_ref, p_ref, n_ref, w):
            e = jnp.concatenate([jnp.where(first, 0.0, p_ref[...]), u_ref[...], n_ref[...]], axis=0)
            e1 = pltpu.roll(e, 1, axis=0)
            e2 = pltpu.roll(e, 2, axis=0)
            return e, e1, e2, e2 * w[0:1] + e1 * w[1:2] + e * w[2:3]

        eg, eg1, eg2, cg = window(ug_ref, ugp_ref, ugn_ref, wg)
        ev, ev1, ev2, cv = window(uv_ref, uvp_ref, uvn_ref, wv)
        dae = jnp.concatenate([zeros8, da_ref[...].astype(F32), jnp.where(last, 0.0, dan_ref[...].astype(F32))],
                              axis=0)
        sg = _sigmoid(cg)
        dgate = dae * cv * sg * (1.0 + cg * (1.0 - sg))
        dval = dae * cg * sg
        n = tm + 2 * SUBLANES

        def back(d, w):
            return d * w[2:3] + pltpu.roll(d, n - 1, axis=0) * w[1:2] + pltpu.roll(d, n - 2, axis=0) * w[0:1]

        inner = slice(SUBLANES, SUBLANES + tm)
        dug_ref[...] = back(dgate, wg)[inner].astype(BF16)
        duv_ref[...] = back(dval, wv)[inner].astype(BF16)

        def wgrad(d, e, e1, e2):
            rows = [jnp.sum((d * t)[inner], axis=0, keepdims=True) for t in (e2, e1, e)]
            return jnp.concatenate(rows + [jnp.zeros((SUBLANES - 3, FFN_TC), F32)], axis=0)

        dwg_ref[...] += wgrad(dgate, eg, eg1, eg2)
        dwv_ref[...] += wgrad(dval, ev, ev1, ev2)

    per = tm // SUBLANES
    last_blk = nt * per - 1
    blk = lambda off: pl.BlockSpec((tm, FFN_TC), lambda j, i: (i, j + off))
    prev = lambda off: pl.BlockSpec((SUBLANES, FFN_TC), lambda j, i: (jnp.maximum(i * per - 1, 0), j + off))
    nxt = lambda off: pl.BlockSpec((SUBLANES, FFN_TC), lambda j, i: (jnp.minimum((i + 1) * per, last_blk), j + off))
    wblk = lambda off: pl.BlockSpec((SUBLANES, FFN_TC), lambda j, i: (0, j + off))
    dug, duv, dwg, dwv = pl.pallas_call(
        body, name="ffn_act_bwd", grid=(nc, nt),
        in_specs=[blk(0), blk(nc), prev(0), prev(nc), nxt(0), nxt(nc), blk(0), nxt(0), wblk(0), wblk(nc)],
        out_specs=[blk(0), blk(0), wblk(0), wblk(0)],
        out_shape=[jax.ShapeDtypeStruct((T, D_FF), BF16)] * 2 + [jax.ShapeDtypeStruct((SUBLANES, D_FF), F32)] * 2,
        compiler_params=_params(("parallel", "arbitrary")),
    )(u, u, u, u, u, u, da, da, w8, w8)
    return jnp.concatenate([dug, duv], axis=1), jnp.concatenate([dwg, dwv], axis=1)


def _norm_bwd(xin, dh, dres, g, sc, *, name):
    T = xin.shape[0]
    tm = min(TOKEN_TILE, T)

    def body(x_ref, dh_ref, dr_ref, g_ref, sc_ref, dx_ref, dsh_ref, dsc_ref, dg_ref):
        i = pl.program_id(0)

        @pl.when(i == 0)
        def _():
            dsh_ref[...] = jnp.zeros_like(dsh_ref)
            dsc_ref[...] = jnp.zeros_like(dsc_ref)
            dg_ref[...] = jnp.zeros_like(dg_ref)

        xv = x_ref[...]
        dh = dh_ref[...]
        gv = g_ref[...]
        one_sc = 1.0 + sc_ref[...]
        inv = lax.rsqrt(jnp.mean(xv * xv, axis=-1, keepdims=True) + RMS_EPS)
        xn = xv * inv
        dxn = dh * (gv * one_sc)
        dx_ref[...] = dr_ref[...] + inv * (dxn - xn * jnp.mean(dxn * xn, axis=-1, keepdims=True))
        dhxn = dh * xn
        dsh_ref[...] += _rows8(dh)
        dsc_ref[...] += _rows8(dhxn * gv)
        dg_ref[...] += _rows8(dhxn * one_sc)

    row = pl.BlockSpec((tm, D), lambda i: (i, 0))
    acc = _full((SUBLANES, D))
    return pl.pallas_call(
        body, name=name, grid=(T // tm,),
        in_specs=[row, row, row, _full((1, D)), _full((1, D))], out_specs=[row, acc, acc, acc],
        out_shape=[jax.ShapeDtypeStruct((T, D), F32)] + [jax.ShapeDtypeStruct((SUBLANES, D), F32)] * 3,
        compiler_params=_params(("arbitrary",)),
    )(xin, dh, dres, g, sc)


def _gate_bwd(dx1, mix, g1):
    T = dx1.shape[0]
    tm = min(TOKEN_TILE, T)

    def body(dx_ref, mix_ref, g1_ref, dmix_ref, dg1_ref):
        @pl.when(pl.program_id(0) == 0)
        def _():
            dg1_ref[...] = jnp.zeros_like(dg1_ref)

        dx = dx_ref[...]
        dmix_ref[...] = (dx * g1_ref[...]).astype(BF16)
        dg1_ref[...] += _rows8(dx * mix_ref[...])

    row = pl.BlockSpec((tm, D), lambda i: (i, 0))
    return pl.pallas_call(
        body, name="gate_bwd", grid=(T // tm,),
        in_specs=[row, row, _full((1, D))], out_specs=[row, _full((SUBLANES, D))],
        out_shape=[jax.ShapeDtypeStruct((T, D), BF16), jax.ShapeDtypeStruct((SUBLANES, D), F32)],
        compiler_params=_params(("arbitrary",)),
    )(dx1, mix, g1)


def _merge_bwd(dmerged, ya, yb, proj):
    T = ya.shape[0]
    tm = min(TOKEN_TILE, T)

    def body(dm_ref, ya_ref, yb_ref, ga_ref, gb_ref, dya_ref, dyb_ref, dga_ref, dgb_ref):
        dm = dm_ref[...]
        sa = _sigmoid(ga_ref[...])
        sb = _sigmoid(gb_ref[...])
        dya_ref[...] = (dm * sa).astype(BF16)
        dyb_ref[...] = (dm * sb).astype(BF16)
        dga_ref[...] = (dm * ya_ref[...] * sa * (1.0 - sa)).astype(BF16)
        dgb_ref[...] = (dm * yb_ref[...] * sb * (1.0 - sb)).astype(BF16)

    row = pl.BlockSpec((tm, D), lambda i: (i, 0))
    return pl.pallas_call(
        body, name="merge_bwd", grid=(T // tm,),
        in_specs=[row, row, row, pl.BlockSpec((tm, D), lambda i: (i, COL_GA // D)),
                  pl.BlockSpec((tm, D), lambda i: (i, COL_GB // D))],
        out_specs=[row] * 4, out_shape=[jax.ShapeDtypeStruct((T, D), BF16)] * 4,
        compiler_params=_params(("parallel",)),
    )(dmerged, ya, yb, proj, proj)


def _conv_branch_bwd(proj, dya0, conv_w8):
    T = proj.shape[0]
    tm = min(FFN_TM, T)
    nt = T // tm

    def body(cb_ref, cc_ref, cv_ref, cbn_ref, ccp_ref, cvp_ref, ccn_ref, cvn_ref, d_ref, dn_ref, w_ref,
             dcb_ref, dcc_ref, dcv_ref, dw_ref):
        i = pl.program_id(0)

        @pl.when(i == 0)
        def _():
            dw_ref[...] = jnp.zeros_like(dw_ref)

        first, last = i == 0, i == nt - 1
        w = w_ref[...]
        cc = jnp.concatenate([ccp_ref[...], cc_ref[...], ccn_ref[...]], axis=0)
        cv = jnp.concatenate([cvp_ref[...], cv_ref[...], cvn_ref[...]], axis=0)
        rows = lax.broadcasted_iota(jnp.int32, cc.shape, 0)
        z = jnp.where(jnp.logical_and(first, rows < SUBLANES), 0.0, cc * cv)
        z1 = pltpu.roll(z, 1, axis=0)
        z2 = pltpu.roll(z, 2, axis=0)
        cz = z2 * w[0:1] + z1 * w[1:2] + z * w[2:3]
        zeros8 = jnp.zeros((SUBLANES, CONV_W), F32)
        de = jnp.concatenate([zeros8, d_ref[...], jnp.where(last, 0.0, dn_ref[...])], axis=0)
        cbe = jnp.concatenate([zeros8, cb_ref[...], cbn_ref[...]], axis=0)
        dcz = de * cbe
        n = tm + 2 * SUBLANES
        dz = dcz * w[2:3] + pltpu.roll(dcz, n - 1, axis=0) * w[1:2] + pltpu.roll(dcz, n - 2, axis=0) * w[0:1]
        inner = slice(SUBLANES, SUBLANES + tm)
        dcb_ref[...] = (de * cz)[inner].astype(BF16)
        dcc_ref[...] = (dz * cv)[inner].astype(BF16)
        dcv_ref[...] = (dz * cc)[inner].astype(BF16)
        wrows = [jnp.sum((dcz * t)[inner], axis=0, keepdims=True) for t in (z2, z1, z)]
        dw_ref[...] += jnp.concatenate(wrows + [jnp.zeros((SUBLANES - 3, CONV_W), F32)], axis=0)

    blk = lambda col: pl.BlockSpec((tm, CONV_W), lambda i: (i, col))
    out_blk = pl.BlockSpec((tm, CONV_W), lambda i: (i, 0))
    return pl.pallas_call(
        body, name="conv_branch_bwd", grid=(nt,),
        in_specs=[blk(0), blk(1), blk(2), _next_spec(tm, CONV_W, 0, nt),
                  _prev_spec(tm, CONV_W, 1), _prev_spec(tm, CONV_W, 2),
                  _next_spec(tm, CONV_W, 1, nt), _next_spec(tm, CONV_W, 2, nt),
                  out_blk, _next_spec(tm, CONV_W, 0, nt), _full((SUBLANES, CONV_W))],
        out_specs=[out_blk, out_blk, out_blk, _full((SUBLANES, CONV_W))],
        out_shape=[jax.ShapeDtypeStruct((T, CONV_W), BF16)] * 3 + [jax.ShapeDtypeStruct((SUBLANES, CONV_W), F32)],
        compiler_params=_params(("arbitrary",)),
    )(proj, proj, proj, proj, proj, proj, proj, proj, dya0, dya0, conv_w8)


def _qk_norm_bwd(proj, dqs, dkh, dlogf, qg, kg, bf_pad, gmat):
    T = proj.shape[0]
    tm = min(TOKEN_TILE, T)

    def body(q_ref, k_ref, f_ref, dqs_ref, dkh_ref, dlf_ref, qg_ref, kg_ref, bf_ref, g_ref,
             dq_ref, dk_ref, dfl_ref, dqg_ref, dkg_ref, dbf_ref):
        @pl.when(pl.program_id(0) == 0)
        def _():
            dqg_ref[...] = jnp.zeros_like(dqg_ref)
            dkg_ref[...] = jnp.zeros_like(dkg_ref)
            dbf_ref[...] = jnp.zeros_like(dbf_ref)

        gm = g_ref[...]
        for src, d_src, gain, scale, dst, dgain in (
                (q_ref, dqs_ref, qg_ref, 1.0 / np.sqrt(HEAD_DIM), dq_ref, dqg_ref),
                (k_ref, dkh_ref, kg_ref, 1.0, dk_ref, dkg_ref)):
            v = src[...]
            dhat = d_src[...] * scale
            inv = lax.rsqrt(_split_dot(v * v, gm, 2) * (1.0 / HEAD_DIM) + RMS_EPS)
            vn = v * inv
            dgain[...] += _rows8(dhat * vn)
            dvn = dhat * gain[...]
            mean = _split_dot(dvn * vn, gm, 2) * (1.0 / HEAD_DIM)
            dst[...] = (inv * (dvn - vn * mean)).astype(BF16)
        fx = f_ref[...] + bf_ref[...]
        dfl = dlf_ref[...] * _sigmoid(-fx)
        dfl_ref[...] = dfl.astype(BF16)
        dbf_ref[...] += _rows8(dfl)

    blk = lambda col: pl.BlockSpec((tm, ATTN_W), lambda i: (i, col))
    out_blk = pl.BlockSpec((tm, ATTN_W), lambda i: (i, 0))
    f_in = pl.BlockSpec((tm, F_PAD), lambda i: (i, COL_F // F_PAD))
    f_blk = pl.BlockSpec((tm, F_PAD), lambda i: (i, 0))
    return pl.pallas_call(
        body, name="qk_norm_bwd", grid=(T // tm,),
        in_specs=[blk(3), blk(4), f_in, out_blk, out_blk, f_blk, _full((1, ATTN_W)), _full((1, ATTN_W)),
                  _full((1, F_PAD)), _full((ATTN_W, ATTN_W))],
        out_specs=[out_blk, out_blk, f_blk, _full((SUBLANES, ATTN_W)), _full((SUBLANES, ATTN_W)),
                   _full((SUBLANES, F_PAD))],
        out_shape=[jax.ShapeDtypeStruct((T, ATTN_W), BF16)] * 2 + [jax.ShapeDtypeStruct((T, F_PAD), BF16)]
        + [jax.ShapeDtypeStruct((SUBLANES, ATTN_W), F32)] * 2 + [jax.ShapeDtypeStruct((SUBLANES, F_PAD), F32)],
        compiler_params=_params(("arbitrary",)),
    )(proj, proj, proj, dqs, dkh, dlogf, qg, kg, bf_pad, gmat)


def _pad_rows8(w):
    return jnp.pad(w, ((0, SUBLANES - w.shape[0]), (0, 0)))


def _fold8(acc):
    return jnp.sum(acc, axis=0, keepdims=True)


def _local_step(x, target, mod, wts):
    T = x.shape[0]
    tm = min(TOKEN_TILE, T)
    sh1, sc1, g1, sh2, sc2, g2 = [mod[:, i * D:(i + 1) * D] for i in range(N_MOD)]
    w_in, w_in_t = wts["w_in"], wts["w_in_t"]
    conv_a8 = _pad_rows8(wts["conv_a_w"])
    conv_f8 = _pad_rows8(wts["conv_ffn_w"])
    qg = jnp.tile(wts["q_norm_g"], (1, N_HEADS))
    kg = jnp.tile(wts["k_norm_g"], (1, N_HEADS))
    bf_pad = jnp.pad(wts["b_f"], ((0, 0), (0, F_PAD - N_HEADS)))
    gmat = _group_matrix()

    h = _norm_mod(x, wts["norm1_g"], sc1, sh1, name="norm1_fwd")
    proj = _matmul(h, w_in, name="mm_in", tm=tm, tn=896, tk=D)
    fcum = _cumsum(proj, reverse=False, name="gate_cumsum", col=COL_F // F_PAD, gate_bias=bf_pad)
    ya0, qa, ka, va = _branch_prep(proj, fcum, conv_a8, qg, kg, gmat)
    o_h, qb = _attn_fwd(qa, ka, va)
    ya, yb, merged = _branch_merge_fwd(ya0, o_h, proj, wts["w_branch_a"], wts["w_branch_b_heads"])
    mix = _matmul(merged, wts["w_out"], name="mm_out", tm=tm, tn=D, tk=D)
    x1, h2 = _resid_norm(x, mix, g1, wts["norm2_g"], sc2, sh2)
    u = _matmul(h2, wts["w_up"], name="mm_up", tm=tm, tn=1408, tk=D)
    act = _ffn_act_fwd(u, conv_f8)
    ff = _matmul(act, wts["w_down"], name="mm_down", tm=tm, tn=D, tk=D_FF)
    dy, dff, sq8, dg2_8 = _loss_head(x1, ff, g2, target)
    sq = jnp.sum(sq8).reshape(1, 1)

    grads = {}
    da = _matmul(dff, wts["w_down_t"], name="mm_down_dx", tm=tm, tn=1408, tk=D)
    grads["w_down"] = _matmul(act, dff, name="mm_down_dw", tm=1408, tn=D, tk=tm, trans_a=True)
    du, dconv_f8 = _ffn_act_bwd(u, da, conv_f8)
    grads["conv_ffn_w"] = dconv_f8[:3]
    dh2 = _matmul(du, wts["w_up_t"], name="mm_up_dx", tm=tm, tn=D, tk=1408)
    grads["w_up"] = _matmul(h2, du, name="mm_up_dw", tm=D, tn=1408, tk=tm, trans_a=True)
    dx1, dsh2_8, dsc2_8, dn2_8 = _norm_bwd(x1, dh2, dy, wts["norm2_g"], sc2, name="norm2_bwd")
    grads["norm2_g"] = _fold8(dn2_8)

    dmix, dg1_8 = _gate_bwd(dx1, mix, g1)
    dmerged = _matmul(dmix, wts["w_out_t"], name="mm_out_dx", tm=tm, tn=D, tk=D)
    grads["w_out"] = _matmul(merged, dmix, name="mm_out_dw", tm=D, tn=D, tk=tm, trans_a=True)
    dya, dyb, dga, dgb = _merge_bwd(dmerged, ya, yb, proj)
    dya0 = _matmul(dya, wts["w_branch_a_t"], name="mm_branch_a_dx", tm=tm, tn=CONV_W, tk=D)
    grads["w_branch_a"] = _matmul(ya0, dya, name="mm_branch_a_dw", tm=CONV_W, tn=D, tk=tm, trans_a=True)
    doa = _branch_b_bwd(dyb, o_h, wts["w_branch_b_heads_t"])
    grads["w_branch_b"] = _branch_b_dw(o_h, dyb)[:, :HEAD_DIM].reshape(ATTN_W, D)
    dcb, dcc, dcv, dconv_a8 = _conv_branch_bwd(proj, dya0, conv_a8)
    grads["conv_a_w"] = dconv_a8[:3]

    dq_h, dk_h, dv_h = _attn_bwd(qb, ka, va, doa)
    dq_tok, dk_tok, dv_tok, dfcum = _attn_unpack(dq_h, dk_h, dv_h)
    dlogf = _cumsum(dfcum, reverse=True, name="gate_cumsum_bwd")
    dq, dk, dfl, dqg8, dkg8, dbf8 = _qk_norm_bwd(proj, dq_tok, dk_tok, dlogf, qg, kg, bf_pad, gmat)
    grads["q_norm_g"] = jnp.sum(_fold8(dqg8).reshape(N_HEADS, HEAD_DIM), axis=0, keepdims=True)
    grads["k_norm_g"] = jnp.sum(_fold8(dkg8).reshape(N_HEADS, HEAD_DIM), axis=0, keepdims=True)
    grads["b_f"] = _fold8(dbf8)[:, :N_HEADS]
    dproj = jnp.concatenate(
        [dcb, dcc, dcv, dq, dk, dv_tok, dga, dgb, dfl, jnp.zeros((T, IN_W_PAD - COL_F - F_PAD), BF16)], axis=1)
    dh = _matmul(dproj, w_in_t, name="mm_in_dx", tm=tm, tn=D, tk=896)
    grads["w_in_pad"] = _matmul(h, dproj, name="mm_in_dw", tm=D, tn=896, tk=tm, trans_a=True)
    grad_x, dsh1_8, dsc1_8, dn1_8 = _norm_bwd(x, dh, dx1, wts["norm1_g"], sc1, name="norm1_bwd")
    grads["norm1_g"] = _fold8(dn1_8)
    grads["mod"] = jnp.concatenate([_fold8(a) for a in (dsh1_8, dsc1_8, dg1_8, dsh2_8, dsc2_8, dg2_8)], axis=1)
    return sq, grad_x, grads


def _me_and_peers():
    mx, my, mc = lax.axis_index("x"), lax.axis_index("y"), lax.axis_index("c")
    me = 4 * mx + 2 * my + mc
    peers = []
    for k in range(1, N_DEV):
        px = 1 - mx if k & 4 else mx
        py = 1 - my if k & 2 else my
        pc = 1 - mc if k & 1 else mc
        peers.append(((px, py, pc), 4 * px + 2 * py + pc))
    return me, peers


HBM_SPEC = pl.BlockSpec(memory_space=pltpu.HBM)


def _exchange(xs, *, name, scatter):
    n = len(xs)
    out_shapes = [jax.ShapeDtypeStruct(x.shape if scatter else (N_DEV,) + x.shape, x.dtype) for x in xs]

    def body(*refs):
        x_refs, out_refs = refs[:n], refs[n:2 * n]
        send_sems, recv_sems, local_sems = refs[2 * n:]
        me, peers = _me_and_peers()

        def src(a, idx):
            return x_refs[a].at[idx] if scatter else x_refs[a]

        def copy(a, k, from_idx, to_slot, device):
            return pltpu.make_async_remote_copy(
                src_ref=src(a, from_idx), dst_ref=out_refs[a].at[to_slot], send_sem=send_sems.at[a, k],
                recv_sem=recv_sems.at[a, k], device_id=device, device_id_type=MESH)

        local = [pltpu.make_async_copy(src(a, me), out_refs[a].at[me], local_sems.at[a]) for a in range(n)]
        for cp in local:
            cp.start()
        sends = [copy(a, k, idx, me, dev) for a in range(n) for k, (dev, idx) in enumerate(peers)]
        for cp in sends:
            cp.start()
        for a in range(n):
            for k, (dev, idx) in enumerate(peers):
                copy(a, k, idx, idx, dev).wait_recv()
        for cp in sends:
            cp.wait_send()
        for cp in local:
            cp.wait()

    return pl.pallas_call(
        body, name=name, in_specs=[HBM_SPEC] * n, out_specs=[HBM_SPEC] * n, out_shape=out_shapes,
        scratch_shapes=[pltpu.SemaphoreType.DMA((n, N_DEV - 1)), pltpu.SemaphoreType.DMA((n, N_DEV - 1)),
                        pltpu.SemaphoreType.DMA((n,))],
        compiler_params=pltpu.CompilerParams(has_side_effects=True),
    )(*xs)


def _ada_fwd(c_all, w_shard, b_shard):
    n = w_shard.shape[1]

    def body(c_ref, w_ref, b_ref, o_ref):
        cv = c_ref[...]
        act = (cv * _sigmoid(cv)).astype(BF16)
        o_ref[...] = jnp.dot(act, w_ref[...].astype(BF16), preferred_element_type=F32) + b_ref[...]

    return pl.pallas_call(
        body, name="ada_fwd", in_specs=[_full((N_DEV, D)), _full((D, n)), _full((1, n))],
        out_specs=_full((N_DEV, n)), out_shape=jax.ShapeDtypeStruct((N_DEV, n), F32), grid=(1,),
        compiler_params=_params(("arbitrary",)),
    )(c_all, w_shard, b_shard)


def _ada_bwd(c_all_t, dmod_pad):
    n = dmod_pad.shape[1]

    def body(c_ref, d_ref, o_ref):
        cv = c_ref[...]
        act = (cv * _sigmoid(cv)).astype(BF16)
        o_ref[...] = jnp.dot(act, d_ref[...].astype(BF16), preferred_element_type=F32)

    return pl.pallas_call(
        body, name="ada_bwd", in_specs=[_full((D, LANES)), _full((LANES, n))],
        out_specs=_full((D, n)), out_shape=jax.ShapeDtypeStruct((D, n), F32), grid=(1,),
        compiler_params=_params(("arbitrary",)),
    )(c_all_t, dmod_pad)


ADAM_ROWS = 64


def _adamw(parts, w, m, v, *, name):
    n, R, C = parts.shape
    tr = next((t for t in (ADAM_ROWS, 32, 16, SUBLANES) if R % t == 0), R)

    def body(p_ref, w_ref, m_ref, v_ref, g_ref, d_ref, nm_ref, nv_ref):
        g = p_ref[0].astype(F32)
        for j in range(1, n):
            g = g + p_ref[j].astype(F32)
        g_ref[...] = g
        nm = ADAM_B1 * m_ref[...] + (1.0 - ADAM_B1) * g
        nv = ADAM_B2 * v_ref[...] + (1.0 - ADAM_B2) * (g * g)
        nm_ref[...] = nm
        nv_ref[...] = nv
        m_hat = nm / (1.0 - ADAM_B1 ** ADAM_STEP)
        v_hat = nv / (1.0 - ADAM_B2 ** ADAM_STEP)
        d_ref[...] = -ADAM_LR * (m_hat / (jnp.sqrt(v_hat) + ADAM_EPS) + ADAM_WD * w_ref[...])

    row = pl.BlockSpec((tr, C), lambda i: (i, 0))
    return pl.pallas_call(
        body, name=name, grid=(R // tr,),
        in_specs=[pl.BlockSpec((n, tr, C), lambda i: (0, i, 0)), row, row, row], out_specs=[row] * 4,
        out_shape=[jax.ShapeDtypeStruct((R, C), F32)] * 4,
        compiler_params=_params(("parallel",)),
    )(parts, w, m, v)


SHARDED = (("w_in", D, IN_W, 1), ("w_branch_a", CONV_W, D, 1), ("w_branch_b", ATTN_W, D, 1), ("w_out", D, D, 0),
           ("w_up", D, 2 * D_FF, 1), ("w_down", D_FF, D, 0), ("conv_a_w", 3, CONV_W, 1),
           ("conv_ffn_w", 3, 2 * D_FF, 1))
MATRICES = SHARDED[:6]
CONVS = SHARDED[6:]
REPLICATED = (("b_ada", N_MOD * D), ("norm1_g", D), ("norm2_g", D), ("b_f", N_HEADS), ("q_norm_g", HEAD_DIM),
              ("k_norm_g", HEAD_DIM))


def _shard_shape(rows, cols, axis):
    return (rows // N_DEV, cols) if axis == 0 else (rows, cols // N_DEV)


def _pack_rows(flat, multiple):
    length = flat.shape[-1]
    rows = -(-length // PACK_W)
    rows = -(-rows // multiple) * multiple
    pad = [(0, 0)] * (flat.ndim - 1) + [(0, rows * PACK_W - length)]
    return jnp.pad(flat, pad).reshape(flat.shape[:-1] + (rows, PACK_W))


def _pack_shards(shards, spec, multiple, dtype):
    flat = jnp.concatenate([shards[name].reshape(-1).astype(dtype) for name, *_ in spec])
    return _pack_rows(flat, multiple)


def _join_shards(gathered, axis):
    if axis == 0:
        return gathered.reshape(N_DEV * gathered.shape[1], gathered.shape[2])
    return jnp.concatenate([gathered[j] for j in range(N_DEV)], axis=1)


def _split_shards(full, axis):
    if axis == 0:
        return full.reshape(N_DEV, full.shape[0] // N_DEV, full.shape[1])
    c = full.shape[1] // N_DEV
    return jnp.stack([full[:, j * c:(j + 1) * c] for j in range(N_DEV)])


def _unpack_shards(packed, spec):
    flat = packed.reshape(-1)
    out, off = {}, 0
    for name, rows, cols, axis in spec:
        r, c = _shard_shape(rows, cols, axis)
        out[name] = flat[off:off + r * c].reshape(r, c)
        off += r * c
    return out


def _unpack_gathered(gathered, spec):
    flat = gathered.reshape(N_DEV, -1)
    out, off = {}, 0
    for name, rows, cols, axis in spec:
        r, c = _shard_shape(rows, cols, axis)
        seg = flat[:, off:off + r * c].reshape(N_DEV, r, c)
        out[name] = seg.reshape(rows, cols) if axis == 0 else seg.transpose(1, 0, 2).reshape(rows, cols)
        off += r * c
    return out


def _pack_full_by_dest(full, spec, multiple):
    segs = []
    for name, rows, cols, axis in spec:
        r, c = _shard_shape(rows, cols, axis)
        a = full[name]
        seg = a.reshape(N_DEV, r, c) if axis == 0 else a.reshape(rows, N_DEV, c).transpose(1, 0, 2)
        segs.append(seg.reshape(N_DEV, r * c))
    return _pack_rows(jnp.concatenate(segs, axis=1), multiple)


def _pad_in(w_in):
    return jnp.concatenate([w_in[:, :COL_GA], w_in[:, COL_GA + N_HEADS:], w_in[:, COL_GA:COL_GA + N_HEADS],
                            jnp.zeros((w_in.shape[0], IN_W_PAD - IN_W), w_in.dtype)], axis=1)


def _pad_head_rows(w):
    n = w.shape[1]
    padded = jnp.pad(w.reshape(N_HEADS, HEAD_DIM, n), ((0, 0), (0, LANES - HEAD_DIM), (0, 0)))
    return padded.reshape(N_HEADS * LANES, n)


def _unpad_in(g):
    return jnp.concatenate([g[:, :COL_GA], g[:, COL_F:COL_F + N_HEADS], g[:, COL_GA:COL_F]], axis=1)


def kernel(x, c, w_ada, b_ada, norm1_g, w_in, b_f, conv_a_w, q_norm_g, k_norm_g, w_branch_a, w_branch_b, w_out, norm2_g, w_up, conv_ffn_w, w_down, loss_target, m_w_ada, m_b_ada, m_norm1_g, m_w_in, m_b_f, m_conv_a_w, m_q_norm_g, m_k_norm_g, m_w_branch_a, m_w_branch_b, m_w_out, m_norm2_g, m_w_up, m_conv_ffn_w, m_w_down, v_w_ada, v_b_ada, v_norm1_g, v_w_in, v_b_f, v_conv_a_w, v_q_norm_g, v_k_norm_g, v_w_branch_a, v_w_branch_b, v_w_out, v_norm2_g, v_w_up, v_conv_ffn_w, v_w_down):
    names = ("w_ada", "b_ada", "norm1_g", "w_in", "b_f", "conv_a_w", "q_norm_g", "k_norm_g", "w_branch_a",
             "w_branch_b", "w_out", "norm2_g", "w_up", "conv_ffn_w", "w_down")
    squeeze = lambda a: a[0] if a.ndim == 3 else a
    W = dict(zip(names, map(squeeze, (w_ada, b_ada, norm1_g, w_in, b_f, conv_a_w, q_norm_g, k_norm_g, w_branch_a,
                                      w_branch_b, w_out, norm2_g, w_up, conv_ffn_w, w_down))))
    M = dict(zip(names, map(squeeze, (m_w_ada, m_b_ada, m_norm1_g, m_w_in, m_b_f, m_conv_a_w, m_q_norm_g,
                                      m_k_norm_g, m_w_branch_a, m_w_branch_b, m_w_out, m_norm2_g, m_w_up,
                                      m_conv_ffn_w, m_w_down))))
    V = dict(zip(names, map(squeeze, (v_w_ada, v_b_ada, v_norm1_g, v_w_in, v_b_f, v_conv_a_w, v_q_norm_g,
                                      v_k_norm_g, v_w_branch_a, v_w_branch_b, v_w_out, v_norm2_g, v_w_up,
                                      v_conv_ffn_w, v_w_down))))
    me = 4 * lax.axis_index("x") + 2 * lax.axis_index("y") + lax.axis_index("c")
    ada_n = N_MOD * D // N_DEV

    small = jnp.concatenate([c.reshape(-1), W["conv_a_w"].reshape(-1), W["conv_ffn_w"].reshape(-1)])
    gathered = _exchange([_pack_rows(small, SUBLANES)] + [W[name].astype(BF16) for name, *_ in MATRICES],
                         name="gather_weights", scatter=False)
    small_all = gathered[0].reshape(N_DEV, -1)
    c_all = small_all[:, :D]
    conv_all = _unpack_gathered(small_all[:, D:], CONVS)
    mats = {name: _join_shards(g, axis) for (name, _, _, axis), g in zip(MATRICES, gathered[1:])}

    b_shard = lax.dynamic_slice(W["b_ada"], (0, me * ada_n), (1, ada_n))
    mod_part = _ada_fwd(c_all, W["w_ada"], b_shard)
    mod_all, = _exchange([mod_part], name="gather_mod", scatter=False)
    mod = lax.dynamic_index_in_dim(mod_all, me, axis=1, keepdims=False).reshape(1, N_MOD * D)

    wts = {"w_in": _pad_in(mats["w_in"])}
    wts["w_in_t"] = wts["w_in"].T
    for name in ("w_branch_a", "w_out", "w_up", "w_down"):
        wts[name] = mats[name]
        wts[name + "_t"] = mats[name].T
    wts["w_branch_b_heads"] = _pad_head_rows(mats["w_branch_b"])
    wts["w_branch_b_heads_t"] = wts["w_branch_b_heads"].T
    wts.update(conv_all)
    for name in ("norm1_g", "norm2_g", "q_norm_g", "k_norm_g", "b_f"):
        wts[name] = W[name]

    sq, grad_x, grads = _local_step(x[0], loss_target[0], mod, wts)
    loss = lax.psum(sq[0, 0] * (0.5 / D), AXES)

    grads["b_ada"] = grads["mod"]
    rep_flat = lambda src: jnp.concatenate([src[name].reshape(-1) for name, _ in REPLICATED])
    rep_parts, = _exchange([_pack_rows(rep_flat(grads), 16)], name="gather_small_grads", scatter=False)
    rep_out = _adamw(rep_parts, *[_pack_rows(rep_flat(s), 16) for s in (W, M, V)], name="adamw_replicated")

    dmod_all = rep_parts.reshape(N_DEV, -1)[:, :N_MOD * D]
    dmod_mine = lax.dynamic_slice(dmod_all, (0, me * ada_n), (N_DEV, ada_n))
    g_ada = _ada_bwd(jnp.pad(c_all.T, ((0, 0), (0, LANES - N_DEV))),
                     jnp.pad(dmod_mine, ((0, LANES - N_DEV), (0, 0))))
    ada_out = _adamw(g_ada[None], W["w_ada"], M["w_ada"], V["w_ada"], name="adamw_ada")

    grads["w_in"] = _unpad_in(grads["w_in_pad"])
    parts = _exchange([_split_shards(grads[name], axis).astype(BF16) for name, _, _, axis in MATRICES]
                      + [_pack_full_by_dest(grads, CONVS, SUBLANES)], name="scatter_grads", scatter=True)
    mat_out = {name: _adamw(p, W[name], M[name], V[name], name="adamw_" + name)
               for (name, *_), p in zip(MATRICES, parts)}
    conv_out = _adamw(parts[-1], *[_pack_shards(s, CONVS, SUBLANES, F32) for s in (W, M, V)], name="adamw_conv")

    results = []
    for kind in range(4):
        per = {"w_ada": ada_out[kind]}
        per.update({name: out[kind] for name, out in mat_out.items()})
        per.update(_unpack_shards(conv_out[kind], CONVS))
        flat, off = rep_out[kind].reshape(-1), 0
        for name, n in REPLICATED:
            per[name] = flat[off:off + n].reshape(1, n)
            off += n
        results.append(per)
    restore = lambda name, a: a[None] if W[name].ndim == 2 and name not in dict(REPLICATED) else a
    outs = [loss, grad_x[None]]
    for per in results:
        outs.extend(restore(name, per[name]) for name in names)
    return tuple(outs)
```

```python
import functools

import jax
import jax.numpy as jnp
import numpy as np
from jax import lax
from jax.experimental import pallas as pl
from jax.experimental.pallas import tpu as pltpu

F32 = jnp.float32
BF16 = jnp.bfloat16

N_DEV = 8
D = 1024
N_HEADS = 8
HEAD_DIM = 64
ATTN_W = 512
CONV_W = 512
D_FF = 2816
N_MOD = 6
IN_W = 5128
RMS_EPS = 1e-6
NEG_INF = -1e30

IN_W_PAD = 5376
COL_GA = 3072
COL_GB = 4096
COL_F = 5120
F_PAD = 128

ADAM_LR = 0.001
ADAM_B1 = 0.9
ADAM_B2 = 0.999
ADAM_EPS = 1e-08
ADAM_WD = 0.01
ADAM_STEP = 10

LANES = 128
SUBLANES = 8
VMEM_LIMIT = 52 * 1024 * 1024
TOKEN_TILE = 512
ATTN_BLOCK = 512
PACK_W = 1024

MESH = pl.DeviceIdType.MESH
AXES = ("x", "y", "c")


def _params(sem=None, **kw):
    return pltpu.CompilerParams(dimension_semantics=sem, vmem_limit_bytes=VMEM_LIMIT, **kw)


def _full(shape):
    nd = len(shape)
    return pl.BlockSpec(shape, lambda *_: (0,) * nd)


def _tn_dot(a, b):
    return lax.dot_general(a, b, (((0,), (0,)), ((), ())), preferred_element_type=F32)


def _matmul(a, b, *, name, tm, tn, tk, out_dtype=F32, trans_a=False, exchange=None):
    if trans_a:
        K, M = a.shape
    else:
        M, K = a.shape
    N = b.shape[1]
    assert b.shape[0] == K and M % tm == 0 and N % tn == 0 and K % tk == 0, (name, a.shape, b.shape)
    nm, nn, nk = M // tm, N // tn, K // tk

    def body(*refs):
        if exchange is None:
            a_ref, b_ref, o_ref, acc_ref = refs
        else:
            (a_ref, b_ref), (o_ref,), (acc_ref,), xrefs = exchange.split(refs, 2, 1)
            ids = [pl.program_id(d) for d in range(3)]
            first = jnp.logical_and(jnp.logical_and(ids[0] == 0, ids[1] == 0), ids[2] == 0)
            last = jnp.logical_and(jnp.logical_and(ids[0] == nn - 1, ids[1] == nm - 1), ids[2] == nk - 1)
            _ride(exchange, first, last, xrefs)
        k = pl.program_id(2)

        @pl.when(k == 0)
        def _():
            acc_ref[...] = jnp.zeros_like(acc_ref)

        av = a_ref[...].astype(BF16)
        bv = b_ref[...].astype(BF16)
        if trans_a:
            acc_ref[...] += _tn_dot(av, bv)
        else:
            acc_ref[...] += jnp.dot(av, bv, preferred_element_type=F32)

        @pl.when(k == nk - 1)
        def _():
            o_ref[...] = acc_ref[...].astype(out_dtype)

    if trans_a:
        a_spec = pl.BlockSpec((tk, tm), lambda j, i, k: (k, i))
    else:
        a_spec = pl.BlockSpec((tm, tk), lambda j, i, k: (i, k))
    in_specs = [a_spec, pl.BlockSpec((tk, tn), lambda j, i, k: (k, j))]
    out_spec = pl.BlockSpec((tm, tn), lambda j, i, k: (i, j))
    out_shape = jax.ShapeDtypeStruct((M, N), out_dtype)
    scratch = [pltpu.VMEM((tm, tn), F32)]
    if exchange is None:
        return pl.pallas_call(
            body, name=name, grid=(nn, nm, nk), in_specs=in_specs, out_specs=out_spec, out_shape=out_shape,
            scratch_shapes=scratch, compiler_params=_params(("parallel", "parallel", "arbitrary")),
        )(a, b)
    return pl.pallas_call(
        body, name=name, grid=(nn, nm, nk), in_specs=in_specs + exchange.in_specs,
        out_specs=[out_spec] + exchange.out_specs, out_shape=[out_shape] + exchange.out_shapes,
        scratch_shapes=scratch + exchange.scratch, compiler_params=_params(("arbitrary",) * 3),
    )(a, b, *exchange.xs)


def _split_dot(x, mat, parts):
    out = None
    rem = x
    for p in range(parts):
        piece = rem.astype(BF16)
        term = jnp.dot(piece, mat, preferred_element_type=F32)
        out = term if out is None else out + term
        if p + 1 < parts:
            rem = rem - piece.astype(F32)
    return out


def _sigmoid(x):
    return 1.0 / (1.0 + jnp.exp(-x))


def _rows8(x):
    r, c = x.shape
    return jnp.sum(x.reshape(r // SUBLANES, SUBLANES, c), axis=0)


def _shift_down(blk, prev8, n):
    rolled = pltpu.roll(blk, n, axis=0)
    prev_rolled = pltpu.roll(prev8, n, axis=0)
    rows = lax.broadcasted_iota(jnp.int32, prev8.shape, 0)
    first = jnp.where(rows < n, prev_rolled, rolled[0:SUBLANES])
    return jnp.concatenate([first, rolled[SUBLANES:]], axis=0)


def _shift_up(blk, next8, n):
    r = blk.shape[0]
    rolled = pltpu.roll(blk, r - n, axis=0)
    next_rolled = pltpu.roll(next8, SUBLANES - n, axis=0)
    rows = lax.broadcasted_iota(jnp.int32, next8.shape, 0)
    last = jnp.where(rows >= SUBLANES - n, next_rolled, rolled[r - SUBLANES:])
    return jnp.concatenate([rolled[:r - SUBLANES], last], axis=0)


def _prev_spec(tm, width, col):
    per = tm // SUBLANES
    return pl.BlockSpec((SUBLANES, width), lambda i, *_: (jnp.maximum(i * per - 1, 0), col))


def _next_spec(tm, width, col, n_tiles):
    per = tm // SUBLANES
    last = n_tiles * per - 1
    return pl.BlockSpec((SUBLANES, width), lambda i, *_: (jnp.minimum((i + 1) * per, last), col))


def _group_matrix():
    idx = np.arange(ATTN_W) // HEAD_DIM
    return jnp.asarray((idx[:, None] == idx[None, :]).astype(np.float32), BF16)


def _norm_mod(x, g, sc, sh, *, name):
    T = x.shape[0]
    tm = min(TOKEN_TILE, T)

    def body(x_ref, g_ref, sc_ref, sh_ref, o_ref):
        xv = x_ref[...]
        inv = lax.rsqrt(jnp.mean(xv * xv, axis=-1, keepdims=True) + RMS_EPS)
        o_ref[...] = ((xv * inv) * g_ref[...] * (1.0 + sc_ref[...]) + sh_ref[...]).astype(BF16)

    row = pl.BlockSpec((tm, D), lambda i: (i, 0))
    return pl.pallas_call(
        body, name=name, grid=(T // tm,),
        in_specs=[row, _full((1, D)), _full((1, D)), _full((1, D))],
        out_specs=row, out_shape=jax.ShapeDtypeStruct((T, D), BF16),
        compiler_params=_params(("parallel",)),
    )(x, g, sc, sh)


LANE_ONE = 64
LANE_F = 67
LANE_LSE = 70
LANE_SUM = 73


def _pieces(x):
    hi = x.astype(BF16).astype(F32)
    rest = x - hi
    mid = rest.astype(BF16).astype(F32)
    return hi, mid, rest - mid


def _aug(lane, data, entries):
    out = jnp.where(lane < HEAD_DIM, data, 0.0)
    for idx, val in entries:
        out = jnp.where(lane == idx, val, out)
    return out


def _run(start, vals):
    return [(start + i, v) for i, v in enumerate(vals)]


def _head_lanes(a, h):
    blk = a[:, LANES * (h // 2):LANES * (h // 2) + LANES]
    return blk if h % 2 == 0 else pltpu.roll(blk, HEAD_DIM, axis=1)


def _branch_prep(proj, fcum, conv_w8, qg, kg, gmat):
    T = proj.shape[0]
    tm = min(TOKEN_TILE, T)
    nt = T // tm

    def body(cb_ref, cc_ref, cv_ref, q_ref, k_ref, v_ref, f_ref, ccp_ref, cvp_ref, w_ref, qg_ref, kg_ref, g_ref,
             ya_ref, qa_ref, ka_ref, va_ref):
        i = pl.program_id(0)
        z = cc_ref[...] * cv_ref[...]
        zp = jnp.where(i > 0, ccp_ref[...] * cvp_ref[...], 0.0)
        w = w_ref[...]
        cz = _shift_down(z, zp, 2) * w[0:1] + _shift_down(z, zp, 1) * w[1:2] + z * w[2:3]
        ya_ref[...] = (cb_ref[...] * cz).astype(BF16)
        gm = g_ref[...]

        def normed(src, gain, scale):
            v = src[...]
            ms = _split_dot(v * v, gm, 2) * (1.0 / HEAD_DIM)
            return (v * lax.rsqrt(ms + RMS_EPS)) * gain[...] * scale

        qn = normed(q_ref, qg_ref, 1.0 / np.sqrt(HEAD_DIM))
        kn = normed(k_ref, kg_ref, 1.0)
        vv = v_ref[...]
        fall = f_ref[...]
        lane = lax.broadcasted_iota(jnp.int32, (tm, LANES), 1)
        ones3 = [1.0, 1.0, 1.0]
        for h in range(N_HEADS):
            hi, mid, lo = _pieces(fall[:, h:h + 1])
            qa_ref[h] = _aug(lane, _head_lanes(qn, h), _run(LANE_ONE, ones3) + _run(LANE_F, [hi, mid, lo])
                             + [(LANE_SUM, 1.0)]).astype(BF16)
            ka_ref[h] = _aug(lane, _head_lanes(kn, h), _run(LANE_ONE, [-hi, -mid, -lo]) + _run(LANE_F, ones3)
                             + _run(LANE_LSE, ones3)).astype(BF16)
            va_ref[h] = _aug(lane, _head_lanes(vv, h), _run(LANE_ONE, ones3)).astype(BF16)

    blk = lambda col: pl.BlockSpec((tm, CONV_W), lambda i: (i, col))
    heads = pl.BlockSpec((N_HEADS, tm, LANES), lambda i: (0, i, 0))
    return pl.pallas_call(
        body, name="branch_prep", grid=(nt,),
        in_specs=[blk(0), blk(1), blk(2), blk(3), blk(4), blk(5), pl.BlockSpec((tm, F_PAD), lambda i: (i, 0)),
                  _prev_spec(tm, CONV_W, 1), _prev_spec(tm, CONV_W, 2),
                  _full((SUBLANES, CONV_W)), _full((1, ATTN_W)), _full((1, ATTN_W)), _full((ATTN_W, ATTN_W))],
        out_specs=[pl.BlockSpec((tm, CONV_W), lambda i: (i, 0)), heads, heads, heads],
        out_shape=[jax.ShapeDtypeStruct((T, CONV_W), BF16)] + [jax.ShapeDtypeStruct((N_HEADS, T, LANES), BF16)] * 3,
        compiler_params=_params(("parallel",)),
    )(proj, proj, proj, proj, proj, proj, fcum, proj, proj, conv_w8, qg, kg, gmat)


def _cumsum(x, *, reverse, name, col=0, gate_bias=None):
    T = x.shape[0]
    tm = min(TOKEN_TILE, T)
    nt = T // tm

    def body(x_ref, b_ref, o_ref, carry_ref):
        i = pl.program_id(0)

        @pl.when(i == 0)
        def _():
            carry_ref[...] = jnp.zeros_like(carry_ref)

        r = lax.broadcasted_iota(jnp.int32, (tm, tm), 0)
        c = lax.broadcasted_iota(jnp.int32, (tm, tm), 1)
        tri = jnp.where((c >= r) if reverse else (c <= r), 1.0, 0.0).astype(BF16)
        xv = x_ref[...]
        if gate_bias is not None:
            fx = xv + b_ref[...]
            xv = jnp.minimum(fx, 0.0) - jnp.log(1.0 + jnp.exp(-jnp.abs(fx)))
        out = _split_dot_left(tri, xv, 3) + carry_ref[0:1]
        o_ref[...] = out
        carry_ref[...] = jnp.broadcast_to(out[0:1] if reverse else out[tm - 1:tm], carry_ref.shape)

    rows = (lambda i: nt - 1 - i) if reverse else (lambda i: i)
    bias = jnp.zeros((1, F_PAD), F32) if gate_bias is None else gate_bias
    return pl.pallas_call(
        body, name=name, grid=(nt,),
        in_specs=[pl.BlockSpec((tm, F_PAD), lambda i: (rows(i), col)), _full((1, F_PAD))],
        out_specs=pl.BlockSpec((tm, F_PAD), lambda i: (rows(i), 0)),
        out_shape=jax.ShapeDtypeStruct((T, F_PAD), F32),
        scratch_shapes=[pltpu.VMEM((SUBLANES, F_PAD), F32)],
        compiler_params=_params(("arbitrary",)),
    )(x, bias)


def _split_dot_left(mat, x, parts):
    out = None
    rem = x
    for p in range(parts):
        piece = rem.astype(BF16)
        term = jnp.dot(mat, piece, preferred_element_type=F32)
        out = term if out is None else out + term
        if p + 1 < parts:
            rem = rem - piece.astype(F32)
    return out


def _resid_norm(x, mix, g1, g, sc, sh):
    T = x.shape[0]
    tm = min(TOKEN_TILE, T)

    def body(x_ref, mix_ref, g1_ref, g_ref, sc_ref, sh_ref, x1_ref, h_ref):
        x1 = x_ref[...] + g1_ref[...] * mix_ref[...]
        x1_ref[...] = x1
        inv = lax.rsqrt(jnp.mean(x1 * x1, axis=-1, keepdims=True) + RMS_EPS)
        h_ref[...] = ((x1 * inv) * g_ref[...] * (1.0 + sc_ref[...]) + sh_ref[...]).astype(BF16)

    row = pl.BlockSpec((tm, D), lambda i: (i, 0))
    vec = _full((1, D))
    return pl.pallas_call(
        body, name="resid_norm", grid=(T // tm,),
        in_specs=[row, row, vec, vec, vec, vec], out_specs=[row, row],
        out_shape=[jax.ShapeDtypeStruct((T, D), F32), jax.ShapeDtypeStruct((T, D), BF16)],
        compiler_params=_params(("parallel",)),
    )(x, mix, g1, g, sc, sh)


FFN_TM = 256
FFN_TC = 1408


def _ffn_act_fwd(u, w8):
    T = u.shape[0]
    tm = min(FFN_TM, T)
    nt = T // tm
    nc = D_FF // FFN_TC

    def body(ug_ref, uv_ref, ugp_ref, uvp_ref, wg_ref, wv_ref, o_ref):
        i = pl.program_id(1)

        def conv(u_ref, p_ref, w_ref):
            uv = u_ref[...]
            up = jnp.where(i > 0, p_ref[...], 0.0)
            w = w_ref[...]
            return _shift_down(uv, up, 2) * w[0:1] + _shift_down(uv, up, 1) * w[1:2] + uv * w[2:3]

        gate = conv(ug_ref, ugp_ref, wg_ref)
        val = conv(uv_ref, uvp_ref, wv_ref)
        o_ref[...] = (gate * _sigmoid(gate) * val).astype(BF16)

    per = tm // SUBLANES
    blk = lambda off: pl.BlockSpec((tm, FFN_TC), lambda j, i: (i, j + off))
    prev = lambda off: pl.BlockSpec((SUBLANES, FFN_TC), lambda j, i: (jnp.maximum(i * per - 1, 0), j + off))
    wblk = lambda off: pl.BlockSpec((SUBLANES, FFN_TC), lambda j, i: (0, j + off))
    return pl.pallas_call(
        body, name="ffn_act_fwd", grid=(nc, nt),
        in_specs=[blk(0), blk(nc), prev(0), prev(nc), wblk(0), wblk(nc)],
        out_specs=pl.BlockSpec((tm, FFN_TC), lambda j, i: (i, j)),
        out_shape=jax.ShapeDtypeStruct((T, D_FF), BF16),
        compiler_params=_params(("parallel", "parallel")),
    )(u, u, u, u, w8, w8)


def _loss_head(x1, ff, g2, target):
    T = x1.shape[0]
    tm = min(TOKEN_TILE, T)

    def body(x1_ref, ff_ref, g2_ref, t_ref, dy_ref, dff_ref, loss_ref, dg2_ref):
        i = pl.program_id(0)

        @pl.when(i == 0)
        def _():
            loss_ref[...] = jnp.zeros_like(loss_ref)
            dg2_ref[...] = jnp.zeros_like(dg2_ref)

        ff = ff_ref[...]
        err = x1_ref[...] + g2_ref[...] * ff - t_ref[...]
        dy = err * (1.0 / D)
        dy_ref[...] = dy
        dff_ref[...] = (dy * g2_ref[...]).astype(BF16)
        loss_ref[...] += _rows8(err * err)
        dg2_ref[...] += _rows8(dy * ff)

    row = pl.BlockSpec((tm, D), lambda i: (i, 0))
    acc = _full((SUBLANES, D))
    return pl.pallas_call(
        body, name="loss_head", grid=(T // tm,),
        in_specs=[row, row, _full((1, D)), row], out_specs=[row, row, acc, acc],
        out_shape=[jax.ShapeDtypeStruct((T, D), F32), jax.ShapeDtypeStruct((T, D), BF16),
                   jax.ShapeDtypeStruct((SUBLANES, D), F32), jax.ShapeDtypeStruct((SUBLANES, D), F32)],
        compiler_params=_params(("arbitrary",)),
    )(x1, ff, g2, target)


def _nt_dot(a, b):
    return lax.dot_general(a, b, (((1,), (1,)), ((), ())), preferred_element_type=F32)


def _causal(n, keys_on_rows=False):
    r = lax.broadcasted_iota(jnp.int32, (n, n), 0)
    c = lax.broadcasted_iota(jnp.int32, (n, n), 1)
    return (c >= r) if keys_on_rows else (c <= r)


def _sweep(lo, hi, step, carry):
    pairs = (hi - lo) // 2
    carry = lax.fori_loop(0, pairs, lambda j, cr: step(lo + 2 * j + 1, step(lo + 2 * j, cr)), carry)
    return lax.fori_loop(lo + 2 * pairs, hi, step, carry)


def _grid_ends(n0, n1):
    i0, i1 = pl.program_id(0), pl.program_id(1)
    return jnp.logical_and(i0 == 0, i1 == 0), jnp.logical_and(i0 == n0 - 1, i1 == n1 - 1)


def _attn_fwd(qa, ka, va, exchange=None):
    nh, T, _ = qa.shape
    bq = min(ATTN_BLOCK, T)
    nq = T // bq

    def body(*refs):
        if exchange is None:
            q_ref, k_ref, v_ref, o_ref, qb_ref = refs
        else:
            (q_ref, k_ref, v_ref), (o_ref, qb_ref), _, xrefs = exchange.split(refs, 3, 2)
            _ride(exchange, *_grid_ends(nh, nq), xrefs)
        qi = pl.program_id(1)
        q = q_ref[0]

        def step(kb, carry, masked):
            m, acc = carry
            start = pl.multiple_of(kb * bq, bq)
            s = _nt_dot(q, k_ref[0, pl.ds(start, bq), :])
            if masked:
                s = jnp.where(_causal(bq), s, NEG_INF)
            m_new = jnp.maximum(m, jnp.max(s, axis=-1, keepdims=True))
            p = jnp.exp(s - m_new).astype(BF16)
            acc = jnp.exp(m - m_new) * acc + jnp.dot(p, v_ref[0, pl.ds(start, bq), :], preferred_element_type=F32)
            return m_new, acc

        init = (jnp.full((bq, 1), NEG_INF, F32), jnp.zeros((bq, LANES), F32))
        carry = _sweep(0, qi, lambda kb, cr: step(kb, cr, False), init)
        m, acc = step(qi, carry, True)
        l = acc[:, LANE_ONE:LANE_ONE + 1]
        o_ref[0] = acc / l
        lse = m + jnp.log(l)
        lane = lax.broadcasted_iota(jnp.int32, (bq, LANES), 1)
        qf = q.astype(F32)
        for idx, piece in _run(LANE_LSE, _pieces(lse)):
            qf = jnp.where(lane == idx, -piece, qf)
        qb_ref[0] = qf.astype(BF16)

    tile = pl.BlockSpec((1, bq, LANES), lambda h, i: (h, i, 0))
    whole = pl.BlockSpec((1, T, LANES), lambda h, i: (h, 0, 0))
    out_shape = [jax.ShapeDtypeStruct((nh, T, LANES), F32), jax.ShapeDtypeStruct((nh, T, LANES), BF16)]
    if exchange is None:
        return pl.pallas_call(
            body, name="attn_fwd", grid=(nh, nq), in_specs=[tile, whole, whole], out_specs=[tile, tile],
            out_shape=out_shape, compiler_params=_params(("parallel", "parallel")),
        )(qa, ka, va)
    return pl.pallas_call(
        body, name="attn_fwd", grid=(nh, nq), in_specs=[tile, whole, whole] + exchange.in_specs,
        out_specs=[tile, tile] + exchange.out_specs, out_shape=out_shape + exchange.out_shapes,
        scratch_shapes=exchange.scratch, compiler_params=_params(("arbitrary", "arbitrary")),
    )(qa, ka, va, *exchange.xs)


def _branch_merge_fwd(ya0, o_h, proj, wba, wbb_heads):
    nh, T, _ = o_h.shape
    tm = min(TOKEN_TILE, T)

    def body(ya0_ref, o_ref, ga_ref, gb_ref, wa_ref, wb_ref, ya_ref, yb_ref, m_ref):
        ya = jnp.dot(ya0_ref[...], wa_ref[...], preferred_element_type=F32)
        yb = jnp.dot(o_ref[0].astype(BF16), wb_ref[0:LANES, :], preferred_element_type=F32)
        for h in range(1, nh):
            yb += jnp.dot(o_ref[h].astype(BF16), wb_ref[h * LANES:(h + 1) * LANES, :], preferred_element_type=F32)
        ya_ref[...] = ya
        yb_ref[...] = yb
        m_ref[...] = (_sigmoid(ga_ref[...]) * ya + _sigmoid(gb_ref[...]) * yb).astype(BF16)

    row = pl.BlockSpec((tm, D), lambda i: (i, 0))
    return pl.pallas_call(
        body, name="branch_merge_fwd", grid=(T // tm,),
        in_specs=[pl.BlockSpec((tm, CONV_W), lambda i: (i, 0)), pl.BlockSpec((nh, tm, LANES), lambda i: (0, i, 0)),
                  pl.BlockSpec((tm, D), lambda i: (i, COL_GA // D)), pl.BlockSpec((tm, D), lambda i: (i, COL_GB // D)),
                  _full((CONV_W, D)), _full((nh * LANES, D))],
        out_specs=[row, row, row],
        out_shape=[jax.ShapeDtypeStruct((T, D), F32), jax.ShapeDtypeStruct((T, D), F32),
                   jax.ShapeDtypeStruct((T, D), BF16)],
        compiler_params=_params(("parallel",)),
    )(ya0, o_h, proj, proj, wba, wbb_heads)


def _branch_b_bwd(dyb, o_h, wbb_heads_t):
    nh, T, _ = o_h.shape
    tm = min(TOKEN_TILE, T)

    def body(dyb_ref, o_ref, w_ref, out_ref):
        do = jnp.dot(dyb_ref[...], w_ref[...], preferred_element_type=F32)
        lane = lax.broadcasted_iota(jnp.int32, (tm, LANES), 1)
        for h in range(nh):
            g = do[:, h * LANES:(h + 1) * LANES].astype(BF16).astype(F32)
            delta = jnp.sum(g * o_ref[h], axis=-1, keepdims=True)
            for idx, piece in _run(LANE_ONE, _pieces(delta)):
                g = jnp.where(lane == idx, -piece, g)
            out_ref[h] = g.astype(BF16)

    heads = pl.BlockSpec((nh, tm, LANES), lambda i: (0, i, 0))
    return pl.pallas_call(
        body, name="branch_b_bwd", grid=(T // tm,),
        in_specs=[pl.BlockSpec((tm, D), lambda i: (i, 0)), heads, _full((D, nh * LANES))],
        out_specs=heads, out_shape=jax.ShapeDtypeStruct((nh, T, LANES), BF16),
        compiler_params=_params(("parallel",)),
    )(dyb, o_h, wbb_heads_t)


def _branch_b_dw(o_h, dyb):
    nh, T, _ = o_h.shape
    tk = min(TOKEN_TILE, T)

    def body(o_ref, dyb_ref, out_ref):
        @pl.when(pl.program_id(0) == 0)
        def _():
            out_ref[...] = jnp.zeros_like(out_ref)

        g = dyb_ref[...]
        for h in range(nh):
            out_ref[h] += _tn_dot(o_ref[h].astype(BF16), g)

    return pl.pallas_call(
        body, name="branch_b_dw", grid=(T // tk,),
        in_specs=[pl.BlockSpec((nh, tk, LANES), lambda k: (0, k, 0)), pl.BlockSpec((tk, D), lambda k: (k, 0))],
        out_specs=_full((nh, LANES, D)), out_shape=jax.ShapeDtypeStruct((nh, LANES, D), F32),
        compiler_params=_params(("arbitrary",)),
    )(o_h, dyb)


def _attn_bwd(qb, ka, va, doa, exchange=None):
    nh, T, _ = qb.shape
    bk = min(ATTN_BLOCK, T)
    nk = T // bk

    def body(*refs):
        if exchange is None:
            q_ref, do_ref, k_ref, v_ref, dq_ref, dk_ref, dv_ref = refs
        else:
            (q_ref, do_ref, k_ref, v_ref), (dq_ref, dk_ref, dv_ref), _, xrefs = exchange.split(refs, 4, 3)
            _ride(exchange, *_grid_ends(nh, nk), xrefs)
        ki = pl.program_id(1)

        @pl.when(ki == 0)
        def _():
            dq_ref[...] = jnp.zeros_like(dq_ref)

        k = k_ref[0]
        v = v_ref[0]

        def step(qi, carry, masked):
            dk, dv = carry
            rows = pl.ds(pl.multiple_of(qi * bk, bk), bk)
            q = q_ref[0, rows, :]
            g = do_ref[0, rows, :]
            pt = jnp.exp(_nt_dot(k, q))
            if masked:
                pt = jnp.where(_causal(bk, keys_on_rows=True), pt, 0.0)
            dv = dv + jnp.dot(pt.astype(BF16), g, preferred_element_type=F32)
            dst = (pt * _nt_dot(v, g)).astype(BF16)
            dk = dk + jnp.dot(dst, q, preferred_element_type=F32)
            dq_ref[0, rows, :] += _tn_dot(dst, k)
            return dk, dv

        init = (jnp.zeros((bk, LANES), F32), jnp.zeros((bk, LANES), F32))
        carry = step(ki, init, True)
        dk_ref[0], dv_ref[0] = _sweep(ki + 1, nk, lambda qi, cr: step(qi, cr, False), carry)

    tile = pl.BlockSpec((1, bk, LANES), lambda h, i: (h, i, 0))
    whole = pl.BlockSpec((1, T, LANES), lambda h, i: (h, 0, 0))
    out_shape = [jax.ShapeDtypeStruct((nh, T, LANES), F32)] * 3
    if exchange is None:
        return pl.pallas_call(
            body, name="attn_bwd", grid=(nh, nk), in_specs=[whole, whole, tile, tile],
            out_specs=[whole, tile, tile], out_shape=out_shape, compiler_params=_params(("parallel", "arbitrary")),
        )(qb, doa, ka, va)
    return pl.pallas_call(
        body, name="attn_bwd", grid=(nh, nk), in_specs=[whole, whole, tile, tile] + exchange.in_specs,
        out_specs=[whole, tile, tile] + exchange.out_specs, out_shape=out_shape + exchange.out_shapes,
        scratch_shapes=exchange.scratch, compiler_params=_params(("arbitrary", "arbitrary")),
    )(qb, doa, ka, va, *exchange.xs)


def _attn_unpack(dq_h, dk_h, dv_h):
    nh, T, _ = dq_h.shape
    tm = min(TOKEN_TILE, T)

    def body(dq_ref, dk_ref, dv_ref, q_out, k_out, v_out, f_out):
        lane = lax.broadcasted_iota(jnp.int32, (tm, LANES), 1)
        low = lane < HEAD_DIM
        for src, dst in ((dq_ref, q_out), (dk_ref, k_out), (dv_ref, v_out)):
            for pair in range(nh // 2):
                both = jnp.where(low, src[2 * pair], pltpu.roll(src[2 * pair + 1], HEAD_DIM, axis=1))
                dst[:, LANES * pair:LANES * (pair + 1)] = both.astype(dst.dtype)
        df = jnp.zeros((tm, LANES), F32)
        for h in range(nh):
            col = dq_ref[h][:, LANE_F:LANE_F + 1] - dk_ref[h][:, LANE_SUM:LANE_SUM + 1]
            df = jnp.where(lane == h, col, df)
        f_out[...] = df

    heads = pl.BlockSpec((nh, tm, LANES), lambda i: (0, i, 0))
    tok = pl.BlockSpec((tm, ATTN_W), lambda i: (i, 0))
    return pl.pallas_call(
        body, name="attn_unpack", grid=(T // tm,), in_specs=[heads, heads, heads],
        out_specs=[tok, tok, tok, pl.BlockSpec((tm, F_PAD), lambda i: (i, 0))],
        out_shape=[jax.ShapeDtypeStruct((T, ATTN_W), F32), jax.ShapeDtypeStruct((T, ATTN_W), F32),
                   jax.ShapeDtypeStruct((T, ATTN_W), BF16), jax.ShapeDtypeStruct((T, F_PAD), F32)],
        compiler_params=_params(("parallel",)),
    )(dq_h, dk_h, dv_h)


def _ffn_act_bwd(u, da, w8):
    T = u.shape[0]
    tm = min(FFN_TM, T)
    nt = T // tm
    nc = D_FF // FFN_TC

    def body(ug_ref, uv_ref, ugp_ref, uvp_ref, ugn_ref, uvn_ref, da_ref, dan_ref, wg_ref, wv_ref,
             dug_ref, duv_ref, dwg_ref, dwv_ref):
        i = pl.program_id(1)

        @pl.when(i == 0)
        def _():
            dwg_ref[...] = jnp.zeros_like(dwg_ref)
            dwv_ref[...] = jnp.zeros_like(dwv_ref)

        first, last = i == 0, i == nt - 1
        wg, wv = wg_ref[...], wv_ref[...]
        zeros8 = jnp.zeros((SUBLANES, FFN_TC), F32)

        def window(u_ref, p_ref, n_ref, w):
            e = jnp.concatenate([jnp.where(first, 0.0, p_ref[...]), u_ref[...], n_ref[...]], axis=0)
            e1 = pltpu.roll(e, 1, axis=0)
            e2 = pltpu.roll(e, 2, axis=0)
            return e, e1, e2, e2 * w[0:1] + e1 * w[1:2] + e * w[2:3]

        eg, eg1, eg2, cg = window(ug_ref, ugp_ref, ugn_ref, wg)
        ev, ev1, ev2, cv = window(uv_ref, uvp_ref, uvn_ref, wv)
        dae = jnp.concatenate([zeros8, da_ref[...].astype(F32), jnp.where(last, 0.0, dan_ref[...].astype(F32))],
                              axis=0)
        sg = _sigmoid(cg)
        dgate = dae * cv * sg * (1.0 + cg * (1.0 - sg))
        dval = dae * cg * sg
        n = tm + 2 * SUBLANES

        def back(d, w):
            return d * w[2:3] + pltpu.roll(d, n - 1, axis=0) * w[1:2] + pltpu.roll(d, n - 2, axis=0) * w[0:1]

        inner = slice(SUBLANES, SUBLANES + tm)
        dug_ref[...] = back(dgate, wg)[inner].astype(BF16)
        duv_ref[...] = back(dval, wv)[inner].astype(BF16)

        def wgrad(d, e, e1, e2):
            rows = [jnp.sum((d * t)[inner], axis=0, keepdims=True) for t in (e2, e1, e)]
            return jnp.concatenate(rows + [jnp.zeros((SUBLANES - 3, FFN_TC), F32)], axis=0)

        dwg_ref[...] += wgrad(dgate, eg, eg1, eg2)
        dwv_ref[...] += wgrad(dval, ev, ev1, ev2)

    per = tm // SUBLANES
    last_blk = nt * per - 1
    blk = lambda off: pl.BlockSpec((tm, FFN_TC), lambda j, i: (i, j + off))
    prev = lambda off: pl.BlockSpec((SUBLANES, FFN_TC), lambda j, i: (jnp.maximum(i * per - 1, 0), j + off))
    nxt = lambda off: pl.BlockSpec((SUBLANES, FFN_TC), lambda j, i: (jnp.minimum((i + 1) * per, last_blk), j + off))
    wblk = lambda off: pl.BlockSpec((SUBLANES, FFN_TC), lambda j, i: (0, j + off))
    dug, duv, dwg, dwv = pl.pallas_call(
        body, name="ffn_act_bwd", grid=(nc, nt),
        in_specs=[blk(0), blk(nc), prev(0), prev(nc), nxt(0), nxt(nc), blk(0), nxt(0), wblk(0), wblk(nc)],
        out_specs=[blk(0), blk(0), wblk(0), wblk(0)],
        out_shape=[jax.ShapeDtypeStruct((T, D_FF), BF16)] * 2 + [jax.ShapeDtypeStruct((SUBLANES, D_FF), F32)] * 2,
        compiler_params=_params(("parallel", "arbitrary")),
    )(u, u, u, u, u, u, da, da, w8, w8)
    return jnp.concatenate([dug, duv], axis=1), jnp.concatenate([dwg, dwv], axis=1)


def _norm_bwd(xin, dh, dres, g, sc, *, name):
    T = xin.shape[0]
    tm = min(TOKEN_TILE, T)

    def body(x_ref, dh_ref, dr_ref, g_ref, sc_ref, dx_ref, dsh_ref, dsc_ref, dg_ref):
        i = pl.program_id(0)

        @pl.when(i == 0)
        def _():
            dsh_ref[...] = jnp.zeros_like(dsh_ref)
            dsc_ref[...] = jnp.zeros_like(dsc_ref)
            dg_ref[...] = jnp.zeros_like(dg_ref)

        xv = x_ref[...]
        dh = dh_ref[...]
        gv = g_ref[...]
        one_sc = 1.0 + sc_ref[...]
        inv = lax.rsqrt(jnp.mean(xv * xv, axis=-1, keepdims=True) + RMS_EPS)
        xn = xv * inv
        dxn = dh * (gv * one_sc)
        dx_ref[...] = dr_ref[...] + inv * (dxn - xn * jnp.mean(dxn * xn, axis=-1, keepdims=True))
        dhxn = dh * xn
        dsh_ref[...] += _rows8(dh)
        dsc_ref[...] += _rows8(dhxn * gv)
        dg_ref[...] += _rows8(dhxn * one_sc)

    row = pl.BlockSpec((tm, D), lambda i: (i, 0))
    acc = _full((SUBLANES, D))
    return pl.pallas_call(
        body, name=name, grid=(T // tm,),
        in_specs=[row, row, row, _full((1, D)), _full((1, D))], out_specs=[row, acc, acc, acc],
        out_shape=[jax.ShapeDtypeStruct((T, D), F32)] + [jax.ShapeDtypeStruct((SUBLANES, D), F32)] * 3,
        compiler_params=_params(("arbitrary",)),
    )(xin, dh, dres, g, sc)


def _gate_bwd(dx1, mix, g1):
    T = dx1.shape[0]
    tm = min(TOKEN_TILE, T)

    def body(dx_ref, mix_ref, g1_ref, dmix_ref, dg1_ref):
        @pl.when(pl.program_id(0) == 0)
        def _():
            dg1_ref[...] = jnp.zeros_like(dg1_ref)

        dx = dx_ref[...]
        dmix_ref[...] = (dx * g1_ref[...]).astype(BF16)
        dg1_ref[...] += _rows8(dx * mix_ref[...])

    row = pl.BlockSpec((tm, D), lambda i: (i, 0))
    return pl.pallas_call(
        body, name="gate_bwd", grid=(T // tm,),
        in_specs=[row, row, _full((1, D))], out_specs=[row, _full((SUBLANES, D))],
        out_shape=[jax.ShapeDtypeStruct((T, D), BF16), jax.ShapeDtypeStruct((SUBLANES, D), F32)],
        compiler_params=_params(("arbitrary",)),
    )(dx1, mix, g1)


def _merge_bwd(dmerged, ya, yb, proj):
    T = ya.shape[0]
    tm = min(TOKEN_TILE, T)

    def body(dm_ref, ya_ref, yb_ref, ga_ref, gb_ref, dya_ref, dyb_ref, dga_ref, dgb_ref):
        dm = dm_ref[...]
        sa = _sigmoid(ga_ref[...])
        sb = _sigmoid(gb_ref[...])
        dya_ref[...] = (dm * sa).astype(BF16)
        dyb_ref[...] = (dm * sb).astype(BF16)
        dga_ref[...] = (dm * ya_ref[...] * sa * (1.0 - sa)).astype(BF16)
        dgb_ref[...] = (dm * yb_ref[...] * sb * (1.0 - sb)).astype(BF16)

    row = pl.BlockSpec((tm, D), lambda i: (i, 0))
    return pl.pallas_call(
        body, name="merge_bwd", grid=(T // tm,),
        in_specs=[row, row, row, pl.BlockSpec((tm, D), lambda i: (i, COL_GA // D)),
                  pl.BlockSpec((tm, D), lambda i: (i, COL_GB // D))],
        out_specs=[row] * 4, out_shape=[jax.ShapeDtypeStruct((T, D), BF16)] * 4,
        compiler_params=_params(("parallel",)),
    )(dmerged, ya, yb, proj, proj)


def _conv_branch_bwd(proj, dya0, conv_w8):
    T = proj.shape[0]
    tm = min(FFN_TM, T)
    nt = T // tm

    def body(cb_ref, cc_ref, cv_ref, cbn_ref, ccp_ref, cvp_ref, ccn_ref, cvn_ref, d_ref, dn_ref, w_ref,
             dcb_ref, dcc_ref, dcv_ref, dw_ref):
        i = pl.program_id(0)

        @pl.when(i == 0)
        def _():
            dw_ref[...] = jnp.zeros_like(dw_ref)

        first, last = i == 0, i == nt - 1
        w = w_ref[...]
        cc = jnp.concatenate([ccp_ref[...], cc_ref[...], ccn_ref[...]], axis=0)
        cv = jnp.concatenate([cvp_ref[...], cv_ref[...], cvn_ref[...]], axis=0)
        rows = lax.broadcasted_iota(jnp.int32, cc.shape, 0)
        z = jnp.where(jnp.logical_and(first, rows < SUBLANES), 0.0, cc * cv)
        z1 = pltpu.roll(z, 1, axis=0)
        z2 = pltpu.roll(z, 2, axis=0)
        cz = z2 * w[0:1] + z1 * w[1:2] + z * w[2:3]
        zeros8 = jnp.zeros((SUBLANES, CONV_W), F32)
        de = jnp.concatenate([zeros8, d_ref[...], jnp.where(last, 0.0, dn_ref[...])], axis=0)
        cbe = jnp.concatenate([zeros8, cb_ref[...], cbn_ref[...]], axis=0)
        dcz = de * cbe
        n = tm + 2 * SUBLANES
        dz = dcz * w[2:3] + pltpu.roll(dcz, n - 1, axis=0) * w[1:2] + pltpu.roll(dcz, n - 2, axis=0) * w[0:1]
        inner = slice(SUBLANES, SUBLANES + tm)
        dcb_ref[...] = (de * cz)[inner].astype(BF16)
        dcc_ref[...] = (dz * cv)[inner].astype(BF16)
        dcv_ref[...] = (dz * cc)[inner].astype(BF16)
        wrows = [jnp.sum((dcz * t)[inner], axis=0, keepdims=True) for t in (z2, z1, z)]
        dw_ref[...] += jnp.concatenate(wrows + [jnp.zeros((SUBLANES - 3, CONV_W), F32)], axis=0)

    blk = lambda col: pl.BlockSpec((tm, CONV_W), lambda i: (i, col))
    out_blk = pl.BlockSpec((tm, CONV_W), lambda i: (i, 0))
    return pl.pallas_call(
        body, name="conv_branch_bwd", grid=(nt,),
        in_specs=[blk(0), blk(1), blk(2), _next_spec(tm, CONV_W, 0, nt),
                  _prev_spec(tm, CONV_W, 1), _prev_spec(tm, CONV_W, 2),
                  _next_spec(tm, CONV_W, 1, nt), _next_spec(tm, CONV_W, 2, nt),
                  out_blk, _next_spec(tm, CONV_W, 0, nt), _full((SUBLANES, CONV_W))],
        out_specs=[out_blk, out_blk, out_blk, _full((SUBLANES, CONV_W))],
        out_shape=[jax.ShapeDtypeStruct((T, CONV_W), BF16)] * 3 + [jax.ShapeDtypeStruct((SUBLANES, CONV_W), F32)],
        compiler_params=_params(("arbitrary",)),
    )(proj, proj, proj, proj, proj, proj, proj, proj, dya0, dya0, conv_w8)


def _qk_norm_bwd(proj, dqs, dkh, dlogf, qg, kg, bf_pad, gmat):
    T = proj.shape[0]
    tm = min(TOKEN_TILE, T)

    def body(q_ref, k_ref, f_ref, dqs_ref, dkh_ref, dlf_ref, qg_ref, kg_ref, bf_ref, g_ref,
             dq_ref, dk_ref, dfl_ref, dqg_ref, dkg_ref, dbf_ref):
        @pl.when(pl.program_id(0) == 0)
        def _():
            dqg_ref[...] = jnp.zeros_like(dqg_ref)
            dkg_ref[...] = jnp.zeros_like(dkg_ref)
            dbf_ref[...] = jnp.zeros_like(dbf_ref)

        gm = g_ref[...]
        for src, d_src, gain, scale, dst, dgain in (
                (q_ref, dqs_ref, qg_ref, 1.0 / np.sqrt(HEAD_DIM), dq_ref, dqg_ref),
                (k_ref, dkh_ref, kg_ref, 1.0, dk_ref, dkg_ref)):
            v = src[...]
            dhat = d_src[...] * scale
            inv = lax.rsqrt(_split_dot(v * v, gm, 2) * (1.0 / HEAD_DIM) + RMS_EPS)
            vn = v * inv
            dgain[...] += _rows8(dhat * vn)
            dvn = dhat * gain[...]
            mean = _split_dot(dvn * vn, gm, 2) * (1.0 / HEAD_DIM)
            dst[...] = (inv * (dvn - vn * mean)).astype(BF16)
        fx = f_ref[...] + bf_ref[...]
        dfl = dlf_ref[...] * _sigmoid(-fx)
        dfl_ref[...] = dfl.astype(BF16)
        dbf_ref[...] += _rows8(dfl)

    blk = lambda col: pl.BlockSpec((tm, ATTN_W), lambda i: (i, col))
    out_blk = pl.BlockSpec((tm, ATTN_W), lambda i: (i, 0))
    f_in = pl.BlockSpec((tm, F_PAD), lambda i: (i, COL_F // F_PAD))
    f_blk = pl.BlockSpec((tm, F_PAD), lambda i: (i, 0))
    return pl.pallas_call(
        body, name="qk_norm_bwd", grid=(T // tm,),
        in_specs=[blk(3), blk(4), f_in, out_blk, out_blk, f_blk, _full((1, ATTN_W)), _full((1, ATTN_W)),
                  _full((1, F_PAD)), _full((ATTN_W, ATTN_W))],
        out_specs=[out_blk, out_blk, f_blk, _full((SUBLANES, ATTN_W)), _full((SUBLANES, ATTN_W)),
                   _full((SUBLANES, F_PAD))],
        out_shape=[jax.ShapeDtypeStruct((T, ATTN_W), BF16)] * 2 + [jax.ShapeDtypeStruct((T, F_PAD), BF16)]
        + [jax.ShapeDtypeStruct((SUBLANES, ATTN_W), F32)] * 2 + [jax.ShapeDtypeStruct((SUBLANES, F_PAD), F32)],
        compiler_params=_params(("arbitrary",)),
    )(proj, proj, proj, dqs, dkh, dlogf, qg, kg, bf_pad, gmat)


def _pad_rows8(w):
    return jnp.pad(w, ((0, SUBLANES - w.shape[0]), (0, 0)))


def _fold8(acc):
    return jnp.sum(acc, axis=0, keepdims=True)


def _late_weights(mats):
    out = {}
    for name in ("w_branch_a", "w_out", "w_up", "w_down"):
        out[name] = mats[name]
        out[name + "_t"] = mats[name].T
    out["w_branch_b_heads"] = _pad_head_rows(mats["w_branch_b"])
    out["w_branch_b_heads_t"] = out["w_branch_b_heads"].T
    return out


def _local_step(x, target, mod, wts, late=None):
    T = x.shape[0]
    tm = min(TOKEN_TILE, T)
    sh1, sc1, g1, sh2, sc2, g2 = [mod[:, i * D:(i + 1) * D] for i in range(N_MOD)]
    w_in, w_in_t = wts["w_in"], wts["w_in_t"]
    conv_a8 = _pad_rows8(wts["conv_a_w"])
    conv_f8 = _pad_rows8(wts["conv_ffn_w"])
    qg = jnp.tile(wts["q_norm_g"], (1, N_HEADS))
    kg = jnp.tile(wts["k_norm_g"], (1, N_HEADS))
    bf_pad = jnp.pad(wts["b_f"], ((0, 0), (0, F_PAD - N_HEADS)))
    gmat = _group_matrix()

    h = _norm_mod(x, wts["norm1_g"], sc1, sh1, name="norm1_fwd")
    proj = _matmul(h, w_in, name="mm_in", tm=tm, tn=896, tk=D)
    fcum = _cumsum(proj, reverse=False, name="gate_cumsum", col=COL_F // F_PAD, gate_bias=bf_pad)
    ya0, qa, ka, va = _branch_prep(proj, fcum, conv_a8, qg, kg, gmat)
    if late is None:
        o_h, qb = _attn_fwd(qa, ka, va)
    else:
        o_h, qb, *gathered = _attn_fwd(qa, ka, va, _Exchange([late[name] for name, *_ in LATE], scatter=False))
        wts = dict(wts)
        wts.update(_late_weights({name: _join_shards(g, axis) for (name, _, _, axis), g in zip(LATE, gathered)}))
    ya, yb, merged = _branch_merge_fwd(ya0, o_h, proj, wts["w_branch_a"], wts["w_branch_b_heads"])
    mix = _matmul(merged, wts["w_out"], name="mm_out", tm=tm, tn=D, tk=D)
    x1, h2 = _resid_norm(x, mix, g1, wts["norm2_g"], sc2, sh2)
    u = _matmul(h2, wts["w_up"], name="mm_up", tm=tm, tn=1408, tk=D)
    act = _ffn_act_fwd(u, conv_f8)
    ff = _matmul(act, wts["w_down"], name="mm_down", tm=tm, tn=D, tk=D_FF)
    dy, dff, sq8, dg2_8 = _loss_head(x1, ff, g2, target)
    sq = jnp.sum(sq8).reshape(1, 1)

    grads = {}
    da = _matmul(dff, wts["w_down_t"], name="mm_down_dx", tm=tm, tn=1408, tk=D)
    grads["w_down"] = _matmul(act, dff, name="mm_down_dw", tm=1408, tn=D, tk=tm, trans_a=True)
    du, dconv_f8 = _ffn_act_bwd(u, da, conv_f8)
    grads["conv_ffn_w"] = dconv_f8[:3]
    dh2 = _matmul(du, wts["w_up_t"], name="mm_up_dx", tm=tm, tn=D, tk=1408)
    grads["w_up"] = _matmul(h2, du, name="mm_up_dw", tm=D, tn=1408, tk=tm, trans_a=True)
    dx1, dsh2_8, dsc2_8, dn2_8 = _norm_bwd(x1, dh2, dy, wts["norm2_g"], sc2, name="norm2_bwd")
    grads["norm2_g"] = _fold8(dn2_8)

    dmix, dg1_8 = _gate_bwd(dx1, mix, g1)
    dmerged = _matmul(dmix, wts["w_out_t"], name="mm_out_dx", tm=tm, tn=D, tk=D)
    grads["w_out"] = _matmul(merged, dmix, name="mm_out_dw", tm=D, tn=D, tk=tm, trans_a=True)
    dya, dyb, dga, dgb = _merge_bwd(dmerged, ya, yb, proj)
    dya0 = _matmul(dya, wts["w_branch_a_t"], name="mm_branch_a_dx", tm=tm, tn=CONV_W, tk=D)
    grads["w_branch_a"] = _matmul(ya0, dya, name="mm_branch_a_dw", tm=CONV_W, tn=D, tk=tm, trans_a=True)
    doa = _branch_b_bwd(dyb, o_h, wts["w_branch_b_heads_t"])
    grads["w_branch_b"] = _branch_b_dw(o_h, dyb)[:, :HEAD_DIM].reshape(ATTN_W, D)
    dcb, dcc, dcv, dconv_a8 = _conv_branch_bwd(proj, dya0, conv_a8)
    grads["conv_a_w"] = dconv_a8[:3]

    parts = {}
    if late is None:
        dq_h, dk_h, dv_h = _attn_bwd(qb, ka, va, doa)
    else:
        ready = [_split_shards(grads[name], axis).astype(BF16) for name, _, _, axis in LATE]
        dq_h, dk_h, dv_h, *recv = _attn_bwd(
            qb, ka, va, doa, _Exchange(ready + [_pack_full_by_dest(grads, CONVS, SUBLANES)], scatter=True))
        parts = dict(zip([name for name, *_ in LATE] + ["conv"], recv))
    dq_tok, dk_tok, dv_tok, dfcum = _attn_unpack(dq_h, dk_h, dv_h)
    dlogf = _cumsum(dfcum, reverse=True, name="gate_cumsum_bwd")
    dq, dk, dfl, dqg8, dkg8, dbf8 = _qk_norm_bwd(proj, dq_tok, dk_tok, dlogf, qg, kg, bf_pad, gmat)
    grads["q_norm_g"] = jnp.sum(_fold8(dqg8).reshape(N_HEADS, HEAD_DIM), axis=0, keepdims=True)
    grads["k_norm_g"] = jnp.sum(_fold8(dkg8).reshape(N_HEADS, HEAD_DIM), axis=0, keepdims=True)
    grads["b_f"] = _fold8(dbf8)[:, :N_HEADS]
    dproj = jnp.concatenate(
        [dcb, dcc, dcv, dq, dk, dv_tok, dga, dgb, dfl, jnp.zeros((T, IN_W_PAD - COL_F - F_PAD), BF16)], axis=1)
    grads["w_in"] = _unpad_in(_matmul(h, dproj, name="mm_in_dw", tm=D, tn=896, tk=tm, trans_a=True))
    if late is None:
        dh = _matmul(dproj, w_in_t, name="mm_in_dx", tm=tm, tn=D, tk=896)
    else:
        dh, parts["w_in"] = _matmul(dproj, w_in_t, name="mm_in_dx", tm=tm, tn=D, tk=896,
                                    exchange=_Exchange([_split_shards(grads["w_in"], 1).astype(BF16)], scatter=True))
    grad_x, dsh1_8, dsc1_8, dn1_8 = _norm_bwd(x, dh, dx1, wts["norm1_g"], sc1, name="norm1_bwd")
    grads["norm1_g"] = _fold8(dn1_8)
    grads["mod"] = jnp.concatenate([_fold8(a) for a in (dsh1_8, dsc1_8, dg1_8, dsh2_8, dsc2_8, dg2_8)], axis=1)
    return sq, grad_x, grads, parts


def _me_and_peers():
    mx, my, mc = lax.axis_index("x"), lax.axis_index("y"), lax.axis_index("c")
    me = 4 * mx + 2 * my + mc
    peers = []
    for k in range(1, N_DEV):
        px = 1 - mx if k & 4 else mx
        py = 1 - my if k & 2 else my
        pc = 1 - mc if k & 1 else mc
        peers.append(((px, py, pc), 4 * px + 2 * py + pc))
    return me, peers


HBM_SPEC = pl.BlockSpec(memory_space=pltpu.HBM)


class _Exchange:
    def __init__(self, xs, scatter):
        self.xs, self.scatter, self.n = list(xs), scatter, len(xs)
        self.out_shapes = [jax.ShapeDtypeStruct(x.shape if scatter else (N_DEV,) + x.shape, x.dtype) for x in xs]
        self.in_specs = [HBM_SPEC] * self.n
        self.out_specs = [HBM_SPEC] * self.n
        self.scratch = [pltpu.SemaphoreType.DMA((self.n, N_DEV - 1)), pltpu.SemaphoreType.DMA((self.n, N_DEV - 1)),
                        pltpu.SemaphoreType.DMA((self.n,))]

    def _copies(self, x_refs, out_refs, sems):
        send_sems, recv_sems, local_sems = sems
        me, peers = _me_and_peers()

        def src(a, idx):
            return x_refs[a].at[idx] if self.scatter else x_refs[a]

        def copy(a, k, from_idx, to_slot, device):
            return pltpu.make_async_remote_copy(
                src_ref=src(a, from_idx), dst_ref=out_refs[a].at[to_slot], send_sem=send_sems.at[a, k],
                recv_sem=recv_sems.at[a, k], device_id=device, device_id_type=MESH)

        local = [pltpu.make_async_copy(src(a, me), out_refs[a].at[me], local_sems.at[a]) for a in range(self.n)]
        sends = [copy(a, k, idx, me, dev) for a in range(self.n) for k, (dev, idx) in enumerate(peers)]
        recvs = [copy(a, k, idx, idx, dev) for a in range(self.n) for k, (dev, idx) in enumerate(peers)]
        return local, sends, recvs

    def start(self, x_refs, out_refs, sems):
        local, sends, _ = self._copies(x_refs, out_refs, sems)
        for cp in local + sends:
            cp.start()

    def wait(self, x_refs, out_refs, sems):
        local, sends, recvs = self._copies(x_refs, out_refs, sems)
        for cp in recvs:
            cp.wait_recv()
        for cp in sends:
            cp.wait_send()
        for cp in local:
            cp.wait()

    def split(self, refs, n_in, n_out):
        n = self.n
        ins, xin = refs[:n_in], refs[n_in:n_in + n]
        outs, xout = refs[n_in + n:n_in + n + n_out], refs[n_in + n + n_out:n_in + 2 * n + n_out]
        rest = refs[n_in + 2 * n + n_out:]
        return ins, outs, rest[:len(rest) - 3], (xin, xout, rest[len(rest) - 3:])


def _ride(exchange, first, last, refs):
    if exchange is None:
        return

    @pl.when(first)
    def _():
        exchange.start(*refs)

    @pl.when(last)
    def _():
        exchange.wait(*refs)


def _exchange(xs, *, name, scatter):
    ex = _Exchange(xs, scatter)

    def body(*refs):
        _, _, _, xrefs = ex.split(refs, 0, 0)
        ex.start(*xrefs)
        ex.wait(*xrefs)

    return pl.pallas_call(
        body, name=name, in_specs=ex.in_specs, out_specs=ex.out_specs, out_shape=ex.out_shapes,
        scratch_shapes=ex.scratch, compiler_params=pltpu.CompilerParams(has_side_effects=True),
    )(*xs)


def _ada_fwd(c_all, w_shard, b_shard):
    n = w_shard.shape[1]

    def body(c_ref, w_ref, b_ref, o_ref):
        cv = c_ref[...]
        act = (cv * _sigmoid(cv)).astype(BF16)
        o_ref[...] = jnp.dot(act, w_ref[...].astype(BF16), preferred_element_type=F32) + b_ref[...]

    return pl.pallas_call(
        body, name="ada_fwd", in_specs=[_full((N_DEV, D)), _full((D, n)), _full((1, n))],
        out_specs=_full((N_DEV, n)), out_shape=jax.ShapeDtypeStruct((N_DEV, n), F32), grid=(1,),
        compiler_params=_params(("arbitrary",)),
    )(c_all, w_shard, b_shard)


def _ada_bwd(c_all_t, dmod_pad):
    n = dmod_pad.shape[1]

    def body(c_ref, d_ref, o_ref):
        cv = c_ref[...]
        act = (cv * _sigmoid(cv)).astype(BF16)
        o_ref[...] = jnp.dot(act, d_ref[...].astype(BF16), preferred_element_type=F32)

    return pl.pallas_call(
        body, name="ada_bwd", in_specs=[_full((D, LANES)), _full((LANES, n))],
        out_specs=_full((D, n)), out_shape=jax.ShapeDtypeStruct((D, n), F32), grid=(1,),
        compiler_params=_params(("arbitrary",)),
    )(c_all_t, dmod_pad)


ADAM_ROWS = 64


def _adamw(parts, w, m, v, *, name):
    n, R, C = parts.shape
    tr = next((t for t in (ADAM_ROWS, 32, 16, SUBLANES) if R % t == 0), R)

    def body(p_ref, w_ref, m_ref, v_ref, g_ref, d_ref, nm_ref, nv_ref):
        g = p_ref[0].astype(F32)
        for j in range(1, n):
            g = g + p_ref[j].astype(F32)
        g_ref[...] = g
        nm = ADAM_B1 * m_ref[...] + (1.0 - ADAM_B1) * g
        nv = ADAM_B2 * v_ref[...] + (1.0 - ADAM_B2) * (g * g)
        nm_ref[...] = nm
        nv_ref[...] = nv
        m_hat = nm / (1.0 - ADAM_B1 ** ADAM_STEP)
        v_hat = nv / (1.0 - ADAM_B2 ** ADAM_STEP)
        d_ref[...] = -ADAM_LR * (m_hat / (jnp.sqrt(v_hat) + ADAM_EPS) + ADAM_WD * w_ref[...])

    row = pl.BlockSpec((tr, C), lambda i: (i, 0))
    return pl.pallas_call(
        body, name=name, grid=(R // tr,),
        in_specs=[pl.BlockSpec((n, tr, C), lambda i: (0, i, 0)), row, row, row], out_specs=[row] * 4,
        out_shape=[jax.ShapeDtypeStruct((R, C), F32)] * 4,
        compiler_params=_params(("parallel",)),
    )(parts, w, m, v)


SHARDED = (("w_in", D, IN_W, 1), ("w_branch_a", CONV_W, D, 1), ("w_branch_b", ATTN_W, D, 1), ("w_out", D, D, 0),
           ("w_up", D, 2 * D_FF, 1), ("w_down", D_FF, D, 0), ("conv_a_w", 3, CONV_W, 1),
           ("conv_ffn_w", 3, 2 * D_FF, 1))
MATRICES = SHARDED[:6]
LATE = MATRICES[1:]
CONVS = SHARDED[6:]
REPLICATED = (("b_ada", N_MOD * D), ("norm1_g", D), ("norm2_g", D), ("b_f", N_HEADS), ("q_norm_g", HEAD_DIM),
              ("k_norm_g", HEAD_DIM))


def _shard_shape(rows, cols, axis):
    return (rows // N_DEV, cols) if axis == 0 else (rows, cols // N_DEV)


def _pack_rows(flat, multiple):
    length = flat.shape[-1]
    rows = -(-length // PACK_W)
    rows = -(-rows // multiple) * multiple
    pad = [(0, 0)] * (flat.ndim - 1) + [(0, rows * PACK_W - length)]
    return jnp.pad(flat, pad).reshape(flat.shape[:-1] + (rows, PACK_W))


def _pack_shards(shards, spec, multiple, dtype):
    flat = jnp.concatenate([shards[name].reshape(-1).astype(dtype) for name, *_ in spec])
    return _pack_rows(flat, multiple)


def _join_shards(gathered, axis):
    if axis == 0:
        return gathered.reshape(N_DEV * gathered.shape[1], gathered.shape[2])
    return jnp.concatenate([gathered[j] for j in range(N_DEV)], axis=1)


def _split_shards(full, axis):
    if axis == 0:
        return full.reshape(N_DEV, full.shape[0] // N_DEV, full.shape[1])
    c = full.shape[1] // N_DEV
    return jnp.stack([full[:, j * c:(j + 1) * c] for j in range(N_DEV)])


def _unpack_shards(packed, spec):
    flat = packed.reshape(-1)
    out, off = {}, 0
    for name, rows, cols, axis in spec:
        r, c = _shard_shape(rows, cols, axis)
        out[name] = flat[off:off + r * c].reshape(r, c)
        off += r * c
    return out


def _unpack_gathered(gathered, spec):
    flat = gathered.reshape(N_DEV, -1)
    out, off = {}, 0
    for name, rows, cols, axis in spec:
        r, c = _shard_shape(rows, cols, axis)
        seg = flat[:, off:off + r * c].reshape(N_DEV, r, c)
        out[name] = seg.reshape(rows, cols) if axis == 0 else seg.transpose(1, 0, 2).reshape(rows, cols)
        off += r * c
    return out


def _pack_full_by_dest(full, spec, multiple):
    segs = []
    for name, rows, cols, axis in spec:
        r, c = _shard_shape(rows, cols, axis)
        a = full[name]
        seg = a.reshape(N_DEV, r, c) if axis == 0 else a.reshape(rows, N_DEV, c).transpose(1, 0, 2)
        segs.append(seg.reshape(N_DEV, r * c))
    return _pack_rows(jnp.concatenate(segs, axis=1), multiple)


def _pad_in(w_in):
    return jnp.concatenate([w_in[:, :COL_GA], w_in[:, COL_GA + N_HEADS:], w_in[:, COL_GA:COL_GA + N_HEADS],
                            jnp.zeros((w_in.shape[0], IN_W_PAD - IN_W), w_in.dtype)], axis=1)


def _pad_head_rows(w):
    n = w.shape[1]
    padded = jnp.pad(w.reshape(N_HEADS, HEAD_DIM, n), ((0, 0), (0, LANES - HEAD_DIM), (0, 0)))
    return padded.reshape(N_HEADS * LANES, n)


def _unpad_in(g):
    return jnp.concatenate([g[:, :COL_GA], g[:, COL_F:COL_F + N_HEADS], g[:, COL_GA:COL_F]], axis=1)


def kernel(x, c, w_ada, b_ada, norm1_g, w_in, b_f, conv_a_w, q_norm_g, k_norm_g, w_branch_a, w_branch_b, w_out, norm2_g, w_up, conv_ffn_w, w_down, loss_target, m_w_ada, m_b_ada, m_norm1_g, m_w_in, m_b_f, m_conv_a_w, m_q_norm_g, m_k_norm_g, m_w_branch_a, m_w_branch_b, m_w_out, m_norm2_g, m_w_up, m_conv_ffn_w, m_w_down, v_w_ada, v_b_ada, v_norm1_g, v_w_in, v_b_f, v_conv_a_w, v_q_norm_g, v_k_norm_g, v_w_branch_a, v_w_branch_b, v_w_out, v_norm2_g, v_w_up, v_conv_ffn_w, v_w_down):
    names = ("w_ada", "b_ada", "norm1_g", "w_in", "b_f", "conv_a_w", "q_norm_g", "k_norm_g", "w_branch_a",
             "w_branch_b", "w_out", "norm2_g", "w_up", "conv_ffn_w", "w_down")
    squeeze = lambda a: a[0] if a.ndim == 3 else a
    W = dict(zip(names, map(squeeze, (w_ada, b_ada, norm1_g, w_in, b_f, conv_a_w, q_norm_g, k_norm_g, w_branch_a,
                                      w_branch_b, w_out, norm2_g, w_up, conv_ffn_w, w_down))))
    M = dict(zip(names, map(squeeze, (m_w_ada, m_b_ada, m_norm1_g, m_w_in, m_b_f, m_conv_a_w, m_q_norm_g,
                                      m_k_norm_g, m_w_branch_a, m_w_branch_b, m_w_out, m_norm2_g, m_w_up,
                                      m_conv_ffn_w, m_w_down))))
    V = dict(zip(names, map(squeeze, (v_w_ada, v_b_ada, v_norm1_g, v_w_in, v_b_f, v_conv_a_w, v_q_norm_g,
                                      v_k_norm_g, v_w_branch_a, v_w_branch_b, v_w_out, v_norm2_g, v_w_up,
                                      v_conv_ffn_w, v_w_down))))
    me = 4 * lax.axis_index("x") + 2 * lax.axis_index("y") + lax.axis_index("c")
    ada_n = N_MOD * D // N_DEV

    small = jnp.concatenate([c.reshape(-1), W["conv_a_w"].reshape(-1), W["conv_ffn_w"].reshape(-1)])
    small_all, w_in_all = _exchange([_pack_rows(small, SUBLANES), W["w_in"].astype(BF16)], name="gather_first",
                                    scatter=False)
    small_all = small_all.reshape(N_DEV, -1)
    c_all = small_all[:, :D]
    conv_all = _unpack_gathered(small_all[:, D:], CONVS)

    b_shard = lax.dynamic_slice(W["b_ada"], (0, me * ada_n), (1, ada_n))
    mod_part = _ada_fwd(c_all, W["w_ada"], b_shard)
    mod_all, = _exchange([mod_part], name="gather_mod", scatter=False)
    mod = lax.dynamic_index_in_dim(mod_all, me, axis=1, keepdims=False).reshape(1, N_MOD * D)

    wts = {"w_in": _pad_in(_join_shards(w_in_all, 1))}
    wts["w_in_t"] = wts["w_in"].T
    wts.update(conv_all)
    for name in ("norm1_g", "norm2_g", "q_norm_g", "k_norm_g", "b_f"):
        wts[name] = W[name]
    late = {name: W[name].astype(BF16) for name, *_ in LATE}

    sq, grad_x, grads, parts = _local_step(x[0], loss_target[0], mod, wts, late)
    loss = lax.psum(sq[0, 0] * (0.5 / D), AXES)

    grads["b_ada"] = grads["mod"]
    rep_flat = lambda src: jnp.concatenate([src[name].reshape(-1) for name, _ in REPLICATED])
    rep_parts, = _exchange([_pack_rows(rep_flat(grads), 16)], name="gather_small_grads", scatter=False)
    rep_out = _adamw(rep_parts, *[_pack_rows(rep_flat(s), 16) for s in (W, M, V)], name="adamw_replicated")

    dmod_all = rep_parts.reshape(N_DEV, -1)[:, :N_MOD * D]
    dmod_mine = lax.dynamic_slice(dmod_all, (0, me * ada_n), (N_DEV, ada_n))
    g_ada = _ada_bwd(jnp.pad(c_all.T, ((0, 0), (0, LANES - N_DEV))),
                     jnp.pad(dmod_mine, ((0, LANES - N_DEV), (0, 0))))
    ada_out = _adamw(g_ada[None], W["w_ada"], M["w_ada"], V["w_ada"], name="adamw_ada")

    mat_out = {name: _adamw(parts[name], W[name], M[name], V[name], name="adamw_" + name) for name, *_ in MATRICES}
    conv_out = _adamw(parts["conv"], *[_pack_shards(s, CONVS, SUBLANES, F32) for s in (W, M, V)],
                      name="adamw_conv")

    results = []
    for kind in range(4):
        per = {"w_ada": ada_out[kind]}
        per.update({name: out[kind] for name, out in mat_out.items()})
        per.update(_unpack_shards(conv_out[kind], CONVS))
        flat, off = rep_out[kind].reshape(-1), 0
        for name, n in REPLICATED:
            per[name] = flat[off:off + n].reshape(1, n)
            off += n
        results.append(per)
    restore = lambda name, a: a[None] if W[name].ndim == 2 and name not in dict(REPLICATED) else a
    outs = [loss, grad_x[None]]
    for per in results:
        outs.extend(restore(name, per[name]) for name in names)
    return tuple(outs)
```

```python
import functools

import jax
import jax.numpy as jnp
import numpy as np
from jax import lax
from jax.experimental import pallas as pl
from jax.experimental.pallas import tpu as pltpu

F32 = jnp.float32
BF16 = jnp.bfloat16

N_DEV = 8
D = 1024
N_HEADS = 8
HEAD_DIM = 64
ATTN_W = 512
CONV_W = 512
D_FF = 2816
N_MOD = 6
IN_W = 5128
RMS_EPS = 1e-6
NEG_INF = -1e30

IN_W_PAD = 5376
COL_GA = 3072
COL_GB = 4096
COL_F = 5120
F_PAD = 128

ADAM_LR = 0.001
ADAM_B1 = 0.9
ADAM_B2 = 0.999
ADAM_EPS = 1e-08
ADAM_WD = 0.01
ADAM_STEP = 10

LANES = 128
SUBLANES = 8
VMEM_LIMIT = 52 * 1024 * 1024
TOKEN_TILE = 512
MATMUL_TILE = 1024
ATTN_BLOCK = 512
PACK_W = 1024

MESH = pl.DeviceIdType.MESH
AXES = ("x", "y", "c")


def _params(sem=None, **kw):
    return pltpu.CompilerParams(dimension_semantics=sem, vmem_limit_bytes=VMEM_LIMIT, **kw)


def _full(shape):
    nd = len(shape)
    return pl.BlockSpec(shape, lambda *_: (0,) * nd)


def _tn_dot(a, b):
    return lax.dot_general(a, b, (((0,), (0,)), ((), ())), preferred_element_type=F32)


def _matmul(a, b, *, name, tm, tn, tk, out_dtype=F32, trans_a=False, exchange=None):
    if trans_a:
        K, M = a.shape
    else:
        M, K = a.shape
    N = b.shape[1]
    assert b.shape[0] == K and M % tm == 0 and N % tn == 0 and K % tk == 0, (name, a.shape, b.shape)
    nm, nn, nk = M // tm, N // tn, K // tk

    def body(*refs):
        if exchange is None:
            a_ref, b_ref, o_ref, *own = refs
        else:
            (a_ref, b_ref), (o_ref,), own, xrefs = exchange.split(refs, 2, 1)
            ids = [pl.program_id(d) for d in range(3)]
            first = jnp.logical_and(jnp.logical_and(ids[0] == 0, ids[1] == 0), ids[2] == 0)
            last = jnp.logical_and(jnp.logical_and(ids[0] == nn - 1, ids[1] == nm - 1), ids[2] == nk - 1)
            _ride(exchange, first, last, xrefs)
        k = pl.program_id(2)
        av = a_ref[...].astype(BF16)
        bv = b_ref[...].astype(BF16)
        prod = _tn_dot(av, bv) if trans_a else jnp.dot(av, bv, preferred_element_type=F32)
        if nk == 1:
            o_ref[...] = prod.astype(out_dtype)
            return
        acc_ref, = own

        @pl.when(k == 0)
        def _():
            acc_ref[...] = prod

        @pl.when(k > 0)
        def _():
            acc_ref[...] += prod

        @pl.when(k == nk - 1)
        def _():
            o_ref[...] = acc_ref[...].astype(out_dtype)

    if trans_a:
        a_spec = pl.BlockSpec((tk, tm), lambda j, i, k: (k, i))
    else:
        a_spec = pl.BlockSpec((tm, tk), lambda j, i, k: (i, k))
    in_specs = [a_spec, pl.BlockSpec((tk, tn), lambda j, i, k: (k, j))]
    out_spec = pl.BlockSpec((tm, tn), lambda j, i, k: (i, j))
    out_shape = jax.ShapeDtypeStruct((M, N), out_dtype)
    scratch = [pltpu.VMEM((tm, tn), F32)] if nk > 1 else []
    if exchange is None:
        return pl.pallas_call(
            body, name=name, grid=(nn, nm, nk), in_specs=in_specs, out_specs=out_spec, out_shape=out_shape,
            scratch_shapes=scratch, compiler_params=_params(("parallel", "parallel", "arbitrary")),
        )(a, b)
    return pl.pallas_call(
        body, name=name, grid=(nn, nm, nk), in_specs=in_specs + exchange.in_specs,
        out_specs=[out_spec] + exchange.out_specs, out_shape=[out_shape] + exchange.out_shapes,
        scratch_shapes=scratch + exchange.scratch, compiler_params=_params(("arbitrary",) * 3),
    )(a, b, *exchange.xs)


def _split_dot(x, mat, parts):
    out = None
    rem = x
    for p in range(parts):
        piece = rem.astype(BF16)
        term = jnp.dot(piece, mat, preferred_element_type=F32)
        out = term if out is None else out + term
        if p + 1 < parts:
            rem = rem - piece.astype(F32)
    return out


def _sigmoid(x):
    return 1.0 / (1.0 + jnp.exp(-x))


def _rows8(x):
    r, c = x.shape
    return jnp.sum(x.reshape(r // SUBLANES, SUBLANES, c), axis=0)


def _shift_down(blk, prev8, n):
    rolled = pltpu.roll(blk, n, axis=0)
    prev_rolled = pltpu.roll(prev8, n, axis=0)
    rows = lax.broadcasted_iota(jnp.int32, prev8.shape, 0)
    first = jnp.where(rows < n, prev_rolled, rolled[0:SUBLANES])
    return jnp.concatenate([first, rolled[SUBLANES:]], axis=0)


def _shift_up(blk, next8, n):
    r = blk.shape[0]
    rolled = pltpu.roll(blk, r - n, axis=0)
    next_rolled = pltpu.roll(next8, SUBLANES - n, axis=0)
    rows = lax.broadcasted_iota(jnp.int32, next8.shape, 0)
    last = jnp.where(rows >= SUBLANES - n, next_rolled, rolled[r - SUBLANES:])
    return jnp.concatenate([rolled[:r - SUBLANES], last], axis=0)


def _prev_spec(tm, width, col):
    per = tm // SUBLANES
    return pl.BlockSpec((SUBLANES, width), lambda i, *_: (jnp.maximum(i * per - 1, 0), col))


def _next_spec(tm, width, col, n_tiles):
    per = tm // SUBLANES
    last = n_tiles * per - 1
    return pl.BlockSpec((SUBLANES, width), lambda i, *_: (jnp.minimum((i + 1) * per, last), col))


def _group_matrix():
    idx = np.arange(ATTN_W) // HEAD_DIM
    return jnp.asarray((idx[:, None] == idx[None, :]).astype(np.float32), BF16)


def _norm_mod(x, g, sc, sh, *, name):
    T = x.shape[0]
    tm = min(TOKEN_TILE, T)

    def body(x_ref, g_ref, sc_ref, sh_ref, o_ref):
        xv = x_ref[...]
        inv = lax.rsqrt(jnp.mean(xv * xv, axis=-1, keepdims=True) + RMS_EPS)
        o_ref[...] = ((xv * inv) * g_ref[...] * (1.0 + sc_ref[...]) + sh_ref[...]).astype(BF16)

    row = pl.BlockSpec((tm, D), lambda i: (i, 0))
    return pl.pallas_call(
        body, name=name, grid=(T // tm,),
        in_specs=[row, _full((1, D)), _full((1, D)), _full((1, D))],
        out_specs=row, out_shape=jax.ShapeDtypeStruct((T, D), BF16),
        compiler_params=_params(("parallel",)),
    )(x, g, sc, sh)


LANE_ONE = 64
LANE_F = 67
LANE_LSE = 70
LANE_SUM = 73


def _pieces(x):
    hi = x.astype(BF16).astype(F32)
    rest = x - hi
    mid = rest.astype(BF16).astype(F32)
    return hi, mid, rest - mid


def _aug(lane, data, entries):
    out = jnp.where(lane < HEAD_DIM, data, 0.0)
    for idx, val in entries:
        out = jnp.where(lane == idx, val, out)
    return out


def _run(start, vals):
    return [(start + i, v) for i, v in enumerate(vals)]


def _head_lanes(a, h):
    blk = a[:, LANES * (h // 2):LANES * (h // 2) + LANES]
    return blk if h % 2 == 0 else pltpu.roll(blk, HEAD_DIM, axis=1)


def _branch_prep(proj, fcum, conv_w8, qg, kg, gmat):
    T = proj.shape[0]
    tm = min(TOKEN_TILE, T)
    nt = T // tm

    def body(cb_ref, cc_ref, cv_ref, q_ref, k_ref, v_ref, f_ref, ccp_ref, cvp_ref, w_ref, qg_ref, kg_ref, g_ref,
             ya_ref, qa_ref, ka_ref, va_ref):
        i = pl.program_id(0)
        z = cc_ref[...] * cv_ref[...]
        zp = jnp.where(i > 0, ccp_ref[...] * cvp_ref[...], 0.0)
        w = w_ref[...]
        cz = _shift_down(z, zp, 2) * w[0:1] + _shift_down(z, zp, 1) * w[1:2] + z * w[2:3]
        ya_ref[...] = (cb_ref[...] * cz).astype(BF16)
        gm = g_ref[...]

        def normed(src, gain, scale):
            v = src[...]
            ms = _split_dot(v * v, gm, 2) * (1.0 / HEAD_DIM)
            return (v * lax.rsqrt(ms + RMS_EPS)) * gain[...] * scale

        qn = normed(q_ref, qg_ref, 1.0 / np.sqrt(HEAD_DIM))
        kn = normed(k_ref, kg_ref, 1.0)
        vv = v_ref[...]
        fall = f_ref[...]
        lane = lax.broadcasted_iota(jnp.int32, (tm, LANES), 1)
        ones3 = [1.0, 1.0, 1.0]
        for h in range(N_HEADS):
            hi, mid, lo = _pieces(fall[:, h:h + 1])
            qa_ref[h] = _aug(lane, _head_lanes(qn, h), _run(LANE_ONE, ones3) + _run(LANE_F, [hi, mid, lo])
                             + [(LANE_SUM, 1.0)]).astype(BF16)
            ka_ref[h] = _aug(lane, _head_lanes(kn, h), _run(LANE_ONE, [-hi, -mid, -lo]) + _run(LANE_F, ones3)
                             + _run(LANE_LSE, ones3)).astype(BF16)
            va_ref[h] = _aug(lane, _head_lanes(vv, h), _run(LANE_ONE, ones3)).astype(BF16)

    blk = lambda col: pl.BlockSpec((tm, CONV_W), lambda i: (i, col))
    heads = pl.BlockSpec((N_HEADS, tm, LANES), lambda i: (0, i, 0))
    return pl.pallas_call(
        body, name="branch_prep", grid=(nt,),
        in_specs=[blk(0), blk(1), blk(2), blk(3), blk(4), blk(5), pl.BlockSpec((tm, F_PAD), lambda i: (i, 0)),
                  _prev_spec(tm, CONV_W, 1), _prev_spec(tm, CONV_W, 2),
                  _full((SUBLANES, CONV_W)), _full((1, ATTN_W)), _full((1, ATTN_W)), _full((ATTN_W, ATTN_W))],
        out_specs=[pl.BlockSpec((tm, CONV_W), lambda i: (i, 0)), heads, heads, heads],
        out_shape=[jax.ShapeDtypeStruct((T, CONV_W), BF16)] + [jax.ShapeDtypeStruct((N_HEADS, T, LANES), BF16)] * 3,
        compiler_params=_params(("parallel",)),
    )(proj, proj, proj, proj, proj, proj, fcum, proj, proj, conv_w8, qg, kg, gmat)


def _cumsum(x, *, reverse, name, col=0, gate_bias=None):
    T = x.shape[0]
    tm = min(TOKEN_TILE, T)
    nt = T // tm

    def body(x_ref, b_ref, o_ref, carry_ref):
        i = pl.program_id(0)

        @pl.when(i == 0)
        def _():
            carry_ref[...] = jnp.zeros_like(carry_ref)

        r = lax.broadcasted_iota(jnp.int32, (tm, tm), 0)
        c = lax.broadcasted_iota(jnp.int32, (tm, tm), 1)
        tri = jnp.where((c >= r) if reverse else (c <= r), 1.0, 0.0).astype(BF16)
        xv = x_ref[...]
        if gate_bias is not None:
            fx = xv + b_ref[...]
            xv = jnp.minimum(fx, 0.0) - jnp.log(1.0 + jnp.exp(-jnp.abs(fx)))
        out = _split_dot_left(tri, xv, 3) + carry_ref[0:1]
        o_ref[...] = out
        carry_ref[...] = jnp.broadcast_to(out[0:1] if reverse else out[tm - 1:tm], carry_ref.shape)

    rows = (lambda i: nt - 1 - i) if reverse else (lambda i: i)
    bias = jnp.zeros((1, F_PAD), F32) if gate_bias is None else gate_bias
    return pl.pallas_call(
        body, name=name, grid=(nt,),
        in_specs=[pl.BlockSpec((tm, F_PAD), lambda i: (rows(i), col)), _full((1, F_PAD))],
        out_specs=pl.BlockSpec((tm, F_PAD), lambda i: (rows(i), 0)),
        out_shape=jax.ShapeDtypeStruct((T, F_PAD), F32),
        scratch_shapes=[pltpu.VMEM((SUBLANES, F_PAD), F32)],
        compiler_params=_params(("arbitrary",)),
    )(x, bias)


def _split_dot_left(mat, x, parts):
    out = None
    rem = x
    for p in range(parts):
        piece = rem.astype(BF16)
        term = jnp.dot(mat, piece, preferred_element_type=F32)
        out = term if out is None else out + term
        if p + 1 < parts:
            rem = rem - piece.astype(F32)
    return out


def _resid_norm(x, mix, g1, g, sc, sh):
    T = x.shape[0]
    tm = min(TOKEN_TILE, T)

    def body(x_ref, mix_ref, g1_ref, g_ref, sc_ref, sh_ref, x1_ref, h_ref):
        x1 = x_ref[...] + g1_ref[...] * mix_ref[...]
        x1_ref[...] = x1
        inv = lax.rsqrt(jnp.mean(x1 * x1, axis=-1, keepdims=True) + RMS_EPS)
        h_ref[...] = ((x1 * inv) * g_ref[...] * (1.0 + sc_ref[...]) + sh_ref[...]).astype(BF16)

    row = pl.BlockSpec((tm, D), lambda i: (i, 0))
    vec = _full((1, D))
    return pl.pallas_call(
        body, name="resid_norm", grid=(T // tm,),
        in_specs=[row, row, vec, vec, vec, vec], out_specs=[row, row],
        out_shape=[jax.ShapeDtypeStruct((T, D), F32), jax.ShapeDtypeStruct((T, D), BF16)],
        compiler_params=_params(("parallel",)),
    )(x, mix, g1, g, sc, sh)


FFN_TM = 256
FFN_TC = 1408


def _ffn_act_fwd(u, w8):
    T = u.shape[0]
    tm = min(FFN_TM, T)
    nt = T // tm
    nc = D_FF // FFN_TC

    def body(ug_ref, uv_ref, ugp_ref, uvp_ref, wg_ref, wv_ref, o_ref):
        i = pl.program_id(1)

        def conv(u_ref, p_ref, w_ref):
            uv = u_ref[...]
            up = jnp.where(i > 0, p_ref[...], 0.0)
            w = w_ref[...]
            return _shift_down(uv, up, 2) * w[0:1] + _shift_down(uv, up, 1) * w[1:2] + uv * w[2:3]

        gate = conv(ug_ref, ugp_ref, wg_ref)
        val = conv(uv_ref, uvp_ref, wv_ref)
        o_ref[...] = (gate * _sigmoid(gate) * val).astype(BF16)

    per = tm // SUBLANES
    blk = lambda off: pl.BlockSpec((tm, FFN_TC), lambda j, i: (i, j + off))
    prev = lambda off: pl.BlockSpec((SUBLANES, FFN_TC), lambda j, i: (jnp.maximum(i * per - 1, 0), j + off))
    wblk = lambda off: pl.BlockSpec((SUBLANES, FFN_TC), lambda j, i: (0, j + off))
    return pl.pallas_call(
        body, name="ffn_act_fwd", grid=(nc, nt),
        in_specs=[blk(0), blk(nc), prev(0), prev(nc), wblk(0), wblk(nc)],
        out_specs=pl.BlockSpec((tm, FFN_TC), lambda j, i: (i, j)),
        out_shape=jax.ShapeDtypeStruct((T, D_FF), BF16),
        compiler_params=_params(("parallel", "parallel")),
    )(u, u, u, u, w8, w8)


def _loss_head(x1, ff, g2, target):
    T = x1.shape[0]
    tm = min(TOKEN_TILE, T)

    def body(x1_ref, ff_ref, g2_ref, t_ref, dy_ref, dff_ref, loss_ref, dg2_ref):
        i = pl.program_id(0)

        @pl.when(i == 0)
        def _():
            loss_ref[...] = jnp.zeros_like(loss_ref)
            dg2_ref[...] = jnp.zeros_like(dg2_ref)

        ff = ff_ref[...]
        err = x1_ref[...] + g2_ref[...] * ff - t_ref[...]
        dy = err * (1.0 / D)
        dy_ref[...] = dy
        dff_ref[...] = (dy * g2_ref[...]).astype(BF16)
        loss_ref[...] += _rows8(err * err)
        dg2_ref[...] += _rows8(dy * ff)

    row = pl.BlockSpec((tm, D), lambda i: (i, 0))
    acc = _full((SUBLANES, D))
    return pl.pallas_call(
        body, name="loss_head", grid=(T // tm,),
        in_specs=[row, row, _full((1, D)), row], out_specs=[row, row, acc, acc],
        out_shape=[jax.ShapeDtypeStruct((T, D), F32), jax.ShapeDtypeStruct((T, D), BF16),
                   jax.ShapeDtypeStruct((SUBLANES, D), F32), jax.ShapeDtypeStruct((SUBLANES, D), F32)],
        compiler_params=_params(("arbitrary",)),
    )(x1, ff, g2, target)


def _nt_dot(a, b):
    return lax.dot_general(a, b, (((1,), (1,)), ((), ())), preferred_element_type=F32)


def _causal(n, keys_on_rows=False):
    r = lax.broadcasted_iota(jnp.int32, (n, n), 0)
    c = lax.broadcasted_iota(jnp.int32, (n, n), 1)
    return (c >= r) if keys_on_rows else (c <= r)


def _sweep(lo, hi, step, carry):
    pairs = (hi - lo) // 2
    carry = lax.fori_loop(0, pairs, lambda j, cr: step(lo + 2 * j + 1, step(lo + 2 * j, cr)), carry)
    return lax.fori_loop(lo + 2 * pairs, hi, step, carry)


def _grid_ends(n0, n1):
    i0, i1 = pl.program_id(0), pl.program_id(1)
    return jnp.logical_and(i0 == 0, i1 == 0), jnp.logical_and(i0 == n0 - 1, i1 == n1 - 1)


def _attn_fwd(qa, ka, va, exchange=None):
    nh, T, _ = qa.shape
    bq = min(ATTN_BLOCK, T)
    nq = T // bq

    def body(*refs):
        if exchange is None:
            q_ref, k_ref, v_ref, o_ref, qb_ref = refs
        else:
            (q_ref, k_ref, v_ref), (o_ref, qb_ref), _, xrefs = exchange.split(refs, 3, 2)
            _ride(exchange, *_grid_ends(nh, nq), xrefs)
        qi = pl.program_id(1)
        q = q_ref[0]

        def step(kb, carry, masked):
            m, acc = carry
            start = pl.multiple_of(kb * bq, bq)
            s = _nt_dot(q, k_ref[0, pl.ds(start, bq), :])
            if masked:
                s = jnp.where(_causal(bq), s, NEG_INF)
            m_new = jnp.maximum(m, jnp.max(s, axis=-1, keepdims=True))
            p = jnp.exp(s - m_new).astype(BF16)
            acc = jnp.exp(m - m_new) * acc + jnp.dot(p, v_ref[0, pl.ds(start, bq), :], preferred_element_type=F32)
            return m_new, acc

        init = (jnp.full((bq, 1), NEG_INF, F32), jnp.zeros((bq, LANES), F32))
        carry = _sweep(0, qi, lambda kb, cr: step(kb, cr, False), init)
        m, acc = step(qi, carry, True)
        l = acc[:, LANE_ONE:LANE_ONE + 1]
        o_ref[0] = acc / l
        lse = m + jnp.log(l)
        lane = lax.broadcasted_iota(jnp.int32, (bq, LANES), 1)
        qf = q.astype(F32)
        for idx, piece in _run(LANE_LSE, _pieces(lse)):
            qf = jnp.where(lane == idx, -piece, qf)
        qb_ref[0] = qf.astype(BF16)

    tile = pl.BlockSpec((1, bq, LANES), lambda h, i: (h, i, 0))
    whole = pl.BlockSpec((1, T, LANES), lambda h, i: (h, 0, 0))
    out_shape = [jax.ShapeDtypeStruct((nh, T, LANES), F32), jax.ShapeDtypeStruct((nh, T, LANES), BF16)]
    if exchange is None:
        return pl.pallas_call(
            body, name="attn_fwd", grid=(nh, nq), in_specs=[tile, whole, whole], out_specs=[tile, tile],
            out_shape=out_shape, compiler_params=_params(("parallel", "parallel")),
        )(qa, ka, va)
    return pl.pallas_call(
        body, name="attn_fwd", grid=(nh, nq), in_specs=[tile, whole, whole] + exchange.in_specs,
        out_specs=[tile, tile] + exchange.out_specs, out_shape=out_shape + exchange.out_shapes,
        scratch_shapes=exchange.scratch, compiler_params=_params(("arbitrary", "arbitrary")),
    )(qa, ka, va, *exchange.xs)


def _branch_merge_fwd(ya0, o_h, proj, wba, wbb_heads):
    nh, T, _ = o_h.shape
    tm = min(TOKEN_TILE, T)

    def body(ya0_ref, o_ref, ga_ref, gb_ref, wa_ref, wb_ref, ya_ref, yb_ref, m_ref):
        ya = jnp.dot(ya0_ref[...], wa_ref[...], preferred_element_type=F32)
        yb = jnp.dot(o_ref[0].astype(BF16), wb_ref[0:LANES, :], preferred_element_type=F32)
        for h in range(1, nh):
            yb += jnp.dot(o_ref[h].astype(BF16), wb_ref[h * LANES:(h + 1) * LANES, :], preferred_element_type=F32)
        ya_ref[...] = ya
        yb_ref[...] = yb
        m_ref[...] = (_sigmoid(ga_ref[...]) * ya + _sigmoid(gb_ref[...]) * yb).astype(BF16)

    row = pl.BlockSpec((tm, D), lambda i: (i, 0))
    return pl.pallas_call(
        body, name="branch_merge_fwd", grid=(T // tm,),
        in_specs=[pl.BlockSpec((tm, CONV_W), lambda i: (i, 0)), pl.BlockSpec((nh, tm, LANES), lambda i: (0, i, 0)),
                  pl.BlockSpec((tm, D), lambda i: (i, COL_GA // D)), pl.BlockSpec((tm, D), lambda i: (i, COL_GB // D)),
                  _full((CONV_W, D)), _full((nh * LANES, D))],
        out_specs=[row, row, row],
        out_shape=[jax.ShapeDtypeStruct((T, D), F32), jax.ShapeDtypeStruct((T, D), F32),
                   jax.ShapeDtypeStruct((T, D), BF16)],
        compiler_params=_params(("parallel",)),
    )(ya0, o_h, proj, proj, wba, wbb_heads)


def _branch_b_bwd(dyb, o_h, wbb_heads_t):
    nh, T, _ = o_h.shape
    tm = min(TOKEN_TILE, T)

    def body(dyb_ref, o_ref, w_ref, out_ref):
        do = jnp.dot(dyb_ref[...], w_ref[...], preferred_element_type=F32)
        lane = lax.broadcasted_iota(jnp.int32, (tm, LANES), 1)
        for h in range(nh):
            g = do[:, h * LANES:(h + 1) * LANES].astype(BF16).astype(F32)
            delta = jnp.sum(g * o_ref[h], axis=-1, keepdims=True)
            for idx, piece in _run(LANE_ONE, _pieces(delta)):
                g = jnp.where(lane == idx, -piece, g)
            out_ref[h] = g.astype(BF16)

    heads = pl.BlockSpec((nh, tm, LANES), lambda i: (0, i, 0))
    return pl.pallas_call(
        body, name="branch_b_bwd", grid=(T // tm,),
        in_specs=[pl.BlockSpec((tm, D), lambda i: (i, 0)), heads, _full((D, nh * LANES))],
        out_specs=heads, out_shape=jax.ShapeDtypeStruct((nh, T, LANES), BF16),
        compiler_params=_params(("parallel",)),
    )(dyb, o_h, wbb_heads_t)


def _branch_b_dw(o_h, dyb):
    nh, T, _ = o_h.shape
    tk = min(TOKEN_TILE, T)

    def body(o_ref, dyb_ref, out_ref):
        @pl.when(pl.program_id(0) == 0)
        def _():
            out_ref[...] = jnp.zeros_like(out_ref)

        g = dyb_ref[...]
        for h in range(nh):
            out_ref[h] += _tn_dot(o_ref[h].astype(BF16), g)

    return pl.pallas_call(
        body, name="branch_b_dw", grid=(T // tk,),
        in_specs=[pl.BlockSpec((nh, tk, LANES), lambda k: (0, k, 0)), pl.BlockSpec((tk, D), lambda k: (k, 0))],
        out_specs=_full((nh, LANES, D)), out_shape=jax.ShapeDtypeStruct((nh, LANES, D), F32),
        compiler_params=_params(("arbitrary",)),
    )(o_h, dyb)


def _attn_bwd(qb, ka, va, doa, exchange=None):
    nh, T, _ = qb.shape
    bk = min(ATTN_BLOCK, T)
    nk = T // bk

    def body(*refs):
        if exchange is None:
            q_ref, do_ref, k_ref, v_ref, dq_ref, dk_ref, dv_ref = refs
        else:
            (q_ref, do_ref, k_ref, v_ref), (dq_ref, dk_ref, dv_ref), _, xrefs = exchange.split(refs, 4, 3)
            _ride(exchange, *_grid_ends(nh, nk), xrefs)
        ki = pl.program_id(1)

        @pl.when(ki == 0)
        def _():
            dq_ref[...] = jnp.zeros_like(dq_ref)

        k = k_ref[0]
        v = v_ref[0]

        def step(qi, carry, masked):
            dk, dv = carry
            rows = pl.ds(pl.multiple_of(qi * bk, bk), bk)
            q = q_ref[0, rows, :]
            g = do_ref[0, rows, :]
            pt = jnp.exp(_nt_dot(k, q))
            if masked:
                pt = jnp.where(_causal(bk, keys_on_rows=True), pt, 0.0)
            dv = dv + jnp.dot(pt.astype(BF16), g, preferred_element_type=F32)
            dst = (pt * _nt_dot(v, g)).astype(BF16)
            dk = dk + jnp.dot(dst, q, preferred_element_type=F32)
            dq_ref[0, rows, :] += _tn_dot(dst, k)
            return dk, dv

        init = (jnp.zeros((bk, LANES), F32), jnp.zeros((bk, LANES), F32))
        carry = step(ki, init, True)
        dk_ref[0], dv_ref[0] = _sweep(ki + 1, nk, lambda qi, cr: step(qi, cr, False), carry)

    tile = pl.BlockSpec((1, bk, LANES), lambda h, i: (h, i, 0))
    whole = pl.BlockSpec((1, T, LANES), lambda h, i: (h, 0, 0))
    out_shape = [jax.ShapeDtypeStruct((nh, T, LANES), F32)] * 3
    if exchange is None:
        return pl.pallas_call(
            body, name="attn_bwd", grid=(nh, nk), in_specs=[whole, whole, tile, tile],
            out_specs=[whole, tile, tile], out_shape=out_shape, compiler_params=_params(("parallel", "arbitrary")),
        )(qb, doa, ka, va)
    return pl.pallas_call(
        body, name="attn_bwd", grid=(nh, nk), in_specs=[whole, whole, tile, tile] + exchange.in_specs,
        out_specs=[whole, tile, tile] + exchange.out_specs, out_shape=out_shape + exchange.out_shapes,
        scratch_shapes=exchange.scratch, compiler_params=_params(("arbitrary", "arbitrary")),
    )(qb, doa, ka, va, *exchange.xs)


def _attn_unpack(dq_h, dk_h, dv_h):
    nh, T, _ = dq_h.shape
    tm = min(TOKEN_TILE, T)

    def body(dq_ref, dk_ref, dv_ref, q_out, k_out, v_out, f_out):
        lane = lax.broadcasted_iota(jnp.int32, (tm, LANES), 1)
        low = lane < HEAD_DIM
        for src, dst in ((dq_ref, q_out), (dk_ref, k_out), (dv_ref, v_out)):
            for pair in range(nh // 2):
                both = jnp.where(low, src[2 * pair], pltpu.roll(src[2 * pair + 1], HEAD_DIM, axis=1))
                dst[:, LANES * pair:LANES * (pair + 1)] = both.astype(dst.dtype)
        df = jnp.zeros((tm, LANES), F32)
        for h in range(nh):
            col = dq_ref[h][:, LANE_F:LANE_F + 1] - dk_ref[h][:, LANE_SUM:LANE_SUM + 1]
            df = jnp.where(lane == h, col, df)
        f_out[...] = df

    heads = pl.BlockSpec((nh, tm, LANES), lambda i: (0, i, 0))
    tok = pl.BlockSpec((tm, ATTN_W), lambda i: (i, 0))
    return pl.pallas_call(
        body, name="attn_unpack", grid=(T // tm,), in_specs=[heads, heads, heads],
        out_specs=[tok, tok, tok, pl.BlockSpec((tm, F_PAD), lambda i: (i, 0))],
        out_shape=[jax.ShapeDtypeStruct((T, ATTN_W), F32), jax.ShapeDtypeStruct((T, ATTN_W), F32),
                   jax.ShapeDtypeStruct((T, ATTN_W), BF16), jax.ShapeDtypeStruct((T, F_PAD), F32)],
        compiler_params=_params(("parallel",)),
    )(dq_h, dk_h, dv_h)


def _ffn_act_bwd(u, da, w8):
    T = u.shape[0]
    tm = min(FFN_TM, T)
    nt = T // tm
    nc = D_FF // FFN_TC

    def body(ug_ref, uv_ref, ugp_ref, uvp_ref, ugn_ref, uvn_ref, da_ref, dan_ref, wg_ref, wv_ref,
             dug_ref, duv_ref, dwg_ref, dwv_ref):
        i = pl.program_id(1)

        @pl.when(i == 0)
        def _():
            dwg_ref[...] = jnp.zeros_like(dwg_ref)
            dwv_ref[...] = jnp.zeros_like(dwv_ref)

        first, last = i == 0, i == nt - 1
        wg, wv = wg_ref[...], wv_ref[...]
        zeros8 = jnp.zeros((SUBLANES, FFN_TC), F32)

        def window(u_ref, p_ref, n_ref, w):
            e = jnp.concatenate([jnp.where(first, 0.0, p_ref[...]), u_ref[...], n_ref[...]], axis=0)
            e1 = pltpu.roll(e, 1, axis=0)
            e2 = pltpu.roll(e, 2, axis=0)
            return e, e1, e2, e2 * w[0:1] + e1 * w[1:2] + e * w[2:3]

        eg, eg1, eg2, cg = window(ug_ref, ugp_ref, ugn_ref, wg)
        ev, ev1, ev2, cv = window(uv_ref, uvp_ref, uvn_ref, wv)
        dae = jnp.concatenate([zeros8, da_ref[...].astype(F32), jnp.where(last, 0.0, dan_ref[...].astype(F32))],
                              axis=0)
        sg = _sigmoid(cg)
        dgate = dae * cv * sg * (1.0 + cg * (1.0 - sg))
        dval = dae * cg * sg
        n = tm + 2 * SUBLANES

        def back(d, w):
            return d * w[2:3] + pltpu.roll(d, n - 1, axis=0) * w[1:2] + pltpu.roll(d, n - 2, axis=0) * w[0:1]

        inner = slice(SUBLANES, SUBLANES + tm)
        dug_ref[...] = back(dgate, wg)[inner].astype(BF16)
        duv_ref[...] = back(dval, wv)[inner].astype(BF16)

        def wgrad(d, e, e1, e2):
            rows = [jnp.sum((d * t)[inner], axis=0, keepdims=True) for t in (e2, e1, e)]
            return jnp.concatenate(rows + [jnp.zeros((SUBLANES - 3, FFN_TC), F32)], axis=0)

        dwg_ref[...] += wgrad(dgate, eg, eg1, eg2)
        dwv_ref[...] += wgrad(dval, ev, ev1, ev2)

    per = tm // SUBLANES
    last_blk = nt * per - 1
    blk = lambda off: pl.BlockSpec((tm, FFN_TC), lambda j, i: (i, j + off))
    prev = lambda off: pl.BlockSpec((SUBLANES, FFN_TC), lambda j, i: (jnp.maximum(i * per - 1, 0), j + off))
    nxt = lambda off: pl.BlockSpec((SUBLANES, FFN_TC), lambda j, i: (jnp.minimum((i + 1) * per, last_blk), j + off))
    wblk = lambda off: pl.BlockSpec((SUBLANES, FFN_TC), lambda j, i: (0, j + off))
    dug, duv, dwg, dwv = pl.pallas_call(
        body, name="ffn_act_bwd", grid=(nc, nt),
        in_specs=[blk(0), blk(nc), prev(0), prev(nc), nxt(0), nxt(nc), blk(0), nxt(0), wblk(0), wblk(nc)],
        out_specs=[blk(0), blk(0), wblk(0), wblk(0)],
        out_shape=[jax.ShapeDtypeStruct((T, D_FF), BF16)] * 2 + [jax.ShapeDtypeStruct((SUBLANES, D_FF), F32)] * 2,
        compiler_params=_params(("parallel", "arbitrary")),
    )(u, u, u, u, u, u, da, da, w8, w8)
    return jnp.concatenate([dug, duv], axis=1), jnp.concatenate([dwg, dwv], axis=1)


def _norm_bwd(xin, dh, dres, g, sc, *, name):
    T = xin.shape[0]
    tm = min(TOKEN_TILE, T)

    def body(x_ref, dh_ref, dr_ref, g_ref, sc_ref, dx_ref, dsh_ref, dsc_ref, dg_ref):
        i = pl.program_id(0)

        @pl.when(i == 0)
        def _():
            dsh_ref[...] = jnp.zeros_like(dsh_ref)
            dsc_ref[...] = jnp.zeros_like(dsc_ref)
            dg_ref[...] = jnp.zeros_like(dg_ref)

        xv = x_ref[...]
        dh = dh_ref[...]
        gv = g_ref[...]
        one_sc = 1.0 + sc_ref[...]
        inv = lax.rsqrt(jnp.mean(xv * xv, axis=-1, keepdims=True) + RMS_EPS)
        xn = xv * inv
        dxn = dh * (gv * one_sc)
        dx_ref[...] = dr_ref[...] + inv * (dxn - xn * jnp.mean(dxn * xn, axis=-1, keepdims=True))
        dhxn = dh * xn
        dsh_ref[...] += _rows8(dh)
        dsc_ref[...] += _rows8(dhxn * gv)
        dg_ref[...] += _rows8(dhxn * one_sc)

    row = pl.BlockSpec((tm, D), lambda i: (i, 0))
    acc = _full((SUBLANES, D))
    return pl.pallas_call(
        body, name=name, grid=(T // tm,),
        in_specs=[row, row, row, _full((1, D)), _full((1, D))], out_specs=[row, acc, acc, acc],
        out_shape=[jax.ShapeDtypeStruct((T, D), F32)] + [jax.ShapeDtypeStruct((SUBLANES, D), F32)] * 3,
        compiler_params=_params(("arbitrary",)),
    )(xin, dh, dres, g, sc)


def _gate_bwd(dx1, mix, g1):
    T = dx1.shape[0]
    tm = min(TOKEN_TILE, T)

    def body(dx_ref, mix_ref, g1_ref, dmix_ref, dg1_ref):
        @pl.when(pl.program_id(0) == 0)
        def _():
            dg1_ref[...] = jnp.zeros_like(dg1_ref)

        dx = dx_ref[...]
        dmix_ref[...] = (dx * g1_ref[...]).astype(BF16)
        dg1_ref[...] += _rows8(dx * mix_ref[...])

    row = pl.BlockSpec((tm, D), lambda i: (i, 0))
    return pl.pallas_call(
        body, name="gate_bwd", grid=(T // tm,),
        in_specs=[row, row, _full((1, D))], out_specs=[row, _full((SUBLANES, D))],
        out_shape=[jax.ShapeDtypeStruct((T, D), BF16), jax.ShapeDtypeStruct((SUBLANES, D), F32)],
        compiler_params=_params(("arbitrary",)),
    )(dx1, mix, g1)


def _merge_bwd(dmerged, ya, yb, proj):
    T = ya.shape[0]
    tm = min(TOKEN_TILE, T)

    def body(dm_ref, ya_ref, yb_ref, ga_ref, gb_ref, dya_ref, dyb_ref, dga_ref, dgb_ref):
        dm = dm_ref[...]
        sa = _sigmoid(ga_ref[...])
        sb = _sigmoid(gb_ref[...])
        dya_ref[...] = (dm * sa).astype(BF16)
        dyb_ref[...] = (dm * sb).astype(BF16)
        dga_ref[...] = (dm * ya_ref[...] * sa * (1.0 - sa)).astype(BF16)
        dgb_ref[...] = (dm * yb_ref[...] * sb * (1.0 - sb)).astype(BF16)

    row = pl.BlockSpec((tm, D), lambda i: (i, 0))
    return pl.pallas_call(
        body, name="merge_bwd", grid=(T // tm,),
        in_specs=[row, row, row, pl.BlockSpec((tm, D), lambda i: (i, COL_GA // D)),
                  pl.BlockSpec((tm, D), lambda i: (i, COL_GB // D))],
        out_specs=[row] * 4, out_shape=[jax.ShapeDtypeStruct((T, D), BF16)] * 4,
        compiler_params=_params(("parallel",)),
    )(dmerged, ya, yb, proj, proj)


def _conv_branch_bwd(proj, dya0, conv_w8):
    T = proj.shape[0]
    tm = min(FFN_TM, T)
    nt = T // tm

    def body(cb_ref, cc_ref, cv_ref, cbn_ref, ccp_ref, cvp_ref, ccn_ref, cvn_ref, d_ref, dn_ref, w_ref,
             dcb_ref, dcc_ref, dcv_ref, dw_ref):
        i = pl.program_id(0)

        @pl.when(i == 0)
        def _():
            dw_ref[...] = jnp.zeros_like(dw_ref)

        first, last = i == 0, i == nt - 1
        w = w_ref[...]
        cc = jnp.concatenate([ccp_ref[...], cc_ref[...], ccn_ref[...]], axis=0)
        cv = jnp.concatenate([cvp_ref[...], cv_ref[...], cvn_ref[...]], axis=0)
        rows = lax.broadcasted_iota(jnp.int32, cc.shape, 0)
        z = jnp.where(jnp.logical_and(first, rows < SUBLANES), 0.0, cc * cv)
        z1 = pltpu.roll(z, 1, axis=0)
        z2 = pltpu.roll(z, 2, axis=0)
        cz = z2 * w[0:1] + z1 * w[1:2] + z * w[2:3]
        zeros8 = jnp.zeros((SUBLANES, CONV_W), F32)
        de = jnp.concatenate([zeros8, d_ref[...], jnp.where(last, 0.0, dn_ref[...])], axis=0)
        cbe = jnp.concatenate([zeros8, cb_ref[...], cbn_ref[...]], axis=0)
        dcz = de * cbe
        n = tm + 2 * SUBLANES
        dz = dcz * w[2:3] + pltpu.roll(dcz, n - 1, axis=0) * w[1:2] + pltpu.roll(dcz, n - 2, axis=0) * w[0:1]
        inner = slice(SUBLANES, SUBLANES + tm)
        dcb_ref[...] = (de * cz)[inner].astype(BF16)
        dcc_ref[...] = (dz * cv)[inner].astype(BF16)
        dcv_ref[...] = (dz * cc)[inner].astype(BF16)
        wrows = [jnp.sum((dcz * t)[inner], axis=0, keepdims=True) for t in (z2, z1, z)]
        dw_ref[...] += jnp.concatenate(wrows + [jnp.zeros((SUBLANES - 3, CONV_W), F32)], axis=0)

    blk = lambda col: pl.BlockSpec((tm, CONV_W), lambda i: (i, col))
    out_blk = pl.BlockSpec((tm, CONV_W), lambda i: (i, 0))
    return pl.pallas_call(
        body, name="conv_branch_bwd", grid=(nt,),
        in_specs=[blk(0), blk(1), blk(2), _next_spec(tm, CONV_W, 0, nt),
                  _prev_spec(tm, CONV_W, 1), _prev_spec(tm, CONV_W, 2),
                  _next_spec(tm, CONV_W, 1, nt), _next_spec(tm, CONV_W, 2, nt),
                  out_blk, _next_spec(tm, CONV_W, 0, nt), _full((SUBLANES, CONV_W))],
        out_specs=[out_blk, out_blk, out_blk, _full((SUBLANES, CONV_W))],
        out_shape=[jax.ShapeDtypeStruct((T, CONV_W), BF16)] * 3 + [jax.ShapeDtypeStruct((SUBLANES, CONV_W), F32)],
        compiler_params=_params(("arbitrary",)),
    )(proj, proj, proj, proj, proj, proj, proj, proj, dya0, dya0, conv_w8)


def _qk_norm_bwd(proj, dqs, dkh, dlogf, qg, kg, bf_pad, gmat):
    T = proj.shape[0]
    tm = min(TOKEN_TILE, T)

    def body(q_ref, k_ref, f_ref, dqs_ref, dkh_ref, dlf_ref, qg_ref, kg_ref, bf_ref, g_ref,
             dq_ref, dk_ref, dfl_ref, dqg_ref, dkg_ref, dbf_ref):
        @pl.when(pl.program_id(0) == 0)
        def _():
            dqg_ref[...] = jnp.zeros_like(dqg_ref)
            dkg_ref[...] = jnp.zeros_like(dkg_ref)
            dbf_ref[...] = jnp.zeros_like(dbf_ref)

        gm = g_ref[...]
        for src, d_src, gain, scale, dst, dgain in (
                (q_ref, dqs_ref, qg_ref, 1.0 / np.sqrt(HEAD_DIM), dq_ref, dqg_ref),
                (k_ref, dkh_ref, kg_ref, 1.0, dk_ref, dkg_ref)):
            v = src[...]
            dhat = d_src[...] * scale
            inv = lax.rsqrt(_split_dot(v * v, gm, 2) * (1.0 / HEAD_DIM) + RMS_EPS)
            vn = v * inv
            dgain[...] += _rows8(dhat * vn)
            dvn = dhat * gain[...]
            mean = _split_dot(dvn * vn, gm, 2) * (1.0 / HEAD_DIM)
            dst[...] = (inv * (dvn - vn * mean)).astype(BF16)
        fx = f_ref[...] + bf_ref[...]
        dfl = dlf_ref[...] * _sigmoid(-fx)
        dfl_ref[...] = dfl.astype(BF16)
        dbf_ref[...] += _rows8(dfl)

    blk = lambda col: pl.BlockSpec((tm, ATTN_W), lambda i: (i, col))
    out_blk = pl.BlockSpec((tm, ATTN_W), lambda i: (i, 0))
    f_in = pl.BlockSpec((tm, F_PAD), lambda i: (i, COL_F // F_PAD))
    f_blk = pl.BlockSpec((tm, F_PAD), lambda i: (i, 0))
    return pl.pallas_call(
        body, name="qk_norm_bwd", grid=(T // tm,),
        in_specs=[blk(3), blk(4), f_in, out_blk, out_blk, f_blk, _full((1, ATTN_W)), _full((1, ATTN_W)),
                  _full((1, F_PAD)), _full((ATTN_W, ATTN_W))],
        out_specs=[out_blk, out_blk, f_blk, _full((SUBLANES, ATTN_W)), _full((SUBLANES, ATTN_W)),
                   _full((SUBLANES, F_PAD))],
        out_shape=[jax.ShapeDtypeStruct((T, ATTN_W), BF16)] * 2 + [jax.ShapeDtypeStruct((T, F_PAD), BF16)]
        + [jax.ShapeDtypeStruct((SUBLANES, ATTN_W), F32)] * 2 + [jax.ShapeDtypeStruct((SUBLANES, F_PAD), F32)],
        compiler_params=_params(("arbitrary",)),
    )(proj, proj, proj, dqs, dkh, dlogf, qg, kg, bf_pad, gmat)


def _pad_rows8(w):
    return jnp.pad(w, ((0, SUBLANES - w.shape[0]), (0, 0)))


def _fold8(acc):
    return jnp.sum(acc, axis=0, keepdims=True)


def _late_weights(mats):
    out = {}
    for name in ("w_branch_a", "w_out", "w_up", "w_down"):
        out[name] = mats[name]
        out[name + "_t"] = mats[name].T
    out["w_branch_b_heads"] = _pad_head_rows(mats["w_branch_b"])
    out["w_branch_b_heads_t"] = out["w_branch_b_heads"].T
    return out


def _local_step(x, target, mod, wts, late=None):
    T = x.shape[0]
    tb = min(MATMUL_TILE, T)
    sh1, sc1, g1, sh2, sc2, g2 = [mod[:, i * D:(i + 1) * D] for i in range(N_MOD)]
    w_in, w_in_t = wts["w_in"], wts["w_in_t"]
    conv_a8 = _pad_rows8(wts["conv_a_w"])
    conv_f8 = _pad_rows8(wts["conv_ffn_w"])
    qg = jnp.tile(wts["q_norm_g"], (1, N_HEADS))
    kg = jnp.tile(wts["k_norm_g"], (1, N_HEADS))
    bf_pad = jnp.pad(wts["b_f"], ((0, 0), (0, F_PAD - N_HEADS)))
    gmat = _group_matrix()

    h = _norm_mod(x, wts["norm1_g"], sc1, sh1, name="norm1_fwd")
    proj = _matmul(h, w_in, name="mm_in", tm=tb, tn=896, tk=D)
    fcum = _cumsum(proj, reverse=False, name="gate_cumsum", col=COL_F // F_PAD, gate_bias=bf_pad)
    ya0, qa, ka, va = _branch_prep(proj, fcum, conv_a8, qg, kg, gmat)
    if late is None:
        o_h, qb = _attn_fwd(qa, ka, va)
    else:
        o_h, qb, *gathered = _attn_fwd(qa, ka, va, _Exchange([late[name] for name, *_ in LATE], scatter=False))
        wts = dict(wts)
        wts.update(_late_weights({name: _join_shards(g, axis) for (name, _, _, axis), g in zip(LATE, gathered)}))
    ya, yb, merged = _branch_merge_fwd(ya0, o_h, proj, wts["w_branch_a"], wts["w_branch_b_heads"])
    mix = _matmul(merged, wts["w_out"], name="mm_out", tm=tb, tn=D, tk=D)
    x1, h2 = _resid_norm(x, mix, g1, wts["norm2_g"], sc2, sh2)
    u = _matmul(h2, wts["w_up"], name="mm_up", tm=tb, tn=1408, tk=D)
    act = _ffn_act_fwd(u, conv_f8)
    ff = _matmul(act, wts["w_down"], name="mm_down", tm=tb, tn=D, tk=D_FF)
    dy, dff, sq8, dg2_8 = _loss_head(x1, ff, g2, target)
    sq = jnp.sum(sq8).reshape(1, 1)

    grads = {}
    da = _matmul(dff, wts["w_down_t"], name="mm_down_dx", tm=tb, tn=1408, tk=D)
    grads["w_down"] = _matmul(act, dff, name="mm_down_dw", tm=1408, tn=D, tk=tb, trans_a=True)
    du, dconv_f8 = _ffn_act_bwd(u, da, conv_f8)
    grads["conv_ffn_w"] = dconv_f8[:3]
    dh2 = _matmul(du, wts["w_up_t"], name="mm_up_dx", tm=tb, tn=D, tk=1408)
    grads["w_up"] = _matmul(h2, du, name="mm_up_dw", tm=D, tn=1408, tk=tb, trans_a=True)
    dx1, dsh2_8, dsc2_8, dn2_8 = _norm_bwd(x1, dh2, dy, wts["norm2_g"], sc2, name="norm2_bwd")
    grads["norm2_g"] = _fold8(dn2_8)

    dmix, dg1_8 = _gate_bwd(dx1, mix, g1)
    dmerged = _matmul(dmix, wts["w_out_t"], name="mm_out_dx", tm=tb, tn=D, tk=D)
    grads["w_out"] = _matmul(merged, dmix, name="mm_out_dw", tm=D, tn=D, tk=tb, trans_a=True)
    dya, dyb, dga, dgb = _merge_bwd(dmerged, ya, yb, proj)
    dya0 = _matmul(dya, wts["w_branch_a_t"], name="mm_branch_a_dx", tm=tb, tn=CONV_W, tk=D)
    grads["w_branch_a"] = _matmul(ya0, dya, name="mm_branch_a_dw", tm=CONV_W, tn=D, tk=tb, trans_a=True)
    doa = _branch_b_bwd(dyb, o_h, wts["w_branch_b_heads_t"])
    grads["w_branch_b"] = _branch_b_dw(o_h, dyb)[:, :HEAD_DIM].reshape(ATTN_W, D)
    dcb, dcc, dcv, dconv_a8 = _conv_branch_bwd(proj, dya0, conv_a8)
    grads["conv_a_w"] = dconv_a8[:3]

    parts = {}
    if late is None:
        dq_h, dk_h, dv_h = _attn_bwd(qb, ka, va, doa)
    else:
        ready = [_split_shards(grads[name], axis).astype(BF16) for name, _, _, axis in LATE]
        dq_h, dk_h, dv_h, *recv = _attn_bwd(
            qb, ka, va, doa, _Exchange(ready + [_pack_full_by_dest(grads, CONVS, SUBLANES)], scatter=True))
        parts = dict(zip([name for name, *_ in LATE] + ["conv"], recv))
    dq_tok, dk_tok, dv_tok, dfcum = _attn_unpack(dq_h, dk_h, dv_h)
    dlogf = _cumsum(dfcum, reverse=True, name="gate_cumsum_bwd")
    dq, dk, dfl, dqg8, dkg8, dbf8 = _qk_norm_bwd(proj, dq_tok, dk_tok, dlogf, qg, kg, bf_pad, gmat)
    grads["q_norm_g"] = jnp.sum(_fold8(dqg8).reshape(N_HEADS, HEAD_DIM), axis=0, keepdims=True)
    grads["k_norm_g"] = jnp.sum(_fold8(dkg8).reshape(N_HEADS, HEAD_DIM), axis=0, keepdims=True)
    grads["b_f"] = _fold8(dbf8)[:, :N_HEADS]
    dproj = jnp.concatenate(
        [dcb, dcc, dcv, dq, dk, dv_tok, dga, dgb, dfl, jnp.zeros((T, IN_W_PAD - COL_F - F_PAD), BF16)], axis=1)
    grads["w_in"] = _unpad_in(_matmul(h, dproj, name="mm_in_dw", tm=D, tn=896, tk=tb, trans_a=True))
    if late is None:
        dh = _matmul(dproj, w_in_t, name="mm_in_dx", tm=tb, tn=D, tk=1792)
    else:
        dh, parts["w_in"] = _matmul(dproj, w_in_t, name="mm_in_dx", tm=tb, tn=D, tk=1792,
                                    exchange=_Exchange([_split_shards(grads["w_in"], 1).astype(BF16)], scatter=True))
    grad_x, dsh1_8, dsc1_8, dn1_8 = _norm_bwd(x, dh, dx1, wts["norm1_g"], sc1, name="norm1_bwd")
    grads["norm1_g"] = _fold8(dn1_8)
    grads["mod"] = jnp.concatenate([_fold8(a) for a in (dsh1_8, dsc1_8, dg1_8, dsh2_8, dsc2_8, dg2_8)], axis=1)
    return sq, grad_x, grads, parts


def _me_and_peers():
    mx, my, mc = lax.axis_index("x"), lax.axis_index("y"), lax.axis_index("c")
    me = 4 * mx + 2 * my + mc
    peers = []
    for k in range(1, N_DEV):
        px = 1 - mx if k & 4 else mx
        py = 1 - my if k & 2 else my
        pc = 1 - mc if k & 1 else mc
        peers.append(((px, py, pc), 4 * px + 2 * py + pc))
    return me, peers


HBM_SPEC = pl.BlockSpec(memory_space=pltpu.HBM)


class _Exchange:
    def __init__(self, xs, scatter):
        self.xs, self.scatter, self.n = list(xs), scatter, len(xs)
        self.out_shapes = [jax.ShapeDtypeStruct(x.shape if scatter else (N_DEV,) + x.shape, x.dtype) for x in xs]
        self.in_specs = [HBM_SPEC] * self.n
        self.out_specs = [HBM_SPEC] * self.n
        self.scratch = [pltpu.SemaphoreType.DMA((self.n, N_DEV - 1)), pltpu.SemaphoreType.DMA((self.n, N_DEV - 1)),
                        pltpu.SemaphoreType.DMA((self.n,))]

    def _copies(self, x_refs, out_refs, sems):
        send_sems, recv_sems, local_sems = sems
        me, peers = _me_and_peers()

        def src(a, idx):
            return x_refs[a].at[idx] if self.scatter else x_refs[a]

        def copy(a, k, from_idx, to_slot, device):
            return pltpu.make_async_remote_copy(
                src_ref=src(a, from_idx), dst_ref=out_refs[a].at[to_slot], send_sem=send_sems.at[a, k],
                recv_sem=recv_sems.at[a, k], device_id=device, device_id_type=MESH)

        local = [pltpu.make_async_copy(src(a, me), out_refs[a].at[me], local_sems.at[a]) for a in range(self.n)]
        sends = [copy(a, k, idx, me, dev) for a in range(self.n) for k, (dev, idx) in enumerate(peers)]
        recvs = [copy(a, k, idx, idx, dev) for a in range(self.n) for k, (dev, idx) in enumerate(peers)]
        return local, sends, recvs

    def start(self, x_refs, out_refs, sems):
        local, sends, _ = self._copies(x_refs, out_refs, sems)
        for cp in local + sends:
            cp.start()

    def wait(self, x_refs, out_refs, sems):
        local, sends, recvs = self._copies(x_refs, out_refs, sems)
        for cp in recvs:
            cp.wait_recv()
        for cp in sends:
            cp.wait_send()
        for cp in local:
            cp.wait()

    def split(self, refs, n_in, n_out):
        n = self.n
        ins, xin = refs[:n_in], refs[n_in:n_in + n]
        outs, xout = refs[n_in + n:n_in + n + n_out], refs[n_in + n + n_out:n_in + 2 * n + n_out]
        rest = refs[n_in + 2 * n + n_out:]
        return ins, outs, rest[:len(rest) - 3], (xin, xout, rest[len(rest) - 3:])


def _ride(exchange, first, last, refs):
    if exchange is None:
        return

    @pl.when(first)
    def _():
        exchange.start(*refs)

    @pl.when(last)
    def _():
        exchange.wait(*refs)


def _gather_two_level(xs, *, name):
    n = len(xs)
    out_shapes = [jax.ShapeDtypeStruct((N_DEV,) + x.shape, x.dtype) for x in xs]

    def body(*refs):
        x_refs, out_refs = refs[:n], refs[n:2 * n]
        send_sems, recv_sems, local_sems = refs[2 * n:]
        x, y, c = lax.axis_index("x"), lax.axis_index("y"), lax.axis_index("c")
        me, sibling = (x, y, c), (x, y, 1 - c)
        chips = [(1 - x, y), (x, 1 - y), (1 - x, 1 - y)]

        def slot(a, dev):
            return out_refs[a].at[4 * dev[0] + 2 * dev[1] + dev[2]]

        def copy(a, k, block, to, src=None):
            return pltpu.make_async_remote_copy(
                src_ref=slot(a, block) if src is None else src, dst_ref=slot(a, block),
                send_sem=send_sems.at[a, k], recv_sem=recv_sems.at[a, k], device_id=to, device_id_type=MESH)

        mine = [pltpu.make_async_copy(x_refs[a], slot(a, me), local_sems.at[a]) for a in range(n)]
        first = [copy(a, 0, me, sibling, src=x_refs[a]) for a in range(n)]
        first += [copy(a, 1 + j, me, (*chip, c), src=x_refs[a]) for a in range(n) for j, chip in enumerate(chips)]
        for cp in mine + first:
            cp.start()
        passed = []
        for a in range(n):
            for j, chip in enumerate(chips):
                copy(a, 1 + j, (*chip, c), me).wait_recv()
                passed.append(copy(a, 4 + j, (*chip, c), sibling))
                passed[-1].start()
        for a in range(n):
            copy(a, 0, sibling, me).wait_recv()
            for j, chip in enumerate(chips):
                copy(a, 4 + j, (*chip, 1 - c), me).wait_recv()
        for cp in first + passed:
            cp.wait_send()
        for cp in mine:
            cp.wait()

    return pl.pallas_call(
        body, name=name, in_specs=[HBM_SPEC] * n, out_specs=[HBM_SPEC] * n, out_shape=out_shapes,
        scratch_shapes=[pltpu.SemaphoreType.DMA((n, N_DEV - 1)), pltpu.SemaphoreType.DMA((n, N_DEV - 1)),
                        pltpu.SemaphoreType.DMA((n,))],
        compiler_params=pltpu.CompilerParams(has_side_effects=True),
    )(*xs)


def _exchange(xs, *, name, scatter):
    ex = _Exchange(xs, scatter)

    def body(*refs):
        _, _, _, xrefs = ex.split(refs, 0, 0)
        ex.start(*xrefs)
        ex.wait(*xrefs)

    return pl.pallas_call(
        body, name=name, in_specs=ex.in_specs, out_specs=ex.out_specs, out_shape=ex.out_shapes,
        scratch_shapes=ex.scratch, compiler_params=pltpu.CompilerParams(has_side_effects=True),
    )(*xs)


def _ada_fwd(c_all, w_shard, b_shard):
    n = w_shard.shape[1]

    def body(c_ref, w_ref, b_ref, o_ref):
        cv = c_ref[...]
        act = (cv * _sigmoid(cv)).astype(BF16)
        o_ref[...] = jnp.dot(act, w_ref[...].astype(BF16), preferred_element_type=F32) + b_ref[...]

    return pl.pallas_call(
        body, name="ada_fwd", in_specs=[_full((N_DEV, D)), _full((D, n)), _full((1, n))],
        out_specs=_full((N_DEV, n)), out_shape=jax.ShapeDtypeStruct((N_DEV, n), F32), grid=(1,),
        compiler_params=_params(("arbitrary",)),
    )(c_all, w_shard, b_shard)


def _ada_bwd(c_all_t, dmod_pad):
    n = dmod_pad.shape[1]

    def body(c_ref, d_ref, o_ref):
        cv = c_ref[...]
        act = (cv * _sigmoid(cv)).astype(BF16)
        o_ref[...] = jnp.dot(act, d_ref[...].astype(BF16), preferred_element_type=F32)

    return pl.pallas_call(
        body, name="ada_bwd", in_specs=[_full((D, LANES)), _full((LANES, n))],
        out_specs=_full((D, n)), out_shape=jax.ShapeDtypeStruct((D, n), F32), grid=(1,),
        compiler_params=_params(("arbitrary",)),
    )(c_all_t, dmod_pad)


ADAM_ROWS = 64


def _adamw(parts, w, m, v, *, name):
    n, R, C = parts.shape
    tr = next((t for t in (ADAM_ROWS, 32, 16, SUBLANES) if R % t == 0), R)

    def body(p_ref, w_ref, m_ref, v_ref, g_ref, d_ref, nm_ref, nv_ref):
        g = p_ref[0].astype(F32)
        for j in range(1, n):
            g = g + p_ref[j].astype(F32)
        g_ref[...] = g
        nm = ADAM_B1 * m_ref[...] + (1.0 - ADAM_B1) * g
        nv = ADAM_B2 * v_ref[...] + (1.0 - ADAM_B2) * (g * g)
        nm_ref[...] = nm
        nv_ref[...] = nv
        m_hat = nm / (1.0 - ADAM_B1 ** ADAM_STEP)
        v_hat = nv / (1.0 - ADAM_B2 ** ADAM_STEP)
        d_ref[...] = -ADAM_LR * (m_hat / (jnp.sqrt(v_hat) + ADAM_EPS) + ADAM_WD * w_ref[...])

    row = pl.BlockSpec((tr, C), lambda i: (i, 0))
    return pl.pallas_call(
        body, name=name, grid=(R // tr,),
        in_specs=[pl.BlockSpec((n, tr, C), lambda i: (0, i, 0)), row, row, row], out_specs=[row] * 4,
        out_shape=[jax.ShapeDtypeStruct((R, C), F32)] * 4,
        compiler_params=_params(("parallel",)),
    )(parts, w, m, v)


SHARDED = (("w_in", D, IN_W, 1), ("w_branch_a", CONV_W, D, 1), ("w_branch_b", ATTN_W, D, 1), ("w_out", D, D, 0),
           ("w_up", D, 2 * D_FF, 1), ("w_down", D_FF, D, 0), ("conv_a_w", 3, CONV_W, 1),
           ("conv_ffn_w", 3, 2 * D_FF, 1))
MATRICES = SHARDED[:6]
LATE = MATRICES[1:]
CONVS = SHARDED[6:]
REPLICATED = (("b_ada", N_MOD * D), ("norm1_g", D), ("norm2_g", D), ("b_f", N_HEADS), ("q_norm_g", HEAD_DIM),
              ("k_norm_g", HEAD_DIM))


def _shard_shape(rows, cols, axis):
    return (rows // N_DEV, cols) if axis == 0 else (rows, cols // N_DEV)


def _pack_rows(flat, multiple):
    length = flat.shape[-1]
    rows = -(-length // PACK_W)
    rows = -(-rows // multiple) * multiple
    pad = [(0, 0)] * (flat.ndim - 1) + [(0, rows * PACK_W - length)]
    return jnp.pad(flat, pad).reshape(flat.shape[:-1] + (rows, PACK_W))


def _pack_shards(shards, spec, multiple, dtype):
    flat = jnp.concatenate([shards[name].reshape(-1).astype(dtype) for name, *_ in spec])
    return _pack_rows(flat, multiple)


def _join_lane_blocks(gathered):
    n, r, c = gathered.shape

    def body(g_ref, o_ref):
        for j in range(n):
            o_ref[:, j * c:(j + 1) * c] = g_ref[j]

    return pl.pallas_call(
        body, name="join_lane_blocks", grid=(1,), in_specs=[_full((n, r, c))], out_specs=_full((r, n * c)),
        out_shape=jax.ShapeDtypeStruct((r, n * c), gathered.dtype), compiler_params=_params(("arbitrary",)),
    )(gathered)


def _join_shards(gathered, axis):
    if axis == 0:
        return gathered.reshape(N_DEV * gathered.shape[1], gathered.shape[2])
    if gathered.shape[2] == LANES:
        return _join_lane_blocks(gathered)
    return jnp.concatenate([gathered[j] for j in range(N_DEV)], axis=1)


def _split_shards(full, axis):
    if axis == 0:
        return full.reshape(N_DEV, full.shape[0] // N_DEV, full.shape[1])
    c = full.shape[1] // N_DEV
    return jnp.stack([full[:, j * c:(j + 1) * c] for j in range(N_DEV)])


def _unpack_shards(packed, spec):
    flat = packed.reshape(-1)
    out, off = {}, 0
    for name, rows, cols, axis in spec:
        r, c = _shard_shape(rows, cols, axis)
        out[name] = flat[off:off + r * c].reshape(r, c)
        off += r * c
    return out


def _unpack_gathered(gathered, spec):
    flat = gathered.reshape(N_DEV, -1)
    out, off = {}, 0
    for name, rows, cols, axis in spec:
        r, c = _shard_shape(rows, cols, axis)
        seg = flat[:, off:off + r * c].reshape(N_DEV, r, c)
        out[name] = seg.reshape(rows, cols) if axis == 0 else seg.transpose(1, 0, 2).reshape(rows, cols)
        off += r * c
    return out


def _pack_full_by_dest(full, spec, multiple):
    segs = []
    for name, rows, cols, axis in spec:
        r, c = _shard_shape(rows, cols, axis)
        a = full[name]
        seg = a.reshape(N_DEV, r, c) if axis == 0 else a.reshape(rows, N_DEV, c).transpose(1, 0, 2)
        segs.append(seg.reshape(N_DEV, r * c))
    return _pack_rows(jnp.concatenate(segs, axis=1), multiple)


def _pad_in(w_in):
    return jnp.concatenate([w_in[:, :COL_GA], w_in[:, COL_GA + N_HEADS:], w_in[:, COL_GA:COL_GA + N_HEADS],
                            jnp.zeros((w_in.shape[0], IN_W_PAD - IN_W), w_in.dtype)], axis=1)


def _pad_head_rows(w):
    n = w.shape[1]
    padded = jnp.pad(w.reshape(N_HEADS, HEAD_DIM, n), ((0, 0), (0, LANES - HEAD_DIM), (0, 0)))
    return padded.reshape(N_HEADS * LANES, n)


def _unpad_in(g):
    return jnp.concatenate([g[:, :COL_GA], g[:, COL_F:COL_F + N_HEADS], g[:, COL_GA:COL_F]], axis=1)


def kernel(x, c, w_ada, b_ada, norm1_g, w_in, b_f, conv_a_w, q_norm_g, k_norm_g, w_branch_a, w_branch_b, w_out, norm2_g, w_up, conv_ffn_w, w_down, loss_target, m_w_ada, m_b_ada, m_norm1_g, m_w_in, m_b_f, m_conv_a_w, m_q_norm_g, m_k_norm_g, m_w_branch_a, m_w_branch_b, m_w_out, m_norm2_g, m_w_up, m_conv_ffn_w, m_w_down, v_w_ada, v_b_ada, v_norm1_g, v_w_in, v_b_f, v_conv_a_w, v_q_norm_g, v_k_norm_g, v_w_branch_a, v_w_branch_b, v_w_out, v_norm2_g, v_w_up, v_conv_ffn_w, v_w_down):
    names = ("w_ada", "b_ada", "norm1_g", "w_in", "b_f", "conv_a_w", "q_norm_g", "k_norm_g", "w_branch_a",
             "w_branch_b", "w_out", "norm2_g", "w_up", "conv_ffn_w", "w_down")
    squeeze = lambda a: a[0] if a.ndim == 3 else a
    W = dict(zip(names, map(squeeze, (w_ada, b_ada, norm1_g, w_in, b_f, conv_a_w, q_norm_g, k_norm_g, w_branch_a,
                                      w_branch_b, w_out, norm2_g, w_up, conv_ffn_w, w_down))))
    M = dict(zip(names, map(squeeze, (m_w_ada, m_b_ada, m_norm1_g, m_w_in, m_b_f, m_conv_a_w, m_q_norm_g,
                                      m_k_norm_g, m_w_branch_a, m_w_branch_b, m_w_out, m_norm2_g, m_w_up,
                                      m_conv_ffn_w, m_w_down))))
    V = dict(zip(names, map(squeeze, (v_w_ada, v_b_ada, v_norm1_g, v_w_in, v_b_f, v_conv_a_w, v_q_norm_g,
                                      v_k_norm_g, v_w_branch_a, v_w_branch_b, v_w_out, v_norm2_g, v_w_up,
                                      v_conv_ffn_w, v_w_down))))
    me = 4 * lax.axis_index("x") + 2 * lax.axis_index("y") + lax.axis_index("c")
    ada_n = N_MOD * D // N_DEV

    small = jnp.concatenate([c.reshape(-1), W["conv_a_w"].reshape(-1), W["conv_ffn_w"].reshape(-1)])
    small_all, w_in_all = _gather_two_level([_pack_rows(small, SUBLANES), W["w_in"].astype(BF16)],
                                            name="gather_first")
    small_all = small_all.reshape(N_DEV, -1)
    c_all = small_all[:, :D]
    conv_all = _unpack_gathered(small_all[:, D:], CONVS)

    b_shard = lax.dynamic_slice(W["b_ada"], (0, me * ada_n), (1, ada_n))
    mod_part = _ada_fwd(c_all, W["w_ada"], b_shard)
    mod_all, = _exchange([mod_part], name="gather_mod", scatter=False)
    mod = lax.dynamic_index_in_dim(mod_all, me, axis=1, keepdims=False).reshape(1, N_MOD * D)

    wts = {"w_in": _pad_in(_join_shards(w_in_all, 1))}
    wts["w_in_t"] = wts["w_in"].T
    wts.update(conv_all)
    for name in ("norm1_g", "norm2_g", "q_norm_g", "k_norm_g", "b_f"):
        wts[name] = W[name]
    late = {name: W[name].astype(BF16) for name, *_ in LATE}

    sq, grad_x, grads, parts = _local_step(x[0], loss_target[0], mod, wts, late)
    loss = lax.psum(sq[0, 0] * (0.5 / D), AXES)

    grads["b_ada"] = grads["mod"]
    rep_flat = lambda src: jnp.concatenate([src[name].reshape(-1) for name, _ in REPLICATED])
    rep_parts, = _exchange([_pack_rows(rep_flat(grads), 16)], name="gather_small_grads", scatter=False)
    rep_out = _adamw(rep_parts, *[_pack_rows(rep_flat(s), 16) for s in (W, M, V)], name="adamw_replicated")

    dmod_all = rep_parts.reshape(N_DEV, -1)[:, :N_MOD * D]
    dmod_mine = lax.dynamic_slice(dmod_all, (0, me * ada_n), (N_DEV, ada_n))
    g_ada = _ada_bwd(jnp.pad(c_all.T, ((0, 0), (0, LANES - N_DEV))),
                     jnp.pad(dmod_mine, ((0, LANES - N_DEV), (0, 0))))
    ada_out = _adamw(g_ada[None], W["w_ada"], M["w_ada"], V["w_ada"], name="adamw_ada")

    mat_out = {name: _adamw(parts[name], W[name], M[name], V[name], name="adamw_" + name) for name, *_ in MATRICES}
    conv_out = _adamw(parts["conv"], *[_pack_shards(s, CONVS, SUBLANES, F32) for s in (W, M, V)],
                      name="adamw_conv")

    results = []
    for kind in range(4):
        per = {"w_ada": ada_out[kind]}
        per.update({name: out[kind] for name, out in mat_out.items()})
        per.update(_unpack_shards(conv_out[kind], CONVS))
        flat, off = rep_out[kind].reshape(-1), 0
        for name, n in REPLICATED:
            per[name] = flat[off:off + n].reshape(1, n)
            off += n
        results.append(per)
    restore = lambda name, a: a[None] if W[name].ndim == 2 and name not in dict(REPLICATED) else a
    outs = [loss, grad_x[None]]
    for per in results:
        outs.extend(restore(name, per[name]) for name in names)
    return tuple(outs)
```

```python
import functools

import jax
import jax.numpy as jnp
import numpy as np
from jax import lax
from jax.experimental import pallas as pl
from jax.experimental.pallas import tpu as pltpu

F32 = jnp.float32
BF16 = jnp.bfloat16

N_DEV = 8
D = 1024
N_HEADS = 8
HEAD_DIM = 64
ATTN_W = 512
CONV_W = 512
D_FF = 2816
N_MOD = 6
IN_W = 5128
RMS_EPS = 1e-6
NEG_INF = -1e30

IN_W_PAD = 5376
COL_GA = 3072
COL_GB = 4096
COL_F = 5120
F_PAD = 128

ADAM_LR = 0.001
ADAM_B1 = 0.9
ADAM_B2 = 0.999
ADAM_EPS = 1e-08
ADAM_WD = 0.01
ADAM_STEP = 10

LANES = 128
SUBLANES = 8
VMEM_LIMIT = 52 * 1024 * 1024
TOKEN_TILE = 512
MATMUL_TILE = 1024
ATTN_BLOCK = 512
PACK_W = 1024

MESH = pl.DeviceIdType.MESH
AXES = ("x", "y", "c")


def _params(sem=None, **kw):
    return pltpu.CompilerParams(dimension_semantics=sem, vmem_limit_bytes=VMEM_LIMIT, **kw)


def _full(shape):
    nd = len(shape)
    return pl.BlockSpec(shape, lambda *_: (0,) * nd)


def _tn_dot(a, b):
    return lax.dot_general(a, b, (((0,), (0,)), ((), ())), preferred_element_type=F32)


def _matmul(a, b, *, name, tm, tn, tk, out_dtype=F32, trans_a=False, exchange=None):
    if trans_a:
        K, M = a.shape
    else:
        M, K = a.shape
    N = b.shape[1]
    assert b.shape[0] == K and M % tm == 0 and N % tn == 0 and K % tk == 0, (name, a.shape, b.shape)
    nm, nn, nk = M // tm, N // tn, K // tk

    def body(*refs):
        if exchange is None:
            a_ref, b_ref, o_ref, *own = refs
        else:
            (a_ref, b_ref), (o_ref,), own, xrefs = exchange.split(refs, 2, 1)
            ids = [pl.program_id(d) for d in range(3)]
            first = jnp.logical_and(jnp.logical_and(ids[0] == 0, ids[1] == 0), ids[2] == 0)
            last = jnp.logical_and(jnp.logical_and(ids[0] == nn - 1, ids[1] == nm - 1), ids[2] == nk - 1)
            _ride(exchange, first, last, xrefs)
        k = pl.program_id(2)
        av = a_ref[...].astype(BF16)
        bv = b_ref[...].astype(BF16)
        prod = _tn_dot(av, bv) if trans_a else jnp.dot(av, bv, preferred_element_type=F32)
        if nk == 1:
            o_ref[...] = prod.astype(out_dtype)
            return
        acc_ref, = own

        @pl.when(k == 0)
        def _():
            acc_ref[...] = prod

        @pl.when(k > 0)
        def _():
            acc_ref[...] += prod

        @pl.when(k == nk - 1)
        def _():
            o_ref[...] = acc_ref[...].astype(out_dtype)

    if trans_a:
        a_spec = pl.BlockSpec((tk, tm), lambda j, i, k: (k, i))
    else:
        a_spec = pl.BlockSpec((tm, tk), lambda j, i, k: (i, k))
    in_specs = [a_spec, pl.BlockSpec((tk, tn), lambda j, i, k: (k, j))]
    out_spec = pl.BlockSpec((tm, tn), lambda j, i, k: (i, j))
    out_shape = jax.ShapeDtypeStruct((M, N), out_dtype)
    scratch = [pltpu.VMEM((tm, tn), F32)] if nk > 1 else []
    if exchange is None:
        return pl.pallas_call(
            body, name=name, grid=(nn, nm, nk), in_specs=in_specs, out_specs=out_spec, out_shape=out_shape,
            scratch_shapes=scratch, compiler_params=_params(("parallel", "parallel", "arbitrary")),
        )(a, b)
    return pl.pallas_call(
        body, name=name, grid=(nn, nm, nk), in_specs=in_specs + exchange.in_specs,
        out_specs=[out_spec] + exchange.out_specs, out_shape=[out_shape] + exchange.out_shapes,
        scratch_shapes=scratch + exchange.scratch, compiler_params=_params(("arbitrary",) * 3),
    )(a, b, *exchange.xs)


def _matmul_pieces(pieces, b, *, name, tm, exchange=None):
    M = pieces[0].shape[0]
    widths = [p.shape[1] for p in pieces]
    offsets = [sum(widths[:i]) for i in range(len(widths))]
    N = b.shape[1]
    assert b.shape[0] >= sum(widths) and M % tm == 0, (name, widths, b.shape)
    n_p, nm = len(pieces), M // tm

    def body(*refs):
        if exchange is None:
            ins, o_ref = refs[:n_p + 1], refs[n_p + 1]
        else:
            ins, (o_ref,), _, xrefs = exchange.split(refs, n_p + 1, 1)
            i = pl.program_id(0)
            _ride(exchange, i == 0, i == nm - 1, xrefs)
        b_ref = ins[n_p]
        acc = None
        for a_ref, off, w in zip(ins[:n_p], offsets, widths):
            term = jnp.dot(a_ref[...].astype(BF16), b_ref[off:off + w, :], preferred_element_type=F32)
            acc = term if acc is None else acc + term
        o_ref[...] = acc

    in_specs = [pl.BlockSpec((tm, w), lambda i: (i, 0)) for w in widths] + [_full(b.shape)]
    out_spec = pl.BlockSpec((tm, N), lambda i: (i, 0))
    out_shape = jax.ShapeDtypeStruct((M, N), F32)
    if exchange is None:
        return pl.pallas_call(
            body, name=name, grid=(nm,), in_specs=in_specs, out_specs=out_spec, out_shape=out_shape,
            compiler_params=_params(("parallel",)),
        )(*pieces, b)
    return pl.pallas_call(
        body, name=name, grid=(nm,), in_specs=in_specs + exchange.in_specs,
        out_specs=[out_spec] + exchange.out_specs, out_shape=[out_shape] + exchange.out_shapes,
        scratch_shapes=exchange.scratch, compiler_params=_params(("arbitrary",)),
    )(*pieces, b, *exchange.xs)


def _matmul_tn_pieces(a, pieces, *, name, tk):
    K, M = a.shape
    widths = [p.shape[1] for p in pieces]
    n_p, nk = len(pieces), K // tk

    def body(*refs):
        a_ref, p_refs, o_refs = refs[0], refs[1:n_p + 1], refs[n_p + 1:]
        k = pl.program_id(0)
        av = a_ref[...].astype(BF16)
        for p_ref, o_ref in zip(p_refs, o_refs):
            prod = _tn_dot(av, p_ref[...].astype(BF16))

            @pl.when(k == 0)
            def _():
                o_ref[...] = prod

            @pl.when(k > 0)
            def _():
                o_ref[...] += prod

    return pl.pallas_call(
        body, name=name, grid=(nk,),
        in_specs=[pl.BlockSpec((tk, M), lambda k: (k, 0))] + [pl.BlockSpec((tk, w), lambda k: (k, 0)) for w in widths],
        out_specs=[_full((M, w)) for w in widths], out_shape=[jax.ShapeDtypeStruct((M, w), F32) for w in widths],
        compiler_params=_params(("arbitrary",)),
    )(a, *pieces)


def _split_dot(x, mat, parts):
    out = None
    rem = x
    for p in range(parts):
        piece = rem.astype(BF16)
        term = jnp.dot(piece, mat, preferred_element_type=F32)
        out = term if out is None else out + term
        if p + 1 < parts:
            rem = rem - piece.astype(F32)
    return out


def _sigmoid(x):
    return 1.0 / (1.0 + jnp.exp(-x))


def _rows8(x):
    r, c = x.shape
    return jnp.sum(x.reshape(r // SUBLANES, SUBLANES, c), axis=0)


def _shift_down(blk, prev8, n):
    rolled = pltpu.roll(blk, n, axis=0)
    prev_rolled = pltpu.roll(prev8, n, axis=0)
    rows = lax.broadcasted_iota(jnp.int32, prev8.shape, 0)
    first = jnp.where(rows < n, prev_rolled, rolled[0:SUBLANES])
    return jnp.concatenate([first, rolled[SUBLANES:]], axis=0)


def _shift_up(blk, next8, n):
    r = blk.shape[0]
    rolled = pltpu.roll(blk, r - n, axis=0)
    next_rolled = pltpu.roll(next8, SUBLANES - n, axis=0)
    rows = lax.broadcasted_iota(jnp.int32, next8.shape, 0)
    last = jnp.where(rows >= SUBLANES - n, next_rolled, rolled[r - SUBLANES:])
    return jnp.concatenate([rolled[:r - SUBLANES], last], axis=0)


def _prev_spec(tm, width, col):
    per = tm // SUBLANES
    return pl.BlockSpec((SUBLANES, width), lambda i, *_: (jnp.maximum(i * per - 1, 0), col))


def _next_spec(tm, width, col, n_tiles):
    per = tm // SUBLANES
    last = n_tiles * per - 1
    return pl.BlockSpec((SUBLANES, width), lambda i, *_: (jnp.minimum((i + 1) * per, last), col))


def _group_matrix():
    idx = np.arange(ATTN_W) // HEAD_DIM
    return jnp.asarray((idx[:, None] == idx[None, :]).astype(np.float32), BF16)


def _norm_mod(x, g, sc, sh, *, name):
    T = x.shape[0]
    tm = min(TOKEN_TILE, T)

    def body(x_ref, g_ref, sc_ref, sh_ref, o_ref):
        xv = x_ref[...]
        inv = lax.rsqrt(jnp.mean(xv * xv, axis=-1, keepdims=True) + RMS_EPS)
        o_ref[...] = ((xv * inv) * g_ref[...] * (1.0 + sc_ref[...]) + sh_ref[...]).astype(BF16)

    row = pl.BlockSpec((tm, D), lambda i: (i, 0))
    return pl.pallas_call(
        body, name=name, grid=(T // tm,),
        in_specs=[row, _full((1, D)), _full((1, D)), _full((1, D))],
        out_specs=row, out_shape=jax.ShapeDtypeStruct((T, D), BF16),
        compiler_params=_params(("parallel",)),
    )(x, g, sc, sh)


LANE_ONE = 64
LANE_F = 67
LANE_LSE = 70
LANE_SUM = 73


def _pieces(x):
    hi = x.astype(BF16).astype(F32)
    rest = x - hi
    mid = rest.astype(BF16).astype(F32)
    return hi, mid, rest - mid


def _aug(lane, data, entries):
    out = jnp.where(lane < HEAD_DIM, data, 0.0)
    for idx, val in entries:
        out = jnp.where(lane == idx, val, out)
    return out


def _run(start, vals):
    return [(start + i, v) for i, v in enumerate(vals)]


def _head_lanes(a, h):
    blk = a[:, LANES * (h // 2):LANES * (h // 2) + LANES]
    return blk if h % 2 == 0 else pltpu.roll(blk, HEAD_DIM, axis=1)


def _branch_prep(proj, fcum, conv_w8, qg, kg, gmat):
    T = proj.shape[0]
    tm = min(TOKEN_TILE, T)
    nt = T // tm

    def body(cb_ref, cc_ref, cv_ref, q_ref, k_ref, v_ref, f_ref, ccp_ref, cvp_ref, w_ref, qg_ref, kg_ref, g_ref,
             ya_ref, qa_ref, ka_ref, va_ref):
        i = pl.program_id(0)
        z = cc_ref[...] * cv_ref[...]
        zp = jnp.where(i > 0, ccp_ref[...] * cvp_ref[...], 0.0)
        w = w_ref[...]
        cz = _shift_down(z, zp, 2) * w[0:1] + _shift_down(z, zp, 1) * w[1:2] + z * w[2:3]
        ya_ref[...] = (cb_ref[...] * cz).astype(BF16)
        gm = g_ref[...]

        def normed(src, gain, scale):
            v = src[...]
            ms = _split_dot(v * v, gm, 2) * (1.0 / HEAD_DIM)
            return (v * lax.rsqrt(ms + RMS_EPS)) * gain[...] * scale

        qn = normed(q_ref, qg_ref, 1.0 / np.sqrt(HEAD_DIM))
        kn = normed(k_ref, kg_ref, 1.0)
        vv = v_ref[...]
        fall = f_ref[...]
        lane = lax.broadcasted_iota(jnp.int32, (tm, LANES), 1)
        ones3 = [1.0, 1.0, 1.0]
        for h in range(N_HEADS):
            hi, mid, lo = _pieces(fall[:, h:h + 1])
            qa_ref[h] = _aug(lane, _head_lanes(qn, h), _run(LANE_ONE, ones3) + _run(LANE_F, [hi, mid, lo])
                             + [(LANE_SUM, 1.0)]).astype(BF16)
            ka_ref[h] = _aug(lane, _head_lanes(kn, h), _run(LANE_ONE, [-hi, -mid, -lo]) + _run(LANE_F, ones3)
                             + _run(LANE_LSE, ones3)).astype(BF16)
            va_ref[h] = _aug(lane, _head_lanes(vv, h), _run(LANE_ONE, ones3)).astype(BF16)

    blk = lambda col: pl.BlockSpec((tm, CONV_W), lambda i: (i, col))
    heads = pl.BlockSpec((N_HEADS, tm, LANES), lambda i: (0, i, 0))
    return pl.pallas_call(
        body, name="branch_prep", grid=(nt,),
        in_specs=[blk(0), blk(1), blk(2), blk(3), blk(4), blk(5), pl.BlockSpec((tm, F_PAD), lambda i: (i, 0)),
                  _prev_spec(tm, CONV_W, 1), _prev_spec(tm, CONV_W, 2),
                  _full((SUBLANES, CONV_W)), _full((1, ATTN_W)), _full((1, ATTN_W)), _full((ATTN_W, ATTN_W))],
        out_specs=[pl.BlockSpec((tm, CONV_W), lambda i: (i, 0)), heads, heads, heads],
        out_shape=[jax.ShapeDtypeStruct((T, CONV_W), BF16)] + [jax.ShapeDtypeStruct((N_HEADS, T, LANES), BF16)] * 3,
        compiler_params=_params(("parallel",)),
    )(proj, proj, proj, proj, proj, proj, fcum, proj, proj, conv_w8, qg, kg, gmat)


def _cumsum(x, *, reverse, name, col=0, gate_bias=None):
    T = x.shape[0]
    tm = min(TOKEN_TILE, T)
    nt = T // tm

    def body(x_ref, b_ref, o_ref, carry_ref):
        i = pl.program_id(0)

        @pl.when(i == 0)
        def _():
            carry_ref[...] = jnp.zeros_like(carry_ref)

        r = lax.broadcasted_iota(jnp.int32, (tm, tm), 0)
        c = lax.broadcasted_iota(jnp.int32, (tm, tm), 1)
        tri = jnp.where((c >= r) if reverse else (c <= r), 1.0, 0.0).astype(BF16)
        xv = x_ref[...]
        if gate_bias is not None:
            fx = xv + b_ref[...]
            xv = jnp.minimum(fx, 0.0) - jnp.log(1.0 + jnp.exp(-jnp.abs(fx)))
        out = _split_dot_left(tri, xv, 3) + carry_ref[0:1]
        o_ref[...] = out
        carry_ref[...] = jnp.broadcast_to(out[0:1] if reverse else out[tm - 1:tm], carry_ref.shape)

    rows = (lambda i: nt - 1 - i) if reverse else (lambda i: i)
    bias = jnp.zeros((1, F_PAD), F32) if gate_bias is None else gate_bias
    return pl.pallas_call(
        body, name=name, grid=(nt,),
        in_specs=[pl.BlockSpec((tm, F_PAD), lambda i: (rows(i), col)), _full((1, F_PAD))],
        out_specs=pl.BlockSpec((tm, F_PAD), lambda i: (rows(i), 0)),
        out_shape=jax.ShapeDtypeStruct((T, F_PAD), F32),
        scratch_shapes=[pltpu.VMEM((SUBLANES, F_PAD), F32)],
        compiler_params=_params(("arbitrary",)),
    )(x, bias)


def _split_dot_left(mat, x, parts):
    out = None
    rem = x
    for p in range(parts):
        piece = rem.astype(BF16)
        term = jnp.dot(mat, piece, preferred_element_type=F32)
        out = term if out is None else out + term
        if p + 1 < parts:
            rem = rem - piece.astype(F32)
    return out


def _resid_norm(x, mix, g1, g, sc, sh):
    T = x.shape[0]
    tm = min(TOKEN_TILE, T)

    def body(x_ref, mix_ref, g1_ref, g_ref, sc_ref, sh_ref, x1_ref, h_ref):
        x1 = x_ref[...] + g1_ref[...] * mix_ref[...]
        x1_ref[...] = x1
        inv = lax.rsqrt(jnp.mean(x1 * x1, axis=-1, keepdims=True) + RMS_EPS)
        h_ref[...] = ((x1 * inv) * g_ref[...] * (1.0 + sc_ref[...]) + sh_ref[...]).astype(BF16)

    row = pl.BlockSpec((tm, D), lambda i: (i, 0))
    vec = _full((1, D))
    return pl.pallas_call(
        body, name="resid_norm", grid=(T // tm,),
        in_specs=[row, row, vec, vec, vec, vec], out_specs=[row, row],
        out_shape=[jax.ShapeDtypeStruct((T, D), F32), jax.ShapeDtypeStruct((T, D), BF16)],
        compiler_params=_params(("parallel",)),
    )(x, mix, g1, g, sc, sh)


FFN_TM = 256
FFN_TC = 1408


def _ffn_act_fwd(u, w8):
    T = u.shape[0]
    tm = min(FFN_TM, T)
    nt = T // tm
    nc = D_FF // FFN_TC

    def body(ug_ref, uv_ref, ugp_ref, uvp_ref, wg_ref, wv_ref, o_ref):
        i = pl.program_id(1)

        def conv(u_ref, p_ref, w_ref):
            uv = u_ref[...]
            up = jnp.where(i > 0, p_ref[...], 0.0)
            w = w_ref[...]
            return _shift_down(uv, up, 2) * w[0:1] + _shift_down(uv, up, 1) * w[1:2] + uv * w[2:3]

        gate = conv(ug_ref, ugp_ref, wg_ref)
        val = conv(uv_ref, uvp_ref, wv_ref)
        o_ref[...] = (gate * _sigmoid(gate) * val).astype(BF16)

    per = tm // SUBLANES
    blk = lambda off: pl.BlockSpec((tm, FFN_TC), lambda j, i: (i, j + off))
    prev = lambda off: pl.BlockSpec((SUBLANES, FFN_TC), lambda j, i: (jnp.maximum(i * per - 1, 0), j + off))
    wblk = lambda off: pl.BlockSpec((SUBLANES, FFN_TC), lambda j, i: (0, j + off))
    return pl.pallas_call(
        body, name="ffn_act_fwd", grid=(nc, nt),
        in_specs=[blk(0), blk(nc), prev(0), prev(nc), wblk(0), wblk(nc)],
        out_specs=pl.BlockSpec((tm, FFN_TC), lambda j, i: (i, j)),
        out_shape=jax.ShapeDtypeStruct((T, D_FF), BF16),
        compiler_params=_params(("parallel", "parallel")),
    )(u, u, u, u, w8, w8)


def _loss_head(x1, ff, g2, target):
    T = x1.shape[0]
    tm = min(TOKEN_TILE, T)

    def body(x1_ref, ff_ref, g2_ref, t_ref, dy_ref, dff_ref, loss_ref, dg2_ref):
        i = pl.program_id(0)

        @pl.when(i == 0)
        def _():
            loss_ref[...] = jnp.zeros_like(loss_ref)
            dg2_ref[...] = jnp.zeros_like(dg2_ref)

        ff = ff_ref[...]
        err = x1_ref[...] + g2_ref[...] * ff - t_ref[...]
        dy = err * (1.0 / D)
        dy_ref[...] = dy
        dff_ref[...] = (dy * g2_ref[...]).astype(BF16)
        loss_ref[...] += _rows8(err * err)
        dg2_ref[...] += _rows8(dy * ff)

    row = pl.BlockSpec((tm, D), lambda i: (i, 0))
    acc = _full((SUBLANES, D))
    return pl.pallas_call(
        body, name="loss_head", grid=(T // tm,),
        in_specs=[row, row, _full((1, D)), row], out_specs=[row, row, acc, acc],
        out_shape=[jax.ShapeDtypeStruct((T, D), F32), jax.ShapeDtypeStruct((T, D), BF16),
                   jax.ShapeDtypeStruct((SUBLANES, D), F32), jax.ShapeDtypeStruct((SUBLANES, D), F32)],
        compiler_params=_params(("arbitrary",)),
    )(x1, ff, g2, target)


def _nt_dot(a, b):
    return lax.dot_general(a, b, (((1,), (1,)), ((), ())), preferred_element_type=F32)


def _causal(n, keys_on_rows=False):
    r = lax.broadcasted_iota(jnp.int32, (n, n), 0)
    c = lax.broadcasted_iota(jnp.int32, (n, n), 1)
    return (c >= r) if keys_on_rows else (c <= r)


def _sweep(lo, hi, step, carry):
    pairs = (hi - lo) // 2
    carry = lax.fori_loop(0, pairs, lambda j, cr: step(lo + 2 * j + 1, step(lo + 2 * j, cr)), carry)
    return lax.fori_loop(lo + 2 * pairs, hi, step, carry)


def _grid_ends(n0, n1):
    i0, i1 = pl.program_id(0), pl.program_id(1)
    return jnp.logical_and(i0 == 0, i1 == 0), jnp.logical_and(i0 == n0 - 1, i1 == n1 - 1)


def _attn_fwd(qa, ka, va, exchange=None):
    nh, T, _ = qa.shape
    bq = min(ATTN_BLOCK, T)
    nq = T // bq

    def body(*refs):
        if exchange is None:
            q_ref, k_ref, v_ref, o_ref, qb_ref = refs
        else:
            (q_ref, k_ref, v_ref), (o_ref, qb_ref), _, xrefs = exchange.split(refs, 3, 2)
            _ride(exchange, *_grid_ends(nh, nq), xrefs)
        qi = pl.program_id(1)
        q = q_ref[0]

        def step(kb, carry, masked):
            m, acc = carry
            start = pl.multiple_of(kb * bq, bq)
            s = _nt_dot(q, k_ref[0, pl.ds(start, bq), :])
            if masked:
                s = jnp.where(_causal(bq), s, NEG_INF)
            m_new = jnp.maximum(m, jnp.max(s, axis=-1, keepdims=True))
            p = jnp.exp(s - m_new).astype(BF16)
            acc = jnp.exp(m - m_new) * acc + jnp.dot(p, v_ref[0, pl.ds(start, bq), :], preferred_element_type=F32)
            return m_new, acc

        init = (jnp.full((bq, 1), NEG_INF, F32), jnp.zeros((bq, LANES), F32))
        carry = _sweep(0, qi, lambda kb, cr: step(kb, cr, False), init)
        m, acc = step(qi, carry, True)
        l = acc[:, LANE_ONE:LANE_ONE + 1]
        o_ref[0] = acc / l
        lse = m + jnp.log(l)
        lane = lax.broadcasted_iota(jnp.int32, (bq, LANES), 1)
        qf = q.astype(F32)
        for idx, piece in _run(LANE_LSE, _pieces(lse)):
            qf = jnp.where(lane == idx, -piece, qf)
        qb_ref[0] = qf.astype(BF16)

    tile = pl.BlockSpec((1, bq, LANES), lambda h, i: (h, i, 0))
    whole = pl.BlockSpec((1, T, LANES), lambda h, i: (h, 0, 0))
    out_shape = [jax.ShapeDtypeStruct((nh, T, LANES), F32), jax.ShapeDtypeStruct((nh, T, LANES), BF16)]
    if exchange is None:
        return pl.pallas_call(
            body, name="attn_fwd", grid=(nh, nq), in_specs=[tile, whole, whole], out_specs=[tile, tile],
            out_shape=out_shape, compiler_params=_params(("parallel", "parallel")),
        )(qa, ka, va)
    return pl.pallas_call(
        body, name="attn_fwd", grid=(nh, nq), in_specs=[tile, whole, whole] + exchange.in_specs,
        out_specs=[tile, tile] + exchange.out_specs, out_shape=out_shape + exchange.out_shapes,
        scratch_shapes=exchange.scratch, compiler_params=_params(("arbitrary", "arbitrary")),
    )(qa, ka, va, *exchange.xs)


def _branch_merge_fwd(ya0, o_h, proj, wba, wbb_heads):
    nh, T, _ = o_h.shape
    tm = min(TOKEN_TILE, T)

    def body(ya0_ref, o_ref, ga_ref, gb_ref, wa_ref, wb_ref, ya_ref, yb_ref, m_ref):
        ya = jnp.dot(ya0_ref[...], wa_ref[...], preferred_element_type=F32)
        yb = jnp.dot(o_ref[0].astype(BF16), wb_ref[0:LANES, :], preferred_element_type=F32)
        for h in range(1, nh):
            yb += jnp.dot(o_ref[h].astype(BF16), wb_ref[h * LANES:(h + 1) * LANES, :], preferred_element_type=F32)
        ya_ref[...] = ya
        yb_ref[...] = yb
        m_ref[...] = (_sigmoid(ga_ref[...]) * ya + _sigmoid(gb_ref[...]) * yb).astype(BF16)

    row = pl.BlockSpec((tm, D), lambda i: (i, 0))
    return pl.pallas_call(
        body, name="branch_merge_fwd", grid=(T // tm,),
        in_specs=[pl.BlockSpec((tm, CONV_W), lambda i: (i, 0)), pl.BlockSpec((nh, tm, LANES), lambda i: (0, i, 0)),
                  pl.BlockSpec((tm, D), lambda i: (i, COL_GA // D)), pl.BlockSpec((tm, D), lambda i: (i, COL_GB // D)),
                  _full((CONV_W, D)), _full((nh * LANES, D))],
        out_specs=[row, row, row],
        out_shape=[jax.ShapeDtypeStruct((T, D), F32), jax.ShapeDtypeStruct((T, D), F32),
                   jax.ShapeDtypeStruct((T, D), BF16)],
        compiler_params=_params(("parallel",)),
    )(ya0, o_h, proj, proj, wba, wbb_heads)


def _branch_b_bwd(dyb, o_h, wbb_heads_t):
    nh, T, _ = o_h.shape
    tm = min(TOKEN_TILE, T)

    def body(dyb_ref, o_ref, w_ref, out_ref):
        do = jnp.dot(dyb_ref[...], w_ref[...], preferred_element_type=F32)
        lane = lax.broadcasted_iota(jnp.int32, (tm, LANES), 1)
        for h in range(nh):
            g = do[:, h * LANES:(h + 1) * LANES].astype(BF16).astype(F32)
            delta = jnp.sum(g * o_ref[h], axis=-1, keepdims=True)
            for idx, piece in _run(LANE_ONE, _pieces(delta)):
                g = jnp.where(lane == idx, -piece, g)
            out_ref[h] = g.astype(BF16)

    heads = pl.BlockSpec((nh, tm, LANES), lambda i: (0, i, 0))
    return pl.pallas_call(
        body, name="branch_b_bwd", grid=(T // tm,),
        in_specs=[pl.BlockSpec((tm, D), lambda i: (i, 0)), heads, _full((D, nh * LANES))],
        out_specs=heads, out_shape=jax.ShapeDtypeStruct((nh, T, LANES), BF16),
        compiler_params=_params(("parallel",)),
    )(dyb, o_h, wbb_heads_t)


def _branch_b_dw(o_h, dyb):
    nh, T, _ = o_h.shape
    tk = min(TOKEN_TILE, T)

    def body(o_ref, dyb_ref, out_ref):
        @pl.when(pl.program_id(0) == 0)
        def _():
            out_ref[...] = jnp.zeros_like(out_ref)

        g = dyb_ref[...]
        for h in range(nh):
            out_ref[h] += _tn_dot(o_ref[h].astype(BF16), g)

    return pl.pallas_call(
        body, name="branch_b_dw", grid=(T // tk,),
        in_specs=[pl.BlockSpec((nh, tk, LANES), lambda k: (0, k, 0)), pl.BlockSpec((tk, D), lambda k: (k, 0))],
        out_specs=_full((nh, LANES, D)), out_shape=jax.ShapeDtypeStruct((nh, LANES, D), F32),
        compiler_params=_params(("arbitrary",)),
    )(o_h, dyb)


def _attn_bwd(qb, ka, va, doa, exchange=None):
    nh, T, _ = qb.shape
    bk = min(ATTN_BLOCK, T)
    nk = T // bk

    def body(*refs):
        if exchange is None:
            q_ref, do_ref, k_ref, v_ref, dq_ref, dk_ref, dv_ref = refs
        else:
            (q_ref, do_ref, k_ref, v_ref), (dq_ref, dk_ref, dv_ref), _, xrefs = exchange.split(refs, 4, 3)
            _ride(exchange, *_grid_ends(nh, nk), xrefs)
        ki = pl.program_id(1)

        @pl.when(ki == 0)
        def _():
            dq_ref[...] = jnp.zeros_like(dq_ref)

        k = k_ref[0]
        v = v_ref[0]

        def step(qi, carry, masked):
            dk, dv = carry
            rows = pl.ds(pl.multiple_of(qi * bk, bk), bk)
            q = q_ref[0, rows, :]
            g = do_ref[0, rows, :]
            pt = jnp.exp(_nt_dot(k, q))
            if masked:
                pt = jnp.where(_causal(bk, keys_on_rows=True), pt, 0.0)
            dv = dv + jnp.dot(pt.astype(BF16), g, preferred_element_type=F32)
            dst = (pt * _nt_dot(v, g)).astype(BF16)
            dk = dk + jnp.dot(dst, q, preferred_element_type=F32)
            dq_ref[0, rows, :] += _tn_dot(dst, k)
            return dk, dv

        init = (jnp.zeros((bk, LANES), F32), jnp.zeros((bk, LANES), F32))
        carry = step(ki, init, True)
        dk_ref[0], dv_ref[0] = _sweep(ki + 1, nk, lambda qi, cr: step(qi, cr, False), carry)

    tile = pl.BlockSpec((1, bk, LANES), lambda h, i: (h, i, 0))
    whole = pl.BlockSpec((1, T, LANES), lambda h, i: (h, 0, 0))
    out_shape = [jax.ShapeDtypeStruct((nh, T, LANES), F32)] * 3
    if exchange is None:
        return pl.pallas_call(
            body, name="attn_bwd", grid=(nh, nk), in_specs=[whole, whole, tile, tile],
            out_specs=[whole, tile, tile], out_shape=out_shape, compiler_params=_params(("parallel", "arbitrary")),
        )(qb, doa, ka, va)
    return pl.pallas_call(
        body, name="attn_bwd", grid=(nh, nk), in_specs=[whole, whole, tile, tile] + exchange.in_specs,
        out_specs=[whole, tile, tile] + exchange.out_specs, out_shape=out_shape + exchange.out_shapes,
        scratch_shapes=exchange.scratch, compiler_params=_params(("arbitrary", "arbitrary")),
    )(qb, doa, ka, va, *exchange.xs)


def _attn_unpack(dq_h, dk_h, dv_h):
    nh, T, _ = dq_h.shape
    tm = min(TOKEN_TILE, T)

    def body(dq_ref, dk_ref, dv_ref, q_out, k_out, v_out, f_out):
        lane = lax.broadcasted_iota(jnp.int32, (tm, LANES), 1)
        low = lane < HEAD_DIM
        for src, dst in ((dq_ref, q_out), (dk_ref, k_out), (dv_ref, v_out)):
            for pair in range(nh // 2):
                both = jnp.where(low, src[2 * pair], pltpu.roll(src[2 * pair + 1], HEAD_DIM, axis=1))
                dst[:, LANES * pair:LANES * (pair + 1)] = both.astype(dst.dtype)
        df = jnp.zeros((tm, LANES), F32)
        for h in range(nh):
            col = dq_ref[h][:, LANE_F:LANE_F + 1] - dk_ref[h][:, LANE_SUM:LANE_SUM + 1]
            df = jnp.where(lane == h, col, df)
        f_out[...] = df

    heads = pl.BlockSpec((nh, tm, LANES), lambda i: (0, i, 0))
    tok = pl.BlockSpec((tm, ATTN_W), lambda i: (i, 0))
    return pl.pallas_call(
        body, name="attn_unpack", grid=(T // tm,), in_specs=[heads, heads, heads],
        out_specs=[tok, tok, tok, pl.BlockSpec((tm, F_PAD), lambda i: (i, 0))],
        out_shape=[jax.ShapeDtypeStruct((T, ATTN_W), F32), jax.ShapeDtypeStruct((T, ATTN_W), F32),
                   jax.ShapeDtypeStruct((T, ATTN_W), BF16), jax.ShapeDtypeStruct((T, F_PAD), F32)],
        compiler_params=_params(("parallel",)),
    )(dq_h, dk_h, dv_h)


def _ffn_act_bwd(u, da, w8):
    T = u.shape[0]
    tm = min(FFN_TM, T)
    nt = T // tm
    nc = D_FF // FFN_TC

    def body(ug_ref, uv_ref, ugp_ref, uvp_ref, ugn_ref, uvn_ref, da_ref, dan_ref, wg_ref, wv_ref,
             dug_ref, duv_ref, dwg_ref, dwv_ref):
        i = pl.program_id(1)

        @pl.when(i == 0)
        def _():
            dwg_ref[...] = jnp.zeros_like(dwg_ref)
            dwv_ref[...] = jnp.zeros_like(dwv_ref)

        first, last = i == 0, i == nt - 1
        wg, wv = wg_ref[...], wv_ref[...]
        zeros8 = jnp.zeros((SUBLANES, FFN_TC), F32)

        def window(u_ref, p_ref, n_ref, w):
            e = jnp.concatenate([jnp.where(first, 0.0, p_ref[...]), u_ref[...], n_ref[...]], axis=0)
            e1 = pltpu.roll(e, 1, axis=0)
            e2 = pltpu.roll(e, 2, axis=0)
            return e, e1, e2, e2 * w[0:1] + e1 * w[1:2] + e * w[2:3]

        eg, eg1, eg2, cg = window(ug_ref, ugp_ref, ugn_ref, wg)
        ev, ev1, ev2, cv = window(uv_ref, uvp_ref, uvn_ref, wv)
        dae = jnp.concatenate([zeros8, da_ref[...].astype(F32), jnp.where(last, 0.0, dan_ref[...].astype(F32))],
                              axis=0)
        sg = _sigmoid(cg)
        dgate = dae * cv * sg * (1.0 + cg * (1.0 - sg))
        dval = dae * cg * sg
        n = tm + 2 * SUBLANES

        def back(d, w):
            return d * w[2:3] + pltpu.roll(d, n - 1, axis=0) * w[1:2] + pltpu.roll(d, n - 2, axis=0) * w[0:1]

        inner = slice(SUBLANES, SUBLANES + tm)
        dug_ref[...] = back(dgate, wg)[inner].astype(BF16)
        duv_ref[...] = back(dval, wv)[inner].astype(BF16)

        def wgrad(d, e, e1, e2):
            rows = [jnp.sum((d * t)[inner], axis=0, keepdims=True) for t in (e2, e1, e)]
            return jnp.concatenate(rows + [jnp.zeros((SUBLANES - 3, FFN_TC), F32)], axis=0)

        dwg_ref[...] += wgrad(dgate, eg, eg1, eg2)
        dwv_ref[...] += wgrad(dval, ev, ev1, ev2)

    per = tm // SUBLANES
    last_blk = nt * per - 1
    blk = lambda off: pl.BlockSpec((tm, FFN_TC), lambda j, i: (i, j + off))
    prev = lambda off: pl.BlockSpec((SUBLANES, FFN_TC), lambda j, i: (jnp.maximum(i * per - 1, 0), j + off))
    nxt = lambda off: pl.BlockSpec((SUBLANES, FFN_TC), lambda j, i: (jnp.minimum((i + 1) * per, last_blk), j + off))
    wblk = lambda off: pl.BlockSpec((SUBLANES, FFN_TC), lambda j, i: (0, j + off))
    dug, duv, dwg, dwv = pl.pallas_call(
        body, name="ffn_act_bwd", grid=(nc, nt),
        in_specs=[blk(0), blk(nc), prev(0), prev(nc), nxt(0), nxt(nc), blk(0), nxt(0), wblk(0), wblk(nc)],
        out_specs=[blk(0), blk(0), wblk(0), wblk(0)],
        out_shape=[jax.ShapeDtypeStruct((T, D_FF), BF16)] * 2 + [jax.ShapeDtypeStruct((SUBLANES, D_FF), F32)] * 2,
        compiler_params=_params(("parallel", "arbitrary")),
    )(u, u, u, u, u, u, da, da, w8, w8)
    return dug, duv, jnp.concatenate([dwg, dwv], axis=1)


def _norm_bwd(xin, dh, dres, g, sc, *, name):
    T = xin.shape[0]
    tm = min(TOKEN_TILE, T)

    def body(x_ref, dh_ref, dr_ref, g_ref, sc_ref, dx_ref, dsh_ref, dsc_ref, dg_ref):
        i = pl.program_id(0)

        @pl.when(i == 0)
        def _():
            dsh_ref[...] = jnp.zeros_like(dsh_ref)
            dsc_ref[...] = jnp.zeros_like(dsc_ref)
            dg_ref[...] = jnp.zeros_like(dg_ref)

        xv = x_ref[...]
        dh = dh_ref[...]
        gv = g_ref[...]
        one_sc = 1.0 + sc_ref[...]
        inv = lax.rsqrt(jnp.mean(xv * xv, axis=-1, keepdims=True) + RMS_EPS)
        xn = xv * inv
        dxn = dh * (gv * one_sc)
        dx_ref[...] = dr_ref[...] + inv * (dxn - xn * jnp.mean(dxn * xn, axis=-1, keepdims=True))
        dhxn = dh * xn
        dsh_ref[...] += _rows8(dh)
        dsc_ref[...] += _rows8(dhxn * gv)
        dg_ref[...] += _rows8(dhxn * one_sc)

    row = pl.BlockSpec((tm, D), lambda i: (i, 0))
    acc = _full((SUBLANES, D))
    return pl.pallas_call(
        body, name=name, grid=(T // tm,),
        in_specs=[row, row, row, _full((1, D)), _full((1, D))], out_specs=[row, acc, acc, acc],
        out_shape=[jax.ShapeDtypeStruct((T, D), F32)] + [jax.ShapeDtypeStruct((SUBLANES, D), F32)] * 3,
        compiler_params=_params(("arbitrary",)),
    )(xin, dh, dres, g, sc)


def _gate_bwd(dx1, mix, g1):
    T = dx1.shape[0]
    tm = min(TOKEN_TILE, T)

    def body(dx_ref, mix_ref, g1_ref, dmix_ref, dg1_ref):
        @pl.when(pl.program_id(0) == 0)
        def _():
            dg1_ref[...] = jnp.zeros_like(dg1_ref)

        dx = dx_ref[...]
        dmix_ref[...] = (dx * g1_ref[...]).astype(BF16)
        dg1_ref[...] += _rows8(dx * mix_ref[...])

    row = pl.BlockSpec((tm, D), lambda i: (i, 0))
    return pl.pallas_call(
        body, name="gate_bwd", grid=(T // tm,),
        in_specs=[row, row, _full((1, D))], out_specs=[row, _full((SUBLANES, D))],
        out_shape=[jax.ShapeDtypeStruct((T, D), BF16), jax.ShapeDtypeStruct((SUBLANES, D), F32)],
        compiler_params=_params(("arbitrary",)),
    )(dx1, mix, g1)


def _merge_bwd(dmerged, ya, yb, proj):
    T = ya.shape[0]
    tm = min(TOKEN_TILE, T)

    def body(dm_ref, ya_ref, yb_ref, ga_ref, gb_ref, dya_ref, dyb_ref, dga_ref, dgb_ref):
        dm = dm_ref[...]
        sa = _sigmoid(ga_ref[...])
        sb = _sigmoid(gb_ref[...])
        dya_ref[...] = (dm * sa).astype(BF16)
        dyb_ref[...] = (dm * sb).astype(BF16)
        dga_ref[...] = (dm * ya_ref[...] * sa * (1.0 - sa)).astype(BF16)
        dgb_ref[...] = (dm * yb_ref[...] * sb * (1.0 - sb)).astype(BF16)

    row = pl.BlockSpec((tm, D), lambda i: (i, 0))
    return pl.pallas_call(
        body, name="merge_bwd", grid=(T // tm,),
        in_specs=[row, row, row, pl.BlockSpec((tm, D), lambda i: (i, COL_GA // D)),
                  pl.BlockSpec((tm, D), lambda i: (i, COL_GB // D))],
        out_specs=[row] * 4, out_shape=[jax.ShapeDtypeStruct((T, D), BF16)] * 4,
        compiler_params=_params(("parallel",)),
    )(dmerged, ya, yb, proj, proj)


def _conv_branch_bwd(proj, dya0, conv_w8):
    T = proj.shape[0]
    tm = min(FFN_TM, T)
    nt = T // tm

    def body(cb_ref, cc_ref, cv_ref, cbn_ref, ccp_ref, cvp_ref, ccn_ref, cvn_ref, d_ref, dn_ref, w_ref,
             dcb_ref, dcc_ref, dcv_ref, dw_ref):
        i = pl.program_id(0)

        @pl.when(i == 0)
        def _():
            dw_ref[...] = jnp.zeros_like(dw_ref)

        first, last = i == 0, i == nt - 1
        w = w_ref[...]
        cc = jnp.concatenate([ccp_ref[...], cc_ref[...], ccn_ref[...]], axis=0)
        cv = jnp.concatenate([cvp_ref[...], cv_ref[...], cvn_ref[...]], axis=0)
        rows = lax.broadcasted_iota(jnp.int32, cc.shape, 0)
        z = jnp.where(jnp.logical_and(first, rows < SUBLANES), 0.0, cc * cv)
        z1 = pltpu.roll(z, 1, axis=0)
        z2 = pltpu.roll(z, 2, axis=0)
        cz = z2 * w[0:1] + z1 * w[1:2] + z * w[2:3]
        zeros8 = jnp.zeros((SUBLANES, CONV_W), F32)
        de = jnp.concatenate([zeros8, d_ref[...], jnp.where(last, 0.0, dn_ref[...])], axis=0)
        cbe = jnp.concatenate([zeros8, cb_ref[...], cbn_ref[...]], axis=0)
        dcz = de * cbe
        n = tm + 2 * SUBLANES
        dz = dcz * w[2:3] + pltpu.roll(dcz, n - 1, axis=0) * w[1:2] + pltpu.roll(dcz, n - 2, axis=0) * w[0:1]
        inner = slice(SUBLANES, SUBLANES + tm)
        dcb_ref[...] = (de * cz)[inner].astype(BF16)
        dcc_ref[...] = (dz * cv)[inner].astype(BF16)
        dcv_ref[...] = (dz * cc)[inner].astype(BF16)
        wrows = [jnp.sum((dcz * t)[inner], axis=0, keepdims=True) for t in (z2, z1, z)]
        dw_ref[...] += jnp.concatenate(wrows + [jnp.zeros((SUBLANES - 3, CONV_W), F32)], axis=0)

    blk = lambda col: pl.BlockSpec((tm, CONV_W), lambda i: (i, col))
    out_blk = pl.BlockSpec((tm, CONV_W), lambda i: (i, 0))
    return pl.pallas_call(
        body, name="conv_branch_bwd", grid=(nt,),
        in_specs=[blk(0), blk(1), blk(2), _next_spec(tm, CONV_W, 0, nt),
                  _prev_spec(tm, CONV_W, 1), _prev_spec(tm, CONV_W, 2),
                  _next_spec(tm, CONV_W, 1, nt), _next_spec(tm, CONV_W, 2, nt),
                  out_blk, _next_spec(tm, CONV_W, 0, nt), _full((SUBLANES, CONV_W))],
        out_specs=[out_blk, out_blk, out_blk, _full((SUBLANES, CONV_W))],
        out_shape=[jax.ShapeDtypeStruct((T, CONV_W), BF16)] * 3 + [jax.ShapeDtypeStruct((SUBLANES, CONV_W), F32)],
        compiler_params=_params(("arbitrary",)),
    )(proj, proj, proj, proj, proj, proj, proj, proj, dya0, dya0, conv_w8)


def _qk_norm_bwd(proj, dqs, dkh, dlogf, qg, kg, bf_pad, gmat):
    T = proj.shape[0]
    tm = min(TOKEN_TILE, T)

    def body(q_ref, k_ref, f_ref, dqs_ref, dkh_ref, dlf_ref, qg_ref, kg_ref, bf_ref, g_ref,
             dq_ref, dk_ref, dfl_ref, dqg_ref, dkg_ref, dbf_ref):
        @pl.when(pl.program_id(0) == 0)
        def _():
            dqg_ref[...] = jnp.zeros_like(dqg_ref)
            dkg_ref[...] = jnp.zeros_like(dkg_ref)
            dbf_ref[...] = jnp.zeros_like(dbf_ref)

        gm = g_ref[...]
        for src, d_src, gain, scale, dst, dgain in (
                (q_ref, dqs_ref, qg_ref, 1.0 / np.sqrt(HEAD_DIM), dq_ref, dqg_ref),
                (k_ref, dkh_ref, kg_ref, 1.0, dk_ref, dkg_ref)):
            v = src[...]
            dhat = d_src[...] * scale
            inv = lax.rsqrt(_split_dot(v * v, gm, 2) * (1.0 / HEAD_DIM) + RMS_EPS)
            vn = v * inv
            dgain[...] += _rows8(dhat * vn)
            dvn = dhat * gain[...]
            mean = _split_dot(dvn * vn, gm, 2) * (1.0 / HEAD_DIM)
            dst[...] = (inv * (dvn - vn * mean)).astype(BF16)
        fx = f_ref[...] + bf_ref[...]
        dfl = dlf_ref[...] * _sigmoid(-fx)
        dfl_ref[...] = dfl.astype(BF16)
        dbf_ref[...] += _rows8(dfl)

    blk = lambda col: pl.BlockSpec((tm, ATTN_W), lambda i: (i, col))
    out_blk = pl.BlockSpec((tm, ATTN_W), lambda i: (i, 0))
    f_in = pl.BlockSpec((tm, F_PAD), lambda i: (i, COL_F // F_PAD))
    f_blk = pl.BlockSpec((tm, F_PAD), lambda i: (i, 0))
    return pl.pallas_call(
        body, name="qk_norm_bwd", grid=(T // tm,),
        in_specs=[blk(3), blk(4), f_in, out_blk, out_blk, f_blk, _full((1, ATTN_W)), _full((1, ATTN_W)),
                  _full((1, F_PAD)), _full((ATTN_W, ATTN_W))],
        out_specs=[out_blk, out_blk, f_blk, _full((SUBLANES, ATTN_W)), _full((SUBLANES, ATTN_W)),
                   _full((SUBLANES, F_PAD))],
        out_shape=[jax.ShapeDtypeStruct((T, ATTN_W), BF16)] * 2 + [jax.ShapeDtypeStruct((T, F_PAD), BF16)]
        + [jax.ShapeDtypeStruct((SUBLANES, ATTN_W), F32)] * 2 + [jax.ShapeDtypeStruct((SUBLANES, F_PAD), F32)],
        compiler_params=_params(("arbitrary",)),
    )(proj, proj, proj, dqs, dkh, dlogf, qg, kg, bf_pad, gmat)


def _pad_rows8(w):
    return jnp.pad(w, ((0, SUBLANES - w.shape[0]), (0, 0)))


def _fold8(acc):
    return jnp.sum(acc, axis=0, keepdims=True)


def _late_weights(mats):
    out = {}
    for name in ("w_branch_a", "w_out", "w_up", "w_down"):
        out[name] = mats[name]
        out[name + "_t"] = mats[name].T
    out["w_branch_b_heads"] = _pad_head_rows(mats["w_branch_b"])
    out["w_branch_b_heads_t"] = out["w_branch_b_heads"].T
    return out


def _local_step(x, target, mod, wts, late=None):
    T = x.shape[0]
    tb = min(MATMUL_TILE, T)
    tm = min(TOKEN_TILE, T)
    sh1, sc1, g1, sh2, sc2, g2 = [mod[:, i * D:(i + 1) * D] for i in range(N_MOD)]
    w_in, w_in_t = wts["w_in"], wts["w_in_t"]
    conv_a8 = _pad_rows8(wts["conv_a_w"])
    conv_f8 = _pad_rows8(wts["conv_ffn_w"])
    qg = jnp.tile(wts["q_norm_g"], (1, N_HEADS))
    kg = jnp.tile(wts["k_norm_g"], (1, N_HEADS))
    bf_pad = jnp.pad(wts["b_f"], ((0, 0), (0, F_PAD - N_HEADS)))
    gmat = _group_matrix()

    h = _norm_mod(x, wts["norm1_g"], sc1, sh1, name="norm1_fwd")
    proj = _matmul(h, w_in, name="mm_in", tm=tb, tn=896, tk=D)
    fcum = _cumsum(proj, reverse=False, name="gate_cumsum", col=COL_F // F_PAD, gate_bias=bf_pad)
    ya0, qa, ka, va = _branch_prep(proj, fcum, conv_a8, qg, kg, gmat)
    if late is None:
        o_h, qb = _attn_fwd(qa, ka, va)
    else:
        o_h, qb, *gathered = _attn_fwd(qa, ka, va, _Exchange([late[name] for name, *_ in LATE], scatter=False))
        wts = dict(wts)
        wts.update(_late_weights({name: _join_shards(g, axis) for (name, _, _, axis), g in zip(LATE, gathered)}))
    ya, yb, merged = _branch_merge_fwd(ya0, o_h, proj, wts["w_branch_a"], wts["w_branch_b_heads"])
    mix = _matmul(merged, wts["w_out"], name="mm_out", tm=tb, tn=D, tk=D)
    x1, h2 = _resid_norm(x, mix, g1, wts["norm2_g"], sc2, sh2)
    u = _matmul(h2, wts["w_up"], name="mm_up", tm=tb, tn=1408, tk=D)
    act = _ffn_act_fwd(u, conv_f8)
    ff = _matmul(act, wts["w_down"], name="mm_down", tm=tb, tn=D, tk=D_FF)
    dy, dff, sq8, dg2_8 = _loss_head(x1, ff, g2, target)
    sq = jnp.sum(sq8).reshape(1, 1)

    grads = {}
    da = _matmul(dff, wts["w_down_t"], name="mm_down_dx", tm=tb, tn=1408, tk=D)
    grads["w_down"] = _matmul(act, dff, name="mm_down_dw", tm=1408, tn=D, tk=tb, trans_a=True)
    dug, duv, dconv_f8 = _ffn_act_bwd(u, da, conv_f8)
    grads["conv_ffn_w"] = dconv_f8[:3]
    dh2 = _matmul_pieces([dug, duv], wts["w_up_t"], name="mm_up_dx", tm=tm)
    grads["w_up"] = jnp.concatenate(
        [_matmul(h2, d, name="mm_up_dw_" + half, tm=D, tn=1408, tk=tb, trans_a=True)
         for half, d in (("gate", dug), ("val", duv))], axis=1)
    dx1, dsh2_8, dsc2_8, dn2_8 = _norm_bwd(x1, dh2, dy, wts["norm2_g"], sc2, name="norm2_bwd")
    grads["norm2_g"] = _fold8(dn2_8)

    dmix, dg1_8 = _gate_bwd(dx1, mix, g1)
    dmerged = _matmul(dmix, wts["w_out_t"], name="mm_out_dx", tm=tb, tn=D, tk=D)
    grads["w_out"] = _matmul(merged, dmix, name="mm_out_dw", tm=D, tn=D, tk=tb, trans_a=True)
    dya, dyb, dga, dgb = _merge_bwd(dmerged, ya, yb, proj)
    dya0 = _matmul(dya, wts["w_branch_a_t"], name="mm_branch_a_dx", tm=tb, tn=CONV_W, tk=D)
    grads["w_branch_a"] = _matmul(ya0, dya, name="mm_branch_a_dw", tm=CONV_W, tn=D, tk=tb, trans_a=True)
    doa = _branch_b_bwd(dyb, o_h, wts["w_branch_b_heads_t"])
    grads["w_branch_b"] = _branch_b_dw(o_h, dyb)[:, :HEAD_DIM].reshape(ATTN_W, D)
    dcb, dcc, dcv, dconv_a8 = _conv_branch_bwd(proj, dya0, conv_a8)
    grads["conv_a_w"] = dconv_a8[:3]

    parts = {}
    if late is None:
        dq_h, dk_h, dv_h = _attn_bwd(qb, ka, va, doa)
    else:
        ready = [_split_shards(grads[name], axis).astype(BF16) for name, _, _, axis in LATE]
        dq_h, dk_h, dv_h, *recv = _attn_bwd(
            qb, ka, va, doa, _Exchange(ready + [_pack_full_by_dest(grads, CONVS, SUBLANES)], scatter=True))
        parts = dict(zip([name for name, *_ in LATE] + ["conv"], recv))
    dq_tok, dk_tok, dv_tok, dfcum = _attn_unpack(dq_h, dk_h, dv_h)
    dlogf = _cumsum(dfcum, reverse=True, name="gate_cumsum_bwd")
    dq, dk, dfl, dqg8, dkg8, dbf8 = _qk_norm_bwd(proj, dq_tok, dk_tok, dlogf, qg, kg, bf_pad, gmat)
    grads["q_norm_g"] = jnp.sum(_fold8(dqg8).reshape(N_HEADS, HEAD_DIM), axis=0, keepdims=True)
    grads["k_norm_g"] = jnp.sum(_fold8(dkg8).reshape(N_HEADS, HEAD_DIM), axis=0, keepdims=True)
    grads["b_f"] = _fold8(dbf8)[:, :N_HEADS]
    narrow, wide = [dcb, dcc, dcv, dq, dk, dv_tok], [dga, dgb, dfl]
    dw_narrow = _matmul_tn_pieces(h, narrow, name="mm_in_dw_narrow", tk=tm)
    dwa, dwb, dwf = _matmul_tn_pieces(h, wide, name="mm_in_dw_wide", tk=tm)
    grads["w_in"] = jnp.concatenate(list(dw_narrow) + [dwf[:, :N_HEADS], dwa, dwb], axis=1)
    if late is None:
        dh = _matmul_pieces(narrow + wide, w_in_t, name="mm_in_dx", tm=tm)
    else:
        dh, parts["w_in"] = _matmul_pieces(
            narrow + wide, w_in_t, name="mm_in_dx", tm=tm,
            exchange=_Exchange([_split_shards(grads["w_in"], 1).astype(BF16)], scatter=True))
    grad_x, dsh1_8, dsc1_8, dn1_8 = _norm_bwd(x, dh, dx1, wts["norm1_g"], sc1, name="norm1_bwd")
    grads["norm1_g"] = _fold8(dn1_8)
    grads["mod"] = jnp.concatenate([_fold8(a) for a in (dsh1_8, dsc1_8, dg1_8, dsh2_8, dsc2_8, dg2_8)], axis=1)
    return sq, grad_x, grads, parts


def _me_and_peers():
    mx, my, mc = lax.axis_index("x"), lax.axis_index("y"), lax.axis_index("c")
    me = 4 * mx + 2 * my + mc
    peers = []
    for k in range(1, N_DEV):
        px = 1 - mx if k & 4 else mx
        py = 1 - my if k & 2 else my
        pc = 1 - mc if k & 1 else mc
        peers.append(((px, py, pc), 4 * px + 2 * py + pc))
    return me, peers


HBM_SPEC = pl.BlockSpec(memory_space=pltpu.HBM)


class _Exchange:
    def __init__(self, xs, scatter):
        self.xs, self.scatter, self.n = list(xs), scatter, len(xs)
        self.out_shapes = [jax.ShapeDtypeStruct(x.shape if scatter else (N_DEV,) + x.shape, x.dtype) for x in xs]
        self.in_specs = [HBM_SPEC] * self.n
        self.out_specs = [HBM_SPEC] * self.n
        self.scratch = [pltpu.SemaphoreType.DMA((self.n, N_DEV - 1)), pltpu.SemaphoreType.DMA((self.n, N_DEV - 1)),
                        pltpu.SemaphoreType.DMA((self.n,))]

    def _copies(self, x_refs, out_refs, sems):
        send_sems, recv_sems, local_sems = sems
        me, peers = _me_and_peers()

        def src(a, idx):
            return x_refs[a].at[idx] if self.scatter else x_refs[a]

        def copy(a, k, from_idx, to_slot, device):
            return pltpu.make_async_remote_copy(
                src_ref=src(a, from_idx), dst_ref=out_refs[a].at[to_slot], send_sem=send_sems.at[a, k],
                recv_sem=recv_sems.at[a, k], device_id=device, device_id_type=MESH)

        local = [pltpu.make_async_copy(src(a, me), out_refs[a].at[me], local_sems.at[a]) for a in range(self.n)]
        sends = [copy(a, k, idx, me, dev) for a in range(self.n) for k, (dev, idx) in enumerate(peers)]
        recvs = [copy(a, k, idx, idx, dev) for a in range(self.n) for k, (dev, idx) in enumerate(peers)]
        return local, sends, recvs

    def start(self, x_refs, out_refs, sems):
        local, sends, _ = self._copies(x_refs, out_refs, sems)
        for cp in local + sends:
            cp.start()

    def wait(self, x_refs, out_refs, sems):
        local, sends, recvs = self._copies(x_refs, out_refs, sems)
        for cp in recvs:
            cp.wait_recv()
        for cp in sends:
            cp.wait_send()
        for cp in local:
            cp.wait()

    def split(self, refs, n_in, n_out):
        n = self.n
        ins, xin = refs[:n_in], refs[n_in:n_in + n]
        outs, xout = refs[n_in + n:n_in + n + n_out], refs[n_in + n + n_out:n_in + 2 * n + n_out]
        rest = refs[n_in + 2 * n + n_out:]
        return ins, outs, rest[:len(rest) - 3], (xin, xout, rest[len(rest) - 3:])


def _ride(exchange, first, last, refs):
    if exchange is None:
        return

    @pl.when(first)
    def _():
        exchange.start(*refs)

    @pl.when(last)
    def _():
        exchange.wait(*refs)


def _gather_two_level(xs, *, name):
    n = len(xs)
    out_shapes = [jax.ShapeDtypeStruct((N_DEV,) + x.shape, x.dtype) for x in xs]

    def body(*refs):
        x_refs, out_refs = refs[:n], refs[n:2 * n]
        send_sems, recv_sems, local_sems = refs[2 * n:]
        x, y, c = lax.axis_index("x"), lax.axis_index("y"), lax.axis_index("c")
        me, sibling = (x, y, c), (x, y, 1 - c)
        chips = [(1 - x, y), (x, 1 - y), (1 - x, 1 - y)]

        def slot(a, dev):
            return out_refs[a].at[4 * dev[0] + 2 * dev[1] + dev[2]]

        def copy(a, k, block, to, src=None):
            return pltpu.make_async_remote_copy(
                src_ref=slot(a, block) if src is None else src, dst_ref=slot(a, block),
                send_sem=send_sems.at[a, k], recv_sem=recv_sems.at[a, k], device_id=to, device_id_type=MESH)

        mine = [pltpu.make_async_copy(x_refs[a], slot(a, me), local_sems.at[a]) for a in range(n)]
        first = [copy(a, 0, me, sibling, src=x_refs[a]) for a in range(n)]
        first += [copy(a, 1 + j, me, (*chip, c), src=x_refs[a]) for a in range(n) for j, chip in enumerate(chips)]
        for cp in mine + first:
            cp.start()
        passed = []
        for a in range(n):
            for j, chip in enumerate(chips):
                copy(a, 1 + j, (*chip, c), me).wait_recv()
                passed.append(copy(a, 4 + j, (*chip, c), sibling))
                passed[-1].start()
        for a in range(n):
            copy(a, 0, sibling, me).wait_recv()
            for j, chip in enumerate(chips):
                copy(a, 4 + j, (*chip, 1 - c), me).wait_recv()
        for cp in first + passed:
            cp.wait_send()
        for cp in mine:
            cp.wait()

    return pl.pallas_call(
        body, name=name, in_specs=[HBM_SPEC] * n, out_specs=[HBM_SPEC] * n, out_shape=out_shapes,
        scratch_shapes=[pltpu.SemaphoreType.DMA((n, N_DEV - 1)), pltpu.SemaphoreType.DMA((n, N_DEV - 1)),
                        pltpu.SemaphoreType.DMA((n,))],
        compiler_params=pltpu.CompilerParams(has_side_effects=True),
    )(*xs)


def _exchange(xs, *, name, scatter):
    ex = _Exchange(xs, scatter)

    def body(*refs):
        _, _, _, xrefs = ex.split(refs, 0, 0)
        ex.start(*xrefs)
        ex.wait(*xrefs)

    return pl.pallas_call(
        body, name=name, in_specs=ex.in_specs, out_specs=ex.out_specs, out_shape=ex.out_shapes,
        scratch_shapes=ex.scratch, compiler_params=pltpu.CompilerParams(has_side_effects=True),
    )(*xs)


def _ada_fwd(c_all, w_shard, b_shard):
    n = w_shard.shape[1]

    def body(c_ref, w_ref, b_ref, o_ref):
        cv = c_ref[...]
        act = (cv * _sigmoid(cv)).astype(BF16)
        o_ref[...] = jnp.dot(act, w_ref[...].astype(BF16), preferred_element_type=F32) + b_ref[...]

    return pl.pallas_call(
        body, name="ada_fwd", in_specs=[_full((N_DEV, D)), _full((D, n)), _full((1, n))],
        out_specs=_full((N_DEV, n)), out_shape=jax.ShapeDtypeStruct((N_DEV, n), F32), grid=(1,),
        compiler_params=_params(("arbitrary",)),
    )(c_all, w_shard, b_shard)


def _ada_bwd(c_all_t, dmod_pad):
    n = dmod_pad.shape[1]

    def body(c_ref, d_ref, o_ref):
        cv = c_ref[...]
        act = (cv * _sigmoid(cv)).astype(BF16)
        o_ref[...] = jnp.dot(act, d_ref[...].astype(BF16), preferred_element_type=F32)

    return pl.pallas_call(
        body, name="ada_bwd", in_specs=[_full((D, LANES)), _full((LANES, n))],
        out_specs=_full((D, n)), out_shape=jax.ShapeDtypeStruct((D, n), F32), grid=(1,),
        compiler_params=_params(("arbitrary",)),
    )(c_all_t, dmod_pad)


ADAM_ROWS = 64


def _adamw(parts, w, m, v, *, name):
    n, R, C = parts.shape
    tr = next((t for t in (ADAM_ROWS, 32, 16, SUBLANES) if R % t == 0), R)

    def body(p_ref, w_ref, m_ref, v_ref, g_ref, d_ref, nm_ref, nv_ref):
        g = p_ref[0].astype(F32)
        for j in range(1, n):
            g = g + p_ref[j].astype(F32)
        g_ref[...] = g
        nm = ADAM_B1 * m_ref[...] + (1.0 - ADAM_B1) * g
        nv = ADAM_B2 * v_ref[...] + (1.0 - ADAM_B2) * (g * g)
        nm_ref[...] = nm
        nv_ref[...] = nv
        m_hat = nm / (1.0 - ADAM_B1 ** ADAM_STEP)
        v_hat = nv / (1.0 - ADAM_B2 ** ADAM_STEP)
        d_ref[...] = -ADAM_LR * (m_hat / (jnp.sqrt(v_hat) + ADAM_EPS) + ADAM_WD * w_ref[...])

    row = pl.BlockSpec((tr, C), lambda i: (i, 0))
    return pl.pallas_call(
        body, name=name, grid=(R // tr,),
        in_specs=[pl.BlockSpec((n, tr, C), lambda i: (0, i, 0)), row, row, row], out_specs=[row] * 4,
        out_shape=[jax.ShapeDtypeStruct((R, C), F32)] * 4,
        compiler_params=_params(("parallel",)),
    )(parts, w, m, v)


SHARDED = (("w_in", D, IN_W, 1), ("w_branch_a", CONV_W, D, 1), ("w_branch_b", ATTN_W, D, 1), ("w_out", D, D, 0),
           ("w_up", D, 2 * D_FF, 1), ("w_down", D_FF, D, 0), ("conv_a_w", 3, CONV_W, 1),
           ("conv_ffn_w", 3, 2 * D_FF, 1))
MATRICES = SHARDED[:6]
LATE = MATRICES[1:]
CONVS = SHARDED[6:]
REPLICATED = (("b_ada", N_MOD * D), ("norm1_g", D), ("norm2_g", D), ("b_f", N_HEADS), ("q_norm_g", HEAD_DIM),
              ("k_norm_g", HEAD_DIM))


def _shard_shape(rows, cols, axis):
    return (rows // N_DEV, cols) if axis == 0 else (rows, cols // N_DEV)


def _pack_rows(flat, multiple):
    length = flat.shape[-1]
    rows = -(-length // PACK_W)
    rows = -(-rows // multiple) * multiple
    pad = [(0, 0)] * (flat.ndim - 1) + [(0, rows * PACK_W - length)]
    return jnp.pad(flat, pad).reshape(flat.shape[:-1] + (rows, PACK_W))


def _pack_shards(shards, spec, multiple, dtype):
    flat = jnp.concatenate([shards[name].reshape(-1).astype(dtype) for name, *_ in spec])
    return _pack_rows(flat, multiple)


def _join_lane_blocks(gathered):
    n, r, c = gathered.shape

    def body(g_ref, o_ref):
        for j in range(n):
            o_ref[:, j * c:(j + 1) * c] = g_ref[j]

    return pl.pallas_call(
        body, name="join_lane_blocks", grid=(1,), in_specs=[_full((n, r, c))], out_specs=_full((r, n * c)),
        out_shape=jax.ShapeDtypeStruct((r, n * c), gathered.dtype), compiler_params=_params(("arbitrary",)),
    )(gathered)


def _join_shards(gathered, axis):
    if axis == 0:
        return gathered.reshape(N_DEV * gathered.shape[1], gathered.shape[2])
    if gathered.shape[2] == LANES:
        return _join_lane_blocks(gathered)
    return jnp.concatenate([gathered[j] for j in range(N_DEV)], axis=1)


def _split_shards(full, axis):
    if axis == 0:
        return full.reshape(N_DEV, full.shape[0] // N_DEV, full.shape[1])
    c = full.shape[1] // N_DEV
    return jnp.stack([full[:, j * c:(j + 1) * c] for j in range(N_DEV)])


def _unpack_shards(packed, spec):
    flat = packed.reshape(-1)
    out, off = {}, 0
    for name, rows, cols, axis in spec:
        r, c = _shard_shape(rows, cols, axis)
        out[name] = flat[off:off + r * c].reshape(r, c)
        off += r * c
    return out


def _unpack_gathered(gathered, spec):
    flat = gathered.reshape(N_DEV, -1)
    out, off = {}, 0
    for name, rows, cols, axis in spec:
        r, c = _shard_shape(rows, cols, axis)
        seg = flat[:, off:off + r * c].reshape(N_DEV, r, c)
        out[name] = seg.reshape(rows, cols) if axis == 0 else seg.transpose(1, 0, 2).reshape(rows, cols)
        off += r * c
    return out


def _pack_full_by_dest(full, spec, multiple):
    segs = []
    for name, rows, cols, axis in spec:
        r, c = _shard_shape(rows, cols, axis)
        a = full[name]
        seg = a.reshape(N_DEV, r, c) if axis == 0 else a.reshape(rows, N_DEV, c).transpose(1, 0, 2)
        segs.append(seg.reshape(N_DEV, r * c))
    return _pack_rows(jnp.concatenate(segs, axis=1), multiple)


def _pad_in(w_in):
    return jnp.concatenate([w_in[:, :COL_GA], w_in[:, COL_GA + N_HEADS:], w_in[:, COL_GA:COL_GA + N_HEADS],
                            jnp.zeros((w_in.shape[0], IN_W_PAD - IN_W), w_in.dtype)], axis=1)


def _pad_head_rows(w):
    n = w.shape[1]
    padded = jnp.pad(w.reshape(N_HEADS, HEAD_DIM, n), ((0, 0), (0, LANES - HEAD_DIM), (0, 0)))
    return padded.reshape(N_HEADS * LANES, n)


def kernel(x, c, w_ada, b_ada, norm1_g, w_in, b_f, conv_a_w, q_norm_g, k_norm_g, w_branch_a, w_branch_b, w_out, norm2_g, w_up, conv_ffn_w, w_down, loss_target, m_w_ada, m_b_ada, m_norm1_g, m_w_in, m_b_f, m_conv_a_w, m_q_norm_g, m_k_norm_g, m_w_branch_a, m_w_branch_b, m_w_out, m_norm2_g, m_w_up, m_conv_ffn_w, m_w_down, v_w_ada, v_b_ada, v_norm1_g, v_w_in, v_b_f, v_conv_a_w, v_q_norm_g, v_k_norm_g, v_w_branch_a, v_w_branch_b, v_w_out, v_norm2_g, v_w_up, v_conv_ffn_w, v_w_down):
    names = ("w_ada", "b_ada", "norm1_g", "w_in", "b_f", "conv_a_w", "q_norm_g", "k_norm_g", "w_branch_a",
             "w_branch_b", "w_out", "norm2_g", "w_up", "conv_ffn_w", "w_down")
    squeeze = lambda a: a[0] if a.ndim == 3 else a
    W = dict(zip(names, map(squeeze, (w_ada, b_ada, norm1_g, w_in, b_f, conv_a_w, q_norm_g, k_norm_g, w_branch_a,
                                      w_branch_b, w_out, norm2_g, w_up, conv_ffn_w, w_down))))
    M = dict(zip(names, map(squeeze, (m_w_ada, m_b_ada, m_norm1_g, m_w_in, m_b_f, m_conv_a_w, m_q_norm_g,
                                      m_k_norm_g, m_w_branch_a, m_w_branch_b, m_w_out, m_norm2_g, m_w_up,
                                      m_conv_ffn_w, m_w_down))))
    V = dict(zip(names, map(squeeze, (v_w_ada, v_b_ada, v_norm1_g, v_w_in, v_b_f, v_conv_a_w, v_q_norm_g,
                                      v_k_norm_g, v_w_branch_a, v_w_branch_b, v_w_out, v_norm2_g, v_w_up,
                                      v_conv_ffn_w, v_w_down))))
    me = 4 * lax.axis_index("x") + 2 * lax.axis_index("y") + lax.axis_index("c")
    ada_n = N_MOD * D // N_DEV

    small = jnp.concatenate([c.reshape(-1), W["conv_a_w"].reshape(-1), W["conv_ffn_w"].reshape(-1)])
    small_all, w_in_all = _gather_two_level([_pack_rows(small, SUBLANES), W["w_in"].astype(BF16)],
                                            name="gather_first")
    small_all = small_all.reshape(N_DEV, -1)
    c_all = small_all[:, :D]
    conv_all = _unpack_gathered(small_all[:, D:], CONVS)

    b_shard = lax.dynamic_slice(W["b_ada"], (0, me * ada_n), (1, ada_n))
    mod_part = _ada_fwd(c_all, W["w_ada"], b_shard)
    mod_all, = _exchange([mod_part], name="gather_mod", scatter=False)
    mod = lax.dynamic_index_in_dim(mod_all, me, axis=1, keepdims=False).reshape(1, N_MOD * D)

    wts = {"w_in": _pad_in(_join_shards(w_in_all, 1))}
    wts["w_in_t"] = wts["w_in"].T
    wts.update(conv_all)
    for name in ("norm1_g", "norm2_g", "q_norm_g", "k_norm_g", "b_f"):
        wts[name] = W[name]
    late = {name: W[name].astype(BF16) for name, *_ in LATE}

    sq, grad_x, grads, parts = _local_step(x[0], loss_target[0], mod, wts, late)
    loss = lax.psum(sq[0, 0] * (0.5 / D), AXES)

    grads["b_ada"] = grads["mod"]
    rep_flat = lambda src: jnp.concatenate([src[name].reshape(-1) for name, _ in REPLICATED])
    rep_parts, = _exchange([_pack_rows(rep_flat(grads), 16)], name="gather_small_grads", scatter=False)
    rep_out = _adamw(rep_parts, *[_pack_rows(rep_flat(s), 16) for s in (W, M, V)], name="adamw_replicated")

    dmod_all = rep_parts.reshape(N_DEV, -1)[:, :N_MOD * D]
    dmod_mine = lax.dynamic_slice(dmod_all, (0, me * ada_n), (N_DEV, ada_n))
    g_ada = _ada_bwd(jnp.pad(c_all.T, ((0, 0), (0, LANES - N_DEV))),
                     jnp.pad(dmod_mine, ((0, LANES - N_DEV), (0, 0))))
    ada_out = _adamw(g_ada[None], W["w_ada"], M["w_ada"], V["w_ada"], name="adamw_ada")

    mat_out = {name: _adamw(parts[name], W[name], M[name], V[name], name="adamw_" + name) for name, *_ in MATRICES}
    conv_out = _adamw(parts["conv"], *[_pack_shards(s, CONVS, SUBLANES, F32) for s in (W, M, V)],
                      name="adamw_conv")

    results = []
    for kind in range(4):
        per = {"w_ada": ada_out[kind]}
        per.update({name: out[kind] for name, out in mat_out.items()})
        per.update(_unpack_shards(conv_out[kind], CONVS))
        flat, off = rep_out[kind].reshape(-1), 0
        for name, n in REPLICATED:
            per[name] = flat[off:off + n].reshape(1, n)
            off += n
        results.append(per)
    restore = lambda name, a: a[None] if W[name].ndim == 2 and name not in dict(REPLICATED) else a
    outs = [loss, grad_x[None]]
    for per in results:
        outs.extend(restore(name, per[name]) for name in names)
    return tuple(outs)
```

```python
import functools

import jax
import jax.numpy as jnp
import numpy as np
from jax import lax
from jax.experimental import pallas as pl
from jax.experimental.pallas import tpu as pltpu

F32 = jnp.float32
BF16 = jnp.bfloat16

N_DEV = 8
D = 1024
N_HEADS = 8
HEAD_DIM = 64
ATTN_W = 512
CONV_W = 512
D_FF = 2816
N_MOD = 6
IN_W = 5128
RMS_EPS = 1e-6
NEG_INF = -1e30

IN_W_PAD = 5376
COL_GA = 3072
COL_GB = 4096
COL_F = 5120
F_PAD = 128

ADAM_LR = 0.001
ADAM_B1 = 0.9
ADAM_B2 = 0.999
ADAM_EPS = 1e-08
ADAM_WD = 0.01
ADAM_STEP = 10

LANES = 128
SUBLANES = 8
VMEM_LIMIT = 52 * 1024 * 1024
TOKEN_TILE = 512
MATMUL_TILE = 1024
ATTN_BLOCK = 512
PACK_W = 1024

MESH = pl.DeviceIdType.MESH
AXES = ("x", "y", "c")


def _params(sem=None, **kw):
    return pltpu.CompilerParams(dimension_semantics=sem, vmem_limit_bytes=VMEM_LIMIT, **kw)


def _full(shape):
    nd = len(shape)
    return pl.BlockSpec(shape, lambda *_: (0,) * nd)


def _tn_dot(a, b):
    return lax.dot_general(a, b, (((0,), (0,)), ((), ())), preferred_element_type=F32)


def _matmul(a, b, *, name, tm, tn, tk, out_dtype=F32, trans_a=False, exchange=None):
    if trans_a:
        K, M = a.shape
    else:
        M, K = a.shape
    N = b.shape[1]
    assert b.shape[0] == K and M % tm == 0 and N % tn == 0 and K % tk == 0, (name, a.shape, b.shape)
    nm, nn, nk = M // tm, N // tn, K // tk

    def body(*refs):
        if exchange is None:
            a_ref, b_ref, o_ref, *own = refs
        else:
            (a_ref, b_ref), (o_ref,), own, xrefs = exchange.split(refs, 2, 1)
            ids = [pl.program_id(d) for d in range(3)]
            first = jnp.logical_and(jnp.logical_and(ids[0] == 0, ids[1] == 0), ids[2] == 0)
            last = jnp.logical_and(jnp.logical_and(ids[0] == nn - 1, ids[1] == nm - 1), ids[2] == nk - 1)
            _ride(exchange, first, last, xrefs)
        k = pl.program_id(2)
        av = a_ref[...].astype(BF16)
        bv = b_ref[...].astype(BF16)
        prod = _tn_dot(av, bv) if trans_a else jnp.dot(av, bv, preferred_element_type=F32)
        if nk == 1:
            o_ref[...] = prod.astype(out_dtype)
            return
        acc_ref, = own

        @pl.when(k == 0)
        def _():
            acc_ref[...] = prod

        @pl.when(k > 0)
        def _():
            acc_ref[...] += prod

        @pl.when(k == nk - 1)
        def _():
            o_ref[...] = acc_ref[...].astype(out_dtype)

    if trans_a:
        a_spec = pl.BlockSpec((tk, tm), lambda j, i, k: (k, i))
    else:
        a_spec = pl.BlockSpec((tm, tk), lambda j, i, k: (i, k))
    in_specs = [a_spec, pl.BlockSpec((tk, tn), lambda j, i, k: (k, j))]
    out_spec = pl.BlockSpec((tm, tn), lambda j, i, k: (i, j))
    out_shape = jax.ShapeDtypeStruct((M, N), out_dtype)
    scratch = [pltpu.VMEM((tm, tn), F32)] if nk > 1 else []
    if exchange is None:
        return pl.pallas_call(
            body, name=name, grid=(nn, nm, nk), in_specs=in_specs, out_specs=out_spec, out_shape=out_shape,
            scratch_shapes=scratch, compiler_params=_params(("parallel", "parallel", "arbitrary")),
        )(a, b)
    return pl.pallas_call(
        body, name=name, grid=(nn, nm, nk), in_specs=in_specs + exchange.in_specs,
        out_specs=[out_spec] + exchange.out_specs, out_shape=[out_shape] + exchange.out_shapes,
        scratch_shapes=scratch + exchange.scratch, compiler_params=_params(("arbitrary",) * 3),
    )(a, b, *exchange.xs)


def _matmul_pieces(pieces, b, *, name, tm, exchange=None):
    M = pieces[0].shape[0]
    widths = [p.shape[1] for p in pieces]
    offsets = [sum(widths[:i]) for i in range(len(widths))]
    N = b.shape[1]
    assert b.shape[0] >= sum(widths) and M % tm == 0, (name, widths, b.shape)
    n_p, nm = len(pieces), M // tm

    def body(*refs):
        if exchange is None:
            ins, o_ref = refs[:n_p + 1], refs[n_p + 1]
        else:
            ins, (o_ref,), _, xrefs = exchange.split(refs, n_p + 1, 1)
            i = pl.program_id(0)
            _ride(exchange, i == 0, i == nm - 1, xrefs)
        b_ref = ins[n_p]
        acc = None
        for a_ref, off, w in zip(ins[:n_p], offsets, widths):
            term = jnp.dot(a_ref[...].astype(BF16), b_ref[off:off + w, :], preferred_element_type=F32)
            acc = term if acc is None else acc + term
        o_ref[...] = acc

    in_specs = [pl.BlockSpec((tm, w), lambda i: (i, 0)) for w in widths] + [_full(b.shape)]
    out_spec = pl.BlockSpec((tm, N), lambda i: (i, 0))
    out_shape = jax.ShapeDtypeStruct((M, N), F32)
    if exchange is None:
        return pl.pallas_call(
            body, name=name, grid=(nm,), in_specs=in_specs, out_specs=out_spec, out_shape=out_shape,
            compiler_params=_params(("parallel",)),
        )(*pieces, b)
    return pl.pallas_call(
        body, name=name, grid=(nm,), in_specs=in_specs + exchange.in_specs,
        out_specs=[out_spec] + exchange.out_specs, out_shape=[out_shape] + exchange.out_shapes,
        scratch_shapes=exchange.scratch, compiler_params=_params(("arbitrary",)),
    )(*pieces, b, *exchange.xs)


def _matmul_tn_pieces(a, pieces, *, name, tk):
    K, M = a.shape
    widths = [p.shape[1] for p in pieces]
    n_p, nk = len(pieces), K // tk

    def body(*refs):
        a_ref, p_refs, o_refs = refs[0], refs[1:n_p + 1], refs[n_p + 1:]
        k = pl.program_id(0)
        av = a_ref[...].astype(BF16)
        for p_ref, o_ref in zip(p_refs, o_refs):
            prod = _tn_dot(av, p_ref[...].astype(BF16))

            @pl.when(k == 0)
            def _():
                o_ref[...] = prod

            @pl.when(k > 0)
            def _():
                o_ref[...] += prod

    return pl.pallas_call(
        body, name=name, grid=(nk,),
        in_specs=[pl.BlockSpec((tk, M), lambda k: (k, 0))] + [pl.BlockSpec((tk, w), lambda k: (k, 0)) for w in widths],
        out_specs=[_full((M, w)) for w in widths], out_shape=[jax.ShapeDtypeStruct((M, w), F32) for w in widths],
        compiler_params=_params(("arbitrary",)),
    )(a, *pieces)


def _split_dot(x, mat, parts):
    out = None
    rem = x
    for p in range(parts):
        piece = rem.astype(BF16)
        term = jnp.dot(piece, mat, preferred_element_type=F32)
        out = term if out is None else out + term
        if p + 1 < parts:
            rem = rem - piece.astype(F32)
    return out


def _sigmoid(x):
    return 1.0 / (1.0 + jnp.exp(-x))


def _rows8(x):
    r, c = x.shape
    return jnp.sum(x.reshape(r // SUBLANES, SUBLANES, c), axis=0)


def _shift_down(blk, prev8, n):
    rolled = pltpu.roll(blk, n, axis=0)
    prev_rolled = pltpu.roll(prev8, n, axis=0)
    rows = lax.broadcasted_iota(jnp.int32, prev8.shape, 0)
    first = jnp.where(rows < n, prev_rolled, rolled[0:SUBLANES])
    return jnp.concatenate([first, rolled[SUBLANES:]], axis=0)


def _shift_up(blk, next8, n):
    r = blk.shape[0]
    rolled = pltpu.roll(blk, r - n, axis=0)
    next_rolled = pltpu.roll(next8, SUBLANES - n, axis=0)
    rows = lax.broadcasted_iota(jnp.int32, next8.shape, 0)
    last = jnp.where(rows >= SUBLANES - n, next_rolled, rolled[r - SUBLANES:])
    return jnp.concatenate([rolled[:r - SUBLANES], last], axis=0)


def _prev_spec(tm, width, col):
    per = tm // SUBLANES
    return pl.BlockSpec((SUBLANES, width), lambda i, *_: (jnp.maximum(i * per - 1, 0), col))


def _next_spec(tm, width, col, n_tiles):
    per = tm // SUBLANES
    last = n_tiles * per - 1
    return pl.BlockSpec((SUBLANES, width), lambda i, *_: (jnp.minimum((i + 1) * per, last), col))


def _group_matrix():
    idx = np.arange(ATTN_W) // HEAD_DIM
    return jnp.asarray((idx[:, None] == idx[None, :]).astype(np.float32), BF16)


def _norm_mod(x, g, sc, sh, *, name):
    T = x.shape[0]
    tm = min(TOKEN_TILE, T)

    def body(x_ref, g_ref, sc_ref, sh_ref, o_ref):
        xv = x_ref[...]
        inv = lax.rsqrt(jnp.mean(xv * xv, axis=-1, keepdims=True) + RMS_EPS)
        o_ref[...] = ((xv * inv) * g_ref[...] * (1.0 + sc_ref[...]) + sh_ref[...]).astype(BF16)

    row = pl.BlockSpec((tm, D), lambda i: (i, 0))
    return pl.pallas_call(
        body, name=name, grid=(T // tm,),
        in_specs=[row, _full((1, D)), _full((1, D)), _full((1, D))],
        out_specs=row, out_shape=jax.ShapeDtypeStruct((T, D), BF16),
        compiler_params=_params(("parallel",)),
    )(x, g, sc, sh)


LANE_ONE = 64
LANE_F = 67
LANE_LSE = 70
LANE_SUM = 73


def _pieces(x):
    hi = x.astype(BF16).astype(F32)
    rest = x - hi
    mid = rest.astype(BF16).astype(F32)
    return hi, mid, rest - mid


def _aug(lane, data, entries):
    out = jnp.where(lane < HEAD_DIM, data, 0.0)
    for idx, val in entries:
        out = jnp.where(lane == idx, val, out)
    return out


def _run(start, vals):
    return [(start + i, v) for i, v in enumerate(vals)]


def _head_lanes(a, h):
    blk = a[:, LANES * (h // 2):LANES * (h // 2) + LANES]
    return blk if h % 2 == 0 else pltpu.roll(blk, HEAD_DIM, axis=1)


def _branch_prep(proj, fcum, conv_w8, qg, kg, gmat):
    T = proj.shape[0]
    tm = min(TOKEN_TILE, T)
    nt = T // tm

    def body(cb_ref, cc_ref, cv_ref, q_ref, k_ref, v_ref, f_ref, ccp_ref, cvp_ref, w_ref, qg_ref, kg_ref, g_ref,
             ya_ref, qa_ref, ka_ref, va_ref):
        i = pl.program_id(0)
        z = cc_ref[...] * cv_ref[...]
        zp = jnp.where(i > 0, ccp_ref[...] * cvp_ref[...], 0.0)
        w = w_ref[...]
        cz = _shift_down(z, zp, 2) * w[0:1] + _shift_down(z, zp, 1) * w[1:2] + z * w[2:3]
        ya_ref[...] = (cb_ref[...] * cz).astype(BF16)
        gm = g_ref[...]

        def normed(src, gain, scale):
            v = src[...]
            ms = _split_dot(v * v, gm, 2) * (1.0 / HEAD_DIM)
            return (v * lax.rsqrt(ms + RMS_EPS)) * gain[...] * scale

        qn = normed(q_ref, qg_ref, 1.0 / np.sqrt(HEAD_DIM))
        kn = normed(k_ref, kg_ref, 1.0)
        vv = v_ref[...]
        fall = f_ref[...]
        lane = lax.broadcasted_iota(jnp.int32, (tm, LANES), 1)
        ones3 = [1.0, 1.0, 1.0]
        for h in range(N_HEADS):
            hi, mid, lo = _pieces(fall[:, h:h + 1])
            qa_ref[h] = _aug(lane, _head_lanes(qn, h), _run(LANE_ONE, ones3) + _run(LANE_F, [hi, mid, lo])
                             + [(LANE_SUM, 1.0)]).astype(BF16)
            ka_ref[h] = _aug(lane, _head_lanes(kn, h), _run(LANE_ONE, [-hi, -mid, -lo]) + _run(LANE_F, ones3)
                             + _run(LANE_LSE, ones3)).astype(BF16)
            va_ref[h] = _aug(lane, _head_lanes(vv, h), _run(LANE_ONE, ones3)).astype(BF16)

    blk = lambda col: pl.BlockSpec((tm, CONV_W), lambda i: (i, col))
    heads = pl.BlockSpec((N_HEADS, tm, LANES), lambda i: (0, i, 0))
    return pl.pallas_call(
        body, name="branch_prep", grid=(nt,),
        in_specs=[blk(0), blk(1), blk(2), blk(3), blk(4), blk(5), pl.BlockSpec((tm, F_PAD), lambda i: (i, 0)),
                  _prev_spec(tm, CONV_W, 1), _prev_spec(tm, CONV_W, 2),
                  _full((SUBLANES, CONV_W)), _full((1, ATTN_W)), _full((1, ATTN_W)), _full((ATTN_W, ATTN_W))],
        out_specs=[pl.BlockSpec((tm, CONV_W), lambda i: (i, 0)), heads, heads, heads],
        out_shape=[jax.ShapeDtypeStruct((T, CONV_W), BF16)] + [jax.ShapeDtypeStruct((N_HEADS, T, LANES), BF16)] * 3,
        compiler_params=_params(("parallel",)),
    )(proj, proj, proj, proj, proj, proj, fcum, proj, proj, conv_w8, qg, kg, gmat)


def _cumsum(x, *, reverse, name, col=0, gate_bias=None):
    T = x.shape[0]
    tm = min(TOKEN_TILE, T)
    nt = T // tm

    def body(x_ref, b_ref, o_ref, carry_ref):
        i = pl.program_id(0)

        @pl.when(i == 0)
        def _():
            carry_ref[...] = jnp.zeros_like(carry_ref)

        r = lax.broadcasted_iota(jnp.int32, (tm, tm), 0)
        c = lax.broadcasted_iota(jnp.int32, (tm, tm), 1)
        tri = jnp.where((c >= r) if reverse else (c <= r), 1.0, 0.0).astype(BF16)
        xv = x_ref[...]
        if gate_bias is not None:
            fx = xv + b_ref[...]
            xv = jnp.minimum(fx, 0.0) - jnp.log(1.0 + jnp.exp(-jnp.abs(fx)))
        out = _split_dot_left(tri, xv, 3) + carry_ref[0:1]
        o_ref[...] = out
        carry_ref[...] = jnp.broadcast_to(out[0:1] if reverse else out[tm - 1:tm], carry_ref.shape)

    rows = (lambda i: nt - 1 - i) if reverse else (lambda i: i)
    bias = jnp.zeros((1, F_PAD), F32) if gate_bias is None else gate_bias
    return pl.pallas_call(
        body, name=name, grid=(nt,),
        in_specs=[pl.BlockSpec((tm, F_PAD), lambda i: (rows(i), col)), _full((1, F_PAD))],
        out_specs=pl.BlockSpec((tm, F_PAD), lambda i: (rows(i), 0)),
        out_shape=jax.ShapeDtypeStruct((T, F_PAD), F32),
        scratch_shapes=[pltpu.VMEM((SUBLANES, F_PAD), F32)],
        compiler_params=_params(("arbitrary",)),
    )(x, bias)


def _split_dot_left(mat, x, parts):
    out = None
    rem = x
    for p in range(parts):
        piece = rem.astype(BF16)
        term = jnp.dot(mat, piece, preferred_element_type=F32)
        out = term if out is None else out + term
        if p + 1 < parts:
            rem = rem - piece.astype(F32)
    return out


def _out_resid_norm(x, merged, w_out, g1, g, sc, sh):
    T = x.shape[0]
    tm = min(TOKEN_TILE, T)

    def body(x_ref, m_ref, w_ref, g1_ref, g_ref, sc_ref, sh_ref, mix_ref, x1_ref, h_ref):
        mix = jnp.dot(m_ref[...], w_ref[...], preferred_element_type=F32)
        mix_ref[...] = mix
        x1 = x_ref[...] + g1_ref[...] * mix
        x1_ref[...] = x1
        inv = lax.rsqrt(jnp.mean(x1 * x1, axis=-1, keepdims=True) + RMS_EPS)
        h_ref[...] = ((x1 * inv) * g_ref[...] * (1.0 + sc_ref[...]) + sh_ref[...]).astype(BF16)

    row = pl.BlockSpec((tm, D), lambda i: (i, 0))
    vec = _full((1, D))
    return pl.pallas_call(
        body, name="out_resid_norm", grid=(T // tm,),
        in_specs=[row, row, _full((D, D)), vec, vec, vec, vec], out_specs=[row, row, row],
        out_shape=[jax.ShapeDtypeStruct((T, D), F32), jax.ShapeDtypeStruct((T, D), F32),
                   jax.ShapeDtypeStruct((T, D), BF16)],
        compiler_params=_params(("parallel",)),
    )(x, merged, w_out, g1, g, sc, sh)


FFN_TM = 256
FFN_TC = 1408


def _ffn_act_fwd(u, w8):
    T = u.shape[0]
    tm = min(FFN_TM, T)
    nt = T // tm
    nc = D_FF // FFN_TC

    def body(ug_ref, uv_ref, ugp_ref, uvp_ref, wg_ref, wv_ref, o_ref):
        i = pl.program_id(1)

        def conv(u_ref, p_ref, w_ref):
            uv = u_ref[...]
            up = jnp.where(i > 0, p_ref[...], 0.0)
            w = w_ref[...]
            return _shift_down(uv, up, 2) * w[0:1] + _shift_down(uv, up, 1) * w[1:2] + uv * w[2:3]

        gate = conv(ug_ref, ugp_ref, wg_ref)
        val = conv(uv_ref, uvp_ref, wv_ref)
        o_ref[...] = (gate * _sigmoid(gate) * val).astype(BF16)

    per = tm // SUBLANES
    blk = lambda off: pl.BlockSpec((tm, FFN_TC), lambda j, i: (i, j + off))
    prev = lambda off: pl.BlockSpec((SUBLANES, FFN_TC), lambda j, i: (jnp.maximum(i * per - 1, 0), j + off))
    wblk = lambda off: pl.BlockSpec((SUBLANES, FFN_TC), lambda j, i: (0, j + off))
    return pl.pallas_call(
        body, name="ffn_act_fwd", grid=(nc, nt),
        in_specs=[blk(0), blk(nc), prev(0), prev(nc), wblk(0), wblk(nc)],
        out_specs=pl.BlockSpec((tm, FFN_TC), lambda j, i: (i, j)),
        out_shape=jax.ShapeDtypeStruct((T, D_FF), BF16),
        compiler_params=_params(("parallel", "parallel")),
    )(u, u, u, u, w8, w8)


def _down_loss_head(x1, act, w_down, g2, target):
    T = x1.shape[0]
    tm = min(TOKEN_TILE, T)

    def body(x1_ref, a_ref, w_ref, g2_ref, t_ref, dy_ref, dff_ref, loss_ref, dg2_ref):
        i = pl.program_id(0)

        @pl.when(i == 0)
        def _():
            loss_ref[...] = jnp.zeros_like(loss_ref)
            dg2_ref[...] = jnp.zeros_like(dg2_ref)

        ff = jnp.dot(a_ref[...], w_ref[...], preferred_element_type=F32)
        err = x1_ref[...] + g2_ref[...] * ff - t_ref[...]
        dy = err * (1.0 / D)
        dy_ref[...] = dy
        dff_ref[...] = (dy * g2_ref[...]).astype(BF16)
        loss_ref[...] += _rows8(err * err)
        dg2_ref[...] += _rows8(dy * ff)

    row = pl.BlockSpec((tm, D), lambda i: (i, 0))
    acc = _full((SUBLANES, D))
    return pl.pallas_call(
        body, name="down_loss_head", grid=(T // tm,),
        in_specs=[row, pl.BlockSpec((tm, D_FF), lambda i: (i, 0)), _full((D_FF, D)), _full((1, D)), row],
        out_specs=[row, row, acc, acc],
        out_shape=[jax.ShapeDtypeStruct((T, D), F32), jax.ShapeDtypeStruct((T, D), BF16),
                   jax.ShapeDtypeStruct((SUBLANES, D), F32), jax.ShapeDtypeStruct((SUBLANES, D), F32)],
        compiler_params=_params(("arbitrary",)),
    )(x1, act, w_down, g2, target)


def _nt_dot(a, b):
    return lax.dot_general(a, b, (((1,), (1,)), ((), ())), preferred_element_type=F32)


def _causal(n, keys_on_rows=False):
    r = lax.broadcasted_iota(jnp.int32, (n, n), 0)
    c = lax.broadcasted_iota(jnp.int32, (n, n), 1)
    return (c >= r) if keys_on_rows else (c <= r)


def _sweep(lo, hi, step, carry, group=2):
    while group >= 1:
        def several(j, cr, lo=lo, group=group):
            for g in range(group):
                cr = step(lo + group * j + g, cr)
            return cr

        passes = (hi - lo) // group
        carry = lax.fori_loop(0, passes, several, carry)
        lo = lo + group * passes
        group //= 2
    return carry


def _grid_ends(n0, n1):
    i0, i1 = pl.program_id(0), pl.program_id(1)
    return jnp.logical_and(i0 == 0, i1 == 0), jnp.logical_and(i0 == n0 - 1, i1 == n1 - 1)


def _attn_fwd(qa, ka, va, exchange=None):
    nh, T, _ = qa.shape
    bq = min(ATTN_BLOCK, T)
    nq = T // bq

    def body(*refs):
        if exchange is None:
            q_ref, k_ref, v_ref, o_ref, qb_ref = refs
        else:
            (q_ref, k_ref, v_ref), (o_ref, qb_ref), _, xrefs = exchange.split(refs, 3, 2)
            _ride(exchange, *_grid_ends(nh, nq), xrefs)
        qi = pl.program_id(1)
        q = q_ref[0]

        def step(kb, carry, masked):
            m, acc = carry
            start = pl.multiple_of(kb * bq, bq)
            s = _nt_dot(q, k_ref[0, pl.ds(start, bq), :])
            if masked:
                s = jnp.where(_causal(bq), s, NEG_INF)
            m_new = jnp.maximum(m, jnp.max(s, axis=-1, keepdims=True))
            p = jnp.exp(s - m_new).astype(BF16)
            acc = jnp.exp(m - m_new) * acc + jnp.dot(p, v_ref[0, pl.ds(start, bq), :], preferred_element_type=F32)
            return m_new, acc

        init = (jnp.full((bq, 1), NEG_INF, F32), jnp.zeros((bq, LANES), F32))
        carry = _sweep(0, qi, lambda kb, cr: step(kb, cr, False), init, group=4)
        m, acc = step(qi, carry, True)
        l = acc[:, LANE_ONE:LANE_ONE + 1]
        o_ref[0] = acc / l
        lse = m + jnp.log(l)
        lane = lax.broadcasted_iota(jnp.int32, (bq, LANES), 1)
        qf = q.astype(F32)
        for idx, piece in _run(LANE_LSE, _pieces(lse)):
            qf = jnp.where(lane == idx, -piece, qf)
        qb_ref[0] = qf.astype(BF16)

    tile = pl.BlockSpec((1, bq, LANES), lambda h, i: (h, i, 0))
    whole = pl.BlockSpec((1, T, LANES), lambda h, i: (h, 0, 0))
    out_shape = [jax.ShapeDtypeStruct((nh, T, LANES), F32), jax.ShapeDtypeStruct((nh, T, LANES), BF16)]
    if exchange is None:
        return pl.pallas_call(
            body, name="attn_fwd", grid=(nh, nq), in_specs=[tile, whole, whole], out_specs=[tile, tile],
            out_shape=out_shape, compiler_params=_params(("parallel", "parallel")),
        )(qa, ka, va)
    return pl.pallas_call(
        body, name="attn_fwd", grid=(nh, nq), in_specs=[tile, whole, whole] + exchange.in_specs,
        out_specs=[tile, tile] + exchange.out_specs, out_shape=out_shape + exchange.out_shapes,
        scratch_shapes=exchange.scratch, compiler_params=_params(("arbitrary", "arbitrary")),
    )(qa, ka, va, *exchange.xs)


def _branch_merge_fwd(ya0, o_h, proj, wba, wbb_heads):
    nh, T, _ = o_h.shape
    tm = min(TOKEN_TILE, T)

    def body(ya0_ref, o_ref, ga_ref, gb_ref, wa_ref, wb_ref, ya_ref, yb_ref, m_ref):
        ya = jnp.dot(ya0_ref[...], wa_ref[...], preferred_element_type=F32)
        yb = jnp.dot(o_ref[0].astype(BF16), wb_ref[0:LANES, :], preferred_element_type=F32)
        for h in range(1, nh):
            yb += jnp.dot(o_ref[h].astype(BF16), wb_ref[h * LANES:(h + 1) * LANES, :], preferred_element_type=F32)
        ya_ref[...] = ya.astype(BF16)
        yb_ref[...] = yb.astype(BF16)
        m_ref[...] = (_sigmoid(ga_ref[...]) * ya + _sigmoid(gb_ref[...]) * yb).astype(BF16)

    row = pl.BlockSpec((tm, D), lambda i: (i, 0))
    return pl.pallas_call(
        body, name="branch_merge_fwd", grid=(T // tm,),
        in_specs=[pl.BlockSpec((tm, CONV_W), lambda i: (i, 0)), pl.BlockSpec((nh, tm, LANES), lambda i: (0, i, 0)),
                  pl.BlockSpec((tm, D), lambda i: (i, COL_GA // D)), pl.BlockSpec((tm, D), lambda i: (i, COL_GB // D)),
                  _full((CONV_W, D)), _full((nh * LANES, D))],
        out_specs=[row, row, row],
        out_shape=[jax.ShapeDtypeStruct((T, D), BF16)] * 3,
        compiler_params=_params(("parallel",)),
    )(ya0, o_h, proj, proj, wba, wbb_heads)


def _branch_b_bwd(dyb, o_h, wbb_heads_t):
    nh, T, _ = o_h.shape
    tm = min(TOKEN_TILE, T)

    def body(dyb_ref, o_ref, w_ref, out_ref):
        do = jnp.dot(dyb_ref[...], w_ref[...], preferred_element_type=F32)
        lane = lax.broadcasted_iota(jnp.int32, (tm, LANES), 1)
        for h in range(nh):
            g = do[:, h * LANES:(h + 1) * LANES].astype(BF16).astype(F32)
            delta = jnp.sum(g * o_ref[h], axis=-1, keepdims=True)
            for idx, piece in _run(LANE_ONE, _pieces(delta)):
                g = jnp.where(lane == idx, -piece, g)
            out_ref[h] = g.astype(BF16)

    heads = pl.BlockSpec((nh, tm, LANES), lambda i: (0, i, 0))
    return pl.pallas_call(
        body, name="branch_b_bwd", grid=(T // tm,),
        in_specs=[pl.BlockSpec((tm, D), lambda i: (i, 0)), heads, _full((D, nh * LANES))],
        out_specs=heads, out_shape=jax.ShapeDtypeStruct((nh, T, LANES), BF16),
        compiler_params=_params(("parallel",)),
    )(dyb, o_h, wbb_heads_t)


def _branch_b_dw(o_h, dyb):
    nh, T, _ = o_h.shape
    tk = min(TOKEN_TILE, T)

    def body(o_ref, dyb_ref, out_ref):
        @pl.when(pl.program_id(0) == 0)
        def _():
            out_ref[...] = jnp.zeros_like(out_ref)

        g = dyb_ref[...]
        for h in range(nh):
            out_ref[h] += _tn_dot(o_ref[h].astype(BF16), g)

    return pl.pallas_call(
        body, name="branch_b_dw", grid=(T // tk,),
        in_specs=[pl.BlockSpec((nh, tk, LANES), lambda k: (0, k, 0)), pl.BlockSpec((tk, D), lambda k: (k, 0))],
        out_specs=_full((nh, LANES, D)), out_shape=jax.ShapeDtypeStruct((nh, LANES, D), F32),
        compiler_params=_params(("arbitrary",)),
    )(o_h, dyb)


def _attn_bwd(qb, ka, va, doa, exchange=None):
    nh, T, _ = qb.shape
    bk = min(ATTN_BLOCK, T)
    nk = T // bk

    def body(*refs):
        if exchange is None:
            q_ref, do_ref, k_ref, v_ref, dq_ref, dk_ref, dv_ref = refs
        else:
            (q_ref, do_ref, k_ref, v_ref), (dq_ref, dk_ref, dv_ref), _, xrefs = exchange.split(refs, 4, 3)
            _ride(exchange, *_grid_ends(nh, nk), xrefs)
        ki = pl.program_id(1)

        @pl.when(ki == 0)
        def _():
            dq_ref[...] = jnp.zeros_like(dq_ref)

        k = k_ref[0]
        v = v_ref[0]

        def step(qi, carry, masked):
            dk, dv = carry
            rows = pl.ds(pl.multiple_of(qi * bk, bk), bk)
            q = q_ref[0, rows, :]
            g = do_ref[0, rows, :]
            pt = jnp.exp(_nt_dot(k, q))
            if masked:
                pt = jnp.where(_causal(bk, keys_on_rows=True), pt, 0.0)
            dv = dv + jnp.dot(pt.astype(BF16), g, preferred_element_type=F32)
            dst = (pt * _nt_dot(v, g)).astype(BF16)
            dk = dk + jnp.dot(dst, q, preferred_element_type=F32)
            dq_ref[0, rows, :] += _tn_dot(dst, k)
            return dk, dv

        init = (jnp.zeros((bk, LANES), F32), jnp.zeros((bk, LANES), F32))
        carry = step(ki, init, True)
        dk_ref[0], dv_ref[0] = _sweep(ki + 1, nk, lambda qi, cr: step(qi, cr, False), carry)

    tile = pl.BlockSpec((1, bk, LANES), lambda h, i: (h, i, 0))
    whole = pl.BlockSpec((1, T, LANES), lambda h, i: (h, 0, 0))
    out_shape = [jax.ShapeDtypeStruct((nh, T, LANES), F32)] * 3
    if exchange is None:
        return pl.pallas_call(
            body, name="attn_bwd", grid=(nh, nk), in_specs=[whole, whole, tile, tile],
            out_specs=[whole, tile, tile], out_shape=out_shape, compiler_params=_params(("parallel", "arbitrary")),
        )(qb, doa, ka, va)
    return pl.pallas_call(
        body, name="attn_bwd", grid=(nh, nk), in_specs=[whole, whole, tile, tile] + exchange.in_specs,
        out_specs=[whole, tile, tile] + exchange.out_specs, out_shape=out_shape + exchange.out_shapes,
        scratch_shapes=exchange.scratch, compiler_params=_params(("arbitrary", "arbitrary")),
    )(qb, doa, ka, va, *exchange.xs)


def _attn_unpack(dq_h, dk_h, dv_h):
    nh, T, _ = dq_h.shape
    tm = min(TOKEN_TILE, T)

    def body(dq_ref, dk_ref, dv_ref, q_out, k_out, v_out, f_out):
        lane = lax.broadcasted_iota(jnp.int32, (tm, LANES), 1)
        low = lane < HEAD_DIM
        for src, dst in ((dq_ref, q_out), (dk_ref, k_out), (dv_ref, v_out)):
            for pair in range(nh // 2):
                both = jnp.where(low, src[2 * pair], pltpu.roll(src[2 * pair + 1], HEAD_DIM, axis=1))
                dst[:, LANES * pair:LANES * (pair + 1)] = both.astype(dst.dtype)
        df = jnp.zeros((tm, LANES), F32)
        for h in range(nh):
            col = dq_ref[h][:, LANE_F:LANE_F + 1] - dk_ref[h][:, LANE_SUM:LANE_SUM + 1]
            df = jnp.where(lane == h, col, df)
        f_out[...] = df

    heads = pl.BlockSpec((nh, tm, LANES), lambda i: (0, i, 0))
    tok = pl.BlockSpec((tm, ATTN_W), lambda i: (i, 0))
    return pl.pallas_call(
        body, name="attn_unpack", grid=(T // tm,), in_specs=[heads, heads, heads],
        out_specs=[tok, tok, tok, pl.BlockSpec((tm, F_PAD), lambda i: (i, 0))],
        out_shape=[jax.ShapeDtypeStruct((T, ATTN_W), F32), jax.ShapeDtypeStruct((T, ATTN_W), F32),
                   jax.ShapeDtypeStruct((T, ATTN_W), BF16), jax.ShapeDtypeStruct((T, F_PAD), F32)],
        compiler_params=_params(("parallel",)),
    )(dq_h, dk_h, dv_h)


def _ffn_act_bwd(u, da, w8):
    T = u.shape[0]
    tm = min(FFN_TM, T)
    nt = T // tm
    nc = D_FF // FFN_TC

    def body(ug_ref, uv_ref, ugp_ref, uvp_ref, ugn_ref, uvn_ref, da_ref, dan_ref, wg_ref, wv_ref,
             dug_ref, duv_ref, dwg_ref, dwv_ref):
        i = pl.program_id(1)

        @pl.when(i == 0)
        def _():
            dwg_ref[...] = jnp.zeros_like(dwg_ref)
            dwv_ref[...] = jnp.zeros_like(dwv_ref)

        first, last = i == 0, i == nt - 1
        wg, wv = wg_ref[...], wv_ref[...]
        zeros8 = jnp.zeros((SUBLANES, FFN_TC), F32)

        def window(u_ref, p_ref, n_ref, w):
            e = jnp.concatenate([jnp.where(first, 0.0, p_ref[...]), u_ref[...], n_ref[...]], axis=0)
            e1 = pltpu.roll(e, 1, axis=0)
            e2 = pltpu.roll(e, 2, axis=0)
            return e, e1, e2, e2 * w[0:1] + e1 * w[1:2] + e * w[2:3]

        eg, eg1, eg2, cg = window(ug_ref, ugp_ref, ugn_ref, wg)
        ev, ev1, ev2, cv = window(uv_ref, uvp_ref, uvn_ref, wv)
        dae = jnp.concatenate([zeros8, da_ref[...].astype(F32), jnp.where(last, 0.0, dan_ref[...].astype(F32))],
                              axis=0)
        sg = _sigmoid(cg)
        dgate = dae * cv * sg * (1.0 + cg * (1.0 - sg))
        dval = dae * cg * sg
        n = tm + 2 * SUBLANES

        def back(d, w):
            return d * w[2:3] + pltpu.roll(d, n - 1, axis=0) * w[1:2] + pltpu.roll(d, n - 2, axis=0) * w[0:1]

        inner = slice(SUBLANES, SUBLANES + tm)
        dug_ref[...] = back(dgate, wg)[inner].astype(BF16)
        duv_ref[...] = back(dval, wv)[inner].astype(BF16)

        def wgrad(d, e, e1, e2):
            rows = [jnp.sum((d * t)[inner], axis=0, keepdims=True) for t in (e2, e1, e)]
            return jnp.concatenate(rows + [jnp.zeros((SUBLANES - 3, FFN_TC), F32)], axis=0)

        dwg_ref[...] += wgrad(dgate, eg, eg1, eg2)
        dwv_ref[...] += wgrad(dval, ev, ev1, ev2)

    per = tm // SUBLANES
    last_blk = nt * per - 1
    blk = lambda off: pl.BlockSpec((tm, FFN_TC), lambda j, i: (i, j + off))
    prev = lambda off: pl.BlockSpec((SUBLANES, FFN_TC), lambda j, i: (jnp.maximum(i * per - 1, 0), j + off))
    nxt = lambda off: pl.BlockSpec((SUBLANES, FFN_TC), lambda j, i: (jnp.minimum((i + 1) * per, last_blk), j + off))
    wblk = lambda off: pl.BlockSpec((SUBLANES, FFN_TC), lambda j, i: (0, j + off))
    dug, duv, dwg, dwv = pl.pallas_call(
        body, name="ffn_act_bwd", grid=(nc, nt),
        in_specs=[blk(0), blk(nc), prev(0), prev(nc), nxt(0), nxt(nc), blk(0), nxt(0), wblk(0), wblk(nc)],
        out_specs=[blk(0), blk(0), wblk(0), wblk(0)],
        out_shape=[jax.ShapeDtypeStruct((T, D_FF), BF16)] * 2 + [jax.ShapeDtypeStruct((SUBLANES, D_FF), F32)] * 2,
        compiler_params=_params(("parallel", "arbitrary")),
    )(u, u, u, u, u, u, da, da, w8, w8)
    return dug, duv, jnp.concatenate([dwg, dwv], axis=1)


def _norm_bwd(xin, dh, dres, g, sc, *, name):
    T = xin.shape[0]
    tm = min(TOKEN_TILE, T)

    def body(x_ref, dh_ref, dr_ref, g_ref, sc_ref, dx_ref, dsh_ref, dsc_ref, dg_ref):
        i = pl.program_id(0)

        @pl.when(i == 0)
        def _():
            dsh_ref[...] = jnp.zeros_like(dsh_ref)
            dsc_ref[...] = jnp.zeros_like(dsc_ref)
            dg_ref[...] = jnp.zeros_like(dg_ref)

        xv = x_ref[...]
        dh = dh_ref[...]
        gv = g_ref[...]
        one_sc = 1.0 + sc_ref[...]
        inv = lax.rsqrt(jnp.mean(xv * xv, axis=-1, keepdims=True) + RMS_EPS)
        xn = xv * inv
        dxn = dh * (gv * one_sc)
        dx_ref[...] = dr_ref[...] + inv * (dxn - xn * jnp.mean(dxn * xn, axis=-1, keepdims=True))
        dhxn = dh * xn
        dsh_ref[...] += _rows8(dh)
        dsc_ref[...] += _rows8(dhxn * gv)
        dg_ref[...] += _rows8(dhxn * one_sc)

    row = pl.BlockSpec((tm, D), lambda i: (i, 0))
    acc = _full((SUBLANES, D))
    return pl.pallas_call(
        body, name=name, grid=(T // tm,),
        in_specs=[row, row, row, _full((1, D)), _full((1, D))], out_specs=[row, acc, acc, acc],
        out_shape=[jax.ShapeDtypeStruct((T, D), F32)] + [jax.ShapeDtypeStruct((SUBLANES, D), F32)] * 3,
        compiler_params=_params(("arbitrary",)),
    )(xin, dh, dres, g, sc)


def _norm2_gate_bwd(x1, dh2, dy, mix, g1, g, sc):
    T = x1.shape[0]
    tm = min(TOKEN_TILE, T)

    def body(x_ref, dh_ref, dr_ref, mix_ref, g1_ref, g_ref, sc_ref, dx_ref, dmix_ref, dsh_ref, dsc_ref, dg_ref,
             dg1_ref):
        @pl.when(pl.program_id(0) == 0)
        def _():
            for ref in (dsh_ref, dsc_ref, dg_ref, dg1_ref):
                ref[...] = jnp.zeros_like(ref)

        xv = x_ref[...]
        dh = dh_ref[...]
        gv = g_ref[...]
        one_sc = 1.0 + sc_ref[...]
        inv = lax.rsqrt(jnp.mean(xv * xv, axis=-1, keepdims=True) + RMS_EPS)
        xn = xv * inv
        dxn = dh * (gv * one_sc)
        dx = dr_ref[...] + inv * (dxn - xn * jnp.mean(dxn * xn, axis=-1, keepdims=True))
        dx_ref[...] = dx
        dmix_ref[...] = (dx * g1_ref[...]).astype(BF16)
        dhxn = dh * xn
        dsh_ref[...] += _rows8(dh)
        dsc_ref[...] += _rows8(dhxn * gv)
        dg_ref[...] += _rows8(dhxn * one_sc)
        dg1_ref[...] += _rows8(dx * mix_ref[...])

    row = pl.BlockSpec((tm, D), lambda i: (i, 0))
    vec, acc = _full((1, D)), _full((SUBLANES, D))
    return pl.pallas_call(
        body, name="norm2_gate_bwd", grid=(T // tm,),
        in_specs=[row, row, row, row, vec, vec, vec], out_specs=[row, row, acc, acc, acc, acc],
        out_shape=[jax.ShapeDtypeStruct((T, D), F32), jax.ShapeDtypeStruct((T, D), BF16)]
        + [jax.ShapeDtypeStruct((SUBLANES, D), F32)] * 4,
        compiler_params=_params(("arbitrary",)),
    )(x1, dh2, dy, mix, g1, g, sc)


def _out_merge_bwd(dmix, w_out_t, ya, yb, proj):
    T = ya.shape[0]
    tm = min(TOKEN_TILE, T)

    def body(dmix_ref, w_ref, ya_ref, yb_ref, ga_ref, gb_ref, dya_ref, dyb_ref, dga_ref, dgb_ref):
        dm = jnp.dot(dmix_ref[...], w_ref[...], preferred_element_type=F32)
        sa = _sigmoid(ga_ref[...])
        sb = _sigmoid(gb_ref[...])
        dya_ref[...] = (dm * sa).astype(BF16)
        dyb_ref[...] = (dm * sb).astype(BF16)
        dga_ref[...] = (dm * ya_ref[...].astype(F32) * sa * (1.0 - sa)).astype(BF16)
        dgb_ref[...] = (dm * yb_ref[...].astype(F32) * sb * (1.0 - sb)).astype(BF16)

    row = pl.BlockSpec((tm, D), lambda i: (i, 0))
    return pl.pallas_call(
        body, name="out_merge_bwd", grid=(T // tm,),
        in_specs=[row, _full((D, D)), row, row, pl.BlockSpec((tm, D), lambda i: (i, COL_GA // D)),
                  pl.BlockSpec((tm, D), lambda i: (i, COL_GB // D))],
        out_specs=[row] * 4, out_shape=[jax.ShapeDtypeStruct((T, D), BF16)] * 4,
        compiler_params=_params(("parallel",)),
    )(dmix, w_out_t, ya, yb, proj, proj)


def _conv_branch_bwd(proj, dya0, conv_w8):
    T = proj.shape[0]
    tm = min(FFN_TM, T)
    nt = T // tm

    def body(cb_ref, cc_ref, cv_ref, cbn_ref, ccp_ref, cvp_ref, ccn_ref, cvn_ref, d_ref, dn_ref, w_ref,
             dcb_ref, dcc_ref, dcv_ref, dw_ref):
        i = pl.program_id(0)

        @pl.when(i == 0)
        def _():
            dw_ref[...] = jnp.zeros_like(dw_ref)

        first, last = i == 0, i == nt - 1
        w = w_ref[...]
        cc = jnp.concatenate([ccp_ref[...], cc_ref[...], ccn_ref[...]], axis=0)
        cv = jnp.concatenate([cvp_ref[...], cv_ref[...], cvn_ref[...]], axis=0)
        rows = lax.broadcasted_iota(jnp.int32, cc.shape, 0)
        z = jnp.where(jnp.logical_and(first, rows < SUBLANES), 0.0, cc * cv)
        z1 = pltpu.roll(z, 1, axis=0)
        z2 = pltpu.roll(z, 2, axis=0)
        cz = z2 * w[0:1] + z1 * w[1:2] + z * w[2:3]
        zeros8 = jnp.zeros((SUBLANES, CONV_W), F32)
        de = jnp.concatenate([zeros8, d_ref[...], jnp.where(last, 0.0, dn_ref[...])], axis=0)
        cbe = jnp.concatenate([zeros8, cb_ref[...], cbn_ref[...]], axis=0)
        dcz = de * cbe
        n = tm + 2 * SUBLANES
        dz = dcz * w[2:3] + pltpu.roll(dcz, n - 1, axis=0) * w[1:2] + pltpu.roll(dcz, n - 2, axis=0) * w[0:1]
        inner = slice(SUBLANES, SUBLANES + tm)
        dcb_ref[...] = (de * cz)[inner].astype(BF16)
        dcc_ref[...] = (dz * cv)[inner].astype(BF16)
        dcv_ref[...] = (dz * cc)[inner].astype(BF16)
        wrows = [jnp.sum((dcz * t)[inner], axis=0, keepdims=True) for t in (z2, z1, z)]
        dw_ref[...] += jnp.concatenate(wrows + [jnp.zeros((SUBLANES - 3, CONV_W), F32)], axis=0)

    blk = lambda col: pl.BlockSpec((tm, CONV_W), lambda i: (i, col))
    out_blk = pl.BlockSpec((tm, CONV_W), lambda i: (i, 0))
    return pl.pallas_call(
        body, name="conv_branch_bwd", grid=(nt,),
        in_specs=[blk(0), blk(1), blk(2), _next_spec(tm, CONV_W, 0, nt),
                  _prev_spec(tm, CONV_W, 1), _prev_spec(tm, CONV_W, 2),
                  _next_spec(tm, CONV_W, 1, nt), _next_spec(tm, CONV_W, 2, nt),
                  out_blk, _next_spec(tm, CONV_W, 0, nt), _full((SUBLANES, CONV_W))],
        out_specs=[out_blk, out_blk, out_blk, _full((SUBLANES, CONV_W))],
        out_shape=[jax.ShapeDtypeStruct((T, CONV_W), BF16)] * 3 + [jax.ShapeDtypeStruct((SUBLANES, CONV_W), F32)],
        compiler_params=_params(("arbitrary",)),
    )(proj, proj, proj, proj, proj, proj, proj, proj, dya0, dya0, conv_w8)


def _qk_norm_bwd(proj, dqs, dkh, dlogf, qg, kg, bf_pad, gmat):
    T = proj.shape[0]
    tm = min(TOKEN_TILE, T)

    def body(q_ref, k_ref, f_ref, dqs_ref, dkh_ref, dlf_ref, qg_ref, kg_ref, bf_ref, g_ref,
             dq_ref, dk_ref, dfl_ref, dqg_ref, dkg_ref, dbf_ref):
        @pl.when(pl.program_id(0) == 0)
        def _():
            dqg_ref[...] = jnp.zeros_like(dqg_ref)
            dkg_ref[...] = jnp.zeros_like(dkg_ref)
            dbf_ref[...] = jnp.zeros_like(dbf_ref)

        gm = g_ref[...]
        for src, d_src, gain, scale, dst, dgain in (
                (q_ref, dqs_ref, qg_ref, 1.0 / np.sqrt(HEAD_DIM), dq_ref, dqg_ref),
                (k_ref, dkh_ref, kg_ref, 1.0, dk_ref, dkg_ref)):
            v = src[...]
            dhat = d_src[...] * scale
            inv = lax.rsqrt(_split_dot(v * v, gm, 2) * (1.0 / HEAD_DIM) + RMS_EPS)
            vn = v * inv
            dgain[...] += _rows8(dhat * vn)
            dvn = dhat * gain[...]
            mean = _split_dot(dvn * vn, gm, 2) * (1.0 / HEAD_DIM)
            dst[...] = (inv * (dvn - vn * mean)).astype(BF16)
        fx = f_ref[...] + bf_ref[...]
        dfl = dlf_ref[...] * _sigmoid(-fx)
        dfl_ref[...] = dfl.astype(BF16)
        dbf_ref[...] += _rows8(dfl)

    blk = lambda col: pl.BlockSpec((tm, ATTN_W), lambda i: (i, col))
    out_blk = pl.BlockSpec((tm, ATTN_W), lambda i: (i, 0))
    f_in = pl.BlockSpec((tm, F_PAD), lambda i: (i, COL_F // F_PAD))
    f_blk = pl.BlockSpec((tm, F_PAD), lambda i: (i, 0))
    return pl.pallas_call(
        body, name="qk_norm_bwd", grid=(T // tm,),
        in_specs=[blk(3), blk(4), f_in, out_blk, out_blk, f_blk, _full((1, ATTN_W)), _full((1, ATTN_W)),
                  _full((1, F_PAD)), _full((ATTN_W, ATTN_W))],
        out_specs=[out_blk, out_blk, f_blk, _full((SUBLANES, ATTN_W)), _full((SUBLANES, ATTN_W)),
                   _full((SUBLANES, F_PAD))],
        out_shape=[jax.ShapeDtypeStruct((T, ATTN_W), BF16)] * 2 + [jax.ShapeDtypeStruct((T, F_PAD), BF16)]
        + [jax.ShapeDtypeStruct((SUBLANES, ATTN_W), F32)] * 2 + [jax.ShapeDtypeStruct((SUBLANES, F_PAD), F32)],
        compiler_params=_params(("arbitrary",)),
    )(proj, proj, proj, dqs, dkh, dlogf, qg, kg, bf_pad, gmat)


def _pad_rows8(w):
    return jnp.pad(w, ((0, SUBLANES - w.shape[0]), (0, 0)))


def _fold8(acc):
    return jnp.sum(acc, axis=0, keepdims=True)


def _late_weights(mats):
    out = {}
    for name in ("w_branch_a", "w_out", "w_up", "w_down"):
        out[name] = mats[name]
        out[name + "_t"] = mats[name].T
    out["w_branch_b_heads"] = _pad_head_rows(mats["w_branch_b"])
    out["w_branch_b_heads_t"] = out["w_branch_b_heads"].T
    return out


def _local_step(x, target, mod, wts, late=None):
    T = x.shape[0]
    tb = min(MATMUL_TILE, T)
    tm = min(TOKEN_TILE, T)
    sh1, sc1, g1, sh2, sc2, g2 = [mod[:, i * D:(i + 1) * D] for i in range(N_MOD)]
    w_in, w_in_t = wts["w_in"], wts["w_in_t"]
    conv_a8 = _pad_rows8(wts["conv_a_w"])
    conv_f8 = _pad_rows8(wts["conv_ffn_w"])
    qg = jnp.tile(wts["q_norm_g"], (1, N_HEADS))
    kg = jnp.tile(wts["k_norm_g"], (1, N_HEADS))
    bf_pad = jnp.pad(wts["b_f"], ((0, 0), (0, F_PAD - N_HEADS)))
    gmat = _group_matrix()

    h = _norm_mod(x, wts["norm1_g"], sc1, sh1, name="norm1_fwd")
    proj = _matmul(h, w_in, name="mm_in", tm=tb, tn=896, tk=D)
    fcum = _cumsum(proj, reverse=False, name="gate_cumsum", col=COL_F // F_PAD, gate_bias=bf_pad)
    ya0, qa, ka, va = _branch_prep(proj, fcum, conv_a8, qg, kg, gmat)
    if late is None:
        o_h, qb = _attn_fwd(qa, ka, va)
    else:
        o_h, qb, *gathered = _attn_fwd(qa, ka, va, _Exchange([late[name] for name, *_ in LATE], scatter=False))
        wts = dict(wts)
        wts.update(_late_weights({name: _join_shards(g, axis) for (name, _, _, axis), g in zip(LATE, gathered)}))
    ya, yb, merged = _branch_merge_fwd(ya0, o_h, proj, wts["w_branch_a"], wts["w_branch_b_heads"])
    mix, x1, h2 = _out_resid_norm(x, merged, wts["w_out"], g1, wts["norm2_g"], sc2, sh2)
    u = _matmul(h2, wts["w_up"], name="mm_up", tm=tb, tn=1408, tk=D)
    act = _ffn_act_fwd(u, conv_f8)
    dy, dff, sq8, dg2_8 = _down_loss_head(x1, act, wts["w_down"], g2, target)
    sq = jnp.sum(sq8).reshape(1, 1)

    grads = {}
    da = _matmul(dff, wts["w_down_t"], name="mm_down_dx", tm=tb, tn=1408, tk=D)
    grads["w_down"] = _matmul(act, dff, name="mm_down_dw", tm=1408, tn=D, tk=tb, trans_a=True)
    dug, duv, dconv_f8 = _ffn_act_bwd(u, da, conv_f8)
    grads["conv_ffn_w"] = dconv_f8[:3]
    dh2 = _matmul_pieces([dug, duv], wts["w_up_t"], name="mm_up_dx", tm=tm)
    dw_up = [_matmul(h2, d, name="mm_up_dw_" + half, tm=D, tn=1408, tk=tb, trans_a=True)
             for half, d in (("gate", dug), ("val", duv))]
    if late is None:
        grads["w_up"] = jnp.concatenate(dw_up, axis=1)
    dx1, dmix, dsh2_8, dsc2_8, dn2_8, dg1_8 = _norm2_gate_bwd(x1, dh2, dy, mix, g1, wts["norm2_g"], sc2)
    grads["norm2_g"] = _fold8(dn2_8)

    grads["w_out"] = _matmul(merged, dmix, name="mm_out_dw", tm=D, tn=D, tk=tb, trans_a=True)
    dya, dyb, dga, dgb = _out_merge_bwd(dmix, wts["w_out_t"], ya, yb, proj)
    dya0 = _matmul(dya, wts["w_branch_a_t"], name="mm_branch_a_dx", tm=tb, tn=CONV_W, tk=D)
    grads["w_branch_a"] = _matmul(ya0, dya, name="mm_branch_a_dw", tm=CONV_W, tn=D, tk=tb, trans_a=True)
    doa = _branch_b_bwd(dyb, o_h, wts["w_branch_b_heads_t"])
    grads["w_branch_b"] = _branch_b_dw(o_h, dyb)[:, :HEAD_DIM].reshape(ATTN_W, D)
    dcb, dcc, dcv, dconv_a8 = _conv_branch_bwd(proj, dya0, conv_a8)
    grads["conv_a_w"] = dconv_a8[:3]

    parts = {}
    if late is None:
        dq_h, dk_h, dv_h = _attn_bwd(qb, ka, va, doa)
    else:
        ready = [(_column_shards(dw_up) if name == "w_up" else _split_shards(grads[name], axis)).astype(BF16)
                 for name, _, _, axis in LATE]
        dq_h, dk_h, dv_h, *recv = _attn_bwd(
            qb, ka, va, doa, _Exchange(ready + [_pack_full_by_dest(grads, CONVS, SUBLANES)], scatter=True))
        parts = dict(zip([name for name, *_ in LATE] + ["conv"], recv))
    dq_tok, dk_tok, dv_tok, dfcum = _attn_unpack(dq_h, dk_h, dv_h)
    dlogf = _cumsum(dfcum, reverse=True, name="gate_cumsum_bwd")
    dq, dk, dfl, dqg8, dkg8, dbf8 = _qk_norm_bwd(proj, dq_tok, dk_tok, dlogf, qg, kg, bf_pad, gmat)
    grads["q_norm_g"] = jnp.sum(_fold8(dqg8).reshape(N_HEADS, HEAD_DIM), axis=0, keepdims=True)
    grads["k_norm_g"] = jnp.sum(_fold8(dkg8).reshape(N_HEADS, HEAD_DIM), axis=0, keepdims=True)
    grads["b_f"] = _fold8(dbf8)[:, :N_HEADS]
    narrow, wide = [dcb, dcc, dcv, dq, dk, dv_tok], [dga, dgb, dfl]
    dw_narrow = _matmul_tn_pieces(h, narrow, name="mm_in_dw_narrow", tk=tm)
    dwa, dwb, dwf = _matmul_tn_pieces(h, wide, name="mm_in_dw_wide", tk=tm)
    dw_in = list(dw_narrow) + [dwf[:, :N_HEADS], dwa, dwb]
    if late is None:
        grads["w_in"] = jnp.concatenate(dw_in, axis=1)
        dh = _matmul_pieces(narrow + wide, w_in_t, name="mm_in_dx", tm=tm)
    else:
        dh, parts["w_in"] = _matmul_pieces(
            narrow + wide, w_in_t, name="mm_in_dx", tm=tm,
            exchange=_Exchange([_column_shards(dw_in).astype(BF16)], scatter=True))
    grad_x, dsh1_8, dsc1_8, dn1_8 = _norm_bwd(x, dh, dx1, wts["norm1_g"], sc1, name="norm1_bwd")
    grads["norm1_g"] = _fold8(dn1_8)
    grads["mod"] = jnp.concatenate([_fold8(a) for a in (dsh1_8, dsc1_8, dg1_8, dsh2_8, dsc2_8, dg2_8)], axis=1)
    return sq, grad_x, grads, parts


def _me_and_peers():
    mx, my, mc = lax.axis_index("x"), lax.axis_index("y"), lax.axis_index("c")
    me = 4 * mx + 2 * my + mc
    peers = []
    for k in range(1, N_DEV):
        px = 1 - mx if k & 4 else mx
        py = 1 - my if k & 2 else my
        pc = 1 - mc if k & 1 else mc
        peers.append(((px, py, pc), 4 * px + 2 * py + pc))
    return me, peers


HBM_SPEC = pl.BlockSpec(memory_space=pltpu.HBM)


class _Exchange:
    def __init__(self, xs, scatter):
        self.xs, self.scatter, self.n = list(xs), scatter, len(xs)
        self.out_shapes = [jax.ShapeDtypeStruct(x.shape if scatter else (N_DEV,) + x.shape, x.dtype) for x in xs]
        self.in_specs = [HBM_SPEC] * self.n
        self.out_specs = [HBM_SPEC] * self.n
        self.scratch = [pltpu.SemaphoreType.DMA((self.n, N_DEV - 1)), pltpu.SemaphoreType.DMA((self.n, N_DEV - 1)),
                        pltpu.SemaphoreType.DMA((self.n,))]

    def _copies(self, x_refs, out_refs, sems):
        send_sems, recv_sems, local_sems = sems
        me, peers = _me_and_peers()

        def src(a, idx):
            return x_refs[a].at[idx] if self.scatter else x_refs[a]

        def copy(a, k, from_idx, to_slot, device):
            return pltpu.make_async_remote_copy(
                src_ref=src(a, from_idx), dst_ref=out_refs[a].at[to_slot], send_sem=send_sems.at[a, k],
                recv_sem=recv_sems.at[a, k], device_id=device, device_id_type=MESH)

        local = [pltpu.make_async_copy(src(a, me), out_refs[a].at[me], local_sems.at[a]) for a in range(self.n)]
        sends = [copy(a, k, idx, me, dev) for a in range(self.n) for k, (dev, idx) in enumerate(peers)]
        recvs = [copy(a, k, idx, idx, dev) for a in range(self.n) for k, (dev, idx) in enumerate(peers)]
        return local, sends, recvs

    def start(self, x_refs, out_refs, sems):
        local, sends, _ = self._copies(x_refs, out_refs, sems)
        for cp in local + sends:
            cp.start()

    def wait(self, x_refs, out_refs, sems):
        local, sends, recvs = self._copies(x_refs, out_refs, sems)
        for cp in recvs:
            cp.wait_recv()
        for cp in sends:
            cp.wait_send()
        for cp in local:
            cp.wait()

    def split(self, refs, n_in, n_out):
        n = self.n
        ins, xin = refs[:n_in], refs[n_in:n_in + n]
        outs, xout = refs[n_in + n:n_in + n + n_out], refs[n_in + n + n_out:n_in + 2 * n + n_out]
        rest = refs[n_in + 2 * n + n_out:]
        return ins, outs, rest[:len(rest) - 3], (xin, xout, rest[len(rest) - 3:])


def _ride(exchange, first, last, refs):
    if exchange is None:
        return

    @pl.when(first)
    def _():
        exchange.start(*refs)

    @pl.when(last)
    def _():
        exchange.wait(*refs)


def _gather_two_level(xs, *, name):
    n = len(xs)
    out_shapes = [jax.ShapeDtypeStruct((N_DEV,) + x.shape, x.dtype) for x in xs]

    def body(*refs):
        x_refs, out_refs = refs[:n], refs[n:2 * n]
        send_sems, recv_sems, local_sems = refs[2 * n:]
        x, y, c = lax.axis_index("x"), lax.axis_index("y"), lax.axis_index("c")
        me, sibling = (x, y, c), (x, y, 1 - c)
        chips = [(1 - x, y), (x, 1 - y), (1 - x, 1 - y)]

        def slot(a, dev):
            return out_refs[a].at[4 * dev[0] + 2 * dev[1] + dev[2]]

        def copy(a, k, block, to, src=None):
            return pltpu.make_async_remote_copy(
                src_ref=slot(a, block) if src is None else src, dst_ref=slot(a, block),
                send_sem=send_sems.at[a, k], recv_sem=recv_sems.at[a, k], device_id=to, device_id_type=MESH)

        mine = [pltpu.make_async_copy(x_refs[a], slot(a, me), local_sems.at[a]) for a in range(n)]
        first = [copy(a, 0, me, sibling, src=x_refs[a]) for a in range(n)]
        first += [copy(a, 1 + j, me, (*chip, c), src=x_refs[a]) for a in range(n) for j, chip in enumerate(chips)]
        for cp in mine + first:
            cp.start()
        passed = []
        for a in range(n):
            for j, chip in enumerate(chips):
                copy(a, 1 + j, (*chip, c), me).wait_recv()
                passed.append(copy(a, 4 + j, (*chip, c), sibling))
                passed[-1].start()
        for a in range(n):
            copy(a, 0, sibling, me).wait_recv()
            for j, chip in enumerate(chips):
                copy(a, 4 + j, (*chip, 1 - c), me).wait_recv()
        for cp in first + passed:
            cp.wait_send()
        for cp in mine:
            cp.wait()

    return pl.pallas_call(
        body, name=name, in_specs=[HBM_SPEC] * n, out_specs=[HBM_SPEC] * n, out_shape=out_shapes,
        scratch_shapes=[pltpu.SemaphoreType.DMA((n, N_DEV - 1)), pltpu.SemaphoreType.DMA((n, N_DEV - 1)),
                        pltpu.SemaphoreType.DMA((n,))],
        compiler_params=pltpu.CompilerParams(has_side_effects=True),
    )(*xs)


def _exchange(xs, *, name, scatter):
    ex = _Exchange(xs, scatter)

    def body(*refs):
        _, _, _, xrefs = ex.split(refs, 0, 0)
        ex.start(*xrefs)
        ex.wait(*xrefs)

    return pl.pallas_call(
        body, name=name, in_specs=ex.in_specs, out_specs=ex.out_specs, out_shape=ex.out_shapes,
        scratch_shapes=ex.scratch, compiler_params=pltpu.CompilerParams(has_side_effects=True),
    )(*xs)


def _ada_fwd(c_all, w_shard, b_shard):
    n = w_shard.shape[1]

    def body(c_ref, w_ref, b_ref, o_ref):
        cv = c_ref[...]
        act = (cv * _sigmoid(cv)).astype(BF16)
        o_ref[...] = jnp.dot(act, w_ref[...].astype(BF16), preferred_element_type=F32) + b_ref[...]

    return pl.pallas_call(
        body, name="ada_fwd", in_specs=[_full((N_DEV, D)), _full((D, n)), _full((1, n))],
        out_specs=_full((N_DEV, n)), out_shape=jax.ShapeDtypeStruct((N_DEV, n), F32), grid=(1,),
        compiler_params=_params(("arbitrary",)),
    )(c_all, w_shard, b_shard)


def _ada_bwd(c_all_t, dmod_pad):
    n = dmod_pad.shape[1]

    def body(c_ref, d_ref, o_ref):
        cv = c_ref[...]
        act = (cv * _sigmoid(cv)).astype(BF16)
        o_ref[...] = jnp.dot(act, d_ref[...].astype(BF16), preferred_element_type=F32)

    return pl.pallas_call(
        body, name="ada_bwd", in_specs=[_full((D, LANES)), _full((LANES, n))],
        out_specs=_full((D, n)), out_shape=jax.ShapeDtypeStruct((D, n), F32), grid=(1,),
        compiler_params=_params(("arbitrary",)),
    )(c_all_t, dmod_pad)


ADAM_ROWS = 64


def _adamw(parts, w, m, v, *, name):
    n, R, C = parts.shape
    tr = next((t for t in (ADAM_ROWS, 32, 16, SUBLANES) if R % t == 0), R)

    def body(p_ref, w_ref, m_ref, v_ref, g_ref, d_ref, nm_ref, nv_ref):
        g = p_ref[0].astype(F32)
        for j in range(1, n):
            g = g + p_ref[j].astype(F32)
        g_ref[...] = g
        nm = ADAM_B1 * m_ref[...] + (1.0 - ADAM_B1) * g
        nv = ADAM_B2 * v_ref[...] + (1.0 - ADAM_B2) * (g * g)
        nm_ref[...] = nm
        nv_ref[...] = nv
        m_hat = nm / (1.0 - ADAM_B1 ** ADAM_STEP)
        v_hat = nv / (1.0 - ADAM_B2 ** ADAM_STEP)
        d_ref[...] = -ADAM_LR * (m_hat / (jnp.sqrt(v_hat) + ADAM_EPS) + ADAM_WD * w_ref[...])

    row = pl.BlockSpec((tr, C), lambda i: (i, 0))
    return pl.pallas_call(
        body, name=name, grid=(R // tr,),
        in_specs=[pl.BlockSpec((n, tr, C), lambda i: (0, i, 0)), row, row, row], out_specs=[row] * 4,
        out_shape=[jax.ShapeDtypeStruct((R, C), F32)] * 4,
        compiler_params=_params(("parallel",)),
    )(parts, w, m, v)


SHARDED = (("w_in", D, IN_W, 1), ("w_branch_a", CONV_W, D, 1), ("w_branch_b", ATTN_W, D, 1), ("w_out", D, D, 0),
           ("w_up", D, 2 * D_FF, 1), ("w_down", D_FF, D, 0), ("conv_a_w", 3, CONV_W, 1),
           ("conv_ffn_w", 3, 2 * D_FF, 1))
MATRICES = SHARDED[:6]
LATE = MATRICES[1:]
CONVS = SHARDED[6:]
REPLICATED = (("b_ada", N_MOD * D), ("norm1_g", D), ("norm2_g", D), ("b_f", N_HEADS), ("q_norm_g", HEAD_DIM),
              ("k_norm_g", HEAD_DIM))


def _shard_shape(rows, cols, axis):
    return (rows // N_DEV, cols) if axis == 0 else (rows, cols // N_DEV)


def _pack_rows(flat, multiple):
    length = flat.shape[-1]
    rows = -(-length // PACK_W)
    rows = -(-rows // multiple) * multiple
    pad = [(0, 0)] * (flat.ndim - 1) + [(0, rows * PACK_W - length)]
    return jnp.pad(flat, pad).reshape(flat.shape[:-1] + (rows, PACK_W))


def _pack_shards(shards, spec, multiple, dtype):
    flat = jnp.concatenate([shards[name].reshape(-1).astype(dtype) for name, *_ in spec])
    return _pack_rows(flat, multiple)


def _join_lane_blocks(gathered):
    n, r, c = gathered.shape

    def body(g_ref, o_ref):
        for j in range(n):
            o_ref[:, j * c:(j + 1) * c] = g_ref[j]

    return pl.pallas_call(
        body, name="join_lane_blocks", grid=(1,), in_specs=[_full((n, r, c))], out_specs=_full((r, n * c)),
        out_shape=jax.ShapeDtypeStruct((r, n * c), gathered.dtype), compiler_params=_params(("arbitrary",)),
    )(gathered)


def _join_shards(gathered, axis):
    if axis == 0:
        return gathered.reshape(N_DEV * gathered.shape[1], gathered.shape[2])
    if gathered.shape[2] == LANES:
        return _join_lane_blocks(gathered)
    return jnp.concatenate([gathered[j] for j in range(N_DEV)], axis=1)


def _column_shards(pieces):
    total = sum(p.shape[1] for p in pieces)
    width = total // N_DEV
    shards = []
    for j in range(N_DEV):
        lo, hi, off, segs = j * width, (j + 1) * width, 0, []
        for p in pieces:
            a, b = max(lo, off), min(hi, off + p.shape[1])
            if a < b:
                segs.append(p[:, a - off:b - off])
            off += p.shape[1]
        shards.append(segs[0] if len(segs) == 1 else jnp.concatenate(segs, axis=1))
    return jnp.stack(shards)


def _split_shards(full, axis):
    if axis == 0:
        return full.reshape(N_DEV, full.shape[0] // N_DEV, full.shape[1])
    c = full.shape[1] // N_DEV
    return jnp.stack([full[:, j * c:(j + 1) * c] for j in range(N_DEV)])


def _unpack_shards(packed, spec):
    flat = packed.reshape(-1)
    out, off = {}, 0
    for name, rows, cols, axis in spec:
        r, c = _shard_shape(rows, cols, axis)
        out[name] = flat[off:off + r * c].reshape(r, c)
        off += r * c
    return out


def _unpack_gathered(gathered, spec):
    flat = gathered.reshape(N_DEV, -1)
    out, off = {}, 0
    for name, rows, cols, axis in spec:
        r, c = _shard_shape(rows, cols, axis)
        seg = flat[:, off:off + r * c].reshape(N_DEV, r, c)
        out[name] = seg.reshape(rows, cols) if axis == 0 else seg.transpose(1, 0, 2).reshape(rows, cols)
        off += r * c
    return out


def _pack_full_by_dest(full, spec, multiple):
    segs = []
    for name, rows, cols, axis in spec:
        r, c = _shard_shape(rows, cols, axis)
        a = full[name]
        seg = a.reshape(N_DEV, r, c) if axis == 0 else a.reshape(rows, N_DEV, c).transpose(1, 0, 2)
        segs.append(seg.reshape(N_DEV, r * c))
    return _pack_rows(jnp.concatenate(segs, axis=1), multiple)


def _pad_in(w_in):
    return jnp.concatenate([w_in[:, :COL_GA], w_in[:, COL_GA + N_HEADS:], w_in[:, COL_GA:COL_GA + N_HEADS],
                            jnp.zeros((w_in.shape[0], IN_W_PAD - IN_W), w_in.dtype)], axis=1)


def _pad_head_rows(w):
    n = w.shape[1]
    padded = jnp.pad(w.reshape(N_HEADS, HEAD_DIM, n), ((0, 0), (0, LANES - HEAD_DIM), (0, 0)))
    return padded.reshape(N_HEADS * LANES, n)


def kernel(x, c, w_ada, b_ada, norm1_g, w_in, b_f, conv_a_w, q_norm_g, k_norm_g, w_branch_a, w_branch_b, w_out, norm2_g, w_up, conv_ffn_w, w_down, loss_target, m_w_ada, m_b_ada, m_norm1_g, m_w_in, m_b_f, m_conv_a_w, m_q_norm_g, m_k_norm_g, m_w_branch_a, m_w_branch_b, m_w_out, m_norm2_g, m_w_up, m_conv_ffn_w, m_w_down, v_w_ada, v_b_ada, v_norm1_g, v_w_in, v_b_f, v_conv_a_w, v_q_norm_g, v_k_norm_g, v_w_branch_a, v_w_branch_b, v_w_out, v_norm2_g, v_w_up, v_conv_ffn_w, v_w_down):
    names = ("w_ada", "b_ada", "norm1_g", "w_in", "b_f", "conv_a_w", "q_norm_g", "k_norm_g", "w_branch_a",
             "w_branch_b", "w_out", "norm2_g", "w_up", "conv_ffn_w", "w_down")
    squeeze = lambda a: a[0] if a.ndim == 3 else a
    W = dict(zip(names, map(squeeze, (w_ada, b_ada, norm1_g, w_in, b_f, conv_a_w, q_norm_g, k_norm_g, w_branch_a,
                                      w_branch_b, w_out, norm2_g, w_up, conv_ffn_w, w_down))))
    M = dict(zip(names, map(squeeze, (m_w_ada, m_b_ada, m_norm1_g, m_w_in, m_b_f, m_conv_a_w, m_q_norm_g,
                                      m_k_norm_g, m_w_branch_a, m_w_branch_b, m_w_out, m_norm2_g, m_w_up,
                                      m_conv_ffn_w, m_w_down))))
    V = dict(zip(names, map(squeeze, (v_w_ada, v_b_ada, v_norm1_g, v_w_in, v_b_f, v_conv_a_w, v_q_norm_g,
                                      v_k_norm_g, v_w_branch_a, v_w_branch_b, v_w_out, v_norm2_g, v_w_up,
                                      v_conv_ffn_w, v_w_down))))
    me = 4 * lax.axis_index("x") + 2 * lax.axis_index("y") + lax.axis_index("c")
    ada_n = N_MOD * D // N_DEV

    small = jnp.concatenate([c.reshape(-1), W["conv_a_w"].reshape(-1), W["conv_ffn_w"].reshape(-1)])
    small_all, w_in_all = _gather_two_level([_pack_rows(small, SUBLANES), W["w_in"].astype(BF16)],
                                            name="gather_first")
    small_all = small_all.reshape(N_DEV, -1)
    c_all = small_all[:, :D]
    conv_all = _unpack_gathered(small_all[:, D:], CONVS)

    b_shard = lax.dynamic_slice(W["b_ada"], (0, me * ada_n), (1, ada_n))
    mod_part = _ada_fwd(c_all, W["w_ada"], b_shard)
    mod_all, = _exchange([mod_part], name="gather_mod", scatter=False)
    mod = lax.dynamic_index_in_dim(mod_all, me, axis=1, keepdims=False).reshape(1, N_MOD * D)

    wts = {"w_in": _pad_in(_join_shards(w_in_all, 1))}
    wts["w_in_t"] = wts["w_in"].T
    wts.update(conv_all)
    for name in ("norm1_g", "norm2_g", "q_norm_g", "k_norm_g", "b_f"):
        wts[name] = W[name]
    late = {name: W[name].astype(BF16) for name, *_ in LATE}

    sq, grad_x, grads, parts = _local_step(x[0], loss_target[0], mod, wts, late)
    loss = lax.psum(sq[0, 0] * (0.5 / D), AXES)

    grads["b_ada"] = grads["mod"]
    rep_flat = lambda src: jnp.concatenate([src[name].reshape(-1) for name, _ in REPLICATED])
    rep_parts, = _exchange([_pack_rows(rep_flat(grads), 16)], name="gather_small_grads", scatter=False)
    rep_out = _adamw(rep_parts, *[_pack_rows(rep_flat(s), 16) for s in (W, M, V)], name="adamw_replicated")

    dmod_all = rep_parts.reshape(N_DEV, -1)[:, :N_MOD * D]
    dmod_mine = lax.dynamic_slice(dmod_all, (0, me * ada_n), (N_DEV, ada_n))
    g_ada = _ada_bwd(jnp.pad(c_all.T, ((0, 0), (0, LANES - N_DEV))),
                     jnp.pad(dmod_mine, ((0, LANES - N_DEV), (0, 0))))
    ada_out = _adamw(g_ada[None], W["w_ada"], M["w_ada"], V["w_ada"], name="adamw_ada")

    mat_out = {name: _adamw(parts[name], W[name], M[name], V[name], name="adamw_" + name) for name, *_ in MATRICES}
    conv_out = _adamw(parts["conv"], *[_pack_shards(s, CONVS, SUBLANES, F32) for s in (W, M, V)],
                      name="adamw_conv")

    results = []
    for kind in range(4):
        per = {"w_ada": ada_out[kind]}
        per.update({name: out[kind] for name, out in mat_out.items()})
        per.update(_unpack_shards(conv_out[kind], CONVS))
        flat, off = rep_out[kind].reshape(-1), 0
        for name, n in REPLICATED:
            per[name] = flat[off:off + n].reshape(1, n)
            off += n
        results.append(per)
    restore = lambda name, a: a[None] if W[name].ndim == 2 and name not in dict(REPLICATED) else a
    outs = [loss, grad_x[None]]
    for per in results:
        outs.extend(restore(name, per[name]) for name in names)
    return tuple(outs)
```

```python
import functools

import jax
import jax.numpy as jnp
import numpy as np
from jax import lax
from jax.experimental import pallas as pl
from jax.experimental.pallas import tpu as pltpu

F32 = jnp.float32
BF16 = jnp.bfloat16

N_DEV = 8
D = 1024
N_HEADS = 8
HEAD_DIM = 64
ATTN_W = 512
CONV_W = 512
D_FF = 2816
N_MOD = 6
IN_W = 5128
RMS_EPS = 1e-6
NEG_INF = -1e30

IN_W_PAD = 5376
COL_GA = 3072
COL_GB = 4096
COL_F = 5120
F_PAD = 128

ADAM_LR = 0.001
ADAM_B1 = 0.9
ADAM_B2 = 0.999
ADAM_EPS = 1e-08
ADAM_WD = 0.01
ADAM_STEP = 10

LANES = 128
SUBLANES = 8
VMEM_LIMIT = 52 * 1024 * 1024
TOKEN_TILE = 512
MATMUL_TILE = 1024
ATTN_BLOCK = 512
PACK_W = 1024

MESH = pl.DeviceIdType.MESH
AXES = ("x", "y", "c")


def _params(sem=None, **kw):
    return pltpu.CompilerParams(dimension_semantics=sem, vmem_limit_bytes=VMEM_LIMIT, **kw)


def _full(shape):
    nd = len(shape)
    return pl.BlockSpec(shape, lambda *_: (0,) * nd)


def _tn_dot(a, b):
    return lax.dot_general(a, b, (((0,), (0,)), ((), ())), preferred_element_type=F32)


def _matmul(a, b, *, name, tm, tn, tk, out_dtype=F32, trans_a=False, exchange=None):
    if trans_a:
        K, M = a.shape
    else:
        M, K = a.shape
    N = b.shape[1]
    assert b.shape[0] == K and M % tm == 0 and N % tn == 0 and K % tk == 0, (name, a.shape, b.shape)
    nm, nn, nk = M // tm, N // tn, K // tk

    def body(*refs):
        if exchange is None:
            a_ref, b_ref, o_ref, *own = refs
        else:
            (a_ref, b_ref), (o_ref,), own, xrefs = exchange.split(refs, 2, 1)
            ids = [pl.program_id(d) for d in range(3)]
            first = jnp.logical_and(jnp.logical_and(ids[0] == 0, ids[1] == 0), ids[2] == 0)
            last = jnp.logical_and(jnp.logical_and(ids[0] == nn - 1, ids[1] == nm - 1), ids[2] == nk - 1)
            _ride(exchange, first, last, xrefs)
        k = pl.program_id(2)
        av = a_ref[...].astype(BF16)
        bv = b_ref[...].astype(BF16)
        prod = _tn_dot(av, bv) if trans_a else jnp.dot(av, bv, preferred_element_type=F32)
        if nk == 1:
            o_ref[...] = prod.astype(out_dtype)
            return
        acc_ref, = own

        @pl.when(k == 0)
        def _():
            acc_ref[...] = prod

        @pl.when(k > 0)
        def _():
            acc_ref[...] += prod

        @pl.when(k == nk - 1)
        def _():
            o_ref[...] = acc_ref[...].astype(out_dtype)

    if trans_a:
        a_spec = pl.BlockSpec((tk, tm), lambda j, i, k: (k, i))
    else:
        a_spec = pl.BlockSpec((tm, tk), lambda j, i, k: (i, k))
    in_specs = [a_spec, pl.BlockSpec((tk, tn), lambda j, i, k: (k, j))]
    out_spec = pl.BlockSpec((tm, tn), lambda j, i, k: (i, j))
    out_shape = jax.ShapeDtypeStruct((M, N), out_dtype)
    scratch = [pltpu.VMEM((tm, tn), F32)] if nk > 1 else []
    if exchange is None:
        return pl.pallas_call(
            body, name=name, grid=(nn, nm, nk), in_specs=in_specs, out_specs=out_spec, out_shape=out_shape,
            scratch_shapes=scratch, compiler_params=_params(("parallel", "parallel", "arbitrary")),
        )(a, b)
    return pl.pallas_call(
        body, name=name, grid=(nn, nm, nk), in_specs=in_specs + exchange.in_specs,
        out_specs=[out_spec] + exchange.out_specs, out_shape=[out_shape] + exchange.out_shapes,
        scratch_shapes=scratch + exchange.scratch, compiler_params=_params(("arbitrary",) * 3),
    )(a, b, *exchange.xs)


def _matmul_pieces(pieces, b, *, name, tm, exchange=None):
    M = pieces[0].shape[0]
    widths = [p.shape[1] for p in pieces]
    offsets = [sum(widths[:i]) for i in range(len(widths))]
    N = b.shape[1]
    assert b.shape[0] >= sum(widths) and M % tm == 0, (name, widths, b.shape)
    n_p, nm = len(pieces), M // tm

    def body(*refs):
        if exchange is None:
            ins, o_ref = refs[:n_p + 1], refs[n_p + 1]
        else:
            ins, (o_ref,), _, xrefs = exchange.split(refs, n_p + 1, 1)
            i = pl.program_id(0)
            _ride(exchange, i == 0, i == nm - 1, xrefs)
        b_ref = ins[n_p]
        acc = None
        for a_ref, off, w in zip(ins[:n_p], offsets, widths):
            term = jnp.dot(a_ref[...].astype(BF16), b_ref[off:off + w, :], preferred_element_type=F32)
            acc = term if acc is None else acc + term
        o_ref[...] = acc

    in_specs = [pl.BlockSpec((tm, w), lambda i: (i, 0)) for w in widths] + [_full(b.shape)]
    out_spec = pl.BlockSpec((tm, N), lambda i: (i, 0))
    out_shape = jax.ShapeDtypeStruct((M, N), F32)
    if exchange is None:
        return pl.pallas_call(
            body, name=name, grid=(nm,), in_specs=in_specs, out_specs=out_spec, out_shape=out_shape,
            compiler_params=_params(("parallel",)),
        )(*pieces, b)
    return pl.pallas_call(
        body, name=name, grid=(nm,), in_specs=in_specs + exchange.in_specs,
        out_specs=[out_spec] + exchange.out_specs, out_shape=[out_shape] + exchange.out_shapes,
        scratch_shapes=exchange.scratch, compiler_params=_params(("arbitrary",)),
    )(*pieces, b, *exchange.xs)


def _matmul_tn_pieces(a, pieces, *, name, tk):
    K, M = a.shape
    widths = [p.shape[1] for p in pieces]
    n_p, nk = len(pieces), K // tk

    def body(*refs):
        a_ref, p_refs, o_refs = refs[0], refs[1:n_p + 1], refs[n_p + 1:]
        k = pl.program_id(0)
        av = a_ref[...].astype(BF16)
        for p_ref, o_ref in zip(p_refs, o_refs):
            prod = _tn_dot(av, p_ref[...].astype(BF16))

            @pl.when(k == 0)
            def _():
                o_ref[...] = prod

            @pl.when(k > 0)
            def _():
                o_ref[...] += prod

    return pl.pallas_call(
        body, name=name, grid=(nk,),
        in_specs=[pl.BlockSpec((tk, M), lambda k: (k, 0))] + [pl.BlockSpec((tk, w), lambda k: (k, 0)) for w in widths],
        out_specs=[_full((M, w)) for w in widths], out_shape=[jax.ShapeDtypeStruct((M, w), F32) for w in widths],
        compiler_params=_params(("arbitrary",)),
    )(a, *pieces)


def _split_dot(x, mat, parts):
    out = None
    rem = x
    for p in range(parts):
        piece = rem.astype(BF16)
        term = jnp.dot(piece, mat, preferred_element_type=F32)
        out = term if out is None else out + term
        if p + 1 < parts:
            rem = rem - piece.astype(F32)
    return out


def _sigmoid(x):
    return 1.0 / (1.0 + jnp.exp(-x))


def _rows8(x):
    r, c = x.shape
    return jnp.sum(x.reshape(r // SUBLANES, SUBLANES, c), axis=0)


def _shift_down(blk, prev8, n):
    rolled = pltpu.roll(blk, n, axis=0)
    prev_rolled = pltpu.roll(prev8, n, axis=0)
    rows = lax.broadcasted_iota(jnp.int32, prev8.shape, 0)
    first = jnp.where(rows < n, prev_rolled, rolled[0:SUBLANES])
    return jnp.concatenate([first, rolled[SUBLANES:]], axis=0)


def _shift_up(blk, next8, n):
    r = blk.shape[0]
    rolled = pltpu.roll(blk, r - n, axis=0)
    next_rolled = pltpu.roll(next8, SUBLANES - n, axis=0)
    rows = lax.broadcasted_iota(jnp.int32, next8.shape, 0)
    last = jnp.where(rows >= SUBLANES - n, next_rolled, rolled[r - SUBLANES:])
    return jnp.concatenate([rolled[:r - SUBLANES], last], axis=0)


def _prev_spec(tm, width, col):
    per = tm // SUBLANES
    return pl.BlockSpec((SUBLANES, width), lambda i, *_: (jnp.maximum(i * per - 1, 0), col))


def _next_spec(tm, width, col, n_tiles):
    per = tm // SUBLANES
    last = n_tiles * per - 1
    return pl.BlockSpec((SUBLANES, width), lambda i, *_: (jnp.minimum((i + 1) * per, last), col))


def _group_matrix():
    idx = np.arange(ATTN_W) // HEAD_DIM
    return jnp.asarray((idx[:, None] == idx[None, :]).astype(np.float32), BF16)


def _norm_mod(x, g, sc, sh, *, name):
    T = x.shape[0]
    tm = min(TOKEN_TILE, T)

    def body(x_ref, g_ref, sc_ref, sh_ref, o_ref):
        xv = x_ref[...]
        inv = lax.rsqrt(jnp.mean(xv * xv, axis=-1, keepdims=True) + RMS_EPS)
        o_ref[...] = ((xv * inv) * g_ref[...] * (1.0 + sc_ref[...]) + sh_ref[...]).astype(BF16)

    row = pl.BlockSpec((tm, D), lambda i: (i, 0))
    return pl.pallas_call(
        body, name=name, grid=(T // tm,),
        in_specs=[row, _full((1, D)), _full((1, D)), _full((1, D))],
        out_specs=row, out_shape=jax.ShapeDtypeStruct((T, D), BF16),
        compiler_params=_params(("parallel",)),
    )(x, g, sc, sh)


LANE_ONE = 64
LANE_F = 67
LANE_LSE = 70
LANE_SUM = 73


def _pieces(x):
    hi = x.astype(BF16).astype(F32)
    rest = x - hi
    mid = rest.astype(BF16).astype(F32)
    return hi, mid, rest - mid


def _aug(lane, data, entries):
    out = jnp.where(lane < HEAD_DIM, data, 0.0)
    for idx, val in entries:
        out = jnp.where(lane == idx, val, out)
    return out


def _run(start, vals):
    return [(start + i, v) for i, v in enumerate(vals)]


def _head_lanes(a, h):
    blk = a[:, LANES * (h // 2):LANES * (h // 2) + LANES]
    return blk if h % 2 == 0 else pltpu.roll(blk, HEAD_DIM, axis=1)


def _branch_prep(proj, fcum, conv_w8, qg, kg, gmat):
    T = proj.shape[0]
    tm = min(TOKEN_TILE, T)
    nt = T // tm

    def body(cb_ref, cc_ref, cv_ref, q_ref, k_ref, v_ref, f_ref, ccp_ref, cvp_ref, w_ref, qg_ref, kg_ref, g_ref,
             ya_ref, qa_ref, ka_ref, va_ref):
        i = pl.program_id(0)
        z = cc_ref[...] * cv_ref[...]
        zp = jnp.where(i > 0, ccp_ref[...] * cvp_ref[...], 0.0)
        w = w_ref[...]
        cz = _shift_down(z, zp, 2) * w[0:1] + _shift_down(z, zp, 1) * w[1:2] + z * w[2:3]
        ya_ref[...] = (cb_ref[...] * cz).astype(BF16)
        gm = g_ref[...]

        def normed(src, gain, scale):
            v = src[...]
            ms = _split_dot(v * v, gm, 2) * (1.0 / HEAD_DIM)
            return (v * lax.rsqrt(ms + RMS_EPS)) * gain[...] * scale

        qn = normed(q_ref, qg_ref, 1.0 / np.sqrt(HEAD_DIM))
        kn = normed(k_ref, kg_ref, 1.0)
        vv = v_ref[...]
        fall = f_ref[...]
        lane = lax.broadcasted_iota(jnp.int32, (tm, LANES), 1)
        ones3 = [1.0, 1.0, 1.0]
        for h in range(N_HEADS):
            hi, mid, lo = _pieces(fall[:, h:h + 1])
            qa_ref[h] = _aug(lane, _head_lanes(qn, h), _run(LANE_ONE, ones3) + _run(LANE_F, [hi, mid, lo])
                             + [(LANE_SUM, 1.0)]).astype(BF16)
            ka_ref[h] = _aug(lane, _head_lanes(kn, h), _run(LANE_ONE, [-hi, -mid, -lo]) + _run(LANE_F, ones3)
                             + _run(LANE_LSE, ones3)).astype(BF16)
            va_ref[h] = _aug(lane, _head_lanes(vv, h), _run(LANE_ONE, ones3)).astype(BF16)

    blk = lambda col: pl.BlockSpec((tm, CONV_W), lambda i: (i, col))
    heads = pl.BlockSpec((N_HEADS, tm, LANES), lambda i: (0, i, 0))
    return pl.pallas_call(
        body, name="branch_prep", grid=(nt,),
        in_specs=[blk(0), blk(1), blk(2), blk(3), blk(4), blk(5), pl.BlockSpec((tm, F_PAD), lambda i: (i, 0)),
                  _prev_spec(tm, CONV_W, 1), _prev_spec(tm, CONV_W, 2),
                  _full((SUBLANES, CONV_W)), _full((1, ATTN_W)), _full((1, ATTN_W)), _full((ATTN_W, ATTN_W))],
        out_specs=[pl.BlockSpec((tm, CONV_W), lambda i: (i, 0)), heads, heads, heads],
        out_shape=[jax.ShapeDtypeStruct((T, CONV_W), BF16)] + [jax.ShapeDtypeStruct((N_HEADS, T, LANES), BF16)] * 3,
        compiler_params=_params(("parallel",)),
    )(proj, proj, proj, proj, proj, proj, fcum, proj, proj, conv_w8, qg, kg, gmat)


def _cumsum(x, *, reverse, name, col=0, gate_bias=None):
    T = x.shape[0]
    tm = min(TOKEN_TILE, T)
    nt = T // tm

    def body(x_ref, b_ref, o_ref, carry_ref):
        i = pl.program_id(0)

        @pl.when(i == 0)
        def _():
            carry_ref[...] = jnp.zeros_like(carry_ref)

        r = lax.broadcasted_iota(jnp.int32, (tm, tm), 0)
        c = lax.broadcasted_iota(jnp.int32, (tm, tm), 1)
        tri = jnp.where((c >= r) if reverse else (c <= r), 1.0, 0.0).astype(BF16)
        xv = x_ref[...]
        if gate_bias is not None:
            fx = xv + b_ref[...]
            xv = jnp.minimum(fx, 0.0) - jnp.log(1.0 + jnp.exp(-jnp.abs(fx)))
        out = _split_dot_left(tri, xv, 3) + carry_ref[0:1]
        o_ref[...] = out
        carry_ref[...] = jnp.broadcast_to(out[0:1] if reverse else out[tm - 1:tm], carry_ref.shape)

    rows = (lambda i: nt - 1 - i) if reverse else (lambda i: i)
    bias = jnp.zeros((1, F_PAD), F32) if gate_bias is None else gate_bias
    return pl.pallas_call(
        body, name=name, grid=(nt,),
        in_specs=[pl.BlockSpec((tm, F_PAD), lambda i: (rows(i), col)), _full((1, F_PAD))],
        out_specs=pl.BlockSpec((tm, F_PAD), lambda i: (rows(i), 0)),
        out_shape=jax.ShapeDtypeStruct((T, F_PAD), F32),
        scratch_shapes=[pltpu.VMEM((SUBLANES, F_PAD), F32)],
        compiler_params=_params(("arbitrary",)),
    )(x, bias)


def _split_dot_left(mat, x, parts):
    out = None
    rem = x
    for p in range(parts):
        piece = rem.astype(BF16)
        term = jnp.dot(mat, piece, preferred_element_type=F32)
        out = term if out is None else out + term
        if p + 1 < parts:
            rem = rem - piece.astype(F32)
    return out


def _out_resid_norm(x, merged, w_out, g1, g, sc, sh):
    T = x.shape[0]
    tm = min(TOKEN_TILE, T)

    def body(x_ref, m_ref, w_ref, g1_ref, g_ref, sc_ref, sh_ref, mix_ref, x1_ref, h_ref):
        mix = jnp.dot(m_ref[...], w_ref[...], preferred_element_type=F32)
        mix_ref[...] = mix
        x1 = x_ref[...] + g1_ref[...] * mix
        x1_ref[...] = x1
        inv = lax.rsqrt(jnp.mean(x1 * x1, axis=-1, keepdims=True) + RMS_EPS)
        h_ref[...] = ((x1 * inv) * g_ref[...] * (1.0 + sc_ref[...]) + sh_ref[...]).astype(BF16)

    row = pl.BlockSpec((tm, D), lambda i: (i, 0))
    vec = _full((1, D))
    return pl.pallas_call(
        body, name="out_resid_norm", grid=(T // tm,),
        in_specs=[row, row, _full((D, D)), vec, vec, vec, vec], out_specs=[row, row, row],
        out_shape=[jax.ShapeDtypeStruct((T, D), F32), jax.ShapeDtypeStruct((T, D), F32),
                   jax.ShapeDtypeStruct((T, D), BF16)],
        compiler_params=_params(("parallel",)),
    )(x, merged, w_out, g1, g, sc, sh)


FFN_TM = 256
FFN_TC = 1408


def _tanh_sigmoid(x):
    return 0.5 * jnp.tanh(0.5 * x) + 0.5


def _ffn_act_fwd(u, w8):
    T = u.shape[0]
    tm = min(FFN_TM, T)
    nt = T // tm
    nc = D_FF // FFN_TC

    def body(ug_ref, uv_ref, ugp_ref, uvp_ref, wg_ref, wv_ref, o_ref, cg_ref, cv_ref):
        i = pl.program_id(1)

        def conv(u_ref, p_ref, w_ref):
            uv = u_ref[...]
            up = jnp.where(i > 0, p_ref[...], 0.0)
            w = w_ref[...]
            return _shift_down(uv, up, 2) * w[0:1] + _shift_down(uv, up, 1) * w[1:2] + uv * w[2:3]

        gate = conv(ug_ref, ugp_ref, wg_ref)
        val = conv(uv_ref, uvp_ref, wv_ref)
        cg_ref[...] = gate
        cv_ref[...] = val
        o_ref[...] = (gate * _tanh_sigmoid(gate) * val).astype(BF16)

    per = tm // SUBLANES
    blk = lambda off: pl.BlockSpec((tm, FFN_TC), lambda j, i: (i, j + off))
    prev = lambda off: pl.BlockSpec((SUBLANES, FFN_TC), lambda j, i: (jnp.maximum(i * per - 1, 0), j + off))
    wblk = lambda off: pl.BlockSpec((SUBLANES, FFN_TC), lambda j, i: (0, j + off))
    return pl.pallas_call(
        body, name="ffn_act_fwd", grid=(nc, nt),
        in_specs=[blk(0), blk(nc), prev(0), prev(nc), wblk(0), wblk(nc)],
        out_specs=[blk(0), blk(0), blk(0)],
        out_shape=[jax.ShapeDtypeStruct((T, D_FF), BF16), jax.ShapeDtypeStruct((T, D_FF), F32),
                   jax.ShapeDtypeStruct((T, D_FF), F32)],
        compiler_params=_params(("parallel", "parallel")),
    )(u, u, u, u, w8, w8)


def _down_loss_head(x1, act, w_down, g2, target):
    T = x1.shape[0]
    tm = min(TOKEN_TILE, T)

    def body(x1_ref, a_ref, w_ref, g2_ref, t_ref, dy_ref, dff_ref, loss_ref, dg2_ref):
        i = pl.program_id(0)

        @pl.when(i == 0)
        def _():
            loss_ref[...] = jnp.zeros_like(loss_ref)
            dg2_ref[...] = jnp.zeros_like(dg2_ref)

        ff = jnp.dot(a_ref[...], w_ref[...], preferred_element_type=F32)
        err = x1_ref[...] + g2_ref[...] * ff - t_ref[...]
        dy = err * (1.0 / D)
        dy_ref[...] = dy
        dff_ref[...] = (dy * g2_ref[...]).astype(BF16)
        loss_ref[...] += _rows8(err * err)
        dg2_ref[...] += _rows8(dy * ff)

    row = pl.BlockSpec((tm, D), lambda i: (i, 0))
    acc = _full((SUBLANES, D))
    return pl.pallas_call(
        body, name="down_loss_head", grid=(T // tm,),
        in_specs=[row, pl.BlockSpec((tm, D_FF), lambda i: (i, 0)), _full((D_FF, D)), _full((1, D)), row],
        out_specs=[row, row, acc, acc],
        out_shape=[jax.ShapeDtypeStruct((T, D), F32), jax.ShapeDtypeStruct((T, D), BF16),
                   jax.ShapeDtypeStruct((SUBLANES, D), F32), jax.ShapeDtypeStruct((SUBLANES, D), F32)],
        compiler_params=_params(("arbitrary",)),
    )(x1, act, w_down, g2, target)


def _nt_dot(a, b):
    return lax.dot_general(a, b, (((1,), (1,)), ((), ())), preferred_element_type=F32)


def _causal(n, keys_on_rows=False):
    r = lax.broadcasted_iota(jnp.int32, (n, n), 0)
    c = lax.broadcasted_iota(jnp.int32, (n, n), 1)
    return (c >= r) if keys_on_rows else (c <= r)


def _sweep(lo, hi, step, carry, group=2):
    while group >= 1:
        def several(j, cr, lo=lo, group=group):
            for g in range(group):
                cr = step(lo + group * j + g, cr)
            return cr

        passes = (hi - lo) // group
        carry = lax.fori_loop(0, passes, several, carry)
        lo = lo + group * passes
        group //= 2
    return carry


def _grid_ends(n0, n1):
    i0, i1 = pl.program_id(0), pl.program_id(1)
    return jnp.logical_and(i0 == 0, i1 == 0), jnp.logical_and(i0 == n0 - 1, i1 == n1 - 1)


def _attn_fwd(qa, ka, va, exchange=None):
    nh, T, _ = qa.shape
    bq = min(ATTN_BLOCK, T)
    nq = T // bq

    def body(*refs):
        if exchange is None:
            q_ref, k_ref, v_ref, o_ref, qb_ref = refs
        else:
            (q_ref, k_ref, v_ref), (o_ref, qb_ref), _, xrefs = exchange.split(refs, 3, 2)
            _ride(exchange, *_grid_ends(nh, nq), xrefs)
        qi = pl.program_id(1)
        q = q_ref[0]

        def step(kb, carry, masked):
            m, acc = carry
            start = pl.multiple_of(kb * bq, bq)
            s = _nt_dot(q, k_ref[0, pl.ds(start, bq), :])
            if masked:
                s = jnp.where(_causal(bq), s, NEG_INF)
            m_new = jnp.maximum(m, jnp.max(s, axis=-1, keepdims=True))
            p = jnp.exp(s - m_new).astype(BF16)
            acc = jnp.exp(m - m_new) * acc + jnp.dot(p, v_ref[0, pl.ds(start, bq), :], preferred_element_type=F32)
            return m_new, acc

        init = (jnp.full((bq, 1), NEG_INF, F32), jnp.zeros((bq, LANES), F32))
        carry = _sweep(0, qi, lambda kb, cr: step(kb, cr, False), init, group=4)
        m, acc = step(qi, carry, True)
        l = acc[:, LANE_ONE:LANE_ONE + 1]
        o_ref[0] = acc / l
        lse = m + jnp.log(l)
        lane = lax.broadcasted_iota(jnp.int32, (bq, LANES), 1)
        qf = q.astype(F32)
        for idx, piece in _run(LANE_LSE, _pieces(lse)):
            qf = jnp.where(lane == idx, -piece, qf)
        qb_ref[0] = qf.astype(BF16)

    tile = pl.BlockSpec((1, bq, LANES), lambda h, i: (h, i, 0))
    whole = pl.BlockSpec((1, T, LANES), lambda h, i: (h, 0, 0))
    out_shape = [jax.ShapeDtypeStruct((nh, T, LANES), F32), jax.ShapeDtypeStruct((nh, T, LANES), BF16)]
    if exchange is None:
        return pl.pallas_call(
            body, name="attn_fwd", grid=(nh, nq), in_specs=[tile, whole, whole], out_specs=[tile, tile],
            out_shape=out_shape, compiler_params=_params(("parallel", "parallel")),
        )(qa, ka, va)
    return pl.pallas_call(
        body, name="attn_fwd", grid=(nh, nq), in_specs=[tile, whole, whole] + exchange.in_specs,
        out_specs=[tile, tile] + exchange.out_specs, out_shape=out_shape + exchange.out_shapes,
        scratch_shapes=exchange.scratch, compiler_params=_params(("arbitrary", "arbitrary")),
    )(qa, ka, va, *exchange.xs)


def _branch_merge_fwd(ya0, o_h, proj, wba, wbb_heads):
    nh, T, _ = o_h.shape
    tm = min(TOKEN_TILE, T)

    def body(ya0_ref, o_ref, ga_ref, gb_ref, wa_ref, wb_ref, ya_ref, yb_ref, m_ref):
        ya = jnp.dot(ya0_ref[...], wa_ref[...], preferred_element_type=F32)
        yb = jnp.dot(o_ref[0].astype(BF16), wb_ref[0:LANES, :], preferred_element_type=F32)
        for h in range(1, nh):
            yb += jnp.dot(o_ref[h].astype(BF16), wb_ref[h * LANES:(h + 1) * LANES, :], preferred_element_type=F32)
        ya_ref[...] = ya.astype(BF16)
        yb_ref[...] = yb.astype(BF16)
        m_ref[...] = (_sigmoid(ga_ref[...]) * ya + _sigmoid(gb_ref[...]) * yb).astype(BF16)

    row = pl.BlockSpec((tm, D), lambda i: (i, 0))
    return pl.pallas_call(
        body, name="branch_merge_fwd", grid=(T // tm,),
        in_specs=[pl.BlockSpec((tm, CONV_W), lambda i: (i, 0)), pl.BlockSpec((nh, tm, LANES), lambda i: (0, i, 0)),
                  pl.BlockSpec((tm, D), lambda i: (i, COL_GA // D)), pl.BlockSpec((tm, D), lambda i: (i, COL_GB // D)),
                  _full((CONV_W, D)), _full((nh * LANES, D))],
        out_specs=[row, row, row],
        out_shape=[jax.ShapeDtypeStruct((T, D), BF16)] * 3,
        compiler_params=_params(("parallel",)),
    )(ya0, o_h, proj, proj, wba, wbb_heads)


def _branch_b_bwd(dyb, o_h, wbb_heads_t):
    nh, T, _ = o_h.shape
    tm = min(TOKEN_TILE, T)

    def body(dyb_ref, o_ref, w_ref, out_ref):
        do = jnp.dot(dyb_ref[...], w_ref[...], preferred_element_type=F32)
        lane = lax.broadcasted_iota(jnp.int32, (tm, LANES), 1)
        for h in range(nh):
            g = do[:, h * LANES:(h + 1) * LANES].astype(BF16).astype(F32)
            delta = jnp.sum(g * o_ref[h], axis=-1, keepdims=True)
            for idx, piece in _run(LANE_ONE, _pieces(delta)):
                g = jnp.where(lane == idx, -piece, g)
            out_ref[h] = g.astype(BF16)

    heads = pl.BlockSpec((nh, tm, LANES), lambda i: (0, i, 0))
    return pl.pallas_call(
        body, name="branch_b_bwd", grid=(T // tm,),
        in_specs=[pl.BlockSpec((tm, D), lambda i: (i, 0)), heads, _full((D, nh * LANES))],
        out_specs=heads, out_shape=jax.ShapeDtypeStruct((nh, T, LANES), BF16),
        compiler_params=_params(("parallel",)),
    )(dyb, o_h, wbb_heads_t)


def _branch_b_dw(o_h, dyb):
    nh, T, _ = o_h.shape
    tk = min(TOKEN_TILE, T)

    def body(o_ref, dyb_ref, out_ref):
        @pl.when(pl.program_id(0) == 0)
        def _():
            out_ref[...] = jnp.zeros_like(out_ref)

        g = dyb_ref[...]
        for h in range(nh):
            out_ref[h] += _tn_dot(o_ref[h].astype(BF16), g)

    return pl.pallas_call(
        body, name="branch_b_dw", grid=(T // tk,),
        in_specs=[pl.BlockSpec((nh, tk, LANES), lambda k: (0, k, 0)), pl.BlockSpec((tk, D), lambda k: (k, 0))],
        out_specs=_full((nh, LANES, D)), out_shape=jax.ShapeDtypeStruct((nh, LANES, D), F32),
        compiler_params=_params(("arbitrary",)),
    )(o_h, dyb)


def _attn_bwd(qb, ka, va, doa, exchange=None):
    nh, T, _ = qb.shape
    bk = min(ATTN_BLOCK, T)
    nk = T // bk

    def body(*refs):
        if exchange is None:
            q_ref, do_ref, k_ref, v_ref, dq_ref, dk_ref, dv_ref = refs
        else:
            (q_ref, do_ref, k_ref, v_ref), (dq_ref, dk_ref, dv_ref), _, xrefs = exchange.split(refs, 4, 3)
            _ride(exchange, *_grid_ends(nh, nk), xrefs)
        ki = pl.program_id(1)

        @pl.when(ki == 0)
        def _():
            dq_ref[...] = jnp.zeros_like(dq_ref)

        k = k_ref[0]
        v = v_ref[0]

        def step(qi, carry, masked):
            dk, dv = carry
            rows = pl.ds(pl.multiple_of(qi * bk, bk), bk)
            q = q_ref[0, rows, :]
            g = do_ref[0, rows, :]
            pt = jnp.exp(_nt_dot(k, q))
            if masked:
                pt = jnp.where(_causal(bk, keys_on_rows=True), pt, 0.0)
            dv = dv + jnp.dot(pt.astype(BF16), g, preferred_element_type=F32)
            dst = (pt * _nt_dot(v, g)).astype(BF16)
            dk = dk + jnp.dot(dst, q, preferred_element_type=F32)
            dq_ref[0, rows, :] += _tn_dot(dst, k)
            return dk, dv

        init = (jnp.zeros((bk, LANES), F32), jnp.zeros((bk, LANES), F32))
        carry = step(ki, init, True)
        dk_ref[0], dv_ref[0] = _sweep(ki + 1, nk, lambda qi, cr: step(qi, cr, False), carry)

    tile = pl.BlockSpec((1, bk, LANES), lambda h, i: (h, i, 0))
    whole = pl.BlockSpec((1, T, LANES), lambda h, i: (h, 0, 0))
    out_shape = [jax.ShapeDtypeStruct((nh, T, LANES), F32)] * 3
    if exchange is None:
        return pl.pallas_call(
            body, name="attn_bwd", grid=(nh, nk), in_specs=[whole, whole, tile, tile],
            out_specs=[whole, tile, tile], out_shape=out_shape, compiler_params=_params(("parallel", "arbitrary")),
        )(qb, doa, ka, va)
    return pl.pallas_call(
        body, name="attn_bwd", grid=(nh, nk), in_specs=[whole, whole, tile, tile] + exchange.in_specs,
        out_specs=[whole, tile, tile] + exchange.out_specs, out_shape=out_shape + exchange.out_shapes,
        scratch_shapes=exchange.scratch, compiler_params=_params(("arbitrary", "arbitrary")),
    )(qb, doa, ka, va, *exchange.xs)


def _attn_unpack(dq_h, dk_h, dv_h):
    nh, T, _ = dq_h.shape
    tm = min(TOKEN_TILE, T)

    def body(dq_ref, dk_ref, dv_ref, q_out, k_out, v_out, f_out):
        lane = lax.broadcasted_iota(jnp.int32, (tm, LANES), 1)
        low = lane < HEAD_DIM
        for src, dst in ((dq_ref, q_out), (dk_ref, k_out), (dv_ref, v_out)):
            for pair in range(nh // 2):
                both = jnp.where(low, src[2 * pair], pltpu.roll(src[2 * pair + 1], HEAD_DIM, axis=1))
                dst[:, LANES * pair:LANES * (pair + 1)] = both.astype(dst.dtype)
        df = jnp.zeros((tm, LANES), F32)
        for h in range(nh):
            col = dq_ref[h][:, LANE_F:LANE_F + 1] - dk_ref[h][:, LANE_SUM:LANE_SUM + 1]
            df = jnp.where(lane == h, col, df)
        f_out[...] = df

    heads = pl.BlockSpec((nh, tm, LANES), lambda i: (0, i, 0))
    tok = pl.BlockSpec((tm, ATTN_W), lambda i: (i, 0))
    return pl.pallas_call(
        body, name="attn_unpack", grid=(T // tm,), in_specs=[heads, heads, heads],
        out_specs=[tok, tok, tok, pl.BlockSpec((tm, F_PAD), lambda i: (i, 0))],
        out_shape=[jax.ShapeDtypeStruct((T, ATTN_W), F32), jax.ShapeDtypeStruct((T, ATTN_W), F32),
                   jax.ShapeDtypeStruct((T, ATTN_W), BF16), jax.ShapeDtypeStruct((T, F_PAD), F32)],
        compiler_params=_params(("parallel",)),
    )(dq_h, dk_h, dv_h)


def _ffn_act_bwd(u, cg, cv, da, w8):
    T = u.shape[0]
    tm = min(FFN_TM, T)
    nt = T // tm
    nc = D_FF // FFN_TC

    def body(ug_ref, uv_ref, cg_ref, cv_ref, cgn_ref, cvn_ref, da_ref, dan_ref, wg_ref, wv_ref,
             dug_ref, duv_ref, dwg_ref, dwv_ref):
        i = pl.program_id(1)

        @pl.when(i == 0)
        def _():
            dwg_ref[...] = jnp.zeros_like(dwg_ref)
            dwv_ref[...] = jnp.zeros_like(dwv_ref)

        gate = jnp.concatenate([cg_ref[...], cgn_ref[...]], axis=0)
        val = jnp.concatenate([cv_ref[...], cvn_ref[...]], axis=0)
        dae = jnp.concatenate([da_ref[...], jnp.where(i == nt - 1, 0.0, dan_ref[...])], axis=0)
        sg = _tanh_sigmoid(gate)
        n = tm + SUBLANES

        def back(d, u_ref, w_ref, du_ref, dw_ref):
            w = w_ref[...]
            uv = u_ref[...]
            d1 = pltpu.roll(d, n - 1, axis=0)[:tm]
            d2 = pltpu.roll(d, n - 2, axis=0)[:tm]
            d0 = d[:tm]
            du_ref[...] = (d0 * w[2:3] + d1 * w[1:2] + d2 * w[0:1]).astype(BF16)
            rows = [jnp.sum(t * uv, axis=0, keepdims=True) for t in (d2, d1, d0)]
            dw_ref[...] += jnp.concatenate(rows + [jnp.zeros((SUBLANES - 3, FFN_TC), F32)], axis=0)

        back(dae * val * sg * (1.0 + gate * (1.0 - sg)), ug_ref, wg_ref, dug_ref, dwg_ref)
        back(dae * gate * sg, uv_ref, wv_ref, duv_ref, dwv_ref)

    per = tm // SUBLANES
    last_blk = nt * per - 1
    blk = lambda off: pl.BlockSpec((tm, FFN_TC), lambda j, i: (i, j + off))
    nxt = pl.BlockSpec((SUBLANES, FFN_TC), lambda j, i: (jnp.minimum((i + 1) * per, last_blk), j))
    wblk = lambda off: pl.BlockSpec((SUBLANES, FFN_TC), lambda j, i: (0, j + off))
    dug, duv, dwg, dwv = pl.pallas_call(
        body, name="ffn_act_bwd", grid=(nc, nt),
        in_specs=[blk(0), blk(nc), blk(0), blk(0), nxt, nxt, blk(0), nxt, wblk(0), wblk(nc)],
        out_specs=[blk(0), blk(0), wblk(0), wblk(0)],
        out_shape=[jax.ShapeDtypeStruct((T, D_FF), BF16)] * 2 + [jax.ShapeDtypeStruct((SUBLANES, D_FF), F32)] * 2,
        compiler_params=_params(("parallel", "arbitrary")),
    )(u, u, cg, cv, cg, cv, da, da, w8, w8)
    return dug, duv, jnp.concatenate([dwg, dwv], axis=1)


def _norm_bwd(xin, dh, dres, g, sc, *, name):
    T = xin.shape[0]
    tm = min(TOKEN_TILE, T)

    def body(x_ref, dh_ref, dr_ref, g_ref, sc_ref, dx_ref, dsh_ref, dsc_ref, dg_ref):
        i = pl.program_id(0)

        @pl.when(i == 0)
        def _():
            dsh_ref[...] = jnp.zeros_like(dsh_ref)
            dsc_ref[...] = jnp.zeros_like(dsc_ref)
            dg_ref[...] = jnp.zeros_like(dg_ref)

        xv = x_ref[...]
        dh = dh_ref[...]
        gv = g_ref[...]
        one_sc = 1.0 + sc_ref[...]
        inv = lax.rsqrt(jnp.mean(xv * xv, axis=-1, keepdims=True) + RMS_EPS)
        xn = xv * inv
        dxn = dh * (gv * one_sc)
        dx_ref[...] = dr_ref[...] + inv * (dxn - xn * jnp.mean(dxn * xn, axis=-1, keepdims=True))
        dhxn = dh * xn
        dsh_ref[...] += _rows8(dh)
        dsc_ref[...] += _rows8(dhxn * gv)
        dg_ref[...] += _rows8(dhxn * one_sc)

    row = pl.BlockSpec((tm, D), lambda i: (i, 0))
    acc = _full((SUBLANES, D))
    return pl.pallas_call(
        body, name=name, grid=(T // tm,),
        in_specs=[row, row, row, _full((1, D)), _full((1, D))], out_specs=[row, acc, acc, acc],
        out_shape=[jax.ShapeDtypeStruct((T, D), F32)] + [jax.ShapeDtypeStruct((SUBLANES, D), F32)] * 3,
        compiler_params=_params(("arbitrary",)),
    )(xin, dh, dres, g, sc)


def _norm2_gate_bwd(x1, dh2, dy, mix, g1, g, sc):
    T = x1.shape[0]
    tm = min(TOKEN_TILE, T)

    def body(x_ref, dh_ref, dr_ref, mix_ref, g1_ref, g_ref, sc_ref, dx_ref, dmix_ref, dsh_ref, dsc_ref, dg_ref,
             dg1_ref):
        @pl.when(pl.program_id(0) == 0)
        def _():
            for ref in (dsh_ref, dsc_ref, dg_ref, dg1_ref):
                ref[...] = jnp.zeros_like(ref)

        xv = x_ref[...]
        dh = dh_ref[...]
        gv = g_ref[...]
        one_sc = 1.0 + sc_ref[...]
        inv = lax.rsqrt(jnp.mean(xv * xv, axis=-1, keepdims=True) + RMS_EPS)
        xn = xv * inv
        dxn = dh * (gv * one_sc)
        dx = dr_ref[...] + inv * (dxn - xn * jnp.mean(dxn * xn, axis=-1, keepdims=True))
        dx_ref[...] = dx
        dmix_ref[...] = (dx * g1_ref[...]).astype(BF16)
        dhxn = dh * xn
        dsh_ref[...] += _rows8(dh)
        dsc_ref[...] += _rows8(dhxn * gv)
        dg_ref[...] += _rows8(dhxn * one_sc)
        dg1_ref[...] += _rows8(dx * mix_ref[...])

    row = pl.BlockSpec((tm, D), lambda i: (i, 0))
    vec, acc = _full((1, D)), _full((SUBLANES, D))
    return pl.pallas_call(
        body, name="norm2_gate_bwd", grid=(T // tm,),
        in_specs=[row, row, row, row, vec, vec, vec], out_specs=[row, row, acc, acc, acc, acc],
        out_shape=[jax.ShapeDtypeStruct((T, D), F32), jax.ShapeDtypeStruct((T, D), BF16)]
        + [jax.ShapeDtypeStruct((SUBLANES, D), F32)] * 4,
        compiler_params=_params(("arbitrary",)),
    )(x1, dh2, dy, mix, g1, g, sc)


def _out_merge_bwd(dmix, w_out_t, ya, yb, proj):
    T = ya.shape[0]
    tm = min(TOKEN_TILE, T)

    def body(dmix_ref, w_ref, ya_ref, yb_ref, ga_ref, gb_ref, dya_ref, dyb_ref, dga_ref, dgb_ref):
        dm = jnp.dot(dmix_ref[...], w_ref[...], preferred_element_type=F32)
        sa = _sigmoid(ga_ref[...])
        sb = _sigmoid(gb_ref[...])
        dya_ref[...] = (dm * sa).astype(BF16)
        dyb_ref[...] = (dm * sb).astype(BF16)
        dga_ref[...] = (dm * ya_ref[...].astype(F32) * sa * (1.0 - sa)).astype(BF16)
        dgb_ref[...] = (dm * yb_ref[...].astype(F32) * sb * (1.0 - sb)).astype(BF16)

    row = pl.BlockSpec((tm, D), lambda i: (i, 0))
    return pl.pallas_call(
        body, name="out_merge_bwd", grid=(T // tm,),
        in_specs=[row, _full((D, D)), row, row, pl.BlockSpec((tm, D), lambda i: (i, COL_GA // D)),
                  pl.BlockSpec((tm, D), lambda i: (i, COL_GB // D))],
        out_specs=[row] * 4, out_shape=[jax.ShapeDtypeStruct((T, D), BF16)] * 4,
        compiler_params=_params(("parallel",)),
    )(dmix, w_out_t, ya, yb, proj, proj)


def _conv_branch_bwd(proj, dya0, conv_w8):
    T = proj.shape[0]
    tm = min(FFN_TM, T)
    nt = T // tm

    def body(cb_ref, cc_ref, cv_ref, cbn_ref, ccp_ref, cvp_ref, ccn_ref, cvn_ref, d_ref, dn_ref, w_ref,
             dcb_ref, dcc_ref, dcv_ref, dw_ref):
        i = pl.program_id(0)

        @pl.when(i == 0)
        def _():
            dw_ref[...] = jnp.zeros_like(dw_ref)

        first, last = i == 0, i == nt - 1
        w = w_ref[...]
        cc = jnp.concatenate([ccp_ref[...], cc_ref[...], ccn_ref[...]], axis=0)
        cv = jnp.concatenate([cvp_ref[...], cv_ref[...], cvn_ref[...]], axis=0)
        rows = lax.broadcasted_iota(jnp.int32, cc.shape, 0)
        z = jnp.where(jnp.logical_and(first, rows < SUBLANES), 0.0, cc * cv)
        z1 = pltpu.roll(z, 1, axis=0)
        z2 = pltpu.roll(z, 2, axis=0)
        cz = z2 * w[0:1] + z1 * w[1:2] + z * w[2:3]
        zeros8 = jnp.zeros((SUBLANES, CONV_W), F32)
        de = jnp.concatenate([zeros8, d_ref[...], jnp.where(last, 0.0, dn_ref[...])], axis=0)
        cbe = jnp.concatenate([zeros8, cb_ref[...], cbn_ref[...]], axis=0)
        dcz = de * cbe
        n = tm + 2 * SUBLANES
        dz = dcz * w[2:3] + pltpu.roll(dcz, n - 1, axis=0) * w[1:2] + pltpu.roll(dcz, n - 2, axis=0) * w[0:1]
        inner = slice(SUBLANES, SUBLANES + tm)
        dcb_ref[...] = (de * cz)[inner].astype(BF16)
        dcc_ref[...] = (dz * cv)[inner].astype(BF16)
        dcv_ref[...] = (dz * cc)[inner].astype(BF16)
        wrows = [jnp.sum((dcz * t)[inner], axis=0, keepdims=True) for t in (z2, z1, z)]
        dw_ref[...] += jnp.concatenate(wrows + [jnp.zeros((SUBLANES - 3, CONV_W), F32)], axis=0)

    blk = lambda col: pl.BlockSpec((tm, CONV_W), lambda i: (i, col))
    out_blk = pl.BlockSpec((tm, CONV_W), lambda i: (i, 0))
    return pl.pallas_call(
        body, name="conv_branch_bwd", grid=(nt,),
        in_specs=[blk(0), blk(1), blk(2), _next_spec(tm, CONV_W, 0, nt),
                  _prev_spec(tm, CONV_W, 1), _prev_spec(tm, CONV_W, 2),
                  _next_spec(tm, CONV_W, 1, nt), _next_spec(tm, CONV_W, 2, nt),
                  out_blk, _next_spec(tm, CONV_W, 0, nt), _full((SUBLANES, CONV_W))],
        out_specs=[out_blk, out_blk, out_blk, _full((SUBLANES, CONV_W))],
        out_shape=[jax.ShapeDtypeStruct((T, CONV_W), BF16)] * 3 + [jax.ShapeDtypeStruct((SUBLANES, CONV_W), F32)],
        compiler_params=_params(("arbitrary",)),
    )(proj, proj, proj, proj, proj, proj, proj, proj, dya0, dya0, conv_w8)


def _qk_norm_bwd(proj, dqs, dkh, dlogf, qg, kg, bf_pad, gmat):
    T = proj.shape[0]
    tm = min(TOKEN_TILE, T)

    def body(q_ref, k_ref, f_ref, dqs_ref, dkh_ref, dlf_ref, qg_ref, kg_ref, bf_ref, g_ref,
             dq_ref, dk_ref, dfl_ref, dqg_ref, dkg_ref, dbf_ref):
        @pl.when(pl.program_id(0) == 0)
        def _():
            dqg_ref[...] = jnp.zeros_like(dqg_ref)
            dkg_ref[...] = jnp.zeros_like(dkg_ref)
            dbf_ref[...] = jnp.zeros_like(dbf_ref)

        gm = g_ref[...]
        for src, d_src, gain, scale, dst, dgain in (
                (q_ref, dqs_ref, qg_ref, 1.0 / np.sqrt(HEAD_DIM), dq_ref, dqg_ref),
                (k_ref, dkh_ref, kg_ref, 1.0, dk_ref, dkg_ref)):
            v = src[...]
            dhat = d_src[...] * scale
            inv = lax.rsqrt(_split_dot(v * v, gm, 2) * (1.0 / HEAD_DIM) + RMS_EPS)
            vn = v * inv
            dgain[...] += _rows8(dhat * vn)
            dvn = dhat * gain[...]
            mean = _split_dot(dvn * vn, gm, 2) * (1.0 / HEAD_DIM)
            dst[...] = (inv * (dvn - vn * mean)).astype(BF16)
        fx = f_ref[...] + bf_ref[...]
        dfl = dlf_ref[...] * _sigmoid(-fx)
        dfl_ref[...] = dfl.astype(BF16)
        dbf_ref[...] += _rows8(dfl)

    blk = lambda col: pl.BlockSpec((tm, ATTN_W), lambda i: (i, col))
    out_blk = pl.BlockSpec((tm, ATTN_W), lambda i: (i, 0))
    f_in = pl.BlockSpec((tm, F_PAD), lambda i: (i, COL_F // F_PAD))
    f_blk = pl.BlockSpec((tm, F_PAD), lambda i: (i, 0))
    return pl.pallas_call(
        body, name="qk_norm_bwd", grid=(T // tm,),
        in_specs=[blk(3), blk(4), f_in, out_blk, out_blk, f_blk, _full((1, ATTN_W)), _full((1, ATTN_W)),
                  _full((1, F_PAD)), _full((ATTN_W, ATTN_W))],
        out_specs=[out_blk, out_blk, f_blk, _full((SUBLANES, ATTN_W)), _full((SUBLANES, ATTN_W)),
                   _full((SUBLANES, F_PAD))],
        out_shape=[jax.ShapeDtypeStruct((T, ATTN_W), BF16)] * 2 + [jax.ShapeDtypeStruct((T, F_PAD), BF16)]
        + [jax.ShapeDtypeStruct((SUBLANES, ATTN_W), F32)] * 2 + [jax.ShapeDtypeStruct((SUBLANES, F_PAD), F32)],
        compiler_params=_params(("arbitrary",)),
    )(proj, proj, proj, dqs, dkh, dlogf, qg, kg, bf_pad, gmat)


def _pad_rows8(w):
    return jnp.pad(w, ((0, SUBLANES - w.shape[0]), (0, 0)))


def _fold8(acc):
    return jnp.sum(acc, axis=0, keepdims=True)


def _late_weights(mats):
    out = {}
    for name in ("w_branch_a", "w_out", "w_up", "w_down"):
        out[name] = mats[name]
        out[name + "_t"] = mats[name].T
    out["w_branch_b_heads"] = _pad_head_rows(mats["w_branch_b"])
    out["w_branch_b_heads_t"] = out["w_branch_b_heads"].T
    return out


def _local_step(x, target, mod, wts, late=None):
    T = x.shape[0]
    tb = min(MATMUL_TILE, T)
    tm = min(TOKEN_TILE, T)
    sh1, sc1, g1, sh2, sc2, g2 = [mod[:, i * D:(i + 1) * D] for i in range(N_MOD)]
    w_in, w_in_t = wts["w_in"], wts["w_in_t"]
    conv_a8 = _pad_rows8(wts["conv_a_w"])
    conv_f8 = _pad_rows8(wts["conv_ffn_w"])
    qg = jnp.tile(wts["q_norm_g"], (1, N_HEADS))
    kg = jnp.tile(wts["k_norm_g"], (1, N_HEADS))
    bf_pad = jnp.pad(wts["b_f"], ((0, 0), (0, F_PAD - N_HEADS)))
    gmat = _group_matrix()

    h = _norm_mod(x, wts["norm1_g"], sc1, sh1, name="norm1_fwd")
    proj = _matmul(h, w_in, name="mm_in", tm=tb, tn=896, tk=D)
    fcum = _cumsum(proj, reverse=False, name="gate_cumsum", col=COL_F // F_PAD, gate_bias=bf_pad)
    ya0, qa, ka, va = _branch_prep(proj, fcum, conv_a8, qg, kg, gmat)
    if late is None:
        o_h, qb = _attn_fwd(qa, ka, va)
    else:
        o_h, qb, *gathered = _attn_fwd(qa, ka, va, _Exchange([late[name] for name, *_ in LATE], scatter=False))
        wts = dict(wts)
        wts.update(_late_weights({name: _join_shards(g, axis) for (name, _, _, axis), g in zip(LATE, gathered)}))
    ya, yb, merged = _branch_merge_fwd(ya0, o_h, proj, wts["w_branch_a"], wts["w_branch_b_heads"])
    mix, x1, h2 = _out_resid_norm(x, merged, wts["w_out"], g1, wts["norm2_g"], sc2, sh2)
    u = _matmul(h2, wts["w_up"], name="mm_up", tm=tb, tn=1408, tk=D)
    act, conv_gate, conv_val = _ffn_act_fwd(u, conv_f8)
    dy, dff, sq8, dg2_8 = _down_loss_head(x1, act, wts["w_down"], g2, target)
    sq = jnp.sum(sq8).reshape(1, 1)

    grads = {}
    da = _matmul(dff, wts["w_down_t"], name="mm_down_dx", tm=tb, tn=1408, tk=D)
    grads["w_down"] = _matmul(act, dff, name="mm_down_dw", tm=1408, tn=D, tk=tb, trans_a=True)
    dug, duv, dconv_f8 = _ffn_act_bwd(u, conv_gate, conv_val, da, conv_f8)
    grads["conv_ffn_w"] = dconv_f8[:3]
    dh2 = _matmul_pieces([dug, duv], wts["w_up_t"], name="mm_up_dx", tm=tm)
    dw_up = [_matmul(h2, d, name="mm_up_dw_" + half, tm=D, tn=1408, tk=tb, trans_a=True)
             for half, d in (("gate", dug), ("val", duv))]
    if late is None:
        grads["w_up"] = jnp.concatenate(dw_up, axis=1)
    dx1, dmix, dsh2_8, dsc2_8, dn2_8, dg1_8 = _norm2_gate_bwd(x1, dh2, dy, mix, g1, wts["norm2_g"], sc2)
    grads["norm2_g"] = _fold8(dn2_8)

    grads["w_out"] = _matmul(merged, dmix, name="mm_out_dw", tm=D, tn=D, tk=tb, trans_a=True)
    dya, dyb, dga, dgb = _out_merge_bwd(dmix, wts["w_out_t"], ya, yb, proj)
    dya0 = _matmul(dya, wts["w_branch_a_t"], name="mm_branch_a_dx", tm=tb, tn=CONV_W, tk=D)
    grads["w_branch_a"] = _matmul(ya0, dya, name="mm_branch_a_dw", tm=CONV_W, tn=D, tk=tb, trans_a=True)
    doa = _branch_b_bwd(dyb, o_h, wts["w_branch_b_heads_t"])
    grads["w_branch_b"] = _branch_b_dw(o_h, dyb)[:, :HEAD_DIM].reshape(ATTN_W, D)
    dcb, dcc, dcv, dconv_a8 = _conv_branch_bwd(proj, dya0, conv_a8)
    grads["conv_a_w"] = dconv_a8[:3]

    parts = {}
    if late is None:
        dq_h, dk_h, dv_h = _attn_bwd(qb, ka, va, doa)
    else:
        ready = [(_column_shards(dw_up) if name == "w_up" else _split_shards(grads[name], axis)).astype(BF16)
                 for name, _, _, axis in LATE]
        dq_h, dk_h, dv_h, *recv = _attn_bwd(
            qb, ka, va, doa, _Exchange(ready + [_pack_full_by_dest(grads, CONVS, SUBLANES)], scatter=True))
        parts = dict(zip([name for name, *_ in LATE] + ["conv"], recv))
    dq_tok, dk_tok, dv_tok, dfcum = _attn_unpack(dq_h, dk_h, dv_h)
    dlogf = _cumsum(dfcum, reverse=True, name="gate_cumsum_bwd")
    dq, dk, dfl, dqg8, dkg8, dbf8 = _qk_norm_bwd(proj, dq_tok, dk_tok, dlogf, qg, kg, bf_pad, gmat)
    grads["q_norm_g"] = jnp.sum(_fold8(dqg8).reshape(N_HEADS, HEAD_DIM), axis=0, keepdims=True)
    grads["k_norm_g"] = jnp.sum(_fold8(dkg8).reshape(N_HEADS, HEAD_DIM), axis=0, keepdims=True)
    grads["b_f"] = _fold8(dbf8)[:, :N_HEADS]
    narrow, wide = [dcb, dcc, dcv, dq, dk, dv_tok], [dga, dgb, dfl]
    dw_narrow = _matmul_tn_pieces(h, narrow, name="mm_in_dw_narrow", tk=tm)
    dwa, dwb, dwf = _matmul_tn_pieces(h, wide, name="mm_in_dw_wide", tk=tm)
    dw_in = list(dw_narrow) + [dwf[:, :N_HEADS], dwa, dwb]
    if late is None:
        grads["w_in"] = jnp.concatenate(dw_in, axis=1)
        dh = _matmul_pieces(narrow + wide, w_in_t, name="mm_in_dx", tm=tm)
    else:
        dh, parts["w_in"] = _matmul_pieces(
            narrow + wide, w_in_t, name="mm_in_dx", tm=tm,
            exchange=_Exchange([_column_shards(dw_in).astype(BF16)], scatter=True))
    grad_x, dsh1_8, dsc1_8, dn1_8 = _norm_bwd(x, dh, dx1, wts["norm1_g"], sc1, name="norm1_bwd")
    grads["norm1_g"] = _fold8(dn1_8)
    grads["mod"] = jnp.concatenate([_fold8(a) for a in (dsh1_8, dsc1_8, dg1_8, dsh2_8, dsc2_8, dg2_8)], axis=1)
    return sq, grad_x, grads, parts


def _me_and_peers():
    mx, my, mc = lax.axis_index("x"), lax.axis_index("y"), lax.axis_index("c")
    me = 4 * mx + 2 * my + mc
    peers = []
    for k in range(1, N_DEV):
        px = 1 - mx if k & 4 else mx
        py = 1 - my if k & 2 else my
        pc = 1 - mc if k & 1 else mc
        peers.append(((px, py, pc), 4 * px + 2 * py + pc))
    return me, peers


HBM_SPEC = pl.BlockSpec(memory_space=pltpu.HBM)


class _Exchange:
    def __init__(self, xs, scatter):
        self.xs, self.scatter, self.n = list(xs), scatter, len(xs)
        self.out_shapes = [jax.ShapeDtypeStruct(x.shape if scatter else (N_DEV,) + x.shape, x.dtype) for x in xs]
        self.in_specs = [HBM_SPEC] * self.n
        self.out_specs = [HBM_SPEC] * self.n
        self.scratch = [pltpu.SemaphoreType.DMA((self.n, N_DEV - 1)), pltpu.SemaphoreType.DMA((self.n, N_DEV - 1)),
                        pltpu.SemaphoreType.DMA((self.n,))]

    def _copies(self, x_refs, out_refs, sems):
        send_sems, recv_sems, local_sems = sems
        me, peers = _me_and_peers()

        def src(a, idx):
            return x_refs[a].at[idx] if self.scatter else x_refs[a]

        def copy(a, k, from_idx, to_slot, device):
            return pltpu.make_async_remote_copy(
                src_ref=src(a, from_idx), dst_ref=out_refs[a].at[to_slot], send_sem=send_sems.at[a, k],
                recv_sem=recv_sems.at[a, k], device_id=device, device_id_type=MESH)

        local = [pltpu.make_async_copy(src(a, me), out_refs[a].at[me], local_sems.at[a]) for a in range(self.n)]
        sends = [copy(a, k, idx, me, dev) for a in range(self.n) for k, (dev, idx) in enumerate(peers)]
        recvs = [copy(a, k, idx, idx, dev) for a in range(self.n) for k, (dev, idx) in enumerate(peers)]
        return local, sends, recvs

    def start(self, x_refs, out_refs, sems):
        local, sends, _ = self._copies(x_refs, out_refs, sems)
        for cp in local + sends:
            cp.start()

    def wait(self, x_refs, out_refs, sems):
        local, sends, recvs = self._copies(x_refs, out_refs, sems)
        for cp in recvs:
            cp.wait_recv()
        for cp in sends:
            cp.wait_send()
        for cp in local:
            cp.wait()

    def split(self, refs, n_in, n_out):
        n = self.n
        ins, xin = refs[:n_in], refs[n_in:n_in + n]
        outs, xout = refs[n_in + n:n_in + n + n_out], refs[n_in + n + n_out:n_in + 2 * n + n_out]
        rest = refs[n_in + 2 * n + n_out:]
        return ins, outs, rest[:len(rest) - 3], (xin, xout, rest[len(rest) - 3:])


def _ride(exchange, first, last, refs):
    if exchange is None:
        return

    @pl.when(first)
    def _():
        exchange.start(*refs)

    @pl.when(last)
    def _():
        exchange.wait(*refs)


def _gather_two_level(xs, *, name):
    n = len(xs)
    out_shapes = [jax.ShapeDtypeStruct((N_DEV,) + x.shape, x.dtype) for x in xs]

    def body(*refs):
        x_refs, out_refs = refs[:n], refs[n:2 * n]
        send_sems, recv_sems, local_sems = refs[2 * n:]
        x, y, c = lax.axis_index("x"), lax.axis_index("y"), lax.axis_index("c")
        me, sibling = (x, y, c), (x, y, 1 - c)
        chips = [(1 - x, y), (x, 1 - y), (1 - x, 1 - y)]

        def slot(a, dev):
            return out_refs[a].at[4 * dev[0] + 2 * dev[1] + dev[2]]

        def copy(a, k, block, to, src=None):
            return pltpu.make_async_remote_copy(
                src_ref=slot(a, block) if src is None else src, dst_ref=slot(a, block),
                send_sem=send_sems.at[a, k], recv_sem=recv_sems.at[a, k], device_id=to, device_id_type=MESH)

        mine = [pltpu.make_async_copy(x_refs[a], slot(a, me), local_sems.at[a]) for a in range(n)]
        first = [copy(a, 0, me, sibling, src=x_refs[a]) for a in range(n)]
        first += [copy(a, 1 + j, me, (*chip, c), src=x_refs[a]) for a in range(n) for j, chip in enumerate(chips)]
        for cp in mine + first:
            cp.start()
        passed = []
        for a in range(n):
            for j, chip in enumerate(chips):
                copy(a, 1 + j, (*chip, c), me).wait_recv()
                passed.append(copy(a, 4 + j, (*chip, c), sibling))
                passed[-1].start()
        for a in range(n):
            copy(a, 0, sibling, me).wait_recv()
            for j, chip in enumerate(chips):
                copy(a, 4 + j, (*chip, 1 - c), me).wait_recv()
        for cp in first + passed:
            cp.wait_send()
        for cp in mine:
            cp.wait()

    return pl.pallas_call(
        body, name=name, in_specs=[HBM_SPEC] * n, out_specs=[HBM_SPEC] * n, out_shape=out_shapes,
        scratch_shapes=[pltpu.SemaphoreType.DMA((n, N_DEV - 1)), pltpu.SemaphoreType.DMA((n, N_DEV - 1)),
                        pltpu.SemaphoreType.DMA((n,))],
        compiler_params=pltpu.CompilerParams(has_side_effects=True),
    )(*xs)


def _exchange(xs, *, name, scatter):
    ex = _Exchange(xs, scatter)

    def body(*refs):
        _, _, _, xrefs = ex.split(refs, 0, 0)
        ex.start(*xrefs)
        ex.wait(*xrefs)

    return pl.pallas_call(
        body, name=name, in_specs=ex.in_specs, out_specs=ex.out_specs, out_shape=ex.out_shapes,
        scratch_shapes=ex.scratch, compiler_params=pltpu.CompilerParams(has_side_effects=True),
    )(*xs)


def _ada_fwd(c_all, w_shard, b_shard):
    n = w_shard.shape[1]

    def body(c_ref, w_ref, b_ref, o_ref):
        cv = c_ref[...]
        act = (cv * _sigmoid(cv)).astype(BF16)
        o_ref[...] = jnp.dot(act, w_ref[...].astype(BF16), preferred_element_type=F32) + b_ref[...]

    return pl.pallas_call(
        body, name="ada_fwd", in_specs=[_full((N_DEV, D)), _full((D, n)), _full((1, n))],
        out_specs=_full((N_DEV, n)), out_shape=jax.ShapeDtypeStruct((N_DEV, n), F32), grid=(1,),
        compiler_params=_params(("arbitrary",)),
    )(c_all, w_shard, b_shard)


def _ada_bwd(c_all_t, dmod_pad):
    n = dmod_pad.shape[1]

    def body(c_ref, d_ref, o_ref):
        cv = c_ref[...]
        act = (cv * _sigmoid(cv)).astype(BF16)
        o_ref[...] = jnp.dot(act, d_ref[...].astype(BF16), preferred_element_type=F32)

    return pl.pallas_call(
        body, name="ada_bwd", in_specs=[_full((D, LANES)), _full((LANES, n))],
        out_specs=_full((D, n)), out_shape=jax.ShapeDtypeStruct((D, n), F32), grid=(1,),
        compiler_params=_params(("arbitrary",)),
    )(c_all_t, dmod_pad)


ADAM_ROWS = 256


def _adamw(parts, w, m, v, *, name):
    n, R, C = parts.shape
    tr = next((t for t in (ADAM_ROWS, 128, 64, 32, 16, SUBLANES) if R % t == 0), R)

    def body(p_ref, w_ref, m_ref, v_ref, g_ref, d_ref, nm_ref, nv_ref):
        g = p_ref[0].astype(F32)
        for j in range(1, n):
            g = g + p_ref[j].astype(F32)
        g_ref[...] = g
        nm = ADAM_B1 * m_ref[...] + (1.0 - ADAM_B1) * g
        nv = ADAM_B2 * v_ref[...] + (1.0 - ADAM_B2) * (g * g)
        nm_ref[...] = nm
        nv_ref[...] = nv
        m_hat = nm / (1.0 - ADAM_B1 ** ADAM_STEP)
        v_hat = nv / (1.0 - ADAM_B2 ** ADAM_STEP)
        d_ref[...] = -ADAM_LR * (m_hat / (jnp.sqrt(v_hat) + ADAM_EPS) + ADAM_WD * w_ref[...])

    row = pl.BlockSpec((tr, C), lambda i: (i, 0))
    return pl.pallas_call(
        body, name=name, grid=(R // tr,),
        in_specs=[pl.BlockSpec((n, tr, C), lambda i: (0, i, 0)), row, row, row], out_specs=[row] * 4,
        out_shape=[jax.ShapeDtypeStruct((R, C), F32)] * 4,
        compiler_params=_params(("parallel",)),
    )(parts, w, m, v)


SHARDED = (("w_in", D, IN_W, 1), ("w_branch_a", CONV_W, D, 1), ("w_branch_b", ATTN_W, D, 1), ("w_out", D, D, 0),
           ("w_up", D, 2 * D_FF, 1), ("w_down", D_FF, D, 0), ("conv_a_w", 3, CONV_W, 1),
           ("conv_ffn_w", 3, 2 * D_FF, 1))
MATRICES = SHARDED[:6]
LATE = MATRICES[1:]
CONVS = SHARDED[6:]
REPLICATED = (("b_ada", N_MOD * D), ("norm1_g", D), ("norm2_g", D), ("b_f", N_HEADS), ("q_norm_g", HEAD_DIM),
              ("k_norm_g", HEAD_DIM))


def _shard_shape(rows, cols, axis):
    return (rows // N_DEV, cols) if axis == 0 else (rows, cols // N_DEV)


def _pack_rows(flat, multiple):
    length = flat.shape[-1]
    rows = -(-length // PACK_W)
    rows = -(-rows // multiple) * multiple
    pad = [(0, 0)] * (flat.ndim - 1) + [(0, rows * PACK_W - length)]
    return jnp.pad(flat, pad).reshape(flat.shape[:-1] + (rows, PACK_W))


def _pack_shards(shards, spec, multiple, dtype):
    flat = jnp.concatenate([shards[name].reshape(-1).astype(dtype) for name, *_ in spec])
    return _pack_rows(flat, multiple)


def _join_lane_blocks(gathered):
    n, r, c = gathered.shape

    def body(g_ref, o_ref):
        for j in range(n):
            o_ref[:, j * c:(j + 1) * c] = g_ref[j]

    return pl.pallas_call(
        body, name="join_lane_blocks", grid=(1,), in_specs=[_full((n, r, c))], out_specs=_full((r, n * c)),
        out_shape=jax.ShapeDtypeStruct((r, n * c), gathered.dtype), compiler_params=_params(("arbitrary",)),
    )(gathered)


def _join_shards(gathered, axis):
    if axis == 0:
        return gathered.reshape(N_DEV * gathered.shape[1], gathered.shape[2])
    if gathered.shape[2] == LANES:
        return _join_lane_blocks(gathered)
    return jnp.concatenate([gathered[j] for j in range(N_DEV)], axis=1)


def _column_shards(pieces):
    total = sum(p.shape[1] for p in pieces)
    width = total // N_DEV
    shards = []
    for j in range(N_DEV):
        lo, hi, off, segs = j * width, (j + 1) * width, 0, []
        for p in pieces:
            a, b = max(lo, off), min(hi, off + p.shape[1])
            if a < b:
                segs.append(p[:, a - off:b - off])
            off += p.shape[1]
        shards.append(segs[0] if len(segs) == 1 else jnp.concatenate(segs, axis=1))
    return jnp.stack(shards)


def _split_shards(full, axis):
    if axis == 0:
        return full.reshape(N_DEV, full.shape[0] // N_DEV, full.shape[1])
    c = full.shape[1] // N_DEV
    return jnp.stack([full[:, j * c:(j + 1) * c] for j in range(N_DEV)])


def _unpack_shards(packed, spec):
    flat = packed.reshape(-1)
    out, off = {}, 0
    for name, rows, cols, axis in spec:
        r, c = _shard_shape(rows, cols, axis)
        out[name] = flat[off:off + r * c].reshape(r, c)
        off += r * c
    return out


def _unpack_gathered(gathered, spec):
    flat = gathered.reshape(N_DEV, -1)
    out, off = {}, 0
    for name, rows, cols, axis in spec:
        r, c = _shard_shape(rows, cols, axis)
        seg = flat[:, off:off + r * c].reshape(N_DEV, r, c)
        out[name] = seg.reshape(rows, cols) if axis == 0 else seg.transpose(1, 0, 2).reshape(rows, cols)
        off += r * c
    return out


def _pack_full_by_dest(full, spec, multiple):
    segs = []
    for name, rows, cols, axis in spec:
        r, c = _shard_shape(rows, cols, axis)
        a = full[name]
        seg = a.reshape(N_DEV, r, c) if axis == 0 else a.reshape(rows, N_DEV, c).transpose(1, 0, 2)
        segs.append(seg.reshape(N_DEV, r * c))
    return _pack_rows(jnp.concatenate(segs, axis=1), multiple)


def _pad_in(w_in):
    return jnp.concatenate([w_in[:, :COL_GA], w_in[:, COL_GA + N_HEADS:], w_in[:, COL_GA:COL_GA + N_HEADS],
                            jnp.zeros((w_in.shape[0], IN_W_PAD - IN_W), w_in.dtype)], axis=1)


def _pad_head_rows(w):
    n = w.shape[1]
    padded = jnp.pad(w.reshape(N_HEADS, HEAD_DIM, n), ((0, 0), (0, LANES - HEAD_DIM), (0, 0)))
    return padded.reshape(N_HEADS * LANES, n)


def kernel(x, c, w_ada, b_ada, norm1_g, w_in, b_f, conv_a_w, q_norm_g, k_norm_g, w_branch_a, w_branch_b, w_out, norm2_g, w_up, conv_ffn_w, w_down, loss_target, m_w_ada, m_b_ada, m_norm1_g, m_w_in, m_b_f, m_conv_a_w, m_q_norm_g, m_k_norm_g, m_w_branch_a, m_w_branch_b, m_w_out, m_norm2_g, m_w_up, m_conv_ffn_w, m_w_down, v_w_ada, v_b_ada, v_norm1_g, v_w_in, v_b_f, v_conv_a_w, v_q_norm_g, v_k_norm_g, v_w_branch_a, v_w_branch_b, v_w_out, v_norm2_g, v_w_up, v_conv_ffn_w, v_w_down):
    names = ("w_ada", "b_ada", "norm1_g", "w_in", "b_f", "conv_a_w", "q_norm_g", "k_norm_g", "w_branch_a",
             "w_branch_b", "w_out", "norm2_g", "w_up", "conv_ffn_w", "w_down")
    squeeze = lambda a: a[0] if a.ndim == 3 else a
    W = dict(zip(names, map(squeeze, (w_ada, b_ada, norm1_g, w_in, b_f, conv_a_w, q_norm_g, k_norm_g, w_branch_a,
                                      w_branch_b, w_out, norm2_g, w_up, conv_ffn_w, w_down))))
    M = dict(zip(names, map(squeeze, (m_w_ada, m_b_ada, m_norm1_g, m_w_in, m_b_f, m_conv_a_w, m_q_norm_g,
                                      m_k_norm_g, m_w_branch_a, m_w_branch_b, m_w_out, m_norm2_g, m_w_up,
                                      m_conv_ffn_w, m_w_down))))
    V = dict(zip(names, map(squeeze, (v_w_ada, v_b_ada, v_norm1_g, v_w_in, v_b_f, v_conv_a_w, v_q_norm_g,
                                      v_k_norm_g, v_w_branch_a, v_w_branch_b, v_w_out, v_norm2_g, v_w_up,
                                      v_conv_ffn_w, v_w_down))))
    me = 4 * lax.axis_index("x") + 2 * lax.axis_index("y") + lax.axis_index("c")
    ada_n = N_MOD * D // N_DEV

    small = jnp.concatenate([c.reshape(-1), W["conv_a_w"].reshape(-1), W["conv_ffn_w"].reshape(-1)])
    small_all, w_in_all = _gather_two_level([_pack_rows(small, SUBLANES), W["w_in"].astype(BF16)],
                                            name="gather_first")
    small_all = small_all.reshape(N_DEV, -1)
    c_all = small_all[:, :D]
    conv_all = _unpack_gathered(small_all[:, D:], CONVS)

    b_shard = lax.dynamic_slice(W["b_ada"], (0, me * ada_n), (1, ada_n))
    mod_part = _ada_fwd(c_all, W["w_ada"], b_shard)
    mod_all, = _exchange([mod_part], name="gather_mod", scatter=False)
    mod = lax.dynamic_index_in_dim(mod_all, me, axis=1, keepdims=False).reshape(1, N_MOD * D)

    wts = {"w_in": _pad_in(_join_shards(w_in_all, 1))}
    wts["w_in_t"] = wts["w_in"].T
    wts.update(conv_all)
    for name in ("norm1_g", "norm2_g", "q_norm_g", "k_norm_g", "b_f"):
        wts[name] = W[name]
    late = {name: W[name].astype(BF16) for name, *_ in LATE}

    sq, grad_x, grads, parts = _local_step(x[0], loss_target[0], mod, wts, late)
    loss = lax.psum(sq[0, 0] * (0.5 / D), AXES)

    grads["b_ada"] = grads["mod"]
    rep_flat = lambda src: jnp.concatenate([src[name].reshape(-1) for name, _ in REPLICATED])
    rep_parts, = _exchange([_pack_rows(rep_flat(grads), 16)], name="gather_small_grads", scatter=False)
    rep_out = _adamw(rep_parts, *[_pack_rows(rep_flat(s), 16) for s in (W, M, V)], name="adamw_replicated")

    dmod_all = rep_parts.reshape(N_DEV, -1)[:, :N_MOD * D]
    dmod_mine = lax.dynamic_slice(dmod_all, (0, me * ada_n), (N_DEV, ada_n))
    g_ada = _ada_bwd(jnp.pad(c_all.T, ((0, 0), (0, LANES - N_DEV))),
                     jnp.pad(dmod_mine, ((0, LANES - N_DEV), (0, 0))))
    ada_out = _adamw(g_ada[None], W["w_ada"], M["w_ada"], V["w_ada"], name="adamw_ada")

    mat_out = {name: _adamw(parts[name], W[name], M[name], V[name], name="adamw_" + name) for name, *_ in MATRICES}
    conv_out = _adamw(parts["conv"], *[_pack_shards(s, CONVS, SUBLANES, F32) for s in (W, M, V)],
                      name="adamw_conv")

    results = []
    for kind in range(4):
        per = {"w_ada": ada_out[kind]}
        per.update({name: out[kind] for name, out in mat_out.items()})
        per.update(_unpack_shards(conv_out[kind], CONVS))
        flat, off = rep_out[kind].reshape(-1), 0
        for name, n in REPLICATED:
            per[name] = flat[off:off + n].reshape(1, n)
            off += n
        results.append(per)
    restore = lambda name, a: a[None] if W[name].ndim == 2 and name not in dict(REPLICATED) else a
    outs = [loss, grad_x[None]]
    for per in results:
        outs.extend(restore(name, per[name]) for name in names)
    return tuple(outs)
```

```python
import functools

import jax
import jax.numpy as jnp
import numpy as np
from jax import lax
from jax.experimental import pallas as pl
from jax.experimental.pallas import tpu as pltpu

F32 = jnp.float32
BF16 = jnp.bfloat16

N_DEV = 8
D = 1024
N_HEADS = 8
HEAD_DIM = 64
ATTN_W = 512
CONV_W = 512
D_FF = 2816
N_MOD = 6
IN_W = 5128
RMS_EPS = 1e-6
NEG_INF = -1e30

IN_W_PAD = 5376
COL_GA = 3072
COL_GB = 4096
COL_F = 5120
F_PAD = 128

ADAM_LR = 0.001
ADAM_B1 = 0.9
ADAM_B2 = 0.999
ADAM_EPS = 1e-08
ADAM_WD = 0.01
ADAM_STEP = 10

LANES = 128
SUBLANES = 8
VMEM_LIMIT = 52 * 1024 * 1024
TOKEN_TILE = 512
MATMUL_TILE = 1024
ATTN_BLOCK = 512
PACK_W = 1024

MESH = pl.DeviceIdType.MESH
AXES = ("x", "y", "c")


def _params(sem=None, **kw):
    return pltpu.CompilerParams(dimension_semantics=sem, vmem_limit_bytes=VMEM_LIMIT, **kw)


def _full(shape):
    nd = len(shape)
    return pl.BlockSpec(shape, lambda *_: (0,) * nd)


def _tn_dot(a, b):
    return lax.dot_general(a, b, (((0,), (0,)), ((), ())), preferred_element_type=F32)


def _matmul(a, b, *, name, tm, tn, tk, out_dtype=F32, trans_a=False, exchange=None):
    if trans_a:
        K, M = a.shape
    else:
        M, K = a.shape
    N = b.shape[1]
    assert b.shape[0] == K and M % tm == 0 and N % tn == 0 and K % tk == 0, (name, a.shape, b.shape)
    nm, nn, nk = M // tm, N // tn, K // tk

    def body(*refs):
        if exchange is None:
            a_ref, b_ref, o_ref, *own = refs
        else:
            (a_ref, b_ref), (o_ref,), own, xrefs = exchange.split(refs, 2, 1)
            ids = [pl.program_id(d) for d in range(3)]
            first = jnp.logical_and(jnp.logical_and(ids[0] == 0, ids[1] == 0), ids[2] == 0)
            last = jnp.logical_and(jnp.logical_and(ids[0] == nn - 1, ids[1] == nm - 1), ids[2] == nk - 1)
            _ride(exchange, first, last, xrefs)
        k = pl.program_id(2)
        av = a_ref[...].astype(BF16)
        bv = b_ref[...].astype(BF16)
        prod = _tn_dot(av, bv) if trans_a else jnp.dot(av, bv, preferred_element_type=F32)
        if nk == 1:
            o_ref[...] = prod.astype(out_dtype)
            return
        acc_ref, = own

        @pl.when(k == 0)
        def _():
            acc_ref[...] = prod

        @pl.when(k > 0)
        def _():
            acc_ref[...] += prod

        @pl.when(k == nk - 1)
        def _():
            o_ref[...] = acc_ref[...].astype(out_dtype)

    if trans_a:
        a_spec = pl.BlockSpec((tk, tm), lambda j, i, k: (k, i))
    else:
        a_spec = pl.BlockSpec((tm, tk), lambda j, i, k: (i, k))
    in_specs = [a_spec, pl.BlockSpec((tk, tn), lambda j, i, k: (k, j))]
    out_spec = pl.BlockSpec((tm, tn), lambda j, i, k: (i, j))
    out_shape = jax.ShapeDtypeStruct((M, N), out_dtype)
    scratch = [pltpu.VMEM((tm, tn), F32)] if nk > 1 else []
    if exchange is None:
        return pl.pallas_call(
            body, name=name, grid=(nn, nm, nk), in_specs=in_specs, out_specs=out_spec, out_shape=out_shape,
            scratch_shapes=scratch, compiler_params=_params(("parallel", "parallel", "arbitrary")),
        )(a, b)
    return pl.pallas_call(
        body, name=name, grid=(nn, nm, nk), in_specs=in_specs + exchange.in_specs,
        out_specs=[out_spec] + exchange.out_specs, out_shape=[out_shape] + exchange.out_shapes,
        scratch_shapes=scratch + exchange.scratch, compiler_params=_params(("arbitrary",) * 3),
    )(a, b, *exchange.xs)


def _matmul_pieces(pieces, b, *, name, tm, exchange=None):
    M = pieces[0].shape[0]
    widths = [p.shape[1] for p in pieces]
    offsets = [sum(widths[:i]) for i in range(len(widths))]
    N = b.shape[1]
    assert b.shape[0] >= sum(widths) and M % tm == 0, (name, widths, b.shape)
    n_p, nm = len(pieces), M // tm

    def body(*refs):
        if exchange is None:
            ins, o_ref = refs[:n_p + 1], refs[n_p + 1]
        else:
            ins, (o_ref,), _, xrefs = exchange.split(refs, n_p + 1, 1)
            i = pl.program_id(0)
            _ride(exchange, i == 0, i == nm - 1, xrefs)
        b_ref = ins[n_p]
        acc = None
        for a_ref, off, w in zip(ins[:n_p], offsets, widths):
            term = jnp.dot(a_ref[...].astype(BF16), b_ref[off:off + w, :], preferred_element_type=F32)
            acc = term if acc is None else acc + term
        o_ref[...] = acc

    in_specs = [pl.BlockSpec((tm, w), lambda i: (i, 0)) for w in widths] + [_full(b.shape)]
    out_spec = pl.BlockSpec((tm, N), lambda i: (i, 0))
    out_shape = jax.ShapeDtypeStruct((M, N), F32)
    if exchange is None:
        return pl.pallas_call(
            body, name=name, grid=(nm,), in_specs=in_specs, out_specs=out_spec, out_shape=out_shape,
            compiler_params=_params(("parallel",)),
        )(*pieces, b)
    return pl.pallas_call(
        body, name=name, grid=(nm,), in_specs=in_specs + exchange.in_specs,
        out_specs=[out_spec] + exchange.out_specs, out_shape=[out_shape] + exchange.out_shapes,
        scratch_shapes=exchange.scratch, compiler_params=_params(("arbitrary",)),
    )(*pieces, b, *exchange.xs)


def _matmul_tn_pieces(a, pieces, *, name, tk):
    K, M = a.shape
    widths = [p.shape[1] for p in pieces]
    n_p, nk = len(pieces), K // tk

    def body(*refs):
        a_ref, p_refs, o_refs = refs[0], refs[1:n_p + 1], refs[n_p + 1:]
        k = pl.program_id(0)
        av = a_ref[...].astype(BF16)
        for p_ref, o_ref in zip(p_refs, o_refs):
            prod = _tn_dot(av, p_ref[...].astype(BF16))

            @pl.when(k == 0)
            def _():
                o_ref[...] = prod

            @pl.when(k > 0)
            def _():
                o_ref[...] += prod

    return pl.pallas_call(
        body, name=name, grid=(nk,),
        in_specs=[pl.BlockSpec((tk, M), lambda k: (k, 0))] + [pl.BlockSpec((tk, w), lambda k: (k, 0)) for w in widths],
        out_specs=[_full((M, w)) for w in widths], out_shape=[jax.ShapeDtypeStruct((M, w), F32) for w in widths],
        compiler_params=_params(("arbitrary",)),
    )(a, *pieces)


def _split_dot(x, mat, parts):
    out = None
    rem = x
    for p in range(parts):
        piece = rem.astype(BF16)
        term = jnp.dot(piece, mat, preferred_element_type=F32)
        out = term if out is None else out + term
        if p + 1 < parts:
            rem = rem - piece.astype(F32)
    return out


def _sigmoid(x):
    return 1.0 / (1.0 + jnp.exp(-x))


def _rows8(x):
    r, c = x.shape
    return jnp.sum(x.reshape(r // SUBLANES, SUBLANES, c), axis=0)


def _shift_down(blk, prev8, n):
    rolled = pltpu.roll(blk, n, axis=0)
    prev_rolled = pltpu.roll(prev8, n, axis=0)
    rows = lax.broadcasted_iota(jnp.int32, prev8.shape, 0)
    first = jnp.where(rows < n, prev_rolled, rolled[0:SUBLANES])
    return jnp.concatenate([first, rolled[SUBLANES:]], axis=0)


def _shift_up(blk, next8, n):
    r = blk.shape[0]
    rolled = pltpu.roll(blk, r - n, axis=0)
    next_rolled = pltpu.roll(next8, SUBLANES - n, axis=0)
    rows = lax.broadcasted_iota(jnp.int32, next8.shape, 0)
    last = jnp.where(rows >= SUBLANES - n, next_rolled, rolled[r - SUBLANES:])
    return jnp.concatenate([rolled[:r - SUBLANES], last], axis=0)


def _prev_spec(tm, width, col):
    per = tm // SUBLANES
    return pl.BlockSpec((SUBLANES, width), lambda i, *_: (jnp.maximum(i * per - 1, 0), col))


def _next_spec(tm, width, col, n_tiles):
    per = tm // SUBLANES
    last = n_tiles * per - 1
    return pl.BlockSpec((SUBLANES, width), lambda i, *_: (jnp.minimum((i + 1) * per, last), col))


def _group_matrix():
    idx = np.arange(ATTN_W) // HEAD_DIM
    return jnp.asarray((idx[:, None] == idx[None, :]).astype(np.float32), BF16)


def _norm_mod(x, g, sc, sh, *, name):
    T = x.shape[0]
    tm = min(TOKEN_TILE, T)

    def body(x_ref, g_ref, sc_ref, sh_ref, o_ref):
        xv = x_ref[...]
        inv = lax.rsqrt(jnp.mean(xv * xv, axis=-1, keepdims=True) + RMS_EPS)
        o_ref[...] = ((xv * inv) * g_ref[...] * (1.0 + sc_ref[...]) + sh_ref[...]).astype(BF16)

    row = pl.BlockSpec((tm, D), lambda i: (i, 0))
    return pl.pallas_call(
        body, name=name, grid=(T // tm,),
        in_specs=[row, _full((1, D)), _full((1, D)), _full((1, D))],
        out_specs=row, out_shape=jax.ShapeDtypeStruct((T, D), BF16),
        compiler_params=_params(("parallel",)),
    )(x, g, sc, sh)


LANE_ONE = 64
LANE_F = 67
LANE_LSE = 70
LANE_SUM = 73


def _pieces(x):
    hi = x.astype(BF16).astype(F32)
    rest = x - hi
    mid = rest.astype(BF16).astype(F32)
    return hi, mid, rest - mid


def _aug(lane, data, entries):
    out = jnp.where(lane < HEAD_DIM, data, 0.0)
    for idx, val in entries:
        out = jnp.where(lane == idx, val, out)
    return out


def _run(start, vals):
    return [(start + i, v) for i, v in enumerate(vals)]


def _head_lanes(a, h):
    blk = a[:, LANES * (h // 2):LANES * (h // 2) + LANES]
    return blk if h % 2 == 0 else pltpu.roll(blk, HEAD_DIM, axis=1)


def _branch_prep(proj, fcum, conv_w8, qg, kg, gmat):
    T = proj.shape[0]
    tm = min(TOKEN_TILE, T)
    nt = T // tm

    def body(cb_ref, cc_ref, cv_ref, q_ref, k_ref, v_ref, f_ref, ccp_ref, cvp_ref, w_ref, qg_ref, kg_ref, g_ref,
             ya_ref, qa_ref, ka_ref, va_ref):
        i = pl.program_id(0)
        z = cc_ref[...] * cv_ref[...]
        zp = jnp.where(i > 0, ccp_ref[...] * cvp_ref[...], 0.0)
        w = w_ref[...]
        cz = _shift_down(z, zp, 2) * w[0:1] + _shift_down(z, zp, 1) * w[1:2] + z * w[2:3]
        ya_ref[...] = (cb_ref[...] * cz).astype(BF16)
        gm = g_ref[...]

        def normed(src, gain, scale):
            v = src[...]
            ms = _split_dot(v * v, gm, 2) * (1.0 / HEAD_DIM)
            return (v * lax.rsqrt(ms + RMS_EPS)) * gain[...] * scale

        qn = normed(q_ref, qg_ref, 1.0 / np.sqrt(HEAD_DIM))
        kn = normed(k_ref, kg_ref, 1.0)
        vv = v_ref[...]
        fall = f_ref[...]
        lane = lax.broadcasted_iota(jnp.int32, (tm, LANES), 1)
        ones3 = [1.0, 1.0, 1.0]
        for h in range(N_HEADS):
            hi, mid, lo = _pieces(fall[:, h:h + 1])
            qa_ref[h] = _aug(lane, _head_lanes(qn, h), _run(LANE_ONE, ones3) + _run(LANE_F, [hi, mid, lo])
                             + [(LANE_SUM, 1.0)]).astype(BF16)
            ka_ref[h] = _aug(lane, _head_lanes(kn, h), _run(LANE_ONE, [-hi, -mid, -lo]) + _run(LANE_F, ones3)
                             + _run(LANE_LSE, ones3)).astype(BF16)
            va_ref[h] = _aug(lane, _head_lanes(vv, h), _run(LANE_ONE, ones3)).astype(BF16)

    blk = lambda col: pl.BlockSpec((tm, CONV_W), lambda i: (i, col))
    heads = pl.BlockSpec((N_HEADS, tm, LANES), lambda i: (0, i, 0))
    return pl.pallas_call(
        body, name="branch_prep", grid=(nt,),
        in_specs=[blk(0), blk(1), blk(2), blk(3), blk(4), blk(5), pl.BlockSpec((tm, F_PAD), lambda i: (i, 0)),
                  _prev_spec(tm, CONV_W, 1), _prev_spec(tm, CONV_W, 2),
                  _full((SUBLANES, CONV_W)), _full((1, ATTN_W)), _full((1, ATTN_W)), _full((ATTN_W, ATTN_W))],
        out_specs=[pl.BlockSpec((tm, CONV_W), lambda i: (i, 0)), heads, heads, heads],
        out_shape=[jax.ShapeDtypeStruct((T, CONV_W), BF16)] + [jax.ShapeDtypeStruct((N_HEADS, T, LANES), BF16)] * 3,
        compiler_params=_params(("parallel",)),
    )(proj, proj, proj, proj, proj, proj, fcum, proj, proj, conv_w8, qg, kg, gmat)


def _cumsum(x, *, reverse, name, col=0, gate_bias=None):
    T = x.shape[0]
    tm = min(TOKEN_TILE, T)
    nt = T // tm

    def body(x_ref, b_ref, o_ref, carry_ref):
        i = pl.program_id(0)

        @pl.when(i == 0)
        def _():
            carry_ref[...] = jnp.zeros_like(carry_ref)

        r = lax.broadcasted_iota(jnp.int32, (tm, tm), 0)
        c = lax.broadcasted_iota(jnp.int32, (tm, tm), 1)
        tri = jnp.where((c >= r) if reverse else (c <= r), 1.0, 0.0).astype(BF16)
        xv = x_ref[...]
        if gate_bias is not None:
            fx = xv + b_ref[...]
            xv = jnp.minimum(fx, 0.0) - jnp.log(1.0 + jnp.exp(-jnp.abs(fx)))
        out = _split_dot_left(tri, xv, 3) + carry_ref[0:1]
        o_ref[...] = out
        carry_ref[...] = jnp.broadcast_to(out[0:1] if reverse else out[tm - 1:tm], carry_ref.shape)

    rows = (lambda i: nt - 1 - i) if reverse else (lambda i: i)
    bias = jnp.zeros((1, F_PAD), F32) if gate_bias is None else gate_bias
    return pl.pallas_call(
        body, name=name, grid=(nt,),
        in_specs=[pl.BlockSpec((tm, F_PAD), lambda i: (rows(i), col)), _full((1, F_PAD))],
        out_specs=pl.BlockSpec((tm, F_PAD), lambda i: (rows(i), 0)),
        out_shape=jax.ShapeDtypeStruct((T, F_PAD), F32),
        scratch_shapes=[pltpu.VMEM((SUBLANES, F_PAD), F32)],
        compiler_params=_params(("arbitrary",)),
    )(x, bias)


def _split_dot_left(mat, x, parts):
    out = None
    rem = x
    for p in range(parts):
        piece = rem.astype(BF16)
        term = jnp.dot(mat, piece, preferred_element_type=F32)
        out = term if out is None else out + term
        if p + 1 < parts:
            rem = rem - piece.astype(F32)
    return out


def _out_resid_norm(x, merged, w_out, g1, g, sc, sh):
    T = x.shape[0]
    tm = min(TOKEN_TILE, T)

    def body(x_ref, m_ref, w_ref, g1_ref, g_ref, sc_ref, sh_ref, mix_ref, x1_ref, h_ref):
        mix = jnp.dot(m_ref[...], w_ref[...], preferred_element_type=F32)
        mix_ref[...] = mix
        x1 = x_ref[...] + g1_ref[...] * mix
        x1_ref[...] = x1
        inv = lax.rsqrt(jnp.mean(x1 * x1, axis=-1, keepdims=True) + RMS_EPS)
        h_ref[...] = ((x1 * inv) * g_ref[...] * (1.0 + sc_ref[...]) + sh_ref[...]).astype(BF16)

    row = pl.BlockSpec((tm, D), lambda i: (i, 0))
    vec = _full((1, D))
    return pl.pallas_call(
        body, name="out_resid_norm", grid=(T // tm,),
        in_specs=[row, row, _full((D, D)), vec, vec, vec, vec], out_specs=[row, row, row],
        out_shape=[jax.ShapeDtypeStruct((T, D), F32), jax.ShapeDtypeStruct((T, D), F32),
                   jax.ShapeDtypeStruct((T, D), BF16)],
        compiler_params=_params(("parallel",)),
    )(x, merged, w_out, g1, g, sc, sh)


FFN_TM = 256
FFN_TC = 1408


def _tanh_sigmoid(x):
    return 0.5 * jnp.tanh(0.5 * x) + 0.5


def _ffn_act_fwd(u, w8):
    T = u.shape[0]
    tm = min(FFN_TM, T)
    nt = T // tm
    nc = D_FF // FFN_TC

    def body(ug_ref, uv_ref, ugp_ref, uvp_ref, wg_ref, wv_ref, o_ref, cg_ref, cv_ref):
        i = pl.program_id(1)

        def conv(u_ref, p_ref, w_ref):
            uv = u_ref[...]
            up = jnp.where(i > 0, p_ref[...], 0.0)
            w = w_ref[...]
            return _shift_down(uv, up, 2) * w[0:1] + _shift_down(uv, up, 1) * w[1:2] + uv * w[2:3]

        gate = conv(ug_ref, ugp_ref, wg_ref)
        val = conv(uv_ref, uvp_ref, wv_ref)
        cg_ref[...] = gate
        cv_ref[...] = val
        o_ref[...] = (gate * _tanh_sigmoid(gate) * val).astype(BF16)

    per = tm // SUBLANES
    blk = lambda off: pl.BlockSpec((tm, FFN_TC), lambda j, i: (i, j + off))
    prev = lambda off: pl.BlockSpec((SUBLANES, FFN_TC), lambda j, i: (jnp.maximum(i * per - 1, 0), j + off))
    wblk = lambda off: pl.BlockSpec((SUBLANES, FFN_TC), lambda j, i: (0, j + off))
    return pl.pallas_call(
        body, name="ffn_act_fwd", grid=(nc, nt),
        in_specs=[blk(0), blk(nc), prev(0), prev(nc), wblk(0), wblk(nc)],
        out_specs=[blk(0), blk(0), blk(0)],
        out_shape=[jax.ShapeDtypeStruct((T, D_FF), BF16), jax.ShapeDtypeStruct((T, D_FF), F32),
                   jax.ShapeDtypeStruct((T, D_FF), F32)],
        compiler_params=_params(("parallel", "parallel")),
    )(u, u, u, u, w8, w8)


def _down_loss_head(x1, act, w_down, g2, target):
    T = x1.shape[0]
    tm = min(TOKEN_TILE, T)

    def body(x1_ref, a_ref, w_ref, g2_ref, t_ref, dy_ref, dff_ref, loss_ref, dg2_ref):
        i = pl.program_id(0)

        @pl.when(i == 0)
        def _():
            loss_ref[...] = jnp.zeros_like(loss_ref)
            dg2_ref[...] = jnp.zeros_like(dg2_ref)

        ff = jnp.dot(a_ref[...], w_ref[...], preferred_element_type=F32)
        err = x1_ref[...] + g2_ref[...] * ff - t_ref[...]
        dy = err * (1.0 / D)
        dy_ref[...] = dy
        dff_ref[...] = (dy * g2_ref[...]).astype(BF16)
        loss_ref[...] += _rows8(err * err)
        dg2_ref[...] += _rows8(dy * ff)

    row = pl.BlockSpec((tm, D), lambda i: (i, 0))
    acc = _full((SUBLANES, D))
    return pl.pallas_call(
        body, name="down_loss_head", grid=(T // tm,),
        in_specs=[row, pl.BlockSpec((tm, D_FF), lambda i: (i, 0)), _full((D_FF, D)), _full((1, D)), row],
        out_specs=[row, row, acc, acc],
        out_shape=[jax.ShapeDtypeStruct((T, D), F32), jax.ShapeDtypeStruct((T, D), BF16),
                   jax.ShapeDtypeStruct((SUBLANES, D), F32), jax.ShapeDtypeStruct((SUBLANES, D), F32)],
        compiler_params=_params(("arbitrary",)),
    )(x1, act, w_down, g2, target)


def _nt_dot(a, b):
    return lax.dot_general(a, b, (((1,), (1,)), ((), ())), preferred_element_type=F32)


def _causal(n, keys_on_rows=False):
    r = lax.broadcasted_iota(jnp.int32, (n, n), 0)
    c = lax.broadcasted_iota(jnp.int32, (n, n), 1)
    return (c >= r) if keys_on_rows else (c <= r)


def _sweep(lo, hi, step, carry, group=2):
    while group >= 1:
        def several(j, cr, lo=lo, group=group):
            for g in range(group):
                cr = step(lo + group * j + g, cr)
            return cr

        passes = (hi - lo) // group
        carry = lax.fori_loop(0, passes, several, carry)
        lo = lo + group * passes
        group //= 2
    return carry


def _grid_ends(n0, n1):
    i0, i1 = pl.program_id(0), pl.program_id(1)
    return jnp.logical_and(i0 == 0, i1 == 0), jnp.logical_and(i0 == n0 - 1, i1 == n1 - 1)


def _attn_fwd(qa, ka, va, exchange=None):
    nh, T, _ = qa.shape
    bq = min(ATTN_BLOCK, T)
    nq = T // bq

    def body(*refs):
        if exchange is None:
            q_ref, k_ref, v_ref, o_ref, qb_ref = refs
        else:
            (q_ref, k_ref, v_ref), (o_ref, qb_ref), _, xrefs = exchange.split(refs, 3, 2)
            _ride(exchange, *_grid_ends(nh, nq), xrefs)
        qi = pl.program_id(1)
        q = q_ref[0]

        def step(kb, carry, masked):
            m, acc = carry
            start = pl.multiple_of(kb * bq, bq)
            s = _nt_dot(q, k_ref[0, pl.ds(start, bq), :])
            if masked:
                s = jnp.where(_causal(bq), s, NEG_INF)
            m_new = jnp.maximum(m, jnp.max(s, axis=-1, keepdims=True))
            p = jnp.exp(s - m_new).astype(BF16)
            acc = jnp.exp(m - m_new) * acc + jnp.dot(p, v_ref[0, pl.ds(start, bq), :], preferred_element_type=F32)
            return m_new, acc

        init = (jnp.full((bq, 1), NEG_INF, F32), jnp.zeros((bq, LANES), F32))
        carry = _sweep(0, qi, lambda kb, cr: step(kb, cr, False), init, group=4)
        m, acc = step(qi, carry, True)
        l = acc[:, LANE_ONE:LANE_ONE + 1]
        o_ref[0] = acc / l
        lse = m + jnp.log(l)
        lane = lax.broadcasted_iota(jnp.int32, (bq, LANES), 1)
        qf = q.astype(F32)
        for idx, piece in _run(LANE_LSE, _pieces(lse)):
            qf = jnp.where(lane == idx, -piece, qf)
        qb_ref[0] = qf.astype(BF16)

    tile = pl.BlockSpec((1, bq, LANES), lambda h, i: (h, i, 0))
    whole = pl.BlockSpec((1, T, LANES), lambda h, i: (h, 0, 0))
    out_shape = [jax.ShapeDtypeStruct((nh, T, LANES), F32), jax.ShapeDtypeStruct((nh, T, LANES), BF16)]
    if exchange is None:
        return pl.pallas_call(
            body, name="attn_fwd", grid=(nh, nq), in_specs=[tile, whole, whole], out_specs=[tile, tile],
            out_shape=out_shape, compiler_params=_params(("parallel", "parallel")),
        )(qa, ka, va)
    return pl.pallas_call(
        body, name="attn_fwd", grid=(nh, nq), in_specs=[tile, whole, whole] + exchange.in_specs,
        out_specs=[tile, tile] + exchange.out_specs, out_shape=out_shape + exchange.out_shapes,
        scratch_shapes=exchange.scratch, compiler_params=_params(("arbitrary", "arbitrary")),
    )(qa, ka, va, *exchange.xs)


def _branch_merge_fwd(ya0, o_h, proj, wba, wbb_heads):
    nh, T, _ = o_h.shape
    tm = min(TOKEN_TILE, T)

    def body(ya0_ref, o_ref, ga_ref, gb_ref, wa_ref, wb_ref, ya_ref, yb_ref, m_ref):
        ya = jnp.dot(ya0_ref[...], wa_ref[...], preferred_element_type=F32)
        yb = jnp.dot(o_ref[0].astype(BF16), wb_ref[0:LANES, :], preferred_element_type=F32)
        for h in range(1, nh):
            yb += jnp.dot(o_ref[h].astype(BF16), wb_ref[h * LANES:(h + 1) * LANES, :], preferred_element_type=F32)
        ya_ref[...] = ya.astype(BF16)
        yb_ref[...] = yb.astype(BF16)
        m_ref[...] = (_sigmoid(ga_ref[...]) * ya + _sigmoid(gb_ref[...]) * yb).astype(BF16)

    row = pl.BlockSpec((tm, D), lambda i: (i, 0))
    return pl.pallas_call(
        body, name="branch_merge_fwd", grid=(T // tm,),
        in_specs=[pl.BlockSpec((tm, CONV_W), lambda i: (i, 0)), pl.BlockSpec((nh, tm, LANES), lambda i: (0, i, 0)),
                  pl.BlockSpec((tm, D), lambda i: (i, COL_GA // D)), pl.BlockSpec((tm, D), lambda i: (i, COL_GB // D)),
                  _full((CONV_W, D)), _full((nh * LANES, D))],
        out_specs=[row, row, row],
        out_shape=[jax.ShapeDtypeStruct((T, D), BF16)] * 3,
        compiler_params=_params(("parallel",)),
    )(ya0, o_h, proj, proj, wba, wbb_heads)


def _branch_b_bwd(dyb, o_h, wbb_heads_t):
    nh, T, _ = o_h.shape
    tm = min(TOKEN_TILE, T)

    def body(dyb_ref, o_ref, w_ref, out_ref):
        do = jnp.dot(dyb_ref[...], w_ref[...], preferred_element_type=F32)
        lane = lax.broadcasted_iota(jnp.int32, (tm, LANES), 1)
        for h in range(nh):
            g = do[:, h * LANES:(h + 1) * LANES].astype(BF16).astype(F32)
            delta = jnp.sum(g * o_ref[h], axis=-1, keepdims=True)
            for idx, piece in _run(LANE_ONE, _pieces(delta)):
                g = jnp.where(lane == idx, -piece, g)
            out_ref[h] = g.astype(BF16)

    heads = pl.BlockSpec((nh, tm, LANES), lambda i: (0, i, 0))
    return pl.pallas_call(
        body, name="branch_b_bwd", grid=(T // tm,),
        in_specs=[pl.BlockSpec((tm, D), lambda i: (i, 0)), heads, _full((D, nh * LANES))],
        out_specs=heads, out_shape=jax.ShapeDtypeStruct((nh, T, LANES), BF16),
        compiler_params=_params(("parallel",)),
    )(dyb, o_h, wbb_heads_t)


def _branch_b_dw(o_h, dyb):
    nh, T, _ = o_h.shape
    tk = min(TOKEN_TILE, T)

    def body(o_ref, dyb_ref, out_ref):
        @pl.when(pl.program_id(0) == 0)
        def _():
            out_ref[...] = jnp.zeros_like(out_ref)

        g = dyb_ref[...]
        for h in range(nh):
            out_ref[h] += _tn_dot(o_ref[h].astype(BF16), g)

    return pl.pallas_call(
        body, name="branch_b_dw", grid=(T // tk,),
        in_specs=[pl.BlockSpec((nh, tk, LANES), lambda k: (0, k, 0)), pl.BlockSpec((tk, D), lambda k: (k, 0))],
        out_specs=_full((nh, LANES, D)), out_shape=jax.ShapeDtypeStruct((nh, LANES, D), F32),
        compiler_params=_params(("arbitrary",)),
    )(o_h, dyb)


def _attn_bwd(qb, ka, va, doa, exchange=None):
    nh, T, _ = qb.shape
    bk = min(ATTN_BLOCK, T)
    nk = T // bk

    def body(*refs):
        if exchange is None:
            q_ref, do_ref, k_ref, v_ref, dq_ref, dk_ref, dv_ref = refs
        else:
            (q_ref, do_ref, k_ref, v_ref), (dq_ref, dk_ref, dv_ref), _, xrefs = exchange.split(refs, 4, 3)
            _ride(exchange, *_grid_ends(nh, nk), xrefs)
        ki = pl.program_id(1)

        @pl.when(ki == 0)
        def _():
            dq_ref[...] = jnp.zeros_like(dq_ref)

        k = k_ref[0]
        v = v_ref[0]

        def step(qi, carry, masked):
            dk, dv = carry
            rows = pl.ds(pl.multiple_of(qi * bk, bk), bk)
            q = q_ref[0, rows, :]
            g = do_ref[0, rows, :]
            pt = jnp.exp(_nt_dot(k, q))
            if masked:
                pt = jnp.where(_causal(bk, keys_on_rows=True), pt, 0.0)
            dv = dv + jnp.dot(pt.astype(BF16), g, preferred_element_type=F32)
            dst = (pt * _nt_dot(v, g)).astype(BF16)
            dk = dk + jnp.dot(dst, q, preferred_element_type=F32)
            dq_ref[0, rows, :] += _tn_dot(dst, k)
            return dk, dv

        init = (jnp.zeros((bk, LANES), F32), jnp.zeros((bk, LANES), F32))
        carry = step(ki, init, True)
        dk_ref[0], dv_ref[0] = _sweep(ki + 1, nk, lambda qi, cr: step(qi, cr, False), carry)

    tile = pl.BlockSpec((1, bk, LANES), lambda h, i: (h, i, 0))
    whole = pl.BlockSpec((1, T, LANES), lambda h, i: (h, 0, 0))
    out_shape = [jax.ShapeDtypeStruct((nh, T, LANES), F32)] * 3
    if exchange is None:
        return pl.pallas_call(
            body, name="attn_bwd", grid=(nh, nk), in_specs=[whole, whole, tile, tile],
            out_specs=[whole, tile, tile], out_shape=out_shape, compiler_params=_params(("parallel", "arbitrary")),
        )(qb, doa, ka, va)
    return pl.pallas_call(
        body, name="attn_bwd", grid=(nh, nk), in_specs=[whole, whole, tile, tile] + exchange.in_specs,
        out_specs=[whole, tile, tile] + exchange.out_specs, out_shape=out_shape + exchange.out_shapes,
        scratch_shapes=exchange.scratch, compiler_params=_params(("arbitrary", "arbitrary")),
    )(qb, doa, ka, va, *exchange.xs)


def _attn_unpack(dq_h, dk_h, dv_h):
    nh, T, _ = dq_h.shape
    tm = min(TOKEN_TILE, T)

    def body(dq_ref, dk_ref, dv_ref, q_out, k_out, v_out, f_out):
        lane = lax.broadcasted_iota(jnp.int32, (tm, LANES), 1)
        low = lane < HEAD_DIM
        for src, dst in ((dq_ref, q_out), (dk_ref, k_out), (dv_ref, v_out)):
            for pair in range(nh // 2):
                both = jnp.where(low, src[2 * pair], pltpu.roll(src[2 * pair + 1], HEAD_DIM, axis=1))
                dst[:, LANES * pair:LANES * (pair + 1)] = both.astype(dst.dtype)
        df = jnp.zeros((tm, LANES), F32)
        for h in range(nh):
            col = dq_ref[h][:, LANE_F:LANE_F + 1] - dk_ref[h][:, LANE_SUM:LANE_SUM + 1]
            df = jnp.where(lane == h, col, df)
        f_out[...] = df

    heads = pl.BlockSpec((nh, tm, LANES), lambda i: (0, i, 0))
    tok = pl.BlockSpec((tm, ATTN_W), lambda i: (i, 0))
    return pl.pallas_call(
        body, name="attn_unpack", grid=(T // tm,), in_specs=[heads, heads, heads],
        out_specs=[tok, tok, tok, pl.BlockSpec((tm, F_PAD), lambda i: (i, 0))],
        out_shape=[jax.ShapeDtypeStruct((T, ATTN_W), F32), jax.ShapeDtypeStruct((T, ATTN_W), F32),
                   jax.ShapeDtypeStruct((T, ATTN_W), BF16), jax.ShapeDtypeStruct((T, F_PAD), F32)],
        compiler_params=_params(("parallel",)),
    )(dq_h, dk_h, dv_h)


def _ffn_act_bwd(u, cg, cv, da, w8):
    T = u.shape[0]
    tm = min(FFN_TM, T)
    nt = T // tm
    nc = D_FF // FFN_TC

    def body(ug_ref, uv_ref, cg_ref, cv_ref, cgn_ref, cvn_ref, da_ref, dan_ref, wg_ref, wv_ref,
             dug_ref, duv_ref, dwg_ref, dwv_ref):
        i = pl.program_id(1)

        @pl.when(i == 0)
        def _():
            dwg_ref[...] = jnp.zeros_like(dwg_ref)
            dwv_ref[...] = jnp.zeros_like(dwv_ref)

        gate = jnp.concatenate([cg_ref[...], cgn_ref[...]], axis=0)
        val = jnp.concatenate([cv_ref[...], cvn_ref[...]], axis=0)
        dae = jnp.concatenate([da_ref[...], jnp.where(i == nt - 1, 0.0, dan_ref[...])], axis=0)
        sg = _tanh_sigmoid(gate)
        n = tm + SUBLANES

        def back(d, u_ref, w_ref, du_ref, dw_ref):
            w = w_ref[...]
            uv = u_ref[...]
            d1 = pltpu.roll(d, n - 1, axis=0)[:tm]
            d2 = pltpu.roll(d, n - 2, axis=0)[:tm]
            d0 = d[:tm]
            du_ref[...] = (d0 * w[2:3] + d1 * w[1:2] + d2 * w[0:1]).astype(BF16)
            rows = [jnp.sum(t * uv, axis=0, keepdims=True) for t in (d2, d1, d0)]
            dw_ref[...] += jnp.concatenate(rows + [jnp.zeros((SUBLANES - 3, FFN_TC), F32)], axis=0)

        back(dae * val * sg * (1.0 + gate * (1.0 - sg)), ug_ref, wg_ref, dug_ref, dwg_ref)
        back(dae * gate * sg, uv_ref, wv_ref, duv_ref, dwv_ref)

    per = tm // SUBLANES
    last_blk = nt * per - 1
    blk = lambda off: pl.BlockSpec((tm, FFN_TC), lambda j, i: (i, j + off))
    nxt = pl.BlockSpec((SUBLANES, FFN_TC), lambda j, i: (jnp.minimum((i + 1) * per, last_blk), j))
    wblk = lambda off: pl.BlockSpec((SUBLANES, FFN_TC), lambda j, i: (0, j + off))
    dug, duv, dwg, dwv = pl.pallas_call(
        body, name="ffn_act_bwd", grid=(nc, nt),
        in_specs=[blk(0), blk(nc), blk(0), blk(0), nxt, nxt, blk(0), nxt, wblk(0), wblk(nc)],
        out_specs=[blk(0), blk(0), wblk(0), wblk(0)],
        out_shape=[jax.ShapeDtypeStruct((T, D_FF), BF16)] * 2 + [jax.ShapeDtypeStruct((SUBLANES, D_FF), F32)] * 2,
        compiler_params=_params(("parallel", "arbitrary")),
    )(u, u, cg, cv, cg, cv, da, da, w8, w8)
    return dug, duv, jnp.concatenate([dwg, dwv], axis=1)


def _norm_bwd(xin, dh, dres, g, sc, *, name):
    T = xin.shape[0]
    tm = min(TOKEN_TILE, T)

    def body(x_ref, dh_ref, dr_ref, g_ref, sc_ref, dx_ref, dsh_ref, dsc_ref, dg_ref):
        i = pl.program_id(0)

        @pl.when(i == 0)
        def _():
            dsh_ref[...] = jnp.zeros_like(dsh_ref)
            dsc_ref[...] = jnp.zeros_like(dsc_ref)
            dg_ref[...] = jnp.zeros_like(dg_ref)

        xv = x_ref[...]
        dh = dh_ref[...]
        gv = g_ref[...]
        one_sc = 1.0 + sc_ref[...]
        inv = lax.rsqrt(jnp.mean(xv * xv, axis=-1, keepdims=True) + RMS_EPS)
        xn = xv * inv
        dxn = dh * (gv * one_sc)
        dx_ref[...] = dr_ref[...] + inv * (dxn - xn * jnp.mean(dxn * xn, axis=-1, keepdims=True))
        dhxn = dh * xn
        dsh_ref[...] += _rows8(dh)
        dsc_ref[...] += _rows8(dhxn * gv)
        dg_ref[...] += _rows8(dhxn * one_sc)

    row = pl.BlockSpec((tm, D), lambda i: (i, 0))
    acc = _full((SUBLANES, D))
    return pl.pallas_call(
        body, name=name, grid=(T // tm,),
        in_specs=[row, row, row, _full((1, D)), _full((1, D))], out_specs=[row, acc, acc, acc],
        out_shape=[jax.ShapeDtypeStruct((T, D), F32)] + [jax.ShapeDtypeStruct((SUBLANES, D), F32)] * 3,
        compiler_params=_params(("arbitrary",)),
    )(xin, dh, dres, g, sc)


def _norm2_gate_bwd(x1, dh2, dy, mix, g1, g, sc):
    T = x1.shape[0]
    tm = min(TOKEN_TILE, T)

    def body(x_ref, dh_ref, dr_ref, mix_ref, g1_ref, g_ref, sc_ref, dx_ref, dmix_ref, dsh_ref, dsc_ref, dg_ref,
             dg1_ref):
        @pl.when(pl.program_id(0) == 0)
        def _():
            for ref in (dsh_ref, dsc_ref, dg_ref, dg1_ref):
                ref[...] = jnp.zeros_like(ref)

        xv = x_ref[...]
        dh = dh_ref[...]
        gv = g_ref[...]
        one_sc = 1.0 + sc_ref[...]
        inv = lax.rsqrt(jnp.mean(xv * xv, axis=-1, keepdims=True) + RMS_EPS)
        xn = xv * inv
        dxn = dh * (gv * one_sc)
        dx = dr_ref[...] + inv * (dxn - xn * jnp.mean(dxn * xn, axis=-1, keepdims=True))
        dx_ref[...] = dx
        dmix_ref[...] = (dx * g1_ref[...]).astype(BF16)
        dhxn = dh * xn
        dsh_ref[...] += _rows8(dh)
        dsc_ref[...] += _rows8(dhxn * gv)
        dg_ref[...] += _rows8(dhxn * one_sc)
        dg1_ref[...] += _rows8(dx * mix_ref[...])

    row = pl.BlockSpec((tm, D), lambda i: (i, 0))
    vec, acc = _full((1, D)), _full((SUBLANES, D))
    return pl.pallas_call(
        body, name="norm2_gate_bwd", grid=(T // tm,),
        in_specs=[row, row, row, row, vec, vec, vec], out_specs=[row, row, acc, acc, acc, acc],
        out_shape=[jax.ShapeDtypeStruct((T, D), F32), jax.ShapeDtypeStruct((T, D), BF16)]
        + [jax.ShapeDtypeStruct((SUBLANES, D), F32)] * 4,
        compiler_params=_params(("arbitrary",)),
    )(x1, dh2, dy, mix, g1, g, sc)


def _out_merge_bwd(dmix, w_out_t, ya, yb, proj):
    T = ya.shape[0]
    tm = min(TOKEN_TILE, T)

    def body(dmix_ref, w_ref, ya_ref, yb_ref, ga_ref, gb_ref, dya_ref, dyb_ref, dga_ref, dgb_ref):
        dm = jnp.dot(dmix_ref[...], w_ref[...], preferred_element_type=F32)
        sa = _sigmoid(ga_ref[...])
        sb = _sigmoid(gb_ref[...])
        dya_ref[...] = (dm * sa).astype(BF16)
        dyb_ref[...] = (dm * sb).astype(BF16)
        dga_ref[...] = (dm * ya_ref[...].astype(F32) * sa * (1.0 - sa)).astype(BF16)
        dgb_ref[...] = (dm * yb_ref[...].astype(F32) * sb * (1.0 - sb)).astype(BF16)

    row = pl.BlockSpec((tm, D), lambda i: (i, 0))
    return pl.pallas_call(
        body, name="out_merge_bwd", grid=(T // tm,),
        in_specs=[row, _full((D, D)), row, row, pl.BlockSpec((tm, D), lambda i: (i, COL_GA // D)),
                  pl.BlockSpec((tm, D), lambda i: (i, COL_GB // D))],
        out_specs=[row] * 4, out_shape=[jax.ShapeDtypeStruct((T, D), BF16)] * 4,
        compiler_params=_params(("parallel",)),
    )(dmix, w_out_t, ya, yb, proj, proj)


def _conv_branch_bwd(proj, dya0, conv_w8):
    T = proj.shape[0]
    tm = min(FFN_TM, T)
    nt = T // tm

    def body(cb_ref, cc_ref, cv_ref, cbn_ref, ccp_ref, cvp_ref, ccn_ref, cvn_ref, d_ref, dn_ref, w_ref,
             dcb_ref, dcc_ref, dcv_ref, dw_ref):
        i = pl.program_id(0)

        @pl.when(i == 0)
        def _():
            dw_ref[...] = jnp.zeros_like(dw_ref)

        first, last = i == 0, i == nt - 1
        w = w_ref[...]
        cc = jnp.concatenate([ccp_ref[...], cc_ref[...], ccn_ref[...]], axis=0)
        cv = jnp.concatenate([cvp_ref[...], cv_ref[...], cvn_ref[...]], axis=0)
        rows = lax.broadcasted_iota(jnp.int32, cc.shape, 0)
        z = jnp.where(jnp.logical_and(first, rows < SUBLANES), 0.0, cc * cv)
        z1 = pltpu.roll(z, 1, axis=0)
        z2 = pltpu.roll(z, 2, axis=0)
        cz = z2 * w[0:1] + z1 * w[1:2] + z * w[2:3]
        zeros8 = jnp.zeros((SUBLANES, CONV_W), F32)
        de = jnp.concatenate([zeros8, d_ref[...], jnp.where(last, 0.0, dn_ref[...])], axis=0)
        cbe = jnp.concatenate([zeros8, cb_ref[...], cbn_ref[...]], axis=0)
        dcz = de * cbe
        n = tm + 2 * SUBLANES
        dz = dcz * w[2:3] + pltpu.roll(dcz, n - 1, axis=0) * w[1:2] + pltpu.roll(dcz, n - 2, axis=0) * w[0:1]
        inner = slice(SUBLANES, SUBLANES + tm)
        dcb_ref[...] = (de * cz)[inner].astype(BF16)
        dcc_ref[...] = (dz * cv)[inner].astype(BF16)
        dcv_ref[...] = (dz * cc)[inner].astype(BF16)
        wrows = [jnp.sum((dcz * t)[inner], axis=0, keepdims=True) for t in (z2, z1, z)]
        dw_ref[...] += jnp.concatenate(wrows + [jnp.zeros((SUBLANES - 3, CONV_W), F32)], axis=0)

    blk = lambda col: pl.BlockSpec((tm, CONV_W), lambda i: (i, col))
    out_blk = pl.BlockSpec((tm, CONV_W), lambda i: (i, 0))
    return pl.pallas_call(
        body, name="conv_branch_bwd", grid=(nt,),
        in_specs=[blk(0), blk(1), blk(2), _next_spec(tm, CONV_W, 0, nt),
                  _prev_spec(tm, CONV_W, 1), _prev_spec(tm, CONV_W, 2),
                  _next_spec(tm, CONV_W, 1, nt), _next_spec(tm, CONV_W, 2, nt),
                  out_blk, _next_spec(tm, CONV_W, 0, nt), _full((SUBLANES, CONV_W))],
        out_specs=[out_blk, out_blk, out_blk, _full((SUBLANES, CONV_W))],
        out_shape=[jax.ShapeDtypeStruct((T, CONV_W), BF16)] * 3 + [jax.ShapeDtypeStruct((SUBLANES, CONV_W), F32)],
        compiler_params=_params(("arbitrary",)),
    )(proj, proj, proj, proj, proj, proj, proj, proj, dya0, dya0, conv_w8)


def _qk_norm_bwd(proj, dqs, dkh, dlogf, qg, kg, bf_pad, gmat):
    T = proj.shape[0]
    tm = min(TOKEN_TILE, T)

    def body(q_ref, k_ref, f_ref, dqs_ref, dkh_ref, dlf_ref, qg_ref, kg_ref, bf_ref, g_ref,
             dq_ref, dk_ref, dfl_ref, dqg_ref, dkg_ref, dbf_ref):
        @pl.when(pl.program_id(0) == 0)
        def _():
            dqg_ref[...] = jnp.zeros_like(dqg_ref)
            dkg_ref[...] = jnp.zeros_like(dkg_ref)
            dbf_ref[...] = jnp.zeros_like(dbf_ref)

        gm = g_ref[...]
        for src, d_src, gain, scale, dst, dgain in (
                (q_ref, dqs_ref, qg_ref, 1.0 / np.sqrt(HEAD_DIM), dq_ref, dqg_ref),
                (k_ref, dkh_ref, kg_ref, 1.0, dk_ref, dkg_ref)):
            v = src[...]
            dhat = d_src[...] * scale
            inv = lax.rsqrt(_split_dot(v * v, gm, 2) * (1.0 / HEAD_DIM) + RMS_EPS)
            vn = v * inv
            dgain[...] += _rows8(dhat * vn)
            dvn = dhat * gain[...]
            mean = _split_dot(dvn * vn, gm, 2) * (1.0 / HEAD_DIM)
            dst[...] = (inv * (dvn - vn * mean)).astype(BF16)
        fx = f_ref[...] + bf_ref[...]
        dfl = dlf_ref[...] * _sigmoid(-fx)
        dfl_ref[...] = dfl.astype(BF16)
        dbf_ref[...] += _rows8(dfl)

    blk = lambda col: pl.BlockSpec((tm, ATTN_W), lambda i: (i, col))
    out_blk = pl.BlockSpec((tm, ATTN_W), lambda i: (i, 0))
    f_in = pl.BlockSpec((tm, F_PAD), lambda i: (i, COL_F // F_PAD))
    f_blk = pl.BlockSpec((tm, F_PAD), lambda i: (i, 0))
    return pl.pallas_call(
        body, name="qk_norm_bwd", grid=(T // tm,),
        in_specs=[blk(3), blk(4), f_in, out_blk, out_blk, f_blk, _full((1, ATTN_W)), _full((1, ATTN_W)),
                  _full((1, F_PAD)), _full((ATTN_W, ATTN_W))],
        out_specs=[out_blk, out_blk, f_blk, _full((SUBLANES, ATTN_W)), _full((SUBLANES, ATTN_W)),
                   _full((SUBLANES, F_PAD))],
        out_shape=[jax.ShapeDtypeStruct((T, ATTN_W), BF16)] * 2 + [jax.ShapeDtypeStruct((T, F_PAD), BF16)]
        + [jax.ShapeDtypeStruct((SUBLANES, ATTN_W), F32)] * 2 + [jax.ShapeDtypeStruct((SUBLANES, F_PAD), F32)],
        compiler_params=_params(("arbitrary",)),
    )(proj, proj, proj, dqs, dkh, dlogf, qg, kg, bf_pad, gmat)


def _pad_rows8(w):
    return jnp.pad(w, ((0, SUBLANES - w.shape[0]), (0, 0)))


def _fold8(acc):
    return jnp.sum(acc, axis=0, keepdims=True)


def _late_weights(mats):
    out = {}
    for name in ("w_branch_a", "w_out", "w_up", "w_down"):
        out[name] = mats[name]
        out[name + "_t"] = mats[name].T
    out["w_branch_b_heads"] = _pad_head_rows(mats["w_branch_b"])
    out["w_branch_b_heads_t"] = out["w_branch_b_heads"].T
    return out


def _local_step(x, target, mod, wts, late=None):
    T = x.shape[0]
    tb = min(MATMUL_TILE, T)
    tm = min(TOKEN_TILE, T)
    sh1, sc1, g1, sh2, sc2, g2 = [mod[:, i * D:(i + 1) * D] for i in range(N_MOD)]
    w_in, w_in_t = wts["w_in"], wts["w_in_t"]
    conv_a8 = _pad_rows8(wts["conv_a_w"])
    conv_f8 = _pad_rows8(wts["conv_ffn_w"])
    qg = jnp.tile(wts["q_norm_g"], (1, N_HEADS))
    kg = jnp.tile(wts["k_norm_g"], (1, N_HEADS))
    bf_pad = jnp.pad(wts["b_f"], ((0, 0), (0, F_PAD - N_HEADS)))
    gmat = _group_matrix()

    h = _norm_mod(x, wts["norm1_g"], sc1, sh1, name="norm1_fwd")
    proj = _matmul(h, w_in, name="mm_in", tm=tb, tn=896, tk=D)
    fcum = _cumsum(proj, reverse=False, name="gate_cumsum", col=COL_F // F_PAD, gate_bias=bf_pad)
    ya0, qa, ka, va = _branch_prep(proj, fcum, conv_a8, qg, kg, gmat)
    if late is None:
        o_h, qb = _attn_fwd(qa, ka, va)
    else:
        o_h, qb, *gathered = _attn_fwd(qa, ka, va, _Exchange([late[name] for name, *_ in LATE], scatter=False))
        wts = dict(wts)
        mats = {name: _join_shards(g, axis) for (name, _, _, axis), g in zip(LATE, gathered) if name != "w_up"}
        mats["w_up"] = _assemble_columns(gathered[[name for name, *_ in LATE].index("w_up")], 2 * D_FF // N_DEV,
                                         2 * D_FF, ((0, 2 * D_FF, 0),), name="assemble_w_up")
        wts.update(_late_weights(mats))
    ya, yb, merged = _branch_merge_fwd(ya0, o_h, proj, wts["w_branch_a"], wts["w_branch_b_heads"])
    mix, x1, h2 = _out_resid_norm(x, merged, wts["w_out"], g1, wts["norm2_g"], sc2, sh2)
    u = _matmul(h2, wts["w_up"], name="mm_up", tm=tb, tn=1408, tk=D)
    act, conv_gate, conv_val = _ffn_act_fwd(u, conv_f8)
    dy, dff, sq8, dg2_8 = _down_loss_head(x1, act, wts["w_down"], g2, target)
    sq = jnp.sum(sq8).reshape(1, 1)

    grads = {}
    da = _matmul(dff, wts["w_down_t"], name="mm_down_dx", tm=tb, tn=1408, tk=D)
    grads["w_down"] = _matmul(act, dff, name="mm_down_dw", tm=1408, tn=D, tk=tb, trans_a=True)
    dug, duv, dconv_f8 = _ffn_act_bwd(u, conv_gate, conv_val, da, conv_f8)
    grads["conv_ffn_w"] = dconv_f8[:3]
    dh2 = _matmul_pieces([dug, duv], wts["w_up_t"], name="mm_up_dx", tm=tm)
    dw_up = [_matmul(h2, d, name="mm_up_dw_" + half, tm=D, tn=1408, tk=tb, trans_a=True)
             for half, d in (("gate", dug), ("val", duv))]
    if late is None:
        grads["w_up"] = jnp.concatenate(dw_up, axis=1)
    dx1, dmix, dsh2_8, dsc2_8, dn2_8, dg1_8 = _norm2_gate_bwd(x1, dh2, dy, mix, g1, wts["norm2_g"], sc2)
    grads["norm2_g"] = _fold8(dn2_8)

    grads["w_out"] = _matmul(merged, dmix, name="mm_out_dw", tm=D, tn=D, tk=tb, trans_a=True)
    dya, dyb, dga, dgb = _out_merge_bwd(dmix, wts["w_out_t"], ya, yb, proj)
    dya0 = _matmul(dya, wts["w_branch_a_t"], name="mm_branch_a_dx", tm=tb, tn=CONV_W, tk=D)
    grads["w_branch_a"] = _matmul(ya0, dya, name="mm_branch_a_dw", tm=CONV_W, tn=D, tk=tb, trans_a=True)
    doa = _branch_b_bwd(dyb, o_h, wts["w_branch_b_heads_t"])
    grads["w_branch_b"] = _branch_b_dw(o_h, dyb)[:, :HEAD_DIM].reshape(ATTN_W, D)
    dcb, dcc, dcv, dconv_a8 = _conv_branch_bwd(proj, dya0, conv_a8)
    grads["conv_a_w"] = dconv_a8[:3]

    parts = {}
    if late is None:
        dq_h, dk_h, dv_h = _attn_bwd(qb, ka, va, doa)
    else:
        ready = [(_column_shards(dw_up) if name == "w_up" else _split_shards(grads[name], axis)).astype(BF16)
                 for name, _, _, axis in LATE]
        dq_h, dk_h, dv_h, *recv = _attn_bwd(
            qb, ka, va, doa, _Exchange(ready + [_pack_full_by_dest(grads, CONVS, SUBLANES)], scatter=True))
        parts = dict(zip([name for name, *_ in LATE] + ["conv"], recv))
    dq_tok, dk_tok, dv_tok, dfcum = _attn_unpack(dq_h, dk_h, dv_h)
    dlogf = _cumsum(dfcum, reverse=True, name="gate_cumsum_bwd")
    dq, dk, dfl, dqg8, dkg8, dbf8 = _qk_norm_bwd(proj, dq_tok, dk_tok, dlogf, qg, kg, bf_pad, gmat)
    grads["q_norm_g"] = jnp.sum(_fold8(dqg8).reshape(N_HEADS, HEAD_DIM), axis=0, keepdims=True)
    grads["k_norm_g"] = jnp.sum(_fold8(dkg8).reshape(N_HEADS, HEAD_DIM), axis=0, keepdims=True)
    grads["b_f"] = _fold8(dbf8)[:, :N_HEADS]
    narrow, wide = [dcb, dcc, dcv, dq, dk, dv_tok], [dga, dgb, dfl]
    dw_narrow = _matmul_tn_pieces(h, narrow, name="mm_in_dw_narrow", tk=tm)
    dwa, dwb, dwf = _matmul_tn_pieces(h, wide, name="mm_in_dw_wide", tk=tm)
    dw_in = list(dw_narrow) + [dwf[:, :N_HEADS], dwa, dwb]
    if late is None:
        grads["w_in"] = jnp.concatenate(dw_in, axis=1)
        dh = _matmul_pieces(narrow + wide, w_in_t, name="mm_in_dx", tm=tm)
    else:
        dh, parts["w_in"] = _matmul_pieces(
            narrow + wide, w_in_t, name="mm_in_dx", tm=tm,
            exchange=_Exchange([_column_shards(dw_in).astype(BF16)], scatter=True))
    grad_x, dsh1_8, dsc1_8, dn1_8 = _norm_bwd(x, dh, dx1, wts["norm1_g"], sc1, name="norm1_bwd")
    grads["norm1_g"] = _fold8(dn1_8)
    grads["mod"] = jnp.concatenate([_fold8(a) for a in (dsh1_8, dsc1_8, dg1_8, dsh2_8, dsc2_8, dg2_8)], axis=1)
    return sq, grad_x, grads, parts


def _me_and_peers():
    mx, my, mc = lax.axis_index("x"), lax.axis_index("y"), lax.axis_index("c")
    me = 4 * mx + 2 * my + mc
    peers = []
    for k in range(1, N_DEV):
        px = 1 - mx if k & 4 else mx
        py = 1 - my if k & 2 else my
        pc = 1 - mc if k & 1 else mc
        peers.append(((px, py, pc), 4 * px + 2 * py + pc))
    return me, peers


HBM_SPEC = pl.BlockSpec(memory_space=pltpu.HBM)


class _Exchange:
    def __init__(self, xs, scatter):
        self.xs, self.scatter, self.n = list(xs), scatter, len(xs)
        self.out_shapes = [jax.ShapeDtypeStruct(x.shape if scatter else (N_DEV,) + x.shape, x.dtype) for x in xs]
        self.in_specs = [HBM_SPEC] * self.n
        self.out_specs = [HBM_SPEC] * self.n
        self.scratch = [pltpu.SemaphoreType.DMA((self.n, N_DEV - 1)), pltpu.SemaphoreType.DMA((self.n, N_DEV - 1)),
                        pltpu.SemaphoreType.DMA((self.n,))]

    def _copies(self, x_refs, out_refs, sems):
        send_sems, recv_sems, local_sems = sems
        me, peers = _me_and_peers()

        def src(a, idx):
            return x_refs[a].at[idx] if self.scatter else x_refs[a]

        def copy(a, k, from_idx, to_slot, device):
            return pltpu.make_async_remote_copy(
                src_ref=src(a, from_idx), dst_ref=out_refs[a].at[to_slot], send_sem=send_sems.at[a, k],
                recv_sem=recv_sems.at[a, k], device_id=device, device_id_type=MESH)

        local = [pltpu.make_async_copy(src(a, me), out_refs[a].at[me], local_sems.at[a]) for a in range(self.n)]
        sends = [copy(a, k, idx, me, dev) for a in range(self.n) for k, (dev, idx) in enumerate(peers)]
        recvs = [copy(a, k, idx, idx, dev) for a in range(self.n) for k, (dev, idx) in enumerate(peers)]
        return local, sends, recvs

    def start(self, x_refs, out_refs, sems):
        local, sends, _ = self._copies(x_refs, out_refs, sems)
        for cp in local + sends:
            cp.start()

    def wait(self, x_refs, out_refs, sems):
        local, sends, recvs = self._copies(x_refs, out_refs, sems)
        for cp in recvs:
            cp.wait_recv()
        for cp in sends:
            cp.wait_send()
        for cp in local:
            cp.wait()

    def split(self, refs, n_in, n_out):
        n = self.n
        ins, xin = refs[:n_in], refs[n_in:n_in + n]
        outs, xout = refs[n_in + n:n_in + n + n_out], refs[n_in + n + n_out:n_in + 2 * n + n_out]
        rest = refs[n_in + 2 * n + n_out:]
        return ins, outs, rest[:len(rest) - 3], (xin, xout, rest[len(rest) - 3:])


def _ride(exchange, first, last, refs):
    if exchange is None:
        return

    @pl.when(first)
    def _():
        exchange.start(*refs)

    @pl.when(last)
    def _():
        exchange.wait(*refs)


def _gather_two_level(xs, *, name):
    n = len(xs)
    out_shapes = [jax.ShapeDtypeStruct((N_DEV,) + x.shape, x.dtype) for x in xs]

    def body(*refs):
        x_refs, out_refs = refs[:n], refs[n:2 * n]
        send_sems, recv_sems, local_sems = refs[2 * n:]
        x, y, c = lax.axis_index("x"), lax.axis_index("y"), lax.axis_index("c")
        me, sibling = (x, y, c), (x, y, 1 - c)
        chips = [(1 - x, y), (x, 1 - y), (1 - x, 1 - y)]

        def slot(a, dev):
            return out_refs[a].at[4 * dev[0] + 2 * dev[1] + dev[2]]

        def copy(a, k, block, to, src=None):
            return pltpu.make_async_remote_copy(
                src_ref=slot(a, block) if src is None else src, dst_ref=slot(a, block),
                send_sem=send_sems.at[a, k], recv_sem=recv_sems.at[a, k], device_id=to, device_id_type=MESH)

        mine = [pltpu.make_async_copy(x_refs[a], slot(a, me), local_sems.at[a]) for a in range(n)]
        first = [copy(a, 0, me, sibling, src=x_refs[a]) for a in range(n)]
        first += [copy(a, 1 + j, me, (*chip, c), src=x_refs[a]) for a in range(n) for j, chip in enumerate(chips)]
        for cp in mine + first:
            cp.start()
        passed = []
        for a in range(n):
            for j, chip in enumerate(chips):
                copy(a, 1 + j, (*chip, c), me).wait_recv()
                passed.append(copy(a, 4 + j, (*chip, c), sibling))
                passed[-1].start()
        for a in range(n):
            copy(a, 0, sibling, me).wait_recv()
            for j, chip in enumerate(chips):
                copy(a, 4 + j, (*chip, 1 - c), me).wait_recv()
        for cp in first + passed:
            cp.wait_send()
        for cp in mine:
            cp.wait()

    return pl.pallas_call(
        body, name=name, in_specs=[HBM_SPEC] * n, out_specs=[HBM_SPEC] * n, out_shape=out_shapes,
        scratch_shapes=[pltpu.SemaphoreType.DMA((n, N_DEV - 1)), pltpu.SemaphoreType.DMA((n, N_DEV - 1)),
                        pltpu.SemaphoreType.DMA((n,))],
        compiler_params=pltpu.CompilerParams(has_side_effects=True),
    )(*xs)


def _exchange(xs, *, name, scatter):
    ex = _Exchange(xs, scatter)

    def body(*refs):
        _, _, _, xrefs = ex.split(refs, 0, 0)
        ex.start(*xrefs)
        ex.wait(*xrefs)

    return pl.pallas_call(
        body, name=name, in_specs=ex.in_specs, out_specs=ex.out_specs, out_shape=ex.out_shapes,
        scratch_shapes=ex.scratch, compiler_params=pltpu.CompilerParams(has_side_effects=True),
    )(*xs)


def _ada_fwd(c_all, w_shard, b_shard):
    n = w_shard.shape[1]

    def body(c_ref, w_ref, b_ref, o_ref):
        cv = c_ref[...]
        act = (cv * _sigmoid(cv)).astype(BF16)
        o_ref[...] = jnp.dot(act, w_ref[...].astype(BF16), preferred_element_type=F32) + b_ref[...]

    return pl.pallas_call(
        body, name="ada_fwd", in_specs=[_full((N_DEV, D)), _full((D, n)), _full((1, n))],
        out_specs=_full((N_DEV, n)), out_shape=jax.ShapeDtypeStruct((N_DEV, n), F32), grid=(1,),
        compiler_params=_params(("arbitrary",)),
    )(c_all, w_shard, b_shard)


def _ada_bwd(c_all_t, dmod_pad):
    n = dmod_pad.shape[1]

    def body(c_ref, d_ref, o_ref):
        cv = c_ref[...]
        act = (cv * _sigmoid(cv)).astype(BF16)
        o_ref[...] = jnp.dot(act, d_ref[...].astype(BF16), preferred_element_type=F32)

    return pl.pallas_call(
        body, name="ada_bwd", in_specs=[_full((D, LANES)), _full((LANES, n))],
        out_specs=_full((D, n)), out_shape=jax.ShapeDtypeStruct((D, n), F32), grid=(1,),
        compiler_params=_params(("arbitrary",)),
    )(c_all_t, dmod_pad)


ADAM_ROWS = 256


def _adamw(parts, w, m, v, *, name):
    n, R, C = parts.shape
    tr = next((t for t in (ADAM_ROWS, 128, 64, 32, 16, SUBLANES) if R % t == 0), R)

    def body(p_ref, w_ref, m_ref, v_ref, g_ref, d_ref, nm_ref, nv_ref):
        g = p_ref[0].astype(F32)
        for j in range(1, n):
            g = g + p_ref[j].astype(F32)
        g_ref[...] = g
        nm = ADAM_B1 * m_ref[...] + (1.0 - ADAM_B1) * g
        nv = ADAM_B2 * v_ref[...] + (1.0 - ADAM_B2) * (g * g)
        nm_ref[...] = nm
        nv_ref[...] = nv
        m_hat = nm / (1.0 - ADAM_B1 ** ADAM_STEP)
        v_hat = nv / (1.0 - ADAM_B2 ** ADAM_STEP)
        d_ref[...] = -ADAM_LR * (m_hat / (jnp.sqrt(v_hat) + ADAM_EPS) + ADAM_WD * w_ref[...])

    row = pl.BlockSpec((tr, C), lambda i: (i, 0))
    return pl.pallas_call(
        body, name=name, grid=(R // tr,),
        in_specs=[pl.BlockSpec((n, tr, C), lambda i: (0, i, 0)), row, row, row], out_specs=[row] * 4,
        out_shape=[jax.ShapeDtypeStruct((R, C), F32)] * 4,
        compiler_params=_params(("parallel",)),
    )(parts, w, m, v)


SHARDED = (("w_in", D, IN_W, 1), ("w_branch_a", CONV_W, D, 1), ("w_branch_b", ATTN_W, D, 1), ("w_out", D, D, 0),
           ("w_up", D, 2 * D_FF, 1), ("w_down", D_FF, D, 0), ("conv_a_w", 3, CONV_W, 1),
           ("conv_ffn_w", 3, 2 * D_FF, 1))
MATRICES = SHARDED[:6]
LATE = MATRICES[1:]
CONVS = SHARDED[6:]
REPLICATED = (("b_ada", N_MOD * D), ("norm1_g", D), ("norm2_g", D), ("b_f", N_HEADS), ("q_norm_g", HEAD_DIM),
              ("k_norm_g", HEAD_DIM))


def _shard_shape(rows, cols, axis):
    return (rows // N_DEV, cols) if axis == 0 else (rows, cols // N_DEV)


def _pack_rows(flat, multiple):
    length = flat.shape[-1]
    rows = -(-length // PACK_W)
    rows = -(-rows // multiple) * multiple
    pad = [(0, 0)] * (flat.ndim - 1) + [(0, rows * PACK_W - length)]
    return jnp.pad(flat, pad).reshape(flat.shape[:-1] + (rows, PACK_W))


def _pack_shards(shards, spec, multiple, dtype):
    flat = jnp.concatenate([shards[name].reshape(-1).astype(dtype) for name, *_ in spec])
    return _pack_rows(flat, multiple)


def _join_lane_blocks(gathered):
    n, r, c = gathered.shape

    def body(g_ref, o_ref):
        for j in range(n):
            o_ref[:, j * c:(j + 1) * c] = g_ref[j]

    return pl.pallas_call(
        body, name="join_lane_blocks", grid=(1,), in_specs=[_full((n, r, c))], out_specs=_full((r, n * c)),
        out_shape=jax.ShapeDtypeStruct((r, n * c), gathered.dtype), compiler_params=_params(("arbitrary",)),
    )(gathered)


SHARD_PAD = 768


def _assemble_columns(gathered, shard_cols, out_cols, segments, *, name):
    n, rows, padw = gathered.shape
    assert n == N_DEV and padw == SHARD_PAD and shard_cols <= SHARD_PAD

    def body(g_ref, o_ref):
        j = pl.program_id(0)

        @pl.when(j == 0)
        def _():
            o_ref[...] = jnp.zeros_like(o_ref)

        for dev in range(N_DEV):
            @pl.when(j == dev)
            def _(dev=dev):
                x = g_ref[0]
                for lo, hi, delta in segments:
                    a, b = max(lo, dev * shard_cols), min(hi, (dev + 1) * shard_cols)
                    if a >= b:
                        continue
                    base = (a + delta) // LANES * LANES
                    width = -(-(b + delta - base) // LANES) * LANES
                    src = lax.broadcasted_iota(jnp.int32, (padw, width), 0) + dev * shard_cols
                    dst = lax.broadcasted_iota(jnp.int32, (padw, width), 1) + (base - delta)
                    place = jnp.where((src == dst) & (src >= a) & (src < b), 1.0, 0.0).astype(BF16)
                    moved = jnp.dot(x, place, preferred_element_type=F32).astype(BF16)
                    o_ref[:, base:base + width] = o_ref[:, base:base + width] + moved

    return pl.pallas_call(
        body, name=name, grid=(N_DEV,), in_specs=[pl.BlockSpec((1, rows, padw), lambda j: (j, 0, 0))],
        out_specs=_full((rows, out_cols)), out_shape=jax.ShapeDtypeStruct((rows, out_cols), BF16),
        compiler_params=_params(("arbitrary",)),
    )(gathered)


def _pad_shard(w):
    return jnp.pad(w.astype(BF16), ((0, 0), (0, SHARD_PAD - w.shape[1])))


W_IN_SEGMENTS = ((0, COL_GA, 0), (COL_GA, COL_GA + N_HEADS, COL_F - COL_GA), (COL_GA + N_HEADS, IN_W, -N_HEADS))


def _join_shards(gathered, axis):
    if axis == 0:
        return gathered.reshape(N_DEV * gathered.shape[1], gathered.shape[2])
    if gathered.shape[2] == LANES:
        return _join_lane_blocks(gathered)
    return jnp.concatenate([gathered[j] for j in range(N_DEV)], axis=1)


def _column_shards(pieces):
    total = sum(p.shape[1] for p in pieces)
    width = total // N_DEV
    shards = []
    for j in range(N_DEV):
        lo, hi, off, segs = j * width, (j + 1) * width, 0, []
        for p in pieces:
            a, b = max(lo, off), min(hi, off + p.shape[1])
            if a < b:
                segs.append(p[:, a - off:b - off])
            off += p.shape[1]
        shards.append(segs[0] if len(segs) == 1 else jnp.concatenate(segs, axis=1))
    return jnp.stack(shards)


def _split_shards(full, axis):
    if axis == 0:
        return full.reshape(N_DEV, full.shape[0] // N_DEV, full.shape[1])
    c = full.shape[1] // N_DEV
    return jnp.stack([full[:, j * c:(j + 1) * c] for j in range(N_DEV)])


def _unpack_shards(packed, spec):
    flat = packed.reshape(-1)
    out, off = {}, 0
    for name, rows, cols, axis in spec:
        r, c = _shard_shape(rows, cols, axis)
        out[name] = flat[off:off + r * c].reshape(r, c)
        off += r * c
    return out


def _unpack_gathered(gathered, spec):
    flat = gathered.reshape(N_DEV, -1)
    out, off = {}, 0
    for name, rows, cols, axis in spec:
        r, c = _shard_shape(rows, cols, axis)
        seg = flat[:, off:off + r * c].reshape(N_DEV, r, c)
        out[name] = seg.reshape(rows, cols) if axis == 0 else seg.transpose(1, 0, 2).reshape(rows, cols)
        off += r * c
    return out


def _pack_full_by_dest(full, spec, multiple):
    segs = []
    for name, rows, cols, axis in spec:
        r, c = _shard_shape(rows, cols, axis)
        a = full[name]
        seg = a.reshape(N_DEV, r, c) if axis == 0 else a.reshape(rows, N_DEV, c).transpose(1, 0, 2)
        segs.append(seg.reshape(N_DEV, r * c))
    return _pack_rows(jnp.concatenate(segs, axis=1), multiple)


def _pad_head_rows(w):
    n = w.shape[1]
    padded = jnp.pad(w.reshape(N_HEADS, HEAD_DIM, n), ((0, 0), (0, LANES - HEAD_DIM), (0, 0)))
    return padded.reshape(N_HEADS * LANES, n)


def kernel(x, c, w_ada, b_ada, norm1_g, w_in, b_f, conv_a_w, q_norm_g, k_norm_g, w_branch_a, w_branch_b, w_out, norm2_g, w_up, conv_ffn_w, w_down, loss_target, m_w_ada, m_b_ada, m_norm1_g, m_w_in, m_b_f, m_conv_a_w, m_q_norm_g, m_k_norm_g, m_w_branch_a, m_w_branch_b, m_w_out, m_norm2_g, m_w_up, m_conv_ffn_w, m_w_down, v_w_ada, v_b_ada, v_norm1_g, v_w_in, v_b_f, v_conv_a_w, v_q_norm_g, v_k_norm_g, v_w_branch_a, v_w_branch_b, v_w_out, v_norm2_g, v_w_up, v_conv_ffn_w, v_w_down):
    names = ("w_ada", "b_ada", "norm1_g", "w_in", "b_f", "conv_a_w", "q_norm_g", "k_norm_g", "w_branch_a",
             "w_branch_b", "w_out", "norm2_g", "w_up", "conv_ffn_w", "w_down")
    squeeze = lambda a: a[0] if a.ndim == 3 else a
    W = dict(zip(names, map(squeeze, (w_ada, b_ada, norm1_g, w_in, b_f, conv_a_w, q_norm_g, k_norm_g, w_branch_a,
                                      w_branch_b, w_out, norm2_g, w_up, conv_ffn_w, w_down))))
    M = dict(zip(names, map(squeeze, (m_w_ada, m_b_ada, m_norm1_g, m_w_in, m_b_f, m_conv_a_w, m_q_norm_g,
                                      m_k_norm_g, m_w_branch_a, m_w_branch_b, m_w_out, m_norm2_g, m_w_up,
                                      m_conv_ffn_w, m_w_down))))
    V = dict(zip(names, map(squeeze, (v_w_ada, v_b_ada, v_norm1_g, v_w_in, v_b_f, v_conv_a_w, v_q_norm_g,
                                      v_k_norm_g, v_w_branch_a, v_w_branch_b, v_w_out, v_norm2_g, v_w_up,
                                      v_conv_ffn_w, v_w_down))))
    me = 4 * lax.axis_index("x") + 2 * lax.axis_index("y") + lax.axis_index("c")
    ada_n = N_MOD * D // N_DEV

    small = jnp.concatenate([c.reshape(-1), W["conv_a_w"].reshape(-1), W["conv_ffn_w"].reshape(-1)])
    small_all, w_in_all = _gather_two_level([_pack_rows(small, SUBLANES), _pad_shard(W["w_in"])],
                                            name="gather_first")
    small_all = small_all.reshape(N_DEV, -1)
    c_all = small_all[:, :D]
    conv_all = _unpack_gathered(small_all[:, D:], CONVS)

    b_shard = lax.dynamic_slice(W["b_ada"], (0, me * ada_n), (1, ada_n))
    mod_part = _ada_fwd(c_all, W["w_ada"], b_shard)
    mod_all, = _exchange([mod_part], name="gather_mod", scatter=False)
    mod = lax.dynamic_index_in_dim(mod_all, me, axis=1, keepdims=False).reshape(1, N_MOD * D)

    wts = {"w_in": _assemble_columns(w_in_all, IN_W // N_DEV, IN_W_PAD, W_IN_SEGMENTS, name="assemble_w_in")}
    wts["w_in_t"] = wts["w_in"].T
    wts.update(conv_all)
    for name in ("norm1_g", "norm2_g", "q_norm_g", "k_norm_g", "b_f"):
        wts[name] = W[name]
    late = {name: _pad_shard(W[name]) if name == "w_up" else W[name].astype(BF16) for name, *_ in LATE}

    sq, grad_x, grads, parts = _local_step(x[0], loss_target[0], mod, wts, late)
    loss = lax.psum(sq[0, 0] * (0.5 / D), AXES)

    grads["b_ada"] = grads["mod"]
    rep_flat = lambda src: jnp.concatenate([src[name].reshape(-1) for name, _ in REPLICATED])
    rep_parts, = _exchange([_pack_rows(rep_flat(grads), 16)], name="gather_small_grads", scatter=False)
    rep_out = _adamw(rep_parts, *[_pack_rows(rep_flat(s), 16) for s in (W, M, V)], name="adamw_replicated")

    dmod_all = rep_parts.reshape(N_DEV, -1)[:, :N_MOD * D]
    dmod_mine = lax.dynamic_slice(dmod_all, (0, me * ada_n), (N_DEV, ada_n))
    g_ada = _ada_bwd(jnp.pad(c_all.T, ((0, 0), (0, LANES - N_DEV))),
                     jnp.pad(dmod_mine, ((0, LANES - N_DEV), (0, 0))))
    ada_out = _adamw(g_ada[None], W["w_ada"], M["w_ada"], V["w_ada"], name="adamw_ada")

    mat_out = {name: _adamw(parts[name], W[name], M[name], V[name], name="adamw_" + name) for name, *_ in MATRICES}
    conv_out = _adamw(parts["conv"], *[_pack_shards(s, CONVS, SUBLANES, F32) for s in (W, M, V)],
                      name="adamw_conv")

    results = []
    for kind in range(4):
        per = {"w_ada": ada_out[kind]}
        per.update({name: out[kind] for name, out in mat_out.items()})
        per.update(_unpack_shards(conv_out[kind], CONVS))
        flat, off = rep_out[kind].reshape(-1), 0
        for name, n in REPLICATED:
            per[name] = flat[off:off + n].reshape(1, n)
            off += n
        results.append(per)
    restore = lambda name, a: a[None] if W[name].ndim == 2 and name not in dict(REPLICATED) else a
    outs = [loss, grad_x[None]]
    for per in results:
        outs.extend(restore(name, per[name]) for name in names)
    return tuple(outs)
```

```python
import functools

import jax
import jax.numpy as jnp
import numpy as np
from jax import lax
from jax.experimental import pallas as pl
from jax.experimental.pallas import tpu as pltpu

F32 = jnp.float32
BF16 = jnp.bfloat16

N_DEV = 8
D = 1024
N_HEADS = 8
HEAD_DIM = 64
ATTN_W = 512
CONV_W = 512
D_FF = 2816
N_MOD = 6
IN_W = 5128
RMS_EPS = 1e-6
NEG_INF = -1e30

IN_W_PAD = 5376
COL_GA = 3072
COL_GB = 4096
COL_F = 5120
F_PAD = 128

ADAM_LR = 0.001
ADAM_B1 = 0.9
ADAM_B2 = 0.999
ADAM_EPS = 1e-08
ADAM_WD = 0.01
ADAM_STEP = 10

LANES = 128
SUBLANES = 8
BF16_ROWS = 16
VMEM_LIMIT = 52 * 1024 * 1024
TOKEN_TILE = 512
MATMUL_TILE = 1024
ATTN_BLOCK = 512
PACK_W = 1024

MESH = pl.DeviceIdType.MESH
AXES = ("x", "y", "c")


def _params(sem=None, **kw):
    return pltpu.CompilerParams(dimension_semantics=sem, vmem_limit_bytes=VMEM_LIMIT, **kw)


def _full(shape):
    nd = len(shape)
    return pl.BlockSpec(shape, lambda *_: (0,) * nd)


def _tn_dot(a, b):
    return lax.dot_general(a, b, (((0,), (0,)), ((), ())), preferred_element_type=F32)


def _matmul(a, b, *, name, tm, tn, tk, out_dtype=F32, trans_a=False, exchange=None):
    if trans_a:
        K, M = a.shape
    else:
        M, K = a.shape
    N = b.shape[1]
    assert b.shape[0] == K and M % tm == 0 and N % tn == 0 and K % tk == 0, (name, a.shape, b.shape)
    nm, nn, nk = M // tm, N // tn, K // tk

    def body(*refs):
        if exchange is None:
            a_ref, b_ref, o_ref, *own = refs
        else:
            (a_ref, b_ref), (o_ref,), own, xrefs = exchange.split(refs, 2, 1)
            ids = [pl.program_id(d) for d in range(3)]
            first = jnp.logical_and(jnp.logical_and(ids[0] == 0, ids[1] == 0), ids[2] == 0)
            last = jnp.logical_and(jnp.logical_and(ids[0] == nn - 1, ids[1] == nm - 1), ids[2] == nk - 1)
            _ride(exchange, first, last, xrefs)
        k = pl.program_id(2)
        av = a_ref[...].astype(BF16)
        bv = b_ref[...].astype(BF16)
        prod = _tn_dot(av, bv) if trans_a else jnp.dot(av, bv, preferred_element_type=F32)
        if nk == 1:
            o_ref[...] = prod.astype(out_dtype)
            return
        acc_ref, = own

        @pl.when(k == 0)
        def _():
            acc_ref[...] = prod

        @pl.when(k > 0)
        def _():
            acc_ref[...] += prod

        @pl.when(k == nk - 1)
        def _():
            o_ref[...] = acc_ref[...].astype(out_dtype)

    if trans_a:
        a_spec = pl.BlockSpec((tk, tm), lambda j, i, k: (k, i))
    else:
        a_spec = pl.BlockSpec((tm, tk), lambda j, i, k: (i, k))
    in_specs = [a_spec, pl.BlockSpec((tk, tn), lambda j, i, k: (k, j))]
    out_spec = pl.BlockSpec((tm, tn), lambda j, i, k: (i, j))
    out_shape = jax.ShapeDtypeStruct((M, N), out_dtype)
    scratch = [pltpu.VMEM((tm, tn), F32)] if nk > 1 else []
    if exchange is None:
        return pl.pallas_call(
            body, name=name, grid=(nn, nm, nk), in_specs=in_specs, out_specs=out_spec, out_shape=out_shape,
            scratch_shapes=scratch, compiler_params=_params(("parallel", "parallel", "arbitrary")),
        )(a, b)
    return pl.pallas_call(
        body, name=name, grid=(nn, nm, nk), in_specs=in_specs + exchange.in_specs,
        out_specs=[out_spec] + exchange.out_specs, out_shape=[out_shape] + exchange.out_shapes,
        scratch_shapes=scratch + exchange.scratch, compiler_params=_params(("arbitrary",) * 3),
    )(a, b, *exchange.xs)


def _matmul_pieces(pieces, b, *, name, tm, exchange=None):
    M = pieces[0].shape[0]
    widths = [p.shape[1] for p in pieces]
    offsets = [sum(widths[:i]) for i in range(len(widths))]
    N = b.shape[1]
    assert b.shape[0] >= sum(widths) and M % tm == 0, (name, widths, b.shape)
    n_p, nm = len(pieces), M // tm

    def body(*refs):
        if exchange is None:
            ins, o_ref = refs[:n_p + 1], refs[n_p + 1]
        else:
            ins, (o_ref,), _, xrefs = exchange.split(refs, n_p + 1, 1)
            i = pl.program_id(0)
            _ride(exchange, i == 0, i == nm - 1, xrefs)
        b_ref = ins[n_p]
        acc = None
        for a_ref, off, w in zip(ins[:n_p], offsets, widths):
            term = jnp.dot(a_ref[...].astype(BF16), b_ref[off:off + w, :], preferred_element_type=F32)
            acc = term if acc is None else acc + term
        o_ref[...] = acc

    in_specs = [pl.BlockSpec((tm, w), lambda i: (i, 0)) for w in widths] + [_full(b.shape)]
    out_spec = pl.BlockSpec((tm, N), lambda i: (i, 0))
    out_shape = jax.ShapeDtypeStruct((M, N), F32)
    if exchange is None:
        return pl.pallas_call(
            body, name=name, grid=(nm,), in_specs=in_specs, out_specs=out_spec, out_shape=out_shape,
            compiler_params=_params(("parallel",)),
        )(*pieces, b)
    return pl.pallas_call(
        body, name=name, grid=(nm,), in_specs=in_specs + exchange.in_specs,
        out_specs=[out_spec] + exchange.out_specs, out_shape=[out_shape] + exchange.out_shapes,
        scratch_shapes=exchange.scratch, compiler_params=_params(("arbitrary",)),
    )(*pieces, b, *exchange.xs)


def _matmul_tn_pieces(a, pieces, *, name, tk):
    K, M = a.shape
    widths = [p.shape[1] for p in pieces]
    n_p, nk = len(pieces), K // tk

    def body(*refs):
        a_ref, p_refs, o_refs = refs[0], refs[1:n_p + 1], refs[n_p + 1:]
        k = pl.program_id(0)
        av = a_ref[...].astype(BF16)
        for p_ref, o_ref in zip(p_refs, o_refs):
            prod = _tn_dot(av, p_ref[...].astype(BF16))

            @pl.when(k == 0)
            def _():
                o_ref[...] = prod

            @pl.when(k > 0)
            def _():
                o_ref[...] += prod

    return pl.pallas_call(
        body, name=name, grid=(nk,),
        in_specs=[pl.BlockSpec((tk, M), lambda k: (k, 0))] + [pl.BlockSpec((tk, w), lambda k: (k, 0)) for w in widths],
        out_specs=[_full((M, w)) for w in widths], out_shape=[jax.ShapeDtypeStruct((M, w), F32) for w in widths],
        compiler_params=_params(("arbitrary",)),
    )(a, *pieces)


def _split_dot(x, mat, parts):
    out = None
    rem = x
    for p in range(parts):
        piece = rem.astype(BF16)
        term = jnp.dot(piece, mat, preferred_element_type=F32)
        out = term if out is None else out + term
        if p + 1 < parts:
            rem = rem - piece.astype(F32)
    return out


def _sigmoid(x):
    return 1.0 / (1.0 + jnp.exp(-x))


def _rows8(x):
    r, c = x.shape
    return jnp.sum(x.reshape(r // SUBLANES, SUBLANES, c), axis=0)


def _shift_down(blk, prev8, n):
    rolled = pltpu.roll(blk, n, axis=0)
    prev_rolled = pltpu.roll(prev8, n, axis=0)
    rows = lax.broadcasted_iota(jnp.int32, prev8.shape, 0)
    first = jnp.where(rows < n, prev_rolled, rolled[0:SUBLANES])
    return jnp.concatenate([first, rolled[SUBLANES:]], axis=0)


def _shift_up(blk, next8, n):
    r = blk.shape[0]
    rolled = pltpu.roll(blk, r - n, axis=0)
    next_rolled = pltpu.roll(next8, SUBLANES - n, axis=0)
    rows = lax.broadcasted_iota(jnp.int32, next8.shape, 0)
    last = jnp.where(rows >= SUBLANES - n, next_rolled, rolled[r - SUBLANES:])
    return jnp.concatenate([rolled[:r - SUBLANES], last], axis=0)


def _prev_spec(tm, width, col):
    per = tm // SUBLANES
    return pl.BlockSpec((SUBLANES, width), lambda i, *_: (jnp.maximum(i * per - 1, 0), col))


def _next_spec(tm, width, col, n_tiles):
    per = tm // SUBLANES
    last = n_tiles * per - 1
    return pl.BlockSpec((SUBLANES, width), lambda i, *_: (jnp.minimum((i + 1) * per, last), col))


def _group_matrix():
    idx = np.arange(ATTN_W) // HEAD_DIM
    return jnp.asarray((idx[:, None] == idx[None, :]).astype(np.float32), BF16)


def _norm_mod(x, g, sc, sh, *, name):
    T = x.shape[0]
    tm = min(TOKEN_TILE, T)

    def body(x_ref, g_ref, sc_ref, sh_ref, o_ref):
        xv = x_ref[...]
        inv = lax.rsqrt(jnp.mean(xv * xv, axis=-1, keepdims=True) + RMS_EPS)
        o_ref[...] = ((xv * inv) * g_ref[...] * (1.0 + sc_ref[...]) + sh_ref[...]).astype(BF16)

    row = pl.BlockSpec((tm, D), lambda i: (i, 0))
    return pl.pallas_call(
        body, name=name, grid=(T // tm,),
        in_specs=[row, _full((1, D)), _full((1, D)), _full((1, D))],
        out_specs=row, out_shape=jax.ShapeDtypeStruct((T, D), BF16),
        compiler_params=_params(("parallel",)),
    )(x, g, sc, sh)


LANE_ONE = 64
LANE_F = 67
LANE_LSE = 70
LANE_SUM = 73


def _pieces(x):
    hi = x.astype(BF16).astype(F32)
    rest = x - hi
    mid = rest.astype(BF16).astype(F32)
    return hi, mid, rest - mid


def _aug(lane, data, entries):
    out = jnp.where(lane < HEAD_DIM, data, 0.0)
    for idx, val in entries:
        out = jnp.where(lane == idx, val, out)
    return out


def _run(start, vals):
    return [(start + i, v) for i, v in enumerate(vals)]


def _head_lanes(a, h):
    blk = a[:, LANES * (h // 2):LANES * (h // 2) + LANES]
    return blk if h % 2 == 0 else pltpu.roll(blk, HEAD_DIM, axis=1)


def _branch_prep(proj, fcum, conv_w8, qg, kg, gmat):
    T = proj.shape[0]
    tm = min(TOKEN_TILE, T)
    nt = T // tm

    def body(cb_ref, cc_ref, cv_ref, q_ref, k_ref, v_ref, f_ref, ccp_ref, cvp_ref, w_ref, qg_ref, kg_ref, g_ref,
             ya_ref, qa_ref, ka_ref, va_ref):
        i = pl.program_id(0)
        z = cc_ref[...] * cv_ref[...]
        zp = jnp.where(i > 0, ccp_ref[...] * cvp_ref[...], 0.0)
        w = w_ref[...]
        cz = _shift_down(z, zp, 2) * w[0:1] + _shift_down(z, zp, 1) * w[1:2] + z * w[2:3]
        ya_ref[...] = (cb_ref[...] * cz).astype(BF16)
        gm = g_ref[...]

        def normed(src, gain, scale):
            v = src[...]
            ms = _split_dot(v * v, gm, 2) * (1.0 / HEAD_DIM)
            return (v * lax.rsqrt(ms + RMS_EPS)) * gain[...] * scale

        qn = normed(q_ref, qg_ref, 1.0 / np.sqrt(HEAD_DIM))
        kn = normed(k_ref, kg_ref, 1.0)
        vv = v_ref[...]
        fall = f_ref[...]
        lane = lax.broadcasted_iota(jnp.int32, (tm, LANES), 1)
        ones3 = [1.0, 1.0, 1.0]
        for h in range(N_HEADS):
            hi, mid, lo = _pieces(fall[:, h:h + 1])
            qa_ref[h] = _aug(lane, _head_lanes(qn, h), _run(LANE_ONE, ones3) + _run(LANE_F, [hi, mid, lo])
                             + [(LANE_SUM, 1.0)]).astype(BF16)
            ka_ref[h] = _aug(lane, _head_lanes(kn, h), _run(LANE_ONE, [-hi, -mid, -lo]) + _run(LANE_F, ones3)
                             + _run(LANE_LSE, ones3)).astype(BF16)
            va_ref[h] = _aug(lane, _head_lanes(vv, h), _run(LANE_ONE, ones3)).astype(BF16)

    blk = lambda col: pl.BlockSpec((tm, CONV_W), lambda i: (i, col))
    heads = pl.BlockSpec((N_HEADS, tm, LANES), lambda i: (0, i, 0))
    return pl.pallas_call(
        body, name="branch_prep", grid=(nt,),
        in_specs=[blk(0), blk(1), blk(2), blk(3), blk(4), blk(5), pl.BlockSpec((tm, F_PAD), lambda i: (i, 0)),
                  _prev_spec(tm, CONV_W, 1), _prev_spec(tm, CONV_W, 2),
                  _full((SUBLANES, CONV_W)), _full((1, ATTN_W)), _full((1, ATTN_W)), _full((ATTN_W, ATTN_W))],
        out_specs=[pl.BlockSpec((tm, CONV_W), lambda i: (i, 0)), heads, heads, heads],
        out_shape=[jax.ShapeDtypeStruct((T, CONV_W), BF16)] + [jax.ShapeDtypeStruct((N_HEADS, T, LANES), BF16)] * 3,
        compiler_params=_params(("parallel",)),
    )(proj, proj, proj, proj, proj, proj, fcum, proj, proj, conv_w8, qg, kg, gmat)


def _cumsum(x, *, reverse, name, col=0, gate_bias=None):
    T = x.shape[0]
    tm = min(TOKEN_TILE, T)
    nt = T // tm

    def body(x_ref, b_ref, o_ref, carry_ref):
        i = pl.program_id(0)

        @pl.when(i == 0)
        def _():
            carry_ref[...] = jnp.zeros_like(carry_ref)

        r = lax.broadcasted_iota(jnp.int32, (tm, tm), 0)
        c = lax.broadcasted_iota(jnp.int32, (tm, tm), 1)
        tri = jnp.where((c >= r) if reverse else (c <= r), 1.0, 0.0).astype(BF16)
        xv = x_ref[...]
        if gate_bias is not None:
            fx = xv + b_ref[...]
            xv = jnp.minimum(fx, 0.0) - jnp.log(1.0 + jnp.exp(-jnp.abs(fx)))
        out = _split_dot_left(tri, xv, 3) + carry_ref[0:1]
        o_ref[...] = out
        carry_ref[...] = jnp.broadcast_to(out[0:1] if reverse else out[tm - 1:tm], carry_ref.shape)

    rows = (lambda i: nt - 1 - i) if reverse else (lambda i: i)
    bias = jnp.zeros((1, F_PAD), F32) if gate_bias is None else gate_bias
    return pl.pallas_call(
        body, name=name, grid=(nt,),
        in_specs=[pl.BlockSpec((tm, F_PAD), lambda i: (rows(i), col)), _full((1, F_PAD))],
        out_specs=pl.BlockSpec((tm, F_PAD), lambda i: (rows(i), 0)),
        out_shape=jax.ShapeDtypeStruct((T, F_PAD), F32),
        scratch_shapes=[pltpu.VMEM((SUBLANES, F_PAD), F32)],
        compiler_params=_params(("arbitrary",)),
    )(x, bias)


def _split_dot_left(mat, x, parts):
    out = None
    rem = x
    for p in range(parts):
        piece = rem.astype(BF16)
        term = jnp.dot(mat, piece, preferred_element_type=F32)
        out = term if out is None else out + term
        if p + 1 < parts:
            rem = rem - piece.astype(F32)
    return out


def _out_resid_norm(x, merged, w_out, g1, g, sc, sh):
    T = x.shape[0]
    tm = min(TOKEN_TILE, T)

    def body(x_ref, m_ref, w_ref, g1_ref, g_ref, sc_ref, sh_ref, mix_ref, x1_ref, h_ref):
        mix = jnp.dot(m_ref[...], w_ref[...], preferred_element_type=F32)
        mix_ref[...] = mix
        x1 = x_ref[...] + g1_ref[...] * mix
        x1_ref[...] = x1
        inv = lax.rsqrt(jnp.mean(x1 * x1, axis=-1, keepdims=True) + RMS_EPS)
        h_ref[...] = ((x1 * inv) * g_ref[...] * (1.0 + sc_ref[...]) + sh_ref[...]).astype(BF16)

    row = pl.BlockSpec((tm, D), lambda i: (i, 0))
    vec = _full((1, D))
    return pl.pallas_call(
        body, name="out_resid_norm", grid=(T // tm,),
        in_specs=[row, row, _full((D, D)), vec, vec, vec, vec], out_specs=[row, row, row],
        out_shape=[jax.ShapeDtypeStruct((T, D), F32), jax.ShapeDtypeStruct((T, D), F32),
                   jax.ShapeDtypeStruct((T, D), BF16)],
        compiler_params=_params(("parallel",)),
    )(x, merged, w_out, g1, g, sc, sh)


FFN_TM = 256
FFN_TC = 1408


def _tanh_sigmoid(x):
    return 0.5 * jnp.tanh(0.5 * x) + 0.5


def _ffn_act_fwd(u, w8):
    T = u.shape[0]
    tm = min(FFN_TM, T)
    nt = T // tm
    nc = D_FF // FFN_TC

    def body(ug_ref, uv_ref, ugp_ref, uvp_ref, wg_ref, wv_ref, o_ref, cg_ref, cv_ref):
        i = pl.program_id(1)

        def conv(u_ref, p_ref, w_ref):
            uv = u_ref[...]
            up = jnp.where(i > 0, p_ref[...], 0.0)
            w = w_ref[...]
            return _shift_down(uv, up, 2) * w[0:1] + _shift_down(uv, up, 1) * w[1:2] + uv * w[2:3]

        gate = conv(ug_ref, ugp_ref, wg_ref)
        val = conv(uv_ref, uvp_ref, wv_ref)
        cg_ref[...] = gate.astype(BF16)
        cv_ref[...] = val.astype(BF16)
        o_ref[...] = (gate * _tanh_sigmoid(gate) * val).astype(BF16)

    per = tm // SUBLANES
    blk = lambda off: pl.BlockSpec((tm, FFN_TC), lambda j, i: (i, j + off))
    prev = lambda off: pl.BlockSpec((SUBLANES, FFN_TC), lambda j, i: (jnp.maximum(i * per - 1, 0), j + off))
    wblk = lambda off: pl.BlockSpec((SUBLANES, FFN_TC), lambda j, i: (0, j + off))
    return pl.pallas_call(
        body, name="ffn_act_fwd", grid=(nc, nt),
        in_specs=[blk(0), blk(nc), prev(0), prev(nc), wblk(0), wblk(nc)],
        out_specs=[blk(0), blk(0), blk(0)],
        out_shape=[jax.ShapeDtypeStruct((T, D_FF), BF16)] * 3,
        compiler_params=_params(("parallel", "parallel")),
    )(u, u, u, u, w8, w8)


def _down_loss_head(x1, act, w_down, g2, target):
    T = x1.shape[0]
    tm = min(TOKEN_TILE, T)

    def body(x1_ref, a_ref, w_ref, g2_ref, t_ref, dy_ref, dff_ref, loss_ref, dg2_ref):
        i = pl.program_id(0)

        @pl.when(i == 0)
        def _():
            loss_ref[...] = jnp.zeros_like(loss_ref)
            dg2_ref[...] = jnp.zeros_like(dg2_ref)

        ff = jnp.dot(a_ref[...], w_ref[...], preferred_element_type=F32)
        err = x1_ref[...] + g2_ref[...] * ff - t_ref[...]
        dy = err * (1.0 / D)
        dy_ref[...] = dy
        dff_ref[...] = (dy * g2_ref[...]).astype(BF16)
        loss_ref[...] += _rows8(err * err)
        dg2_ref[...] += _rows8(dy * ff)

    row = pl.BlockSpec((tm, D), lambda i: (i, 0))
    acc = _full((SUBLANES, D))
    return pl.pallas_call(
        body, name="down_loss_head", grid=(T // tm,),
        in_specs=[row, pl.BlockSpec((tm, D_FF), lambda i: (i, 0)), _full((D_FF, D)), _full((1, D)), row],
        out_specs=[row, row, acc, acc],
        out_shape=[jax.ShapeDtypeStruct((T, D), F32), jax.ShapeDtypeStruct((T, D), BF16),
                   jax.ShapeDtypeStruct((SUBLANES, D), F32), jax.ShapeDtypeStruct((SUBLANES, D), F32)],
        compiler_params=_params(("arbitrary",)),
    )(x1, act, w_down, g2, target)


def _nt_dot(a, b):
    return lax.dot_general(a, b, (((1,), (1,)), ((), ())), preferred_element_type=F32)


def _causal(n, keys_on_rows=False):
    r = lax.broadcasted_iota(jnp.int32, (n, n), 0)
    c = lax.broadcasted_iota(jnp.int32, (n, n), 1)
    return (c >= r) if keys_on_rows else (c <= r)


def _sweep(lo, hi, step, carry, group=2):
    while group >= 1:
        def several(j, cr, lo=lo, group=group):
            for g in range(group):
                cr = step(lo + group * j + g, cr)
            return cr

        passes = (hi - lo) // group
        carry = lax.fori_loop(0, passes, several, carry)
        lo = lo + group * passes
        group //= 2
    return carry


def _grid_ends(n0, n1):
    i0, i1 = pl.program_id(0), pl.program_id(1)
    return jnp.logical_and(i0 == 0, i1 == 0), jnp.logical_and(i0 == n0 - 1, i1 == n1 - 1)


def _attn_fwd(qa, ka, va, exchange=None):
    nh, T, _ = qa.shape
    bq = min(ATTN_BLOCK, T)
    nq = T // bq

    def body(*refs):
        if exchange is None:
            q_ref, k_ref, v_ref, o_ref, qb_ref = refs
        else:
            (q_ref, k_ref, v_ref), (o_ref, qb_ref), _, xrefs = exchange.split(refs, 3, 2)
            _ride(exchange, *_grid_ends(nh, nq), xrefs)
        qi = pl.program_id(1)
        q = q_ref[0]

        def step(kb, carry, masked):
            m, acc = carry
            start = pl.multiple_of(kb * bq, bq)
            s = _nt_dot(q, k_ref[0, pl.ds(start, bq), :])
            if masked:
                s = jnp.where(_causal(bq), s, NEG_INF)
            m_new = jnp.maximum(m, jnp.max(s, axis=-1, keepdims=True))
            p = jnp.exp(s - m_new).astype(BF16)
            acc = jnp.exp(m - m_new) * acc + jnp.dot(p, v_ref[0, pl.ds(start, bq), :], preferred_element_type=F32)
            return m_new, acc

        init = (jnp.full((bq, 1), NEG_INF, F32), jnp.zeros((bq, LANES), F32))
        carry = _sweep(0, qi, lambda kb, cr: step(kb, cr, False), init, group=4)
        m, acc = step(qi, carry, True)
        l = acc[:, LANE_ONE:LANE_ONE + 1]
        o_ref[0] = acc / l
        lse = m + jnp.log(l)
        lane = lax.broadcasted_iota(jnp.int32, (bq, LANES), 1)
        qf = q.astype(F32)
        for idx, piece in _run(LANE_LSE, _pieces(lse)):
            qf = jnp.where(lane == idx, -piece, qf)
        qb_ref[0] = qf.astype(BF16)

    tile = pl.BlockSpec((1, bq, LANES), lambda h, i: (h, i, 0))
    whole = pl.BlockSpec((1, T, LANES), lambda h, i: (h, 0, 0))
    out_shape = [jax.ShapeDtypeStruct((nh, T, LANES), F32), jax.ShapeDtypeStruct((nh, T, LANES), BF16)]
    if exchange is None:
        return pl.pallas_call(
            body, name="attn_fwd", grid=(nh, nq), in_specs=[tile, whole, whole], out_specs=[tile, tile],
            out_shape=out_shape, compiler_params=_params(("parallel", "parallel")),
        )(qa, ka, va)
    return pl.pallas_call(
        body, name="attn_fwd", grid=(nh, nq), in_specs=[tile, whole, whole] + exchange.in_specs,
        out_specs=[tile, tile] + exchange.out_specs, out_shape=out_shape + exchange.out_shapes,
        scratch_shapes=exchange.scratch, compiler_params=_params(("arbitrary", "arbitrary")),
    )(qa, ka, va, *exchange.xs)


def _branch_merge_fwd(ya0, o_h, proj, wba, wbb_heads):
    nh, T, _ = o_h.shape
    tm = min(TOKEN_TILE, T)

    def body(ya0_ref, o_ref, ga_ref, gb_ref, wa_ref, wb_ref, ya_ref, yb_ref, m_ref):
        ya = jnp.dot(ya0_ref[...], wa_ref[...], preferred_element_type=F32)
        yb = jnp.dot(o_ref[0].astype(BF16), wb_ref[0:LANES, :], preferred_element_type=F32)
        for h in range(1, nh):
            yb += jnp.dot(o_ref[h].astype(BF16), wb_ref[h * LANES:(h + 1) * LANES, :], preferred_element_type=F32)
        ya_ref[...] = ya.astype(BF16)
        yb_ref[...] = yb.astype(BF16)
        m_ref[...] = (_sigmoid(ga_ref[...]) * ya + _sigmoid(gb_ref[...]) * yb).astype(BF16)

    row = pl.BlockSpec((tm, D), lambda i: (i, 0))
    return pl.pallas_call(
        body, name="branch_merge_fwd", grid=(T // tm,),
        in_specs=[pl.BlockSpec((tm, CONV_W), lambda i: (i, 0)), pl.BlockSpec((nh, tm, LANES), lambda i: (0, i, 0)),
                  pl.BlockSpec((tm, D), lambda i: (i, COL_GA // D)), pl.BlockSpec((tm, D), lambda i: (i, COL_GB // D)),
                  _full((CONV_W, D)), _full((nh * LANES, D))],
        out_specs=[row, row, row],
        out_shape=[jax.ShapeDtypeStruct((T, D), BF16)] * 3,
        compiler_params=_params(("parallel",)),
    )(ya0, o_h, proj, proj, wba, wbb_heads)


def _branch_b_bwd(dyb, o_h, wbb_heads_t):
    nh, T, _ = o_h.shape
    tm = min(TOKEN_TILE, T)

    def body(dyb_ref, o_ref, w_ref, out_ref):
        do = jnp.dot(dyb_ref[...], w_ref[...], preferred_element_type=F32)
        lane = lax.broadcasted_iota(jnp.int32, (tm, LANES), 1)
        for h in range(nh):
            g = do[:, h * LANES:(h + 1) * LANES].astype(BF16).astype(F32)
            delta = jnp.sum(g * o_ref[h], axis=-1, keepdims=True)
            for idx, piece in _run(LANE_ONE, _pieces(delta)):
                g = jnp.where(lane == idx, -piece, g)
            out_ref[h] = g.astype(BF16)

    heads = pl.BlockSpec((nh, tm, LANES), lambda i: (0, i, 0))
    return pl.pallas_call(
        body, name="branch_b_bwd", grid=(T // tm,),
        in_specs=[pl.BlockSpec((tm, D), lambda i: (i, 0)), heads, _full((D, nh * LANES))],
        out_specs=heads, out_shape=jax.ShapeDtypeStruct((nh, T, LANES), BF16),
        compiler_params=_params(("parallel",)),
    )(dyb, o_h, wbb_heads_t)


def _branch_b_dw(o_h, dyb):
    nh, T, _ = o_h.shape
    tk = min(TOKEN_TILE, T)

    def body(o_ref, dyb_ref, out_ref):
        @pl.when(pl.program_id(0) == 0)
        def _():
            out_ref[...] = jnp.zeros_like(out_ref)

        g = dyb_ref[...]
        for h in range(nh):
            out_ref[h] += _tn_dot(o_ref[h].astype(BF16), g)

    return pl.pallas_call(
        body, name="branch_b_dw", grid=(T // tk,),
        in_specs=[pl.BlockSpec((nh, tk, LANES), lambda k: (0, k, 0)), pl.BlockSpec((tk, D), lambda k: (k, 0))],
        out_specs=_full((nh, LANES, D)), out_shape=jax.ShapeDtypeStruct((nh, LANES, D), F32),
        compiler_params=_params(("arbitrary",)),
    )(o_h, dyb)


def _attn_bwd(qb, ka, va, doa, exchange=None):
    nh, T, _ = qb.shape
    bk = min(ATTN_BLOCK, T)
    nk = T // bk

    def body(*refs):
        if exchange is None:
            q_ref, do_ref, k_ref, v_ref, dq_ref, dk_ref, dv_ref = refs
        else:
            (q_ref, do_ref, k_ref, v_ref), (dq_ref, dk_ref, dv_ref), _, xrefs = exchange.split(refs, 4, 3)
            _ride(exchange, *_grid_ends(nh, nk), xrefs)
        ki = pl.program_id(1)

        @pl.when(ki == 0)
        def _():
            dq_ref[...] = jnp.zeros_like(dq_ref)

        k = k_ref[0]
        v = v_ref[0]

        def step(qi, carry, masked):
            dk, dv = carry
            rows = pl.ds(pl.multiple_of(qi * bk, bk), bk)
            q = q_ref[0, rows, :]
            g = do_ref[0, rows, :]
            pt = jnp.exp(_nt_dot(k, q))
            if masked:
                pt = jnp.where(_causal(bk, keys_on_rows=True), pt, 0.0)
            dv = dv + jnp.dot(pt.astype(BF16), g, preferred_element_type=F32)
            dst = (pt * _nt_dot(v, g)).astype(BF16)
            dk = dk + jnp.dot(dst, q, preferred_element_type=F32)
            dq_ref[0, rows, :] += _tn_dot(dst, k)
            return dk, dv

        init = (jnp.zeros((bk, LANES), F32), jnp.zeros((bk, LANES), F32))
        carry = step(ki, init, True)
        dk_ref[0], dv_ref[0] = _sweep(ki + 1, nk, lambda qi, cr: step(qi, cr, False), carry)

    tile = pl.BlockSpec((1, bk, LANES), lambda h, i: (h, i, 0))
    whole = pl.BlockSpec((1, T, LANES), lambda h, i: (h, 0, 0))
    out_shape = [jax.ShapeDtypeStruct((nh, T, LANES), F32)] * 3
    if exchange is None:
        return pl.pallas_call(
            body, name="attn_bwd", grid=(nh, nk), in_specs=[whole, whole, tile, tile],
            out_specs=[whole, tile, tile], out_shape=out_shape, compiler_params=_params(("parallel", "arbitrary")),
        )(qb, doa, ka, va)
    return pl.pallas_call(
        body, name="attn_bwd", grid=(nh, nk), in_specs=[whole, whole, tile, tile] + exchange.in_specs,
        out_specs=[whole, tile, tile] + exchange.out_specs, out_shape=out_shape + exchange.out_shapes,
        scratch_shapes=exchange.scratch, compiler_params=_params(("arbitrary", "arbitrary")),
    )(qb, doa, ka, va, *exchange.xs)


def _attn_unpack(dq_h, dk_h, dv_h):
    nh, T, _ = dq_h.shape
    tm = min(TOKEN_TILE, T)

    def body(dq_ref, dk_ref, dv_ref, q_out, k_out, v_out, f_out):
        lane = lax.broadcasted_iota(jnp.int32, (tm, LANES), 1)
        low = lane < HEAD_DIM
        for src, dst in ((dq_ref, q_out), (dk_ref, k_out), (dv_ref, v_out)):
            for pair in range(nh // 2):
                both = jnp.where(low, src[2 * pair], pltpu.roll(src[2 * pair + 1], HEAD_DIM, axis=1))
                dst[:, LANES * pair:LANES * (pair + 1)] = both.astype(dst.dtype)
        df = jnp.zeros((tm, LANES), F32)
        for h in range(nh):
            col = dq_ref[h][:, LANE_F:LANE_F + 1] - dk_ref[h][:, LANE_SUM:LANE_SUM + 1]
            df = jnp.where(lane == h, col, df)
        f_out[...] = df

    heads = pl.BlockSpec((nh, tm, LANES), lambda i: (0, i, 0))
    tok = pl.BlockSpec((tm, ATTN_W), lambda i: (i, 0))
    return pl.pallas_call(
        body, name="attn_unpack", grid=(T // tm,), in_specs=[heads, heads, heads],
        out_specs=[tok, tok, tok, pl.BlockSpec((tm, F_PAD), lambda i: (i, 0))],
        out_shape=[jax.ShapeDtypeStruct((T, ATTN_W), F32), jax.ShapeDtypeStruct((T, ATTN_W), F32),
                   jax.ShapeDtypeStruct((T, ATTN_W), BF16), jax.ShapeDtypeStruct((T, F_PAD), F32)],
        compiler_params=_params(("parallel",)),
    )(dq_h, dk_h, dv_h)


def _ffn_act_bwd(u, cg, cv, da, w8):
    T = u.shape[0]
    tm = min(FFN_TM, T)
    nt = T // tm
    nc = D_FF // FFN_TC

    def body(ug_ref, uv_ref, cg_ref, cv_ref, cgn_ref, cvn_ref, da_ref, dan_ref, wg_ref, wv_ref,
             dug_ref, duv_ref, dwg_ref, dwv_ref):
        i = pl.program_id(1)

        @pl.when(i == 0)
        def _():
            dwg_ref[...] = jnp.zeros_like(dwg_ref)
            dwv_ref[...] = jnp.zeros_like(dwv_ref)

        gate = jnp.concatenate([cg_ref[...], cgn_ref[...]], axis=0).astype(F32)
        val = jnp.concatenate([cv_ref[...], cvn_ref[...]], axis=0).astype(F32)
        dae = jnp.concatenate([da_ref[...], dan_ref[...]], axis=0).astype(F32)
        rows_e = lax.broadcasted_iota(jnp.int32, dae.shape, 0)
        dae = jnp.where(jnp.logical_and(i == nt - 1, rows_e >= tm), 0.0, dae)
        sg = _tanh_sigmoid(gate)
        n = tm + BF16_ROWS

        def back(d, u_ref, w_ref, du_ref, dw_ref):
            w = w_ref[...]
            uv = u_ref[...]
            d1 = pltpu.roll(d, n - 1, axis=0)[:tm]
            d2 = pltpu.roll(d, n - 2, axis=0)[:tm]
            d0 = d[:tm]
            du_ref[...] = (d0 * w[2:3] + d1 * w[1:2] + d2 * w[0:1]).astype(BF16)
            rows = [jnp.sum(t * uv, axis=0, keepdims=True) for t in (d2, d1, d0)]
            dw_ref[...] += jnp.concatenate(rows + [jnp.zeros((SUBLANES - 3, FFN_TC), F32)], axis=0)

        back(dae * val * sg * (1.0 + gate * (1.0 - sg)), ug_ref, wg_ref, dug_ref, dwg_ref)
        back(dae * gate * sg, uv_ref, wv_ref, duv_ref, dwv_ref)

    per = tm // BF16_ROWS
    last_blk = nt * per - 1
    blk = lambda off: pl.BlockSpec((tm, FFN_TC), lambda j, i: (i, j + off))
    nxt = pl.BlockSpec((BF16_ROWS, FFN_TC), lambda j, i: (jnp.minimum((i + 1) * per, last_blk), j))
    wblk = lambda off: pl.BlockSpec((SUBLANES, FFN_TC), lambda j, i: (0, j + off))
    dug, duv, dwg, dwv = pl.pallas_call(
        body, name="ffn_act_bwd", grid=(nc, nt),
        in_specs=[blk(0), blk(nc), blk(0), blk(0), nxt, nxt, blk(0), nxt, wblk(0), wblk(nc)],
        out_specs=[blk(0), blk(0), wblk(0), wblk(0)],
        out_shape=[jax.ShapeDtypeStruct((T, D_FF), BF16)] * 2 + [jax.ShapeDtypeStruct((SUBLANES, D_FF), F32)] * 2,
        compiler_params=_params(("parallel", "arbitrary")),
    )(u, u, cg, cv, cg, cv, da, da, w8, w8)
    return dug, duv, jnp.concatenate([dwg, dwv], axis=1)


def _norm_bwd(xin, dh, dres, g, sc, *, name):
    T = xin.shape[0]
    tm = min(TOKEN_TILE, T)

    def body(x_ref, dh_ref, dr_ref, g_ref, sc_ref, dx_ref, dsh_ref, dsc_ref, dg_ref):
        i = pl.program_id(0)

        @pl.when(i == 0)
        def _():
            dsh_ref[...] = jnp.zeros_like(dsh_ref)
            dsc_ref[...] = jnp.zeros_like(dsc_ref)
            dg_ref[...] = jnp.zeros_like(dg_ref)

        xv = x_ref[...]
        dh = dh_ref[...]
        gv = g_ref[...]
        one_sc = 1.0 + sc_ref[...]
        inv = lax.rsqrt(jnp.mean(xv * xv, axis=-1, keepdims=True) + RMS_EPS)
        xn = xv * inv
        dxn = dh * (gv * one_sc)
        dx_ref[...] = dr_ref[...] + inv * (dxn - xn * jnp.mean(dxn * xn, axis=-1, keepdims=True))
        dhxn = dh * xn
        dsh_ref[...] += _rows8(dh)
        dsc_ref[...] += _rows8(dhxn * gv)
        dg_ref[...] += _rows8(dhxn * one_sc)

    row = pl.BlockSpec((tm, D), lambda i: (i, 0))
    acc = _full((SUBLANES, D))
    return pl.pallas_call(
        body, name=name, grid=(T // tm,),
        in_specs=[row, row, row, _full((1, D)), _full((1, D))], out_specs=[row, acc, acc, acc],
        out_shape=[jax.ShapeDtypeStruct((T, D), F32)] + [jax.ShapeDtypeStruct((SUBLANES, D), F32)] * 3,
        compiler_params=_params(("arbitrary",)),
    )(xin, dh, dres, g, sc)


def _norm2_gate_bwd(x1, dh2, dy, mix, g1, g, sc):
    T = x1.shape[0]
    tm = min(TOKEN_TILE, T)

    def body(x_ref, dh_ref, dr_ref, mix_ref, g1_ref, g_ref, sc_ref, dx_ref, dmix_ref, dsh_ref, dsc_ref, dg_ref,
             dg1_ref):
        @pl.when(pl.program_id(0) == 0)
        def _():
            for ref in (dsh_ref, dsc_ref, dg_ref, dg1_ref):
                ref[...] = jnp.zeros_like(ref)

        xv = x_ref[...]
        dh = dh_ref[...]
        gv = g_ref[...]
        one_sc = 1.0 + sc_ref[...]
        inv = lax.rsqrt(jnp.mean(xv * xv, axis=-1, keepdims=True) + RMS_EPS)
        xn = xv * inv
        dxn = dh * (gv * one_sc)
        dx = dr_ref[...] + inv * (dxn - xn * jnp.mean(dxn * xn, axis=-1, keepdims=True))
        dx_ref[...] = dx
        dmix_ref[...] = (dx * g1_ref[...]).astype(BF16)
        dhxn = dh * xn
        dsh_ref[...] += _rows8(dh)
        dsc_ref[...] += _rows8(dhxn * gv)
        dg_ref[...] += _rows8(dhxn * one_sc)
        dg1_ref[...] += _rows8(dx * mix_ref[...])

    row = pl.BlockSpec((tm, D), lambda i: (i, 0))
    vec, acc = _full((1, D)), _full((SUBLANES, D))
    return pl.pallas_call(
        body, name="norm2_gate_bwd", grid=(T // tm,),
        in_specs=[row, row, row, row, vec, vec, vec], out_specs=[row, row, acc, acc, acc, acc],
        out_shape=[jax.ShapeDtypeStruct((T, D), F32), jax.ShapeDtypeStruct((T, D), BF16)]
        + [jax.ShapeDtypeStruct((SUBLANES, D), F32)] * 4,
        compiler_params=_params(("arbitrary",)),
    )(x1, dh2, dy, mix, g1, g, sc)


def _out_merge_bwd(dmix, w_out_t, ya, yb, proj):
    T = ya.shape[0]
    tm = min(TOKEN_TILE, T)

    def body(dmix_ref, w_ref, ya_ref, yb_ref, ga_ref, gb_ref, dya_ref, dyb_ref, dga_ref, dgb_ref):
        dm = jnp.dot(dmix_ref[...], w_ref[...], preferred_element_type=F32)
        sa = _sigmoid(ga_ref[...])
        sb = _sigmoid(gb_ref[...])
        dya_ref[...] = (dm * sa).astype(BF16)
        dyb_ref[...] = (dm * sb).astype(BF16)
        dga_ref[...] = (dm * ya_ref[...].astype(F32) * sa * (1.0 - sa)).astype(BF16)
        dgb_ref[...] = (dm * yb_ref[...].astype(F32) * sb * (1.0 - sb)).astype(BF16)

    row = pl.BlockSpec((tm, D), lambda i: (i, 0))
    return pl.pallas_call(
        body, name="out_merge_bwd", grid=(T // tm,),
        in_specs=[row, _full((D, D)), row, row, pl.BlockSpec((tm, D), lambda i: (i, COL_GA // D)),
                  pl.BlockSpec((tm, D), lambda i: (i, COL_GB // D))],
        out_specs=[row] * 4, out_shape=[jax.ShapeDtypeStruct((T, D), BF16)] * 4,
        compiler_params=_params(("parallel",)),
    )(dmix, w_out_t, ya, yb, proj, proj)


def _conv_branch_bwd(proj, dya0, conv_w8):
    T = proj.shape[0]
    tm = min(FFN_TM, T)
    nt = T // tm

    def body(cb_ref, cc_ref, cv_ref, cbn_ref, ccp_ref, cvp_ref, ccn_ref, cvn_ref, d_ref, dn_ref, w_ref,
             dcb_ref, dcc_ref, dcv_ref, dw_ref):
        i = pl.program_id(0)

        @pl.when(i == 0)
        def _():
            dw_ref[...] = jnp.zeros_like(dw_ref)

        first, last = i == 0, i == nt - 1
        w = w_ref[...]
        cc = jnp.concatenate([ccp_ref[...], cc_ref[...], ccn_ref[...]], axis=0)
        cv = jnp.concatenate([cvp_ref[...], cv_ref[...], cvn_ref[...]], axis=0)
        rows = lax.broadcasted_iota(jnp.int32, cc.shape, 0)
        z = jnp.where(jnp.logical_and(first, rows < SUBLANES), 0.0, cc * cv)
        z1 = pltpu.roll(z, 1, axis=0)
        z2 = pltpu.roll(z, 2, axis=0)
        cz = z2 * w[0:1] + z1 * w[1:2] + z * w[2:3]
        zeros8 = jnp.zeros((SUBLANES, CONV_W), F32)
        de = jnp.concatenate([zeros8, d_ref[...], jnp.where(last, 0.0, dn_ref[...])], axis=0)
        cbe = jnp.concatenate([zeros8, cb_ref[...], cbn_ref[...]], axis=0)
        dcz = de * cbe
        n = tm + 2 * SUBLANES
        dz = dcz * w[2:3] + pltpu.roll(dcz, n - 1, axis=0) * w[1:2] + pltpu.roll(dcz, n - 2, axis=0) * w[0:1]
        inner = slice(SUBLANES, SUBLANES + tm)
        dcb_ref[...] = (de * cz)[inner].astype(BF16)
        dcc_ref[...] = (dz * cv)[inner].astype(BF16)
        dcv_ref[...] = (dz * cc)[inner].astype(BF16)
        wrows = [jnp.sum((dcz * t)[inner], axis=0, keepdims=True) for t in (z2, z1, z)]
        dw_ref[...] += jnp.concatenate(wrows + [jnp.zeros((SUBLANES - 3, CONV_W), F32)], axis=0)

    blk = lambda col: pl.BlockSpec((tm, CONV_W), lambda i: (i, col))
    out_blk = pl.BlockSpec((tm, CONV_W), lambda i: (i, 0))
    return pl.pallas_call(
        body, name="conv_branch_bwd", grid=(nt,),
        in_specs=[blk(0), blk(1), blk(2), _next_spec(tm, CONV_W, 0, nt),
                  _prev_spec(tm, CONV_W, 1), _prev_spec(tm, CONV_W, 2),
                  _next_spec(tm, CONV_W, 1, nt), _next_spec(tm, CONV_W, 2, nt),
                  out_blk, _next_spec(tm, CONV_W, 0, nt), _full((SUBLANES, CONV_W))],
        out_specs=[out_blk, out_blk, out_blk, _full((SUBLANES, CONV_W))],
        out_shape=[jax.ShapeDtypeStruct((T, CONV_W), BF16)] * 3 + [jax.ShapeDtypeStruct((SUBLANES, CONV_W), F32)],
        compiler_params=_params(("arbitrary",)),
    )(proj, proj, proj, proj, proj, proj, proj, proj, dya0, dya0, conv_w8)


def _qk_norm_bwd(proj, dqs, dkh, dlogf, qg, kg, bf_pad, gmat):
    T = proj.shape[0]
    tm = min(TOKEN_TILE, T)

    def body(q_ref, k_ref, f_ref, dqs_ref, dkh_ref, dlf_ref, qg_ref, kg_ref, bf_ref, g_ref,
             dq_ref, dk_ref, dfl_ref, dqg_ref, dkg_ref, dbf_ref):
        @pl.when(pl.program_id(0) == 0)
        def _():
            dqg_ref[...] = jnp.zeros_like(dqg_ref)
            dkg_ref[...] = jnp.zeros_like(dkg_ref)
            dbf_ref[...] = jnp.zeros_like(dbf_ref)

        gm = g_ref[...]
        for src, d_src, gain, scale, dst, dgain in (
                (q_ref, dqs_ref, qg_ref, 1.0 / np.sqrt(HEAD_DIM), dq_ref, dqg_ref),
                (k_ref, dkh_ref, kg_ref, 1.0, dk_ref, dkg_ref)):
            v = src[...]
            dhat = d_src[...] * scale
            inv = lax.rsqrt(_split_dot(v * v, gm, 2) * (1.0 / HEAD_DIM) + RMS_EPS)
            vn = v * inv
            dgain[...] += _rows8(dhat * vn)
            dvn = dhat * gain[...]
            mean = _split_dot(dvn * vn, gm, 2) * (1.0 / HEAD_DIM)
            dst[...] = (inv * (dvn - vn * mean)).astype(BF16)
        fx = f_ref[...] + bf_ref[...]
        dfl = dlf_ref[...] * _sigmoid(-fx)
        dfl_ref[...] = dfl.astype(BF16)
        dbf_ref[...] += _rows8(dfl)

    blk = lambda col: pl.BlockSpec((tm, ATTN_W), lambda i: (i, col))
    out_blk = pl.BlockSpec((tm, ATTN_W), lambda i: (i, 0))
    f_in = pl.BlockSpec((tm, F_PAD), lambda i: (i, COL_F // F_PAD))
    f_blk = pl.BlockSpec((tm, F_PAD), lambda i: (i, 0))
    return pl.pallas_call(
        body, name="qk_norm_bwd", grid=(T // tm,),
        in_specs=[blk(3), blk(4), f_in, out_blk, out_blk, f_blk, _full((1, ATTN_W)), _full((1, ATTN_W)),
                  _full((1, F_PAD)), _full((ATTN_W, ATTN_W))],
        out_specs=[out_blk, out_blk, f_blk, _full((SUBLANES, ATTN_W)), _full((SUBLANES, ATTN_W)),
                   _full((SUBLANES, F_PAD))],
        out_shape=[jax.ShapeDtypeStruct((T, ATTN_W), BF16)] * 2 + [jax.ShapeDtypeStruct((T, F_PAD), BF16)]
        + [jax.ShapeDtypeStruct((SUBLANES, ATTN_W), F32)] * 2 + [jax.ShapeDtypeStruct((SUBLANES, F_PAD), F32)],
        compiler_params=_params(("arbitrary",)),
    )(proj, proj, proj, dqs, dkh, dlogf, qg, kg, bf_pad, gmat)


def _pad_rows8(w):
    return jnp.pad(w, ((0, SUBLANES - w.shape[0]), (0, 0)))


def _fold8(acc):
    return jnp.sum(acc, axis=0, keepdims=True)


def _late_weights(mats):
    out = {}
    for name in ("w_branch_a", "w_out", "w_up", "w_down"):
        out[name] = mats[name]
        out[name + "_t"] = mats[name].T
    out["w_branch_b_heads"] = _pad_head_rows(mats["w_branch_b"])
    out["w_branch_b_heads_t"] = out["w_branch_b_heads"].T
    return out


def _local_step(x, target, mod, wts, late=None):
    T = x.shape[0]
    tb = min(MATMUL_TILE, T)
    tk_long = min(2 * MATMUL_TILE, T)
    tm = min(TOKEN_TILE, T)
    sh1, sc1, g1, sh2, sc2, g2 = [mod[:, i * D:(i + 1) * D] for i in range(N_MOD)]
    w_in, w_in_t = wts["w_in"], wts["w_in_t"]
    conv_a8 = _pad_rows8(wts["conv_a_w"])
    conv_f8 = _pad_rows8(wts["conv_ffn_w"])
    qg = jnp.tile(wts["q_norm_g"], (1, N_HEADS))
    kg = jnp.tile(wts["k_norm_g"], (1, N_HEADS))
    bf_pad = jnp.pad(wts["b_f"], ((0, 0), (0, F_PAD - N_HEADS)))
    gmat = _group_matrix()

    h = _norm_mod(x, wts["norm1_g"], sc1, sh1, name="norm1_fwd")
    proj = _matmul(h, w_in, name="mm_in", tm=tb, tn=896, tk=D)
    fcum = _cumsum(proj, reverse=False, name="gate_cumsum", col=COL_F // F_PAD, gate_bias=bf_pad)
    ya0, qa, ka, va = _branch_prep(proj, fcum, conv_a8, qg, kg, gmat)
    if late is None:
        o_h, qb = _attn_fwd(qa, ka, va)
    else:
        o_h, qb, *gathered = _attn_fwd(qa, ka, va, _Exchange([late[name] for name, *_ in LATE], scatter=False))
        wts = dict(wts)
        mats = {name: _join_shards(g, axis) for (name, _, _, axis), g in zip(LATE, gathered) if name != "w_up"}
        mats["w_up"] = _assemble_columns(gathered[[name for name, *_ in LATE].index("w_up")], 2 * D_FF // N_DEV,
                                         2 * D_FF, ((0, 2 * D_FF, 0),), name="assemble_w_up")
        wts.update(_late_weights(mats))
    ya, yb, merged = _branch_merge_fwd(ya0, o_h, proj, wts["w_branch_a"], wts["w_branch_b_heads"])
    mix, x1, h2 = _out_resid_norm(x, merged, wts["w_out"], g1, wts["norm2_g"], sc2, sh2)
    u = _matmul(h2, wts["w_up"], name="mm_up", tm=tb, tn=1408, tk=D)
    act, conv_gate, conv_val = _ffn_act_fwd(u, conv_f8)
    dy, dff, sq8, dg2_8 = _down_loss_head(x1, act, wts["w_down"], g2, target)
    sq = jnp.sum(sq8).reshape(1, 1)

    grads = {}
    da = _matmul(dff, wts["w_down_t"], name="mm_down_dx", tm=tb, tn=1408, tk=D, out_dtype=BF16)
    grads["w_down"] = _matmul(act, dff, name="mm_down_dw", tm=1408, tn=D, tk=tk_long, trans_a=True)
    dug, duv, dconv_f8 = _ffn_act_bwd(u, conv_gate, conv_val, da, conv_f8)
    grads["conv_ffn_w"] = dconv_f8[:3]
    dh2 = _matmul_pieces([dug, duv], wts["w_up_t"], name="mm_up_dx", tm=tm)
    dw_up = [_matmul(h2, d, name="mm_up_dw_" + half, tm=D, tn=1408, tk=tk_long, trans_a=True)
             for half, d in (("gate", dug), ("val", duv))]
    if late is None:
        grads["w_up"] = jnp.concatenate(dw_up, axis=1)
    dx1, dmix, dsh2_8, dsc2_8, dn2_8, dg1_8 = _norm2_gate_bwd(x1, dh2, dy, mix, g1, wts["norm2_g"], sc2)
    grads["norm2_g"] = _fold8(dn2_8)

    grads["w_out"] = _matmul(merged, dmix, name="mm_out_dw", tm=D, tn=D, tk=tk_long, trans_a=True)
    dya, dyb, dga, dgb = _out_merge_bwd(dmix, wts["w_out_t"], ya, yb, proj)
    dya0 = _matmul(dya, wts["w_branch_a_t"], name="mm_branch_a_dx", tm=tb, tn=CONV_W, tk=D)
    grads["w_branch_a"] = _matmul(ya0, dya, name="mm_branch_a_dw", tm=CONV_W, tn=D, tk=tk_long, trans_a=True)
    doa = _branch_b_bwd(dyb, o_h, wts["w_branch_b_heads_t"])
    grads["w_branch_b"] = _branch_b_dw(o_h, dyb)[:, :HEAD_DIM].reshape(ATTN_W, D)
    dcb, dcc, dcv, dconv_a8 = _conv_branch_bwd(proj, dya0, conv_a8)
    grads["conv_a_w"] = dconv_a8[:3]

    parts = {}
    if late is None:
        dq_h, dk_h, dv_h = _attn_bwd(qb, ka, va, doa)
    else:
        ready = [(_column_shards(dw_up) if name == "w_up" else _split_shards(grads[name], axis)).astype(BF16)
                 for name, _, _, axis in LATE]
        dq_h, dk_h, dv_h, *recv = _attn_bwd(
            qb, ka, va, doa, _Exchange(ready + [_pack_full_by_dest(grads, CONVS, SUBLANES)], scatter=True))
        parts = dict(zip([name for name, *_ in LATE] + ["conv"], recv))
    dq_tok, dk_tok, dv_tok, dfcum = _attn_unpack(dq_h, dk_h, dv_h)
    dlogf = _cumsum(dfcum, reverse=True, name="gate_cumsum_bwd")
    dq, dk, dfl, dqg8, dkg8, dbf8 = _qk_norm_bwd(proj, dq_tok, dk_tok, dlogf, qg, kg, bf_pad, gmat)
    grads["q_norm_g"] = jnp.sum(_fold8(dqg8).reshape(N_HEADS, HEAD_DIM), axis=0, keepdims=True)
    grads["k_norm_g"] = jnp.sum(_fold8(dkg8).reshape(N_HEADS, HEAD_DIM), axis=0, keepdims=True)
    grads["b_f"] = _fold8(dbf8)[:, :N_HEADS]
    narrow, wide = [dcb, dcc, dcv, dq, dk, dv_tok], [dga, dgb, dfl]
    dw_narrow = _matmul_tn_pieces(h, narrow, name="mm_in_dw_narrow", tk=tb)
    dwa, dwb, dwf = _matmul_tn_pieces(h, wide, name="mm_in_dw_wide", tk=tk_long)
    dw_in = list(dw_narrow) + [dwf[:, :N_HEADS], dwa, dwb]
    if late is None:
        grads["w_in"] = jnp.concatenate(dw_in, axis=1)
        dh = _matmul_pieces(narrow + wide, w_in_t, name="mm_in_dx", tm=tm)
    else:
        dh, parts["w_in"] = _matmul_pieces(
            narrow + wide, w_in_t, name="mm_in_dx", tm=tm,
            exchange=_Exchange([_column_shards(dw_in).astype(BF16)], scatter=True))
    grad_x, dsh1_8, dsc1_8, dn1_8 = _norm_bwd(x, dh, dx1, wts["norm1_g"], sc1, name="norm1_bwd")
    grads["norm1_g"] = _fold8(dn1_8)
    grads["mod"] = jnp.concatenate([_fold8(a) for a in (dsh1_8, dsc1_8, dg1_8, dsh2_8, dsc2_8, dg2_8)], axis=1)
    return sq, grad_x, grads, parts


def _me_and_peers():
    mx, my, mc = lax.axis_index("x"), lax.axis_index("y"), lax.axis_index("c")
    me = 4 * mx + 2 * my + mc
    peers = []
    for k in range(1, N_DEV):
        px = 1 - mx if k & 4 else mx
        py = 1 - my if k & 2 else my
        pc = 1 - mc if k & 1 else mc
        peers.append(((px, py, pc), 4 * px + 2 * py + pc))
    return me, peers


HBM_SPEC = pl.BlockSpec(memory_space=pltpu.HBM)


class _Exchange:
    def __init__(self, xs, scatter):
        self.xs, self.scatter, self.n = list(xs), scatter, len(xs)
        self.out_shapes = [jax.ShapeDtypeStruct(x.shape if scatter else (N_DEV,) + x.shape, x.dtype) for x in xs]
        self.in_specs = [HBM_SPEC] * self.n
        self.out_specs = [HBM_SPEC] * self.n
        self.scratch = [pltpu.SemaphoreType.DMA((self.n, N_DEV - 1)), pltpu.SemaphoreType.DMA((self.n, N_DEV - 1)),
                        pltpu.SemaphoreType.DMA((self.n,))]

    def _copies(self, x_refs, out_refs, sems):
        send_sems, recv_sems, local_sems = sems
        me, peers = _me_and_peers()

        def src(a, idx):
            return x_refs[a].at[idx] if self.scatter else x_refs[a]

        def copy(a, k, from_idx, to_slot, device):
            return pltpu.make_async_remote_copy(
                src_ref=src(a, from_idx), dst_ref=out_refs[a].at[to_slot], send_sem=send_sems.at[a, k],
                recv_sem=recv_sems.at[a, k], device_id=device, device_id_type=MESH)

        local = [pltpu.make_async_copy(src(a, me), out_refs[a].at[me], local_sems.at[a]) for a in range(self.n)]
        sends = [copy(a, k, idx, me, dev) for a in range(self.n) for k, (dev, idx) in enumerate(peers)]
        recvs = [copy(a, k, idx, idx, dev) for a in range(self.n) for k, (dev, idx) in enumerate(peers)]
        return local, sends, recvs

    def start(self, x_refs, out_refs, sems):
        local, sends, _ = self._copies(x_refs, out_refs, sems)
        for cp in local + sends:
            cp.start()

    def wait(self, x_refs, out_refs, sems):
        local, sends, recvs = self._copies(x_refs, out_refs, sems)
        for cp in recvs:
            cp.wait_recv()
        for cp in sends:
            cp.wait_send()
        for cp in local:
            cp.wait()

    def split(self, refs, n_in, n_out):
        n = self.n
        ins, xin = refs[:n_in], refs[n_in:n_in + n]
        outs, xout = refs[n_in + n:n_in + n + n_out], refs[n_in + n + n_out:n_in + 2 * n + n_out]
        rest = refs[n_in + 2 * n + n_out:]
        return ins, outs, rest[:len(rest) - 3], (xin, xout, rest[len(rest) - 3:])


def _ride(exchange, first, last, refs):
    if exchange is None:
        return

    @pl.when(first)
    def _():
        exchange.start(*refs)

    @pl.when(last)
    def _():
        exchange.wait(*refs)


def _gather_two_level(xs, *, name):
    n = len(xs)
    out_shapes = [jax.ShapeDtypeStruct((N_DEV,) + x.shape, x.dtype) for x in xs]

    def body(*refs):
        x_refs, out_refs = refs[:n], refs[n:2 * n]
        send_sems, recv_sems, local_sems = refs[2 * n:]
        x, y, c = lax.axis_index("x"), lax.axis_index("y"), lax.axis_index("c")
        me, sibling = (x, y, c), (x, y, 1 - c)
        chips = [(1 - x, y), (x, 1 - y), (1 - x, 1 - y)]

        def slot(a, dev):
            return out_refs[a].at[4 * dev[0] + 2 * dev[1] + dev[2]]

        def copy(a, k, block, to, src=None):
            return pltpu.make_async_remote_copy(
                src_ref=slot(a, block) if src is None else src, dst_ref=slot(a, block),
                send_sem=send_sems.at[a, k], recv_sem=recv_sems.at[a, k], device_id=to, device_id_type=MESH)

        mine = [pltpu.make_async_copy(x_refs[a], slot(a, me), local_sems.at[a]) for a in range(n)]
        first = [copy(a, 0, me, sibling, src=x_refs[a]) for a in range(n)]
        first += [copy(a, 1 + j, me, (*chip, c), src=x_refs[a]) for a in range(n) for j, chip in enumerate(chips)]
        for cp in mine + first:
            cp.start()
        passed = []
        for a in range(n):
            for j, chip in enumerate(chips):
                copy(a, 1 + j, (*chip, c), me).wait_recv()
                passed.append(copy(a, 4 + j, (*chip, c), sibling))
                passed[-1].start()
        for a in range(n):
            copy(a, 0, sibling, me).wait_recv()
            for j, chip in enumerate(chips):
                copy(a, 4 + j, (*chip, 1 - c), me).wait_recv()
        for cp in first + passed:
            cp.wait_send()
        for cp in mine:
            cp.wait()

    return pl.pallas_call(
        body, name=name, in_specs=[HBM_SPEC] * n, out_specs=[HBM_SPEC] * n, out_shape=out_shapes,
        scratch_shapes=[pltpu.SemaphoreType.DMA((n, N_DEV - 1)), pltpu.SemaphoreType.DMA((n, N_DEV - 1)),
                        pltpu.SemaphoreType.DMA((n,))],
        compiler_params=pltpu.CompilerParams(has_side_effects=True),
    )(*xs)


def _exchange(xs, *, name, scatter):
    ex = _Exchange(xs, scatter)

    def body(*refs):
        _, _, _, xrefs = ex.split(refs, 0, 0)
        ex.start(*xrefs)
        ex.wait(*xrefs)

    return pl.pallas_call(
        body, name=name, in_specs=ex.in_specs, out_specs=ex.out_specs, out_shape=ex.out_shapes,
        scratch_shapes=ex.scratch, compiler_params=pltpu.CompilerParams(has_side_effects=True),
    )(*xs)


def _ada_fwd(c_all, w_shard, b_shard):
    n = w_shard.shape[1]

    def body(c_ref, w_ref, b_ref, o_ref):
        cv = c_ref[...]
        act = (cv * _sigmoid(cv)).astype(BF16)
        o_ref[...] = jnp.dot(act, w_ref[...].astype(BF16), preferred_element_type=F32) + b_ref[...]

    return pl.pallas_call(
        body, name="ada_fwd", in_specs=[_full((N_DEV, D)), _full((D, n)), _full((1, n))],
        out_specs=_full((N_DEV, n)), out_shape=jax.ShapeDtypeStruct((N_DEV, n), F32), grid=(1,),
        compiler_params=_params(("arbitrary",)),
    )(c_all, w_shard, b_shard)


def _ada_bwd(c_all_t, dmod_pad):
    n = dmod_pad.shape[1]

    def body(c_ref, d_ref, o_ref):
        cv = c_ref[...]
        act = (cv * _sigmoid(cv)).astype(BF16)
        o_ref[...] = jnp.dot(act, d_ref[...].astype(BF16), preferred_element_type=F32)

    return pl.pallas_call(
        body, name="ada_bwd", in_specs=[_full((D, LANES)), _full((LANES, n))],
        out_specs=_full((D, n)), out_shape=jax.ShapeDtypeStruct((D, n), F32), grid=(1,),
        compiler_params=_params(("arbitrary",)),
    )(c_all_t, dmod_pad)


ADAM_ROWS = 256


def _adamw(parts, w, m, v, *, name):
    n, R, C = parts.shape
    tr = next((t for t in (ADAM_ROWS, 128, 64, 32, 16, SUBLANES) if R % t == 0), R)

    def body(p_ref, w_ref, m_ref, v_ref, g_ref, d_ref, nm_ref, nv_ref):
        g = p_ref[0].astype(F32)
        for j in range(1, n):
            g = g + p_ref[j].astype(F32)
        g_ref[...] = g
        nm = ADAM_B1 * m_ref[...] + (1.0 - ADAM_B1) * g
        nv = ADAM_B2 * v_ref[...] + (1.0 - ADAM_B2) * (g * g)
        nm_ref[...] = nm
        nv_ref[...] = nv
        m_hat = nm / (1.0 - ADAM_B1 ** ADAM_STEP)
        v_hat = nv / (1.0 - ADAM_B2 ** ADAM_STEP)
        d_ref[...] = -ADAM_LR * (m_hat / (jnp.sqrt(v_hat) + ADAM_EPS) + ADAM_WD * w_ref[...])

    row = pl.BlockSpec((tr, C), lambda i: (i, 0))
    return pl.pallas_call(
        body, name=name, grid=(R // tr,),
        in_specs=[pl.BlockSpec((n, tr, C), lambda i: (0, i, 0)), row, row, row], out_specs=[row] * 4,
        out_shape=[jax.ShapeDtypeStruct((R, C), F32)] * 4,
        compiler_params=_params(("parallel",)),
    )(parts, w, m, v)


SHARDED = (("w_in", D, IN_W, 1), ("w_branch_a", CONV_W, D, 1), ("w_branch_b", ATTN_W, D, 1), ("w_out", D, D, 0),
           ("w_up", D, 2 * D_FF, 1), ("w_down", D_FF, D, 0), ("conv_a_w", 3, CONV_W, 1),
           ("conv_ffn_w", 3, 2 * D_FF, 1))
MATRICES = SHARDED[:6]
LATE = MATRICES[1:]
CONVS = SHARDED[6:]
REPLICATED = (("b_ada", N_MOD * D), ("norm1_g", D), ("norm2_g", D), ("b_f", N_HEADS), ("q_norm_g", HEAD_DIM),
              ("k_norm_g", HEAD_DIM))


def _shard_shape(rows, cols, axis):
    return (rows // N_DEV, cols) if axis == 0 else (rows, cols // N_DEV)


def _pack_rows(flat, multiple):
    length = flat.shape[-1]
    rows = -(-length // PACK_W)
    rows = -(-rows // multiple) * multiple
    pad = [(0, 0)] * (flat.ndim - 1) + [(0, rows * PACK_W - length)]
    return jnp.pad(flat, pad).reshape(flat.shape[:-1] + (rows, PACK_W))


def _pack_shards(shards, spec, multiple, dtype):
    flat = jnp.concatenate([shards[name].reshape(-1).astype(dtype) for name, *_ in spec])
    return _pack_rows(flat, multiple)


def _join_lane_blocks(gathered):
    n, r, c = gathered.shape

    def body(g_ref, o_ref):
        for j in range(n):
            o_ref[:, j * c:(j + 1) * c] = g_ref[j]

    return pl.pallas_call(
        body, name="join_lane_blocks", grid=(1,), in_specs=[_full((n, r, c))], out_specs=_full((r, n * c)),
        out_shape=jax.ShapeDtypeStruct((r, n * c), gathered.dtype), compiler_params=_params(("arbitrary",)),
    )(gathered)


SHARD_PAD = 768


def _assemble_columns(gathered, shard_cols, out_cols, segments, *, name):
    n, rows, padw = gathered.shape
    assert n == N_DEV and padw == SHARD_PAD and shard_cols <= SHARD_PAD

    def body(g_ref, o_ref):
        j = pl.program_id(0)

        @pl.when(j == 0)
        def _():
            o_ref[...] = jnp.zeros_like(o_ref)

        for dev in range(N_DEV):
            @pl.when(j == dev)
            def _(dev=dev):
                x = g_ref[0]
                for lo, hi, delta in segments:
                    a, b = max(lo, dev * shard_cols), min(hi, (dev + 1) * shard_cols)
                    if a >= b:
                        continue
                    base = (a + delta) // LANES * LANES
                    width = -(-(b + delta - base) // LANES) * LANES
                    src = lax.broadcasted_iota(jnp.int32, (padw, width), 0) + dev * shard_cols
                    dst = lax.broadcasted_iota(jnp.int32, (padw, width), 1) + (base - delta)
                    place = jnp.where((src == dst) & (src >= a) & (src < b), 1.0, 0.0).astype(BF16)
                    moved = jnp.dot(x, place, preferred_element_type=F32).astype(BF16)
                    o_ref[:, base:base + width] = o_ref[:, base:base + width] + moved

    return pl.pallas_call(
        body, name=name, grid=(N_DEV,), in_specs=[pl.BlockSpec((1, rows, padw), lambda j: (j, 0, 0))],
        out_specs=_full((rows, out_cols)), out_shape=jax.ShapeDtypeStruct((rows, out_cols), BF16),
        compiler_params=_params(("arbitrary",)),
    )(gathered)


def _pad_shard(w):
    return jnp.pad(w.astype(BF16), ((0, 0), (0, SHARD_PAD - w.shape[1])))


W_IN_SEGMENTS = ((0, COL_GA, 0), (COL_GA, COL_GA + N_HEADS, COL_F - COL_GA), (COL_GA + N_HEADS, IN_W, -N_HEADS))


def _join_shards(gathered, axis):
    if axis == 0:
        return gathered.reshape(N_DEV * gathered.shape[1], gathered.shape[2])
    if gathered.shape[2] == LANES:
        return _join_lane_blocks(gathered)
    return jnp.concatenate([gathered[j] for j in range(N_DEV)], axis=1)


def _column_shards(pieces):
    total = sum(p.shape[1] for p in pieces)
    width = total // N_DEV
    shards = []
    for j in range(N_DEV):
        lo, hi, off, segs = j * width, (j + 1) * width, 0, []
        for p in pieces:
            a, b = max(lo, off), min(hi, off + p.shape[1])
            if a < b:
                segs.append(p[:, a - off:b - off])
            off += p.shape[1]
        shards.append(segs[0] if len(segs) == 1 else jnp.concatenate(segs, axis=1))
    return jnp.stack(shards)


def _split_shards(full, axis):
    if axis == 0:
        return full.reshape(N_DEV, full.shape[0] // N_DEV, full.shape[1])
    c = full.shape[1] // N_DEV
    return jnp.stack([full[:, j * c:(j + 1) * c] for j in range(N_DEV)])


def _unpack_shards(packed, spec):
    flat = packed.reshape(-1)
    out, off = {}, 0
    for name, rows, cols, axis in spec:
        r, c = _shard_shape(rows, cols, axis)
        out[name] = flat[off:off + r * c].reshape(r, c)
        off += r * c
    return out


def _unpack_gathered(gathered, spec):
    flat = gathered.reshape(N_DEV, -1)
    out, off = {}, 0
    for name, rows, cols, axis in spec:
        r, c = _shard_shape(rows, cols, axis)
        seg = flat[:, off:off + r * c].reshape(N_DEV, r, c)
        out[name] = seg.reshape(rows, cols) if axis == 0 else seg.transpose(1, 0, 2).reshape(rows, cols)
        off += r * c
    return out


def _pack_full_by_dest(full, spec, multiple):
    segs = []
    for name, rows, cols, axis in spec:
        r, c = _shard_shape(rows, cols, axis)
        a = full[name]
        seg = a.reshape(N_DEV, r, c) if axis == 0 else a.reshape(rows, N_DEV, c).transpose(1, 0, 2)
        segs.append(seg.reshape(N_DEV, r * c))
    return _pack_rows(jnp.concatenate(segs, axis=1), multiple)


def _pad_head_rows(w):
    n = w.shape[1]
    padded = jnp.pad(w.reshape(N_HEADS, HEAD_DIM, n), ((0, 0), (0, LANES - HEAD_DIM), (0, 0)))
    return padded.reshape(N_HEADS * LANES, n)


def kernel(x, c, w_ada, b_ada, norm1_g, w_in, b_f, conv_a_w, q_norm_g, k_norm_g, w_branch_a, w_branch_b, w_out, norm2_g, w_up, conv_ffn_w, w_down, loss_target, m_w_ada, m_b_ada, m_norm1_g, m_w_in, m_b_f, m_conv_a_w, m_q_norm_g, m_k_norm_g, m_w_branch_a, m_w_branch_b, m_w_out, m_norm2_g, m_w_up, m_conv_ffn_w, m_w_down, v_w_ada, v_b_ada, v_norm1_g, v_w_in, v_b_f, v_conv_a_w, v_q_norm_g, v_k_norm_g, v_w_branch_a, v_w_branch_b, v_w_out, v_norm2_g, v_w_up, v_conv_ffn_w, v_w_down):
    names = ("w_ada", "b_ada", "norm1_g", "w_in", "b_f", "conv_a_w", "q_norm_g", "k_norm_g", "w_branch_a",
             "w_branch_b", "w_out", "norm2_g", "w_up", "conv_ffn_w", "w_down")
    squeeze = lambda a: a[0] if a.ndim == 3 else a
    W = dict(zip(names, map(squeeze, (w_ada, b_ada, norm1_g, w_in, b_f, conv_a_w, q_norm_g, k_norm_g, w_branch_a,
                                      w_branch_b, w_out, norm2_g, w_up, conv_ffn_w, w_down))))
    M = dict(zip(names, map(squeeze, (m_w_ada, m_b_ada, m_norm1_g, m_w_in, m_b_f, m_conv_a_w, m_q_norm_g,
                                      m_k_norm_g, m_w_branch_a, m_w_branch_b, m_w_out, m_norm2_g, m_w_up,
                                      m_conv_ffn_w, m_w_down))))
    V = dict(zip(names, map(squeeze, (v_w_ada, v_b_ada, v_norm1_g, v_w_in, v_b_f, v_conv_a_w, v_q_norm_g,
                                      v_k_norm_g, v_w_branch_a, v_w_branch_b, v_w_out, v_norm2_g, v_w_up,
                                      v_conv_ffn_w, v_w_down))))
    me = 4 * lax.axis_index("x") + 2 * lax.axis_index("y") + lax.axis_index("c")
    ada_n = N_MOD * D // N_DEV

    small = jnp.concatenate([c.reshape(-1), W["conv_a_w"].reshape(-1), W["conv_ffn_w"].reshape(-1)])
    small_all, w_in_all = _gather_two_level([_pack_rows(small, SUBLANES), _pad_shard(W["w_in"])],
                                            name="gather_first")
    small_all = small_all.reshape(N_DEV, -1)
    c_all = small_all[:, :D]
    conv_all = _unpack_gathered(small_all[:, D:], CONVS)

    b_shard = lax.dynamic_slice(W["b_ada"], (0, me * ada_n), (1, ada_n))
    mod_part = _ada_fwd(c_all, W["w_ada"], b_shard)
    mod_all, = _exchange([mod_part], name="gather_mod", scatter=False)
    mod = lax.dynamic_index_in_dim(mod_all, me, axis=1, keepdims=False).reshape(1, N_MOD * D)

    wts = {"w_in": _assemble_columns(w_in_all, IN_W // N_DEV, IN_W_PAD, W_IN_SEGMENTS, name="assemble_w_in")}
    wts["w_in_t"] = wts["w_in"].T
    wts.update(conv_all)
    for name in ("norm1_g", "norm2_g", "q_norm_g", "k_norm_g", "b_f"):
        wts[name] = W[name]
    late = {name: _pad_shard(W[name]) if name == "w_up" else W[name].astype(BF16) for name, *_ in LATE}

    sq, grad_x, grads, parts = _local_step(x[0], loss_target[0], mod, wts, late)
    loss = lax.psum(sq[0, 0] * (0.5 / D), AXES)

    grads["b_ada"] = grads["mod"]
    rep_flat = lambda src: jnp.concatenate([src[name].reshape(-1) for name, _ in REPLICATED])
    rep_parts, = _exchange([_pack_rows(rep_flat(grads), 16)], name="gather_small_grads", scatter=False)
    rep_out = _adamw(rep_parts, *[_pack_rows(rep_flat(s), 16) for s in (W, M, V)], name="adamw_replicated")

    dmod_all = rep_parts.reshape(N_DEV, -1)[:, :N_MOD * D]
    dmod_mine = lax.dynamic_slice(dmod_all, (0, me * ada_n), (N_DEV, ada_n))
    g_ada = _ada_bwd(jnp.pad(c_all.T, ((0, 0), (0, LANES - N_DEV))),
                     jnp.pad(dmod_mine, ((0, LANES - N_DEV), (0, 0))))
    ada_out = _adamw(g_ada[None], W["w_ada"], M["w_ada"], V["w_ada"], name="adamw_ada")

    mat_out = {name: _adamw(parts[name], W[name], M[name], V[name], name="adamw_" + name) for name, *_ in MATRICES}
    conv_out = _adamw(parts["conv"], *[_pack_shards(s, CONVS, SUBLANES, F32) for s in (W, M, V)],
                      name="adamw_conv")

    results = []
    for kind in range(4):
        per = {"w_ada": ada_out[kind]}
        per.update({name: out[kind] for name, out in mat_out.items()})
        per.update(_unpack_shards(conv_out[kind], CONVS))
        flat, off = rep_out[kind].reshape(-1), 0
        for name, n in REPLICATED:
            per[name] = flat[off:off + n].reshape(1, n)
            off += n
        results.append(per)
    restore = lambda name, a: a[None] if W[name].ndim == 2 and name not in dict(REPLICATED) else a
    outs = [loss, grad_x[None]]
    for per in results:
        outs.extend(restore(name, per[name]) for name in names)
    return tuple(outs)
```

```python
import functools

import jax
import jax.numpy as jnp
import numpy as np
from jax import lax
from jax.experimental import pallas as pl
from jax.experimental.pallas import tpu as pltpu

F32 = jnp.float32
BF16 = jnp.bfloat16

N_DEV = 8
D = 1024
N_HEADS = 8
HEAD_DIM = 64
ATTN_W = 512
CONV_W = 512
D_FF = 2816
N_MOD = 6
IN_W = 5128
RMS_EPS = 1e-6
NEG_INF = -1e30

IN_W_PAD = 5376
COL_GA = 3072
COL_GB = 4096
COL_F = 5120
F_PAD = 128

ADAM_LR = 0.001
ADAM_B1 = 0.9
ADAM_B2 = 0.999
ADAM_EPS = 1e-08
ADAM_WD = 0.01
ADAM_STEP = 10

LANES = 128
SUBLANES = 8
BF16_ROWS = 16
VMEM_LIMIT = 52 * 1024 * 1024
TOKEN_TILE = 512
MATMUL_TILE = 1024
ATTN_BLOCK = 512
ATTN_FWD_HEADS = 1
PACK_W = 1024

MESH = pl.DeviceIdType.MESH
AXES = ("x", "y", "c")


def _params(sem=None, **kw):
    return pltpu.CompilerParams(dimension_semantics=sem, vmem_limit_bytes=VMEM_LIMIT, **kw)


def _full(shape):
    nd = len(shape)
    return pl.BlockSpec(shape, lambda *_: (0,) * nd)


def _tn_dot(a, b):
    return lax.dot_general(a, b, (((0,), (0,)), ((), ())), preferred_element_type=F32)


def _matmul(a, b, *, name, tm, tn, tk, out_dtype=F32, trans_a=False, exchange=None):
    if trans_a:
        K, M = a.shape
    else:
        M, K = a.shape
    N = b.shape[1]
    assert b.shape[0] == K and M % tm == 0 and N % tn == 0 and K % tk == 0, (name, a.shape, b.shape)
    nm, nn, nk = M // tm, N // tn, K // tk

    def body(*refs):
        if exchange is None:
            a_ref, b_ref, o_ref, *own = refs
        else:
            (a_ref, b_ref), (o_ref,), own, xrefs = exchange.split(refs, 2, 1)
            ids = [pl.program_id(d) for d in range(3)]
            first = jnp.logical_and(jnp.logical_and(ids[0] == 0, ids[1] == 0), ids[2] == 0)
            last = jnp.logical_and(jnp.logical_and(ids[0] == nn - 1, ids[1] == nm - 1), ids[2] == nk - 1)
            _ride(exchange, first, last, xrefs)
        k = pl.program_id(2)
        av = a_ref[...].astype(BF16)
        bv = b_ref[...].astype(BF16)
        prod = _tn_dot(av, bv) if trans_a else jnp.dot(av, bv, preferred_element_type=F32)
        if nk == 1:
            o_ref[...] = prod.astype(out_dtype)
            return
        acc_ref, = own

        @pl.when(k == 0)
        def _():
            acc_ref[...] = prod

        @pl.when(k > 0)
        def _():
            acc_ref[...] += prod

        @pl.when(k == nk - 1)
        def _():
            o_ref[...] = acc_ref[...].astype(out_dtype)

    if trans_a:
        a_spec = pl.BlockSpec((tk, tm), lambda j, i, k: (k, i))
    else:
        a_spec = pl.BlockSpec((tm, tk), lambda j, i, k: (i, k))
    in_specs = [a_spec, pl.BlockSpec((tk, tn), lambda j, i, k: (k, j))]
    out_spec = pl.BlockSpec((tm, tn), lambda j, i, k: (i, j))
    out_shape = jax.ShapeDtypeStruct((M, N), out_dtype)
    scratch = [pltpu.VMEM((tm, tn), F32)] if nk > 1 else []
    if exchange is None:
        return pl.pallas_call(
            body, name=name, grid=(nn, nm, nk), in_specs=in_specs, out_specs=out_spec, out_shape=out_shape,
            scratch_shapes=scratch, compiler_params=_params(("parallel", "parallel", "arbitrary")),
        )(a, b)
    return pl.pallas_call(
        body, name=name, grid=(nn, nm, nk), in_specs=in_specs + exchange.in_specs,
        out_specs=[out_spec] + exchange.out_specs, out_shape=[out_shape] + exchange.out_shapes,
        scratch_shapes=scratch + exchange.scratch, compiler_params=_params(("arbitrary",) * 3),
    )(a, b, *exchange.xs)


def _matmul_pieces(pieces, b, *, name, tm, exchange=None):
    M = pieces[0].shape[0]
    widths = [p.shape[1] for p in pieces]
    offsets = [sum(widths[:i]) for i in range(len(widths))]
    N = b.shape[1]
    assert b.shape[0] >= sum(widths) and M % tm == 0, (name, widths, b.shape)
    n_p, nm = len(pieces), M // tm

    def body(*refs):
        if exchange is None:
            ins, o_ref = refs[:n_p + 1], refs[n_p + 1]
        else:
            ins, (o_ref,), _, xrefs = exchange.split(refs, n_p + 1, 1)
            i = pl.program_id(0)
            _ride(exchange, i == 0, i == nm - 1, xrefs)
        b_ref = ins[n_p]
        acc = None
        for a_ref, off, w in zip(ins[:n_p], offsets, widths):
            term = jnp.dot(a_ref[...].astype(BF16), b_ref[off:off + w, :], preferred_element_type=F32)
            acc = term if acc is None else acc + term
        o_ref[...] = acc

    in_specs = [pl.BlockSpec((tm, w), lambda i: (i, 0)) for w in widths] + [_full(b.shape)]
    out_spec = pl.BlockSpec((tm, N), lambda i: (i, 0))
    out_shape = jax.ShapeDtypeStruct((M, N), F32)
    if exchange is None:
        return pl.pallas_call(
            body, name=name, grid=(nm,), in_specs=in_specs, out_specs=out_spec, out_shape=out_shape,
            compiler_params=_params(("parallel",)),
        )(*pieces, b)
    return pl.pallas_call(
        body, name=name, grid=(nm,), in_specs=in_specs + exchange.in_specs,
        out_specs=[out_spec] + exchange.out_specs, out_shape=[out_shape] + exchange.out_shapes,
        scratch_shapes=exchange.scratch, compiler_params=_params(("arbitrary",)),
    )(*pieces, b, *exchange.xs)


def _matmul_tn_pieces(a, pieces, *, name, tk):
    K, M = a.shape
    widths = [p.shape[1] for p in pieces]
    n_p, nk = len(pieces), K // tk

    def body(*refs):
        a_ref, p_refs, o_refs = refs[0], refs[1:n_p + 1], refs[n_p + 1:]
        k = pl.program_id(0)
        av = a_ref[...].astype(BF16)
        for p_ref, o_ref in zip(p_refs, o_refs):
            prod = _tn_dot(av, p_ref[...].astype(BF16))

            @pl.when(k == 0)
            def _():
                o_ref[...] = prod

            @pl.when(k > 0)
            def _():
                o_ref[...] += prod

    return pl.pallas_call(
        body, name=name, grid=(nk,),
        in_specs=[pl.BlockSpec((tk, M), lambda k: (k, 0))] + [pl.BlockSpec((tk, w), lambda k: (k, 0)) for w in widths],
        out_specs=[_full((M, w)) for w in widths], out_shape=[jax.ShapeDtypeStruct((M, w), F32) for w in widths],
        compiler_params=_params(("arbitrary",)),
    )(a, *pieces)


def _split_dot(x, mat, parts):
    out = None
    rem = x
    for p in range(parts):
        piece = rem.astype(BF16)
        term = jnp.dot(piece, mat, preferred_element_type=F32)
        out = term if out is None else out + term
        if p + 1 < parts:
            rem = rem - piece.astype(F32)
    return out


def _sigmoid(x):
    return 1.0 / (1.0 + jnp.exp(-x))


def _rows8(x):
    r, c = x.shape
    return jnp.sum(x.reshape(r // SUBLANES, SUBLANES, c), axis=0)


def _shift_down(blk, prev8, n):
    rolled = pltpu.roll(blk, n, axis=0)
    prev_rolled = pltpu.roll(prev8, n, axis=0)
    rows = lax.broadcasted_iota(jnp.int32, prev8.shape, 0)
    first = jnp.where(rows < n, prev_rolled, rolled[0:SUBLANES])
    return jnp.concatenate([first, rolled[SUBLANES:]], axis=0)


def _shift_up(blk, next8, n):
    r = blk.shape[0]
    rolled = pltpu.roll(blk, r - n, axis=0)
    next_rolled = pltpu.roll(next8, SUBLANES - n, axis=0)
    rows = lax.broadcasted_iota(jnp.int32, next8.shape, 0)
    last = jnp.where(rows >= SUBLANES - n, next_rolled, rolled[r - SUBLANES:])
    return jnp.concatenate([rolled[:r - SUBLANES], last], axis=0)


def _prev_spec(tm, width, col):
    per = tm // SUBLANES
    return pl.BlockSpec((SUBLANES, width), lambda i, *_: (jnp.maximum(i * per - 1, 0), col))


def _next_spec(tm, width, col, n_tiles):
    per = tm // SUBLANES
    last = n_tiles * per - 1
    return pl.BlockSpec((SUBLANES, width), lambda i, *_: (jnp.minimum((i + 1) * per, last), col))


def _group_matrix():
    idx = np.arange(ATTN_W) // HEAD_DIM
    return jnp.asarray((idx[:, None] == idx[None, :]).astype(np.float32), BF16)


def _norm_mod(x, g, sc, sh, *, name):
    T = x.shape[0]
    tm = min(TOKEN_TILE, T)

    def body(x_ref, g_ref, sc_ref, sh_ref, o_ref):
        xv = x_ref[...]
        inv = lax.rsqrt(jnp.mean(xv * xv, axis=-1, keepdims=True) + RMS_EPS)
        o_ref[...] = ((xv * inv) * g_ref[...] * (1.0 + sc_ref[...]) + sh_ref[...]).astype(BF16)

    row = pl.BlockSpec((tm, D), lambda i: (i, 0))
    return pl.pallas_call(
        body, name=name, grid=(T // tm,),
        in_specs=[row, _full((1, D)), _full((1, D)), _full((1, D))],
        out_specs=row, out_shape=jax.ShapeDtypeStruct((T, D), BF16),
        compiler_params=_params(("parallel",)),
    )(x, g, sc, sh)


LANE_ONE = 64
LANE_F = 67
LANE_LSE = 70
LANE_SUM = 73


def _pieces(x):
    hi = x.astype(BF16).astype(F32)
    rest = x - hi
    mid = rest.astype(BF16).astype(F32)
    return hi, mid, rest - mid


def _aug(lane, data, entries):
    out = jnp.where(lane < HEAD_DIM, data, 0.0)
    for idx, val in entries:
        out = jnp.where(lane == idx, val, out)
    return out


def _run(start, vals):
    return [(start + i, v) for i, v in enumerate(vals)]


def _head_lanes(a, h):
    blk = a[:, LANES * (h // 2):LANES * (h // 2) + LANES]
    return blk if h % 2 == 0 else pltpu.roll(blk, HEAD_DIM, axis=1)


def _branch_prep(proj, fcum, conv_w8, qg, kg, gmat):
    T = proj.shape[0]
    tm = min(TOKEN_TILE, T)
    nt = T // tm

    def body(cb_ref, cc_ref, cv_ref, q_ref, k_ref, v_ref, f_ref, ccp_ref, cvp_ref, w_ref, qg_ref, kg_ref, g_ref,
             ya_ref, qa_ref, ka_ref, va_ref):
        i = pl.program_id(0)
        z = cc_ref[...] * cv_ref[...]
        zp = jnp.where(i > 0, ccp_ref[...] * cvp_ref[...], 0.0)
        w = w_ref[...]
        cz = _shift_down(z, zp, 2) * w[0:1] + _shift_down(z, zp, 1) * w[1:2] + z * w[2:3]
        ya_ref[...] = (cb_ref[...] * cz).astype(BF16)
        gm = g_ref[...]

        def normed(src, gain, scale):
            v = src[...]
            ms = _split_dot(v * v, gm, 2) * (1.0 / HEAD_DIM)
            return (v * lax.rsqrt(ms + RMS_EPS)) * gain[...] * scale

        qn = normed(q_ref, qg_ref, 1.0 / np.sqrt(HEAD_DIM))
        kn = normed(k_ref, kg_ref, 1.0)
        vv = v_ref[...]
        fall = f_ref[...]
        lane = lax.broadcasted_iota(jnp.int32, (tm, LANES), 1)
        ones3 = [1.0, 1.0, 1.0]
        for h in range(N_HEADS):
            hi, mid, lo = _pieces(fall[:, h:h + 1])
            qa_ref[h] = _aug(lane, _head_lanes(qn, h), _run(LANE_ONE, ones3) + _run(LANE_F, [hi, mid, lo])
                             + [(LANE_SUM, 1.0)]).astype(BF16)
            ka_ref[h] = _aug(lane, _head_lanes(kn, h), _run(LANE_ONE, [-hi, -mid, -lo]) + _run(LANE_F, ones3)
                             + _run(LANE_LSE, ones3)).astype(BF16)
            va_ref[h] = _aug(lane, _head_lanes(vv, h), _run(LANE_ONE, ones3)).astype(BF16)

    blk = lambda col: pl.BlockSpec((tm, CONV_W), lambda i: (i, col))
    heads = pl.BlockSpec((N_HEADS, tm, LANES), lambda i: (0, i, 0))
    return pl.pallas_call(
        body, name="branch_prep", grid=(nt,),
        in_specs=[blk(0), blk(1), blk(2), blk(3), blk(4), blk(5), pl.BlockSpec((tm, F_PAD), lambda i: (i, 0)),
                  _prev_spec(tm, CONV_W, 1), _prev_spec(tm, CONV_W, 2),
                  _full((SUBLANES, CONV_W)), _full((1, ATTN_W)), _full((1, ATTN_W)), _full((ATTN_W, ATTN_W))],
        out_specs=[pl.BlockSpec((tm, CONV_W), lambda i: (i, 0)), heads, heads, heads],
        out_shape=[jax.ShapeDtypeStruct((T, CONV_W), BF16)] + [jax.ShapeDtypeStruct((N_HEADS, T, LANES), BF16)] * 3,
        compiler_params=_params(("parallel",)),
    )(proj, proj, proj, proj, proj, proj, fcum, proj, proj, conv_w8, qg, kg, gmat)


def _cumsum(x, *, reverse, name, col=0, gate_bias=None):
    T = x.shape[0]
    tm = min(TOKEN_TILE, T)
    nt = T // tm

    def body(x_ref, b_ref, o_ref, carry_ref):
        i = pl.program_id(0)

        @pl.when(i == 0)
        def _():
            carry_ref[...] = jnp.zeros_like(carry_ref)

        r = lax.broadcasted_iota(jnp.int32, (tm, tm), 0)
        c = lax.broadcasted_iota(jnp.int32, (tm, tm), 1)
        tri = jnp.where((c >= r) if reverse else (c <= r), 1.0, 0.0).astype(BF16)
        xv = x_ref[...]
        if gate_bias is not None:
            fx = xv + b_ref[...]
            xv = jnp.minimum(fx, 0.0) - jnp.log(1.0 + jnp.exp(-jnp.abs(fx)))
        out = _split_dot_left(tri, xv, 3) + carry_ref[0:1]
        o_ref[...] = out
        carry_ref[...] = jnp.broadcast_to(out[0:1] if reverse else out[tm - 1:tm], carry_ref.shape)

    rows = (lambda i: nt - 1 - i) if reverse else (lambda i: i)
    bias = jnp.zeros((1, F_PAD), F32) if gate_bias is None else gate_bias
    return pl.pallas_call(
        body, name=name, grid=(nt,),
        in_specs=[pl.BlockSpec((tm, F_PAD), lambda i: (rows(i), col)), _full((1, F_PAD))],
        out_specs=pl.BlockSpec((tm, F_PAD), lambda i: (rows(i), 0)),
        out_shape=jax.ShapeDtypeStruct((T, F_PAD), F32),
        scratch_shapes=[pltpu.VMEM((SUBLANES, F_PAD), F32)],
        compiler_params=_params(("arbitrary",)),
    )(x, bias)


def _split_dot_left(mat, x, parts):
    out = None
    rem = x
    for p in range(parts):
        piece = rem.astype(BF16)
        term = jnp.dot(mat, piece, preferred_element_type=F32)
        out = term if out is None else out + term
        if p + 1 < parts:
            rem = rem - piece.astype(F32)
    return out


def _out_resid_norm(x, merged, w_out, g1, g, sc, sh):
    T = x.shape[0]
    tm = min(TOKEN_TILE, T)

    def body(x_ref, m_ref, w_ref, g1_ref, g_ref, sc_ref, sh_ref, mix_ref, x1_ref, h_ref):
        mix = jnp.dot(m_ref[...], w_ref[...], preferred_element_type=F32)
        mix_ref[...] = mix
        x1 = x_ref[...] + g1_ref[...] * mix
        x1_ref[...] = x1
        inv = lax.rsqrt(jnp.mean(x1 * x1, axis=-1, keepdims=True) + RMS_EPS)
        h_ref[...] = ((x1 * inv) * g_ref[...] * (1.0 + sc_ref[...]) + sh_ref[...]).astype(BF16)

    row = pl.BlockSpec((tm, D), lambda i: (i, 0))
    vec = _full((1, D))
    return pl.pallas_call(
        body, name="out_resid_norm", grid=(T // tm,),
        in_specs=[row, row, _full((D, D)), vec, vec, vec, vec], out_specs=[row, row, row],
        out_shape=[jax.ShapeDtypeStruct((T, D), F32), jax.ShapeDtypeStruct((T, D), F32),
                   jax.ShapeDtypeStruct((T, D), BF16)],
        compiler_params=_params(("parallel",)),
    )(x, merged, w_out, g1, g, sc, sh)


FFN_TM = 256
FFN_TC = 1408


def _tanh_sigmoid(x):
    return 0.5 * jnp.tanh(0.5 * x) + 0.5


def _ffn_act_fwd(u, w8):
    T = u.shape[0]
    tm = min(FFN_TM, T)
    nt = T // tm
    nc = D_FF // FFN_TC

    def body(ug_ref, uv_ref, ugp_ref, uvp_ref, wg_ref, wv_ref, o_ref, cg_ref, cv_ref):
        i = pl.program_id(1)

        def conv(u_ref, p_ref, w_ref):
            uv = u_ref[...]
            up = jnp.where(i > 0, p_ref[...], 0.0)
            w = w_ref[...]
            return _shift_down(uv, up, 2) * w[0:1] + _shift_down(uv, up, 1) * w[1:2] + uv * w[2:3]

        gate = conv(ug_ref, ugp_ref, wg_ref)
        val = conv(uv_ref, uvp_ref, wv_ref)
        cg_ref[...] = gate.astype(BF16)
        cv_ref[...] = val.astype(BF16)
        o_ref[...] = (gate * _tanh_sigmoid(gate) * val).astype(BF16)

    per = tm // SUBLANES
    blk = lambda off: pl.BlockSpec((tm, FFN_TC), lambda j, i: (i, j + off))
    prev = lambda off: pl.BlockSpec((SUBLANES, FFN_TC), lambda j, i: (jnp.maximum(i * per - 1, 0), j + off))
    wblk = lambda off: pl.BlockSpec((SUBLANES, FFN_TC), lambda j, i: (0, j + off))
    return pl.pallas_call(
        body, name="ffn_act_fwd", grid=(nc, nt),
        in_specs=[blk(0), blk(nc), prev(0), prev(nc), wblk(0), wblk(nc)],
        out_specs=[blk(0), blk(0), blk(0)],
        out_shape=[jax.ShapeDtypeStruct((T, D_FF), BF16)] * 3,
        compiler_params=_params(("parallel", "parallel")),
    )(u, u, u, u, w8, w8)


def _down_loss_head(x1, act, w_down, g2, target):
    T = x1.shape[0]
    tm = min(TOKEN_TILE, T)

    def body(x1_ref, a_ref, w_ref, g2_ref, t_ref, dy_ref, dff_ref, loss_ref, dg2_ref):
        i = pl.program_id(0)

        @pl.when(i == 0)
        def _():
            loss_ref[...] = jnp.zeros_like(loss_ref)
            dg2_ref[...] = jnp.zeros_like(dg2_ref)

        ff = jnp.dot(a_ref[...], w_ref[...], preferred_element_type=F32)
        err = x1_ref[...] + g2_ref[...] * ff - t_ref[...]
        dy = err * (1.0 / D)
        dy_ref[...] = dy
        dff_ref[...] = (dy * g2_ref[...]).astype(BF16)
        loss_ref[...] += _rows8(err * err)
        dg2_ref[...] += _rows8(dy * ff)

    row = pl.BlockSpec((tm, D), lambda i: (i, 0))
    acc = _full((SUBLANES, D))
    return pl.pallas_call(
        body, name="down_loss_head", grid=(T // tm,),
        in_specs=[row, pl.BlockSpec((tm, D_FF), lambda i: (i, 0)), _full((D_FF, D)), _full((1, D)), row],
        out_specs=[row, row, acc, acc],
        out_shape=[jax.ShapeDtypeStruct((T, D), F32), jax.ShapeDtypeStruct((T, D), BF16),
                   jax.ShapeDtypeStruct((SUBLANES, D), F32), jax.ShapeDtypeStruct((SUBLANES, D), F32)],
        compiler_params=_params(("arbitrary",)),
    )(x1, act, w_down, g2, target)


def _nt_dot(a, b):
    return lax.dot_general(a, b, (((1,), (1,)), ((), ())), preferred_element_type=F32)


def _causal(n, keys_on_rows=False):
    r = lax.broadcasted_iota(jnp.int32, (n, n), 0)
    c = lax.broadcasted_iota(jnp.int32, (n, n), 1)
    return (c >= r) if keys_on_rows else (c <= r)


def _sweep(lo, hi, step, carry, group=2):
    while group >= 1:
        def several(j, cr, lo=lo, group=group):
            for g in range(group):
                cr = step(lo + group * j + g, cr)
            return cr

        passes = (hi - lo) // group
        carry = lax.fori_loop(0, passes, several, carry)
        lo = lo + group * passes
        group //= 2
    return carry


def _grid_ends(n0, n1):
    i0, i1 = pl.program_id(0), pl.program_id(1)
    return jnp.logical_and(i0 == 0, i1 == 0), jnp.logical_and(i0 == n0 - 1, i1 == n1 - 1)


def _attn_fwd(qa, ka, va, exchange=None):
    nh, T, _ = qa.shape
    bq = min(ATTN_BLOCK, T)
    nq = T // bq
    hp = ATTN_FWD_HEADS
    ng = nh // hp

    def body(*refs):
        if exchange is None:
            q_ref, k_ref, v_ref, o_ref, qb_ref = refs
        else:
            (q_ref, k_ref, v_ref), (o_ref, qb_ref), _, xrefs = exchange.split(refs, 3, 2)
            _ride(exchange, *_grid_ends(ng, nq), xrefs)
        qi = pl.program_id(1)

        def step(kb, carry, masked):
            start = pl.multiple_of(kb * bq, bq)
            out = []
            for h, (m, acc) in enumerate(carry):
                s = _nt_dot(q_ref[h], k_ref[h, pl.ds(start, bq), :])
                if masked:
                    s = jnp.where(_causal(bq), s, NEG_INF)
                m_new = jnp.maximum(m, jnp.max(s, axis=-1, keepdims=True))
                p = jnp.exp(s - m_new).astype(BF16)
                acc = jnp.exp(m - m_new) * acc + jnp.dot(p, v_ref[h, pl.ds(start, bq), :],
                                                         preferred_element_type=F32)
                out.append((m_new, acc))
            return tuple(out)

        init = tuple((jnp.full((bq, 1), NEG_INF, F32), jnp.zeros((bq, LANES), F32)) for _ in range(hp))
        carry = _sweep(0, qi, lambda kb, cr: step(kb, cr, False), init, group=4)
        lane = lax.broadcasted_iota(jnp.int32, (bq, LANES), 1)
        for h, (m, acc) in enumerate(step(qi, carry, True)):
            l = acc[:, LANE_ONE:LANE_ONE + 1]
            o_ref[h] = acc / l
            qf = q_ref[h].astype(F32)
            for idx, piece in _run(LANE_LSE, _pieces(m + jnp.log(l))):
                qf = jnp.where(lane == idx, -piece, qf)
            qb_ref[h] = qf.astype(BF16)

    tile = pl.BlockSpec((hp, bq, LANES), lambda h, i: (h, i, 0))
    whole = pl.BlockSpec((hp, T, LANES), lambda h, i: (h, 0, 0))
    out_shape = [jax.ShapeDtypeStruct((nh, T, LANES), F32), jax.ShapeDtypeStruct((nh, T, LANES), BF16)]
    if exchange is None:
        return pl.pallas_call(
            body, name="attn_fwd", grid=(ng, nq), in_specs=[tile, whole, whole], out_specs=[tile, tile],
            out_shape=out_shape, compiler_params=_params(("parallel", "parallel")),
        )(qa, ka, va)
    return pl.pallas_call(
        body, name="attn_fwd", grid=(ng, nq), in_specs=[tile, whole, whole] + exchange.in_specs,
        out_specs=[tile, tile] + exchange.out_specs, out_shape=out_shape + exchange.out_shapes,
        scratch_shapes=exchange.scratch, compiler_params=_params(("arbitrary", "arbitrary")),
    )(qa, ka, va, *exchange.xs)


def _branch_merge_fwd(ya0, o_h, proj, wba, wbb_heads):
    nh, T, _ = o_h.shape
    tm = min(TOKEN_TILE, T)

    def body(ya0_ref, o_ref, ga_ref, gb_ref, wa_ref, wb_ref, ya_ref, yb_ref, m_ref):
        ya = jnp.dot(ya0_ref[...], wa_ref[...], preferred_element_type=F32)
        yb = jnp.dot(o_ref[0].astype(BF16), wb_ref[0:LANES, :], preferred_element_type=F32)
        for h in range(1, nh):
            yb += jnp.dot(o_ref[h].astype(BF16), wb_ref[h * LANES:(h + 1) * LANES, :], preferred_element_type=F32)
        ya_ref[...] = ya.astype(BF16)
        yb_ref[...] = yb.astype(BF16)
        m_ref[...] = (_sigmoid(ga_ref[...]) * ya + _sigmoid(gb_ref[...]) * yb).astype(BF16)

    row = pl.BlockSpec((tm, D), lambda i: (i, 0))
    return pl.pallas_call(
        body, name="branch_merge_fwd", grid=(T // tm,),
        in_specs=[pl.BlockSpec((tm, CONV_W), lambda i: (i, 0)), pl.BlockSpec((nh, tm, LANES), lambda i: (0, i, 0)),
                  pl.BlockSpec((tm, D), lambda i: (i, COL_GA // D)), pl.BlockSpec((tm, D), lambda i: (i, COL_GB // D)),
                  _full((CONV_W, D)), _full((nh * LANES, D))],
        out_specs=[row, row, row],
        out_shape=[jax.ShapeDtypeStruct((T, D), BF16)] * 3,
        compiler_params=_params(("parallel",)),
    )(ya0, o_h, proj, proj, wba, wbb_heads)


def _branch_b_bwd(dyb, o_h, wbb_heads_t):
    nh, T, _ = o_h.shape
    tm = min(TOKEN_TILE, T)

    def body(dyb_ref, o_ref, w_ref, out_ref):
        do = jnp.dot(dyb_ref[...], w_ref[...], preferred_element_type=F32)
        lane = lax.broadcasted_iota(jnp.int32, (tm, LANES), 1)
        for h in range(nh):
            g = do[:, h * LANES:(h + 1) * LANES].astype(BF16).astype(F32)
            delta = jnp.sum(g * o_ref[h], axis=-1, keepdims=True)
            for idx, piece in _run(LANE_ONE, _pieces(delta)):
                g = jnp.where(lane == idx, -piece, g)
            out_ref[h] = g.astype(BF16)

    heads = pl.BlockSpec((nh, tm, LANES), lambda i: (0, i, 0))
    return pl.pallas_call(
        body, name="branch_b_bwd", grid=(T // tm,),
        in_specs=[pl.BlockSpec((tm, D), lambda i: (i, 0)), heads, _full((D, nh * LANES))],
        out_specs=heads, out_shape=jax.ShapeDtypeStruct((nh, T, LANES), BF16),
        compiler_params=_params(("parallel",)),
    )(dyb, o_h, wbb_heads_t)


def _branch_b_dw(o_h, dyb):
    nh, T, _ = o_h.shape
    tk = min(TOKEN_TILE, T)

    def body(o_ref, dyb_ref, out_ref):
        @pl.when(pl.program_id(0) == 0)
        def _():
            out_ref[...] = jnp.zeros_like(out_ref)

        g = dyb_ref[...]
        for h in range(nh):
            out_ref[h] += _tn_dot(o_ref[h].astype(BF16), g)

    return pl.pallas_call(
        body, name="branch_b_dw", grid=(T // tk,),
        in_specs=[pl.BlockSpec((nh, tk, LANES), lambda k: (0, k, 0)), pl.BlockSpec((tk, D), lambda k: (k, 0))],
        out_specs=_full((nh, LANES, D)), out_shape=jax.ShapeDtypeStruct((nh, LANES, D), F32),
        compiler_params=_params(("arbitrary",)),
    )(o_h, dyb)


def _attn_bwd(qb, ka, va, doa, exchange=None):
    nh, T, _ = qb.shape
    bk = min(ATTN_BLOCK, T)
    nk = T // bk

    def body(*refs):
        if exchange is None:
            q_ref, do_ref, k_ref, v_ref, dq_ref, dk_ref, dv_ref = refs
        else:
            (q_ref, do_ref, k_ref, v_ref), (dq_ref, dk_ref, dv_ref), _, xrefs = exchange.split(refs, 4, 3)
            _ride(exchange, *_grid_ends(nh, nk), xrefs)
        ki = pl.program_id(1)

        @pl.when(ki == 0)
        def _():
            dq_ref[...] = jnp.zeros_like(dq_ref)

        k = k_ref[0]
        v = v_ref[0]

        def step(qi, carry, masked):
            dk, dv = carry
            rows = pl.ds(pl.multiple_of(qi * bk, bk), bk)
            q = q_ref[0, rows, :]
            g = do_ref[0, rows, :]
            pt = jnp.exp(_nt_dot(k, q))
            if masked:
                pt = jnp.where(_causal(bk, keys_on_rows=True), pt, 0.0)
            dv = dv + jnp.dot(pt.astype(BF16), g, preferred_element_type=F32)
            dst = (pt * _nt_dot(v, g)).astype(BF16)
            dk = dk + jnp.dot(dst, q, preferred_element_type=F32)
            dq_ref[0, rows, :] += _tn_dot(dst, k)
            return dk, dv

        init = (jnp.zeros((bk, LANES), F32), jnp.zeros((bk, LANES), F32))
        carry = step(ki, init, True)
        dk_ref[0], dv_ref[0] = _sweep(ki + 1, nk, lambda qi, cr: step(qi, cr, False), carry)

    tile = pl.BlockSpec((1, bk, LANES), lambda h, i: (h, i, 0))
    whole = pl.BlockSpec((1, T, LANES), lambda h, i: (h, 0, 0))
    out_shape = [jax.ShapeDtypeStruct((nh, T, LANES), F32)] * 3
    if exchange is None:
        return pl.pallas_call(
            body, name="attn_bwd", grid=(nh, nk), in_specs=[whole, whole, tile, tile],
            out_specs=[whole, tile, tile], out_shape=out_shape, compiler_params=_params(("parallel", "arbitrary")),
        )(qb, doa, ka, va)
    return pl.pallas_call(
        body, name="attn_bwd", grid=(nh, nk), in_specs=[whole, whole, tile, tile] + exchange.in_specs,
        out_specs=[whole, tile, tile] + exchange.out_specs, out_shape=out_shape + exchange.out_shapes,
        scratch_shapes=exchange.scratch, compiler_params=_params(("arbitrary", "arbitrary")),
    )(qb, doa, ka, va, *exchange.xs)


def _attn_unpack(dq_h, dk_h, dv_h):
    nh, T, _ = dq_h.shape
    tm = min(TOKEN_TILE, T)

    def body(dq_ref, dk_ref, dv_ref, q_out, k_out, v_out, f_out):
        lane = lax.broadcasted_iota(jnp.int32, (tm, LANES), 1)
        low = lane < HEAD_DIM
        for src, dst in ((dq_ref, q_out), (dk_ref, k_out), (dv_ref, v_out)):
            for pair in range(nh // 2):
                both = jnp.where(low, src[2 * pair], pltpu.roll(src[2 * pair + 1], HEAD_DIM, axis=1))
                dst[:, LANES * pair:LANES * (pair + 1)] = both.astype(dst.dtype)
        df = jnp.zeros((tm, LANES), F32)
        for h in range(nh):
            col = dq_ref[h][:, LANE_F:LANE_F + 1] - dk_ref[h][:, LANE_SUM:LANE_SUM + 1]
            df = jnp.where(lane == h, col, df)
        f_out[...] = df

    heads = pl.BlockSpec((nh, tm, LANES), lambda i: (0, i, 0))
    tok = pl.BlockSpec((tm, ATTN_W), lambda i: (i, 0))
    return pl.pallas_call(
        body, name="attn_unpack", grid=(T // tm,), in_specs=[heads, heads, heads],
        out_specs=[tok, tok, tok, pl.BlockSpec((tm, F_PAD), lambda i: (i, 0))],
        out_shape=[jax.ShapeDtypeStruct((T, ATTN_W), F32), jax.ShapeDtypeStruct((T, ATTN_W), F32),
                   jax.ShapeDtypeStruct((T, ATTN_W), BF16), jax.ShapeDtypeStruct((T, F_PAD), F32)],
        compiler_params=_params(("parallel",)),
    )(dq_h, dk_h, dv_h)


def _ffn_act_bwd(u, cg, cv, da, w8):
    T = u.shape[0]
    tm = min(FFN_TM, T)
    nt = T // tm
    nc = D_FF // FFN_TC

    def body(ug_ref, uv_ref, cg_ref, cv_ref, cgn_ref, cvn_ref, da_ref, dan_ref, wg_ref, wv_ref,
             dug_ref, duv_ref, dwg_ref, dwv_ref):
        i = pl.program_id(1)

        @pl.when(i == 0)
        def _():
            dwg_ref[...] = jnp.zeros_like(dwg_ref)
            dwv_ref[...] = jnp.zeros_like(dwv_ref)

        gate = jnp.concatenate([cg_ref[...], cgn_ref[...]], axis=0).astype(F32)
        val = jnp.concatenate([cv_ref[...], cvn_ref[...]], axis=0).astype(F32)
        dae = jnp.concatenate([da_ref[...], dan_ref[...]], axis=0).astype(F32)
        rows_e = lax.broadcasted_iota(jnp.int32, dae.shape, 0)
        dae = jnp.where(jnp.logical_and(i == nt - 1, rows_e >= tm), 0.0, dae)
        sg = _tanh_sigmoid(gate)
        n = tm + BF16_ROWS

        def back(d, u_ref, w_ref, du_ref, dw_ref):
            w = w_ref[...]
            uv = u_ref[...]
            d1 = pltpu.roll(d, n - 1, axis=0)[:tm]
            d2 = pltpu.roll(d, n - 2, axis=0)[:tm]
            d0 = d[:tm]
            du_ref[...] = (d0 * w[2:3] + d1 * w[1:2] + d2 * w[0:1]).astype(BF16)
            rows = [jnp.sum(t * uv, axis=0, keepdims=True) for t in (d2, d1, d0)]
            dw_ref[...] += jnp.concatenate(rows + [jnp.zeros((SUBLANES - 3, FFN_TC), F32)], axis=0)

        back(dae * val * sg * (1.0 + gate * (1.0 - sg)), ug_ref, wg_ref, dug_ref, dwg_ref)
        back(dae * gate * sg, uv_ref, wv_ref, duv_ref, dwv_ref)

    per = tm // BF16_ROWS
    last_blk = nt * per - 1
    blk = lambda off: pl.BlockSpec((tm, FFN_TC), lambda j, i: (i, j + off))
    nxt = pl.BlockSpec((BF16_ROWS, FFN_TC), lambda j, i: (jnp.minimum((i + 1) * per, last_blk), j))
    wblk = lambda off: pl.BlockSpec((SUBLANES, FFN_TC), lambda j, i: (0, j + off))
    dug, duv, dwg, dwv = pl.pallas_call(
        body, name="ffn_act_bwd", grid=(nc, nt),
        in_specs=[blk(0), blk(nc), blk(0), blk(0), nxt, nxt, blk(0), nxt, wblk(0), wblk(nc)],
        out_specs=[blk(0), blk(0), wblk(0), wblk(0)],
        out_shape=[jax.ShapeDtypeStruct((T, D_FF), BF16)] * 2 + [jax.ShapeDtypeStruct((SUBLANES, D_FF), F32)] * 2,
        compiler_params=_params(("parallel", "arbitrary")),
    )(u, u, cg, cv, cg, cv, da, da, w8, w8)
    return dug, duv, jnp.concatenate([dwg, dwv], axis=1)


def _norm_bwd(xin, dh, dres, g, sc, *, name):
    T = xin.shape[0]
    tm = min(TOKEN_TILE, T)

    def body(x_ref, dh_ref, dr_ref, g_ref, sc_ref, dx_ref, dsh_ref, dsc_ref, dg_ref):
        i = pl.program_id(0)

        @pl.when(i == 0)
        def _():
            dsh_ref[...] = jnp.zeros_like(dsh_ref)
            dsc_ref[...] = jnp.zeros_like(dsc_ref)
            dg_ref[...] = jnp.zeros_like(dg_ref)

        xv = x_ref[...]
        dh = dh_ref[...]
        gv = g_ref[...]
        one_sc = 1.0 + sc_ref[...]
        inv = lax.rsqrt(jnp.mean(xv * xv, axis=-1, keepdims=True) + RMS_EPS)
        xn = xv * inv
        dxn = dh * (gv * one_sc)
        dx_ref[...] = dr_ref[...] + inv * (dxn - xn * jnp.mean(dxn * xn, axis=-1, keepdims=True))
        dhxn = dh * xn
        dsh_ref[...] += _rows8(dh)
        dsc_ref[...] += _rows8(dhxn * gv)
        dg_ref[...] += _rows8(dhxn * one_sc)

    row = pl.BlockSpec((tm, D), lambda i: (i, 0))
    acc = _full((SUBLANES, D))
    return pl.pallas_call(
        body, name=name, grid=(T // tm,),
        in_specs=[row, row, row, _full((1, D)), _full((1, D))], out_specs=[row, acc, acc, acc],
        out_shape=[jax.ShapeDtypeStruct((T, D), F32)] + [jax.ShapeDtypeStruct((SUBLANES, D), F32)] * 3,
        compiler_params=_params(("arbitrary",)),
    )(xin, dh, dres, g, sc)


def _norm2_gate_bwd(x1, dh2, dy, mix, g1, g, sc):
    T = x1.shape[0]
    tm = min(TOKEN_TILE, T)

    def body(x_ref, dh_ref, dr_ref, mix_ref, g1_ref, g_ref, sc_ref, dx_ref, dmix_ref, dsh_ref, dsc_ref, dg_ref,
             dg1_ref):
        @pl.when(pl.program_id(0) == 0)
        def _():
            for ref in (dsh_ref, dsc_ref, dg_ref, dg1_ref):
                ref[...] = jnp.zeros_like(ref)

        xv = x_ref[...]
        dh = dh_ref[...]
        gv = g_ref[...]
        one_sc = 1.0 + sc_ref[...]
        inv = lax.rsqrt(jnp.mean(xv * xv, axis=-1, keepdims=True) + RMS_EPS)
        xn = xv * inv
        dxn = dh * (gv * one_sc)
        dx = dr_ref[...] + inv * (dxn - xn * jnp.mean(dxn * xn, axis=-1, keepdims=True))
        dx_ref[...] = dx
        dmix_ref[...] = (dx * g1_ref[...]).astype(BF16)
        dhxn = dh * xn
        dsh_ref[...] += _rows8(dh)
        dsc_ref[...] += _rows8(dhxn * gv)
        dg_ref[...] += _rows8(dhxn * one_sc)
        dg1_ref[...] += _rows8(dx * mix_ref[...])

    row = pl.BlockSpec((tm, D), lambda i: (i, 0))
    vec, acc = _full((1, D)), _full((SUBLANES, D))
    return pl.pallas_call(
        body, name="norm2_gate_bwd", grid=(T // tm,),
        in_specs=[row, row, row, row, vec, vec, vec], out_specs=[row, row, acc, acc, acc, acc],
        out_shape=[jax.ShapeDtypeStruct((T, D), F32), jax.ShapeDtypeStruct((T, D), BF16)]
        + [jax.ShapeDtypeStruct((SUBLANES, D), F32)] * 4,
        compiler_params=_params(("arbitrary",)),
    )(x1, dh2, dy, mix, g1, g, sc)


def _out_merge_bwd(dmix, w_out_t, ya, yb, proj):
    T = ya.shape[0]
    tm = min(TOKEN_TILE, T)

    def body(dmix_ref, w_ref, ya_ref, yb_ref, ga_ref, gb_ref, dya_ref, dyb_ref, dga_ref, dgb_ref):
        dm = jnp.dot(dmix_ref[...], w_ref[...], preferred_element_type=F32)
        sa = _sigmoid(ga_ref[...])
        sb = _sigmoid(gb_ref[...])
        dya_ref[...] = (dm * sa).astype(BF16)
        dyb_ref[...] = (dm * sb).astype(BF16)
        dga_ref[...] = (dm * ya_ref[...].astype(F32) * sa * (1.0 - sa)).astype(BF16)
        dgb_ref[...] = (dm * yb_ref[...].astype(F32) * sb * (1.0 - sb)).astype(BF16)

    row = pl.BlockSpec((tm, D), lambda i: (i, 0))
    return pl.pallas_call(
        body, name="out_merge_bwd", grid=(T // tm,),
        in_specs=[row, _full((D, D)), row, row, pl.BlockSpec((tm, D), lambda i: (i, COL_GA // D)),
                  pl.BlockSpec((tm, D), lambda i: (i, COL_GB // D))],
        out_specs=[row] * 4, out_shape=[jax.ShapeDtypeStruct((T, D), BF16)] * 4,
        compiler_params=_params(("parallel",)),
    )(dmix, w_out_t, ya, yb, proj, proj)


def _conv_branch_bwd(proj, dya0, conv_w8):
    T = proj.shape[0]
    tm = min(FFN_TM, T)
    nt = T // tm

    def body(cb_ref, cc_ref, cv_ref, cbn_ref, ccp_ref, cvp_ref, ccn_ref, cvn_ref, d_ref, dn_ref, w_ref,
             d3_ref, dw_ref):
        i = pl.program_id(0)

        @pl.when(i == 0)
        def _():
            dw_ref[...] = jnp.zeros_like(dw_ref)

        first, last = i == 0, i == nt - 1
        w = w_ref[...]
        cc = jnp.concatenate([ccp_ref[...], cc_ref[...], ccn_ref[...]], axis=0)
        cv = jnp.concatenate([cvp_ref[...], cv_ref[...], cvn_ref[...]], axis=0)
        rows = lax.broadcasted_iota(jnp.int32, cc.shape, 0)
        z = jnp.where(jnp.logical_and(first, rows < SUBLANES), 0.0, cc * cv)
        z1 = pltpu.roll(z, 1, axis=0)
        z2 = pltpu.roll(z, 2, axis=0)
        cz = z2 * w[0:1] + z1 * w[1:2] + z * w[2:3]
        zeros8 = jnp.zeros((SUBLANES, CONV_W), F32)
        de = jnp.concatenate([zeros8, d_ref[...], jnp.where(last, 0.0, dn_ref[...])], axis=0)
        cbe = jnp.concatenate([zeros8, cb_ref[...], cbn_ref[...]], axis=0)
        dcz = de * cbe
        n = tm + 2 * SUBLANES
        dz = dcz * w[2:3] + pltpu.roll(dcz, n - 1, axis=0) * w[1:2] + pltpu.roll(dcz, n - 2, axis=0) * w[0:1]
        inner = slice(SUBLANES, SUBLANES + tm)
        d3_ref[:, 0:CONV_W] = (de * cz)[inner].astype(BF16)
        d3_ref[:, CONV_W:2 * CONV_W] = (dz * cv)[inner].astype(BF16)
        d3_ref[:, 2 * CONV_W:3 * CONV_W] = (dz * cc)[inner].astype(BF16)
        wrows = [jnp.sum((dcz * t)[inner], axis=0, keepdims=True) for t in (z2, z1, z)]
        dw_ref[...] += jnp.concatenate(wrows + [jnp.zeros((SUBLANES - 3, CONV_W), F32)], axis=0)

    blk = lambda col: pl.BlockSpec((tm, CONV_W), lambda i: (i, col))
    out_blk = pl.BlockSpec((tm, CONV_W), lambda i: (i, 0))
    return pl.pallas_call(
        body, name="conv_branch_bwd", grid=(nt,),
        in_specs=[blk(0), blk(1), blk(2), _next_spec(tm, CONV_W, 0, nt),
                  _prev_spec(tm, CONV_W, 1), _prev_spec(tm, CONV_W, 2),
                  _next_spec(tm, CONV_W, 1, nt), _next_spec(tm, CONV_W, 2, nt),
                  out_blk, _next_spec(tm, CONV_W, 0, nt), _full((SUBLANES, CONV_W))],
        out_specs=[pl.BlockSpec((tm, 3 * CONV_W), lambda i: (i, 0)), _full((SUBLANES, CONV_W))],
        out_shape=[jax.ShapeDtypeStruct((T, 3 * CONV_W), BF16), jax.ShapeDtypeStruct((SUBLANES, CONV_W), F32)],
        compiler_params=_params(("arbitrary",)),
    )(proj, proj, proj, proj, proj, proj, proj, proj, dya0, dya0, conv_w8)


def _qk_norm_bwd(proj, dqs, dkh, dlogf, qg, kg, bf_pad, gmat):
    T = proj.shape[0]
    tm = min(TOKEN_TILE, T)

    def body(q_ref, k_ref, f_ref, dqs_ref, dkh_ref, dlf_ref, qg_ref, kg_ref, bf_ref, g_ref,
             dqk_ref, dfl_ref, dqg_ref, dkg_ref, dbf_ref):
        @pl.when(pl.program_id(0) == 0)
        def _():
            dqg_ref[...] = jnp.zeros_like(dqg_ref)
            dkg_ref[...] = jnp.zeros_like(dkg_ref)
            dbf_ref[...] = jnp.zeros_like(dbf_ref)

        gm = g_ref[...]
        for src, d_src, gain, scale, dst, dgain in (
                (q_ref, dqs_ref, qg_ref, 1.0 / np.sqrt(HEAD_DIM), dqk_ref.at[:, 0:ATTN_W], dqg_ref),
                (k_ref, dkh_ref, kg_ref, 1.0, dqk_ref.at[:, ATTN_W:2 * ATTN_W], dkg_ref)):
            v = src[...]
            dhat = d_src[...] * scale
            inv = lax.rsqrt(_split_dot(v * v, gm, 2) * (1.0 / HEAD_DIM) + RMS_EPS)
            vn = v * inv
            dgain[...] += _rows8(dhat * vn)
            dvn = dhat * gain[...]
            mean = _split_dot(dvn * vn, gm, 2) * (1.0 / HEAD_DIM)
            dst[...] = (inv * (dvn - vn * mean)).astype(BF16)
        fx = f_ref[...] + bf_ref[...]
        dfl = dlf_ref[...] * _sigmoid(-fx)
        dfl_ref[...] = dfl.astype(BF16)
        dbf_ref[...] += _rows8(dfl)

    blk = lambda col: pl.BlockSpec((tm, ATTN_W), lambda i: (i, col))
    out_blk = pl.BlockSpec((tm, ATTN_W), lambda i: (i, 0))
    f_in = pl.BlockSpec((tm, F_PAD), lambda i: (i, COL_F // F_PAD))
    f_blk = pl.BlockSpec((tm, F_PAD), lambda i: (i, 0))
    return pl.pallas_call(
        body, name="qk_norm_bwd", grid=(T // tm,),
        in_specs=[blk(3), blk(4), f_in, out_blk, out_blk, f_blk, _full((1, ATTN_W)), _full((1, ATTN_W)),
                  _full((1, F_PAD)), _full((ATTN_W, ATTN_W))],
        out_specs=[pl.BlockSpec((tm, 2 * ATTN_W), lambda i: (i, 0)), f_blk, _full((SUBLANES, ATTN_W)),
                   _full((SUBLANES, ATTN_W)), _full((SUBLANES, F_PAD))],
        out_shape=[jax.ShapeDtypeStruct((T, 2 * ATTN_W), BF16), jax.ShapeDtypeStruct((T, F_PAD), BF16)]
        + [jax.ShapeDtypeStruct((SUBLANES, ATTN_W), F32)] * 2 + [jax.ShapeDtypeStruct((SUBLANES, F_PAD), F32)],
        compiler_params=_params(("arbitrary",)),
    )(proj, proj, proj, dqs, dkh, dlogf, qg, kg, bf_pad, gmat)


def _pad_rows8(w):
    return jnp.pad(w, ((0, SUBLANES - w.shape[0]), (0, 0)))


def _fold8(acc):
    return jnp.sum(acc, axis=0, keepdims=True)


def _late_weights(mats):
    out = {}
    for name in ("w_branch_a", "w_out", "w_up", "w_down"):
        out[name] = mats[name]
        out[name + "_t"] = mats[name].T
    out["w_branch_b_heads"] = _pad_head_rows(mats["w_branch_b"])
    out["w_branch_b_heads_t"] = out["w_branch_b_heads"].T
    return out


def _local_step(x, target, mod, wts, late=None):
    T = x.shape[0]
    tb = min(MATMUL_TILE, T)
    tk_long = min(2 * MATMUL_TILE, T)
    tm = min(TOKEN_TILE, T)
    sh1, sc1, g1, sh2, sc2, g2 = [mod[:, i * D:(i + 1) * D] for i in range(N_MOD)]
    w_in, w_in_t = wts["w_in"], wts["w_in_t"]
    conv_a8 = _pad_rows8(wts["conv_a_w"])
    conv_f8 = _pad_rows8(wts["conv_ffn_w"])
    qg = jnp.tile(wts["q_norm_g"], (1, N_HEADS))
    kg = jnp.tile(wts["k_norm_g"], (1, N_HEADS))
    bf_pad = jnp.pad(wts["b_f"], ((0, 0), (0, F_PAD - N_HEADS)))
    gmat = _group_matrix()

    h = _norm_mod(x, wts["norm1_g"], sc1, sh1, name="norm1_fwd")
    proj = _matmul(h, w_in, name="mm_in", tm=tb, tn=896, tk=D)
    fcum = _cumsum(proj, reverse=False, name="gate_cumsum", col=COL_F // F_PAD, gate_bias=bf_pad)
    ya0, qa, ka, va = _branch_prep(proj, fcum, conv_a8, qg, kg, gmat)
    if late is None:
        o_h, qb = _attn_fwd(qa, ka, va)
    else:
        o_h, qb, *gathered = _attn_fwd(qa, ka, va, _Exchange([late[name] for name, *_ in LATE], scatter=False))
        wts = dict(wts)
        mats = {name: _join_shards(g, axis) for (name, _, _, axis), g in zip(LATE, gathered) if name != "w_up"}
        mats["w_up"] = _assemble_columns(gathered[[name for name, *_ in LATE].index("w_up")], 2 * D_FF // N_DEV,
                                         2 * D_FF, ((0, 2 * D_FF, 0),), name="assemble_w_up")
        wts.update(_late_weights(mats))
    ya, yb, merged = _branch_merge_fwd(ya0, o_h, proj, wts["w_branch_a"], wts["w_branch_b_heads"])
    mix, x1, h2 = _out_resid_norm(x, merged, wts["w_out"], g1, wts["norm2_g"], sc2, sh2)
    u = _matmul(h2, wts["w_up"], name="mm_up", tm=tb, tn=1408, tk=D)
    act, conv_gate, conv_val = _ffn_act_fwd(u, conv_f8)
    dy, dff, sq8, dg2_8 = _down_loss_head(x1, act, wts["w_down"], g2, target)
    sq = jnp.sum(sq8).reshape(1, 1)

    grads = {}
    da = _matmul(dff, wts["w_down_t"], name="mm_down_dx", tm=tb, tn=1408, tk=D, out_dtype=BF16)
    grads["w_down"] = _matmul(act, dff, name="mm_down_dw", tm=1408, tn=D, tk=tk_long, trans_a=True)
    dug, duv, dconv_f8 = _ffn_act_bwd(u, conv_gate, conv_val, da, conv_f8)
    grads["conv_ffn_w"] = dconv_f8[:3]
    dh2 = _matmul_pieces([dug, duv], wts["w_up_t"], name="mm_up_dx", tm=tm)
    dw_up = [_matmul(h2, d, name="mm_up_dw_" + half, tm=D, tn=1408, tk=tk_long, trans_a=True)
             for half, d in (("gate", dug), ("val", duv))]
    if late is None:
        grads["w_up"] = jnp.concatenate(dw_up, axis=1)
    dx1, dmix, dsh2_8, dsc2_8, dn2_8, dg1_8 = _norm2_gate_bwd(x1, dh2, dy, mix, g1, wts["norm2_g"], sc2)
    grads["norm2_g"] = _fold8(dn2_8)

    grads["w_out"] = _matmul(merged, dmix, name="mm_out_dw", tm=D, tn=D, tk=tk_long, trans_a=True)
    dya, dyb, dga, dgb = _out_merge_bwd(dmix, wts["w_out_t"], ya, yb, proj)
    dya0 = _matmul(dya, wts["w_branch_a_t"], name="mm_branch_a_dx", tm=tb, tn=CONV_W, tk=D)
    grads["w_branch_a"] = _matmul(ya0, dya, name="mm_branch_a_dw", tm=CONV_W, tn=D, tk=tk_long, trans_a=True)
    doa = _branch_b_bwd(dyb, o_h, wts["w_branch_b_heads_t"])
    grads["w_branch_b"] = _branch_b_dw(o_h, dyb)[:, :HEAD_DIM].reshape(ATTN_W, D)
    dconv3, dconv_a8 = _conv_branch_bwd(proj, dya0, conv_a8)
    grads["conv_a_w"] = dconv_a8[:3]

    parts = {}
    if late is None:
        dq_h, dk_h, dv_h = _attn_bwd(qb, ka, va, doa)
    else:
        ready = [(_column_shards(dw_up) if name == "w_up" else _split_shards(grads[name], axis)).astype(BF16)
                 for name, _, _, axis in LATE]
        dq_h, dk_h, dv_h, *recv = _attn_bwd(
            qb, ka, va, doa, _Exchange(ready + [_pack_full_by_dest(grads, CONVS, SUBLANES)], scatter=True))
        parts = dict(zip([name for name, *_ in LATE] + ["conv"], recv))
    dq_tok, dk_tok, dv_tok, dfcum = _attn_unpack(dq_h, dk_h, dv_h)
    dlogf = _cumsum(dfcum, reverse=True, name="gate_cumsum_bwd")
    dqk, dfl, dqg8, dkg8, dbf8 = _qk_norm_bwd(proj, dq_tok, dk_tok, dlogf, qg, kg, bf_pad, gmat)
    grads["q_norm_g"] = jnp.sum(_fold8(dqg8).reshape(N_HEADS, HEAD_DIM), axis=0, keepdims=True)
    grads["k_norm_g"] = jnp.sum(_fold8(dkg8).reshape(N_HEADS, HEAD_DIM), axis=0, keepdims=True)
    grads["b_f"] = _fold8(dbf8)[:, :N_HEADS]
    narrow, wide = [dconv3, dqk, dv_tok], [dga, dgb, dfl]
    dw_narrow = _matmul_tn_pieces(h, narrow, name="mm_in_dw_narrow", tk=tb)
    dwa, dwb, dwf = _matmul_tn_pieces(h, wide, name="mm_in_dw_wide", tk=tk_long)
    dw_in = list(dw_narrow) + [dwf[:, :N_HEADS], dwa, dwb]
    if late is None:
        grads["w_in"] = jnp.concatenate(dw_in, axis=1)
        dh = _matmul_pieces(narrow + wide, w_in_t, name="mm_in_dx", tm=tm)
    else:
        dh, parts["w_in"] = _matmul_pieces(
            narrow + wide, w_in_t, name="mm_in_dx", tm=tm,
            exchange=_Exchange([_column_shards(dw_in).astype(BF16)], scatter=True))
    grad_x, dsh1_8, dsc1_8, dn1_8 = _norm_bwd(x, dh, dx1, wts["norm1_g"], sc1, name="norm1_bwd")
    grads["norm1_g"] = _fold8(dn1_8)
    grads["mod"] = jnp.concatenate([_fold8(a) for a in (dsh1_8, dsc1_8, dg1_8, dsh2_8, dsc2_8, dg2_8)], axis=1)
    return sq, grad_x, grads, parts


def _me_and_peers():
    mx, my, mc = lax.axis_index("x"), lax.axis_index("y"), lax.axis_index("c")
    me = 4 * mx + 2 * my + mc
    peers = []
    for k in range(1, N_DEV):
        px = 1 - mx if k & 4 else mx
        py = 1 - my if k & 2 else my
        pc = 1 - mc if k & 1 else mc
        peers.append(((px, py, pc), 4 * px + 2 * py + pc))
    return me, peers


HBM_SPEC = pl.BlockSpec(memory_space=pltpu.HBM)


class _Exchange:
    def __init__(self, xs, scatter):
        self.xs, self.scatter, self.n = list(xs), scatter, len(xs)
        self.out_shapes = [jax.ShapeDtypeStruct(x.shape if scatter else (N_DEV,) + x.shape, x.dtype) for x in xs]
        self.in_specs = [HBM_SPEC] * self.n
        self.out_specs = [HBM_SPEC] * self.n
        self.scratch = [pltpu.SemaphoreType.DMA((self.n, N_DEV - 1)), pltpu.SemaphoreType.DMA((self.n, N_DEV - 1)),
                        pltpu.SemaphoreType.DMA((self.n,))]

    def _copies(self, x_refs, out_refs, sems):
        send_sems, recv_sems, local_sems = sems
        me, peers = _me_and_peers()

        def src(a, idx):
            return x_refs[a].at[idx] if self.scatter else x_refs[a]

        def copy(a, k, from_idx, to_slot, device):
            return pltpu.make_async_remote_copy(
                src_ref=src(a, from_idx), dst_ref=out_refs[a].at[to_slot], send_sem=send_sems.at[a, k],
                recv_sem=recv_sems.at[a, k], device_id=device, device_id_type=MESH)

        local = [pltpu.make_async_copy(src(a, me), out_refs[a].at[me], local_sems.at[a]) for a in range(self.n)]
        sends = [copy(a, k, idx, me, dev) for a in range(self.n) for k, (dev, idx) in enumerate(peers)]
        recvs = [copy(a, k, idx, idx, dev) for a in range(self.n) for k, (dev, idx) in enumerate(peers)]
        return local, sends, recvs

    def start(self, x_refs, out_refs, sems):
        local, sends, _ = self._copies(x_refs, out_refs, sems)
        for cp in local + sends:
            cp.start()

    def wait(self, x_refs, out_refs, sems):
        local, sends, recvs = self._copies(x_refs, out_refs, sems)
        for cp in recvs:
            cp.wait_recv()
        for cp in sends:
            cp.wait_send()
        for cp in local:
            cp.wait()

    def split(self, refs, n_in, n_out):
        n = self.n
        ins, xin = refs[:n_in], refs[n_in:n_in + n]
        outs, xout = refs[n_in + n:n_in + n + n_out], refs[n_in + n + n_out:n_in + 2 * n + n_out]
        rest = refs[n_in + 2 * n + n_out:]
        return ins, outs, rest[:len(rest) - 3], (xin, xout, rest[len(rest) - 3:])


def _ride(exchange, first, last, refs):
    if exchange is None:
        return

    @pl.when(first)
    def _():
        exchange.start(*refs)

    @pl.when(last)
    def _():
        exchange.wait(*refs)


def _gather_two_level(xs, *, name):
    n = len(xs)
    out_shapes = [jax.ShapeDtypeStruct((N_DEV,) + x.shape, x.dtype) for x in xs]

    def body(*refs):
        x_refs, out_refs = refs[:n], refs[n:2 * n]
        send_sems, recv_sems, local_sems = refs[2 * n:]
        x, y, c = lax.axis_index("x"), lax.axis_index("y"), lax.axis_index("c")
        me, sibling = (x, y, c), (x, y, 1 - c)
        chips = [(1 - x, y), (x, 1 - y), (1 - x, 1 - y)]

        def slot(a, dev):
            return out_refs[a].at[4 * dev[0] + 2 * dev[1] + dev[2]]

        def copy(a, k, block, to, src=None):
            return pltpu.make_async_remote_copy(
                src_ref=slot(a, block) if src is None else src, dst_ref=slot(a, block),
                send_sem=send_sems.at[a, k], recv_sem=recv_sems.at[a, k], device_id=to, device_id_type=MESH)

        mine = [pltpu.make_async_copy(x_refs[a], slot(a, me), local_sems.at[a]) for a in range(n)]
        first = [copy(a, 0, me, sibling, src=x_refs[a]) for a in range(n)]
        first += [copy(a, 1 + j, me, (*chip, c), src=x_refs[a]) for a in range(n) for j, chip in enumerate(chips)]
        for cp in mine + first:
            cp.start()
        passed = []
        for a in range(n):
            for j, chip in enumerate(chips):
                copy(a, 1 + j, (*chip, c), me).wait_recv()
                passed.append(copy(a, 4 + j, (*chip, c), sibling))
                passed[-1].start()
        for a in range(n):
            copy(a, 0, sibling, me).wait_recv()
            for j, chip in enumerate(chips):
                copy(a, 4 + j, (*chip, 1 - c), me).wait_recv()
        for cp in first + passed:
            cp.wait_send()
        for cp in mine:
            cp.wait()

    return pl.pallas_call(
        body, name=name, in_specs=[HBM_SPEC] * n, out_specs=[HBM_SPEC] * n, out_shape=out_shapes,
        scratch_shapes=[pltpu.SemaphoreType.DMA((n, N_DEV - 1)), pltpu.SemaphoreType.DMA((n, N_DEV - 1)),
                        pltpu.SemaphoreType.DMA((n,))],
        compiler_params=pltpu.CompilerParams(has_side_effects=True),
    )(*xs)


def _exchange(xs, *, name, scatter):
    ex = _Exchange(xs, scatter)

    def body(*refs):
        _, _, _, xrefs = ex.split(refs, 0, 0)
        ex.start(*xrefs)
        ex.wait(*xrefs)

    return pl.pallas_call(
        body, name=name, in_specs=ex.in_specs, out_specs=ex.out_specs, out_shape=ex.out_shapes,
        scratch_shapes=ex.scratch, compiler_params=pltpu.CompilerParams(has_side_effects=True),
    )(*xs)


def _ada_fwd(c_all, w_shard, b_shard):
    n = w_shard.shape[1]

    def body(c_ref, w_ref, b_ref, o_ref):
        cv = c_ref[...]
        act = (cv * _sigmoid(cv)).astype(BF16)
        o_ref[...] = jnp.dot(act, w_ref[...].astype(BF16), preferred_element_type=F32) + b_ref[...]

    return pl.pallas_call(
        body, name="ada_fwd", in_specs=[_full((N_DEV, D)), _full((D, n)), _full((1, n))],
        out_specs=_full((N_DEV, n)), out_shape=jax.ShapeDtypeStruct((N_DEV, n), F32), grid=(1,),
        compiler_params=_params(("arbitrary",)),
    )(c_all, w_shard, b_shard)


def _ada_bwd(c_all_t, dmod_pad):
    n = dmod_pad.shape[1]

    def body(c_ref, d_ref, o_ref):
        cv = c_ref[...]
        act = (cv * _sigmoid(cv)).astype(BF16)
        o_ref[...] = jnp.dot(act, d_ref[...].astype(BF16), preferred_element_type=F32)

    return pl.pallas_call(
        body, name="ada_bwd", in_specs=[_full((D, LANES)), _full((LANES, n))],
        out_specs=_full((D, n)), out_shape=jax.ShapeDtypeStruct((D, n), F32), grid=(1,),
        compiler_params=_params(("arbitrary",)),
    )(c_all_t, dmod_pad)


ADAM_ROWS = 256


def _adamw(parts, w, m, v, *, name):
    n, R, C = parts.shape
    tr = next((t for t in (ADAM_ROWS, 128, 64, 32, 16, SUBLANES) if R % t == 0), R)

    def body(p_ref, w_ref, m_ref, v_ref, g_ref, d_ref, nm_ref, nv_ref):
        g = p_ref[0].astype(F32)
        for j in range(1, n):
            g = g + p_ref[j].astype(F32)
        g_ref[...] = g
        nm = ADAM_B1 * m_ref[...] + (1.0 - ADAM_B1) * g
        nv = ADAM_B2 * v_ref[...] + (1.0 - ADAM_B2) * (g * g)
        nm_ref[...] = nm
        nv_ref[...] = nv
        m_hat = nm / (1.0 - ADAM_B1 ** ADAM_STEP)
        v_hat = nv / (1.0 - ADAM_B2 ** ADAM_STEP)
        d_ref[...] = -ADAM_LR * (m_hat / (jnp.sqrt(v_hat) + ADAM_EPS) + ADAM_WD * w_ref[...])

    row = pl.BlockSpec((tr, C), lambda i: (i, 0))
    return pl.pallas_call(
        body, name=name, grid=(R // tr,),
        in_specs=[pl.BlockSpec((n, tr, C), lambda i: (0, i, 0)), row, row, row], out_specs=[row] * 4,
        out_shape=[jax.ShapeDtypeStruct((R, C), F32)] * 4,
        compiler_params=_params(("parallel",)),
    )(parts, w, m, v)


SHARDED = (("w_in", D, IN_W, 1), ("w_branch_a", CONV_W, D, 1), ("w_branch_b", ATTN_W, D, 1), ("w_out", D, D, 0),
           ("w_up", D, 2 * D_FF, 1), ("w_down", D_FF, D, 0), ("conv_a_w", 3, CONV_W, 1),
           ("conv_ffn_w", 3, 2 * D_FF, 1))
MATRICES = SHARDED[:6]
LATE = MATRICES[1:]
CONVS = SHARDED[6:]
REPLICATED = (("b_ada", N_MOD * D), ("norm1_g", D), ("norm2_g", D), ("b_f", N_HEADS), ("q_norm_g", HEAD_DIM),
              ("k_norm_g", HEAD_DIM))


def _shard_shape(rows, cols, axis):
    return (rows // N_DEV, cols) if axis == 0 else (rows, cols // N_DEV)


def _pack_rows(flat, multiple):
    length = flat.shape[-1]
    rows = -(-length // PACK_W)
    rows = -(-rows // multiple) * multiple
    pad = [(0, 0)] * (flat.ndim - 1) + [(0, rows * PACK_W - length)]
    return jnp.pad(flat, pad).reshape(flat.shape[:-1] + (rows, PACK_W))


def _pack_shards(shards, spec, multiple, dtype):
    flat = jnp.concatenate([shards[name].reshape(-1).astype(dtype) for name, *_ in spec])
    return _pack_rows(flat, multiple)


def _join_lane_blocks(gathered):
    n, r, c = gathered.shape

    def body(g_ref, o_ref):
        for j in range(n):
            o_ref[:, j * c:(j + 1) * c] = g_ref[j]

    return pl.pallas_call(
        body, name="join_lane_blocks", grid=(1,), in_specs=[_full((n, r, c))], out_specs=_full((r, n * c)),
        out_shape=jax.ShapeDtypeStruct((r, n * c), gathered.dtype), compiler_params=_params(("arbitrary",)),
    )(gathered)


SHARD_PAD = 768


def _assemble_columns(gathered, shard_cols, out_cols, segments, *, name):
    n, rows, padw = gathered.shape
    assert n == N_DEV and padw == SHARD_PAD and shard_cols <= SHARD_PAD

    def body(g_ref, o_ref):
        j = pl.program_id(0)

        @pl.when(j == 0)
        def _():
            o_ref[...] = jnp.zeros_like(o_ref)

        for dev in range(N_DEV):
            @pl.when(j == dev)
            def _(dev=dev):
                x = g_ref[0]
                for lo, hi, delta in segments:
                    a, b = max(lo, dev * shard_cols), min(hi, (dev + 1) * shard_cols)
                    if a >= b:
                        continue
                    base = (a + delta) // LANES * LANES
                    width = -(-(b + delta - base) // LANES) * LANES
                    src = lax.broadcasted_iota(jnp.int32, (padw, width), 0) + dev * shard_cols
                    dst = lax.broadcasted_iota(jnp.int32, (padw, width), 1) + (base - delta)
                    place = jnp.where((src == dst) & (src >= a) & (src < b), 1.0, 0.0).astype(BF16)
                    moved = jnp.dot(x, place, preferred_element_type=F32).astype(BF16)
                    o_ref[:, base:base + width] = o_ref[:, base:base + width] + moved

    return pl.pallas_call(
        body, name=name, grid=(N_DEV,), in_specs=[pl.BlockSpec((1, rows, padw), lambda j: (j, 0, 0))],
        out_specs=_full((rows, out_cols)), out_shape=jax.ShapeDtypeStruct((rows, out_cols), BF16),
        compiler_params=_params(("arbitrary",)),
    )(gathered)


def _pad_shard(w):
    return jnp.pad(w.astype(BF16), ((0, 0), (0, SHARD_PAD - w.shape[1])))


W_IN_SEGMENTS = ((0, COL_GA, 0), (COL_GA, COL_GA + N_HEADS, COL_F - COL_GA), (COL_GA + N_HEADS, IN_W, -N_HEADS))


def _join_shards(gathered, axis):
    if axis == 0:
        return gathered.reshape(N_DEV * gathered.shape[1], gathered.shape[2])
    if gathered.shape[2] == LANES:
        return _join_lane_blocks(gathered)
    return jnp.concatenate([gathered[j] for j in range(N_DEV)], axis=1)


def _column_shards(pieces):
    total = sum(p.shape[1] for p in pieces)
    width = total // N_DEV
    shards = []
    for j in range(N_DEV):
        lo, hi, off, segs = j * width, (j + 1) * width, 0, []
        for p in pieces:
            a, b = max(lo, off), min(hi, off + p.shape[1])
            if a < b:
                segs.append(p[:, a - off:b - off])
            off += p.shape[1]
        shards.append(segs[0] if len(segs) == 1 else jnp.concatenate(segs, axis=1))
    return jnp.stack(shards)


def _split_shards(full, axis):
    if axis == 0:
        return full.reshape(N_DEV, full.shape[0] // N_DEV, full.shape[1])
    c = full.shape[1] // N_DEV
    return jnp.stack([full[:, j * c:(j + 1) * c] for j in range(N_DEV)])


def _unpack_shards(packed, spec):
    flat = packed.reshape(-1)
    out, off = {}, 0
    for name, rows, cols, axis in spec:
        r, c = _shard_shape(rows, cols, axis)
        out[name] = flat[off:off + r * c].reshape(r, c)
        off += r * c
    return out


def _unpack_gathered(gathered, spec):
    flat = gathered.reshape(N_DEV, -1)
    out, off = {}, 0
    for name, rows, cols, axis in spec:
        r, c = _shard_shape(rows, cols, axis)
        seg = flat[:, off:off + r * c].reshape(N_DEV, r, c)
        out[name] = seg.reshape(rows, cols) if axis == 0 else seg.transpose(1, 0, 2).reshape(rows, cols)
        off += r * c
    return out


def _pack_full_by_dest(full, spec, multiple):
    segs = []
    for name, rows, cols, axis in spec:
        r, c = _shard_shape(rows, cols, axis)
        a = full[name]
        seg = a.reshape(N_DEV, r, c) if axis == 0 else a.reshape(rows, N_DEV, c).transpose(1, 0, 2)
        segs.append(seg.reshape(N_DEV, r * c))
    return _pack_rows(jnp.concatenate(segs, axis=1), multiple)


def _pad_head_rows(w):
    n = w.shape[1]
    padded = jnp.pad(w.reshape(N_HEADS, HEAD_DIM, n), ((0, 0), (0, LANES - HEAD_DIM), (0, 0)))
    return padded.reshape(N_HEADS * LANES, n)


def kernel(x, c, w_ada, b_ada, norm1_g, w_in, b_f, conv_a_w, q_norm_g, k_norm_g, w_branch_a, w_branch_b, w_out, norm2_g, w_up, conv_ffn_w, w_down, loss_target, m_w_ada, m_b_ada, m_norm1_g, m_w_in, m_b_f, m_conv_a_w, m_q_norm_g, m_k_norm_g, m_w_branch_a, m_w_branch_b, m_w_out, m_norm2_g, m_w_up, m_conv_ffn_w, m_w_down, v_w_ada, v_b_ada, v_norm1_g, v_w_in, v_b_f, v_conv_a_w, v_q_norm_g, v_k_norm_g, v_w_branch_a, v_w_branch_b, v_w_out, v_norm2_g, v_w_up, v_conv_ffn_w, v_w_down):
    names = ("w_ada", "b_ada", "norm1_g", "w_in", "b_f", "conv_a_w", "q_norm_g", "k_norm_g", "w_branch_a",
             "w_branch_b", "w_out", "norm2_g", "w_up", "conv_ffn_w", "w_down")
    squeeze = lambda a: a[0] if a.ndim == 3 else a
    W = dict(zip(names, map(squeeze, (w_ada, b_ada, norm1_g, w_in, b_f, conv_a_w, q_norm_g, k_norm_g, w_branch_a,
                                      w_branch_b, w_out, norm2_g, w_up, conv_ffn_w, w_down))))
    M = dict(zip(names, map(squeeze, (m_w_ada, m_b_ada, m_norm1_g, m_w_in, m_b_f, m_conv_a_w, m_q_norm_g,
                                      m_k_norm_g, m_w_branch_a, m_w_branch_b, m_w_out, m_norm2_g, m_w_up,
                                      m_conv_ffn_w, m_w_down))))
    V = dict(zip(names, map(squeeze, (v_w_ada, v_b_ada, v_norm1_g, v_w_in, v_b_f, v_conv_a_w, v_q_norm_g,
                                      v_k_norm_g, v_w_branch_a, v_w_branch_b, v_w_out, v_norm2_g, v_w_up,
                                      v_conv_ffn_w, v_w_down))))
    me = 4 * lax.axis_index("x") + 2 * lax.axis_index("y") + lax.axis_index("c")
    ada_n = N_MOD * D // N_DEV

    small = jnp.concatenate([c.reshape(-1), W["conv_a_w"].reshape(-1), W["conv_ffn_w"].reshape(-1)])
    small_all, w_in_all = _gather_two_level([_pack_rows(small, SUBLANES), _pad_shard(W["w_in"])],
                                            name="gather_first")
    small_all = small_all.reshape(N_DEV, -1)
    c_all = small_all[:, :D]
    conv_all = _unpack_gathered(small_all[:, D:], CONVS)

    b_shard = lax.dynamic_slice(W["b_ada"], (0, me * ada_n), (1, ada_n))
    mod_part = _ada_fwd(c_all, W["w_ada"], b_shard)
    mod_all, = _exchange([mod_part], name="gather_mod", scatter=False)
    mod = lax.dynamic_index_in_dim(mod_all, me, axis=1, keepdims=False).reshape(1, N_MOD * D)

    wts = {"w_in": _assemble_columns(w_in_all, IN_W // N_DEV, IN_W_PAD, W_IN_SEGMENTS, name="assemble_w_in")}
    wts["w_in_t"] = wts["w_in"].T
    wts.update(conv_all)
    for name in ("norm1_g", "norm2_g", "q_norm_g", "k_norm_g", "b_f"):
        wts[name] = W[name]
    late = {name: _pad_shard(W[name]) if name == "w_up" else W[name].astype(BF16) for name, *_ in LATE}

    sq, grad_x, grads, parts = _local_step(x[0], loss_target[0], mod, wts, late)
    loss = lax.psum(sq[0, 0] * (0.5 / D), AXES)

    grads["b_ada"] = grads["mod"]
    rep_flat = lambda src: jnp.concatenate([src[name].reshape(-1) for name, _ in REPLICATED])
    rep_parts, = _exchange([_pack_rows(rep_flat(grads), 16)], name="gather_small_grads", scatter=False)
    rep_out = _adamw(rep_parts, *[_pack_rows(rep_flat(s), 16) for s in (W, M, V)], name="adamw_replicated")

    dmod_all = rep_parts.reshape(N_DEV, -1)[:, :N_MOD * D]
    dmod_mine = lax.dynamic_slice(dmod_all, (0, me * ada_n), (N_DEV, ada_n))
    g_ada = _ada_bwd(jnp.pad(c_all.T, ((0, 0), (0, LANES - N_DEV))),
                     jnp.pad(dmod_mine, ((0, LANES - N_DEV), (0, 0))))
    ada_out = _adamw(g_ada[None], W["w_ada"], M["w_ada"], V["w_ada"], name="adamw_ada")

    mat_out = {name: _adamw(parts[name], W[name], M[name], V[name], name="adamw_" + name) for name, *_ in MATRICES}
    conv_out = _adamw(parts["conv"], *[_pack_shards(s, CONVS, SUBLANES, F32) for s in (W, M, V)],
                      name="adamw_conv")

    results = []
    for kind in range(4):
        per = {"w_ada": ada_out[kind]}
        per.update({name: out[kind] for name, out in mat_out.items()})
        per.update(_unpack_shards(conv_out[kind], CONVS))
        flat, off = rep_out[kind].reshape(-1), 0
        for name, n in REPLICATED:
            per[name] = flat[off:off + n].reshape(1, n)
            off += n
        results.append(per)
    restore = lambda name, a: a[None] if W[name].ndim == 2 and name not in dict(REPLICATED) else a
    outs = [loss, grad_x[None]]
    for per in results:
        outs.extend(restore(name, per[name]) for name in names)
    return tuple(outs)
```

```python
import functools

import jax
import jax.numpy as jnp
import numpy as np
from jax import lax
from jax.experimental import pallas as pl
from jax.experimental.pallas import tpu as pltpu

F32 = jnp.float32
BF16 = jnp.bfloat16

N_DEV = 8
D = 1024
N_HEADS = 8
HEAD_DIM = 64
ATTN_W = 512
CONV_W = 512
D_FF = 2816
N_MOD = 6
IN_W = 5128
RMS_EPS = 1e-6
NEG_INF = -1e30

IN_W_PAD = 5376
COL_GA = 3072
COL_GB = 4096
COL_F = 5120
F_PAD = 128

ADAM_LR = 0.001
ADAM_B1 = 0.9
ADAM_B2 = 0.999
ADAM_EPS = 1e-08
ADAM_WD = 0.01
ADAM_STEP = 10

LANES = 128
SUBLANES = 8
BF16_ROWS = 16
VMEM_LIMIT = 52 * 1024 * 1024
TOKEN_TILE = 512
MATMUL_TILE = 1024
ATTN_BLOCK = 512
ATTN_FWD_HEADS = 1
PACK_W = 1024

MESH = pl.DeviceIdType.MESH
AXES = ("x", "y", "c")


def _params(sem=None, **kw):
    return pltpu.CompilerParams(dimension_semantics=sem, vmem_limit_bytes=VMEM_LIMIT, **kw)


def _full(shape):
    nd = len(shape)
    return pl.BlockSpec(shape, lambda *_: (0,) * nd)


def _tn_dot(a, b):
    return lax.dot_general(a, b, (((0,), (0,)), ((), ())), preferred_element_type=F32)


def _matmul(a, b, *, name, tm, tn, tk, out_dtype=F32, trans_a=False, exchange=None):
    if trans_a:
        K, M = a.shape
    else:
        M, K = a.shape
    N = b.shape[1]
    assert b.shape[0] == K and M % tm == 0 and N % tn == 0 and K % tk == 0, (name, a.shape, b.shape)
    nm, nn, nk = M // tm, N // tn, K // tk

    def body(*refs):
        if exchange is None:
            a_ref, b_ref, o_ref, *own = refs
        else:
            (a_ref, b_ref), (o_ref,), own, xrefs = exchange.split(refs, 2, 1)
            ids = [pl.program_id(d) for d in range(3)]
            first = jnp.logical_and(jnp.logical_and(ids[0] == 0, ids[1] == 0), ids[2] == 0)
            last = jnp.logical_and(jnp.logical_and(ids[0] == nn - 1, ids[1] == nm - 1), ids[2] == nk - 1)
            _ride(exchange, first, last, xrefs)
        k = pl.program_id(2)
        av = a_ref[...].astype(BF16)
        bv = b_ref[...].astype(BF16)
        prod = _tn_dot(av, bv) if trans_a else jnp.dot(av, bv, preferred_element_type=F32)
        if nk == 1:
            o_ref[...] = prod.astype(out_dtype)
            return
        acc_ref, = own

        @pl.when(k == 0)
        def _():
            acc_ref[...] = prod

        @pl.when(k > 0)
        def _():
            acc_ref[...] += prod

        @pl.when(k == nk - 1)
        def _():
            o_ref[...] = acc_ref[...].astype(out_dtype)

    if trans_a:
        a_spec = pl.BlockSpec((tk, tm), lambda j, i, k: (k, i))
    else:
        a_spec = pl.BlockSpec((tm, tk), lambda j, i, k: (i, k))
    in_specs = [a_spec, pl.BlockSpec((tk, tn), lambda j, i, k: (k, j))]
    out_spec = pl.BlockSpec((tm, tn), lambda j, i, k: (i, j))
    out_shape = jax.ShapeDtypeStruct((M, N), out_dtype)
    scratch = [pltpu.VMEM((tm, tn), F32)] if nk > 1 else []
    if exchange is None:
        return pl.pallas_call(
            body, name=name, grid=(nn, nm, nk), in_specs=in_specs, out_specs=out_spec, out_shape=out_shape,
            scratch_shapes=scratch, compiler_params=_params(("parallel", "parallel", "arbitrary")),
        )(a, b)
    return pl.pallas_call(
        body, name=name, grid=(nn, nm, nk), in_specs=in_specs + exchange.in_specs,
        out_specs=[out_spec] + exchange.out_specs, out_shape=[out_shape] + exchange.out_shapes,
        scratch_shapes=scratch + exchange.scratch, compiler_params=_params(("arbitrary",) * 3),
    )(a, b, *exchange.xs)


def _norm_bwd_tile(dh, ins, outs, first):
    x_ref, dr_ref, g_ref, sc_ref = ins[:4]
    dx_ref, dsh_ref, dsc_ref, dg_ref = outs[:4]

    @pl.when(first)
    def _():
        for ref in outs[1:4] + outs[5:]:
            ref[...] = jnp.zeros_like(ref)

    xv = x_ref[...]
    gv = g_ref[...]
    one_sc = 1.0 + sc_ref[...]
    inv = lax.rsqrt(jnp.mean(xv * xv, axis=-1, keepdims=True) + RMS_EPS)
    xn = xv * inv
    dxn = dh * (gv * one_sc)
    dx = dr_ref[...] + inv * (dxn - xn * jnp.mean(dxn * xn, axis=-1, keepdims=True))
    dx_ref[...] = dx
    dhxn = dh * xn
    dsh_ref[...] += _rows8(dh)
    dsc_ref[...] += _rows8(dhxn * gv)
    dg_ref[...] += _rows8(dhxn * one_sc)
    if len(ins) == 6:
        mix_ref, g1_ref = ins[4:]
        dmix_ref, dg1_ref = outs[4:]
        dmix_ref[...] = (dx * g1_ref[...]).astype(BF16)
        dg1_ref[...] += _rows8(dx * mix_ref[...])


def _matmul_pieces(pieces, b, *, name, tm, exchange=None, norm_bwd=None):
    M = pieces[0].shape[0]
    widths = [p.shape[1] for p in pieces]
    offsets = [sum(widths[:i]) for i in range(len(widths))]
    N = b.shape[1]
    assert b.shape[0] >= sum(widths) and M % tm == 0, (name, widths, b.shape)
    n_p, nm = len(pieces), M // tm
    extra = list(norm_bwd) if norm_bwd is not None else []
    n_in = n_p + 1 + len(extra)
    n_out = len(extra) if norm_bwd is not None else 1

    def body(*refs):
        i = pl.program_id(0)
        if exchange is None:
            ins, outs = refs[:n_in], refs[n_in:]
        else:
            ins, outs, _, xrefs = exchange.split(refs, n_in, n_out)
            _ride(exchange, i == 0, i == nm - 1, xrefs)
        b_ref = ins[n_p]
        acc = None
        for a_ref, off, w in zip(ins[:n_p], offsets, widths):
            term = jnp.dot(a_ref[...].astype(BF16), b_ref[off:off + w, :], preferred_element_type=F32)
            acc = term if acc is None else acc + term
        if norm_bwd is None:
            outs[0][...] = acc
        else:
            _norm_bwd_tile(acc, tuple(ins[n_p + 1:]), tuple(outs), first=i == 0)

    row, vec, part = pl.BlockSpec((tm, N), lambda i: (i, 0)), _full((1, N)), _full((SUBLANES, N))
    part_shape = jax.ShapeDtypeStruct((SUBLANES, N), F32)
    in_specs = [pl.BlockSpec((tm, w), lambda i: (i, 0)) for w in widths] + [_full(b.shape)]
    out_specs, out_shape = [row], [jax.ShapeDtypeStruct((M, N), F32)]
    if norm_bwd is not None:
        in_specs += [row, row, vec, vec] + ([row, vec] if len(extra) == 6 else [])
        out_specs += [part] * 3 + ([row, part] if len(extra) == 6 else [])
        out_shape += [part_shape] * 3 + ([jax.ShapeDtypeStruct((M, N), BF16), part_shape] if len(extra) == 6 else [])
    sequential = exchange is not None or norm_bwd is not None
    xs = exchange.xs if exchange is not None else []
    result = pl.pallas_call(
        body, name=name, grid=(nm,), in_specs=in_specs + (exchange.in_specs if exchange else []),
        out_specs=out_specs + (exchange.out_specs if exchange else []),
        out_shape=out_shape + (exchange.out_shapes if exchange else []),
        scratch_shapes=exchange.scratch if exchange else [],
        compiler_params=_params(("arbitrary" if sequential else "parallel",)),
    )(*pieces, b, *extra, *xs)
    return result[0] if len(result) == 1 else result


def _matmul_tn_pieces(a, pieces, *, name, tk):
    K, M = a.shape
    widths = [p.shape[1] for p in pieces]
    n_p, nk = len(pieces), K // tk

    def body(*refs):
        a_ref, p_refs, o_refs = refs[0], refs[1:n_p + 1], refs[n_p + 1:]
        k = pl.program_id(0)
        av = a_ref[...].astype(BF16)
        for p_ref, o_ref in zip(p_refs, o_refs):
            prod = _tn_dot(av, p_ref[...].astype(BF16))

            @pl.when(k == 0)
            def _():
                o_ref[...] = prod

            @pl.when(k > 0)
            def _():
                o_ref[...] += prod

    return pl.pallas_call(
        body, name=name, grid=(nk,),
        in_specs=[pl.BlockSpec((tk, M), lambda k: (k, 0))] + [pl.BlockSpec((tk, w), lambda k: (k, 0)) for w in widths],
        out_specs=[_full((M, w)) for w in widths], out_shape=[jax.ShapeDtypeStruct((M, w), F32) for w in widths],
        compiler_params=_params(("arbitrary",)),
    )(a, *pieces)


def _split_dot(x, mat, parts):
    out = None
    rem = x
    for p in range(parts):
        piece = rem.astype(BF16)
        term = jnp.dot(piece, mat, preferred_element_type=F32)
        out = term if out is None else out + term
        if p + 1 < parts:
            rem = rem - piece.astype(F32)
    return out


def _sigmoid(x):
    return 1.0 / (1.0 + jnp.exp(-x))


def _rows8(x):
    r, c = x.shape
    return jnp.sum(x.reshape(r // SUBLANES, SUBLANES, c), axis=0)


def _shift_down(blk, prev8, n):
    rolled = pltpu.roll(blk, n, axis=0)
    prev_rolled = pltpu.roll(prev8, n, axis=0)
    rows = lax.broadcasted_iota(jnp.int32, prev8.shape, 0)
    first = jnp.where(rows < n, prev_rolled, rolled[0:SUBLANES])
    return jnp.concatenate([first, rolled[SUBLANES:]], axis=0)


def _shift_up(blk, next8, n):
    r = blk.shape[0]
    rolled = pltpu.roll(blk, r - n, axis=0)
    next_rolled = pltpu.roll(next8, SUBLANES - n, axis=0)
    rows = lax.broadcasted_iota(jnp.int32, next8.shape, 0)
    last = jnp.where(rows >= SUBLANES - n, next_rolled, rolled[r - SUBLANES:])
    return jnp.concatenate([rolled[:r - SUBLANES], last], axis=0)


def _prev_spec(tm, width, col):
    per = tm // SUBLANES
    return pl.BlockSpec((SUBLANES, width), lambda i, *_: (jnp.maximum(i * per - 1, 0), col))


def _next_spec(tm, width, col, n_tiles):
    per = tm // SUBLANES
    last = n_tiles * per - 1
    return pl.BlockSpec((SUBLANES, width), lambda i, *_: (jnp.minimum((i + 1) * per, last), col))


def _group_matrix():
    idx = np.arange(ATTN_W) // HEAD_DIM
    return jnp.asarray((idx[:, None] == idx[None, :]).astype(np.float32), BF16)


def _norm_mod(x, g, sc, sh, *, name):
    T = x.shape[0]
    tm = min(TOKEN_TILE, T)

    def body(x_ref, g_ref, sc_ref, sh_ref, o_ref):
        xv = x_ref[...]
        inv = lax.rsqrt(jnp.mean(xv * xv, axis=-1, keepdims=True) + RMS_EPS)
        o_ref[...] = ((xv * inv) * g_ref[...] * (1.0 + sc_ref[...]) + sh_ref[...]).astype(BF16)

    row = pl.BlockSpec((tm, D), lambda i: (i, 0))
    return pl.pallas_call(
        body, name=name, grid=(T // tm,),
        in_specs=[row, _full((1, D)), _full((1, D)), _full((1, D))],
        out_specs=row, out_shape=jax.ShapeDtypeStruct((T, D), BF16),
        compiler_params=_params(("parallel",)),
    )(x, g, sc, sh)


LANE_ONE = 64
LANE_F = 67
LANE_LSE = 70
LANE_SUM = 73


def _pieces(x):
    hi = x.astype(BF16).astype(F32)
    rest = x - hi
    mid = rest.astype(BF16).astype(F32)
    return hi, mid, rest - mid


def _aug(lane, data, entries):
    out = jnp.where(lane < HEAD_DIM, data, 0.0)
    for idx, val in entries:
        out = jnp.where(lane == idx, val, out)
    return out


def _run(start, vals):
    return [(start + i, v) for i, v in enumerate(vals)]


def _head_lanes(a, h):
    blk = a[:, LANES * (h // 2):LANES * (h // 2) + LANES]
    return blk if h % 2 == 0 else pltpu.roll(blk, HEAD_DIM, axis=1)


def _branch_prep(proj, fcum, conv_w8, qg, kg, gmat):
    T = proj.shape[0]
    tm = min(TOKEN_TILE, T)
    nt = T // tm

    def body(cb_ref, cc_ref, cv_ref, q_ref, k_ref, v_ref, f_ref, ccp_ref, cvp_ref, w_ref, qg_ref, kg_ref, g_ref,
             ya_ref, qa_ref, ka_ref, va_ref):
        i = pl.program_id(0)
        z = cc_ref[...] * cv_ref[...]
        zp = jnp.where(i > 0, ccp_ref[...] * cvp_ref[...], 0.0)
        w = w_ref[...]
        cz = _shift_down(z, zp, 2) * w[0:1] + _shift_down(z, zp, 1) * w[1:2] + z * w[2:3]
        ya_ref[...] = (cb_ref[...] * cz).astype(BF16)
        gm = g_ref[...]

        def normed(src, gain, scale):
            v = src[...]
            ms = _split_dot(v * v, gm, 2) * (1.0 / HEAD_DIM)
            return (v * lax.rsqrt(ms + RMS_EPS)) * gain[...] * scale

        qn = normed(q_ref, qg_ref, 1.0 / np.sqrt(HEAD_DIM))
        kn = normed(k_ref, kg_ref, 1.0)
        vv = v_ref[...]
        fall = f_ref[...]
        lane = lax.broadcasted_iota(jnp.int32, (tm, LANES), 1)
        ones3 = [1.0, 1.0, 1.0]
        for h in range(N_HEADS):
            hi, mid, lo = _pieces(fall[:, h:h + 1])
            qa_ref[h] = _aug(lane, _head_lanes(qn, h), _run(LANE_ONE, ones3) + _run(LANE_F, [hi, mid, lo])
                             + [(LANE_SUM, 1.0)]).astype(BF16)
            ka_ref[h] = _aug(lane, _head_lanes(kn, h), _run(LANE_ONE, [-hi, -mid, -lo]) + _run(LANE_F, ones3)
                             + _run(LANE_LSE, ones3)).astype(BF16)
            va_ref[h] = _aug(lane, _head_lanes(vv, h), _run(LANE_ONE, ones3)).astype(BF16)

    blk = lambda col: pl.BlockSpec((tm, CONV_W), lambda i: (i, col))
    heads = pl.BlockSpec((N_HEADS, tm, LANES), lambda i: (0, i, 0))
    return pl.pallas_call(
        body, name="branch_prep", grid=(nt,),
        in_specs=[blk(0), blk(1), blk(2), blk(3), blk(4), blk(5), pl.BlockSpec((tm, F_PAD), lambda i: (i, 0)),
                  _prev_spec(tm, CONV_W, 1), _prev_spec(tm, CONV_W, 2),
                  _full((SUBLANES, CONV_W)), _full((1, ATTN_W)), _full((1, ATTN_W)), _full((ATTN_W, ATTN_W))],
        out_specs=[pl.BlockSpec((tm, CONV_W), lambda i: (i, 0)), heads, heads, heads],
        out_shape=[jax.ShapeDtypeStruct((T, CONV_W), BF16)] + [jax.ShapeDtypeStruct((N_HEADS, T, LANES), BF16)] * 3,
        compiler_params=_params(("parallel",)),
    )(proj, proj, proj, proj, proj, proj, fcum, proj, proj, conv_w8, qg, kg, gmat)


def _cumsum(x, *, reverse, name, col=0, gate_bias=None):
    T = x.shape[0]
    tm = min(TOKEN_TILE, T)
    nt = T // tm

    def body(x_ref, b_ref, o_ref, carry_ref):
        i = pl.program_id(0)

        @pl.when(i == 0)
        def _():
            carry_ref[...] = jnp.zeros_like(carry_ref)

        r = lax.broadcasted_iota(jnp.int32, (tm, tm), 0)
        c = lax.broadcasted_iota(jnp.int32, (tm, tm), 1)
        tri = jnp.where((c >= r) if reverse else (c <= r), 1.0, 0.0).astype(BF16)
        xv = x_ref[...]
        if gate_bias is not None:
            fx = xv + b_ref[...]
            xv = jnp.minimum(fx, 0.0) - jnp.log(1.0 + jnp.exp(-jnp.abs(fx)))
        out = _split_dot_left(tri, xv, 3) + carry_ref[0:1]
        o_ref[...] = out
        carry_ref[...] = jnp.broadcast_to(out[0:1] if reverse else out[tm - 1:tm], carry_ref.shape)

    rows = (lambda i: nt - 1 - i) if reverse else (lambda i: i)
    bias = jnp.zeros((1, F_PAD), F32) if gate_bias is None else gate_bias
    return pl.pallas_call(
        body, name=name, grid=(nt,),
        in_specs=[pl.BlockSpec((tm, F_PAD), lambda i: (rows(i), col)), _full((1, F_PAD))],
        out_specs=pl.BlockSpec((tm, F_PAD), lambda i: (rows(i), 0)),
        out_shape=jax.ShapeDtypeStruct((T, F_PAD), F32),
        scratch_shapes=[pltpu.VMEM((SUBLANES, F_PAD), F32)],
        compiler_params=_params(("arbitrary",)),
    )(x, bias)


def _split_dot_left(mat, x, parts):
    out = None
    rem = x
    for p in range(parts):
        piece = rem.astype(BF16)
        term = jnp.dot(mat, piece, preferred_element_type=F32)
        out = term if out is None else out + term
        if p + 1 < parts:
            rem = rem - piece.astype(F32)
    return out


def _out_resid_norm(x, merged, w_out, g1, g, sc, sh):
    T = x.shape[0]
    tm = min(TOKEN_TILE, T)

    def body(x_ref, m_ref, w_ref, g1_ref, g_ref, sc_ref, sh_ref, mix_ref, x1_ref, h_ref):
        mix = jnp.dot(m_ref[...], w_ref[...], preferred_element_type=F32)
        mix_ref[...] = mix
        x1 = x_ref[...] + g1_ref[...] * mix
        x1_ref[...] = x1
        inv = lax.rsqrt(jnp.mean(x1 * x1, axis=-1, keepdims=True) + RMS_EPS)
        h_ref[...] = ((x1 * inv) * g_ref[...] * (1.0 + sc_ref[...]) + sh_ref[...]).astype(BF16)

    row = pl.BlockSpec((tm, D), lambda i: (i, 0))
    vec = _full((1, D))
    return pl.pallas_call(
        body, name="out_resid_norm", grid=(T // tm,),
        in_specs=[row, row, _full((D, D)), vec, vec, vec, vec], out_specs=[row, row, row],
        out_shape=[jax.ShapeDtypeStruct((T, D), F32), jax.ShapeDtypeStruct((T, D), F32),
                   jax.ShapeDtypeStruct((T, D), BF16)],
        compiler_params=_params(("parallel",)),
    )(x, merged, w_out, g1, g, sc, sh)


FFN_TM = 256
FFN_TC = 1408


def _tanh_sigmoid(x):
    return 0.5 * jnp.tanh(0.5 * x) + 0.5


def _ffn_act_fwd(u, w8):
    T = u.shape[0]
    tm = min(FFN_TM, T)
    nt = T // tm
    nc = D_FF // FFN_TC

    def body(ug_ref, uv_ref, ugp_ref, uvp_ref, wg_ref, wv_ref, o_ref, cg_ref, cv_ref):
        i = pl.program_id(1)

        def conv(u_ref, p_ref, w_ref):
            uv = u_ref[...]
            up = jnp.where(i > 0, p_ref[...], 0.0)
            w = w_ref[...]
            return _shift_down(uv, up, 2) * w[0:1] + _shift_down(uv, up, 1) * w[1:2] + uv * w[2:3]

        gate = conv(ug_ref, ugp_ref, wg_ref)
        val = conv(uv_ref, uvp_ref, wv_ref)
        cg_ref[...] = gate.astype(BF16)
        cv_ref[...] = val.astype(BF16)
        o_ref[...] = (gate * _tanh_sigmoid(gate) * val).astype(BF16)

    per = tm // SUBLANES
    blk = lambda off: pl.BlockSpec((tm, FFN_TC), lambda j, i: (i, j + off))
    prev = lambda off: pl.BlockSpec((SUBLANES, FFN_TC), lambda j, i: (jnp.maximum(i * per - 1, 0), j + off))
    wblk = lambda off: pl.BlockSpec((SUBLANES, FFN_TC), lambda j, i: (0, j + off))
    return pl.pallas_call(
        body, name="ffn_act_fwd", grid=(nc, nt),
        in_specs=[blk(0), blk(nc), prev(0), prev(nc), wblk(0), wblk(nc)],
        out_specs=[blk(0), blk(0), blk(0)],
        out_shape=[jax.ShapeDtypeStruct((T, D_FF), BF16)] * 3,
        compiler_params=_params(("parallel", "parallel")),
    )(u, u, u, u, w8, w8)


def _down_loss_head(x1, act, w_down, g2, target):
    T = x1.shape[0]
    tm = min(TOKEN_TILE, T)

    def body(x1_ref, a_ref, w_ref, g2_ref, t_ref, dy_ref, dff_ref, loss_ref, dg2_ref):
        i = pl.program_id(0)

        @pl.when(i == 0)
        def _():
            loss_ref[...] = jnp.zeros_like(loss_ref)
            dg2_ref[...] = jnp.zeros_like(dg2_ref)

        ff = jnp.dot(a_ref[...], w_ref[...], preferred_element_type=F32)
        err = x1_ref[...] + g2_ref[...] * ff - t_ref[...]
        dy = err * (1.0 / D)
        dy_ref[...] = dy
        dff_ref[...] = (dy * g2_ref[...]).astype(BF16)
        loss_ref[...] += _rows8(err * err)
        dg2_ref[...] += _rows8(dy * ff)

    row = pl.BlockSpec((tm, D), lambda i: (i, 0))
    acc = _full((SUBLANES, D))
    return pl.pallas_call(
        body, name="down_loss_head", grid=(T // tm,),
        in_specs=[row, pl.BlockSpec((tm, D_FF), lambda i: (i, 0)), _full((D_FF, D)), _full((1, D)), row],
        out_specs=[row, row, acc, acc],
        out_shape=[jax.ShapeDtypeStruct((T, D), F32), jax.ShapeDtypeStruct((T, D), BF16),
                   jax.ShapeDtypeStruct((SUBLANES, D), F32), jax.ShapeDtypeStruct((SUBLANES, D), F32)],
        compiler_params=_params(("arbitrary",)),
    )(x1, act, w_down, g2, target)


def _nt_dot(a, b):
    return lax.dot_general(a, b, (((1,), (1,)), ((), ())), preferred_element_type=F32)


def _causal(n, keys_on_rows=False):
    r = lax.broadcasted_iota(jnp.int32, (n, n), 0)
    c = lax.broadcasted_iota(jnp.int32, (n, n), 1)
    return (c >= r) if keys_on_rows else (c <= r)


def _sweep(lo, hi, step, carry, group=2):
    while group >= 1:
        def several(j, cr, lo=lo, group=group):
            for g in range(group):
                cr = step(lo + group * j + g, cr)
            return cr

        passes = (hi - lo) // group
        carry = lax.fori_loop(0, passes, several, carry)
        lo = lo + group * passes
        group //= 2
    return carry


def _grid_ends(n0, n1):
    i0, i1 = pl.program_id(0), pl.program_id(1)
    return jnp.logical_and(i0 == 0, i1 == 0), jnp.logical_and(i0 == n0 - 1, i1 == n1 - 1)


def _attn_fwd(qa, ka, va, exchange=None):
    nh, T, _ = qa.shape
    bq = min(ATTN_BLOCK, T)
    nq = T // bq
    hp = ATTN_FWD_HEADS
    ng = nh // hp

    def body(*refs):
        if exchange is None:
            q_ref, k_ref, v_ref, o_ref, qb_ref = refs
        else:
            (q_ref, k_ref, v_ref), (o_ref, qb_ref), _, xrefs = exchange.split(refs, 3, 2)
            _ride(exchange, *_grid_ends(ng, nq), xrefs)
        qi = pl.program_id(1)

        def step(kb, carry, masked):
            start = pl.multiple_of(kb * bq, bq)
            out = []
            for h, (m, acc) in enumerate(carry):
                s = _nt_dot(q_ref[h], k_ref[h, pl.ds(start, bq), :])
                if masked:
                    s = jnp.where(_causal(bq), s, NEG_INF)
                m_new = jnp.maximum(m, jnp.max(s, axis=-1, keepdims=True))
                p = jnp.exp(s - m_new).astype(BF16)
                acc = jnp.exp(m - m_new) * acc + jnp.dot(p, v_ref[h, pl.ds(start, bq), :],
                                                         preferred_element_type=F32)
                out.append((m_new, acc))
            return tuple(out)

        init = tuple((jnp.full((bq, 1), NEG_INF, F32), jnp.zeros((bq, LANES), F32)) for _ in range(hp))
        carry = _sweep(0, qi, lambda kb, cr: step(kb, cr, False), init, group=4)
        lane = lax.broadcasted_iota(jnp.int32, (bq, LANES), 1)
        for h, (m, acc) in enumerate(step(qi, carry, True)):
            l = acc[:, LANE_ONE:LANE_ONE + 1]
            o_ref[h] = acc / l
            qf = q_ref[h].astype(F32)
            for idx, piece in _run(LANE_LSE, _pieces(m + jnp.log(l))):
                qf = jnp.where(lane == idx, -piece, qf)
            qb_ref[h] = qf.astype(BF16)

    tile = pl.BlockSpec((hp, bq, LANES), lambda h, i: (h, i, 0))
    whole = pl.BlockSpec((hp, T, LANES), lambda h, i: (h, 0, 0))
    out_shape = [jax.ShapeDtypeStruct((nh, T, LANES), F32), jax.ShapeDtypeStruct((nh, T, LANES), BF16)]
    if exchange is None:
        return pl.pallas_call(
            body, name="attn_fwd", grid=(ng, nq), in_specs=[tile, whole, whole], out_specs=[tile, tile],
            out_shape=out_shape, compiler_params=_params(("parallel", "parallel")),
        )(qa, ka, va)
    return pl.pallas_call(
        body, name="attn_fwd", grid=(ng, nq), in_specs=[tile, whole, whole] + exchange.in_specs,
        out_specs=[tile, tile] + exchange.out_specs, out_shape=out_shape + exchange.out_shapes,
        scratch_shapes=exchange.scratch, compiler_params=_params(("arbitrary", "arbitrary")),
    )(qa, ka, va, *exchange.xs)


def _branch_merge_fwd(ya0, o_h, proj, wba, wbb_heads):
    nh, T, _ = o_h.shape
    tm = min(TOKEN_TILE, T)

    def body(ya0_ref, o_ref, ga_ref, gb_ref, wa_ref, wb_ref, ya_ref, yb_ref, m_ref):
        ya = jnp.dot(ya0_ref[...], wa_ref[...], preferred_element_type=F32)
        yb = jnp.dot(o_ref[0].astype(BF16), wb_ref[0:LANES, :], preferred_element_type=F32)
        for h in range(1, nh):
            yb += jnp.dot(o_ref[h].astype(BF16), wb_ref[h * LANES:(h + 1) * LANES, :], preferred_element_type=F32)
        ya_ref[...] = ya.astype(BF16)
        yb_ref[...] = yb.astype(BF16)
        m_ref[...] = (_sigmoid(ga_ref[...]) * ya + _sigmoid(gb_ref[...]) * yb).astype(BF16)

    row = pl.BlockSpec((tm, D), lambda i: (i, 0))
    return pl.pallas_call(
        body, name="branch_merge_fwd", grid=(T // tm,),
        in_specs=[pl.BlockSpec((tm, CONV_W), lambda i: (i, 0)), pl.BlockSpec((nh, tm, LANES), lambda i: (0, i, 0)),
                  pl.BlockSpec((tm, D), lambda i: (i, COL_GA // D)), pl.BlockSpec((tm, D), lambda i: (i, COL_GB // D)),
                  _full((CONV_W, D)), _full((nh * LANES, D))],
        out_specs=[row, row, row],
        out_shape=[jax.ShapeDtypeStruct((T, D), BF16)] * 3,
        compiler_params=_params(("parallel",)),
    )(ya0, o_h, proj, proj, wba, wbb_heads)


def _branch_b_bwd(dyb, o_h, wbb_heads_t):
    nh, T, _ = o_h.shape
    tm = min(TOKEN_TILE, T)

    def body(dyb_ref, o_ref, w_ref, out_ref):
        do = jnp.dot(dyb_ref[...], w_ref[...], preferred_element_type=F32)
        lane = lax.broadcasted_iota(jnp.int32, (tm, LANES), 1)
        for h in range(nh):
            g = do[:, h * LANES:(h + 1) * LANES].astype(BF16).astype(F32)
            delta = jnp.sum(g * o_ref[h], axis=-1, keepdims=True)
            for idx, piece in _run(LANE_ONE, _pieces(delta)):
                g = jnp.where(lane == idx, -piece, g)
            out_ref[h] = g.astype(BF16)

    heads = pl.BlockSpec((nh, tm, LANES), lambda i: (0, i, 0))
    return pl.pallas_call(
        body, name="branch_b_bwd", grid=(T // tm,),
        in_specs=[pl.BlockSpec((tm, D), lambda i: (i, 0)), heads, _full((D, nh * LANES))],
        out_specs=heads, out_shape=jax.ShapeDtypeStruct((nh, T, LANES), BF16),
        compiler_params=_params(("parallel",)),
    )(dyb, o_h, wbb_heads_t)


def _branch_b_dw(o_h, dyb):
    nh, T, _ = o_h.shape
    tk = min(TOKEN_TILE, T)

    def body(o_ref, dyb_ref, out_ref):
        @pl.when(pl.program_id(0) == 0)
        def _():
            out_ref[...] = jnp.zeros_like(out_ref)

        g = dyb_ref[...]
        for h in range(nh):
            out_ref[h] += _tn_dot(o_ref[h].astype(BF16), g)

    return pl.pallas_call(
        body, name="branch_b_dw", grid=(T // tk,),
        in_specs=[pl.BlockSpec((nh, tk, LANES), lambda k: (0, k, 0)), pl.BlockSpec((tk, D), lambda k: (k, 0))],
        out_specs=_full((nh, LANES, D)), out_shape=jax.ShapeDtypeStruct((nh, LANES, D), F32),
        compiler_params=_params(("arbitrary",)),
    )(o_h, dyb)


def _attn_bwd(qb, ka, va, doa, exchange=None):
    nh, T, _ = qb.shape
    bk = min(ATTN_BLOCK, T)
    nk = T // bk

    def body(*refs):
        if exchange is None:
            q_ref, do_ref, k_ref, v_ref, dq_ref, dk_ref, dv_ref = refs
        else:
            (q_ref, do_ref, k_ref, v_ref), (dq_ref, dk_ref, dv_ref), _, xrefs = exchange.split(refs, 4, 3)
            _ride(exchange, *_grid_ends(nh, nk), xrefs)
        ki = pl.program_id(1)

        @pl.when(ki == 0)
        def _():
            dq_ref[...] = jnp.zeros_like(dq_ref)

        k = k_ref[0]
        v = v_ref[0]

        def step(qi, carry, masked):
            dk, dv = carry
            rows = pl.ds(pl.multiple_of(qi * bk, bk), bk)
            q = q_ref[0, rows, :]
            g = do_ref[0, rows, :]
            pt = jnp.exp(_nt_dot(k, q))
            if masked:
                pt = jnp.where(_causal(bk, keys_on_rows=True), pt, 0.0)
            dv = dv + jnp.dot(pt.astype(BF16), g, preferred_element_type=F32)
            dst = (pt * _nt_dot(v, g)).astype(BF16)
            dk = dk + jnp.dot(dst, q, preferred_element_type=F32)
            dq_ref[0, rows, :] += _tn_dot(dst, k)
            return dk, dv

        init = (jnp.zeros((bk, LANES), F32), jnp.zeros((bk, LANES), F32))
        carry = step(ki, init, True)
        dk_ref[0], dv_ref[0] = _sweep(ki + 1, nk, lambda qi, cr: step(qi, cr, False), carry)

    tile = pl.BlockSpec((1, bk, LANES), lambda h, i: (h, i, 0))
    whole = pl.BlockSpec((1, T, LANES), lambda h, i: (h, 0, 0))
    out_shape = [jax.ShapeDtypeStruct((nh, T, LANES), F32)] * 3
    if exchange is None:
        return pl.pallas_call(
            body, name="attn_bwd", grid=(nh, nk), in_specs=[whole, whole, tile, tile],
            out_specs=[whole, tile, tile], out_shape=out_shape, compiler_params=_params(("parallel", "arbitrary")),
        )(qb, doa, ka, va)
    return pl.pallas_call(
        body, name="attn_bwd", grid=(nh, nk), in_specs=[whole, whole, tile, tile] + exchange.in_specs,
        out_specs=[whole, tile, tile] + exchange.out_specs, out_shape=out_shape + exchange.out_shapes,
        scratch_shapes=exchange.scratch, compiler_params=_params(("arbitrary", "arbitrary")),
    )(qb, doa, ka, va, *exchange.xs)


def _attn_unpack(dq_h, dk_h, dv_h):
    nh, T, _ = dq_h.shape
    tm = min(TOKEN_TILE, T)

    def body(dq_ref, dk_ref, dv_ref, q_out, k_out, v_out, f_out):
        lane = lax.broadcasted_iota(jnp.int32, (tm, LANES), 1)
        low = lane < HEAD_DIM
        for src, dst in ((dq_ref, q_out), (dk_ref, k_out), (dv_ref, v_out)):
            for pair in range(nh // 2):
                both = jnp.where(low, src[2 * pair], pltpu.roll(src[2 * pair + 1], HEAD_DIM, axis=1))
                dst[:, LANES * pair:LANES * (pair + 1)] = both.astype(dst.dtype)
        df = jnp.zeros((tm, LANES), F32)
        for h in range(nh):
            col = dq_ref[h][:, LANE_F:LANE_F + 1] - dk_ref[h][:, LANE_SUM:LANE_SUM + 1]
            df = jnp.where(lane == h, col, df)
        f_out[...] = df

    heads = pl.BlockSpec((nh, tm, LANES), lambda i: (0, i, 0))
    tok = pl.BlockSpec((tm, ATTN_W), lambda i: (i, 0))
    return pl.pallas_call(
        body, name="attn_unpack", grid=(T // tm,), in_specs=[heads, heads, heads],
        out_specs=[tok, tok, tok, pl.BlockSpec((tm, F_PAD), lambda i: (i, 0))],
        out_shape=[jax.ShapeDtypeStruct((T, ATTN_W), F32), jax.ShapeDtypeStruct((T, ATTN_W), F32),
                   jax.ShapeDtypeStruct((T, ATTN_W), BF16), jax.ShapeDtypeStruct((T, F_PAD), F32)],
        compiler_params=_params(("parallel",)),
    )(dq_h, dk_h, dv_h)


def _ffn_act_bwd(u, cg, cv, da, w8):
    T = u.shape[0]
    tm = min(FFN_TM, T)
    nt = T // tm
    nc = D_FF // FFN_TC

    def body(ug_ref, uv_ref, cg_ref, cv_ref, cgn_ref, cvn_ref, da_ref, dan_ref, wg_ref, wv_ref,
             dug_ref, duv_ref, dwg_ref, dwv_ref):
        i = pl.program_id(1)

        @pl.when(i == 0)
        def _():
            dwg_ref[...] = jnp.zeros_like(dwg_ref)
            dwv_ref[...] = jnp.zeros_like(dwv_ref)

        gate = jnp.concatenate([cg_ref[...], cgn_ref[...]], axis=0).astype(F32)
        val = jnp.concatenate([cv_ref[...], cvn_ref[...]], axis=0).astype(F32)
        dae = jnp.concatenate([da_ref[...], dan_ref[...]], axis=0).astype(F32)
        rows_e = lax.broadcasted_iota(jnp.int32, dae.shape, 0)
        dae = jnp.where(jnp.logical_and(i == nt - 1, rows_e >= tm), 0.0, dae)
        sg = _tanh_sigmoid(gate)
        n = tm + BF16_ROWS

        def back(d, u_ref, w_ref, du_ref, dw_ref):
            w = w_ref[...]
            uv = u_ref[...]
            d1 = pltpu.roll(d, n - 1, axis=0)[:tm]
            d2 = pltpu.roll(d, n - 2, axis=0)[:tm]
            d0 = d[:tm]
            du_ref[...] = (d0 * w[2:3] + d1 * w[1:2] + d2 * w[0:1]).astype(BF16)
            rows = [jnp.sum(t * uv, axis=0, keepdims=True) for t in (d2, d1, d0)]
            dw_ref[...] += jnp.concatenate(rows + [jnp.zeros((SUBLANES - 3, FFN_TC), F32)], axis=0)

        back(dae * val * sg * (1.0 + gate * (1.0 - sg)), ug_ref, wg_ref, dug_ref, dwg_ref)
        back(dae * gate * sg, uv_ref, wv_ref, duv_ref, dwv_ref)

    per = tm // BF16_ROWS
    last_blk = nt * per - 1
    blk = lambda off: pl.BlockSpec((tm, FFN_TC), lambda j, i: (i, j + off))
    nxt = pl.BlockSpec((BF16_ROWS, FFN_TC), lambda j, i: (jnp.minimum((i + 1) * per, last_blk), j))
    wblk = lambda off: pl.BlockSpec((SUBLANES, FFN_TC), lambda j, i: (0, j + off))
    dug, duv, dwg, dwv = pl.pallas_call(
        body, name="ffn_act_bwd", grid=(nc, nt),
        in_specs=[blk(0), blk(nc), blk(0), blk(0), nxt, nxt, blk(0), nxt, wblk(0), wblk(nc)],
        out_specs=[blk(0), blk(0), wblk(0), wblk(0)],
        out_shape=[jax.ShapeDtypeStruct((T, D_FF), BF16)] * 2 + [jax.ShapeDtypeStruct((SUBLANES, D_FF), F32)] * 2,
        compiler_params=_params(("parallel", "arbitrary")),
    )(u, u, cg, cv, cg, cv, da, da, w8, w8)
    return dug, duv, jnp.concatenate([dwg, dwv], axis=1)


def _out_merge_bwd(dmix, w_out_t, ya, yb, proj):
    T = ya.shape[0]
    tm = min(TOKEN_TILE, T)

    def body(dmix_ref, w_ref, ya_ref, yb_ref, ga_ref, gb_ref, dya_ref, dyb_ref, dga_ref, dgb_ref):
        dm = jnp.dot(dmix_ref[...], w_ref[...], preferred_element_type=F32)
        sa = _sigmoid(ga_ref[...])
        sb = _sigmoid(gb_ref[...])
        dya_ref[...] = (dm * sa).astype(BF16)
        dyb_ref[...] = (dm * sb).astype(BF16)
        dga_ref[...] = (dm * ya_ref[...].astype(F32) * sa * (1.0 - sa)).astype(BF16)
        dgb_ref[...] = (dm * yb_ref[...].astype(F32) * sb * (1.0 - sb)).astype(BF16)

    row = pl.BlockSpec((tm, D), lambda i: (i, 0))
    return pl.pallas_call(
        body, name="out_merge_bwd", grid=(T // tm,),
        in_specs=[row, _full((D, D)), row, row, pl.BlockSpec((tm, D), lambda i: (i, COL_GA // D)),
                  pl.BlockSpec((tm, D), lambda i: (i, COL_GB // D))],
        out_specs=[row] * 4, out_shape=[jax.ShapeDtypeStruct((T, D), BF16)] * 4,
        compiler_params=_params(("parallel",)),
    )(dmix, w_out_t, ya, yb, proj, proj)


def _conv_branch_bwd(proj, dya0, conv_w8):
    T = proj.shape[0]
    tm = min(FFN_TM, T)
    nt = T // tm

    def body(cb_ref, cc_ref, cv_ref, cbn_ref, ccp_ref, cvp_ref, ccn_ref, cvn_ref, d_ref, dn_ref, w_ref,
             d3_ref, dw_ref):
        i = pl.program_id(0)

        @pl.when(i == 0)
        def _():
            dw_ref[...] = jnp.zeros_like(dw_ref)

        first, last = i == 0, i == nt - 1
        w = w_ref[...]
        cc = jnp.concatenate([ccp_ref[...], cc_ref[...], ccn_ref[...]], axis=0)
        cv = jnp.concatenate([cvp_ref[...], cv_ref[...], cvn_ref[...]], axis=0)
        rows = lax.broadcasted_iota(jnp.int32, cc.shape, 0)
        z = jnp.where(jnp.logical_and(first, rows < SUBLANES), 0.0, cc * cv)
        z1 = pltpu.roll(z, 1, axis=0)
        z2 = pltpu.roll(z, 2, axis=0)
        cz = z2 * w[0:1] + z1 * w[1:2] + z * w[2:3]
        zeros8 = jnp.zeros((SUBLANES, CONV_W), F32)
        de = jnp.concatenate([zeros8, d_ref[...], jnp.where(last, 0.0, dn_ref[...])], axis=0)
        cbe = jnp.concatenate([zeros8, cb_ref[...], cbn_ref[...]], axis=0)
        dcz = de * cbe
        n = tm + 2 * SUBLANES
        dz = dcz * w[2:3] + pltpu.roll(dcz, n - 1, axis=0) * w[1:2] + pltpu.roll(dcz, n - 2, axis=0) * w[0:1]
        inner = slice(SUBLANES, SUBLANES + tm)
        d3_ref[:, 0:CONV_W] = (de * cz)[inner].astype(BF16)
        d3_ref[:, CONV_W:2 * CONV_W] = (dz * cv)[inner].astype(BF16)
        d3_ref[:, 2 * CONV_W:3 * CONV_W] = (dz * cc)[inner].astype(BF16)
        wrows = [jnp.sum((dcz * t)[inner], axis=0, keepdims=True) for t in (z2, z1, z)]
        dw_ref[...] += jnp.concatenate(wrows + [jnp.zeros((SUBLANES - 3, CONV_W), F32)], axis=0)

    blk = lambda col: pl.BlockSpec((tm, CONV_W), lambda i: (i, col))
    out_blk = pl.BlockSpec((tm, CONV_W), lambda i: (i, 0))
    return pl.pallas_call(
        body, name="conv_branch_bwd", grid=(nt,),
        in_specs=[blk(0), blk(1), blk(2), _next_spec(tm, CONV_W, 0, nt),
                  _prev_spec(tm, CONV_W, 1), _prev_spec(tm, CONV_W, 2),
                  _next_spec(tm, CONV_W, 1, nt), _next_spec(tm, CONV_W, 2, nt),
                  out_blk, _next_spec(tm, CONV_W, 0, nt), _full((SUBLANES, CONV_W))],
        out_specs=[pl.BlockSpec((tm, 3 * CONV_W), lambda i: (i, 0)), _full((SUBLANES, CONV_W))],
        out_shape=[jax.ShapeDtypeStruct((T, 3 * CONV_W), BF16), jax.ShapeDtypeStruct((SUBLANES, CONV_W), F32)],
        compiler_params=_params(("arbitrary",)),
    )(proj, proj, proj, proj, proj, proj, proj, proj, dya0, dya0, conv_w8)


def _qk_norm_bwd(proj, dqs, dkh, dlogf, qg, kg, bf_pad, gmat):
    T = proj.shape[0]
    tm = min(TOKEN_TILE, T)

    def body(q_ref, k_ref, f_ref, dqs_ref, dkh_ref, dlf_ref, qg_ref, kg_ref, bf_ref, g_ref,
             dqk_ref, dfl_ref, dqg_ref, dkg_ref, dbf_ref):
        @pl.when(pl.program_id(0) == 0)
        def _():
            dqg_ref[...] = jnp.zeros_like(dqg_ref)
            dkg_ref[...] = jnp.zeros_like(dkg_ref)
            dbf_ref[...] = jnp.zeros_like(dbf_ref)

        gm = g_ref[...]
        for src, d_src, gain, scale, dst, dgain in (
                (q_ref, dqs_ref, qg_ref, 1.0 / np.sqrt(HEAD_DIM), dqk_ref.at[:, 0:ATTN_W], dqg_ref),
                (k_ref, dkh_ref, kg_ref, 1.0, dqk_ref.at[:, ATTN_W:2 * ATTN_W], dkg_ref)):
            v = src[...]
            dhat = d_src[...] * scale
            inv = lax.rsqrt(_split_dot(v * v, gm, 2) * (1.0 / HEAD_DIM) + RMS_EPS)
            vn = v * inv
            dgain[...] += _rows8(dhat * vn)
            dvn = dhat * gain[...]
            mean = _split_dot(dvn * vn, gm, 2) * (1.0 / HEAD_DIM)
            dst[...] = (inv * (dvn - vn * mean)).astype(BF16)
        fx = f_ref[...] + bf_ref[...]
        dfl = dlf_ref[...] * _sigmoid(-fx)
        dfl_ref[...] = dfl.astype(BF16)
        dbf_ref[...] += _rows8(dfl)

    blk = lambda col: pl.BlockSpec((tm, ATTN_W), lambda i: (i, col))
    out_blk = pl.BlockSpec((tm, ATTN_W), lambda i: (i, 0))
    f_in = pl.BlockSpec((tm, F_PAD), lambda i: (i, COL_F // F_PAD))
    f_blk = pl.BlockSpec((tm, F_PAD), lambda i: (i, 0))
    return pl.pallas_call(
        body, name="qk_norm_bwd", grid=(T // tm,),
        in_specs=[blk(3), blk(4), f_in, out_blk, out_blk, f_blk, _full((1, ATTN_W)), _full((1, ATTN_W)),
                  _full((1, F_PAD)), _full((ATTN_W, ATTN_W))],
        out_specs=[pl.BlockSpec((tm, 2 * ATTN_W), lambda i: (i, 0)), f_blk, _full((SUBLANES, ATTN_W)),
                   _full((SUBLANES, ATTN_W)), _full((SUBLANES, F_PAD))],
        out_shape=[jax.ShapeDtypeStruct((T, 2 * ATTN_W), BF16), jax.ShapeDtypeStruct((T, F_PAD), BF16)]
        + [jax.ShapeDtypeStruct((SUBLANES, ATTN_W), F32)] * 2 + [jax.ShapeDtypeStruct((SUBLANES, F_PAD), F32)],
        compiler_params=_params(("arbitrary",)),
    )(proj, proj, proj, dqs, dkh, dlogf, qg, kg, bf_pad, gmat)


def _pad_rows8(w):
    return jnp.pad(w, ((0, SUBLANES - w.shape[0]), (0, 0)))


def _fold8(acc):
    return jnp.sum(acc, axis=0, keepdims=True)


def _late_weights(mats):
    out = {}
    for name in ("w_branch_a", "w_out", "w_up", "w_down"):
        out[name] = mats[name]
        out[name + "_t"] = mats[name].T
    out["w_branch_b_heads"] = _pad_head_rows(mats["w_branch_b"])
    out["w_branch_b_heads_t"] = out["w_branch_b_heads"].T
    return out


def _local_step(x, target, mod, wts, late=None):
    T = x.shape[0]
    tb = min(MATMUL_TILE, T)
    tk_long = min(2 * MATMUL_TILE, T)
    tm = min(TOKEN_TILE, T)
    sh1, sc1, g1, sh2, sc2, g2 = [mod[:, i * D:(i + 1) * D] for i in range(N_MOD)]
    w_in, w_in_t = wts["w_in"], wts["w_in_t"]
    conv_a8 = _pad_rows8(wts["conv_a_w"])
    conv_f8 = _pad_rows8(wts["conv_ffn_w"])
    qg = jnp.tile(wts["q_norm_g"], (1, N_HEADS))
    kg = jnp.tile(wts["k_norm_g"], (1, N_HEADS))
    bf_pad = jnp.pad(wts["b_f"], ((0, 0), (0, F_PAD - N_HEADS)))
    gmat = _group_matrix()

    h = _norm_mod(x, wts["norm1_g"], sc1, sh1, name="norm1_fwd")
    proj = _matmul(h, w_in, name="mm_in", tm=tb, tn=896, tk=D)
    fcum = _cumsum(proj, reverse=False, name="gate_cumsum", col=COL_F // F_PAD, gate_bias=bf_pad)
    ya0, qa, ka, va = _branch_prep(proj, fcum, conv_a8, qg, kg, gmat)
    if late is None:
        o_h, qb = _attn_fwd(qa, ka, va)
    else:
        o_h, qb, *gathered = _attn_fwd(qa, ka, va, _Exchange([late[name] for name, *_ in LATE], scatter=False))
        wts = dict(wts)
        mats = {name: _join_shards(g, axis) for (name, _, _, axis), g in zip(LATE, gathered) if name != "w_up"}
        mats["w_up"] = _assemble_columns(gathered[[name for name, *_ in LATE].index("w_up")], 2 * D_FF // N_DEV,
                                         2 * D_FF, ((0, 2 * D_FF, 0),), name="assemble_w_up")
        wts.update(_late_weights(mats))
    ya, yb, merged = _branch_merge_fwd(ya0, o_h, proj, wts["w_branch_a"], wts["w_branch_b_heads"])
    mix, x1, h2 = _out_resid_norm(x, merged, wts["w_out"], g1, wts["norm2_g"], sc2, sh2)
    u = _matmul(h2, wts["w_up"], name="mm_up", tm=tb, tn=1408, tk=D)
    act, conv_gate, conv_val = _ffn_act_fwd(u, conv_f8)
    dy, dff, sq8, dg2_8 = _down_loss_head(x1, act, wts["w_down"], g2, target)
    sq = jnp.sum(sq8).reshape(1, 1)

    grads = {}
    da = _matmul(dff, wts["w_down_t"], name="mm_down_dx", tm=tb, tn=1408, tk=D, out_dtype=BF16)
    grads["w_down"] = _matmul(act, dff, name="mm_down_dw", tm=1408, tn=D, tk=tk_long, trans_a=True)
    dug, duv, dconv_f8 = _ffn_act_bwd(u, conv_gate, conv_val, da, conv_f8)
    grads["conv_ffn_w"] = dconv_f8[:3]
    dx1, dsh2_8, dsc2_8, dn2_8, dmix, dg1_8 = _matmul_pieces(
        [dug, duv], wts["w_up_t"], name="mm_up_dx", tm=tm, norm_bwd=(x1, dy, wts["norm2_g"], sc2, mix, g1))
    dw_up = [_matmul(h2, d, name="mm_up_dw_" + half, tm=D, tn=1408, tk=tk_long, trans_a=True)
             for half, d in (("gate", dug), ("val", duv))]
    if late is None:
        grads["w_up"] = jnp.concatenate(dw_up, axis=1)
    grads["norm2_g"] = _fold8(dn2_8)

    grads["w_out"] = _matmul(merged, dmix, name="mm_out_dw", tm=D, tn=D, tk=tk_long, trans_a=True)
    dya, dyb, dga, dgb = _out_merge_bwd(dmix, wts["w_out_t"], ya, yb, proj)
    dya0 = _matmul(dya, wts["w_branch_a_t"], name="mm_branch_a_dx", tm=tb, tn=CONV_W, tk=D)
    grads["w_branch_a"] = _matmul(ya0, dya, name="mm_branch_a_dw", tm=CONV_W, tn=D, tk=tk_long, trans_a=True)
    doa = _branch_b_bwd(dyb, o_h, wts["w_branch_b_heads_t"])
    grads["w_branch_b"] = _branch_b_dw(o_h, dyb)[:, :HEAD_DIM].reshape(ATTN_W, D)
    dconv3, dconv_a8 = _conv_branch_bwd(proj, dya0, conv_a8)
    grads["conv_a_w"] = dconv_a8[:3]

    parts = {}
    if late is None:
        dq_h, dk_h, dv_h = _attn_bwd(qb, ka, va, doa)
    else:
        ready = [(_column_shards(dw_up) if name == "w_up" else _split_shards(grads[name], axis)).astype(BF16)
                 for name, _, _, axis in LATE]
        dq_h, dk_h, dv_h, *recv = _attn_bwd(
            qb, ka, va, doa, _Exchange(ready + [_pack_full_by_dest(grads, CONVS, SUBLANES)], scatter=True))
        parts = dict(zip([name for name, *_ in LATE] + ["conv"], recv))
    dq_tok, dk_tok, dv_tok, dfcum = _attn_unpack(dq_h, dk_h, dv_h)
    dlogf = _cumsum(dfcum, reverse=True, name="gate_cumsum_bwd")
    dqk, dfl, dqg8, dkg8, dbf8 = _qk_norm_bwd(proj, dq_tok, dk_tok, dlogf, qg, kg, bf_pad, gmat)
    grads["q_norm_g"] = jnp.sum(_fold8(dqg8).reshape(N_HEADS, HEAD_DIM), axis=0, keepdims=True)
    grads["k_norm_g"] = jnp.sum(_fold8(dkg8).reshape(N_HEADS, HEAD_DIM), axis=0, keepdims=True)
    grads["b_f"] = _fold8(dbf8)[:, :N_HEADS]
    narrow, wide = [dconv3, dqk, dv_tok], [dga, dgb, dfl]
    dw_narrow = _matmul_tn_pieces(h, narrow, name="mm_in_dw_narrow", tk=tb)
    dwa, dwb, dwf = _matmul_tn_pieces(h, wide, name="mm_in_dw_wide", tk=tk_long)
    dw_in = list(dw_narrow) + [dwf[:, :N_HEADS], dwa, dwb]
    norm1 = (x, dx1, wts["norm1_g"], sc1)
    if late is None:
        grads["w_in"] = jnp.concatenate(dw_in, axis=1)
        grad_x, dsh1_8, dsc1_8, dn1_8 = _matmul_pieces(narrow + wide, w_in_t, name="mm_in_dx", tm=tm,
                                                       norm_bwd=norm1)
    else:
        grad_x, dsh1_8, dsc1_8, dn1_8, parts["w_in"] = _matmul_pieces(
            narrow + wide, w_in_t, name="mm_in_dx", tm=tm, norm_bwd=norm1,
            exchange=_Exchange([_column_shards(dw_in).astype(BF16)], scatter=True))
    grads["norm1_g"] = _fold8(dn1_8)
    grads["mod"] = jnp.concatenate([_fold8(a) for a in (dsh1_8, dsc1_8, dg1_8, dsh2_8, dsc2_8, dg2_8)], axis=1)
    return sq, grad_x, grads, parts


def _me_and_peers():
    mx, my, mc = lax.axis_index("x"), lax.axis_index("y"), lax.axis_index("c")
    me = 4 * mx + 2 * my + mc
    peers = []
    for k in range(1, N_DEV):
        px = 1 - mx if k & 4 else mx
        py = 1 - my if k & 2 else my
        pc = 1 - mc if k & 1 else mc
        peers.append(((px, py, pc), 4 * px + 2 * py + pc))
    return me, peers


HBM_SPEC = pl.BlockSpec(memory_space=pltpu.HBM)


class _Exchange:
    def __init__(self, xs, scatter):
        self.xs, self.scatter, self.n = list(xs), scatter, len(xs)
        self.out_shapes = [jax.ShapeDtypeStruct(x.shape if scatter else (N_DEV,) + x.shape, x.dtype) for x in xs]
        self.in_specs = [HBM_SPEC] * self.n
        self.out_specs = [HBM_SPEC] * self.n
        self.scratch = [pltpu.SemaphoreType.DMA((self.n, N_DEV - 1)), pltpu.SemaphoreType.DMA((self.n, N_DEV - 1)),
                        pltpu.SemaphoreType.DMA((self.n,))]

    def _copies(self, x_refs, out_refs, sems):
        send_sems, recv_sems, local_sems = sems
        me, peers = _me_and_peers()

        def src(a, idx):
            return x_refs[a].at[idx] if self.scatter else x_refs[a]

        def copy(a, k, from_idx, to_slot, device):
            return pltpu.make_async_remote_copy(
                src_ref=src(a, from_idx), dst_ref=out_refs[a].at[to_slot], send_sem=send_sems.at[a, k],
                recv_sem=recv_sems.at[a, k], device_id=device, device_id_type=MESH)

        local = [pltpu.make_async_copy(src(a, me), out_refs[a].at[me], local_sems.at[a]) for a in range(self.n)]
        sends = [copy(a, k, idx, me, dev) for a in range(self.n) for k, (dev, idx) in enumerate(peers)]
        recvs = [copy(a, k, idx, idx, dev) for a in range(self.n) for k, (dev, idx) in enumerate(peers)]
        return local, sends, recvs

    def start(self, x_refs, out_refs, sems):
        local, sends, _ = self._copies(x_refs, out_refs, sems)
        for cp in local + sends:
            cp.start()

    def wait(self, x_refs, out_refs, sems):
        local, sends, recvs = self._copies(x_refs, out_refs, sems)
        for cp in recvs:
            cp.wait_recv()
        for cp in sends:
            cp.wait_send()
        for cp in local:
            cp.wait()

    def split(self, refs, n_in, n_out):
        n = self.n
        ins, xin = refs[:n_in], refs[n_in:n_in + n]
        outs, xout = refs[n_in + n:n_in + n + n_out], refs[n_in + n + n_out:n_in + 2 * n + n_out]
        rest = refs[n_in + 2 * n + n_out:]
        return ins, outs, rest[:len(rest) - 3], (xin, xout, rest[len(rest) - 3:])


def _ride(exchange, first, last, refs):
    if exchange is None:
        return

    @pl.when(first)
    def _():
        exchange.start(*refs)

    @pl.when(last)
    def _():
        exchange.wait(*refs)


def _gather_two_level(xs, *, name):
    n = len(xs)
    out_shapes = [jax.ShapeDtypeStruct((N_DEV,) + x.shape, x.dtype) for x in xs]

    def body(*refs):
        x_refs, out_refs = refs[:n], refs[n:2 * n]
        send_sems, recv_sems, local_sems = refs[2 * n:]
        x, y, c = lax.axis_index("x"), lax.axis_index("y"), lax.axis_index("c")
        me, sibling = (x, y, c), (x, y, 1 - c)
        chips = [(1 - x, y), (x, 1 - y), (1 - x, 1 - y)]

        def slot(a, dev):
            return out_refs[a].at[4 * dev[0] + 2 * dev[1] + dev[2]]

        def copy(a, k, block, to, src=None):
            return pltpu.make_async_remote_copy(
                src_ref=slot(a, block) if src is None else src, dst_ref=slot(a, block),
                send_sem=send_sems.at[a, k], recv_sem=recv_sems.at[a, k], device_id=to, device_id_type=MESH)

        mine = [pltpu.make_async_copy(x_refs[a], slot(a, me), local_sems.at[a]) for a in range(n)]
        first = [copy(a, 0, me, sibling, src=x_refs[a]) for a in range(n)]
        first += [copy(a, 1 + j, me, (*chip, c), src=x_refs[a]) for a in range(n) for j, chip in enumerate(chips)]
        for cp in mine + first:
            cp.start()
        passed = []
        for a in range(n):
            for j, chip in enumerate(chips):
                copy(a, 1 + j, (*chip, c), me).wait_recv()
                passed.append(copy(a, 4 + j, (*chip, c), sibling))
                passed[-1].start()
        for a in range(n):
            copy(a, 0, sibling, me).wait_recv()
            for j, chip in enumerate(chips):
                copy(a, 4 + j, (*chip, 1 - c), me).wait_recv()
        for cp in first + passed:
            cp.wait_send()
        for cp in mine:
            cp.wait()

    return pl.pallas_call(
        body, name=name, in_specs=[HBM_SPEC] * n, out_specs=[HBM_SPEC] * n, out_shape=out_shapes,
        scratch_shapes=[pltpu.SemaphoreType.DMA((n, N_DEV - 1)), pltpu.SemaphoreType.DMA((n, N_DEV - 1)),
                        pltpu.SemaphoreType.DMA((n,))],
        compiler_params=pltpu.CompilerParams(has_side_effects=True),
    )(*xs)


def _exchange(xs, *, name, scatter):
    ex = _Exchange(xs, scatter)

    def body(*refs):
        _, _, _, xrefs = ex.split(refs, 0, 0)
        ex.start(*xrefs)
        ex.wait(*xrefs)

    return pl.pallas_call(
        body, name=name, in_specs=ex.in_specs, out_specs=ex.out_specs, out_shape=ex.out_shapes,
        scratch_shapes=ex.scratch, compiler_params=pltpu.CompilerParams(has_side_effects=True),
    )(*xs)


def _ada_fwd(c_all, w_shard, b_shard):
    n = w_shard.shape[1]

    def body(c_ref, w_ref, b_ref, o_ref):
        cv = c_ref[...]
        act = (cv * _sigmoid(cv)).astype(BF16)
        o_ref[...] = jnp.dot(act, w_ref[...].astype(BF16), preferred_element_type=F32) + b_ref[...]

    return pl.pallas_call(
        body, name="ada_fwd", in_specs=[_full((N_DEV, D)), _full((D, n)), _full((1, n))],
        out_specs=_full((N_DEV, n)), out_shape=jax.ShapeDtypeStruct((N_DEV, n), F32), grid=(1,),
        compiler_params=_params(("arbitrary",)),
    )(c_all, w_shard, b_shard)


def _ada_bwd(c_all_t, dmod_pad):
    n = dmod_pad.shape[1]

    def body(c_ref, d_ref, o_ref):
        cv = c_ref[...]
        act = (cv * _sigmoid(cv)).astype(BF16)
        o_ref[...] = jnp.dot(act, d_ref[...].astype(BF16), preferred_element_type=F32)

    return pl.pallas_call(
        body, name="ada_bwd", in_specs=[_full((D, LANES)), _full((LANES, n))],
        out_specs=_full((D, n)), out_shape=jax.ShapeDtypeStruct((D, n), F32), grid=(1,),
        compiler_params=_params(("arbitrary",)),
    )(c_all_t, dmod_pad)


ADAM_ROWS = 256


def _adamw(parts, w, m, v, *, name):
    n, R, C = parts.shape
    tr = next((t for t in (ADAM_ROWS, 128, 64, 32, 16, SUBLANES) if R % t == 0), R)

    def body(p_ref, w_ref, m_ref, v_ref, g_ref, d_ref, nm_ref, nv_ref):
        g = p_ref[0].astype(F32)
        for j in range(1, n):
            g = g + p_ref[j].astype(F32)
        g_ref[...] = g
        nm = ADAM_B1 * m_ref[...] + (1.0 - ADAM_B1) * g
        nv = ADAM_B2 * v_ref[...] + (1.0 - ADAM_B2) * (g * g)
        nm_ref[...] = nm
        nv_ref[...] = nv
        m_hat = nm / (1.0 - ADAM_B1 ** ADAM_STEP)
        v_hat = nv / (1.0 - ADAM_B2 ** ADAM_STEP)
        d_ref[...] = -ADAM_LR * (m_hat / (jnp.sqrt(v_hat) + ADAM_EPS) + ADAM_WD * w_ref[...])

    row = pl.BlockSpec((tr, C), lambda i: (i, 0))
    return pl.pallas_call(
        body, name=name, grid=(R // tr,),
        in_specs=[pl.BlockSpec((n, tr, C), lambda i: (0, i, 0)), row, row, row], out_specs=[row] * 4,
        out_shape=[jax.ShapeDtypeStruct((R, C), F32)] * 4,
        compiler_params=_params(("parallel",)),
    )(parts, w, m, v)


SHARDED = (("w_in", D, IN_W, 1), ("w_branch_a", CONV_W, D, 1), ("w_branch_b", ATTN_W, D, 1), ("w_out", D, D, 0),
           ("w_up", D, 2 * D_FF, 1), ("w_down", D_FF, D, 0), ("conv_a_w", 3, CONV_W, 1),
           ("conv_ffn_w", 3, 2 * D_FF, 1))
MATRICES = SHARDED[:6]
LATE = MATRICES[1:]
CONVS = SHARDED[6:]
REPLICATED = (("b_ada", N_MOD * D), ("norm1_g", D), ("norm2_g", D), ("b_f", N_HEADS), ("q_norm_g", HEAD_DIM),
              ("k_norm_g", HEAD_DIM))


def _shard_shape(rows, cols, axis):
    return (rows // N_DEV, cols) if axis == 0 else (rows, cols // N_DEV)


def _pack_rows(flat, multiple):
    length = flat.shape[-1]
    rows = -(-length // PACK_W)
    rows = -(-rows // multiple) * multiple
    pad = [(0, 0)] * (flat.ndim - 1) + [(0, rows * PACK_W - length)]
    return jnp.pad(flat, pad).reshape(flat.shape[:-1] + (rows, PACK_W))


def _pack_shards(shards, spec, multiple, dtype):
    flat = jnp.concatenate([shards[name].reshape(-1).astype(dtype) for name, *_ in spec])
    return _pack_rows(flat, multiple)


def _join_lane_blocks(gathered):
    n, r, c = gathered.shape

    def body(g_ref, o_ref):
        for j in range(n):
            o_ref[:, j * c:(j + 1) * c] = g_ref[j]

    return pl.pallas_call(
        body, name="join_lane_blocks", grid=(1,), in_specs=[_full((n, r, c))], out_specs=_full((r, n * c)),
        out_shape=jax.ShapeDtypeStruct((r, n * c), gathered.dtype), compiler_params=_params(("arbitrary",)),
    )(gathered)


SHARD_PAD = 768


def _assemble_columns(gathered, shard_cols, out_cols, segments, *, name):
    n, rows, padw = gathered.shape
    assert n == N_DEV and padw == SHARD_PAD and shard_cols <= SHARD_PAD

    def body(g_ref, o_ref):
        j = pl.program_id(0)

        @pl.when(j == 0)
        def _():
            o_ref[...] = jnp.zeros_like(o_ref)

        for dev in range(N_DEV):
            @pl.when(j == dev)
            def _(dev=dev):
                x = g_ref[0]
                for lo, hi, delta in segments:
                    a, b = max(lo, dev * shard_cols), min(hi, (dev + 1) * shard_cols)
                    if a >= b:
                        continue
                    base = (a + delta) // LANES * LANES
                    width = -(-(b + delta - base) // LANES) * LANES
                    src = lax.broadcasted_iota(jnp.int32, (padw, width), 0) + dev * shard_cols
                    dst = lax.broadcasted_iota(jnp.int32, (padw, width), 1) + (base - delta)
                    place = jnp.where((src == dst) & (src >= a) & (src < b), 1.0, 0.0).astype(BF16)
                    moved = jnp.dot(x, place, preferred_element_type=F32).astype(BF16)
                    o_ref[:, base:base + width] = o_ref[:, base:base + width] + moved

    return pl.pallas_call(
        body, name=name, grid=(N_DEV,), in_specs=[pl.BlockSpec((1, rows, padw), lambda j: (j, 0, 0))],
        out_specs=_full((rows, out_cols)), out_shape=jax.ShapeDtypeStruct((rows, out_cols), BF16),
        compiler_params=_params(("arbitrary",)),
    )(gathered)


def _pad_shard(w):
    return jnp.pad(w.astype(BF16), ((0, 0), (0, SHARD_PAD - w.shape[1])))


W_IN_SEGMENTS = ((0, COL_GA, 0), (COL_GA, COL_GA + N_HEADS, COL_F - COL_GA), (COL_GA + N_HEADS, IN_W, -N_HEADS))


def _join_shards(gathered, axis):
    if axis == 0:
        return gathered.reshape(N_DEV * gathered.shape[1], gathered.shape[2])
    if gathered.shape[2] == LANES:
        return _join_lane_blocks(gathered)
    return jnp.concatenate([gathered[j] for j in range(N_DEV)], axis=1)


def _column_shards(pieces):
    total = sum(p.shape[1] for p in pieces)
    width = total // N_DEV
    shards = []
    for j in range(N_DEV):
        lo, hi, off, segs = j * width, (j + 1) * width, 0, []
        for p in pieces:
            a, b = max(lo, off), min(hi, off + p.shape[1])
            if a < b:
                segs.append(p[:, a - off:b - off])
            off += p.shape[1]
        shards.append(segs[0] if len(segs) == 1 else jnp.concatenate(segs, axis=1))
    return jnp.stack(shards)


def _split_shards(full, axis):
    if axis == 0:
        return full.reshape(N_DEV, full.shape[0] // N_DEV, full.shape[1])
    c = full.shape[1] // N_DEV
    return jnp.stack([full[:, j * c:(j + 1) * c] for j in range(N_DEV)])


def _unpack_shards(packed, spec):
    flat = packed.reshape(-1)
    out, off = {}, 0
    for name, rows, cols, axis in spec:
        r, c = _shard_shape(rows, cols, axis)
        out[name] = flat[off:off + r * c].reshape(r, c)
        off += r * c
    return out


def _unpack_gathered(gathered, spec):
    flat = gathered.reshape(N_DEV, -1)
    out, off = {}, 0
    for name, rows, cols, axis in spec:
        r, c = _shard_shape(rows, cols, axis)
        seg = flat[:, off:off + r * c].reshape(N_DEV, r, c)
        out[name] = seg.reshape(rows, cols) if axis == 0 else seg.transpose(1, 0, 2).reshape(rows, cols)
        off += r * c
    return out


def _pack_full_by_dest(full, spec, multiple):
    segs = []
    for name, rows, cols, axis in spec:
        r, c = _shard_shape(rows, cols, axis)
        a = full[name]
        seg = a.reshape(N_DEV, r, c) if axis == 0 else a.reshape(rows, N_DEV, c).transpose(1, 0, 2)
        segs.append(seg.reshape(N_DEV, r * c))
    return _pack_rows(jnp.concatenate(segs, axis=1), multiple)


def _pad_head_rows(w):
    n = w.shape[1]
    padded = jnp.pad(w.reshape(N_HEADS, HEAD_DIM, n), ((0, 0), (0, LANES - HEAD_DIM), (0, 0)))
    return padded.reshape(N_HEADS * LANES, n)


def kernel(x, c, w_ada, b_ada, norm1_g, w_in, b_f, conv_a_w, q_norm_g, k_norm_g, w_branch_a, w_branch_b, w_out, norm2_g, w_up, conv_ffn_w, w_down, loss_target, m_w_ada, m_b_ada, m_norm1_g, m_w_in, m_b_f, m_conv_a_w, m_q_norm_g, m_k_norm_g, m_w_branch_a, m_w_branch_b, m_w_out, m_norm2_g, m_w_up, m_conv_ffn_w, m_w_down, v_w_ada, v_b_ada, v_norm1_g, v_w_in, v_b_f, v_conv_a_w, v_q_norm_g, v_k_norm_g, v_w_branch_a, v_w_branch_b, v_w_out, v_norm2_g, v_w_up, v_conv_ffn_w, v_w_down):
    names = ("w_ada", "b_ada", "norm1_g", "w_in", "b_f", "conv_a_w", "q_norm_g", "k_norm_g", "w_branch_a",
             "w_branch_b", "w_out", "norm2_g", "w_up", "conv_ffn_w", "w_down")
    squeeze = lambda a: a[0] if a.ndim == 3 else a
    W = dict(zip(names, map(squeeze, (w_ada, b_ada, norm1_g, w_in, b_f, conv_a_w, q_norm_g, k_norm_g, w_branch_a,
                                      w_branch_b, w_out, norm2_g, w_up, conv_ffn_w, w_down))))
    M = dict(zip(names, map(squeeze, (m_w_ada, m_b_ada, m_norm1_g, m_w_in, m_b_f, m_conv_a_w, m_q_norm_g,
                                      m_k_norm_g, m_w_branch_a, m_w_branch_b, m_w_out, m_norm2_g, m_w_up,
                                      m_conv_ffn_w, m_w_down))))
    V = dict(zip(names, map(squeeze, (v_w_ada, v_b_ada, v_norm1_g, v_w_in, v_b_f, v_conv_a_w, v_q_norm_g,
                                      v_k_norm_g, v_w_branch_a, v_w_branch_b, v_w_out, v_norm2_g, v_w_up,
                                      v_conv_ffn_w, v_w_down))))
    me = 4 * lax.axis_index("x") + 2 * lax.axis_index("y") + lax.axis_index("c")
    ada_n = N_MOD * D // N_DEV

    small = jnp.concatenate([c.reshape(-1), W["conv_a_w"].reshape(-1), W["conv_ffn_w"].reshape(-1)])
    small_all, w_in_all = _gather_two_level([_pack_rows(small, SUBLANES), _pad_shard(W["w_in"])],
                                            name="gather_first")
    small_all = small_all.reshape(N_DEV, -1)
    c_all = small_all[:, :D]
    conv_all = _unpack_gathered(small_all[:, D:], CONVS)

    b_shard = lax.dynamic_slice(W["b_ada"], (0, me * ada_n), (1, ada_n))
    mod_part = _ada_fwd(c_all, W["w_ada"], b_shard)
    mod_all, = _exchange([mod_part], name="gather_mod", scatter=False)
    mod = lax.dynamic_index_in_dim(mod_all, me, axis=1, keepdims=False).reshape(1, N_MOD * D)

    wts = {"w_in": _assemble_columns(w_in_all, IN_W // N_DEV, IN_W_PAD, W_IN_SEGMENTS, name="assemble_w_in")}
    wts["w_in_t"] = wts["w_in"].T
    wts.update(conv_all)
    for name in ("norm1_g", "norm2_g", "q_norm_g", "k_norm_g", "b_f"):
        wts[name] = W[name]
    late = {name: _pad_shard(W[name]) if name == "w_up" else W[name].astype(BF16) for name, *_ in LATE}

    sq, grad_x, grads, parts = _local_step(x[0], loss_target[0], mod, wts, late)
    loss = lax.psum(sq[0, 0] * (0.5 / D), AXES)

    grads["b_ada"] = grads["mod"]
    rep_flat = lambda src: jnp.concatenate([src[name].reshape(-1) for name, _ in REPLICATED])
    rep_parts, = _exchange([_pack_rows(rep_flat(grads), 16)], name="gather_small_grads", scatter=False)
    rep_out = _adamw(rep_parts, *[_pack_rows(rep_flat(s), 16) for s in (W, M, V)], name="adamw_replicated")

    dmod_all = rep_parts.reshape(N_DEV, -1)[:, :N_MOD * D]
    dmod_mine = lax.dynamic_slice(dmod_all, (0, me * ada_n), (N_DEV, ada_n))
    g_ada = _ada_bwd(jnp.pad(c_all.T, ((0, 0), (0, LANES - N_DEV))),
                     jnp.pad(dmod_mine, ((0, LANES - N_DEV), (0, 0))))
    ada_out = _adamw(g_ada[None], W["w_ada"], M["w_ada"], V["w_ada"], name="adamw_ada")

    mat_out = {name: _adamw(parts[name], W[name], M[name], V[name], name="adamw_" + name) for name, *_ in MATRICES}
    conv_out = _adamw(parts["conv"], *[_pack_shards(s, CONVS, SUBLANES, F32) for s in (W, M, V)],
                      name="adamw_conv")

    results = []
    for kind in range(4):
        per = {"w_ada": ada_out[kind]}
        per.update({name: out[kind] for name, out in mat_out.items()})
        per.update(_unpack_shards(conv_out[kind], CONVS))
        flat, off = rep_out[kind].reshape(-1), 0
        for name, n in REPLICATED:
            per[name] = flat[off:off + n].reshape(1, n)
            off += n
        results.append(per)
    restore = lambda name, a: a[None] if W[name].ndim == 2 and name not in dict(REPLICATED) else a
    outs = [loss, grad_x[None]]
    for per in results:
        outs.extend(restore(name, per[name]) for name in names)
    return tuple(outs)
```

```python
import functools

import jax
import jax.numpy as jnp
import numpy as np
from jax import lax
from jax.experimental import pallas as pl
from jax.experimental.pallas import tpu as pltpu

F32 = jnp.float32
BF16 = jnp.bfloat16

N_DEV = 8
D = 1024
N_HEADS = 8
HEAD_DIM = 64
ATTN_W = 512
CONV_W = 512
D_FF = 2816
N_MOD = 6
IN_W = 5128
RMS_EPS = 1e-6
NEG_INF = -1e30

IN_W_PAD = 5376
COL_GA = 3072
COL_GB = 4096
COL_F = 5120
F_PAD = 128

ADAM_LR = 0.001
ADAM_B1 = 0.9
ADAM_B2 = 0.999
ADAM_EPS = 1e-08
ADAM_WD = 0.01
ADAM_STEP = 10

LANES = 128
SUBLANES = 8
BF16_ROWS = 16
VMEM_LIMIT = 52 * 1024 * 1024
TOKEN_TILE = 512
MATMUL_TILE = 1024
ATTN_BLOCK = 512
PACK_W = 1024

MESH = pl.DeviceIdType.MESH
AXES = ("x", "y", "c")


def _params(sem=None, **kw):
    return pltpu.CompilerParams(dimension_semantics=sem, vmem_limit_bytes=VMEM_LIMIT, **kw)


def _full(shape):
    nd = len(shape)
    return pl.BlockSpec(shape, lambda *_: (0,) * nd)


def _tn_dot(a, b):
    return lax.dot_general(a, b, (((0,), (0,)), ((), ())), preferred_element_type=F32)


def _matmul(a, b, *, name, tm, tn, tk, out_dtype=F32, trans_a=False, exchange=None):
    if trans_a:
        K, M = a.shape
    else:
        M, K = a.shape
    N = b.shape[1]
    assert b.shape[0] == K and M % tm == 0 and N % tn == 0 and K % tk == 0, (name, a.shape, b.shape)
    nm, nn, nk = M // tm, N // tn, K // tk

    def body(*refs):
        if exchange is None:
            a_ref, b_ref, o_ref, *own = refs
        else:
            (a_ref, b_ref), (o_ref,), own, xrefs = exchange.split(refs, 2, 1)
            ids = [pl.program_id(d) for d in range(3)]
            first = jnp.logical_and(jnp.logical_and(ids[0] == 0, ids[1] == 0), ids[2] == 0)
            last = jnp.logical_and(jnp.logical_and(ids[0] == nn - 1, ids[1] == nm - 1), ids[2] == nk - 1)
            _ride(exchange, first, last, xrefs)
        k = pl.program_id(2)
        av = a_ref[...].astype(BF16)
        bv = b_ref[...].astype(BF16)
        prod = _tn_dot(av, bv) if trans_a else jnp.dot(av, bv, preferred_element_type=F32)
        if nk == 1:
            o_ref[...] = prod.astype(out_dtype)
            return
        acc_ref, = own

        @pl.when(k == 0)
        def _():
            acc_ref[...] = prod

        @pl.when(k > 0)
        def _():
            acc_ref[...] += prod

        @pl.when(k == nk - 1)
        def _():
            o_ref[...] = acc_ref[...].astype(out_dtype)

    if trans_a:
        a_spec = pl.BlockSpec((tk, tm), lambda j, i, k: (k, i))
    else:
        a_spec = pl.BlockSpec((tm, tk), lambda j, i, k: (i, k))
    in_specs = [a_spec, pl.BlockSpec((tk, tn), lambda j, i, k: (k, j))]
    out_spec = pl.BlockSpec((tm, tn), lambda j, i, k: (i, j))
    out_shape = jax.ShapeDtypeStruct((M, N), out_dtype)
    scratch = [pltpu.VMEM((tm, tn), F32)] if nk > 1 else []
    if exchange is None:
        return pl.pallas_call(
            body, name=name, grid=(nn, nm, nk), in_specs=in_specs, out_specs=out_spec, out_shape=out_shape,
            scratch_shapes=scratch, compiler_params=_params(("parallel", "parallel", "arbitrary")),
        )(a, b)
    return pl.pallas_call(
        body, name=name, grid=(nn, nm, nk), in_specs=in_specs + exchange.in_specs,
        out_specs=[out_spec] + exchange.out_specs, out_shape=[out_shape] + exchange.out_shapes,
        scratch_shapes=scratch + exchange.scratch, compiler_params=_params(("arbitrary",) * 3),
    )(a, b, *exchange.xs)


def _norm_bwd_tile(dh, ins, outs, first):
    x_ref, dr_ref, g_ref, sc_ref = ins[:4]
    dx_ref, dsh_ref, dsc_ref, dg_ref = outs[:4]

    @pl.when(first)
    def _():
        for ref in outs[1:4] + outs[5:]:
            ref[...] = jnp.zeros_like(ref)

    xv = x_ref[...]
    gv = g_ref[...]
    one_sc = 1.0 + sc_ref[...]
    inv = lax.rsqrt(jnp.mean(xv * xv, axis=-1, keepdims=True) + RMS_EPS)
    xn = xv * inv
    dxn = dh * (gv * one_sc)
    dx = dr_ref[...] + inv * (dxn - xn * jnp.mean(dxn * xn, axis=-1, keepdims=True))
    dx_ref[...] = dx
    dhxn = dh * xn
    dsh_ref[...] += _rows8(dh)
    dsc_ref[...] += _rows8(dhxn * gv)
    dg_ref[...] += _rows8(dhxn * one_sc)
    if len(ins) == 6:
        mix_ref, g1_ref = ins[4:]
        dmix_ref, dg1_ref = outs[4:]
        dmix_ref[...] = (dx * g1_ref[...]).astype(BF16)
        dg1_ref[...] += _rows8(dx * mix_ref[...])


def _matmul_pieces(pieces, b, *, name, tm, exchange=None, norm_bwd=None):
    M = pieces[0].shape[0]
    widths = [p.shape[1] for p in pieces]
    offsets = [sum(widths[:i]) for i in range(len(widths))]
    N = b.shape[1]
    assert b.shape[0] >= sum(widths) and M % tm == 0, (name, widths, b.shape)
    n_p, nm = len(pieces), M // tm
    extra = list(norm_bwd) if norm_bwd is not None else []
    n_in = n_p + 1 + len(extra)
    n_out = len(extra) if norm_bwd is not None else 1

    def body(*refs):
        i = pl.program_id(0)
        if exchange is None:
            ins, outs = refs[:n_in], refs[n_in:]
        else:
            ins, outs, _, xrefs = exchange.split(refs, n_in, n_out)
            _ride(exchange, i == 0, i == nm - 1, xrefs)
        b_ref = ins[n_p]
        acc = None
        for a_ref, off, w in zip(ins[:n_p], offsets, widths):
            term = jnp.dot(a_ref[...].astype(BF16), b_ref[off:off + w, :], preferred_element_type=F32)
            acc = term if acc is None else acc + term
        if norm_bwd is None:
            outs[0][...] = acc
        else:
            _norm_bwd_tile(acc, tuple(ins[n_p + 1:]), tuple(outs), first=i == 0)

    row, vec, part = pl.BlockSpec((tm, N), lambda i: (i, 0)), _full((1, N)), _full((SUBLANES, N))
    part_shape = jax.ShapeDtypeStruct((SUBLANES, N), F32)
    in_specs = [pl.BlockSpec((tm, w), lambda i: (i, 0)) for w in widths] + [_full(b.shape)]
    out_specs, out_shape = [row], [jax.ShapeDtypeStruct((M, N), F32)]
    if norm_bwd is not None:
        in_specs += [row, row, vec, vec] + ([row, vec] if len(extra) == 6 else [])
        out_specs += [part] * 3 + ([row, part] if len(extra) == 6 else [])
        out_shape += [part_shape] * 3 + ([jax.ShapeDtypeStruct((M, N), BF16), part_shape] if len(extra) == 6 else [])
    sequential = exchange is not None or norm_bwd is not None
    xs = exchange.xs if exchange is not None else []
    result = pl.pallas_call(
        body, name=name, grid=(nm,), in_specs=in_specs + (exchange.in_specs if exchange else []),
        out_specs=out_specs + (exchange.out_specs if exchange else []),
        out_shape=out_shape + (exchange.out_shapes if exchange else []),
        scratch_shapes=exchange.scratch if exchange else [],
        compiler_params=_params(("arbitrary" if sequential else "parallel",)),
    )(*pieces, b, *extra, *xs)
    return result[0] if len(result) == 1 else result


def _matmul_tn_pieces(a, pieces, *, name, tk):
    K, M = a.shape
    widths = [p.shape[1] for p in pieces]
    n_p, nk = len(pieces), K // tk

    def body(*refs):
        a_ref, p_refs, o_refs = refs[0], refs[1:n_p + 1], refs[n_p + 1:]
        k = pl.program_id(0)
        av = a_ref[...].astype(BF16)
        for p_ref, o_ref in zip(p_refs, o_refs):
            prod = _tn_dot(av, p_ref[...].astype(BF16))

            @pl.when(k == 0)
            def _():
                o_ref[...] = prod

            @pl.when(k > 0)
            def _():
                o_ref[...] += prod

    return pl.pallas_call(
        body, name=name, grid=(nk,),
        in_specs=[pl.BlockSpec((tk, M), lambda k: (k, 0))] + [pl.BlockSpec((tk, w), lambda k: (k, 0)) for w in widths],
        out_specs=[_full((M, w)) for w in widths], out_shape=[jax.ShapeDtypeStruct((M, w), F32) for w in widths],
        compiler_params=_params(("arbitrary",)),
    )(a, *pieces)


def _split_dot(x, mat, parts):
    out = None
    rem = x
    for p in range(parts):
        piece = rem.astype(BF16)
        term = jnp.dot(piece, mat, preferred_element_type=F32)
        out = term if out is None else out + term
        if p + 1 < parts:
            rem = rem - piece.astype(F32)
    return out


def _sigmoid(x):
    return 0.5 * jnp.tanh(0.5 * x) + 0.5


def _rows8(x):
    r, c = x.shape
    return jnp.sum(x.reshape(r // SUBLANES, SUBLANES, c), axis=0)


def _shift_down(blk, prev8, n):
    rolled = pltpu.roll(blk, n, axis=0)
    prev_rolled = pltpu.roll(prev8, n, axis=0)
    rows = lax.broadcasted_iota(jnp.int32, prev8.shape, 0)
    first = jnp.where(rows < n, prev_rolled, rolled[0:SUBLANES])
    return jnp.concatenate([first, rolled[SUBLANES:]], axis=0)


def _shift_up(blk, next8, n):
    r = blk.shape[0]
    rolled = pltpu.roll(blk, r - n, axis=0)
    next_rolled = pltpu.roll(next8, SUBLANES - n, axis=0)
    rows = lax.broadcasted_iota(jnp.int32, next8.shape, 0)
    last = jnp.where(rows >= SUBLANES - n, next_rolled, rolled[r - SUBLANES:])
    return jnp.concatenate([rolled[:r - SUBLANES], last], axis=0)


def _prev_spec(tm, width, col):
    per = tm // SUBLANES
    return pl.BlockSpec((SUBLANES, width), lambda i, *_: (jnp.maximum(i * per - 1, 0), col))


def _next_spec(tm, width, col, n_tiles):
    per = tm // SUBLANES
    last = n_tiles * per - 1
    return pl.BlockSpec((SUBLANES, width), lambda i, *_: (jnp.minimum((i + 1) * per, last), col))


def _group_matrix():
    idx = np.arange(ATTN_W) // HEAD_DIM
    return jnp.asarray((idx[:, None] == idx[None, :]).astype(np.float32), BF16)


def _norm_mod(x, g, sc, sh, *, name):
    T = x.shape[0]
    tm = min(TOKEN_TILE, T)

    def body(x_ref, g_ref, sc_ref, sh_ref, o_ref):
        xv = x_ref[...]
        inv = lax.rsqrt(jnp.mean(xv * xv, axis=-1, keepdims=True) + RMS_EPS)
        o_ref[...] = ((xv * inv) * g_ref[...] * (1.0 + sc_ref[...]) + sh_ref[...]).astype(BF16)

    row = pl.BlockSpec((tm, D), lambda i: (i, 0))
    return pl.pallas_call(
        body, name=name, grid=(T // tm,),
        in_specs=[row, _full((1, D)), _full((1, D)), _full((1, D))],
        out_specs=row, out_shape=jax.ShapeDtypeStruct((T, D), BF16),
        compiler_params=_params(("parallel",)),
    )(x, g, sc, sh)


LANE_ONE = 64
LANE_F = 67
LANE_LSE = 70
LANE_SUM = 73


def _pieces(x):
    hi = x.astype(BF16).astype(F32)
    rest = x - hi
    mid = rest.astype(BF16).astype(F32)
    return hi, mid, rest - mid


def _run(start, vals):
    return [(start + i, v) for i, v in enumerate(vals)]


def _head_lanes(a, h):
    blk = a[:, LANES * (h // 2):LANES * (h // 2) + LANES]
    return blk if h % 2 == 0 else pltpu.roll(blk, HEAD_DIM, axis=1)


def _branch_prep(proj, fcum, conv_w8, qg, kg, gmat):
    T = proj.shape[0]
    tm = min(TOKEN_TILE, T)
    nt = T // tm

    def body(cb_ref, cc_ref, cv_ref, q_ref, k_ref, v_ref, f_ref, ccp_ref, cvp_ref, w_ref, qg_ref, kg_ref, g_ref,
             ya_ref, qa_ref, ka_ref, va_ref):
        i = pl.program_id(0)
        z = cc_ref[...] * cv_ref[...]
        zp = jnp.where(i > 0, ccp_ref[...] * cvp_ref[...], 0.0)
        w = w_ref[...]
        cz = _shift_down(z, zp, 2) * w[0:1] + _shift_down(z, zp, 1) * w[1:2] + z * w[2:3]
        ya_ref[...] = (cb_ref[...] * cz).astype(BF16)
        gm = g_ref[...]

        def normed(src, gain, scale):
            v = src[...]
            ms = _split_dot(v * v, gm, 2) * (1.0 / HEAD_DIM)
            return (v * lax.rsqrt(ms + RMS_EPS)) * gain[...] * scale

        qn = normed(q_ref, qg_ref, 1.0 / np.sqrt(HEAD_DIM))
        kn = normed(k_ref, kg_ref, 1.0)
        vv = v_ref[...]
        lane = lax.broadcasted_iota(jnp.int32, (tm, LANES), 1)
        low = lane < HEAD_DIM
        in_run = lambda start: jnp.logical_and(lane >= start, lane < start + 3)
        q_ones = jnp.where(jnp.logical_or(in_run(LANE_ONE), lane == LANE_SUM), 1.0, 0.0)
        k_ones = jnp.where(jnp.logical_or(in_run(LANE_F), in_run(LANE_LSE)), 1.0, 0.0)
        v_ones = jnp.where(in_run(LANE_ONE), 1.0, 0.0)
        f3 = jnp.concatenate(_pieces(f_ref[...]), axis=1).astype(BF16)
        src = lax.broadcasted_iota(jnp.int32, (3 * LANES, LANES), 0)
        dst = lax.broadcasted_iota(jnp.int32, (3 * LANES, LANES), 1)
        for h in range(N_HEADS):
            def pick(start, h=h):
                hit = jnp.logical_and(src - h == (dst - start) * LANES, jnp.logical_and(dst >= start, dst < start + 3))
                return jnp.dot(f3, jnp.where(hit, 1.0, 0.0).astype(BF16), preferred_element_type=F32)

            qa_ref[h] = jnp.where(low, _head_lanes(qn, h), q_ones + pick(LANE_F)).astype(BF16)
            ka_ref[h] = jnp.where(low, _head_lanes(kn, h), k_ones - pick(LANE_ONE)).astype(BF16)
            va_ref[h] = jnp.where(low, _head_lanes(vv, h), v_ones).astype(BF16)

    blk = lambda col: pl.BlockSpec((tm, CONV_W), lambda i: (i, col))
    heads = pl.BlockSpec((N_HEADS, tm, LANES), lambda i: (0, i, 0))
    return pl.pallas_call(
        body, name="branch_prep", grid=(nt,),
        in_specs=[blk(0), blk(1), blk(2), blk(3), blk(4), blk(5), pl.BlockSpec((tm, F_PAD), lambda i: (i, 0)),
                  _prev_spec(tm, CONV_W, 1), _prev_spec(tm, CONV_W, 2),
                  _full((SUBLANES, CONV_W)), _full((1, ATTN_W)), _full((1, ATTN_W)), _full((ATTN_W, ATTN_W))],
        out_specs=[pl.BlockSpec((tm, CONV_W), lambda i: (i, 0)), heads, heads, heads],
        out_shape=[jax.ShapeDtypeStruct((T, CONV_W), BF16)] + [jax.ShapeDtypeStruct((N_HEADS, T, LANES), BF16)] * 3,
        compiler_params=_params(("parallel",)),
    )(proj, proj, proj, proj, proj, proj, fcum, proj, proj, conv_w8, qg, kg, gmat)


def _cumsum(x, *, reverse, name, col=0, gate_bias=None):
    T = x.shape[0]
    tm = min(TOKEN_TILE, T)
    nt = T // tm

    def body(x_ref, b_ref, o_ref, carry_ref):
        i = pl.program_id(0)

        @pl.when(i == 0)
        def _():
            carry_ref[...] = jnp.zeros_like(carry_ref)

        r = lax.broadcasted_iota(jnp.int32, (tm, tm), 0)
        c = lax.broadcasted_iota(jnp.int32, (tm, tm), 1)
        tri = jnp.where((c >= r) if reverse else (c <= r), 1.0, 0.0).astype(BF16)
        xv = x_ref[...]
        if gate_bias is not None:
            fx = xv + b_ref[...]
            xv = jnp.minimum(fx, 0.0) - jnp.log(1.0 + jnp.exp(-jnp.abs(fx)))
        out = _split_dot_left(tri, xv, 3) + carry_ref[0:1]
        o_ref[...] = out
        carry_ref[...] = jnp.broadcast_to(out[0:1] if reverse else out[tm - 1:tm], carry_ref.shape)

    rows = (lambda i: nt - 1 - i) if reverse else (lambda i: i)
    bias = jnp.zeros((1, F_PAD), F32) if gate_bias is None else gate_bias
    return pl.pallas_call(
        body, name=name, grid=(nt,),
        in_specs=[pl.BlockSpec((tm, F_PAD), lambda i: (rows(i), col)), _full((1, F_PAD))],
        out_specs=pl.BlockSpec((tm, F_PAD), lambda i: (rows(i), 0)),
        out_shape=jax.ShapeDtypeStruct((T, F_PAD), F32),
        scratch_shapes=[pltpu.VMEM((SUBLANES, F_PAD), F32)],
        compiler_params=_params(("arbitrary",)),
    )(x, bias)


def _split_dot_left(mat, x, parts):
    out = None
    rem = x
    for p in range(parts):
        piece = rem.astype(BF16)
        term = jnp.dot(mat, piece, preferred_element_type=F32)
        out = term if out is None else out + term
        if p + 1 < parts:
            rem = rem - piece.astype(F32)
    return out


def _out_resid_norm(x, merged, w_out, g1, g, sc, sh):
    T = x.shape[0]
    tm = min(TOKEN_TILE, T)

    def body(x_ref, m_ref, w_ref, g1_ref, g_ref, sc_ref, sh_ref, mix_ref, x1_ref, h_ref):
        mix = jnp.dot(m_ref[...], w_ref[...], preferred_element_type=F32)
        mix_ref[...] = mix
        x1 = x_ref[...] + g1_ref[...] * mix
        x1_ref[...] = x1
        inv = lax.rsqrt(jnp.mean(x1 * x1, axis=-1, keepdims=True) + RMS_EPS)
        h_ref[...] = ((x1 * inv) * g_ref[...] * (1.0 + sc_ref[...]) + sh_ref[...]).astype(BF16)

    row = pl.BlockSpec((tm, D), lambda i: (i, 0))
    vec = _full((1, D))
    return pl.pallas_call(
        body, name="out_resid_norm", grid=(T // tm,),
        in_specs=[row, row, _full((D, D)), vec, vec, vec, vec], out_specs=[row, row, row],
        out_shape=[jax.ShapeDtypeStruct((T, D), F32), jax.ShapeDtypeStruct((T, D), F32),
                   jax.ShapeDtypeStruct((T, D), BF16)],
        compiler_params=_params(("parallel",)),
    )(x, merged, w_out, g1, g, sc, sh)


FFN_TM = 256
FFN_TC = 1408


def _ffn_act_fwd(u, w8):
    T = u.shape[0]
    tm = min(FFN_TM, T)
    nt = T // tm
    nc = D_FF // FFN_TC

    def body(ug_ref, uv_ref, ugp_ref, uvp_ref, wg_ref, wv_ref, o_ref, cg_ref, cv_ref):
        i = pl.program_id(1)

        def conv(u_ref, p_ref, w_ref):
            uv = u_ref[...]
            up = jnp.where(i > 0, p_ref[...], 0.0)
            w = w_ref[...]
            return _shift_down(uv, up, 2) * w[0:1] + _shift_down(uv, up, 1) * w[1:2] + uv * w[2:3]

        gate = conv(ug_ref, ugp_ref, wg_ref)
        val = conv(uv_ref, uvp_ref, wv_ref)
        cg_ref[...] = gate.astype(BF16)
        cv_ref[...] = val.astype(BF16)
        o_ref[...] = (gate * _sigmoid(gate) * val).astype(BF16)

    per = tm // SUBLANES
    blk = lambda off: pl.BlockSpec((tm, FFN_TC), lambda j, i: (i, j + off))
    prev = lambda off: pl.BlockSpec((SUBLANES, FFN_TC), lambda j, i: (jnp.maximum(i * per - 1, 0), j + off))
    wblk = lambda off: pl.BlockSpec((SUBLANES, FFN_TC), lambda j, i: (0, j + off))
    return pl.pallas_call(
        body, name="ffn_act_fwd", grid=(nc, nt),
        in_specs=[blk(0), blk(nc), prev(0), prev(nc), wblk(0), wblk(nc)],
        out_specs=[blk(0), blk(0), blk(0)],
        out_shape=[jax.ShapeDtypeStruct((T, D_FF), BF16)] * 3,
        compiler_params=_params(("parallel", "parallel")),
    )(u, u, u, u, w8, w8)


def _down_loss_head(x1, act, w_down, g2, target):
    T = x1.shape[0]
    tm = min(TOKEN_TILE, T)

    def body(x1_ref, a_ref, w_ref, g2_ref, t_ref, dy_ref, dff_ref, loss_ref, dg2_ref):
        i = pl.program_id(0)

        @pl.when(i == 0)
        def _():
            loss_ref[...] = jnp.zeros_like(loss_ref)
            dg2_ref[...] = jnp.zeros_like(dg2_ref)

        ff = jnp.dot(a_ref[...], w_ref[...], preferred_element_type=F32)
        err = x1_ref[...] + g2_ref[...] * ff - t_ref[...]
        dy = err * (1.0 / D)
        dy_ref[...] = dy
        dff_ref[...] = (dy * g2_ref[...]).astype(BF16)
        loss_ref[...] += _rows8(err * err)
        dg2_ref[...] += _rows8(dy * ff)

    row = pl.BlockSpec((tm, D), lambda i: (i, 0))
    acc = _full((SUBLANES, D))
    return pl.pallas_call(
        body, name="down_loss_head", grid=(T // tm,),
        in_specs=[row, pl.BlockSpec((tm, D_FF), lambda i: (i, 0)), _full((D_FF, D)), _full((1, D)), row],
        out_specs=[row, row, acc, acc],
        out_shape=[jax.ShapeDtypeStruct((T, D), F32), jax.ShapeDtypeStruct((T, D), BF16),
                   jax.ShapeDtypeStruct((SUBLANES, D), F32), jax.ShapeDtypeStruct((SUBLANES, D), F32)],
        compiler_params=_params(("arbitrary",)),
    )(x1, act, w_down, g2, target)


def _nt_dot(a, b):
    return lax.dot_general(a, b, (((1,), (1,)), ((), ())), preferred_element_type=F32)


def _causal(n, keys_on_rows=False):
    r = lax.broadcasted_iota(jnp.int32, (n, n), 0)
    c = lax.broadcasted_iota(jnp.int32, (n, n), 1)
    return (c >= r) if keys_on_rows else (c <= r)


def _sweep(lo, hi, step, carry, group=2):
    while group >= 1:
        def several(j, cr, lo=lo, group=group):
            for g in range(group):
                cr = step(lo + group * j + g, cr)
            return cr

        passes = (hi - lo) // group
        carry = lax.fori_loop(0, passes, several, carry)
        lo = lo + group * passes
        group //= 2
    return carry


def _grid_ends(n0, n1):
    i0, i1 = pl.program_id(0), pl.program_id(1)
    return jnp.logical_and(i0 == 0, i1 == 0), jnp.logical_and(i0 == n0 - 1, i1 == n1 - 1)


def _attn_fwd(qa, ka, va, exchange=None):
    nh, T, _ = qa.shape
    bq = min(ATTN_BLOCK, T)
    nq = T // bq

    def body(*refs):
        if exchange is None:
            q_ref, k_ref, v_ref, o_ref, qb_ref = refs
        else:
            (q_ref, k_ref, v_ref), (o_ref, qb_ref), _, xrefs = exchange.split(refs, 3, 2)
            _ride(exchange, *_grid_ends(nh, nq), xrefs)
        qi = pl.program_id(1)
        q = q_ref[0]

        def step(kb, carry, masked=False):
            m, acc = carry
            start = pl.multiple_of(kb * bq, bq)
            s = _nt_dot(q, k_ref[0, pl.ds(start, bq), :])
            if masked:
                s = jnp.where(_causal(bq), s, NEG_INF)
            m_new = jnp.maximum(m, jnp.max(s, axis=-1, keepdims=True))
            p = jnp.exp(s - m_new).astype(BF16)
            acc = jnp.exp(m - m_new) * acc + jnp.dot(p, v_ref[0, pl.ds(start, bq), :], preferred_element_type=F32)
            return m_new, acc

        init = (jnp.full((bq, 1), NEG_INF, F32), jnp.zeros((bq, LANES), F32))
        m, acc = step(qi, _sweep(0, qi, step, init, group=4), masked=True)
        l = acc[:, LANE_ONE:LANE_ONE + 1]
        o_ref[0] = acc / l
        lane = lax.broadcasted_iota(jnp.int32, (bq, LANES), 1)
        qf = q.astype(F32)
        for idx, piece in _run(LANE_LSE, _pieces(m + jnp.log(l))):
            qf = jnp.where(lane == idx, -piece, qf)
        qb_ref[0] = qf.astype(BF16)

    tile = pl.BlockSpec((1, bq, LANES), lambda h, i: (h, i, 0))
    whole = pl.BlockSpec((1, T, LANES), lambda h, i: (h, 0, 0))
    out_shape = [jax.ShapeDtypeStruct((nh, T, LANES), F32), jax.ShapeDtypeStruct((nh, T, LANES), BF16)]
    if exchange is None:
        return pl.pallas_call(
            body, name="attn_fwd", grid=(nh, nq), in_specs=[tile, whole, whole], out_specs=[tile, tile],
            out_shape=out_shape, compiler_params=_params(("parallel", "parallel")),
        )(qa, ka, va)
    return pl.pallas_call(
        body, name="attn_fwd", grid=(nh, nq), in_specs=[tile, whole, whole] + exchange.in_specs,
        out_specs=[tile, tile] + exchange.out_specs, out_shape=out_shape + exchange.out_shapes,
        scratch_shapes=exchange.scratch, compiler_params=_params(("arbitrary", "arbitrary")),
    )(qa, ka, va, *exchange.xs)


def _branch_merge_fwd(ya0, o_h, proj, wba, wbb_heads):
    nh, T, _ = o_h.shape
    tm = min(TOKEN_TILE, T)

    def body(ya0_ref, o_ref, ga_ref, gb_ref, wa_ref, wb_ref, ya_ref, yb_ref, m_ref):
        ya = jnp.dot(ya0_ref[...], wa_ref[...], preferred_element_type=F32)
        yb = jnp.dot(o_ref[0].astype(BF16), wb_ref[0:LANES, :], preferred_element_type=F32)
        for h in range(1, nh):
            yb += jnp.dot(o_ref[h].astype(BF16), wb_ref[h * LANES:(h + 1) * LANES, :], preferred_element_type=F32)
        ya_ref[...] = ya.astype(BF16)
        yb_ref[...] = yb.astype(BF16)
        m_ref[...] = (_sigmoid(ga_ref[...]) * ya + _sigmoid(gb_ref[...]) * yb).astype(BF16)

    row = pl.BlockSpec((tm, D), lambda i: (i, 0))
    return pl.pallas_call(
        body, name="branch_merge_fwd", grid=(T // tm,),
        in_specs=[pl.BlockSpec((tm, CONV_W), lambda i: (i, 0)), pl.BlockSpec((nh, tm, LANES), lambda i: (0, i, 0)),
                  pl.BlockSpec((tm, D), lambda i: (i, COL_GA // D)), pl.BlockSpec((tm, D), lambda i: (i, COL_GB // D)),
                  _full((CONV_W, D)), _full((nh * LANES, D))],
        out_specs=[row, row, row],
        out_shape=[jax.ShapeDtypeStruct((T, D), BF16)] * 3,
        compiler_params=_params(("parallel",)),
    )(ya0, o_h, proj, proj, wba, wbb_heads)


def _branch_b_bwd(dyb, o_h, wbb_heads_t):
    nh, T, _ = o_h.shape
    tm = min(TOKEN_TILE, T)

    def body(dyb_ref, o_ref, w_ref, out_ref):
        do = jnp.dot(dyb_ref[...], w_ref[...], preferred_element_type=F32)
        lane = lax.broadcasted_iota(jnp.int32, (tm, LANES), 1)
        for h in range(nh):
            g = do[:, h * LANES:(h + 1) * LANES].astype(BF16).astype(F32)
            delta = jnp.sum(g * o_ref[h], axis=-1, keepdims=True)
            for idx, piece in _run(LANE_ONE, _pieces(delta)):
                g = jnp.where(lane == idx, -piece, g)
            out_ref[h] = g.astype(BF16)

    heads = pl.BlockSpec((nh, tm, LANES), lambda i: (0, i, 0))
    return pl.pallas_call(
        body, name="branch_b_bwd", grid=(T // tm,),
        in_specs=[pl.BlockSpec((tm, D), lambda i: (i, 0)), heads, _full((D, nh * LANES))],
        out_specs=heads, out_shape=jax.ShapeDtypeStruct((nh, T, LANES), BF16),
        compiler_params=_params(("parallel",)),
    )(dyb, o_h, wbb_heads_t)


def _branch_b_dw(o_h, dyb):
    nh, T, _ = o_h.shape
    tk = min(TOKEN_TILE, T)

    def body(o_ref, dyb_ref, out_ref):
        @pl.when(pl.program_id(0) == 0)
        def _():
            out_ref[...] = jnp.zeros_like(out_ref)

        g = dyb_ref[...]
        for h in range(nh):
            out_ref[h] += _tn_dot(o_ref[h].astype(BF16), g)

    return pl.pallas_call(
        body, name="branch_b_dw", grid=(T // tk,),
        in_specs=[pl.BlockSpec((nh, tk, LANES), lambda k: (0, k, 0)), pl.BlockSpec((tk, D), lambda k: (k, 0))],
        out_specs=_full((nh, LANES, D)), out_shape=jax.ShapeDtypeStruct((nh, LANES, D), F32),
        compiler_params=_params(("arbitrary",)),
    )(o_h, dyb)


def _attn_bwd(qb, ka, va, doa, exchange=None):
    nh, T, _ = qb.shape
    bk = min(ATTN_BLOCK, T)
    nk = T // bk

    def body(*refs):
        if exchange is None:
            q_ref, do_ref, k_ref, v_ref, dq_ref, dk_ref, dv_ref = refs
        else:
            (q_ref, do_ref, k_ref, v_ref), (dq_ref, dk_ref, dv_ref), _, xrefs = exchange.split(refs, 4, 3)
            _ride(exchange, *_grid_ends(nh, nk), xrefs)
        ki = pl.program_id(1)

        @pl.when(ki == 0)
        def _():
            dq_ref[...] = jnp.zeros_like(dq_ref)

        k = k_ref[0]
        v = v_ref[0]

        def step(qi, carry, masked):
            dk, dv = carry
            rows = pl.ds(pl.multiple_of(qi * bk, bk), bk)
            q = q_ref[0, rows, :]
            g = do_ref[0, rows, :]
            pt = jnp.exp(_nt_dot(k, q))
            if masked:
                pt = jnp.where(_causal(bk, keys_on_rows=True), pt, 0.0)
            dv = dv + jnp.dot(pt.astype(BF16), g, preferred_element_type=F32)
            dst = (pt * _nt_dot(v, g)).astype(BF16)
            dk = dk + jnp.dot(dst, q, preferred_element_type=F32)
            dq_ref[0, rows, :] += _tn_dot(dst, k)
            return dk, dv

        init = (jnp.zeros((bk, LANES), F32), jnp.zeros((bk, LANES), F32))
        carry = step(ki, init, True)
        dk_ref[0], dv_ref[0] = _sweep(ki + 1, nk, lambda qi, cr: step(qi, cr, False), carry)

    tile = pl.BlockSpec((1, bk, LANES), lambda h, i: (h, i, 0))
    whole = pl.BlockSpec((1, T, LANES), lambda h, i: (h, 0, 0))
    out_shape = [jax.ShapeDtypeStruct((nh, T, LANES), F32)] * 3
    if exchange is None:
        return pl.pallas_call(
            body, name="attn_bwd", grid=(nh, nk), in_specs=[whole, whole, tile, tile],
            out_specs=[whole, tile, tile], out_shape=out_shape, compiler_params=_params(("parallel", "arbitrary")),
        )(qb, doa, ka, va)
    return pl.pallas_call(
        body, name="attn_bwd", grid=(nh, nk), in_specs=[whole, whole, tile, tile] + exchange.in_specs,
        out_specs=[whole, tile, tile] + exchange.out_specs, out_shape=out_shape + exchange.out_shapes,
        scratch_shapes=exchange.scratch, compiler_params=_params(("arbitrary", "arbitrary")),
    )(qb, doa, ka, va, *exchange.xs)


def _attn_unpack(dq_h, dk_h, dv_h):
    nh, T, _ = dq_h.shape
    tm = min(TOKEN_TILE, T)

    def body(dq_ref, dk_ref, dv_ref, q_out, k_out, v_out, f_out):
        lane = lax.broadcasted_iota(jnp.int32, (tm, LANES), 1)
        low = lane < HEAD_DIM
        for src, dst in ((dq_ref, q_out), (dk_ref, k_out), (dv_ref, v_out)):
            for pair in range(nh // 2):
                both = jnp.where(low, src[2 * pair], pltpu.roll(src[2 * pair + 1], HEAD_DIM, axis=1))
                dst[:, LANES * pair:LANES * (pair + 1)] = both.astype(dst.dtype)
        df = jnp.zeros((tm, LANES), F32)
        for h in range(nh):
            col = dq_ref[h][:, LANE_F:LANE_F + 1] - dk_ref[h][:, LANE_SUM:LANE_SUM + 1]
            df = jnp.where(lane == h, col, df)
        f_out[...] = df

    heads = pl.BlockSpec((nh, tm, LANES), lambda i: (0, i, 0))
    tok = pl.BlockSpec((tm, ATTN_W), lambda i: (i, 0))
    return pl.pallas_call(
        body, name="attn_unpack", grid=(T // tm,), in_specs=[heads, heads, heads],
        out_specs=[tok, tok, tok, pl.BlockSpec((tm, F_PAD), lambda i: (i, 0))],
        out_shape=[jax.ShapeDtypeStruct((T, ATTN_W), BF16)] * 3 + [jax.ShapeDtypeStruct((T, F_PAD), F32)],
        compiler_params=_params(("parallel",)),
    )(dq_h, dk_h, dv_h)


def _ffn_act_bwd(u, cg, cv, da, w8):
    T = u.shape[0]
    tm = min(FFN_TM, T)
    nt = T // tm
    nc = D_FF // FFN_TC

    def body(ug_ref, uv_ref, cg_ref, cv_ref, cgn_ref, cvn_ref, da_ref, dan_ref, wg_ref, wv_ref,
             dug_ref, duv_ref, dwg_ref, dwv_ref):
        i = pl.program_id(1)

        @pl.when(i == 0)
        def _():
            dwg_ref[...] = jnp.zeros_like(dwg_ref)
            dwv_ref[...] = jnp.zeros_like(dwv_ref)

        gate = jnp.concatenate([cg_ref[...], cgn_ref[...]], axis=0).astype(F32)
        val = jnp.concatenate([cv_ref[...], cvn_ref[...]], axis=0).astype(F32)
        dae = jnp.concatenate([da_ref[...], dan_ref[...]], axis=0).astype(F32)
        rows_e = lax.broadcasted_iota(jnp.int32, dae.shape, 0)
        dae = jnp.where(jnp.logical_and(i == nt - 1, rows_e >= tm), 0.0, dae)
        sg = _sigmoid(gate)
        n = tm + BF16_ROWS

        def back(d, u_ref, w_ref, du_ref, dw_ref):
            w = w_ref[...]
            uv = u_ref[...]
            d1 = pltpu.roll(d, n - 1, axis=0)[:tm]
            d2 = pltpu.roll(d, n - 2, axis=0)[:tm]
            d0 = d[:tm]
            du_ref[...] = (d0 * w[2:3] + d1 * w[1:2] + d2 * w[0:1]).astype(BF16)
            rows = [jnp.sum(t * uv, axis=0, keepdims=True) for t in (d2, d1, d0)]
            dw_ref[...] += jnp.concatenate(rows + [jnp.zeros((SUBLANES - 3, FFN_TC), F32)], axis=0)

        back(dae * val * sg * (1.0 + gate * (1.0 - sg)), ug_ref, wg_ref, dug_ref, dwg_ref)
        back(dae * gate * sg, uv_ref, wv_ref, duv_ref, dwv_ref)

    per = tm // BF16_ROWS
    last_blk = nt * per - 1
    blk = lambda off: pl.BlockSpec((tm, FFN_TC), lambda j, i: (i, j + off))
    nxt = pl.BlockSpec((BF16_ROWS, FFN_TC), lambda j, i: (jnp.minimum((i + 1) * per, last_blk), j))
    wblk = lambda off: pl.BlockSpec((SUBLANES, FFN_TC), lambda j, i: (0, j + off))
    dug, duv, dwg, dwv = pl.pallas_call(
        body, name="ffn_act_bwd", grid=(nc, nt),
        in_specs=[blk(0), blk(nc), blk(0), blk(0), nxt, nxt, blk(0), nxt, wblk(0), wblk(nc)],
        out_specs=[blk(0), blk(0), wblk(0), wblk(0)],
        out_shape=[jax.ShapeDtypeStruct((T, D_FF), BF16)] * 2 + [jax.ShapeDtypeStruct((SUBLANES, D_FF), F32)] * 2,
        compiler_params=_params(("parallel", "arbitrary")),
    )(u, u, cg, cv, cg, cv, da, da, w8, w8)
    return dug, duv, jnp.concatenate([dwg, dwv], axis=1)


def _out_merge_bwd(dmix, w_out_t, ya, yb, proj):
    T = ya.shape[0]
    tm = min(TOKEN_TILE, T)

    def body(dmix_ref, w_ref, ya_ref, yb_ref, ga_ref, gb_ref, dya_ref, dyb_ref, dga_ref, dgb_ref):
        dm = jnp.dot(dmix_ref[...], w_ref[...], preferred_element_type=F32)
        sa = _sigmoid(ga_ref[...])
        sb = _sigmoid(gb_ref[...])
        dya_ref[...] = (dm * sa).astype(BF16)
        dyb_ref[...] = (dm * sb).astype(BF16)
        dga_ref[...] = (dm * ya_ref[...].astype(F32) * sa * (1.0 - sa)).astype(BF16)
        dgb_ref[...] = (dm * yb_ref[...].astype(F32) * sb * (1.0 - sb)).astype(BF16)

    row = pl.BlockSpec((tm, D), lambda i: (i, 0))
    return pl.pallas_call(
        body, name="out_merge_bwd", grid=(T // tm,),
        in_specs=[row, _full((D, D)), row, row, pl.BlockSpec((tm, D), lambda i: (i, COL_GA // D)),
                  pl.BlockSpec((tm, D), lambda i: (i, COL_GB // D))],
        out_specs=[row] * 4, out_shape=[jax.ShapeDtypeStruct((T, D), BF16)] * 4,
        compiler_params=_params(("parallel",)),
    )(dmix, w_out_t, ya, yb, proj, proj)


def _conv_branch_bwd(proj, dya0, conv_w8):
    T = proj.shape[0]
    tm = min(FFN_TM, T)
    nt = T // tm

    def body(cb_ref, cc_ref, cv_ref, cbn_ref, ccp_ref, cvp_ref, ccn_ref, cvn_ref, d_ref, dn_ref, w_ref,
             d3_ref, dw_ref):
        i = pl.program_id(0)

        @pl.when(i == 0)
        def _():
            dw_ref[...] = jnp.zeros_like(dw_ref)

        first, last = i == 0, i == nt - 1
        w = w_ref[...]
        cc = jnp.concatenate([ccp_ref[...], cc_ref[...], ccn_ref[...]], axis=0)
        cv = jnp.concatenate([cvp_ref[...], cv_ref[...], cvn_ref[...]], axis=0)
        rows = lax.broadcasted_iota(jnp.int32, cc.shape, 0)
        z = jnp.where(jnp.logical_and(first, rows < SUBLANES), 0.0, cc * cv)
        z1 = pltpu.roll(z, 1, axis=0)
        z2 = pltpu.roll(z, 2, axis=0)
        cz = z2 * w[0:1] + z1 * w[1:2] + z * w[2:3]
        zeros8 = jnp.zeros((SUBLANES, CONV_W), F32)
        de = jnp.concatenate([zeros8, d_ref[...], jnp.where(last, 0.0, dn_ref[...])], axis=0)
        cbe = jnp.concatenate([zeros8, cb_ref[...], cbn_ref[...]], axis=0)
        dcz = de * cbe
        n = tm + 2 * SUBLANES
        dz = dcz * w[2:3] + pltpu.roll(dcz, n - 1, axis=0) * w[1:2] + pltpu.roll(dcz, n - 2, axis=0) * w[0:1]
        inner = slice(SUBLANES, SUBLANES + tm)
        d3_ref[:, 0:CONV_W] = (de * cz)[inner].astype(BF16)
        d3_ref[:, CONV_W:2 * CONV_W] = (dz * cv)[inner].astype(BF16)
        d3_ref[:, 2 * CONV_W:3 * CONV_W] = (dz * cc)[inner].astype(BF16)
        wrows = [jnp.sum((dcz * t)[inner], axis=0, keepdims=True) for t in (z2, z1, z)]
        dw_ref[...] += jnp.concatenate(wrows + [jnp.zeros((SUBLANES - 3, CONV_W), F32)], axis=0)

    blk = lambda col: pl.BlockSpec((tm, CONV_W), lambda i: (i, col))
    out_blk = pl.BlockSpec((tm, CONV_W), lambda i: (i, 0))
    return pl.pallas_call(
        body, name="conv_branch_bwd", grid=(nt,),
        in_specs=[blk(0), blk(1), blk(2), _next_spec(tm, CONV_W, 0, nt),
                  _prev_spec(tm, CONV_W, 1), _prev_spec(tm, CONV_W, 2),
                  _next_spec(tm, CONV_W, 1, nt), _next_spec(tm, CONV_W, 2, nt),
                  out_blk, _next_spec(tm, CONV_W, 0, nt), _full((SUBLANES, CONV_W))],
        out_specs=[pl.BlockSpec((tm, 3 * CONV_W), lambda i: (i, 0)), _full((SUBLANES, CONV_W))],
        out_shape=[jax.ShapeDtypeStruct((T, 3 * CONV_W), BF16), jax.ShapeDtypeStruct((SUBLANES, CONV_W), F32)],
        compiler_params=_params(("arbitrary",)),
    )(proj, proj, proj, proj, proj, proj, proj, proj, dya0, dya0, conv_w8)


def _qk_norm_bwd(proj, dqs, dkh, dlogf, qg, kg, bf_pad, gmat):
    T = proj.shape[0]
    tm = min(TOKEN_TILE, T)

    def body(q_ref, k_ref, f_ref, dqs_ref, dkh_ref, dlf_ref, qg_ref, kg_ref, bf_ref, g_ref,
             dqk_ref, dfl_ref, dqg_ref, dkg_ref, dbf_ref):
        @pl.when(pl.program_id(0) == 0)
        def _():
            dqg_ref[...] = jnp.zeros_like(dqg_ref)
            dkg_ref[...] = jnp.zeros_like(dkg_ref)
            dbf_ref[...] = jnp.zeros_like(dbf_ref)

        gm = g_ref[...]
        for src, d_src, gain, scale, dst, dgain in (
                (q_ref, dqs_ref, qg_ref, 1.0 / np.sqrt(HEAD_DIM), dqk_ref.at[:, 0:ATTN_W], dqg_ref),
                (k_ref, dkh_ref, kg_ref, 1.0, dqk_ref.at[:, ATTN_W:2 * ATTN_W], dkg_ref)):
            v = src[...]
            dhat = d_src[...].astype(F32) * scale
            inv = lax.rsqrt(_split_dot(v * v, gm, 2) * (1.0 / HEAD_DIM) + RMS_EPS)
            vn = v * inv
            dgain[...] += _rows8(dhat * vn)
            dvn = dhat * gain[...]
            mean = _split_dot(dvn * vn, gm, 2) * (1.0 / HEAD_DIM)
            dst[...] = (inv * (dvn - vn * mean)).astype(BF16)
        fx = f_ref[...] + bf_ref[...]
        dfl = dlf_ref[...] * _sigmoid(-fx)
        dfl_ref[...] = dfl.astype(BF16)
        dbf_ref[...] += _rows8(dfl)

    blk = lambda col: pl.BlockSpec((tm, ATTN_W), lambda i: (i, col))
    out_blk = pl.BlockSpec((tm, ATTN_W), lambda i: (i, 0))
    f_in = pl.BlockSpec((tm, F_PAD), lambda i: (i, COL_F // F_PAD))
    f_blk = pl.BlockSpec((tm, F_PAD), lambda i: (i, 0))
    return pl.pallas_call(
        body, name="qk_norm_bwd", grid=(T // tm,),
        in_specs=[blk(3), blk(4), f_in, out_blk, out_blk, f_blk, _full((1, ATTN_W)), _full((1, ATTN_W)),
                  _full((1, F_PAD)), _full((ATTN_W, ATTN_W))],
        out_specs=[pl.BlockSpec((tm, 2 * ATTN_W), lambda i: (i, 0)), f_blk, _full((SUBLANES, ATTN_W)),
                   _full((SUBLANES, ATTN_W)), _full((SUBLANES, F_PAD))],
        out_shape=[jax.ShapeDtypeStruct((T, 2 * ATTN_W), BF16), jax.ShapeDtypeStruct((T, F_PAD), BF16)]
        + [jax.ShapeDtypeStruct((SUBLANES, ATTN_W), F32)] * 2 + [jax.ShapeDtypeStruct((SUBLANES, F_PAD), F32)],
        compiler_params=_params(("arbitrary",)),
    )(proj, proj, proj, dqs, dkh, dlogf, qg, kg, bf_pad, gmat)


def _pad_rows8(w):
    return jnp.pad(w, ((0, SUBLANES - w.shape[0]), (0, 0)))


def _fold8(acc):
    return jnp.sum(acc, axis=0, keepdims=True)


def _late_weights(mats):
    out = {}
    for name in ("w_branch_a", "w_out", "w_up", "w_down"):
        out[name] = mats[name]
        out[name + "_t"] = mats[name].T
    out["w_branch_b_heads"] = _pad_head_rows(mats["w_branch_b"])
    out["w_branch_b_heads_t"] = out["w_branch_b_heads"].T
    return out


def _local_step(x, target, mod, wts, late=None):
    T = x.shape[0]
    tb = min(MATMUL_TILE, T)
    tk_long = min(2 * MATMUL_TILE, T)
    tm = min(TOKEN_TILE, T)
    sh1, sc1, g1, sh2, sc2, g2 = [mod[:, i * D:(i + 1) * D] for i in range(N_MOD)]
    w_in, w_in_t = wts["w_in"], wts["w_in_t"]
    conv_a8 = _pad_rows8(wts["conv_a_w"])
    conv_f8 = _pad_rows8(wts["conv_ffn_w"])
    qg = jnp.tile(wts["q_norm_g"], (1, N_HEADS))
    kg = jnp.tile(wts["k_norm_g"], (1, N_HEADS))
    bf_pad = jnp.pad(wts["b_f"], ((0, 0), (0, F_PAD - N_HEADS)))
    gmat = _group_matrix()

    h = _norm_mod(x, wts["norm1_g"], sc1, sh1, name="norm1_fwd")
    proj = _matmul(h, w_in, name="mm_in", tm=tb, tn=896, tk=D)
    fcum = _cumsum(proj, reverse=False, name="gate_cumsum", col=COL_F // F_PAD, gate_bias=bf_pad)
    ya0, qa, ka, va = _branch_prep(proj, fcum, conv_a8, qg, kg, gmat)
    if late is None:
        o_h, qb = _attn_fwd(qa, ka, va)
    else:
        o_h, qb, *gathered = _attn_fwd(qa, ka, va, _Exchange([late[name] for name, *_ in LATE], scatter=False))
        wts = dict(wts)
        mats = {name: _join_shards(g, axis) for (name, _, _, axis), g in zip(LATE, gathered) if name != "w_up"}
        mats["w_up"] = _assemble_columns(gathered[[name for name, *_ in LATE].index("w_up")], 2 * D_FF // N_DEV,
                                         2 * D_FF, ((0, 2 * D_FF, 0),), name="assemble_w_up")
        wts.update(_late_weights(mats))
    ya, yb, merged = _branch_merge_fwd(ya0, o_h, proj, wts["w_branch_a"], wts["w_branch_b_heads"])
    mix, x1, h2 = _out_resid_norm(x, merged, wts["w_out"], g1, wts["norm2_g"], sc2, sh2)
    u = _matmul(h2, wts["w_up"], name="mm_up", tm=tb, tn=1408, tk=D)
    act, conv_gate, conv_val = _ffn_act_fwd(u, conv_f8)
    dy, dff, sq8, dg2_8 = _down_loss_head(x1, act, wts["w_down"], g2, target)
    sq = jnp.sum(sq8).reshape(1, 1)

    grads = {}
    da = _matmul(dff, wts["w_down_t"], name="mm_down_dx", tm=tb, tn=1408, tk=D, out_dtype=BF16)
    grads["w_down"] = _matmul(act, dff, name="mm_down_dw", tm=1408, tn=D, tk=tk_long, trans_a=True)
    dug, duv, dconv_f8 = _ffn_act_bwd(u, conv_gate, conv_val, da, conv_f8)
    grads["conv_ffn_w"] = dconv_f8[:3]
    dx1, dsh2_8, dsc2_8, dn2_8, dmix, dg1_8 = _matmul_pieces(
        [dug, duv], wts["w_up_t"], name="mm_up_dx", tm=tm, norm_bwd=(x1, dy, wts["norm2_g"], sc2, mix, g1))
    dw_up = [_matmul(h2, d, name="mm_up_dw_" + half, tm=D, tn=1408, tk=tk_long, trans_a=True)
             for half, d in (("gate", dug), ("val", duv))]
    if late is None:
        grads["w_up"] = jnp.concatenate(dw_up, axis=1)
    grads["norm2_g"] = _fold8(dn2_8)

    grads["w_out"] = _matmul(merged, dmix, name="mm_out_dw", tm=D, tn=D, tk=tk_long, trans_a=True)
    dya, dyb, dga, dgb = _out_merge_bwd(dmix, wts["w_out_t"], ya, yb, proj)
    dya0 = _matmul(dya, wts["w_branch_a_t"], name="mm_branch_a_dx", tm=tb, tn=CONV_W, tk=D)
    grads["w_branch_a"] = _matmul(ya0, dya, name="mm_branch_a_dw", tm=CONV_W, tn=D, tk=tk_long, trans_a=True)
    doa = _branch_b_bwd(dyb, o_h, wts["w_branch_b_heads_t"])
    grads["w_branch_b"] = _branch_b_dw(o_h, dyb)[:, :HEAD_DIM].reshape(ATTN_W, D)
    dconv3, dconv_a8 = _conv_branch_bwd(proj, dya0, conv_a8)
    grads["conv_a_w"] = dconv_a8[:3]

    parts = {}
    if late is None:
        dq_h, dk_h, dv_h = _attn_bwd(qb, ka, va, doa)
    else:
        ready = [(_column_shards(dw_up) if name == "w_up" else _split_shards(grads[name], axis)).astype(BF16)
                 for name, _, _, axis in LATE]
        dq_h, dk_h, dv_h, *recv = _attn_bwd(
            qb, ka, va, doa, _Exchange(ready + [_pack_full_by_dest(grads, CONVS, SUBLANES)], scatter=True))
        parts = dict(zip([name for name, *_ in LATE] + ["conv"], recv))
    dq_tok, dk_tok, dv_tok, dfcum = _attn_unpack(dq_h, dk_h, dv_h)
    dlogf = _cumsum(dfcum, reverse=True, name="gate_cumsum_bwd")
    dqk, dfl, dqg8, dkg8, dbf8 = _qk_norm_bwd(proj, dq_tok, dk_tok, dlogf, qg, kg, bf_pad, gmat)
    grads["q_norm_g"] = jnp.sum(_fold8(dqg8).reshape(N_HEADS, HEAD_DIM), axis=0, keepdims=True)
    grads["k_norm_g"] = jnp.sum(_fold8(dkg8).reshape(N_HEADS, HEAD_DIM), axis=0, keepdims=True)
    grads["b_f"] = _fold8(dbf8)[:, :N_HEADS]
    narrow, wide = [dconv3, dqk, dv_tok], [dga, dgb, dfl]
    dw_narrow = _matmul_tn_pieces(h, narrow, name="mm_in_dw_narrow", tk=tb)
    dwa, dwb, dwf = _matmul_tn_pieces(h, wide, name="mm_in_dw_wide", tk=tk_long)
    dw_in = list(dw_narrow) + [dwf[:, :N_HEADS], dwa, dwb]
    norm1 = (x, dx1, wts["norm1_g"], sc1)
    if late is None:
        grads["w_in"] = jnp.concatenate(dw_in, axis=1)
        grad_x, dsh1_8, dsc1_8, dn1_8 = _matmul_pieces(narrow + wide, w_in_t, name="mm_in_dx", tm=tm,
                                                       norm_bwd=norm1)
    else:
        grad_x, dsh1_8, dsc1_8, dn1_8, parts["w_in"] = _matmul_pieces(
            narrow + wide, w_in_t, name="mm_in_dx", tm=tm, norm_bwd=norm1,
            exchange=_Exchange([_column_shards(dw_in).astype(BF16)], scatter=True))
    grads["norm1_g"] = _fold8(dn1_8)
    grads["mod"] = jnp.concatenate([_fold8(a) for a in (dsh1_8, dsc1_8, dg1_8, dsh2_8, dsc2_8, dg2_8)], axis=1)
    return sq, grad_x, grads, parts


def _me_and_peers():
    mx, my, mc = lax.axis_index("x"), lax.axis_index("y"), lax.axis_index("c")
    me = 4 * mx + 2 * my + mc
    peers = []
    for k in range(1, N_DEV):
        px = 1 - mx if k & 4 else mx
        py = 1 - my if k & 2 else my
        pc = 1 - mc if k & 1 else mc
        peers.append(((px, py, pc), 4 * px + 2 * py + pc))
    return me, peers


HBM_SPEC = pl.BlockSpec(memory_space=pltpu.HBM)


class _Exchange:
    def __init__(self, xs, scatter):
        self.xs, self.scatter, self.n = list(xs), scatter, len(xs)
        self.out_shapes = [jax.ShapeDtypeStruct(x.shape if scatter else (N_DEV,) + x.shape, x.dtype) for x in xs]
        self.in_specs = [HBM_SPEC] * self.n
        self.out_specs = [HBM_SPEC] * self.n
        self.scratch = [pltpu.SemaphoreType.DMA((self.n, N_DEV - 1)), pltpu.SemaphoreType.DMA((self.n, N_DEV - 1)),
                        pltpu.SemaphoreType.DMA((self.n,))]

    def _copies(self, x_refs, out_refs, sems):
        send_sems, recv_sems, local_sems = sems
        me, peers = _me_and_peers()

        def src(a, idx):
            return x_refs[a].at[idx] if self.scatter else x_refs[a]

        def copy(a, k, from_idx, to_slot, device):
            return pltpu.make_async_remote_copy(
                src_ref=src(a, from_idx), dst_ref=out_refs[a].at[to_slot], send_sem=send_sems.at[a, k],
                recv_sem=recv_sems.at[a, k], device_id=device, device_id_type=MESH)

        local = [pltpu.make_async_copy(src(a, me), out_refs[a].at[me], local_sems.at[a]) for a in range(self.n)]
        sends = [copy(a, k, idx, me, dev) for a in range(self.n) for k, (dev, idx) in enumerate(peers)]
        recvs = [copy(a, k, idx, idx, dev) for a in range(self.n) for k, (dev, idx) in enumerate(peers)]
        return local, sends, recvs

    def start(self, x_refs, out_refs, sems):
        local, sends, _ = self._copies(x_refs, out_refs, sems)
        for cp in local + sends:
            cp.start()

    def wait(self, x_refs, out_refs, sems):
        local, sends, recvs = self._copies(x_refs, out_refs, sems)
        for cp in recvs:
            cp.wait_recv()
        for cp in sends:
            cp.wait_send()
        for cp in local:
            cp.wait()

    def split(self, refs, n_in, n_out):
        n = self.n
        ins, xin = refs[:n_in], refs[n_in:n_in + n]
        outs, xout = refs[n_in + n:n_in + n + n_out], refs[n_in + n + n_out:n_in + 2 * n + n_out]
        rest = refs[n_in + 2 * n + n_out:]
        return ins, outs, rest[:len(rest) - 3], (xin, xout, rest[len(rest) - 3:])


def _ride(exchange, first, last, refs):
    if exchange is None:
        return

    @pl.when(first)
    def _():
        exchange.start(*refs)

    @pl.when(last)
    def _():
        exchange.wait(*refs)


def _gather_two_level(xs, *, name):
    n = len(xs)
    out_shapes = [jax.ShapeDtypeStruct((N_DEV,) + x.shape, x.dtype) for x in xs]

    def body(*refs):
        x_refs, out_refs = refs[:n], refs[n:2 * n]
        send_sems, recv_sems, local_sems = refs[2 * n:]
        x, y, c = lax.axis_index("x"), lax.axis_index("y"), lax.axis_index("c")
        me, sibling = (x, y, c), (x, y, 1 - c)
        chips = [(1 - x, y), (x, 1 - y), (1 - x, 1 - y)]

        def slot(a, dev):
            return out_refs[a].at[4 * dev[0] + 2 * dev[1] + dev[2]]

        def copy(a, k, block, to, src=None):
            return pltpu.make_async_remote_copy(
                src_ref=slot(a, block) if src is None else src, dst_ref=slot(a, block),
                send_sem=send_sems.at[a, k], recv_sem=recv_sems.at[a, k], device_id=to, device_id_type=MESH)

        mine = [pltpu.make_async_copy(x_refs[a], slot(a, me), local_sems.at[a]) for a in range(n)]
        first = [copy(a, 0, me, sibling, src=x_refs[a]) for a in range(n)]
        first += [copy(a, 1 + j, me, (*chip, c), src=x_refs[a]) for a in range(n) for j, chip in enumerate(chips)]
        for cp in mine + first:
            cp.start()
        passed = []
        for a in range(n):
            for j, chip in enumerate(chips):
                copy(a, 1 + j, (*chip, c), me).wait_recv()
                passed.append(copy(a, 4 + j, (*chip, c), sibling))
                passed[-1].start()
        for a in range(n):
            copy(a, 0, sibling, me).wait_recv()
            for j, chip in enumerate(chips):
                copy(a, 4 + j, (*chip, 1 - c), me).wait_recv()
        for cp in first + passed:
            cp.wait_send()
        for cp in mine:
            cp.wait()

    return pl.pallas_call(
        body, name=name, in_specs=[HBM_SPEC] * n, out_specs=[HBM_SPEC] * n, out_shape=out_shapes,
        scratch_shapes=[pltpu.SemaphoreType.DMA((n, N_DEV - 1)), pltpu.SemaphoreType.DMA((n, N_DEV - 1)),
                        pltpu.SemaphoreType.DMA((n,))],
        compiler_params=pltpu.CompilerParams(has_side_effects=True),
    )(*xs)


def _exchange(xs, *, name, scatter):
    ex = _Exchange(xs, scatter)

    def body(*refs):
        _, _, _, xrefs = ex.split(refs, 0, 0)
        ex.start(*xrefs)
        ex.wait(*xrefs)

    return pl.pallas_call(
        body, name=name, in_specs=ex.in_specs, out_specs=ex.out_specs, out_shape=ex.out_shapes,
        scratch_shapes=ex.scratch, compiler_params=pltpu.CompilerParams(has_side_effects=True),
    )(*xs)


def _ada_fwd(c_all, w_shard, b_shard):
    n = w_shard.shape[1]

    def body(c_ref, w_ref, b_ref, o_ref):
        cv = c_ref[...]
        act = (cv * _sigmoid(cv)).astype(BF16)
        o_ref[...] = jnp.dot(act, w_ref[...].astype(BF16), preferred_element_type=F32) + b_ref[...]

    return pl.pallas_call(
        body, name="ada_fwd", in_specs=[_full((N_DEV, D)), _full((D, n)), _full((1, n))],
        out_specs=_full((N_DEV, n)), out_shape=jax.ShapeDtypeStruct((N_DEV, n), F32), grid=(1,),
        compiler_params=_params(("arbitrary",)),
    )(c_all, w_shard, b_shard)


def _ada_bwd(c_all_t, dmod_pad):
    n = dmod_pad.shape[1]

    def body(c_ref, d_ref, o_ref):
        cv = c_ref[...]
        act = (cv * _sigmoid(cv)).astype(BF16)
        o_ref[...] = jnp.dot(act, d_ref[...].astype(BF16), preferred_element_type=F32)

    return pl.pallas_call(
        body, name="ada_bwd", in_specs=[_full((D, LANES)), _full((LANES, n))],
        out_specs=_full((D, n)), out_shape=jax.ShapeDtypeStruct((D, n), F32), grid=(1,),
        compiler_params=_params(("arbitrary",)),
    )(c_all_t, dmod_pad)


ADAM_ROWS = 256


def _adamw(parts, w, m, v, *, name):
    n, R, C = parts.shape
    tr = next((t for t in (ADAM_ROWS, 128, 64, 32, 16, SUBLANES) if R % t == 0), R)

    def body(p_ref, w_ref, m_ref, v_ref, g_ref, d_ref, nm_ref, nv_ref):
        g = p_ref[0].astype(F32)
        for j in range(1, n):
            g = g + p_ref[j].astype(F32)
        g_ref[...] = g
        nm = ADAM_B1 * m_ref[...] + (1.0 - ADAM_B1) * g
        nv = ADAM_B2 * v_ref[...] + (1.0 - ADAM_B2) * (g * g)
        nm_ref[...] = nm
        nv_ref[...] = nv
        m_hat = nm / (1.0 - ADAM_B1 ** ADAM_STEP)
        v_hat = nv / (1.0 - ADAM_B2 ** ADAM_STEP)
        d_ref[...] = -ADAM_LR * (m_hat / (jnp.sqrt(v_hat) + ADAM_EPS) + ADAM_WD * w_ref[...])

    row = pl.BlockSpec((tr, C), lambda i: (i, 0))
    return pl.pallas_call(
        body, name=name, grid=(R // tr,),
        in_specs=[pl.BlockSpec((n, tr, C), lambda i: (0, i, 0)), row, row, row], out_specs=[row] * 4,
        out_shape=[jax.ShapeDtypeStruct((R, C), F32)] * 4,
        compiler_params=_params(("parallel",)),
    )(parts, w, m, v)


SHARDED = (("w_in", D, IN_W, 1), ("w_branch_a", CONV_W, D, 1), ("w_branch_b", ATTN_W, D, 1), ("w_out", D, D, 0),
           ("w_up", D, 2 * D_FF, 1), ("w_down", D_FF, D, 0), ("conv_a_w", 3, CONV_W, 1),
           ("conv_ffn_w", 3, 2 * D_FF, 1))
MATRICES = SHARDED[:6]
LATE = MATRICES[1:]
CONVS = SHARDED[6:]
REPLICATED = (("b_ada", N_MOD * D), ("norm1_g", D), ("norm2_g", D), ("b_f", N_HEADS), ("q_norm_g", HEAD_DIM),
              ("k_norm_g", HEAD_DIM))


def _shard_shape(rows, cols, axis):
    return (rows // N_DEV, cols) if axis == 0 else (rows, cols // N_DEV)


def _pack_rows(flat, multiple):
    length = flat.shape[-1]
    rows = -(-length // PACK_W)
    rows = -(-rows // multiple) * multiple
    pad = [(0, 0)] * (flat.ndim - 1) + [(0, rows * PACK_W - length)]
    return jnp.pad(flat, pad).reshape(flat.shape[:-1] + (rows, PACK_W))


def _pack_shards(shards, spec, multiple, dtype):
    flat = jnp.concatenate([shards[name].reshape(-1).astype(dtype) for name, *_ in spec])
    return _pack_rows(flat, multiple)


def _join_lane_blocks(gathered):
    n, r, c = gathered.shape

    def body(g_ref, o_ref):
        for j in range(n):
            o_ref[:, j * c:(j + 1) * c] = g_ref[j]

    return pl.pallas_call(
        body, name="join_lane_blocks", grid=(1,), in_specs=[_full((n, r, c))], out_specs=_full((r, n * c)),
        out_shape=jax.ShapeDtypeStruct((r, n * c), gathered.dtype), compiler_params=_params(("arbitrary",)),
    )(gathered)


SHARD_PAD = 768


def _assemble_columns(gathered, shard_cols, out_cols, segments, *, name):
    n, rows, padw = gathered.shape
    assert n == N_DEV and padw == SHARD_PAD and shard_cols <= SHARD_PAD

    def body(g_ref, o_ref):
        j = pl.program_id(0)

        @pl.when(j == 0)
        def _():
            o_ref[...] = jnp.zeros_like(o_ref)

        for dev in range(N_DEV):
            @pl.when(j == dev)
            def _(dev=dev):
                x = g_ref[0]
                for lo, hi, delta in segments:
                    a, b = max(lo, dev * shard_cols), min(hi, (dev + 1) * shard_cols)
                    if a >= b:
                        continue
                    base = (a + delta) // LANES * LANES
                    width = -(-(b + delta - base) // LANES) * LANES
                    src = lax.broadcasted_iota(jnp.int32, (padw, width), 0) + dev * shard_cols
                    dst = lax.broadcasted_iota(jnp.int32, (padw, width), 1) + (base - delta)
                    place = jnp.where((src == dst) & (src >= a) & (src < b), 1.0, 0.0).astype(BF16)
                    moved = jnp.dot(x, place, preferred_element_type=F32).astype(BF16)
                    o_ref[:, base:base + width] = o_ref[:, base:base + width] + moved

    return pl.pallas_call(
        body, name=name, grid=(N_DEV,), in_specs=[pl.BlockSpec((1, rows, padw), lambda j: (j, 0, 0))],
        out_specs=_full((rows, out_cols)), out_shape=jax.ShapeDtypeStruct((rows, out_cols), BF16),
        compiler_params=_params(("arbitrary",)),
    )(gathered)


def _pad_shard(w):
    return jnp.pad(w.astype(BF16), ((0, 0), (0, SHARD_PAD - w.shape[1])))


W_IN_SEGMENTS = ((0, COL_GA, 0), (COL_GA, COL_GA + N_HEADS, COL_F - COL_GA), (COL_GA + N_HEADS, IN_W, -N_HEADS))


def _join_shards(gathered, axis):
    if axis == 0:
        return gathered.reshape(N_DEV * gathered.shape[1], gathered.shape[2])
    if gathered.shape[2] == LANES:
        return _join_lane_blocks(gathered)
    return jnp.concatenate([gathered[j] for j in range(N_DEV)], axis=1)


def _column_shards(pieces):
    total = sum(p.shape[1] for p in pieces)
    width = total // N_DEV
    shards = []
    for j in range(N_DEV):
        lo, hi, off, segs = j * width, (j + 1) * width, 0, []
        for p in pieces:
            a, b = max(lo, off), min(hi, off + p.shape[1])
            if a < b:
                segs.append(p[:, a - off:b - off])
            off += p.shape[1]
        shards.append(segs[0] if len(segs) == 1 else jnp.concatenate(segs, axis=1))
    return jnp.stack(shards)


def _split_shards(full, axis):
    if axis == 0:
        return full.reshape(N_DEV, full.shape[0] // N_DEV, full.shape[1])
    c = full.shape[1] // N_DEV
    return jnp.stack([full[:, j * c:(j + 1) * c] for j in range(N_DEV)])


def _unpack_shards(packed, spec):
    flat = packed.reshape(-1)
    out, off = {}, 0
    for name, rows, cols, axis in spec:
        r, c = _shard_shape(rows, cols, axis)
        out[name] = flat[off:off + r * c].reshape(r, c)
        off += r * c
    return out


def _unpack_gathered(gathered, spec):
    flat = gathered.reshape(N_DEV, -1)
    out, off = {}, 0
    for name, rows, cols, axis in spec:
        r, c = _shard_shape(rows, cols, axis)
        seg = flat[:, off:off + r * c].reshape(N_DEV, r, c)
        out[name] = seg.reshape(rows, cols) if axis == 0 else seg.transpose(1, 0, 2).reshape(rows, cols)
        off += r * c
    return out


def _pack_full_by_dest(full, spec, multiple):
    segs = []
    for name, rows, cols, axis in spec:
        r, c = _shard_shape(rows, cols, axis)
        a = full[name]
        seg = a.reshape(N_DEV, r, c) if axis == 0 else a.reshape(rows, N_DEV, c).transpose(1, 0, 2)
        segs.append(seg.reshape(N_DEV, r * c))
    return _pack_rows(jnp.concatenate(segs, axis=1), multiple)


def _pad_head_rows(w):
    n = w.shape[1]
    padded = jnp.pad(w.reshape(N_HEADS, HEAD_DIM, n), ((0, 0), (0, LANES - HEAD_DIM), (0, 0)))
    return padded.reshape(N_HEADS * LANES, n)


def kernel(x, c, w_ada, b_ada, norm1_g, w_in, b_f, conv_a_w, q_norm_g, k_norm_g, w_branch_a, w_branch_b, w_out, norm2_g, w_up, conv_ffn_w, w_down, loss_target, m_w_ada, m_b_ada, m_norm1_g, m_w_in, m_b_f, m_conv_a_w, m_q_norm_g, m_k_norm_g, m_w_branch_a, m_w_branch_b, m_w_out, m_norm2_g, m_w_up, m_conv_ffn_w, m_w_down, v_w_ada, v_b_ada, v_norm1_g, v_w_in, v_b_f, v_conv_a_w, v_q_norm_g, v_k_norm_g, v_w_branch_a, v_w_branch_b, v_w_out, v_norm2_g, v_w_up, v_conv_ffn_w, v_w_down):
    names = ("w_ada", "b_ada", "norm1_g", "w_in", "b_f", "conv_a_w", "q_norm_g", "k_norm_g", "w_branch_a",
             "w_branch_b", "w_out", "norm2_g", "w_up", "conv_ffn_w", "w_down")
    squeeze = lambda a: a[0] if a.ndim == 3 else a
    W = dict(zip(names, map(squeeze, (w_ada, b_ada, norm1_g, w_in, b_f, conv_a_w, q_norm_g, k_norm_g, w_branch_a,
                                      w_branch_b, w_out, norm2_g, w_up, conv_ffn_w, w_down))))
    M = dict(zip(names, map(squeeze, (m_w_ada, m_b_ada, m_norm1_g, m_w_in, m_b_f, m_conv_a_w, m_q_norm_g,
                                      m_k_norm_g, m_w_branch_a, m_w_branch_b, m_w_out, m_norm2_g, m_w_up,
                                      m_conv_ffn_w, m_w_down))))
    V = dict(zip(names, map(squeeze, (v_w_ada, v_b_ada, v_norm1_g, v_w_in, v_b_f, v_conv_a_w, v_q_norm_g,
                                      v_k_norm_g, v_w_branch_a, v_w_branch_b, v_w_out, v_norm2_g, v_w_up,
                                      v_conv_ffn_w, v_w_down))))
    me = 4 * lax.axis_index("x") + 2 * lax.axis_index("y") + lax.axis_index("c")
    ada_n = N_MOD * D // N_DEV

    small = jnp.concatenate([c.reshape(-1), W["conv_a_w"].reshape(-1), W["conv_ffn_w"].reshape(-1)])
    small_all, w_in_all = _gather_two_level([_pack_rows(small, SUBLANES), _pad_shard(W["w_in"])],
                                            name="gather_first")
    small_all = small_all.reshape(N_DEV, -1)
    c_all = small_all[:, :D]
    conv_all = _unpack_gathered(small_all[:, D:], CONVS)

    b_shard = lax.dynamic_slice(W["b_ada"], (0, me * ada_n), (1, ada_n))
    mod_part = _ada_fwd(c_all, W["w_ada"], b_shard)
    mod_all, = _exchange([mod_part], name="gather_mod", scatter=False)
    mod = lax.dynamic_index_in_dim(mod_all, me, axis=1, keepdims=False).reshape(1, N_MOD * D)

    wts = {"w_in": _assemble_columns(w_in_all, IN_W // N_DEV, IN_W_PAD, W_IN_SEGMENTS, name="assemble_w_in")}
    wts["w_in_t"] = wts["w_in"].T
    wts.update(conv_all)
    for name in ("norm1_g", "norm2_g", "q_norm_g", "k_norm_g", "b_f"):
        wts[name] = W[name]
    late = {name: _pad_shard(W[name]) if name == "w_up" else W[name].astype(BF16) for name, *_ in LATE}

    sq, grad_x, grads, parts = _local_step(x[0], loss_target[0], mod, wts, late)
    loss = lax.psum(sq[0, 0] * (0.5 / D), AXES)

    grads["b_ada"] = grads["mod"]
    rep_flat = lambda src: jnp.concatenate([src[name].reshape(-1) for name, _ in REPLICATED])
    rep_parts, = _exchange([_pack_rows(rep_flat(grads), 16)], name="gather_small_grads", scatter=False)
    rep_out = _adamw(rep_parts, *[_pack_rows(rep_flat(s), 16) for s in (W, M, V)], name="adamw_replicated")

    dmod_all = rep_parts.reshape(N_DEV, -1)[:, :N_MOD * D]
    dmod_mine = lax.dynamic_slice(dmod_all, (0, me * ada_n), (N_DEV, ada_n))
    g_ada = _ada_bwd(jnp.pad(c_all.T, ((0, 0), (0, LANES - N_DEV))),
                     jnp.pad(dmod_mine, ((0, LANES - N_DEV), (0, 0))))
    ada_out = _adamw(g_ada[None], W["w_ada"], M["w_ada"], V["w_ada"], name="adamw_ada")

    mat_out = {name: _adamw(parts[name], W[name], M[name], V[name], name="adamw_" + name) for name, *_ in MATRICES}
    conv_out = _adamw(parts["conv"], *[_pack_shards(s, CONVS, SUBLANES, F32) for s in (W, M, V)],
                      name="adamw_conv")

    results = []
    for kind in range(4):
        per = {"w_ada": ada_out[kind]}
        per.update({name: out[kind] for name, out in mat_out.items()})
        per.update(_unpack_shards(conv_out[kind], CONVS))
        flat, off = rep_out[kind].reshape(-1), 0
        for name, n in REPLICATED:
            per[name] = flat[off:off + n].reshape(1, n)
            off += n
        results.append(per)
    restore = lambda name, a: a[None] if W[name].ndim == 2 and name not in dict(REPLICATED) else a
    outs = [loss, grad_x[None]]
    for per in results:
        outs.extend(restore(name, per[name]) for name in names)
    return tuple(outs)
```

```python
import functools

import jax
import jax.numpy as jnp
import numpy as np
from jax import lax
from jax.experimental import pallas as pl
from jax.experimental.pallas import tpu as pltpu

F32 = jnp.float32
BF16 = jnp.bfloat16

N_DEV = 8
D = 1024
N_HEADS = 8
HEAD_DIM = 64
ATTN_W = 512
CONV_W = 512
D_FF = 2816
N_MOD = 6
IN_W = 5128
RMS_EPS = 1e-6
NEG_INF = -1e30

IN_W_PAD = 5376
COL_GA = 3072
COL_GB = 4096
COL_F = 5120
F_PAD = 128

ADAM_LR = 0.001
ADAM_B1 = 0.9
ADAM_B2 = 0.999
ADAM_EPS = 1e-08
ADAM_WD = 0.01
ADAM_STEP = 10

LANES = 128
SUBLANES = 8
BF16_ROWS = 16
VMEM_LIMIT = 52 * 1024 * 1024
TOKEN_TILE = 512
MATMUL_TILE = 1024
ATTN_BLOCK = 512
PACK_W = 1024

MESH = pl.DeviceIdType.MESH
AXES = ("x", "y", "c")


def _params(sem=None, **kw):
    return pltpu.CompilerParams(dimension_semantics=sem, vmem_limit_bytes=VMEM_LIMIT, **kw)


def _full(shape):
    nd = len(shape)
    return pl.BlockSpec(shape, lambda *_: (0,) * nd)


def _tn_dot(a, b):
    return lax.dot_general(a, b, (((0,), (0,)), ((), ())), preferred_element_type=F32)


def _matmul(a, b, *, name, tm, tn, tk, out_dtype=F32, trans_a=False, exchange=None):
    if trans_a:
        K, M = a.shape
    else:
        M, K = a.shape
    N = b.shape[1]
    assert b.shape[0] == K and M % tm == 0 and N % tn == 0 and K % tk == 0, (name, a.shape, b.shape)
    nm, nn, nk = M // tm, N // tn, K // tk

    def body(*refs):
        if exchange is None:
            a_ref, b_ref, o_ref, *own = refs
        else:
            (a_ref, b_ref), (o_ref,), own, xrefs = exchange.split(refs, 2, 1)
            ids = [pl.program_id(d) for d in range(3)]
            first = jnp.logical_and(jnp.logical_and(ids[0] == 0, ids[1] == 0), ids[2] == 0)
            last = jnp.logical_and(jnp.logical_and(ids[0] == nn - 1, ids[1] == nm - 1), ids[2] == nk - 1)
            _ride(exchange, first, last, xrefs)
        k = pl.program_id(2)
        av = a_ref[...].astype(BF16)
        bv = b_ref[...].astype(BF16)
        prod = _tn_dot(av, bv) if trans_a else jnp.dot(av, bv, preferred_element_type=F32)
        if nk == 1:
            o_ref[...] = prod.astype(out_dtype)
            return
        acc_ref, = own

        @pl.when(k == 0)
        def _():
            acc_ref[...] = prod

        @pl.when(k > 0)
        def _():
            acc_ref[...] += prod

        @pl.when(k == nk - 1)
        def _():
            o_ref[...] = acc_ref[...].astype(out_dtype)

    if trans_a:
        a_spec = pl.BlockSpec((tk, tm), lambda j, i, k: (k, i))
    else:
        a_spec = pl.BlockSpec((tm, tk), lambda j, i, k: (i, k))
    in_specs = [a_spec, pl.BlockSpec((tk, tn), lambda j, i, k: (k, j))]
    out_spec = pl.BlockSpec((tm, tn), lambda j, i, k: (i, j))
    out_shape = jax.ShapeDtypeStruct((M, N), out_dtype)
    scratch = [pltpu.VMEM((tm, tn), F32)] if nk > 1 else []
    if exchange is None:
        return pl.pallas_call(
            body, name=name, grid=(nn, nm, nk), in_specs=in_specs, out_specs=out_spec, out_shape=out_shape,
            scratch_shapes=scratch, compiler_params=_params(("parallel", "parallel", "arbitrary")),
        )(a, b)
    return pl.pallas_call(
        body, name=name, grid=(nn, nm, nk), in_specs=in_specs + exchange.in_specs,
        out_specs=[out_spec] + exchange.out_specs, out_shape=[out_shape] + exchange.out_shapes,
        scratch_shapes=scratch + exchange.scratch, compiler_params=_params(("arbitrary",) * 3),
    )(a, b, *exchange.xs)


def _norm_bwd_tile(dh, ins, outs, first):
    x_ref, dr_ref, g_ref, sc_ref = ins[:4]
    dx_ref, dsh_ref, dsc_ref, dg_ref = outs[:4]

    @pl.when(first)
    def _():
        for ref in outs[1:4] + outs[5:]:
            ref[...] = jnp.zeros_like(ref)

    xv = x_ref[...]
    gv = g_ref[...]
    one_sc = 1.0 + sc_ref[...]
    inv = lax.rsqrt(jnp.mean(xv * xv, axis=-1, keepdims=True) + RMS_EPS)
    xn = xv * inv
    dxn = dh * (gv * one_sc)
    dx = dr_ref[...] + inv * (dxn - xn * jnp.mean(dxn * xn, axis=-1, keepdims=True))
    dx_ref[...] = dx
    dhxn = dh * xn
    dsh_ref[...] += _rows8(dh)
    dsc_ref[...] += _rows8(dhxn * gv)
    dg_ref[...] += _rows8(dhxn * one_sc)
    if len(ins) == 6:
        mix_ref, g1_ref = ins[4:]
        dmix_ref, dg1_ref = outs[4:]
        dmix_ref[...] = (dx * g1_ref[...]).astype(BF16)
        dg1_ref[...] += _rows8(dx * mix_ref[...])


def _matmul_pieces(pieces, b, *, name, tm, exchange=None, norm_bwd=None):
    M = pieces[0].shape[0]
    widths = [p.shape[1] for p in pieces]
    offsets = [sum(widths[:i]) for i in range(len(widths))]
    N = b.shape[1]
    assert b.shape[0] >= sum(widths) and M % tm == 0, (name, widths, b.shape)
    n_p, nm = len(pieces), M // tm
    extra = list(norm_bwd) if norm_bwd is not None else []
    n_in = n_p + 1 + len(extra)
    n_out = len(extra) if norm_bwd is not None else 1

    def body(*refs):
        i = pl.program_id(0)
        if exchange is None:
            ins, outs = refs[:n_in], refs[n_in:]
        else:
            ins, outs, _, xrefs = exchange.split(refs, n_in, n_out)
            _ride(exchange, i == 0, i == nm - 1, xrefs)
        b_ref = ins[n_p]
        acc = None
        for a_ref, off, w in zip(ins[:n_p], offsets, widths):
            term = jnp.dot(a_ref[...].astype(BF16), b_ref[off:off + w, :], preferred_element_type=F32)
            acc = term if acc is None else acc + term
        if norm_bwd is None:
            outs[0][...] = acc
        else:
            _norm_bwd_tile(acc, tuple(ins[n_p + 1:]), tuple(outs), first=i == 0)

    row, vec, part = pl.BlockSpec((tm, N), lambda i: (i, 0)), _full((1, N)), _full((SUBLANES, N))
    part_shape = jax.ShapeDtypeStruct((SUBLANES, N), F32)
    in_specs = [pl.BlockSpec((tm, w), lambda i: (i, 0)) for w in widths] + [_full(b.shape)]
    out_specs, out_shape = [row], [jax.ShapeDtypeStruct((M, N), F32)]
    if norm_bwd is not None:
        in_specs += [row, row, vec, vec] + ([row, vec] if len(extra) == 6 else [])
        out_specs += [part] * 3 + ([row, part] if len(extra) == 6 else [])
        out_shape += [part_shape] * 3 + ([jax.ShapeDtypeStruct((M, N), BF16), part_shape] if len(extra) == 6 else [])
    sequential = exchange is not None or norm_bwd is not None
    xs = exchange.xs if exchange is not None else []
    result = pl.pallas_call(
        body, name=name, grid=(nm,), in_specs=in_specs + (exchange.in_specs if exchange else []),
        out_specs=out_specs + (exchange.out_specs if exchange else []),
        out_shape=out_shape + (exchange.out_shapes if exchange else []),
        scratch_shapes=exchange.scratch if exchange else [],
        compiler_params=_params(("arbitrary" if sequential else "parallel",)),
    )(*pieces, b, *extra, *xs)
    return result[0] if len(result) == 1 else result


def _matmul_tn_pieces(a, pieces, *, name, tk):
    K, M = a.shape
    widths = [p.shape[1] for p in pieces]
    n_p, nk = len(pieces), K // tk

    def body(*refs):
        a_ref, p_refs, o_refs = refs[0], refs[1:n_p + 1], refs[n_p + 1:]
        k = pl.program_id(0)
        av = a_ref[...].astype(BF16)
        for p_ref, o_ref in zip(p_refs, o_refs):
            prod = _tn_dot(av, p_ref[...].astype(BF16))

            @pl.when(k == 0)
            def _():
                o_ref[...] = prod

            @pl.when(k > 0)
            def _():
                o_ref[...] += prod

    return pl.pallas_call(
        body, name=name, grid=(nk,),
        in_specs=[pl.BlockSpec((tk, M), lambda k: (k, 0))] + [pl.BlockSpec((tk, w), lambda k: (k, 0)) for w in widths],
        out_specs=[_full((M, w)) for w in widths], out_shape=[jax.ShapeDtypeStruct((M, w), F32) for w in widths],
        compiler_params=_params(("arbitrary",)),
    )(a, *pieces)


def _split_dot(x, mat, parts):
    out = None
    rem = x
    for p in range(parts):
        piece = rem.astype(BF16)
        term = jnp.dot(piece, mat, preferred_element_type=F32)
        out = term if out is None else out + term
        if p + 1 < parts:
            rem = rem - piece.astype(F32)
    return out


def _sigmoid(x):
    return 0.5 * jnp.tanh(0.5 * x) + 0.5


def _rows8(x):
    r, c = x.shape
    return jnp.sum(x.reshape(r // SUBLANES, SUBLANES, c), axis=0)


def _shift_down(blk, prev8, n):
    rolled = pltpu.roll(blk, n, axis=0)
    prev_rolled = pltpu.roll(prev8, n, axis=0)
    rows = lax.broadcasted_iota(jnp.int32, prev8.shape, 0)
    first = jnp.where(rows < n, prev_rolled, rolled[0:SUBLANES])
    return jnp.concatenate([first, rolled[SUBLANES:]], axis=0)


def _prev_spec(tm, width, col):
    per = tm // SUBLANES
    return pl.BlockSpec((SUBLANES, width), lambda i, *_: (jnp.maximum(i * per - 1, 0), col))


def _next_spec(tm, width, col, n_tiles):
    per = tm // SUBLANES
    last = n_tiles * per - 1
    return pl.BlockSpec((SUBLANES, width), lambda i, *_: (jnp.minimum((i + 1) * per, last), col))


def _group_matrix():
    idx = np.arange(ATTN_W) // HEAD_DIM
    return jnp.asarray((idx[:, None] == idx[None, :]).astype(np.float32), BF16)


def _norm_mod(x, g, sc, sh, *, name):
    T = x.shape[0]
    tm = min(TOKEN_TILE, T)

    def body(x_ref, g_ref, sc_ref, sh_ref, o_ref):
        xv = x_ref[...]
        inv = lax.rsqrt(jnp.mean(xv * xv, axis=-1, keepdims=True) + RMS_EPS)
        o_ref[...] = ((xv * inv) * g_ref[...] * (1.0 + sc_ref[...]) + sh_ref[...]).astype(BF16)

    row = pl.BlockSpec((tm, D), lambda i: (i, 0))
    return pl.pallas_call(
        body, name=name, grid=(T // tm,),
        in_specs=[row, _full((1, D)), _full((1, D)), _full((1, D))],
        out_specs=row, out_shape=jax.ShapeDtypeStruct((T, D), BF16),
        compiler_params=_params(("parallel",)),
    )(x, g, sc, sh)


LANE_ONE = 64
LANE_F = 67
LANE_LSE = 70
LANE_SUM = 73


def _pieces(x):
    hi = x.astype(BF16).astype(F32)
    rest = x - hi
    mid = rest.astype(BF16).astype(F32)
    return hi, mid, rest - mid


def _run(start, vals):
    return [(start + i, v) for i, v in enumerate(vals)]


def _head_lanes(a, h):
    blk = a[:, LANES * (h // 2):LANES * (h // 2) + LANES]
    return blk if h % 2 == 0 else pltpu.roll(blk, HEAD_DIM, axis=1)


def _branch_prep(proj, fcum, conv_w8, qg, kg, gmat):
    T = proj.shape[0]
    tm = min(TOKEN_TILE, T)
    nt = T // tm

    def body(cb_ref, cc_ref, cv_ref, q_ref, k_ref, v_ref, f_ref, ccp_ref, cvp_ref, w_ref, qg_ref, kg_ref, g_ref,
             ya_ref, qa_ref, ka_ref, va_ref):
        i = pl.program_id(0)
        z = cc_ref[...] * cv_ref[...]
        zp = jnp.where(i > 0, ccp_ref[...] * cvp_ref[...], 0.0)
        w = w_ref[...]
        cz = _shift_down(z, zp, 2) * w[0:1] + _shift_down(z, zp, 1) * w[1:2] + z * w[2:3]
        ya_ref[...] = (cb_ref[...] * cz).astype(BF16)
        gm = g_ref[...]

        def normed(src, gain, scale):
            v = src[...]
            ms = _split_dot(v * v, gm, 2) * (1.0 / HEAD_DIM)
            return (v * lax.rsqrt(ms + RMS_EPS)) * gain[...] * scale

        qn = normed(q_ref, qg_ref, 1.0 / np.sqrt(HEAD_DIM))
        kn = normed(k_ref, kg_ref, 1.0)
        vv = v_ref[...]
        lane = lax.broadcasted_iota(jnp.int32, (tm, LANES), 1)
        low = lane < HEAD_DIM
        in_run = lambda start: jnp.logical_and(lane >= start, lane < start + 3)
        q_ones = jnp.where(jnp.logical_or(in_run(LANE_ONE), lane == LANE_SUM), 1.0, 0.0)
        k_ones = jnp.where(jnp.logical_or(in_run(LANE_F), in_run(LANE_LSE)), 1.0, 0.0)
        v_ones = jnp.where(in_run(LANE_ONE), 1.0, 0.0)
        f3 = jnp.concatenate(_pieces(f_ref[...]), axis=1).astype(BF16)
        src = lax.broadcasted_iota(jnp.int32, (3 * LANES, LANES), 0)
        dst = lax.broadcasted_iota(jnp.int32, (3 * LANES, LANES), 1)
        for h in range(N_HEADS):
            def pick(start, h=h):
                hit = jnp.logical_and(src - h == (dst - start) * LANES, jnp.logical_and(dst >= start, dst < start + 3))
                return jnp.dot(f3, jnp.where(hit, 1.0, 0.0).astype(BF16), preferred_element_type=F32)

            qa_ref[h] = jnp.where(low, _head_lanes(qn, h), q_ones + pick(LANE_F)).astype(BF16)
            ka_ref[h] = jnp.where(low, _head_lanes(kn, h), k_ones - pick(LANE_ONE)).astype(BF16)
            va_ref[h] = jnp.where(low, _head_lanes(vv, h), v_ones).astype(BF16)

    blk = lambda col: pl.BlockSpec((tm, CONV_W), lambda i: (i, col))
    heads = pl.BlockSpec((N_HEADS, tm, LANES), lambda i: (0, i, 0))
    return pl.pallas_call(
        body, name="branch_prep", grid=(nt,),
        in_specs=[blk(0), blk(1), blk(2), blk(3), blk(4), blk(5), pl.BlockSpec((tm, F_PAD), lambda i: (i, 0)),
                  _prev_spec(tm, CONV_W, 1), _prev_spec(tm, CONV_W, 2),
                  _full((SUBLANES, CONV_W)), _full((1, ATTN_W)), _full((1, ATTN_W)), _full((ATTN_W, ATTN_W))],
        out_specs=[pl.BlockSpec((tm, CONV_W), lambda i: (i, 0)), heads, heads, heads],
        out_shape=[jax.ShapeDtypeStruct((T, CONV_W), BF16)] + [jax.ShapeDtypeStruct((N_HEADS, T, LANES), BF16)] * 3,
        compiler_params=_params(("parallel",)),
    )(proj, proj, proj, proj, proj, proj, fcum, proj, proj, conv_w8, qg, kg, gmat)


def _cumsum(x, *, reverse, name, col=0, gate_bias=None):
    T = x.shape[0]
    tm = min(TOKEN_TILE, T)
    nt = T // tm

    def body(x_ref, b_ref, o_ref, carry_ref):
        i = pl.program_id(0)

        @pl.when(i == 0)
        def _():
            carry_ref[...] = jnp.zeros_like(carry_ref)

        r = lax.broadcasted_iota(jnp.int32, (tm, tm), 0)
        c = lax.broadcasted_iota(jnp.int32, (tm, tm), 1)
        tri = jnp.where((c >= r) if reverse else (c <= r), 1.0, 0.0).astype(BF16)
        xv = x_ref[...]
        if gate_bias is not None:
            fx = xv + b_ref[...]
            xv = jnp.minimum(fx, 0.0) - jnp.log(1.0 + jnp.exp(-jnp.abs(fx)))
        out = _split_dot_left(tri, xv, 3) + carry_ref[0:1]
        o_ref[...] = out
        carry_ref[...] = jnp.broadcast_to(out[0:1] if reverse else out[tm - 1:tm], carry_ref.shape)

    rows = (lambda i: nt - 1 - i) if reverse else (lambda i: i)
    bias = jnp.zeros((1, F_PAD), F32) if gate_bias is None else gate_bias
    return pl.pallas_call(
        body, name=name, grid=(nt,),
        in_specs=[pl.BlockSpec((tm, F_PAD), lambda i: (rows(i), col)), _full((1, F_PAD))],
        out_specs=pl.BlockSpec((tm, F_PAD), lambda i: (rows(i), 0)),
        out_shape=jax.ShapeDtypeStruct((T, F_PAD), F32),
        scratch_shapes=[pltpu.VMEM((SUBLANES, F_PAD), F32)],
        compiler_params=_params(("arbitrary",)),
    )(x, bias)


def _split_dot_left(mat, x, parts):
    out = None
    rem = x
    for p in range(parts):
        piece = rem.astype(BF16)
        term = jnp.dot(mat, piece, preferred_element_type=F32)
        out = term if out is None else out + term
        if p + 1 < parts:
            rem = rem - piece.astype(F32)
    return out


def _out_resid_norm(x, merged, w_out, g1, g, sc, sh):
    T = x.shape[0]
    tm = min(TOKEN_TILE, T)

    def body(x_ref, m_ref, w_ref, g1_ref, g_ref, sc_ref, sh_ref, mix_ref, x1_ref, h_ref):
        mix = jnp.dot(m_ref[...], w_ref[...], preferred_element_type=F32)
        mix_ref[...] = mix
        x1 = x_ref[...] + g1_ref[...] * mix
        x1_ref[...] = x1
        inv = lax.rsqrt(jnp.mean(x1 * x1, axis=-1, keepdims=True) + RMS_EPS)
        h_ref[...] = ((x1 * inv) * g_ref[...] * (1.0 + sc_ref[...]) + sh_ref[...]).astype(BF16)

    row = pl.BlockSpec((tm, D), lambda i: (i, 0))
    vec = _full((1, D))
    return pl.pallas_call(
        body, name="out_resid_norm", grid=(T // tm,),
        in_specs=[row, row, _full((D, D)), vec, vec, vec, vec], out_specs=[row, row, row],
        out_shape=[jax.ShapeDtypeStruct((T, D), F32), jax.ShapeDtypeStruct((T, D), F32),
                   jax.ShapeDtypeStruct((T, D), BF16)],
        compiler_params=_params(("parallel",)),
    )(x, merged, w_out, g1, g, sc, sh)


FFN_TM = 256
FFN_TC = 1408


def _ffn_act_fwd(u, w8):
    T = u.shape[0]
    tm = min(FFN_TM, T)
    nt = T // tm
    nc = D_FF // FFN_TC

    def body(ug_ref, uv_ref, ugp_ref, uvp_ref, wg_ref, wv_ref, o_ref, cg_ref, cv_ref):
        i = pl.program_id(1)

        def conv(u_ref, p_ref, w_ref):
            uv = u_ref[...]
            up = jnp.where(i > 0, p_ref[...], 0.0)
            w = w_ref[...]
            return _shift_down(uv, up, 2) * w[0:1] + _shift_down(uv, up, 1) * w[1:2] + uv * w[2:3]

        gate = conv(ug_ref, ugp_ref, wg_ref)
        val = conv(uv_ref, uvp_ref, wv_ref)
        cg_ref[...] = gate.astype(BF16)
        cv_ref[...] = val.astype(BF16)
        o_ref[...] = (gate * _sigmoid(gate) * val).astype(BF16)

    per = tm // SUBLANES
    blk = lambda off: pl.BlockSpec((tm, FFN_TC), lambda j, i: (i, j + off))
    prev = lambda off: pl.BlockSpec((SUBLANES, FFN_TC), lambda j, i: (jnp.maximum(i * per - 1, 0), j + off))
    wblk = lambda off: pl.BlockSpec((SUBLANES, FFN_TC), lambda j, i: (0, j + off))
    return pl.pallas_call(
        body, name="ffn_act_fwd", grid=(nc, nt),
        in_specs=[blk(0), blk(nc), prev(0), prev(nc), wblk(0), wblk(nc)],
        out_specs=[blk(0), blk(0), blk(0)],
        out_shape=[jax.ShapeDtypeStruct((T, D_FF), BF16)] * 3,
        compiler_params=_params(("parallel", "parallel")),
    )(u, u, u, u, w8, w8)


def _down_loss_head(x1, act, w_down, g2, target):
    T = x1.shape[0]
    tm = min(TOKEN_TILE, T)

    def body(x1_ref, a_ref, w_ref, g2_ref, t_ref, dy_ref, dff_ref, loss_ref, dg2_ref):
        i = pl.program_id(0)

        @pl.when(i == 0)
        def _():
            loss_ref[...] = jnp.zeros_like(loss_ref)
            dg2_ref[...] = jnp.zeros_like(dg2_ref)

        ff = jnp.dot(a_ref[...], w_ref[...], preferred_element_type=F32)
        err = x1_ref[...] + g2_ref[...] * ff - t_ref[...]
        dy = err * (1.0 / D)
        dy_ref[...] = dy
        dff_ref[...] = (dy * g2_ref[...]).astype(BF16)
        loss_ref[...] += _rows8(err * err)
        dg2_ref[...] += _rows8(dy * ff)

    row = pl.BlockSpec((tm, D), lambda i: (i, 0))
    acc = _full((SUBLANES, D))
    return pl.pallas_call(
        body, name="down_loss_head", grid=(T // tm,),
        in_specs=[row, pl.BlockSpec((tm, D_FF), lambda i: (i, 0)), _full((D_FF, D)), _full((1, D)), row],
        out_specs=[row, row, acc, acc],
        out_shape=[jax.ShapeDtypeStruct((T, D), F32), jax.ShapeDtypeStruct((T, D), BF16),
                   jax.ShapeDtypeStruct((SUBLANES, D), F32), jax.ShapeDtypeStruct((SUBLANES, D), F32)],
        compiler_params=_params(("arbitrary",)),
    )(x1, act, w_down, g2, target)


def _nt_dot(a, b):
    return lax.dot_general(a, b, (((1,), (1,)), ((), ())), preferred_element_type=F32)


def _causal(n, keys_on_rows=False):
    r = lax.broadcasted_iota(jnp.int32, (n, n), 0)
    c = lax.broadcasted_iota(jnp.int32, (n, n), 1)
    return (c >= r) if keys_on_rows else (c <= r)


def _sweep(lo, hi, step, carry, group=2):
    while group >= 1:
        def several(j, cr, lo=lo, group=group):
            for g in range(group):
                cr = step(lo + group * j + g, cr)
            return cr

        passes = (hi - lo) // group
        carry = lax.fori_loop(0, passes, several, carry)
        lo = lo + group * passes
        group //= 2
    return carry


def _grid_ends(n0, n1):
    i0, i1 = pl.program_id(0), pl.program_id(1)
    return jnp.logical_and(i0 == 0, i1 == 0), jnp.logical_and(i0 == n0 - 1, i1 == n1 - 1)


def _attn_fwd(qa, ka, va, exchange=None):
    nh, T, _ = qa.shape
    bq = min(ATTN_BLOCK, T)
    nq = T // bq

    def body(*refs):
        if exchange is None:
            q_ref, k_ref, v_ref, o_ref, qb_ref = refs
        else:
            (q_ref, k_ref, v_ref), (o_ref, qb_ref), _, xrefs = exchange.split(refs, 3, 2)
            _ride(exchange, *_grid_ends(nh, nq), xrefs)
        qi = pl.program_id(1)
        q = q_ref[0]

        def step(kb, carry, masked=False):
            m, acc = carry
            start = pl.multiple_of(kb * bq, bq)
            s = _nt_dot(q, k_ref[0, pl.ds(start, bq), :])
            if masked:
                s = jnp.where(_causal(bq), s, NEG_INF)
            m_new = jnp.maximum(m, jnp.max(s, axis=-1, keepdims=True))
            p = jnp.exp(s - m_new).astype(BF16)
            acc = jnp.exp(m - m_new) * acc + jnp.dot(p, v_ref[0, pl.ds(start, bq), :], preferred_element_type=F32)
            return m_new, acc

        init = (jnp.full((bq, 1), NEG_INF, F32), jnp.zeros((bq, LANES), F32))
        m, acc = step(qi, _sweep(0, qi, step, init, group=4), masked=True)
        l = acc[:, LANE_ONE:LANE_ONE + 1]
        o_ref[0] = acc / l
        lane = lax.broadcasted_iota(jnp.int32, (bq, LANES), 1)
        qf = q.astype(F32)
        for idx, piece in _run(LANE_LSE, _pieces(m + jnp.log(l))):
            qf = jnp.where(lane == idx, -piece, qf)
        qb_ref[0] = qf.astype(BF16)

    tile = pl.BlockSpec((1, bq, LANES), lambda h, i: (h, i, 0))
    whole = pl.BlockSpec((1, T, LANES), lambda h, i: (h, 0, 0))
    out_shape = [jax.ShapeDtypeStruct((nh, T, LANES), F32), jax.ShapeDtypeStruct((nh, T, LANES), BF16)]
    if exchange is None:
        return pl.pallas_call(
            body, name="attn_fwd", grid=(nh, nq), in_specs=[tile, whole, whole], out_specs=[tile, tile],
            out_shape=out_shape, compiler_params=_params(("parallel", "parallel")),
        )(qa, ka, va)
    return pl.pallas_call(
        body, name="attn_fwd", grid=(nh, nq), in_specs=[tile, whole, whole] + exchange.in_specs,
        out_specs=[tile, tile] + exchange.out_specs, out_shape=out_shape + exchange.out_shapes,
        scratch_shapes=exchange.scratch, compiler_params=_params(("arbitrary", "arbitrary")),
    )(qa, ka, va, *exchange.xs)


def _branch_merge_fwd(ya0, o_h, proj, wba, wbb_heads):
    nh, T, _ = o_h.shape
    tm = min(TOKEN_TILE, T)

    def body(ya0_ref, o_ref, ga_ref, gb_ref, wa_ref, wb_ref, ya_ref, yb_ref, m_ref):
        ya = jnp.dot(ya0_ref[...], wa_ref[...], preferred_element_type=F32)
        yb = jnp.dot(o_ref[0].astype(BF16), wb_ref[0:LANES, :], preferred_element_type=F32)
        for h in range(1, nh):
            yb += jnp.dot(o_ref[h].astype(BF16), wb_ref[h * LANES:(h + 1) * LANES, :], preferred_element_type=F32)
        ya_ref[...] = ya.astype(BF16)
        yb_ref[...] = yb.astype(BF16)
        m_ref[...] = (_sigmoid(ga_ref[...]) * ya + _sigmoid(gb_ref[...]) * yb).astype(BF16)

    row = pl.BlockSpec((tm, D), lambda i: (i, 0))
    return pl.pallas_call(
        body, name="branch_merge_fwd", grid=(T // tm,),
        in_specs=[pl.BlockSpec((tm, CONV_W), lambda i: (i, 0)), pl.BlockSpec((nh, tm, LANES), lambda i: (0, i, 0)),
                  pl.BlockSpec((tm, D), lambda i: (i, COL_GA // D)), pl.BlockSpec((tm, D), lambda i: (i, COL_GB // D)),
                  _full((CONV_W, D)), _full((nh * LANES, D))],
        out_specs=[row, row, row],
        out_shape=[jax.ShapeDtypeStruct((T, D), BF16)] * 3,
        compiler_params=_params(("parallel",)),
    )(ya0, o_h, proj, proj, wba, wbb_heads)


def _branch_b_bwd(dyb, o_h, wbb_heads_t):
    nh, T, _ = o_h.shape
    tm = min(TOKEN_TILE, T)

    def body(dyb_ref, o_ref, w_ref, out_ref):
        do = jnp.dot(dyb_ref[...], w_ref[...], preferred_element_type=F32)
        lane = lax.broadcasted_iota(jnp.int32, (tm, LANES), 1)
        for h in range(nh):
            g = do[:, h * LANES:(h + 1) * LANES].astype(BF16).astype(F32)
            delta = jnp.sum(g * o_ref[h], axis=-1, keepdims=True)
            for idx, piece in _run(LANE_ONE, _pieces(delta)):
                g = jnp.where(lane == idx, -piece, g)
            out_ref[h] = g.astype(BF16)

    heads = pl.BlockSpec((nh, tm, LANES), lambda i: (0, i, 0))
    return pl.pallas_call(
        body, name="branch_b_bwd", grid=(T // tm,),
        in_specs=[pl.BlockSpec((tm, D), lambda i: (i, 0)), heads, _full((D, nh * LANES))],
        out_specs=heads, out_shape=jax.ShapeDtypeStruct((nh, T, LANES), BF16),
        compiler_params=_params(("parallel",)),
    )(dyb, o_h, wbb_heads_t)


def _branch_b_dw(o_h, dyb):
    nh, T, _ = o_h.shape
    tk = min(TOKEN_TILE, T)

    def body(o_ref, dyb_ref, out_ref):
        @pl.when(pl.program_id(0) == 0)
        def _():
            out_ref[...] = jnp.zeros_like(out_ref)

        g = dyb_ref[...]
        for h in range(nh):
            out_ref[h] += _tn_dot(o_ref[h].astype(BF16), g)

    return pl.pallas_call(
        body, name="branch_b_dw", grid=(T // tk,),
        in_specs=[pl.BlockSpec((nh, tk, LANES), lambda k: (0, k, 0)), pl.BlockSpec((tk, D), lambda k: (k, 0))],
        out_specs=_full((nh, LANES, D)), out_shape=jax.ShapeDtypeStruct((nh, LANES, D), F32),
        compiler_params=_params(("arbitrary",)),
    )(o_h, dyb)


def _attn_bwd(qb, ka, va, doa, exchange=None):
    nh, T, _ = qb.shape
    bk = min(ATTN_BLOCK, T)
    nk = T // bk

    def body(*refs):
        if exchange is None:
            q_ref, do_ref, k_ref, v_ref, dq_ref, dk_ref, dv_ref = refs
        else:
            (q_ref, do_ref, k_ref, v_ref), (dq_ref, dk_ref, dv_ref), _, xrefs = exchange.split(refs, 4, 3)
            _ride(exchange, *_grid_ends(nh, nk), xrefs)
        ki = pl.program_id(1)

        @pl.when(ki == 0)
        def _():
            dq_ref[...] = jnp.zeros_like(dq_ref)

        k = k_ref[0]
        v = v_ref[0]

        def step(qi, carry, masked):
            dk, dv = carry
            rows = pl.ds(pl.multiple_of(qi * bk, bk), bk)
            q = q_ref[0, rows, :]
            g = do_ref[0, rows, :]
            pt = jnp.exp(_nt_dot(k, q))
            if masked:
                pt = jnp.where(_causal(bk, keys_on_rows=True), pt, 0.0)
            dv = dv + jnp.dot(pt.astype(BF16), g, preferred_element_type=F32)
            dst = (pt * _nt_dot(v, g)).astype(BF16)
            dk = dk + jnp.dot(dst, q, preferred_element_type=F32)
            dq_ref[0, rows, :] += _tn_dot(dst, k)
            return dk, dv

        init = (jnp.zeros((bk, LANES), F32), jnp.zeros((bk, LANES), F32))
        carry = step(ki, init, True)
        dk_ref[0], dv_ref[0] = _sweep(ki + 1, nk, lambda qi, cr: step(qi, cr, False), carry)

    tile = pl.BlockSpec((1, bk, LANES), lambda h, i: (h, i, 0))
    whole = pl.BlockSpec((1, T, LANES), lambda h, i: (h, 0, 0))
    out_shape = [jax.ShapeDtypeStruct((nh, T, LANES), F32)] * 3
    if exchange is None:
        return pl.pallas_call(
            body, name="attn_bwd", grid=(nh, nk), in_specs=[whole, whole, tile, tile],
            out_specs=[whole, tile, tile], out_shape=out_shape, compiler_params=_params(("parallel", "arbitrary")),
        )(qb, doa, ka, va)
    return pl.pallas_call(
        body, name="attn_bwd", grid=(nh, nk), in_specs=[whole, whole, tile, tile] + exchange.in_specs,
        out_specs=[whole, tile, tile] + exchange.out_specs, out_shape=out_shape + exchange.out_shapes,
        scratch_shapes=exchange.scratch, compiler_params=_params(("arbitrary", "arbitrary")),
    )(qb, doa, ka, va, *exchange.xs)


def _attn_unpack(dq_h, dk_h, dv_h):
    nh, T, _ = dq_h.shape
    tm = min(TOKEN_TILE, T)

    def body(dq_ref, dk_ref, dv_ref, q_out, k_out, v_out, f_out):
        lane = lax.broadcasted_iota(jnp.int32, (tm, LANES), 1)
        low = lane < HEAD_DIM
        for src, dst in ((dq_ref, q_out), (dk_ref, k_out), (dv_ref, v_out)):
            for pair in range(nh // 2):
                both = jnp.where(low, src[2 * pair], pltpu.roll(src[2 * pair + 1], HEAD_DIM, axis=1))
                dst[:, LANES * pair:LANES * (pair + 1)] = both.astype(dst.dtype)
        df = jnp.zeros((tm, LANES), F32)
        for h in range(nh):
            col = dq_ref[h][:, LANE_F:LANE_F + 1] - dk_ref[h][:, LANE_SUM:LANE_SUM + 1]
            df = jnp.where(lane == h, col, df)
        f_out[...] = df

    heads = pl.BlockSpec((nh, tm, LANES), lambda i: (0, i, 0))
    tok = pl.BlockSpec((tm, ATTN_W), lambda i: (i, 0))
    return pl.pallas_call(
        body, name="attn_unpack", grid=(T // tm,), in_specs=[heads, heads, heads],
        out_specs=[tok, tok, tok, pl.BlockSpec((tm, F_PAD), lambda i: (i, 0))],
        out_shape=[jax.ShapeDtypeStruct((T, ATTN_W), BF16)] * 3 + [jax.ShapeDtypeStruct((T, F_PAD), F32)],
        compiler_params=_params(("parallel",)),
    )(dq_h, dk_h, dv_h)


def _ffn_act_bwd(u, cg, cv, da, w8):
    T = u.shape[0]
    tm = min(FFN_TM, T)
    nt = T // tm
    nc = D_FF // FFN_TC

    def body(ug_ref, uv_ref, cg_ref, cv_ref, cgn_ref, cvn_ref, da_ref, dan_ref, wg_ref, wv_ref,
             dug_ref, duv_ref, dwg_ref, dwv_ref):
        i = pl.program_id(1)

        @pl.when(i == 0)
        def _():
            dwg_ref[...] = jnp.zeros_like(dwg_ref)
            dwv_ref[...] = jnp.zeros_like(dwv_ref)

        gate = jnp.concatenate([cg_ref[...], cgn_ref[...]], axis=0).astype(F32)
        val = jnp.concatenate([cv_ref[...], cvn_ref[...]], axis=0).astype(F32)
        dae = jnp.concatenate([da_ref[...], dan_ref[...]], axis=0).astype(F32)
        rows_e = lax.broadcasted_iota(jnp.int32, dae.shape, 0)
        dae = jnp.where(jnp.logical_and(i == nt - 1, rows_e >= tm), 0.0, dae)
        sg = _sigmoid(gate)
        n = tm + BF16_ROWS

        def back(d, u_ref, w_ref, du_ref, dw_ref):
            w = w_ref[...]
            uv = u_ref[...]
            d1 = pltpu.roll(d, n - 1, axis=0)[:tm]
            d2 = pltpu.roll(d, n - 2, axis=0)[:tm]
            d0 = d[:tm]
            du_ref[...] = (d0 * w[2:3] + d1 * w[1:2] + d2 * w[0:1]).astype(BF16)
            rows = [jnp.sum(t * uv, axis=0, keepdims=True) for t in (d2, d1, d0)]
            dw_ref[...] += jnp.concatenate(rows + [jnp.zeros((SUBLANES - 3, FFN_TC), F32)], axis=0)

        back(dae * val * sg * (1.0 + gate * (1.0 - sg)), ug_ref, wg_ref, dug_ref, dwg_ref)
        back(dae * gate * sg, uv_ref, wv_ref, duv_ref, dwv_ref)

    per = tm // BF16_ROWS
    last_blk = nt * per - 1
    blk = lambda off: pl.BlockSpec((tm, FFN_TC), lambda j, i: (i, j + off))
    nxt = pl.BlockSpec((BF16_ROWS, FFN_TC), lambda j, i: (jnp.minimum((i + 1) * per, last_blk), j))
    wblk = lambda off: pl.BlockSpec((SUBLANES, FFN_TC), lambda j, i: (0, j + off))
    dug, duv, dwg, dwv = pl.pallas_call(
        body, name="ffn_act_bwd", grid=(nc, nt),
        in_specs=[blk(0), blk(nc), blk(0), blk(0), nxt, nxt, blk(0), nxt, wblk(0), wblk(nc)],
        out_specs=[blk(0), blk(0), wblk(0), wblk(0)],
        out_shape=[jax.ShapeDtypeStruct((T, D_FF), BF16)] * 2 + [jax.ShapeDtypeStruct((SUBLANES, D_FF), F32)] * 2,
        compiler_params=_params(("parallel", "arbitrary")),
    )(u, u, cg, cv, cg, cv, da, da, w8, w8)
    return dug, duv, jnp.concatenate([dwg, dwv], axis=1)


def _out_merge_bwd(dmix, w_out_t, ya, yb, proj):
    T = ya.shape[0]
    tm = min(TOKEN_TILE, T)

    def body(dmix_ref, w_ref, ya_ref, yb_ref, ga_ref, gb_ref, dya_ref, dyb_ref, dga_ref, dgb_ref):
        dm = jnp.dot(dmix_ref[...], w_ref[...], preferred_element_type=F32)
        sa = _sigmoid(ga_ref[...])
        sb = _sigmoid(gb_ref[...])
        dya_ref[...] = (dm * sa).astype(BF16)
        dyb_ref[...] = (dm * sb).astype(BF16)
        dga_ref[...] = (dm * ya_ref[...].astype(F32) * sa * (1.0 - sa)).astype(BF16)
        dgb_ref[...] = (dm * yb_ref[...].astype(F32) * sb * (1.0 - sb)).astype(BF16)

    row = pl.BlockSpec((tm, D), lambda i: (i, 0))
    return pl.pallas_call(
        body, name="out_merge_bwd", grid=(T // tm,),
        in_specs=[row, _full((D, D)), row, row, pl.BlockSpec((tm, D), lambda i: (i, COL_GA // D)),
                  pl.BlockSpec((tm, D), lambda i: (i, COL_GB // D))],
        out_specs=[row] * 4, out_shape=[jax.ShapeDtypeStruct((T, D), BF16)] * 4,
        compiler_params=_params(("parallel",)),
    )(dmix, w_out_t, ya, yb, proj, proj)


def _conv_branch_bwd(proj, dya0, conv_w8):
    T = proj.shape[0]
    tm = min(FFN_TM, T)
    nt = T // tm

    def body(cb_ref, cc_ref, cv_ref, cbn_ref, ccp_ref, cvp_ref, ccn_ref, cvn_ref, d_ref, dn_ref, w_ref,
             d3_ref, dw_ref):
        i = pl.program_id(0)

        @pl.when(i == 0)
        def _():
            dw_ref[...] = jnp.zeros_like(dw_ref)

        first, last = i == 0, i == nt - 1
        w = w_ref[...]
        cc = jnp.concatenate([ccp_ref[...], cc_ref[...], ccn_ref[...]], axis=0)
        cv = jnp.concatenate([cvp_ref[...], cv_ref[...], cvn_ref[...]], axis=0)
        rows = lax.broadcasted_iota(jnp.int32, cc.shape, 0)
        z = jnp.where(jnp.logical_and(first, rows < SUBLANES), 0.0, cc * cv)
        z1 = pltpu.roll(z, 1, axis=0)
        z2 = pltpu.roll(z, 2, axis=0)
        cz = z2 * w[0:1] + z1 * w[1:2] + z * w[2:3]
        zeros8 = jnp.zeros((SUBLANES, CONV_W), F32)
        de = jnp.concatenate([zeros8, d_ref[...], jnp.where(last, 0.0, dn_ref[...])], axis=0)
        cbe = jnp.concatenate([zeros8, cb_ref[...], cbn_ref[...]], axis=0)
        dcz = de * cbe
        n = tm + 2 * SUBLANES
        dz = dcz * w[2:3] + pltpu.roll(dcz, n - 1, axis=0) * w[1:2] + pltpu.roll(dcz, n - 2, axis=0) * w[0:1]
        inner = slice(SUBLANES, SUBLANES + tm)
        d3_ref[:, 0:CONV_W] = (de * cz)[inner].astype(BF16)
        d3_ref[:, CONV_W:2 * CONV_W] = (dz * cv)[inner].astype(BF16)
        d3_ref[:, 2 * CONV_W:3 * CONV_W] = (dz * cc)[inner].astype(BF16)
        wrows = [jnp.sum((dcz * t)[inner], axis=0, keepdims=True) for t in (z2, z1, z)]
        dw_ref[...] += jnp.concatenate(wrows + [jnp.zeros((SUBLANES - 3, CONV_W), F32)], axis=0)

    blk = lambda col: pl.BlockSpec((tm, CONV_W), lambda i: (i, col))
    out_blk = pl.BlockSpec((tm, CONV_W), lambda i: (i, 0))
    return pl.pallas_call(
        body, name="conv_branch_bwd", grid=(nt,),
        in_specs=[blk(0), blk(1), blk(2), _next_spec(tm, CONV_W, 0, nt),
                  _prev_spec(tm, CONV_W, 1), _prev_spec(tm, CONV_W, 2),
                  _next_spec(tm, CONV_W, 1, nt), _next_spec(tm, CONV_W, 2, nt),
                  out_blk, _next_spec(tm, CONV_W, 0, nt), _full((SUBLANES, CONV_W))],
        out_specs=[pl.BlockSpec((tm, 3 * CONV_W), lambda i: (i, 0)), _full((SUBLANES, CONV_W))],
        out_shape=[jax.ShapeDtypeStruct((T, 3 * CONV_W), BF16), jax.ShapeDtypeStruct((SUBLANES, CONV_W), F32)],
        compiler_params=_params(("arbitrary",)),
    )(proj, proj, proj, proj, proj, proj, proj, proj, dya0, dya0, conv_w8)


def _qk_norm_bwd(proj, dqs, dkh, dlogf, qg, kg, bf_pad, gmat):
    T = proj.shape[0]
    tm = min(TOKEN_TILE, T)

    def body(q_ref, k_ref, f_ref, dqs_ref, dkh_ref, dlf_ref, qg_ref, kg_ref, bf_ref, g_ref,
             dqk_ref, dfl_ref, dqg_ref, dkg_ref, dbf_ref):
        @pl.when(pl.program_id(0) == 0)
        def _():
            dqg_ref[...] = jnp.zeros_like(dqg_ref)
            dkg_ref[...] = jnp.zeros_like(dkg_ref)
            dbf_ref[...] = jnp.zeros_like(dbf_ref)

        gm = g_ref[...]
        for src, d_src, gain, scale, dst, dgain in (
                (q_ref, dqs_ref, qg_ref, 1.0 / np.sqrt(HEAD_DIM), dqk_ref.at[:, 0:ATTN_W], dqg_ref),
                (k_ref, dkh_ref, kg_ref, 1.0, dqk_ref.at[:, ATTN_W:2 * ATTN_W], dkg_ref)):
            v = src[...]
            dhat = d_src[...].astype(F32) * scale
            inv = lax.rsqrt(_split_dot(v * v, gm, 2) * (1.0 / HEAD_DIM) + RMS_EPS)
            vn = v * inv
            dgain[...] += _rows8(dhat * vn)
            dvn = dhat * gain[...]
            mean = _split_dot(dvn * vn, gm, 2) * (1.0 / HEAD_DIM)
            dst[...] = (inv * (dvn - vn * mean)).astype(BF16)
        fx = f_ref[...] + bf_ref[...]
        dfl = dlf_ref[...] * _sigmoid(-fx)
        dfl_ref[...] = dfl.astype(BF16)
        dbf_ref[...] += _rows8(dfl)

    blk = lambda col: pl.BlockSpec((tm, ATTN_W), lambda i: (i, col))
    out_blk = pl.BlockSpec((tm, ATTN_W), lambda i: (i, 0))
    f_in = pl.BlockSpec((tm, F_PAD), lambda i: (i, COL_F // F_PAD))
    f_blk = pl.BlockSpec((tm, F_PAD), lambda i: (i, 0))
    return pl.pallas_call(
        body, name="qk_norm_bwd", grid=(T // tm,),
        in_specs=[blk(3), blk(4), f_in, out_blk, out_blk, f_blk, _full((1, ATTN_W)), _full((1, ATTN_W)),
                  _full((1, F_PAD)), _full((ATTN_W, ATTN_W))],
        out_specs=[pl.BlockSpec((tm, 2 * ATTN_W), lambda i: (i, 0)), f_blk, _full((SUBLANES, ATTN_W)),
                   _full((SUBLANES, ATTN_W)), _full((SUBLANES, F_PAD))],
        out_shape=[jax.ShapeDtypeStruct((T, 2 * ATTN_W), BF16), jax.ShapeDtypeStruct((T, F_PAD), BF16)]
        + [jax.ShapeDtypeStruct((SUBLANES, ATTN_W), F32)] * 2 + [jax.ShapeDtypeStruct((SUBLANES, F_PAD), F32)],
        compiler_params=_params(("arbitrary",)),
    )(proj, proj, proj, dqs, dkh, dlogf, qg, kg, bf_pad, gmat)


def _pad_rows8(w):
    return jnp.pad(w, ((0, SUBLANES - w.shape[0]), (0, 0)))


def _fold8(acc):
    return jnp.sum(acc, axis=0, keepdims=True)


def _late_weights(mats):
    out = {}
    for name in ("w_branch_a", "w_out", "w_up", "w_down"):
        out[name] = mats[name]
        out[name + "_t"] = mats[name].T
    out["w_branch_b_heads"] = _pad_head_rows(mats["w_branch_b"])
    out["w_branch_b_heads_t"] = out["w_branch_b_heads"].T
    return out


def _local_step(x, target, mod, wts, late=None):
    T = x.shape[0]
    tb = min(MATMUL_TILE, T)
    tk_long = min(2 * MATMUL_TILE, T)
    tm = min(TOKEN_TILE, T)
    sh1, sc1, g1, sh2, sc2, g2 = [mod[:, i * D:(i + 1) * D] for i in range(N_MOD)]
    w_in, w_in_t = wts["w_in"], wts["w_in_t"]
    conv_a8 = _pad_rows8(wts["conv_a_w"])
    conv_f8 = _pad_rows8(wts["conv_ffn_w"])
    qg = jnp.tile(wts["q_norm_g"], (1, N_HEADS))
    kg = jnp.tile(wts["k_norm_g"], (1, N_HEADS))
    bf_pad = jnp.pad(wts["b_f"], ((0, 0), (0, F_PAD - N_HEADS)))
    gmat = _group_matrix()

    h = _norm_mod(x, wts["norm1_g"], sc1, sh1, name="norm1_fwd")
    proj = _matmul(h, w_in, name="mm_in", tm=tb, tn=896, tk=D)
    fcum = _cumsum(proj, reverse=False, name="gate_cumsum", col=COL_F // F_PAD, gate_bias=bf_pad)
    ya0, qa, ka, va = _branch_prep(proj, fcum, conv_a8, qg, kg, gmat)
    if late is None:
        o_h, qb = _attn_fwd(qa, ka, va)
    else:
        o_h, qb, *gathered = _attn_fwd(qa, ka, va, _Exchange([late[name] for name, *_ in LATE], scatter=False))
        wts = dict(wts)
        mats = {name: _join_shards(g, axis) for (name, _, _, axis), g in zip(LATE, gathered) if name != "w_up"}
        mats["w_up"] = _assemble_columns(gathered[[name for name, *_ in LATE].index("w_up")], 2 * D_FF // N_DEV,
                                         2 * D_FF, ((0, 2 * D_FF, 0),), name="assemble_w_up")
        wts.update(_late_weights(mats))
    ya, yb, merged = _branch_merge_fwd(ya0, o_h, proj, wts["w_branch_a"], wts["w_branch_b_heads"])
    mix, x1, h2 = _out_resid_norm(x, merged, wts["w_out"], g1, wts["norm2_g"], sc2, sh2)
    u = _matmul(h2, wts["w_up"], name="mm_up", tm=tb, tn=1408, tk=D)
    act, conv_gate, conv_val = _ffn_act_fwd(u, conv_f8)
    dy, dff, sq8, dg2_8 = _down_loss_head(x1, act, wts["w_down"], g2, target)
    sq = jnp.sum(sq8).reshape(1, 1)

    grads = {}
    da = _matmul(dff, wts["w_down_t"], name="mm_down_dx", tm=tb, tn=1408, tk=D, out_dtype=BF16)
    grads["w_down"] = _matmul(act, dff, name="mm_down_dw", tm=1408, tn=D, tk=tk_long, trans_a=True)
    dug, duv, dconv_f8 = _ffn_act_bwd(u, conv_gate, conv_val, da, conv_f8)
    grads["conv_ffn_w"] = dconv_f8[:3]
    dx1, dsh2_8, dsc2_8, dn2_8, dmix, dg1_8 = _matmul_pieces(
        [dug, duv], wts["w_up_t"], name="mm_up_dx", tm=tm, norm_bwd=(x1, dy, wts["norm2_g"], sc2, mix, g1))
    dw_up = [_matmul(h2, d, name="mm_up_dw_" + half, tm=D, tn=1408, tk=tk_long, trans_a=True)
             for half, d in (("gate", dug), ("val", duv))]
    if late is None:
        grads["w_up"] = jnp.concatenate(dw_up, axis=1)
    grads["norm2_g"] = _fold8(dn2_8)

    grads["w_out"] = _matmul(merged, dmix, name="mm_out_dw", tm=D, tn=D, tk=tk_long, trans_a=True)
    dya, dyb, dga, dgb = _out_merge_bwd(dmix, wts["w_out_t"], ya, yb, proj)
    dya0 = _matmul(dya, wts["w_branch_a_t"], name="mm_branch_a_dx", tm=tb, tn=CONV_W, tk=D)
    grads["w_branch_a"] = _matmul(ya0, dya, name="mm_branch_a_dw", tm=CONV_W, tn=D, tk=tk_long, trans_a=True)
    doa = _branch_b_bwd(dyb, o_h, wts["w_branch_b_heads_t"])
    grads["w_branch_b"] = _branch_b_dw(o_h, dyb)[:, :HEAD_DIM].reshape(ATTN_W, D)
    dconv3, dconv_a8 = _conv_branch_bwd(proj, dya0, conv_a8)
    grads["conv_a_w"] = dconv_a8[:3]

    dw_conv, = _matmul_tn_pieces(h, [dconv3], name="mm_in_dw_conv", tk=tk_long)
    dwa, dwb = _matmul_tn_pieces(h, [dga, dgb], name="mm_in_dw_gates", tk=tk_long)
    parts = {}
    if late is None:
        dq_h, dk_h, dv_h = _attn_bwd(qb, ka, va, doa)
    else:
        ready = [(_column_shards(dw_up) if name == "w_up" else _split_shards(grads[name], axis)).astype(BF16)
                 for name, _, _, axis in LATE]
        blank = [jnp.zeros((D, n), F32) for n in (2 * ATTN_W, ATTN_W, N_HEADS)]
        early_in = _column_shards([dw_conv] + blank + [dwa, dwb]).astype(BF16)
        dq_h, dk_h, dv_h, *recv = _attn_bwd(
            qb, ka, va, doa,
            _Exchange(ready + [_pack_full_by_dest(grads, CONVS, SUBLANES), early_in], scatter=True,
                      dests=[None] * (len(ready) + 1) + [W_IN_EARLY_DESTS]))
        parts = dict(zip([name for name, *_ in LATE] + ["conv", "w_in_early"], recv))
    dq_tok, dk_tok, dv_tok, dfcum = _attn_unpack(dq_h, dk_h, dv_h)
    dlogf = _cumsum(dfcum, reverse=True, name="gate_cumsum_bwd")
    dqk, dfl, dqg8, dkg8, dbf8 = _qk_norm_bwd(proj, dq_tok, dk_tok, dlogf, qg, kg, bf_pad, gmat)
    grads["q_norm_g"] = jnp.sum(_fold8(dqg8).reshape(N_HEADS, HEAD_DIM), axis=0, keepdims=True)
    grads["k_norm_g"] = jnp.sum(_fold8(dkg8).reshape(N_HEADS, HEAD_DIM), axis=0, keepdims=True)
    grads["b_f"] = _fold8(dbf8)[:, :N_HEADS]
    pieces = [dconv3, dqk, dv_tok, dga, dgb, dfl]
    dw_qk, dw_v, dwf = _matmul_tn_pieces(h, [dqk, dv_tok, dfl], name="mm_in_dw_late", tk=tk_long)
    dw_in = [dw_conv, dw_qk, dw_v, dwf[:, :N_HEADS], dwa, dwb]
    norm1 = (x, dx1, wts["norm1_g"], sc1)
    if late is None:
        grads["w_in"] = jnp.concatenate(dw_in, axis=1)
        grad_x, dsh1_8, dsc1_8, dn1_8 = _matmul_pieces(pieces, w_in_t, name="mm_in_dx", tm=tm, norm_bwd=norm1)
    else:
        late_dests = tuple(j for j in range(N_DEV) if j not in W_IN_EARLY_DESTS)
        grad_x, dsh1_8, dsc1_8, dn1_8, late_in = _matmul_pieces(
            pieces, w_in_t, name="mm_in_dx", tm=tm, norm_bwd=norm1,
            exchange=_Exchange([_column_shards(dw_in).astype(BF16)], scatter=True, dests=[late_dests]))
        me = 4 * lax.axis_index("x") + 2 * lax.axis_index("y") + lax.axis_index("c")
        is_early = functools.reduce(jnp.logical_or, [me == j for j in W_IN_EARLY_DESTS])
        parts["w_in"] = jnp.where(is_early, parts.pop("w_in_early"), late_in)
    grads["norm1_g"] = _fold8(dn1_8)
    grads["mod"] = jnp.concatenate([_fold8(a) for a in (dsh1_8, dsc1_8, dg1_8, dsh2_8, dsc2_8, dg2_8)], axis=1)
    return sq, grad_x, grads, parts


def _me_and_peers():
    mx, my, mc = lax.axis_index("x"), lax.axis_index("y"), lax.axis_index("c")
    me = 4 * mx + 2 * my + mc
    peers = []
    for k in range(1, N_DEV):
        px = 1 - mx if k & 4 else mx
        py = 1 - my if k & 2 else my
        pc = 1 - mc if k & 1 else mc
        peers.append(((px, py, pc), 4 * px + 2 * py + pc))
    return me, peers


HBM_SPEC = pl.BlockSpec(memory_space=pltpu.HBM)


class _Exchange:
    def __init__(self, xs, scatter, dests=None):
        self.xs, self.scatter, self.n = list(xs), scatter, len(xs)
        self.dests = list(dests) if dests is not None else [None] * self.n
        assert scatter or all(d is None for d in self.dests)
        self.out_shapes = [jax.ShapeDtypeStruct(x.shape if scatter else (N_DEV,) + x.shape, x.dtype) for x in xs]
        self.in_specs = [HBM_SPEC] * self.n
        self.out_specs = [HBM_SPEC] * self.n
        self.scratch = [pltpu.SemaphoreType.DMA((self.n, N_DEV - 1)), pltpu.SemaphoreType.DMA((self.n, N_DEV - 1)),
                        pltpu.SemaphoreType.DMA((self.n,))]

    def _copies(self, x_refs, out_refs, sems):
        send_sems, recv_sems, local_sems = sems
        me, peers = _me_and_peers()

        def src(a, idx):
            return x_refs[a].at[idx] if self.scatter else x_refs[a]

        def copy(a, k, from_idx, to_slot, device):
            return pltpu.make_async_remote_copy(
                src_ref=src(a, from_idx), dst_ref=out_refs[a].at[to_slot], send_sem=send_sems.at[a, k],
                recv_sem=recv_sems.at[a, k], device_id=device, device_id_type=MESH)

        local = [(a, me, pltpu.make_async_copy(src(a, me), out_refs[a].at[me], local_sems.at[a]))
                 for a in range(self.n)]
        sends = [(a, idx, copy(a, k, idx, me, dev)) for a in range(self.n) for k, (dev, idx) in enumerate(peers)]
        recvs = [(a, me, copy(a, k, idx, idx, dev)) for a in range(self.n) for k, (dev, idx) in enumerate(peers)]
        return local, sends, recvs

    def _each(self, triples, action):
        for a, dest, cp in triples:
            allowed = self.dests[a]
            if allowed is None:
                action(cp)
            else:
                pl.when(functools.reduce(jnp.logical_or, [dest == d for d in allowed]))(functools.partial(action, cp))

    def start(self, x_refs, out_refs, sems):
        local, sends, _ = self._copies(x_refs, out_refs, sems)
        self._each(local + sends, lambda cp: cp.start())

    def wait(self, x_refs, out_refs, sems):
        local, sends, recvs = self._copies(x_refs, out_refs, sems)
        self._each(recvs, lambda cp: cp.wait_recv())
        self._each(sends, lambda cp: cp.wait_send())
        self._each(local, lambda cp: cp.wait())

    def split(self, refs, n_in, n_out):
        n = self.n
        ins, xin = refs[:n_in], refs[n_in:n_in + n]
        outs, xout = refs[n_in + n:n_in + n + n_out], refs[n_in + n + n_out:n_in + 2 * n + n_out]
        rest = refs[n_in + 2 * n + n_out:]
        return ins, outs, rest[:len(rest) - 3], (xin, xout, rest[len(rest) - 3:])


def _ride(exchange, first, last, refs):
    if exchange is None:
        return

    @pl.when(first)
    def _():
        exchange.start(*refs)

    @pl.when(last)
    def _():
        exchange.wait(*refs)


def _gather_two_level(xs, *, name):
    n = len(xs)
    out_shapes = [jax.ShapeDtypeStruct((N_DEV,) + x.shape, x.dtype) for x in xs]

    def body(*refs):
        x_refs, out_refs = refs[:n], refs[n:2 * n]
        send_sems, recv_sems, local_sems = refs[2 * n:]
        x, y, c = lax.axis_index("x"), lax.axis_index("y"), lax.axis_index("c")
        me, sibling = (x, y, c), (x, y, 1 - c)
        chips = [(1 - x, y), (x, 1 - y), (1 - x, 1 - y)]

        def slot(a, dev):
            return out_refs[a].at[4 * dev[0] + 2 * dev[1] + dev[2]]

        def copy(a, k, block, to, src=None):
            return pltpu.make_async_remote_copy(
                src_ref=slot(a, block) if src is None else src, dst_ref=slot(a, block),
                send_sem=send_sems.at[a, k], recv_sem=recv_sems.at[a, k], device_id=to, device_id_type=MESH)

        mine = [pltpu.make_async_copy(x_refs[a], slot(a, me), local_sems.at[a]) for a in range(n)]
        first = [copy(a, 0, me, sibling, src=x_refs[a]) for a in range(n)]
        first += [copy(a, 1 + j, me, (*chip, c), src=x_refs[a]) for a in range(n) for j, chip in enumerate(chips)]
        for cp in mine + first:
            cp.start()
        passed = []
        for a in range(n):
            for j, chip in enumerate(chips):
                copy(a, 1 + j, (*chip, c), me).wait_recv()
                passed.append(copy(a, 4 + j, (*chip, c), sibling))
                passed[-1].start()
        for a in range(n):
            copy(a, 0, sibling, me).wait_recv()
            for j, chip in enumerate(chips):
                copy(a, 4 + j, (*chip, 1 - c), me).wait_recv()
        for cp in first + passed:
            cp.wait_send()
        for cp in mine:
            cp.wait()

    return pl.pallas_call(
        body, name=name, in_specs=[HBM_SPEC] * n, out_specs=[HBM_SPEC] * n, out_shape=out_shapes,
        scratch_shapes=[pltpu.SemaphoreType.DMA((n, N_DEV - 1)), pltpu.SemaphoreType.DMA((n, N_DEV - 1)),
                        pltpu.SemaphoreType.DMA((n,))],
        compiler_params=pltpu.CompilerParams(has_side_effects=True),
    )(*xs)


def _exchange(xs, *, name, scatter):
    ex = _Exchange(xs, scatter)

    def body(*refs):
        _, _, _, xrefs = ex.split(refs, 0, 0)
        ex.start(*xrefs)
        ex.wait(*xrefs)

    return pl.pallas_call(
        body, name=name, in_specs=ex.in_specs, out_specs=ex.out_specs, out_shape=ex.out_shapes,
        scratch_shapes=ex.scratch, compiler_params=pltpu.CompilerParams(has_side_effects=True),
    )(*xs)


def _ada_fwd(c_all, w_shard, b_shard):
    n = w_shard.shape[1]

    def body(c_ref, w_ref, b_ref, o_ref):
        cv = c_ref[...]
        act = (cv * _sigmoid(cv)).astype(BF16)
        o_ref[...] = jnp.dot(act, w_ref[...].astype(BF16), preferred_element_type=F32) + b_ref[...]

    return pl.pallas_call(
        body, name="ada_fwd", in_specs=[_full((N_DEV, D)), _full((D, n)), _full((1, n))],
        out_specs=_full((N_DEV, n)), out_shape=jax.ShapeDtypeStruct((N_DEV, n), F32), grid=(1,),
        compiler_params=_params(("arbitrary",)),
    )(c_all, w_shard, b_shard)


def _ada_bwd(c_all_t, dmod_pad):
    n = dmod_pad.shape[1]

    def body(c_ref, d_ref, o_ref):
        cv = c_ref[...]
        act = (cv * _sigmoid(cv)).astype(BF16)
        o_ref[...] = jnp.dot(act, d_ref[...].astype(BF16), preferred_element_type=F32)

    return pl.pallas_call(
        body, name="ada_bwd", in_specs=[_full((D, LANES)), _full((LANES, n))],
        out_specs=_full((D, n)), out_shape=jax.ShapeDtypeStruct((D, n), F32), grid=(1,),
        compiler_params=_params(("arbitrary",)),
    )(c_all_t, dmod_pad)


ADAM_ROWS = 256


def _adamw(parts, w, m, v, *, name):
    n, R, C = parts.shape
    tr = next((t for t in (ADAM_ROWS, 128, 64, 32, 16, SUBLANES) if R % t == 0), R)

    def body(p_ref, w_ref, m_ref, v_ref, g_ref, d_ref, nm_ref, nv_ref):
        g = p_ref[0].astype(F32)
        for j in range(1, n):
            g = g + p_ref[j].astype(F32)
        g_ref[...] = g
        nm = ADAM_B1 * m_ref[...] + (1.0 - ADAM_B1) * g
        nv = ADAM_B2 * v_ref[...] + (1.0 - ADAM_B2) * (g * g)
        nm_ref[...] = nm
        nv_ref[...] = nv
        m_hat = nm / (1.0 - ADAM_B1 ** ADAM_STEP)
        v_hat = nv / (1.0 - ADAM_B2 ** ADAM_STEP)
        d_ref[...] = -ADAM_LR * (m_hat / (jnp.sqrt(v_hat) + ADAM_EPS) + ADAM_WD * w_ref[...])

    row = pl.BlockSpec((tr, C), lambda i: (i, 0))
    return pl.pallas_call(
        body, name=name, grid=(R // tr,),
        in_specs=[pl.BlockSpec((n, tr, C), lambda i: (0, i, 0)), row, row, row], out_specs=[row] * 4,
        out_shape=[jax.ShapeDtypeStruct((R, C), F32)] * 4,
        compiler_params=_params(("parallel",)),
    )(parts, w, m, v)


SHARDED = (("w_in", D, IN_W, 1), ("w_branch_a", CONV_W, D, 1), ("w_branch_b", ATTN_W, D, 1), ("w_out", D, D, 0),
           ("w_up", D, 2 * D_FF, 1), ("w_down", D_FF, D, 0), ("conv_a_w", 3, CONV_W, 1),
           ("conv_ffn_w", 3, 2 * D_FF, 1))
MATRICES = SHARDED[:6]
LATE = MATRICES[1:]
CONVS = SHARDED[6:]
REPLICATED = (("b_ada", N_MOD * D), ("norm1_g", D), ("norm2_g", D), ("b_f", N_HEADS), ("q_norm_g", HEAD_DIM),
              ("k_norm_g", HEAD_DIM))


def _shard_shape(rows, cols, axis):
    return (rows // N_DEV, cols) if axis == 0 else (rows, cols // N_DEV)


def _pack_rows(flat, multiple):
    length = flat.shape[-1]
    rows = -(-length // PACK_W)
    rows = -(-rows // multiple) * multiple
    pad = [(0, 0)] * (flat.ndim - 1) + [(0, rows * PACK_W - length)]
    return jnp.pad(flat, pad).reshape(flat.shape[:-1] + (rows, PACK_W))


def _pack_shards(shards, spec, multiple, dtype):
    flat = jnp.concatenate([shards[name].reshape(-1).astype(dtype) for name, *_ in spec])
    return _pack_rows(flat, multiple)


def _join_lane_blocks(gathered):
    n, r, c = gathered.shape

    def body(g_ref, o_ref):
        for j in range(n):
            o_ref[:, j * c:(j + 1) * c] = g_ref[j]

    return pl.pallas_call(
        body, name="join_lane_blocks", grid=(1,), in_specs=[_full((n, r, c))], out_specs=_full((r, n * c)),
        out_shape=jax.ShapeDtypeStruct((r, n * c), gathered.dtype), compiler_params=_params(("arbitrary",)),
    )(gathered)


SHARD_PAD = 768


def _assemble_columns(gathered, shard_cols, out_cols, segments, *, name):
    n, rows, padw = gathered.shape
    assert n == N_DEV and padw == SHARD_PAD and shard_cols <= SHARD_PAD

    def body(g_ref, o_ref):
        j = pl.program_id(0)

        @pl.when(j == 0)
        def _():
            o_ref[...] = jnp.zeros_like(o_ref)

        for dev in range(N_DEV):
            @pl.when(j == dev)
            def _(dev=dev):
                x = g_ref[0]
                for lo, hi, delta in segments:
                    a, b = max(lo, dev * shard_cols), min(hi, (dev + 1) * shard_cols)
                    if a >= b:
                        continue
                    base = (a + delta) // LANES * LANES
                    width = -(-(b + delta - base) // LANES) * LANES
                    src = lax.broadcasted_iota(jnp.int32, (padw, width), 0) + dev * shard_cols
                    dst = lax.broadcasted_iota(jnp.int32, (padw, width), 1) + (base - delta)
                    place = jnp.where((src == dst) & (src >= a) & (src < b), 1.0, 0.0).astype(BF16)
                    moved = jnp.dot(x, place, preferred_element_type=F32).astype(BF16)
                    o_ref[:, base:base + width] = o_ref[:, base:base + width] + moved

    return pl.pallas_call(
        body, name=name, grid=(N_DEV,), in_specs=[pl.BlockSpec((1, rows, padw), lambda j: (j, 0, 0))],
        out_specs=_full((rows, out_cols)), out_shape=jax.ShapeDtypeStruct((rows, out_cols), BF16),
        compiler_params=_params(("arbitrary",)),
    )(gathered)


def _pad_shard(w):
    return jnp.pad(w.astype(BF16), ((0, 0), (0, SHARD_PAD - w.shape[1])))


W_IN_SEGMENTS = ((0, COL_GA, 0), (COL_GA, COL_GA + N_HEADS, COL_F - COL_GA), (COL_GA + N_HEADS, IN_W, -N_HEADS))


W_IN_EARLY_DESTS = tuple(j for j in range(N_DEV) if (j + 1) * (IN_W // N_DEV) <= 3 * CONV_W
                         or j * (IN_W // N_DEV) >= COL_GA + N_HEADS)


def _join_shards(gathered, axis):
    if axis == 0:
        return gathered.reshape(N_DEV * gathered.shape[1], gathered.shape[2])
    if gathered.shape[2] == LANES:
        return _join_lane_blocks(gathered)
    return jnp.concatenate([gathered[j] for j in range(N_DEV)], axis=1)


def _column_shards(pieces):
    total = sum(p.shape[1] for p in pieces)
    width = total // N_DEV
    shards = []
    for j in range(N_DEV):
        lo, hi, off, segs = j * width, (j + 1) * width, 0, []
        for p in pieces:
            a, b = max(lo, off), min(hi, off + p.shape[1])
            if a < b:
                segs.append(p[:, a - off:b - off])
            off += p.shape[1]
        shards.append(segs[0] if len(segs) == 1 else jnp.concatenate(segs, axis=1))
    return jnp.stack(shards)


def _split_shards(full, axis):
    if axis == 0:
        return full.reshape(N_DEV, full.shape[0] // N_DEV, full.shape[1])
    c = full.shape[1] // N_DEV
    return jnp.stack([full[:, j * c:(j + 1) * c] for j in range(N_DEV)])


def _unpack_shards(packed, spec):
    flat = packed.reshape(-1)
    out, off = {}, 0
    for name, rows, cols, axis in spec:
        r, c = _shard_shape(rows, cols, axis)
        out[name] = flat[off:off + r * c].reshape(r, c)
        off += r * c
    return out


def _unpack_gathered(gathered, spec):
    flat = gathered.reshape(N_DEV, -1)
    out, off = {}, 0
    for name, rows, cols, axis in spec:
        r, c = _shard_shape(rows, cols, axis)
        seg = flat[:, off:off + r * c].reshape(N_DEV, r, c)
        out[name] = seg.reshape(rows, cols) if axis == 0 else seg.transpose(1, 0, 2).reshape(rows, cols)
        off += r * c
    return out


def _pack_full_by_dest(full, spec, multiple):
    segs = []
    for name, rows, cols, axis in spec:
        r, c = _shard_shape(rows, cols, axis)
        a = full[name]
        seg = a.reshape(N_DEV, r, c) if axis == 0 else a.reshape(rows, N_DEV, c).transpose(1, 0, 2)
        segs.append(seg.reshape(N_DEV, r * c))
    return _pack_rows(jnp.concatenate(segs, axis=1), multiple)


def _pad_head_rows(w):
    n = w.shape[1]
    padded = jnp.pad(w.reshape(N_HEADS, HEAD_DIM, n), ((0, 0), (0, LANES - HEAD_DIM), (0, 0)))
    return padded.reshape(N_HEADS * LANES, n)


def kernel(x, c, w_ada, b_ada, norm1_g, w_in, b_f, conv_a_w, q_norm_g, k_norm_g, w_branch_a, w_branch_b, w_out, norm2_g, w_up, conv_ffn_w, w_down, loss_target, m_w_ada, m_b_ada, m_norm1_g, m_w_in, m_b_f, m_conv_a_w, m_q_norm_g, m_k_norm_g, m_w_branch_a, m_w_branch_b, m_w_out, m_norm2_g, m_w_up, m_conv_ffn_w, m_w_down, v_w_ada, v_b_ada, v_norm1_g, v_w_in, v_b_f, v_conv_a_w, v_q_norm_g, v_k_norm_g, v_w_branch_a, v_w_branch_b, v_w_out, v_norm2_g, v_w_up, v_conv_ffn_w, v_w_down):
    names = ("w_ada", "b_ada", "norm1_g", "w_in", "b_f", "conv_a_w", "q_norm_g", "k_norm_g", "w_branch_a",
             "w_branch_b", "w_out", "norm2_g", "w_up", "conv_ffn_w", "w_down")
    squeeze = lambda a: a[0] if a.ndim == 3 else a
    W = dict(zip(names, map(squeeze, (w_ada, b_ada, norm1_g, w_in, b_f, conv_a_w, q_norm_g, k_norm_g, w_branch_a,
                                      w_branch_b, w_out, norm2_g, w_up, conv_ffn_w, w_down))))
    M = dict(zip(names, map(squeeze, (m_w_ada, m_b_ada, m_norm1_g, m_w_in, m_b_f, m_conv_a_w, m_q_norm_g,
                                      m_k_norm_g, m_w_branch_a, m_w_branch_b, m_w_out, m_norm2_g, m_w_up,
                                      m_conv_ffn_w, m_w_down))))
    V = dict(zip(names, map(squeeze, (v_w_ada, v_b_ada, v_norm1_g, v_w_in, v_b_f, v_conv_a_w, v_q_norm_g,
                                      v_k_norm_g, v_w_branch_a, v_w_branch_b, v_w_out, v_norm2_g, v_w_up,
                                      v_conv_ffn_w, v_w_down))))
    me = 4 * lax.axis_index("x") + 2 * lax.axis_index("y") + lax.axis_index("c")
    ada_n = N_MOD * D // N_DEV

    small = jnp.concatenate([c.reshape(-1), W["conv_a_w"].reshape(-1), W["conv_ffn_w"].reshape(-1)])
    small_all, w_in_all = _gather_two_level([_pack_rows(small, SUBLANES), _pad_shard(W["w_in"])],
                                            name="gather_first")
    small_all = small_all.reshape(N_DEV, -1)
    c_all = small_all[:, :D]
    conv_all = _unpack_gathered(small_all[:, D:], CONVS)

    b_shard = lax.dynamic_slice(W["b_ada"], (0, me * ada_n), (1, ada_n))
    mod_part = _ada_fwd(c_all, W["w_ada"], b_shard)
    mod_all, = _exchange([mod_part], name="gather_mod", scatter=False)
    mod = lax.dynamic_index_in_dim(mod_all, me, axis=1, keepdims=False).reshape(1, N_MOD * D)

    wts = {"w_in": _assemble_columns(w_in_all, IN_W // N_DEV, IN_W_PAD, W_IN_SEGMENTS, name="assemble_w_in")}
    wts["w_in_t"] = wts["w_in"].T
    wts.update(conv_all)
    for name in ("norm1_g", "norm2_g", "q_norm_g", "k_norm_g", "b_f"):
        wts[name] = W[name]
    late = {name: _pad_shard(W[name]) if name == "w_up" else W[name].astype(BF16) for name, *_ in LATE}

    sq, grad_x, grads, parts = _local_step(x[0], loss_target[0], mod, wts, late)
    loss = lax.psum(sq[0, 0] * (0.5 / D), AXES)

    grads["b_ada"] = grads["mod"]
    rep_flat = lambda src: jnp.concatenate([src[name].reshape(-1) for name, _ in REPLICATED])
    rep_parts, = _exchange([_pack_rows(rep_flat(grads), 16)], name="gather_small_grads", scatter=False)
    rep_out = _adamw(rep_parts, *[_pack_rows(rep_flat(s), 16) for s in (W, M, V)], name="adamw_replicated")

    dmod_all = rep_parts.reshape(N_DEV, -1)[:, :N_MOD * D]
    dmod_mine = lax.dynamic_slice(dmod_all, (0, me * ada_n), (N_DEV, ada_n))
    g_ada = _ada_bwd(jnp.pad(c_all.T, ((0, 0), (0, LANES - N_DEV))),
                     jnp.pad(dmod_mine, ((0, LANES - N_DEV), (0, 0))))
    ada_out = _adamw(g_ada[None], W["w_ada"], M["w_ada"], V["w_ada"], name="adamw_ada")

    mat_out = {name: _adamw(parts[name], W[name], M[name], V[name], name="adamw_" + name) for name, *_ in MATRICES}
    conv_out = _adamw(parts["conv"], *[_pack_shards(s, CONVS, SUBLANES, F32) for s in (W, M, V)],
                      name="adamw_conv")

    results = []
    for kind in range(4):
        per = {"w_ada": ada_out[kind]}
        per.update({name: out[kind] for name, out in mat_out.items()})
        per.update(_unpack_shards(conv_out[kind], CONVS))
        flat, off = rep_out[kind].reshape(-1), 0
        for name, n in REPLICATED:
            per[name] = flat[off:off + n].reshape(1, n)
            off += n
        results.append(per)
    restore = lambda name, a: a[None] if W[name].ndim == 2 and name not in dict(REPLICATED) else a
    outs = [loss, grad_x[None]]
    for per in results:
        outs.extend(restore(name, per[name]) for name in names)
    return tuple(outs)
```

```python
import jax
import jax.numpy as jnp
import numpy as np
from jax import lax
from jax.experimental import pallas as pl
from jax.experimental.pallas import tpu as pltpu

F32 = jnp.float32
BF16 = jnp.bfloat16

N_DEV = 8
D = 1024
N_HEADS = 8
HEAD_DIM = 64
ATTN_W = 512
CONV_W = 512
D_FF = 2816
N_MOD = 6
IN_W = 5128
RMS_EPS = 1e-6
NEG_INF = -1e30

IN_W_PAD = 5376
COL_GA = 3072
COL_GB = 4096
COL_F = 5120
F_PAD = 128

ADAM_LR = 0.001
ADAM_B1 = 0.9
ADAM_B2 = 0.999
ADAM_EPS = 1e-08
ADAM_WD = 0.01
ADAM_STEP = 10

LANES = 128
SUBLANES = 8
BF16_ROWS = 16
VMEM_LIMIT = 52 * 1024 * 1024
TOKEN_TILE = 512
MATMUL_TILE = 1024
ATTN_BLOCK = 512
PACK_W = 1024

MESH = pl.DeviceIdType.MESH
AXES = ("x", "y", "c")


def _params(sem=None, **kw):
    return pltpu.CompilerParams(dimension_semantics=sem, vmem_limit_bytes=VMEM_LIMIT, **kw)


def _full(shape):
    nd = len(shape)
    return pl.BlockSpec(shape, lambda *_: (0,) * nd)


def _tn_dot(a, b):
    return lax.dot_general(a, b, (((0,), (0,)), ((), ())), preferred_element_type=F32)


def _matmul(a, b, *, name, tm, tn, tk, out_dtype=F32, trans_a=False, exchange=None):
    if trans_a:
        K, M = a.shape
    else:
        M, K = a.shape
    N = b.shape[1]
    assert b.shape[0] == K and M % tm == 0 and N % tn == 0 and K % tk == 0, (name, a.shape, b.shape)
    nm, nn, nk = M // tm, N // tn, K // tk

    def body(*refs):
        if exchange is None:
            a_ref, b_ref, o_ref, *own = refs
        else:
            (a_ref, b_ref), (o_ref,), own, xrefs = exchange.split(refs, 2, 1)
            ids = [pl.program_id(d) for d in range(3)]
            first = jnp.logical_and(jnp.logical_and(ids[0] == 0, ids[1] == 0), ids[2] == 0)
            last = jnp.logical_and(jnp.logical_and(ids[0] == nn - 1, ids[1] == nm - 1), ids[2] == nk - 1)
            _ride(exchange, first, last, xrefs)
        k = pl.program_id(2)
        av = a_ref[...].astype(BF16)
        bv = b_ref[...].astype(BF16)
        prod = _tn_dot(av, bv) if trans_a else jnp.dot(av, bv, preferred_element_type=F32)
        if nk == 1:
            o_ref[...] = prod.astype(out_dtype)
            return
        acc_ref, = own

        @pl.when(k == 0)
        def _():
            acc_ref[...] = prod

        @pl.when(k > 0)
        def _():
            acc_ref[...] += prod

        @pl.when(k == nk - 1)
        def _():
            o_ref[...] = acc_ref[...].astype(out_dtype)

    if trans_a:
        a_spec = pl.BlockSpec((tk, tm), lambda j, i, k: (k, i))
    else:
        a_spec = pl.BlockSpec((tm, tk), lambda j, i, k: (i, k))
    in_specs = [a_spec, pl.BlockSpec((tk, tn), lambda j, i, k: (k, j))]
    out_spec = pl.BlockSpec((tm, tn), lambda j, i, k: (i, j))
    out_shape = jax.ShapeDtypeStruct((M, N), out_dtype)
    scratch = [pltpu.VMEM((tm, tn), F32)] if nk > 1 else []
    if exchange is None:
        return pl.pallas_call(
            body, name=name, grid=(nn, nm, nk), in_specs=in_specs, out_specs=out_spec, out_shape=out_shape,
            scratch_shapes=scratch, compiler_params=_params(("parallel", "parallel", "arbitrary")),
        )(a, b)
    return pl.pallas_call(
        body, name=name, grid=(nn, nm, nk), in_specs=in_specs + exchange.in_specs,
        out_specs=[out_spec] + exchange.out_specs, out_shape=[out_shape] + exchange.out_shapes,
        scratch_shapes=scratch + exchange.scratch, compiler_params=_params(("arbitrary",) * 3),
    )(a, b, *exchange.xs)


def _norm_bwd_tile(dh, ins, outs, first):
    x_ref, dr_ref, g_ref, sc_ref = ins[:4]
    dx_ref, dsh_ref, dsc_ref, dg_ref = outs[:4]

    @pl.when(first)
    def _():
        for ref in outs[1:4] + outs[5:]:
            ref[...] = jnp.zeros_like(ref)

    xv = x_ref[...]
    gv = g_ref[...]
    one_sc = 1.0 + sc_ref[...]
    inv = lax.rsqrt(jnp.mean(xv * xv, axis=-1, keepdims=True) + RMS_EPS)
    xn = xv * inv
    dxn = dh * (gv * one_sc)
    dx = dr_ref[...] + inv * (dxn - xn * jnp.mean(dxn * xn, axis=-1, keepdims=True))
    dx_ref[...] = dx
    dhxn = dh * xn
    dsh_ref[...] += _rows8(dh)
    dsc_ref[...] += _rows8(dhxn * gv)
    dg_ref[...] += _rows8(dhxn * one_sc)
    if len(ins) == 6:
        mix_ref, g1_ref = ins[4:]
        dmix_ref, dg1_ref = outs[4:]
        dmix_ref[...] = (dx * g1_ref[...]).astype(BF16)
        dg1_ref[...] += _rows8(dx * mix_ref[...])


def _matmul_pieces(pieces, b, *, name, tm, exchange=None, norm_bwd=None):
    M = pieces[0].shape[0]
    widths = [p.shape[1] for p in pieces]
    offsets = [sum(widths[:i]) for i in range(len(widths))]
    N = b.shape[0]
    assert b.shape[1] >= sum(widths) and M % tm == 0, (name, widths, b.shape)
    n_p, nm = len(pieces), M // tm
    extra = list(norm_bwd) if norm_bwd is not None else []
    n_in = n_p + 1 + len(extra)
    n_out = len(extra) if norm_bwd is not None else 1

    def body(*refs):
        i = pl.program_id(0)
        if exchange is None:
            ins, outs = refs[:n_in], refs[n_in:]
        else:
            ins, outs, _, xrefs = exchange.split(refs, n_in, n_out)
            _ride(exchange, i == 0, i == nm - 1, xrefs)
        b_ref = ins[n_p]
        acc = None
        for a_ref, off, w in zip(ins[:n_p], offsets, widths):
            term = _nt_dot(a_ref[...].astype(BF16), b_ref[:, off:off + w])
            acc = term if acc is None else acc + term
        if norm_bwd is None:
            outs[0][...] = acc
        else:
            _norm_bwd_tile(acc, tuple(ins[n_p + 1:]), tuple(outs), first=i == 0)

    row, vec, part = pl.BlockSpec((tm, N), lambda i: (i, 0)), _full((1, N)), _full((SUBLANES, N))
    part_shape = jax.ShapeDtypeStruct((SUBLANES, N), F32)
    in_specs = [pl.BlockSpec((tm, w), lambda i: (i, 0)) for w in widths] + [_full(b.shape)]
    out_specs, out_shape = [row], [jax.ShapeDtypeStruct((M, N), F32)]
    if norm_bwd is not None:
        in_specs += [row, row, vec, vec] + ([row, vec] if len(extra) == 6 else [])
        out_specs += [part] * 3 + ([row, part] if len(extra) == 6 else [])
        out_shape += [part_shape] * 3 + ([jax.ShapeDtypeStruct((M, N), BF16), part_shape] if len(extra) == 6 else [])
    sequential = exchange is not None or norm_bwd is not None
    xs = exchange.xs if exchange is not None else []
    result = pl.pallas_call(
        body, name=name, grid=(nm,), in_specs=in_specs + (exchange.in_specs if exchange else []),
        out_specs=out_specs + (exchange.out_specs if exchange else []),
        out_shape=out_shape + (exchange.out_shapes if exchange else []),
        scratch_shapes=exchange.scratch if exchange else [],
        compiler_params=_params(("arbitrary" if sequential else "parallel",)),
    )(*pieces, b, *extra, *xs)
    return result[0] if len(result) == 1 else result


def _matmul_tn_pieces(a, pieces, *, name, tk):
    K, M = a.shape
    widths = [p.shape[1] for p in pieces]
    n_p, nk = len(pieces), K // tk

    def body(*refs):
        a_ref, p_refs, o_refs = refs[0], refs[1:n_p + 1], refs[n_p + 1:]
        k = pl.program_id(0)
        av = a_ref[...].astype(BF16)
        for p_ref, o_ref in zip(p_refs, o_refs):
            prod = _tn_dot(av, p_ref[...].astype(BF16))

            @pl.when(k == 0)
            def _():
                o_ref[...] = prod

            @pl.when(k > 0)
            def _():
                o_ref[...] += prod

    return pl.pallas_call(
        body, name=name, grid=(nk,),
        in_specs=[pl.BlockSpec((tk, M), lambda k: (k, 0))] + [pl.BlockSpec((tk, w), lambda k: (k, 0)) for w in widths],
        out_specs=[_full((M, w)) for w in widths], out_shape=[jax.ShapeDtypeStruct((M, w), F32) for w in widths],
        compiler_params=_params(("arbitrary",)),
    )(a, *pieces)


def _split_dot(x, mat, parts):
    out = None
    rem = x
    for p in range(parts):
        piece = rem.astype(BF16)
        term = jnp.dot(piece, mat, preferred_element_type=F32)
        out = term if out is None else out + term
        if p + 1 < parts:
            rem = rem - piece.astype(F32)
    return out


def _sigmoid(x):
    return 0.5 * jnp.tanh(0.5 * x) + 0.5


def _rows8(x):
    r, c = x.shape
    return jnp.sum(x.reshape(r // SUBLANES, SUBLANES, c), axis=0)


def _shift_down(blk, prev8, n):
    rolled = pltpu.roll(blk, n, axis=0)
    prev_rolled = pltpu.roll(prev8, n, axis=0)
    rows = lax.broadcasted_iota(jnp.int32, prev8.shape, 0)
    first = jnp.where(rows < n, prev_rolled, rolled[0:SUBLANES])
    return jnp.concatenate([first, rolled[SUBLANES:]], axis=0)


def _prev_spec(tm, width, col):
    per = tm // SUBLANES
    return pl.BlockSpec((SUBLANES, width), lambda i, *_: (jnp.maximum(i * per - 1, 0), col))


def _next_spec(tm, width, col, n_tiles):
    per = tm // SUBLANES
    last = n_tiles * per - 1
    return pl.BlockSpec((SUBLANES, width), lambda i, *_: (jnp.minimum((i + 1) * per, last), col))


def _group_matrix():
    idx = np.arange(ATTN_W) // HEAD_DIM
    return jnp.asarray((idx[:, None] == idx[None, :]).astype(np.float32), BF16)


def _norm_mod(x, g, sc, sh, *, name):
    T = x.shape[0]
    tm = min(TOKEN_TILE, T)

    def body(x_ref, g_ref, sc_ref, sh_ref, o_ref):
        xv = x_ref[...]
        inv = lax.rsqrt(jnp.mean(xv * xv, axis=-1, keepdims=True) + RMS_EPS)
        o_ref[...] = ((xv * inv) * g_ref[...] * (1.0 + sc_ref[...]) + sh_ref[...]).astype(BF16)

    row = pl.BlockSpec((tm, D), lambda i: (i, 0))
    return pl.pallas_call(
        body, name=name, grid=(T // tm,),
        in_specs=[row, _full((1, D)), _full((1, D)), _full((1, D))],
        out_specs=row, out_shape=jax.ShapeDtypeStruct((T, D), BF16),
        compiler_params=_params(("parallel",)),
    )(x, g, sc, sh)


LANE_ONE = 64
LANE_F = 67
LANE_LSE = 70
LANE_SUM = 73


def _pieces(x):
    hi = x.astype(BF16).astype(F32)
    rest = x - hi
    mid = rest.astype(BF16).astype(F32)
    return hi, mid, rest - mid


def _run(start, vals):
    return [(start + i, v) for i, v in enumerate(vals)]


def _head_lanes(a, h):
    blk = a[:, LANES * (h // 2):LANES * (h // 2) + LANES]
    return blk if h % 2 == 0 else pltpu.roll(blk, HEAD_DIM, axis=1)


def _branch_prep(proj, fcum, conv_w8, qg, kg, gmat):
    T = proj.shape[0]
    tm = min(TOKEN_TILE, T)
    nt = T // tm

    def body(cb_ref, cc_ref, cv_ref, q_ref, k_ref, v_ref, f_ref, ccp_ref, cvp_ref, w_ref, qg_ref, kg_ref, g_ref,
             ya_ref, qa_ref, ka_ref, va_ref):
        i = pl.program_id(0)
        z = cc_ref[...] * cv_ref[...]
        zp = jnp.where(i > 0, ccp_ref[...] * cvp_ref[...], 0.0)
        w = w_ref[...]
        cz = _shift_down(z, zp, 2) * w[0:1] + _shift_down(z, zp, 1) * w[1:2] + z * w[2:3]
        ya_ref[...] = (cb_ref[...] * cz).astype(BF16)
        gm = g_ref[...]

        def normed(src, gain, scale):
            v = src[...]
            ms = _split_dot(v * v, gm, 2) * (1.0 / HEAD_DIM)
            return (v * lax.rsqrt(ms + RMS_EPS)) * gain[...] * scale

        qn = normed(q_ref, qg_ref, 1.0 / np.sqrt(HEAD_DIM))
        kn = normed(k_ref, kg_ref, 1.0)
        vv = v_ref[...]
        lane = lax.broadcasted_iota(jnp.int32, (tm, LANES), 1)
        low = lane < HEAD_DIM
        in_run = lambda start: jnp.logical_and(lane >= start, lane < start + 3)
        q_ones = jnp.where(jnp.logical_or(in_run(LANE_ONE), lane == LANE_SUM), 1.0, 0.0)
        k_ones = jnp.where(jnp.logical_or(in_run(LANE_F), in_run(LANE_LSE)), 1.0, 0.0)
        v_ones = jnp.where(in_run(LANE_ONE), 1.0, 0.0)
        f3 = jnp.concatenate(_pieces(f_ref[...]), axis=1).astype(BF16)
        src = lax.broadcasted_iota(jnp.int32, (3 * LANES, LANES), 0)
        dst = lax.broadcasted_iota(jnp.int32, (3 * LANES, LANES), 1)
        for h in range(N_HEADS):
            def pick(start, h=h):
                hit = jnp.logical_and(src - h == (dst - start) * LANES, jnp.logical_and(dst >= start, dst < start + 3))
                return jnp.dot(f3, jnp.where(hit, 1.0, 0.0).astype(BF16), preferred_element_type=F32)

            qa_ref[h] = jnp.where(low, _head_lanes(qn, h), q_ones + pick(LANE_F)).astype(BF16)
            ka_ref[h] = jnp.where(low, _head_lanes(kn, h), k_ones - pick(LANE_ONE)).astype(BF16)
            va_ref[h] = jnp.where(low, _head_lanes(vv, h), v_ones).astype(BF16)

    blk = lambda col: pl.BlockSpec((tm, CONV_W), lambda i: (i, col))
    heads = pl.BlockSpec((N_HEADS, tm, LANES), lambda i: (0, i, 0))
    return pl.pallas_call(
        body, name="branch_prep", grid=(nt,),
        in_specs=[blk(0), blk(1), blk(2), blk(3), blk(4), blk(5), pl.BlockSpec((tm, F_PAD), lambda i: (i, 0)),
                  _prev_spec(tm, CONV_W, 1), _prev_spec(tm, CONV_W, 2),
                  _full((SUBLANES, CONV_W)), _full((1, ATTN_W)), _full((1, ATTN_W)), _full((ATTN_W, ATTN_W))],
        out_specs=[pl.BlockSpec((tm, CONV_W), lambda i: (i, 0)), heads, heads, heads],
        out_shape=[jax.ShapeDtypeStruct((T, CONV_W), BF16)] + [jax.ShapeDtypeStruct((N_HEADS, T, LANES), BF16)] * 3,
        compiler_params=_params(("parallel",)),
    )(proj, proj, proj, proj, proj, proj, fcum, proj, proj, conv_w8, qg, kg, gmat)


def _cumsum(x, *, reverse, name, col=0, gate_bias=None):
    T = x.shape[0]
    tm = min(TOKEN_TILE, T)
    nt = T // tm

    def body(x_ref, b_ref, o_ref, carry_ref):
        i = pl.program_id(0)

        @pl.when(i == 0)
        def _():
            carry_ref[...] = jnp.zeros_like(carry_ref)

        r = lax.broadcasted_iota(jnp.int32, (tm, tm), 0)
        c = lax.broadcasted_iota(jnp.int32, (tm, tm), 1)
        tri = jnp.where((c >= r) if reverse else (c <= r), 1.0, 0.0).astype(BF16)
        xv = x_ref[...]
        if gate_bias is not None:
            fx = xv + b_ref[...]
            xv = jnp.minimum(fx, 0.0) - jnp.log(1.0 + jnp.exp(-jnp.abs(fx)))
        out = _split_dot_left(tri, xv, 3) + carry_ref[0:1]
        o_ref[...] = out
        carry_ref[...] = jnp.broadcast_to(out[0:1] if reverse else out[tm - 1:tm], carry_ref.shape)

    rows = (lambda i: nt - 1 - i) if reverse else (lambda i: i)
    bias = jnp.zeros((1, F_PAD), F32) if gate_bias is None else gate_bias
    return pl.pallas_call(
        body, name=name, grid=(nt,),
        in_specs=[pl.BlockSpec((tm, F_PAD), lambda i: (rows(i), col)), _full((1, F_PAD))],
        out_specs=pl.BlockSpec((tm, F_PAD), lambda i: (rows(i), 0)),
        out_shape=jax.ShapeDtypeStruct((T, F_PAD), F32),
        scratch_shapes=[pltpu.VMEM((SUBLANES, F_PAD), F32)],
        compiler_params=_params(("arbitrary",)),
    )(x, bias)


def _split_dot_left(mat, x, parts):
    out = None
    rem = x
    for p in range(parts):
        piece = rem.astype(BF16)
        term = jnp.dot(mat, piece, preferred_element_type=F32)
        out = term if out is None else out + term
        if p + 1 < parts:
            rem = rem - piece.astype(F32)
    return out


def _out_resid_norm(x, merged, w_out, g1, g, sc, sh):
    T = x.shape[0]
    tm = min(TOKEN_TILE, T)

    def body(x_ref, m_ref, w_ref, g1_ref, g_ref, sc_ref, sh_ref, mix_ref, x1_ref, h_ref):
        mix = jnp.dot(m_ref[...], w_ref[...], preferred_element_type=F32)
        mix_ref[...] = mix
        x1 = x_ref[...] + g1_ref[...] * mix
        x1_ref[...] = x1
        inv = lax.rsqrt(jnp.mean(x1 * x1, axis=-1, keepdims=True) + RMS_EPS)
        h_ref[...] = ((x1 * inv) * g_ref[...] * (1.0 + sc_ref[...]) + sh_ref[...]).astype(BF16)

    row = pl.BlockSpec((tm, D), lambda i: (i, 0))
    vec = _full((1, D))
    return pl.pallas_call(
        body, name="out_resid_norm", grid=(T // tm,),
        in_specs=[row, row, _full((D, D)), vec, vec, vec, vec], out_specs=[row, row, row],
        out_shape=[jax.ShapeDtypeStruct((T, D), F32), jax.ShapeDtypeStruct((T, D), F32),
                   jax.ShapeDtypeStruct((T, D), BF16)],
        compiler_params=_params(("parallel",)),
    )(x, merged, w_out, g1, g, sc, sh)


FFN_TM = 256
FFN_TC = 1408


def _ffn_act_fwd(u, w8):
    T = u.shape[0]
    tm = min(FFN_TM, T)
    nt = T // tm
    nc = D_FF // FFN_TC

    def body(ug_ref, uv_ref, ugp_ref, uvp_ref, wg_ref, wv_ref, o_ref, cg_ref, cv_ref):
        i = pl.program_id(1)

        def conv(u_ref, p_ref, w_ref):
            uv = u_ref[...]
            up = jnp.where(i > 0, p_ref[...], 0.0)
            w = w_ref[...]
            return _shift_down(uv, up, 2) * w[0:1] + _shift_down(uv, up, 1) * w[1:2] + uv * w[2:3]

        gate = conv(ug_ref, ugp_ref, wg_ref)
        val = conv(uv_ref, uvp_ref, wv_ref)
        cg_ref[...] = gate.astype(BF16)
        cv_ref[...] = val.astype(BF16)
        o_ref[...] = (gate * _sigmoid(gate) * val).astype(BF16)

    per = tm // SUBLANES
    blk = lambda off: pl.BlockSpec((tm, FFN_TC), lambda j, i: (i, j + off))
    prev = lambda off: pl.BlockSpec((SUBLANES, FFN_TC), lambda j, i: (jnp.maximum(i * per - 1, 0), j + off))
    wblk = lambda off: pl.BlockSpec((SUBLANES, FFN_TC), lambda j, i: (0, j + off))
    return pl.pallas_call(
        body, name="ffn_act_fwd", grid=(nc, nt),
        in_specs=[blk(0), blk(nc), prev(0), prev(nc), wblk(0), wblk(nc)],
        out_specs=[blk(0), blk(0), blk(0)],
        out_shape=[jax.ShapeDtypeStruct((T, D_FF), BF16)] * 3,
        compiler_params=_params(("parallel", "parallel")),
    )(u, u, u, u, w8, w8)


def _down_loss_head(x1, act, w_down, g2, target):
    T = x1.shape[0]
    tm = min(TOKEN_TILE, T)

    def body(x1_ref, a_ref, w_ref, g2_ref, t_ref, dy_ref, dff_ref, loss_ref, dg2_ref):
        i = pl.program_id(0)

        @pl.when(i == 0)
        def _():
            loss_ref[...] = jnp.zeros_like(loss_ref)
            dg2_ref[...] = jnp.zeros_like(dg2_ref)

        ff = jnp.dot(a_ref[...], w_ref[...], preferred_element_type=F32)
        err = x1_ref[...] + g2_ref[...] * ff - t_ref[...]
        dy = err * (1.0 / D)
        dy_ref[...] = dy
        dff_ref[...] = (dy * g2_ref[...]).astype(BF16)
        loss_ref[...] += _rows8(err * err)
        dg2_ref[...] += _rows8(dy * ff)

    row = pl.BlockSpec((tm, D), lambda i: (i, 0))
    acc = _full((SUBLANES, D))
    return pl.pallas_call(
        body, name="down_loss_head", grid=(T // tm,),
        in_specs=[row, pl.BlockSpec((tm, D_FF), lambda i: (i, 0)), _full((D_FF, D)), _full((1, D)), row],
        out_specs=[row, row, acc, acc],
        out_shape=[jax.ShapeDtypeStruct((T, D), F32), jax.ShapeDtypeStruct((T, D), BF16),
                   jax.ShapeDtypeStruct((SUBLANES, D), F32), jax.ShapeDtypeStruct((SUBLANES, D), F32)],
        compiler_params=_params(("arbitrary",)),
    )(x1, act, w_down, g2, target)


def _nt_dot(a, b):
    return lax.dot_general(a, b, (((1,), (1,)), ((), ())), preferred_element_type=F32)


def _causal(n, keys_on_rows=False):
    r = lax.broadcasted_iota(jnp.int32, (n, n), 0)
    c = lax.broadcasted_iota(jnp.int32, (n, n), 1)
    return (c >= r) if keys_on_rows else (c <= r)


def _sweep(lo, hi, step, carry, group=2):
    while group >= 1:
        def several(j, cr, lo=lo, group=group):
            for g in range(group):
                cr = step(lo + group * j + g, cr)
            return cr

        passes = (hi - lo) // group
        carry = lax.fori_loop(0, passes, several, carry)
        lo = lo + group * passes
        group //= 2
    return carry


def _grid_ends(n0, n1):
    i0, i1 = pl.program_id(0), pl.program_id(1)
    return jnp.logical_and(i0 == 0, i1 == 0), jnp.logical_and(i0 == n0 - 1, i1 == n1 - 1)


def _attn_fwd(qa, ka, va, exchange=None):
    nh, T, _ = qa.shape
    bq = min(ATTN_BLOCK, T)
    nq = T // bq

    def body(*refs):
        if exchange is None:
            q_ref, k_ref, v_ref, o_ref, qb_ref = refs
        else:
            (q_ref, k_ref, v_ref), (o_ref, qb_ref), _, xrefs = exchange.split(refs, 3, 2)
            _ride(exchange, *_grid_ends(nh, nq), xrefs)
        qi = pl.program_id(1)
        q = q_ref[0]

        def step(kb, carry, masked=False):
            m, acc = carry
            start = pl.multiple_of(kb * bq, bq)
            s = _nt_dot(q, k_ref[0, pl.ds(start, bq), :])
            if masked:
                s = jnp.where(_causal(bq), s, NEG_INF)
            m_new = jnp.maximum(m, jnp.max(s, axis=-1, keepdims=True))
            p = jnp.exp(s - m_new).astype(BF16)
            acc = jnp.exp(m - m_new) * acc + jnp.dot(p, v_ref[0, pl.ds(start, bq), :], preferred_element_type=F32)
            return m_new, acc

        init = (jnp.full((bq, 1), NEG_INF, F32), jnp.zeros((bq, LANES), F32))
        m, acc = step(qi, _sweep(0, qi, step, init, group=4), masked=True)
        l = acc[:, LANE_ONE:LANE_ONE + 1]
        o_ref[0] = acc / l
        lane = lax.broadcasted_iota(jnp.int32, (bq, LANES), 1)
        qf = q.astype(F32)
        for idx, piece in _run(LANE_LSE, _pieces(m + jnp.log(l))):
            qf = jnp.where(lane == idx, -piece, qf)
        qb_ref[0] = qf.astype(BF16)

    tile = pl.BlockSpec((1, bq, LANES), lambda h, i: (h, i, 0))
    whole = pl.BlockSpec((1, T, LANES), lambda h, i: (h, 0, 0))
    out_shape = [jax.ShapeDtypeStruct((nh, T, LANES), F32), jax.ShapeDtypeStruct((nh, T, LANES), BF16)]
    if exchange is None:
        return pl.pallas_call(
            body, name="attn_fwd", grid=(nh, nq), in_specs=[tile, whole, whole], out_specs=[tile, tile],
            out_shape=out_shape, compiler_params=_params(("parallel", "parallel")),
        )(qa, ka, va)
    return pl.pallas_call(
        body, name="attn_fwd", grid=(nh, nq), in_specs=[tile, whole, whole] + exchange.in_specs,
        out_specs=[tile, tile] + exchange.out_specs, out_shape=out_shape + exchange.out_shapes,
        scratch_shapes=exchange.scratch, compiler_params=_params(("arbitrary", "arbitrary")),
    )(qa, ka, va, *exchange.xs)


def _branch_merge_fwd(ya0, o_h, proj, wba, wbb_heads):
    nh, T, _ = o_h.shape
    tm = min(TOKEN_TILE, T)

    def body(ya0_ref, o_ref, ga_ref, gb_ref, wa_ref, wb_ref, ya_ref, yb_ref, m_ref):
        ya = jnp.dot(ya0_ref[...], wa_ref[...], preferred_element_type=F32)
        yb = jnp.dot(o_ref[0].astype(BF16), wb_ref[0:LANES, :], preferred_element_type=F32)
        for h in range(1, nh):
            yb += jnp.dot(o_ref[h].astype(BF16), wb_ref[h * LANES:(h + 1) * LANES, :], preferred_element_type=F32)
        ya_ref[...] = ya.astype(BF16)
        yb_ref[...] = yb.astype(BF16)
        m_ref[...] = (_sigmoid(ga_ref[...]) * ya + _sigmoid(gb_ref[...]) * yb).astype(BF16)

    row = pl.BlockSpec((tm, D), lambda i: (i, 0))
    return pl.pallas_call(
        body, name="branch_merge_fwd", grid=(T // tm,),
        in_specs=[pl.BlockSpec((tm, CONV_W), lambda i: (i, 0)), pl.BlockSpec((nh, tm, LANES), lambda i: (0, i, 0)),
                  pl.BlockSpec((tm, D), lambda i: (i, COL_GA // D)), pl.BlockSpec((tm, D), lambda i: (i, COL_GB // D)),
                  _full((CONV_W, D)), _full((nh * LANES, D))],
        out_specs=[row, row, row],
        out_shape=[jax.ShapeDtypeStruct((T, D), BF16)] * 3,
        compiler_params=_params(("parallel",)),
    )(ya0, o_h, proj, proj, wba, wbb_heads)


def _branch_b_bwd(dyb, o_h, wbb_heads_t):
    nh, T, _ = o_h.shape
    tm = min(TOKEN_TILE, T)

    def body(dyb_ref, o_ref, w_ref, out_ref):
        do = jnp.dot(dyb_ref[...], w_ref[...], preferred_element_type=F32)
        lane = lax.broadcasted_iota(jnp.int32, (tm, LANES), 1)
        for h in range(nh):
            g = do[:, h * LANES:(h + 1) * LANES].astype(BF16).astype(F32)
            delta = jnp.sum(g * o_ref[h], axis=-1, keepdims=True)
            for idx, piece in _run(LANE_ONE, _pieces(delta)):
                g = jnp.where(lane == idx, -piece, g)
            out_ref[h] = g.astype(BF16)

    heads = pl.BlockSpec((nh, tm, LANES), lambda i: (0, i, 0))
    return pl.pallas_call(
        body, name="branch_b_bwd", grid=(T // tm,),
        in_specs=[pl.BlockSpec((tm, D), lambda i: (i, 0)), heads, _full((D, nh * LANES))],
        out_specs=heads, out_shape=jax.ShapeDtypeStruct((nh, T, LANES), BF16),
        compiler_params=_params(("parallel",)),
    )(dyb, o_h, wbb_heads_t)


def _branch_b_dw(o_h, dyb):
    nh, T, _ = o_h.shape
    tk = min(TOKEN_TILE, T)

    def body(o_ref, dyb_ref, out_ref):
        @pl.when(pl.program_id(0) == 0)
        def _():
            out_ref[...] = jnp.zeros_like(out_ref)

        g = dyb_ref[...]
        for h in range(nh):
            out_ref[h] += _tn_dot(o_ref[h].astype(BF16), g)

    return pl.pallas_call(
        body, name="branch_b_dw", grid=(T // tk,),
        in_specs=[pl.BlockSpec((nh, tk, LANES), lambda k: (0, k, 0)), pl.BlockSpec((tk, D), lambda k: (k, 0))],
        out_specs=_full((nh, LANES, D)), out_shape=jax.ShapeDtypeStruct((nh, LANES, D), F32),
        compiler_params=_params(("arbitrary",)),
    )(o_h, dyb)


def _attn_bwd(qb, ka, va, doa, exchange=None):
    nh, T, _ = qb.shape
    bk = min(ATTN_BLOCK, T)
    nk = T // bk

    def body(*refs):
        if exchange is None:
            q_ref, do_ref, k_ref, v_ref, dq_ref, dk_ref, dv_ref = refs
        else:
            (q_ref, do_ref, k_ref, v_ref), (dq_ref, dk_ref, dv_ref), _, xrefs = exchange.split(refs, 4, 3)
            _ride(exchange, *_grid_ends(nh, nk), xrefs)
        ki = pl.program_id(1)

        @pl.when(ki == 0)
        def _():
            dq_ref[...] = jnp.zeros_like(dq_ref)

        k = k_ref[0]
        v = v_ref[0]

        def step(qi, carry, masked):
            dk, dv = carry
            rows = pl.ds(pl.multiple_of(qi * bk, bk), bk)
            q = q_ref[0, rows, :]
            g = do_ref[0, rows, :]
            pt = jnp.exp(_nt_dot(k, q))
            if masked:
                pt = jnp.where(_causal(bk, keys_on_rows=True), pt, 0.0)
            dv = dv + jnp.dot(pt.astype(BF16), g, preferred_element_type=F32)
            dst = (pt * _nt_dot(v, g)).astype(BF16)
            dk = dk + jnp.dot(dst, q, preferred_element_type=F32)
            dq_ref[0, rows, :] += _tn_dot(dst, k)
            return dk, dv

        init = (jnp.zeros((bk, LANES), F32), jnp.zeros((bk, LANES), F32))
        carry = step(ki, init, True)
        dk_ref[0], dv_ref[0] = _sweep(ki + 1, nk, lambda qi, cr: step(qi, cr, False), carry)

    tile = pl.BlockSpec((1, bk, LANES), lambda h, i: (h, i, 0))
    whole = pl.BlockSpec((1, T, LANES), lambda h, i: (h, 0, 0))
    out_shape = [jax.ShapeDtypeStruct((nh, T, LANES), F32)] * 3
    if exchange is None:
        return pl.pallas_call(
            body, name="attn_bwd", grid=(nh, nk), in_specs=[whole, whole, tile, tile],
            out_specs=[whole, tile, tile], out_shape=out_shape, compiler_params=_params(("parallel", "arbitrary")),
        )(qb, doa, ka, va)
    return pl.pallas_call(
        body, name="attn_bwd", grid=(nh, nk), in_specs=[whole, whole, tile, tile] + exchange.in_specs,
        out_specs=[whole, tile, tile] + exchange.out_specs, out_shape=out_shape + exchange.out_shapes,
        scratch_shapes=exchange.scratch, compiler_params=_params(("arbitrary", "arbitrary")),
    )(qb, doa, ka, va, *exchange.xs)


def _attn_unpack(dq_h, dk_h, dv_h):
    nh, T, _ = dq_h.shape
    tm = min(TOKEN_TILE, T)

    def body(dq_ref, dk_ref, dv_ref, q_out, k_out, v_out, f_out):
        lane = lax.broadcasted_iota(jnp.int32, (tm, LANES), 1)
        low = lane < HEAD_DIM
        for src, dst in ((dq_ref, q_out), (dk_ref, k_out), (dv_ref, v_out)):
            for pair in range(nh // 2):
                both = jnp.where(low, src[2 * pair], pltpu.roll(src[2 * pair + 1], HEAD_DIM, axis=1))
                dst[:, LANES * pair:LANES * (pair + 1)] = both.astype(dst.dtype)
        df = jnp.zeros((tm, LANES), F32)
        for h in range(nh):
            col = dq_ref[h][:, LANE_F:LANE_F + 1] - dk_ref[h][:, LANE_SUM:LANE_SUM + 1]
            df = jnp.where(lane == h, col, df)
        f_out[...] = df

    heads = pl.BlockSpec((nh, tm, LANES), lambda i: (0, i, 0))
    tok = pl.BlockSpec((tm, ATTN_W), lambda i: (i, 0))
    return pl.pallas_call(
        body, name="attn_unpack", grid=(T // tm,), in_specs=[heads, heads, heads],
        out_specs=[tok, tok, tok, pl.BlockSpec((tm, F_PAD), lambda i: (i, 0))],
        out_shape=[jax.ShapeDtypeStruct((T, ATTN_W), BF16)] * 3 + [jax.ShapeDtypeStruct((T, F_PAD), F32)],
        compiler_params=_params(("parallel",)),
    )(dq_h, dk_h, dv_h)


def _ffn_act_bwd(u, cg, cv, da, w8):
    T = u.shape[0]
    tm = min(FFN_TM, T)
    nt = T // tm
    nc = D_FF // FFN_TC

    def body(ug_ref, uv_ref, cg_ref, cv_ref, cgn_ref, cvn_ref, da_ref, dan_ref, wg_ref, wv_ref,
             dug_ref, duv_ref, dwg_ref, dwv_ref):
        i = pl.program_id(1)

        @pl.when(i == 0)
        def _():
            dwg_ref[...] = jnp.zeros_like(dwg_ref)
            dwv_ref[...] = jnp.zeros_like(dwv_ref)

        gate = jnp.concatenate([cg_ref[...], cgn_ref[...]], axis=0).astype(F32)
        val = jnp.concatenate([cv_ref[...], cvn_ref[...]], axis=0).astype(F32)
        dae = jnp.concatenate([da_ref[...], dan_ref[...]], axis=0).astype(F32)
        rows_e = lax.broadcasted_iota(jnp.int32, dae.shape, 0)
        dae = jnp.where(jnp.logical_and(i == nt - 1, rows_e >= tm), 0.0, dae)
        sg = _sigmoid(gate)
        n = tm + BF16_ROWS

        def back(d, u_ref, w_ref, du_ref, dw_ref):
            w = w_ref[...]
            uv = u_ref[...]
            d1 = pltpu.roll(d, n - 1, axis=0)[:tm]
            d2 = pltpu.roll(d, n - 2, axis=0)[:tm]
            d0 = d[:tm]
            du_ref[...] = (d0 * w[2:3] + d1 * w[1:2] + d2 * w[0:1]).astype(BF16)
            rows = [jnp.sum(t * uv, axis=0, keepdims=True) for t in (d2, d1, d0)]
            dw_ref[...] += jnp.concatenate(rows + [jnp.zeros((SUBLANES - 3, FFN_TC), F32)], axis=0)

        back(dae * val * sg * (1.0 + gate * (1.0 - sg)), ug_ref, wg_ref, dug_ref, dwg_ref)
        back(dae * gate * sg, uv_ref, wv_ref, duv_ref, dwv_ref)

    per = tm // BF16_ROWS
    last_blk = nt * per - 1
    blk = lambda off: pl.BlockSpec((tm, FFN_TC), lambda j, i: (i, j + off))
    nxt = pl.BlockSpec((BF16_ROWS, FFN_TC), lambda j, i: (jnp.minimum((i + 1) * per, last_blk), j))
    wblk = lambda off: pl.BlockSpec((SUBLANES, FFN_TC), lambda j, i: (0, j + off))
    dug, duv, dwg, dwv = pl.pallas_call(
        body, name="ffn_act_bwd", grid=(nc, nt),
        in_specs=[blk(0), blk(nc), blk(0), blk(0), nxt, nxt, blk(0), nxt, wblk(0), wblk(nc)],
        out_specs=[blk(0), blk(0), wblk(0), wblk(0)],
        out_shape=[jax.ShapeDtypeStruct((T, D_FF), BF16)] * 2 + [jax.ShapeDtypeStruct((SUBLANES, D_FF), F32)] * 2,
        compiler_params=_params(("parallel", "arbitrary")),
    )(u, u, cg, cv, cg, cv, da, da, w8, w8)
    return dug, duv, jnp.concatenate([dwg, dwv], axis=1)


def _out_merge_bwd(dmix, w_out_t, ya, yb, proj):
    T = ya.shape[0]
    tm = min(TOKEN_TILE, T)

    def body(dmix_ref, w_ref, ya_ref, yb_ref, ga_ref, gb_ref, dya_ref, dyb_ref, dga_ref, dgb_ref):
        dm = jnp.dot(dmix_ref[...], w_ref[...], preferred_element_type=F32)
        sa = _sigmoid(ga_ref[...])
        sb = _sigmoid(gb_ref[...])
        dya_ref[...] = (dm * sa).astype(BF16)
        dyb_ref[...] = (dm * sb).astype(BF16)
        dga_ref[...] = (dm * ya_ref[...].astype(F32) * sa * (1.0 - sa)).astype(BF16)
        dgb_ref[...] = (dm * yb_ref[...].astype(F32) * sb * (1.0 - sb)).astype(BF16)

    row = pl.BlockSpec((tm, D), lambda i: (i, 0))
    return pl.pallas_call(
        body, name="out_merge_bwd", grid=(T // tm,),
        in_specs=[row, _full((D, D)), row, row, pl.BlockSpec((tm, D), lambda i: (i, COL_GA // D)),
                  pl.BlockSpec((tm, D), lambda i: (i, COL_GB // D))],
        out_specs=[row] * 4, out_shape=[jax.ShapeDtypeStruct((T, D), BF16)] * 4,
        compiler_params=_params(("parallel",)),
    )(dmix, w_out_t, ya, yb, proj, proj)


def _conv_branch_bwd(proj, dya0, conv_w8):
    T = proj.shape[0]
    tm = min(FFN_TM, T)
    nt = T // tm

    def body(cb_ref, cc_ref, cv_ref, cbn_ref, ccp_ref, cvp_ref, ccn_ref, cvn_ref, d_ref, dn_ref, w_ref,
             d3_ref, dw_ref):
        i = pl.program_id(0)

        @pl.when(i == 0)
        def _():
            dw_ref[...] = jnp.zeros_like(dw_ref)

        first, last = i == 0, i == nt - 1
        w = w_ref[...]
        cc = jnp.concatenate([ccp_ref[...], cc_ref[...], ccn_ref[...]], axis=0)
        cv = jnp.concatenate([cvp_ref[...], cv_ref[...], cvn_ref[...]], axis=0)
        rows = lax.broadcasted_iota(jnp.int32, cc.shape, 0)
        z = jnp.where(jnp.logical_and(first, rows < SUBLANES), 0.0, cc * cv)
        z1 = pltpu.roll(z, 1, axis=0)
        z2 = pltpu.roll(z, 2, axis=0)
        cz = z2 * w[0:1] + z1 * w[1:2] + z * w[2:3]
        zeros8 = jnp.zeros((SUBLANES, CONV_W), F32)
        de = jnp.concatenate([zeros8, d_ref[...], jnp.where(last, 0.0, dn_ref[...])], axis=0)
        cbe = jnp.concatenate([zeros8, cb_ref[...], cbn_ref[...]], axis=0)
        dcz = de * cbe
        n = tm + 2 * SUBLANES
        dz = dcz * w[2:3] + pltpu.roll(dcz, n - 1, axis=0) * w[1:2] + pltpu.roll(dcz, n - 2, axis=0) * w[0:1]
        inner = slice(SUBLANES, SUBLANES + tm)
        d3_ref[:, 0:CONV_W] = (de * cz)[inner].astype(BF16)
        d3_ref[:, CONV_W:2 * CONV_W] = (dz * cv)[inner].astype(BF16)
        d3_ref[:, 2 * CONV_W:3 * CONV_W] = (dz * cc)[inner].astype(BF16)
        wrows = [jnp.sum((dcz * t)[inner], axis=0, keepdims=True) for t in (z2, z1, z)]
        dw_ref[...] += jnp.concatenate(wrows + [jnp.zeros((SUBLANES - 3, CONV_W), F32)], axis=0)

    blk = lambda col: pl.BlockSpec((tm, CONV_W), lambda i: (i, col))
    out_blk = pl.BlockSpec((tm, CONV_W), lambda i: (i, 0))
    return pl.pallas_call(
        body, name="conv_branch_bwd", grid=(nt,),
        in_specs=[blk(0), blk(1), blk(2), _next_spec(tm, CONV_W, 0, nt),
                  _prev_spec(tm, CONV_W, 1), _prev_spec(tm, CONV_W, 2),
                  _next_spec(tm, CONV_W, 1, nt), _next_spec(tm, CONV_W, 2, nt),
                  out_blk, _next_spec(tm, CONV_W, 0, nt), _full((SUBLANES, CONV_W))],
        out_specs=[pl.BlockSpec((tm, 3 * CONV_W), lambda i: (i, 0)), _full((SUBLANES, CONV_W))],
        out_shape=[jax.ShapeDtypeStruct((T, 3 * CONV_W), BF16), jax.ShapeDtypeStruct((SUBLANES, CONV_W), F32)],
        compiler_params=_params(("arbitrary",)),
    )(proj, proj, proj, proj, proj, proj, proj, proj, dya0, dya0, conv_w8)


def _qk_norm_bwd(proj, dqs, dkh, dlogf, qg, kg, bf_pad, gmat):
    T = proj.shape[0]
    tm = min(TOKEN_TILE, T)

    def body(q_ref, k_ref, f_ref, dqs_ref, dkh_ref, dlf_ref, qg_ref, kg_ref, bf_ref, g_ref,
             dqk_ref, dfl_ref, dqg_ref, dkg_ref, dbf_ref):
        @pl.when(pl.program_id(0) == 0)
        def _():
            dqg_ref[...] = jnp.zeros_like(dqg_ref)
            dkg_ref[...] = jnp.zeros_like(dkg_ref)
            dbf_ref[...] = jnp.zeros_like(dbf_ref)

        gm = g_ref[...]
        for src, d_src, gain, scale, dst, dgain in (
                (q_ref, dqs_ref, qg_ref, 1.0 / np.sqrt(HEAD_DIM), dqk_ref.at[:, 0:ATTN_W], dqg_ref),
                (k_ref, dkh_ref, kg_ref, 1.0, dqk_ref.at[:, ATTN_W:2 * ATTN_W], dkg_ref)):
            v = src[...]
            dhat = d_src[...].astype(F32) * scale
            inv = lax.rsqrt(_split_dot(v * v, gm, 2) * (1.0 / HEAD_DIM) + RMS_EPS)
            vn = v * inv
            dgain[...] += _rows8(dhat * vn)
            dvn = dhat * gain[...]
            mean = _split_dot(dvn * vn, gm, 2) * (1.0 / HEAD_DIM)
            dst[...] = (inv * (dvn - vn * mean)).astype(BF16)
        fx = f_ref[...] + bf_ref[...]
        dfl = dlf_ref[...] * _sigmoid(-fx)
        dfl_ref[...] = dfl.astype(BF16)
        dbf_ref[...] += _rows8(dfl)

    blk = lambda col: pl.BlockSpec((tm, ATTN_W), lambda i: (i, col))
    out_blk = pl.BlockSpec((tm, ATTN_W), lambda i: (i, 0))
    f_in = pl.BlockSpec((tm, F_PAD), lambda i: (i, COL_F // F_PAD))
    f_blk = pl.BlockSpec((tm, F_PAD), lambda i: (i, 0))
    return pl.pallas_call(
        body, name="qk_norm_bwd", grid=(T // tm,),
        in_specs=[blk(3), blk(4), f_in, out_blk, out_blk, f_blk, _full((1, ATTN_W)), _full((1, ATTN_W)),
                  _full((1, F_PAD)), _full((ATTN_W, ATTN_W))],
        out_specs=[pl.BlockSpec((tm, 2 * ATTN_W), lambda i: (i, 0)), f_blk, _full((SUBLANES, ATTN_W)),
                   _full((SUBLANES, ATTN_W)), _full((SUBLANES, F_PAD))],
        out_shape=[jax.ShapeDtypeStruct((T, 2 * ATTN_W), BF16), jax.ShapeDtypeStruct((T, F_PAD), BF16)]
        + [jax.ShapeDtypeStruct((SUBLANES, ATTN_W), F32)] * 2 + [jax.ShapeDtypeStruct((SUBLANES, F_PAD), F32)],
        compiler_params=_params(("arbitrary",)),
    )(proj, proj, proj, dqs, dkh, dlogf, qg, kg, bf_pad, gmat)


def _pad_rows8(w):
    return jnp.pad(w, ((0, SUBLANES - w.shape[0]), (0, 0)))


def _fold8(acc):
    return jnp.sum(acc, axis=0, keepdims=True)


def _late_weights(mats):
    out = {"w_up": mats["w_up"]}
    for name in ("w_branch_a", "w_out", "w_down"):
        out[name] = mats[name]
        out[name + "_t"] = mats[name].T
    out["w_branch_b_heads"] = _pad_head_rows(mats["w_branch_b"])
    out["w_branch_b_heads_t"] = out["w_branch_b_heads"].T
    return out


def _local_step(x, target, mod, wts, late=None):
    T = x.shape[0]
    tb = min(MATMUL_TILE, T)
    tk_long = min(2 * MATMUL_TILE, T)
    tm = min(TOKEN_TILE, T)
    sh1, sc1, g1, sh2, sc2, g2 = [mod[:, i * D:(i + 1) * D] for i in range(N_MOD)]
    w_in = wts["w_in"]
    conv_a8 = _pad_rows8(wts["conv_a_w"])
    conv_f8 = _pad_rows8(wts["conv_ffn_w"])
    qg = jnp.tile(wts["q_norm_g"], (1, N_HEADS))
    kg = jnp.tile(wts["k_norm_g"], (1, N_HEADS))
    bf_pad = jnp.pad(wts["b_f"], ((0, 0), (0, F_PAD - N_HEADS)))
    gmat = _group_matrix()

    h = _norm_mod(x, wts["norm1_g"], sc1, sh1, name="norm1_fwd")
    proj = _matmul(h, w_in, name="mm_in", tm=tb, tn=896, tk=D)
    fcum = _cumsum(proj, reverse=False, name="gate_cumsum", col=COL_F // F_PAD, gate_bias=bf_pad)
    ya0, qa, ka, va = _branch_prep(proj, fcum, conv_a8, qg, kg, gmat)
    if late is None:
        o_h, qb = _attn_fwd(qa, ka, va)
    else:
        o_h, qb, *gathered = _attn_fwd(qa, ka, va, _Exchange([late[name] for name, *_ in LATE], scatter=False))
        wts = dict(wts)
        mats = {name: _join_shards(g, axis) for (name, _, _, axis), g in zip(LATE, gathered) if name != "w_up"}
        mats["w_up"] = _assemble_columns(gathered[[name for name, *_ in LATE].index("w_up")], 2 * D_FF // N_DEV,
                                         2 * D_FF, ((0, 2 * D_FF, 0),), name="assemble_w_up")
        wts.update(_late_weights(mats))
    ya, yb, merged = _branch_merge_fwd(ya0, o_h, proj, wts["w_branch_a"], wts["w_branch_b_heads"])
    mix, x1, h2 = _out_resid_norm(x, merged, wts["w_out"], g1, wts["norm2_g"], sc2, sh2)
    u = _matmul(h2, wts["w_up"], name="mm_up", tm=tb, tn=1408, tk=D)
    act, conv_gate, conv_val = _ffn_act_fwd(u, conv_f8)
    dy, dff, sq8, dg2_8 = _down_loss_head(x1, act, wts["w_down"], g2, target)
    sq = jnp.sum(sq8).reshape(1, 1)

    grads = {}
    da = _matmul(dff, wts["w_down_t"], name="mm_down_dx", tm=tb, tn=1408, tk=D, out_dtype=BF16)
    grads["w_down"] = _matmul(act, dff, name="mm_down_dw", tm=1408, tn=D, tk=tk_long, trans_a=True)
    dug, duv, dconv_f8 = _ffn_act_bwd(u, conv_gate, conv_val, da, conv_f8)
    grads["conv_ffn_w"] = dconv_f8[:3]
    dx1, dsh2_8, dsc2_8, dn2_8, dmix, dg1_8 = _matmul_pieces(
        [dug, duv], wts["w_up"], name="mm_up_dx", tm=tm, norm_bwd=(x1, dy, wts["norm2_g"], sc2, mix, g1))
    dw_up = [_matmul(h2, d, name="mm_up_dw_" + half, tm=D, tn=1408, tk=tk_long, trans_a=True)
             for half, d in (("gate", dug), ("val", duv))]
    if late is None:
        grads["w_up"] = jnp.concatenate(dw_up, axis=1)
    grads["norm2_g"] = _fold8(dn2_8)

    grads["w_out"] = _matmul(merged, dmix, name="mm_out_dw", tm=D, tn=D, tk=tk_long, trans_a=True)
    dya, dyb, dga, dgb = _out_merge_bwd(dmix, wts["w_out_t"], ya, yb, proj)
    dya0 = _matmul(dya, wts["w_branch_a_t"], name="mm_branch_a_dx", tm=tb, tn=CONV_W, tk=D)
    grads["w_branch_a"] = _matmul(ya0, dya, name="mm_branch_a_dw", tm=CONV_W, tn=D, tk=tk_long, trans_a=True)
    doa = _branch_b_bwd(dyb, o_h, wts["w_branch_b_heads_t"])
    grads["w_branch_b"] = _branch_b_dw(o_h, dyb)[:, :HEAD_DIM].reshape(ATTN_W, D)
    dconv3, dconv_a8 = _conv_branch_bwd(proj, dya0, conv_a8)
    grads["conv_a_w"] = dconv_a8[:3]

    parts = {}
    if late is None:
        dq_h, dk_h, dv_h = _attn_bwd(qb, ka, va, doa)
    else:
        ready = [(_column_shards(dw_up) if name == "w_up" else _split_shards(grads[name], axis)).astype(BF16)
                 for name, _, _, axis in LATE]
        dq_h, dk_h, dv_h, *recv = _attn_bwd(
            qb, ka, va, doa, _Exchange(ready + [_pack_full_by_dest(grads, CONVS, SUBLANES)], scatter=True))
        parts = dict(zip([name for name, *_ in LATE] + ["conv"], recv))
    dq_tok, dk_tok, dv_tok, dfcum = _attn_unpack(dq_h, dk_h, dv_h)
    dlogf = _cumsum(dfcum, reverse=True, name="gate_cumsum_bwd")
    dqk, dfl, dqg8, dkg8, dbf8 = _qk_norm_bwd(proj, dq_tok, dk_tok, dlogf, qg, kg, bf_pad, gmat)
    grads["q_norm_g"] = jnp.sum(_fold8(dqg8).reshape(N_HEADS, HEAD_DIM), axis=0, keepdims=True)
    grads["k_norm_g"] = jnp.sum(_fold8(dkg8).reshape(N_HEADS, HEAD_DIM), axis=0, keepdims=True)
    grads["b_f"] = _fold8(dbf8)[:, :N_HEADS]
    narrow, wide = [dconv3, dqk, dv_tok], [dga, dgb, dfl]
    dw_narrow = _matmul_tn_pieces(h, narrow, name="mm_in_dw_narrow", tk=tb)
    dwa, dwb, dwf = _matmul_tn_pieces(h, wide, name="mm_in_dw_wide", tk=tk_long)
    dw_in = list(dw_narrow) + [dwf[:, :N_HEADS], dwa, dwb]
    norm1 = (x, dx1, wts["norm1_g"], sc1)
    if late is None:
        grads["w_in"] = jnp.concatenate(dw_in, axis=1)
        grad_x, dsh1_8, dsc1_8, dn1_8 = _matmul_pieces(narrow + wide, w_in, name="mm_in_dx", tm=tm,
                                                       norm_bwd=norm1)
    else:
        grad_x, dsh1_8, dsc1_8, dn1_8, parts["w_in"] = _matmul_pieces(
            narrow + wide, w_in, name="mm_in_dx", tm=tm, norm_bwd=norm1,
            exchange=_Exchange([_column_shards(dw_in).astype(BF16)], scatter=True))
    grads["norm1_g"] = _fold8(dn1_8)
    grads["mod"] = jnp.concatenate([_fold8(a) for a in (dsh1_8, dsc1_8, dg1_8, dsh2_8, dsc2_8, dg2_8)], axis=1)
    return sq, grad_x, grads, parts


def _me_and_peers():
    mx, my, mc = lax.axis_index("x"), lax.axis_index("y"), lax.axis_index("c")
    me = 4 * mx + 2 * my + mc
    peers = []
    for k in range(1, N_DEV):
        px = 1 - mx if k & 4 else mx
        py = 1 - my if k & 2 else my
        pc = 1 - mc if k & 1 else mc
        peers.append(((px, py, pc), 4 * px + 2 * py + pc))
    return me, peers


HBM_SPEC = pl.BlockSpec(memory_space=pltpu.HBM)


class _Exchange:
    def __init__(self, xs, scatter):
        self.xs, self.scatter, self.n = list(xs), scatter, len(xs)
        self.out_shapes = [jax.ShapeDtypeStruct(x.shape if scatter else (N_DEV,) + x.shape, x.dtype) for x in xs]
        self.in_specs = [HBM_SPEC] * self.n
        self.out_specs = [HBM_SPEC] * self.n
        self.scratch = [pltpu.SemaphoreType.DMA((self.n, N_DEV - 1)), pltpu.SemaphoreType.DMA((self.n, N_DEV - 1)),
                        pltpu.SemaphoreType.DMA((self.n,))]

    def _copies(self, x_refs, out_refs, sems):
        send_sems, recv_sems, local_sems = sems
        me, peers = _me_and_peers()

        def src(a, idx):
            return x_refs[a].at[idx] if self.scatter else x_refs[a]

        def copy(a, k, from_idx, to_slot, device):
            return pltpu.make_async_remote_copy(
                src_ref=src(a, from_idx), dst_ref=out_refs[a].at[to_slot], send_sem=send_sems.at[a, k],
                recv_sem=recv_sems.at[a, k], device_id=device, device_id_type=MESH)

        local = [pltpu.make_async_copy(src(a, me), out_refs[a].at[me], local_sems.at[a]) for a in range(self.n)]
        sends = [copy(a, k, idx, me, dev) for a in range(self.n) for k, (dev, idx) in enumerate(peers)]
        recvs = [copy(a, k, idx, idx, dev) for a in range(self.n) for k, (dev, idx) in enumerate(peers)]
        return local, sends, recvs

    def start(self, x_refs, out_refs, sems):
        local, sends, _ = self._copies(x_refs, out_refs, sems)
        for cp in local + sends:
            cp.start()

    def wait(self, x_refs, out_refs, sems):
        local, sends, recvs = self._copies(x_refs, out_refs, sems)
        for cp in recvs:
            cp.wait_recv()
        for cp in sends:
            cp.wait_send()
        for cp in local:
            cp.wait()

    def split(self, refs, n_in, n_out):
        n = self.n
        ins, xin = refs[:n_in], refs[n_in:n_in + n]
        outs, xout = refs[n_in + n:n_in + n + n_out], refs[n_in + n + n_out:n_in + 2 * n + n_out]
        rest = refs[n_in + 2 * n + n_out:]
        return ins, outs, rest[:len(rest) - 3], (xin, xout, rest[len(rest) - 3:])


def _ride(exchange, first, last, refs):
    if exchange is None:
        return

    @pl.when(first)
    def _():
        exchange.start(*refs)

    @pl.when(last)
    def _():
        exchange.wait(*refs)


def _gather_two_level(xs, *, name):
    n = len(xs)
    out_shapes = [jax.ShapeDtypeStruct((N_DEV,) + x.shape, x.dtype) for x in xs]

    def body(*refs):
        x_refs, out_refs = refs[:n], refs[n:2 * n]
        send_sems, recv_sems, local_sems = refs[2 * n:]
        x, y, c = lax.axis_index("x"), lax.axis_index("y"), lax.axis_index("c")
        me, sibling = (x, y, c), (x, y, 1 - c)
        chips = [(1 - x, y), (x, 1 - y), (1 - x, 1 - y)]

        def slot(a, dev):
            return out_refs[a].at[4 * dev[0] + 2 * dev[1] + dev[2]]

        def copy(a, k, block, to, src=None):
            return pltpu.make_async_remote_copy(
                src_ref=slot(a, block) if src is None else src, dst_ref=slot(a, block),
                send_sem=send_sems.at[a, k], recv_sem=recv_sems.at[a, k], device_id=to, device_id_type=MESH)

        mine = [pltpu.make_async_copy(x_refs[a], slot(a, me), local_sems.at[a]) for a in range(n)]
        first = [copy(a, 0, me, sibling, src=x_refs[a]) for a in range(n)]
        first += [copy(a, 1 + j, me, (*chip, c), src=x_refs[a]) for a in range(n) for j, chip in enumerate(chips)]
        for cp in mine + first:
            cp.start()
        passed = []
        for a in range(n):
            for j, chip in enumerate(chips):
                copy(a, 1 + j, (*chip, c), me).wait_recv()
                passed.append(copy(a, 4 + j, (*chip, c), sibling))
                passed[-1].start()
        for a in range(n):
            copy(a, 0, sibling, me).wait_recv()
            for j, chip in enumerate(chips):
                copy(a, 4 + j, (*chip, 1 - c), me).wait_recv()
        for cp in first + passed:
            cp.wait_send()
        for cp in mine:
            cp.wait()

    return pl.pallas_call(
        body, name=name, in_specs=[HBM_SPEC] * n, out_specs=[HBM_SPEC] * n, out_shape=out_shapes,
        scratch_shapes=[pltpu.SemaphoreType.DMA((n, N_DEV - 1)), pltpu.SemaphoreType.DMA((n, N_DEV - 1)),
                        pltpu.SemaphoreType.DMA((n,))],
        compiler_params=pltpu.CompilerParams(has_side_effects=True),
    )(*xs)


def _exchange(xs, *, name, scatter):
    ex = _Exchange(xs, scatter)

    def body(*refs):
        _, _, _, xrefs = ex.split(refs, 0, 0)
        ex.start(*xrefs)
        ex.wait(*xrefs)

    return pl.pallas_call(
        body, name=name, in_specs=ex.in_specs, out_specs=ex.out_specs, out_shape=ex.out_shapes,
        scratch_shapes=ex.scratch, compiler_params=pltpu.CompilerParams(has_side_effects=True),
    )(*xs)


def _ada_fwd(c_all, w_shard, b_shard):
    n = w_shard.shape[1]

    def body(c_ref, w_ref, b_ref, o_ref):
        cv = c_ref[...]
        act = (cv * _sigmoid(cv)).astype(BF16)
        o_ref[...] = jnp.dot(act, w_ref[...].astype(BF16), preferred_element_type=F32) + b_ref[...]

    return pl.pallas_call(
        body, name="ada_fwd", in_specs=[_full((N_DEV, D)), _full((D, n)), _full((1, n))],
        out_specs=_full((N_DEV, n)), out_shape=jax.ShapeDtypeStruct((N_DEV, n), F32), grid=(1,),
        compiler_params=_params(("arbitrary",)),
    )(c_all, w_shard, b_shard)


def _ada_bwd(c_all_t, dmod_pad):
    n = dmod_pad.shape[1]

    def body(c_ref, d_ref, o_ref):
        cv = c_ref[...]
        act = (cv * _sigmoid(cv)).astype(BF16)
        o_ref[...] = jnp.dot(act, d_ref[...].astype(BF16), preferred_element_type=F32)

    return pl.pallas_call(
        body, name="ada_bwd", in_specs=[_full((D, LANES)), _full((LANES, n))],
        out_specs=_full((D, n)), out_shape=jax.ShapeDtypeStruct((D, n), F32), grid=(1,),
        compiler_params=_params(("arbitrary",)),
    )(c_all_t, dmod_pad)


ADAM_ROWS = 256


def _adamw(parts, w, m, v, *, name):
    n, R, C = parts.shape
    tr = next((t for t in (ADAM_ROWS, 128, 64, 32, 16, SUBLANES) if R % t == 0), R)

    def body(p_ref, w_ref, m_ref, v_ref, g_ref, d_ref, nm_ref, nv_ref):
        g = p_ref[0].astype(F32)
        for j in range(1, n):
            g = g + p_ref[j].astype(F32)
        g_ref[...] = g
        nm = ADAM_B1 * m_ref[...] + (1.0 - ADAM_B1) * g
        nv = ADAM_B2 * v_ref[...] + (1.0 - ADAM_B2) * (g * g)
        nm_ref[...] = nm
        nv_ref[...] = nv
        m_hat = nm / (1.0 - ADAM_B1 ** ADAM_STEP)
        v_hat = nv / (1.0 - ADAM_B2 ** ADAM_STEP)
        d_ref[...] = -ADAM_LR * (m_hat / (jnp.sqrt(v_hat) + ADAM_EPS) + ADAM_WD * w_ref[...])

    row = pl.BlockSpec((tr, C), lambda i: (i, 0))
    return pl.pallas_call(
        body, name=name, grid=(R // tr,),
        in_specs=[pl.BlockSpec((n, tr, C), lambda i: (0, i, 0)), row, row, row], out_specs=[row] * 4,
        out_shape=[jax.ShapeDtypeStruct((R, C), F32)] * 4,
        compiler_params=_params(("parallel",)),
    )(parts, w, m, v)


SHARDED = (("w_in", D, IN_W, 1), ("w_branch_a", CONV_W, D, 1), ("w_branch_b", ATTN_W, D, 1), ("w_out", D, D, 0),
           ("w_up", D, 2 * D_FF, 1), ("w_down", D_FF, D, 0), ("conv_a_w", 3, CONV_W, 1),
           ("conv_ffn_w", 3, 2 * D_FF, 1))
MATRICES = SHARDED[:6]
LATE = MATRICES[1:]
CONVS = SHARDED[6:]
REPLICATED = (("b_ada", N_MOD * D), ("norm1_g", D), ("norm2_g", D), ("b_f", N_HEADS), ("q_norm_g", HEAD_DIM),
              ("k_norm_g", HEAD_DIM))


def _shard_shape(rows, cols, axis):
    return (rows // N_DEV, cols) if axis == 0 else (rows, cols // N_DEV)


def _pack_rows(flat, multiple):
    length = flat.shape[-1]
    rows = -(-length // PACK_W)
    rows = -(-rows // multiple) * multiple
    pad = [(0, 0)] * (flat.ndim - 1) + [(0, rows * PACK_W - length)]
    return jnp.pad(flat, pad).reshape(flat.shape[:-1] + (rows, PACK_W))


def _pack_shards(shards, spec, multiple, dtype):
    flat = jnp.concatenate([shards[name].reshape(-1).astype(dtype) for name, *_ in spec])
    return _pack_rows(flat, multiple)


def _join_lane_blocks(gathered):
    n, r, c = gathered.shape

    def body(g_ref, o_ref):
        for j in range(n):
            o_ref[:, j * c:(j + 1) * c] = g_ref[j]

    return pl.pallas_call(
        body, name="join_lane_blocks", grid=(1,), in_specs=[_full((n, r, c))], out_specs=_full((r, n * c)),
        out_shape=jax.ShapeDtypeStruct((r, n * c), gathered.dtype), compiler_params=_params(("arbitrary",)),
    )(gathered)


SHARD_PAD = 768


def _assemble_columns(gathered, shard_cols, out_cols, segments, *, name):
    n, rows, padw = gathered.shape
    assert n == N_DEV and padw == SHARD_PAD and shard_cols <= SHARD_PAD

    def body(g_ref, o_ref):
        j = pl.program_id(0)

        @pl.when(j == 0)
        def _():
            o_ref[...] = jnp.zeros_like(o_ref)

        for dev in range(N_DEV):
            @pl.when(j == dev)
            def _(dev=dev):
                x = g_ref[0]
                for lo, hi, delta in segments:
                    a, b = max(lo, dev * shard_cols), min(hi, (dev + 1) * shard_cols)
                    if a >= b:
                        continue
                    base = (a + delta) // LANES * LANES
                    width = -(-(b + delta - base) // LANES) * LANES
                    src = lax.broadcasted_iota(jnp.int32, (padw, width), 0) + dev * shard_cols
                    dst = lax.broadcasted_iota(jnp.int32, (padw, width), 1) + (base - delta)
                    place = jnp.where((src == dst) & (src >= a) & (src < b), 1.0, 0.0).astype(BF16)
                    moved = jnp.dot(x, place, preferred_element_type=F32).astype(BF16)
                    o_ref[:, base:base + width] = o_ref[:, base:base + width] + moved

    return pl.pallas_call(
        body, name=name, grid=(N_DEV,), in_specs=[pl.BlockSpec((1, rows, padw), lambda j: (j, 0, 0))],
        out_specs=_full((rows, out_cols)), out_shape=jax.ShapeDtypeStruct((rows, out_cols), BF16),
        compiler_params=_params(("arbitrary",)),
    )(gathered)


def _pad_shard(w):
    return jnp.pad(w.astype(BF16), ((0, 0), (0, SHARD_PAD - w.shape[1])))


W_IN_SEGMENTS = ((0, COL_GA, 0), (COL_GA, COL_GA + N_HEADS, COL_F - COL_GA), (COL_GA + N_HEADS, IN_W, -N_HEADS))


def _join_shards(gathered, axis):
    if axis == 0:
        return gathered.reshape(N_DEV * gathered.shape[1], gathered.shape[2])
    if gathered.shape[2] == LANES:
        return _join_lane_blocks(gathered)
    return jnp.concatenate([gathered[j] for j in range(N_DEV)], axis=1)


def _column_shards(pieces):
    total = sum(p.shape[1] for p in pieces)
    width = total // N_DEV
    shards = []
    for j in range(N_DEV):
        lo, hi, off, segs = j * width, (j + 1) * width, 0, []
        for p in pieces:
            a, b = max(lo, off), min(hi, off + p.shape[1])
            if a < b:
                segs.append(p[:, a - off:b - off])
            off += p.shape[1]
        shards.append(segs[0] if len(segs) == 1 else jnp.concatenate(segs, axis=1))
    return jnp.stack(shards)


def _split_shards(full, axis):
    if axis == 0:
        return full.reshape(N_DEV, full.shape[0] // N_DEV, full.shape[1])
    c = full.shape[1] // N_DEV
    return jnp.stack([full[:, j * c:(j + 1) * c] for j in range(N_DEV)])


def _unpack_shards(packed, spec):
    flat = packed.reshape(-1)
    out, off = {}, 0
    for name, rows, cols, axis in spec:
        r, c = _shard_shape(rows, cols, axis)
        out[name] = flat[off:off + r * c].reshape(r, c)
        off += r * c
    return out


def _unpack_gathered(gathered, spec):
    flat = gathered.reshape(N_DEV, -1)
    out, off = {}, 0
    for name, rows, cols, axis in spec:
        r, c = _shard_shape(rows, cols, axis)
        seg = flat[:, off:off + r * c].reshape(N_DEV, r, c)
        out[name] = seg.reshape(rows, cols) if axis == 0 else seg.transpose(1, 0, 2).reshape(rows, cols)
        off += r * c
    return out


def _pack_full_by_dest(full, spec, multiple):
    segs = []
    for name, rows, cols, axis in spec:
        r, c = _shard_shape(rows, cols, axis)
        a = full[name]
        seg = a.reshape(N_DEV, r, c) if axis == 0 else a.reshape(rows, N_DEV, c).transpose(1, 0, 2)
        segs.append(seg.reshape(N_DEV, r * c))
    return _pack_rows(jnp.concatenate(segs, axis=1), multiple)


def _pad_head_rows(w):
    n = w.shape[1]
    padded = jnp.pad(w.reshape(N_HEADS, HEAD_DIM, n), ((0, 0), (0, LANES - HEAD_DIM), (0, 0)))
    return padded.reshape(N_HEADS * LANES, n)


def kernel(x, c, w_ada, b_ada, norm1_g, w_in, b_f, conv_a_w, q_norm_g, k_norm_g, w_branch_a, w_branch_b, w_out, norm2_g, w_up, conv_ffn_w, w_down, loss_target, m_w_ada, m_b_ada, m_norm1_g, m_w_in, m_b_f, m_conv_a_w, m_q_norm_g, m_k_norm_g, m_w_branch_a, m_w_branch_b, m_w_out, m_norm2_g, m_w_up, m_conv_ffn_w, m_w_down, v_w_ada, v_b_ada, v_norm1_g, v_w_in, v_b_f, v_conv_a_w, v_q_norm_g, v_k_norm_g, v_w_branch_a, v_w_branch_b, v_w_out, v_norm2_g, v_w_up, v_conv_ffn_w, v_w_down):
    names = ("w_ada", "b_ada", "norm1_g", "w_in", "b_f", "conv_a_w", "q_norm_g", "k_norm_g", "w_branch_a",
             "w_branch_b", "w_out", "norm2_g", "w_up", "conv_ffn_w", "w_down")
    squeeze = lambda a: a[0] if a.ndim == 3 else a
    W = dict(zip(names, map(squeeze, (w_ada, b_ada, norm1_g, w_in, b_f, conv_a_w, q_norm_g, k_norm_g, w_branch_a,
                                      w_branch_b, w_out, norm2_g, w_up, conv_ffn_w, w_down))))
    M = dict(zip(names, map(squeeze, (m_w_ada, m_b_ada, m_norm1_g, m_w_in, m_b_f, m_conv_a_w, m_q_norm_g,
                                      m_k_norm_g, m_w_branch_a, m_w_branch_b, m_w_out, m_norm2_g, m_w_up,
                                      m_conv_ffn_w, m_w_down))))
    V = dict(zip(names, map(squeeze, (v_w_ada, v_b_ada, v_norm1_g, v_w_in, v_b_f, v_conv_a_w, v_q_norm_g,
                                      v_k_norm_g, v_w_branch_a, v_w_branch_b, v_w_out, v_norm2_g, v_w_up,
                                      v_conv_ffn_w, v_w_down))))
    me = 4 * lax.axis_index("x") + 2 * lax.axis_index("y") + lax.axis_index("c")
    ada_n = N_MOD * D // N_DEV

    small = jnp.concatenate([c.reshape(-1), W["conv_a_w"].reshape(-1), W["conv_ffn_w"].reshape(-1)])
    small_all, w_in_all = _gather_two_level([_pack_rows(small, SUBLANES), _pad_shard(W["w_in"])],
                                            name="gather_first")
    small_all = small_all.reshape(N_DEV, -1)
    c_all = small_all[:, :D]
    conv_all = _unpack_gathered(small_all[:, D:], CONVS)

    b_shard = lax.dynamic_slice(W["b_ada"], (0, me * ada_n), (1, ada_n))
    mod_part = _ada_fwd(c_all, W["w_ada"], b_shard)
    mod_all, = _exchange([mod_part], name="gather_mod", scatter=False)
    mod = lax.dynamic_index_in_dim(mod_all, me, axis=1, keepdims=False).reshape(1, N_MOD * D)

    wts = {"w_in": _assemble_columns(w_in_all, IN_W // N_DEV, IN_W_PAD, W_IN_SEGMENTS, name="assemble_w_in")}
    wts.update(conv_all)
    for name in ("norm1_g", "norm2_g", "q_norm_g", "k_norm_g", "b_f"):
        wts[name] = W[name]
    late = {name: _pad_shard(W[name]) if name == "w_up" else W[name].astype(BF16) for name, *_ in LATE}

    sq, grad_x, grads, parts = _local_step(x[0], loss_target[0], mod, wts, late)
    loss = lax.psum(sq[0, 0] * (0.5 / D), AXES)

    grads["b_ada"] = grads["mod"]
    rep_flat = lambda src: jnp.concatenate([src[name].reshape(-1) for name, _ in REPLICATED])
    rep_parts, = _exchange([_pack_rows(rep_flat(grads), 16)], name="gather_small_grads", scatter=False)
    rep_out = _adamw(rep_parts, *[_pack_rows(rep_flat(s), 16) for s in (W, M, V)], name="adamw_replicated")

    dmod_all = rep_parts.reshape(N_DEV, -1)[:, :N_MOD * D]
    dmod_mine = lax.dynamic_slice(dmod_all, (0, me * ada_n), (N_DEV, ada_n))
    g_ada = _ada_bwd(jnp.pad(c_all.T, ((0, 0), (0, LANES - N_DEV))),
                     jnp.pad(dmod_mine, ((0, LANES - N_DEV), (0, 0))))
    ada_out = _adamw(g_ada[None], W["w_ada"], M["w_ada"], V["w_ada"], name="adamw_ada")

    mat_out = {name: _adamw(parts[name], W[name], M[name], V[name], name="adamw_" + name) for name, *_ in MATRICES}
    conv_out = _adamw(parts["conv"], *[_pack_shards(s, CONVS, SUBLANES, F32) for s in (W, M, V)],
                      name="adamw_conv")

    results = []
    for kind in range(4):
        per = {"w_ada": ada_out[kind]}
        per.update({name: out[kind] for name, out in mat_out.items()})
        per.update(_unpack_shards(conv_out[kind], CONVS))
        flat, off = rep_out[kind].reshape(-1), 0
        for name, n in REPLICATED:
            per[name] = flat[off:off + n].reshape(1, n)
            off += n
        results.append(per)
    restore = lambda name, a: a[None] if W[name].ndim == 2 and name not in dict(REPLICATED) else a
    outs = [loss, grad_x[None]]
    for per in results:
        outs.extend(restore(name, per[name]) for name in names)
    return tuple(outs)
```

```python
import jax
import jax.numpy as jnp
import numpy as np
from jax import lax
from jax.experimental import pallas as pl
from jax.experimental.pallas import tpu as pltpu

F32 = jnp.float32
BF16 = jnp.bfloat16

N_DEV = 8
D = 1024
N_HEADS = 8
HEAD_DIM = 64
ATTN_W = 512
CONV_W = 512
D_FF = 2816
N_MOD = 6
IN_W = 5128
RMS_EPS = 1e-6
NEG_INF = -1e30

IN_W_PAD = 5376
COL_GA = 3072
COL_GB = 4096
COL_F = 5120
F_PAD = 128

ADAM_LR = 0.001
ADAM_B1 = 0.9
ADAM_B2 = 0.999
ADAM_EPS = 1e-08
ADAM_WD = 0.01
ADAM_STEP = 10

LANES = 128
SUBLANES = 8
BF16_ROWS = 16
VMEM_LIMIT = 52 * 1024 * 1024
TOKEN_TILE = 512
MATMUL_TILE = 1024
ATTN_BLOCK = 512
PACK_W = 1024

MESH = pl.DeviceIdType.MESH
AXES = ("x", "y", "c")


def _params(sem=None, **kw):
    return pltpu.CompilerParams(dimension_semantics=sem, vmem_limit_bytes=VMEM_LIMIT, **kw)


def _full(shape):
    nd = len(shape)
    return pl.BlockSpec(shape, lambda *_: (0,) * nd)


def _tn_dot(a, b):
    return lax.dot_general(a, b, (((0,), (0,)), ((), ())), preferred_element_type=F32)


def _matmul(a, b, *, name, tm, tn, tk, out_dtype=F32, trans_a=False, trans_b=False, exchange=None):
    assert not (trans_a and trans_b)
    if trans_a:
        K, M = a.shape
    else:
        M, K = a.shape
    N = b.shape[0] if trans_b else b.shape[1]
    assert b.shape[1 if trans_b else 0] == K and M % tm == 0 and N % tn == 0 and K % tk == 0, (name, a.shape, b.shape)
    nm, nn, nk = M // tm, N // tn, K // tk

    def body(*refs):
        if exchange is None:
            a_ref, b_ref, o_ref, *own = refs
        else:
            (a_ref, b_ref), (o_ref,), own, xrefs = exchange.split(refs, 2, 1)
            ids = [pl.program_id(d) for d in range(3)]
            first = jnp.logical_and(jnp.logical_and(ids[0] == 0, ids[1] == 0), ids[2] == 0)
            last = jnp.logical_and(jnp.logical_and(ids[0] == nn - 1, ids[1] == nm - 1), ids[2] == nk - 1)
            _ride(exchange, first, last, xrefs)
        k = pl.program_id(2)
        av = a_ref[...].astype(BF16)
        bv = b_ref[...].astype(BF16)
        if trans_a:
            prod = _tn_dot(av, bv)
        elif trans_b:
            prod = _nt_dot(av, bv)
        else:
            prod = jnp.dot(av, bv, preferred_element_type=F32)
        if nk == 1:
            o_ref[...] = prod.astype(out_dtype)
            return
        acc_ref, = own

        @pl.when(k == 0)
        def _():
            acc_ref[...] = prod

        @pl.when(k > 0)
        def _():
            acc_ref[...] += prod

        @pl.when(k == nk - 1)
        def _():
            o_ref[...] = acc_ref[...].astype(out_dtype)

    if trans_a:
        a_spec = pl.BlockSpec((tk, tm), lambda j, i, k: (k, i))
    else:
        a_spec = pl.BlockSpec((tm, tk), lambda j, i, k: (i, k))
    b_spec = pl.BlockSpec((tn, tk), lambda j, i, k: (j, k)) if trans_b else pl.BlockSpec((tk, tn), lambda j, i, k: (k, j))
    in_specs = [a_spec, b_spec]
    out_spec = pl.BlockSpec((tm, tn), lambda j, i, k: (i, j))
    out_shape = jax.ShapeDtypeStruct((M, N), out_dtype)
    scratch = [pltpu.VMEM((tm, tn), F32)] if nk > 1 else []
    if exchange is None:
        return pl.pallas_call(
            body, name=name, grid=(nn, nm, nk), in_specs=in_specs, out_specs=out_spec, out_shape=out_shape,
            scratch_shapes=scratch, compiler_params=_params(("parallel", "parallel", "arbitrary")),
        )(a, b)
    return pl.pallas_call(
        body, name=name, grid=(nn, nm, nk), in_specs=in_specs + exchange.in_specs,
        out_specs=[out_spec] + exchange.out_specs, out_shape=[out_shape] + exchange.out_shapes,
        scratch_shapes=scratch + exchange.scratch, compiler_params=_params(("arbitrary",) * 3),
    )(a, b, *exchange.xs)


def _norm_bwd_tile(dh, ins, outs, first):
    x_ref, dr_ref, g_ref, sc_ref = ins[:4]
    dx_ref, dsh_ref, dsc_ref, dg_ref = outs[:4]

    @pl.when(first)
    def _():
        for ref in outs[1:4] + outs[5:]:
            ref[...] = jnp.zeros_like(ref)

    xv = x_ref[...]
    gv = g_ref[...]
    one_sc = 1.0 + sc_ref[...]
    inv = lax.rsqrt(jnp.mean(xv * xv, axis=-1, keepdims=True) + RMS_EPS)
    xn = xv * inv
    dxn = dh * (gv * one_sc)
    dx = dr_ref[...] + inv * (dxn - xn * jnp.mean(dxn * xn, axis=-1, keepdims=True))
    dx_ref[...] = dx
    dhxn = dh * xn
    dsh_ref[...] += _rows8(dh)
    dsc_ref[...] += _rows8(dhxn * gv)
    dg_ref[...] += _rows8(dhxn * one_sc)
    if len(ins) == 6:
        mix_ref, g1_ref = ins[4:]
        dmix_ref, dg1_ref = outs[4:]
        dmix_ref[...] = (dx * g1_ref[...]).astype(BF16)
        dg1_ref[...] += _rows8(dx * mix_ref[...])


def _matmul_pieces(pieces, b, *, name, tm, exchange=None, norm_bwd=None):
    M = pieces[0].shape[0]
    widths = [p.shape[1] for p in pieces]
    offsets = [sum(widths[:i]) for i in range(len(widths))]
    N = b.shape[0]
    assert b.shape[1] >= sum(widths) and M % tm == 0, (name, widths, b.shape)
    n_p, nm = len(pieces), M // tm
    extra = list(norm_bwd) if norm_bwd is not None else []
    n_in = n_p + 1 + len(extra)
    n_out = len(extra) if norm_bwd is not None else 1

    def body(*refs):
        i = pl.program_id(0)
        if exchange is None:
            ins, outs = refs[:n_in], refs[n_in:]
        else:
            ins, outs, _, xrefs = exchange.split(refs, n_in, n_out)
            _ride(exchange, i == 0, i == nm - 1, xrefs)
        b_ref = ins[n_p]
        acc = None
        for a_ref, off, w in zip(ins[:n_p], offsets, widths):
            term = _nt_dot(a_ref[...].astype(BF16), b_ref[:, off:off + w])
            acc = term if acc is None else acc + term
        if norm_bwd is None:
            outs[0][...] = acc
        else:
            _norm_bwd_tile(acc, tuple(ins[n_p + 1:]), tuple(outs), first=i == 0)

    row, vec, part = pl.BlockSpec((tm, N), lambda i: (i, 0)), _full((1, N)), _full((SUBLANES, N))
    part_shape = jax.ShapeDtypeStruct((SUBLANES, N), F32)
    in_specs = [pl.BlockSpec((tm, w), lambda i: (i, 0)) for w in widths] + [_full(b.shape)]
    out_specs, out_shape = [row], [jax.ShapeDtypeStruct((M, N), F32)]
    if norm_bwd is not None:
        in_specs += [row, row, vec, vec] + ([row, vec] if len(extra) == 6 else [])
        out_specs += [part] * 3 + ([row, part] if len(extra) == 6 else [])
        out_shape += [part_shape] * 3 + ([jax.ShapeDtypeStruct((M, N), BF16), part_shape] if len(extra) == 6 else [])
    sequential = exchange is not None or norm_bwd is not None
    xs = exchange.xs if exchange is not None else []
    result = pl.pallas_call(
        body, name=name, grid=(nm,), in_specs=in_specs + (exchange.in_specs if exchange else []),
        out_specs=out_specs + (exchange.out_specs if exchange else []),
        out_shape=out_shape + (exchange.out_shapes if exchange else []),
        scratch_shapes=exchange.scratch if exchange else [],
        compiler_params=_params(("arbitrary" if sequential else "parallel",)),
    )(*pieces, b, *extra, *xs)
    return result[0] if len(result) == 1 else result


def _matmul_tn_pieces(a, pieces, *, name, tk):
    K, M = a.shape
    widths = [p.shape[1] for p in pieces]
    n_p, nk = len(pieces), K // tk

    def body(*refs):
        a_ref, p_refs, o_refs = refs[0], refs[1:n_p + 1], refs[n_p + 1:]
        k = pl.program_id(0)
        av = a_ref[...].astype(BF16)
        for p_ref, o_ref in zip(p_refs, o_refs):
            prod = _tn_dot(av, p_ref[...].astype(BF16))

            @pl.when(k == 0)
            def _():
                o_ref[...] = prod

            @pl.when(k > 0)
            def _():
                o_ref[...] += prod

    return pl.pallas_call(
        body, name=name, grid=(nk,),
        in_specs=[pl.BlockSpec((tk, M), lambda k: (k, 0))] + [pl.BlockSpec((tk, w), lambda k: (k, 0)) for w in widths],
        out_specs=[_full((M, w)) for w in widths], out_shape=[jax.ShapeDtypeStruct((M, w), F32) for w in widths],
        compiler_params=_params(("arbitrary",)),
    )(a, *pieces)


def _split_dot(x, mat, parts):
    out = None
    rem = x
    for p in range(parts):
        piece = rem.astype(BF16)
        term = jnp.dot(piece, mat, preferred_element_type=F32)
        out = term if out is None else out + term
        if p + 1 < parts:
            rem = rem - piece.astype(F32)
    return out


def _sigmoid(x):
    return 0.5 * jnp.tanh(0.5 * x) + 0.5


def _rows8(x):
    r, c = x.shape
    return jnp.sum(x.reshape(r // SUBLANES, SUBLANES, c), axis=0)


def _shift_down(blk, prev8, n):
    rolled = pltpu.roll(blk, n, axis=0)
    prev_rolled = pltpu.roll(prev8, n, axis=0)
    rows = lax.broadcasted_iota(jnp.int32, prev8.shape, 0)
    first = jnp.where(rows < n, prev_rolled, rolled[0:SUBLANES])
    return jnp.concatenate([first, rolled[SUBLANES:]], axis=0)


def _prev_spec(tm, width, col):
    per = tm // SUBLANES
    return pl.BlockSpec((SUBLANES, width), lambda i, *_: (jnp.maximum(i * per - 1, 0), col))


def _next_spec(tm, width, col, n_tiles):
    per = tm // SUBLANES
    last = n_tiles * per - 1
    return pl.BlockSpec((SUBLANES, width), lambda i, *_: (jnp.minimum((i + 1) * per, last), col))


def _group_matrix():
    idx = np.arange(ATTN_W) // HEAD_DIM
    return jnp.asarray((idx[:, None] == idx[None, :]).astype(np.float32), BF16)


def _norm_mod(x, g, sc, sh, *, name):
    T = x.shape[0]
    tm = min(TOKEN_TILE, T)

    def body(x_ref, g_ref, sc_ref, sh_ref, o_ref):
        xv = x_ref[...]
        inv = lax.rsqrt(jnp.mean(xv * xv, axis=-1, keepdims=True) + RMS_EPS)
        o_ref[...] = ((xv * inv) * g_ref[...] * (1.0 + sc_ref[...]) + sh_ref[...]).astype(BF16)

    row = pl.BlockSpec((tm, D), lambda i: (i, 0))
    return pl.pallas_call(
        body, name=name, grid=(T // tm,),
        in_specs=[row, _full((1, D)), _full((1, D)), _full((1, D))],
        out_specs=row, out_shape=jax.ShapeDtypeStruct((T, D), BF16),
        compiler_params=_params(("parallel",)),
    )(x, g, sc, sh)


LANE_ONE = 64
LANE_F = 67
LANE_LSE = 70
LANE_SUM = 73


def _pieces(x):
    hi = x.astype(BF16).astype(F32)
    rest = x - hi
    mid = rest.astype(BF16).astype(F32)
    return hi, mid, rest - mid


def _run(start, vals):
    return [(start + i, v) for i, v in enumerate(vals)]


def _head_lanes(a, h):
    blk = a[:, LANES * (h // 2):LANES * (h // 2) + LANES]
    return blk if h % 2 == 0 else pltpu.roll(blk, HEAD_DIM, axis=1)


def _branch_prep(proj, fcum, conv_w8, qg, kg, gmat):
    T = proj.shape[0]
    tm = min(TOKEN_TILE, T)
    nt = T // tm

    def body(cb_ref, cc_ref, cv_ref, q_ref, k_ref, v_ref, f_ref, ccp_ref, cvp_ref, w_ref, qg_ref, kg_ref, g_ref,
             ya_ref, qa_ref, ka_ref, va_ref):
        i = pl.program_id(0)
        z = cc_ref[...] * cv_ref[...]
        zp = jnp.where(i > 0, ccp_ref[...] * cvp_ref[...], 0.0)
        w = w_ref[...]
        cz = _shift_down(z, zp, 2) * w[0:1] + _shift_down(z, zp, 1) * w[1:2] + z * w[2:3]
        ya_ref[...] = (cb_ref[...] * cz).astype(BF16)
        gm = g_ref[...]

        def normed(src, gain, scale):
            v = src[...]
            ms = _split_dot(v * v, gm, 2) * (1.0 / HEAD_DIM)
            return (v * lax.rsqrt(ms + RMS_EPS)) * gain[...] * scale

        qn = normed(q_ref, qg_ref, 1.0 / np.sqrt(HEAD_DIM))
        kn = normed(k_ref, kg_ref, 1.0)
        vv = v_ref[...]
        lane = lax.broadcasted_iota(jnp.int32, (tm, LANES), 1)
        low = lane < HEAD_DIM
        in_run = lambda start: jnp.logical_and(lane >= start, lane < start + 3)
        q_ones = jnp.where(jnp.logical_or(in_run(LANE_ONE), lane == LANE_SUM), 1.0, 0.0)
        k_ones = jnp.where(jnp.logical_or(in_run(LANE_F), in_run(LANE_LSE)), 1.0, 0.0)
        v_ones = jnp.where(in_run(LANE_ONE), 1.0, 0.0)
        f3 = jnp.concatenate(_pieces(f_ref[...]), axis=1).astype(BF16)
        src = lax.broadcasted_iota(jnp.int32, (3 * LANES, LANES), 0)
        dst = lax.broadcasted_iota(jnp.int32, (3 * LANES, LANES), 1)
        for h in range(N_HEADS):
            def pick(start, h=h):
                hit = jnp.logical_and(src - h == (dst - start) * LANES, jnp.logical_and(dst >= start, dst < start + 3))
                return jnp.dot(f3, jnp.where(hit, 1.0, 0.0).astype(BF16), preferred_element_type=F32)

            qa_ref[h] = jnp.where(low, _head_lanes(qn, h), q_ones + pick(LANE_F)).astype(BF16)
            ka_ref[h] = jnp.where(low, _head_lanes(kn, h), k_ones - pick(LANE_ONE)).astype(BF16)
            va_ref[h] = jnp.where(low, _head_lanes(vv, h), v_ones).astype(BF16)

    blk = lambda col: pl.BlockSpec((tm, CONV_W), lambda i: (i, col))
    heads = pl.BlockSpec((N_HEADS, tm, LANES), lambda i: (0, i, 0))
    return pl.pallas_call(
        body, name="branch_prep", grid=(nt,),
        in_specs=[blk(0), blk(1), blk(2), blk(3), blk(4), blk(5), pl.BlockSpec((tm, F_PAD), lambda i: (i, 0)),
                  _prev_spec(tm, CONV_W, 1), _prev_spec(tm, CONV_W, 2),
                  _full((SUBLANES, CONV_W)), _full((1, ATTN_W)), _full((1, ATTN_W)), _full((ATTN_W, ATTN_W))],
        out_specs=[pl.BlockSpec((tm, CONV_W), lambda i: (i, 0)), heads, heads, heads],
        out_shape=[jax.ShapeDtypeStruct((T, CONV_W), BF16)] + [jax.ShapeDtypeStruct((N_HEADS, T, LANES), BF16)] * 3,
        compiler_params=_params(("parallel",)),
    )(proj, proj, proj, proj, proj, proj, fcum, proj, proj, conv_w8, qg, kg, gmat)


def _cumsum(x, *, reverse, name, col=0, gate_bias=None):
    T = x.shape[0]
    tm = min(TOKEN_TILE, T)
    nt = T // tm

    def body(x_ref, b_ref, o_ref, carry_ref):
        i = pl.program_id(0)

        @pl.when(i == 0)
        def _():
            carry_ref[...] = jnp.zeros_like(carry_ref)

        r = lax.broadcasted_iota(jnp.int32, (tm, tm), 0)
        c = lax.broadcasted_iota(jnp.int32, (tm, tm), 1)
        tri = jnp.where((c >= r) if reverse else (c <= r), 1.0, 0.0).astype(BF16)
        xv = x_ref[...]
        if gate_bias is not None:
            fx = xv + b_ref[...]
            xv = jnp.minimum(fx, 0.0) - jnp.log(1.0 + jnp.exp(-jnp.abs(fx)))
        out = _split_dot_left(tri, xv, 3) + carry_ref[0:1]
        o_ref[...] = out
        carry_ref[...] = jnp.broadcast_to(out[0:1] if reverse else out[tm - 1:tm], carry_ref.shape)

    rows = (lambda i: nt - 1 - i) if reverse else (lambda i: i)
    bias = jnp.zeros((1, F_PAD), F32) if gate_bias is None else gate_bias
    return pl.pallas_call(
        body, name=name, grid=(nt,),
        in_specs=[pl.BlockSpec((tm, F_PAD), lambda i: (rows(i), col)), _full((1, F_PAD))],
        out_specs=pl.BlockSpec((tm, F_PAD), lambda i: (rows(i), 0)),
        out_shape=jax.ShapeDtypeStruct((T, F_PAD), F32),
        scratch_shapes=[pltpu.VMEM((SUBLANES, F_PAD), F32)],
        compiler_params=_params(("arbitrary",)),
    )(x, bias)


def _split_dot_left(mat, x, parts):
    out = None
    rem = x
    for p in range(parts):
        piece = rem.astype(BF16)
        term = jnp.dot(mat, piece, preferred_element_type=F32)
        out = term if out is None else out + term
        if p + 1 < parts:
            rem = rem - piece.astype(F32)
    return out


def _out_resid_norm(x, merged, w_out, g1, g, sc, sh):
    T = x.shape[0]
    tm = min(TOKEN_TILE, T)

    def body(x_ref, m_ref, w_ref, g1_ref, g_ref, sc_ref, sh_ref, mix_ref, x1_ref, h_ref):
        mix = jnp.dot(m_ref[...], w_ref[...], preferred_element_type=F32)
        mix_ref[...] = mix
        x1 = x_ref[...] + g1_ref[...] * mix
        x1_ref[...] = x1
        inv = lax.rsqrt(jnp.mean(x1 * x1, axis=-1, keepdims=True) + RMS_EPS)
        h_ref[...] = ((x1 * inv) * g_ref[...] * (1.0 + sc_ref[...]) + sh_ref[...]).astype(BF16)

    row = pl.BlockSpec((tm, D), lambda i: (i, 0))
    vec = _full((1, D))
    return pl.pallas_call(
        body, name="out_resid_norm", grid=(T // tm,),
        in_specs=[row, row, _full((D, D)), vec, vec, vec, vec], out_specs=[row, row, row],
        out_shape=[jax.ShapeDtypeStruct((T, D), F32), jax.ShapeDtypeStruct((T, D), F32),
                   jax.ShapeDtypeStruct((T, D), BF16)],
        compiler_params=_params(("parallel",)),
    )(x, merged, w_out, g1, g, sc, sh)


FFN_TM = 256
FFN_TC = 1408


def _ffn_act_fwd(u, w8):
    T = u.shape[0]
    tm = min(FFN_TM, T)
    nt = T // tm
    nc = D_FF // FFN_TC

    def body(ug_ref, uv_ref, ugp_ref, uvp_ref, wg_ref, wv_ref, o_ref, cg_ref, cv_ref):
        i = pl.program_id(1)

        def conv(u_ref, p_ref, w_ref):
            uv = u_ref[...]
            up = jnp.where(i > 0, p_ref[...], 0.0)
            w = w_ref[...]
            return _shift_down(uv, up, 2) * w[0:1] + _shift_down(uv, up, 1) * w[1:2] + uv * w[2:3]

        gate = conv(ug_ref, ugp_ref, wg_ref)
        val = conv(uv_ref, uvp_ref, wv_ref)
        cg_ref[...] = gate.astype(BF16)
        cv_ref[...] = val.astype(BF16)
        o_ref[...] = (gate * _sigmoid(gate) * val).astype(BF16)

    per = tm // SUBLANES
    blk = lambda off: pl.BlockSpec((tm, FFN_TC), lambda j, i: (i, j + off))
    prev = lambda off: pl.BlockSpec((SUBLANES, FFN_TC), lambda j, i: (jnp.maximum(i * per - 1, 0), j + off))
    wblk = lambda off: pl.BlockSpec((SUBLANES, FFN_TC), lambda j, i: (0, j + off))
    return pl.pallas_call(
        body, name="ffn_act_fwd", grid=(nc, nt),
        in_specs=[blk(0), blk(nc), prev(0), prev(nc), wblk(0), wblk(nc)],
        out_specs=[blk(0), blk(0), blk(0)],
        out_shape=[jax.ShapeDtypeStruct((T, D_FF), BF16)] * 3,
        compiler_params=_params(("parallel", "parallel")),
    )(u, u, u, u, w8, w8)


def _down_loss_head(x1, act, w_down, g2, target):
    T = x1.shape[0]
    tm = min(TOKEN_TILE, T)

    def body(x1_ref, a_ref, w_ref, g2_ref, t_ref, dy_ref, dff_ref, loss_ref, dg2_ref):
        i = pl.program_id(0)

        @pl.when(i == 0)
        def _():
            loss_ref[...] = jnp.zeros_like(loss_ref)
            dg2_ref[...] = jnp.zeros_like(dg2_ref)

        ff = jnp.dot(a_ref[...], w_ref[...], preferred_element_type=F32)
        err = x1_ref[...] + g2_ref[...] * ff - t_ref[...]
        dy = err * (1.0 / D)
        dy_ref[...] = dy
        dff_ref[...] = (dy * g2_ref[...]).astype(BF16)
        loss_ref[...] += _rows8(err * err)
        dg2_ref[...] += _rows8(dy * ff)

    row = pl.BlockSpec((tm, D), lambda i: (i, 0))
    acc = _full((SUBLANES, D))
    return pl.pallas_call(
        body, name="down_loss_head", grid=(T // tm,),
        in_specs=[row, pl.BlockSpec((tm, D_FF), lambda i: (i, 0)), _full((D_FF, D)), _full((1, D)), row],
        out_specs=[row, row, acc, acc],
        out_shape=[jax.ShapeDtypeStruct((T, D), F32), jax.ShapeDtypeStruct((T, D), BF16),
                   jax.ShapeDtypeStruct((SUBLANES, D), F32), jax.ShapeDtypeStruct((SUBLANES, D), F32)],
        compiler_params=_params(("arbitrary",)),
    )(x1, act, w_down, g2, target)


def _nt_dot(a, b):
    return lax.dot_general(a, b, (((1,), (1,)), ((), ())), preferred_element_type=F32)


def _causal(n, keys_on_rows=False):
    r = lax.broadcasted_iota(jnp.int32, (n, n), 0)
    c = lax.broadcasted_iota(jnp.int32, (n, n), 1)
    return (c >= r) if keys_on_rows else (c <= r)


def _sweep(lo, hi, step, carry, group=2):
    while group >= 1:
        def several(j, cr, lo=lo, group=group):
            for g in range(group):
                cr = step(lo + group * j + g, cr)
            return cr

        passes = (hi - lo) // group
        carry = lax.fori_loop(0, passes, several, carry)
        lo = lo + group * passes
        group //= 2
    return carry


def _grid_ends(n0, n1):
    i0, i1 = pl.program_id(0), pl.program_id(1)
    return jnp.logical_and(i0 == 0, i1 == 0), jnp.logical_and(i0 == n0 - 1, i1 == n1 - 1)


def _attn_fwd(qa, ka, va, exchange=None):
    nh, T, _ = qa.shape
    bq = min(ATTN_BLOCK, T)
    nq = T // bq

    def body(*refs):
        if exchange is None:
            q_ref, k_ref, v_ref, o_ref, qb_ref = refs
        else:
            (q_ref, k_ref, v_ref), (o_ref, qb_ref), _, xrefs = exchange.split(refs, 3, 2)
            _ride(exchange, *_grid_ends(nh, nq), xrefs)
        qi = pl.program_id(1)
        q = q_ref[0]

        def step(kb, carry, masked=False):
            m, acc = carry
            start = pl.multiple_of(kb * bq, bq)
            s = _nt_dot(q, k_ref[0, pl.ds(start, bq), :])
            if masked:
                s = jnp.where(_causal(bq), s, NEG_INF)
            m_new = jnp.maximum(m, jnp.max(s, axis=-1, keepdims=True))
            p = jnp.exp(s - m_new).astype(BF16)
            acc = jnp.exp(m - m_new) * acc + jnp.dot(p, v_ref[0, pl.ds(start, bq), :], preferred_element_type=F32)
            return m_new, acc

        init = (jnp.full((bq, 1), NEG_INF, F32), jnp.zeros((bq, LANES), F32))
        m, acc = step(qi, _sweep(0, qi, step, init, group=4), masked=True)
        l = acc[:, LANE_ONE:LANE_ONE + 1]
        o_ref[0] = acc / l
        lane = lax.broadcasted_iota(jnp.int32, (bq, LANES), 1)
        qf = q.astype(F32)
        for idx, piece in _run(LANE_LSE, _pieces(m + jnp.log(l))):
            qf = jnp.where(lane == idx, -piece, qf)
        qb_ref[0] = qf.astype(BF16)

    tile = pl.BlockSpec((1, bq, LANES), lambda h, i: (h, i, 0))
    whole = pl.BlockSpec((1, T, LANES), lambda h, i: (h, 0, 0))
    out_shape = [jax.ShapeDtypeStruct((nh, T, LANES), F32), jax.ShapeDtypeStruct((nh, T, LANES), BF16)]
    if exchange is None:
        return pl.pallas_call(
            body, name="attn_fwd", grid=(nh, nq), in_specs=[tile, whole, whole], out_specs=[tile, tile],
            out_shape=out_shape, compiler_params=_params(("parallel", "parallel")),
        )(qa, ka, va)
    return pl.pallas_call(
        body, name="attn_fwd", grid=(nh, nq), in_specs=[tile, whole, whole] + exchange.in_specs,
        out_specs=[tile, tile] + exchange.out_specs, out_shape=out_shape + exchange.out_shapes,
        scratch_shapes=exchange.scratch, compiler_params=_params(("arbitrary", "arbitrary")),
    )(qa, ka, va, *exchange.xs)


def _branch_merge_fwd(ya0, o_h, proj, wba, wbb_heads):
    nh, T, _ = o_h.shape
    tm = min(TOKEN_TILE, T)

    def body(ya0_ref, o_ref, ga_ref, gb_ref, wa_ref, wb_ref, ya_ref, yb_ref, m_ref):
        ya = jnp.dot(ya0_ref[...], wa_ref[...], preferred_element_type=F32)
        yb = jnp.dot(o_ref[0].astype(BF16), wb_ref[0:LANES, :], preferred_element_type=F32)
        for h in range(1, nh):
            yb += jnp.dot(o_ref[h].astype(BF16), wb_ref[h * LANES:(h + 1) * LANES, :], preferred_element_type=F32)
        ya_ref[...] = ya.astype(BF16)
        yb_ref[...] = yb.astype(BF16)
        m_ref[...] = (_sigmoid(ga_ref[...]) * ya + _sigmoid(gb_ref[...]) * yb).astype(BF16)

    row = pl.BlockSpec((tm, D), lambda i: (i, 0))
    return pl.pallas_call(
        body, name="branch_merge_fwd", grid=(T // tm,),
        in_specs=[pl.BlockSpec((tm, CONV_W), lambda i: (i, 0)), pl.BlockSpec((nh, tm, LANES), lambda i: (0, i, 0)),
                  pl.BlockSpec((tm, D), lambda i: (i, COL_GA // D)), pl.BlockSpec((tm, D), lambda i: (i, COL_GB // D)),
                  _full((CONV_W, D)), _full((nh * LANES, D))],
        out_specs=[row, row, row],
        out_shape=[jax.ShapeDtypeStruct((T, D), BF16)] * 3,
        compiler_params=_params(("parallel",)),
    )(ya0, o_h, proj, proj, wba, wbb_heads)


def _branch_b_bwd(dyb, o_h, wbb_heads):
    nh, T, _ = o_h.shape
    tm = min(TOKEN_TILE, T)

    def body(dyb_ref, o_ref, w_ref, out_ref):
        do = _nt_dot(dyb_ref[...], w_ref[...])
        lane = lax.broadcasted_iota(jnp.int32, (tm, LANES), 1)
        for h in range(nh):
            g = do[:, h * LANES:(h + 1) * LANES].astype(BF16).astype(F32)
            delta = jnp.sum(g * o_ref[h], axis=-1, keepdims=True)
            for idx, piece in _run(LANE_ONE, _pieces(delta)):
                g = jnp.where(lane == idx, -piece, g)
            out_ref[h] = g.astype(BF16)

    heads = pl.BlockSpec((nh, tm, LANES), lambda i: (0, i, 0))
    return pl.pallas_call(
        body, name="branch_b_bwd", grid=(T // tm,),
        in_specs=[pl.BlockSpec((tm, D), lambda i: (i, 0)), heads, _full((D, nh * LANES))],
        out_specs=heads, out_shape=jax.ShapeDtypeStruct((nh, T, LANES), BF16),
        compiler_params=_params(("parallel",)),
    )(dyb, o_h, wbb_heads)


def _branch_b_dw(o_h, dyb):
    nh, T, _ = o_h.shape
    tk = min(TOKEN_TILE, T)

    def body(o_ref, dyb_ref, out_ref):
        @pl.when(pl.program_id(0) == 0)
        def _():
            out_ref[...] = jnp.zeros_like(out_ref)

        g = dyb_ref[...]
        for h in range(nh):
            out_ref[h] += _tn_dot(o_ref[h].astype(BF16), g)

    return pl.pallas_call(
        body, name="branch_b_dw", grid=(T // tk,),
        in_specs=[pl.BlockSpec((nh, tk, LANES), lambda k: (0, k, 0)), pl.BlockSpec((tk, D), lambda k: (k, 0))],
        out_specs=_full((nh, LANES, D)), out_shape=jax.ShapeDtypeStruct((nh, LANES, D), F32),
        compiler_params=_params(("arbitrary",)),
    )(o_h, dyb)


def _attn_bwd(qb, ka, va, doa, exchange=None):
    nh, T, _ = qb.shape
    bk = min(ATTN_BLOCK, T)
    nk = T // bk

    def body(*refs):
        if exchange is None:
            q_ref, do_ref, k_ref, v_ref, dq_ref, dk_ref, dv_ref = refs
        else:
            (q_ref, do_ref, k_ref, v_ref), (dq_ref, dk_ref, dv_ref), _, xrefs = exchange.split(refs, 4, 3)
            _ride(exchange, *_grid_ends(nh, nk), xrefs)
        ki = pl.program_id(1)

        @pl.when(ki == 0)
        def _():
            dq_ref[...] = jnp.zeros_like(dq_ref)

        k = k_ref[0]
        v = v_ref[0]

        def step(qi, carry, masked):
            dk, dv = carry
            rows = pl.ds(pl.multiple_of(qi * bk, bk), bk)
            q = q_ref[0, rows, :]
            g = do_ref[0, rows, :]
            pt = jnp.exp(_nt_dot(k, q))
            if masked:
                pt = jnp.where(_causal(bk, keys_on_rows=True), pt, 0.0)
            dv = dv + jnp.dot(pt.astype(BF16), g, preferred_element_type=F32)
            dst = (pt * _nt_dot(v, g)).astype(BF16)
            dk = dk + jnp.dot(dst, q, preferred_element_type=F32)
            dq_ref[0, rows, :] += _tn_dot(dst, k)
            return dk, dv

        init = (jnp.zeros((bk, LANES), F32), jnp.zeros((bk, LANES), F32))
        carry = step(ki, init, True)
        dk_ref[0], dv_ref[0] = _sweep(ki + 1, nk, lambda qi, cr: step(qi, cr, False), carry)

    tile = pl.BlockSpec((1, bk, LANES), lambda h, i: (h, i, 0))
    whole = pl.BlockSpec((1, T, LANES), lambda h, i: (h, 0, 0))
    out_shape = [jax.ShapeDtypeStruct((nh, T, LANES), F32)] * 3
    if exchange is None:
        return pl.pallas_call(
            body, name="attn_bwd", grid=(nh, nk), in_specs=[whole, whole, tile, tile],
            out_specs=[whole, tile, tile], out_shape=out_shape, compiler_params=_params(("parallel", "arbitrary")),
        )(qb, doa, ka, va)
    return pl.pallas_call(
        body, name="attn_bwd", grid=(nh, nk), in_specs=[whole, whole, tile, tile] + exchange.in_specs,
        out_specs=[whole, tile, tile] + exchange.out_specs, out_shape=out_shape + exchange.out_shapes,
        scratch_shapes=exchange.scratch, compiler_params=_params(("arbitrary", "arbitrary")),
    )(qb, doa, ka, va, *exchange.xs)


def _attn_unpack(dq_h, dk_h, dv_h):
    nh, T, _ = dq_h.shape
    tm = min(TOKEN_TILE, T)

    def body(dq_ref, dk_ref, dv_ref, q_out, k_out, v_out, f_out):
        lane = lax.broadcasted_iota(jnp.int32, (tm, LANES), 1)
        low = lane < HEAD_DIM
        for src, dst in ((dq_ref, q_out), (dk_ref, k_out), (dv_ref, v_out)):
            for pair in range(nh // 2):
                both = jnp.where(low, src[2 * pair], pltpu.roll(src[2 * pair + 1], HEAD_DIM, axis=1))
                dst[:, LANES * pair:LANES * (pair + 1)] = both.astype(dst.dtype)
        df = jnp.zeros((tm, LANES), F32)
        for h in range(nh):
            col = dq_ref[h][:, LANE_F:LANE_F + 1] - dk_ref[h][:, LANE_SUM:LANE_SUM + 1]
            df = jnp.where(lane == h, col, df)
        f_out[...] = df

    heads = pl.BlockSpec((nh, tm, LANES), lambda i: (0, i, 0))
    tok = pl.BlockSpec((tm, ATTN_W), lambda i: (i, 0))
    return pl.pallas_call(
        body, name="attn_unpack", grid=(T // tm,), in_specs=[heads, heads, heads],
        out_specs=[tok, tok, tok, pl.BlockSpec((tm, F_PAD), lambda i: (i, 0))],
        out_shape=[jax.ShapeDtypeStruct((T, ATTN_W), BF16)] * 3 + [jax.ShapeDtypeStruct((T, F_PAD), F32)],
        compiler_params=_params(("parallel",)),
    )(dq_h, dk_h, dv_h)


def _ffn_act_bwd(u, cg, cv, da, w8):
    T = u.shape[0]
    tm = min(FFN_TM, T)
    nt = T // tm
    nc = D_FF // FFN_TC

    def body(ug_ref, uv_ref, cg_ref, cv_ref, cgn_ref, cvn_ref, da_ref, dan_ref, wg_ref, wv_ref,
             dug_ref, duv_ref, dwg_ref, dwv_ref):
        i = pl.program_id(1)

        @pl.when(i == 0)
        def _():
            dwg_ref[...] = jnp.zeros_like(dwg_ref)
            dwv_ref[...] = jnp.zeros_like(dwv_ref)

        gate = jnp.concatenate([cg_ref[...], cgn_ref[...]], axis=0).astype(F32)
        val = jnp.concatenate([cv_ref[...], cvn_ref[...]], axis=0).astype(F32)
        dae = jnp.concatenate([da_ref[...], dan_ref[...]], axis=0).astype(F32)
        rows_e = lax.broadcasted_iota(jnp.int32, dae.shape, 0)
        dae = jnp.where(jnp.logical_and(i == nt - 1, rows_e >= tm), 0.0, dae)
        sg = _sigmoid(gate)
        n = tm + BF16_ROWS

        def back(d, u_ref, w_ref, du_ref, dw_ref):
            w = w_ref[...]
            uv = u_ref[...]
            d1 = pltpu.roll(d, n - 1, axis=0)[:tm]
            d2 = pltpu.roll(d, n - 2, axis=0)[:tm]
            d0 = d[:tm]
            du_ref[...] = (d0 * w[2:3] + d1 * w[1:2] + d2 * w[0:1]).astype(BF16)
            rows = [jnp.sum(t * uv, axis=0, keepdims=True) for t in (d2, d1, d0)]
            dw_ref[...] += jnp.concatenate(rows + [jnp.zeros((SUBLANES - 3, FFN_TC), F32)], axis=0)

        back(dae * val * sg * (1.0 + gate * (1.0 - sg)), ug_ref, wg_ref, dug_ref, dwg_ref)
        back(dae * gate * sg, uv_ref, wv_ref, duv_ref, dwv_ref)

    per = tm // BF16_ROWS
    last_blk = nt * per - 1
    blk = lambda off: pl.BlockSpec((tm, FFN_TC), lambda j, i: (i, j + off))
    nxt = pl.BlockSpec((BF16_ROWS, FFN_TC), lambda j, i: (jnp.minimum((i + 1) * per, last_blk), j))
    wblk = lambda off: pl.BlockSpec((SUBLANES, FFN_TC), lambda j, i: (0, j + off))
    dug, duv, dwg, dwv = pl.pallas_call(
        body, name="ffn_act_bwd", grid=(nc, nt),
        in_specs=[blk(0), blk(nc), blk(0), blk(0), nxt, nxt, blk(0), nxt, wblk(0), wblk(nc)],
        out_specs=[blk(0), blk(0), wblk(0), wblk(0)],
        out_shape=[jax.ShapeDtypeStruct((T, D_FF), BF16)] * 2 + [jax.ShapeDtypeStruct((SUBLANES, D_FF), F32)] * 2,
        compiler_params=_params(("parallel", "arbitrary")),
    )(u, u, cg, cv, cg, cv, da, da, w8, w8)
    return dug, duv, jnp.concatenate([dwg, dwv], axis=1)


def _out_merge_bwd(dmix, w_out, ya, yb, proj):
    T = ya.shape[0]
    tm = min(TOKEN_TILE, T)

    def body(dmix_ref, w_ref, ya_ref, yb_ref, ga_ref, gb_ref, dya_ref, dyb_ref, dga_ref, dgb_ref):
        dm = _nt_dot(dmix_ref[...], w_ref[...])
        sa = _sigmoid(ga_ref[...])
        sb = _sigmoid(gb_ref[...])
        dya_ref[...] = (dm * sa).astype(BF16)
        dyb_ref[...] = (dm * sb).astype(BF16)
        dga_ref[...] = (dm * ya_ref[...].astype(F32) * sa * (1.0 - sa)).astype(BF16)
        dgb_ref[...] = (dm * yb_ref[...].astype(F32) * sb * (1.0 - sb)).astype(BF16)

    row = pl.BlockSpec((tm, D), lambda i: (i, 0))
    return pl.pallas_call(
        body, name="out_merge_bwd", grid=(T // tm,),
        in_specs=[row, _full((D, D)), row, row, pl.BlockSpec((tm, D), lambda i: (i, COL_GA // D)),
                  pl.BlockSpec((tm, D), lambda i: (i, COL_GB // D))],
        out_specs=[row] * 4, out_shape=[jax.ShapeDtypeStruct((T, D), BF16)] * 4,
        compiler_params=_params(("parallel",)),
    )(dmix, w_out, ya, yb, proj, proj)


def _conv_branch_bwd(proj, dya0, conv_w8):
    T = proj.shape[0]
    tm = min(FFN_TM, T)
    nt = T // tm

    def body(cb_ref, cc_ref, cv_ref, cbn_ref, ccp_ref, cvp_ref, ccn_ref, cvn_ref, d_ref, dn_ref, w_ref,
             d3_ref, dw_ref):
        i = pl.program_id(0)

        @pl.when(i == 0)
        def _():
            dw_ref[...] = jnp.zeros_like(dw_ref)

        first, last = i == 0, i == nt - 1
        w = w_ref[...]
        cc = jnp.concatenate([ccp_ref[...], cc_ref[...], ccn_ref[...]], axis=0)
        cv = jnp.concatenate([cvp_ref[...], cv_ref[...], cvn_ref[...]], axis=0)
        rows = lax.broadcasted_iota(jnp.int32, cc.shape, 0)
        z = jnp.where(jnp.logical_and(first, rows < SUBLANES), 0.0, cc * cv)
        z1 = pltpu.roll(z, 1, axis=0)
        z2 = pltpu.roll(z, 2, axis=0)
        cz = z2 * w[0:1] + z1 * w[1:2] + z * w[2:3]
        zeros8 = jnp.zeros((SUBLANES, CONV_W), F32)
        de = jnp.concatenate([zeros8, d_ref[...], jnp.where(last, 0.0, dn_ref[...])], axis=0)
        cbe = jnp.concatenate([zeros8, cb_ref[...], cbn_ref[...]], axis=0)
        dcz = de * cbe
        n = tm + 2 * SUBLANES
        dz = dcz * w[2:3] + pltpu.roll(dcz, n - 1, axis=0) * w[1:2] + pltpu.roll(dcz, n - 2, axis=0) * w[0:1]
        inner = slice(SUBLANES, SUBLANES + tm)
        d3_ref[:, 0:CONV_W] = (de * cz)[inner].astype(BF16)
        d3_ref[:, CONV_W:2 * CONV_W] = (dz * cv)[inner].astype(BF16)
        d3_ref[:, 2 * CONV_W:3 * CONV_W] = (dz * cc)[inner].astype(BF16)
        wrows = [jnp.sum((dcz * t)[inner], axis=0, keepdims=True) for t in (z2, z1, z)]
        dw_ref[...] += jnp.concatenate(wrows + [jnp.zeros((SUBLANES - 3, CONV_W), F32)], axis=0)

    blk = lambda col: pl.BlockSpec((tm, CONV_W), lambda i: (i, col))
    out_blk = pl.BlockSpec((tm, CONV_W), lambda i: (i, 0))
    return pl.pallas_call(
        body, name="conv_branch_bwd", grid=(nt,),
        in_specs=[blk(0), blk(1), blk(2), _next_spec(tm, CONV_W, 0, nt),
                  _prev_spec(tm, CONV_W, 1), _prev_spec(tm, CONV_W, 2),
                  _next_spec(tm, CONV_W, 1, nt), _next_spec(tm, CONV_W, 2, nt),
                  out_blk, _next_spec(tm, CONV_W, 0, nt), _full((SUBLANES, CONV_W))],
        out_specs=[pl.BlockSpec((tm, 3 * CONV_W), lambda i: (i, 0)), _full((SUBLANES, CONV_W))],
        out_shape=[jax.ShapeDtypeStruct((T, 3 * CONV_W), BF16), jax.ShapeDtypeStruct((SUBLANES, CONV_W), F32)],
        compiler_params=_params(("arbitrary",)),
    )(proj, proj, proj, proj, proj, proj, proj, proj, dya0, dya0, conv_w8)


def _qk_norm_bwd(proj, dqs, dkh, dlogf, qg, kg, bf_pad, gmat):
    T = proj.shape[0]
    tm = min(TOKEN_TILE, T)

    def body(q_ref, k_ref, f_ref, dqs_ref, dkh_ref, dlf_ref, qg_ref, kg_ref, bf_ref, g_ref,
             dqk_ref, dfl_ref, dqg_ref, dkg_ref, dbf_ref):
        @pl.when(pl.program_id(0) == 0)
        def _():
            dqg_ref[...] = jnp.zeros_like(dqg_ref)
            dkg_ref[...] = jnp.zeros_like(dkg_ref)
            dbf_ref[...] = jnp.zeros_like(dbf_ref)

        gm = g_ref[...]
        for src, d_src, gain, scale, dst, dgain in (
                (q_ref, dqs_ref, qg_ref, 1.0 / np.sqrt(HEAD_DIM), dqk_ref.at[:, 0:ATTN_W], dqg_ref),
                (k_ref, dkh_ref, kg_ref, 1.0, dqk_ref.at[:, ATTN_W:2 * ATTN_W], dkg_ref)):
            v = src[...]
            dhat = d_src[...].astype(F32) * scale
            inv = lax.rsqrt(_split_dot(v * v, gm, 2) * (1.0 / HEAD_DIM) + RMS_EPS)
            vn = v * inv
            dgain[...] += _rows8(dhat * vn)
            dvn = dhat * gain[...]
            mean = _split_dot(dvn * vn, gm, 2) * (1.0 / HEAD_DIM)
            dst[...] = (inv * (dvn - vn * mean)).astype(BF16)
        fx = f_ref[...] + bf_ref[...]
        dfl = dlf_ref[...] * _sigmoid(-fx)
        dfl_ref[...] = dfl.astype(BF16)
        dbf_ref[...] += _rows8(dfl)

    blk = lambda col: pl.BlockSpec((tm, ATTN_W), lambda i: (i, col))
    out_blk = pl.BlockSpec((tm, ATTN_W), lambda i: (i, 0))
    f_in = pl.BlockSpec((tm, F_PAD), lambda i: (i, COL_F // F_PAD))
    f_blk = pl.BlockSpec((tm, F_PAD), lambda i: (i, 0))
    return pl.pallas_call(
        body, name="qk_norm_bwd", grid=(T // tm,),
        in_specs=[blk(3), blk(4), f_in, out_blk, out_blk, f_blk, _full((1, ATTN_W)), _full((1, ATTN_W)),
                  _full((1, F_PAD)), _full((ATTN_W, ATTN_W))],
        out_specs=[pl.BlockSpec((tm, 2 * ATTN_W), lambda i: (i, 0)), f_blk, _full((SUBLANES, ATTN_W)),
                   _full((SUBLANES, ATTN_W)), _full((SUBLANES, F_PAD))],
        out_shape=[jax.ShapeDtypeStruct((T, 2 * ATTN_W), BF16), jax.ShapeDtypeStruct((T, F_PAD), BF16)]
        + [jax.ShapeDtypeStruct((SUBLANES, ATTN_W), F32)] * 2 + [jax.ShapeDtypeStruct((SUBLANES, F_PAD), F32)],
        compiler_params=_params(("arbitrary",)),
    )(proj, proj, proj, dqs, dkh, dlogf, qg, kg, bf_pad, gmat)


def _pad_rows8(w):
    return jnp.pad(w, ((0, SUBLANES - w.shape[0]), (0, 0)))


def _fold8(acc):
    return jnp.sum(acc, axis=0, keepdims=True)


def _late_weights(mats):
    out = {name: mats[name] for name in ("w_branch_a", "w_out", "w_up", "w_down")}
    out["w_branch_b_heads"] = _pad_head_rows(mats["w_branch_b"])
    return out


def _local_step(x, target, mod, wts, late=None):
    T = x.shape[0]
    tb = min(MATMUL_TILE, T)
    tk_long = min(2 * MATMUL_TILE, T)
    tm = min(TOKEN_TILE, T)
    sh1, sc1, g1, sh2, sc2, g2 = [mod[:, i * D:(i + 1) * D] for i in range(N_MOD)]
    w_in = wts["w_in"]
    conv_a8 = _pad_rows8(wts["conv_a_w"])
    conv_f8 = _pad_rows8(wts["conv_ffn_w"])
    qg = jnp.tile(wts["q_norm_g"], (1, N_HEADS))
    kg = jnp.tile(wts["k_norm_g"], (1, N_HEADS))
    bf_pad = jnp.pad(wts["b_f"], ((0, 0), (0, F_PAD - N_HEADS)))
    gmat = _group_matrix()

    h = _norm_mod(x, wts["norm1_g"], sc1, sh1, name="norm1_fwd")
    proj = _matmul(h, w_in, name="mm_in", tm=tb, tn=896, tk=D)
    fcum = _cumsum(proj, reverse=False, name="gate_cumsum", col=COL_F // F_PAD, gate_bias=bf_pad)
    ya0, qa, ka, va = _branch_prep(proj, fcum, conv_a8, qg, kg, gmat)
    if late is None:
        o_h, qb = _attn_fwd(qa, ka, va)
    else:
        o_h, qb, *gathered = _attn_fwd(qa, ka, va, _Exchange([late[name] for name, *_ in LATE], scatter=False))
        wts = dict(wts)
        mats = {name: _join_shards(g, axis) for (name, _, _, axis), g in zip(LATE, gathered) if name != "w_up"}
        mats["w_up"] = _assemble_columns(gathered[[name for name, *_ in LATE].index("w_up")], 2 * D_FF // N_DEV,
                                         2 * D_FF, ((0, 2 * D_FF, 0),), name="assemble_w_up")
        wts.update(_late_weights(mats))
    ya, yb, merged = _branch_merge_fwd(ya0, o_h, proj, wts["w_branch_a"], wts["w_branch_b_heads"])
    mix, x1, h2 = _out_resid_norm(x, merged, wts["w_out"], g1, wts["norm2_g"], sc2, sh2)
    u = _matmul(h2, wts["w_up"], name="mm_up", tm=tb, tn=1408, tk=D)
    act, conv_gate, conv_val = _ffn_act_fwd(u, conv_f8)
    dy, dff, sq8, dg2_8 = _down_loss_head(x1, act, wts["w_down"], g2, target)
    sq = jnp.sum(sq8).reshape(1, 1)

    grads = {}
    da = _matmul(dff, wts["w_down"], name="mm_down_dx", tm=tb, tn=1408, tk=D, out_dtype=BF16, trans_b=True)
    grads["w_down"] = _matmul(act, dff, name="mm_down_dw", tm=1408, tn=D, tk=tk_long, trans_a=True)
    dug, duv, dconv_f8 = _ffn_act_bwd(u, conv_gate, conv_val, da, conv_f8)
    grads["conv_ffn_w"] = dconv_f8[:3]
    dx1, dsh2_8, dsc2_8, dn2_8, dmix, dg1_8 = _matmul_pieces(
        [dug, duv], wts["w_up"], name="mm_up_dx", tm=tm, norm_bwd=(x1, dy, wts["norm2_g"], sc2, mix, g1))
    dw_up = [_matmul(h2, d, name="mm_up_dw_" + half, tm=D, tn=1408, tk=tk_long, trans_a=True)
             for half, d in (("gate", dug), ("val", duv))]
    if late is None:
        grads["w_up"] = jnp.concatenate(dw_up, axis=1)
    grads["norm2_g"] = _fold8(dn2_8)

    grads["w_out"] = _matmul(merged, dmix, name="mm_out_dw", tm=D, tn=D, tk=tk_long, trans_a=True)
    dya, dyb, dga, dgb = _out_merge_bwd(dmix, wts["w_out"], ya, yb, proj)
    dya0 = _matmul(dya, wts["w_branch_a"], name="mm_branch_a_dx", tm=tb, tn=CONV_W, tk=D, trans_b=True)
    grads["w_branch_a"] = _matmul(ya0, dya, name="mm_branch_a_dw", tm=CONV_W, tn=D, tk=tk_long, trans_a=True)
    doa = _branch_b_bwd(dyb, o_h, wts["w_branch_b_heads"])
    grads["w_branch_b"] = _branch_b_dw(o_h, dyb)[:, :HEAD_DIM].reshape(ATTN_W, D)
    dconv3, dconv_a8 = _conv_branch_bwd(proj, dya0, conv_a8)
    grads["conv_a_w"] = dconv_a8[:3]

    parts = {}
    if late is None:
        dq_h, dk_h, dv_h = _attn_bwd(qb, ka, va, doa)
    else:
        ready = [(_column_shards(dw_up) if name == "w_up" else _split_shards(grads[name], axis)).astype(BF16)
                 for name, _, _, axis in LATE]
        dq_h, dk_h, dv_h, *recv = _attn_bwd(
            qb, ka, va, doa, _Exchange(ready + [_pack_full_by_dest(grads, CONVS, SUBLANES)], scatter=True))
        parts = dict(zip([name for name, *_ in LATE] + ["conv"], recv))
    dq_tok, dk_tok, dv_tok, dfcum = _attn_unpack(dq_h, dk_h, dv_h)
    dlogf = _cumsum(dfcum, reverse=True, name="gate_cumsum_bwd")
    dqk, dfl, dqg8, dkg8, dbf8 = _qk_norm_bwd(proj, dq_tok, dk_tok, dlogf, qg, kg, bf_pad, gmat)
    grads["q_norm_g"] = jnp.sum(_fold8(dqg8).reshape(N_HEADS, HEAD_DIM), axis=0, keepdims=True)
    grads["k_norm_g"] = jnp.sum(_fold8(dkg8).reshape(N_HEADS, HEAD_DIM), axis=0, keepdims=True)
    grads["b_f"] = _fold8(dbf8)[:, :N_HEADS]
    narrow, wide = [dconv3, dqk, dv_tok], [dga, dgb, dfl]
    dw_narrow = _matmul_tn_pieces(h, narrow, name="mm_in_dw_narrow", tk=tb)
    dwa, dwb, dwf = _matmul_tn_pieces(h, wide, name="mm_in_dw_wide", tk=tk_long)
    dw_in = list(dw_narrow) + [dwf[:, :N_HEADS], dwa, dwb]
    norm1 = (x, dx1, wts["norm1_g"], sc1)
    if late is None:
        grads["w_in"] = jnp.concatenate(dw_in, axis=1)
        grad_x, dsh1_8, dsc1_8, dn1_8 = _matmul_pieces(narrow + wide, w_in, name="mm_in_dx", tm=tm,
                                                       norm_bwd=norm1)
    else:
        grad_x, dsh1_8, dsc1_8, dn1_8, parts["w_in"] = _matmul_pieces(
            narrow + wide, w_in, name="mm_in_dx", tm=tm, norm_bwd=norm1,
            exchange=_Exchange([_column_shards(dw_in).astype(BF16)], scatter=True))
    grads["norm1_g"] = _fold8(dn1_8)
    grads["mod"] = jnp.concatenate([_fold8(a) for a in (dsh1_8, dsc1_8, dg1_8, dsh2_8, dsc2_8, dg2_8)], axis=1)
    return sq, grad_x, grads, parts


def _me_and_peers():
    mx, my, mc = lax.axis_index("x"), lax.axis_index("y"), lax.axis_index("c")
    me = 4 * mx + 2 * my + mc
    peers = []
    for k in range(1, N_DEV):
        px = 1 - mx if k & 4 else mx
        py = 1 - my if k & 2 else my
        pc = 1 - mc if k & 1 else mc
        peers.append(((px, py, pc), 4 * px + 2 * py + pc))
    return me, peers


HBM_SPEC = pl.BlockSpec(memory_space=pltpu.HBM)


class _Exchange:
    def __init__(self, xs, scatter):
        self.xs, self.scatter, self.n = list(xs), scatter, len(xs)
        self.out_shapes = [jax.ShapeDtypeStruct(x.shape if scatter else (N_DEV,) + x.shape, x.dtype) for x in xs]
        self.in_specs = [HBM_SPEC] * self.n
        self.out_specs = [HBM_SPEC] * self.n
        self.scratch = [pltpu.SemaphoreType.DMA((self.n, N_DEV - 1)), pltpu.SemaphoreType.DMA((self.n, N_DEV - 1)),
                        pltpu.SemaphoreType.DMA((self.n,))]

    def _copies(self, x_refs, out_refs, sems):
        send_sems, recv_sems, local_sems = sems
        me, peers = _me_and_peers()

        def src(a, idx):
            return x_refs[a].at[idx] if self.scatter else x_refs[a]

        def copy(a, k, from_idx, to_slot, device):
            return pltpu.make_async_remote_copy(
                src_ref=src(a, from_idx), dst_ref=out_refs[a].at[to_slot], send_sem=send_sems.at[a, k],
                recv_sem=recv_sems.at[a, k], device_id=device, device_id_type=MESH)

        local = [pltpu.make_async_copy(src(a, me), out_refs[a].at[me], local_sems.at[a]) for a in range(self.n)]
        sends = [copy(a, k, idx, me, dev) for a in range(self.n) for k, (dev, idx) in enumerate(peers)]
        recvs = [copy(a, k, idx, idx, dev) for a in range(self.n) for k, (dev, idx) in enumerate(peers)]
        return local, sends, recvs

    def start(self, x_refs, out_refs, sems):
        local, sends, _ = self._copies(x_refs, out_refs, sems)
        for cp in local + sends:
            cp.start()

    def wait(self, x_refs, out_refs, sems):
        local, sends, recvs = self._copies(x_refs, out_refs, sems)
        for cp in recvs:
            cp.wait_recv()
        for cp in sends:
            cp.wait_send()
        for cp in local:
            cp.wait()

    def split(self, refs, n_in, n_out):
        n = self.n
        ins, xin = refs[:n_in], refs[n_in:n_in + n]
        outs, xout = refs[n_in + n:n_in + n + n_out], refs[n_in + n + n_out:n_in + 2 * n + n_out]
        rest = refs[n_in + 2 * n + n_out:]
        return ins, outs, rest[:len(rest) - 3], (xin, xout, rest[len(rest) - 3:])


def _ride(exchange, first, last, refs):
    if exchange is None:
        return

    @pl.when(first)
    def _():
        exchange.start(*refs)

    @pl.when(last)
    def _():
        exchange.wait(*refs)


def _gather_two_level(xs, *, name):
    n = len(xs)
    out_shapes = [jax.ShapeDtypeStruct((N_DEV,) + x.shape, x.dtype) for x in xs]

    def body(*refs):
        x_refs, out_refs = refs[:n], refs[n:2 * n]
        send_sems, recv_sems, local_sems = refs[2 * n:]
        x, y, c = lax.axis_index("x"), lax.axis_index("y"), lax.axis_index("c")
        me, sibling = (x, y, c), (x, y, 1 - c)
        chips = [(1 - x, y), (x, 1 - y), (1 - x, 1 - y)]

        def slot(a, dev):
            return out_refs[a].at[4 * dev[0] + 2 * dev[1] + dev[2]]

        def copy(a, k, block, to, src=None):
            return pltpu.make_async_remote_copy(
                src_ref=slot(a, block) if src is None else src, dst_ref=slot(a, block),
                send_sem=send_sems.at[a, k], recv_sem=recv_sems.at[a, k], device_id=to, device_id_type=MESH)

        mine = [pltpu.make_async_copy(x_refs[a], slot(a, me), local_sems.at[a]) for a in range(n)]
        first = [copy(a, 0, me, sibling, src=x_refs[a]) for a in range(n)]
        first += [copy(a, 1 + j, me, (*chip, c), src=x_refs[a]) for a in range(n) for j, chip in enumerate(chips)]
        for cp in mine + first:
            cp.start()
        passed = []
        for a in range(n):
            for j, chip in enumerate(chips):
                copy(a, 1 + j, (*chip, c), me).wait_recv()
                passed.append(copy(a, 4 + j, (*chip, c), sibling))
                passed[-1].start()
        for a in range(n):
            copy(a, 0, sibling, me).wait_recv()
            for j, chip in enumerate(chips):
                copy(a, 4 + j, (*chip, 1 - c), me).wait_recv()
        for cp in first + passed:
            cp.wait_send()
        for cp in mine:
            cp.wait()

    return pl.pallas_call(
        body, name=name, in_specs=[HBM_SPEC] * n, out_specs=[HBM_SPEC] * n, out_shape=out_shapes,
        scratch_shapes=[pltpu.SemaphoreType.DMA((n, N_DEV - 1)), pltpu.SemaphoreType.DMA((n, N_DEV - 1)),
                        pltpu.SemaphoreType.DMA((n,))],
        compiler_params=pltpu.CompilerParams(has_side_effects=True),
    )(*xs)


def _exchange(xs, *, name, scatter):
    ex = _Exchange(xs, scatter)

    def body(*refs):
        _, _, _, xrefs = ex.split(refs, 0, 0)
        ex.start(*xrefs)
        ex.wait(*xrefs)

    return pl.pallas_call(
        body, name=name, in_specs=ex.in_specs, out_specs=ex.out_specs, out_shape=ex.out_shapes,
        scratch_shapes=ex.scratch, compiler_params=pltpu.CompilerParams(has_side_effects=True),
    )(*xs)


def _ada_fwd(c_all, w_shard, b_shard):
    n = w_shard.shape[1]

    def body(c_ref, w_ref, b_ref, o_ref):
        cv = c_ref[...]
        act = (cv * _sigmoid(cv)).astype(BF16)
        o_ref[...] = jnp.dot(act, w_ref[...].astype(BF16), preferred_element_type=F32) + b_ref[...]

    return pl.pallas_call(
        body, name="ada_fwd", in_specs=[_full((N_DEV, D)), _full((D, n)), _full((1, n))],
        out_specs=_full((N_DEV, n)), out_shape=jax.ShapeDtypeStruct((N_DEV, n), F32), grid=(1,),
        compiler_params=_params(("arbitrary",)),
    )(c_all, w_shard, b_shard)


def _ada_bwd(c_all_t, dmod_pad):
    n = dmod_pad.shape[1]

    def body(c_ref, d_ref, o_ref):
        cv = c_ref[...]
        act = (cv * _sigmoid(cv)).astype(BF16)
        o_ref[...] = jnp.dot(act, d_ref[...].astype(BF16), preferred_element_type=F32)

    return pl.pallas_call(
        body, name="ada_bwd", in_specs=[_full((D, LANES)), _full((LANES, n))],
        out_specs=_full((D, n)), out_shape=jax.ShapeDtypeStruct((D, n), F32), grid=(1,),
        compiler_params=_params(("arbitrary",)),
    )(c_all_t, dmod_pad)


ADAM_ROWS = 256


def _adamw(parts, w, m, v, *, name):
    n, R, C = parts.shape
    tr = next((t for t in (ADAM_ROWS, 128, 64, 32, 16, SUBLANES) if R % t == 0), R)

    def body(p_ref, w_ref, m_ref, v_ref, g_ref, d_ref, nm_ref, nv_ref):
        g = p_ref[0].astype(F32)
        for j in range(1, n):
            g = g + p_ref[j].astype(F32)
        g_ref[...] = g
        nm = ADAM_B1 * m_ref[...] + (1.0 - ADAM_B1) * g
        nv = ADAM_B2 * v_ref[...] + (1.0 - ADAM_B2) * (g * g)
        nm_ref[...] = nm
        nv_ref[...] = nv
        m_hat = nm / (1.0 - ADAM_B1 ** ADAM_STEP)
        v_hat = nv / (1.0 - ADAM_B2 ** ADAM_STEP)
        d_ref[...] = -ADAM_LR * (m_hat / (jnp.sqrt(v_hat) + ADAM_EPS) + ADAM_WD * w_ref[...])

    row = pl.BlockSpec((tr, C), lambda i: (i, 0))
    return pl.pallas_call(
        body, name=name, grid=(R // tr,),
        in_specs=[pl.BlockSpec((n, tr, C), lambda i: (0, i, 0)), row, row, row], out_specs=[row] * 4,
        out_shape=[jax.ShapeDtypeStruct((R, C), F32)] * 4,
        compiler_params=_params(("parallel",)),
    )(parts, w, m, v)


SHARDED = (("w_in", D, IN_W, 1), ("w_branch_a", CONV_W, D, 1), ("w_branch_b", ATTN_W, D, 1), ("w_out", D, D, 0),
           ("w_up", D, 2 * D_FF, 1), ("w_down", D_FF, D, 0), ("conv_a_w", 3, CONV_W, 1),
           ("conv_ffn_w", 3, 2 * D_FF, 1))
MATRICES = SHARDED[:6]
LATE = MATRICES[1:]
CONVS = SHARDED[6:]
REPLICATED = (("b_ada", N_MOD * D), ("norm1_g", D), ("norm2_g", D), ("b_f", N_HEADS), ("q_norm_g", HEAD_DIM),
              ("k_norm_g", HEAD_DIM))


def _shard_shape(rows, cols, axis):
    return (rows // N_DEV, cols) if axis == 0 else (rows, cols // N_DEV)


def _pack_rows(flat, multiple):
    length = flat.shape[-1]
    rows = -(-length // PACK_W)
    rows = -(-rows // multiple) * multiple
    pad = [(0, 0)] * (flat.ndim - 1) + [(0, rows * PACK_W - length)]
    return jnp.pad(flat, pad).reshape(flat.shape[:-1] + (rows, PACK_W))


def _pack_shards(shards, spec, multiple, dtype):
    flat = jnp.concatenate([shards[name].reshape(-1).astype(dtype) for name, *_ in spec])
    return _pack_rows(flat, multiple)


def _join_lane_blocks(gathered):
    n, r, c = gathered.shape

    def body(g_ref, o_ref):
        for j in range(n):
            o_ref[:, j * c:(j + 1) * c] = g_ref[j]

    return pl.pallas_call(
        body, name="join_lane_blocks", grid=(1,), in_specs=[_full((n, r, c))], out_specs=_full((r, n * c)),
        out_shape=jax.ShapeDtypeStruct((r, n * c), gathered.dtype), compiler_params=_params(("arbitrary",)),
    )(gathered)


SHARD_PAD = 768


def _assemble_columns(gathered, shard_cols, out_cols, segments, *, name):
    n, rows, padw = gathered.shape
    assert n == N_DEV and padw == SHARD_PAD and shard_cols <= SHARD_PAD

    def body(g_ref, o_ref):
        j = pl.program_id(0)

        @pl.when(j == 0)
        def _():
            o_ref[...] = jnp.zeros_like(o_ref)

        for dev in range(N_DEV):
            @pl.when(j == dev)
            def _(dev=dev):
                x = g_ref[0]
                for lo, hi, delta in segments:
                    a, b = max(lo, dev * shard_cols), min(hi, (dev + 1) * shard_cols)
                    if a >= b:
                        continue
                    base = (a + delta) // LANES * LANES
                    width = -(-(b + delta - base) // LANES) * LANES
                    src = lax.broadcasted_iota(jnp.int32, (padw, width), 0) + dev * shard_cols
                    dst = lax.broadcasted_iota(jnp.int32, (padw, width), 1) + (base - delta)
                    place = jnp.where((src == dst) & (src >= a) & (src < b), 1.0, 0.0).astype(BF16)
                    moved = jnp.dot(x, place, preferred_element_type=F32).astype(BF16)
                    o_ref[:, base:base + width] = o_ref[:, base:base + width] + moved

    return pl.pallas_call(
        body, name=name, grid=(N_DEV,), in_specs=[pl.BlockSpec((1, rows, padw), lambda j: (j, 0, 0))],
        out_specs=_full((rows, out_cols)), out_shape=jax.ShapeDtypeStruct((rows, out_cols), BF16),
        compiler_params=_params(("arbitrary",)),
    )(gathered)


def _pad_shard(w):
    return jnp.pad(w.astype(BF16), ((0, 0), (0, SHARD_PAD - w.shape[1])))


W_IN_SEGMENTS = ((0, COL_GA, 0), (COL_GA, COL_GA + N_HEADS, COL_F - COL_GA), (COL_GA + N_HEADS, IN_W, -N_HEADS))


def _join_shards(gathered, axis):
    if axis == 0:
        return gathered.reshape(N_DEV * gathered.shape[1], gathered.shape[2])
    if gathered.shape[2] == LANES:
        return _join_lane_blocks(gathered)
    return jnp.concatenate([gathered[j] for j in range(N_DEV)], axis=1)


def _column_shards(pieces):
    total = sum(p.shape[1] for p in pieces)
    width = total // N_DEV
    shards = []
    for j in range(N_DEV):
        lo, hi, off, segs = j * width, (j + 1) * width, 0, []
        for p in pieces:
            a, b = max(lo, off), min(hi, off + p.shape[1])
            if a < b:
                segs.append(p[:, a - off:b - off])
            off += p.shape[1]
        shards.append(segs[0] if len(segs) == 1 else jnp.concatenate(segs, axis=1))
    return jnp.stack(shards)


def _split_shards(full, axis):
    if axis == 0:
        return full.reshape(N_DEV, full.shape[0] // N_DEV, full.shape[1])
    c = full.shape[1] // N_DEV
    return jnp.stack([full[:, j * c:(j + 1) * c] for j in range(N_DEV)])


def _unpack_shards(packed, spec):
    flat = packed.reshape(-1)
    out, off = {}, 0
    for name, rows, cols, axis in spec:
        r, c = _shard_shape(rows, cols, axis)
        out[name] = flat[off:off + r * c].reshape(r, c)
        off += r * c
    return out


def _unpack_gathered(gathered, spec):
    flat = gathered.reshape(N_DEV, -1)
    out, off = {}, 0
    for name, rows, cols, axis in spec:
        r, c = _shard_shape(rows, cols, axis)
        seg = flat[:, off:off + r * c].reshape(N_DEV, r, c)
        out[name] = seg.reshape(rows, cols) if axis == 0 else seg.transpose(1, 0, 2).reshape(rows, cols)
        off += r * c
    return out


def _pack_full_by_dest(full, spec, multiple):
    segs = []
    for name, rows, cols, axis in spec:
        r, c = _shard_shape(rows, cols, axis)
        a = full[name]
        seg = a.reshape(N_DEV, r, c) if axis == 0 else a.reshape(rows, N_DEV, c).transpose(1, 0, 2)
        segs.append(seg.reshape(N_DEV, r * c))
    return _pack_rows(jnp.concatenate(segs, axis=1), multiple)


def _pad_head_rows(w):
    n = w.shape[1]
    padded = jnp.pad(w.reshape(N_HEADS, HEAD_DIM, n), ((0, 0), (0, LANES - HEAD_DIM), (0, 0)))
    return padded.reshape(N_HEADS * LANES, n)


def kernel(x, c, w_ada, b_ada, norm1_g, w_in, b_f, conv_a_w, q_norm_g, k_norm_g, w_branch_a, w_branch_b, w_out, norm2_g, w_up, conv_ffn_w, w_down, loss_target, m_w_ada, m_b_ada, m_norm1_g, m_w_in, m_b_f, m_conv_a_w, m_q_norm_g, m_k_norm_g, m_w_branch_a, m_w_branch_b, m_w_out, m_norm2_g, m_w_up, m_conv_ffn_w, m_w_down, v_w_ada, v_b_ada, v_norm1_g, v_w_in, v_b_f, v_conv_a_w, v_q_norm_g, v_k_norm_g, v_w_branch_a, v_w_branch_b, v_w_out, v_norm2_g, v_w_up, v_conv_ffn_w, v_w_down):
    names = ("w_ada", "b_ada", "norm1_g", "w_in", "b_f", "conv_a_w", "q_norm_g", "k_norm_g", "w_branch_a",
             "w_branch_b", "w_out", "norm2_g", "w_up", "conv_ffn_w", "w_down")
    squeeze = lambda a: a[0] if a.ndim == 3 else a
    W = dict(zip(names, map(squeeze, (w_ada, b_ada, norm1_g, w_in, b_f, conv_a_w, q_norm_g, k_norm_g, w_branch_a,
                                      w_branch_b, w_out, norm2_g, w_up, conv_ffn_w, w_down))))
    M = dict(zip(names, map(squeeze, (m_w_ada, m_b_ada, m_norm1_g, m_w_in, m_b_f, m_conv_a_w, m_q_norm_g,
                                      m_k_norm_g, m_w_branch_a, m_w_branch_b, m_w_out, m_norm2_g, m_w_up,
                                      m_conv_ffn_w, m_w_down))))
    V = dict(zip(names, map(squeeze, (v_w_ada, v_b_ada, v_norm1_g, v_w_in, v_b_f, v_conv_a_w, v_q_norm_g,
                                      v_k_norm_g, v_w_branch_a, v_w_branch_b, v_w_out, v_norm2_g, v_w_up,
                                      v_conv_ffn_w, v_w_down))))
    me = 4 * lax.axis_index("x") + 2 * lax.axis_index("y") + lax.axis_index("c")
    ada_n = N_MOD * D // N_DEV

    small = jnp.concatenate([c.reshape(-1), W["conv_a_w"].reshape(-1), W["conv_ffn_w"].reshape(-1)])
    small_all, w_in_all = _gather_two_level([_pack_rows(small, SUBLANES), _pad_shard(W["w_in"])],
                                            name="gather_first")
    small_all = small_all.reshape(N_DEV, -1)
    c_all = small_all[:, :D]
    conv_all = _unpack_gathered(small_all[:, D:], CONVS)

    b_shard = lax.dynamic_slice(W["b_ada"], (0, me * ada_n), (1, ada_n))
    mod_part = _ada_fwd(c_all, W["w_ada"], b_shard)
    mod_all, = _exchange([mod_part], name="gather_mod", scatter=False)
    mod = lax.dynamic_index_in_dim(mod_all, me, axis=1, keepdims=False).reshape(1, N_MOD * D)

    wts = {"w_in": _assemble_columns(w_in_all, IN_W // N_DEV, IN_W_PAD, W_IN_SEGMENTS, name="assemble_w_in")}
    wts.update(conv_all)
    for name in ("norm1_g", "norm2_g", "q_norm_g", "k_norm_g", "b_f"):
        wts[name] = W[name]
    late = {name: _pad_shard(W[name]) if name == "w_up" else W[name].astype(BF16) for name, *_ in LATE}

    sq, grad_x, grads, parts = _local_step(x[0], loss_target[0], mod, wts, late)
    loss = lax.psum(sq[0, 0] * (0.5 / D), AXES)

    grads["b_ada"] = grads["mod"]
    rep_flat = lambda src: jnp.concatenate([src[name].reshape(-1) for name, _ in REPLICATED])
    rep_parts, = _exchange([_pack_rows(rep_flat(grads), 16)], name="gather_small_grads", scatter=False)
    rep_out = _adamw(rep_parts, *[_pack_rows(rep_flat(s), 16) for s in (W, M, V)], name="adamw_replicated")

    dmod_all = rep_parts.reshape(N_DEV, -1)[:, :N_MOD * D]
    dmod_mine = lax.dynamic_slice(dmod_all, (0, me * ada_n), (N_DEV, ada_n))
    g_ada = _ada_bwd(jnp.pad(c_all.T, ((0, 0), (0, LANES - N_DEV))),
                     jnp.pad(dmod_mine, ((0, LANES - N_DEV), (0, 0))))
    ada_out = _adamw(g_ada[None], W["w_ada"], M["w_ada"], V["w_ada"], name="adamw_ada")

    mat_out = {name: _adamw(parts[name], W[name], M[name], V[name], name="adamw_" + name) for name, *_ in MATRICES}
    conv_out = _adamw(parts["conv"], *[_pack_shards(s, CONVS, SUBLANES, F32) for s in (W, M, V)],
                      name="adamw_conv")

    results = []
    for kind in range(4):
        per = {"w_ada": ada_out[kind]}
        per.update({name: out[kind] for name, out in mat_out.items()})
        per.update(_unpack_shards(conv_out[kind], CONVS))
        flat, off = rep_out[kind].reshape(-1), 0
        for name, n in REPLICATED:
            per[name] = flat[off:off + n].reshape(1, n)
            off += n
        results.append(per)
    restore = lambda name, a: a[None] if W[name].ndim == 2 and name not in dict(REPLICATED) else a
    outs = [loss, grad_x[None]]
    for per in results:
        outs.extend(restore(name, per[name]) for name in names)
    return tuple(outs)
```

```python
import jax
import jax.numpy as jnp
import numpy as np
from jax import lax
from jax.experimental import pallas as pl
from jax.experimental.pallas import tpu as pltpu

F32 = jnp.float32
BF16 = jnp.bfloat16

N_DEV = 8
D = 1024
N_HEADS = 8
HEAD_DIM = 64
ATTN_W = 512
CONV_W = 512
D_FF = 2816
N_MOD = 6
IN_W = 5128
RMS_EPS = 1e-6
NEG_INF = -1e30

IN_W_PAD = 5376
COL_GA = 3072
COL_GB = 4096
COL_F = 5120
F_PAD = 128

ADAM_LR = 0.001
ADAM_B1 = 0.9
ADAM_B2 = 0.999
ADAM_EPS = 1e-08
ADAM_WD = 0.01
ADAM_STEP = 10

LANES = 128
SUBLANES = 8
BF16_ROWS = 16
VMEM_LIMIT = 52 * 1024 * 1024
TOKEN_TILE = 512
MATMUL_TILE = 1024
ATTN_BLOCK = 512
PACK_W = 1024

MESH = pl.DeviceIdType.MESH
AXES = ("x", "y", "c")


def _params(sem=None, **kw):
    return pltpu.CompilerParams(dimension_semantics=sem, vmem_limit_bytes=VMEM_LIMIT, **kw)


def _full(shape):
    nd = len(shape)
    return pl.BlockSpec(shape, lambda *_: (0,) * nd)


def _tn_dot(a, b):
    return lax.dot_general(a, b, (((0,), (0,)), ((), ())), preferred_element_type=F32)


def _matmul(a, b, *, name, tm, tn, tk, out_dtype=F32, trans_a=False, trans_b=False, exchange=None):
    assert not (trans_a and trans_b)
    if trans_a:
        K, M = a.shape
    else:
        M, K = a.shape
    N = b.shape[0] if trans_b else b.shape[1]
    assert b.shape[1 if trans_b else 0] == K and M % tm == 0 and N % tn == 0 and K % tk == 0, (name, a.shape, b.shape)
    nm, nn, nk = M // tm, N // tn, K // tk

    def body(*refs):
        if exchange is None:
            a_ref, b_ref, o_ref, *own = refs
        else:
            (a_ref, b_ref), (o_ref,), own, xrefs = exchange.split(refs, 2, 1)
            ids = [pl.program_id(d) for d in range(3)]
            first = jnp.logical_and(jnp.logical_and(ids[0] == 0, ids[1] == 0), ids[2] == 0)
            last = jnp.logical_and(jnp.logical_and(ids[0] == nn - 1, ids[1] == nm - 1), ids[2] == nk - 1)
            _ride(exchange, first, last, xrefs)
        k = pl.program_id(2)
        av = a_ref[...].astype(BF16)
        bv = b_ref[...].astype(BF16)
        if trans_a:
            prod = _tn_dot(av, bv)
        elif trans_b:
            prod = _nt_dot(av, bv)
        else:
            prod = jnp.dot(av, bv, preferred_element_type=F32)
        if nk == 1:
            o_ref[...] = prod.astype(out_dtype)
            return
        acc_ref, = own

        @pl.when(k == 0)
        def _():
            acc_ref[...] = prod

        @pl.when(k > 0)
        def _():
            acc_ref[...] += prod

        @pl.when(k == nk - 1)
        def _():
            o_ref[...] = acc_ref[...].astype(out_dtype)

    if trans_a:
        a_spec = pl.BlockSpec((tk, tm), lambda j, i, k: (k, i))
    else:
        a_spec = pl.BlockSpec((tm, tk), lambda j, i, k: (i, k))
    b_spec = pl.BlockSpec((tn, tk), lambda j, i, k: (j, k)) if trans_b else pl.BlockSpec((tk, tn), lambda j, i, k: (k, j))
    in_specs = [a_spec, b_spec]
    out_spec = pl.BlockSpec((tm, tn), lambda j, i, k: (i, j))
    out_shape = jax.ShapeDtypeStruct((M, N), out_dtype)
    scratch = [pltpu.VMEM((tm, tn), F32)] if nk > 1 else []
    if exchange is None:
        return pl.pallas_call(
            body, name=name, grid=(nn, nm, nk), in_specs=in_specs, out_specs=out_spec, out_shape=out_shape,
            scratch_shapes=scratch, compiler_params=_params(("parallel", "parallel", "arbitrary")),
        )(a, b)
    return pl.pallas_call(
        body, name=name, grid=(nn, nm, nk), in_specs=in_specs + exchange.in_specs,
        out_specs=[out_spec] + exchange.out_specs, out_shape=[out_shape] + exchange.out_shapes,
        scratch_shapes=scratch + exchange.scratch, compiler_params=_params(("arbitrary",) * 3),
    )(a, b, *exchange.xs)


def _norm_bwd_tile(dh, ins, outs, first):
    x_ref, dr_ref, g_ref, sc_ref = ins[:4]
    dx_ref, dsh_ref, dsc_ref, dg_ref = outs[:4]

    @pl.when(first)
    def _():
        for ref in outs[1:4] + outs[5:]:
            ref[...] = jnp.zeros_like(ref)

    xv = x_ref[...]
    gv = g_ref[...]
    one_sc = 1.0 + sc_ref[...]
    inv = lax.rsqrt(jnp.mean(xv * xv, axis=-1, keepdims=True) + RMS_EPS)
    xn = xv * inv
    dxn = dh * (gv * one_sc)
    dx = dr_ref[...] + inv * (dxn - xn * jnp.mean(dxn * xn, axis=-1, keepdims=True))
    dx_ref[...] = dx
    dhxn = dh * xn
    dsh_ref[...] += _rows8(dh)
    dsc_ref[...] += _rows8(dhxn * gv)
    dg_ref[...] += _rows8(dhxn * one_sc)
    if len(ins) == 6:
        mix_ref, g1_ref = ins[4:]
        dmix_ref, dg1_ref = outs[4:]
        dmix_ref[...] = (dx * g1_ref[...]).astype(BF16)
        dg1_ref[...] += _rows8(dx * mix_ref[...])


def _matmul_pieces(pieces, b, *, name, tm, exchange=None, norm_bwd=None):
    M = pieces[0].shape[0]
    widths = [p.shape[1] for p in pieces]
    offsets = [sum(widths[:i]) for i in range(len(widths))]
    N = b.shape[0]
    assert b.shape[1] >= sum(widths) and M % tm == 0, (name, widths, b.shape)
    n_p, nm = len(pieces), M // tm
    extra = list(norm_bwd) if norm_bwd is not None else []
    n_in = n_p + 1 + len(extra)
    n_out = len(extra) if norm_bwd is not None else 1

    def body(*refs):
        i = pl.program_id(0)
        if exchange is None:
            ins, outs = refs[:n_in], refs[n_in:]
        else:
            ins, outs, _, xrefs = exchange.split(refs, n_in, n_out)
            _ride(exchange, i == 0, i == nm - 1, xrefs)
        b_ref = ins[n_p]
        acc = None
        for a_ref, off, w in zip(ins[:n_p], offsets, widths):
            term = _nt_dot(a_ref[...].astype(BF16), b_ref[:, off:off + w])
            acc = term if acc is None else acc + term
        if norm_bwd is None:
            outs[0][...] = acc
        else:
            _norm_bwd_tile(acc, tuple(ins[n_p + 1:]), tuple(outs), first=i == 0)

    row, vec, part = pl.BlockSpec((tm, N), lambda i: (i, 0)), _full((1, N)), _full((SUBLANES, N))
    part_shape = jax.ShapeDtypeStruct((SUBLANES, N), F32)
    in_specs = [pl.BlockSpec((tm, w), lambda i: (i, 0)) for w in widths] + [_full(b.shape)]
    out_specs, out_shape = [row], [jax.ShapeDtypeStruct((M, N), F32)]
    if norm_bwd is not None:
        in_specs += [row, row, vec, vec] + ([row, vec] if len(extra) == 6 else [])
        out_specs += [part] * 3 + ([row, part] if len(extra) == 6 else [])
        out_shape += [part_shape] * 3 + ([jax.ShapeDtypeStruct((M, N), BF16), part_shape] if len(extra) == 6 else [])
    sequential = exchange is not None or norm_bwd is not None
    xs = exchange.xs if exchange is not None else []
    result = pl.pallas_call(
        body, name=name, grid=(nm,), in_specs=in_specs + (exchange.in_specs if exchange else []),
        out_specs=out_specs + (exchange.out_specs if exchange else []),
        out_shape=out_shape + (exchange.out_shapes if exchange else []),
        scratch_shapes=exchange.scratch if exchange else [],
        compiler_params=_params(("arbitrary" if sequential else "parallel",)),
    )(*pieces, b, *extra, *xs)
    return result[0] if len(result) == 1 else result


def _matmul_tn_pieces(a, pieces, *, name, tk):
    K, M = a.shape
    widths = [p.shape[1] for p in pieces]
    n_p, nk = len(pieces), K // tk

    def body(*refs):
        a_ref, p_refs, o_refs = refs[0], refs[1:n_p + 1], refs[n_p + 1:]
        k = pl.program_id(0)
        av = a_ref[...].astype(BF16)
        for p_ref, o_ref in zip(p_refs, o_refs):
            prod = _tn_dot(av, p_ref[...].astype(BF16))

            @pl.when(k == 0)
            def _():
                o_ref[...] = prod

            @pl.when(k > 0)
            def _():
                o_ref[...] += prod

    return pl.pallas_call(
        body, name=name, grid=(nk,),
        in_specs=[pl.BlockSpec((tk, M), lambda k: (k, 0))] + [pl.BlockSpec((tk, w), lambda k: (k, 0)) for w in widths],
        out_specs=[_full((M, w)) for w in widths], out_shape=[jax.ShapeDtypeStruct((M, w), F32) for w in widths],
        compiler_params=_params(("arbitrary",)),
    )(a, *pieces)


def _split_dot(x, mat, parts):
    out = None
    rem = x
    for p in range(parts):
        piece = rem.astype(BF16)
        term = jnp.dot(piece, mat, preferred_element_type=F32)
        out = term if out is None else out + term
        if p + 1 < parts:
            rem = rem - piece.astype(F32)
    return out


def _sigmoid(x):
    return 0.5 * jnp.tanh(0.5 * x) + 0.5


def _rows8(x):
    r, c = x.shape
    return jnp.sum(x.reshape(r // SUBLANES, SUBLANES, c), axis=0)


def _shift_down(blk, prev8, n):
    rolled = pltpu.roll(blk, n, axis=0)
    prev_rolled = pltpu.roll(prev8, n, axis=0)
    rows = lax.broadcasted_iota(jnp.int32, prev8.shape, 0)
    first = jnp.where(rows < n, prev_rolled, rolled[0:SUBLANES])
    return jnp.concatenate([first, rolled[SUBLANES:]], axis=0)


def _prev_spec(tm, width, col):
    per = tm // SUBLANES
    return pl.BlockSpec((SUBLANES, width), lambda i, *_: (jnp.maximum(i * per - 1, 0), col))


def _next_spec(tm, width, col, n_tiles):
    per = tm // SUBLANES
    last = n_tiles * per - 1
    return pl.BlockSpec((SUBLANES, width), lambda i, *_: (jnp.minimum((i + 1) * per, last), col))


def _group_matrix():
    idx = np.arange(ATTN_W) // HEAD_DIM
    return jnp.asarray((idx[:, None] == idx[None, :]).astype(np.float32), BF16)


def _norm_mod(x, g, sc, sh, *, name):
    T = x.shape[0]
    tm = min(TOKEN_TILE, T)

    def body(x_ref, g_ref, sc_ref, sh_ref, o_ref):
        xv = x_ref[...]
        inv = lax.rsqrt(jnp.mean(xv * xv, axis=-1, keepdims=True) + RMS_EPS)
        o_ref[...] = ((xv * inv) * g_ref[...] * (1.0 + sc_ref[...]) + sh_ref[...]).astype(BF16)

    row = pl.BlockSpec((tm, D), lambda i: (i, 0))
    return pl.pallas_call(
        body, name=name, grid=(T // tm,),
        in_specs=[row, _full((1, D)), _full((1, D)), _full((1, D))],
        out_specs=row, out_shape=jax.ShapeDtypeStruct((T, D), BF16),
        compiler_params=_params(("parallel",)),
    )(x, g, sc, sh)


LANE_ONE = 64
LANE_F = 67
LANE_LSE = 70
LANE_SUM = 73


def _pieces(x):
    hi = x.astype(BF16).astype(F32)
    rest = x - hi
    mid = rest.astype(BF16).astype(F32)
    return hi, mid, rest - mid


def _run(start, vals):
    return [(start + i, v) for i, v in enumerate(vals)]


def _head_lanes(a, h):
    blk = a[:, LANES * (h // 2):LANES * (h // 2) + LANES]
    return blk if h % 2 == 0 else pltpu.roll(blk, HEAD_DIM, axis=1)


def _branch_prep(proj, fcum, conv_w8, qg, kg, gmat):
    T = proj.shape[0]
    tm = min(TOKEN_TILE, T)
    nt = T // tm

    def body(cb_ref, cc_ref, cv_ref, q_ref, k_ref, v_ref, f_ref, ccp_ref, cvp_ref, w_ref, qg_ref, kg_ref, g_ref,
             ya_ref, qa_ref, ka_ref, va_ref):
        i = pl.program_id(0)
        z = cc_ref[...] * cv_ref[...]
        zp = jnp.where(i > 0, ccp_ref[...] * cvp_ref[...], 0.0)
        w = w_ref[...]
        cz = _shift_down(z, zp, 2) * w[0:1] + _shift_down(z, zp, 1) * w[1:2] + z * w[2:3]
        ya_ref[...] = (cb_ref[...] * cz).astype(BF16)
        gm = g_ref[...]

        def normed(src, gain, scale):
            v = src[...]
            ms = _split_dot(v * v, gm, 2) * (1.0 / HEAD_DIM)
            return (v * lax.rsqrt(ms + RMS_EPS)) * gain[...] * scale

        qn = normed(q_ref, qg_ref, 1.0 / np.sqrt(HEAD_DIM))
        kn = normed(k_ref, kg_ref, 1.0)
        vv = v_ref[...]
        lane = lax.broadcasted_iota(jnp.int32, (tm, LANES), 1)
        low = lane < HEAD_DIM
        in_run = lambda start: jnp.logical_and(lane >= start, lane < start + 3)
        q_ones = jnp.where(jnp.logical_or(in_run(LANE_ONE), lane == LANE_SUM), 1.0, 0.0)
        k_ones = jnp.where(jnp.logical_or(in_run(LANE_F), in_run(LANE_LSE)), 1.0, 0.0)
        v_ones = jnp.where(in_run(LANE_ONE), 1.0, 0.0)
        f3 = jnp.concatenate(_pieces(f_ref[...]), axis=1).astype(BF16)
        src = lax.broadcasted_iota(jnp.int32, (3 * LANES, LANES), 0)
        dst = lax.broadcasted_iota(jnp.int32, (3 * LANES, LANES), 1)
        for h in range(N_HEADS):
            def pick(start, h=h):
                hit = jnp.logical_and(src - h == (dst - start) * LANES, jnp.logical_and(dst >= start, dst < start + 3))
                return jnp.dot(f3, jnp.where(hit, 1.0, 0.0).astype(BF16), preferred_element_type=F32)

            qa_ref[h] = jnp.where(low, _head_lanes(qn, h), q_ones + pick(LANE_F)).astype(BF16)
            ka_ref[h] = jnp.where(low, _head_lanes(kn, h), k_ones - pick(LANE_ONE)).astype(BF16)
            va_ref[h] = jnp.where(low, _head_lanes(vv, h), v_ones).astype(BF16)

    blk = lambda col: pl.BlockSpec((tm, CONV_W), lambda i: (i, col))
    heads = pl.BlockSpec((N_HEADS, tm, LANES), lambda i: (0, i, 0))
    return pl.pallas_call(
        body, name="branch_prep", grid=(nt,),
        in_specs=[blk(0), blk(1), blk(2), blk(3), blk(4), blk(5), pl.BlockSpec((tm, F_PAD), lambda i: (i, 0)),
                  _prev_spec(tm, CONV_W, 1), _prev_spec(tm, CONV_W, 2),
                  _full((SUBLANES, CONV_W)), _full((1, ATTN_W)), _full((1, ATTN_W)), _full((ATTN_W, ATTN_W))],
        out_specs=[pl.BlockSpec((tm, CONV_W), lambda i: (i, 0)), heads, heads, heads],
        out_shape=[jax.ShapeDtypeStruct((T, CONV_W), BF16)] + [jax.ShapeDtypeStruct((N_HEADS, T, LANES), BF16)] * 3,
        compiler_params=_params(("parallel",)),
    )(proj, proj, proj, proj, proj, proj, fcum, proj, proj, conv_w8, qg, kg, gmat)


def _cumsum(x, *, reverse, name, col=0, gate_bias=None):
    T = x.shape[0]
    tm = min(TOKEN_TILE, T)
    nt = T // tm

    def body(x_ref, b_ref, o_ref, carry_ref):
        i = pl.program_id(0)

        @pl.when(i == 0)
        def _():
            carry_ref[...] = jnp.zeros_like(carry_ref)

        r = lax.broadcasted_iota(jnp.int32, (tm, tm), 0)
        c = lax.broadcasted_iota(jnp.int32, (tm, tm), 1)
        tri = jnp.where((c >= r) if reverse else (c <= r), 1.0, 0.0).astype(BF16)
        xv = x_ref[...]
        if gate_bias is not None:
            fx = xv + b_ref[...]
            xv = jnp.minimum(fx, 0.0) - jnp.log(1.0 + jnp.exp(-jnp.abs(fx)))
        out = _split_dot_left(tri, xv, 3) + carry_ref[0:1]
        o_ref[...] = out
        carry_ref[...] = jnp.broadcast_to(out[0:1] if reverse else out[tm - 1:tm], carry_ref.shape)

    rows = (lambda i: nt - 1 - i) if reverse else (lambda i: i)
    bias = jnp.zeros((1, F_PAD), F32) if gate_bias is None else gate_bias
    return pl.pallas_call(
        body, name=name, grid=(nt,),
        in_specs=[pl.BlockSpec((tm, F_PAD), lambda i: (rows(i), col)), _full((1, F_PAD))],
        out_specs=pl.BlockSpec((tm, F_PAD), lambda i: (rows(i), 0)),
        out_shape=jax.ShapeDtypeStruct((T, F_PAD), F32),
        scratch_shapes=[pltpu.VMEM((SUBLANES, F_PAD), F32)],
        compiler_params=_params(("arbitrary",)),
    )(x, bias)


def _split_dot_left(mat, x, parts):
    out = None
    rem = x
    for p in range(parts):
        piece = rem.astype(BF16)
        term = jnp.dot(mat, piece, preferred_element_type=F32)
        out = term if out is None else out + term
        if p + 1 < parts:
            rem = rem - piece.astype(F32)
    return out


def _out_resid_norm(x, merged, w_out, g1, g, sc, sh):
    T = x.shape[0]
    tm = min(TOKEN_TILE, T)

    def body(x_ref, m_ref, w_ref, g1_ref, g_ref, sc_ref, sh_ref, mix_ref, x1_ref, h_ref):
        mix = jnp.dot(m_ref[...], w_ref[...], preferred_element_type=F32)
        mix_ref[...] = mix
        x1 = x_ref[...] + g1_ref[...] * mix
        x1_ref[...] = x1
        inv = lax.rsqrt(jnp.mean(x1 * x1, axis=-1, keepdims=True) + RMS_EPS)
        h_ref[...] = ((x1 * inv) * g_ref[...] * (1.0 + sc_ref[...]) + sh_ref[...]).astype(BF16)

    row = pl.BlockSpec((tm, D), lambda i: (i, 0))
    vec = _full((1, D))
    return pl.pallas_call(
        body, name="out_resid_norm", grid=(T // tm,),
        in_specs=[row, row, _full((D, D)), vec, vec, vec, vec], out_specs=[row, row, row],
        out_shape=[jax.ShapeDtypeStruct((T, D), F32), jax.ShapeDtypeStruct((T, D), F32),
                   jax.ShapeDtypeStruct((T, D), BF16)],
        compiler_params=_params(("parallel",)),
    )(x, merged, w_out, g1, g, sc, sh)


FFN_TM = 256
FFN_TC = 1408


def _ffn_act_fwd(u, w8):
    T = u.shape[0]
    tm = min(FFN_TM, T)
    nt = T // tm
    nc = D_FF // FFN_TC

    def body(ug_ref, uv_ref, ugp_ref, uvp_ref, wg_ref, wv_ref, o_ref, cg_ref, cv_ref):
        i = pl.program_id(1)

        def conv(u_ref, p_ref, w_ref):
            uv = u_ref[...]
            up = jnp.where(i > 0, p_ref[...], 0.0)
            w = w_ref[...]
            return _shift_down(uv, up, 2) * w[0:1] + _shift_down(uv, up, 1) * w[1:2] + uv * w[2:3]

        gate = conv(ug_ref, ugp_ref, wg_ref)
        val = conv(uv_ref, uvp_ref, wv_ref)
        cg_ref[...] = gate.astype(BF16)
        cv_ref[...] = val.astype(BF16)
        o_ref[...] = (gate * _sigmoid(gate) * val).astype(BF16)

    per = tm // SUBLANES
    blk = lambda off: pl.BlockSpec((tm, FFN_TC), lambda j, i: (i, j + off))
    prev = lambda off: pl.BlockSpec((SUBLANES, FFN_TC), lambda j, i: (jnp.maximum(i * per - 1, 0), j + off))
    wblk = lambda off: pl.BlockSpec((SUBLANES, FFN_TC), lambda j, i: (0, j + off))
    return pl.pallas_call(
        body, name="ffn_act_fwd", grid=(nc, nt),
        in_specs=[blk(0), blk(nc), prev(0), prev(nc), wblk(0), wblk(nc)],
        out_specs=[blk(0), blk(0), blk(0)],
        out_shape=[jax.ShapeDtypeStruct((T, D_FF), BF16)] * 3,
        compiler_params=_params(("parallel", "parallel")),
    )(u, u, u, u, w8, w8)


def _down_loss_head(x1, act, w_down, g2, target):
    T = x1.shape[0]
    tm = min(TOKEN_TILE, T)

    def body(x1_ref, a_ref, w_ref, g2_ref, t_ref, dy_ref, dff_ref, loss_ref, dg2_ref):
        i = pl.program_id(0)

        @pl.when(i == 0)
        def _():
            loss_ref[...] = jnp.zeros_like(loss_ref)
            dg2_ref[...] = jnp.zeros_like(dg2_ref)

        ff = jnp.dot(a_ref[...], w_ref[...], preferred_element_type=F32)
        err = x1_ref[...] + g2_ref[...] * ff - t_ref[...]
        dy = err * (1.0 / D)
        dy_ref[...] = dy
        dff_ref[...] = (dy * g2_ref[...]).astype(BF16)
        loss_ref[...] += _rows8(err * err)
        dg2_ref[...] += _rows8(dy * ff)

    row = pl.BlockSpec((tm, D), lambda i: (i, 0))
    acc = _full((SUBLANES, D))
    return pl.pallas_call(
        body, name="down_loss_head", grid=(T // tm,),
        in_specs=[row, pl.BlockSpec((tm, D_FF), lambda i: (i, 0)), _full((D_FF, D)), _full((1, D)), row],
        out_specs=[row, row, acc, acc],
        out_shape=[jax.ShapeDtypeStruct((T, D), F32), jax.ShapeDtypeStruct((T, D), BF16),
                   jax.ShapeDtypeStruct((SUBLANES, D), F32), jax.ShapeDtypeStruct((SUBLANES, D), F32)],
        compiler_params=_params(("arbitrary",)),
    )(x1, act, w_down, g2, target)


def _nt_dot(a, b):
    return lax.dot_general(a, b, (((1,), (1,)), ((), ())), preferred_element_type=F32)


def _causal(n, keys_on_rows=False):
    r = lax.broadcasted_iota(jnp.int32, (n, n), 0)
    c = lax.broadcasted_iota(jnp.int32, (n, n), 1)
    return (c >= r) if keys_on_rows else (c <= r)


def _sweep(lo, hi, step, carry, group=2):
    while group >= 1:
        def several(j, cr, lo=lo, group=group):
            for g in range(group):
                cr = step(lo + group * j + g, cr)
            return cr

        passes = (hi - lo) // group
        carry = lax.fori_loop(0, passes, several, carry)
        lo = lo + group * passes
        group //= 2
    return carry


def _grid_ends(n0, n1):
    i0, i1 = pl.program_id(0), pl.program_id(1)
    return jnp.logical_and(i0 == 0, i1 == 0), jnp.logical_and(i0 == n0 - 1, i1 == n1 - 1)


def _attn_fwd(qa, ka, va, exchange=None):
    nh, T, _ = qa.shape
    bq = min(ATTN_BLOCK, T)
    nq = T // bq

    def body(*refs):
        if exchange is None:
            q_ref, k_ref, v_ref, o_ref, qb_ref = refs
        else:
            (q_ref, k_ref, v_ref), (o_ref, qb_ref), _, xrefs = exchange.split(refs, 3, 2)
            _ride(exchange, *_grid_ends(nh, nq), xrefs)
        qi = pl.program_id(1)
        q = q_ref[0]

        def step(kb, carry, masked=False):
            m, acc = carry
            start = pl.multiple_of(kb * bq, bq)
            s = _nt_dot(q, k_ref[0, pl.ds(start, bq), :])
            if masked:
                s = jnp.where(_causal(bq), s, NEG_INF)
            m_new = jnp.maximum(m, jnp.max(s, axis=-1, keepdims=True))
            p = jnp.exp(s - m_new).astype(BF16)
            acc = jnp.exp(m - m_new) * acc + jnp.dot(p, v_ref[0, pl.ds(start, bq), :], preferred_element_type=F32)
            return m_new, acc

        init = (jnp.full((bq, 1), NEG_INF, F32), jnp.zeros((bq, LANES), F32))
        m, acc = step(qi, _sweep(0, qi, step, init, group=4), masked=True)
        l = acc[:, LANE_ONE:LANE_ONE + 1]
        o_ref[0] = acc / l
        lane = lax.broadcasted_iota(jnp.int32, (bq, LANES), 1)
        qf = q.astype(F32)
        for idx, piece in _run(LANE_LSE, _pieces(m + jnp.log(l))):
            qf = jnp.where(lane == idx, -piece, qf)
        qb_ref[0] = qf.astype(BF16)

    tile = pl.BlockSpec((1, bq, LANES), lambda h, i: (h, i, 0))
    whole = pl.BlockSpec((1, T, LANES), lambda h, i: (h, 0, 0))
    out_shape = [jax.ShapeDtypeStruct((nh, T, LANES), F32), jax.ShapeDtypeStruct((nh, T, LANES), BF16)]
    if exchange is None:
        return pl.pallas_call(
            body, name="attn_fwd", grid=(nh, nq), in_specs=[tile, whole, whole], out_specs=[tile, tile],
            out_shape=out_shape, compiler_params=_params(("parallel", "parallel")),
        )(qa, ka, va)
    return pl.pallas_call(
        body, name="attn_fwd", grid=(nh, nq), in_specs=[tile, whole, whole] + exchange.in_specs,
        out_specs=[tile, tile] + exchange.out_specs, out_shape=out_shape + exchange.out_shapes,
        scratch_shapes=exchange.scratch, compiler_params=_params(("arbitrary", "arbitrary")),
    )(qa, ka, va, *exchange.xs)


def _branch_merge_fwd(ya0, o_h, proj, wba, wbb_heads):
    nh, T, _ = o_h.shape
    tm = min(TOKEN_TILE, T)

    def body(ya0_ref, o_ref, ga_ref, gb_ref, wa_ref, wb_ref, ya_ref, yb_ref, m_ref):
        ya = jnp.dot(ya0_ref[...], wa_ref[...], preferred_element_type=F32)
        yb = jnp.dot(o_ref[0].astype(BF16), wb_ref[0:LANES, :], preferred_element_type=F32)
        for h in range(1, nh):
            yb += jnp.dot(o_ref[h].astype(BF16), wb_ref[h * LANES:(h + 1) * LANES, :], preferred_element_type=F32)
        ya_ref[...] = ya.astype(BF16)
        yb_ref[...] = yb.astype(BF16)
        m_ref[...] = (_sigmoid(ga_ref[...]) * ya + _sigmoid(gb_ref[...]) * yb).astype(BF16)

    row = pl.BlockSpec((tm, D), lambda i: (i, 0))
    return pl.pallas_call(
        body, name="branch_merge_fwd", grid=(T // tm,),
        in_specs=[pl.BlockSpec((tm, CONV_W), lambda i: (i, 0)), pl.BlockSpec((nh, tm, LANES), lambda i: (0, i, 0)),
                  pl.BlockSpec((tm, D), lambda i: (i, COL_GA // D)), pl.BlockSpec((tm, D), lambda i: (i, COL_GB // D)),
                  _full((CONV_W, D)), _full((nh * LANES, D))],
        out_specs=[row, row, row],
        out_shape=[jax.ShapeDtypeStruct((T, D), BF16)] * 3,
        compiler_params=_params(("parallel",)),
    )(ya0, o_h, proj, proj, wba, wbb_heads)


def _branch_b_bwd(dyb, o_h, wbb_heads):
    nh, T, _ = o_h.shape
    tm = min(TOKEN_TILE, T)

    def body(dyb_ref, o_ref, w_ref, out_ref):
        do = _nt_dot(dyb_ref[...], w_ref[...])
        lane = lax.broadcasted_iota(jnp.int32, (tm, LANES), 1)
        for h in range(nh):
            g = do[:, h * LANES:(h + 1) * LANES].astype(BF16).astype(F32)
            delta = jnp.sum(g * o_ref[h], axis=-1, keepdims=True)
            for idx, piece in _run(LANE_ONE, _pieces(delta)):
                g = jnp.where(lane == idx, -piece, g)
            out_ref[h] = g.astype(BF16)

    heads = pl.BlockSpec((nh, tm, LANES), lambda i: (0, i, 0))
    return pl.pallas_call(
        body, name="branch_b_bwd", grid=(T // tm,),
        in_specs=[pl.BlockSpec((tm, D), lambda i: (i, 0)), heads, _full((D, nh * LANES))],
        out_specs=heads, out_shape=jax.ShapeDtypeStruct((nh, T, LANES), BF16),
        compiler_params=_params(("parallel",)),
    )(dyb, o_h, wbb_heads)


def _branch_b_dw(o_h, dyb):
    nh, T, _ = o_h.shape
    tk = min(TOKEN_TILE, T)

    def body(o_ref, dyb_ref, out_ref):
        @pl.when(pl.program_id(0) == 0)
        def _():
            out_ref[...] = jnp.zeros_like(out_ref)

        g = dyb_ref[...]
        for h in range(nh):
            out_ref[h] += _tn_dot(o_ref[h].astype(BF16), g)

    return pl.pallas_call(
        body, name="branch_b_dw", grid=(T // tk,),
        in_specs=[pl.BlockSpec((nh, tk, LANES), lambda k: (0, k, 0)), pl.BlockSpec((tk, D), lambda k: (k, 0))],
        out_specs=_full((nh, LANES, D)), out_shape=jax.ShapeDtypeStruct((nh, LANES, D), F32),
        compiler_params=_params(("arbitrary",)),
    )(o_h, dyb)


def _attn_bwd(qb, ka, va, doa, exchange=None):
    nh, T, _ = qb.shape
    bk = min(ATTN_BLOCK, T)
    nk = T // bk

    def body(*refs):
        if exchange is None:
            q_ref, do_ref, k_ref, v_ref, dq_ref, dk_ref, dv_ref = refs
        else:
            (q_ref, do_ref, k_ref, v_ref), (dq_ref, dk_ref, dv_ref), _, xrefs = exchange.split(refs, 4, 3)
            _ride(exchange, *_grid_ends(nh, nk), xrefs)
        ki = pl.program_id(1)

        @pl.when(ki == 0)
        def _():
            dq_ref[...] = jnp.zeros_like(dq_ref)

        k = k_ref[0]
        v = v_ref[0]

        def step(qi, carry, masked):
            dk, dv = carry
            rows = pl.ds(pl.multiple_of(qi * bk, bk), bk)
            q = q_ref[0, rows, :]
            g = do_ref[0, rows, :]
            pt = jnp.exp(_nt_dot(k, q))
            if masked:
                pt = jnp.where(_causal(bk, keys_on_rows=True), pt, 0.0)
            dv = dv + jnp.dot(pt.astype(BF16), g, preferred_element_type=F32)
            dst = (pt * _nt_dot(v, g)).astype(BF16)
            dk = dk + jnp.dot(dst, q, preferred_element_type=F32)
            dq_ref[0, rows, :] += _tn_dot(dst, k)
            return dk, dv

        init = (jnp.zeros((bk, LANES), F32), jnp.zeros((bk, LANES), F32))
        carry = step(ki, init, True)
        dk_ref[0], dv_ref[0] = _sweep(ki + 1, nk, lambda qi, cr: step(qi, cr, False), carry)

    tile = pl.BlockSpec((1, bk, LANES), lambda h, i: (h, i, 0))
    whole = pl.BlockSpec((1, T, LANES), lambda h, i: (h, 0, 0))
    out_shape = [jax.ShapeDtypeStruct((nh, T, LANES), F32)] * 3
    if exchange is None:
        return pl.pallas_call(
            body, name="attn_bwd", grid=(nh, nk), in_specs=[whole, whole, tile, tile],
            out_specs=[whole, tile, tile], out_shape=out_shape, compiler_params=_params(("parallel", "arbitrary")),
        )(qb, doa, ka, va)
    return pl.pallas_call(
        body, name="attn_bwd", grid=(nh, nk), in_specs=[whole, whole, tile, tile] + exchange.in_specs,
        out_specs=[whole, tile, tile] + exchange.out_specs, out_shape=out_shape + exchange.out_shapes,
        scratch_shapes=exchange.scratch, compiler_params=_params(("arbitrary", "arbitrary")),
    )(qb, doa, ka, va, *exchange.xs)


def _attn_unpack(dq_h, dk_h, dv_h):
    nh, T, _ = dq_h.shape
    tm = min(TOKEN_TILE, T)

    def body(dq_ref, dk_ref, dv_ref, q_out, k_out, v_out, f_out):
        lane = lax.broadcasted_iota(jnp.int32, (tm, LANES), 1)
        low = lane < HEAD_DIM
        for src, dst in ((dq_ref, q_out), (dk_ref, k_out), (dv_ref, v_out)):
            for pair in range(nh // 2):
                both = jnp.where(low, src[2 * pair], pltpu.roll(src[2 * pair + 1], HEAD_DIM, axis=1))
                dst[:, LANES * pair:LANES * (pair + 1)] = both.astype(dst.dtype)
        df = jnp.zeros((tm, LANES), F32)
        for h in range(nh):
            col = dq_ref[h][:, LANE_F:LANE_F + 1] - dk_ref[h][:, LANE_SUM:LANE_SUM + 1]
            df = jnp.where(lane == h, col, df)
        f_out[...] = df

    heads = pl.BlockSpec((nh, tm, LANES), lambda i: (0, i, 0))
    tok = pl.BlockSpec((tm, ATTN_W), lambda i: (i, 0))
    return pl.pallas_call(
        body, name="attn_unpack", grid=(T // tm,), in_specs=[heads, heads, heads],
        out_specs=[tok, tok, tok, pl.BlockSpec((tm, F_PAD), lambda i: (i, 0))],
        out_shape=[jax.ShapeDtypeStruct((T, ATTN_W), BF16)] * 3 + [jax.ShapeDtypeStruct((T, F_PAD), F32)],
        compiler_params=_params(("parallel",)),
    )(dq_h, dk_h, dv_h)


def _ffn_act_bwd(u, cg, cv, da, w8):
    T = u.shape[0]
    tm = min(FFN_TM, T)
    nt = T // tm
    nc = D_FF // FFN_TC

    def body(ug_ref, uv_ref, cg_ref, cv_ref, cgn_ref, cvn_ref, da_ref, dan_ref, wg_ref, wv_ref,
             dug_ref, duv_ref, dwg_ref, dwv_ref):
        i = pl.program_id(1)

        @pl.when(i == 0)
        def _():
            dwg_ref[...] = jnp.zeros_like(dwg_ref)
            dwv_ref[...] = jnp.zeros_like(dwv_ref)

        gate = jnp.concatenate([cg_ref[...], cgn_ref[...]], axis=0).astype(F32)
        val = jnp.concatenate([cv_ref[...], cvn_ref[...]], axis=0).astype(F32)
        dae = jnp.concatenate([da_ref[...], dan_ref[...]], axis=0).astype(F32)
        rows_e = lax.broadcasted_iota(jnp.int32, dae.shape, 0)
        dae = jnp.where(jnp.logical_and(i == nt - 1, rows_e >= tm), 0.0, dae)
        sg = _sigmoid(gate)
        n = tm + BF16_ROWS

        def back(d, u_ref, w_ref, du_ref, dw_ref):
            w = w_ref[...]
            uv = u_ref[...]
            d1 = pltpu.roll(d, n - 1, axis=0)[:tm]
            d2 = pltpu.roll(d, n - 2, axis=0)[:tm]
            d0 = d[:tm]
            du_ref[...] = (d0 * w[2:3] + d1 * w[1:2] + d2 * w[0:1]).astype(BF16)
            rows = [jnp.sum(t * uv, axis=0, keepdims=True) for t in (d2, d1, d0)]
            dw_ref[...] += jnp.concatenate(rows + [jnp.zeros((SUBLANES - 3, FFN_TC), F32)], axis=0)

        back(dae * val * sg * (1.0 + gate * (1.0 - sg)), ug_ref, wg_ref, dug_ref, dwg_ref)
        back(dae * gate * sg, uv_ref, wv_ref, duv_ref, dwv_ref)

    per = tm // BF16_ROWS
    last_blk = nt * per - 1
    blk = lambda off: pl.BlockSpec((tm, FFN_TC), lambda j, i: (i, j + off))
    nxt = pl.BlockSpec((BF16_ROWS, FFN_TC), lambda j, i: (jnp.minimum((i + 1) * per, last_blk), j))
    wblk = lambda off: pl.BlockSpec((SUBLANES, FFN_TC), lambda j, i: (0, j + off))
    dug, duv, dwg, dwv = pl.pallas_call(
        body, name="ffn_act_bwd", grid=(nc, nt),
        in_specs=[blk(0), blk(nc), blk(0), blk(0), nxt, nxt, blk(0), nxt, wblk(0), wblk(nc)],
        out_specs=[blk(0), blk(0), wblk(0), wblk(0)],
        out_shape=[jax.ShapeDtypeStruct((T, D_FF), BF16)] * 2 + [jax.ShapeDtypeStruct((SUBLANES, D_FF), F32)] * 2,
        compiler_params=_params(("parallel", "arbitrary")),
    )(u, u, cg, cv, cg, cv, da, da, w8, w8)
    return dug, duv, jnp.concatenate([dwg, dwv], axis=1)


def _out_merge_bwd(dmix, w_out, ya, yb, proj):
    T = ya.shape[0]
    tm = min(TOKEN_TILE, T)

    def body(dmix_ref, w_ref, ya_ref, yb_ref, ga_ref, gb_ref, dya_ref, dyb_ref, dga_ref, dgb_ref):
        dm = _nt_dot(dmix_ref[...], w_ref[...])
        sa = _sigmoid(ga_ref[...])
        sb = _sigmoid(gb_ref[...])
        dya_ref[...] = (dm * sa).astype(BF16)
        dyb_ref[...] = (dm * sb).astype(BF16)
        dga_ref[...] = (dm * ya_ref[...].astype(F32) * sa * (1.0 - sa)).astype(BF16)
        dgb_ref[...] = (dm * yb_ref[...].astype(F32) * sb * (1.0 - sb)).astype(BF16)

    row = pl.BlockSpec((tm, D), lambda i: (i, 0))
    return pl.pallas_call(
        body, name="out_merge_bwd", grid=(T // tm,),
        in_specs=[row, _full((D, D)), row, row, pl.BlockSpec((tm, D), lambda i: (i, COL_GA // D)),
                  pl.BlockSpec((tm, D), lambda i: (i, COL_GB // D))],
        out_specs=[row] * 4, out_shape=[jax.ShapeDtypeStruct((T, D), BF16)] * 4,
        compiler_params=_params(("parallel",)),
    )(dmix, w_out, ya, yb, proj, proj)


def _conv_branch_bwd(proj, dya0, conv_w8):
    T = proj.shape[0]
    tm = min(FFN_TM, T)
    nt = T // tm

    def body(cb_ref, cc_ref, cv_ref, cbn_ref, ccp_ref, cvp_ref, ccn_ref, cvn_ref, d_ref, dn_ref, w_ref,
             d3_ref, dw_ref):
        i = pl.program_id(0)

        @pl.when(i == 0)
        def _():
            dw_ref[...] = jnp.zeros_like(dw_ref)

        first, last = i == 0, i == nt - 1
        w = w_ref[...]
        cc = jnp.concatenate([ccp_ref[...], cc_ref[...], ccn_ref[...]], axis=0)
        cv = jnp.concatenate([cvp_ref[...], cv_ref[...], cvn_ref[...]], axis=0)
        rows = lax.broadcasted_iota(jnp.int32, cc.shape, 0)
        z = jnp.where(jnp.logical_and(first, rows < SUBLANES), 0.0, cc * cv)
        z1 = pltpu.roll(z, 1, axis=0)
        z2 = pltpu.roll(z, 2, axis=0)
        cz = z2 * w[0:1] + z1 * w[1:2] + z * w[2:3]
        zeros8 = jnp.zeros((SUBLANES, CONV_W), F32)
        de = jnp.concatenate([zeros8, d_ref[...], jnp.where(last, 0.0, dn_ref[...])], axis=0)
        cbe = jnp.concatenate([zeros8, cb_ref[...], cbn_ref[...]], axis=0)
        dcz = de * cbe
        n = tm + 2 * SUBLANES
        dz = dcz * w[2:3] + pltpu.roll(dcz, n - 1, axis=0) * w[1:2] + pltpu.roll(dcz, n - 2, axis=0) * w[0:1]
        inner = slice(SUBLANES, SUBLANES + tm)
        d3_ref[:, 0:CONV_W] = (de * cz)[inner].astype(BF16)
        d3_ref[:, CONV_W:2 * CONV_W] = (dz * cv)[inner].astype(BF16)
        d3_ref[:, 2 * CONV_W:3 * CONV_W] = (dz * cc)[inner].astype(BF16)
        wrows = [jnp.sum((dcz * t)[inner], axis=0, keepdims=True) for t in (z2, z1, z)]
        dw_ref[...] += jnp.concatenate(wrows + [jnp.zeros((SUBLANES - 3, CONV_W), F32)], axis=0)

    blk = lambda col: pl.BlockSpec((tm, CONV_W), lambda i: (i, col))
    out_blk = pl.BlockSpec((tm, CONV_W), lambda i: (i, 0))
    return pl.pallas_call(
        body, name="conv_branch_bwd", grid=(nt,),
        in_specs=[blk(0), blk(1), blk(2), _next_spec(tm, CONV_W, 0, nt),
                  _prev_spec(tm, CONV_W, 1), _prev_spec(tm, CONV_W, 2),
                  _next_spec(tm, CONV_W, 1, nt), _next_spec(tm, CONV_W, 2, nt),
                  out_blk, _next_spec(tm, CONV_W, 0, nt), _full((SUBLANES, CONV_W))],
        out_specs=[pl.BlockSpec((tm, 3 * CONV_W), lambda i: (i, 0)), _full((SUBLANES, CONV_W))],
        out_shape=[jax.ShapeDtypeStruct((T, 3 * CONV_W), BF16), jax.ShapeDtypeStruct((SUBLANES, CONV_W), F32)],
        compiler_params=_params(("arbitrary",)),
    )(proj, proj, proj, proj, proj, proj, proj, proj, dya0, dya0, conv_w8)


def _qk_norm_bwd(proj, dqs, dkh, dlogf, qg, kg, bf_pad, gmat):
    T = proj.shape[0]
    tm = min(TOKEN_TILE, T)

    def body(q_ref, k_ref, f_ref, dqs_ref, dkh_ref, dlf_ref, qg_ref, kg_ref, bf_ref, g_ref,
             dqk_ref, dfl_ref, dqg_ref, dkg_ref, dbf_ref):
        @pl.when(pl.program_id(0) == 0)
        def _():
            dqg_ref[...] = jnp.zeros_like(dqg_ref)
            dkg_ref[...] = jnp.zeros_like(dkg_ref)
            dbf_ref[...] = jnp.zeros_like(dbf_ref)

        gm = g_ref[...]
        for src, d_src, gain, scale, dst, dgain in (
                (q_ref, dqs_ref, qg_ref, 1.0 / np.sqrt(HEAD_DIM), dqk_ref.at[:, 0:ATTN_W], dqg_ref),
                (k_ref, dkh_ref, kg_ref, 1.0, dqk_ref.at[:, ATTN_W:2 * ATTN_W], dkg_ref)):
            v = src[...]
            dhat = d_src[...].astype(F32) * scale
            inv = lax.rsqrt(_split_dot(v * v, gm, 2) * (1.0 / HEAD_DIM) + RMS_EPS)
            vn = v * inv
            dgain[...] += _rows8(dhat * vn)
            dvn = dhat * gain[...]
            mean = _split_dot(dvn * vn, gm, 2) * (1.0 / HEAD_DIM)
            dst[...] = (inv * (dvn - vn * mean)).astype(BF16)
        fx = f_ref[...] + bf_ref[...]
        dfl = dlf_ref[...] * _sigmoid(-fx)
        dfl_ref[...] = dfl.astype(BF16)
        dbf_ref[...] += _rows8(dfl)

    blk = lambda col: pl.BlockSpec((tm, ATTN_W), lambda i: (i, col))
    out_blk = pl.BlockSpec((tm, ATTN_W), lambda i: (i, 0))
    f_in = pl.BlockSpec((tm, F_PAD), lambda i: (i, COL_F // F_PAD))
    f_blk = pl.BlockSpec((tm, F_PAD), lambda i: (i, 0))
    return pl.pallas_call(
        body, name="qk_norm_bwd", grid=(T // tm,),
        in_specs=[blk(3), blk(4), f_in, out_blk, out_blk, f_blk, _full((1, ATTN_W)), _full((1, ATTN_W)),
                  _full((1, F_PAD)), _full((ATTN_W, ATTN_W))],
        out_specs=[pl.BlockSpec((tm, 2 * ATTN_W), lambda i: (i, 0)), f_blk, _full((SUBLANES, ATTN_W)),
                   _full((SUBLANES, ATTN_W)), _full((SUBLANES, F_PAD))],
        out_shape=[jax.ShapeDtypeStruct((T, 2 * ATTN_W), BF16), jax.ShapeDtypeStruct((T, F_PAD), BF16)]
        + [jax.ShapeDtypeStruct((SUBLANES, ATTN_W), F32)] * 2 + [jax.ShapeDtypeStruct((SUBLANES, F_PAD), F32)],
        compiler_params=_params(("arbitrary",)),
    )(proj, proj, proj, dqs, dkh, dlogf, qg, kg, bf_pad, gmat)


def _pad_rows8(w):
    return jnp.pad(w, ((0, SUBLANES - w.shape[0]), (0, 0)))


def _fold8(acc):
    return jnp.sum(acc, axis=0, keepdims=True)


def _late_weights(mats):
    out = {name: mats[name] for name in ("w_branch_a", "w_out", "w_up", "w_down")}
    out["w_branch_b_heads"] = _pad_head_rows(mats["w_branch_b"])
    return out


def _local_step(x, target, mod, wts, late=None):
    T = x.shape[0]
    tb = min(MATMUL_TILE, T)
    tk_long = min(2 * MATMUL_TILE, T)
    tm = min(TOKEN_TILE, T)
    sh1, sc1, g1, sh2, sc2, g2 = [mod[:, i * D:(i + 1) * D] for i in range(N_MOD)]
    w_in = wts["w_in"]
    conv_a8 = _pad_rows8(wts["conv_a_w"])
    conv_f8 = _pad_rows8(wts["conv_ffn_w"])
    qg = jnp.tile(wts["q_norm_g"], (1, N_HEADS))
    kg = jnp.tile(wts["k_norm_g"], (1, N_HEADS))
    bf_pad = jnp.pad(wts["b_f"], ((0, 0), (0, F_PAD - N_HEADS)))
    gmat = _group_matrix()

    h = _norm_mod(x, wts["norm1_g"], sc1, sh1, name="norm1_fwd")
    proj = _matmul(h, w_in, name="mm_in", tm=tb, tn=1792, tk=D)
    fcum = _cumsum(proj, reverse=False, name="gate_cumsum", col=COL_F // F_PAD, gate_bias=bf_pad)
    ya0, qa, ka, va = _branch_prep(proj, fcum, conv_a8, qg, kg, gmat)
    if late is None:
        o_h, qb = _attn_fwd(qa, ka, va)
    else:
        o_h, qb, *gathered = _attn_fwd(qa, ka, va, _Exchange([late[name] for name, *_ in LATE], scatter=False))
        wts = dict(wts)
        mats = {name: _join_shards(g, axis) for (name, _, _, axis), g in zip(LATE, gathered) if name != "w_up"}
        mats["w_up"] = _assemble_columns(gathered[[name for name, *_ in LATE].index("w_up")], 2 * D_FF // N_DEV,
                                         2 * D_FF, ((0, 2 * D_FF, 0),), name="assemble_w_up")
        wts.update(_late_weights(mats))
    ya, yb, merged = _branch_merge_fwd(ya0, o_h, proj, wts["w_branch_a"], wts["w_branch_b_heads"])
    mix, x1, h2 = _out_resid_norm(x, merged, wts["w_out"], g1, wts["norm2_g"], sc2, sh2)
    u = _matmul(h2, wts["w_up"], name="mm_up", tm=tb, tn=2816, tk=D)
    act, conv_gate, conv_val = _ffn_act_fwd(u, conv_f8)
    dy, dff, sq8, dg2_8 = _down_loss_head(x1, act, wts["w_down"], g2, target)
    sq = jnp.sum(sq8).reshape(1, 1)

    grads = {}
    da = _matmul(dff, wts["w_down"], name="mm_down_dx", tm=tb, tn=1408, tk=D, out_dtype=BF16, trans_b=True)
    grads["w_down"] = _matmul(act, dff, name="mm_down_dw", tm=1408, tn=D, tk=tk_long, trans_a=True)
    dug, duv, dconv_f8 = _ffn_act_bwd(u, conv_gate, conv_val, da, conv_f8)
    grads["conv_ffn_w"] = dconv_f8[:3]
    dx1, dsh2_8, dsc2_8, dn2_8, dmix, dg1_8 = _matmul_pieces(
        [dug, duv], wts["w_up"], name="mm_up_dx", tm=tm, norm_bwd=(x1, dy, wts["norm2_g"], sc2, mix, g1))
    dw_up = [_matmul(h2, d, name="mm_up_dw_" + half, tm=D, tn=1408, tk=tk_long, trans_a=True)
             for half, d in (("gate", dug), ("val", duv))]
    if late is None:
        grads["w_up"] = jnp.concatenate(dw_up, axis=1)
    grads["norm2_g"] = _fold8(dn2_8)

    grads["w_out"] = _matmul(merged, dmix, name="mm_out_dw", tm=D, tn=D, tk=tk_long, trans_a=True)
    dya, dyb, dga, dgb = _out_merge_bwd(dmix, wts["w_out"], ya, yb, proj)
    dya0 = _matmul(dya, wts["w_branch_a"], name="mm_branch_a_dx", tm=tb, tn=CONV_W, tk=D, trans_b=True)
    grads["w_branch_a"] = _matmul(ya0, dya, name="mm_branch_a_dw", tm=CONV_W, tn=D, tk=tk_long, trans_a=True)
    doa = _branch_b_bwd(dyb, o_h, wts["w_branch_b_heads"])
    grads["w_branch_b"] = _branch_b_dw(o_h, dyb)[:, :HEAD_DIM].reshape(ATTN_W, D)
    dconv3, dconv_a8 = _conv_branch_bwd(proj, dya0, conv_a8)
    grads["conv_a_w"] = dconv_a8[:3]

    parts = {}
    if late is None:
        dq_h, dk_h, dv_h = _attn_bwd(qb, ka, va, doa)
    else:
        ready = [(_column_shards(dw_up) if name == "w_up" else _split_shards(grads[name], axis)).astype(BF16)
                 for name, _, _, axis in LATE]
        dq_h, dk_h, dv_h, *recv = _attn_bwd(
            qb, ka, va, doa, _Exchange(ready + [_pack_full_by_dest(grads, CONVS, SUBLANES)], scatter=True))
        parts = dict(zip([name for name, *_ in LATE] + ["conv"], recv))
    dq_tok, dk_tok, dv_tok, dfcum = _attn_unpack(dq_h, dk_h, dv_h)
    dlogf = _cumsum(dfcum, reverse=True, name="gate_cumsum_bwd")
    dqk, dfl, dqg8, dkg8, dbf8 = _qk_norm_bwd(proj, dq_tok, dk_tok, dlogf, qg, kg, bf_pad, gmat)
    grads["q_norm_g"] = jnp.sum(_fold8(dqg8).reshape(N_HEADS, HEAD_DIM), axis=0, keepdims=True)
    grads["k_norm_g"] = jnp.sum(_fold8(dkg8).reshape(N_HEADS, HEAD_DIM), axis=0, keepdims=True)
    grads["b_f"] = _fold8(dbf8)[:, :N_HEADS]
    narrow, wide = [dconv3, dqk, dv_tok], [dga, dgb, dfl]
    dw_narrow = _matmul_tn_pieces(h, narrow, name="mm_in_dw_narrow", tk=tb)
    dwa, dwb, dwf = _matmul_tn_pieces(h, wide, name="mm_in_dw_wide", tk=tk_long)
    dw_in = list(dw_narrow) + [dwf[:, :N_HEADS], dwa, dwb]
    norm1 = (x, dx1, wts["norm1_g"], sc1)
    if late is None:
        grads["w_in"] = jnp.concatenate(dw_in, axis=1)
        grad_x, dsh1_8, dsc1_8, dn1_8 = _matmul_pieces(narrow + wide, w_in, name="mm_in_dx", tm=tm,
                                                       norm_bwd=norm1)
    else:
        grad_x, dsh1_8, dsc1_8, dn1_8, parts["w_in"] = _matmul_pieces(
            narrow + wide, w_in, name="mm_in_dx", tm=tm, norm_bwd=norm1,
            exchange=_Exchange([_column_shards(dw_in).astype(BF16)], scatter=True))
    grads["norm1_g"] = _fold8(dn1_8)
    grads["mod"] = jnp.concatenate([_fold8(a) for a in (dsh1_8, dsc1_8, dg1_8, dsh2_8, dsc2_8, dg2_8)], axis=1)
    return sq, grad_x, grads, parts


def _me_and_peers():
    mx, my, mc = lax.axis_index("x"), lax.axis_index("y"), lax.axis_index("c")
    me = 4 * mx + 2 * my + mc
    peers = []
    for k in range(1, N_DEV):
        px = 1 - mx if k & 4 else mx
        py = 1 - my if k & 2 else my
        pc = 1 - mc if k & 1 else mc
        peers.append(((px, py, pc), 4 * px + 2 * py + pc))
    return me, peers


HBM_SPEC = pl.BlockSpec(memory_space=pltpu.HBM)


class _Exchange:
    def __init__(self, xs, scatter):
        self.xs, self.scatter, self.n = list(xs), scatter, len(xs)
        self.out_shapes = [jax.ShapeDtypeStruct(x.shape if scatter else (N_DEV,) + x.shape, x.dtype) for x in xs]
        self.in_specs = [HBM_SPEC] * self.n
        self.out_specs = [HBM_SPEC] * self.n
        self.scratch = [pltpu.SemaphoreType.DMA((self.n, N_DEV - 1)), pltpu.SemaphoreType.DMA((self.n, N_DEV - 1)),
                        pltpu.SemaphoreType.DMA((self.n,))]

    def _copies(self, x_refs, out_refs, sems):
        send_sems, recv_sems, local_sems = sems
        me, peers = _me_and_peers()

        def src(a, idx):
            return x_refs[a].at[idx] if self.scatter else x_refs[a]

        def copy(a, k, from_idx, to_slot, device):
            return pltpu.make_async_remote_copy(
                src_ref=src(a, from_idx), dst_ref=out_refs[a].at[to_slot], send_sem=send_sems.at[a, k],
                recv_sem=recv_sems.at[a, k], device_id=device, device_id_type=MESH)

        local = [pltpu.make_async_copy(src(a, me), out_refs[a].at[me], local_sems.at[a]) for a in range(self.n)]
        sends = [copy(a, k, idx, me, dev) for a in range(self.n) for k, (dev, idx) in enumerate(peers)]
        recvs = [copy(a, k, idx, idx, dev) for a in range(self.n) for k, (dev, idx) in enumerate(peers)]
        return local, sends, recvs

    def start(self, x_refs, out_refs, sems):
        local, sends, _ = self._copies(x_refs, out_refs, sems)
        for cp in local + sends:
            cp.start()

    def wait(self, x_refs, out_refs, sems):
        local, sends, recvs = self._copies(x_refs, out_refs, sems)
        for cp in recvs:
            cp.wait_recv()
        for cp in sends:
            cp.wait_send()
        for cp in local:
            cp.wait()

    def split(self, refs, n_in, n_out):
        n = self.n
        ins, xin = refs[:n_in], refs[n_in:n_in + n]
        outs, xout = refs[n_in + n:n_in + n + n_out], refs[n_in + n + n_out:n_in + 2 * n + n_out]
        rest = refs[n_in + 2 * n + n_out:]
        return ins, outs, rest[:len(rest) - 3], (xin, xout, rest[len(rest) - 3:])


def _ride(exchange, first, last, refs):
    if exchange is None:
        return

    @pl.when(first)
    def _():
        exchange.start(*refs)

    @pl.when(last)
    def _():
        exchange.wait(*refs)


def _gather_two_level(xs, *, name):
    n = len(xs)
    out_shapes = [jax.ShapeDtypeStruct((N_DEV,) + x.shape, x.dtype) for x in xs]

    def body(*refs):
        x_refs, out_refs = refs[:n], refs[n:2 * n]
        send_sems, recv_sems, local_sems = refs[2 * n:]
        x, y, c = lax.axis_index("x"), lax.axis_index("y"), lax.axis_index("c")
        me, sibling = (x, y, c), (x, y, 1 - c)
        chips = [(1 - x, y), (x, 1 - y), (1 - x, 1 - y)]

        def slot(a, dev):
            return out_refs[a].at[4 * dev[0] + 2 * dev[1] + dev[2]]

        def copy(a, k, block, to, src=None):
            return pltpu.make_async_remote_copy(
                src_ref=slot(a, block) if src is None else src, dst_ref=slot(a, block),
                send_sem=send_sems.at[a, k], recv_sem=recv_sems.at[a, k], device_id=to, device_id_type=MESH)

        mine = [pltpu.make_async_copy(x_refs[a], slot(a, me), local_sems.at[a]) for a in range(n)]
        first = [copy(a, 0, me, sibling, src=x_refs[a]) for a in range(n)]
        first += [copy(a, 1 + j, me, (*chip, c), src=x_refs[a]) for a in range(n) for j, chip in enumerate(chips)]
        for cp in mine + first:
            cp.start()
        passed = []
        for a in range(n):
            for j, chip in enumerate(chips):
                copy(a, 1 + j, (*chip, c), me).wait_recv()
                passed.append(copy(a, 4 + j, (*chip, c), sibling))
                passed[-1].start()
        for a in range(n):
            copy(a, 0, sibling, me).wait_recv()
            for j, chip in enumerate(chips):
                copy(a, 4 + j, (*chip, 1 - c), me).wait_recv()
        for cp in first + passed:
            cp.wait_send()
        for cp in mine:
            cp.wait()

    return pl.pallas_call(
        body, name=name, in_specs=[HBM_SPEC] * n, out_specs=[HBM_SPEC] * n, out_shape=out_shapes,
        scratch_shapes=[pltpu.SemaphoreType.DMA((n, N_DEV - 1)), pltpu.SemaphoreType.DMA((n, N_DEV - 1)),
                        pltpu.SemaphoreType.DMA((n,))],
        compiler_params=pltpu.CompilerParams(has_side_effects=True),
    )(*xs)


def _exchange(xs, *, name, scatter):
    ex = _Exchange(xs, scatter)

    def body(*refs):
        _, _, _, xrefs = ex.split(refs, 0, 0)
        ex.start(*xrefs)
        ex.wait(*xrefs)

    return pl.pallas_call(
        body, name=name, in_specs=ex.in_specs, out_specs=ex.out_specs, out_shape=ex.out_shapes,
        scratch_shapes=ex.scratch, compiler_params=pltpu.CompilerParams(has_side_effects=True),
    )(*xs)


def _ada_fwd(c_all, w_shard, b_shard):
    n = w_shard.shape[1]

    def body(c_ref, w_ref, b_ref, o_ref):
        cv = c_ref[...]
        act = (cv * _sigmoid(cv)).astype(BF16)
        o_ref[...] = jnp.dot(act, w_ref[...].astype(BF16), preferred_element_type=F32) + b_ref[...]

    return pl.pallas_call(
        body, name="ada_fwd", in_specs=[_full((N_DEV, D)), _full((D, n)), _full((1, n))],
        out_specs=_full((N_DEV, n)), out_shape=jax.ShapeDtypeStruct((N_DEV, n), F32), grid=(1,),
        compiler_params=_params(("arbitrary",)),
    )(c_all, w_shard, b_shard)


def _ada_bwd(c_all_t, dmod_pad):
    n = dmod_pad.shape[1]

    def body(c_ref, d_ref, o_ref):
        cv = c_ref[...]
        act = (cv * _sigmoid(cv)).astype(BF16)
        o_ref[...] = jnp.dot(act, d_ref[...].astype(BF16), preferred_element_type=F32)

    return pl.pallas_call(
        body, name="ada_bwd", in_specs=[_full((D, LANES)), _full((LANES, n))],
        out_specs=_full((D, n)), out_shape=jax.ShapeDtypeStruct((D, n), F32), grid=(1,),
        compiler_params=_params(("arbitrary",)),
    )(c_all_t, dmod_pad)


ADAM_ROWS = 256


def _adamw(parts, w, m, v, *, name):
    n, R, C = parts.shape
    tr = next((t for t in (ADAM_ROWS, 128, 64, 32, 16, SUBLANES) if R % t == 0), R)

    def body(p_ref, w_ref, m_ref, v_ref, g_ref, d_ref, nm_ref, nv_ref):
        g = p_ref[0].astype(F32)
        for j in range(1, n):
            g = g + p_ref[j].astype(F32)
        g_ref[...] = g
        nm = ADAM_B1 * m_ref[...] + (1.0 - ADAM_B1) * g
        nv = ADAM_B2 * v_ref[...] + (1.0 - ADAM_B2) * (g * g)
        nm_ref[...] = nm
        nv_ref[...] = nv
        m_hat = nm / (1.0 - ADAM_B1 ** ADAM_STEP)
        v_hat = nv / (1.0 - ADAM_B2 ** ADAM_STEP)
        d_ref[...] = -ADAM_LR * (m_hat / (jnp.sqrt(v_hat) + ADAM_EPS) + ADAM_WD * w_ref[...])

    row = pl.BlockSpec((tr, C), lambda i: (i, 0))
    return pl.pallas_call(
        body, name=name, grid=(R // tr,),
        in_specs=[pl.BlockSpec((n, tr, C), lambda i: (0, i, 0)), row, row, row], out_specs=[row] * 4,
        out_shape=[jax.ShapeDtypeStruct((R, C), F32)] * 4,
        compiler_params=_params(("parallel",)),
    )(parts, w, m, v)


SHARDED = (("w_in", D, IN_W, 1), ("w_branch_a", CONV_W, D, 1), ("w_branch_b", ATTN_W, D, 1), ("w_out", D, D, 0),
           ("w_up", D, 2 * D_FF, 1), ("w_down", D_FF, D, 0), ("conv_a_w", 3, CONV_W, 1),
           ("conv_ffn_w", 3, 2 * D_FF, 1))
MATRICES = SHARDED[:6]
LATE = MATRICES[1:]
CONVS = SHARDED[6:]
REPLICATED = (("b_ada", N_MOD * D), ("norm1_g", D), ("norm2_g", D), ("b_f", N_HEADS), ("q_norm_g", HEAD_DIM),
              ("k_norm_g", HEAD_DIM))


def _shard_shape(rows, cols, axis):
    return (rows // N_DEV, cols) if axis == 0 else (rows, cols // N_DEV)


def _pack_rows(flat, multiple):
    length = flat.shape[-1]
    rows = -(-length // PACK_W)
    rows = -(-rows // multiple) * multiple
    pad = [(0, 0)] * (flat.ndim - 1) + [(0, rows * PACK_W - length)]
    return jnp.pad(flat, pad).reshape(flat.shape[:-1] + (rows, PACK_W))


def _pack_shards(shards, spec, multiple, dtype):
    flat = jnp.concatenate([shards[name].reshape(-1).astype(dtype) for name, *_ in spec])
    return _pack_rows(flat, multiple)


def _join_lane_blocks(gathered):
    n, r, c = gathered.shape

    def body(g_ref, o_ref):
        for j in range(n):
            o_ref[:, j * c:(j + 1) * c] = g_ref[j]

    return pl.pallas_call(
        body, name="join_lane_blocks", grid=(1,), in_specs=[_full((n, r, c))], out_specs=_full((r, n * c)),
        out_shape=jax.ShapeDtypeStruct((r, n * c), gathered.dtype), compiler_params=_params(("arbitrary",)),
    )(gathered)


SHARD_PAD = 768


def _assemble_columns(gathered, shard_cols, out_cols, segments, *, name):
    n, rows, padw = gathered.shape
    assert n == N_DEV and padw == SHARD_PAD and shard_cols <= SHARD_PAD

    def body(g_ref, o_ref):
        j = pl.program_id(0)

        @pl.when(j == 0)
        def _():
            o_ref[...] = jnp.zeros_like(o_ref)

        for dev in range(N_DEV):
            @pl.when(j == dev)
            def _(dev=dev):
                x = g_ref[0]
                for lo, hi, delta in segments:
                    a, b = max(lo, dev * shard_cols), min(hi, (dev + 1) * shard_cols)
                    if a >= b:
                        continue
                    base = (a + delta) // LANES * LANES
                    width = -(-(b + delta - base) // LANES) * LANES
                    src = lax.broadcasted_iota(jnp.int32, (padw, width), 0) + dev * shard_cols
                    dst = lax.broadcasted_iota(jnp.int32, (padw, width), 1) + (base - delta)
                    place = jnp.where((src == dst) & (src >= a) & (src < b), 1.0, 0.0).astype(BF16)
                    moved = jnp.dot(x, place, preferred_element_type=F32).astype(BF16)
                    o_ref[:, base:base + width] = o_ref[:, base:base + width] + moved

    return pl.pallas_call(
        body, name=name, grid=(N_DEV,), in_specs=[pl.BlockSpec((1, rows, padw), lambda j: (j, 0, 0))],
        out_specs=_full((rows, out_cols)), out_shape=jax.ShapeDtypeStruct((rows, out_cols), BF16),
        compiler_params=_params(("arbitrary",)),
    )(gathered)


def _pad_shard(w):
    return jnp.pad(w.astype(BF16), ((0, 0), (0, SHARD_PAD - w.shape[1])))


W_IN_SEGMENTS = ((0, COL_GA, 0), (COL_GA, COL_GA + N_HEADS, COL_F - COL_GA), (COL_GA + N_HEADS, IN_W, -N_HEADS))


def _join_shards(gathered, axis):
    if axis == 0:
        return gathered.reshape(N_DEV * gathered.shape[1], gathered.shape[2])
    if gathered.shape[2] == LANES:
        return _join_lane_blocks(gathered)
    return jnp.concatenate([gathered[j] for j in range(N_DEV)], axis=1)


def _column_shards(pieces):
    total = sum(p.shape[1] for p in pieces)
    width = total // N_DEV
    shards = []
    for j in range(N_DEV):
        lo, hi, off, segs = j * width, (j + 1) * width, 0, []
        for p in pieces:
            a, b = max(lo, off), min(hi, off + p.shape[1])
            if a < b:
                segs.append(p[:, a - off:b - off])
            off += p.shape[1]
        shards.append(segs[0] if len(segs) == 1 else jnp.concatenate(segs, axis=1))
    return jnp.stack(shards)


def _split_shards(full, axis):
    if axis == 0:
        return full.reshape(N_DEV, full.shape[0] // N_DEV, full.shape[1])
    c = full.shape[1] // N_DEV
    return jnp.stack([full[:, j * c:(j + 1) * c] for j in range(N_DEV)])


def _unpack_shards(packed, spec):
    flat = packed.reshape(-1)
    out, off = {}, 0
    for name, rows, cols, axis in spec:
        r, c = _shard_shape(rows, cols, axis)
        out[name] = flat[off:off + r * c].reshape(r, c)
        off += r * c
    return out


def _unpack_gathered(gathered, spec):
    flat = gathered.reshape(N_DEV, -1)
    out, off = {}, 0
    for name, rows, cols, axis in spec:
        r, c = _shard_shape(rows, cols, axis)
        seg = flat[:, off:off + r * c].reshape(N_DEV, r, c)
        out[name] = seg.reshape(rows, cols) if axis == 0 else seg.transpose(1, 0, 2).reshape(rows, cols)
        off += r * c
    return out


def _pack_full_by_dest(full, spec, multiple):
    segs = []
    for name, rows, cols, axis in spec:
        r, c = _shard_shape(rows, cols, axis)
        a = full[name]
        seg = a.reshape(N_DEV, r, c) if axis == 0 else a.reshape(rows, N_DEV, c).transpose(1, 0, 2)
        segs.append(seg.reshape(N_DEV, r * c))
    return _pack_rows(jnp.concatenate(segs, axis=1), multiple)


def _pad_head_rows(w):
    n = w.shape[1]
    padded = jnp.pad(w.reshape(N_HEADS, HEAD_DIM, n), ((0, 0), (0, LANES - HEAD_DIM), (0, 0)))
    return padded.reshape(N_HEADS * LANES, n)


def kernel(x, c, w_ada, b_ada, norm1_g, w_in, b_f, conv_a_w, q_norm_g, k_norm_g, w_branch_a, w_branch_b, w_out, norm2_g, w_up, conv_ffn_w, w_down, loss_target, m_w_ada, m_b_ada, m_norm1_g, m_w_in, m_b_f, m_conv_a_w, m_q_norm_g, m_k_norm_g, m_w_branch_a, m_w_branch_b, m_w_out, m_norm2_g, m_w_up, m_conv_ffn_w, m_w_down, v_w_ada, v_b_ada, v_norm1_g, v_w_in, v_b_f, v_conv_a_w, v_q_norm_g, v_k_norm_g, v_w_branch_a, v_w_branch_b, v_w_out, v_norm2_g, v_w_up, v_conv_ffn_w, v_w_down):
    names = ("w_ada", "b_ada", "norm1_g", "w_in", "b_f", "conv_a_w", "q_norm_g", "k_norm_g", "w_branch_a",
             "w_branch_b", "w_out", "norm2_g", "w_up", "conv_ffn_w", "w_down")
    squeeze = lambda a: a[0] if a.ndim == 3 else a
    W = dict(zip(names, map(squeeze, (w_ada, b_ada, norm1_g, w_in, b_f, conv_a_w, q_norm_g, k_norm_g, w_branch_a,
                                      w_branch_b, w_out, norm2_g, w_up, conv_ffn_w, w_down))))
    M = dict(zip(names, map(squeeze, (m_w_ada, m_b_ada, m_norm1_g, m_w_in, m_b_f, m_conv_a_w, m_q_norm_g,
                                      m_k_norm_g, m_w_branch_a, m_w_branch_b, m_w_out, m_norm2_g, m_w_up,
                                      m_conv_ffn_w, m_w_down))))
    V = dict(zip(names, map(squeeze, (v_w_ada, v_b_ada, v_norm1_g, v_w_in, v_b_f, v_conv_a_w, v_q_norm_g,
                                      v_k_norm_g, v_w_branch_a, v_w_branch_b, v_w_out, v_norm2_g, v_w_up,
                                      v_conv_ffn_w, v_w_down))))
    me = 4 * lax.axis_index("x") + 2 * lax.axis_index("y") + lax.axis_index("c")
    ada_n = N_MOD * D // N_DEV

    small = jnp.concatenate([c.reshape(-1), W["conv_a_w"].reshape(-1), W["conv_ffn_w"].reshape(-1)])
    small_all, w_in_all = _gather_two_level([_pack_rows(small, SUBLANES), _pad_shard(W["w_in"])],
                                            name="gather_first")
    small_all = small_all.reshape(N_DEV, -1)
    c_all = small_all[:, :D]
    conv_all = _unpack_gathered(small_all[:, D:], CONVS)

    b_shard = lax.dynamic_slice(W["b_ada"], (0, me * ada_n), (1, ada_n))
    mod_part = _ada_fwd(c_all, W["w_ada"], b_shard)
    mod_all, = _exchange([mod_part], name="gather_mod", scatter=False)
    mod = lax.dynamic_index_in_dim(mod_all, me, axis=1, keepdims=False).reshape(1, N_MOD * D)

    wts = {"w_in": _assemble_columns(w_in_all, IN_W // N_DEV, IN_W_PAD, W_IN_SEGMENTS, name="assemble_w_in")}
    wts.update(conv_all)
    for name in ("norm1_g", "norm2_g", "q_norm_g", "k_norm_g", "b_f"):
        wts[name] = W[name]
    late = {name: _pad_shard(W[name]) if name == "w_up" else W[name].astype(BF16) for name, *_ in LATE}

    sq, grad_x, grads, parts = _local_step(x[0], loss_target[0], mod, wts, late)
    loss = lax.psum(sq[0, 0] * (0.5 / D), AXES)

    grads["b_ada"] = grads["mod"]
    rep_flat = lambda src: jnp.concatenate([src[name].reshape(-1) for name, _ in REPLICATED])
    rep_parts, = _exchange([_pack_rows(rep_flat(grads), 16)], name="gather_small_grads", scatter=False)
    rep_out = _adamw(rep_parts, *[_pack_rows(rep_flat(s), 16) for s in (W, M, V)], name="adamw_replicated")

    dmod_all = rep_parts.reshape(N_DEV, -1)[:, :N_MOD * D]
    dmod_mine = lax.dynamic_slice(dmod_all, (0, me * ada_n), (N_DEV, ada_n))
    g_ada = _ada_bwd(jnp.pad(c_all.T, ((0, 0), (0, LANES - N_DEV))),
                     jnp.pad(dmod_mine, ((0, LANES - N_DEV), (0, 0))))
    ada_out = _adamw(g_ada[None], W["w_ada"], M["w_ada"], V["w_ada"], name="adamw_ada")

    mat_out = {name: _adamw(parts[name], W[name], M[name], V[name], name="adamw_" + name) for name, *_ in MATRICES}
    conv_out = _adamw(parts["conv"], *[_pack_shards(s, CONVS, SUBLANES, F32) for s in (W, M, V)],
                      name="adamw_conv")

    results = []
    for kind in range(4):
        per = {"w_ada": ada_out[kind]}
        per.update({name: out[kind] for name, out in mat_out.items()})
        per.update(_unpack_shards(conv_out[kind], CONVS))
        flat, off = rep_out[kind].reshape(-1), 0
        for name, n in REPLICATED:
            per[name] = flat[off:off + n].reshape(1, n)
            off += n
        results.append(per)
    restore = lambda name, a: a[None] if W[name].ndim == 2 and name not in dict(REPLICATED) else a
    outs = [loss, grad_x[None]]
    for per in results:
        outs.extend(restore(name, per[name]) for name in names)
    return tuple(outs)
```

```python
import jax
import jax.numpy as jnp
import numpy as np
from jax import lax
from jax.experimental import pallas as pl
from jax.experimental.pallas import tpu as pltpu

F32 = jnp.float32
BF16 = jnp.bfloat16

N_DEV = 8
D = 1024
N_HEADS = 8
HEAD_DIM = 64
ATTN_W = 512
CONV_W = 512
D_FF = 2816
N_MOD = 6
IN_W = 5128
RMS_EPS = 1e-6
NEG_INF = -1e30

IN_W_PAD = 5376
COL_GA = 3072
COL_GB = 4096
COL_F = 5120
F_PAD = 128

ADAM_LR = 0.001
ADAM_B1 = 0.9
ADAM_B2 = 0.999
ADAM_EPS = 1e-08
ADAM_WD = 0.01
ADAM_STEP = 10

LANES = 128
SUBLANES = 8
BF16_ROWS = 16
VMEM_LIMIT = 52 * 1024 * 1024
TOKEN_TILE = 512
MATMUL_TILE = 1024
ATTN_BLOCK = 512
PACK_W = 1024

MESH = pl.DeviceIdType.MESH
AXES = ("x", "y", "c")


def _params(sem=None, **kw):
    return pltpu.CompilerParams(dimension_semantics=sem, vmem_limit_bytes=VMEM_LIMIT, **kw)


def _full(shape):
    nd = len(shape)
    return pl.BlockSpec(shape, lambda *_: (0,) * nd)


def _tn_dot(a, b):
    return lax.dot_general(a, b, (((0,), (0,)), ((), ())), preferred_element_type=F32)


def _matmul(a, b, *, name, tm, tn, tk, out_dtype=F32, trans_a=False, trans_b=False, exchange=None):
    assert not (trans_a and trans_b)
    if trans_a:
        K, M = a.shape
    else:
        M, K = a.shape
    N = b.shape[0] if trans_b else b.shape[1]
    assert b.shape[1 if trans_b else 0] == K and M % tm == 0 and N % tn == 0 and K % tk == 0, (name, a.shape, b.shape)
    nm, nn, nk = M // tm, N // tn, K // tk

    def body(*refs):
        if exchange is None:
            a_ref, b_ref, o_ref, *own = refs
        else:
            (a_ref, b_ref), (o_ref,), own, xrefs = exchange.split(refs, 2, 1)
            ids = [pl.program_id(d) for d in range(3)]
            first = jnp.logical_and(jnp.logical_and(ids[0] == 0, ids[1] == 0), ids[2] == 0)
            last = jnp.logical_and(jnp.logical_and(ids[0] == nn - 1, ids[1] == nm - 1), ids[2] == nk - 1)
            _ride(exchange, first, last, xrefs)
        k = pl.program_id(2)
        av = a_ref[...].astype(BF16)
        bv = b_ref[...].astype(BF16)
        if trans_a:
            prod = _tn_dot(av, bv)
        elif trans_b:
            prod = _nt_dot(av, bv)
        else:
            prod = jnp.dot(av, bv, preferred_element_type=F32)
        if nk == 1:
            o_ref[...] = prod.astype(out_dtype)
            return
        acc_ref, = own

        @pl.when(k == 0)
        def _():
            acc_ref[...] = prod

        @pl.when(k > 0)
        def _():
            acc_ref[...] += prod

        @pl.when(k == nk - 1)
        def _():
            o_ref[...] = acc_ref[...].astype(out_dtype)

    if trans_a:
        a_spec = pl.BlockSpec((tk, tm), lambda j, i, k: (k, i))
    else:
        a_spec = pl.BlockSpec((tm, tk), lambda j, i, k: (i, k))
    b_spec = pl.BlockSpec((tn, tk), lambda j, i, k: (j, k)) if trans_b else pl.BlockSpec((tk, tn), lambda j, i, k: (k, j))
    in_specs = [a_spec, b_spec]
    out_spec = pl.BlockSpec((tm, tn), lambda j, i, k: (i, j))
    out_shape = jax.ShapeDtypeStruct((M, N), out_dtype)
    scratch = [pltpu.VMEM((tm, tn), F32)] if nk > 1 else []
    if exchange is None:
        return pl.pallas_call(
            body, name=name, grid=(nn, nm, nk), in_specs=in_specs, out_specs=out_spec, out_shape=out_shape,
            scratch_shapes=scratch, compiler_params=_params(("parallel", "parallel", "arbitrary")),
        )(a, b)
    return pl.pallas_call(
        body, name=name, grid=(nn, nm, nk), in_specs=in_specs + exchange.in_specs,
        out_specs=[out_spec] + exchange.out_specs, out_shape=[out_shape] + exchange.out_shapes,
        scratch_shapes=scratch + exchange.scratch, compiler_params=_params(("arbitrary",) * 3),
    )(a, b, *exchange.xs)


def _norm_bwd_tile(dh, ins, outs, first):
    x_ref, dr_ref, g_ref, sc_ref = ins[:4]
    dx_ref, dsh_ref, dsc_ref, dg_ref = outs[:4]

    @pl.when(first)
    def _():
        for ref in outs[1:4] + outs[5:]:
            ref[...] = jnp.zeros_like(ref)

    xv = x_ref[...]
    gv = g_ref[...]
    one_sc = 1.0 + sc_ref[...]
    inv = lax.rsqrt(jnp.mean(xv * xv, axis=-1, keepdims=True) + RMS_EPS)
    xn = xv * inv
    dxn = dh * (gv * one_sc)
    dx = dr_ref[...] + inv * (dxn - xn * jnp.mean(dxn * xn, axis=-1, keepdims=True))
    dx_ref[...] = dx
    dhxn = dh * xn
    dsh_ref[...] += _rows8(dh)
    dsc_ref[...] += _rows8(dhxn * gv)
    dg_ref[...] += _rows8(dhxn * one_sc)
    if len(ins) == 6:
        mix_ref, g1_ref = ins[4:]
        dmix_ref, dg1_ref = outs[4:]
        dmix_ref[...] = (dx * g1_ref[...]).astype(BF16)
        dg1_ref[...] += _rows8(dx * mix_ref[...])


def _matmul_pieces(pieces, b, *, name, tm, exchange=None, norm_bwd=None):
    M = pieces[0].shape[0]
    widths = [p.shape[1] for p in pieces]
    offsets = [sum(widths[:i]) for i in range(len(widths))]
    N = b.shape[0]
    assert b.shape[1] >= sum(widths) and M % tm == 0, (name, widths, b.shape)
    n_p, nm = len(pieces), M // tm
    extra = list(norm_bwd) if norm_bwd is not None else []
    n_in = n_p + 1 + len(extra)
    n_out = len(extra) if norm_bwd is not None else 1

    def body(*refs):
        i = pl.program_id(0)
        if exchange is None:
            ins, outs = refs[:n_in], refs[n_in:]
        else:
            ins, outs, _, xrefs = exchange.split(refs, n_in, n_out)
            _ride(exchange, i == 0, i == nm - 1, xrefs)
        b_ref = ins[n_p]
        acc = None
        for a_ref, off, w in zip(ins[:n_p], offsets, widths):
            term = _nt_dot(a_ref[...].astype(BF16), b_ref[:, off:off + w])
            acc = term if acc is None else acc + term
        if norm_bwd is None:
            outs[0][...] = acc
        else:
            _norm_bwd_tile(acc, tuple(ins[n_p + 1:]), tuple(outs), first=i == 0)

    row, vec, part = pl.BlockSpec((tm, N), lambda i: (i, 0)), _full((1, N)), _full((SUBLANES, N))
    part_shape = jax.ShapeDtypeStruct((SUBLANES, N), F32)
    in_specs = [pl.BlockSpec((tm, w), lambda i: (i, 0)) for w in widths] + [_full(b.shape)]
    out_specs, out_shape = [row], [jax.ShapeDtypeStruct((M, N), F32)]
    if norm_bwd is not None:
        in_specs += [row, row, vec, vec] + ([row, vec] if len(extra) == 6 else [])
        out_specs += [part] * 3 + ([row, part] if len(extra) == 6 else [])
        out_shape += [part_shape] * 3 + ([jax.ShapeDtypeStruct((M, N), BF16), part_shape] if len(extra) == 6 else [])
    sequential = exchange is not None or norm_bwd is not None
    xs = exchange.xs if exchange is not None else []
    result = pl.pallas_call(
        body, name=name, grid=(nm,), in_specs=in_specs + (exchange.in_specs if exchange else []),
        out_specs=out_specs + (exchange.out_specs if exchange else []),
        out_shape=out_shape + (exchange.out_shapes if exchange else []),
        scratch_shapes=exchange.scratch if exchange else [],
        compiler_params=_params(("arbitrary" if sequential else "parallel",)),
    )(*pieces, b, *extra, *xs)
    return result[0] if len(result) == 1 else result


def _matmul_tn_pieces(a, pieces, *, name, tk):
    K, M = a.shape
    widths = [p.shape[1] for p in pieces]
    n_p, nk = len(pieces), K // tk

    def body(*refs):
        a_ref, p_refs, o_refs = refs[0], refs[1:n_p + 1], refs[n_p + 1:]
        k = pl.program_id(0)
        av = a_ref[...].astype(BF16)
        for p_ref, o_ref in zip(p_refs, o_refs):
            prod = _tn_dot(av, p_ref[...].astype(BF16))

            @pl.when(k == 0)
            def _():
                o_ref[...] = prod

            @pl.when(k > 0)
            def _():
                o_ref[...] += prod

    return pl.pallas_call(
        body, name=name, grid=(nk,),
        in_specs=[pl.BlockSpec((tk, M), lambda k: (k, 0))] + [pl.BlockSpec((tk, w), lambda k: (k, 0)) for w in widths],
        out_specs=[_full((M, w)) for w in widths], out_shape=[jax.ShapeDtypeStruct((M, w), F32) for w in widths],
        compiler_params=_params(("arbitrary",)),
    )(a, *pieces)


def _split_dot(x, mat, parts):
    out = None
    rem = x
    for p in range(parts):
        piece = rem.astype(BF16)
        term = jnp.dot(piece, mat, preferred_element_type=F32)
        out = term if out is None else out + term
        if p + 1 < parts:
            rem = rem - piece.astype(F32)
    return out


def _sigmoid(x):
    return 0.5 * jnp.tanh(0.5 * x) + 0.5


def _rows8(x):
    r, c = x.shape
    return jnp.sum(x.reshape(r // SUBLANES, SUBLANES, c), axis=0)


def _shift_down(blk, prev8, n):
    rolled = pltpu.roll(blk, n, axis=0)
    prev_rolled = pltpu.roll(prev8, n, axis=0)
    rows = lax.broadcasted_iota(jnp.int32, prev8.shape, 0)
    first = jnp.where(rows < n, prev_rolled, rolled[0:SUBLANES])
    return jnp.concatenate([first, rolled[SUBLANES:]], axis=0)


def _prev_spec(tm, width, col):
    per = tm // SUBLANES
    return pl.BlockSpec((SUBLANES, width), lambda i, *_: (jnp.maximum(i * per - 1, 0), col))


def _next_spec(tm, width, col, n_tiles):
    per = tm // SUBLANES
    last = n_tiles * per - 1
    return pl.BlockSpec((SUBLANES, width), lambda i, *_: (jnp.minimum((i + 1) * per, last), col))


def _group_matrix():
    idx = np.arange(ATTN_W) // HEAD_DIM
    return jnp.asarray((idx[:, None] == idx[None, :]).astype(np.float32), BF16)


def _norm_mod(x, g, sc, sh, *, name):
    T = x.shape[0]
    tm = min(TOKEN_TILE, T)

    def body(x_ref, g_ref, sc_ref, sh_ref, o_ref):
        xv = x_ref[...]
        inv = lax.rsqrt(jnp.mean(xv * xv, axis=-1, keepdims=True) + RMS_EPS)
        o_ref[...] = ((xv * inv) * g_ref[...] * (1.0 + sc_ref[...]) + sh_ref[...]).astype(BF16)

    row = pl.BlockSpec((tm, D), lambda i: (i, 0))
    return pl.pallas_call(
        body, name=name, grid=(T // tm,),
        in_specs=[row, _full((1, D)), _full((1, D)), _full((1, D))],
        out_specs=row, out_shape=jax.ShapeDtypeStruct((T, D), BF16),
        compiler_params=_params(("parallel",)),
    )(x, g, sc, sh)


LANE_ONE = 64
LANE_F = 67
LANE_LSE = 70
LANE_SUM = 73


def _pieces(x):
    hi = x.astype(BF16).astype(F32)
    rest = x - hi
    mid = rest.astype(BF16).astype(F32)
    return hi, mid, rest - mid


def _run(start, vals):
    return [(start + i, v) for i, v in enumerate(vals)]


def _head_lanes(a, h):
    blk = a[:, LANES * (h // 2):LANES * (h // 2) + LANES]
    return blk if h % 2 == 0 else pltpu.roll(blk, HEAD_DIM, axis=1)


def _branch_prep(proj, fcum, conv_w8, qg, kg, gmat):
    T = proj.shape[0]
    tm = min(TOKEN_TILE, T)
    nt = T // tm

    def body(cb_ref, cc_ref, cv_ref, q_ref, k_ref, v_ref, f_ref, ccp_ref, cvp_ref, w_ref, qg_ref, kg_ref, g_ref,
             ya_ref, qa_ref, ka_ref, va_ref):
        i = pl.program_id(0)
        z = cc_ref[...] * cv_ref[...]
        zp = jnp.where(i > 0, ccp_ref[...] * cvp_ref[...], 0.0)
        w = w_ref[...]
        cz = _shift_down(z, zp, 2) * w[0:1] + _shift_down(z, zp, 1) * w[1:2] + z * w[2:3]
        ya_ref[...] = (cb_ref[...] * cz).astype(BF16)
        gm = g_ref[...]

        def normed(src, gain, scale):
            v = src[...]
            ms = _split_dot(v * v, gm, 2) * (1.0 / HEAD_DIM)
            return (v * lax.rsqrt(ms + RMS_EPS)) * gain[...] * scale

        qn = normed(q_ref, qg_ref, 1.0 / np.sqrt(HEAD_DIM))
        kn = normed(k_ref, kg_ref, 1.0)
        vv = v_ref[...]
        lane = lax.broadcasted_iota(jnp.int32, (tm, LANES), 1)
        low = lane < HEAD_DIM
        in_run = lambda start: jnp.logical_and(lane >= start, lane < start + 3)
        q_ones = jnp.where(jnp.logical_or(in_run(LANE_ONE), lane == LANE_SUM), 1.0, 0.0)
        k_ones = jnp.where(jnp.logical_or(in_run(LANE_F), in_run(LANE_LSE)), 1.0, 0.0)
        v_ones = jnp.where(in_run(LANE_ONE), 1.0, 0.0)
        f3 = jnp.concatenate(_pieces(f_ref[...]), axis=1).astype(BF16)
        src = lax.broadcasted_iota(jnp.int32, (3 * LANES, LANES), 0)
        dst = lax.broadcasted_iota(jnp.int32, (3 * LANES, LANES), 1)
        for h in range(N_HEADS):
            def pick(start, h=h):
                hit = jnp.logical_and(src - h == (dst - start) * LANES, jnp.logical_and(dst >= start, dst < start + 3))
                return jnp.dot(f3, jnp.where(hit, 1.0, 0.0).astype(BF16), preferred_element_type=F32)

            qa_ref[h] = jnp.where(low, _head_lanes(qn, h), q_ones + pick(LANE_F)).astype(BF16)
            ka_ref[h] = jnp.where(low, _head_lanes(kn, h), k_ones - pick(LANE_ONE)).astype(BF16)
            va_ref[h] = jnp.where(low, _head_lanes(vv, h), v_ones).astype(BF16)

    blk = lambda col: pl.BlockSpec((tm, CONV_W), lambda i: (i, col))
    heads = pl.BlockSpec((N_HEADS, tm, LANES), lambda i: (0, i, 0))
    return pl.pallas_call(
        body, name="branch_prep", grid=(nt,),
        in_specs=[blk(0), blk(1), blk(2), blk(3), blk(4), blk(5), pl.BlockSpec((tm, F_PAD), lambda i: (i, 0)),
                  _prev_spec(tm, CONV_W, 1), _prev_spec(tm, CONV_W, 2),
                  _full((SUBLANES, CONV_W)), _full((1, ATTN_W)), _full((1, ATTN_W)), _full((ATTN_W, ATTN_W))],
        out_specs=[pl.BlockSpec((tm, CONV_W), lambda i: (i, 0)), heads, heads, heads],
        out_shape=[jax.ShapeDtypeStruct((T, CONV_W), BF16)] + [jax.ShapeDtypeStruct((N_HEADS, T, LANES), BF16)] * 3,
        compiler_params=_params(("parallel",)),
    )(proj, proj, proj, proj, proj, proj, fcum, proj, proj, conv_w8, qg, kg, gmat)


def _cumsum(x, *, reverse, name, col=0, gate_bias=None):
    T = x.shape[0]
    tm = min(TOKEN_TILE, T)
    nt = T // tm

    def body(x_ref, b_ref, o_ref, carry_ref):
        i = pl.program_id(0)

        @pl.when(i == 0)
        def _():
            carry_ref[...] = jnp.zeros_like(carry_ref)

        r = lax.broadcasted_iota(jnp.int32, (tm, tm), 0)
        c = lax.broadcasted_iota(jnp.int32, (tm, tm), 1)
        tri = jnp.where((c >= r) if reverse else (c <= r), 1.0, 0.0).astype(BF16)
        xv = x_ref[...]
        if gate_bias is not None:
            fx = xv + b_ref[...]
            xv = jnp.minimum(fx, 0.0) - jnp.log(1.0 + jnp.exp(-jnp.abs(fx)))
        out = _split_dot_left(tri, xv, 3) + carry_ref[0:1]
        o_ref[...] = out
        carry_ref[...] = jnp.broadcast_to(out[0:1] if reverse else out[tm - 1:tm], carry_ref.shape)

    rows = (lambda i: nt - 1 - i) if reverse else (lambda i: i)
    bias = jnp.zeros((1, F_PAD), F32) if gate_bias is None else gate_bias
    return pl.pallas_call(
        body, name=name, grid=(nt,),
        in_specs=[pl.BlockSpec((tm, F_PAD), lambda i: (rows(i), col)), _full((1, F_PAD))],
        out_specs=pl.BlockSpec((tm, F_PAD), lambda i: (rows(i), 0)),
        out_shape=jax.ShapeDtypeStruct((T, F_PAD), F32),
        scratch_shapes=[pltpu.VMEM((SUBLANES, F_PAD), F32)],
        compiler_params=_params(("arbitrary",)),
    )(x, bias)


def _split_dot_left(mat, x, parts):
    out = None
    rem = x
    for p in range(parts):
        piece = rem.astype(BF16)
        term = jnp.dot(mat, piece, preferred_element_type=F32)
        out = term if out is None else out + term
        if p + 1 < parts:
            rem = rem - piece.astype(F32)
    return out


def _out_resid_norm(x, merged, w_out, g1, g, sc, sh):
    T = x.shape[0]
    tm = min(TOKEN_TILE, T)

    def body(x_ref, m_ref, w_ref, g1_ref, g_ref, sc_ref, sh_ref, mix_ref, x1_ref, h_ref):
        mix = jnp.dot(m_ref[...], w_ref[...], preferred_element_type=F32)
        mix_ref[...] = mix
        x1 = x_ref[...] + g1_ref[...] * mix
        x1_ref[...] = x1
        inv = lax.rsqrt(jnp.mean(x1 * x1, axis=-1, keepdims=True) + RMS_EPS)
        h_ref[...] = ((x1 * inv) * g_ref[...] * (1.0 + sc_ref[...]) + sh_ref[...]).astype(BF16)

    row = pl.BlockSpec((tm, D), lambda i: (i, 0))
    vec = _full((1, D))
    return pl.pallas_call(
        body, name="out_resid_norm", grid=(T // tm,),
        in_specs=[row, row, _full((D, D)), vec, vec, vec, vec], out_specs=[row, row, row],
        out_shape=[jax.ShapeDtypeStruct((T, D), F32), jax.ShapeDtypeStruct((T, D), F32),
                   jax.ShapeDtypeStruct((T, D), BF16)],
        compiler_params=_params(("parallel",)),
    )(x, merged, w_out, g1, g, sc, sh)


FFN_TM = 256
FFN_TC = 1408


def _ffn_act_fwd(u, w8):
    T = u.shape[0]
    tm = min(FFN_TM, T)
    nt = T // tm
    nc = D_FF // FFN_TC

    def body(ug_ref, uv_ref, ugp_ref, uvp_ref, wg_ref, wv_ref, o_ref, cg_ref, cv_ref):
        i = pl.program_id(1)

        def conv(u_ref, p_ref, w_ref):
            uv = u_ref[...]
            up = jnp.where(i > 0, p_ref[...], 0.0)
            w = w_ref[...]
            return _shift_down(uv, up, 2) * w[0:1] + _shift_down(uv, up, 1) * w[1:2] + uv * w[2:3]

        gate = conv(ug_ref, ugp_ref, wg_ref)
        val = conv(uv_ref, uvp_ref, wv_ref)
        cg_ref[...] = gate.astype(BF16)
        cv_ref[...] = val.astype(BF16)
        o_ref[...] = (gate * _sigmoid(gate) * val).astype(BF16)

    per = tm // SUBLANES
    blk = lambda off: pl.BlockSpec((tm, FFN_TC), lambda j, i: (i, j + off))
    prev = lambda off: pl.BlockSpec((SUBLANES, FFN_TC), lambda j, i: (jnp.maximum(i * per - 1, 0), j + off))
    wblk = lambda off: pl.BlockSpec((SUBLANES, FFN_TC), lambda j, i: (0, j + off))
    return pl.pallas_call(
        body, name="ffn_act_fwd", grid=(nc, nt),
        in_specs=[blk(0), blk(nc), prev(0), prev(nc), wblk(0), wblk(nc)],
        out_specs=[blk(0), blk(0), blk(0)],
        out_shape=[jax.ShapeDtypeStruct((T, D_FF), BF16)] * 3,
        compiler_params=_params(("parallel", "parallel")),
    )(u, u, u, u, w8, w8)


def _down_loss_head(x1, act, w_down, g2, target):
    T = x1.shape[0]
    tm = min(TOKEN_TILE, T)

    def body(x1_ref, a_ref, w_ref, g2_ref, t_ref, dy_ref, dff_ref, loss_ref, dg2_ref):
        i = pl.program_id(0)

        @pl.when(i == 0)
        def _():
            loss_ref[...] = jnp.zeros_like(loss_ref)
            dg2_ref[...] = jnp.zeros_like(dg2_ref)

        ff = jnp.dot(a_ref[...], w_ref[...], preferred_element_type=F32)
        err = x1_ref[...] + g2_ref[...] * ff - t_ref[...]
        dy = err * (1.0 / D)
        dy_ref[...] = dy
        dff_ref[...] = (dy * g2_ref[...]).astype(BF16)
        loss_ref[...] += _rows8(err * err)
        dg2_ref[...] += _rows8(dy * ff)

    row = pl.BlockSpec((tm, D), lambda i: (i, 0))
    acc = _full((SUBLANES, D))
    return pl.pallas_call(
        body, name="down_loss_head", grid=(T // tm,),
        in_specs=[row, pl.BlockSpec((tm, D_FF), lambda i: (i, 0)), _full((D_FF, D)), _full((1, D)), row],
        out_specs=[row, row, acc, acc],
        out_shape=[jax.ShapeDtypeStruct((T, D), F32), jax.ShapeDtypeStruct((T, D), BF16),
                   jax.ShapeDtypeStruct((SUBLANES, D), F32), jax.ShapeDtypeStruct((SUBLANES, D), F32)],
        compiler_params=_params(("arbitrary",)),
    )(x1, act, w_down, g2, target)


def _nt_dot(a, b):
    return lax.dot_general(a, b, (((1,), (1,)), ((), ())), preferred_element_type=F32)


def _causal(n, keys_on_rows=False):
    r = lax.broadcasted_iota(jnp.int32, (n, n), 0)
    c = lax.broadcasted_iota(jnp.int32, (n, n), 1)
    return (c >= r) if keys_on_rows else (c <= r)


def _sweep(lo, hi, step, carry, group=2):
    while group >= 1:
        def several(j, cr, lo=lo, group=group):
            for g in range(group):
                cr = step(lo + group * j + g, cr)
            return cr

        passes = (hi - lo) // group
        carry = lax.fori_loop(0, passes, several, carry)
        lo = lo + group * passes
        group //= 2
    return carry


def _grid_ends(n0, n1):
    i0, i1 = pl.program_id(0), pl.program_id(1)
    return jnp.logical_and(i0 == 0, i1 == 0), jnp.logical_and(i0 == n0 - 1, i1 == n1 - 1)


def _attn_fwd(qa, ka, va, exchange=None):
    nh, T, _ = qa.shape
    bq = min(ATTN_BLOCK, T)
    nq = T // bq

    def body(*refs):
        if exchange is None:
            q_ref, k_ref, v_ref, o_ref, qb_ref = refs
        else:
            (q_ref, k_ref, v_ref), (o_ref, qb_ref), _, xrefs = exchange.split(refs, 3, 2)
            _ride(exchange, *_grid_ends(nh, nq), xrefs)
        qi = pl.program_id(1)
        q = q_ref[0]

        def step(kb, carry, masked=False):
            m, acc = carry
            start = pl.multiple_of(kb * bq, bq)
            s = _nt_dot(q, k_ref[0, pl.ds(start, bq), :])
            if masked:
                s = jnp.where(_causal(bq), s, NEG_INF)
            m_new = jnp.maximum(m, jnp.max(s, axis=-1, keepdims=True))
            p = jnp.exp(s - m_new).astype(BF16)
            acc = jnp.exp(m - m_new) * acc + jnp.dot(p, v_ref[0, pl.ds(start, bq), :], preferred_element_type=F32)
            return m_new, acc

        init = (jnp.full((bq, 1), NEG_INF, F32), jnp.zeros((bq, LANES), F32))
        m, acc = step(qi, _sweep(0, qi, step, init, group=4), masked=True)
        l = acc[:, LANE_ONE:LANE_ONE + 1]
        o_ref[0] = acc / l
        lane = lax.broadcasted_iota(jnp.int32, (bq, LANES), 1)
        qf = q.astype(F32)
        for idx, piece in _run(LANE_LSE, _pieces(m + jnp.log(l))):
            qf = jnp.where(lane == idx, -piece, qf)
        qb_ref[0] = qf.astype(BF16)

    tile = pl.BlockSpec((1, bq, LANES), lambda h, i: (h, i, 0))
    whole = pl.BlockSpec((1, T, LANES), lambda h, i: (h, 0, 0))
    out_shape = [jax.ShapeDtypeStruct((nh, T, LANES), F32), jax.ShapeDtypeStruct((nh, T, LANES), BF16)]
    if exchange is None:
        return pl.pallas_call(
            body, name="attn_fwd", grid=(nh, nq), in_specs=[tile, whole, whole], out_specs=[tile, tile],
            out_shape=out_shape, compiler_params=_params(("parallel", "parallel")),
        )(qa, ka, va)
    return pl.pallas_call(
        body, name="attn_fwd", grid=(nh, nq), in_specs=[tile, whole, whole] + exchange.in_specs,
        out_specs=[tile, tile] + exchange.out_specs, out_shape=out_shape + exchange.out_shapes,
        scratch_shapes=exchange.scratch, compiler_params=_params(("arbitrary", "arbitrary")),
    )(qa, ka, va, *exchange.xs)


def _branch_merge_fwd(ya0, o_h, proj, wba, wbb_heads):
    nh, T, _ = o_h.shape
    tm = min(TOKEN_TILE, T)

    def body(ya0_ref, o_ref, ga_ref, gb_ref, wa_ref, wb_ref, ya_ref, yb_ref, m_ref):
        ya = jnp.dot(ya0_ref[...], wa_ref[...], preferred_element_type=F32)
        yb = jnp.dot(o_ref[0].astype(BF16), wb_ref[0:LANES, :], preferred_element_type=F32)
        for h in range(1, nh):
            yb += jnp.dot(o_ref[h].astype(BF16), wb_ref[h * LANES:(h + 1) * LANES, :], preferred_element_type=F32)
        ya_ref[...] = ya.astype(BF16)
        yb_ref[...] = yb.astype(BF16)
        m_ref[...] = (_sigmoid(ga_ref[...]) * ya + _sigmoid(gb_ref[...]) * yb).astype(BF16)

    row = pl.BlockSpec((tm, D), lambda i: (i, 0))
    return pl.pallas_call(
        body, name="branch_merge_fwd", grid=(T // tm,),
        in_specs=[pl.BlockSpec((tm, CONV_W), lambda i: (i, 0)), pl.BlockSpec((nh, tm, LANES), lambda i: (0, i, 0)),
                  pl.BlockSpec((tm, D), lambda i: (i, COL_GA // D)), pl.BlockSpec((tm, D), lambda i: (i, COL_GB // D)),
                  _full((CONV_W, D)), _full((nh * LANES, D))],
        out_specs=[row, row, row],
        out_shape=[jax.ShapeDtypeStruct((T, D), BF16)] * 3,
        compiler_params=_params(("parallel",)),
    )(ya0, o_h, proj, proj, wba, wbb_heads)


def _branch_b_bwd(dyb, o_h, wbb_heads):
    nh, T, _ = o_h.shape
    tm = min(TOKEN_TILE, T)

    def body(dyb_ref, o_ref, w_ref, out_ref):
        do = _nt_dot(dyb_ref[...], w_ref[...])
        lane = lax.broadcasted_iota(jnp.int32, (tm, LANES), 1)
        for h in range(nh):
            g = do[:, h * LANES:(h + 1) * LANES].astype(BF16).astype(F32)
            delta = jnp.sum(g * o_ref[h], axis=-1, keepdims=True)
            for idx, piece in _run(LANE_ONE, _pieces(delta)):
                g = jnp.where(lane == idx, -piece, g)
            out_ref[h] = g.astype(BF16)

    heads = pl.BlockSpec((nh, tm, LANES), lambda i: (0, i, 0))
    return pl.pallas_call(
        body, name="branch_b_bwd", grid=(T // tm,),
        in_specs=[pl.BlockSpec((tm, D), lambda i: (i, 0)), heads, _full((D, nh * LANES))],
        out_specs=heads, out_shape=jax.ShapeDtypeStruct((nh, T, LANES), BF16),
        compiler_params=_params(("parallel",)),
    )(dyb, o_h, wbb_heads)


def _branch_b_dw(o_h, dyb):
    nh, T, _ = o_h.shape
    tk = min(TOKEN_TILE, T)

    def body(o_ref, dyb_ref, out_ref):
        @pl.when(pl.program_id(0) == 0)
        def _():
            out_ref[...] = jnp.zeros_like(out_ref)

        g = dyb_ref[...]
        for h in range(nh):
            out_ref[h] += _tn_dot(o_ref[h].astype(BF16), g)

    return pl.pallas_call(
        body, name="branch_b_dw", grid=(T // tk,),
        in_specs=[pl.BlockSpec((nh, tk, LANES), lambda k: (0, k, 0)), pl.BlockSpec((tk, D), lambda k: (k, 0))],
        out_specs=_full((nh, LANES, D)), out_shape=jax.ShapeDtypeStruct((nh, LANES, D), F32),
        compiler_params=_params(("arbitrary",)),
    )(o_h, dyb)


def _attn_bwd(qb, ka, va, doa, exchange=None):
    nh, T, _ = qb.shape
    bk = min(ATTN_BLOCK, T)
    nk = T // bk

    def body(*refs):
        if exchange is None:
            q_ref, do_ref, k_ref, v_ref, dq_ref, dk_ref, dv_ref = refs
        else:
            (q_ref, do_ref, k_ref, v_ref), (dq_ref, dk_ref, dv_ref), _, xrefs = exchange.split(refs, 4, 3)
            _ride(exchange, *_grid_ends(nh, nk), xrefs)
        ki = pl.program_id(1)

        @pl.when(ki == 0)
        def _():
            dq_ref[...] = jnp.zeros_like(dq_ref)

        k = k_ref[0]
        v = v_ref[0]

        def step(qi, carry, masked):
            dk, dv = carry
            rows = pl.ds(pl.multiple_of(qi * bk, bk), bk)
            q = q_ref[0, rows, :]
            g = do_ref[0, rows, :]
            pt = jnp.exp(_nt_dot(k, q))
            if masked:
                pt = jnp.where(_causal(bk, keys_on_rows=True), pt, 0.0)
            dv = dv + jnp.dot(pt.astype(BF16), g, preferred_element_type=F32)
            dst = (pt * _nt_dot(v, g)).astype(BF16)
            dk = dk + jnp.dot(dst, q, preferred_element_type=F32)
            dq_ref[0, rows, :] += _tn_dot(dst, k)
            return dk, dv

        init = (jnp.zeros((bk, LANES), F32), jnp.zeros((bk, LANES), F32))
        carry = step(ki, init, True)
        dk_ref[0], dv_ref[0] = _sweep(ki + 1, nk, lambda qi, cr: step(qi, cr, False), carry)

    tile = pl.BlockSpec((1, bk, LANES), lambda h, i: (h, i, 0))
    whole = pl.BlockSpec((1, T, LANES), lambda h, i: (h, 0, 0))
    out_shape = [jax.ShapeDtypeStruct((nh, T, LANES), F32)] * 3
    if exchange is None:
        return pl.pallas_call(
            body, name="attn_bwd", grid=(nh, nk), in_specs=[whole, whole, tile, tile],
            out_specs=[whole, tile, tile], out_shape=out_shape, compiler_params=_params(("parallel", "arbitrary")),
        )(qb, doa, ka, va)
    return pl.pallas_call(
        body, name="attn_bwd", grid=(nh, nk), in_specs=[whole, whole, tile, tile] + exchange.in_specs,
        out_specs=[whole, tile, tile] + exchange.out_specs, out_shape=out_shape + exchange.out_shapes,
        scratch_shapes=exchange.scratch, compiler_params=_params(("arbitrary", "arbitrary")),
    )(qb, doa, ka, va, *exchange.xs)


def _attn_unpack(dq_h, dk_h, dv_h):
    nh, T, _ = dq_h.shape
    tm = min(TOKEN_TILE, T)

    def body(dq_ref, dk_ref, dv_ref, q_out, k_out, v_out, f_out):
        lane = lax.broadcasted_iota(jnp.int32, (tm, LANES), 1)
        low = lane < HEAD_DIM
        for src, dst in ((dq_ref, q_out), (dk_ref, k_out), (dv_ref, v_out)):
            for pair in range(nh // 2):
                both = jnp.where(low, src[2 * pair], pltpu.roll(src[2 * pair + 1], HEAD_DIM, axis=1))
                dst[:, LANES * pair:LANES * (pair + 1)] = both.astype(dst.dtype)
        df = jnp.zeros((tm, LANES), F32)
        for h in range(nh):
            col = dq_ref[h][:, LANE_F:LANE_F + 1] - dk_ref[h][:, LANE_SUM:LANE_SUM + 1]
            df = jnp.where(lane == h, col, df)
        f_out[...] = df

    heads = pl.BlockSpec((nh, tm, LANES), lambda i: (0, i, 0))
    tok = pl.BlockSpec((tm, ATTN_W), lambda i: (i, 0))
    return pl.pallas_call(
        body, name="attn_unpack", grid=(T // tm,), in_specs=[heads, heads, heads],
        out_specs=[tok, tok, tok, pl.BlockSpec((tm, F_PAD), lambda i: (i, 0))],
        out_shape=[jax.ShapeDtypeStruct((T, ATTN_W), BF16)] * 3 + [jax.ShapeDtypeStruct((T, F_PAD), F32)],
        compiler_params=_params(("parallel",)),
    )(dq_h, dk_h, dv_h)


def _ffn_act_bwd(u, cg, cv, da, w8):
    T = u.shape[0]
    tm = min(FFN_TM, T)
    nt = T // tm
    nc = D_FF // FFN_TC

    def body(ug_ref, uv_ref, cg_ref, cv_ref, cgn_ref, cvn_ref, da_ref, dan_ref, wg_ref, wv_ref,
             dug_ref, duv_ref, dwg_ref, dwv_ref):
        i = pl.program_id(1)

        @pl.when(i == 0)
        def _():
            dwg_ref[...] = jnp.zeros_like(dwg_ref)
            dwv_ref[...] = jnp.zeros_like(dwv_ref)

        gate = jnp.concatenate([cg_ref[...], cgn_ref[...]], axis=0).astype(F32)
        val = jnp.concatenate([cv_ref[...], cvn_ref[...]], axis=0).astype(F32)
        dae = jnp.concatenate([da_ref[...], dan_ref[...]], axis=0).astype(F32)
        rows_e = lax.broadcasted_iota(jnp.int32, dae.shape, 0)
        dae = jnp.where(jnp.logical_and(i == nt - 1, rows_e >= tm), 0.0, dae)
        sg = _sigmoid(gate)
        n = tm + BF16_ROWS

        def back(d, u_ref, w_ref, du_ref, dw_ref):
            w = w_ref[...]
            uv = u_ref[...]
            d1 = pltpu.roll(d, n - 1, axis=0)[:tm]
            d2 = pltpu.roll(d, n - 2, axis=0)[:tm]
            d0 = d[:tm]
            du_ref[...] = (d0 * w[2:3] + d1 * w[1:2] + d2 * w[0:1]).astype(BF16)
            rows = [jnp.sum(t * uv, axis=0, keepdims=True) for t in (d2, d1, d0)]
            dw_ref[...] += jnp.concatenate(rows + [jnp.zeros((SUBLANES - 3, FFN_TC), F32)], axis=0)

        back(dae * val * sg * (1.0 + gate * (1.0 - sg)), ug_ref, wg_ref, dug_ref, dwg_ref)
        back(dae * gate * sg, uv_ref, wv_ref, duv_ref, dwv_ref)

    per = tm // BF16_ROWS
    last_blk = nt * per - 1
    blk = lambda off: pl.BlockSpec((tm, FFN_TC), lambda j, i: (i, j + off))
    nxt = pl.BlockSpec((BF16_ROWS, FFN_TC), lambda j, i: (jnp.minimum((i + 1) * per, last_blk), j))
    wblk = lambda off: pl.BlockSpec((SUBLANES, FFN_TC), lambda j, i: (0, j + off))
    dug, duv, dwg, dwv = pl.pallas_call(
        body, name="ffn_act_bwd", grid=(nc, nt),
        in_specs=[blk(0), blk(nc), blk(0), blk(0), nxt, nxt, blk(0), nxt, wblk(0), wblk(nc)],
        out_specs=[blk(0), blk(0), wblk(0), wblk(0)],
        out_shape=[jax.ShapeDtypeStruct((T, D_FF), BF16)] * 2 + [jax.ShapeDtypeStruct((SUBLANES, D_FF), F32)] * 2,
        compiler_params=_params(("parallel", "arbitrary")),
    )(u, u, cg, cv, cg, cv, da, da, w8, w8)
    return dug, duv, jnp.concatenate([dwg, dwv], axis=1)


def _out_merge_bwd(dmix, w_out, ya, yb, proj):
    T = ya.shape[0]
    tm = min(TOKEN_TILE, T)

    def body(dmix_ref, w_ref, ya_ref, yb_ref, ga_ref, gb_ref, dya_ref, dyb_ref, dga_ref, dgb_ref):
        dm = _nt_dot(dmix_ref[...], w_ref[...])
        sa = _sigmoid(ga_ref[...])
        sb = _sigmoid(gb_ref[...])
        dya_ref[...] = (dm * sa).astype(BF16)
        dyb_ref[...] = (dm * sb).astype(BF16)
        dga_ref[...] = (dm * ya_ref[...].astype(F32) * sa * (1.0 - sa)).astype(BF16)
        dgb_ref[...] = (dm * yb_ref[...].astype(F32) * sb * (1.0 - sb)).astype(BF16)

    row = pl.BlockSpec((tm, D), lambda i: (i, 0))
    return pl.pallas_call(
        body, name="out_merge_bwd", grid=(T // tm,),
        in_specs=[row, _full((D, D)), row, row, pl.BlockSpec((tm, D), lambda i: (i, COL_GA // D)),
                  pl.BlockSpec((tm, D), lambda i: (i, COL_GB // D))],
        out_specs=[row] * 4, out_shape=[jax.ShapeDtypeStruct((T, D), BF16)] * 4,
        compiler_params=_params(("parallel",)),
    )(dmix, w_out, ya, yb, proj, proj)


def _conv_branch_bwd(proj, dya0, conv_w8):
    T = proj.shape[0]
    tm = min(FFN_TM, T)
    nt = T // tm

    def body(cb_ref, cc_ref, cv_ref, cbn_ref, ccp_ref, cvp_ref, ccn_ref, cvn_ref, d_ref, dn_ref, w_ref,
             d3_ref, dw_ref):
        i = pl.program_id(0)

        @pl.when(i == 0)
        def _():
            dw_ref[...] = jnp.zeros_like(dw_ref)

        first, last = i == 0, i == nt - 1
        w = w_ref[...]
        cc = jnp.concatenate([ccp_ref[...], cc_ref[...], ccn_ref[...]], axis=0)
        cv = jnp.concatenate([cvp_ref[...], cv_ref[...], cvn_ref[...]], axis=0)
        rows = lax.broadcasted_iota(jnp.int32, cc.shape, 0)
        z = jnp.where(jnp.logical_and(first, rows < SUBLANES), 0.0, cc * cv)
        z1 = pltpu.roll(z, 1, axis=0)
        z2 = pltpu.roll(z, 2, axis=0)
        cz = z2 * w[0:1] + z1 * w[1:2] + z * w[2:3]
        zeros8 = jnp.zeros((SUBLANES, CONV_W), F32)
        de = jnp.concatenate([zeros8, d_ref[...], jnp.where(last, 0.0, dn_ref[...])], axis=0)
        cbe = jnp.concatenate([zeros8, cb_ref[...], cbn_ref[...]], axis=0)
        dcz = de * cbe
        n = tm + 2 * SUBLANES
        dz = dcz * w[2:3] + pltpu.roll(dcz, n - 1, axis=0) * w[1:2] + pltpu.roll(dcz, n - 2, axis=0) * w[0:1]
        inner = slice(SUBLANES, SUBLANES + tm)
        d3_ref[:, 0:CONV_W] = (de * cz)[inner].astype(BF16)
        d3_ref[:, CONV_W:2 * CONV_W] = (dz * cv)[inner].astype(BF16)
        d3_ref[:, 2 * CONV_W:3 * CONV_W] = (dz * cc)[inner].astype(BF16)
        wrows = [jnp.sum((dcz * t)[inner], axis=0, keepdims=True) for t in (z2, z1, z)]
        dw_ref[...] += jnp.concatenate(wrows + [jnp.zeros((SUBLANES - 3, CONV_W), F32)], axis=0)

    blk = lambda col: pl.BlockSpec((tm, CONV_W), lambda i: (i, col))
    out_blk = pl.BlockSpec((tm, CONV_W), lambda i: (i, 0))
    return pl.pallas_call(
        body, name="conv_branch_bwd", grid=(nt,),
        in_specs=[blk(0), blk(1), blk(2), _next_spec(tm, CONV_W, 0, nt),
                  _prev_spec(tm, CONV_W, 1), _prev_spec(tm, CONV_W, 2),
                  _next_spec(tm, CONV_W, 1, nt), _next_spec(tm, CONV_W, 2, nt),
                  out_blk, _next_spec(tm, CONV_W, 0, nt), _full((SUBLANES, CONV_W))],
        out_specs=[pl.BlockSpec((tm, 3 * CONV_W), lambda i: (i, 0)), _full((SUBLANES, CONV_W))],
        out_shape=[jax.ShapeDtypeStruct((T, 3 * CONV_W), BF16), jax.ShapeDtypeStruct((SUBLANES, CONV_W), F32)],
        compiler_params=_params(("arbitrary",)),
    )(proj, proj, proj, proj, proj, proj, proj, proj, dya0, dya0, conv_w8)


def _qk_norm_bwd(proj, dqs, dkh, dlogf, qg, kg, bf_pad, gmat):
    T = proj.shape[0]
    tm = min(TOKEN_TILE, T)

    def body(q_ref, k_ref, f_ref, dqs_ref, dkh_ref, dlf_ref, qg_ref, kg_ref, bf_ref, g_ref,
             dqk_ref, dfl_ref, dqg_ref, dkg_ref, dbf_ref):
        @pl.when(pl.program_id(0) == 0)
        def _():
            dqg_ref[...] = jnp.zeros_like(dqg_ref)
            dkg_ref[...] = jnp.zeros_like(dkg_ref)
            dbf_ref[...] = jnp.zeros_like(dbf_ref)

        gm = g_ref[...]
        for src, d_src, gain, scale, dst, dgain in (
                (q_ref, dqs_ref, qg_ref, 1.0 / np.sqrt(HEAD_DIM), dqk_ref.at[:, 0:ATTN_W], dqg_ref),
                (k_ref, dkh_ref, kg_ref, 1.0, dqk_ref.at[:, ATTN_W:2 * ATTN_W], dkg_ref)):
            v = src[...]
            dhat = d_src[...].astype(F32) * scale
            inv = lax.rsqrt(_split_dot(v * v, gm, 2) * (1.0 / HEAD_DIM) + RMS_EPS)
            vn = v * inv
            dgain[...] += _rows8(dhat * vn)
            dvn = dhat * gain[...]
            mean = _split_dot(dvn * vn, gm, 2) * (1.0 / HEAD_DIM)
            dst[...] = (inv * (dvn - vn * mean)).astype(BF16)
        fx = f_ref[...] + bf_ref[...]
        dfl = dlf_ref[...] * _sigmoid(-fx)
        dfl_ref[...] = dfl.astype(BF16)
        dbf_ref[...] += _rows8(dfl)

    blk = lambda col: pl.BlockSpec((tm, ATTN_W), lambda i: (i, col))
    out_blk = pl.BlockSpec((tm, ATTN_W), lambda i: (i, 0))
    f_in = pl.BlockSpec((tm, F_PAD), lambda i: (i, COL_F // F_PAD))
    f_blk = pl.BlockSpec((tm, F_PAD), lambda i: (i, 0))
    return pl.pallas_call(
        body, name="qk_norm_bwd", grid=(T // tm,),
        in_specs=[blk(3), blk(4), f_in, out_blk, out_blk, f_blk, _full((1, ATTN_W)), _full((1, ATTN_W)),
                  _full((1, F_PAD)), _full((ATTN_W, ATTN_W))],
        out_specs=[pl.BlockSpec((tm, 2 * ATTN_W), lambda i: (i, 0)), f_blk, _full((SUBLANES, ATTN_W)),
                   _full((SUBLANES, ATTN_W)), _full((SUBLANES, F_PAD))],
        out_shape=[jax.ShapeDtypeStruct((T, 2 * ATTN_W), BF16), jax.ShapeDtypeStruct((T, F_PAD), BF16)]
        + [jax.ShapeDtypeStruct((SUBLANES, ATTN_W), F32)] * 2 + [jax.ShapeDtypeStruct((SUBLANES, F_PAD), F32)],
        compiler_params=_params(("arbitrary",)),
    )(proj, proj, proj, dqs, dkh, dlogf, qg, kg, bf_pad, gmat)


def _pad_rows8(w):
    return jnp.pad(w, ((0, SUBLANES - w.shape[0]), (0, 0)))


def _fold8(acc):
    return jnp.sum(acc, axis=0, keepdims=True)


def _late_weights(mats):
    out = {name: mats[name] for name in ("w_branch_a", "w_out", "w_up", "w_down")}
    out["w_branch_b_heads"] = _pad_head_rows(mats["w_branch_b"])
    return out


def _local_step(x, target, mod, wts, late=None):
    T = x.shape[0]
    tb = min(MATMUL_TILE, T)
    tk_long = min(2 * MATMUL_TILE, T)
    tm = min(TOKEN_TILE, T)
    sh1, sc1, g1, sh2, sc2, g2 = [mod[:, i * D:(i + 1) * D] for i in range(N_MOD)]
    w_in = wts["w_in"]
    conv_a8 = _pad_rows8(wts["conv_a_w"])
    conv_f8 = _pad_rows8(wts["conv_ffn_w"])
    qg = jnp.tile(wts["q_norm_g"], (1, N_HEADS))
    kg = jnp.tile(wts["k_norm_g"], (1, N_HEADS))
    bf_pad = jnp.pad(wts["b_f"], ((0, 0), (0, F_PAD - N_HEADS)))
    gmat = _group_matrix()

    h = _norm_mod(x, wts["norm1_g"], sc1, sh1, name="norm1_fwd")
    proj = _matmul(h, w_in, name="mm_in", tm=tb, tn=2688, tk=D)
    fcum = _cumsum(proj, reverse=False, name="gate_cumsum", col=COL_F // F_PAD, gate_bias=bf_pad)
    ya0, qa, ka, va = _branch_prep(proj, fcum, conv_a8, qg, kg, gmat)
    if late is None:
        o_h, qb = _attn_fwd(qa, ka, va)
    else:
        o_h, qb, *gathered = _attn_fwd(qa, ka, va, _Exchange([late[name] for name, *_ in LATE], scatter=False))
        wts = dict(wts)
        mats = {name: _join_shards(g, axis) for (name, _, _, axis), g in zip(LATE, gathered) if name != "w_up"}
        mats["w_up"] = _assemble_columns(gathered[[name for name, *_ in LATE].index("w_up")], 2 * D_FF // N_DEV,
                                         2 * D_FF, ((0, 2 * D_FF, 0),), name="assemble_w_up")
        wts.update(_late_weights(mats))
    ya, yb, merged = _branch_merge_fwd(ya0, o_h, proj, wts["w_branch_a"], wts["w_branch_b_heads"])
    mix, x1, h2 = _out_resid_norm(x, merged, wts["w_out"], g1, wts["norm2_g"], sc2, sh2)
    u = _matmul(h2, wts["w_up"], name="mm_up", tm=tb, tn=2816, tk=D)
    act, conv_gate, conv_val = _ffn_act_fwd(u, conv_f8)
    dy, dff, sq8, dg2_8 = _down_loss_head(x1, act, wts["w_down"], g2, target)
    sq = jnp.sum(sq8).reshape(1, 1)

    grads = {}
    da = _matmul(dff, wts["w_down"], name="mm_down_dx", tm=tb, tn=D_FF, tk=D, out_dtype=BF16, trans_b=True)
    grads["w_down"] = _matmul(act, dff, name="mm_down_dw", tm=1408, tn=D, tk=tk_long, trans_a=True)
    dug, duv, dconv_f8 = _ffn_act_bwd(u, conv_gate, conv_val, da, conv_f8)
    grads["conv_ffn_w"] = dconv_f8[:3]
    dx1, dsh2_8, dsc2_8, dn2_8, dmix, dg1_8 = _matmul_pieces(
        [dug, duv], wts["w_up"], name="mm_up_dx", tm=tm, norm_bwd=(x1, dy, wts["norm2_g"], sc2, mix, g1))
    dw_up = [_matmul(h2, d, name="mm_up_dw_" + half, tm=D, tn=1408, tk=tk_long, trans_a=True)
             for half, d in (("gate", dug), ("val", duv))]
    if late is None:
        grads["w_up"] = jnp.concatenate(dw_up, axis=1)
    grads["norm2_g"] = _fold8(dn2_8)

    grads["w_out"] = _matmul(merged, dmix, name="mm_out_dw", tm=D, tn=D, tk=tk_long, trans_a=True)
    dya, dyb, dga, dgb = _out_merge_bwd(dmix, wts["w_out"], ya, yb, proj)
    dya0 = _matmul(dya, wts["w_branch_a"], name="mm_branch_a_dx", tm=tb, tn=CONV_W, tk=D, trans_b=True)
    grads["w_branch_a"] = _matmul(ya0, dya, name="mm_branch_a_dw", tm=CONV_W, tn=D, tk=tk_long, trans_a=True)
    doa = _branch_b_bwd(dyb, o_h, wts["w_branch_b_heads"])
    grads["w_branch_b"] = _branch_b_dw(o_h, dyb)[:, :HEAD_DIM].reshape(ATTN_W, D)
    dconv3, dconv_a8 = _conv_branch_bwd(proj, dya0, conv_a8)
    grads["conv_a_w"] = dconv_a8[:3]

    parts = {}
    if late is None:
        dq_h, dk_h, dv_h = _attn_bwd(qb, ka, va, doa)
    else:
        ready = [(_column_shards(dw_up) if name == "w_up" else _split_shards(grads[name], axis)).astype(BF16)
                 for name, _, _, axis in LATE]
        dq_h, dk_h, dv_h, *recv = _attn_bwd(
            qb, ka, va, doa, _Exchange(ready + [_pack_full_by_dest(grads, CONVS, SUBLANES)], scatter=True))
        parts = dict(zip([name for name, *_ in LATE] + ["conv"], recv))
    dq_tok, dk_tok, dv_tok, dfcum = _attn_unpack(dq_h, dk_h, dv_h)
    dlogf = _cumsum(dfcum, reverse=True, name="gate_cumsum_bwd")
    dqk, dfl, dqg8, dkg8, dbf8 = _qk_norm_bwd(proj, dq_tok, dk_tok, dlogf, qg, kg, bf_pad, gmat)
    grads["q_norm_g"] = jnp.sum(_fold8(dqg8).reshape(N_HEADS, HEAD_DIM), axis=0, keepdims=True)
    grads["k_norm_g"] = jnp.sum(_fold8(dkg8).reshape(N_HEADS, HEAD_DIM), axis=0, keepdims=True)
    grads["b_f"] = _fold8(dbf8)[:, :N_HEADS]
    narrow, wide = [dconv3, dqk, dv_tok], [dga, dgb, dfl]
    dw_narrow = _matmul_tn_pieces(h, narrow, name="mm_in_dw_narrow", tk=tb)
    dwa, dwb, dwf = _matmul_tn_pieces(h, wide, name="mm_in_dw_wide", tk=tk_long)
    dw_in = list(dw_narrow) + [dwf[:, :N_HEADS], dwa, dwb]
    norm1 = (x, dx1, wts["norm1_g"], sc1)
    if late is None:
        grads["w_in"] = jnp.concatenate(dw_in, axis=1)
        grad_x, dsh1_8, dsc1_8, dn1_8 = _matmul_pieces(narrow + wide, w_in, name="mm_in_dx", tm=tm,
                                                       norm_bwd=norm1)
    else:
        grad_x, dsh1_8, dsc1_8, dn1_8, parts["w_in"] = _matmul_pieces(
            narrow + wide, w_in, name="mm_in_dx", tm=tm, norm_bwd=norm1,
            exchange=_Exchange([_column_shards(dw_in).astype(BF16)], scatter=True))
    grads["norm1_g"] = _fold8(dn1_8)
    grads["mod"] = jnp.concatenate([_fold8(a) for a in (dsh1_8, dsc1_8, dg1_8, dsh2_8, dsc2_8, dg2_8)], axis=1)
    return sq, grad_x, grads, parts


def _me_and_peers():
    mx, my, mc = lax.axis_index("x"), lax.axis_index("y"), lax.axis_index("c")
    me = 4 * mx + 2 * my + mc
    peers = []
    for k in range(1, N_DEV):
        px = 1 - mx if k & 4 else mx
        py = 1 - my if k & 2 else my
        pc = 1 - mc if k & 1 else mc
        peers.append(((px, py, pc), 4 * px + 2 * py + pc))
    return me, peers


HBM_SPEC = pl.BlockSpec(memory_space=pltpu.HBM)


class _Exchange:
    def __init__(self, xs, scatter):
        self.xs, self.scatter, self.n = list(xs), scatter, len(xs)
        self.out_shapes = [jax.ShapeDtypeStruct(x.shape if scatter else (N_DEV,) + x.shape, x.dtype) for x in xs]
        self.in_specs = [HBM_SPEC] * self.n
        self.out_specs = [HBM_SPEC] * self.n
        self.scratch = [pltpu.SemaphoreType.DMA((self.n, N_DEV - 1)), pltpu.SemaphoreType.DMA((self.n, N_DEV - 1)),
                        pltpu.SemaphoreType.DMA((self.n,))]

    def _copies(self, x_refs, out_refs, sems):
        send_sems, recv_sems, local_sems = sems
        me, peers = _me_and_peers()

        def src(a, idx):
            return x_refs[a].at[idx] if self.scatter else x_refs[a]

        def copy(a, k, from_idx, to_slot, device):
            return pltpu.make_async_remote_copy(
                src_ref=src(a, from_idx), dst_ref=out_refs[a].at[to_slot], send_sem=send_sems.at[a, k],
                recv_sem=recv_sems.at[a, k], device_id=device, device_id_type=MESH)

        local = [pltpu.make_async_copy(src(a, me), out_refs[a].at[me], local_sems.at[a]) for a in range(self.n)]
        sends = [copy(a, k, idx, me, dev) for a in range(self.n) for k, (dev, idx) in enumerate(peers)]
        recvs = [copy(a, k, idx, idx, dev) for a in range(self.n) for k, (dev, idx) in enumerate(peers)]
        return local, sends, recvs

    def start(self, x_refs, out_refs, sems):
        local, sends, _ = self._copies(x_refs, out_refs, sems)
        for cp in local + sends:
            cp.start()

    def wait(self, x_refs, out_refs, sems):
        local, sends, recvs = self._copies(x_refs, out_refs, sems)
        for cp in recvs:
            cp.wait_recv()
        for cp in sends:
            cp.wait_send()
        for cp in local:
            cp.wait()

    def split(self, refs, n_in, n_out):
        n = self.n
        ins, xin = refs[:n_in], refs[n_in:n_in + n]
        outs, xout = refs[n_in + n:n_in + n + n_out], refs[n_in + n + n_out:n_in + 2 * n + n_out]
        rest = refs[n_in + 2 * n + n_out:]
        return ins, outs, rest[:len(rest) - 3], (xin, xout, rest[len(rest) - 3:])


def _ride(exchange, first, last, refs):
    if exchange is None:
        return

    @pl.when(first)
    def _():
        exchange.start(*refs)

    @pl.when(last)
    def _():
        exchange.wait(*refs)


def _gather_two_level(xs, *, name):
    n = len(xs)
    out_shapes = [jax.ShapeDtypeStruct((N_DEV,) + x.shape, x.dtype) for x in xs]

    def body(*refs):
        x_refs, out_refs = refs[:n], refs[n:2 * n]
        send_sems, recv_sems, local_sems = refs[2 * n:]
        x, y, c = lax.axis_index("x"), lax.axis_index("y"), lax.axis_index("c")
        me, sibling = (x, y, c), (x, y, 1 - c)
        chips = [(1 - x, y), (x, 1 - y), (1 - x, 1 - y)]

        def slot(a, dev):
            return out_refs[a].at[4 * dev[0] + 2 * dev[1] + dev[2]]

        def copy(a, k, block, to, src=None):
            return pltpu.make_async_remote_copy(
                src_ref=slot(a, block) if src is None else src, dst_ref=slot(a, block),
                send_sem=send_sems.at[a, k], recv_sem=recv_sems.at[a, k], device_id=to, device_id_type=MESH)

        mine = [pltpu.make_async_copy(x_refs[a], slot(a, me), local_sems.at[a]) for a in range(n)]
        first = [copy(a, 0, me, sibling, src=x_refs[a]) for a in range(n)]
        first += [copy(a, 1 + j, me, (*chip, c), src=x_refs[a]) for a in range(n) for j, chip in enumerate(chips)]
        for cp in mine + first:
            cp.start()
        passed = []
        for a in range(n):
            for j, chip in enumerate(chips):
                copy(a, 1 + j, (*chip, c), me).wait_recv()
                passed.append(copy(a, 4 + j, (*chip, c), sibling))
                passed[-1].start()
        for a in range(n):
            copy(a, 0, sibling, me).wait_recv()
            for j, chip in enumerate(chips):
                copy(a, 4 + j, (*chip, 1 - c), me).wait_recv()
        for cp in first + passed:
            cp.wait_send()
        for cp in mine:
            cp.wait()

    return pl.pallas_call(
        body, name=name, in_specs=[HBM_SPEC] * n, out_specs=[HBM_SPEC] * n, out_shape=out_shapes,
        scratch_shapes=[pltpu.SemaphoreType.DMA((n, N_DEV - 1)), pltpu.SemaphoreType.DMA((n, N_DEV - 1)),
                        pltpu.SemaphoreType.DMA((n,))],
        compiler_params=pltpu.CompilerParams(has_side_effects=True),
    )(*xs)


def _exchange(xs, *, name, scatter):
    ex = _Exchange(xs, scatter)

    def body(*refs):
        _, _, _, xrefs = ex.split(refs, 0, 0)
        ex.start(*xrefs)
        ex.wait(*xrefs)

    return pl.pallas_call(
        body, name=name, in_specs=ex.in_specs, out_specs=ex.out_specs, out_shape=ex.out_shapes,
        scratch_shapes=ex.scratch, compiler_params=pltpu.CompilerParams(has_side_effects=True),
    )(*xs)


def _ada_fwd(c_all, w_shard, b_shard):
    n = w_shard.shape[1]

    def body(c_ref, w_ref, b_ref, o_ref):
        cv = c_ref[...]
        act = (cv * _sigmoid(cv)).astype(BF16)
        o_ref[...] = jnp.dot(act, w_ref[...].astype(BF16), preferred_element_type=F32) + b_ref[...]

    return pl.pallas_call(
        body, name="ada_fwd", in_specs=[_full((N_DEV, D)), _full((D, n)), _full((1, n))],
        out_specs=_full((N_DEV, n)), out_shape=jax.ShapeDtypeStruct((N_DEV, n), F32), grid=(1,),
        compiler_params=_params(("arbitrary",)),
    )(c_all, w_shard, b_shard)


def _ada_bwd(c_all_t, dmod_pad):
    n = dmod_pad.shape[1]

    def body(c_ref, d_ref, o_ref):
        cv = c_ref[...]
        act = (cv * _sigmoid(cv)).astype(BF16)
        o_ref[...] = jnp.dot(act, d_ref[...].astype(BF16), preferred_element_type=F32)

    return pl.pallas_call(
        body, name="ada_bwd", in_specs=[_full((D, LANES)), _full((LANES, n))],
        out_specs=_full((D, n)), out_shape=jax.ShapeDtypeStruct((D, n), F32), grid=(1,),
        compiler_params=_params(("arbitrary",)),
    )(c_all_t, dmod_pad)


ADAM_ROWS = 256


def _adamw(parts, w, m, v, *, name):
    n, R, C = parts.shape
    tr = next((t for t in (ADAM_ROWS, 128, 64, 32, 16, SUBLANES) if R % t == 0), R)

    def body(p_ref, w_ref, m_ref, v_ref, g_ref, d_ref, nm_ref, nv_ref):
        g = p_ref[0].astype(F32)
        for j in range(1, n):
            g = g + p_ref[j].astype(F32)
        g_ref[...] = g
        nm = ADAM_B1 * m_ref[...] + (1.0 - ADAM_B1) * g
        nv = ADAM_B2 * v_ref[...] + (1.0 - ADAM_B2) * (g * g)
        nm_ref[...] = nm
        nv_ref[...] = nv
        m_hat = nm / (1.0 - ADAM_B1 ** ADAM_STEP)
        v_hat = nv / (1.0 - ADAM_B2 ** ADAM_STEP)
        d_ref[...] = -ADAM_LR * (m_hat / (jnp.sqrt(v_hat) + ADAM_EPS) + ADAM_WD * w_ref[...])

    row = pl.BlockSpec((tr, C), lambda i: (i, 0))
    return pl.pallas_call(
        body, name=name, grid=(R // tr,),
        in_specs=[pl.BlockSpec((n, tr, C), lambda i: (0, i, 0)), row, row, row], out_specs=[row] * 4,
        out_shape=[jax.ShapeDtypeStruct((R, C), F32)] * 4,
        compiler_params=_params(("parallel",)),
    )(parts, w, m, v)


SHARDED = (("w_in", D, IN_W, 1), ("w_branch_a", CONV_W, D, 1), ("w_branch_b", ATTN_W, D, 1), ("w_out", D, D, 0),
           ("w_up", D, 2 * D_FF, 1), ("w_down", D_FF, D, 0), ("conv_a_w", 3, CONV_W, 1),
           ("conv_ffn_w", 3, 2 * D_FF, 1))
MATRICES = SHARDED[:6]
LATE = MATRICES[1:]
CONVS = SHARDED[6:]
REPLICATED = (("b_ada", N_MOD * D), ("norm1_g", D), ("norm2_g", D), ("b_f", N_HEADS), ("q_norm_g", HEAD_DIM),
              ("k_norm_g", HEAD_DIM))


def _shard_shape(rows, cols, axis):
    return (rows // N_DEV, cols) if axis == 0 else (rows, cols // N_DEV)


def _pack_rows(flat, multiple):
    length = flat.shape[-1]
    rows = -(-length // PACK_W)
    rows = -(-rows // multiple) * multiple
    pad = [(0, 0)] * (flat.ndim - 1) + [(0, rows * PACK_W - length)]
    return jnp.pad(flat, pad).reshape(flat.shape[:-1] + (rows, PACK_W))


def _pack_shards(shards, spec, multiple, dtype):
    flat = jnp.concatenate([shards[name].reshape(-1).astype(dtype) for name, *_ in spec])
    return _pack_rows(flat, multiple)


def _join_lane_blocks(gathered):
    n, r, c = gathered.shape

    def body(g_ref, o_ref):
        for j in range(n):
            o_ref[:, j * c:(j + 1) * c] = g_ref[j]

    return pl.pallas_call(
        body, name="join_lane_blocks", grid=(1,), in_specs=[_full((n, r, c))], out_specs=_full((r, n * c)),
        out_shape=jax.ShapeDtypeStruct((r, n * c), gathered.dtype), compiler_params=_params(("arbitrary",)),
    )(gathered)


SHARD_PAD = 768


def _assemble_columns(gathered, shard_cols, out_cols, segments, *, name):
    n, rows, padw = gathered.shape
    assert n == N_DEV and padw == SHARD_PAD and shard_cols <= SHARD_PAD

    def body(g_ref, o_ref):
        j = pl.program_id(0)

        @pl.when(j == 0)
        def _():
            o_ref[...] = jnp.zeros_like(o_ref)

        for dev in range(N_DEV):
            @pl.when(j == dev)
            def _(dev=dev):
                x = g_ref[0]
                for lo, hi, delta in segments:
                    a, b = max(lo, dev * shard_cols), min(hi, (dev + 1) * shard_cols)
                    if a >= b:
                        continue
                    base = (a + delta) // LANES * LANES
                    width = -(-(b + delta - base) // LANES) * LANES
                    src = lax.broadcasted_iota(jnp.int32, (padw, width), 0) + dev * shard_cols
                    dst = lax.broadcasted_iota(jnp.int32, (padw, width), 1) + (base - delta)
                    place = jnp.where((src == dst) & (src >= a) & (src < b), 1.0, 0.0).astype(BF16)
                    moved = jnp.dot(x, place, preferred_element_type=F32).astype(BF16)
                    o_ref[:, base:base + width] = o_ref[:, base:base + width] + moved

    return pl.pallas_call(
        body, name=name, grid=(N_DEV,), in_specs=[pl.BlockSpec((1, rows, padw), lambda j: (j, 0, 0))],
        out_specs=_full((rows, out_cols)), out_shape=jax.ShapeDtypeStruct((rows, out_cols), BF16),
        compiler_params=_params(("arbitrary",)),
    )(gathered)


def _pad_shard(w):
    return jnp.pad(w.astype(BF16), ((0, 0), (0, SHARD_PAD - w.shape[1])))


W_IN_SEGMENTS = ((0, COL_GA, 0), (COL_GA, COL_GA + N_HEADS, COL_F - COL_GA), (COL_GA + N_HEADS, IN_W, -N_HEADS))


def _join_shards(gathered, axis):
    if axis == 0:
        return gathered.reshape(N_DEV * gathered.shape[1], gathered.shape[2])
    if gathered.shape[2] == LANES:
        return _join_lane_blocks(gathered)
    return jnp.concatenate([gathered[j] for j in range(N_DEV)], axis=1)


def _column_shards(pieces):
    total = sum(p.shape[1] for p in pieces)
    width = total // N_DEV
    shards = []
    for j in range(N_DEV):
        lo, hi, off, segs = j * width, (j + 1) * width, 0, []
        for p in pieces:
            a, b = max(lo, off), min(hi, off + p.shape[1])
            if a < b:
                segs.append(p[:, a - off:b - off])
            off += p.shape[1]
        shards.append(segs[0] if len(segs) == 1 else jnp.concatenate(segs, axis=1))
    return jnp.stack(shards)


def _split_shards(full, axis):
    if axis == 0:
        return full.reshape(N_DEV, full.shape[0] // N_DEV, full.shape[1])
    c = full.shape[1] // N_DEV
    return jnp.stack([full[:, j * c:(j + 1) * c] for j in range(N_DEV)])


def _unpack_shards(packed, spec):
    flat = packed.reshape(-1)
    out, off = {}, 0
    for name, rows, cols, axis in spec:
        r, c = _shard_shape(rows, cols, axis)
        out[name] = flat[off:off + r * c].reshape(r, c)
        off += r * c
    return out


def _unpack_gathered(gathered, spec):
    flat = gathered.reshape(N_DEV, -1)
    out, off = {}, 0
    for name, rows, cols, axis in spec:
        r, c = _shard_shape(rows, cols, axis)
        seg = flat[:, off:off + r * c].reshape(N_DEV, r, c)
        out[name] = seg.reshape(rows, cols) if axis == 0 else seg.transpose(1, 0, 2).reshape(rows, cols)
        off += r * c
    return out


def _pack_full_by_dest(full, spec, multiple):
    segs = []
    for name, rows, cols, axis in spec:
        r, c = _shard_shape(rows, cols, axis)
        a = full[name]
        seg = a.reshape(N_DEV, r, c) if axis == 0 else a.reshape(rows, N_DEV, c).transpose(1, 0, 2)
        segs.append(seg.reshape(N_DEV, r * c))
    return _pack_rows(jnp.concatenate(segs, axis=1), multiple)


def _pad_head_rows(w):
    n = w.shape[1]
    padded = jnp.pad(w.reshape(N_HEADS, HEAD_DIM, n), ((0, 0), (0, LANES - HEAD_DIM), (0, 0)))
    return padded.reshape(N_HEADS * LANES, n)


def kernel(x, c, w_ada, b_ada, norm1_g, w_in, b_f, conv_a_w, q_norm_g, k_norm_g, w_branch_a, w_branch_b, w_out, norm2_g, w_up, conv_ffn_w, w_down, loss_target, m_w_ada, m_b_ada, m_norm1_g, m_w_in, m_b_f, m_conv_a_w, m_q_norm_g, m_k_norm_g, m_w_branch_a, m_w_branch_b, m_w_out, m_norm2_g, m_w_up, m_conv_ffn_w, m_w_down, v_w_ada, v_b_ada, v_norm1_g, v_w_in, v_b_f, v_conv_a_w, v_q_norm_g, v_k_norm_g, v_w_branch_a, v_w_branch_b, v_w_out, v_norm2_g, v_w_up, v_conv_ffn_w, v_w_down):
    names = ("w_ada", "b_ada", "norm1_g", "w_in", "b_f", "conv_a_w", "q_norm_g", "k_norm_g", "w_branch_a",
             "w_branch_b", "w_out", "norm2_g", "w_up", "conv_ffn_w", "w_down")
    squeeze = lambda a: a[0] if a.ndim == 3 else a
    W = dict(zip(names, map(squeeze, (w_ada, b_ada, norm1_g, w_in, b_f, conv_a_w, q_norm_g, k_norm_g, w_branch_a,
                                      w_branch_b, w_out, norm2_g, w_up, conv_ffn_w, w_down))))
    M = dict(zip(names, map(squeeze, (m_w_ada, m_b_ada, m_norm1_g, m_w_in, m_b_f, m_conv_a_w, m_q_norm_g,
                                      m_k_norm_g, m_w_branch_a, m_w_branch_b, m_w_out, m_norm2_g, m_w_up,
                                      m_conv_ffn_w, m_w_down))))
    V = dict(zip(names, map(squeeze, (v_w_ada, v_b_ada, v_norm1_g, v_w_in, v_b_f, v_conv_a_w, v_q_norm_g,
                                      v_k_norm_g, v_w_branch_a, v_w_branch_b, v_w_out, v_norm2_g, v_w_up,
                                      v_conv_ffn_w, v_w_down))))
    me = 4 * lax.axis_index("x") + 2 * lax.axis_index("y") + lax.axis_index("c")
    ada_n = N_MOD * D // N_DEV

    small = jnp.concatenate([c.reshape(-1), W["conv_a_w"].reshape(-1), W["conv_ffn_w"].reshape(-1)])
    small_all, w_in_all = _gather_two_level([_pack_rows(small, SUBLANES), _pad_shard(W["w_in"])],
                                            name="gather_first")
    small_all = small_all.reshape(N_DEV, -1)
    c_all = small_all[:, :D]
    conv_all = _unpack_gathered(small_all[:, D:], CONVS)

    b_shard = lax.dynamic_slice(W["b_ada"], (0, me * ada_n), (1, ada_n))
    mod_part = _ada_fwd(c_all, W["w_ada"], b_shard)
    mod_all, = _exchange([mod_part], name="gather_mod", scatter=False)
    mod = lax.dynamic_index_in_dim(mod_all, me, axis=1, keepdims=False).reshape(1, N_MOD * D)

    wts = {"w_in": _assemble_columns(w_in_all, IN_W // N_DEV, IN_W_PAD, W_IN_SEGMENTS, name="assemble_w_in")}
    wts.update(conv_all)
    for name in ("norm1_g", "norm2_g", "q_norm_g", "k_norm_g", "b_f"):
        wts[name] = W[name]
    late = {name: _pad_shard(W[name]) if name == "w_up" else W[name].astype(BF16) for name, *_ in LATE}

    sq, grad_x, grads, parts = _local_step(x[0], loss_target[0], mod, wts, late)
    loss = lax.psum(sq[0, 0] * (0.5 / D), AXES)

    grads["b_ada"] = grads["mod"]
    rep_flat = lambda src: jnp.concatenate([src[name].reshape(-1) for name, _ in REPLICATED])
    rep_parts, = _exchange([_pack_rows(rep_flat(grads), 16)], name="gather_small_grads", scatter=False)
    rep_out = _adamw(rep_parts, *[_pack_rows(rep_flat(s), 16) for s in (W, M, V)], name="adamw_replicated")

    dmod_all = rep_parts.reshape(N_DEV, -1)[:, :N_MOD * D]
    dmod_mine = lax.dynamic_slice(dmod_all, (0, me * ada_n), (N_DEV, ada_n))
    g_ada = _ada_bwd(jnp.pad(c_all.T, ((0, 0), (0, LANES - N_DEV))),
                     jnp.pad(dmod_mine, ((0, LANES - N_DEV), (0, 0))))
    ada_out = _adamw(g_ada[None], W["w_ada"], M["w_ada"], V["w_ada"], name="adamw_ada")

    mat_out = {name: _adamw(parts[name], W[name], M[name], V[name], name="adamw_" + name) for name, *_ in MATRICES}
    conv_out = _adamw(parts["conv"], *[_pack_shards(s, CONVS, SUBLANES, F32) for s in (W, M, V)],
                      name="adamw_conv")

    results = []
    for kind in range(4):
        per = {"w_ada": ada_out[kind]}
        per.update({name: out[kind] for name, out in mat_out.items()})
        per.update(_unpack_shards(conv_out[kind], CONVS))
        flat, off = rep_out[kind].reshape(-1), 0
        for name, n in REPLICATED:
            per[name] = flat[off:off + n].reshape(1, n)
            off += n
        results.append(per)
    restore = lambda name, a: a[None] if W[name].ndim == 2 and name not in dict(REPLICATED) else a
    outs = [loss, grad_x[None]]
    for per in results:
        outs.extend(restore(name, per[name]) for name in names)
    return tuple(outs)
```

```python
import jax
import jax.numpy as jnp
import numpy as np
from jax import lax
from jax.experimental import pallas as pl
from jax.experimental.pallas import tpu as pltpu

F32 = jnp.float32
BF16 = jnp.bfloat16

N_DEV = 8
D = 1024
N_HEADS = 8
HEAD_DIM = 64
ATTN_W = 512
CONV_W = 512
D_FF = 2816
N_MOD = 6
IN_W = 5128
RMS_EPS = 1e-6
NEG_INF = -1e30

IN_W_PAD = 5376
COL_GA = 3072
COL_GB = 4096
COL_F = 5120
F_PAD = 128

ADAM_LR = 0.001
ADAM_B1 = 0.9
ADAM_B2 = 0.999
ADAM_EPS = 1e-08
ADAM_WD = 0.01
ADAM_STEP = 10

LANES = 128
SUBLANES = 8
BF16_ROWS = 16
VMEM_LIMIT = 52 * 1024 * 1024
TOKEN_TILE = 512
MATMUL_TILE = 1024
ATTN_BLOCK = 512
PACK_W = 1024

MESH = pl.DeviceIdType.MESH
AXES = ("x", "y", "c")


def _params(sem=None, **kw):
    return pltpu.CompilerParams(dimension_semantics=sem, vmem_limit_bytes=VMEM_LIMIT, **kw)


def _full(shape):
    nd = len(shape)
    return pl.BlockSpec(shape, lambda *_: (0,) * nd)


def _tn_dot(a, b):
    return lax.dot_general(a, b, (((0,), (0,)), ((), ())), preferred_element_type=F32)


def _matmul(a, b, *, name, tm, tn, tk, out_dtype=F32, trans_a=False, trans_b=False, exchange=None):
    assert not (trans_a and trans_b)
    if trans_a:
        K, M = a.shape
    else:
        M, K = a.shape
    N = b.shape[0] if trans_b else b.shape[1]
    assert b.shape[1 if trans_b else 0] == K and M % tm == 0 and N % tn == 0 and K % tk == 0, (name, a.shape, b.shape)
    nm, nn, nk = M // tm, N // tn, K // tk

    def body(*refs):
        if exchange is None:
            a_ref, b_ref, o_ref, *own = refs
        else:
            (a_ref, b_ref), (o_ref,), own, xrefs = exchange.split(refs, 2, 1)
            ids = [pl.program_id(d) for d in range(3)]
            first = jnp.logical_and(jnp.logical_and(ids[0] == 0, ids[1] == 0), ids[2] == 0)
            last = jnp.logical_and(jnp.logical_and(ids[0] == nn - 1, ids[1] == nm - 1), ids[2] == nk - 1)
            _ride(exchange, first, last, xrefs)
        k = pl.program_id(2)
        av = a_ref[...].astype(BF16)
        bv = b_ref[...].astype(BF16)
        if trans_a:
            prod = _tn_dot(av, bv)
        elif trans_b:
            prod = _nt_dot(av, bv)
        else:
            prod = jnp.dot(av, bv, preferred_element_type=F32)
        if nk == 1:
            o_ref[...] = prod.astype(out_dtype)
            return
        acc_ref, = own

        @pl.when(k == 0)
        def _():
            acc_ref[...] = prod

        @pl.when(k > 0)
        def _():
            acc_ref[...] += prod

        @pl.when(k == nk - 1)
        def _():
            o_ref[...] = acc_ref[...].astype(out_dtype)

    if trans_a:
        a_spec = pl.BlockSpec((tk, tm), lambda j, i, k: (k, i))
    else:
        a_spec = pl.BlockSpec((tm, tk), lambda j, i, k: (i, k))
    b_spec = pl.BlockSpec((tn, tk), lambda j, i, k: (j, k)) if trans_b else pl.BlockSpec((tk, tn), lambda j, i, k: (k, j))
    in_specs = [a_spec, b_spec]
    out_spec = pl.BlockSpec((tm, tn), lambda j, i, k: (i, j))
    out_shape = jax.ShapeDtypeStruct((M, N), out_dtype)
    scratch = [pltpu.VMEM((tm, tn), F32)] if nk > 1 else []
    if exchange is None:
        return pl.pallas_call(
            body, name=name, grid=(nn, nm, nk), in_specs=in_specs, out_specs=out_spec, out_shape=out_shape,
            scratch_shapes=scratch, compiler_params=_params(("parallel", "parallel", "arbitrary")),
        )(a, b)
    return pl.pallas_call(
        body, name=name, grid=(nn, nm, nk), in_specs=in_specs + exchange.in_specs,
        out_specs=[out_spec] + exchange.out_specs, out_shape=[out_shape] + exchange.out_shapes,
        scratch_shapes=scratch + exchange.scratch, compiler_params=_params(("arbitrary",) * 3),
    )(a, b, *exchange.xs)


def _norm_bwd_tile(dh, ins, outs, first):
    x_ref, dr_ref, g_ref, sc_ref = ins[:4]
    dx_ref, dsh_ref, dsc_ref, dg_ref = outs[:4]

    @pl.when(first)
    def _():
        for ref in outs[1:4] + outs[5:]:
            ref[...] = jnp.zeros_like(ref)

    xv = x_ref[...]
    gv = g_ref[...]
    one_sc = 1.0 + sc_ref[...]
    inv = lax.rsqrt(jnp.mean(xv * xv, axis=-1, keepdims=True) + RMS_EPS)
    xn = xv * inv
    dxn = dh * (gv * one_sc)
    dx = dr_ref[...] + inv * (dxn - xn * jnp.mean(dxn * xn, axis=-1, keepdims=True))
    dx_ref[...] = dx
    dhxn = dh * xn
    dsh_ref[...] += _rows8(dh)
    dsc_ref[...] += _rows8(dhxn * gv)
    dg_ref[...] += _rows8(dhxn * one_sc)
    if len(ins) == 6:
        mix_ref, g1_ref = ins[4:]
        dmix_ref, dg1_ref = outs[4:]
        dmix_ref[...] = (dx * g1_ref[...]).astype(BF16)
        dg1_ref[...] += _rows8(dx * mix_ref[...])


def _matmul_pieces(pieces, b, *, name, tm, exchange=None, norm_bwd=None):
    M = pieces[0].shape[0]
    widths = [p.shape[1] for p in pieces]
    offsets = [sum(widths[:i]) for i in range(len(widths))]
    N = b.shape[0]
    assert b.shape[1] >= sum(widths) and M % tm == 0, (name, widths, b.shape)
    n_p, nm = len(pieces), M // tm
    extra = list(norm_bwd) if norm_bwd is not None else []
    n_in = n_p + 1 + len(extra)
    n_out = len(extra) if norm_bwd is not None else 1

    def body(*refs):
        i = pl.program_id(0)
        if exchange is None:
            ins, outs = refs[:n_in], refs[n_in:]
        else:
            ins, outs, _, xrefs = exchange.split(refs, n_in, n_out)
            _ride(exchange, i == 0, i == nm - 1, xrefs)
        b_ref = ins[n_p]
        acc = None
        for a_ref, off, w in zip(ins[:n_p], offsets, widths):
            term = _nt_dot(a_ref[...].astype(BF16), b_ref[:, off:off + w])
            acc = term if acc is None else acc + term
        if norm_bwd is None:
            outs[0][...] = acc
        else:
            _norm_bwd_tile(acc, tuple(ins[n_p + 1:]), tuple(outs), first=i == 0)

    row, vec, part = pl.BlockSpec((tm, N), lambda i: (i, 0)), _full((1, N)), _full((SUBLANES, N))
    part_shape = jax.ShapeDtypeStruct((SUBLANES, N), F32)
    in_specs = [pl.BlockSpec((tm, w), lambda i: (i, 0)) for w in widths] + [_full(b.shape)]
    out_specs, out_shape = [row], [jax.ShapeDtypeStruct((M, N), F32)]
    if norm_bwd is not None:
        in_specs += [row, row, vec, vec] + ([row, vec] if len(extra) == 6 else [])
        out_specs += [part] * 3 + ([row, part] if len(extra) == 6 else [])
        out_shape += [part_shape] * 3 + ([jax.ShapeDtypeStruct((M, N), BF16), part_shape] if len(extra) == 6 else [])
    sequential = exchange is not None or norm_bwd is not None
    xs = exchange.xs if exchange is not None else []
    result = pl.pallas_call(
        body, name=name, grid=(nm,), in_specs=in_specs + (exchange.in_specs if exchange else []),
        out_specs=out_specs + (exchange.out_specs if exchange else []),
        out_shape=out_shape + (exchange.out_shapes if exchange else []),
        scratch_shapes=exchange.scratch if exchange else [],
        compiler_params=_params(("arbitrary" if sequential else "parallel",)),
    )(*pieces, b, *extra, *xs)
    return result[0] if len(result) == 1 else result


def _matmul_tn_pieces(a, pieces, *, name, tk):
    K, M = a.shape
    widths = [p.shape[1] for p in pieces]
    n_p, nk = len(pieces), K // tk

    def body(*refs):
        a_ref, p_refs, o_refs = refs[0], refs[1:n_p + 1], refs[n_p + 1:]
        k = pl.program_id(0)
        av = a_ref[...].astype(BF16)
        for p_ref, o_ref in zip(p_refs, o_refs):
            prod = _tn_dot(av, p_ref[...].astype(BF16))

            @pl.when(k == 0)
            def _():
                o_ref[...] = prod

            @pl.when(k > 0)
            def _():
                o_ref[...] += prod

    return pl.pallas_call(
        body, name=name, grid=(nk,),
        in_specs=[pl.BlockSpec((tk, M), lambda k: (k, 0))] + [pl.BlockSpec((tk, w), lambda k: (k, 0)) for w in widths],
        out_specs=[_full((M, w)) for w in widths], out_shape=[jax.ShapeDtypeStruct((M, w), F32) for w in widths],
        compiler_params=_params(("arbitrary",)),
    )(a, *pieces)


def _split_dot(x, mat, parts):
    out = None
    rem = x
    for p in range(parts):
        piece = rem.astype(BF16)
        term = jnp.dot(piece, mat, preferred_element_type=F32)
        out = term if out is None else out + term
        if p + 1 < parts:
            rem = rem - piece.astype(F32)
    return out


def _sigmoid(x):
    return 0.5 * jnp.tanh(0.5 * x) + 0.5


def _rows8(x):
    r, c = x.shape
    return jnp.sum(x.reshape(r // SUBLANES, SUBLANES, c), axis=0)


def _shift_down(blk, prev8, n):
    rolled = pltpu.roll(blk, n, axis=0)
    prev_rolled = pltpu.roll(prev8, n, axis=0)
    rows = lax.broadcasted_iota(jnp.int32, prev8.shape, 0)
    first = jnp.where(rows < n, prev_rolled, rolled[0:SUBLANES])
    return jnp.concatenate([first, rolled[SUBLANES:]], axis=0)


def _prev_spec(tm, width, col):
    per = tm // SUBLANES
    return pl.BlockSpec((SUBLANES, width), lambda i, *_: (jnp.maximum(i * per - 1, 0), col))


def _next_spec(tm, width, col, n_tiles):
    per = tm // SUBLANES
    last = n_tiles * per - 1
    return pl.BlockSpec((SUBLANES, width), lambda i, *_: (jnp.minimum((i + 1) * per, last), col))


def _group_matrix():
    idx = np.arange(ATTN_W) // HEAD_DIM
    return jnp.asarray((idx[:, None] == idx[None, :]).astype(np.float32), BF16)


def _norm_mod(x, g, sc, sh, *, name):
    T = x.shape[0]
    tm = min(TOKEN_TILE, T)

    def body(x_ref, g_ref, sc_ref, sh_ref, o_ref):
        xv = x_ref[...]
        inv = lax.rsqrt(jnp.mean(xv * xv, axis=-1, keepdims=True) + RMS_EPS)
        o_ref[...] = ((xv * inv) * g_ref[...] * (1.0 + sc_ref[...]) + sh_ref[...]).astype(BF16)

    row = pl.BlockSpec((tm, D), lambda i: (i, 0))
    return pl.pallas_call(
        body, name=name, grid=(T // tm,),
        in_specs=[row, _full((1, D)), _full((1, D)), _full((1, D))],
        out_specs=row, out_shape=jax.ShapeDtypeStruct((T, D), BF16),
        compiler_params=_params(("parallel",)),
    )(x, g, sc, sh)


LANE_ONE = 64
LANE_F = 67
LANE_LSE = 70
LANE_SUM = 73


def _pieces(x):
    hi = x.astype(BF16).astype(F32)
    rest = x - hi
    mid = rest.astype(BF16).astype(F32)
    return hi, mid, rest - mid


def _run(start, vals):
    return [(start + i, v) for i, v in enumerate(vals)]


def _head_lanes(a, h):
    blk = a[:, LANES * (h // 2):LANES * (h // 2) + LANES]
    return blk if h % 2 == 0 else pltpu.roll(blk, HEAD_DIM, axis=1)


def _branch_prep(proj, fcum, conv_w8, qg, kg, gmat):
    T = proj.shape[0]
    tm = min(TOKEN_TILE, T)
    nt = T // tm

    def body(cb_ref, cc_ref, cv_ref, q_ref, k_ref, v_ref, f_ref, ccp_ref, cvp_ref, w_ref, qg_ref, kg_ref, g_ref,
             ya_ref, qa_ref, ka_ref, va_ref):
        i = pl.program_id(0)
        z = cc_ref[...] * cv_ref[...]
        zp = jnp.where(i > 0, ccp_ref[...] * cvp_ref[...], 0.0)
        w = w_ref[...]
        cz = _shift_down(z, zp, 2) * w[0:1] + _shift_down(z, zp, 1) * w[1:2] + z * w[2:3]
        ya_ref[...] = (cb_ref[...] * cz).astype(BF16)
        gm = g_ref[...]

        def normed(src, gain, scale):
            v = src[...]
            ms = _split_dot(v * v, gm, 2) * (1.0 / HEAD_DIM)
            return (v * lax.rsqrt(ms + RMS_EPS)) * gain[...] * scale

        qn = normed(q_ref, qg_ref, 1.0 / np.sqrt(HEAD_DIM))
        kn = normed(k_ref, kg_ref, 1.0)
        vv = v_ref[...]
        lane = lax.broadcasted_iota(jnp.int32, (tm, LANES), 1)
        low = lane < HEAD_DIM
        in_run = lambda start: jnp.logical_and(lane >= start, lane < start + 3)
        q_ones = jnp.where(jnp.logical_or(in_run(LANE_ONE), lane == LANE_SUM), 1.0, 0.0)
        k_ones = jnp.where(jnp.logical_or(in_run(LANE_F), in_run(LANE_LSE)), 1.0, 0.0)
        v_ones = jnp.where(in_run(LANE_ONE), 1.0, 0.0)
        f3 = jnp.concatenate(_pieces(f_ref[...]), axis=1).astype(BF16)
        src = lax.broadcasted_iota(jnp.int32, (3 * LANES, LANES), 0)
        dst = lax.broadcasted_iota(jnp.int32, (3 * LANES, LANES), 1)
        for h in range(N_HEADS):
            def pick(start, h=h):
                hit = jnp.logical_and(src - h == (dst - start) * LANES, jnp.logical_and(dst >= start, dst < start + 3))
                return jnp.dot(f3, jnp.where(hit, 1.0, 0.0).astype(BF16), preferred_element_type=F32)

            qa_ref[h] = jnp.where(low, _head_lanes(qn, h), q_ones + pick(LANE_F)).astype(BF16)
            ka_ref[h] = jnp.where(low, _head_lanes(kn, h), k_ones - pick(LANE_ONE)).astype(BF16)
            va_ref[h] = jnp.where(low, _head_lanes(vv, h), v_ones).astype(BF16)

    blk = lambda col: pl.BlockSpec((tm, CONV_W), lambda i: (i, col))
    heads = pl.BlockSpec((N_HEADS, tm, LANES), lambda i: (0, i, 0))
    return pl.pallas_call(
        body, name="branch_prep", grid=(nt,),
        in_specs=[blk(0), blk(1), blk(2), blk(3), blk(4), blk(5), pl.BlockSpec((tm, F_PAD), lambda i: (i, 0)),
                  _prev_spec(tm, CONV_W, 1), _prev_spec(tm, CONV_W, 2),
                  _full((SUBLANES, CONV_W)), _full((1, ATTN_W)), _full((1, ATTN_W)), _full((ATTN_W, ATTN_W))],
        out_specs=[pl.BlockSpec((tm, CONV_W), lambda i: (i, 0)), heads, heads, heads],
        out_shape=[jax.ShapeDtypeStruct((T, CONV_W), BF16)] + [jax.ShapeDtypeStruct((N_HEADS, T, LANES), BF16)] * 3,
        compiler_params=_params(("parallel",)),
    )(proj, proj, proj, proj, proj, proj, fcum, proj, proj, conv_w8, qg, kg, gmat)


def _cumsum(x, *, reverse, name, col=0, gate_bias=None):
    T = x.shape[0]
    tm = min(TOKEN_TILE, T)
    nt = T // tm

    def body(x_ref, b_ref, o_ref, carry_ref):
        i = pl.program_id(0)

        @pl.when(i == 0)
        def _():
            carry_ref[...] = jnp.zeros_like(carry_ref)

        r = lax.broadcasted_iota(jnp.int32, (tm, tm), 0)
        c = lax.broadcasted_iota(jnp.int32, (tm, tm), 1)
        tri = jnp.where((c >= r) if reverse else (c <= r), 1.0, 0.0).astype(BF16)
        xv = x_ref[...]
        if gate_bias is not None:
            fx = xv + b_ref[...]
            xv = jnp.minimum(fx, 0.0) - jnp.log(1.0 + jnp.exp(-jnp.abs(fx)))
        out = _split_dot_left(tri, xv, 3) + carry_ref[0:1]
        o_ref[...] = out
        carry_ref[...] = jnp.broadcast_to(out[0:1] if reverse else out[tm - 1:tm], carry_ref.shape)

    rows = (lambda i: nt - 1 - i) if reverse else (lambda i: i)
    bias = jnp.zeros((1, F_PAD), F32) if gate_bias is None else gate_bias
    return pl.pallas_call(
        body, name=name, grid=(nt,),
        in_specs=[pl.BlockSpec((tm, F_PAD), lambda i: (rows(i), col)), _full((1, F_PAD))],
        out_specs=pl.BlockSpec((tm, F_PAD), lambda i: (rows(i), 0)),
        out_shape=jax.ShapeDtypeStruct((T, F_PAD), F32),
        scratch_shapes=[pltpu.VMEM((SUBLANES, F_PAD), F32)],
        compiler_params=_params(("arbitrary",)),
    )(x, bias)


def _split_dot_left(mat, x, parts):
    out = None
    rem = x
    for p in range(parts):
        piece = rem.astype(BF16)
        term = jnp.dot(mat, piece, preferred_element_type=F32)
        out = term if out is None else out + term
        if p + 1 < parts:
            rem = rem - piece.astype(F32)
    return out


def _out_resid_norm(x, merged, w_out, g1, g, sc, sh):
    T = x.shape[0]
    tm = min(TOKEN_TILE, T)

    def body(x_ref, m_ref, w_ref, g1_ref, g_ref, sc_ref, sh_ref, mix_ref, x1_ref, h_ref):
        mix = jnp.dot(m_ref[...], w_ref[...], preferred_element_type=F32)
        mix_ref[...] = mix
        x1 = x_ref[...] + g1_ref[...] * mix
        x1_ref[...] = x1
        inv = lax.rsqrt(jnp.mean(x1 * x1, axis=-1, keepdims=True) + RMS_EPS)
        h_ref[...] = ((x1 * inv) * g_ref[...] * (1.0 + sc_ref[...]) + sh_ref[...]).astype(BF16)

    row = pl.BlockSpec((tm, D), lambda i: (i, 0))
    vec = _full((1, D))
    return pl.pallas_call(
        body, name="out_resid_norm", grid=(T // tm,),
        in_specs=[row, row, _full((D, D)), vec, vec, vec, vec], out_specs=[row, row, row],
        out_shape=[jax.ShapeDtypeStruct((T, D), F32), jax.ShapeDtypeStruct((T, D), F32),
                   jax.ShapeDtypeStruct((T, D), BF16)],
        compiler_params=_params(("parallel",)),
    )(x, merged, w_out, g1, g, sc, sh)


FFN_TM = 256
FFN_TC = 1408


def _ffn_act_fwd(u, w8):
    T = u.shape[0]
    tm = min(FFN_TM, T)
    nt = T // tm
    nc = D_FF // FFN_TC

    def body(ug_ref, uv_ref, ugp_ref, uvp_ref, wg_ref, wv_ref, o_ref, cg_ref, cv_ref):
        i = pl.program_id(1)

        def conv(u_ref, p_ref, w_ref):
            uv = u_ref[...]
            up = jnp.where(i > 0, p_ref[...], 0.0)
            w = w_ref[...]
            return _shift_down(uv, up, 2) * w[0:1] + _shift_down(uv, up, 1) * w[1:2] + uv * w[2:3]

        gate = conv(ug_ref, ugp_ref, wg_ref)
        val = conv(uv_ref, uvp_ref, wv_ref)
        cg_ref[...] = gate.astype(BF16)
        cv_ref[...] = val.astype(BF16)
        o_ref[...] = (gate * _sigmoid(gate) * val).astype(BF16)

    per = tm // SUBLANES
    blk = lambda off: pl.BlockSpec((tm, FFN_TC), lambda j, i: (i, j + off))
    prev = lambda off: pl.BlockSpec((SUBLANES, FFN_TC), lambda j, i: (jnp.maximum(i * per - 1, 0), j + off))
    wblk = lambda off: pl.BlockSpec((SUBLANES, FFN_TC), lambda j, i: (0, j + off))
    return pl.pallas_call(
        body, name="ffn_act_fwd", grid=(nc, nt),
        in_specs=[blk(0), blk(nc), prev(0), prev(nc), wblk(0), wblk(nc)],
        out_specs=[blk(0), blk(0), blk(0)],
        out_shape=[jax.ShapeDtypeStruct((T, D_FF), BF16)] * 3,
        compiler_params=_params(("parallel", "parallel")),
    )(u, u, u, u, w8, w8)


def _down_loss_head(x1, act, w_down, g2, target):
    T = x1.shape[0]
    tm = min(TOKEN_TILE, T)

    def body(x1_ref, a_ref, w_ref, g2_ref, t_ref, dy_ref, dff_ref, loss_ref, dg2_ref):
        i = pl.program_id(0)

        @pl.when(i == 0)
        def _():
            loss_ref[...] = jnp.zeros_like(loss_ref)
            dg2_ref[...] = jnp.zeros_like(dg2_ref)

        ff = jnp.dot(a_ref[...], w_ref[...], preferred_element_type=F32)
        err = x1_ref[...] + g2_ref[...] * ff - t_ref[...]
        dy = err * (1.0 / D)
        dy_ref[...] = dy
        dff_ref[...] = (dy * g2_ref[...]).astype(BF16)
        loss_ref[...] += _rows8(err * err)
        dg2_ref[...] += _rows8(dy * ff)

    row = pl.BlockSpec((tm, D), lambda i: (i, 0))
    acc = _full((SUBLANES, D))
    return pl.pallas_call(
        body, name="down_loss_head", grid=(T // tm,),
        in_specs=[row, pl.BlockSpec((tm, D_FF), lambda i: (i, 0)), _full((D_FF, D)), _full((1, D)), row],
        out_specs=[row, row, acc, acc],
        out_shape=[jax.ShapeDtypeStruct((T, D), F32), jax.ShapeDtypeStruct((T, D), BF16),
                   jax.ShapeDtypeStruct((SUBLANES, D), F32), jax.ShapeDtypeStruct((SUBLANES, D), F32)],
        compiler_params=_params(("arbitrary",)),
    )(x1, act, w_down, g2, target)


def _nt_dot(a, b):
    return lax.dot_general(a, b, (((1,), (1,)), ((), ())), preferred_element_type=F32)


def _causal(n, keys_on_rows=False):
    r = lax.broadcasted_iota(jnp.int32, (n, n), 0)
    c = lax.broadcasted_iota(jnp.int32, (n, n), 1)
    return (c >= r) if keys_on_rows else (c <= r)


def _sweep(lo, hi, step, carry, group=2):
    while group >= 1:
        def several(j, cr, lo=lo, group=group):
            for g in range(group):
                cr = step(lo + group * j + g, cr)
            return cr

        passes = (hi - lo) // group
        carry = lax.fori_loop(0, passes, several, carry)
        lo = lo + group * passes
        group //= 2
    return carry


def _grid_ends(n0, n1):
    i0, i1 = pl.program_id(0), pl.program_id(1)
    return jnp.logical_and(i0 == 0, i1 == 0), jnp.logical_and(i0 == n0 - 1, i1 == n1 - 1)


def _attn_fwd(qa, ka, va, exchange=None):
    nh, T, _ = qa.shape
    bq = min(ATTN_BLOCK, T)
    nq = T // bq

    def body(*refs):
        if exchange is None:
            q_ref, k_ref, v_ref, o_ref, qb_ref = refs
        else:
            (q_ref, k_ref, v_ref), (o_ref, qb_ref), _, xrefs = exchange.split(refs, 3, 2)
            _ride(exchange, *_grid_ends(nh, nq), xrefs)
        qi = pl.program_id(1)
        q = q_ref[0]

        def step(kb, carry, masked=False):
            m, acc = carry
            start = pl.multiple_of(kb * bq, bq)
            s = _nt_dot(q, k_ref[0, pl.ds(start, bq), :])
            if masked:
                s = jnp.where(_causal(bq), s, NEG_INF)
            m_new = jnp.maximum(m, jnp.max(s, axis=-1, keepdims=True))
            p = jnp.exp(s - m_new).astype(BF16)
            acc = jnp.exp(m - m_new) * acc + jnp.dot(p, v_ref[0, pl.ds(start, bq), :], preferred_element_type=F32)
            return m_new, acc

        init = (jnp.full((bq, 1), NEG_INF, F32), jnp.zeros((bq, LANES), F32))
        m, acc = step(qi, _sweep(0, qi, step, init, group=4), masked=True)
        l = acc[:, LANE_ONE:LANE_ONE + 1]
        o_ref[0] = acc / l
        lane = lax.broadcasted_iota(jnp.int32, (bq, LANES), 1)
        qf = q.astype(F32)
        for idx, piece in _run(LANE_LSE, _pieces(m + jnp.log(l))):
            qf = jnp.where(lane == idx, -piece, qf)
        qb_ref[0] = qf.astype(BF16)

    tile = pl.BlockSpec((1, bq, LANES), lambda h, i: (h, i, 0))
    whole = pl.BlockSpec((1, T, LANES), lambda h, i: (h, 0, 0))
    out_shape = [jax.ShapeDtypeStruct((nh, T, LANES), F32), jax.ShapeDtypeStruct((nh, T, LANES), BF16)]
    if exchange is None:
        return pl.pallas_call(
            body, name="attn_fwd", grid=(nh, nq), in_specs=[tile, whole, whole], out_specs=[tile, tile],
            out_shape=out_shape, compiler_params=_params(("parallel", "parallel")),
        )(qa, ka, va)
    return pl.pallas_call(
        body, name="attn_fwd", grid=(nh, nq), in_specs=[tile, whole, whole] + exchange.in_specs,
        out_specs=[tile, tile] + exchange.out_specs, out_shape=out_shape + exchange.out_shapes,
        scratch_shapes=exchange.scratch, compiler_params=_params(("arbitrary", "arbitrary")),
    )(qa, ka, va, *exchange.xs)


def _branch_merge_fwd(ya0, o_h, proj, wba, wbb_heads):
    nh, T, _ = o_h.shape
    tm = min(TOKEN_TILE, T)

    def body(ya0_ref, o_ref, ga_ref, gb_ref, wa_ref, wb_ref, ya_ref, yb_ref, m_ref):
        ya = jnp.dot(ya0_ref[...], wa_ref[...], preferred_element_type=F32)
        yb = jnp.dot(o_ref[0].astype(BF16), wb_ref[0:LANES, :], preferred_element_type=F32)
        for h in range(1, nh):
            yb += jnp.dot(o_ref[h].astype(BF16), wb_ref[h * LANES:(h + 1) * LANES, :], preferred_element_type=F32)
        ya_ref[...] = ya.astype(BF16)
        yb_ref[...] = yb.astype(BF16)
        m_ref[...] = (_sigmoid(ga_ref[...]) * ya + _sigmoid(gb_ref[...]) * yb).astype(BF16)

    row = pl.BlockSpec((tm, D), lambda i: (i, 0))
    return pl.pallas_call(
        body, name="branch_merge_fwd", grid=(T // tm,),
        in_specs=[pl.BlockSpec((tm, CONV_W), lambda i: (i, 0)), pl.BlockSpec((nh, tm, LANES), lambda i: (0, i, 0)),
                  pl.BlockSpec((tm, D), lambda i: (i, COL_GA // D)), pl.BlockSpec((tm, D), lambda i: (i, COL_GB // D)),
                  _full((CONV_W, D)), _full((nh * LANES, D))],
        out_specs=[row, row, row],
        out_shape=[jax.ShapeDtypeStruct((T, D), BF16)] * 3,
        compiler_params=_params(("parallel",)),
    )(ya0, o_h, proj, proj, wba, wbb_heads)


def _branch_b_bwd(dyb, o_h, wbb_heads):
    nh, T, _ = o_h.shape
    tm = min(TOKEN_TILE, T)

    def body(dyb_ref, o_ref, w_ref, out_ref):
        do = _nt_dot(dyb_ref[...], w_ref[...])
        lane = lax.broadcasted_iota(jnp.int32, (tm, LANES), 1)
        for h in range(nh):
            g = do[:, h * LANES:(h + 1) * LANES].astype(BF16).astype(F32)
            delta = jnp.sum(g * o_ref[h], axis=-1, keepdims=True)
            for idx, piece in _run(LANE_ONE, _pieces(delta)):
                g = jnp.where(lane == idx, -piece, g)
            out_ref[h] = g.astype(BF16)

    heads = pl.BlockSpec((nh, tm, LANES), lambda i: (0, i, 0))
    return pl.pallas_call(
        body, name="branch_b_bwd", grid=(T // tm,),
        in_specs=[pl.BlockSpec((tm, D), lambda i: (i, 0)), heads, _full((D, nh * LANES))],
        out_specs=heads, out_shape=jax.ShapeDtypeStruct((nh, T, LANES), BF16),
        compiler_params=_params(("parallel",)),
    )(dyb, o_h, wbb_heads)


def _branch_b_dw(o_h, dyb):
    nh, T, _ = o_h.shape
    tk = min(2 * MATMUL_TILE, T)

    def body(o_ref, dyb_ref, out_ref):
        @pl.when(pl.program_id(0) == 0)
        def _():
            out_ref[...] = jnp.zeros_like(out_ref)

        g = dyb_ref[...]
        for h in range(nh):
            out_ref[h] += _tn_dot(o_ref[h].astype(BF16), g)

    return pl.pallas_call(
        body, name="branch_b_dw", grid=(T // tk,),
        in_specs=[pl.BlockSpec((nh, tk, LANES), lambda k: (0, k, 0)), pl.BlockSpec((tk, D), lambda k: (k, 0))],
        out_specs=_full((nh, LANES, D)), out_shape=jax.ShapeDtypeStruct((nh, LANES, D), F32),
        compiler_params=_params(("arbitrary",)),
    )(o_h, dyb)


def _attn_bwd(qb, ka, va, doa, exchange=None):
    nh, T, _ = qb.shape
    bk = min(ATTN_BLOCK, T)
    nk = T // bk

    def body(*refs):
        if exchange is None:
            q_ref, do_ref, k_ref, v_ref, dq_ref, dk_ref, dv_ref = refs
        else:
            (q_ref, do_ref, k_ref, v_ref), (dq_ref, dk_ref, dv_ref), _, xrefs = exchange.split(refs, 4, 3)
            _ride(exchange, *_grid_ends(nh, nk), xrefs)
        ki = pl.program_id(1)

        @pl.when(ki == 0)
        def _():
            dq_ref[...] = jnp.zeros_like(dq_ref)

        k = k_ref[0]
        v = v_ref[0]

        def step(qi, carry, masked):
            dk, dv = carry
            rows = pl.ds(pl.multiple_of(qi * bk, bk), bk)
            q = q_ref[0, rows, :]
            g = do_ref[0, rows, :]
            pt = jnp.exp(_nt_dot(k, q))
            if masked:
                pt = jnp.where(_causal(bk, keys_on_rows=True), pt, 0.0)
            dv = dv + jnp.dot(pt.astype(BF16), g, preferred_element_type=F32)
            dst = (pt * _nt_dot(v, g)).astype(BF16)
            dk = dk + jnp.dot(dst, q, preferred_element_type=F32)
            dq_ref[0, rows, :] += _tn_dot(dst, k)
            return dk, dv

        init = (jnp.zeros((bk, LANES), F32), jnp.zeros((bk, LANES), F32))
        carry = step(ki, init, True)
        dk_ref[0], dv_ref[0] = _sweep(ki + 1, nk, lambda qi, cr: step(qi, cr, False), carry)

    tile = pl.BlockSpec((1, bk, LANES), lambda h, i: (h, i, 0))
    whole = pl.BlockSpec((1, T, LANES), lambda h, i: (h, 0, 0))
    out_shape = [jax.ShapeDtypeStruct((nh, T, LANES), F32)] * 3
    if exchange is None:
        return pl.pallas_call(
            body, name="attn_bwd", grid=(nh, nk), in_specs=[whole, whole, tile, tile],
            out_specs=[whole, tile, tile], out_shape=out_shape, compiler_params=_params(("parallel", "arbitrary")),
        )(qb, doa, ka, va)
    return pl.pallas_call(
        body, name="attn_bwd", grid=(nh, nk), in_specs=[whole, whole, tile, tile] + exchange.in_specs,
        out_specs=[whole, tile, tile] + exchange.out_specs, out_shape=out_shape + exchange.out_shapes,
        scratch_shapes=exchange.scratch, compiler_params=_params(("arbitrary", "arbitrary")),
    )(qb, doa, ka, va, *exchange.xs)


def _attn_unpack(dq_h, dk_h, dv_h):
    nh, T, _ = dq_h.shape
    tm = min(TOKEN_TILE, T)

    def body(dq_ref, dk_ref, dv_ref, q_out, k_out, v_out, f_out):
        lane = lax.broadcasted_iota(jnp.int32, (tm, LANES), 1)
        low = lane < HEAD_DIM
        for src, dst in ((dq_ref, q_out), (dk_ref, k_out), (dv_ref, v_out)):
            for pair in range(nh // 2):
                both = jnp.where(low, src[2 * pair], pltpu.roll(src[2 * pair + 1], HEAD_DIM, axis=1))
                dst[:, LANES * pair:LANES * (pair + 1)] = both.astype(dst.dtype)
        df = jnp.zeros((tm, LANES), F32)
        for h in range(nh):
            col = dq_ref[h][:, LANE_F:LANE_F + 1] - dk_ref[h][:, LANE_SUM:LANE_SUM + 1]
            df = jnp.where(lane == h, col, df)
        f_out[...] = df

    heads = pl.BlockSpec((nh, tm, LANES), lambda i: (0, i, 0))
    tok = pl.BlockSpec((tm, ATTN_W), lambda i: (i, 0))
    return pl.pallas_call(
        body, name="attn_unpack", grid=(T // tm,), in_specs=[heads, heads, heads],
        out_specs=[tok, tok, tok, pl.BlockSpec((tm, F_PAD), lambda i: (i, 0))],
        out_shape=[jax.ShapeDtypeStruct((T, ATTN_W), BF16)] * 3 + [jax.ShapeDtypeStruct((T, F_PAD), F32)],
        compiler_params=_params(("parallel",)),
    )(dq_h, dk_h, dv_h)


def _ffn_act_bwd(u, cg, cv, da, w8):
    T = u.shape[0]
    tm = min(FFN_TM, T)
    nt = T // tm
    nc = D_FF // FFN_TC

    def body(ug_ref, uv_ref, cg_ref, cv_ref, cgn_ref, cvn_ref, da_ref, dan_ref, wg_ref, wv_ref,
             dug_ref, duv_ref, dwg_ref, dwv_ref):
        i = pl.program_id(1)

        @pl.when(i == 0)
        def _():
            dwg_ref[...] = jnp.zeros_like(dwg_ref)
            dwv_ref[...] = jnp.zeros_like(dwv_ref)

        gate = jnp.concatenate([cg_ref[...], cgn_ref[...]], axis=0).astype(F32)
        val = jnp.concatenate([cv_ref[...], cvn_ref[...]], axis=0).astype(F32)
        dae = jnp.concatenate([da_ref[...], dan_ref[...]], axis=0).astype(F32)
        rows_e = lax.broadcasted_iota(jnp.int32, dae.shape, 0)
        dae = jnp.where(jnp.logical_and(i == nt - 1, rows_e >= tm), 0.0, dae)
        sg = _sigmoid(gate)
        n = tm + BF16_ROWS

        def back(d, u_ref, w_ref, du_ref, dw_ref):
            w = w_ref[...]
            uv = u_ref[...]
            d1 = pltpu.roll(d, n - 1, axis=0)[:tm]
            d2 = pltpu.roll(d, n - 2, axis=0)[:tm]
            d0 = d[:tm]
            du_ref[...] = (d0 * w[2:3] + d1 * w[1:2] + d2 * w[0:1]).astype(BF16)
            rows = [jnp.sum(t * uv, axis=0, keepdims=True) for t in (d2, d1, d0)]
            dw_ref[...] += jnp.concatenate(rows + [jnp.zeros((SUBLANES - 3, FFN_TC), F32)], axis=0)

        back(dae * val * sg * (1.0 + gate * (1.0 - sg)), ug_ref, wg_ref, dug_ref, dwg_ref)
        back(dae * gate * sg, uv_ref, wv_ref, duv_ref, dwv_ref)

    per = tm // BF16_ROWS
    last_blk = nt * per - 1
    blk = lambda off: pl.BlockSpec((tm, FFN_TC), lambda j, i: (i, j + off))
    nxt = pl.BlockSpec((BF16_ROWS, FFN_TC), lambda j, i: (jnp.minimum((i + 1) * per, last_blk), j))
    wblk = lambda off: pl.BlockSpec((SUBLANES, FFN_TC), lambda j, i: (0, j + off))
    dug, duv, dwg, dwv = pl.pallas_call(
        body, name="ffn_act_bwd", grid=(nc, nt),
        in_specs=[blk(0), blk(nc), blk(0), blk(0), nxt, nxt, blk(0), nxt, wblk(0), wblk(nc)],
        out_specs=[blk(0), blk(0), wblk(0), wblk(0)],
        out_shape=[jax.ShapeDtypeStruct((T, D_FF), BF16)] * 2 + [jax.ShapeDtypeStruct((SUBLANES, D_FF), F32)] * 2,
        compiler_params=_params(("parallel", "arbitrary")),
    )(u, u, cg, cv, cg, cv, da, da, w8, w8)
    return dug, duv, jnp.concatenate([dwg, dwv], axis=1)


def _out_merge_bwd(dmix, w_out, ya, yb, proj):
    T = ya.shape[0]
    tm = min(TOKEN_TILE, T)

    def body(dmix_ref, w_ref, ya_ref, yb_ref, ga_ref, gb_ref, dya_ref, dyb_ref, dga_ref, dgb_ref):
        dm = _nt_dot(dmix_ref[...], w_ref[...])
        sa = _sigmoid(ga_ref[...])
        sb = _sigmoid(gb_ref[...])
        dya_ref[...] = (dm * sa).astype(BF16)
        dyb_ref[...] = (dm * sb).astype(BF16)
        dga_ref[...] = (dm * ya_ref[...].astype(F32) * sa * (1.0 - sa)).astype(BF16)
        dgb_ref[...] = (dm * yb_ref[...].astype(F32) * sb * (1.0 - sb)).astype(BF16)

    row = pl.BlockSpec((tm, D), lambda i: (i, 0))
    return pl.pallas_call(
        body, name="out_merge_bwd", grid=(T // tm,),
        in_specs=[row, _full((D, D)), row, row, pl.BlockSpec((tm, D), lambda i: (i, COL_GA // D)),
                  pl.BlockSpec((tm, D), lambda i: (i, COL_GB // D))],
        out_specs=[row] * 4, out_shape=[jax.ShapeDtypeStruct((T, D), BF16)] * 4,
        compiler_params=_params(("parallel",)),
    )(dmix, w_out, ya, yb, proj, proj)


def _conv_branch_bwd(proj, dya0, conv_w8):
    T = proj.shape[0]
    tm = min(FFN_TM, T)
    nt = T // tm

    def body(cb_ref, cc_ref, cv_ref, cbn_ref, ccp_ref, cvp_ref, ccn_ref, cvn_ref, d_ref, dn_ref, w_ref,
             d3_ref, dw_ref):
        i = pl.program_id(0)

        @pl.when(i == 0)
        def _():
            dw_ref[...] = jnp.zeros_like(dw_ref)

        first, last = i == 0, i == nt - 1
        w = w_ref[...]
        cc = jnp.concatenate([ccp_ref[...], cc_ref[...], ccn_ref[...]], axis=0)
        cv = jnp.concatenate([cvp_ref[...], cv_ref[...], cvn_ref[...]], axis=0)
        rows = lax.broadcasted_iota(jnp.int32, cc.shape, 0)
        z = jnp.where(jnp.logical_and(first, rows < SUBLANES), 0.0, cc * cv)
        z1 = pltpu.roll(z, 1, axis=0)
        z2 = pltpu.roll(z, 2, axis=0)
        cz = z2 * w[0:1] + z1 * w[1:2] + z * w[2:3]
        zeros8 = jnp.zeros((SUBLANES, CONV_W), F32)
        de = jnp.concatenate([zeros8, d_ref[...], jnp.where(last, 0.0, dn_ref[...])], axis=0)
        cbe = jnp.concatenate([zeros8, cb_ref[...], cbn_ref[...]], axis=0)
        dcz = de * cbe
        n = tm + 2 * SUBLANES
        dz = dcz * w[2:3] + pltpu.roll(dcz, n - 1, axis=0) * w[1:2] + pltpu.roll(dcz, n - 2, axis=0) * w[0:1]
        inner = slice(SUBLANES, SUBLANES + tm)
        d3_ref[:, 0:CONV_W] = (de * cz)[inner].astype(BF16)
        d3_ref[:, CONV_W:2 * CONV_W] = (dz * cv)[inner].astype(BF16)
        d3_ref[:, 2 * CONV_W:3 * CONV_W] = (dz * cc)[inner].astype(BF16)
        wrows = [jnp.sum((dcz * t)[inner], axis=0, keepdims=True) for t in (z2, z1, z)]
        dw_ref[...] += jnp.concatenate(wrows + [jnp.zeros((SUBLANES - 3, CONV_W), F32)], axis=0)

    blk = lambda col: pl.BlockSpec((tm, CONV_W), lambda i: (i, col))
    out_blk = pl.BlockSpec((tm, CONV_W), lambda i: (i, 0))
    return pl.pallas_call(
        body, name="conv_branch_bwd", grid=(nt,),
        in_specs=[blk(0), blk(1), blk(2), _next_spec(tm, CONV_W, 0, nt),
                  _prev_spec(tm, CONV_W, 1), _prev_spec(tm, CONV_W, 2),
                  _next_spec(tm, CONV_W, 1, nt), _next_spec(tm, CONV_W, 2, nt),
                  out_blk, _next_spec(tm, CONV_W, 0, nt), _full((SUBLANES, CONV_W))],
        out_specs=[pl.BlockSpec((tm, 3 * CONV_W), lambda i: (i, 0)), _full((SUBLANES, CONV_W))],
        out_shape=[jax.ShapeDtypeStruct((T, 3 * CONV_W), BF16), jax.ShapeDtypeStruct((SUBLANES, CONV_W), F32)],
        compiler_params=_params(("arbitrary",)),
    )(proj, proj, proj, proj, proj, proj, proj, proj, dya0, dya0, conv_w8)


def _qk_norm_bwd(proj, dqs, dkh, dlogf, qg, kg, bf_pad, gmat):
    T = proj.shape[0]
    tm = min(TOKEN_TILE, T)

    def body(q_ref, k_ref, f_ref, dqs_ref, dkh_ref, dlf_ref, qg_ref, kg_ref, bf_ref, g_ref,
             dqk_ref, dfl_ref, dqg_ref, dkg_ref, dbf_ref):
        @pl.when(pl.program_id(0) == 0)
        def _():
            dqg_ref[...] = jnp.zeros_like(dqg_ref)
            dkg_ref[...] = jnp.zeros_like(dkg_ref)
            dbf_ref[...] = jnp.zeros_like(dbf_ref)

        gm = g_ref[...]
        for src, d_src, gain, scale, dst, dgain in (
                (q_ref, dqs_ref, qg_ref, 1.0 / np.sqrt(HEAD_DIM), dqk_ref.at[:, 0:ATTN_W], dqg_ref),
                (k_ref, dkh_ref, kg_ref, 1.0, dqk_ref.at[:, ATTN_W:2 * ATTN_W], dkg_ref)):
            v = src[...]
            dhat = d_src[...].astype(F32) * scale
            inv = lax.rsqrt(_split_dot(v * v, gm, 2) * (1.0 / HEAD_DIM) + RMS_EPS)
            vn = v * inv
            dgain[...] += _rows8(dhat * vn)
            dvn = dhat * gain[...]
            mean = _split_dot(dvn * vn, gm, 2) * (1.0 / HEAD_DIM)
            dst[...] = (inv * (dvn - vn * mean)).astype(BF16)
        fx = f_ref[...] + bf_ref[...]
        dfl = dlf_ref[...] * _sigmoid(-fx)
        dfl_ref[...] = dfl.astype(BF16)
        dbf_ref[...] += _rows8(dfl)

    blk = lambda col: pl.BlockSpec((tm, ATTN_W), lambda i: (i, col))
    out_blk = pl.BlockSpec((tm, ATTN_W), lambda i: (i, 0))
    f_in = pl.BlockSpec((tm, F_PAD), lambda i: (i, COL_F // F_PAD))
    f_blk = pl.BlockSpec((tm, F_PAD), lambda i: (i, 0))
    return pl.pallas_call(
        body, name="qk_norm_bwd", grid=(T // tm,),
        in_specs=[blk(3), blk(4), f_in, out_blk, out_blk, f_blk, _full((1, ATTN_W)), _full((1, ATTN_W)),
                  _full((1, F_PAD)), _full((ATTN_W, ATTN_W))],
        out_specs=[pl.BlockSpec((tm, 2 * ATTN_W), lambda i: (i, 0)), f_blk, _full((SUBLANES, ATTN_W)),
                   _full((SUBLANES, ATTN_W)), _full((SUBLANES, F_PAD))],
        out_shape=[jax.ShapeDtypeStruct((T, 2 * ATTN_W), BF16), jax.ShapeDtypeStruct((T, F_PAD), BF16)]
        + [jax.ShapeDtypeStruct((SUBLANES, ATTN_W), F32)] * 2 + [jax.ShapeDtypeStruct((SUBLANES, F_PAD), F32)],
        compiler_params=_params(("arbitrary",)),
    )(proj, proj, proj, dqs, dkh, dlogf, qg, kg, bf_pad, gmat)


def _pad_rows8(w):
    return jnp.pad(w, ((0, SUBLANES - w.shape[0]), (0, 0)))


def _fold8(acc):
    return jnp.sum(acc, axis=0, keepdims=True)


def _late_weights(mats):
    out = {name: mats[name] for name in ("w_branch_a", "w_out", "w_up", "w_down")}
    out["w_branch_b_heads"] = _pad_head_rows(mats["w_branch_b"])
    return out


def _local_step(x, target, mod, wts, late=None):
    T = x.shape[0]
    tb = min(MATMUL_TILE, T)
    tk_long = min(2 * MATMUL_TILE, T)
    tm = min(TOKEN_TILE, T)
    sh1, sc1, g1, sh2, sc2, g2 = [mod[:, i * D:(i + 1) * D] for i in range(N_MOD)]
    w_in = wts["w_in"]
    conv_a8 = _pad_rows8(wts["conv_a_w"])
    conv_f8 = _pad_rows8(wts["conv_ffn_w"])
    qg = jnp.tile(wts["q_norm_g"], (1, N_HEADS))
    kg = jnp.tile(wts["k_norm_g"], (1, N_HEADS))
    bf_pad = jnp.pad(wts["b_f"], ((0, 0), (0, F_PAD - N_HEADS)))
    gmat = _group_matrix()

    h = _norm_mod(x, wts["norm1_g"], sc1, sh1, name="norm1_fwd")
    proj = _matmul(h, w_in, name="mm_in", tm=tb, tn=2688, tk=D)
    fcum = _cumsum(proj, reverse=False, name="gate_cumsum", col=COL_F // F_PAD, gate_bias=bf_pad)
    ya0, qa, ka, va = _branch_prep(proj, fcum, conv_a8, qg, kg, gmat)
    if late is None:
        o_h, qb = _attn_fwd(qa, ka, va)
    else:
        o_h, qb, *gathered = _attn_fwd(qa, ka, va, _Exchange([late[name] for name, *_ in LATE], scatter=False))
        wts = dict(wts)
        mats = {name: _join_shards(g, axis) for (name, _, _, axis), g in zip(LATE, gathered) if name != "w_up"}
        mats["w_up"] = _assemble_columns(gathered[[name for name, *_ in LATE].index("w_up")], 2 * D_FF // N_DEV,
                                         2 * D_FF, ((0, 2 * D_FF, 0),), name="assemble_w_up")
        wts.update(_late_weights(mats))
    ya, yb, merged = _branch_merge_fwd(ya0, o_h, proj, wts["w_branch_a"], wts["w_branch_b_heads"])
    mix, x1, h2 = _out_resid_norm(x, merged, wts["w_out"], g1, wts["norm2_g"], sc2, sh2)
    u = _matmul(h2, wts["w_up"], name="mm_up", tm=tb, tn=2816, tk=D)
    act, conv_gate, conv_val = _ffn_act_fwd(u, conv_f8)
    dy, dff, sq8, dg2_8 = _down_loss_head(x1, act, wts["w_down"], g2, target)
    sq = jnp.sum(sq8).reshape(1, 1)

    grads = {}
    da = _matmul(dff, wts["w_down"], name="mm_down_dx", tm=tb, tn=D_FF, tk=D, out_dtype=BF16, trans_b=True)
    grads["w_down"] = _matmul(act, dff, name="mm_down_dw", tm=1408, tn=D, tk=tk_long, trans_a=True)
    dug, duv, dconv_f8 = _ffn_act_bwd(u, conv_gate, conv_val, da, conv_f8)
    grads["conv_ffn_w"] = dconv_f8[:3]
    dx1, dsh2_8, dsc2_8, dn2_8, dmix, dg1_8 = _matmul_pieces(
        [dug, duv], wts["w_up"], name="mm_up_dx", tm=tm, norm_bwd=(x1, dy, wts["norm2_g"], sc2, mix, g1))
    dw_up = [_matmul(h2, d, name="mm_up_dw_" + half, tm=D, tn=1408, tk=tk_long, trans_a=True)
             for half, d in (("gate", dug), ("val", duv))]
    if late is None:
        grads["w_up"] = jnp.concatenate(dw_up, axis=1)
    grads["norm2_g"] = _fold8(dn2_8)

    grads["w_out"] = _matmul(merged, dmix, name="mm_out_dw", tm=D, tn=D, tk=tk_long, trans_a=True)
    dya, dyb, dga, dgb = _out_merge_bwd(dmix, wts["w_out"], ya, yb, proj)
    dya0 = _matmul(dya, wts["w_branch_a"], name="mm_branch_a_dx", tm=tb, tn=CONV_W, tk=D, trans_b=True)
    grads["w_branch_a"] = _matmul(ya0, dya, name="mm_branch_a_dw", tm=CONV_W, tn=D, tk=tk_long, trans_a=True)
    doa = _branch_b_bwd(dyb, o_h, wts["w_branch_b_heads"])
    grads["w_branch_b"] = _branch_b_dw(o_h, dyb)[:, :HEAD_DIM].reshape(ATTN_W, D)
    dconv3, dconv_a8 = _conv_branch_bwd(proj, dya0, conv_a8)
    grads["conv_a_w"] = dconv_a8[:3]

    parts = {}
    if late is None:
        dq_h, dk_h, dv_h = _attn_bwd(qb, ka, va, doa)
    else:
        ready = [(_column_shards(dw_up) if name == "w_up" else _split_shards(grads[name], axis)).astype(BF16)
                 for name, _, _, axis in LATE]
        dq_h, dk_h, dv_h, *recv = _attn_bwd(
            qb, ka, va, doa, _Exchange(ready + [_pack_full_by_dest(grads, CONVS, SUBLANES)], scatter=True))
        parts = dict(zip([name for name, *_ in LATE] + ["conv"], recv))
    dq_tok, dk_tok, dv_tok, dfcum = _attn_unpack(dq_h, dk_h, dv_h)
    dlogf = _cumsum(dfcum, reverse=True, name="gate_cumsum_bwd")
    dqk, dfl, dqg8, dkg8, dbf8 = _qk_norm_bwd(proj, dq_tok, dk_tok, dlogf, qg, kg, bf_pad, gmat)
    grads["q_norm_g"] = jnp.sum(_fold8(dqg8).reshape(N_HEADS, HEAD_DIM), axis=0, keepdims=True)
    grads["k_norm_g"] = jnp.sum(_fold8(dkg8).reshape(N_HEADS, HEAD_DIM), axis=0, keepdims=True)
    grads["b_f"] = _fold8(dbf8)[:, :N_HEADS]
    narrow, wide = [dconv3, dqk, dv_tok], [dga, dgb, dfl]
    dw_narrow = _matmul_tn_pieces(h, narrow, name="mm_in_dw_narrow", tk=tb)
    dwa, dwb, dwf = _matmul_tn_pieces(h, wide, name="mm_in_dw_wide", tk=tk_long)
    dw_in = list(dw_narrow) + [dwf[:, :N_HEADS], dwa, dwb]
    norm1 = (x, dx1, wts["norm1_g"], sc1)
    if late is None:
        grads["w_in"] = jnp.concatenate(dw_in, axis=1)
        grad_x, dsh1_8, dsc1_8, dn1_8 = _matmul_pieces(narrow + wide, w_in, name="mm_in_dx", tm=tm,
                                                       norm_bwd=norm1)
    else:
        grad_x, dsh1_8, dsc1_8, dn1_8, parts["w_in"] = _matmul_pieces(
            narrow + wide, w_in, name="mm_in_dx", tm=tm, norm_bwd=norm1,
            exchange=_Exchange([_column_shards(dw_in).astype(BF16)], scatter=True))
    grads["norm1_g"] = _fold8(dn1_8)
    grads["mod"] = jnp.concatenate([_fold8(a) for a in (dsh1_8, dsc1_8, dg1_8, dsh2_8, dsc2_8, dg2_8)], axis=1)
    return sq, grad_x, grads, parts


def _me_and_peers():
    mx, my, mc = lax.axis_index("x"), lax.axis_index("y"), lax.axis_index("c")
    me = 4 * mx + 2 * my + mc
    peers = []
    for k in range(1, N_DEV):
        px = 1 - mx if k & 4 else mx
        py = 1 - my if k & 2 else my
        pc = 1 - mc if k & 1 else mc
        peers.append(((px, py, pc), 4 * px + 2 * py + pc))
    return me, peers


HBM_SPEC = pl.BlockSpec(memory_space=pltpu.HBM)


class _Exchange:
    def __init__(self, xs, scatter):
        self.xs, self.scatter, self.n = list(xs), scatter, len(xs)
        self.out_shapes = [jax.ShapeDtypeStruct(x.shape if scatter else (N_DEV,) + x.shape, x.dtype) for x in xs]
        self.in_specs = [HBM_SPEC] * self.n
        self.out_specs = [HBM_SPEC] * self.n
        self.scratch = [pltpu.SemaphoreType.DMA((self.n, N_DEV - 1)), pltpu.SemaphoreType.DMA((self.n, N_DEV - 1)),
                        pltpu.SemaphoreType.DMA((self.n,))]

    def _copies(self, x_refs, out_refs, sems):
        send_sems, recv_sems, local_sems = sems
        me, peers = _me_and_peers()

        def src(a, idx):
            return x_refs[a].at[idx] if self.scatter else x_refs[a]

        def copy(a, k, from_idx, to_slot, device):
            return pltpu.make_async_remote_copy(
                src_ref=src(a, from_idx), dst_ref=out_refs[a].at[to_slot], send_sem=send_sems.at[a, k],
                recv_sem=recv_sems.at[a, k], device_id=device, device_id_type=MESH)

        local = [pltpu.make_async_copy(src(a, me), out_refs[a].at[me], local_sems.at[a]) for a in range(self.n)]
        sends = [copy(a, k, idx, me, dev) for a in range(self.n) for k, (dev, idx) in enumerate(peers)]
        recvs = [copy(a, k, idx, idx, dev) for a in range(self.n) for k, (dev, idx) in enumerate(peers)]
        return local, sends, recvs

    def start(self, x_refs, out_refs, sems):
        local, sends, _ = self._copies(x_refs, out_refs, sems)
        for cp in local + sends:
            cp.start()

    def wait(self, x_refs, out_refs, sems):
        local, sends, recvs = self._copies(x_refs, out_refs, sems)
        for cp in recvs:
            cp.wait_recv()
        for cp in sends:
            cp.wait_send()
        for cp in local:
            cp.wait()

    def split(self, refs, n_in, n_out):
        n = self.n
        ins, xin = refs[:n_in], refs[n_in:n_in + n]
        outs, xout = refs[n_in + n:n_in + n + n_out], refs[n_in + n + n_out:n_in + 2 * n + n_out]
        rest = refs[n_in + 2 * n + n_out:]
        return ins, outs, rest[:len(rest) - 3], (xin, xout, rest[len(rest) - 3:])


def _ride(exchange, first, last, refs):
    if exchange is None:
        return

    @pl.when(first)
    def _():
        exchange.start(*refs)

    @pl.when(last)
    def _():
        exchange.wait(*refs)


def _gather_two_level(xs, *, name):
    n = len(xs)
    out_shapes = [jax.ShapeDtypeStruct((N_DEV,) + x.shape, x.dtype) for x in xs]

    def body(*refs):
        x_refs, out_refs = refs[:n], refs[n:2 * n]
        send_sems, recv_sems, local_sems = refs[2 * n:]
        x, y, c = lax.axis_index("x"), lax.axis_index("y"), lax.axis_index("c")
        me, sibling = (x, y, c), (x, y, 1 - c)
        chips = [(1 - x, y), (x, 1 - y), (1 - x, 1 - y)]

        def slot(a, dev):
            return out_refs[a].at[4 * dev[0] + 2 * dev[1] + dev[2]]

        def copy(a, k, block, to, src=None):
            return pltpu.make_async_remote_copy(
                src_ref=slot(a, block) if src is None else src, dst_ref=slot(a, block),
                send_sem=send_sems.at[a, k], recv_sem=recv_sems.at[a, k], device_id=to, device_id_type=MESH)

        mine = [pltpu.make_async_copy(x_refs[a], slot(a, me), local_sems.at[a]) for a in range(n)]
        first = [copy(a, 0, me, sibling, src=x_refs[a]) for a in range(n)]
        first += [copy(a, 1 + j, me, (*chip, c), src=x_refs[a]) for a in range(n) for j, chip in enumerate(chips)]
        for cp in mine + first:
            cp.start()
        passed = []
        for a in range(n):
            for j, chip in enumerate(chips):
                copy(a, 1 + j, (*chip, c), me).wait_recv()
                passed.append(copy(a, 4 + j, (*chip, c), sibling))
                passed[-1].start()
        for a in range(n):
            copy(a, 0, sibling, me).wait_recv()
            for j, chip in enumerate(chips):
                copy(a, 4 + j, (*chip, 1 - c), me).wait_recv()
        for cp in first + passed:
            cp.wait_send()
        for cp in mine:
            cp.wait()

    return pl.pallas_call(
        body, name=name, in_specs=[HBM_SPEC] * n, out_specs=[HBM_SPEC] * n, out_shape=out_shapes,
        scratch_shapes=[pltpu.SemaphoreType.DMA((n, N_DEV - 1)), pltpu.SemaphoreType.DMA((n, N_DEV - 1)),
                        pltpu.SemaphoreType.DMA((n,))],
        compiler_params=pltpu.CompilerParams(has_side_effects=True),
    )(*xs)


def _exchange(xs, *, name, scatter):
    ex = _Exchange(xs, scatter)

    def body(*refs):
        _, _, _, xrefs = ex.split(refs, 0, 0)
        ex.start(*xrefs)
        ex.wait(*xrefs)

    return pl.pallas_call(
        body, name=name, in_specs=ex.in_specs, out_specs=ex.out_specs, out_shape=ex.out_shapes,
        scratch_shapes=ex.scratch, compiler_params=pltpu.CompilerParams(has_side_effects=True),
    )(*xs)


def _ada_fwd(c_all, w_shard, b_shard):
    n = w_shard.shape[1]

    def body(c_ref, w_ref, b_ref, o_ref):
        cv = c_ref[...]
        act = (cv * _sigmoid(cv)).astype(BF16)
        o_ref[...] = jnp.dot(act, w_ref[...].astype(BF16), preferred_element_type=F32) + b_ref[...]

    return pl.pallas_call(
        body, name="ada_fwd", in_specs=[_full((N_DEV, D)), _full((D, n)), _full((1, n))],
        out_specs=_full((N_DEV, n)), out_shape=jax.ShapeDtypeStruct((N_DEV, n), F32), grid=(1,),
        compiler_params=_params(("arbitrary",)),
    )(c_all, w_shard, b_shard)


def _ada_bwd(c_all_t, dmod_pad):
    n = dmod_pad.shape[1]

    def body(c_ref, d_ref, o_ref):
        cv = c_ref[...]
        act = (cv * _sigmoid(cv)).astype(BF16)
        o_ref[...] = jnp.dot(act, d_ref[...].astype(BF16), preferred_element_type=F32)

    return pl.pallas_call(
        body, name="ada_bwd", in_specs=[_full((D, LANES)), _full((LANES, n))],
        out_specs=_full((D, n)), out_shape=jax.ShapeDtypeStruct((D, n), F32), grid=(1,),
        compiler_params=_params(("arbitrary",)),
    )(c_all_t, dmod_pad)


ADAM_ROWS = 256


def _adamw(parts, w, m, v, *, name):
    n, R, C = parts.shape
    tr = next((t for t in (ADAM_ROWS, 128, 64, 32, 16, SUBLANES) if R % t == 0), R)

    def body(p_ref, w_ref, m_ref, v_ref, g_ref, d_ref, nm_ref, nv_ref):
        g = p_ref[0].astype(F32)
        for j in range(1, n):
            g = g + p_ref[j].astype(F32)
        g_ref[...] = g
        nm = ADAM_B1 * m_ref[...] + (1.0 - ADAM_B1) * g
        nv = ADAM_B2 * v_ref[...] + (1.0 - ADAM_B2) * (g * g)
        nm_ref[...] = nm
        nv_ref[...] = nv
        m_hat = nm / (1.0 - ADAM_B1 ** ADAM_STEP)
        v_hat = nv / (1.0 - ADAM_B2 ** ADAM_STEP)
        d_ref[...] = -ADAM_LR * (m_hat / (jnp.sqrt(v_hat) + ADAM_EPS) + ADAM_WD * w_ref[...])

    row = pl.BlockSpec((tr, C), lambda i: (i, 0))
    return pl.pallas_call(
        body, name=name, grid=(R // tr,),
        in_specs=[pl.BlockSpec((n, tr, C), lambda i: (0, i, 0)), row, row, row], out_specs=[row] * 4,
        out_shape=[jax.ShapeDtypeStruct((R, C), F32)] * 4,
        compiler_params=_params(("parallel",)),
    )(parts, w, m, v)


SHARDED = (("w_in", D, IN_W, 1), ("w_branch_a", CONV_W, D, 1), ("w_branch_b", ATTN_W, D, 1), ("w_out", D, D, 0),
           ("w_up", D, 2 * D_FF, 1), ("w_down", D_FF, D, 0), ("conv_a_w", 3, CONV_W, 1),
           ("conv_ffn_w", 3, 2 * D_FF, 1))
MATRICES = SHARDED[:6]
LATE = MATRICES[1:]
CONVS = SHARDED[6:]
REPLICATED = (("b_ada", N_MOD * D), ("norm1_g", D), ("norm2_g", D), ("b_f", N_HEADS), ("q_norm_g", HEAD_DIM),
              ("k_norm_g", HEAD_DIM))


def _shard_shape(rows, cols, axis):
    return (rows // N_DEV, cols) if axis == 0 else (rows, cols // N_DEV)


def _pack_rows(flat, multiple):
    length = flat.shape[-1]
    rows = -(-length // PACK_W)
    rows = -(-rows // multiple) * multiple
    pad = [(0, 0)] * (flat.ndim - 1) + [(0, rows * PACK_W - length)]
    return jnp.pad(flat, pad).reshape(flat.shape[:-1] + (rows, PACK_W))


def _pack_shards(shards, spec, multiple, dtype):
    flat = jnp.concatenate([shards[name].reshape(-1).astype(dtype) for name, *_ in spec])
    return _pack_rows(flat, multiple)


def _join_lane_blocks(gathered):
    n, r, c = gathered.shape

    def body(g_ref, o_ref):
        for j in range(n):
            o_ref[:, j * c:(j + 1) * c] = g_ref[j]

    return pl.pallas_call(
        body, name="join_lane_blocks", grid=(1,), in_specs=[_full((n, r, c))], out_specs=_full((r, n * c)),
        out_shape=jax.ShapeDtypeStruct((r, n * c), gathered.dtype), compiler_params=_params(("arbitrary",)),
    )(gathered)


SHARD_PAD = 768


def _assemble_columns(gathered, shard_cols, out_cols, segments, *, name):
    n, rows, padw = gathered.shape
    assert n == N_DEV and padw == SHARD_PAD and shard_cols <= SHARD_PAD

    def body(g_ref, o_ref):
        j = pl.program_id(0)

        @pl.when(j == 0)
        def _():
            o_ref[...] = jnp.zeros_like(o_ref)

        for dev in range(N_DEV):
            @pl.when(j == dev)
            def _(dev=dev):
                x = g_ref[0]
                for lo, hi, delta in segments:
                    a, b = max(lo, dev * shard_cols), min(hi, (dev + 1) * shard_cols)
                    if a >= b:
                        continue
                    base = (a + delta) // LANES * LANES
                    width = -(-(b + delta - base) // LANES) * LANES
                    src = lax.broadcasted_iota(jnp.int32, (padw, width), 0) + dev * shard_cols
                    dst = lax.broadcasted_iota(jnp.int32, (padw, width), 1) + (base - delta)
                    place = jnp.where((src == dst) & (src >= a) & (src < b), 1.0, 0.0).astype(BF16)
                    moved = jnp.dot(x, place, preferred_element_type=F32).astype(BF16)
                    o_ref[:, base:base + width] = o_ref[:, base:base + width] + moved

    return pl.pallas_call(
        body, name=name, grid=(N_DEV,), in_specs=[pl.BlockSpec((1, rows, padw), lambda j: (j, 0, 0))],
        out_specs=_full((rows, out_cols)), out_shape=jax.ShapeDtypeStruct((rows, out_cols), BF16),
        compiler_params=_params(("arbitrary",)),
    )(gathered)


def _pad_shard(w):
    return jnp.pad(w.astype(BF16), ((0, 0), (0, SHARD_PAD - w.shape[1])))


W_IN_SEGMENTS = ((0, COL_GA, 0), (COL_GA, COL_GA + N_HEADS, COL_F - COL_GA), (COL_GA + N_HEADS, IN_W, -N_HEADS))


def _join_shards(gathered, axis):
    if axis == 0:
        return gathered.reshape(N_DEV * gathered.shape[1], gathered.shape[2])
    if gathered.shape[2] == LANES:
        return _join_lane_blocks(gathered)
    return jnp.concatenate([gathered[j] for j in range(N_DEV)], axis=1)


def _column_shards(pieces):
    total = sum(p.shape[1] for p in pieces)
    width = total // N_DEV
    shards = []
    for j in range(N_DEV):
        lo, hi, off, segs = j * width, (j + 1) * width, 0, []
        for p in pieces:
            a, b = max(lo, off), min(hi, off + p.shape[1])
            if a < b:
                segs.append(p[:, a - off:b - off])
            off += p.shape[1]
        shards.append(segs[0] if len(segs) == 1 else jnp.concatenate(segs, axis=1))
    return jnp.stack(shards)


def _split_shards(full, axis):
    if axis == 0:
        return full.reshape(N_DEV, full.shape[0] // N_DEV, full.shape[1])
    c = full.shape[1] // N_DEV
    return jnp.stack([full[:, j * c:(j + 1) * c] for j in range(N_DEV)])


def _unpack_shards(packed, spec):
    flat = packed.reshape(-1)
    out, off = {}, 0
    for name, rows, cols, axis in spec:
        r, c = _shard_shape(rows, cols, axis)
        out[name] = flat[off:off + r * c].reshape(r, c)
        off += r * c
    return out


def _unpack_gathered(gathered, spec):
    flat = gathered.reshape(N_DEV, -1)
    out, off = {}, 0
    for name, rows, cols, axis in spec:
        r, c = _shard_shape(rows, cols, axis)
        seg = flat[:, off:off + r * c].reshape(N_DEV, r, c)
        out[name] = seg.reshape(rows, cols) if axis == 0 else seg.transpose(1, 0, 2).reshape(rows, cols)
        off += r * c
    return out


def _pack_full_by_dest(full, spec, multiple):
    segs = []
    for name, rows, cols, axis in spec:
        r, c = _shard_shape(rows, cols, axis)
        a = full[name]
        seg = a.reshape(N_DEV, r, c) if axis == 0 else a.reshape(rows, N_DEV, c).transpose(1, 0, 2)
        segs.append(seg.reshape(N_DEV, r * c))
    return _pack_rows(jnp.concatenate(segs, axis=1), multiple)


def _pad_head_rows(w):
    n = w.shape[1]
    padded = jnp.pad(w.reshape(N_HEADS, HEAD_DIM, n), ((0, 0), (0, LANES - HEAD_DIM), (0, 0)))
    return padded.reshape(N_HEADS * LANES, n)


def kernel(x, c, w_ada, b_ada, norm1_g, w_in, b_f, conv_a_w, q_norm_g, k_norm_g, w_branch_a, w_branch_b, w_out, norm2_g, w_up, conv_ffn_w, w_down, loss_target, m_w_ada, m_b_ada, m_norm1_g, m_w_in, m_b_f, m_conv_a_w, m_q_norm_g, m_k_norm_g, m_w_branch_a, m_w_branch_b, m_w_out, m_norm2_g, m_w_up, m_conv_ffn_w, m_w_down, v_w_ada, v_b_ada, v_norm1_g, v_w_in, v_b_f, v_conv_a_w, v_q_norm_g, v_k_norm_g, v_w_branch_a, v_w_branch_b, v_w_out, v_norm2_g, v_w_up, v_conv_ffn_w, v_w_down):
    names = ("w_ada", "b_ada", "norm1_g", "w_in", "b_f", "conv_a_w", "q_norm_g", "k_norm_g", "w_branch_a",
             "w_branch_b", "w_out", "norm2_g", "w_up", "conv_ffn_w", "w_down")
    squeeze = lambda a: a[0] if a.ndim == 3 else a
    W = dict(zip(names, map(squeeze, (w_ada, b_ada, norm1_g, w_in, b_f, conv_a_w, q_norm_g, k_norm_g, w_branch_a,
                                      w_branch_b, w_out, norm2_g, w_up, conv_ffn_w, w_down))))
    M = dict(zip(names, map(squeeze, (m_w_ada, m_b_ada, m_norm1_g, m_w_in, m_b_f, m_conv_a_w, m_q_norm_g,
                                      m_k_norm_g, m_w_branch_a, m_w_branch_b, m_w_out, m_norm2_g, m_w_up,
                                      m_conv_ffn_w, m_w_down))))
    V = dict(zip(names, map(squeeze, (v_w_ada, v_b_ada, v_norm1_g, v_w_in, v_b_f, v_conv_a_w, v_q_norm_g,
                                      v_k_norm_g, v_w_branch_a, v_w_branch_b, v_w_out, v_norm2_g, v_w_up,
                                      v_conv_ffn_w, v_w_down))))
    me = 4 * lax.axis_index("x") + 2 * lax.axis_index("y") + lax.axis_index("c")
    ada_n = N_MOD * D // N_DEV

    small = jnp.concatenate([c.reshape(-1), W["conv_a_w"].reshape(-1), W["conv_ffn_w"].reshape(-1)])
    small_all, w_in_all = _gather_two_level([_pack_rows(small, SUBLANES), _pad_shard(W["w_in"])],
                                            name="gather_first")
    small_all = small_all.reshape(N_DEV, -1)
    c_all = small_all[:, :D]
    conv_all = _unpack_gathered(small_all[:, D:], CONVS)

    b_shard = lax.dynamic_slice(W["b_ada"], (0, me * ada_n), (1, ada_n))
    mod_part = _ada_fwd(c_all, W["w_ada"], b_shard)
    mod_all, = _exchange([mod_part], name="gather_mod", scatter=False)
    mod = lax.dynamic_index_in_dim(mod_all, me, axis=1, keepdims=False).reshape(1, N_MOD * D)

    wts = {"w_in": _assemble_columns(w_in_all, IN_W // N_DEV, IN_W_PAD, W_IN_SEGMENTS, name="assemble_w_in")}
    wts.update(conv_all)
    for name in ("norm1_g", "norm2_g", "q_norm_g", "k_norm_g", "b_f"):
        wts[name] = W[name]
    late = {name: _pad_shard(W[name]) if name == "w_up" else W[name].astype(BF16) for name, *_ in LATE}

    sq, grad_x, grads, parts = _local_step(x[0], loss_target[0], mod, wts, late)
    loss = lax.psum(sq[0, 0] * (0.5 / D), AXES)

    grads["b_ada"] = grads["mod"]
    rep_flat = lambda src: jnp.concatenate([src[name].reshape(-1) for name, _ in REPLICATED])
    rep_parts, = _exchange([_pack_rows(rep_flat(grads), 16)], name="gather_small_grads", scatter=False)
    rep_out = _adamw(rep_parts, *[_pack_rows(rep_flat(s), 16) for s in (W, M, V)], name="adamw_replicated")

    dmod_all = rep_parts.reshape(N_DEV, -1)[:, :N_MOD * D]
    dmod_mine = lax.dynamic_slice(dmod_all, (0, me * ada_n), (N_DEV, ada_n))
    g_ada = _ada_bwd(jnp.pad(c_all.T, ((0, 0), (0, LANES - N_DEV))),
                     jnp.pad(dmod_mine, ((0, LANES - N_DEV), (0, 0))))
    ada_out = _adamw(g_ada[None], W["w_ada"], M["w_ada"], V["w_ada"], name="adamw_ada")

    mat_out = {name: _adamw(parts[name], W[name], M[name], V[name], name="adamw_" + name) for name, *_ in MATRICES}
    conv_out = _adamw(parts["conv"], *[_pack_shards(s, CONVS, SUBLANES, F32) for s in (W, M, V)],
                      name="adamw_conv")

    results = []
    for kind in range(4):
        per = {"w_ada": ada_out[kind]}
        per.update({name: out[kind] for name, out in mat_out.items()})
        per.update(_unpack_shards(conv_out[kind], CONVS))
        flat, off = rep_out[kind].reshape(-1), 0
        for name, n in REPLICATED:
            per[name] = flat[off:off + n].reshape(1, n)
            off += n
        results.append(per)
    restore = lambda name, a: a[None] if W[name].ndim == 2 and name not in dict(REPLICATED) else a
    outs = [loss, grad_x[None]]
    for per in results:
        outs.extend(restore(name, per[name]) for name in names)
    return tuple(outs)
```

```python
import jax
import jax.numpy as jnp
import numpy as np
from jax import lax
from jax.experimental import pallas as pl
from jax.experimental.pallas import tpu as pltpu

F32 = jnp.float32
BF16 = jnp.bfloat16

N_DEV = 8
D = 1024
N_HEADS = 8
HEAD_DIM = 64
ATTN_W = 512
CONV_W = 512
D_FF = 2816
N_MOD = 6
IN_W = 5128
RMS_EPS = 1e-6
NEG_INF = -1e30
LOG2E = 1.4426950408889634

IN_W_PAD = 5376
COL_GA = 3072
COL_GB = 4096
COL_F = 5120
F_PAD = 128

ADAM_LR = 0.001
ADAM_B1 = 0.9
ADAM_B2 = 0.999
ADAM_EPS = 1e-08
ADAM_WD = 0.01
ADAM_STEP = 10

LANES = 128
SUBLANES = 8
BF16_ROWS = 16
VMEM_LIMIT = 52 * 1024 * 1024
TOKEN_TILE = 512
MATMUL_TILE = 1024
ATTN_BLOCK = 512
PACK_W = 1024

MESH = pl.DeviceIdType.MESH
AXES = ("x", "y", "c")


def _params(sem=None, **kw):
    return pltpu.CompilerParams(dimension_semantics=sem, vmem_limit_bytes=VMEM_LIMIT, **kw)


def _full(shape):
    nd = len(shape)
    return pl.BlockSpec(shape, lambda *_: (0,) * nd)


def _tn_dot(a, b):
    return lax.dot_general(a, b, (((0,), (0,)), ((), ())), preferred_element_type=F32)


def _matmul(a, b, *, name, tm, tn, tk, out_dtype=F32, trans_a=False, trans_b=False, exchange=None):
    assert not (trans_a and trans_b)
    if trans_a:
        K, M = a.shape
    else:
        M, K = a.shape
    N = b.shape[0] if trans_b else b.shape[1]
    assert b.shape[1 if trans_b else 0] == K and M % tm == 0 and N % tn == 0 and K % tk == 0, (name, a.shape, b.shape)
    nm, nn, nk = M // tm, N // tn, K // tk

    def body(*refs):
        if exchange is None:
            a_ref, b_ref, o_ref, *own = refs
        else:
            (a_ref, b_ref), (o_ref,), own, xrefs = exchange.split(refs, 2, 1)
            ids = [pl.program_id(d) for d in range(3)]
            first = jnp.logical_and(jnp.logical_and(ids[0] == 0, ids[1] == 0), ids[2] == 0)
            last = jnp.logical_and(jnp.logical_and(ids[0] == nn - 1, ids[1] == nm - 1), ids[2] == nk - 1)
            _ride(exchange, first, last, xrefs)
        k = pl.program_id(2)
        av = a_ref[...].astype(BF16)
        bv = b_ref[...].astype(BF16)
        if trans_a:
            prod = _tn_dot(av, bv)
        elif trans_b:
            prod = _nt_dot(av, bv)
        else:
            prod = jnp.dot(av, bv, preferred_element_type=F32)
        if nk == 1:
            o_ref[...] = prod.astype(out_dtype)
            return
        acc_ref, = own

        @pl.when(k == 0)
        def _():
            acc_ref[...] = prod

        @pl.when(k > 0)
        def _():
            acc_ref[...] += prod

        @pl.when(k == nk - 1)
        def _():
            o_ref[...] = acc_ref[...].astype(out_dtype)

    if trans_a:
        a_spec = pl.BlockSpec((tk, tm), lambda j, i, k: (k, i))
    else:
        a_spec = pl.BlockSpec((tm, tk), lambda j, i, k: (i, k))
    b_spec = pl.BlockSpec((tn, tk), lambda j, i, k: (j, k)) if trans_b else pl.BlockSpec((tk, tn), lambda j, i, k: (k, j))
    in_specs = [a_spec, b_spec]
    out_spec = pl.BlockSpec((tm, tn), lambda j, i, k: (i, j))
    out_shape = jax.ShapeDtypeStruct((M, N), out_dtype)
    scratch = [pltpu.VMEM((tm, tn), F32)] if nk > 1 else []
    if exchange is None:
        return pl.pallas_call(
            body, name=name, grid=(nn, nm, nk), in_specs=in_specs, out_specs=out_spec, out_shape=out_shape,
            scratch_shapes=scratch, compiler_params=_params(("parallel", "parallel", "arbitrary")),
        )(a, b)
    return pl.pallas_call(
        body, name=name, grid=(nn, nm, nk), in_specs=in_specs + exchange.in_specs,
        out_specs=[out_spec] + exchange.out_specs, out_shape=[out_shape] + exchange.out_shapes,
        scratch_shapes=scratch + exchange.scratch, compiler_params=_params(("arbitrary",) * 3),
    )(a, b, *exchange.xs)


def _norm_bwd_tile(dh, ins, outs, first):
    x_ref, dr_ref, g_ref, sc_ref = ins[:4]
    dx_ref, dsh_ref, dsc_ref, dg_ref = outs[:4]

    @pl.when(first)
    def _():
        for ref in outs[1:4] + outs[5:]:
            ref[...] = jnp.zeros_like(ref)

    xv = x_ref[...]
    gv = g_ref[...]
    one_sc = 1.0 + sc_ref[...]
    inv = lax.rsqrt(jnp.mean(xv * xv, axis=-1, keepdims=True) + RMS_EPS)
    xn = xv * inv
    dxn = dh * (gv * one_sc)
    dx = dr_ref[...] + inv * (dxn - xn * jnp.mean(dxn * xn, axis=-1, keepdims=True))
    dx_ref[...] = dx
    dhxn = dh * xn
    dsh_ref[...] += _rows8(dh)
    dsc_ref[...] += _rows8(dhxn * gv)
    dg_ref[...] += _rows8(dhxn * one_sc)
    if len(ins) == 6:
        mix_ref, g1_ref = ins[4:]
        dmix_ref, dg1_ref = outs[4:]
        dmix_ref[...] = (dx * g1_ref[...]).astype(BF16)
        dg1_ref[...] += _rows8(dx * mix_ref[...])


def _matmul_pieces(pieces, b, *, name, tm, exchange=None, norm_bwd=None):
    M = pieces[0].shape[0]
    widths = [p.shape[1] for p in pieces]
    offsets = [sum(widths[:i]) for i in range(len(widths))]
    N = b.shape[0]
    assert b.shape[1] >= sum(widths) and M % tm == 0, (name, widths, b.shape)
    n_p, nm = len(pieces), M // tm
    extra = list(norm_bwd) if norm_bwd is not None else []
    n_in = n_p + 1 + len(extra)
    n_out = len(extra) if norm_bwd is not None else 1

    def body(*refs):
        i = pl.program_id(0)
        if exchange is None:
            ins, outs = refs[:n_in], refs[n_in:]
        else:
            ins, outs, _, xrefs = exchange.split(refs, n_in, n_out)
            _ride(exchange, i == 0, i == nm - 1, xrefs)
        b_ref = ins[n_p]
        acc = None
        for a_ref, off, w in zip(ins[:n_p], offsets, widths):
            term = _nt_dot(a_ref[...].astype(BF16), b_ref[:, off:off + w])
            acc = term if acc is None else acc + term
        if norm_bwd is None:
            outs[0][...] = acc
        else:
            _norm_bwd_tile(acc, tuple(ins[n_p + 1:]), tuple(outs), first=i == 0)

    row, vec, part = pl.BlockSpec((tm, N), lambda i: (i, 0)), _full((1, N)), _full((SUBLANES, N))
    part_shape = jax.ShapeDtypeStruct((SUBLANES, N), F32)
    in_specs = [pl.BlockSpec((tm, w), lambda i: (i, 0)) for w in widths] + [_full(b.shape)]
    out_specs, out_shape = [row], [jax.ShapeDtypeStruct((M, N), F32)]
    if norm_bwd is not None:
        in_specs += [row, row, vec, vec] + ([row, vec] if len(extra) == 6 else [])
        out_specs += [part] * 3 + ([row, part] if len(extra) == 6 else [])
        out_shape += [part_shape] * 3 + ([jax.ShapeDtypeStruct((M, N), BF16), part_shape] if len(extra) == 6 else [])
    sequential = exchange is not None or norm_bwd is not None
    xs = exchange.xs if exchange is not None else []
    result = pl.pallas_call(
        body, name=name, grid=(nm,), in_specs=in_specs + (exchange.in_specs if exchange else []),
        out_specs=out_specs + (exchange.out_specs if exchange else []),
        out_shape=out_shape + (exchange.out_shapes if exchange else []),
        scratch_shapes=exchange.scratch if exchange else [],
        compiler_params=_params(("arbitrary" if sequential else "parallel",)),
    )(*pieces, b, *extra, *xs)
    return result[0] if len(result) == 1 else result


def _matmul_tn_pieces(a, pieces, *, name, tk):
    K, M = a.shape
    widths = [p.shape[1] for p in pieces]
    n_p, nk = len(pieces), K // tk

    def body(*refs):
        a_ref, p_refs, o_refs = refs[0], refs[1:n_p + 1], refs[n_p + 1:]
        k = pl.program_id(0)
        av = a_ref[...].astype(BF16)
        for p_ref, o_ref in zip(p_refs, o_refs):
            prod = _tn_dot(av, p_ref[...].astype(BF16))

            @pl.when(k == 0)
            def _():
                o_ref[...] = prod

            @pl.when(k > 0)
            def _():
                o_ref[...] += prod

    return pl.pallas_call(
        body, name=name, grid=(nk,),
        in_specs=[pl.BlockSpec((tk, M), lambda k: (k, 0))] + [pl.BlockSpec((tk, w), lambda k: (k, 0)) for w in widths],
        out_specs=[_full((M, w)) for w in widths], out_shape=[jax.ShapeDtypeStruct((M, w), F32) for w in widths],
        compiler_params=_params(("arbitrary",)),
    )(a, *pieces)


def _split_dot(x, mat, parts):
    out = None
    rem = x
    for p in range(parts):
        piece = rem.astype(BF16)
        term = jnp.dot(piece, mat, preferred_element_type=F32)
        out = term if out is None else out + term
        if p + 1 < parts:
            rem = rem - piece.astype(F32)
    return out


def _sigmoid(x):
    return 0.5 * jnp.tanh(0.5 * x) + 0.5


def _rows8(x):
    r, c = x.shape
    return jnp.sum(x.reshape(r // SUBLANES, SUBLANES, c), axis=0)


def _shift_down(blk, prev8, n):
    rolled = pltpu.roll(blk, n, axis=0)
    prev_rolled = pltpu.roll(prev8, n, axis=0)
    rows = lax.broadcasted_iota(jnp.int32, prev8.shape, 0)
    first = jnp.where(rows < n, prev_rolled, rolled[0:SUBLANES])
    return jnp.concatenate([first, rolled[SUBLANES:]], axis=0)


def _prev_spec(tm, width, col):
    per = tm // SUBLANES
    return pl.BlockSpec((SUBLANES, width), lambda i, *_: (jnp.maximum(i * per - 1, 0), col))


def _next_spec(tm, width, col, n_tiles):
    per = tm // SUBLANES
    last = n_tiles * per - 1
    return pl.BlockSpec((SUBLANES, width), lambda i, *_: (jnp.minimum((i + 1) * per, last), col))


def _group_matrix():
    idx = np.arange(ATTN_W) // HEAD_DIM
    return jnp.asarray((idx[:, None] == idx[None, :]).astype(np.float32), BF16)


def _norm_mod(x, g, sc, sh, *, name):
    T = x.shape[0]
    tm = min(TOKEN_TILE, T)

    def body(x_ref, g_ref, sc_ref, sh_ref, o_ref):
        xv = x_ref[...]
        inv = lax.rsqrt(jnp.mean(xv * xv, axis=-1, keepdims=True) + RMS_EPS)
        o_ref[...] = ((xv * inv) * g_ref[...] * (1.0 + sc_ref[...]) + sh_ref[...]).astype(BF16)

    row = pl.BlockSpec((tm, D), lambda i: (i, 0))
    return pl.pallas_call(
        body, name=name, grid=(T // tm,),
        in_specs=[row, _full((1, D)), _full((1, D)), _full((1, D))],
        out_specs=row, out_shape=jax.ShapeDtypeStruct((T, D), BF16),
        compiler_params=_params(("parallel",)),
    )(x, g, sc, sh)


LANE_ONE = 64
LANE_F = 67
LANE_LSE = 70
LANE_SUM = 73


def _pieces(x):
    hi = x.astype(BF16).astype(F32)
    rest = x - hi
    mid = rest.astype(BF16).astype(F32)
    return hi, mid, rest - mid


def _run(start, vals):
    return [(start + i, v) for i, v in enumerate(vals)]


def _head_lanes(a, h):
    blk = a[:, LANES * (h // 2):LANES * (h // 2) + LANES]
    return blk if h % 2 == 0 else pltpu.roll(blk, HEAD_DIM, axis=1)


def _branch_prep(proj, fcum, conv_w8, qg, kg, gmat):
    T = proj.shape[0]
    tm = min(TOKEN_TILE, T)
    nt = T // tm

    def body(cb_ref, cc_ref, cv_ref, q_ref, k_ref, v_ref, f_ref, ccp_ref, cvp_ref, w_ref, qg_ref, kg_ref, g_ref,
             ya_ref, qa_ref, ka_ref, va_ref):
        i = pl.program_id(0)
        z = cc_ref[...] * cv_ref[...]
        zp = jnp.where(i > 0, ccp_ref[...] * cvp_ref[...], 0.0)
        w = w_ref[...]
        cz = _shift_down(z, zp, 2) * w[0:1] + _shift_down(z, zp, 1) * w[1:2] + z * w[2:3]
        ya_ref[...] = (cb_ref[...] * cz).astype(BF16)
        gm = g_ref[...]

        def normed(src, gain, scale):
            v = src[...]
            ms = _split_dot(v * v, gm, 2) * (1.0 / HEAD_DIM)
            return (v * lax.rsqrt(ms + RMS_EPS)) * gain[...] * scale

        qn = normed(q_ref, qg_ref, LOG2E / np.sqrt(HEAD_DIM))
        kn = normed(k_ref, kg_ref, 1.0)
        vv = v_ref[...]
        lane = lax.broadcasted_iota(jnp.int32, (tm, LANES), 1)
        low = lane < HEAD_DIM
        in_run = lambda start: jnp.logical_and(lane >= start, lane < start + 3)
        q_ones = jnp.where(jnp.logical_or(in_run(LANE_ONE), lane == LANE_SUM), 1.0, 0.0)
        k_ones = jnp.where(jnp.logical_or(in_run(LANE_F), in_run(LANE_LSE)), 1.0, 0.0)
        v_ones = jnp.where(in_run(LANE_ONE), 1.0, 0.0)
        f3 = jnp.concatenate(_pieces(f_ref[...] * LOG2E), axis=1).astype(BF16)
        src = lax.broadcasted_iota(jnp.int32, (3 * LANES, LANES), 0)
        dst = lax.broadcasted_iota(jnp.int32, (3 * LANES, LANES), 1)
        for h in range(N_HEADS):
            def pick(start, h=h):
                hit = jnp.logical_and(src - h == (dst - start) * LANES, jnp.logical_and(dst >= start, dst < start + 3))
                return jnp.dot(f3, jnp.where(hit, 1.0, 0.0).astype(BF16), preferred_element_type=F32)

            qa_ref[h] = jnp.where(low, _head_lanes(qn, h), q_ones + pick(LANE_F)).astype(BF16)
            ka_ref[h] = jnp.where(low, _head_lanes(kn, h), k_ones - pick(LANE_ONE)).astype(BF16)
            va_ref[h] = jnp.where(low, _head_lanes(vv, h), v_ones).astype(BF16)

    blk = lambda col: pl.BlockSpec((tm, CONV_W), lambda i: (i, col))
    heads = pl.BlockSpec((N_HEADS, tm, LANES), lambda i: (0, i, 0))
    return pl.pallas_call(
        body, name="branch_prep", grid=(nt,),
        in_specs=[blk(0), blk(1), blk(2), blk(3), blk(4), blk(5), pl.BlockSpec((tm, F_PAD), lambda i: (i, 0)),
                  _prev_spec(tm, CONV_W, 1), _prev_spec(tm, CONV_W, 2),
                  _full((SUBLANES, CONV_W)), _full((1, ATTN_W)), _full((1, ATTN_W)), _full((ATTN_W, ATTN_W))],
        out_specs=[pl.BlockSpec((tm, CONV_W), lambda i: (i, 0)), heads, heads, heads],
        out_shape=[jax.ShapeDtypeStruct((T, CONV_W), BF16)] + [jax.ShapeDtypeStruct((N_HEADS, T, LANES), BF16)] * 3,
        compiler_params=_params(("parallel",)),
    )(proj, proj, proj, proj, proj, proj, fcum, proj, proj, conv_w8, qg, kg, gmat)


def _cumsum(x, *, reverse, name, col=0, gate_bias=None):
    T = x.shape[0]
    tm = min(TOKEN_TILE, T)
    nt = T // tm

    def body(x_ref, b_ref, o_ref, carry_ref):
        i = pl.program_id(0)

        @pl.when(i == 0)
        def _():
            carry_ref[...] = jnp.zeros_like(carry_ref)

        r = lax.broadcasted_iota(jnp.int32, (tm, tm), 0)
        c = lax.broadcasted_iota(jnp.int32, (tm, tm), 1)
        tri = jnp.where((c >= r) if reverse else (c <= r), 1.0, 0.0).astype(BF16)
        xv = x_ref[...]
        if gate_bias is not None:
            fx = xv + b_ref[...]
            xv = jnp.minimum(fx, 0.0) - jnp.log(1.0 + jnp.exp(-jnp.abs(fx)))
        out = _split_dot_left(tri, xv, 3) + carry_ref[0:1]
        o_ref[...] = out
        carry_ref[...] = jnp.broadcast_to(out[0:1] if reverse else out[tm - 1:tm], carry_ref.shape)

    rows = (lambda i: nt - 1 - i) if reverse else (lambda i: i)
    bias = jnp.zeros((1, F_PAD), F32) if gate_bias is None else gate_bias
    return pl.pallas_call(
        body, name=name, grid=(nt,),
        in_specs=[pl.BlockSpec((tm, F_PAD), lambda i: (rows(i), col)), _full((1, F_PAD))],
        out_specs=pl.BlockSpec((tm, F_PAD), lambda i: (rows(i), 0)),
        out_shape=jax.ShapeDtypeStruct((T, F_PAD), F32),
        scratch_shapes=[pltpu.VMEM((SUBLANES, F_PAD), F32)],
        compiler_params=_params(("arbitrary",)),
    )(x, bias)


def _split_dot_left(mat, x, parts):
    out = None
    rem = x
    for p in range(parts):
        piece = rem.astype(BF16)
        term = jnp.dot(mat, piece, preferred_element_type=F32)
        out = term if out is None else out + term
        if p + 1 < parts:
            rem = rem - piece.astype(F32)
    return out


def _out_resid_norm(x, merged, w_out, g1, g, sc, sh):
    T = x.shape[0]
    tm = min(TOKEN_TILE, T)

    def body(x_ref, m_ref, w_ref, g1_ref, g_ref, sc_ref, sh_ref, mix_ref, x1_ref, h_ref):
        mix = jnp.dot(m_ref[...], w_ref[...], preferred_element_type=F32)
        mix_ref[...] = mix
        x1 = x_ref[...] + g1_ref[...] * mix
        x1_ref[...] = x1
        inv = lax.rsqrt(jnp.mean(x1 * x1, axis=-1, keepdims=True) + RMS_EPS)
        h_ref[...] = ((x1 * inv) * g_ref[...] * (1.0 + sc_ref[...]) + sh_ref[...]).astype(BF16)

    row = pl.BlockSpec((tm, D), lambda i: (i, 0))
    vec = _full((1, D))
    return pl.pallas_call(
        body, name="out_resid_norm", grid=(T // tm,),
        in_specs=[row, row, _full((D, D)), vec, vec, vec, vec], out_specs=[row, row, row],
        out_shape=[jax.ShapeDtypeStruct((T, D), F32), jax.ShapeDtypeStruct((T, D), F32),
                   jax.ShapeDtypeStruct((T, D), BF16)],
        compiler_params=_params(("parallel",)),
    )(x, merged, w_out, g1, g, sc, sh)


FFN_TM = 256
FFN_TC = 1408


def _ffn_act_fwd(u, w8):
    T = u.shape[0]
    tm = min(FFN_TM, T)
    nt = T // tm
    nc = D_FF // FFN_TC

    def body(ug_ref, uv_ref, ugp_ref, uvp_ref, wg_ref, wv_ref, o_ref, cg_ref, cv_ref):
        i = pl.program_id(1)

        def conv(u_ref, p_ref, w_ref):
            uv = u_ref[...]
            up = jnp.where(i > 0, p_ref[...], 0.0)
            w = w_ref[...]
            return _shift_down(uv, up, 2) * w[0:1] + _shift_down(uv, up, 1) * w[1:2] + uv * w[2:3]

        gate = conv(ug_ref, ugp_ref, wg_ref)
        val = conv(uv_ref, uvp_ref, wv_ref)
        cg_ref[...] = gate.astype(BF16)
        cv_ref[...] = val.astype(BF16)
        o_ref[...] = (gate * _sigmoid(gate) * val).astype(BF16)

    per = tm // SUBLANES
    blk = lambda off: pl.BlockSpec((tm, FFN_TC), lambda j, i: (i, j + off))
    prev = lambda off: pl.BlockSpec((SUBLANES, FFN_TC), lambda j, i: (jnp.maximum(i * per - 1, 0), j + off))
    wblk = lambda off: pl.BlockSpec((SUBLANES, FFN_TC), lambda j, i: (0, j + off))
    return pl.pallas_call(
        body, name="ffn_act_fwd", grid=(nc, nt),
        in_specs=[blk(0), blk(nc), prev(0), prev(nc), wblk(0), wblk(nc)],
        out_specs=[blk(0), blk(0), blk(0)],
        out_shape=[jax.ShapeDtypeStruct((T, D_FF), BF16)] * 3,
        compiler_params=_params(("parallel", "parallel")),
    )(u, u, u, u, w8, w8)


def _down_loss_head(x1, act, w_down, g2, target):
    T = x1.shape[0]
    tm = min(TOKEN_TILE, T)

    def body(x1_ref, a_ref, w_ref, g2_ref, t_ref, dy_ref, dff_ref, loss_ref, dg2_ref):
        i = pl.program_id(0)

        @pl.when(i == 0)
        def _():
            loss_ref[...] = jnp.zeros_like(loss_ref)
            dg2_ref[...] = jnp.zeros_like(dg2_ref)

        ff = jnp.dot(a_ref[...], w_ref[...], preferred_element_type=F32)
        err = x1_ref[...] + g2_ref[...] * ff - t_ref[...]
        dy = err * (1.0 / D)
        dy_ref[...] = dy
        dff_ref[...] = (dy * g2_ref[...]).astype(BF16)
        loss_ref[...] += _rows8(err * err)
        dg2_ref[...] += _rows8(dy * ff)

    row = pl.BlockSpec((tm, D), lambda i: (i, 0))
    acc = _full((SUBLANES, D))
    return pl.pallas_call(
        body, name="down_loss_head", grid=(T // tm,),
        in_specs=[row, pl.BlockSpec((tm, D_FF), lambda i: (i, 0)), _full((D_FF, D)), _full((1, D)), row],
        out_specs=[row, row, acc, acc],
        out_shape=[jax.ShapeDtypeStruct((T, D), F32), jax.ShapeDtypeStruct((T, D), BF16),
                   jax.ShapeDtypeStruct((SUBLANES, D), F32), jax.ShapeDtypeStruct((SUBLANES, D), F32)],
        compiler_params=_params(("arbitrary",)),
    )(x1, act, w_down, g2, target)


def _nt_dot(a, b):
    return lax.dot_general(a, b, (((1,), (1,)), ((), ())), preferred_element_type=F32)


def _causal(n, keys_on_rows=False):
    r = lax.broadcasted_iota(jnp.int32, (n, n), 0)
    c = lax.broadcasted_iota(jnp.int32, (n, n), 1)
    return (c >= r) if keys_on_rows else (c <= r)


def _sweep(lo, hi, step, carry, group=2):
    while group >= 1:
        def several(j, cr, lo=lo, group=group):
            for g in range(group):
                cr = step(lo + group * j + g, cr)
            return cr

        passes = (hi - lo) // group
        carry = lax.fori_loop(0, passes, several, carry)
        lo = lo + group * passes
        group //= 2
    return carry


def _grid_ends(n0, n1):
    i0, i1 = pl.program_id(0), pl.program_id(1)
    return jnp.logical_and(i0 == 0, i1 == 0), jnp.logical_and(i0 == n0 - 1, i1 == n1 - 1)


def _attn_fwd(qa, ka, va, exchange=None):
    nh, T, _ = qa.shape
    bq = min(ATTN_BLOCK, T)
    nq = T // bq

    def body(*refs):
        if exchange is None:
            q_ref, k_ref, v_ref, o_ref, qb_ref = refs
        else:
            (q_ref, k_ref, v_ref), (o_ref, qb_ref), _, xrefs = exchange.split(refs, 3, 2)
            _ride(exchange, *_grid_ends(nh, nq), xrefs)
        qi = pl.program_id(1)
        q = q_ref[0]

        def step(kb, carry, masked=False):
            m, acc = carry
            start = pl.multiple_of(kb * bq, bq)
            s = _nt_dot(q, k_ref[0, pl.ds(start, bq), :])
            if masked:
                s = jnp.where(_causal(bq), s, NEG_INF)
            m_new = jnp.maximum(m, jnp.max(s, axis=-1, keepdims=True))
            p = jnp.exp2(s - m_new).astype(BF16)
            acc = jnp.exp2(m - m_new) * acc + jnp.dot(p, v_ref[0, pl.ds(start, bq), :], preferred_element_type=F32)
            return m_new, acc

        init = (jnp.full((bq, 1), NEG_INF, F32), jnp.zeros((bq, LANES), F32))
        m, acc = step(qi, _sweep(0, qi, step, init, group=4), masked=True)
        l = acc[:, LANE_ONE:LANE_ONE + 1]
        o_ref[0] = acc / l
        lane = lax.broadcasted_iota(jnp.int32, (bq, LANES), 1)
        qf = q.astype(F32)
        for idx, piece in _run(LANE_LSE, _pieces(m + jnp.log2(l))):
            qf = jnp.where(lane == idx, -piece, qf)
        qb_ref[0] = qf.astype(BF16)

    tile = pl.BlockSpec((1, bq, LANES), lambda h, i: (h, i, 0))
    whole = pl.BlockSpec((1, T, LANES), lambda h, i: (h, 0, 0))
    out_shape = [jax.ShapeDtypeStruct((nh, T, LANES), F32), jax.ShapeDtypeStruct((nh, T, LANES), BF16)]
    if exchange is None:
        return pl.pallas_call(
            body, name="attn_fwd", grid=(nh, nq), in_specs=[tile, whole, whole], out_specs=[tile, tile],
            out_shape=out_shape, compiler_params=_params(("parallel", "parallel")),
        )(qa, ka, va)
    return pl.pallas_call(
        body, name="attn_fwd", grid=(nh, nq), in_specs=[tile, whole, whole] + exchange.in_specs,
        out_specs=[tile, tile] + exchange.out_specs, out_shape=out_shape + exchange.out_shapes,
        scratch_shapes=exchange.scratch, compiler_params=_params(("arbitrary", "arbitrary")),
    )(qa, ka, va, *exchange.xs)


def _branch_merge_fwd(ya0, o_h, proj, wba, wbb_heads):
    nh, T, _ = o_h.shape
    tm = min(TOKEN_TILE, T)

    def body(ya0_ref, o_ref, ga_ref, gb_ref, wa_ref, wb_ref, ya_ref, yb_ref, m_ref):
        ya = jnp.dot(ya0_ref[...], wa_ref[...], preferred_element_type=F32)
        yb = jnp.dot(o_ref[0].astype(BF16), wb_ref[0:LANES, :], preferred_element_type=F32)
        for h in range(1, nh):
            yb += jnp.dot(o_ref[h].astype(BF16), wb_ref[h * LANES:(h + 1) * LANES, :], preferred_element_type=F32)
        ya_ref[...] = ya.astype(BF16)
        yb_ref[...] = yb.astype(BF16)
        m_ref[...] = (_sigmoid(ga_ref[...]) * ya + _sigmoid(gb_ref[...]) * yb).astype(BF16)

    row = pl.BlockSpec((tm, D), lambda i: (i, 0))
    return pl.pallas_call(
        body, name="branch_merge_fwd", grid=(T // tm,),
        in_specs=[pl.BlockSpec((tm, CONV_W), lambda i: (i, 0)), pl.BlockSpec((nh, tm, LANES), lambda i: (0, i, 0)),
                  pl.BlockSpec((tm, D), lambda i: (i, COL_GA // D)), pl.BlockSpec((tm, D), lambda i: (i, COL_GB // D)),
                  _full((CONV_W, D)), _full((nh * LANES, D))],
        out_specs=[row, row, row],
        out_shape=[jax.ShapeDtypeStruct((T, D), BF16)] * 3,
        compiler_params=_params(("parallel",)),
    )(ya0, o_h, proj, proj, wba, wbb_heads)


def _branch_b_bwd(dyb, o_h, wbb_heads):
    nh, T, _ = o_h.shape
    tm = min(TOKEN_TILE, T)

    def body(dyb_ref, o_ref, w_ref, out_ref):
        do = _nt_dot(dyb_ref[...], w_ref[...])
        lane = lax.broadcasted_iota(jnp.int32, (tm, LANES), 1)
        for h in range(nh):
            g = do[:, h * LANES:(h + 1) * LANES].astype(BF16).astype(F32)
            delta = jnp.sum(g * o_ref[h], axis=-1, keepdims=True)
            for idx, piece in _run(LANE_ONE, _pieces(delta)):
                g = jnp.where(lane == idx, -piece, g)
            out_ref[h] = g.astype(BF16)

    heads = pl.BlockSpec((nh, tm, LANES), lambda i: (0, i, 0))
    return pl.pallas_call(
        body, name="branch_b_bwd", grid=(T // tm,),
        in_specs=[pl.BlockSpec((tm, D), lambda i: (i, 0)), heads, _full((D, nh * LANES))],
        out_specs=heads, out_shape=jax.ShapeDtypeStruct((nh, T, LANES), BF16),
        compiler_params=_params(("parallel",)),
    )(dyb, o_h, wbb_heads)


def _branch_b_dw(o_h, dyb):
    nh, T, _ = o_h.shape
    tk = min(2 * MATMUL_TILE, T)

    def body(o_ref, dyb_ref, out_ref):
        @pl.when(pl.program_id(0) == 0)
        def _():
            out_ref[...] = jnp.zeros_like(out_ref)

        g = dyb_ref[...]
        for h in range(nh):
            out_ref[h] += _tn_dot(o_ref[h].astype(BF16), g)

    return pl.pallas_call(
        body, name="branch_b_dw", grid=(T // tk,),
        in_specs=[pl.BlockSpec((nh, tk, LANES), lambda k: (0, k, 0)), pl.BlockSpec((tk, D), lambda k: (k, 0))],
        out_specs=_full((nh, LANES, D)), out_shape=jax.ShapeDtypeStruct((nh, LANES, D), F32),
        compiler_params=_params(("arbitrary",)),
    )(o_h, dyb)


def _attn_bwd(qb, ka, va, doa, exchange=None):
    nh, T, _ = qb.shape
    bk = min(ATTN_BLOCK, T)
    nk = T // bk

    def body(*refs):
        if exchange is None:
            q_ref, do_ref, k_ref, v_ref, dq_ref, dk_ref, dv_ref = refs
        else:
            (q_ref, do_ref, k_ref, v_ref), (dq_ref, dk_ref, dv_ref), _, xrefs = exchange.split(refs, 4, 3)
            _ride(exchange, *_grid_ends(nh, nk), xrefs)
        ki = pl.program_id(1)

        @pl.when(ki == 0)
        def _():
            dq_ref[...] = jnp.zeros_like(dq_ref)

        k = k_ref[0]
        v = v_ref[0]

        def step(qi, carry, masked):
            dk, dv = carry
            rows = pl.ds(pl.multiple_of(qi * bk, bk), bk)
            q = q_ref[0, rows, :]
            g = do_ref[0, rows, :]
            pt = jnp.exp2(_nt_dot(k, q))
            if masked:
                pt = jnp.where(_causal(bk, keys_on_rows=True), pt, 0.0)
            dv = dv + jnp.dot(pt.astype(BF16), g, preferred_element_type=F32)
            dst = (pt * _nt_dot(v, g)).astype(BF16)
            dk = dk + jnp.dot(dst, q, preferred_element_type=F32)
            dq_ref[0, rows, :] += _tn_dot(dst, k)
            return dk, dv

        init = (jnp.zeros((bk, LANES), F32), jnp.zeros((bk, LANES), F32))
        carry = step(ki, init, True)
        dk_ref[0], dv_ref[0] = _sweep(ki + 1, nk, lambda qi, cr: step(qi, cr, False), carry)

    tile = pl.BlockSpec((1, bk, LANES), lambda h, i: (h, i, 0))
    whole = pl.BlockSpec((1, T, LANES), lambda h, i: (h, 0, 0))
    out_shape = [jax.ShapeDtypeStruct((nh, T, LANES), F32)] * 3
    if exchange is None:
        return pl.pallas_call(
            body, name="attn_bwd", grid=(nh, nk), in_specs=[whole, whole, tile, tile],
            out_specs=[whole, tile, tile], out_shape=out_shape, compiler_params=_params(("parallel", "arbitrary")),
        )(qb, doa, ka, va)
    return pl.pallas_call(
        body, name="attn_bwd", grid=(nh, nk), in_specs=[whole, whole, tile, tile] + exchange.in_specs,
        out_specs=[whole, tile, tile] + exchange.out_specs, out_shape=out_shape + exchange.out_shapes,
        scratch_shapes=exchange.scratch, compiler_params=_params(("arbitrary", "arbitrary")),
    )(qb, doa, ka, va, *exchange.xs)


def _attn_unpack(dq_h, dk_h, dv_h):
    nh, T, _ = dq_h.shape
    tm = min(TOKEN_TILE, T)

    def body(dq_ref, dk_ref, dv_ref, q_out, k_out, v_out, f_out):
        lane = lax.broadcasted_iota(jnp.int32, (tm, LANES), 1)
        low = lane < HEAD_DIM
        for src, dst in ((dq_ref, q_out), (dk_ref, k_out), (dv_ref, v_out)):
            for pair in range(nh // 2):
                both = jnp.where(low, src[2 * pair], pltpu.roll(src[2 * pair + 1], HEAD_DIM, axis=1))
                dst[:, LANES * pair:LANES * (pair + 1)] = both.astype(dst.dtype)
        df = jnp.zeros((tm, LANES), F32)
        for h in range(nh):
            col = dq_ref[h][:, LANE_F:LANE_F + 1] - dk_ref[h][:, LANE_SUM:LANE_SUM + 1]
            df = jnp.where(lane == h, col, df)
        f_out[...] = df

    heads = pl.BlockSpec((nh, tm, LANES), lambda i: (0, i, 0))
    tok = pl.BlockSpec((tm, ATTN_W), lambda i: (i, 0))
    return pl.pallas_call(
        body, name="attn_unpack", grid=(T // tm,), in_specs=[heads, heads, heads],
        out_specs=[tok, tok, tok, pl.BlockSpec((tm, F_PAD), lambda i: (i, 0))],
        out_shape=[jax.ShapeDtypeStruct((T, ATTN_W), BF16)] * 3 + [jax.ShapeDtypeStruct((T, F_PAD), F32)],
        compiler_params=_params(("parallel",)),
    )(dq_h, dk_h, dv_h)


def _ffn_act_bwd(u, cg, cv, da, w8):
    T = u.shape[0]
    tm = min(FFN_TM, T)
    nt = T // tm
    nc = D_FF // FFN_TC

    def body(ug_ref, uv_ref, cg_ref, cv_ref, cgn_ref, cvn_ref, da_ref, dan_ref, wg_ref, wv_ref,
             dug_ref, duv_ref, dwg_ref, dwv_ref):
        i = pl.program_id(1)

        @pl.when(i == 0)
        def _():
            dwg_ref[...] = jnp.zeros_like(dwg_ref)
            dwv_ref[...] = jnp.zeros_like(dwv_ref)

        gate = jnp.concatenate([cg_ref[...], cgn_ref[...]], axis=0).astype(F32)
        val = jnp.concatenate([cv_ref[...], cvn_ref[...]], axis=0).astype(F32)
        dae = jnp.concatenate([da_ref[...], dan_ref[...]], axis=0).astype(F32)
        rows_e = lax.broadcasted_iota(jnp.int32, dae.shape, 0)
        dae = jnp.where(jnp.logical_and(i == nt - 1, rows_e >= tm), 0.0, dae)
        sg = _sigmoid(gate)
        n = tm + BF16_ROWS

        def back(d, u_ref, w_ref, du_ref, dw_ref):
            w = w_ref[...]
            uv = u_ref[...]
            d1 = pltpu.roll(d, n - 1, axis=0)[:tm]
            d2 = pltpu.roll(d, n - 2, axis=0)[:tm]
            d0 = d[:tm]
            du_ref[...] = (d0 * w[2:3] + d1 * w[1:2] + d2 * w[0:1]).astype(BF16)
            rows = [jnp.sum(t * uv, axis=0, keepdims=True) for t in (d2, d1, d0)]
            dw_ref[...] += jnp.concatenate(rows + [jnp.zeros((SUBLANES - 3, FFN_TC), F32)], axis=0)

        back(dae * val * sg * (1.0 + gate * (1.0 - sg)), ug_ref, wg_ref, dug_ref, dwg_ref)
        back(dae * gate * sg, uv_ref, wv_ref, duv_ref, dwv_ref)

    per = tm // BF16_ROWS
    last_blk = nt * per - 1
    blk = lambda off: pl.BlockSpec((tm, FFN_TC), lambda j, i: (i, j + off))
    nxt = pl.BlockSpec((BF16_ROWS, FFN_TC), lambda j, i: (jnp.minimum((i + 1) * per, last_blk), j))
    wblk = lambda off: pl.BlockSpec((SUBLANES, FFN_TC), lambda j, i: (0, j + off))
    dug, duv, dwg, dwv = pl.pallas_call(
        body, name="ffn_act_bwd", grid=(nc, nt),
        in_specs=[blk(0), blk(nc), blk(0), blk(0), nxt, nxt, blk(0), nxt, wblk(0), wblk(nc)],
        out_specs=[blk(0), blk(0), wblk(0), wblk(0)],
        out_shape=[jax.ShapeDtypeStruct((T, D_FF), BF16)] * 2 + [jax.ShapeDtypeStruct((SUBLANES, D_FF), F32)] * 2,
        compiler_params=_params(("parallel", "arbitrary")),
    )(u, u, cg, cv, cg, cv, da, da, w8, w8)
    return dug, duv, jnp.concatenate([dwg, dwv], axis=1)


def _out_merge_bwd(dmix, w_out, ya, yb, proj):
    T = ya.shape[0]
    tm = min(TOKEN_TILE, T)

    def body(dmix_ref, w_ref, ya_ref, yb_ref, ga_ref, gb_ref, dya_ref, dyb_ref, dga_ref, dgb_ref):
        dm = _nt_dot(dmix_ref[...], w_ref[...])
        sa = _sigmoid(ga_ref[...])
        sb = _sigmoid(gb_ref[...])
        dya_ref[...] = (dm * sa).astype(BF16)
        dyb_ref[...] = (dm * sb).astype(BF16)
        dga_ref[...] = (dm * ya_ref[...].astype(F32) * sa * (1.0 - sa)).astype(BF16)
        dgb_ref[...] = (dm * yb_ref[...].astype(F32) * sb * (1.0 - sb)).astype(BF16)

    row = pl.BlockSpec((tm, D), lambda i: (i, 0))
    return pl.pallas_call(
        body, name="out_merge_bwd", grid=(T // tm,),
        in_specs=[row, _full((D, D)), row, row, pl.BlockSpec((tm, D), lambda i: (i, COL_GA // D)),
                  pl.BlockSpec((tm, D), lambda i: (i, COL_GB // D))],
        out_specs=[row] * 4, out_shape=[jax.ShapeDtypeStruct((T, D), BF16)] * 4,
        compiler_params=_params(("parallel",)),
    )(dmix, w_out, ya, yb, proj, proj)


def _conv_branch_bwd(proj, dya0, conv_w8):
    T = proj.shape[0]
    tm = min(FFN_TM, T)
    nt = T // tm

    def body(cb_ref, cc_ref, cv_ref, cbn_ref, ccp_ref, cvp_ref, ccn_ref, cvn_ref, d_ref, dn_ref, w_ref,
             d3_ref, dw_ref):
        i = pl.program_id(0)

        @pl.when(i == 0)
        def _():
            dw_ref[...] = jnp.zeros_like(dw_ref)

        first, last = i == 0, i == nt - 1
        w = w_ref[...]
        cc = jnp.concatenate([ccp_ref[...], cc_ref[...], ccn_ref[...]], axis=0)
        cv = jnp.concatenate([cvp_ref[...], cv_ref[...], cvn_ref[...]], axis=0)
        rows = lax.broadcasted_iota(jnp.int32, cc.shape, 0)
        z = jnp.where(jnp.logical_and(first, rows < SUBLANES), 0.0, cc * cv)
        z1 = pltpu.roll(z, 1, axis=0)
        z2 = pltpu.roll(z, 2, axis=0)
        cz = z2 * w[0:1] + z1 * w[1:2] + z * w[2:3]
        zeros8 = jnp.zeros((SUBLANES, CONV_W), F32)
        de = jnp.concatenate([zeros8, d_ref[...], jnp.where(last, 0.0, dn_ref[...])], axis=0)
        cbe = jnp.concatenate([zeros8, cb_ref[...], cbn_ref[...]], axis=0)
        dcz = de * cbe
        n = tm + 2 * SUBLANES
        dz = dcz * w[2:3] + pltpu.roll(dcz, n - 1, axis=0) * w[1:2] + pltpu.roll(dcz, n - 2, axis=0) * w[0:1]
        inner = slice(SUBLANES, SUBLANES + tm)
        d3_ref[:, 0:CONV_W] = (de * cz)[inner].astype(BF16)
        d3_ref[:, CONV_W:2 * CONV_W] = (dz * cv)[inner].astype(BF16)
        d3_ref[:, 2 * CONV_W:3 * CONV_W] = (dz * cc)[inner].astype(BF16)
        wrows = [jnp.sum((dcz * t)[inner], axis=0, keepdims=True) for t in (z2, z1, z)]
        dw_ref[...] += jnp.concatenate(wrows + [jnp.zeros((SUBLANES - 3, CONV_W), F32)], axis=0)

    blk = lambda col: pl.BlockSpec((tm, CONV_W), lambda i: (i, col))
    out_blk = pl.BlockSpec((tm, CONV_W), lambda i: (i, 0))
    return pl.pallas_call(
        body, name="conv_branch_bwd", grid=(nt,),
        in_specs=[blk(0), blk(1), blk(2), _next_spec(tm, CONV_W, 0, nt),
                  _prev_spec(tm, CONV_W, 1), _prev_spec(tm, CONV_W, 2),
                  _next_spec(tm, CONV_W, 1, nt), _next_spec(tm, CONV_W, 2, nt),
                  out_blk, _next_spec(tm, CONV_W, 0, nt), _full((SUBLANES, CONV_W))],
        out_specs=[pl.BlockSpec((tm, 3 * CONV_W), lambda i: (i, 0)), _full((SUBLANES, CONV_W))],
        out_shape=[jax.ShapeDtypeStruct((T, 3 * CONV_W), BF16), jax.ShapeDtypeStruct((SUBLANES, CONV_W), F32)],
        compiler_params=_params(("arbitrary",)),
    )(proj, proj, proj, proj, proj, proj, proj, proj, dya0, dya0, conv_w8)


def _qk_norm_bwd(proj, dqs, dkh, dlogf, qg, kg, bf_pad, gmat):
    T = proj.shape[0]
    tm = min(TOKEN_TILE, T)

    def body(q_ref, k_ref, f_ref, dqs_ref, dkh_ref, dlf_ref, qg_ref, kg_ref, bf_ref, g_ref,
             dqk_ref, dfl_ref, dqg_ref, dkg_ref, dbf_ref):
        @pl.when(pl.program_id(0) == 0)
        def _():
            dqg_ref[...] = jnp.zeros_like(dqg_ref)
            dkg_ref[...] = jnp.zeros_like(dkg_ref)
            dbf_ref[...] = jnp.zeros_like(dbf_ref)

        gm = g_ref[...]
        for src, d_src, gain, scale, dst, dgain in (
                (q_ref, dqs_ref, qg_ref, 1.0 / np.sqrt(HEAD_DIM), dqk_ref.at[:, 0:ATTN_W], dqg_ref),
                (k_ref, dkh_ref, kg_ref, 1.0 / LOG2E, dqk_ref.at[:, ATTN_W:2 * ATTN_W], dkg_ref)):
            v = src[...]
            dhat = d_src[...].astype(F32) * scale
            inv = lax.rsqrt(_split_dot(v * v, gm, 2) * (1.0 / HEAD_DIM) + RMS_EPS)
            vn = v * inv
            dgain[...] += _rows8(dhat * vn)
            dvn = dhat * gain[...]
            mean = _split_dot(dvn * vn, gm, 2) * (1.0 / HEAD_DIM)
            dst[...] = (inv * (dvn - vn * mean)).astype(BF16)
        fx = f_ref[...] + bf_ref[...]
        dfl = dlf_ref[...] * _sigmoid(-fx)
        dfl_ref[...] = dfl.astype(BF16)
        dbf_ref[...] += _rows8(dfl)

    blk = lambda col: pl.BlockSpec((tm, ATTN_W), lambda i: (i, col))
    out_blk = pl.BlockSpec((tm, ATTN_W), lambda i: (i, 0))
    f_in = pl.BlockSpec((tm, F_PAD), lambda i: (i, COL_F // F_PAD))
    f_blk = pl.BlockSpec((tm, F_PAD), lambda i: (i, 0))
    return pl.pallas_call(
        body, name="qk_norm_bwd", grid=(T // tm,),
        in_specs=[blk(3), blk(4), f_in, out_blk, out_blk, f_blk, _full((1, ATTN_W)), _full((1, ATTN_W)),
                  _full((1, F_PAD)), _full((ATTN_W, ATTN_W))],
        out_specs=[pl.BlockSpec((tm, 2 * ATTN_W), lambda i: (i, 0)), f_blk, _full((SUBLANES, ATTN_W)),
                   _full((SUBLANES, ATTN_W)), _full((SUBLANES, F_PAD))],
        out_shape=[jax.ShapeDtypeStruct((T, 2 * ATTN_W), BF16), jax.ShapeDtypeStruct((T, F_PAD), BF16)]
        + [jax.ShapeDtypeStruct((SUBLANES, ATTN_W), F32)] * 2 + [jax.ShapeDtypeStruct((SUBLANES, F_PAD), F32)],
        compiler_params=_params(("arbitrary",)),
    )(proj, proj, proj, dqs, dkh, dlogf, qg, kg, bf_pad, gmat)


def _pad_rows8(w):
    return jnp.pad(w, ((0, SUBLANES - w.shape[0]), (0, 0)))


def _fold8(acc):
    return jnp.sum(acc, axis=0, keepdims=True)


def _late_weights(mats):
    out = {name: mats[name] for name in ("w_branch_a", "w_out", "w_up", "w_down")}
    out["w_branch_b_heads"] = _pad_head_rows(mats["w_branch_b"])
    return out


def _local_step(x, target, mod, wts, late=None):
    T = x.shape[0]
    tb = min(MATMUL_TILE, T)
    tk_long = min(2 * MATMUL_TILE, T)
    tm = min(TOKEN_TILE, T)
    sh1, sc1, g1, sh2, sc2, g2 = [mod[:, i * D:(i + 1) * D] for i in range(N_MOD)]
    w_in = wts["w_in"]
    conv_a8 = _pad_rows8(wts["conv_a_w"])
    conv_f8 = _pad_rows8(wts["conv_ffn_w"])
    qg = jnp.tile(wts["q_norm_g"], (1, N_HEADS))
    kg = jnp.tile(wts["k_norm_g"], (1, N_HEADS))
    bf_pad = jnp.pad(wts["b_f"], ((0, 0), (0, F_PAD - N_HEADS)))
    gmat = _group_matrix()

    h = _norm_mod(x, wts["norm1_g"], sc1, sh1, name="norm1_fwd")
    proj = _matmul(h, w_in, name="mm_in", tm=tb, tn=2688, tk=D)
    fcum = _cumsum(proj, reverse=False, name="gate_cumsum", col=COL_F // F_PAD, gate_bias=bf_pad)
    ya0, qa, ka, va = _branch_prep(proj, fcum, conv_a8, qg, kg, gmat)
    if late is None:
        o_h, qb = _attn_fwd(qa, ka, va)
    else:
        o_h, qb, *gathered = _attn_fwd(qa, ka, va, _Exchange([late[name] for name, *_ in LATE], scatter=False))
        wts = dict(wts)
        mats = {name: _join_shards(g, axis) for (name, _, _, axis), g in zip(LATE, gathered) if name != "w_up"}
        mats["w_up"] = _assemble_columns(gathered[[name for name, *_ in LATE].index("w_up")], 2 * D_FF // N_DEV,
                                         2 * D_FF, ((0, 2 * D_FF, 0),), name="assemble_w_up")
        wts.update(_late_weights(mats))
    ya, yb, merged = _branch_merge_fwd(ya0, o_h, proj, wts["w_branch_a"], wts["w_branch_b_heads"])
    mix, x1, h2 = _out_resid_norm(x, merged, wts["w_out"], g1, wts["norm2_g"], sc2, sh2)
    u = _matmul(h2, wts["w_up"], name="mm_up", tm=tb, tn=2816, tk=D)
    act, conv_gate, conv_val = _ffn_act_fwd(u, conv_f8)
    dy, dff, sq8, dg2_8 = _down_loss_head(x1, act, wts["w_down"], g2, target)
    sq = jnp.sum(sq8).reshape(1, 1)

    grads = {}
    da = _matmul(dff, wts["w_down"], name="mm_down_dx", tm=tb, tn=D_FF, tk=D, out_dtype=BF16, trans_b=True)
    grads["w_down"] = _matmul(act, dff, name="mm_down_dw", tm=1408, tn=D, tk=tk_long, trans_a=True)
    dug, duv, dconv_f8 = _ffn_act_bwd(u, conv_gate, conv_val, da, conv_f8)
    grads["conv_ffn_w"] = dconv_f8[:3]
    dx1, dsh2_8, dsc2_8, dn2_8, dmix, dg1_8 = _matmul_pieces(
        [dug, duv], wts["w_up"], name="mm_up_dx", tm=tm, norm_bwd=(x1, dy, wts["norm2_g"], sc2, mix, g1))
    dw_up = [_matmul(h2, d, name="mm_up_dw_" + half, tm=D, tn=1408, tk=tk_long, trans_a=True)
             for half, d in (("gate", dug), ("val", duv))]
    if late is None:
        grads["w_up"] = jnp.concatenate(dw_up, axis=1)
    grads["norm2_g"] = _fold8(dn2_8)

    grads["w_out"] = _matmul(merged, dmix, name="mm_out_dw", tm=D, tn=D, tk=tk_long, trans_a=True)
    dya, dyb, dga, dgb = _out_merge_bwd(dmix, wts["w_out"], ya, yb, proj)
    dya0 = _matmul(dya, wts["w_branch_a"], name="mm_branch_a_dx", tm=tb, tn=CONV_W, tk=D, trans_b=True)
    grads["w_branch_a"] = _matmul(ya0, dya, name="mm_branch_a_dw", tm=CONV_W, tn=D, tk=tk_long, trans_a=True)
    doa = _branch_b_bwd(dyb, o_h, wts["w_branch_b_heads"])
    grads["w_branch_b"] = _branch_b_dw(o_h, dyb)[:, :HEAD_DIM].reshape(ATTN_W, D)
    dconv3, dconv_a8 = _conv_branch_bwd(proj, dya0, conv_a8)
    grads["conv_a_w"] = dconv_a8[:3]

    parts = {}
    if late is None:
        dq_h, dk_h, dv_h = _attn_bwd(qb, ka, va, doa)
    else:
        ready = [(_column_shards(dw_up) if name == "w_up" else _split_shards(grads[name], axis)).astype(BF16)
                 for name, _, _, axis in LATE]
        dq_h, dk_h, dv_h, *recv = _attn_bwd(
            qb, ka, va, doa, _Exchange(ready + [_pack_full_by_dest(grads, CONVS, SUBLANES)], scatter=True))
        parts = dict(zip([name for name, *_ in LATE] + ["conv"], recv))
    dq_tok, dk_tok, dv_tok, dfcum = _attn_unpack(dq_h, dk_h, dv_h)
    dlogf = _cumsum(dfcum, reverse=True, name="gate_cumsum_bwd")
    dqk, dfl, dqg8, dkg8, dbf8 = _qk_norm_bwd(proj, dq_tok, dk_tok, dlogf, qg, kg, bf_pad, gmat)
    grads["q_norm_g"] = jnp.sum(_fold8(dqg8).reshape(N_HEADS, HEAD_DIM), axis=0, keepdims=True)
    grads["k_norm_g"] = jnp.sum(_fold8(dkg8).reshape(N_HEADS, HEAD_DIM), axis=0, keepdims=True)
    grads["b_f"] = _fold8(dbf8)[:, :N_HEADS]
    narrow, wide = [dconv3, dqk, dv_tok], [dga, dgb, dfl]
    dw_narrow = _matmul_tn_pieces(h, narrow, name="mm_in_dw_narrow", tk=tb)
    dwa, dwb, dwf = _matmul_tn_pieces(h, wide, name="mm_in_dw_wide", tk=tk_long)
    dw_in = list(dw_narrow) + [dwf[:, :N_HEADS], dwa, dwb]
    norm1 = (x, dx1, wts["norm1_g"], sc1)
    if late is None:
        grads["w_in"] = jnp.concatenate(dw_in, axis=1)
        grad_x, dsh1_8, dsc1_8, dn1_8 = _matmul_pieces(narrow + wide, w_in, name="mm_in_dx", tm=tm,
                                                       norm_bwd=norm1)
    else:
        grad_x, dsh1_8, dsc1_8, dn1_8, parts["w_in"] = _matmul_pieces(
            narrow + wide, w_in, name="mm_in_dx", tm=tm, norm_bwd=norm1,
            exchange=_Exchange([_column_shards(dw_in).astype(BF16)], scatter=True))
    grads["norm1_g"] = _fold8(dn1_8)
    grads["mod"] = jnp.concatenate([_fold8(a) for a in (dsh1_8, dsc1_8, dg1_8, dsh2_8, dsc2_8, dg2_8)], axis=1)
    return sq, grad_x, grads, parts


def _me_and_peers():
    mx, my, mc = lax.axis_index("x"), lax.axis_index("y"), lax.axis_index("c")
    me = 4 * mx + 2 * my + mc
    peers = []
    for k in range(1, N_DEV):
        px = 1 - mx if k & 4 else mx
        py = 1 - my if k & 2 else my
        pc = 1 - mc if k & 1 else mc
        peers.append(((px, py, pc), 4 * px + 2 * py + pc))
    return me, peers


HBM_SPEC = pl.BlockSpec(memory_space=pltpu.HBM)


class _Exchange:
    def __init__(self, xs, scatter):
        self.xs, self.scatter, self.n = list(xs), scatter, len(xs)
        self.out_shapes = [jax.ShapeDtypeStruct(x.shape if scatter else (N_DEV,) + x.shape, x.dtype) for x in xs]
        self.in_specs = [HBM_SPEC] * self.n
        self.out_specs = [HBM_SPEC] * self.n
        self.scratch = [pltpu.SemaphoreType.DMA((self.n, N_DEV - 1)), pltpu.SemaphoreType.DMA((self.n, N_DEV - 1)),
                        pltpu.SemaphoreType.DMA((self.n,))]

    def _copies(self, x_refs, out_refs, sems):
        send_sems, recv_sems, local_sems = sems
        me, peers = _me_and_peers()

        def src(a, idx):
            return x_refs[a].at[idx] if self.scatter else x_refs[a]

        def copy(a, k, from_idx, to_slot, device):
            return pltpu.make_async_remote_copy(
                src_ref=src(a, from_idx), dst_ref=out_refs[a].at[to_slot], send_sem=send_sems.at[a, k],
                recv_sem=recv_sems.at[a, k], device_id=device, device_id_type=MESH)

        local = [pltpu.make_async_copy(src(a, me), out_refs[a].at[me], local_sems.at[a]) for a in range(self.n)]
        sends = [copy(a, k, idx, me, dev) for a in range(self.n) for k, (dev, idx) in enumerate(peers)]
        recvs = [copy(a, k, idx, idx, dev) for a in range(self.n) for k, (dev, idx) in enumerate(peers)]
        return local, sends, recvs

    def start(self, x_refs, out_refs, sems):
        local, sends, _ = self._copies(x_refs, out_refs, sems)
        for cp in local + sends:
            cp.start()

    def wait(self, x_refs, out_refs, sems):
        local, sends, recvs = self._copies(x_refs, out_refs, sems)
        for cp in recvs:
            cp.wait_recv()
        for cp in sends:
            cp.wait_send()
        for cp in local:
            cp.wait()

    def split(self, refs, n_in, n_out):
        n = self.n
        ins, xin = refs[:n_in], refs[n_in:n_in + n]
        outs, xout = refs[n_in + n:n_in + n + n_out], refs[n_in + n + n_out:n_in + 2 * n + n_out]
        rest = refs[n_in + 2 * n + n_out:]
        return ins, outs, rest[:len(rest) - 3], (xin, xout, rest[len(rest) - 3:])


def _ride(exchange, first, last, refs):
    if exchange is None:
        return

    @pl.when(first)
    def _():
        exchange.start(*refs)

    @pl.when(last)
    def _():
        exchange.wait(*refs)


def _gather_two_level(xs, *, name):
    n = len(xs)
    out_shapes = [jax.ShapeDtypeStruct((N_DEV,) + x.shape, x.dtype) for x in xs]

    def body(*refs):
        x_refs, out_refs = refs[:n], refs[n:2 * n]
        send_sems, recv_sems, local_sems = refs[2 * n:]
        x, y, c = lax.axis_index("x"), lax.axis_index("y"), lax.axis_index("c")
        me, sibling = (x, y, c), (x, y, 1 - c)
        chips = [(1 - x, y), (x, 1 - y), (1 - x, 1 - y)]

        def slot(a, dev):
            return out_refs[a].at[4 * dev[0] + 2 * dev[1] + dev[2]]

        def copy(a, k, block, to, src=None):
            return pltpu.make_async_remote_copy(
                src_ref=slot(a, block) if src is None else src, dst_ref=slot(a, block),
                send_sem=send_sems.at[a, k], recv_sem=recv_sems.at[a, k], device_id=to, device_id_type=MESH)

        mine = [pltpu.make_async_copy(x_refs[a], slot(a, me), local_sems.at[a]) for a in range(n)]
        first = [copy(a, 0, me, sibling, src=x_refs[a]) for a in range(n)]
        first += [copy(a, 1 + j, me, (*chip, c), src=x_refs[a]) for a in range(n) for j, chip in enumerate(chips)]
        for cp in mine + first:
            cp.start()
        passed = []
        for a in range(n):
            for j, chip in enumerate(chips):
                copy(a, 1 + j, (*chip, c), me).wait_recv()
                passed.append(copy(a, 4 + j, (*chip, c), sibling))
                passed[-1].start()
        for a in range(n):
            copy(a, 0, sibling, me).wait_recv()
            for j, chip in enumerate(chips):
                copy(a, 4 + j, (*chip, 1 - c), me).wait_recv()
        for cp in first + passed:
            cp.wait_send()
        for cp in mine:
            cp.wait()

    return pl.pallas_call(
        body, name=name, in_specs=[HBM_SPEC] * n, out_specs=[HBM_SPEC] * n, out_shape=out_shapes,
        scratch_shapes=[pltpu.SemaphoreType.DMA((n, N_DEV - 1)), pltpu.SemaphoreType.DMA((n, N_DEV - 1)),
                        pltpu.SemaphoreType.DMA((n,))],
        compiler_params=pltpu.CompilerParams(has_side_effects=True),
    )(*xs)


def _exchange(xs, *, name, scatter):
    ex = _Exchange(xs, scatter)

    def body(*refs):
        _, _, _, xrefs = ex.split(refs, 0, 0)
        ex.start(*xrefs)
        ex.wait(*xrefs)

    return pl.pallas_call(
        body, name=name, in_specs=ex.in_specs, out_specs=ex.out_specs, out_shape=ex.out_shapes,
        scratch_shapes=ex.scratch, compiler_params=pltpu.CompilerParams(has_side_effects=True),
    )(*xs)


def _ada_fwd(c_all, w_shard, b_shard):
    n = w_shard.shape[1]

    def body(c_ref, w_ref, b_ref, o_ref):
        cv = c_ref[...]
        act = (cv * _sigmoid(cv)).astype(BF16)
        o_ref[...] = jnp.dot(act, w_ref[...].astype(BF16), preferred_element_type=F32) + b_ref[...]

    return pl.pallas_call(
        body, name="ada_fwd", in_specs=[_full((N_DEV, D)), _full((D, n)), _full((1, n))],
        out_specs=_full((N_DEV, n)), out_shape=jax.ShapeDtypeStruct((N_DEV, n), F32), grid=(1,),
        compiler_params=_params(("arbitrary",)),
    )(c_all, w_shard, b_shard)


def _ada_bwd(c_all_t, dmod_pad):
    n = dmod_pad.shape[1]

    def body(c_ref, d_ref, o_ref):
        cv = c_ref[...]
        act = (cv * _sigmoid(cv)).astype(BF16)
        o_ref[...] = jnp.dot(act, d_ref[...].astype(BF16), preferred_element_type=F32)

    return pl.pallas_call(
        body, name="ada_bwd", in_specs=[_full((D, LANES)), _full((LANES, n))],
        out_specs=_full((D, n)), out_shape=jax.ShapeDtypeStruct((D, n), F32), grid=(1,),
        compiler_params=_params(("arbitrary",)),
    )(c_all_t, dmod_pad)


ADAM_ROWS = 256


def _adamw(parts, w, m, v, *, name):
    n, R, C = parts.shape
    tr = next((t for t in (ADAM_ROWS, 128, 64, 32, 16, SUBLANES) if R % t == 0), R)

    def body(p_ref, w_ref, m_ref, v_ref, g_ref, d_ref, nm_ref, nv_ref):
        g = p_ref[0].astype(F32)
        for j in range(1, n):
            g = g + p_ref[j].astype(F32)
        g_ref[...] = g
        nm = ADAM_B1 * m_ref[...] + (1.0 - ADAM_B1) * g
        nv = ADAM_B2 * v_ref[...] + (1.0 - ADAM_B2) * (g * g)
        nm_ref[...] = nm
        nv_ref[...] = nv
        m_hat = nm / (1.0 - ADAM_B1 ** ADAM_STEP)
        v_hat = nv / (1.0 - ADAM_B2 ** ADAM_STEP)
        d_ref[...] = -ADAM_LR * (m_hat / (jnp.sqrt(v_hat) + ADAM_EPS) + ADAM_WD * w_ref[...])

    row = pl.BlockSpec((tr, C), lambda i: (i, 0))
    return pl.pallas_call(
        body, name=name, grid=(R // tr,),
        in_specs=[pl.BlockSpec((n, tr, C), lambda i: (0, i, 0)), row, row, row], out_specs=[row] * 4,
        out_shape=[jax.ShapeDtypeStruct((R, C), F32)] * 4,
        compiler_params=_params(("parallel",)),
    )(parts, w, m, v)


SHARDED = (("w_in", D, IN_W, 1), ("w_branch_a", CONV_W, D, 1), ("w_branch_b", ATTN_W, D, 1), ("w_out", D, D, 0),
           ("w_up", D, 2 * D_FF, 1), ("w_down", D_FF, D, 0), ("conv_a_w", 3, CONV_W, 1),
           ("conv_ffn_w", 3, 2 * D_FF, 1))
MATRICES = SHARDED[:6]
LATE = MATRICES[1:]
CONVS = SHARDED[6:]
REPLICATED = (("b_ada", N_MOD * D), ("norm1_g", D), ("norm2_g", D), ("b_f", N_HEADS), ("q_norm_g", HEAD_DIM),
              ("k_norm_g", HEAD_DIM))


def _shard_shape(rows, cols, axis):
    return (rows // N_DEV, cols) if axis == 0 else (rows, cols // N_DEV)


def _pack_rows(flat, multiple):
    length = flat.shape[-1]
    rows = -(-length // PACK_W)
    rows = -(-rows // multiple) * multiple
    pad = [(0, 0)] * (flat.ndim - 1) + [(0, rows * PACK_W - length)]
    return jnp.pad(flat, pad).reshape(flat.shape[:-1] + (rows, PACK_W))


def _pack_shards(shards, spec, multiple, dtype):
    flat = jnp.concatenate([shards[name].reshape(-1).astype(dtype) for name, *_ in spec])
    return _pack_rows(flat, multiple)


def _join_lane_blocks(gathered):
    n, r, c = gathered.shape

    def body(g_ref, o_ref):
        for j in range(n):
            o_ref[:, j * c:(j + 1) * c] = g_ref[j]

    return pl.pallas_call(
        body, name="join_lane_blocks", grid=(1,), in_specs=[_full((n, r, c))], out_specs=_full((r, n * c)),
        out_shape=jax.ShapeDtypeStruct((r, n * c), gathered.dtype), compiler_params=_params(("arbitrary",)),
    )(gathered)


SHARD_PAD = 768


def _assemble_columns(gathered, shard_cols, out_cols, segments, *, name):
    n, rows, padw = gathered.shape
    assert n == N_DEV and padw == SHARD_PAD and shard_cols <= SHARD_PAD

    def body(g_ref, o_ref):
        j = pl.program_id(0)

        @pl.when(j == 0)
        def _():
            o_ref[...] = jnp.zeros_like(o_ref)

        for dev in range(N_DEV):
            @pl.when(j == dev)
            def _(dev=dev):
                x = g_ref[0]
                for lo, hi, delta in segments:
                    a, b = max(lo, dev * shard_cols), min(hi, (dev + 1) * shard_cols)
                    if a >= b:
                        continue
                    base = (a + delta) // LANES * LANES
                    width = -(-(b + delta - base) // LANES) * LANES
                    src = lax.broadcasted_iota(jnp.int32, (padw, width), 0) + dev * shard_cols
                    dst = lax.broadcasted_iota(jnp.int32, (padw, width), 1) + (base - delta)
                    place = jnp.where((src == dst) & (src >= a) & (src < b), 1.0, 0.0).astype(BF16)
                    moved = jnp.dot(x, place, preferred_element_type=F32).astype(BF16)
                    o_ref[:, base:base + width] = o_ref[:, base:base + width] + moved

    return pl.pallas_call(
        body, name=name, grid=(N_DEV,), in_specs=[pl.BlockSpec((1, rows, padw), lambda j: (j, 0, 0))],
        out_specs=_full((rows, out_cols)), out_shape=jax.ShapeDtypeStruct((rows, out_cols), BF16),
        compiler_params=_params(("arbitrary",)),
    )(gathered)


def _pad_shard(w):
    return jnp.pad(w.astype(BF16), ((0, 0), (0, SHARD_PAD - w.shape[1])))


W_IN_SEGMENTS = ((0, COL_GA, 0), (COL_GA, COL_GA + N_HEADS, COL_F - COL_GA), (COL_GA + N_HEADS, IN_W, -N_HEADS))


def _join_shards(gathered, axis):
    if axis == 0:
        return gathered.reshape(N_DEV * gathered.shape[1], gathered.shape[2])
    if gathered.shape[2] == LANES:
        return _join_lane_blocks(gathered)
    return jnp.concatenate([gathered[j] for j in range(N_DEV)], axis=1)


def _column_shards(pieces):
    total = sum(p.shape[1] for p in pieces)
    width = total // N_DEV
    shards = []
    for j in range(N_DEV):
        lo, hi, off, segs = j * width, (j + 1) * width, 0, []
        for p in pieces:
            a, b = max(lo, off), min(hi, off + p.shape[1])
            if a < b:
                segs.append(p[:, a - off:b - off])
            off += p.shape[1]
        shards.append(segs[0] if len(segs) == 1 else jnp.concatenate(segs, axis=1))
    return jnp.stack(shards)


def _split_shards(full, axis):
    if axis == 0:
        return full.reshape(N_DEV, full.shape[0] // N_DEV, full.shape[1])
    c = full.shape[1] // N_DEV
    return jnp.stack([full[:, j * c:(j + 1) * c] for j in range(N_DEV)])


def _unpack_shards(packed, spec):
    flat = packed.reshape(-1)
    out, off = {}, 0
    for name, rows, cols, axis in spec:
        r, c = _shard_shape(rows, cols, axis)
        out[name] = flat[off:off + r * c].reshape(r, c)
        off += r * c
    return out


def _unpack_gathered(gathered, spec):
    flat = gathered.reshape(N_DEV, -1)
    out, off = {}, 0
    for name, rows, cols, axis in spec:
        r, c = _shard_shape(rows, cols, axis)
        seg = flat[:, off:off + r * c].reshape(N_DEV, r, c)
        out[name] = seg.reshape(rows, cols) if axis == 0 else seg.transpose(1, 0, 2).reshape(rows, cols)
        off += r * c
    return out


def _pack_full_by_dest(full, spec, multiple):
    segs = []
    for name, rows, cols, axis in spec:
        r, c = _shard_shape(rows, cols, axis)
        a = full[name]
        seg = a.reshape(N_DEV, r, c) if axis == 0 else a.reshape(rows, N_DEV, c).transpose(1, 0, 2)
        segs.append(seg.reshape(N_DEV, r * c))
    return _pack_rows(jnp.concatenate(segs, axis=1), multiple)


def _pad_head_rows(w):
    n = w.shape[1]
    padded = jnp.pad(w.reshape(N_HEADS, HEAD_DIM, n), ((0, 0), (0, LANES - HEAD_DIM), (0, 0)))
    return padded.reshape(N_HEADS * LANES, n)


def kernel(x, c, w_ada, b_ada, norm1_g, w_in, b_f, conv_a_w, q_norm_g, k_norm_g, w_branch_a, w_branch_b, w_out, norm2_g, w_up, conv_ffn_w, w_down, loss_target, m_w_ada, m_b_ada, m_norm1_g, m_w_in, m_b_f, m_conv_a_w, m_q_norm_g, m_k_norm_g, m_w_branch_a, m_w_branch_b, m_w_out, m_norm2_g, m_w_up, m_conv_ffn_w, m_w_down, v_w_ada, v_b_ada, v_norm1_g, v_w_in, v_b_f, v_conv_a_w, v_q_norm_g, v_k_norm_g, v_w_branch_a, v_w_branch_b, v_w_out, v_norm2_g, v_w_up, v_conv_ffn_w, v_w_down):
    names = ("w_ada", "b_ada", "norm1_g", "w_in", "b_f", "conv_a_w", "q_norm_g", "k_norm_g", "w_branch_a",
             "w_branch_b", "w_out", "norm2_g", "w_up", "conv_ffn_w", "w_down")
    squeeze = lambda a: a[0] if a.ndim == 3 else a
    W = dict(zip(names, map(squeeze, (w_ada, b_ada, norm1_g, w_in, b_f, conv_a_w, q_norm_g, k_norm_g, w_branch_a,
                                      w_branch_b, w_out, norm2_g, w_up, conv_ffn_w, w_down))))
    M = dict(zip(names, map(squeeze, (m_w_ada, m_b_ada, m_norm1_g, m_w_in, m_b_f, m_conv_a_w, m_q_norm_g,
                                      m_k_norm_g, m_w_branch_a, m_w_branch_b, m_w_out, m_norm2_g, m_w_up,
                                      m_conv_ffn_w, m_w_down))))
    V = dict(zip(names, map(squeeze, (v_w_ada, v_b_ada, v_norm1_g, v_w_in, v_b_f, v_conv_a_w, v_q_norm_g,
                                      v_k_norm_g, v_w_branch_a, v_w_branch_b, v_w_out, v_norm2_g, v_w_up,
                                      v_conv_ffn_w, v_w_down))))
    me = 4 * lax.axis_index("x") + 2 * lax.axis_index("y") + lax.axis_index("c")
    ada_n = N_MOD * D // N_DEV

    small = jnp.concatenate([c.reshape(-1), W["conv_a_w"].reshape(-1), W["conv_ffn_w"].reshape(-1)])
    small_all, w_in_all = _gather_two_level([_pack_rows(small, SUBLANES), _pad_shard(W["w_in"])],
                                            name="gather_first")
    small_all = small_all.reshape(N_DEV, -1)
    c_all = small_all[:, :D]
    conv_all = _unpack_gathered(small_all[:, D:], CONVS)

    b_shard = lax.dynamic_slice(W["b_ada"], (0, me * ada_n), (1, ada_n))
    mod_part = _ada_fwd(c_all, W["w_ada"], b_shard)
    mod_all, = _exchange([mod_part], name="gather_mod", scatter=False)
    mod = lax.dynamic_index_in_dim(mod_all, me, axis=1, keepdims=False).reshape(1, N_MOD * D)

    wts = {"w_in": _assemble_columns(w_in_all, IN_W // N_DEV, IN_W_PAD, W_IN_SEGMENTS, name="assemble_w_in")}
    wts.update(conv_all)
    for name in ("norm1_g", "norm2_g", "q_norm_g", "k_norm_g", "b_f"):
        wts[name] = W[name]
    late = {name: _pad_shard(W[name]) if name == "w_up" else W[name].astype(BF16) for name, *_ in LATE}

    sq, grad_x, grads, parts = _local_step(x[0], loss_target[0], mod, wts, late)
    loss = lax.psum(sq[0, 0] * (0.5 / D), AXES)

    grads["b_ada"] = grads["mod"]
    rep_flat = lambda src: jnp.concatenate([src[name].reshape(-1) for name, _ in REPLICATED])
    rep_parts, = _exchange([_pack_rows(rep_flat(grads), 16)], name="gather_small_grads", scatter=False)
    rep_out = _adamw(rep_parts, *[_pack_rows(rep_flat(s), 16) for s in (W, M, V)], name="adamw_replicated")

    dmod_all = rep_parts.reshape(N_DEV, -1)[:, :N_MOD * D]
    dmod_mine = lax.dynamic_slice(dmod_all, (0, me * ada_n), (N_DEV, ada_n))
    g_ada = _ada_bwd(jnp.pad(c_all.T, ((0, 0), (0, LANES - N_DEV))),
                     jnp.pad(dmod_mine, ((0, LANES - N_DEV), (0, 0))))
    ada_out = _adamw(g_ada[None], W["w_ada"], M["w_ada"], V["w_ada"], name="adamw_ada")

    mat_out = {name: _adamw(parts[name], W[name], M[name], V[name], name="adamw_" + name) for name, *_ in MATRICES}
    conv_out = _adamw(parts["conv"], *[_pack_shards(s, CONVS, SUBLANES, F32) for s in (W, M, V)],
                      name="adamw_conv")

    results = []
    for kind in range(4):
        per = {"w_ada": ada_out[kind]}
        per.update({name: out[kind] for name, out in mat_out.items()})
        per.update(_unpack_shards(conv_out[kind], CONVS))
        flat, off = rep_out[kind].reshape(-1), 0
        for name, n in REPLICATED:
            per[name] = flat[off:off + n].reshape(1, n)
            off += n
        results.append(per)
    restore = lambda name, a: a[None] if W[name].ndim == 2 and name not in dict(REPLICATED) else a
    outs = [loss, grad_x[None]]
    for per in results:
        outs.extend(restore(name, per[name]) for name in names)
    return tuple(outs)
```

```python
import jax
import jax.numpy as jnp
import numpy as np
from jax import lax
from jax.experimental import pallas as pl
from jax.experimental.pallas import tpu as pltpu

F32 = jnp.float32
BF16 = jnp.bfloat16

N_DEV = 8
D = 1024
N_HEADS = 8
HEAD_DIM = 64
ATTN_W = 512
CONV_W = 512
D_FF = 2816
N_MOD = 6
IN_W = 5128
RMS_EPS = 1e-6
NEG_INF = -1e30
LOG2E = 1.4426950408889634

IN_W_PAD = 5376
COL_GA = 3072
COL_GB = 4096
COL_F = 5120
F_PAD = 128

ADAM_LR = 0.001
ADAM_B1 = 0.9
ADAM_B2 = 0.999
ADAM_EPS = 1e-08
ADAM_WD = 0.01
ADAM_STEP = 10

LANES = 128
SUBLANES = 8
BF16_ROWS = 16
VMEM_LIMIT = 52 * 1024 * 1024
TOKEN_TILE = 512
MATMUL_TILE = 1024
ATTN_BLOCK = 512
PACK_W = 1024

MESH = pl.DeviceIdType.MESH
AXES = ("x", "y", "c")


def _params(sem=None, **kw):
    return pltpu.CompilerParams(dimension_semantics=sem, vmem_limit_bytes=VMEM_LIMIT, **kw)


def _full(shape):
    nd = len(shape)
    return pl.BlockSpec(shape, lambda *_: (0,) * nd)


def _tn_dot(a, b):
    return lax.dot_general(a, b, (((0,), (0,)), ((), ())), preferred_element_type=F32)


def _matmul(a, b, *, name, tm, tn, tk, out_dtype=F32, trans_a=False, trans_b=False, exchange=None):
    assert not (trans_a and trans_b)
    if trans_a:
        K, M = a.shape
    else:
        M, K = a.shape
    N = b.shape[0] if trans_b else b.shape[1]
    assert b.shape[1 if trans_b else 0] == K and M % tm == 0 and N % tn == 0 and K % tk == 0, (name, a.shape, b.shape)
    nm, nn, nk = M // tm, N // tn, K // tk

    def body(*refs):
        if exchange is None:
            a_ref, b_ref, o_ref, *own = refs
        else:
            (a_ref, b_ref), (o_ref,), own, xrefs = exchange.split(refs, 2, 1)
            ids = [pl.program_id(d) for d in range(3)]
            first = jnp.logical_and(jnp.logical_and(ids[0] == 0, ids[1] == 0), ids[2] == 0)
            last = jnp.logical_and(jnp.logical_and(ids[0] == nn - 1, ids[1] == nm - 1), ids[2] == nk - 1)
            _ride(exchange, first, last, xrefs)
        k = pl.program_id(2)
        av = a_ref[...].astype(BF16)
        bv = b_ref[...].astype(BF16)
        if trans_a:
            prod = _tn_dot(av, bv)
        elif trans_b:
            prod = _nt_dot(av, bv)
        else:
            prod = jnp.dot(av, bv, preferred_element_type=F32)
        if nk == 1:
            o_ref[...] = prod.astype(out_dtype)
            return
        acc_ref, = own

        @pl.when(k == 0)
        def _():
            acc_ref[...] = prod

        @pl.when(k > 0)
        def _():
            acc_ref[...] += prod

        @pl.when(k == nk - 1)
        def _():
            o_ref[...] = acc_ref[...].astype(out_dtype)

    if trans_a:
        a_spec = pl.BlockSpec((tk, tm), lambda j, i, k: (k, i))
    else:
        a_spec = pl.BlockSpec((tm, tk), lambda j, i, k: (i, k))
    b_spec = pl.BlockSpec((tn, tk), lambda j, i, k: (j, k)) if trans_b else pl.BlockSpec((tk, tn), lambda j, i, k: (k, j))
    in_specs = [a_spec, b_spec]
    out_spec = pl.BlockSpec((tm, tn), lambda j, i, k: (i, j))
    out_shape = jax.ShapeDtypeStruct((M, N), out_dtype)
    scratch = [pltpu.VMEM((tm, tn), F32)] if nk > 1 else []
    if exchange is None:
        return pl.pallas_call(
            body, name=name, grid=(nn, nm, nk), in_specs=in_specs, out_specs=out_spec, out_shape=out_shape,
            scratch_shapes=scratch, compiler_params=_params(("parallel", "parallel", "arbitrary")),
        )(a, b)
    return pl.pallas_call(
        body, name=name, grid=(nn, nm, nk), in_specs=in_specs + exchange.in_specs,
        out_specs=[out_spec] + exchange.out_specs, out_shape=[out_shape] + exchange.out_shapes,
        scratch_shapes=scratch + exchange.scratch, compiler_params=_params(("arbitrary",) * 3),
    )(a, b, *exchange.xs)


def _norm_bwd_tile(dh, ins, outs, first):
    x_ref, dr_ref, g_ref, sc_ref = ins[:4]
    dx_ref, dsh_ref, dsc_ref, dg_ref = outs[:4]

    @pl.when(first)
    def _():
        for ref in outs[1:4] + outs[5:]:
            ref[...] = jnp.zeros_like(ref)

    xv = x_ref[...]
    gv = g_ref[...]
    one_sc = 1.0 + sc_ref[...]
    inv = lax.rsqrt(jnp.mean(xv * xv, axis=-1, keepdims=True) + RMS_EPS)
    xn = xv * inv
    dxn = dh * (gv * one_sc)
    dx = dr_ref[...] + inv * (dxn - xn * jnp.mean(dxn * xn, axis=-1, keepdims=True))
    dx_ref[...] = dx
    dhxn = dh * xn
    dsh_ref[...] += _rows8(dh)
    dsc_ref[...] += _rows8(dhxn * gv)
    dg_ref[...] += _rows8(dhxn * one_sc)
    if len(ins) == 6:
        mix_ref, g1_ref = ins[4:]
        dmix_ref, dg1_ref = outs[4:]
        dmix_ref[...] = (dx * g1_ref[...]).astype(BF16)
        dg1_ref[...] += _rows8(dx * mix_ref[...])


def _matmul_pieces(pieces, b, *, name, tm, exchange=None, norm_bwd=None):
    M = pieces[0].shape[0]
    widths = [p.shape[1] for p in pieces]
    offsets = [sum(widths[:i]) for i in range(len(widths))]
    N = b.shape[0]
    assert b.shape[1] >= sum(widths) and M % tm == 0, (name, widths, b.shape)
    n_p, nm = len(pieces), M // tm
    extra = list(norm_bwd) if norm_bwd is not None else []
    n_in = n_p + 1 + len(extra)
    n_out = len(extra) if norm_bwd is not None else 1

    def body(*refs):
        i = pl.program_id(0)
        if exchange is None:
            ins, outs = refs[:n_in], refs[n_in:]
        else:
            ins, outs, _, xrefs = exchange.split(refs, n_in, n_out)
            _ride(exchange, i == 0, i == nm - 1, xrefs)
        b_ref = ins[n_p]
        acc = None
        for a_ref, off, w in zip(ins[:n_p], offsets, widths):
            term = _nt_dot(a_ref[...].astype(BF16), b_ref[:, off:off + w])
            acc = term if acc is None else acc + term
        if norm_bwd is None:
            outs[0][...] = acc
        else:
            _norm_bwd_tile(acc, tuple(ins[n_p + 1:]), tuple(outs), first=i == 0)

    row, vec, part = pl.BlockSpec((tm, N), lambda i: (i, 0)), _full((1, N)), _full((SUBLANES, N))
    part_shape = jax.ShapeDtypeStruct((SUBLANES, N), F32)
    in_specs = [pl.BlockSpec((tm, w), lambda i: (i, 0)) for w in widths] + [_full(b.shape)]
    out_specs, out_shape = [row], [jax.ShapeDtypeStruct((M, N), F32)]
    if norm_bwd is not None:
        in_specs += [row, row, vec, vec] + ([row, vec] if len(extra) == 6 else [])
        out_specs += [part] * 3 + ([row, part] if len(extra) == 6 else [])
        out_shape += [part_shape] * 3 + ([jax.ShapeDtypeStruct((M, N), BF16), part_shape] if len(extra) == 6 else [])
    sequential = exchange is not None or norm_bwd is not None
    xs = exchange.xs if exchange is not None else []
    result = pl.pallas_call(
        body, name=name, grid=(nm,), in_specs=in_specs + (exchange.in_specs if exchange else []),
        out_specs=out_specs + (exchange.out_specs if exchange else []),
        out_shape=out_shape + (exchange.out_shapes if exchange else []),
        scratch_shapes=exchange.scratch if exchange else [],
        compiler_params=_params(("arbitrary" if sequential else "parallel",)),
    )(*pieces, b, *extra, *xs)
    return result[0] if len(result) == 1 else result


def _matmul_tn_pieces(a, pieces, *, name, tk):
    K, M = a.shape
    widths = [p.shape[1] for p in pieces]
    n_p, nk = len(pieces), K // tk

    def body(*refs):
        a_ref, p_refs, o_refs = refs[0], refs[1:n_p + 1], refs[n_p + 1:]
        k = pl.program_id(0)
        av = a_ref[...].astype(BF16)
        for p_ref, o_ref in zip(p_refs, o_refs):
            prod = _tn_dot(av, p_ref[...].astype(BF16))

            @pl.when(k == 0)
            def _():
                o_ref[...] = prod

            @pl.when(k > 0)
            def _():
                o_ref[...] += prod

    return pl.pallas_call(
        body, name=name, grid=(nk,),
        in_specs=[pl.BlockSpec((tk, M), lambda k: (k, 0))] + [pl.BlockSpec((tk, w), lambda k: (k, 0)) for w in widths],
        out_specs=[_full((M, w)) for w in widths], out_shape=[jax.ShapeDtypeStruct((M, w), F32) for w in widths],
        compiler_params=_params(("arbitrary",)),
    )(a, *pieces)


def _split_dot(x, mat, parts):
    out = None
    rem = x
    for p in range(parts):
        piece = rem.astype(BF16)
        term = jnp.dot(piece, mat, preferred_element_type=F32)
        out = term if out is None else out + term
        if p + 1 < parts:
            rem = rem - piece.astype(F32)
    return out


def _sigmoid(x):
    return 0.5 * jnp.tanh(0.5 * x) + 0.5


def _rows8(x):
    r, c = x.shape
    return jnp.sum(x.reshape(r // SUBLANES, SUBLANES, c), axis=0)


def _shift_down(blk, prev8, n):
    rolled = pltpu.roll(blk, n, axis=0)
    prev_rolled = pltpu.roll(prev8, n, axis=0)
    rows = lax.broadcasted_iota(jnp.int32, prev8.shape, 0)
    first = jnp.where(rows < n, prev_rolled, rolled[0:SUBLANES])
    return jnp.concatenate([first, rolled[SUBLANES:]], axis=0)


def _prev_spec(tm, width, col):
    per = tm // SUBLANES
    return pl.BlockSpec((SUBLANES, width), lambda i, *_: (jnp.maximum(i * per - 1, 0), col))


def _next_spec(tm, width, col, n_tiles):
    per = tm // SUBLANES
    last = n_tiles * per - 1
    return pl.BlockSpec((SUBLANES, width), lambda i, *_: (jnp.minimum((i + 1) * per, last), col))


def _group_matrix():
    idx = np.arange(ATTN_W) // HEAD_DIM
    return jnp.asarray((idx[:, None] == idx[None, :]).astype(np.float32), BF16)


def _norm_mod(x, g, sc, sh, *, name):
    T = x.shape[0]
    tm = min(TOKEN_TILE, T)

    def body(x_ref, g_ref, sc_ref, sh_ref, o_ref):
        xv = x_ref[...]
        inv = lax.rsqrt(jnp.mean(xv * xv, axis=-1, keepdims=True) + RMS_EPS)
        o_ref[...] = ((xv * inv) * g_ref[...] * (1.0 + sc_ref[...]) + sh_ref[...]).astype(BF16)

    row = pl.BlockSpec((tm, D), lambda i: (i, 0))
    return pl.pallas_call(
        body, name=name, grid=(T // tm,),
        in_specs=[row, _full((1, D)), _full((1, D)), _full((1, D))],
        out_specs=row, out_shape=jax.ShapeDtypeStruct((T, D), BF16),
        compiler_params=_params(("parallel",)),
    )(x, g, sc, sh)


LANE_ONE = 64
LANE_F = 67
LANE_LSE = 70
LANE_SUM = 73


def _pieces(x):
    hi = x.astype(BF16).astype(F32)
    rest = x - hi
    mid = rest.astype(BF16).astype(F32)
    return hi, mid, rest - mid


def _run(start, vals):
    return [(start + i, v) for i, v in enumerate(vals)]


def _head_lanes(a, h):
    blk = a[:, LANES * (h // 2):LANES * (h // 2) + LANES]
    return blk if h % 2 == 0 else pltpu.roll(blk, HEAD_DIM, axis=1)


def _branch_prep(proj, fcum, conv_w8, qg, kg, gmat):
    T = proj.shape[0]
    tm = min(TOKEN_TILE, T)
    nt = T // tm

    def body(cb_ref, cc_ref, cv_ref, q_ref, k_ref, v_ref, f_ref, ccp_ref, cvp_ref, w_ref, qg_ref, kg_ref, g_ref,
             ya_ref, qa_ref, ka_ref, va_ref):
        i = pl.program_id(0)
        z = cc_ref[...] * cv_ref[...]
        zp = jnp.where(i > 0, ccp_ref[...] * cvp_ref[...], 0.0)
        w = w_ref[...]
        cz = _shift_down(z, zp, 2) * w[0:1] + _shift_down(z, zp, 1) * w[1:2] + z * w[2:3]
        ya_ref[...] = (cb_ref[...] * cz).astype(BF16)
        gm = g_ref[...]

        def normed(src, gain, scale):
            v = src[...]
            ms = _split_dot(v * v, gm, 2) * (1.0 / HEAD_DIM)
            return (v * lax.rsqrt(ms + RMS_EPS)) * gain[...] * scale

        qn = normed(q_ref, qg_ref, LOG2E / np.sqrt(HEAD_DIM))
        kn = normed(k_ref, kg_ref, 1.0)
        vv = v_ref[...]
        lane = lax.broadcasted_iota(jnp.int32, (tm, LANES), 1)
        low = lane < HEAD_DIM
        in_run = lambda start: jnp.logical_and(lane >= start, lane < start + 3)
        q_ones = jnp.where(jnp.logical_or(in_run(LANE_ONE), lane == LANE_SUM), 1.0, 0.0)
        k_ones = jnp.where(jnp.logical_or(in_run(LANE_F), in_run(LANE_LSE)), 1.0, 0.0)
        v_ones = jnp.where(in_run(LANE_ONE), 1.0, 0.0)
        f3 = jnp.concatenate(_pieces(f_ref[...] * LOG2E), axis=1).astype(BF16)
        src = lax.broadcasted_iota(jnp.int32, (3 * LANES, LANES), 0)
        dst = lax.broadcasted_iota(jnp.int32, (3 * LANES, LANES), 1)
        for h in range(N_HEADS):
            def pick(start, h=h):
                hit = jnp.logical_and(src - h == (dst - start) * LANES, jnp.logical_and(dst >= start, dst < start + 3))
                return jnp.dot(f3, jnp.where(hit, 1.0, 0.0).astype(BF16), preferred_element_type=F32)

            qa_ref[h] = jnp.where(low, _head_lanes(qn, h), q_ones + pick(LANE_F)).astype(BF16)
            ka_ref[h] = jnp.where(low, _head_lanes(kn, h), k_ones - pick(LANE_ONE)).astype(BF16)
            va_ref[h] = jnp.where(low, _head_lanes(vv, h), v_ones).astype(BF16)

    blk = lambda col: pl.BlockSpec((tm, CONV_W), lambda i: (i, col))
    heads = pl.BlockSpec((N_HEADS, tm, LANES), lambda i: (0, i, 0))
    return pl.pallas_call(
        body, name="branch_prep", grid=(nt,),
        in_specs=[blk(0), blk(1), blk(2), blk(3), blk(4), blk(5), pl.BlockSpec((tm, F_PAD), lambda i: (i, 0)),
                  _prev_spec(tm, CONV_W, 1), _prev_spec(tm, CONV_W, 2),
                  _full((SUBLANES, CONV_W)), _full((1, ATTN_W)), _full((1, ATTN_W)), _full((ATTN_W, ATTN_W))],
        out_specs=[pl.BlockSpec((tm, CONV_W), lambda i: (i, 0)), heads, heads, heads],
        out_shape=[jax.ShapeDtypeStruct((T, CONV_W), BF16)] + [jax.ShapeDtypeStruct((N_HEADS, T, LANES), BF16)] * 3,
        compiler_params=_params(("parallel",)),
    )(proj, proj, proj, proj, proj, proj, fcum, proj, proj, conv_w8, qg, kg, gmat)


def _cumsum(x, *, reverse, name, col=0, gate_bias=None):
    T = x.shape[0]
    tm = min(TOKEN_TILE, T)
    nt = T // tm

    def body(x_ref, b_ref, o_ref, carry_ref):
        i = pl.program_id(0)

        @pl.when(i == 0)
        def _():
            carry_ref[...] = jnp.zeros_like(carry_ref)

        r = lax.broadcasted_iota(jnp.int32, (tm, tm), 0)
        c = lax.broadcasted_iota(jnp.int32, (tm, tm), 1)
        tri = jnp.where((c >= r) if reverse else (c <= r), 1.0, 0.0).astype(BF16)
        xv = x_ref[...]
        if gate_bias is not None:
            fx = xv + b_ref[...]
            xv = jnp.minimum(fx, 0.0) - jnp.log(1.0 + jnp.exp(-jnp.abs(fx)))
        out = _split_dot_left(tri, xv, 3) + carry_ref[0:1]
        o_ref[...] = out
        carry_ref[...] = jnp.broadcast_to(out[0:1] if reverse else out[tm - 1:tm], carry_ref.shape)

    rows = (lambda i: nt - 1 - i) if reverse else (lambda i: i)
    bias = jnp.zeros((1, F_PAD), F32) if gate_bias is None else gate_bias
    return pl.pallas_call(
        body, name=name, grid=(nt,),
        in_specs=[pl.BlockSpec((tm, F_PAD), lambda i: (rows(i), col)), _full((1, F_PAD))],
        out_specs=pl.BlockSpec((tm, F_PAD), lambda i: (rows(i), 0)),
        out_shape=jax.ShapeDtypeStruct((T, F_PAD), F32),
        scratch_shapes=[pltpu.VMEM((SUBLANES, F_PAD), F32)],
        compiler_params=_params(("arbitrary",)),
    )(x, bias)


def _split_dot_left(mat, x, parts):
    out = None
    rem = x
    for p in range(parts):
        piece = rem.astype(BF16)
        term = jnp.dot(mat, piece, preferred_element_type=F32)
        out = term if out is None else out + term
        if p + 1 < parts:
            rem = rem - piece.astype(F32)
    return out


def _out_resid_norm(x, merged, w_out, g1, g, sc, sh):
    T = x.shape[0]
    tm = min(TOKEN_TILE, T)

    def body(x_ref, m_ref, w_ref, g1_ref, g_ref, sc_ref, sh_ref, mix_ref, x1_ref, h_ref):
        mix = jnp.dot(m_ref[...], w_ref[...], preferred_element_type=F32)
        mix_ref[...] = mix
        x1 = x_ref[...] + g1_ref[...] * mix
        x1_ref[...] = x1
        inv = lax.rsqrt(jnp.mean(x1 * x1, axis=-1, keepdims=True) + RMS_EPS)
        h_ref[...] = ((x1 * inv) * g_ref[...] * (1.0 + sc_ref[...]) + sh_ref[...]).astype(BF16)

    row = pl.BlockSpec((tm, D), lambda i: (i, 0))
    vec = _full((1, D))
    return pl.pallas_call(
        body, name="out_resid_norm", grid=(T // tm,),
        in_specs=[row, row, _full((D, D)), vec, vec, vec, vec], out_specs=[row, row, row],
        out_shape=[jax.ShapeDtypeStruct((T, D), F32), jax.ShapeDtypeStruct((T, D), F32),
                   jax.ShapeDtypeStruct((T, D), BF16)],
        compiler_params=_params(("parallel",)),
    )(x, merged, w_out, g1, g, sc, sh)


FFN_TM = 256
FFN_TC = 1408


def _ffn_act_fwd(u, w8):
    T = u.shape[0]
    tm = min(FFN_TM, T)
    nt = T // tm
    nc = D_FF // FFN_TC

    def body(ug_ref, uv_ref, ugp_ref, uvp_ref, wg_ref, wv_ref, o_ref, cg_ref, cv_ref):
        i = pl.program_id(1)

        def conv(u_ref, p_ref, w_ref):
            uv = u_ref[...]
            up = jnp.where(i > 0, p_ref[...], 0.0)
            w = w_ref[...]
            return _shift_down(uv, up, 2) * w[0:1] + _shift_down(uv, up, 1) * w[1:2] + uv * w[2:3]

        gate = conv(ug_ref, ugp_ref, wg_ref)
        val = conv(uv_ref, uvp_ref, wv_ref)
        cg_ref[...] = gate.astype(BF16)
        cv_ref[...] = val.astype(BF16)
        o_ref[...] = (gate * _sigmoid(gate) * val).astype(BF16)

    per = tm // SUBLANES
    blk = lambda off: pl.BlockSpec((tm, FFN_TC), lambda j, i: (i, j + off))
    prev = lambda off: pl.BlockSpec((SUBLANES, FFN_TC), lambda j, i: (jnp.maximum(i * per - 1, 0), j + off))
    wblk = lambda off: pl.BlockSpec((SUBLANES, FFN_TC), lambda j, i: (0, j + off))
    return pl.pallas_call(
        body, name="ffn_act_fwd", grid=(nc, nt),
        in_specs=[blk(0), blk(nc), prev(0), prev(nc), wblk(0), wblk(nc)],
        out_specs=[blk(0), blk(0), blk(0)],
        out_shape=[jax.ShapeDtypeStruct((T, D_FF), BF16)] * 3,
        compiler_params=_params(("parallel", "parallel")),
    )(u, u, u, u, w8, w8)


def _down_loss_head(x1, act, w_down, g2, target):
    T = x1.shape[0]
    tm = min(TOKEN_TILE, T)

    def body(x1_ref, a_ref, w_ref, g2_ref, t_ref, dy_ref, dff_ref, loss_ref, dg2_ref):
        i = pl.program_id(0)

        @pl.when(i == 0)
        def _():
            loss_ref[...] = jnp.zeros_like(loss_ref)
            dg2_ref[...] = jnp.zeros_like(dg2_ref)

        ff = jnp.dot(a_ref[...], w_ref[...], preferred_element_type=F32)
        err = x1_ref[...] + g2_ref[...] * ff - t_ref[...]
        dy = err * (1.0 / D)
        dy_ref[...] = dy
        dff_ref[...] = (dy * g2_ref[...]).astype(BF16)
        loss_ref[...] += _rows8(err * err)
        dg2_ref[...] += _rows8(dy * ff)

    row = pl.BlockSpec((tm, D), lambda i: (i, 0))
    acc = _full((SUBLANES, D))
    return pl.pallas_call(
        body, name="down_loss_head", grid=(T // tm,),
        in_specs=[row, pl.BlockSpec((tm, D_FF), lambda i: (i, 0)), _full((D_FF, D)), _full((1, D)), row],
        out_specs=[row, row, acc, acc],
        out_shape=[jax.ShapeDtypeStruct((T, D), F32), jax.ShapeDtypeStruct((T, D), BF16),
                   jax.ShapeDtypeStruct((SUBLANES, D), F32), jax.ShapeDtypeStruct((SUBLANES, D), F32)],
        compiler_params=_params(("arbitrary",)),
    )(x1, act, w_down, g2, target)


def _nt_dot(a, b):
    return lax.dot_general(a, b, (((1,), (1,)), ((), ())), preferred_element_type=F32)


def _causal(n, keys_on_rows=False):
    r = lax.broadcasted_iota(jnp.int32, (n, n), 0)
    c = lax.broadcasted_iota(jnp.int32, (n, n), 1)
    return (c >= r) if keys_on_rows else (c <= r)


def _sweep(lo, hi, step, carry, group=2):
    while group >= 1:
        def several(j, cr, lo=lo, group=group):
            for g in range(group):
                cr = step(lo + group * j + g, cr)
            return cr

        passes = (hi - lo) // group
        carry = lax.fori_loop(0, passes, several, carry)
        lo = lo + group * passes
        group //= 2
    return carry


def _grid_ends(n0, n1):
    i0, i1 = pl.program_id(0), pl.program_id(1)
    return jnp.logical_and(i0 == 0, i1 == 0), jnp.logical_and(i0 == n0 - 1, i1 == n1 - 1)


def _attn_fwd(qa, ka, va, exchange=None):
    nh, T, _ = qa.shape
    bq = min(ATTN_BLOCK, T)
    nq = T // bq

    def body(*refs):
        if exchange is None:
            q_ref, k_ref, v_ref, o_ref, qb_ref = refs
        else:
            (q_ref, k_ref, v_ref), (o_ref, qb_ref), _, xrefs = exchange.split(refs, 3, 2)
            _ride(exchange, *_grid_ends(nh, nq), xrefs)
        qi = pl.program_id(1)
        q = q_ref[0]

        def step(kb, carry, masked=False):
            m, acc = carry
            start = pl.multiple_of(kb * bq, bq)
            s = _nt_dot(q, k_ref[0, pl.ds(start, bq), :])
            if masked:
                s = jnp.where(_causal(bq), s, NEG_INF)
            m_new = jnp.maximum(m, jnp.max(s, axis=-1, keepdims=True))
            p = jnp.exp2(s - m_new).astype(BF16)
            acc = jnp.exp2(m - m_new) * acc + jnp.dot(p, v_ref[0, pl.ds(start, bq), :], preferred_element_type=F32)
            return m_new, acc

        init = (jnp.full((bq, 1), NEG_INF, F32), jnp.zeros((bq, LANES), F32))
        m, acc = step(qi, _sweep(0, qi, step, init, group=4), masked=True)
        l = acc[:, LANE_ONE:LANE_ONE + 1]
        o_ref[0] = (acc / l).astype(BF16)
        lane = lax.broadcasted_iota(jnp.int32, (bq, LANES), 1)
        qf = q.astype(F32)
        for idx, piece in _run(LANE_LSE, _pieces(m + jnp.log2(l))):
            qf = jnp.where(lane == idx, -piece, qf)
        qb_ref[0] = qf.astype(BF16)

    tile = pl.BlockSpec((1, bq, LANES), lambda h, i: (h, i, 0))
    whole = pl.BlockSpec((1, T, LANES), lambda h, i: (h, 0, 0))
    out_shape = [jax.ShapeDtypeStruct((nh, T, LANES), BF16)] * 2
    if exchange is None:
        return pl.pallas_call(
            body, name="attn_fwd", grid=(nh, nq), in_specs=[tile, whole, whole], out_specs=[tile, tile],
            out_shape=out_shape, compiler_params=_params(("parallel", "parallel")),
        )(qa, ka, va)
    return pl.pallas_call(
        body, name="attn_fwd", grid=(nh, nq), in_specs=[tile, whole, whole] + exchange.in_specs,
        out_specs=[tile, tile] + exchange.out_specs, out_shape=out_shape + exchange.out_shapes,
        scratch_shapes=exchange.scratch, compiler_params=_params(("arbitrary", "arbitrary")),
    )(qa, ka, va, *exchange.xs)


def _branch_merge_fwd(ya0, o_h, proj, wba, wbb_heads):
    nh, T, _ = o_h.shape
    tm = min(TOKEN_TILE, T)

    def body(ya0_ref, o_ref, ga_ref, gb_ref, wa_ref, wb_ref, ya_ref, yb_ref, m_ref):
        ya = jnp.dot(ya0_ref[...], wa_ref[...], preferred_element_type=F32)
        yb = jnp.dot(o_ref[0].astype(BF16), wb_ref[0:LANES, :], preferred_element_type=F32)
        for h in range(1, nh):
            yb += jnp.dot(o_ref[h].astype(BF16), wb_ref[h * LANES:(h + 1) * LANES, :], preferred_element_type=F32)
        ya_ref[...] = ya.astype(BF16)
        yb_ref[...] = yb.astype(BF16)
        m_ref[...] = (_sigmoid(ga_ref[...]) * ya + _sigmoid(gb_ref[...]) * yb).astype(BF16)

    row = pl.BlockSpec((tm, D), lambda i: (i, 0))
    return pl.pallas_call(
        body, name="branch_merge_fwd", grid=(T // tm,),
        in_specs=[pl.BlockSpec((tm, CONV_W), lambda i: (i, 0)), pl.BlockSpec((nh, tm, LANES), lambda i: (0, i, 0)),
                  pl.BlockSpec((tm, D), lambda i: (i, COL_GA // D)), pl.BlockSpec((tm, D), lambda i: (i, COL_GB // D)),
                  _full((CONV_W, D)), _full((nh * LANES, D))],
        out_specs=[row, row, row],
        out_shape=[jax.ShapeDtypeStruct((T, D), BF16)] * 3,
        compiler_params=_params(("parallel",)),
    )(ya0, o_h, proj, proj, wba, wbb_heads)


def _branch_b_bwd(dyb, o_h, wbb_heads):
    nh, T, _ = o_h.shape
    tm = min(TOKEN_TILE, T)

    def body(dyb_ref, o_ref, w_ref, out_ref):
        do = _nt_dot(dyb_ref[...], w_ref[...])
        lane = lax.broadcasted_iota(jnp.int32, (tm, LANES), 1)
        for h in range(nh):
            g = do[:, h * LANES:(h + 1) * LANES].astype(BF16).astype(F32)
            delta = jnp.sum(g * o_ref[h].astype(F32), axis=-1, keepdims=True)
            for idx, piece in _run(LANE_ONE, _pieces(delta)):
                g = jnp.where(lane == idx, -piece, g)
            out_ref[h] = g.astype(BF16)

    heads = pl.BlockSpec((nh, tm, LANES), lambda i: (0, i, 0))
    return pl.pallas_call(
        body, name="branch_b_bwd", grid=(T // tm,),
        in_specs=[pl.BlockSpec((tm, D), lambda i: (i, 0)), heads, _full((D, nh * LANES))],
        out_specs=heads, out_shape=jax.ShapeDtypeStruct((nh, T, LANES), BF16),
        compiler_params=_params(("parallel",)),
    )(dyb, o_h, wbb_heads)


def _branch_b_dw(o_h, dyb):
    nh, T, _ = o_h.shape
    tk = min(2 * MATMUL_TILE, T)

    def body(o_ref, dyb_ref, out_ref):
        @pl.when(pl.program_id(0) == 0)
        def _():
            out_ref[...] = jnp.zeros_like(out_ref)

        g = dyb_ref[...]
        for h in range(nh):
            out_ref[h] += _tn_dot(o_ref[h].astype(BF16), g)

    return pl.pallas_call(
        body, name="branch_b_dw", grid=(T // tk,),
        in_specs=[pl.BlockSpec((nh, tk, LANES), lambda k: (0, k, 0)), pl.BlockSpec((tk, D), lambda k: (k, 0))],
        out_specs=_full((nh, LANES, D)), out_shape=jax.ShapeDtypeStruct((nh, LANES, D), F32),
        compiler_params=_params(("arbitrary",)),
    )(o_h, dyb)


def _attn_bwd(qb, ka, va, doa, exchange=None):
    nh, T, _ = qb.shape
    bk = min(ATTN_BLOCK, T)
    nk = T // bk

    def body(*refs):
        if exchange is None:
            q_ref, do_ref, k_ref, v_ref, dq_ref, dk_ref, dv_ref = refs
        else:
            (q_ref, do_ref, k_ref, v_ref), (dq_ref, dk_ref, dv_ref), _, xrefs = exchange.split(refs, 4, 3)
            _ride(exchange, *_grid_ends(nh, nk), xrefs)
        ki = pl.program_id(1)

        @pl.when(ki == 0)
        def _():
            dq_ref[...] = jnp.zeros_like(dq_ref)

        k = k_ref[0]
        v = v_ref[0]

        def step(qi, carry, masked):
            dk, dv = carry
            rows = pl.ds(pl.multiple_of(qi * bk, bk), bk)
            q = q_ref[0, rows, :]
            g = do_ref[0, rows, :]
            pt = jnp.exp2(_nt_dot(k, q))
            if masked:
                pt = jnp.where(_causal(bk, keys_on_rows=True), pt, 0.0)
            dv = dv + jnp.dot(pt.astype(BF16), g, preferred_element_type=F32)
            dst = (pt * _nt_dot(v, g)).astype(BF16)
            dk = dk + jnp.dot(dst, q, preferred_element_type=F32)
            dq_ref[0, rows, :] += _tn_dot(dst, k)
            return dk, dv

        init = (jnp.zeros((bk, LANES), F32), jnp.zeros((bk, LANES), F32))
        carry = step(ki, init, True)
        dk_ref[0], dv_ref[0] = _sweep(ki + 1, nk, lambda qi, cr: step(qi, cr, False), carry)

    tile = pl.BlockSpec((1, bk, LANES), lambda h, i: (h, i, 0))
    whole = pl.BlockSpec((1, T, LANES), lambda h, i: (h, 0, 0))
    out_shape = [jax.ShapeDtypeStruct((nh, T, LANES), F32)] * 3
    if exchange is None:
        return pl.pallas_call(
            body, name="attn_bwd", grid=(nh, nk), in_specs=[whole, whole, tile, tile],
            out_specs=[whole, tile, tile], out_shape=out_shape, compiler_params=_params(("parallel", "arbitrary")),
        )(qb, doa, ka, va)
    return pl.pallas_call(
        body, name="attn_bwd", grid=(nh, nk), in_specs=[whole, whole, tile, tile] + exchange.in_specs,
        out_specs=[whole, tile, tile] + exchange.out_specs, out_shape=out_shape + exchange.out_shapes,
        scratch_shapes=exchange.scratch, compiler_params=_params(("arbitrary", "arbitrary")),
    )(qb, doa, ka, va, *exchange.xs)


def _attn_unpack(dq_h, dk_h, dv_h):
    nh, T, _ = dq_h.shape
    tm = min(TOKEN_TILE, T)

    def body(dq_ref, dk_ref, dv_ref, q_out, k_out, v_out, f_out):
        lane = lax.broadcasted_iota(jnp.int32, (tm, LANES), 1)
        low = lane < HEAD_DIM
        for src, dst in ((dq_ref, q_out), (dk_ref, k_out), (dv_ref, v_out)):
            for pair in range(nh // 2):
                both = jnp.where(low, src[2 * pair], pltpu.roll(src[2 * pair + 1], HEAD_DIM, axis=1))
                dst[:, LANES * pair:LANES * (pair + 1)] = both.astype(dst.dtype)
        df = jnp.zeros((tm, LANES), F32)
        for h in range(nh):
            col = dq_ref[h][:, LANE_F:LANE_F + 1] - dk_ref[h][:, LANE_SUM:LANE_SUM + 1]
            df = jnp.where(lane == h, col, df)
        f_out[...] = df

    heads = pl.BlockSpec((nh, tm, LANES), lambda i: (0, i, 0))
    tok = pl.BlockSpec((tm, ATTN_W), lambda i: (i, 0))
    return pl.pallas_call(
        body, name="attn_unpack", grid=(T // tm,), in_specs=[heads, heads, heads],
        out_specs=[tok, tok, tok, pl.BlockSpec((tm, F_PAD), lambda i: (i, 0))],
        out_shape=[jax.ShapeDtypeStruct((T, ATTN_W), BF16)] * 3 + [jax.ShapeDtypeStruct((T, F_PAD), F32)],
        compiler_params=_params(("parallel",)),
    )(dq_h, dk_h, dv_h)


def _ffn_act_bwd(u, cg, cv, da, w8):
    T = u.shape[0]
    tm = min(FFN_TM, T)
    nt = T // tm
    nc = D_FF // FFN_TC

    def body(ug_ref, uv_ref, cg_ref, cv_ref, cgn_ref, cvn_ref, da_ref, dan_ref, wg_ref, wv_ref,
             dug_ref, duv_ref, dwg_ref, dwv_ref):
        i = pl.program_id(1)

        @pl.when(i == 0)
        def _():
            dwg_ref[...] = jnp.zeros_like(dwg_ref)
            dwv_ref[...] = jnp.zeros_like(dwv_ref)

        gate = jnp.concatenate([cg_ref[...], cgn_ref[...]], axis=0).astype(F32)
        val = jnp.concatenate([cv_ref[...], cvn_ref[...]], axis=0).astype(F32)
        dae = jnp.concatenate([da_ref[...], dan_ref[...]], axis=0).astype(F32)
        rows_e = lax.broadcasted_iota(jnp.int32, dae.shape, 0)
        dae = jnp.where(jnp.logical_and(i == nt - 1, rows_e >= tm), 0.0, dae)
        sg = _sigmoid(gate)
        n = tm + BF16_ROWS

        def back(d, u_ref, w_ref, du_ref, dw_ref):
            w = w_ref[...]
            uv = u_ref[...]
            d1 = pltpu.roll(d, n - 1, axis=0)[:tm]
            d2 = pltpu.roll(d, n - 2, axis=0)[:tm]
            d0 = d[:tm]
            du_ref[...] = (d0 * w[2:3] + d1 * w[1:2] + d2 * w[0:1]).astype(BF16)
            rows = [jnp.sum(t * uv, axis=0, keepdims=True) for t in (d2, d1, d0)]
            dw_ref[...] += jnp.concatenate(rows + [jnp.zeros((SUBLANES - 3, FFN_TC), F32)], axis=0)

        back(dae * val * sg * (1.0 + gate * (1.0 - sg)), ug_ref, wg_ref, dug_ref, dwg_ref)
        back(dae * gate * sg, uv_ref, wv_ref, duv_ref, dwv_ref)

    per = tm // BF16_ROWS
    last_blk = nt * per - 1
    blk = lambda off: pl.BlockSpec((tm, FFN_TC), lambda j, i: (i, j + off))
    nxt = pl.BlockSpec((BF16_ROWS, FFN_TC), lambda j, i: (jnp.minimum((i + 1) * per, last_blk), j))
    wblk = lambda off: pl.BlockSpec((SUBLANES, FFN_TC), lambda j, i: (0, j + off))
    dug, duv, dwg, dwv = pl.pallas_call(
        body, name="ffn_act_bwd", grid=(nc, nt),
        in_specs=[blk(0), blk(nc), blk(0), blk(0), nxt, nxt, blk(0), nxt, wblk(0), wblk(nc)],
        out_specs=[blk(0), blk(0), wblk(0), wblk(0)],
        out_shape=[jax.ShapeDtypeStruct((T, D_FF), BF16)] * 2 + [jax.ShapeDtypeStruct((SUBLANES, D_FF), F32)] * 2,
        compiler_params=_params(("parallel", "arbitrary")),
    )(u, u, cg, cv, cg, cv, da, da, w8, w8)
    return dug, duv, jnp.concatenate([dwg, dwv], axis=1)


def _out_merge_bwd(dmix, w_out, ya, yb, proj):
    T = ya.shape[0]
    tm = min(TOKEN_TILE, T)

    def body(dmix_ref, w_ref, ya_ref, yb_ref, ga_ref, gb_ref, dya_ref, dyb_ref, dga_ref, dgb_ref):
        dm = _nt_dot(dmix_ref[...], w_ref[...])
        sa = _sigmoid(ga_ref[...])
        sb = _sigmoid(gb_ref[...])
        dya_ref[...] = (dm * sa).astype(BF16)
        dyb_ref[...] = (dm * sb).astype(BF16)
        dga_ref[...] = (dm * ya_ref[...].astype(F32) * sa * (1.0 - sa)).astype(BF16)
        dgb_ref[...] = (dm * yb_ref[...].astype(F32) * sb * (1.0 - sb)).astype(BF16)

    row = pl.BlockSpec((tm, D), lambda i: (i, 0))
    return pl.pallas_call(
        body, name="out_merge_bwd", grid=(T // tm,),
        in_specs=[row, _full((D, D)), row, row, pl.BlockSpec((tm, D), lambda i: (i, COL_GA // D)),
                  pl.BlockSpec((tm, D), lambda i: (i, COL_GB // D))],
        out_specs=[row] * 4, out_shape=[jax.ShapeDtypeStruct((T, D), BF16)] * 4,
        compiler_params=_params(("parallel",)),
    )(dmix, w_out, ya, yb, proj, proj)


def _conv_branch_bwd(proj, dya0, conv_w8):
    T = proj.shape[0]
    tm = min(FFN_TM, T)
    nt = T // tm

    def body(cb_ref, cc_ref, cv_ref, cbn_ref, ccp_ref, cvp_ref, ccn_ref, cvn_ref, d_ref, dn_ref, w_ref,
             d3_ref, dw_ref):
        i = pl.program_id(0)

        @pl.when(i == 0)
        def _():
            dw_ref[...] = jnp.zeros_like(dw_ref)

        first, last = i == 0, i == nt - 1
        w = w_ref[...]
        cc = jnp.concatenate([ccp_ref[...], cc_ref[...], ccn_ref[...]], axis=0)
        cv = jnp.concatenate([cvp_ref[...], cv_ref[...], cvn_ref[...]], axis=0)
        rows = lax.broadcasted_iota(jnp.int32, cc.shape, 0)
        z = jnp.where(jnp.logical_and(first, rows < SUBLANES), 0.0, cc * cv)
        z1 = pltpu.roll(z, 1, axis=0)
        z2 = pltpu.roll(z, 2, axis=0)
        cz = z2 * w[0:1] + z1 * w[1:2] + z * w[2:3]
        zeros8 = jnp.zeros((SUBLANES, CONV_W), F32)
        de = jnp.concatenate([zeros8, d_ref[...], jnp.where(last, 0.0, dn_ref[...])], axis=0)
        cbe = jnp.concatenate([zeros8, cb_ref[...], cbn_ref[...]], axis=0)
        dcz = de * cbe
        n = tm + 2 * SUBLANES
        dz = dcz * w[2:3] + pltpu.roll(dcz, n - 1, axis=0) * w[1:2] + pltpu.roll(dcz, n - 2, axis=0) * w[0:1]
        inner = slice(SUBLANES, SUBLANES + tm)
        d3_ref[:, 0:CONV_W] = (de * cz)[inner].astype(BF16)
        d3_ref[:, CONV_W:2 * CONV_W] = (dz * cv)[inner].astype(BF16)
        d3_ref[:, 2 * CONV_W:3 * CONV_W] = (dz * cc)[inner].astype(BF16)
        wrows = [jnp.sum((dcz * t)[inner], axis=0, keepdims=True) for t in (z2, z1, z)]
        dw_ref[...] += jnp.concatenate(wrows + [jnp.zeros((SUBLANES - 3, CONV_W), F32)], axis=0)

    blk = lambda col: pl.BlockSpec((tm, CONV_W), lambda i: (i, col))
    out_blk = pl.BlockSpec((tm, CONV_W), lambda i: (i, 0))
    return pl.pallas_call(
        body, name="conv_branch_bwd", grid=(nt,),
        in_specs=[blk(0), blk(1), blk(2), _next_spec(tm, CONV_W, 0, nt),
                  _prev_spec(tm, CONV_W, 1), _prev_spec(tm, CONV_W, 2),
                  _next_spec(tm, CONV_W, 1, nt), _next_spec(tm, CONV_W, 2, nt),
                  out_blk, _next_spec(tm, CONV_W, 0, nt), _full((SUBLANES, CONV_W))],
        out_specs=[pl.BlockSpec((tm, 3 * CONV_W), lambda i: (i, 0)), _full((SUBLANES, CONV_W))],
        out_shape=[jax.ShapeDtypeStruct((T, 3 * CONV_W), BF16), jax.ShapeDtypeStruct((SUBLANES, CONV_W), F32)],
        compiler_params=_params(("arbitrary",)),
    )(proj, proj, proj, proj, proj, proj, proj, proj, dya0, dya0, conv_w8)


def _qk_norm_bwd(proj, dqs, dkh, dlogf, qg, kg, bf_pad, gmat):
    T = proj.shape[0]
    tm = min(TOKEN_TILE, T)

    def body(q_ref, k_ref, f_ref, dqs_ref, dkh_ref, dlf_ref, qg_ref, kg_ref, bf_ref, g_ref,
             dqk_ref, dfl_ref, dqg_ref, dkg_ref, dbf_ref):
        @pl.when(pl.program_id(0) == 0)
        def _():
            dqg_ref[...] = jnp.zeros_like(dqg_ref)
            dkg_ref[...] = jnp.zeros_like(dkg_ref)
            dbf_ref[...] = jnp.zeros_like(dbf_ref)

        gm = g_ref[...]
        for src, d_src, gain, scale, dst, dgain in (
                (q_ref, dqs_ref, qg_ref, 1.0 / np.sqrt(HEAD_DIM), dqk_ref.at[:, 0:ATTN_W], dqg_ref),
                (k_ref, dkh_ref, kg_ref, 1.0 / LOG2E, dqk_ref.at[:, ATTN_W:2 * ATTN_W], dkg_ref)):
            v = src[...]
            dhat = d_src[...].astype(F32) * scale
            inv = lax.rsqrt(_split_dot(v * v, gm, 2) * (1.0 / HEAD_DIM) + RMS_EPS)
            vn = v * inv
            dgain[...] += _rows8(dhat * vn)
            dvn = dhat * gain[...]
            mean = _split_dot(dvn * vn, gm, 2) * (1.0 / HEAD_DIM)
            dst[...] = (inv * (dvn - vn * mean)).astype(BF16)
        fx = f_ref[...] + bf_ref[...]
        dfl = dlf_ref[...] * _sigmoid(-fx)
        dfl_ref[...] = dfl.astype(BF16)
        dbf_ref[...] += _rows8(dfl)

    blk = lambda col: pl.BlockSpec((tm, ATTN_W), lambda i: (i, col))
    out_blk = pl.BlockSpec((tm, ATTN_W), lambda i: (i, 0))
    f_in = pl.BlockSpec((tm, F_PAD), lambda i: (i, COL_F // F_PAD))
    f_blk = pl.BlockSpec((tm, F_PAD), lambda i: (i, 0))
    return pl.pallas_call(
        body, name="qk_norm_bwd", grid=(T // tm,),
        in_specs=[blk(3), blk(4), f_in, out_blk, out_blk, f_blk, _full((1, ATTN_W)), _full((1, ATTN_W)),
                  _full((1, F_PAD)), _full((ATTN_W, ATTN_W))],
        out_specs=[pl.BlockSpec((tm, 2 * ATTN_W), lambda i: (i, 0)), f_blk, _full((SUBLANES, ATTN_W)),
                   _full((SUBLANES, ATTN_W)), _full((SUBLANES, F_PAD))],
        out_shape=[jax.ShapeDtypeStruct((T, 2 * ATTN_W), BF16), jax.ShapeDtypeStruct((T, F_PAD), BF16)]
        + [jax.ShapeDtypeStruct((SUBLANES, ATTN_W), F32)] * 2 + [jax.ShapeDtypeStruct((SUBLANES, F_PAD), F32)],
        compiler_params=_params(("arbitrary",)),
    )(proj, proj, proj, dqs, dkh, dlogf, qg, kg, bf_pad, gmat)


def _pad_rows8(w):
    return jnp.pad(w, ((0, SUBLANES - w.shape[0]), (0, 0)))


def _fold8(acc):
    return jnp.sum(acc, axis=0, keepdims=True)


def _late_weights(mats):
    out = {name: mats[name] for name in ("w_branch_a", "w_out", "w_up", "w_down")}
    out["w_branch_b_heads"] = _pad_head_rows(mats["w_branch_b"])
    return out


def _local_step(x, target, mod, wts, late=None):
    T = x.shape[0]
    tb = min(MATMUL_TILE, T)
    tk_long = min(2 * MATMUL_TILE, T)
    tm = min(TOKEN_TILE, T)
    sh1, sc1, g1, sh2, sc2, g2 = [mod[:, i * D:(i + 1) * D] for i in range(N_MOD)]
    w_in = wts["w_in"]
    conv_a8 = _pad_rows8(wts["conv_a_w"])
    conv_f8 = _pad_rows8(wts["conv_ffn_w"])
    qg = jnp.tile(wts["q_norm_g"], (1, N_HEADS))
    kg = jnp.tile(wts["k_norm_g"], (1, N_HEADS))
    bf_pad = jnp.pad(wts["b_f"], ((0, 0), (0, F_PAD - N_HEADS)))
    gmat = _group_matrix()

    h = _norm_mod(x, wts["norm1_g"], sc1, sh1, name="norm1_fwd")
    proj = _matmul(h, w_in, name="mm_in", tm=tb, tn=2688, tk=D)
    fcum = _cumsum(proj, reverse=False, name="gate_cumsum", col=COL_F // F_PAD, gate_bias=bf_pad)
    ya0, qa, ka, va = _branch_prep(proj, fcum, conv_a8, qg, kg, gmat)
    if late is None:
        o_h, qb = _attn_fwd(qa, ka, va)
    else:
        o_h, qb, *gathered = _attn_fwd(qa, ka, va, _Exchange([late[name] for name, *_ in LATE], scatter=False))
        wts = dict(wts)
        mats = {name: _join_shards(g, axis) for (name, _, _, axis), g in zip(LATE, gathered) if name != "w_up"}
        mats["w_up"] = _assemble_columns(gathered[[name for name, *_ in LATE].index("w_up")], 2 * D_FF // N_DEV,
                                         2 * D_FF, ((0, 2 * D_FF, 0),), name="assemble_w_up")
        wts.update(_late_weights(mats))
    ya, yb, merged = _branch_merge_fwd(ya0, o_h, proj, wts["w_branch_a"], wts["w_branch_b_heads"])
    mix, x1, h2 = _out_resid_norm(x, merged, wts["w_out"], g1, wts["norm2_g"], sc2, sh2)
    u = _matmul(h2, wts["w_up"], name="mm_up", tm=tb, tn=2816, tk=D)
    act, conv_gate, conv_val = _ffn_act_fwd(u, conv_f8)
    dy, dff, sq8, dg2_8 = _down_loss_head(x1, act, wts["w_down"], g2, target)
    sq = jnp.sum(sq8).reshape(1, 1)

    grads = {}
    da = _matmul(dff, wts["w_down"], name="mm_down_dx", tm=tb, tn=D_FF, tk=D, out_dtype=BF16, trans_b=True)
    grads["w_down"] = _matmul(act, dff, name="mm_down_dw", tm=1408, tn=D, tk=tk_long, trans_a=True)
    dug, duv, dconv_f8 = _ffn_act_bwd(u, conv_gate, conv_val, da, conv_f8)
    grads["conv_ffn_w"] = dconv_f8[:3]
    dx1, dsh2_8, dsc2_8, dn2_8, dmix, dg1_8 = _matmul_pieces(
        [dug, duv], wts["w_up"], name="mm_up_dx", tm=tm, norm_bwd=(x1, dy, wts["norm2_g"], sc2, mix, g1))
    dw_up = [_matmul(h2, d, name="mm_up_dw_" + half, tm=D, tn=1408, tk=tk_long, trans_a=True)
             for half, d in (("gate", dug), ("val", duv))]
    if late is None:
        grads["w_up"] = jnp.concatenate(dw_up, axis=1)
    grads["norm2_g"] = _fold8(dn2_8)

    grads["w_out"] = _matmul(merged, dmix, name="mm_out_dw", tm=D, tn=D, tk=tk_long, trans_a=True)
    dya, dyb, dga, dgb = _out_merge_bwd(dmix, wts["w_out"], ya, yb, proj)
    dya0 = _matmul(dya, wts["w_branch_a"], name="mm_branch_a_dx", tm=tb, tn=CONV_W, tk=D, trans_b=True)
    grads["w_branch_a"] = _matmul(ya0, dya, name="mm_branch_a_dw", tm=CONV_W, tn=D, tk=tk_long, trans_a=True)
    doa = _branch_b_bwd(dyb, o_h, wts["w_branch_b_heads"])
    grads["w_branch_b"] = _branch_b_dw(o_h, dyb)[:, :HEAD_DIM].reshape(ATTN_W, D)
    dconv3, dconv_a8 = _conv_branch_bwd(proj, dya0, conv_a8)
    grads["conv_a_w"] = dconv_a8[:3]

    parts = {}
    if late is None:
        dq_h, dk_h, dv_h = _attn_bwd(qb, ka, va, doa)
    else:
        ready = [(_column_shards(dw_up) if name == "w_up" else _split_shards(grads[name], axis)).astype(BF16)
                 for name, _, _, axis in LATE]
        dq_h, dk_h, dv_h, *recv = _attn_bwd(
            qb, ka, va, doa, _Exchange(ready + [_pack_full_by_dest(grads, CONVS, SUBLANES)], scatter=True))
        parts = dict(zip([name for name, *_ in LATE] + ["conv"], recv))
    dq_tok, dk_tok, dv_tok, dfcum = _attn_unpack(dq_h, dk_h, dv_h)
    dlogf = _cumsum(dfcum, reverse=True, name="gate_cumsum_bwd")
    dqk, dfl, dqg8, dkg8, dbf8 = _qk_norm_bwd(proj, dq_tok, dk_tok, dlogf, qg, kg, bf_pad, gmat)
    grads["q_norm_g"] = jnp.sum(_fold8(dqg8).reshape(N_HEADS, HEAD_DIM), axis=0, keepdims=True)
    grads["k_norm_g"] = jnp.sum(_fold8(dkg8).reshape(N_HEADS, HEAD_DIM), axis=0, keepdims=True)
    grads["b_f"] = _fold8(dbf8)[:, :N_HEADS]
    narrow, wide = [dconv3, dqk, dv_tok], [dga, dgb, dfl]
    dw_narrow = _matmul_tn_pieces(h, narrow, name="mm_in_dw_narrow", tk=tb)
    dwa, dwb, dwf = _matmul_tn_pieces(h, wide, name="mm_in_dw_wide", tk=tk_long)
    dw_in = list(dw_narrow) + [dwf[:, :N_HEADS], dwa, dwb]
    norm1 = (x, dx1, wts["norm1_g"], sc1)
    if late is None:
        grads["w_in"] = jnp.concatenate(dw_in, axis=1)
        grad_x, dsh1_8, dsc1_8, dn1_8 = _matmul_pieces(narrow + wide, w_in, name="mm_in_dx", tm=tm,
                                                       norm_bwd=norm1)
    else:
        grad_x, dsh1_8, dsc1_8, dn1_8, parts["w_in"] = _matmul_pieces(
            narrow + wide, w_in, name="mm_in_dx", tm=tm, norm_bwd=norm1,
            exchange=_Exchange([_column_shards(dw_in).astype(BF16)], scatter=True))
    grads["norm1_g"] = _fold8(dn1_8)
    grads["mod"] = jnp.concatenate([_fold8(a) for a in (dsh1_8, dsc1_8, dg1_8, dsh2_8, dsc2_8, dg2_8)], axis=1)
    return sq, grad_x, grads, parts


def _me_and_peers():
    mx, my, mc = lax.axis_index("x"), lax.axis_index("y"), lax.axis_index("c")
    me = 4 * mx + 2 * my + mc
    peers = []
    for k in range(1, N_DEV):
        px = 1 - mx if k & 4 else mx
        py = 1 - my if k & 2 else my
        pc = 1 - mc if k & 1 else mc
        peers.append(((px, py, pc), 4 * px + 2 * py + pc))
    return me, peers


HBM_SPEC = pl.BlockSpec(memory_space=pltpu.HBM)


class _Exchange:
    def __init__(self, xs, scatter):
        self.xs, self.scatter, self.n = list(xs), scatter, len(xs)
        self.out_shapes = [jax.ShapeDtypeStruct(x.shape if scatter else (N_DEV,) + x.shape, x.dtype) for x in xs]
        self.in_specs = [HBM_SPEC] * self.n
        self.out_specs = [HBM_SPEC] * self.n
        self.scratch = [pltpu.SemaphoreType.DMA((self.n, N_DEV - 1)), pltpu.SemaphoreType.DMA((self.n, N_DEV - 1)),
                        pltpu.SemaphoreType.DMA((self.n,))]

    def _copies(self, x_refs, out_refs, sems):
        send_sems, recv_sems, local_sems = sems
        me, peers = _me_and_peers()

        def src(a, idx):
            return x_refs[a].at[idx] if self.scatter else x_refs[a]

        def copy(a, k, from_idx, to_slot, device):
            return pltpu.make_async_remote_copy(
                src_ref=src(a, from_idx), dst_ref=out_refs[a].at[to_slot], send_sem=send_sems.at[a, k],
                recv_sem=recv_sems.at[a, k], device_id=device, device_id_type=MESH)

        local = [pltpu.make_async_copy(src(a, me), out_refs[a].at[me], local_sems.at[a]) for a in range(self.n)]
        sends = [copy(a, k, idx, me, dev) for a in range(self.n) for k, (dev, idx) in enumerate(peers)]
        recvs = [copy(a, k, idx, idx, dev) for a in range(self.n) for k, (dev, idx) in enumerate(peers)]
        return local, sends, recvs

    def start(self, x_refs, out_refs, sems):
        local, sends, _ = self._copies(x_refs, out_refs, sems)
        for cp in local + sends:
            cp.start()

    def wait(self, x_refs, out_refs, sems):
        local, sends, recvs = self._copies(x_refs, out_refs, sems)
        for cp in recvs:
            cp.wait_recv()
        for cp in sends:
            cp.wait_send()
        for cp in local:
            cp.wait()

    def split(self, refs, n_in, n_out):
        n = self.n
        ins, xin = refs[:n_in], refs[n_in:n_in + n]
        outs, xout = refs[n_in + n:n_in + n + n_out], refs[n_in + n + n_out:n_in + 2 * n + n_out]
        rest = refs[n_in + 2 * n + n_out:]
        return ins, outs, rest[:len(rest) - 3], (xin, xout, rest[len(rest) - 3:])


def _ride(exchange, first, last, refs):
    if exchange is None:
        return

    @pl.when(first)
    def _():
        exchange.start(*refs)

    @pl.when(last)
    def _():
        exchange.wait(*refs)


def _gather_two_level(xs, *, name):
    n = len(xs)
    out_shapes = [jax.ShapeDtypeStruct((N_DEV,) + x.shape, x.dtype) for x in xs]

    def body(*refs):
        x_refs, out_refs = refs[:n], refs[n:2 * n]
        send_sems, recv_sems, local_sems = refs[2 * n:]
        x, y, c = lax.axis_index("x"), lax.axis_index("y"), lax.axis_index("c")
        me, sibling = (x, y, c), (x, y, 1 - c)
        chips = [(1 - x, y), (x, 1 - y), (1 - x, 1 - y)]

        def slot(a, dev):
            return out_refs[a].at[4 * dev[0] + 2 * dev[1] + dev[2]]

        def copy(a, k, block, to, src=None):
            return pltpu.make_async_remote_copy(
                src_ref=slot(a, block) if src is None else src, dst_ref=slot(a, block),
                send_sem=send_sems.at[a, k], recv_sem=recv_sems.at[a, k], device_id=to, device_id_type=MESH)

        mine = [pltpu.make_async_copy(x_refs[a], slot(a, me), local_sems.at[a]) for a in range(n)]
        first = [copy(a, 0, me, sibling, src=x_refs[a]) for a in range(n)]
        first += [copy(a, 1 + j, me, (*chip, c), src=x_refs[a]) for a in range(n) for j, chip in enumerate(chips)]
        for cp in mine + first:
            cp.start()
        passed = []
        for a in range(n):
            for j, chip in enumerate(chips):
                copy(a, 1 + j, (*chip, c), me).wait_recv()
                passed.append(copy(a, 4 + j, (*chip, c), sibling))
                passed[-1].start()
        for a in range(n):
            copy(a, 0, sibling, me).wait_recv()
            for j, chip in enumerate(chips):
                copy(a, 4 + j, (*chip, 1 - c), me).wait_recv()
        for cp in first + passed:
            cp.wait_send()
        for cp in mine:
            cp.wait()

    return pl.pallas_call(
        body, name=name, in_specs=[HBM_SPEC] * n, out_specs=[HBM_SPEC] * n, out_shape=out_shapes,
        scratch_shapes=[pltpu.SemaphoreType.DMA((n, N_DEV - 1)), pltpu.SemaphoreType.DMA((n, N_DEV - 1)),
                        pltpu.SemaphoreType.DMA((n,))],
        compiler_params=pltpu.CompilerParams(has_side_effects=True),
    )(*xs)


def _exchange(xs, *, name, scatter):
    ex = _Exchange(xs, scatter)

    def body(*refs):
        _, _, _, xrefs = ex.split(refs, 0, 0)
        ex.start(*xrefs)
        ex.wait(*xrefs)

    return pl.pallas_call(
        body, name=name, in_specs=ex.in_specs, out_specs=ex.out_specs, out_shape=ex.out_shapes,
        scratch_shapes=ex.scratch, compiler_params=pltpu.CompilerParams(has_side_effects=True),
    )(*xs)


def _ada_fwd(c_all, w_shard, b_shard):
    n = w_shard.shape[1]

    def body(c_ref, w_ref, b_ref, o_ref):
        cv = c_ref[...]
        act = (cv * _sigmoid(cv)).astype(BF16)
        o_ref[...] = jnp.dot(act, w_ref[...].astype(BF16), preferred_element_type=F32) + b_ref[...]

    return pl.pallas_call(
        body, name="ada_fwd", in_specs=[_full((N_DEV, D)), _full((D, n)), _full((1, n))],
        out_specs=_full((N_DEV, n)), out_shape=jax.ShapeDtypeStruct((N_DEV, n), F32), grid=(1,),
        compiler_params=_params(("arbitrary",)),
    )(c_all, w_shard, b_shard)


def _ada_bwd(c_all_t, dmod_pad):
    n = dmod_pad.shape[1]

    def body(c_ref, d_ref, o_ref):
        cv = c_ref[...]
        act = (cv * _sigmoid(cv)).astype(BF16)
        o_ref[...] = jnp.dot(act, d_ref[...].astype(BF16), preferred_element_type=F32)

    return pl.pallas_call(
        body, name="ada_bwd", in_specs=[_full((D, LANES)), _full((LANES, n))],
        out_specs=_full((D, n)), out_shape=jax.ShapeDtypeStruct((D, n), F32), grid=(1,),
        compiler_params=_params(("arbitrary",)),
    )(c_all_t, dmod_pad)


ADAM_ROWS = 256


def _adamw(parts, w, m, v, *, name):
    n, R, C = parts.shape
    tr = next((t for t in (ADAM_ROWS, 128, 64, 32, 16, SUBLANES) if R % t == 0), R)

    def body(p_ref, w_ref, m_ref, v_ref, g_ref, d_ref, nm_ref, nv_ref):
        g = p_ref[0].astype(F32)
        for j in range(1, n):
            g = g + p_ref[j].astype(F32)
        g_ref[...] = g
        nm = ADAM_B1 * m_ref[...] + (1.0 - ADAM_B1) * g
        nv = ADAM_B2 * v_ref[...] + (1.0 - ADAM_B2) * (g * g)
        nm_ref[...] = nm
        nv_ref[...] = nv
        m_hat = nm / (1.0 - ADAM_B1 ** ADAM_STEP)
        v_hat = nv / (1.0 - ADAM_B2 ** ADAM_STEP)
        d_ref[...] = -ADAM_LR * (m_hat / (jnp.sqrt(v_hat) + ADAM_EPS) + ADAM_WD * w_ref[...])

    row = pl.BlockSpec((tr, C), lambda i: (i, 0))
    return pl.pallas_call(
        body, name=name, grid=(R // tr,),
        in_specs=[pl.BlockSpec((n, tr, C), lambda i: (0, i, 0)), row, row, row], out_specs=[row] * 4,
        out_shape=[jax.ShapeDtypeStruct((R, C), F32)] * 4,
        compiler_params=_params(("parallel",)),
    )(parts, w, m, v)


SHARDED = (("w_in", D, IN_W, 1), ("w_branch_a", CONV_W, D, 1), ("w_branch_b", ATTN_W, D, 1), ("w_out", D, D, 0),
           ("w_up", D, 2 * D_FF, 1), ("w_down", D_FF, D, 0), ("conv_a_w", 3, CONV_W, 1),
           ("conv_ffn_w", 3, 2 * D_FF, 1))
MATRICES = SHARDED[:6]
LATE = MATRICES[1:]
CONVS = SHARDED[6:]
REPLICATED = (("b_ada", N_MOD * D), ("norm1_g", D), ("norm2_g", D), ("b_f", N_HEADS), ("q_norm_g", HEAD_DIM),
              ("k_norm_g", HEAD_DIM))


def _shard_shape(rows, cols, axis):
    return (rows // N_DEV, cols) if axis == 0 else (rows, cols // N_DEV)


def _pack_rows(flat, multiple):
    length = flat.shape[-1]
    rows = -(-length // PACK_W)
    rows = -(-rows // multiple) * multiple
    pad = [(0, 0)] * (flat.ndim - 1) + [(0, rows * PACK_W - length)]
    return jnp.pad(flat, pad).reshape(flat.shape[:-1] + (rows, PACK_W))


def _pack_shards(shards, spec, multiple, dtype):
    flat = jnp.concatenate([shards[name].reshape(-1).astype(dtype) for name, *_ in spec])
    return _pack_rows(flat, multiple)


def _join_lane_blocks(gathered):
    n, r, c = gathered.shape

    def body(g_ref, o_ref):
        for j in range(n):
            o_ref[:, j * c:(j + 1) * c] = g_ref[j]

    return pl.pallas_call(
        body, name="join_lane_blocks", grid=(1,), in_specs=[_full((n, r, c))], out_specs=_full((r, n * c)),
        out_shape=jax.ShapeDtypeStruct((r, n * c), gathered.dtype), compiler_params=_params(("arbitrary",)),
    )(gathered)


SHARD_PAD = 768


def _assemble_columns(gathered, shard_cols, out_cols, segments, *, name):
    n, rows, padw = gathered.shape
    assert n == N_DEV and padw == SHARD_PAD and shard_cols <= SHARD_PAD

    def body(g_ref, o_ref):
        j = pl.program_id(0)

        @pl.when(j == 0)
        def _():
            o_ref[...] = jnp.zeros_like(o_ref)

        for dev in range(N_DEV):
            @pl.when(j == dev)
            def _(dev=dev):
                x = g_ref[0]
                for lo, hi, delta in segments:
                    a, b = max(lo, dev * shard_cols), min(hi, (dev + 1) * shard_cols)
                    if a >= b:
                        continue
                    base = (a + delta) // LANES * LANES
                    width = -(-(b + delta - base) // LANES) * LANES
                    src = lax.broadcasted_iota(jnp.int32, (padw, width), 0) + dev * shard_cols
                    dst = lax.broadcasted_iota(jnp.int32, (padw, width), 1) + (base - delta)
                    place = jnp.where((src == dst) & (src >= a) & (src < b), 1.0, 0.0).astype(BF16)
                    moved = jnp.dot(x, place, preferred_element_type=F32).astype(BF16)
                    o_ref[:, base:base + width] = o_ref[:, base:base + width] + moved

    return pl.pallas_call(
        body, name=name, grid=(N_DEV,), in_specs=[pl.BlockSpec((1, rows, padw), lambda j: (j, 0, 0))],
        out_specs=_full((rows, out_cols)), out_shape=jax.ShapeDtypeStruct((rows, out_cols), BF16),
        compiler_params=_params(("arbitrary",)),
    )(gathered)


def _pad_shard(w):
    return jnp.pad(w.astype(BF16), ((0, 0), (0, SHARD_PAD - w.shape[1])))


W_IN_SEGMENTS = ((0, COL_GA, 0), (COL_GA, COL_GA + N_HEADS, COL_F - COL_GA), (COL_GA + N_HEADS, IN_W, -N_HEADS))


def _join_shards(gathered, axis):
    if axis == 0:
        return gathered.reshape(N_DEV * gathered.shape[1], gathered.shape[2])
    if gathered.shape[2] == LANES:
        return _join_lane_blocks(gathered)
    return jnp.concatenate([gathered[j] for j in range(N_DEV)], axis=1)


def _column_shards(pieces):
    total = sum(p.shape[1] for p in pieces)
    width = total // N_DEV
    shards = []
    for j in range(N_DEV):
        lo, hi, off, segs = j * width, (j + 1) * width, 0, []
        for p in pieces:
            a, b = max(lo, off), min(hi, off + p.shape[1])
            if a < b:
                segs.append(p[:, a - off:b - off])
            off += p.shape[1]
        shards.append(segs[0] if len(segs) == 1 else jnp.concatenate(segs, axis=1))
    return jnp.stack(shards)


def _split_shards(full, axis):
    if axis == 0:
        return full.reshape(N_DEV, full.shape[0] // N_DEV, full.shape[1])
    c = full.shape[1] // N_DEV
    return jnp.stack([full[:, j * c:(j + 1) * c] for j in range(N_DEV)])


def _unpack_shards(packed, spec):
    flat = packed.reshape(-1)
    out, off = {}, 0
    for name, rows, cols, axis in spec:
        r, c = _shard_shape(rows, cols, axis)
        out[name] = flat[off:off + r * c].reshape(r, c)
        off += r * c
    return out


def _unpack_gathered(gathered, spec):
    flat = gathered.reshape(N_DEV, -1)
    out, off = {}, 0
    for name, rows, cols, axis in spec:
        r, c = _shard_shape(rows, cols, axis)
        seg = flat[:, off:off + r * c].reshape(N_DEV, r, c)
        out[name] = seg.reshape(rows, cols) if axis == 0 else seg.transpose(1, 0, 2).reshape(rows, cols)
        off += r * c
    return out


def _pack_full_by_dest(full, spec, multiple):
    segs = []
    for name, rows, cols, axis in spec:
        r, c = _shard_shape(rows, cols, axis)
        a = full[name]
        seg = a.reshape(N_DEV, r, c) if axis == 0 else a.reshape(rows, N_DEV, c).transpose(1, 0, 2)
        segs.append(seg.reshape(N_DEV, r * c))
    return _pack_rows(jnp.concatenate(segs, axis=1), multiple)


def _pad_head_rows(w):
    n = w.shape[1]
    padded = jnp.pad(w.reshape(N_HEADS, HEAD_DIM, n), ((0, 0), (0, LANES - HEAD_DIM), (0, 0)))
    return padded.reshape(N_HEADS * LANES, n)


def kernel(x, c, w_ada, b_ada, norm1_g, w_in, b_f, conv_a_w, q_norm_g, k_norm_g, w_branch_a, w_branch_b, w_out, norm2_g, w_up, conv_ffn_w, w_down, loss_target, m_w_ada, m_b_ada, m_norm1_g, m_w_in, m_b_f, m_conv_a_w, m_q_norm_g, m_k_norm_g, m_w_branch_a, m_w_branch_b, m_w_out, m_norm2_g, m_w_up, m_conv_ffn_w, m_w_down, v_w_ada, v_b_ada, v_norm1_g, v_w_in, v_b_f, v_conv_a_w, v_q_norm_g, v_k_norm_g, v_w_branch_a, v_w_branch_b, v_w_out, v_norm2_g, v_w_up, v_conv_ffn_w, v_w_down):
    names = ("w_ada", "b_ada", "norm1_g", "w_in", "b_f", "conv_a_w", "q_norm_g", "k_norm_g", "w_branch_a",
             "w_branch_b", "w_out", "norm2_g", "w_up", "conv_ffn_w", "w_down")
    squeeze = lambda a: a[0] if a.ndim == 3 else a
    W = dict(zip(names, map(squeeze, (w_ada, b_ada, norm1_g, w_in, b_f, conv_a_w, q_norm_g, k_norm_g, w_branch_a,
                                      w_branch_b, w_out, norm2_g, w_up, conv_ffn_w, w_down))))
    M = dict(zip(names, map(squeeze, (m_w_ada, m_b_ada, m_norm1_g, m_w_in, m_b_f, m_conv_a_w, m_q_norm_g,
                                      m_k_norm_g, m_w_branch_a, m_w_branch_b, m_w_out, m_norm2_g, m_w_up,
                                      m_conv_ffn_w, m_w_down))))
    V = dict(zip(names, map(squeeze, (v_w_ada, v_b_ada, v_norm1_g, v_w_in, v_b_f, v_conv_a_w, v_q_norm_g,
                                      v_k_norm_g, v_w_branch_a, v_w_branch_b, v_w_out, v_norm2_g, v_w_up,
                                      v_conv_ffn_w, v_w_down))))
    me = 4 * lax.axis_index("x") + 2 * lax.axis_index("y") + lax.axis_index("c")
    ada_n = N_MOD * D // N_DEV

    small = jnp.concatenate([c.reshape(-1), W["conv_a_w"].reshape(-1), W["conv_ffn_w"].reshape(-1)])
    small_all, w_in_all = _gather_two_level([_pack_rows(small, SUBLANES), _pad_shard(W["w_in"])],
                                            name="gather_first")
    small_all = small_all.reshape(N_DEV, -1)
    c_all = small_all[:, :D]
    conv_all = _unpack_gathered(small_all[:, D:], CONVS)

    b_shard = lax.dynamic_slice(W["b_ada"], (0, me * ada_n), (1, ada_n))
    mod_part = _ada_fwd(c_all, W["w_ada"], b_shard)
    mod_all, = _exchange([mod_part], name="gather_mod", scatter=False)
    mod = lax.dynamic_index_in_dim(mod_all, me, axis=1, keepdims=False).reshape(1, N_MOD * D)

    wts = {"w_in": _assemble_columns(w_in_all, IN_W // N_DEV, IN_W_PAD, W_IN_SEGMENTS, name="assemble_w_in")}
    wts.update(conv_all)
    for name in ("norm1_g", "norm2_g", "q_norm_g", "k_norm_g", "b_f"):
        wts[name] = W[name]
    late = {name: _pad_shard(W[name]) if name == "w_up" else W[name].astype(BF16) for name, *_ in LATE}

    sq, grad_x, grads, parts = _local_step(x[0], loss_target[0], mod, wts, late)
    loss = lax.psum(sq[0, 0] * (0.5 / D), AXES)

    grads["b_ada"] = grads["mod"]
    rep_flat = lambda src: jnp.concatenate([src[name].reshape(-1) for name, _ in REPLICATED])
    rep_parts, = _exchange([_pack_rows(rep_flat(grads), 16)], name="gather_small_grads", scatter=False)
    rep_out = _adamw(rep_parts, *[_pack_rows(rep_flat(s), 16) for s in (W, M, V)], name="adamw_replicated")

    dmod_all = rep_parts.reshape(N_DEV, -1)[:, :N_MOD * D]
    dmod_mine = lax.dynamic_slice(dmod_all, (0, me * ada_n), (N_DEV, ada_n))
    g_ada = _ada_bwd(jnp.pad(c_all.T, ((0, 0), (0, LANES - N_DEV))),
                     jnp.pad(dmod_mine, ((0, LANES - N_DEV), (0, 0))))
    ada_out = _adamw(g_ada[None], W["w_ada"], M["w_ada"], V["w_ada"], name="adamw_ada")

    mat_out = {name: _adamw(parts[name], W[name], M[name], V[name], name="adamw_" + name) for name, *_ in MATRICES}
    conv_out = _adamw(parts["conv"], *[_pack_shards(s, CONVS, SUBLANES, F32) for s in (W, M, V)],
                      name="adamw_conv")

    results = []
    for kind in range(4):
        per = {"w_ada": ada_out[kind]}
        per.update({name: out[kind] for name, out in mat_out.items()})
        per.update(_unpack_shards(conv_out[kind], CONVS))
        flat, off = rep_out[kind].reshape(-1), 0
        for name, n in REPLICATED:
            per[name] = flat[off:off + n].reshape(1, n)
            off += n
        results.append(per)
    restore = lambda name, a: a[None] if W[name].ndim == 2 and name not in dict(REPLICATED) else a
    outs = [loss, grad_x[None]]
    for per in results:
        outs.extend(restore(name, per[name]) for name in names)
    return tuple(outs)
```
